```python
import math
import jax
import jax.numpy as jnp
from jax import lax
import numpy as np


D_MODEL = 1024
BATCH = 16
SEQ = 2048
DEPTH = 2

HEAD_DIM = 64
MIX_WIDTH = D_MODEL
N_MIXERS = 4
GROUP_WIDTH = MIX_WIDTH // N_MIXERS
A_HEADS = GROUP_WIDTH // HEAD_DIM
B_Q_HEADS = GROUP_WIDTH // HEAD_DIM
B_KV_HEADS = 2
D_Q_HEADS = GROUP_WIDTH // HEAD_DIM
D_KV_HEADS = 2
C_GROUPS = 4
C_CHUNK = 128
DILATED_CFGS = ((128, 1), (512, 4), (2048, 16))
DIL_BLOCK = 64
SWA_RADIUS = 128
SWA_BLOCK = 128
DENSE_BLOCK = 128
GRID_W = 64
ROPE_THETA = 10000.0
REL_BUCKETS = 32
REL_MAX_DIST = 1024
D_FF = 2816
CONV_WIDTH = 3
PLE_DIM = 256
EPS = 1e-6
NEG_INF = -1e30
ATTN_SCALE = HEAD_DIM ** -0.5

SPLIT_SIZES = (
    A_HEADS * HEAD_DIM, A_HEADS * HEAD_DIM, A_HEADS * HEAD_DIM,
    B_Q_HEADS * HEAD_DIM, B_KV_HEADS * HEAD_DIM, B_KV_HEADS * HEAD_DIM,
    GROUP_WIDTH, GROUP_WIDTH,
    D_Q_HEADS * HEAD_DIM, D_KV_HEADS * HEAD_DIM, D_KV_HEADS * HEAD_DIM,
)
IN_WIDTH = sum(SPLIT_SIZES)
SPLIT_POINTS = tuple(int(c) for c in np.cumsum(SPLIT_SIZES)[:-1])

kernel_name = "hybrid_parallel_mixer_encoder"


def rms_norm(x, g):
    xf = x.astype(jnp.float32)
    y = xf * lax.rsqrt(jnp.mean(xf * xf, axis=-1, keepdims=True) + EPS)
    return (y * g.astype(jnp.float32)).astype(x.dtype)


def layer_norm(x, g, b):
    xf = x.astype(jnp.float32)
    mu = jnp.mean(xf, axis=-1, keepdims=True)
    var = jnp.mean(jnp.square(xf - mu), axis=-1, keepdims=True)
    y = (xf - mu) * lax.rsqrt(var + EPS) * g.astype(jnp.float32) + b.astype(jnp.float32)
    return y.astype(x.dtype)


def split_heads(t):
    return t.reshape(t.shape[0], t.shape[1], -1, HEAD_DIM)


def t5_bucket(rel):
    nb = REL_BUCKETS // 2
    ret = jnp.where(rel > 0, nb, 0)
    n = jnp.abs(rel)
    max_exact = nb // 2
    nf = jnp.maximum(n, 1).astype(jnp.float32)
    large = max_exact + (jnp.log(nf / max_exact) / math.log(REL_MAX_DIST / max_exact)
                         * (nb - max_exact)).astype(jnp.int32)
    large = jnp.minimum(large, nb - 1)
    return ret + jnp.where(n < max_exact, n, large)


def rel_bias_pattern(table, block, radius, dil):
    kw = block + 2 * radius
    rel = (jnp.arange(kw)[None, :] - radius - jnp.arange(block)[:, None]) * dil
    return jnp.transpose(table.astype(jnp.float32)[t5_bucket(rel)], (2, 0, 1))


def banded_attention(q, k, v, radius, block, bias, sink=None):
    n, l, h, dh = q.shape
    g = k.shape[2]
    rep = h // g
    nblk = -(-l // block)
    lp = nblk * block
    kw = block + 2 * radius
    qp = jnp.pad(q, ((0, 0), (0, lp - l), (0, 0), (0, 0)))
    kp = jnp.pad(k, ((0, 0), (radius, lp - l + radius), (0, 0), (0, 0)))
    vp = jnp.pad(v, ((0, 0), (radius, lp - l + radius), (0, 0), (0, 0)))
    idx = jnp.arange(nblk)[:, None] * block + jnp.arange(kw)[None, :]
    kb = kp[:, idx]
    vb = vp[:, idx].astype(jnp.float32)
    qb = qp.reshape(n, nblk, block, g, rep, dh)
    s = jnp.einsum('nbqgrd,nbkgd->nbgrqk', qb, kb, preferred_element_type=jnp.float32) * ATTN_SCALE
    s = s + bias.reshape(g, rep, block, kw)
    key_pos = idx - radius
    key_ok = (key_pos >= 0) & (key_pos < l)
    rel = jnp.arange(kw)[None, :] - radius - jnp.arange(block)[:, None]
    mask = key_ok[:, None, :] & (jnp.abs(rel) <= radius)[None]
    s = jnp.where(mask[None, :, None, None], s, NEG_INF)
    m = jnp.max(s, axis=-1, keepdims=True)
    if sink is not None:
        sk = sink.astype(jnp.float32).reshape(g, rep, 1, 1)
        m = jnp.maximum(m, sk)
    pr = jnp.exp(s - m)
    den = jnp.sum(pr, axis=-1, keepdims=True)
    if sink is not None:
        den = den + jnp.exp(sk - m)
    o = jnp.einsum('nbgrqk,nbkgd->nbqgrd', pr / den, vb)
    o = o.reshape(n, lp, h, dh)[:, :l]
    lse = jnp.transpose((m + jnp.log(den))[..., 0], (0, 1, 4, 2, 3)).reshape(n, lp, h)[:, :l]
    return o.astype(q.dtype), lse


def dilated_attention(q, k, v, table):
    b, s, h, dh = q.shape
    outs, lses = [], []
    for window, dil in DILATED_CFGS:
        radius = window // (2 * dil)
        ls = s // dil

        def gather(t):
            return jnp.transpose(t.reshape(b, ls, dil, h, dh), (0, 2, 1, 3, 4)).reshape(b * dil, ls, h, dh)

        bias = rel_bias_pattern(table, DIL_BLOCK, radius, dil)
        o, lse = banded_attention(gather(q), gather(k), gather(v), radius, DIL_BLOCK, bias)
        outs.append(jnp.transpose(o.reshape(b, dil, ls, h, dh), (0, 2, 1, 3, 4)).reshape(b, s, h, dh))
        lses.append(jnp.transpose(lse.reshape(b, dil, ls, h), (0, 2, 1, 3)).reshape(b, s, h))
    w = jax.nn.softmax(jnp.stack(lses, axis=0), axis=0)
    o = jnp.sum(w[..., None] * jnp.stack(outs, axis=0).astype(jnp.float32), axis=0)
    return o.astype(q.dtype)


def rope_1d(x, pos):
    dim = x.shape[-1]
    inv = ROPE_THETA ** (-jnp.arange(0, dim, 2, dtype=jnp.float32) / dim)
    ang = pos.astype(jnp.float32)[:, None] * inv[None, :]
    cos = jnp.cos(ang)[:, None, :]
    sin = jnp.sin(ang)[:, None, :]
    xf = x.astype(jnp.float32)
    x1, x2 = xf[..., :dim // 2], xf[..., dim // 2:]
    return jnp.concatenate([x1 * cos - x2 * sin, x2 * cos + x1 * sin], axis=-1).astype(x.dtype)


def axial_rope(x, row, col):
    half = x.shape[-1] // 2
    return jnp.concatenate([rope_1d(x[..., :half], row), rope_1d(x[..., half:], col)], axis=-1)


def dense_gqa_blocks(q, k, v):
    b, s, h, dh = q.shape
    g = k.shape[2]
    rep = h // g
    nqb = s // DENSE_BLOCK
    qb = jnp.moveaxis(q.reshape(b, nqb, DENSE_BLOCK, g, rep, dh), 1, 0)
    vf = v.astype(jnp.float32)

    def one(qblk):
        sc = jnp.einsum('bqgrd,bkgd->bgrqk', qblk, k, preferred_element_type=jnp.float32) * ATTN_SCALE
        pr = jax.nn.softmax(sc, axis=-1)
        return jnp.einsum('bgrqk,bkgd->bqgrd', pr, vf)

    o = lax.map(one, qb)
    return jnp.moveaxis(o, 0, 1).reshape(b, s, h, dh).astype(q.dtype)


def spatial_gating(u, v, ln_g, ln_b, ws, bs):
    b, s, c = u.shape
    u = jax.nn.gelu(u)
    v = layer_norm(jax.nn.gelu(v), ln_g, ln_b)
    vc = v.reshape(b, s // C_CHUNK, C_CHUNK, C_GROUPS, c // C_GROUPS)
    mixed = jnp.einsum('gpq,bnqgc->bnpgc', ws, vc) + jnp.transpose(bs)[:, :, None]
    return u * mixed.reshape(b, s, c)


def depthwise_conv(h, w, bias):
    c = h.shape[-1]
    y = lax.conv_general_dilated(h, w[:, None, :].astype(h.dtype), window_strides=(1,),
                                 padding=((CONV_WIDTH // 2, CONV_WIDTH // 2),),
                                 dimension_numbers=('NWC', 'WIO', 'NWC'), feature_group_count=c)
    return y + bias


def conv_gated_ffn(x, w_up, conv_w, conv_b, w_down):
    h = depthwise_conv(x @ w_up, conv_w, conv_b)
    gt, up = jnp.split(h, 2, axis=-1)
    return (jax.nn.silu(gt) * up) @ w_down


def _fwd_setup_inputs(seed: int = 0) -> dict:
    key = jax.random.key(seed)
    ks = jax.random.split(key, 24)

    def nrm(k, shape, scale):
        return jax.random.normal(k, shape, jnp.float32) * scale

    return {
        'x': nrm(ks[0], (BATCH, SEQ, D_MODEL), 1.0),
        'p': nrm(ks[1], (DEPTH, BATCH, SEQ, PLE_DIM), 1.0),
        'rel_bias': nrm(ks[2], (REL_BUCKETS, A_HEADS + B_Q_HEADS), 0.3),
        'ln_mix_g': 1.0 + nrm(ks[3], (DEPTH, D_MODEL), 0.02),
        'w_in': nrm(ks[4], (DEPTH, D_MODEL, IN_WIDTH), D_MODEL ** -0.5),
        'qk_gain': 1.0 + nrm(ks[5], (DEPTH, 3, 2, HEAD_DIM), 0.02),
        'sink': nrm(ks[6], (DEPTH, B_Q_HEADS), 0.5),
        'c_norm_g': 1.0 + nrm(ks[7], (DEPTH, GROUP_WIDTH), 0.02),
        'c_norm_b': nrm(ks[8], (DEPTH, GROUP_WIDTH), 0.02),
        'c_ws': nrm(ks[9], (DEPTH, C_GROUPS, C_CHUNK, C_CHUNK), C_CHUNK ** -0.5),
        'c_bs': 1.0 + nrm(ks[10], (DEPTH, C_GROUPS, C_CHUNK), 0.02),
        'out_gain': 1.0 + nrm(ks[11], (DEPTH, N_MIXERS, GROUP_WIDTH), 0.02),
        'w_out': nrm(ks[12], (DEPTH, MIX_WIDTH, D_MODEL), MIX_WIDTH ** -0.5),
        'ln_ffn_g': 1.0 + nrm(ks[13], (DEPTH, D_MODEL), 0.02),
        'w_up': nrm(ks[14], (DEPTH, D_MODEL, 2 * D_FF), D_MODEL ** -0.5),
        'conv_w': nrm(ks[15], (DEPTH, CONV_WIDTH, 2 * D_FF), CONV_WIDTH ** -0.5),
        'conv_b': nrm(ks[16], (DEPTH, 2 * D_FF), 0.02),
        'w_down': nrm(ks[17], (DEPTH, D_FF, D_MODEL), D_FF ** -0.5),
        'ln_ple_g': 1.0 + nrm(ks[18], (DEPTH, D_MODEL), 0.02),
        'w_ple_gate': nrm(ks[19], (DEPTH, D_MODEL, D_MODEL), D_MODEL ** -0.5),
        'w_ple_proj': nrm(ks[20], (DEPTH, PLE_DIM, D_MODEL), PLE_DIM ** -0.5),
    }


def _fwd_reference(x, p, rel_bias, ln_mix_g, w_in, qk_gain, sink, c_norm_g, c_norm_b, c_ws, c_bs,
              out_gain, w_out, ln_ffn_g, w_up, conv_w, conv_b, w_down, ln_ple_g, w_ple_gate,
              w_ple_proj):
    b, s, _ = x.shape
    rows = s // GRID_W
    row_idx = jnp.repeat(jnp.arange(rows), GRID_W)
    col_idx = jnp.tile(jnp.arange(GRID_W), rows)
    table_a = rel_bias[:, :A_HEADS]
    table_b = rel_bias[:, A_HEADS:]
    bias_b = rel_bias_pattern(table_b, SWA_BLOCK, SWA_RADIUS, 1)
    for i in range(DEPTH):
        hn = rms_norm(x, ln_mix_g[i])
        proj = hn @ w_in[i]
        a_q, a_k, a_v, b_q, b_k, b_v, c_u, c_v, d_q, d_k, d_v = jnp.split(proj, SPLIT_POINTS, axis=-1)
        a_q = rms_norm(split_heads(a_q), qk_gain[i, 0, 0])
        a_k = rms_norm(split_heads(a_k), qk_gain[i, 0, 1])
        y_a = dilated_attention(a_q, a_k, split_heads(a_v), table_a).reshape(b, s, GROUP_WIDTH)
        b_q = rms_norm(split_heads(b_q), qk_gain[i, 1, 0])
        b_k = rms_norm(split_heads(b_k), qk_gain[i, 1, 1])
        y_b, _ = banded_attention(b_q, b_k, split_heads(b_v), SWA_RADIUS, SWA_BLOCK, bias_b, sink[i])
        y_b = y_b.reshape(b, s, GROUP_WIDTH)
        y_c = spatial_gating(c_u, c_v, c_norm_g[i], c_norm_b[i], c_ws[i], c_bs[i])
        d_q = axial_rope(rms_norm(split_heads(d_q), qk_gain[i, 2, 0]), row_idx, col_idx)
        d_k = axial_rope(rms_norm(split_heads(d_k), qk_gain[i, 2, 1]), row_idx, col_idx)
        y_d = dense_gqa_blocks(d_q, d_k, split_heads(d_v)).reshape(b, s, GROUP_WIDTH)
        groups = (y_a, y_b, y_c, y_d)
        mixed = jnp.concatenate([rms_norm(y, out_gain[i, m]) for m, y in enumerate(groups)], axis=-1)
        x = x + mixed @ w_out[i]
        x = x + conv_gated_ffn(rms_norm(x, ln_ffn_g[i]), w_up[i], conv_w[i], conv_b[i], w_down[i])
        gate = jax.nn.sigmoid(rms_norm(x, ln_ple_g[i]) @ w_ple_gate[i])
        x = x + (p[i] @ w_ple_proj[i]) * gate
    return x


import jax as _jax
import jax.numpy as _jnp

TWIN_FORMAT = 'train_step'
FWD_PARAMS = ['x', 'p', 'rel_bias', 'ln_mix_g', 'w_in', 'qk_gain', 'sink', 'c_norm_g', 'c_norm_b', 'c_ws', 'c_bs', 'out_gain', 'w_out', 'ln_ffn_g', 'w_up', 'conv_w', 'conv_b', 'w_down', 'ln_ple_g', 'w_ple_gate', 'w_ple_proj']
TWIN_WEIGHTS = ['rel_bias', 'ln_mix_g', 'w_in', 'qk_gain', 'sink', 'c_norm_g', 'c_norm_b', 'c_ws', 'c_bs', 'out_gain', 'w_out', 'ln_ffn_g', 'w_up', 'conv_w', 'conv_b', 'w_down', 'ln_ple_g', 'w_ple_gate', 'w_ple_proj']
TWIN_DIFF_INPUT = 'x'
TWIN_INPUTS = ['x', 'p', 'rel_bias', 'ln_mix_g', 'w_in', 'qk_gain', 'sink', 'c_norm_g', 'c_norm_b', 'c_ws', 'c_bs', 'out_gain', 'w_out', 'ln_ffn_g', 'w_up', 'conv_w', 'conv_b', 'w_down', 'ln_ple_g', 'w_ple_gate', 'w_ple_proj', 'loss_target', 'm_rel_bias', 'm_ln_mix_g', 'm_w_in', 'm_qk_gain', 'm_sink', 'm_c_norm_g', 'm_c_norm_b', 'm_c_ws', 'm_c_bs', 'm_out_gain', 'm_w_out', 'm_ln_ffn_g', 'm_w_up', 'm_conv_w', 'm_conv_b', 'm_w_down', 'm_ln_ple_g', 'm_w_ple_gate', 'm_w_ple_proj', 'v_rel_bias', 'v_ln_mix_g', 'v_w_in', 'v_qk_gain', 'v_sink', 'v_c_norm_g', 'v_c_norm_b', 'v_c_ws', 'v_c_bs', 'v_out_gain', 'v_w_out', 'v_ln_ffn_g', 'v_w_up', 'v_conv_w', 'v_conv_b', 'v_w_down', 'v_ln_ple_g', 'v_w_ple_gate', 'v_w_ple_proj']
TWIN_OUTPUTS = ['loss', 'grad_x', 'grad_rel_bias', 'grad_ln_mix_g', 'grad_w_in', 'grad_qk_gain', 'grad_sink', 'grad_c_norm_g', 'grad_c_norm_b', 'grad_c_ws', 'grad_c_bs', 'grad_out_gain', 'grad_w_out', 'grad_ln_ffn_g', 'grad_w_up', 'grad_conv_w', 'grad_conv_b', 'grad_w_down', 'grad_ln_ple_g', 'grad_w_ple_gate', 'grad_w_ple_proj', 'delta_rel_bias', 'delta_ln_mix_g', 'delta_w_in', 'delta_qk_gain', 'delta_sink', 'delta_c_norm_g', 'delta_c_norm_b', 'delta_c_ws', 'delta_c_bs', 'delta_out_gain', 'delta_w_out', 'delta_ln_ffn_g', 'delta_w_up', 'delta_conv_w', 'delta_conv_b', 'delta_w_down', 'delta_ln_ple_g', 'delta_w_ple_gate', 'delta_w_ple_proj', 'new_m_rel_bias', 'new_m_ln_mix_g', 'new_m_w_in', 'new_m_qk_gain', 'new_m_sink', 'new_m_c_norm_g', 'new_m_c_norm_b', 'new_m_c_ws', 'new_m_c_bs', 'new_m_out_gain', 'new_m_w_out', 'new_m_ln_ffn_g', 'new_m_w_up', 'new_m_conv_w', 'new_m_conv_b', 'new_m_w_down', 'new_m_ln_ple_g', 'new_m_w_ple_gate', 'new_m_w_ple_proj', 'new_v_rel_bias', 'new_v_ln_mix_g', 'new_v_w_in', 'new_v_qk_gain', 'new_v_sink', 'new_v_c_norm_g', 'new_v_c_norm_b', 'new_v_c_ws', 'new_v_c_bs', 'new_v_out_gain', 'new_v_w_out', 'new_v_ln_ffn_g', 'new_v_w_up', 'new_v_conv_w', 'new_v_conv_b', 'new_v_w_down', 'new_v_ln_ple_g', 'new_v_w_ple_gate', 'new_v_w_ple_proj']
TWIN_LEAF_KINDS = {'loss': 'loss', 'grad_x': 'grad_x', 'grad_rel_bias': 'grad_w', 'grad_ln_mix_g': 'grad_w', 'grad_w_in': 'grad_w', 'grad_qk_gain': 'grad_w', 'grad_sink': 'grad_w', 'grad_c_norm_g': 'grad_w', 'grad_c_norm_b': 'grad_w', 'grad_c_ws': 'grad_w', 'grad_c_bs': 'grad_w', 'grad_out_gain': 'grad_w', 'grad_w_out': 'grad_w', 'grad_ln_ffn_g': 'grad_w', 'grad_w_up': 'grad_w', 'grad_conv_w': 'grad_w', 'grad_conv_b': 'grad_w', 'grad_w_down': 'grad_w', 'grad_ln_ple_g': 'grad_w', 'grad_w_ple_gate': 'grad_w', 'grad_w_ple_proj': 'grad_w', 'delta_rel_bias': 'delta_w', 'delta_ln_mix_g': 'delta_w', 'delta_w_in': 'delta_w', 'delta_qk_gain': 'delta_w', 'delta_sink': 'delta_w', 'delta_c_norm_g': 'delta_w', 'delta_c_norm_b': 'delta_w', 'delta_c_ws': 'delta_w', 'delta_c_bs': 'delta_w', 'delta_out_gain': 'delta_w', 'delta_w_out': 'delta_w', 'delta_ln_ffn_g': 'delta_w', 'delta_w_up': 'delta_w', 'delta_conv_w': 'delta_w', 'delta_conv_b': 'delta_w', 'delta_w_down': 'delta_w', 'delta_ln_ple_g': 'delta_w', 'delta_w_ple_gate': 'delta_w', 'delta_w_ple_proj': 'delta_w', 'new_m_rel_bias': 'new_m', 'new_m_ln_mix_g': 'new_m', 'new_m_w_in': 'new_m', 'new_m_qk_gain': 'new_m', 'new_m_sink': 'new_m', 'new_m_c_norm_g': 'new_m', 'new_m_c_norm_b': 'new_m', 'new_m_c_ws': 'new_m', 'new_m_c_bs': 'new_m', 'new_m_out_gain': 'new_m', 'new_m_w_out': 'new_m', 'new_m_ln_ffn_g': 'new_m', 'new_m_w_up': 'new_m', 'new_m_conv_w': 'new_m', 'new_m_conv_b': 'new_m', 'new_m_w_down': 'new_m', 'new_m_ln_ple_g': 'new_m', 'new_m_w_ple_gate': 'new_m', 'new_m_w_ple_proj': 'new_m', 'new_v_rel_bias': 'new_v', 'new_v_ln_mix_g': 'new_v', 'new_v_w_in': 'new_v', 'new_v_qk_gain': 'new_v', 'new_v_sink': 'new_v', 'new_v_c_norm_g': 'new_v', 'new_v_c_norm_b': 'new_v', 'new_v_c_ws': 'new_v', 'new_v_c_bs': 'new_v', 'new_v_out_gain': 'new_v', 'new_v_w_out': 'new_v', 'new_v_ln_ffn_g': 'new_v', 'new_v_w_up': 'new_v', 'new_v_conv_w': 'new_v', 'new_v_conv_b': 'new_v', 'new_v_w_down': 'new_v', 'new_v_ln_ple_g': 'new_v', 'new_v_w_ple_gate': 'new_v', 'new_v_w_ple_proj': 'new_v'}


def _forward(args):
    return _fwd_reference(*[args[k] for k in FWD_PARAMS])


def _output_shape():
    out = _jax.eval_shape(lambda: _forward(_fwd_setup_inputs(0)))
    return out.shape, out.dtype

N_MICROBATCH = 1
ADAM_LR = 0.001
ADAM_B1 = 0.9
ADAM_B2 = 0.999
ADAM_EPS = 1e-08
ADAM_WD = 0.01
ADAM_STEP = 10
PER_EXAMPLE_BATCH_AXIS = {'x': 0, 'p': 1, 'loss_target': 0}
SHARED_INPUTS = []
_WEIGHT_DTYPES = {'rel_bias': _jnp.float32, 'ln_mix_g': _jnp.float32, 'w_in': _jnp.float32, 'qk_gain': _jnp.float32, 'sink': _jnp.float32, 'c_norm_g': _jnp.float32, 'c_norm_b': _jnp.float32, 'c_ws': _jnp.float32, 'c_bs': _jnp.float32, 'out_gain': _jnp.float32, 'w_out': _jnp.float32, 'ln_ffn_g': _jnp.float32, 'w_up': _jnp.float32, 'conv_w': _jnp.float32, 'conv_b': _jnp.float32, 'w_down': _jnp.float32, 'ln_ple_g': _jnp.float32, 'w_ple_gate': _jnp.float32, 'w_ple_proj': _jnp.float32}
MOMENT_SCALE = {'rel_bias': 1.739368e+00, 'ln_mix_g': 9.268826e+00, 'w_in': 5.689619e+00, 'qk_gain': 2.345323e+00, 'sink': 1.268078e-01, 'c_norm_g': 9.003157e-01, 'c_norm_b': 6.563494e-01, 'c_ws': 3.392098e-01, 'c_bs': 6.157842e-01, 'out_gain': 3.384556e+01, 'w_out': 1.049266e+01, 'ln_ffn_g': 2.659851e+01, 'w_up': 1.549467e+00, 'conv_w': 4.289592e+00, 'conv_b': 5.462047e+00, 'w_down': 1.662430e+00, 'ln_ple_g': 1.058968e+00, 'w_ple_gate': 5.566646e-01, 'w_ple_proj': 4.708144e-01}


def _to_microbatches(a, axis):
    t = _jnp.moveaxis(a, axis, 0)
    t = t.reshape((N_MICROBATCH, t.shape[0] // N_MICROBATCH) + t.shape[1:])
    return _jnp.moveaxis(t, 1, axis + 1)


def setup_inputs(seed: int = 0) -> dict:
    inp = _fwd_setup_inputs(seed)
    key = _jax.random.fold_in(_jax.random.key(seed), 7919)
    shape, _ = _output_shape()
    out = dict(inp)
    out["loss_target"] = _jax.random.normal(_jax.random.fold_in(key, 0), shape, _jnp.float32)
    for i, name in enumerate(TWIN_WEIGHTS):
        w = inp[name].astype(_jnp.float32)
        if MOMENT_SCALE is None:
            s = _jnp.sqrt(_jnp.mean(_jnp.square(w)) + 1e-30)
        else:
            s = MOMENT_SCALE[name]
        km, kv = _jax.random.split(_jax.random.fold_in(key, i + 1))
        out[name] = w
        out["m_" + name] = s * _jax.random.normal(km, w.shape, _jnp.float32)
        out["v_" + name] = (s * s) * _jax.random.uniform(kv, w.shape, _jnp.float32, 0.5, 1.5)
    if N_MICROBATCH > 1:
        for name, axis in PER_EXAMPLE_BATCH_AXIS.items():
            out[name] = _to_microbatches(out[name], axis)
    return {'x': out['x'], 'p': out['p'], 'rel_bias': out['rel_bias'], 'ln_mix_g': out['ln_mix_g'], 'w_in': out['w_in'], 'qk_gain': out['qk_gain'], 'sink': out['sink'], 'c_norm_g': out['c_norm_g'], 'c_norm_b': out['c_norm_b'], 'c_ws': out['c_ws'], 'c_bs': out['c_bs'], 'out_gain': out['out_gain'], 'w_out': out['w_out'], 'ln_ffn_g': out['ln_ffn_g'], 'w_up': out['w_up'], 'conv_w': out['conv_w'], 'conv_b': out['conv_b'], 'w_down': out['w_down'], 'ln_ple_g': out['ln_ple_g'], 'w_ple_gate': out['w_ple_gate'], 'w_ple_proj': out['w_ple_proj'], 'loss_target': out['loss_target'], 'm_rel_bias': out['m_rel_bias'], 'm_ln_mix_g': out['m_ln_mix_g'], 'm_w_in': out['m_w_in'], 'm_qk_gain': out['m_qk_gain'], 'm_sink': out['m_sink'], 'm_c_norm_g': out['m_c_norm_g'], 'm_c_norm_b': out['m_c_norm_b'], 'm_c_ws': out['m_c_ws'], 'm_c_bs': out['m_c_bs'], 'm_out_gain': out['m_out_gain'], 'm_w_out': out['m_w_out'], 'm_ln_ffn_g': out['m_ln_ffn_g'], 'm_w_up': out['m_w_up'], 'm_conv_w': out['m_conv_w'], 'm_conv_b': out['m_conv_b'], 'm_w_down': out['m_w_down'], 'm_ln_ple_g': out['m_ln_ple_g'], 'm_w_ple_gate': out['m_w_ple_gate'], 'm_w_ple_proj': out['m_w_ple_proj'], 'v_rel_bias': out['v_rel_bias'], 'v_ln_mix_g': out['v_ln_mix_g'], 'v_w_in': out['v_w_in'], 'v_qk_gain': out['v_qk_gain'], 'v_sink': out['v_sink'], 'v_c_norm_g': out['v_c_norm_g'], 'v_c_norm_b': out['v_c_norm_b'], 'v_c_ws': out['v_c_ws'], 'v_c_bs': out['v_c_bs'], 'v_out_gain': out['v_out_gain'], 'v_w_out': out['v_w_out'], 'v_ln_ffn_g': out['v_ln_ffn_g'], 'v_w_up': out['v_w_up'], 'v_conv_w': out['v_conv_w'], 'v_conv_b': out['v_conv_b'], 'v_w_down': out['v_w_down'], 'v_ln_ple_g': out['v_ln_ple_g'], 'v_w_ple_gate': out['v_w_ple_gate'], 'v_w_ple_proj': out['v_w_ple_proj']}


def _loss(weights, diff, rest, loss_target):
    with _jax.named_scope("forward"):
        args = {**rest, TWIN_DIFF_INPUT: diff, **{k: w.astype(_WEIGHT_DTYPES[k]) for k, w in weights.items()}}
        y = _forward(args)
    with _jax.named_scope("loss_head"):
        err = _jnp.square(y.astype(_jnp.float32) - loss_target)
        return 0.5 * _jnp.sum(_jnp.mean(err, axis=-1)) if err.ndim else 0.5 * err


def _adamw(w, g, m, v):
    m = ADAM_B1 * m + (1.0 - ADAM_B1) * g
    v = ADAM_B2 * v + (1.0 - ADAM_B2) * _jnp.square(g)
    m_hat = m / (1.0 - ADAM_B1 ** ADAM_STEP)
    v_hat = v / (1.0 - ADAM_B2 ** ADAM_STEP)
    delta = -ADAM_LR * (m_hat / (_jnp.sqrt(v_hat) + ADAM_EPS) + ADAM_WD * w)
    return delta, m, v


def reference(x, p, rel_bias, ln_mix_g, w_in, qk_gain, sink, c_norm_g, c_norm_b, c_ws, c_bs, out_gain, w_out, ln_ffn_g, w_up, conv_w, conv_b, w_down, ln_ple_g, w_ple_gate, w_ple_proj, loss_target, m_rel_bias, m_ln_mix_g, m_w_in, m_qk_gain, m_sink, m_c_norm_g, m_c_norm_b, m_c_ws, m_c_bs, m_out_gain, m_w_out, m_ln_ffn_g, m_w_up, m_conv_w, m_conv_b, m_w_down, m_ln_ple_g, m_w_ple_gate, m_w_ple_proj, v_rel_bias, v_ln_mix_g, v_w_in, v_qk_gain, v_sink, v_c_norm_g, v_c_norm_b, v_c_ws, v_c_bs, v_out_gain, v_w_out, v_ln_ffn_g, v_w_up, v_conv_w, v_conv_b, v_w_down, v_ln_ple_g, v_w_ple_gate, v_w_ple_proj):
    given = dict(x=x, p=p, rel_bias=rel_bias, ln_mix_g=ln_mix_g, w_in=w_in, qk_gain=qk_gain, sink=sink, c_norm_g=c_norm_g, c_norm_b=c_norm_b, c_ws=c_ws, c_bs=c_bs, out_gain=out_gain, w_out=w_out, ln_ffn_g=ln_ffn_g, w_up=w_up, conv_w=conv_w, conv_b=conv_b, w_down=w_down, ln_ple_g=ln_ple_g, w_ple_gate=w_ple_gate, w_ple_proj=w_ple_proj, loss_target=loss_target, m_rel_bias=m_rel_bias, m_ln_mix_g=m_ln_mix_g, m_w_in=m_w_in, m_qk_gain=m_qk_gain, m_sink=m_sink, m_c_norm_g=m_c_norm_g, m_c_norm_b=m_c_norm_b, m_c_ws=m_c_ws, m_c_bs=m_c_bs, m_out_gain=m_out_gain, m_w_out=m_w_out, m_ln_ffn_g=m_ln_ffn_g, m_w_up=m_w_up, m_conv_w=m_conv_w, m_conv_b=m_conv_b, m_w_down=m_w_down, m_ln_ple_g=m_ln_ple_g, m_w_ple_gate=m_w_ple_gate, m_w_ple_proj=m_w_ple_proj, v_rel_bias=v_rel_bias, v_ln_mix_g=v_ln_mix_g, v_w_in=v_w_in, v_qk_gain=v_qk_gain, v_sink=v_sink, v_c_norm_g=v_c_norm_g, v_c_norm_b=v_c_norm_b, v_c_ws=v_c_ws, v_c_bs=v_c_bs, v_out_gain=v_out_gain, v_w_out=v_w_out, v_ln_ffn_g=v_ln_ffn_g, v_w_up=v_w_up, v_conv_w=v_conv_w, v_conv_b=v_conv_b, v_w_down=v_w_down, v_ln_ple_g=v_ln_ple_g, v_w_ple_gate=v_w_ple_gate, v_w_ple_proj=v_w_ple_proj)
    weights = {n: given[n] for n in TWIN_WEIGHTS}
    shared = {n: given[n] for n in SHARED_INPUTS}
    per_example = {n: given[n] for n in ['x', 'p']}
    grad_fn = _jax.value_and_grad(_loss, argnums=(0, 1))

    def one_microbatch(ex, loss_target):
        ex = dict(ex)
        diff = ex.pop(TWIN_DIFF_INPUT)
        return grad_fn(weights, diff, {**shared, **ex}, loss_target)

    if N_MICROBATCH == 1:
        loss, (grad_w, grad_x) = one_microbatch(per_example, given["loss_target"])
    else:
        def body(carry, xs):
            loss_sum, grad_sum = carry
            l_k, (gw_k, gx_k) = one_microbatch(xs[0], xs[1])
            with _jax.named_scope("update"):
                return (loss_sum + l_k, _jax.tree.map(_jnp.add, grad_sum, gw_k)), gx_k

        init = (_jnp.zeros((), _jnp.float32), _jax.tree.map(_jnp.zeros_like, weights))
        (loss, grad_w), grad_x = _jax.lax.scan(body, init, (per_example, given["loss_target"]))
    with _jax.named_scope("update"):
        delta_w, new_m, new_v = {}, {}, {}
        for n in TWIN_WEIGHTS:
            delta_w[n], new_m[n], new_v[n] = _adamw(weights[n], grad_w[n], given["m_" + n], given["v_" + n])
    return (loss, grad_x, *[grad_w[n] for n in TWIN_WEIGHTS], *[delta_w[n] for n in TWIN_WEIGHTS],
            *[new_m[n] for n in TWIN_WEIGHTS], *[new_v[n] for n in TWIN_WEIGHTS])
```

```python
import functools
import math

import jax
import jax.numpy as jnp
import numpy as np
from jax import lax
from jax.experimental import pallas as pl
from jax.experimental.pallas import tpu as pltpu

F32 = jnp.float32
BF16 = jnp.bfloat16
HI = lax.Precision.HIGHEST

N_DEV = 8
D_MODEL = 1024
SEQ = 2048
DEPTH = 2
HEAD_DIM = 64
GROUP_WIDTH = 256
IN_WIDTH = 2304
D_FF = 2816
PLE_DIM = 256
C_CHUNK = 128
C_GROUPS = 4
DILATED_CFGS = ((128, 1), (512, 4), (2048, 16))
DIL_BLOCK = 64
SWA_RADIUS = 128
SWA_BLOCK = 128
GRID_W = 64
ROPE_THETA = 10000.0
REL_BUCKETS = 32
REL_MAX_DIST = 1024
EPS = 1e-6
NEG_INF = -1e30
ATTN_SCALE = HEAD_DIM ** -0.5

ADAM_LR = 0.001
ADAM_B1 = 0.9
ADAM_B2 = 0.999
ADAM_EPS = 1e-08
ADAM_WD = 0.01
ADAM_STEP = 10

MESH = pl.DeviceIdType.MESH
NT = (((1,), (1,)), ((), ()))
TN = (((0,), (0,)), ((), ()))
ARB = "arbitrary"
PAR = "parallel"


def _cparams(*sem):
    return pltpu.CompilerParams(dimension_semantics=tuple(sem))


def _sds(shape, dtype):
    return jax.ShapeDtypeStruct(tuple(shape), dtype)


def _group_sum_matrix(n, same_group):
    r = lax.broadcasted_iota(jnp.int32, (n, n), 0)
    c = lax.broadcasted_iota(jnp.int32, (n, n), 1)
    if same_group:
        return ((r >> 6) == (c >> 6)).astype(F32)
    return ((r & 63) == (c & 63)).astype(F32)


def _seg_sum(x, e):
    return jnp.dot(x, e, precision=HI, preferred_element_type=F32)


def _gelu(x):
    c = math.sqrt(2.0 / math.pi)
    return 0.5 * x * (1.0 + jnp.tanh(c * (x + 0.044715 * (x * x * x))))


def _gelu_grad(x):
    c = math.sqrt(2.0 / math.pi)
    t = jnp.tanh(c * (x + 0.044715 * (x * x * x)))
    return 0.5 * (1.0 + t) + 0.5 * x * (1.0 - t * t) * c * (1.0 + 3.0 * 0.044715 * (x * x))


def _sigmoid(x):
    return 1.0 / (1.0 + jnp.exp(-x))


def _norm_mm(xs, gain, w, res, *, tm, tn, name):
    t = xs[0].shape[0]
    k = sum(x.shape[1] for x in xs)
    n = w.shape[1]
    ng = len(xs)
    has_res = res is not None

    def body(*refs):
        x_refs = refs[:ng]
        g_ref, w_ref = refs[ng], refs[ng + 1]
        res_ref = refs[ng + 2] if has_res else None
        hn_ref, o_ref, hn_s = refs[ng + 2 + has_res:]

        @pl.when(pl.program_id(1) == 0)
        def _():
            off = 0
            for xr in x_refs:
                x = xr[...]
                wd = x.shape[1]
                r = lax.rsqrt(jnp.mean(x * x, axis=-1, keepdims=True) + EPS)
                hn_s[:, off:off + wd] = (x * r * g_ref[:, off:off + wd]).astype(BF16)
                off += wd
            hn_ref[...] = hn_s[...]

        acc = jnp.dot(hn_s[...], w_ref[...], preferred_element_type=F32)
        if has_res:
            acc = acc + res_ref[...]
        o_ref[...] = acc

    in_specs = [pl.BlockSpec((tm, x.shape[1]), lambda i, j: (i, 0)) for x in xs]
    in_specs += [pl.BlockSpec((1, k), lambda i, j: (0, 0)), pl.BlockSpec((k, tn), lambda i, j: (0, j))]
    args = list(xs) + [gain, w]
    if has_res:
        in_specs.append(pl.BlockSpec((tm, tn), lambda i, j: (i, j)))
        args.append(res)
    return pl.pallas_call(
        body, name=name, grid=(t // tm, n // tn), in_specs=in_specs,
        out_specs=[pl.BlockSpec((tm, k), lambda i, j: (i, 0)), pl.BlockSpec((tm, tn), lambda i, j: (i, j))],
        out_shape=[_sds((t, k), BF16), _sds((t, n), F32)],
        scratch_shapes=[pltpu.VMEM((tm, k), BF16)],
        compiler_params=_cparams(PAR, ARB),
    )(*args)


def _mm(a, b, mode, res, *, tm, tn, out_dtype, name):
    if mode == "tn":
        kk, m = a.shape
        a_spec = pl.BlockSpec((kk, tm), lambda i, j: (0, i))
    else:
        m, kk = a.shape
        a_spec = pl.BlockSpec((tm, kk), lambda i, j: (i, 0))
    if mode == "nt":
        n = b.shape[0]
        b_spec = pl.BlockSpec((tn, kk), lambda i, j: (j, 0))
    else:
        n = b.shape[1]
        b_spec = pl.BlockSpec((kk, tn), lambda i, j: (0, j))
    has_res = res is not None

    def body(*refs):
        a_ref, b_ref = refs[0], refs[1]
        o_ref = refs[-1]
        av = a_ref[...].astype(BF16)
        bv = b_ref[...].astype(BF16)
        if mode == "nn":
            acc = jnp.dot(av, bv, preferred_element_type=F32)
        elif mode == "nt":
            acc = lax.dot_general(av, bv, NT, preferred_element_type=F32)
        else:
            acc = lax.dot_general(av, bv, TN, preferred_element_type=F32)
        if has_res:
            acc = acc + refs[2][...]
        o_ref[...] = acc.astype(out_dtype)

    in_specs = [a_spec, b_spec]
    args = [a, b]
    if has_res:
        in_specs.append(pl.BlockSpec((tm, tn), lambda i, j: (i, j)))
        args.append(res)
    return pl.pallas_call(
        body, name=name, grid=(m // tm, n // tn), in_specs=in_specs,
        out_specs=pl.BlockSpec((tm, tn), lambda i, j: (i, j)),
        out_shape=_sds((m, n), out_dtype),
        compiler_params=_cparams(PAR, PAR),
    )(*args)


def _mm_bt_normbwd(dy, w, xs, gain, dres, *, tm, tn, name):
    t, n = dy.shape
    k = w.shape[0]
    ng = len(xs)
    nj = n // tn
    has_res = dres is not None

    def body(*refs):
        dy_ref, w_ref = refs[0], refs[1]
        x_refs = refs[2:2 + ng]
        g_ref = refs[2 + ng]
        dres_ref = refs[3 + ng] if has_res else None
        dx_ref, dg_ref, acc = refs[3 + ng + has_res:]
        i, j = pl.program_id(0), pl.program_id(1)

        @pl.when(j == 0)
        def _():
            acc[...] = jnp.zeros_like(acc)

        acc[...] += lax.dot_general(dy_ref[...].astype(BF16), w_ref[...], NT, preferred_element_type=F32)

        @pl.when(j == nj - 1)
        def _():
            @pl.when(i == 0)
            def _():
                dg_ref[...] = jnp.zeros_like(dg_ref)

            off = 0
            for xr in x_refs:
                x = xr[...]
                wd = x.shape[1]
                g = g_ref[:, off:off + wd]
                dyn = acc[:, off:off + wd]
                r = lax.rsqrt(jnp.mean(x * x, axis=-1, keepdims=True) + EPS)
                gdy = dyn * g
                dx = r * gdy - x * (r * r * r * jnp.mean(gdy * x, axis=-1, keepdims=True))
                if has_res:
                    dx = dx + dres_ref[:, off:off + wd]
                dx_ref[:, off:off + wd] = dx
                dg_ref[:, off:off + wd] += jnp.sum(dyn * x * r, axis=0, keepdims=True)
                off += wd

    in_specs = [pl.BlockSpec((tm, tn), lambda i, j: (i, j)), pl.BlockSpec((k, tn), lambda i, j: (0, j))]
    in_specs += [pl.BlockSpec((tm, x.shape[1]), lambda i, j: (i, 0)) for x in xs]
    in_specs.append(pl.BlockSpec((1, k), lambda i, j: (0, 0)))
    args = [dy, w] + list(xs) + [gain]
    if has_res:
        in_specs.append(pl.BlockSpec((tm, k), lambda i, j: (i, 0)))
        args.append(dres)
    return pl.pallas_call(
        body, name=name, grid=(t // tm, nj), in_specs=in_specs,
        out_specs=[pl.BlockSpec((tm, k), lambda i, j: (i, 0)), pl.BlockSpec((1, k), lambda i, j: (0, 0))],
        out_shape=[_sds((t, k), F32), _sds((1, k), F32)],
        scratch_shapes=[pltpu.VMEM((tm, k), F32)],
        compiler_params=_cparams(ARB, ARB),
    )(*args)


def _rope_partner(y):
    n = y.shape[1]
    lane = lax.broadcasted_iota(jnp.int32, y.shape, 1)
    return jnp.where((lane & 31) < 16, pltpu.roll(y, n - 16, 1), pltpu.roll(y, 16, 1))


def _qkprep_fwd(proj, gains, cos, sins, *, tm, name):
    t = proj.shape[0]
    nrow = SEQ // tm

    def body(p_ref, g_ref, c_ref, s_ref, qa_ref, qb_ref, qd_ref):
        e = _group_sum_matrix(256, True)

        def hn(x, row):
            wd = x.shape[1]
            ms = _seg_sum(x * x, e[:wd, :wd]) * (1.0 / HEAD_DIM)
            return x * lax.rsqrt(ms + EPS) * g_ref[row:row + 1, :wd]

        qa_ref[:, 0:256] = (hn(p_ref[:, 0:256], 0) * ATTN_SCALE).astype(BF16)
        qa_ref[:, 256:512] = hn(p_ref[:, 256:512], 1).astype(BF16)
        qa_ref[:, 512:768] = p_ref[:, 512:768].astype(BF16)
        qb_ref[:, 0:256] = (hn(p_ref[:, 768:1024], 2) * ATTN_SCALE).astype(BF16)
        qb_ref[:, 256:384] = hn(p_ref[:, 1024:1152], 3).astype(BF16)
        qb_ref[:, 384:512] = p_ref[:, 1152:1280].astype(BF16)
        yq = hn(p_ref[:, 1792:2048], 4)
        yq = yq * c_ref[...] + _rope_partner(yq) * s_ref[...]
        qd_ref[:, 0:256] = (yq * ATTN_SCALE).astype(BF16)
        yk = hn(p_ref[:, 2048:2176], 5)
        yk = yk * c_ref[:, 0:128] + _rope_partner(yk) * s_ref[:, 0:128]
        qd_ref[:, 256:384] = yk.astype(BF16)
        qd_ref[:, 384:512] = p_ref[:, 2176:2304].astype(BF16)

    return pl.pallas_call(
        body, name=name, grid=(t // tm,),
        in_specs=[pl.BlockSpec((tm, IN_WIDTH), lambda i: (i, 0)), pl.BlockSpec((8, 256), lambda i: (0, 0)),
                  pl.BlockSpec((tm, 256), lambda i: (i % nrow, 0)), pl.BlockSpec((tm, 256), lambda i: (i % nrow, 0))],
        out_specs=[pl.BlockSpec((tm, 768), lambda i: (i, 0)), pl.BlockSpec((tm, 512), lambda i: (i, 0)),
                   pl.BlockSpec((tm, 512), lambda i: (i, 0))],
        out_shape=[_sds((t, 768), BF16), _sds((t, 512), BF16), _sds((t, 512), BF16)],
        compiler_params=_cparams(PAR),
    )(proj, gains, cos, sins)


def _qkprep_bwd(proj, da, db, dd, dcu, dcv, gains, cos, sins, *, tm, name):
    t = proj.shape[0]
    nrow = SEQ // tm
    nstep = t // tm
    flat = [a for cfg in da for a in cfg] + list(db) + list(dd) + [dcu, dcv]

    def body(*refs):
        p_ref, g_ref, c_ref, s_ref = refs[:4]
        d_refs = refs[4:4 + len(flat)]
        dp_ref, dg_ref = refs[4 + len(flat):]
        a_refs = d_refs[:9]
        dqb_ref, dkb_ref, dvb_ref, dqd_ref, dkd_ref, dvd_ref, dcu_ref, dcv_ref = d_refs[9:]
        e = _group_sum_matrix(256, True)
        step = pl.program_id(0)

        @pl.when(step == 0)
        def _():
            dg_ref[...] = jnp.zeros_like(dg_ref)

        def hn_bwd(x, dy, row):
            wd = x.shape[1]
            ee = e[:wd, :wd]
            g = g_ref[row:row + 1, :wd]
            r = lax.rsqrt(_seg_sum(x * x, ee) * (1.0 / HEAD_DIM) + EPS)
            gdy = dy * g
            dx = r * gdy - x * (r * r * r * (_seg_sum(gdy * x, ee) * (1.0 / HEAD_DIM)))
            dg_ref[row:row + 1, :wd] += jnp.sum(dy * x * r, axis=0, keepdims=True)
            return dx

        def rope_bwd(dy, wd):
            return dy * c_ref[:, :wd] + _rope_partner(dy * s_ref[:, :wd])

        dq = (a_refs[0][...] + a_refs[3][...] + a_refs[6][...]) * ATTN_SCALE
        dp_ref[:, 0:256] = hn_bwd(p_ref[:, 0:256], dq, 0).astype(BF16)
        dk = a_refs[1][...] + a_refs[4][...] + a_refs[7][...]
        dp_ref[:, 256:512] = hn_bwd(p_ref[:, 256:512], dk, 1).astype(BF16)
        dp_ref[:, 512:768] = (a_refs[2][...] + a_refs[5][...] + a_refs[8][...]).astype(BF16)
        dp_ref[:, 768:1024] = hn_bwd(p_ref[:, 768:1024], dqb_ref[...] * ATTN_SCALE, 2).astype(BF16)
        dp_ref[:, 1024:1152] = hn_bwd(p_ref[:, 1024:1152], dkb_ref[...], 3).astype(BF16)
        dp_ref[:, 1152:1280] = dvb_ref[...].astype(BF16)
        dp_ref[:, 1280:1536] = dcu_ref[...].astype(BF16)
        dp_ref[:, 1536:1792] = dcv_ref[...].astype(BF16)
        dp_ref[:, 1792:2048] = hn_bwd(p_ref[:, 1792:2048], rope_bwd(dqd_ref[...] * ATTN_SCALE, 256), 4).astype(BF16)
        dp_ref[:, 2048:2176] = hn_bwd(p_ref[:, 2048:2176], rope_bwd(dkd_ref[...], 128), 5).astype(BF16)
        dp_ref[:, 2176:2304] = dvd_ref[...].astype(BF16)

        @pl.when(step == nstep - 1)
        def _():
            dg_ref[...] = _seg_sum(dg_ref[...], _group_sum_matrix(256, False))

    in_specs = [pl.BlockSpec((tm, IN_WIDTH), lambda i: (i, 0)), pl.BlockSpec((8, 256), lambda i: (0, 0)),
                pl.BlockSpec((tm, 256), lambda i: (i % nrow, 0)), pl.BlockSpec((tm, 256), lambda i: (i % nrow, 0))]
    in_specs += [pl.BlockSpec((tm, a.shape[1]), lambda i: (i, 0)) for a in flat]
    return pl.pallas_call(
        body, name=name, grid=(nstep,), in_specs=in_specs,
        out_specs=[pl.BlockSpec((tm, IN_WIDTH), lambda i: (i, 0)), pl.BlockSpec((8, 256), lambda i: (0, 0))],
        out_shape=[_sds((t, IN_WIDTH), BF16), _sds((8, 256), F32)],
        compiler_params=_cparams(ARB),
    )(proj, gains, cos, sins, *flat)


def _band_specs(src_w, dil, seq_len, spec):
    width, idx = spec
    per_row = src_w // width
    return pl.BlockSpec((None, seq_len, width), lambda b, r: (b, 0, r * per_row + idx))


def _fill_padded(dst, src_ref, rad, seq_len):
    z = jnp.zeros((rad, dst.shape[1]), dst.dtype)
    dst[0:rad, :] = z
    dst[rad + seq_len:rad + seq_len + rad, :] = z
    dst[rad:rad + seq_len, :] = src_ref[...]


def _band_fwd(src, qs, ks, vs, bias, sink, *, dil, blk, rad, nh, nkv, name):
    t, w = src.shape
    bl = t // SEQ
    sl = SEQ // dil
    kw = blk + 2 * rad
    nb = sl // blk
    rep = nh // nkv
    has_sink = sink is not None

    def body(*refs):
        q_ref, k_ref, v_ref, b_ref = refs[:4]
        s_ref = refs[4] if has_sink else None
        o_ref, l_ref, kp, vp = refs[4 + has_sink:]
        _fill_padded(kp, k_ref, rad, sl)
        _fill_padded(vp, v_ref, rad, sl)

        def blk_body(i, carry):
            r0 = pl.multiple_of(i * blk, blk)
            qb = q_ref[pl.ds(r0, blk), :]
            kwin = kp[pl.ds(r0, kw), :]
            vwin = vp[pl.ds(r0, kw), :]
            col = r0 - rad + lax.broadcasted_iota(jnp.int32, (blk, kw), 1)
            neg = jnp.where((col >= 0) & (col < sl), 0.0, NEG_INF).astype(F32)
            for h in range(nh):
                g = h // rep
                hs = slice(h * HEAD_DIM, (h + 1) * HEAD_DIM)
                gs = slice(g * HEAD_DIM, (g + 1) * HEAD_DIM)
                s = lax.dot_general(qb[:, hs], kwin[:, gs], NT, preferred_element_type=F32)
                s = s + b_ref[h] + neg
                m = jnp.max(s, axis=1, keepdims=True)
                if has_sink:
                    sk = s_ref[h][0:1, 0:1]
                    m = jnp.maximum(m, sk)
                p = jnp.exp(s - m)
                den = jnp.sum(p, axis=1, keepdims=True)
                if has_sink:
                    den = den + jnp.exp(sk - m)
                o = jnp.dot(p.astype(BF16), vwin[:, gs], preferred_element_type=F32) / den
                o_ref[pl.ds(r0, blk), hs] = o
                l_ref[pl.ds(r0, blk), hs] = jnp.broadcast_to(m + jnp.log(den), (blk, HEAD_DIM))
            return carry

        lax.fori_loop(0, nb, blk_body, 0)

    in_specs = [_band_specs(w, dil, sl, qs), _band_specs(w, dil, sl, ks), _band_specs(w, dil, sl, vs),
                pl.BlockSpec((nh, blk, kw), lambda b, r: (0, 0, 0))]
    args = [src.reshape(bl, sl, dil * w)] * 3 + [bias]
    if has_sink:
        in_specs.append(pl.BlockSpec((nh, 8, 128), lambda b, r: (0, 0, 0)))
        args.append(sink)
    o, lse = pl.pallas_call(
        body, name=name, grid=(bl, dil), in_specs=in_specs,
        out_specs=[pl.BlockSpec((None, sl, 256), lambda b, r: (b, 0, r))] * 2,
        out_shape=[_sds((bl, sl, dil * 256), F32)] * 2,
        scratch_shapes=[pltpu.VMEM((sl + 2 * rad, ks[0]), BF16), pltpu.VMEM((sl + 2 * rad, vs[0]), BF16)],
        compiler_params=_cparams(PAR, PAR),
    )(*args)
    return o.reshape(t, 256), lse.reshape(t, 256)


def _band_bwd(src, qs, ks, vs, bias, sink, dycat, dcol, lse, delta, *, dil, blk, rad, nh, nkv, name):
    t, w = src.shape
    bl = t // SEQ
    sl = SEQ // dil
    kw = blk + 2 * rad
    nb = sl // blk
    rep = nh // nkv
    has_sink = sink is not None
    wk, wv = ks[0], vs[0]

    def body(*refs):
        q_ref, k_ref, v_ref, b_ref = refs[:4]
        s_ref = refs[4] if has_sink else None
        do_ref, l_ref, dl_ref = refs[4 + has_sink:7 + has_sink]
        outs = refs[7 + has_sink:]
        if has_sink:
            dq_ref, dk_ref, dv_ref, db_ref, dsk_ref, kp, vp, dka, dva = outs
        else:
            dq_ref, dk_ref, dv_ref, db_ref, kp, vp, dka, dva = outs

        @pl.when((pl.program_id(0) == 0) & (pl.program_id(1) == 0))
        def _():
            db_ref[...] = jnp.zeros_like(db_ref)
            if has_sink:
                dsk_ref[...] = jnp.zeros_like(dsk_ref)

        _fill_padded(kp, k_ref, rad, sl)
        _fill_padded(vp, v_ref, rad, sl)
        dka[...] = jnp.zeros_like(dka)
        dva[...] = jnp.zeros_like(dva)

        def blk_body(i, carry):
            r0 = pl.multiple_of(i * blk, blk)
            qb = q_ref[pl.ds(r0, blk), :]
            kwin = kp[pl.ds(r0, kw), :]
            vwin = vp[pl.ds(r0, kw), :]
            dob = do_ref[pl.ds(r0, blk), :].astype(BF16)
            lb = l_ref[pl.ds(r0, blk), :]
            dlb = dl_ref[pl.ds(r0, blk), :]
            col = r0 - rad + lax.broadcasted_iota(jnp.int32, (blk, kw), 1)
            neg = jnp.where((col >= 0) & (col < sl), 0.0, NEG_INF).astype(F32)
            for h in range(nh):
                g = h // rep
                hs = slice(h * HEAD_DIM, (h + 1) * HEAD_DIM)
                gs = slice(g * HEAD_DIM, (g + 1) * HEAD_DIM)
                qh, kh, vh, doh = qb[:, hs], kwin[:, gs], vwin[:, gs], dob[:, hs]
                lh = lb[:, h * HEAD_DIM:h * HEAD_DIM + 1]
                dlh = dlb[:, h * HEAD_DIM:h * HEAD_DIM + 1]
                s = lax.dot_general(qh, kh, NT, preferred_element_type=F32) + b_ref[h] + neg
                p = jnp.exp(s - lh)
                dp = lax.dot_general(doh, vh, NT, preferred_element_type=F32)
                ds = p * (dp - dlh)
                dsb = ds.astype(BF16)
                dq_ref[pl.ds(r0, blk), hs] = jnp.dot(dsb, kh, preferred_element_type=F32)
                dka[pl.ds(r0, kw), gs] += lax.dot_general(dsb, qh, TN, preferred_element_type=F32)
                dva[pl.ds(r0, kw), gs] += lax.dot_general(p.astype(BF16), doh, TN, preferred_element_type=F32)
                db_ref[h] += ds
                if has_sink:
                    ps = jnp.exp(s_ref[h][0:1, 0:1] - lh)
                    dsk_ref[h] += jnp.broadcast_to(-jnp.sum(ps * dlh, axis=0, keepdims=True), (8, 128))
            return carry

        lax.fori_loop(0, nb, blk_body, 0)
        dk_ref[...] = dka[rad:rad + sl, :]
        dv_ref[...] = dva[rad:rad + sl, :]

    const3 = lambda b, r: (0, 0, 0)
    in_specs = [_band_specs(w, dil, sl, qs), _band_specs(w, dil, sl, ks), _band_specs(w, dil, sl, vs),
                pl.BlockSpec((nh, blk, kw), const3)]
    args = [src.reshape(bl, sl, dil * w)] * 3 + [bias]
    if has_sink:
        in_specs.append(pl.BlockSpec((nh, 8, 128), const3))
        args.append(sink)
    row = pl.BlockSpec((None, sl, 256), lambda b, r: (b, 0, r))
    in_specs += [pl.BlockSpec((None, sl, 256), lambda b, r: (b, 0, r * 4 + dcol)), row, row]
    args += [dycat.reshape(bl, sl, dil * 1024), lse.reshape(bl, sl, dil * 256), delta.reshape(bl, sl, dil * 256)]
    out_specs = [row, pl.BlockSpec((None, sl, wk), lambda b, r: (b, 0, r)),
                 pl.BlockSpec((None, sl, wv), lambda b, r: (b, 0, r)), pl.BlockSpec((nh, blk, kw), const3)]
    out_shape = [_sds((bl, sl, dil * 256), F32), _sds((bl, sl, dil * wk), F32), _sds((bl, sl, dil * wv), F32),
                 _sds((nh, blk, kw), F32)]
    if has_sink:
        out_specs.append(pl.BlockSpec((nh, 8, 128), const3))
        out_shape.append(_sds((nh, 8, 128), F32))
    outs = pl.pallas_call(
        body, name=name, grid=(bl, dil), in_specs=in_specs, out_specs=out_specs, out_shape=out_shape,
        scratch_shapes=[pltpu.VMEM((sl + 2 * rad, wk), BF16), pltpu.VMEM((sl + 2 * rad, wv), BF16),
                        pltpu.VMEM((sl + 2 * rad, wk), F32), pltpu.VMEM((sl + 2 * rad, wv), F32)],
        compiler_params=_cparams(ARB, ARB),
    )(*args)
    res = [outs[0].reshape(t, 256), outs[1].reshape(t, wk), outs[2].reshape(t, wv), outs[3]]
    if has_sink:
        res.append(outs[4])
    return res


def _combine_a(os_, ls_, *, tm, name):
    t = os_[0].shape[0]

    def body(o1, o2, o3, l1, l2, l3, y_ref, lt_ref):
        a, b, c = l1[...], l2[...], l3[...]
        m = jnp.maximum(jnp.maximum(a, b), c)
        ea, eb, ec = jnp.exp(a - m), jnp.exp(b - m), jnp.exp(c - m)
        den = ea + eb + ec
        y_ref[...] = (ea / den) * o1[...] + (eb / den) * o2[...] + (ec / den) * o3[...]
        lt_ref[...] = m + jnp.log(den)

    spec = pl.BlockSpec((tm, 256), lambda i: (i, 0))
    return pl.pallas_call(
        body, name=name, grid=(t // tm,), in_specs=[spec] * 6, out_specs=[spec] * 2,
        out_shape=[_sds((t, 256), F32)] * 2, compiler_params=_cparams(PAR),
    )(*os_, *ls_)


def _deltas(dycat, ya, yb, yd, *, tm, name):
    t = ya.shape[0]

    def body(dy_ref, ya_ref, yb_ref, yd_ref, da_ref, db_ref, dd_ref):
        e = _group_sum_matrix(256, True)
        da_ref[...] = _seg_sum(dy_ref[:, 0:256] * ya_ref[...], e)
        db_ref[...] = _seg_sum(dy_ref[:, 256:512] * yb_ref[...], e)
        dd_ref[...] = _seg_sum(dy_ref[:, 768:1024] * yd_ref[...], e)

    spec = pl.BlockSpec((tm, 256), lambda i: (i, 0))
    return pl.pallas_call(
        body, name=name, grid=(t // tm,), in_specs=[pl.BlockSpec((tm, 1024), lambda i: (i, 0))] + [spec] * 3,
        out_specs=[spec] * 3, out_shape=[_sds((t, 256), F32)] * 3, compiler_params=_cparams(PAR),
    )(dycat, ya, yb, yd)


def _dense_fwd(qd, *, tq, name):
    t = qd.shape[0]
    bl = t // SEQ
    nq = SEQ // tq

    def body(q_ref, k_ref, v_ref, o_ref, l_ref):
        q = q_ref[...]
        for g in range(2):
            h0, h1 = 2 * g, 2 * g + 1
            q2 = jnp.concatenate([q[:, h0 * 64:(h0 + 1) * 64], q[:, h1 * 64:(h1 + 1) * 64]], axis=0)
            kg = k_ref[:, g * 64:(g + 1) * 64]
            vg = v_ref[:, g * 64:(g + 1) * 64]
            s = lax.dot_general(q2, kg, NT, preferred_element_type=F32)
            m = jnp.max(s, axis=1, keepdims=True)
            p = jnp.exp(s - m)
            den = jnp.sum(p, axis=1, keepdims=True)
            o2 = jnp.dot(p.astype(BF16), vg, preferred_element_type=F32) / den
            l2 = jnp.broadcast_to(m + jnp.log(den), (2 * tq, 64))
            o_ref[:, h0 * 64:(h0 + 1) * 64] = o2[:tq]
            o_ref[:, h1 * 64:(h1 + 1) * 64] = o2[tq:]
            l_ref[:, h0 * 64:(h0 + 1) * 64] = l2[:tq]
            l_ref[:, h1 * 64:(h1 + 1) * 64] = l2[tq:]

    q3 = qd.reshape(bl, SEQ, 512)
    o, lse = pl.pallas_call(
        body, name=name, grid=(bl, nq),
        in_specs=[pl.BlockSpec((None, tq, 256), lambda b, i: (b, i, 0)),
                  pl.BlockSpec((None, SEQ, 128), lambda b, i: (b, 0, 2)),
                  pl.BlockSpec((None, SEQ, 128), lambda b, i: (b, 0, 3))],
        out_specs=[pl.BlockSpec((None, tq, 256), lambda b, i: (b, i, 0))] * 2,
        out_shape=[_sds((bl, SEQ, 256), F32)] * 2,
        compiler_params=_cparams(PAR, PAR),
    )(q3, q3, q3)
    return o.reshape(t, 256), lse.reshape(t, 256)


def _dense_bwd(qd, dycat, lse, delta, *, tq, name):
    t = qd.shape[0]
    bl = t // SEQ
    nq = SEQ // tq

    def body(q_ref, k_ref, v_ref, do_ref, l_ref, dl_ref, dq_ref, dk_ref, dv_ref):
        @pl.when(pl.program_id(1) == 0)
        def _():
            dk_ref[...] = jnp.zeros_like(dk_ref)
            dv_ref[...] = jnp.zeros_like(dv_ref)

        q = q_ref[...]
        do = do_ref[...].astype(BF16)
        lv = l_ref[...]
        dlv = dl_ref[...]
        for g in range(2):
            h0, h1 = 2 * g, 2 * g + 1
            q2 = jnp.concatenate([q[:, h0 * 64:(h0 + 1) * 64], q[:, h1 * 64:(h1 + 1) * 64]], axis=0)
            do2 = jnp.concatenate([do[:, h0 * 64:(h0 + 1) * 64], do[:, h1 * 64:(h1 + 1) * 64]], axis=0)
            l2 = jnp.concatenate([lv[:, h0 * 64:h0 * 64 + 1], lv[:, h1 * 64:h1 * 64 + 1]], axis=0)
            dl2 = jnp.concatenate([dlv[:, h0 * 64:h0 * 64 + 1], dlv[:, h1 * 64:h1 * 64 + 1]], axis=0)
            kg = k_ref[:, g * 64:(g + 1) * 64]
            vg = v_ref[:, g * 64:(g + 1) * 64]
            s = lax.dot_general(q2, kg, NT, preferred_element_type=F32)
            p = jnp.exp(s - l2)
            dp = lax.dot_general(do2, vg, NT, preferred_element_type=F32)
            ds = (p * (dp - dl2)).astype(BF16)
            dq2 = jnp.dot(ds, kg, preferred_element_type=F32)
            dq_ref[:, h0 * 64:(h0 + 1) * 64] = dq2[:tq]
            dq_ref[:, h1 * 64:(h1 + 1) * 64] = dq2[tq:]
            dk_ref[:, g * 64:(g + 1) * 64] += lax.dot_general(ds, q2, TN, preferred_element_type=F32)
            dv_ref[:, g * 64:(g + 1) * 64] += lax.dot_general(p.astype(BF16), do2, TN, preferred_element_type=F32)

    q3 = qd.reshape(bl, SEQ, 512)
    tile = pl.BlockSpec((None, tq, 256), lambda b, i: (b, i, 0))
    full = pl.BlockSpec((None, SEQ, 128), lambda b, i: (b, 0, 0))
    dq, dk, dv = pl.pallas_call(
        body, name=name, grid=(bl, nq),
        in_specs=[tile, pl.BlockSpec((None, SEQ, 128), lambda b, i: (b, 0, 2)),
                  pl.BlockSpec((None, SEQ, 128), lambda b, i: (b, 0, 3)),
                  pl.BlockSpec((None, tq, 256), lambda b, i: (b, i, 3)), tile, tile],
        out_specs=[tile, full, full],
        out_shape=[_sds((bl, SEQ, 256), F32), _sds((bl, SEQ, 128), F32), _sds((bl, SEQ, 128), F32)],
        compiler_params=_cparams(PAR, ARB),
    )(q3, q3, q3, dycat.reshape(bl, SEQ, 1024), lse.reshape(bl, SEQ, 256), delta.reshape(bl, SEQ, 256))
    return dq.reshape(t, 256), dk.reshape(t, 128), dv.reshape(t, 128)


def _c_norm(cv, gam, bet):
    vg = _gelu(cv)
    mu = jnp.mean(vg, axis=-1, keepdims=True)
    xc = vg - mu
    r = lax.rsqrt(jnp.mean(xc * xc, axis=-1, keepdims=True) + EPS)
    xhat = xc * r
    return xhat * gam + bet, xhat, r


def _c_fwd(proj, gam, bet, ws, bst, *, tm, name):
    t = proj.shape[0]
    nch = tm // C_CHUNK

    def body(u_ref, v_ref, g_ref, b_ref, ws_ref, bs_ref, y_ref):
        vn, _, _ = _c_norm(v_ref[...], g_ref[...], b_ref[...])
        vnb = vn.astype(BF16)
        for c in range(nch):
            rows = slice(c * C_CHUNK, (c + 1) * C_CHUNK)
            for g in range(C_GROUPS):
                gs = slice(g * 64, (g + 1) * 64)
                mixed = jnp.dot(ws_ref[g], vnb[rows, gs], preferred_element_type=F32) + bs_ref[:, gs]
                y_ref[rows, gs] = _gelu(u_ref[rows, gs]) * mixed

    vec = pl.BlockSpec((1, 256), lambda i: (0, 0))
    return pl.pallas_call(
        body, name=name, grid=(t // tm,),
        in_specs=[pl.BlockSpec((tm, 256), lambda i: (i, 5)), pl.BlockSpec((tm, 256), lambda i: (i, 6)), vec, vec,
                  pl.BlockSpec((C_GROUPS, C_CHUNK, C_CHUNK), lambda i: (0, 0, 0)),
                  pl.BlockSpec((C_CHUNK, 256), lambda i: (0, 0))],
        out_specs=pl.BlockSpec((tm, 256), lambda i: (i, 0)), out_shape=_sds((t, 256), F32),
        compiler_params=_cparams(PAR),
    )(proj, proj, gam, bet, ws, bst)


def _c_bwd(proj, dycat, gam, bet, ws, wst, bst, *, tm, name):
    t = proj.shape[0]
    nch = tm // C_CHUNK
    nstep = t // tm

    def body(u_ref, v_ref, dy_ref, g_ref, b_ref, ws_ref, wst_ref, bs_ref,
             du_ref, dv_ref, dws_ref, dbs_ref, dg_ref, db_ref, dvn_s):
        step = pl.program_id(0)

        @pl.when(step == 0)
        def _():
            dws_ref[...] = jnp.zeros_like(dws_ref)
            dbs_ref[...] = jnp.zeros_like(dbs_ref)
            dg_ref[...] = jnp.zeros_like(dg_ref)
            db_ref[...] = jnp.zeros_like(db_ref)

        cv = v_ref[...]
        gam_v = g_ref[...]
        vn, xhat, r = _c_norm(cv, gam_v, b_ref[...])
        vnb = vn.astype(BF16)
        for c in range(nch):
            rows = slice(c * C_CHUNK, (c + 1) * C_CHUNK)
            for g in range(C_GROUPS):
                gs = slice(g * 64, (g + 1) * 64)
                cu = u_ref[rows, gs]
                dy = dy_ref[rows, gs]
                mixed = jnp.dot(ws_ref[g], vnb[rows, gs], preferred_element_type=F32) + bs_ref[:, gs]
                du_ref[rows, gs] = dy * mixed * _gelu_grad(cu)
                dmix = dy * _gelu(cu)
                dbs_ref[:, gs] += dmix
                dmb = dmix.astype(BF16)
                dws_ref[g] += lax.dot_general(dmb, vnb[rows, gs], NT, preferred_element_type=F32)
                dvn_s[rows, gs] = jnp.dot(wst_ref[g], dmb, preferred_element_type=F32)
        dvn = dvn_s[...]
        dg_ref[...] += jnp.sum(dvn * xhat, axis=0, keepdims=True)
        db_ref[...] += jnp.sum(dvn, axis=0, keepdims=True)
        dxh = dvn * gam_v
        dvg = r * (dxh - jnp.mean(dxh, axis=-1, keepdims=True) - xhat * jnp.mean(dxh * xhat, axis=-1, keepdims=True))
        dv_ref[...] = dvg * _gelu_grad(cv)

        @pl.when(step == nstep - 1)
        def _():
            dbs_ref[...] = _seg_sum(dbs_ref[...], _group_sum_matrix(256, True))

    vec = pl.BlockSpec((1, 256), lambda i: (0, 0))
    mat = pl.BlockSpec((C_GROUPS, C_CHUNK, C_CHUNK), lambda i: (0, 0, 0))
    bsp = pl.BlockSpec((C_CHUNK, 256), lambda i: (0, 0))
    tile = pl.BlockSpec((tm, 256), lambda i: (i, 0))
    return pl.pallas_call(
        body, name=name, grid=(nstep,),
        in_specs=[pl.BlockSpec((tm, 256), lambda i: (i, 5)), pl.BlockSpec((tm, 256), lambda i: (i, 6)),
                  pl.BlockSpec((tm, 256), lambda i: (i, 2)), vec, vec, mat, mat, bsp],
        out_specs=[tile, tile, mat, bsp, vec, vec],
        out_shape=[_sds((t, 256), F32), _sds((t, 256), F32), _sds((C_GROUPS, C_CHUNK, C_CHUNK), F32),
                   _sds((C_CHUNK, 256), F32), _sds((1, 256), F32), _sds((1, 256), F32)],
        scratch_shapes=[pltpu.VMEM((tm, 256), F32)],
        compiler_params=_cparams(ARB),
    )(proj, proj, dycat, gam, bet, ws, wst, bst)


FF_TC = 256
FF_NB = D_FF // FF_TC


def _shift_down(h):
    row = lax.broadcasted_iota(jnp.int32, h.shape, 0)
    return jnp.where(row == 0, 0.0, pltpu.roll(h, 1, 0))


def _shift_up(h):
    n = h.shape[0]
    row = lax.broadcasted_iota(jnp.int32, h.shape, 0)
    return jnp.where(row == n - 1, 0.0, pltpu.roll(h, n - 1, 0))


def _conv3(h, w_ref, b_ref):
    return w_ref[0:1, :] * _shift_down(h) + w_ref[1:2, :] * h + w_ref[2:3, :] * _shift_up(h) + b_ref[...]


def _conv_gate_fwd(h, cw, cb, *, name):
    t = h.shape[0]
    bl = t // SEQ

    def body(hg_ref, hu_ref, wg_ref, wu_ref, bg_ref, bu_ref, a_ref):
        cg = _conv3(hg_ref[...], wg_ref, bg_ref)
        cu = _conv3(hu_ref[...], wu_ref, bu_ref)
        a_ref[...] = (cg * _sigmoid(cg) * cu).astype(BF16)

    h3 = h.reshape(bl, SEQ, 2 * D_FF)
    act = pl.pallas_call(
        body, name=name, grid=(bl, FF_NB),
        in_specs=[pl.BlockSpec((None, SEQ, FF_TC), lambda b, j: (b, 0, j)),
                  pl.BlockSpec((None, SEQ, FF_TC), lambda b, j: (b, 0, j + FF_NB)),
                  pl.BlockSpec((3, FF_TC), lambda b, j: (0, j)), pl.BlockSpec((3, FF_TC), lambda b, j: (0, j + FF_NB)),
                  pl.BlockSpec((1, FF_TC), lambda b, j: (0, j)), pl.BlockSpec((1, FF_TC), lambda b, j: (0, j + FF_NB))],
        out_specs=pl.BlockSpec((None, SEQ, FF_TC), lambda b, j: (b, 0, j)),
        out_shape=_sds((bl, SEQ, D_FF), BF16),
        compiler_params=_cparams(PAR, PAR),
    )(h3, h3, cw, cw, cb, cb)
    return act.reshape(t, D_FF)


def _conv_gate_bwd(h, dact, cw, cb, *, name):
    t = h.shape[0]
    bl = t // SEQ
    nb2 = 2 * FF_NB

    def body(hs_ref, hp_ref, da_ref, ws_ref, wp_ref, bs_ref, bp_ref, dh_ref, dw_ref, db_ref):
        j, b = pl.program_id(0), pl.program_id(1)

        @pl.when(b == 0)
        def _():
            dw_ref[...] = jnp.zeros_like(dw_ref)
            db_ref[...] = jnp.zeros_like(db_ref)

        hs = hs_ref[...]
        cs = _conv3(hs, ws_ref, bs_ref)
        cp = _conv3(hp_ref[...], wp_ref, bp_ref)
        da = da_ref[...]
        sg = _sigmoid(cs)
        d_gate = da * cp * (sg * (1.0 + cs * (1.0 - sg)))
        d_up = da * (cp * _sigmoid(cp))
        dhc = jnp.where(j < FF_NB, d_gate, d_up)
        db_ref[...] += jnp.sum(dhc, axis=0, keepdims=True)
        dw_ref[0:1, :] += jnp.sum(dhc * _shift_down(hs), axis=0, keepdims=True)
        dw_ref[1:2, :] += jnp.sum(dhc * hs, axis=0, keepdims=True)
        dw_ref[2:3, :] += jnp.sum(dhc * _shift_up(hs), axis=0, keepdims=True)
        dh = ws_ref[0:1, :] * _shift_up(dhc) + ws_ref[1:2, :] * dhc + ws_ref[2:3, :] * _shift_down(dhc)
        dh_ref[...] = dh.astype(BF16)

    h3 = h.reshape(bl, SEQ, 2 * D_FF)
    da3 = dact.reshape(bl, SEQ, D_FF)
    self_c = lambda j, b: (0, j)
    part_c = lambda j, b: (0, (j + FF_NB) % nb2)
    dh, dcw, dcb = pl.pallas_call(
        body, name=name, grid=(nb2, bl),
        in_specs=[pl.BlockSpec((None, SEQ, FF_TC), lambda j, b: (b, 0, j)),
                  pl.BlockSpec((None, SEQ, FF_TC), lambda j, b: (b, 0, (j + FF_NB) % nb2)),
                  pl.BlockSpec((None, SEQ, FF_TC), lambda j, b: (b, 0, j % FF_NB)),
                  pl.BlockSpec((3, FF_TC), self_c), pl.BlockSpec((3, FF_TC), part_c),
                  pl.BlockSpec((1, FF_TC), self_c), pl.BlockSpec((1, FF_TC), part_c)],
        out_specs=[pl.BlockSpec((None, SEQ, FF_TC), lambda j, b: (b, 0, j)),
                   pl.BlockSpec((3, FF_TC), self_c), pl.BlockSpec((1, FF_TC), self_c)],
        out_shape=[_sds((bl, SEQ, 2 * D_FF), BF16), _sds((3, 2 * D_FF), F32), _sds((1, 2 * D_FF), F32)],
        compiler_params=_cparams(PAR, ARB),
    )(h3, h3, da3, cw, cw, cb, cb)
    return dh.reshape(t, 2 * D_FF), dcw, dcb


def _ple_fwd(x2, gain, wg, pe, wp, *, tm, tn, name):
    t, k = x2.shape
    n = wg.shape[1]

    def body(x_ref, g_ref, wg_ref, pe_ref, wp_ref, xr_ref, hn_ref, x3_ref, gt_ref, pp_ref, hn_s):
        @pl.when(pl.program_id(1) == 0)
        def _():
            x = x_ref[...]
            r = lax.rsqrt(jnp.mean(x * x, axis=-1, keepdims=True) + EPS)
            hn_s[...] = (x * r * g_ref[...]).astype(BF16)
            hn_ref[...] = hn_s[...]

        gate = _sigmoid(jnp.dot(hn_s[...], wg_ref[...], preferred_element_type=F32))
        pp = jnp.dot(pe_ref[...].astype(BF16), wp_ref[...], preferred_element_type=F32)
        gt_ref[...] = gate
        pp_ref[...] = pp
        x3_ref[...] = xr_ref[...] + pp * gate

    tile = pl.BlockSpec((tm, tn), lambda i, j: (i, j))
    return pl.pallas_call(
        body, name=name, grid=(t // tm, n // tn),
        in_specs=[pl.BlockSpec((tm, k), lambda i, j: (i, 0)), pl.BlockSpec((1, k), lambda i, j: (0, 0)),
                  pl.BlockSpec((k, tn), lambda i, j: (0, j)), pl.BlockSpec((tm, PLE_DIM), lambda i, j: (i, 0)),
                  pl.BlockSpec((PLE_DIM, tn), lambda i, j: (0, j)), tile],
        out_specs=[pl.BlockSpec((tm, k), lambda i, j: (i, 0)), tile, tile, tile],
        out_shape=[_sds((t, k), BF16), _sds((t, n), F32), _sds((t, n), F32), _sds((t, n), F32)],
        scratch_shapes=[pltpu.VMEM((tm, k), BF16)],
        compiler_params=_cparams(PAR, ARB),
    )(x2, gain, wg, pe, wp, x2)


def _ple_bwd_ew(dx3, gate, pp, *, tm, name):
    t, n = dx3.shape

    def body(d_ref, g_ref, p_ref, dz_ref, dpp_ref):
        d, g = d_ref[...], g_ref[...]
        dz_ref[...] = (d * p_ref[...] * g * (1.0 - g)).astype(BF16)
        dpp_ref[...] = (d * g).astype(BF16)

    spec = pl.BlockSpec((tm, n), lambda i: (i, 0))
    return pl.pallas_call(
        body, name=name, grid=(t // tm,), in_specs=[spec] * 3, out_specs=[spec] * 2,
        out_shape=[_sds((t, n), BF16)] * 2, compiler_params=_cparams(PAR),
    )(dx3, gate, pp)


def _loss_head(y, tgt, *, tm, name):
    t, d = y.shape

    def body(y_ref, t_ref, l_ref, dy_ref):
        @pl.when(pl.program_id(0) == 0)
        def _():
            l_ref[...] = jnp.zeros_like(l_ref)

        e = y_ref[...] - t_ref[...]
        dy_ref[...] = e * (1.0 / d)
        s = jnp.sum(jnp.sum(e * e, axis=1, keepdims=True), axis=0, keepdims=True)
        l_ref[...] += jnp.broadcast_to(s * (0.5 / d), (8, 128))

    spec = pl.BlockSpec((tm, d), lambda i: (i, 0))
    return pl.pallas_call(
        body, name=name, grid=(t // tm,), in_specs=[spec, spec],
        out_specs=[pl.BlockSpec((8, 128), lambda i: (0, 0)), spec],
        out_shape=[_sds((8, 128), F32), _sds((t, d), F32)], compiler_params=_cparams(ARB),
    )(y, tgt)


def _bucket_reduce(onehot, dbias_flat, *, name):
    nb, _ = onehot.shape
    h = dbias_flat.shape[1]

    def body(a_ref, b_ref, o_ref):
        o_ref[...] = jnp.dot(a_ref[...], b_ref[...], precision=HI, preferred_element_type=F32)

    return pl.pallas_call(body, name=name, out_shape=_sds((nb, h), F32))(onehot, dbias_flat)


def _adamw_math(w, g, m, v):
    m = ADAM_B1 * m + (1.0 - ADAM_B1) * g
    v = ADAM_B2 * v + (1.0 - ADAM_B2) * (g * g)
    m_hat = m / (1.0 - ADAM_B1 ** ADAM_STEP)
    v_hat = v / (1.0 - ADAM_B2 ** ADAM_STEP)
    delta = -ADAM_LR * (m_hat / (jnp.sqrt(v_hat) + ADAM_EPS) + ADAM_WD * w)
    return delta, m, v


def _adamw_reduce(parts, w, m, v, *, tr, name):
    r, c = w.shape

    def body(p_ref, w_ref, m_ref, v_ref, g_ref, d_ref, nm_ref, nv_ref):
        g = p_ref[0].astype(F32)
        for k in range(1, N_DEV):
            g = g + p_ref[k].astype(F32)
        d, nm, nv = _adamw_math(w_ref[...], g, m_ref[...], v_ref[...])
        g_ref[...] = g
        d_ref[...] = d
        nm_ref[...] = nm
        nv_ref[...] = nv

    spec = pl.BlockSpec((tr, c), lambda i: (i, 0))
    return pl.pallas_call(
        body, name=name, grid=(r // tr,),
        in_specs=[pl.BlockSpec((N_DEV, tr, c), lambda i: (0, i, 0)), spec, spec, spec],
        out_specs=[spec] * 4, out_shape=[_sds((r, c), F32)] * 4, compiler_params=_cparams(PAR),
    )(parts, w, m, v)


def _adamw_plain(g, w, m, v, *, name):
    def body(g_ref, w_ref, m_ref, v_ref, d_ref, nm_ref, nv_ref):
        d, nm, nv = _adamw_math(w_ref[...], g_ref[...], m_ref[...], v_ref[...])
        d_ref[...] = d
        nm_ref[...] = nm
        nv_ref[...] = nv

    return pl.pallas_call(body, name=name, out_shape=[_sds(w.shape, F32)] * 3)(g, w, m, v)


def _mesh_pos():
    return lax.axis_index("x"), lax.axis_index("y"), lax.axis_index("c")


def _allgather_body(x_refs, out_refs, send_sems, recv_sems, local_sems, slot):
    x, y, c = _mesh_pos()
    me, sibling = (x, y, c), (x, y, 1 - c)
    chips = [(1 - x, y), (x, 1 - y), (1 - x, 1 - y)]
    waits = []
    for a, (x_ref, out_ref) in enumerate(zip(x_refs, out_refs)):
        def copy(k, block, to, src=None, out_ref=out_ref, a=a):
            return pltpu.make_async_remote_copy(
                src_ref=slot(out_ref, block) if src is None else src, dst_ref=slot(out_ref, block),
                send_sem=send_sems.at[a, k], recv_sem=recv_sems.at[a, k], device_id=to, device_id_type=MESH)

        mine = pltpu.make_async_copy(x_ref, slot(out_ref, me), local_sems.at[a])
        mine.start()
        first = [copy(0, me, sibling, src=x_ref)]
        first += [copy(1 + j, me, (*chip, c), src=x_ref) for j, chip in enumerate(chips)]
        for cp in first:
            cp.start()
        waits.append((copy, mine, first))
    sends = []
    for copy, mine, first in waits:
        passed = [copy(4 + j, (*chip, c), sibling) for j, chip in enumerate(chips)]
        for j, chip in enumerate(chips):
            copy(1 + j, (*chip, c), me).wait_recv()
            passed[j].start()
        sends.append(passed)
    for (copy, mine, first), passed in zip(waits, sends):
        copy(0, sibling, me).wait_recv()
        for j, chip in enumerate(chips):
            copy(4 + j, (*chip, 1 - c), me).wait_recv()
        for cp in first + passed:
            cp.wait_send()
        mine.wait()


def _allgather_hbm(xs, *, name):
    na = len(xs)

    def body(*refs):
        x_refs, out_refs = refs[:na], refs[na:2 * na]
        send_sems, recv_sems, local_sems = refs[2 * na:]
        _allgather_body(x_refs, out_refs, send_sems, recv_sems, local_sems,
                        lambda ref, pos: ref.at[4 * pos[0] + 2 * pos[1] + pos[2]])

    hbm = pl.BlockSpec(memory_space=pltpu.HBM)
    return pl.pallas_call(
        body, name=name, in_specs=[hbm] * na, out_specs=[hbm] * na,
        out_shape=[_sds((N_DEV,) + x.shape, x.dtype) for x in xs],
        scratch_shapes=[pltpu.SemaphoreType.DMA((na, 7)), pltpu.SemaphoreType.DMA((na, 7)),
                        pltpu.SemaphoreType.DMA((na,))],
    )(*xs)


def _allgather_vmem(x, *, reduce, name):
    r, c = x.shape

    def body(x_ref, out_ref, *rest):
        if reduce:
            gath, send_sems, recv_sems, local_sems = rest
        else:
            send_sems, recv_sems, local_sems = rest
            gath = out_ref
        _allgather_body([x_ref], [gath], send_sems, recv_sems, local_sems,
                        lambda ref, pos: ref.at[pl.ds((4 * pos[0] + 2 * pos[1] + pos[2]) * r, r), :])
        if reduce:
            acc = gath[0:r, :]
            for k in range(1, N_DEV):
                acc = acc + gath[k * r:(k + 1) * r, :]
            out_ref[...] = acc

    vm = pl.BlockSpec(memory_space=pltpu.VMEM)
    scratch = [pltpu.SemaphoreType.DMA((1, 7)), pltpu.SemaphoreType.DMA((1, 7)), pltpu.SemaphoreType.DMA((1,))]
    if reduce:
        scratch = [pltpu.VMEM((N_DEV * r, c), x.dtype)] + scratch
    return pl.pallas_call(
        body, name=name, in_specs=[vm], out_specs=vm,
        out_shape=_sds((r, c) if reduce else (N_DEV * r, c), x.dtype), scratch_shapes=scratch,
    )(x)


def _all_to_all_hbm(xs, *, name):
    na = len(xs)

    def body(*refs):
        x_refs, out_refs = refs[:na], refs[na:2 * na]
        send_sems, recv_sems, local_sems = refs[2 * na:]
        x, y, c = _mesh_pos()
        me = 4 * x + 2 * y + c
        rel = [(0, 0, 1), (1, 0, 0), (0, 1, 0), (1, 1, 0), (1, 0, 1), (0, 1, 1), (1, 1, 1)]
        copies = []
        for a, (x_ref, out_ref) in enumerate(zip(x_refs, out_refs)):
            own = pltpu.make_async_copy(x_ref.at[me], out_ref.at[me], local_sems.at[a])
            own.start()
            copies.append(own)
            for k, (fx, fy, fc) in enumerate(rel):
                px, py, pc = x ^ fx, y ^ fy, c ^ fc
                peer = 4 * px + 2 * py + pc
                cp = pltpu.make_async_remote_copy(
                    src_ref=x_ref.at[peer], dst_ref=out_ref.at[me], send_sem=send_sems.at[a, k],
                    recv_sem=recv_sems.at[a, k], device_id=(px, py, pc), device_id_type=MESH)
                cp.start()
                copies.append(cp)
        for cp in copies:
            cp.wait()

    hbm = pl.BlockSpec(memory_space=pltpu.HBM)
    return pl.pallas_call(
        body, name=name, in_specs=[hbm] * na, out_specs=[hbm] * na,
        out_shape=[_sds(x.shape, x.dtype) for x in xs],
        scratch_shapes=[pltpu.SemaphoreType.DMA((na, 7)), pltpu.SemaphoreType.DMA((na, 7)),
                        pltpu.SemaphoreType.DMA((na,))],
    )(*xs)


def _t5_bucket(rel):
    nb = REL_BUCKETS // 2
    ret = jnp.where(rel > 0, nb, 0)
    n = jnp.abs(rel)
    max_exact = nb // 2
    nf = jnp.maximum(n, 1).astype(F32)
    large = max_exact + (jnp.log(nf / max_exact) / math.log(REL_MAX_DIST / max_exact)
                         * (nb - max_exact)).astype(jnp.int32)
    large = jnp.minimum(large, nb - 1)
    return ret + jnp.where(n < max_exact, n, large)


def _band_pattern(block, radius, dil):
    kw = block + 2 * radius
    rel = jnp.arange(kw)[None, :] - radius - jnp.arange(block)[:, None]
    return _t5_bucket(rel * dil), jnp.abs(rel) <= radius


def _bias_from_table(table, bucket, band):
    b = jnp.transpose(table.astype(F32)[bucket], (2, 0, 1))
    return jnp.where(band[None], b, NEG_INF)


def _rope_tables():
    lane = np.arange(64)
    seg, j = lane // 32, lane % 32
    inv = ROPE_THETA ** (-jnp.arange(0, 32, 2, dtype=F32) / 32)
    tpos = jnp.arange(SEQ)
    pos = jnp.where(jnp.asarray(seg)[None, :] == 0, (tpos // GRID_W)[:, None], (tpos % GRID_W)[:, None])
    ang = pos.astype(F32) * inv[jnp.asarray(j % 16)][None, :]
    cos = jnp.cos(ang)
    sins = jnp.where(jnp.asarray(j)[None, :] < 16, -jnp.sin(ang), jnp.sin(ang))
    return jnp.tile(cos, (1, 4)), jnp.tile(sins, (1, 4))


A_Q, A_K, A_V = (256, 0), (256, 1), (256, 2)
B_Q, B_K, B_V = (256, 0), (128, 2), (128, 3)


def _local_step(x, pe, tgt, rel_bias, wts):
    cos, sins = _rope_tables()
    pats_a = [_band_pattern(DIL_BLOCK, window // (2 * dil), dil) for window, dil in DILATED_CFGS]
    pat_b = _band_pattern(SWA_BLOCK, SWA_RADIUS, 1)
    table_a, table_b = rel_bias[:, :4], rel_bias[:, 4:]
    bias_a = [_bias_from_table(table_a, *pt) for pt in pats_a]
    bias_b = _bias_from_table(table_b, *pat_b)
    a_cfg = [dict(dil=dil, blk=DIL_BLOCK, rad=window // (2 * dil), nh=4, nkv=4) for window, dil in DILATED_CFGS]
    b_cfg = dict(dil=1, blk=SWA_BLOCK, rad=SWA_RADIUS, nh=4, nkv=2)

    saved = []
    for li in range(DEPTH):
        w = wts[li]
        hn0, proj = _norm_mm((x,), w["g_mix"], w["w_in"], None, tm=512, tn=768, name="mix_in_fwd")
        qa, qb, qd = _qkprep_fwd(proj, w["qk_gains"], cos, sins, tm=512, name="qkprep_fwd")
        oa, la = [], []
        for ci, cfg in enumerate(a_cfg):
            o, l = _band_fwd(qa, A_Q, A_K, A_V, bias_a[ci], None, name=f"band_a{ci}_fwd", **cfg)
            oa.append(o)
            la.append(l)
        ya, lse_a = _combine_a(oa, la, tm=1024, name="combine_a")
        yb, lse_b = _band_fwd(qb, B_Q, B_K, B_V, bias_b, w["sink_t"], name="band_b_fwd", **b_cfg)
        yc = _c_fwd(proj, w["c_g"], w["c_b"], w["c_ws"], w["c_bst"], tm=512, name="c_fwd")
        yd, lse_d = _dense_fwd(qd, tq=128, name="dense_fwd")
        mixed, x1 = _norm_mm((ya, yb, yc, yd), w["out_gain"], w["w_out"], x, tm=512, tn=512, name="mix_out_fwd")
        hn1, h = _norm_mm((x1,), w["g_ffn"], w["w_up"], None, tm=512, tn=512, name="ffn_up_fwd")
        act = _conv_gate_fwd(h, w["conv_w"], w["conv_b"], name="conv_gate_fwd")
        x2 = _mm(act, w["w_down"], "nn", x1, tm=512, tn=512, out_dtype=F32, name="ffn_down_fwd")
        hn2, x3, gate, pp = _ple_fwd(x2, w["g_ple"], w["w_gate"], pe[li], w["w_proj"], tm=512, tn=512, name="ple_fwd")
        saved.append(dict(x0=x, hn0=hn0, proj=proj, qa=qa, qb=qb, qd=qd, ya=ya, lse_a=lse_a, yb=yb, lse_b=lse_b,
                          yc=yc, yd=yd, lse_d=lse_d, mixed=mixed, x1=x1, hn1=hn1, h=h, act=act, x2=x2, hn2=hn2,
                          gate=gate, pp=pp))
        x = x3

    loss_tile, dx = _loss_head(x, tgt, tm=512, name="loss_head")
    grads = [None] * DEPTH
    d_table_a = jnp.zeros((REL_BUCKETS, 4), F32)
    d_table_b = jnp.zeros((REL_BUCKETS, 4), F32)
    for li in reversed(range(DEPTH)):
        w, s = wts[li], saved[li]
        g = {}
        dz, dpp = _ple_bwd_ew(dx, s["gate"], s["pp"], tm=512, name="ple_bwd_ew")
        g["w_gate"] = _mm(s["hn2"], dz, "tn", None, tm=512, tn=512, out_dtype=BF16, name="dw_gate")
        g["w_proj"] = _mm(pe[li], dpp, "tn", None, tm=256, tn=512, out_dtype=BF16, name="dw_proj")
        dx2, g["g_ple"] = _mm_bt_normbwd(dz, w["w_gate"], (s["x2"],), w["g_ple"], dx, tm=512, tn=512, name="ple_bwd")
        g["w_down"] = _mm(s["act"], dx2, "tn", None, tm=256, tn=512, out_dtype=BF16, name="dw_down")
        dact = _mm(dx2, w["w_down"], "nt", None, tm=512, tn=256, out_dtype=F32, name="ffn_down_bwd")
        dh, g["conv_w"], g["conv_b"] = _conv_gate_bwd(s["h"], dact, w["conv_w"], w["conv_b"], name="conv_gate_bwd")
        g["w_up"] = _mm(s["hn1"], dh, "tn", None, tm=512, tn=512, out_dtype=BF16, name="dw_up")
        dx1, g["g_ffn"] = _mm_bt_normbwd(dh, w["w_up"], (s["x1"],), w["g_ffn"], dx2, tm=512, tn=512, name="ffn_up_bwd")
        g["w_out"] = _mm(s["mixed"], dx1, "tn", None, tm=512, tn=512, out_dtype=BF16, name="dw_out")
        dycat, g["out_gain"] = _mm_bt_normbwd(dx1, w["w_out"], (s["ya"], s["yb"], s["yc"], s["yd"]), w["out_gain"],
                                              None, tm=512, tn=512, name="mix_out_bwd")
        dl_a, dl_b, dl_d = _deltas(dycat, s["ya"], s["yb"], s["yd"], tm=512, name="deltas")
        da = []
        for ci, cfg in enumerate(a_cfg):
            dq, dk, dv, dbias = _band_bwd(s["qa"], A_Q, A_K, A_V, bias_a[ci], None, dycat, 0, s["lse_a"], dl_a,
                                          name=f"band_a{ci}_bwd", **cfg)
            da.append((dq, dk, dv))
            oh = (pats_a[ci][0].reshape(1, -1) == jnp.arange(REL_BUCKETS)[:, None]).astype(F32)
            d_table_a = d_table_a + _bucket_reduce(oh, dbias.reshape(4, -1).T, name=f"bucket_a{ci}")
        dqb, dkb, dvb, dbias_b, dsink = _band_bwd(s["qb"], B_Q, B_K, B_V, bias_b, w["sink_t"], dycat, 1, s["lse_b"],
                                                  dl_b, name="band_b_bwd", **b_cfg)
        oh = (pat_b[0].reshape(1, -1) == jnp.arange(REL_BUCKETS)[:, None]).astype(F32)
        d_table_b = d_table_b + _bucket_reduce(oh, dbias_b.reshape(4, -1).T, name="bucket_b")
        g["sink"] = dsink[:, 0, 0]
        dd = _dense_bwd(s["qd"], dycat, s["lse_d"], dl_d, tq=128, name="dense_bwd")
        dcu, dcv, g["c_ws"], dbs, g["c_g"], g["c_b"] = _c_bwd(s["proj"], dycat, w["c_g"], w["c_b"], w["c_ws"],
                                                               w["c_wst"], w["c_bst"], tm=512, name="c_bwd")
        g["c_bs"] = dbs[:, ::64].T
        dproj, dgains = _qkprep_bwd(s["proj"], da, (dqb, dkb, dvb), dd, dcu, dcv, w["qk_gains"], cos, sins,
                                    tm=512, name="qkprep_bwd")
        g["qk_gain"] = dgains[:6, :64].reshape(3, 2, HEAD_DIM)
        g["w_in"] = _mm(s["hn0"], dproj, "tn", None, tm=512, tn=768, out_dtype=BF16, name="dw_in")
        dx, g["g_mix"] = _mm_bt_normbwd(dproj, w["w_in"], (s["x0"],), w["g_mix"], dx1, tm=512, tn=768, name="mix_in_bwd")
        grads[li] = g
    d_rel_bias = jnp.concatenate([d_table_a, d_table_b], axis=1)
    return loss_tile[0, 0], dx, grads, d_rel_bias


WEIGHT_NAMES = ("rel_bias", "ln_mix_g", "w_in", "qk_gain", "sink", "c_norm_g", "c_norm_b", "c_ws", "c_bs", "out_gain",
                "w_out", "ln_ffn_g", "w_up", "conv_w", "conv_b", "w_down", "ln_ple_g", "w_ple_gate", "w_ple_proj")
COL_SHARDED = ("w_in", "w_up", "w_ple_proj")
ROW_SHARDED = ("w_out", "w_down", "w_ple_gate")
SMALL_SHARDED = ("conv_w", "out_gain")
REPLICATED = tuple(n for n in WEIGHT_NAMES if n not in COL_SHARDED + ROW_SHARDED + SMALL_SHARDED)
LOCAL_GRAD_KEY = {"ln_mix_g": "g_mix", "ln_ffn_g": "g_ffn", "ln_ple_g": "g_ple", "c_norm_g": "c_g", "c_norm_b": "c_b",
                  "w_ple_gate": "w_gate", "w_ple_proj": "w_proj"}


def _full_from_gathered(name, gathered, shard_shape):
    g = gathered.reshape((N_DEV,) + tuple(shard_shape))
    if name in ROW_SHARDED:
        g = jnp.transpose(g, (1, 0, 2, 3))
        return g.reshape(shard_shape[0], N_DEV * shard_shape[1], shard_shape[2])
    g = jnp.transpose(g, (1, 2, 0, 3))
    return g.reshape(shard_shape[0], shard_shape[1], N_DEV * shard_shape[2])


def _slots_from_full(name, full, shard_shape):
    d, r, c = shard_shape
    if name in ROW_SHARDED:
        g = jnp.transpose(full.reshape(d, N_DEV, r, c), (1, 0, 2, 3))
    else:
        g = jnp.transpose(full.reshape(d, r, N_DEV, c), (2, 0, 1, 3))
    return g.reshape(N_DEV, d * r, c)


def _pack_rows(arrays):
    flat = jnp.concatenate([a.reshape(-1).astype(F32) for a in arrays])
    rows = -(-flat.shape[0] // 1024) * 8
    return jnp.pad(flat, (0, rows * 128 - flat.shape[0])).reshape(rows, 128)


def _unpack_rows(packed, shapes):
    flat = packed.reshape(-1)
    out, off = [], 0
    for shp in shapes:
        n = int(np.prod(shp))
        out.append(flat[off:off + n].reshape(shp))
        off += n
    return out


def kernel(x, p, rel_bias, ln_mix_g, w_in, qk_gain, sink, c_norm_g, c_norm_b, c_ws, c_bs, out_gain, w_out, ln_ffn_g, w_up, conv_w, conv_b, w_down, ln_ple_g, w_ple_gate, w_ple_proj, loss_target, m_rel_bias, m_ln_mix_g, m_w_in, m_qk_gain, m_sink, m_c_norm_g, m_c_norm_b, m_c_ws, m_c_bs, m_out_gain, m_w_out, m_ln_ffn_g, m_w_up, m_conv_w, m_conv_b, m_w_down, m_ln_ple_g, m_w_ple_gate, m_w_ple_proj, v_rel_bias, v_ln_mix_g, v_w_in, v_qk_gain, v_sink, v_c_norm_g, v_c_norm_b, v_c_ws, v_c_bs, v_out_gain, v_w_out, v_ln_ffn_g, v_w_up, v_conv_w, v_conv_b, v_w_down, v_ln_ple_g, v_w_ple_gate, v_w_ple_proj):
    env = dict(locals())
    wt = {n: env[n] for n in WEIGHT_NAMES}
    mom_m = {n: env["m_" + n] for n in WEIGHT_NAMES}
    mom_v = {n: env["v_" + n] for n in WEIGHT_NAMES}
    bl = x.shape[0]
    t = bl * SEQ
    me = 4 * lax.axis_index("x") + 2 * lax.axis_index("y") + lax.axis_index("c")

    big = COL_SHARDED + ROW_SHARDED
    gathered = _allgather_hbm([wt[n].astype(BF16).reshape(-1, wt[n].shape[-1]) for n in big], name="gather_weights")
    full = {n: _full_from_gathered(n, g, wt[n].shape) for n, g in zip(big, gathered)}
    small_shapes = [wt[n].shape for n in SMALL_SHARDED]
    small_n = sum(int(np.prod(s)) for s in small_shapes)
    small = _allgather_vmem(_pack_rows([wt[n] for n in SMALL_SHARDED]), reduce=False, name="gather_small")
    small = small.reshape(N_DEV, -1)[:, :small_n]
    off = 0
    for n, shp in zip(SMALL_SHARDED, small_shapes):
        cnt = int(np.prod(shp))
        g = small[:, off:off + cnt].reshape((N_DEV,) + tuple(shp))
        full[n] = jnp.transpose(g, (1, 2, 0, 3)).reshape(shp[0], shp[1], N_DEV * shp[2])
        off += cnt

    def head_gain(li, a, b, reps):
        g = jnp.tile(qk_gain[li, a, b], reps)
        return jnp.pad(g, (0, 256 - g.shape[0]))

    wts = []
    for li in range(DEPTH):
        rows = [head_gain(li, 0, 0, 4), head_gain(li, 0, 1, 4), head_gain(li, 1, 0, 4), head_gain(li, 1, 1, 2),
                head_gain(li, 2, 0, 4), head_gain(li, 2, 1, 2), jnp.zeros((256,), F32), jnp.zeros((256,), F32)]
        wts.append(dict(
            g_mix=ln_mix_g[li].reshape(1, -1), w_in=full["w_in"][li], qk_gains=jnp.stack(rows),
            sink_t=jnp.broadcast_to(sink[li][:, None, None], (4, 8, 128)),
            c_g=c_norm_g[li].reshape(1, -1), c_b=c_norm_b[li].reshape(1, -1), c_ws=c_ws[li].astype(BF16),
            c_wst=jnp.transpose(c_ws[li], (0, 2, 1)).astype(BF16), c_bst=jnp.repeat(c_bs[li].T, 64, axis=1),
            out_gain=full["out_gain"][li].reshape(1, -1), w_out=full["w_out"][li],
            g_ffn=ln_ffn_g[li].reshape(1, -1), w_up=full["w_up"][li], conv_w=full["conv_w"][li],
            conv_b=conv_b[li].reshape(1, -1), w_down=full["w_down"][li], g_ple=ln_ple_g[li].reshape(1, -1),
            w_gate=full["w_ple_gate"][li], w_proj=full["w_ple_proj"][li]))

    loss_part, dx, grads, d_rel_bias = _local_step(
        x.reshape(t, D_MODEL), p.reshape(DEPTH, t, PLE_DIM), loss_target.reshape(t, D_MODEL), rel_bias, wts)
    loss = lax.psum(loss_part, ("x", "y", "c"))

    def local_grad(n):
        if n == "rel_bias":
            return d_rel_bias
        key = LOCAL_GRAD_KEY.get(n, n)
        return jnp.stack([grads[li][key].reshape(wt[n].shape[1:]) if n in REPLICATED else grads[li][key]
                          for li in range(DEPTH)])

    slots = [_slots_from_full(n, local_grad(n), wt[n].shape) for n in big]
    landed = _all_to_all_hbm(slots, name="exchange_grads")
    out_g, out_d, out_m, out_v = {}, {}, {}, {}
    for n, parts in zip(big, landed):
        shp = wt[n].shape
        two_d = lambda a: a.reshape(-1, shp[-1])
        res = _adamw_reduce(parts, two_d(wt[n]), two_d(mom_m[n]), two_d(mom_v[n]),
                            tr=64 if n == "w_down" else 128, name="adamw_" + n)
        out_g[n], out_d[n], out_m[n], out_v[n] = [r.reshape(shp) for r in res]

    small_names = REPLICATED + SMALL_SHARDED
    small_full_shapes = [wt[n].shape if n in REPLICATED else full[n].shape for n in small_names]
    reduced = _allgather_vmem(_pack_rows([local_grad(n) for n in small_names]), reduce=True, name="allreduce_small")
    reduced = dict(zip(small_names, _unpack_rows(reduced, small_full_shapes)))
    rep_shapes = [wt[n].shape for n in REPLICATED]
    upd = _adamw_plain(_pack_rows([reduced[n] for n in REPLICATED]), _pack_rows([wt[n] for n in REPLICATED]),
                       _pack_rows([mom_m[n] for n in REPLICATED]), _pack_rows([mom_v[n] for n in REPLICATED]),
                       name="adamw_replicated")
    for dst, packed in zip((out_d, out_m, out_v), upd):
        dst.update(zip(REPLICATED, _unpack_rows(packed, rep_shapes)))
    for n in REPLICATED:
        out_g[n] = reduced[n]
    for n in SMALL_SHARDED:
        shp = wt[n].shape
        g = reduced[n].reshape(shp[0], shp[1], N_DEV, shp[2])
        g = lax.dynamic_index_in_dim(g, me, axis=2, keepdims=False)
        two_d = lambda a: a.reshape(-1, shp[-1])
        res = _adamw_plain(two_d(g), two_d(wt[n]), two_d(mom_m[n]), two_d(mom_v[n]), name="adamw_" + n)
        out_g[n] = g
        out_d[n], out_m[n], out_v[n] = [r.reshape(shp) for r in res]

    return (loss, dx.reshape(bl, SEQ, D_MODEL), *[out_g[n] for n in WEIGHT_NAMES], *[out_d[n] for n in WEIGHT_NAMES],
            *[out_m[n] for n in WEIGHT_NAMES], *[out_v[n] for n in WEIGHT_NAMES])
```

```python
import math

import jax
import jax.numpy as jnp
import numpy as np
from jax import lax
from jax.experimental import pallas as pl
from jax.experimental.pallas import tpu as pltpu

F32 = jnp.float32
BF16 = jnp.bfloat16
HI = lax.Precision.HIGHEST

N_DEV = 8
D_MODEL = 1024
SEQ = 2048
DEPTH = 2
HEAD_DIM = 64
IN_WIDTH = 2304
D_FF = 2816
PLE_DIM = 256
C_CHUNK = 128
C_GROUPS = 4
DILATED_CFGS = ((128, 1), (512, 4), (2048, 16))
DILATIONS = tuple(d for _, d in DILATED_CFGS)
A_RADIUS = 64
SWA_RADIUS = 128
BAND_BLOCK = 256
GRID_W = 64
ROPE_THETA = 10000.0
REL_BUCKETS = 32
REL_MAX_DIST = 1024
EPS = 1e-6
NEG_INF = -1e30
ATTN_SCALE = HEAD_DIM ** -0.5
LANES = 128

ADAM_LR = 0.001
ADAM_B1 = 0.9
ADAM_B2 = 0.999
ADAM_EPS = 1e-08
ADAM_WD = 0.01
ADAM_STEP = 10

MESH = pl.DeviceIdType.MESH
NT = (((1,), (1,)), ((), ()))
TN = (((0,), (0,)), ((), ()))
ARB = "arbitrary"
PAR = "parallel"


def _cparams(*sem):
    return pltpu.CompilerParams(dimension_semantics=tuple(sem))


def _sds(shape, dtype):
    return jax.ShapeDtypeStruct(tuple(shape), dtype)


def _group_sum_matrix(n, same_group):
    r = lax.broadcasted_iota(jnp.int32, (n, n), 0)
    c = lax.broadcasted_iota(jnp.int32, (n, n), 1)
    if same_group:
        return ((r >> 6) == (c >> 6)).astype(F32)
    return ((r & 63) == (c & 63)).astype(F32)


def _seg_sum(x, e):
    return jnp.dot(x, e, precision=HI, preferred_element_type=F32)


def _gelu(x):
    c = math.sqrt(2.0 / math.pi)
    return 0.5 * x * (1.0 + jnp.tanh(c * (x + 0.044715 * (x * x * x))))


def _gelu_grad(x):
    c = math.sqrt(2.0 / math.pi)
    t = jnp.tanh(c * (x + 0.044715 * (x * x * x)))
    return 0.5 * (1.0 + t) + 0.5 * x * (1.0 - t * t) * c * (1.0 + 3.0 * 0.044715 * (x * x))


def _sigmoid(x):
    return 1.0 / (1.0 + jnp.exp(-x))


def _scatter_cols(scratch, first, val):
    for c in range(val.shape[1] // LANES):
        scratch[first + c] = val[:, c * LANES:(c + 1) * LANES]


def _gather_cols(scratch, first, ncol):
    return jnp.concatenate([scratch[first + c] for c in range(ncol)], axis=1)


def _read_residue(scratch, first, ncol, r, d):
    n = scratch.shape[1] // d
    return jnp.concatenate([scratch.at[first + c][pl.ds(r, n, stride=d), :] for c in range(ncol)], axis=1)


def _write_residue(scratch, first, r, d, val):
    n = scratch.shape[1] // d
    for c in range(val.shape[1] // LANES):
        scratch.at[first + c][pl.ds(r, n, stride=d), :] = val[:, c * LANES:(c + 1) * LANES]


def _norm_mm(xs, gain, w, res, *, tm, tn, name):
    t = xs[0].shape[0]
    k = sum(x.shape[1] for x in xs)
    n = w.shape[1]
    ng = len(xs)
    has_res = res is not None

    def body(*refs):
        x_refs = refs[:ng]
        g_ref, w_ref = refs[ng], refs[ng + 1]
        res_ref = refs[ng + 2] if has_res else None
        hn_ref, o_ref, hn_s = refs[ng + 2 + has_res:]

        @pl.when(pl.program_id(1) == 0)
        def _():
            off = 0
            for xr in x_refs:
                x = xr[...]
                wd = x.shape[1]
                r = lax.rsqrt(jnp.mean(x * x, axis=-1, keepdims=True) + EPS)
                hn_s[:, off:off + wd] = (x * r * g_ref[:, off:off + wd]).astype(BF16)
                off += wd
            hn_ref[...] = hn_s[...]

        acc = jnp.dot(hn_s[...], w_ref[...], preferred_element_type=F32)
        if has_res:
            acc = acc + res_ref[...]
        o_ref[...] = acc

    in_specs = [pl.BlockSpec((tm, x.shape[1]), lambda i, j: (i, 0)) for x in xs]
    in_specs += [pl.BlockSpec((1, k), lambda i, j: (0, 0)), pl.BlockSpec((k, tn), lambda i, j: (0, j))]
    args = list(xs) + [gain, w]
    if has_res:
        in_specs.append(pl.BlockSpec((tm, tn), lambda i, j: (i, j)))
        args.append(res)
    return pl.pallas_call(
        body, name=name, grid=(t // tm, n // tn), in_specs=in_specs,
        out_specs=[pl.BlockSpec((tm, k), lambda i, j: (i, 0)), pl.BlockSpec((tm, tn), lambda i, j: (i, j))],
        out_shape=[_sds((t, k), BF16), _sds((t, n), F32)],
        scratch_shapes=[pltpu.VMEM((tm, k), BF16)],
        compiler_params=_cparams(PAR, ARB),
    )(*args)


def _mm(a, b, mode, res, *, tm, tn, out_dtype, name, a_rows=None):
    if mode == "tn":
        kk, m = a.shape
        blk_a = 0
        if a_rows is not None:
            blk_a, kk = a_rows
        a_spec = pl.BlockSpec((kk, tm), lambda i, j: (blk_a, i))
    else:
        m, kk = a.shape
        a_spec = pl.BlockSpec((tm, kk), lambda i, j: (i, 0))
    if mode == "nt":
        n = b.shape[0]
        b_spec = pl.BlockSpec((tn, kk), lambda i, j: (j, 0))
    else:
        n = b.shape[1]
        b_spec = pl.BlockSpec((kk, tn), lambda i, j: (0, j))
    has_res = res is not None

    def body(*refs):
        a_ref, b_ref = refs[0], refs[1]
        o_ref = refs[-1]
        av = a_ref[...].astype(BF16)
        bv = b_ref[...].astype(BF16)
        if mode == "nn":
            acc = jnp.dot(av, bv, preferred_element_type=F32)
        elif mode == "nt":
            acc = lax.dot_general(av, bv, NT, preferred_element_type=F32)
        else:
            acc = lax.dot_general(av, bv, TN, preferred_element_type=F32)
        if has_res:
            acc = acc + refs[2][...]
        o_ref[...] = acc.astype(out_dtype)

    in_specs = [a_spec, b_spec]
    args = [a, b]
    if has_res:
        in_specs.append(pl.BlockSpec((tm, tn), lambda i, j: (i, j)))
        args.append(res)
    return pl.pallas_call(
        body, name=name, grid=(m // tm, n // tn), in_specs=in_specs,
        out_specs=pl.BlockSpec((tm, tn), lambda i, j: (i, j)),
        out_shape=_sds((m, n), out_dtype),
        compiler_params=_cparams(PAR, PAR),
    )(*args)


def _mm_bt_normbwd(dy, w, xs, gain, dres, *, tm, tn, name):
    t, n = dy.shape
    k = w.shape[0]
    ng = len(xs)
    nj = n // tn
    has_res = dres is not None

    def body(*refs):
        dy_ref, w_ref = refs[0], refs[1]
        x_refs = refs[2:2 + ng]
        g_ref = refs[2 + ng]
        dres_ref = refs[3 + ng] if has_res else None
        dx_ref, dg_ref, acc = refs[3 + ng + has_res:]
        i, j = pl.program_id(0), pl.program_id(1)

        @pl.when(j == 0)
        def _():
            acc[...] = jnp.zeros_like(acc)

        acc[...] += lax.dot_general(dy_ref[...].astype(BF16), w_ref[...], NT, preferred_element_type=F32)

        @pl.when(j == nj - 1)
        def _():
            @pl.when(i == 0)
            def _():
                dg_ref[...] = jnp.zeros_like(dg_ref)

            off = 0
            for xr in x_refs:
                x = xr[...]
                wd = x.shape[1]
                g = g_ref[:, off:off + wd]
                dyn = acc[:, off:off + wd]
                r = lax.rsqrt(jnp.mean(x * x, axis=-1, keepdims=True) + EPS)
                gdy = dyn * g
                dx = r * gdy - x * (r * r * r * jnp.mean(gdy * x, axis=-1, keepdims=True))
                if has_res:
                    dx = dx + dres_ref[:, off:off + wd]
                dx_ref[:, off:off + wd] = dx
                dg_ref[:, off:off + wd] += jnp.sum(dyn * x * r, axis=0, keepdims=True)
                off += wd

    in_specs = [pl.BlockSpec((tm, tn), lambda i, j: (i, j)), pl.BlockSpec((k, tn), lambda i, j: (0, j))]
    in_specs += [pl.BlockSpec((tm, x.shape[1]), lambda i, j: (i, 0)) for x in xs]
    in_specs.append(pl.BlockSpec((1, k), lambda i, j: (0, 0)))
    args = [dy, w] + list(xs) + [gain]
    if has_res:
        in_specs.append(pl.BlockSpec((tm, k), lambda i, j: (i, 0)))
        args.append(dres)
    return pl.pallas_call(
        body, name=name, grid=(t // tm, nj), in_specs=in_specs,
        out_specs=[pl.BlockSpec((tm, k), lambda i, j: (i, 0)), pl.BlockSpec((1, k), lambda i, j: (0, 0))],
        out_shape=[_sds((t, k), F32), _sds((1, k), F32)],
        scratch_shapes=[pltpu.VMEM((tm, k), F32)],
        compiler_params=_cparams(ARB, ARB),
    )(*args)


def _rope_partner(y):
    n = y.shape[1]
    lane = lax.broadcasted_iota(jnp.int32, y.shape, 1)
    return jnp.where((lane & 31) < 16, pltpu.roll(y, n - 16, 1), pltpu.roll(y, 16, 1))


def _residue_specs(tm, width, nt):
    specs = [pl.BlockSpec((tm, width), lambda b, i: (b * nt + i, 0))]
    for d in DILATIONS[1:]:
        specs.append(pl.BlockSpec((None, d, tm // d, width), lambda b, i: (b, 0, i, 0)))
    return specs


def _residue_shapes(bl, width, dtype):
    return [_sds((bl * SEQ, width), dtype)] + [_sds((bl, d, SEQ // d, width), dtype) for d in DILATIONS[1:]]


def _qkprep_fwd(proj, gains, cos, sins, *, tm, name):
    t = proj.shape[0]
    bl = t // SEQ
    nt = SEQ // tm

    def body(p_ref, g_ref, c_ref, s_ref, qa1_ref, qa4_ref, qa16_ref, qb_ref, qd_ref, scr):
        e = _group_sum_matrix(256, True)

        def hn(x, row):
            wd = x.shape[1]
            ms = _seg_sum(x * x, e[:wd, :wd]) * (1.0 / HEAD_DIM)
            return x * lax.rsqrt(ms + EPS) * g_ref[row:row + 1, :wd]

        qa = jnp.concatenate([hn(p_ref[:, 0:256], 0) * ATTN_SCALE, hn(p_ref[:, 256:512], 1), p_ref[:, 512:768]], axis=1)
        qa1_ref[...] = qa.astype(BF16)
        _scatter_cols(scr, 0, qa)
        for d, ref in ((4, qa4_ref), (16, qa16_ref)):
            for r in range(d):
                ref[r] = _read_residue(scr, 0, 6, r, d).astype(BF16)
        qb_ref[:, 0:256] = (hn(p_ref[:, 768:1024], 2) * ATTN_SCALE).astype(BF16)
        qb_ref[:, 256:384] = hn(p_ref[:, 1024:1152], 3).astype(BF16)
        qb_ref[:, 384:512] = p_ref[:, 1152:1280].astype(BF16)
        yq = hn(p_ref[:, 1792:2048], 4)
        yq = yq * c_ref[...] + _rope_partner(yq) * s_ref[...]
        qd_ref[:, 0:256] = (yq * ATTN_SCALE).astype(BF16)
        yk = hn(p_ref[:, 2048:2176], 5)
        yk = yk * c_ref[:, 0:128] + _rope_partner(yk) * s_ref[:, 0:128]
        qd_ref[:, 256:384] = yk.astype(BF16)
        qd_ref[:, 384:512] = p_ref[:, 2176:2304].astype(BF16)

    row = lambda width: pl.BlockSpec((tm, width), lambda b, i: (b * nt + i, 0))
    tab = pl.BlockSpec((tm, 256), lambda b, i: (i, 0))
    return pl.pallas_call(
        body, name=name, grid=(bl, nt),
        in_specs=[row(IN_WIDTH), pl.BlockSpec((8, 256), lambda b, i: (0, 0)), tab, tab],
        out_specs=_residue_specs(tm, 768, nt) + [row(512), row(512)],
        out_shape=_residue_shapes(bl, 768, BF16) + [_sds((t, 512), BF16), _sds((t, 512), BF16)],
        scratch_shapes=[pltpu.VMEM((6, tm, LANES), F32)],
        compiler_params=_cparams(PAR, PAR),
    )(proj, gains, cos, sins)


def _qkprep_bwd(proj, da, db, dd, dcu, dcv, gains, cos, sins, *, tm, name):
    t = proj.shape[0]
    bl = t // SEQ
    nt = SEQ // tm
    flat = [a for cfg in da for a in cfg] + list(db) + list(dd) + [dcu, dcv]

    def body(*refs):
        p_ref, g_ref, c_ref, s_ref = refs[:4]
        d_refs = refs[4:4 + len(flat)]
        dp_ref, dg_ref, scr = refs[4 + len(flat):]
        a_refs = d_refs[:9]
        dqb_ref, dkb_ref, dvb_ref, dqd_ref, dkd_ref, dvd_ref, dcu_ref, dcv_ref = d_refs[9:]
        e = _group_sum_matrix(256, True)
        first = (pl.program_id(0) == 0) & (pl.program_id(1) == 0)
        last = (pl.program_id(0) == bl - 1) & (pl.program_id(1) == nt - 1)

        @pl.when(first)
        def _():
            dg_ref[...] = jnp.zeros_like(dg_ref)

        def hn_bwd(x, dy, row):
            wd = x.shape[1]
            ee = e[:wd, :wd]
            g = g_ref[row:row + 1, :wd]
            r = lax.rsqrt(_seg_sum(x * x, ee) * (1.0 / HEAD_DIM) + EPS)
            gdy = dy * g
            dx = r * gdy - x * (r * r * r * (_seg_sum(gdy * x, ee) * (1.0 / HEAD_DIM)))
            dg_ref[row:row + 1, :wd] += jnp.sum(dy * x * r, axis=0, keepdims=True)
            return dx

        def rope_bwd(dy, wd):
            return dy * c_ref[:, :wd] + _rope_partner(dy * s_ref[:, :wd])

        dqkv = jnp.concatenate([a_refs[0][...], a_refs[1][...], a_refs[2][...]], axis=1)
        for ci, d in ((1, 4), (2, 16)):
            for r in range(d):
                part = jnp.concatenate([a_refs[3 * ci + m][r] for m in range(3)], axis=1)
                _write_residue(scr, 0, r, d, part)
            dqkv = dqkv + _gather_cols(scr, 0, 6)
        dp_ref[:, 0:256] = hn_bwd(p_ref[:, 0:256], dqkv[:, 0:256] * ATTN_SCALE, 0).astype(BF16)
        dp_ref[:, 256:512] = hn_bwd(p_ref[:, 256:512], dqkv[:, 256:512], 1).astype(BF16)
        dp_ref[:, 512:768] = dqkv[:, 512:768].astype(BF16)
        dp_ref[:, 768:1024] = hn_bwd(p_ref[:, 768:1024], dqb_ref[...] * ATTN_SCALE, 2).astype(BF16)
        dp_ref[:, 1024:1152] = hn_bwd(p_ref[:, 1024:1152], dkb_ref[...], 3).astype(BF16)
        dp_ref[:, 1152:1280] = dvb_ref[...].astype(BF16)
        dp_ref[:, 1280:1536] = dcu_ref[...].astype(BF16)
        dp_ref[:, 1536:1792] = dcv_ref[...].astype(BF16)
        dp_ref[:, 1792:2048] = hn_bwd(p_ref[:, 1792:2048], rope_bwd(dqd_ref[...] * ATTN_SCALE, 256), 4).astype(BF16)
        dp_ref[:, 2048:2176] = hn_bwd(p_ref[:, 2048:2176], rope_bwd(dkd_ref[...], 128), 5).astype(BF16)
        dp_ref[:, 2176:2304] = dvd_ref[...].astype(BF16)

        @pl.when(last)
        def _():
            dg_ref[...] = _seg_sum(dg_ref[...], _group_sum_matrix(256, False))

    row = lambda width: pl.BlockSpec((tm, width), lambda b, i: (b * nt + i, 0))
    tab = pl.BlockSpec((tm, 256), lambda b, i: (i, 0))
    in_specs = [row(IN_WIDTH), pl.BlockSpec((8, 256), lambda b, i: (0, 0)), tab, tab]
    res_specs = _residue_specs(tm, 256, nt)
    in_specs += [res_specs[ci] for ci in range(3) for _ in range(3)]
    in_specs += [row(a.shape[1]) for a in flat[9:]]
    return pl.pallas_call(
        body, name=name, grid=(bl, nt), in_specs=in_specs,
        out_specs=[row(IN_WIDTH), pl.BlockSpec((8, 256), lambda b, i: (0, 0))],
        out_shape=[_sds((t, IN_WIDTH), BF16), _sds((8, 256), F32)],
        scratch_shapes=[pltpu.VMEM((6, tm, LANES), F32)],
        compiler_params=_cparams(ARB, ARB),
    )(proj, gains, cos, sins, *flat)


def _band_spec(seq_len, spec):
    width, idx = spec
    return pl.BlockSpec((None, None, seq_len, width), lambda b, r: (b, r, 0, idx))


def _fill_padded(dst, src_ref, rad, seq_len):
    z = jnp.zeros((rad, dst.shape[1]), dst.dtype)
    dst[0:rad, :] = z
    dst[rad + seq_len:rad + seq_len + rad, :] = z
    dst[rad:rad + seq_len, :] = src_ref[...]


def _band_fwd(src, qs, ks, vs, bias, sink, *, rad, nh, nkv, name):
    bl, dil, sl, _ = src.shape
    blk = bias.shape[1]
    kw = blk + 2 * rad
    nb = sl // blk
    rep = nh // nkv
    has_sink = sink is not None

    def body(*refs):
        q_ref, k_ref, v_ref, b_ref = refs[:4]
        s_ref = refs[4] if has_sink else None
        o_ref, l_ref, kp, vp = refs[4 + has_sink:]
        _fill_padded(kp, k_ref, rad, sl)
        _fill_padded(vp, v_ref, rad, sl)

        def blk_body(i, carry):
            r0 = pl.multiple_of(i * blk, blk)
            qb = q_ref[pl.ds(r0, blk), :]
            kwin = kp[pl.ds(r0, kw), :]
            vwin = vp[pl.ds(r0, kw), :]
            col = r0 - rad + lax.broadcasted_iota(jnp.int32, (blk, kw), 1)
            neg = jnp.where((col >= 0) & (col < sl), 0.0, NEG_INF).astype(F32)
            for h in range(nh):
                g = h // rep
                hs = slice(h * HEAD_DIM, (h + 1) * HEAD_DIM)
                gs = slice(g * HEAD_DIM, (g + 1) * HEAD_DIM)
                s = lax.dot_general(qb[:, hs], kwin[:, gs], NT, preferred_element_type=F32)
                s = s + b_ref[h] + neg
                m = jnp.max(s, axis=1, keepdims=True)
                if has_sink:
                    sk = s_ref[h][0:1, 0:1]
                    m = jnp.maximum(m, sk)
                p = jnp.exp(s - m)
                den = jnp.sum(p, axis=1, keepdims=True)
                if has_sink:
                    den = den + jnp.exp(sk - m)
                o = jnp.dot(p.astype(BF16), vwin[:, gs], preferred_element_type=F32) / den
                o_ref[pl.ds(r0, blk), hs] = o
                l_ref[pl.ds(r0, blk), hs] = jnp.broadcast_to(m + jnp.log(den), (blk, HEAD_DIM))
            return carry

        lax.fori_loop(0, nb, blk_body, 0)

    in_specs = [_band_spec(sl, qs), _band_spec(sl, ks), _band_spec(sl, vs),
                pl.BlockSpec((nh, blk, kw), lambda b, r: (0, 0, 0))]
    args = [src] * 3 + [bias]
    if has_sink:
        in_specs.append(pl.BlockSpec((nh, 8, 128), lambda b, r: (0, 0, 0)))
        args.append(sink)
    return pl.pallas_call(
        body, name=name, grid=(bl, dil), in_specs=in_specs,
        out_specs=[_band_spec(sl, (256, 0))] * 2,
        out_shape=[_sds((bl, dil, sl, 256), F32)] * 2,
        scratch_shapes=[pltpu.VMEM((sl + 2 * rad, ks[0]), BF16), pltpu.VMEM((sl + 2 * rad, vs[0]), BF16)],
        compiler_params=_cparams(PAR, PAR),
    )(*args)


def _band_bwd(src, qs, ks, vs, bias, sink, dy, dcol, lse, delta, *, rad, nh, nkv, name):
    bl, dil, sl, _ = src.shape
    blk = bias.shape[1]
    kw = blk + 2 * rad
    nb = sl // blk
    rep = nh // nkv
    has_sink = sink is not None
    wk, wv = ks[0], vs[0]

    def body(*refs):
        q_ref, k_ref, v_ref, b_ref = refs[:4]
        s_ref = refs[4] if has_sink else None
        do_ref, l_ref, dl_ref = refs[4 + has_sink:7 + has_sink]
        outs = refs[7 + has_sink:]
        if has_sink:
            dq_ref, dk_ref, dv_ref, db_ref, dsk_ref, kp, vp, dka, dva = outs
        else:
            dq_ref, dk_ref, dv_ref, db_ref, kp, vp, dka, dva = outs

        @pl.when((pl.program_id(0) == 0) & (pl.program_id(1) == 0))
        def _():
            db_ref[...] = jnp.zeros_like(db_ref)
            if has_sink:
                dsk_ref[...] = jnp.zeros_like(dsk_ref)

        _fill_padded(kp, k_ref, rad, sl)
        _fill_padded(vp, v_ref, rad, sl)
        dka[...] = jnp.zeros_like(dka)
        dva[...] = jnp.zeros_like(dva)

        def blk_body(i, carry):
            r0 = pl.multiple_of(i * blk, blk)
            qb = q_ref[pl.ds(r0, blk), :]
            kwin = kp[pl.ds(r0, kw), :]
            vwin = vp[pl.ds(r0, kw), :]
            dob = do_ref[pl.ds(r0, blk), :].astype(BF16)
            lb = l_ref[pl.ds(r0, blk), :]
            dlb = dl_ref[pl.ds(r0, blk), :]
            col = r0 - rad + lax.broadcasted_iota(jnp.int32, (blk, kw), 1)
            neg = jnp.where((col >= 0) & (col < sl), 0.0, NEG_INF).astype(F32)
            for h in range(nh):
                g = h // rep
                hs = slice(h * HEAD_DIM, (h + 1) * HEAD_DIM)
                gs = slice(g * HEAD_DIM, (g + 1) * HEAD_DIM)
                qh, kh, vh, doh = qb[:, hs], kwin[:, gs], vwin[:, gs], dob[:, hs]
                lh = lb[:, h * HEAD_DIM:h * HEAD_DIM + 1]
                dlh = dlb[:, h * HEAD_DIM:h * HEAD_DIM + 1]
                s = lax.dot_general(qh, kh, NT, preferred_element_type=F32) + b_ref[h] + neg
                p = jnp.exp(s - lh)
                dp = lax.dot_general(doh, vh, NT, preferred_element_type=F32)
                ds = p * (dp - dlh)
                dsb = ds.astype(BF16)
                dq_ref[pl.ds(r0, blk), hs] = jnp.dot(dsb, kh, preferred_element_type=F32)
                dka[pl.ds(r0, kw), gs] += lax.dot_general(dsb, qh, TN, preferred_element_type=F32)
                dva[pl.ds(r0, kw), gs] += lax.dot_general(p.astype(BF16), doh, TN, preferred_element_type=F32)
                db_ref[h] += ds
                if has_sink:
                    ps = jnp.exp(s_ref[h][0:1, 0:1] - lh)
                    dsk_ref[h] += jnp.broadcast_to(-jnp.sum(ps * dlh, axis=0, keepdims=True), (8, 128))
            return carry

        lax.fori_loop(0, nb, blk_body, 0)
        dk_ref[...] = dka[rad:rad + sl, :]
        dv_ref[...] = dva[rad:rad + sl, :]

    const3 = lambda b, r: (0, 0, 0)
    in_specs = [_band_spec(sl, qs), _band_spec(sl, ks), _band_spec(sl, vs), pl.BlockSpec((nh, blk, kw), const3)]
    args = [src] * 3 + [bias]
    if has_sink:
        in_specs.append(pl.BlockSpec((nh, 8, 128), const3))
        args.append(sink)
    row = _band_spec(sl, (256, 0))
    in_specs += [_band_spec(sl, (256, dcol)), row, row]
    args += [dy, lse, delta]
    out_specs = [row, _band_spec(sl, (wk, 0)), _band_spec(sl, (wv, 0)), pl.BlockSpec((nh, blk, kw), const3)]
    out_shape = [_sds((bl, dil, sl, 256), F32), _sds((bl, dil, sl, wk), F32), _sds((bl, dil, sl, wv), F32),
                 _sds((nh, blk, kw), F32)]
    if has_sink:
        out_specs.append(pl.BlockSpec((nh, 8, 128), const3))
        out_shape.append(_sds((nh, 8, 128), F32))
    return pl.pallas_call(
        body, name=name, grid=(bl, dil), in_specs=in_specs, out_specs=out_specs, out_shape=out_shape,
        scratch_shapes=[pltpu.VMEM((sl + 2 * rad, wk), BF16), pltpu.VMEM((sl + 2 * rad, wv), BF16),
                        pltpu.VMEM((sl + 2 * rad, wk), F32), pltpu.VMEM((sl + 2 * rad, wv), F32)],
        compiler_params=_cparams(ARB, ARB),
    )(*args)


def _combine_a(os_, ls_, *, tm, name):
    bl = os_[1].shape[0]
    t = bl * SEQ
    nt = SEQ // tm

    def body(o1, o4, o16, l1, l4, l16, y_ref, lt_ref, scr):
        for k, (d, ref) in enumerate(((4, o4), (16, o16), (4, l4), (16, l16))):
            for r in range(d):
                _write_residue(scr, 2 * k, r, d, ref[r])
        o2, o3, b, c = (_gather_cols(scr, 2 * k, 2) for k in range(4))
        a = l1[...]
        m = jnp.maximum(jnp.maximum(a, b), c)
        ea, eb, ec = jnp.exp(a - m), jnp.exp(b - m), jnp.exp(c - m)
        den = ea + eb + ec
        y_ref[...] = (ea / den) * o1[...] + (eb / den) * o2 + (ec / den) * o3
        lt_ref[...] = m + jnp.log(den)

    specs = _residue_specs(tm, 256, nt)
    return pl.pallas_call(
        body, name=name, grid=(bl, nt), in_specs=specs * 2, out_specs=[specs[0]] * 2,
        out_shape=[_sds((t, 256), F32)] * 2, scratch_shapes=[pltpu.VMEM((8, tm, LANES), F32)],
        compiler_params=_cparams(PAR, PAR),
    )(*os_, *ls_)


def _deltas(dycat, ya, yb, yd, lse_a, *, tm, name):
    t = ya.shape[0]
    bl = t // SEQ
    nt = SEQ // tm

    def body(dy_ref, ya_ref, yb_ref, yd_ref, la_ref, dy4, dy16, l4, l16, da1, da4, da16, db_ref, dd_ref, scr):
        e = _group_sum_matrix(256, True)
        dya = dy_ref[:, 0:256]
        dla = _seg_sum(dya * ya_ref[...], e)
        da1[...] = dla
        db_ref[...] = _seg_sum(dy_ref[:, 256:512] * yb_ref[...], e)
        dd_ref[...] = _seg_sum(dy_ref[:, 768:1024] * yd_ref[...], e)
        for k, (val, r4, r16) in enumerate(((dya, dy4, dy16), (la_ref[...], l4, l16), (dla, da4, da16))):
            _scatter_cols(scr, 2 * k, val)
            for d, ref in ((4, r4), (16, r16)):
                for r in range(d):
                    ref[r] = _read_residue(scr, 2 * k, 2, r, d)

    specs = _residue_specs(tm, 256, nt)
    nat = specs[0]
    shapes = _residue_shapes(bl, 256, F32)
    outs = pl.pallas_call(
        body, name=name, grid=(bl, nt),
        in_specs=[pl.BlockSpec((tm, 1024), lambda b, i: (b * nt + i, 0)), nat, nat, nat, nat],
        out_specs=specs[1:] + specs[1:] + specs + [nat, nat],
        out_shape=shapes[1:] + shapes[1:] + shapes + [shapes[0], shapes[0]],
        scratch_shapes=[pltpu.VMEM((6, tm, LANES), F32)],
        compiler_params=_cparams(PAR, PAR),
    )(dycat, ya, yb, yd, lse_a)
    return outs[0:2], outs[2:4], outs[4:7], outs[7], outs[8]


def _dense_fwd(qd, *, tq, name):
    t = qd.shape[0]
    bl = t // SEQ
    nq = SEQ // tq

    def body(q_ref, k_ref, v_ref, o_ref, l_ref):
        q = q_ref[...]
        for g in range(2):
            h0, h1 = 2 * g, 2 * g + 1
            q2 = jnp.concatenate([q[:, h0 * 64:(h0 + 1) * 64], q[:, h1 * 64:(h1 + 1) * 64]], axis=0)
            kg = k_ref[:, g * 64:(g + 1) * 64]
            vg = v_ref[:, g * 64:(g + 1) * 64]
            s = lax.dot_general(q2, kg, NT, preferred_element_type=F32)
            m = jnp.max(s, axis=1, keepdims=True)
            p = jnp.exp(s - m)
            den = jnp.sum(p, axis=1, keepdims=True)
            o2 = jnp.dot(p.astype(BF16), vg, preferred_element_type=F32) / den
            l2 = jnp.broadcast_to(m + jnp.log(den), (2 * tq, 64))
            o_ref[:, h0 * 64:(h0 + 1) * 64] = o2[:tq]
            o_ref[:, h1 * 64:(h1 + 1) * 64] = o2[tq:]
            l_ref[:, h0 * 64:(h0 + 1) * 64] = l2[:tq]
            l_ref[:, h1 * 64:(h1 + 1) * 64] = l2[tq:]

    q3 = qd.reshape(bl, SEQ, 512)
    o, lse = pl.pallas_call(
        body, name=name, grid=(bl, nq),
        in_specs=[pl.BlockSpec((None, tq, 256), lambda b, i: (b, i, 0)),
                  pl.BlockSpec((None, SEQ, 128), lambda b, i: (b, 0, 2)),
                  pl.BlockSpec((None, SEQ, 128), lambda b, i: (b, 0, 3))],
        out_specs=[pl.BlockSpec((None, tq, 256), lambda b, i: (b, i, 0))] * 2,
        out_shape=[_sds((bl, SEQ, 256), F32)] * 2,
        compiler_params=_cparams(PAR, PAR),
    )(q3, q3, q3)
    return o.reshape(t, 256), lse.reshape(t, 256)


def _dense_bwd(qd, dycat, lse, delta, *, tq, name):
    t = qd.shape[0]
    bl = t // SEQ
    nq = SEQ // tq

    def body(q_ref, k_ref, v_ref, do_ref, l_ref, dl_ref, dq_ref, dk_ref, dv_ref):
        @pl.when(pl.program_id(1) == 0)
        def _():
            dk_ref[...] = jnp.zeros_like(dk_ref)
            dv_ref[...] = jnp.zeros_like(dv_ref)

        q = q_ref[...]
        do = do_ref[...].astype(BF16)
        lv = l_ref[...]
        dlv = dl_ref[...]
        for g in range(2):
            h0, h1 = 2 * g, 2 * g + 1
            q2 = jnp.concatenate([q[:, h0 * 64:(h0 + 1) * 64], q[:, h1 * 64:(h1 + 1) * 64]], axis=0)
            do2 = jnp.concatenate([do[:, h0 * 64:(h0 + 1) * 64], do[:, h1 * 64:(h1 + 1) * 64]], axis=0)
            l2 = jnp.concatenate([lv[:, h0 * 64:h0 * 64 + 1], lv[:, h1 * 64:h1 * 64 + 1]], axis=0)
            dl2 = jnp.concatenate([dlv[:, h0 * 64:h0 * 64 + 1], dlv[:, h1 * 64:h1 * 64 + 1]], axis=0)
            kg = k_ref[:, g * 64:(g + 1) * 64]
            vg = v_ref[:, g * 64:(g + 1) * 64]
            s = lax.dot_general(q2, kg, NT, preferred_element_type=F32)
            p = jnp.exp(s - l2)
            dp = lax.dot_general(do2, vg, NT, preferred_element_type=F32)
            ds = (p * (dp - dl2)).astype(BF16)
            dq2 = jnp.dot(ds, kg, preferred_element_type=F32)
            dq_ref[:, h0 * 64:(h0 + 1) * 64] = dq2[:tq]
            dq_ref[:, h1 * 64:(h1 + 1) * 64] = dq2[tq:]
            dk_ref[:, g * 64:(g + 1) * 64] += lax.dot_general(ds, q2, TN, preferred_element_type=F32)
            dv_ref[:, g * 64:(g + 1) * 64] += lax.dot_general(p.astype(BF16), do2, TN, preferred_element_type=F32)

    q3 = qd.reshape(bl, SEQ, 512)
    tile = pl.BlockSpec((None, tq, 256), lambda b, i: (b, i, 0))
    full = pl.BlockSpec((None, SEQ, 128), lambda b, i: (b, 0, 0))
    dq, dk, dv = pl.pallas_call(
        body, name=name, grid=(bl, nq),
        in_specs=[tile, pl.BlockSpec((None, SEQ, 128), lambda b, i: (b, 0, 2)),
                  pl.BlockSpec((None, SEQ, 128), lambda b, i: (b, 0, 3)),
                  pl.BlockSpec((None, tq, 256), lambda b, i: (b, i, 3)), tile, tile],
        out_specs=[tile, full, full],
        out_shape=[_sds((bl, SEQ, 256), F32), _sds((bl, SEQ, 128), F32), _sds((bl, SEQ, 128), F32)],
        compiler_params=_cparams(PAR, ARB),
    )(q3, q3, q3, dycat.reshape(bl, SEQ, 1024), lse.reshape(bl, SEQ, 256), delta.reshape(bl, SEQ, 256))
    return dq.reshape(t, 256), dk.reshape(t, 128), dv.reshape(t, 128)


def _c_norm(cv, gam, bet):
    vg = _gelu(cv)
    mu = jnp.mean(vg, axis=-1, keepdims=True)
    xc = vg - mu
    r = lax.rsqrt(jnp.mean(xc * xc, axis=-1, keepdims=True) + EPS)
    xhat = xc * r
    return xhat * gam + bet, xhat, r


def _c_fwd(proj, gam, bet, ws, bst, *, tm, name):
    t = proj.shape[0]
    nch = tm // C_CHUNK

    def body(u_ref, v_ref, g_ref, b_ref, ws_ref, bs_ref, y_ref):
        vn, _, _ = _c_norm(v_ref[...], g_ref[...], b_ref[...])
        vnb = vn.astype(BF16)
        for c in range(nch):
            rows = slice(c * C_CHUNK, (c + 1) * C_CHUNK)
            for g in range(C_GROUPS):
                gs = slice(g * 64, (g + 1) * 64)
                mixed = jnp.dot(ws_ref[g], vnb[rows, gs], preferred_element_type=F32) + bs_ref[:, gs]
                y_ref[rows, gs] = _gelu(u_ref[rows, gs]) * mixed

    vec = pl.BlockSpec((1, 256), lambda i: (0, 0))
    return pl.pallas_call(
        body, name=name, grid=(t // tm,),
        in_specs=[pl.BlockSpec((tm, 256), lambda i: (i, 5)), pl.BlockSpec((tm, 256), lambda i: (i, 6)), vec, vec,
                  pl.BlockSpec((C_GROUPS, C_CHUNK, C_CHUNK), lambda i: (0, 0, 0)),
                  pl.BlockSpec((C_CHUNK, 256), lambda i: (0, 0))],
        out_specs=pl.BlockSpec((tm, 256), lambda i: (i, 0)), out_shape=_sds((t, 256), F32),
        compiler_params=_cparams(PAR),
    )(proj, proj, gam, bet, ws, bst)


def _c_bwd(proj, dycat, gam, bet, ws, wst, bst, *, tm, name):
    t = proj.shape[0]
    nch = tm // C_CHUNK
    nstep = t // tm

    def body(u_ref, v_ref, dy_ref, g_ref, b_ref, ws_ref, wst_ref, bs_ref,
             du_ref, dv_ref, dws_ref, dbs_ref, dg_ref, db_ref, dvn_s):
        step = pl.program_id(0)

        @pl.when(step == 0)
        def _():
            dws_ref[...] = jnp.zeros_like(dws_ref)
            dbs_ref[...] = jnp.zeros_like(dbs_ref)
            dg_ref[...] = jnp.zeros_like(dg_ref)
            db_ref[...] = jnp.zeros_like(db_ref)

        cv = v_ref[...]
        gam_v = g_ref[...]
        vn, xhat, r = _c_norm(cv, gam_v, b_ref[...])
        vnb = vn.astype(BF16)
        for c in range(nch):
            rows = slice(c * C_CHUNK, (c + 1) * C_CHUNK)
            for g in range(C_GROUPS):
                gs = slice(g * 64, (g + 1) * 64)
                cu = u_ref[rows, gs]
                dy = dy_ref[rows, gs]
                mixed = jnp.dot(ws_ref[g], vnb[rows, gs], preferred_element_type=F32) + bs_ref[:, gs]
                du_ref[rows, gs] = dy * mixed * _gelu_grad(cu)
                dmix = dy * _gelu(cu)
                dbs_ref[:, gs] += dmix
                dmb = dmix.astype(BF16)
                dws_ref[g] += lax.dot_general(dmb, vnb[rows, gs], NT, preferred_element_type=F32)
                dvn_s[rows, gs] = jnp.dot(wst_ref[g], dmb, preferred_element_type=F32)
        dvn = dvn_s[...]
        dg_ref[...] += jnp.sum(dvn * xhat, axis=0, keepdims=True)
        db_ref[...] += jnp.sum(dvn, axis=0, keepdims=True)
        dxh = dvn * gam_v
        dvg = r * (dxh - jnp.mean(dxh, axis=-1, keepdims=True) - xhat * jnp.mean(dxh * xhat, axis=-1, keepdims=True))
        dv_ref[...] = dvg * _gelu_grad(cv)

        @pl.when(step == nstep - 1)
        def _():
            dbs_ref[...] = _seg_sum(dbs_ref[...], _group_sum_matrix(256, True))

    vec = pl.BlockSpec((1, 256), lambda i: (0, 0))
    mat = pl.BlockSpec((C_GROUPS, C_CHUNK, C_CHUNK), lambda i: (0, 0, 0))
    bsp = pl.BlockSpec((C_CHUNK, 256), lambda i: (0, 0))
    tile = pl.BlockSpec((tm, 256), lambda i: (i, 0))
    return pl.pallas_call(
        body, name=name, grid=(nstep,),
        in_specs=[pl.BlockSpec((tm, 256), lambda i: (i, 5)), pl.BlockSpec((tm, 256), lambda i: (i, 6)),
                  pl.BlockSpec((tm, 256), lambda i: (i, 2)), vec, vec, mat, mat, bsp],
        out_specs=[tile, tile, mat, bsp, vec, vec],
        out_shape=[_sds((t, 256), F32), _sds((t, 256), F32), _sds((C_GROUPS, C_CHUNK, C_CHUNK), F32),
                   _sds((C_CHUNK, 256), F32), _sds((1, 256), F32), _sds((1, 256), F32)],
        scratch_shapes=[pltpu.VMEM((tm, 256), F32)],
        compiler_params=_cparams(ARB),
    )(proj, proj, dycat, gam, bet, ws, wst, bst)


FF_TC = 128
FF_NB = D_FF // FF_TC
FF_CH = 64
FF_PAD = 8


def _ff_interleave(w):
    lead = w.shape[:-1]
    w = w.reshape(lead + (2, FF_NB, FF_TC))
    return jnp.swapaxes(w, -3, -2).reshape(lead + (2 * D_FF,))


def _ff_deinterleave(w):
    lead = w.shape[:-1]
    w = w.reshape(lead + (FF_NB, 2, FF_TC))
    return jnp.swapaxes(w, -3, -2).reshape(lead + (2 * D_FF,))


def _fill_rows_padded(dst, val):
    z = jnp.zeros((FF_PAD, dst.shape[1]), dst.dtype)
    dst[0:FF_PAD, :] = z
    dst[FF_PAD + SEQ:FF_PAD + SEQ + FF_PAD, :] = z
    if val is not None:
        dst[FF_PAD:FF_PAD + SEQ, :] = val


def _taps(padded, r0):
    ext = padded[pl.ds(r0, FF_CH + 2 * FF_PAD), :]
    n = ext.shape[0]
    body = slice(FF_PAD, FF_PAD + FF_CH)
    return pltpu.roll(ext, 1, 0)[body], ext[body], pltpu.roll(ext, n - 1, 0)[body]


def _conv_gate_fwd(h, cw, cb, *, name):
    t = h.shape[0]
    bl = t // SEQ

    def body(h_ref, w_ref, b_ref, a_ref, hp):
        _fill_rows_padded(hp, h_ref[...])
        w0, w1, w2, b = w_ref[0:1, :], w_ref[1:2, :], w_ref[2:3, :], b_ref[...]

        def chunk(i, carry):
            r0 = pl.multiple_of(i * FF_CH, FF_CH)
            dn, md, up = _taps(hp, r0)
            conv = w0 * dn + w1 * md + w2 * up + b
            cg, cu = conv[:, :FF_TC], conv[:, FF_TC:]
            a_ref[pl.ds(r0, FF_CH), :] = (cg * _sigmoid(cg) * cu).astype(BF16)
            return carry

        lax.fori_loop(0, SEQ // FF_CH, chunk, 0)

    act = pl.pallas_call(
        body, name=name, grid=(bl, FF_NB),
        in_specs=[pl.BlockSpec((None, SEQ, 2 * FF_TC), lambda b, j: (b, 0, j)),
                  pl.BlockSpec((3, 2 * FF_TC), lambda b, j: (0, j)), pl.BlockSpec((1, 2 * FF_TC), lambda b, j: (0, j))],
        out_specs=pl.BlockSpec((None, SEQ, FF_TC), lambda b, j: (b, 0, j)),
        out_shape=_sds((bl, SEQ, D_FF), BF16),
        scratch_shapes=[pltpu.VMEM((SEQ + 2 * FF_PAD, 2 * FF_TC), F32)],
        compiler_params=_cparams(PAR, PAR),
    )(h.reshape(bl, SEQ, 2 * D_FF), cw, cb)
    return act.reshape(t, D_FF)


def _conv_gate_bwd(h, dact, cw, cb, *, name):
    t = h.shape[0]
    bl = t // SEQ

    def body(h_ref, da_ref, w_ref, b_ref, dh_ref, dw_ref, db_ref, hp, dp):
        @pl.when(pl.program_id(1) == 0)
        def _():
            dw_ref[...] = jnp.zeros_like(dw_ref)
            db_ref[...] = jnp.zeros_like(db_ref)

        _fill_rows_padded(hp, h_ref[...])
        _fill_rows_padded(dp, None)
        w0, w1, w2, b = w_ref[0:1, :], w_ref[1:2, :], w_ref[2:3, :], b_ref[...]

        def pass1(i, carry):
            s0, s1, s2, sb = carry
            r0 = pl.multiple_of(i * FF_CH, FF_CH)
            dn, md, up = _taps(hp, r0)
            conv = w0 * dn + w1 * md + w2 * up + b
            cg, cu = conv[:, :FF_TC], conv[:, FF_TC:]
            da = da_ref[pl.ds(r0, FF_CH), :]
            sg = _sigmoid(cg)
            dhc = jnp.concatenate([da * cu * (sg * (1.0 + cg * (1.0 - sg))), da * (cg * sg)], axis=1)
            dp[pl.ds(r0 + FF_PAD, FF_CH), :] = dhc
            red = lambda x: jnp.sum(x, axis=0, keepdims=True)
            return s0 + red(dhc * dn), s1 + red(dhc * md), s2 + red(dhc * up), sb + red(dhc)

        z = jnp.zeros((1, 2 * FF_TC), F32)
        s0, s1, s2, sb = lax.fori_loop(0, SEQ // FF_CH, pass1, (z, z, z, z))
        dw_ref[0:1, :] += s0
        dw_ref[1:2, :] += s1
        dw_ref[2:3, :] += s2
        db_ref[...] += sb

        def pass2(i, carry):
            r0 = pl.multiple_of(i * FF_CH, FF_CH)
            dn, md, up = _taps(dp, r0)
            dh_ref[pl.ds(r0, FF_CH), :] = (w0 * up + w1 * md + w2 * dn).astype(BF16)
            return carry

        lax.fori_loop(0, SEQ // FF_CH, pass2, 0)

    col = lambda j, b: (0, j)
    dh, dcw, dcb = pl.pallas_call(
        body, name=name, grid=(FF_NB, bl),
        in_specs=[pl.BlockSpec((None, SEQ, 2 * FF_TC), lambda j, b: (b, 0, j)),
                  pl.BlockSpec((None, SEQ, FF_TC), lambda j, b: (b, 0, j)),
                  pl.BlockSpec((3, 2 * FF_TC), col), pl.BlockSpec((1, 2 * FF_TC), col)],
        out_specs=[pl.BlockSpec((None, SEQ, 2 * FF_TC), lambda j, b: (b, 0, j)),
                   pl.BlockSpec((3, 2 * FF_TC), col), pl.BlockSpec((1, 2 * FF_TC), col)],
        out_shape=[_sds((bl, SEQ, 2 * D_FF), BF16), _sds((3, 2 * D_FF), F32), _sds((1, 2 * D_FF), F32)],
        scratch_shapes=[pltpu.VMEM((SEQ + 2 * FF_PAD, 2 * FF_TC), F32), pltpu.VMEM((SEQ + 2 * FF_PAD, 2 * FF_TC), F32)],
        compiler_params=_cparams(PAR, ARB),
    )(h.reshape(bl, SEQ, 2 * D_FF), dact.reshape(bl, SEQ, D_FF), cw, cb)
    return dh.reshape(t, 2 * D_FF), dcw, dcb


def _ple_fwd(x2, gain, wg, pe, pe_blk, wp, *, tm, tn, name):
    t, k = x2.shape
    n = wg.shape[1]

    def body(x_ref, g_ref, wg_ref, pe_ref, wp_ref, xr_ref, hn_ref, x3_ref, gt_ref, pp_ref, hn_s):
        @pl.when(pl.program_id(1) == 0)
        def _():
            x = x_ref[...]
            r = lax.rsqrt(jnp.mean(x * x, axis=-1, keepdims=True) + EPS)
            hn_s[...] = (x * r * g_ref[...]).astype(BF16)
            hn_ref[...] = hn_s[...]

        gate = _sigmoid(jnp.dot(hn_s[...], wg_ref[...], preferred_element_type=F32))
        pp = jnp.dot(pe_ref[...].astype(BF16), wp_ref[...], preferred_element_type=F32)
        gt_ref[...] = gate
        pp_ref[...] = pp
        x3_ref[...] = xr_ref[...] + pp * gate

    tile = pl.BlockSpec((tm, tn), lambda i, j: (i, j))
    return pl.pallas_call(
        body, name=name, grid=(t // tm, n // tn),
        in_specs=[pl.BlockSpec((tm, k), lambda i, j: (i, 0)), pl.BlockSpec((1, k), lambda i, j: (0, 0)),
                  pl.BlockSpec((k, tn), lambda i, j: (0, j)), pl.BlockSpec((tm, PLE_DIM), lambda i, j: (pe_blk + i, 0)),
                  pl.BlockSpec((PLE_DIM, tn), lambda i, j: (0, j)), tile],
        out_specs=[pl.BlockSpec((tm, k), lambda i, j: (i, 0)), tile, tile, tile],
        out_shape=[_sds((t, k), BF16), _sds((t, n), F32), _sds((t, n), F32), _sds((t, n), F32)],
        scratch_shapes=[pltpu.VMEM((tm, k), BF16)],
        compiler_params=_cparams(PAR, ARB),
    )(x2, gain, wg, pe, wp, x2)


def _ple_bwd_ew(dx3, gate, pp, *, tm, name):
    t, n = dx3.shape

    def body(d_ref, g_ref, p_ref, dz_ref, dpp_ref):
        d, g = d_ref[...], g_ref[...]
        dz_ref[...] = (d * p_ref[...] * g * (1.0 - g)).astype(BF16)
        dpp_ref[...] = (d * g).astype(BF16)

    spec = pl.BlockSpec((tm, n), lambda i: (i, 0))
    return pl.pallas_call(
        body, name=name, grid=(t // tm,), in_specs=[spec] * 3, out_specs=[spec] * 2,
        out_shape=[_sds((t, n), BF16)] * 2, compiler_params=_cparams(PAR),
    )(dx3, gate, pp)


def _loss_head(y, tgt, *, tm, name):
    t, d = y.shape

    def body(y_ref, t_ref, l_ref, dy_ref):
        @pl.when(pl.program_id(0) == 0)
        def _():
            l_ref[...] = jnp.zeros_like(l_ref)

        e = y_ref[...] - t_ref[...]
        dy_ref[...] = e * (1.0 / d)
        s = jnp.sum(jnp.sum(e * e, axis=1, keepdims=True), axis=0, keepdims=True)
        l_ref[...] += jnp.broadcast_to(s * (0.5 / d), (8, 128))

    spec = pl.BlockSpec((tm, d), lambda i: (i, 0))
    return pl.pallas_call(
        body, name=name, grid=(t // tm,), in_specs=[spec, spec],
        out_specs=[pl.BlockSpec((8, 128), lambda i: (0, 0)), spec],
        out_shape=[_sds((8, 128), F32), _sds((t, d), F32)], compiler_params=_cparams(ARB),
    )(y, tgt)


BIAS_PC = 8192


def _onehot(bucket_row):
    rows = lax.broadcasted_iota(jnp.int32, (REL_BUCKETS, bucket_row.shape[1]), 0)
    return (rows == bucket_row).astype(F32)


def _bias_lookup(table_t, bucket, *, name):
    h = table_t.shape[0]
    p = bucket.shape[1]

    def body(t_ref, b_ref, o_ref):
        bk = b_ref[...]
        val = jnp.dot(t_ref[...], _onehot(bk), precision=HI, preferred_element_type=F32)
        o_ref[...] = jnp.where(bk >= 0, val, NEG_INF)

    return pl.pallas_call(
        body, name=name, grid=(p // BIAS_PC,),
        in_specs=[pl.BlockSpec((h, REL_BUCKETS), lambda i: (0, 0)), pl.BlockSpec((1, BIAS_PC), lambda i: (0, i))],
        out_specs=pl.BlockSpec((h, BIAS_PC), lambda i: (0, i)), out_shape=_sds((h, p), F32),
        compiler_params=_cparams(PAR),
    )(table_t, bucket)


def _bucket_reduce(dbias, bucket, *, name):
    h, p = dbias.shape

    def body(d_ref, b_ref, o_ref):
        @pl.when(pl.program_id(0) == 0)
        def _():
            o_ref[...] = jnp.zeros_like(o_ref)

        o_ref[...] += lax.dot_general(d_ref[...], _onehot(b_ref[...]), NT, precision=HI, preferred_element_type=F32)

    return pl.pallas_call(
        body, name=name, grid=(p // BIAS_PC,),
        in_specs=[pl.BlockSpec((h, BIAS_PC), lambda i: (0, i)), pl.BlockSpec((1, BIAS_PC), lambda i: (0, i))],
        out_specs=pl.BlockSpec((h, REL_BUCKETS), lambda i: (0, 0)), out_shape=_sds((h, REL_BUCKETS), F32),
        compiler_params=_cparams(ARB),
    )(dbias, bucket)


def _adamw_math(w, g, m, v):
    m = ADAM_B1 * m + (1.0 - ADAM_B1) * g
    v = ADAM_B2 * v + (1.0 - ADAM_B2) * (g * g)
    m_hat = m / (1.0 - ADAM_B1 ** ADAM_STEP)
    v_hat = v / (1.0 - ADAM_B2 ** ADAM_STEP)
    delta = -ADAM_LR * (m_hat / (jnp.sqrt(v_hat) + ADAM_EPS) + ADAM_WD * w)
    return delta, m, v


def _adamw_reduce(parts, w, m, v, *, tr, name):
    r, c = w.shape

    def body(p_ref, w_ref, m_ref, v_ref, g_ref, d_ref, nm_ref, nv_ref):
        g = p_ref[0].astype(F32)
        for k in range(1, N_DEV):
            g = g + p_ref[k].astype(F32)
        d, nm, nv = _adamw_math(w_ref[...], g, m_ref[...], v_ref[...])
        g_ref[...] = g
        d_ref[...] = d
        nm_ref[...] = nm
        nv_ref[...] = nv

    spec = pl.BlockSpec((tr, c), lambda i: (i, 0))
    return pl.pallas_call(
        body, name=name, grid=(r // tr,),
        in_specs=[pl.BlockSpec((N_DEV, tr, c), lambda i: (0, i, 0)), spec, spec, spec],
        out_specs=[spec] * 4, out_shape=[_sds((r, c), F32)] * 4, compiler_params=_cparams(PAR),
    )(parts, w, m, v)


def _adamw_plain(g, w, m, v, *, name):
    def body(g_ref, w_ref, m_ref, v_ref, d_ref, nm_ref, nv_ref):
        d, nm, nv = _adamw_math(w_ref[...], g_ref[...], m_ref[...], v_ref[...])
        d_ref[...] = d
        nm_ref[...] = nm
        nv_ref[...] = nv

    return pl.pallas_call(body, name=name, out_shape=[_sds(w.shape, F32)] * 3)(g, w, m, v)


def _mesh_pos():
    return lax.axis_index("x"), lax.axis_index("y"), lax.axis_index("c")


def _allgather_body(x_refs, out_refs, send_sems, recv_sems, local_sems, slot):
    x, y, c = _mesh_pos()
    me, sibling = (x, y, c), (x, y, 1 - c)
    chips = [(1 - x, y), (x, 1 - y), (1 - x, 1 - y)]
    waits = []
    for a, (x_ref, out_ref) in enumerate(zip(x_refs, out_refs)):
        def copy(k, block, to, src=None, out_ref=out_ref, a=a):
            return pltpu.make_async_remote_copy(
                src_ref=slot(out_ref, block) if src is None else src, dst_ref=slot(out_ref, block),
                send_sem=send_sems.at[a, k], recv_sem=recv_sems.at[a, k], device_id=to, device_id_type=MESH)

        mine = pltpu.make_async_copy(x_ref, slot(out_ref, me), local_sems.at[a])
        mine.start()
        first = [copy(0, me, sibling, src=x_ref)]
        first += [copy(1 + j, me, (*chip, c), src=x_ref) for j, chip in enumerate(chips)]
        for cp in first:
            cp.start()
        waits.append((copy, mine, first))
    sends = []
    for copy, mine, first in waits:
        passed = [copy(4 + j, (*chip, c), sibling) for j, chip in enumerate(chips)]
        for j, chip in enumerate(chips):
            copy(1 + j, (*chip, c), me).wait_recv()
            passed[j].start()
        sends.append(passed)
    for (copy, mine, first), passed in zip(waits, sends):
        copy(0, sibling, me).wait_recv()
        for j, chip in enumerate(chips):
            copy(4 + j, (*chip, 1 - c), me).wait_recv()
        for cp in first + passed:
            cp.wait_send()
        mine.wait()


def _allgather_hbm(xs, *, name):
    na = len(xs)

    def body(*refs):
        x_refs, out_refs = refs[:na], refs[na:2 * na]
        send_sems, recv_sems, local_sems = refs[2 * na:]
        _allgather_body(x_refs, out_refs, send_sems, recv_sems, local_sems,
                        lambda ref, pos: ref.at[4 * pos[0] + 2 * pos[1] + pos[2]])

    hbm = pl.BlockSpec(memory_space=pltpu.HBM)
    return pl.pallas_call(
        body, name=name, in_specs=[hbm] * na, out_specs=[hbm] * na,
        out_shape=[_sds((N_DEV,) + x.shape, x.dtype) for x in xs],
        scratch_shapes=[pltpu.SemaphoreType.DMA((na, 7)), pltpu.SemaphoreType.DMA((na, 7)),
                        pltpu.SemaphoreType.DMA((na,))],
    )(*xs)


def _allgather_vmem(x, *, reduce, name):
    r, c = x.shape

    def body(x_ref, out_ref, *rest):
        if reduce:
            gath, send_sems, recv_sems, local_sems = rest
        else:
            send_sems, recv_sems, local_sems = rest
            gath = out_ref
        _allgather_body([x_ref], [gath], send_sems, recv_sems, local_sems,
                        lambda ref, pos: ref.at[pl.ds((4 * pos[0] + 2 * pos[1] + pos[2]) * r, r), :])
        if reduce:
            acc = gath[0:r, :]
            for k in range(1, N_DEV):
                acc = acc + gath[k * r:(k + 1) * r, :]
            out_ref[...] = acc

    vm = pl.BlockSpec(memory_space=pltpu.VMEM)
    scratch = [pltpu.SemaphoreType.DMA((1, 7)), pltpu.SemaphoreType.DMA((1, 7)), pltpu.SemaphoreType.DMA((1,))]
    if reduce:
        scratch = [pltpu.VMEM((N_DEV * r, c), x.dtype)] + scratch
    return pl.pallas_call(
        body, name=name, in_specs=[vm], out_specs=vm,
        out_shape=_sds((r, c) if reduce else (N_DEV * r, c), x.dtype), scratch_shapes=scratch,
    )(x)


def _all_to_all_hbm(xs, *, name):
    na = len(xs)

    def body(*refs):
        x_refs, out_refs = refs[:na], refs[na:2 * na]
        send_sems, recv_sems, local_sems = refs[2 * na:]
        x, y, c = _mesh_pos()
        me = 4 * x + 2 * y + c
        rel = [(0, 0, 1), (1, 0, 0), (0, 1, 0), (1, 1, 0), (1, 0, 1), (0, 1, 1), (1, 1, 1)]
        copies = []
        for a, (x_ref, out_ref) in enumerate(zip(x_refs, out_refs)):
            own = pltpu.make_async_copy(x_ref.at[me], out_ref.at[me], local_sems.at[a])
            own.start()
            copies.append(own)
            for k, (fx, fy, fc) in enumerate(rel):
                px, py, pc = x ^ fx, y ^ fy, c ^ fc
                peer = 4 * px + 2 * py + pc
                cp = pltpu.make_async_remote_copy(
                    src_ref=x_ref.at[peer], dst_ref=out_ref.at[me], send_sem=send_sems.at[a, k],
                    recv_sem=recv_sems.at[a, k], device_id=(px, py, pc), device_id_type=MESH)
                cp.start()
                copies.append(cp)
        for cp in copies:
            cp.wait()

    hbm = pl.BlockSpec(memory_space=pltpu.HBM)
    return pl.pallas_call(
        body, name=name, in_specs=[hbm] * na, out_specs=[hbm] * na,
        out_shape=[_sds(x.shape, x.dtype) for x in xs],
        scratch_shapes=[pltpu.SemaphoreType.DMA((na, 7)), pltpu.SemaphoreType.DMA((na, 7)),
                        pltpu.SemaphoreType.DMA((na,))],
    )(*xs)


def _t5_bucket(rel):
    nb = REL_BUCKETS // 2
    ret = jnp.where(rel > 0, nb, 0)
    n = jnp.abs(rel)
    max_exact = nb // 2
    nf = jnp.maximum(n, 1).astype(F32)
    large = max_exact + (jnp.log(nf / max_exact) / math.log(REL_MAX_DIST / max_exact)
                         * (nb - max_exact)).astype(jnp.int32)
    large = jnp.minimum(large, nb - 1)
    return ret + jnp.where(n < max_exact, n, large)


def _band_pattern(block, radius, dil):
    kw = block + 2 * radius
    rel = jnp.arange(kw)[None, :] - radius - jnp.arange(block)[:, None]
    return jnp.where(jnp.abs(rel) <= radius, _t5_bucket(rel * dil), -1).astype(jnp.int32).reshape(1, block * kw)


def _rope_tables():
    lane = np.arange(64)
    seg, j = lane // 32, lane % 32
    inv = ROPE_THETA ** (-jnp.arange(0, 32, 2, dtype=F32) / 32)
    tpos = jnp.arange(SEQ)
    pos = jnp.where(jnp.asarray(seg)[None, :] == 0, (tpos // GRID_W)[:, None], (tpos % GRID_W)[:, None])
    ang = pos.astype(F32) * inv[jnp.asarray(j % 16)][None, :]
    cos = jnp.cos(ang)
    sins = jnp.where(jnp.asarray(j)[None, :] < 16, -jnp.sin(ang), jnp.sin(ang))
    return jnp.tile(cos, (1, 4)), jnp.tile(sins, (1, 4))


A_Q, A_K, A_V = (256, 0), (256, 1), (256, 2)
B_Q, B_K, B_V = (256, 0), (128, 2), (128, 3)
A_HEADS = dict(rad=A_RADIUS, nh=4, nkv=4)
B_HEADS = dict(rad=SWA_RADIUS, nh=4, nkv=2)


def _local_step(x, pe, tgt, rel_bias, wts):
    t = x.shape[0]
    bl = t // SEQ
    cos, sins = _rope_tables()
    blocks_a = [min(BAND_BLOCK, SEQ // d) for d in DILATIONS]
    pats_a = [_band_pattern(blk, A_RADIUS, d) for blk, d in zip(blocks_a, DILATIONS)]
    pat_b = _band_pattern(BAND_BLOCK, SWA_RADIUS, 1)
    table_t = rel_bias.T
    bias_a = [_bias_lookup(table_t[:4], pt, name=f"bias_a{ci}").reshape(4, blk, blk + 2 * A_RADIUS)
              for ci, (pt, blk) in enumerate(zip(pats_a, blocks_a))]
    bias_b = _bias_lookup(table_t[4:], pat_b, name="bias_b").reshape(4, BAND_BLOCK, BAND_BLOCK + 2 * SWA_RADIUS)
    nat4 = lambda a: a.reshape(bl, 1, SEQ, a.shape[-1])

    saved = []
    for li in range(DEPTH):
        w = wts[li]
        hn0, proj = _norm_mm((x,), w["g_mix"], w["w_in"], None, tm=512, tn=768, name="mix_in_fwd")
        qa1, qa4, qa16, qb, qd = _qkprep_fwd(proj, w["qk_gains"], cos, sins, tm=512, name="qkprep_fwd")
        qa = (nat4(qa1), qa4, qa16)
        oa, la = [], []
        for ci in range(3):
            o, l = _band_fwd(qa[ci], A_Q, A_K, A_V, bias_a[ci], None, name=f"band_a{ci}_fwd", **A_HEADS)
            oa.append(o)
            la.append(l)
        oa[0], la[0] = oa[0].reshape(t, 256), la[0].reshape(t, 256)
        ya, lse_a = _combine_a(oa, la, tm=512, name="combine_a")
        yb, lse_b = _band_fwd(nat4(qb), B_Q, B_K, B_V, bias_b, w["sink_t"], name="band_b_fwd", **B_HEADS)
        yb = yb.reshape(t, 256)
        yc = _c_fwd(proj, w["c_g"], w["c_b"], w["c_ws"], w["c_bst"], tm=512, name="c_fwd")
        yd, lse_d = _dense_fwd(qd, tq=128, name="dense_fwd")
        mixed, x1 = _norm_mm((ya, yb, yc, yd), w["out_gain"], w["w_out"], x, tm=512, tn=512, name="mix_out_fwd")
        hn1, h = _norm_mm((x1,), w["g_ffn"], w["w_up"], None, tm=512, tn=512, name="ffn_up_fwd")
        act = _conv_gate_fwd(h, w["conv_w"], w["conv_b"], name="conv_gate_fwd")
        x2 = _mm(act, w["w_down"], "nn", x1, tm=512, tn=512, out_dtype=F32, name="ffn_down_fwd")
        hn2, x3, gate, pp = _ple_fwd(x2, w["g_ple"], w["w_gate"], pe, li * (t // 512), w["w_proj"], tm=512, tn=512,
                                     name="ple_fwd")
        saved.append(dict(x0=x, hn0=hn0, proj=proj, qa=qa, qb=qb, qd=qd, ya=ya, lse_a=lse_a, yb=yb, lse_b=lse_b,
                          yc=yc, yd=yd, lse_d=lse_d, mixed=mixed, x1=x1, hn1=hn1, h=h, act=act, x2=x2, hn2=hn2,
                          gate=gate, pp=pp))
        x = x3

    loss_tile, dx = _loss_head(x, tgt, tm=512, name="loss_head")
    grads = [None] * DEPTH
    d_table_a = jnp.zeros((4, REL_BUCKETS), F32)
    d_table_b = jnp.zeros((4, REL_BUCKETS), F32)
    for li in reversed(range(DEPTH)):
        w, s = wts[li], saved[li]
        g = {}
        dz, dpp = _ple_bwd_ew(dx, s["gate"], s["pp"], tm=512, name="ple_bwd_ew")
        g["w_gate"] = _mm(s["hn2"], dz, "tn", None, tm=512, tn=512, out_dtype=BF16, name="dw_gate")
        g["w_proj"] = _mm(pe, dpp, "tn", None, tm=256, tn=512, out_dtype=BF16, name="dw_proj", a_rows=(li, t))
        dx2, g["g_ple"] = _mm_bt_normbwd(dz, w["w_gate"], (s["x2"],), w["g_ple"], dx, tm=512, tn=512, name="ple_bwd")
        g["w_down"] = _mm(s["act"], dx2, "tn", None, tm=256, tn=512, out_dtype=BF16, name="dw_down")
        dact = _mm(dx2, w["w_down"], "nt", None, tm=512, tn=256, out_dtype=F32, name="ffn_down_bwd")
        dh, g["conv_w"], g["conv_b"] = _conv_gate_bwd(s["h"], dact, w["conv_w"], w["conv_b"], name="conv_gate_bwd")
        g["w_up"] = _mm(s["hn1"], dh, "tn", None, tm=512, tn=512, out_dtype=BF16, name="dw_up")
        dx1, g["g_ffn"] = _mm_bt_normbwd(dh, w["w_up"], (s["x1"],), w["g_ffn"], dx2, tm=512, tn=512, name="ffn_up_bwd")
        g["w_out"] = _mm(s["mixed"], dx1, "tn", None, tm=512, tn=512, out_dtype=BF16, name="dw_out")
        dycat, g["out_gain"] = _mm_bt_normbwd(dx1, w["w_out"], (s["ya"], s["yb"], s["yc"], s["yd"]), w["out_gain"],
                                              None, tm=512, tn=512, name="mix_out_bwd")
        dy_r, lse_r, dl_a, dl_b, dl_d = _deltas(dycat, s["ya"], s["yb"], s["yd"], s["lse_a"], tm=512, name="deltas")
        dy_a = (nat4(dycat),) + tuple(dy_r)
        lse_a = (nat4(s["lse_a"]),) + tuple(lse_r)
        dl_a = (nat4(dl_a[0]),) + tuple(dl_a[1:])
        da = []
        for ci in range(3):
            dq, dk, dv, dbias = _band_bwd(s["qa"][ci], A_Q, A_K, A_V, bias_a[ci], None, dy_a[ci], 0, lse_a[ci],
                                          dl_a[ci], name=f"band_a{ci}_bwd", **A_HEADS)
            if ci == 0:
                dq, dk, dv = (a.reshape(t, 256) for a in (dq, dk, dv))
            da.append((dq, dk, dv))
            d_table_a = d_table_a + _bucket_reduce(dbias.reshape(4, -1), pats_a[ci], name=f"bucket_a{ci}")
        dqb, dkb, dvb, dbias_b, dsink = _band_bwd(nat4(s["qb"]), B_Q, B_K, B_V, bias_b, w["sink_t"], nat4(dycat), 1,
                                                  nat4(s["lse_b"]), nat4(dl_b), name="band_b_bwd", **B_HEADS)
        d_table_b = d_table_b + _bucket_reduce(dbias_b.reshape(4, -1), pat_b, name="bucket_b")
        g["sink"] = dsink[:, 0, 0]
        dd = _dense_bwd(s["qd"], dycat, s["lse_d"], dl_d, tq=128, name="dense_bwd")
        dcu, dcv, g["c_ws"], dbs, g["c_g"], g["c_b"] = _c_bwd(s["proj"], dycat, w["c_g"], w["c_b"], w["c_ws"],
                                                               w["c_wst"], w["c_bst"], tm=512, name="c_bwd")
        g["c_bs"] = dbs[:, ::64].T
        db = (dqb.reshape(t, 256), dkb.reshape(t, 128), dvb.reshape(t, 128))
        dproj, dgains = _qkprep_bwd(s["proj"], da, db, dd, dcu, dcv, w["qk_gains"], cos, sins, tm=512, name="qkprep_bwd")
        g["qk_gain"] = dgains[:6, :64].reshape(3, 2, HEAD_DIM)
        g["w_in"] = _mm(s["hn0"], dproj, "tn", None, tm=512, tn=768, out_dtype=BF16, name="dw_in")
        dx, g["g_mix"] = _mm_bt_normbwd(dproj, w["w_in"], (s["x0"],), w["g_mix"], dx1, tm=512, tn=768, name="mix_in_bwd")
        grads[li] = g
    d_rel_bias = jnp.concatenate([d_table_a, d_table_b], axis=0).T
    return loss_tile[0, 0], dx, grads, d_rel_bias


WEIGHT_NAMES = ("rel_bias", "ln_mix_g", "w_in", "qk_gain", "sink", "c_norm_g", "c_norm_b", "c_ws", "c_bs", "out_gain",
                "w_out", "ln_ffn_g", "w_up", "conv_w", "conv_b", "w_down", "ln_ple_g", "w_ple_gate", "w_ple_proj")
COL_SHARDED = ("w_in", "w_up", "w_ple_proj")
ROW_SHARDED = ("w_out", "w_down", "w_ple_gate")
SMALL_SHARDED = ("conv_w", "out_gain")
REPLICATED = tuple(n for n in WEIGHT_NAMES if n not in COL_SHARDED + ROW_SHARDED + SMALL_SHARDED)
LOCAL_GRAD_KEY = {"ln_mix_g": "g_mix", "ln_ffn_g": "g_ffn", "ln_ple_g": "g_ple", "c_norm_g": "c_g", "c_norm_b": "c_b",
                  "w_ple_gate": "w_gate", "w_ple_proj": "w_proj"}


def _full_from_gathered(name, gathered):
    _, r, c = gathered.shape
    if name in ROW_SHARDED:
        return gathered.reshape(N_DEV * r, c)
    return jnp.transpose(gathered, (1, 0, 2)).reshape(r, N_DEV * c)


def _slots_from_full(name, full, shard_shape):
    d, r, c = shard_shape
    if name in ROW_SHARDED:
        g = jnp.transpose(full.reshape(d, N_DEV, r, c), (1, 0, 2, 3))
    else:
        g = jnp.transpose(full.reshape(d, r, N_DEV, c), (2, 0, 1, 3))
    return g.reshape(N_DEV, d * r, c)


def _pack_rows(arrays):
    flat = jnp.concatenate([a.reshape(-1).astype(F32) for a in arrays])
    rows = -(-flat.shape[0] // 1024) * 8
    return jnp.pad(flat, (0, rows * 128 - flat.shape[0])).reshape(rows, 128)


def _unpack_rows(packed, shapes):
    flat = packed.reshape(-1)
    out, off = [], 0
    for shp in shapes:
        n = int(np.prod(shp))
        out.append(flat[off:off + n].reshape(shp))
        off += n
    return out


def kernel(x, p, rel_bias, ln_mix_g, w_in, qk_gain, sink, c_norm_g, c_norm_b, c_ws, c_bs, out_gain, w_out, ln_ffn_g, w_up, conv_w, conv_b, w_down, ln_ple_g, w_ple_gate, w_ple_proj, loss_target, m_rel_bias, m_ln_mix_g, m_w_in, m_qk_gain, m_sink, m_c_norm_g, m_c_norm_b, m_c_ws, m_c_bs, m_out_gain, m_w_out, m_ln_ffn_g, m_w_up, m_conv_w, m_conv_b, m_w_down, m_ln_ple_g, m_w_ple_gate, m_w_ple_proj, v_rel_bias, v_ln_mix_g, v_w_in, v_qk_gain, v_sink, v_c_norm_g, v_c_norm_b, v_c_ws, v_c_bs, v_out_gain, v_w_out, v_ln_ffn_g, v_w_up, v_conv_w, v_conv_b, v_w_down, v_ln_ple_g, v_w_ple_gate, v_w_ple_proj):
    env = dict(locals())
    wt = {n: env[n] for n in WEIGHT_NAMES}
    mom_m = {n: env["m_" + n] for n in WEIGHT_NAMES}
    mom_v = {n: env["v_" + n] for n in WEIGHT_NAMES}
    bl = x.shape[0]
    t = bl * SEQ
    me = 4 * lax.axis_index("x") + 2 * lax.axis_index("y") + lax.axis_index("c")

    big = COL_SHARDED + ROW_SHARDED
    big_layers = [(n, li) for n in big for li in range(DEPTH)]
    gathered = _allgather_hbm([wt[n][li].astype(BF16) for n, li in big_layers], name="gather_weights")
    full = {nl: _full_from_gathered(nl[0], g) for nl, g in zip(big_layers, gathered)}
    small_shapes = [wt[n].shape for n in SMALL_SHARDED]
    small_n = sum(int(np.prod(s)) for s in small_shapes)
    small = _allgather_vmem(_pack_rows([wt[n] for n in SMALL_SHARDED]), reduce=False, name="gather_small")
    small = small.reshape(N_DEV, -1)[:, :small_n]
    off = 0
    for n, shp in zip(SMALL_SHARDED, small_shapes):
        cnt = int(np.prod(shp))
        g = small[:, off:off + cnt].reshape((N_DEV,) + tuple(shp))
        full[n] = jnp.transpose(g, (1, 2, 0, 3)).reshape(shp[0], shp[1], N_DEV * shp[2])
        off += cnt

    def head_gain(li, a, b, reps):
        g = jnp.tile(qk_gain[li, a, b], reps)
        return jnp.pad(g, (0, 256 - g.shape[0]))

    wts = []
    for li in range(DEPTH):
        rows = [head_gain(li, 0, 0, 4), head_gain(li, 0, 1, 4), head_gain(li, 1, 0, 4), head_gain(li, 1, 1, 2),
                head_gain(li, 2, 0, 4), head_gain(li, 2, 1, 2), jnp.zeros((256,), F32), jnp.zeros((256,), F32)]
        wts.append(dict(
            g_mix=ln_mix_g[li].reshape(1, -1), w_in=full["w_in", li], qk_gains=jnp.stack(rows),
            sink_t=jnp.broadcast_to(sink[li][:, None, None], (4, 8, 128)),
            c_g=c_norm_g[li].reshape(1, -1), c_b=c_norm_b[li].reshape(1, -1), c_ws=c_ws[li].astype(BF16),
            c_wst=jnp.transpose(c_ws[li], (0, 2, 1)).astype(BF16), c_bst=jnp.repeat(c_bs[li].T, 64, axis=1),
            out_gain=full["out_gain"][li].reshape(1, -1), w_out=full["w_out", li],
            g_ffn=ln_ffn_g[li].reshape(1, -1), w_up=_ff_interleave(full["w_up", li]),
            conv_w=_ff_interleave(full["conv_w"][li]), conv_b=_ff_interleave(conv_b[li].reshape(1, -1)),
            w_down=full["w_down", li], g_ple=ln_ple_g[li].reshape(1, -1),
            w_gate=full["w_ple_gate", li], w_proj=full["w_ple_proj", li]))

    loss_part, dx, grads, d_rel_bias = _local_step(
        x.reshape(t, D_MODEL), p.reshape(DEPTH * t, PLE_DIM), loss_target.reshape(t, D_MODEL), rel_bias, wts)
    loss = lax.psum(loss_part, ("x", "y", "c"))
    for g in grads:
        for key in ("w_up", "conv_w", "conv_b"):
            g[key] = _ff_deinterleave(g[key])

    def local_grad(n):
        if n == "rel_bias":
            return d_rel_bias
        key = LOCAL_GRAD_KEY.get(n, n)
        return jnp.stack([grads[li][key].reshape(wt[n].shape[1:]) if n in REPLICATED else grads[li][key]
                          for li in range(DEPTH)])

    slots = [_slots_from_full(n, local_grad(n), wt[n].shape) for n in big]
    landed = _all_to_all_hbm(slots, name="exchange_grads")
    out_g, out_d, out_m, out_v = {}, {}, {}, {}
    for n, parts in zip(big, landed):
        shp = wt[n].shape
        two_d = lambda a: a.reshape(-1, shp[-1])
        res = _adamw_reduce(parts, two_d(wt[n]), two_d(mom_m[n]), two_d(mom_v[n]),
                            tr=64 if n == "w_down" else 128, name="adamw_" + n)
        out_g[n], out_d[n], out_m[n], out_v[n] = [r.reshape(shp) for r in res]

    small_names = REPLICATED + SMALL_SHARDED
    small_full_shapes = [wt[n].shape if n in REPLICATED else full[n].shape for n in small_names]
    reduced = _allgather_vmem(_pack_rows([local_grad(n) for n in small_names]), reduce=True, name="allreduce_small")
    reduced = dict(zip(small_names, _unpack_rows(reduced, small_full_shapes)))
    rep_shapes = [wt[n].shape for n in REPLICATED]
    upd = _adamw_plain(_pack_rows([reduced[n] for n in REPLICATED]), _pack_rows([wt[n] for n in REPLICATED]),
                       _pack_rows([mom_m[n] for n in REPLICATED]), _pack_rows([mom_v[n] for n in REPLICATED]),
                       name="adamw_replicated")
    for dst, packed in zip((out_d, out_m, out_v), upd):
        dst.update(zip(REPLICATED, _unpack_rows(packed, rep_shapes)))
    for n in REPLICATED:
        out_g[n] = reduced[n]
    for n in SMALL_SHARDED:
        shp = wt[n].shape
        g = reduced[n].reshape(shp[0], shp[1], N_DEV, shp[2])
        g = lax.dynamic_index_in_dim(g, me, axis=2, keepdims=False)
        two_d = lambda a: a.reshape(-1, shp[-1])
        res = _adamw_plain(two_d(g), two_d(wt[n]), two_d(mom_m[n]), two_d(mom_v[n]), name="adamw_" + n)
        out_g[n] = g
        out_d[n], out_m[n], out_v[n] = [r.reshape(shp) for r in res]

    return (loss, dx.reshape(bl, SEQ, D_MODEL), *[out_g[n] for n in WEIGHT_NAMES], *[out_d[n] for n in WEIGHT_NAMES],
            *[out_m[n] for n in WEIGHT_NAMES], *[out_v[n] for n in WEIGHT_NAMES])
```

```python
import math

import jax
import jax.numpy as jnp
import numpy as np
from jax import lax
from jax.experimental import pallas as pl
from jax.experimental.pallas import tpu as pltpu

F32 = jnp.float32
BF16 = jnp.bfloat16
HI = lax.Precision.HIGHEST

N_DEV = 8
D_MODEL = 1024
SEQ = 2048
DEPTH = 2
HEAD_DIM = 64
IN_WIDTH = 2304
D_FF = 2816
PLE_DIM = 256
C_CHUNK = 128
C_GROUPS = 4
DILATED_CFGS = ((128, 1), (512, 4), (2048, 16))
DILATIONS = tuple(d for _, d in DILATED_CFGS)
A_RADIUS = 64
SWA_RADIUS = 128
BAND_BLOCK = 256
GRID_W = 64
ROPE_THETA = 10000.0
REL_BUCKETS = 32
REL_MAX_DIST = 1024
EPS = 1e-6
NEG_INF = -1e30
ATTN_SCALE = HEAD_DIM ** -0.5
LANES = 128

ADAM_LR = 0.001
ADAM_B1 = 0.9
ADAM_B2 = 0.999
ADAM_EPS = 1e-08
ADAM_WD = 0.01
ADAM_STEP = 10

MESH = pl.DeviceIdType.MESH
NT = (((1,), (1,)), ((), ()))
TN = (((0,), (0,)), ((), ()))
ARB = "arbitrary"
PAR = "parallel"


def _cparams(*sem):
    return pltpu.CompilerParams(dimension_semantics=tuple(sem))


def _sds(shape, dtype):
    return jax.ShapeDtypeStruct(tuple(shape), dtype)


def _group_sum_matrix(n, same_group):
    r = lax.broadcasted_iota(jnp.int32, (n, n), 0)
    c = lax.broadcasted_iota(jnp.int32, (n, n), 1)
    if same_group:
        return ((r >> 6) == (c >> 6)).astype(F32)
    return ((r & 63) == (c & 63)).astype(F32)


def _seg_sum(x, e):
    return jnp.dot(x, e, precision=HI, preferred_element_type=F32)


def _gelu(x):
    c = math.sqrt(2.0 / math.pi)
    return 0.5 * x * (1.0 + jnp.tanh(c * (x + 0.044715 * (x * x * x))))


def _gelu_grad(x):
    c = math.sqrt(2.0 / math.pi)
    t = jnp.tanh(c * (x + 0.044715 * (x * x * x)))
    return 0.5 * (1.0 + t) + 0.5 * x * (1.0 - t * t) * c * (1.0 + 3.0 * 0.044715 * (x * x))


def _sigmoid(x):
    return 1.0 / (1.0 + jnp.exp(-x))


def _scatter_cols(scratch, first, val):
    for c in range(val.shape[1] // LANES):
        scratch[first + c] = val[:, c * LANES:(c + 1) * LANES]


def _gather_cols(scratch, first, ncol):
    return jnp.concatenate([scratch[first + c] for c in range(ncol)], axis=1)


def _read_residue(scratch, first, ncol, r, d):
    n = scratch.shape[1] // d
    return jnp.concatenate([scratch.at[first + c][pl.ds(r, n, stride=d), :] for c in range(ncol)], axis=1)


def _write_residue(scratch, first, r, d, val):
    n = scratch.shape[1] // d
    for c in range(val.shape[1] // LANES):
        scratch.at[first + c][pl.ds(r, n, stride=d), :] = val[:, c * LANES:(c + 1) * LANES]


def _norm_mm(xs, gain, w, res, *, tm, tn, name, out_dtype=F32):
    t = xs[0].shape[0]
    k = sum(x.shape[1] for x in xs)
    n = w.shape[1]
    ng = len(xs)
    has_res = res is not None

    def body(*refs):
        x_refs = refs[:ng]
        g_ref, w_ref = refs[ng], refs[ng + 1]
        res_ref = refs[ng + 2] if has_res else None
        hn_ref, o_ref, hn_s = refs[ng + 2 + has_res:]

        @pl.when(pl.program_id(1) == 0)
        def _():
            off = 0
            for xr in x_refs:
                x = xr[...]
                wd = x.shape[1]
                r = lax.rsqrt(jnp.mean(x * x, axis=-1, keepdims=True) + EPS)
                hn_s[:, off:off + wd] = (x * r * g_ref[:, off:off + wd]).astype(BF16)
                off += wd
            hn_ref[...] = hn_s[...]

        acc = jnp.dot(hn_s[...], w_ref[...], preferred_element_type=F32)
        if has_res:
            acc = acc + res_ref[...]
        o_ref[...] = acc.astype(out_dtype)

    in_specs = [pl.BlockSpec((tm, x.shape[1]), lambda i, j: (i, 0)) for x in xs]
    in_specs += [pl.BlockSpec((1, k), lambda i, j: (0, 0)), pl.BlockSpec((k, tn), lambda i, j: (0, j))]
    args = list(xs) + [gain, w]
    if has_res:
        in_specs.append(pl.BlockSpec((tm, tn), lambda i, j: (i, j)))
        args.append(res)
    return pl.pallas_call(
        body, name=name, grid=(t // tm, n // tn), in_specs=in_specs,
        out_specs=[pl.BlockSpec((tm, k), lambda i, j: (i, 0)), pl.BlockSpec((tm, tn), lambda i, j: (i, j))],
        out_shape=[_sds((t, k), BF16), _sds((t, n), out_dtype)],
        scratch_shapes=[pltpu.VMEM((tm, k), BF16)],
        compiler_params=_cparams(PAR, ARB),
    )(*args)


def _mm(a, b, mode, res, *, tm, tn, out_dtype, name, a_rows=None):
    if mode == "tn":
        kk, m = a.shape
        blk_a = 0
        if a_rows is not None:
            blk_a, kk = a_rows
        a_spec = pl.BlockSpec((kk, tm), lambda i, j: (blk_a, i))
    else:
        m, kk = a.shape
        a_spec = pl.BlockSpec((tm, kk), lambda i, j: (i, 0))
    if mode == "nt":
        n = b.shape[0]
        b_spec = pl.BlockSpec((tn, kk), lambda i, j: (j, 0))
    else:
        n = b.shape[1]
        b_spec = pl.BlockSpec((kk, tn), lambda i, j: (0, j))
    has_res = res is not None

    def body(*refs):
        a_ref, b_ref = refs[0], refs[1]
        o_ref = refs[-1]
        av = a_ref[...].astype(BF16)
        bv = b_ref[...].astype(BF16)
        if mode == "nn":
            acc = jnp.dot(av, bv, preferred_element_type=F32)
        elif mode == "nt":
            acc = lax.dot_general(av, bv, NT, preferred_element_type=F32)
        else:
            acc = lax.dot_general(av, bv, TN, preferred_element_type=F32)
        if has_res:
            acc = acc + refs[2][...]
        o_ref[...] = acc.astype(out_dtype)

    in_specs = [a_spec, b_spec]
    args = [a, b]
    if has_res:
        in_specs.append(pl.BlockSpec((tm, tn), lambda i, j: (i, j)))
        args.append(res)
    return pl.pallas_call(
        body, name=name, grid=(m // tm, n // tn), in_specs=in_specs,
        out_specs=pl.BlockSpec((tm, tn), lambda i, j: (i, j)),
        out_shape=_sds((m, n), out_dtype),
        compiler_params=_cparams(PAR, PAR),
    )(*args)


def _mm_bt_normbwd(dys, w, xs, gain, dres, *, tm, tn, name, emit_bf16=False):
    t, wd_each = dys[0].shape
    nd = len(dys)
    per = wd_each // tn
    nj = nd * per
    k = w.shape[0]
    ng = len(xs)
    has_res = dres is not None

    def body(*refs):
        dy_refs = refs[:nd]
        w_ref = refs[nd]
        x_refs = refs[nd + 1:nd + 1 + ng]
        g_ref = refs[nd + 1 + ng]
        dres_ref = refs[nd + 2 + ng] if has_res else None
        outs = refs[nd + 2 + ng + has_res:]
        dx_ref = outs[0]
        dxb_ref = outs[1] if emit_bf16 else None
        dg_ref, acc = outs[1 + emit_bf16:]
        i, j = pl.program_id(0), pl.program_id(1)

        @pl.when(j == 0)
        def _():
            acc[...] = jnp.zeros_like(acc)

        for d, dy_ref in enumerate(dy_refs):
            @pl.when((j >= d * per) & (j < (d + 1) * per))
            def _(dy_ref=dy_ref):
                acc[...] += lax.dot_general(dy_ref[...].astype(BF16), w_ref[...], NT, preferred_element_type=F32)

        @pl.when(j == nj - 1)
        def _():
            @pl.when(i == 0)
            def _():
                dg_ref[...] = jnp.zeros_like(dg_ref)

            off = 0
            for xr in x_refs:
                x = xr[...]
                wd = x.shape[1]
                g = g_ref[:, off:off + wd]
                dyn = acc[:, off:off + wd]
                r = lax.rsqrt(jnp.mean(x * x, axis=-1, keepdims=True) + EPS)
                gdy = dyn * g
                dx = r * gdy - x * (r * r * r * jnp.mean(gdy * x, axis=-1, keepdims=True))
                if has_res:
                    dx = dx + dres_ref[:, off:off + wd]
                dx_ref[:, off:off + wd] = dx
                if emit_bf16:
                    dxb_ref[:, off:off + wd] = dx.astype(BF16)
                dg_ref[:, off:off + wd] += jnp.sum(dyn * x * r, axis=0, keepdims=True)
                off += wd

    def dy_map(d):
        return lambda i, j: (i, jnp.clip(j - d * per, 0, per - 1))

    in_specs = [pl.BlockSpec((tm, tn), dy_map(d)) for d in range(nd)]
    in_specs.append(pl.BlockSpec((k, tn), lambda i, j: (0, j)))
    in_specs += [pl.BlockSpec((tm, x.shape[1]), lambda i, j: (i, 0)) for x in xs]
    in_specs.append(pl.BlockSpec((1, k), lambda i, j: (0, 0)))
    args = list(dys) + [w] + list(xs) + [gain]
    if has_res:
        in_specs.append(pl.BlockSpec((tm, k), lambda i, j: (i, 0)))
        args.append(dres)
    row = pl.BlockSpec((tm, k), lambda i, j: (i, 0))
    out_specs = [row] + ([row] if emit_bf16 else []) + [pl.BlockSpec((1, k), lambda i, j: (0, 0))]
    out_shape = [_sds((t, k), F32)] + ([_sds((t, k), BF16)] if emit_bf16 else []) + [_sds((1, k), F32)]
    return pl.pallas_call(
        body, name=name, grid=(t // tm, nj), in_specs=in_specs, out_specs=out_specs, out_shape=out_shape,
        scratch_shapes=[pltpu.VMEM((tm, k), F32)],
        compiler_params=_cparams(ARB, ARB),
    )(*args)


def _rope_partner(y):
    n = y.shape[1]
    lane = lax.broadcasted_iota(jnp.int32, y.shape, 1)
    return jnp.where((lane & 31) < 16, pltpu.roll(y, n - 16, 1), pltpu.roll(y, 16, 1))


def _residue_specs(tm, width, nt):
    specs = [pl.BlockSpec((tm, width), lambda b, i: (b * nt + i, 0))]
    for d in DILATIONS[1:]:
        specs.append(pl.BlockSpec((None, d, tm // d, width), lambda b, i: (b, 0, i, 0)))
    return specs


def _residue_shapes(bl, width, dtype):
    return [_sds((bl * SEQ, width), dtype)] + [_sds((bl, d, SEQ // d, width), dtype) for d in DILATIONS[1:]]


def _qkprep_fwd(proj, gains, cos, sins, *, tm, name):
    t = proj.shape[0]
    bl = t // SEQ
    nt = SEQ // tm

    def body(p_ref, g_ref, c_ref, s_ref, qa1_ref, qa4_ref, qa16_ref, qb_ref, qd_ref, scr):
        e = _group_sum_matrix(256, True)

        def hn(x, row):
            wd = x.shape[1]
            ms = _seg_sum(x * x, e[:wd, :wd]) * (1.0 / HEAD_DIM)
            return x * lax.rsqrt(ms + EPS) * g_ref[row:row + 1, :wd]

        qa = jnp.concatenate([hn(p_ref[:, 0:256], 0) * ATTN_SCALE, hn(p_ref[:, 256:512], 1), p_ref[:, 512:768]], axis=1)
        qa1_ref[...] = qa.astype(BF16)
        _scatter_cols(scr, 0, qa)
        for d, ref in ((4, qa4_ref), (16, qa16_ref)):
            for r in range(d):
                ref[r] = _read_residue(scr, 0, 6, r, d).astype(BF16)
        qb_ref[:, 0:256] = (hn(p_ref[:, 768:1024], 2) * ATTN_SCALE).astype(BF16)
        qb_ref[:, 256:384] = hn(p_ref[:, 1024:1152], 3).astype(BF16)
        qb_ref[:, 384:512] = p_ref[:, 1152:1280].astype(BF16)
        yq = hn(p_ref[:, 1792:2048], 4)
        yq = yq * c_ref[...] + _rope_partner(yq) * s_ref[...]
        qd_ref[:, 0:256] = (yq * ATTN_SCALE).astype(BF16)
        yk = hn(p_ref[:, 2048:2176], 5)
        yk = yk * c_ref[:, 0:128] + _rope_partner(yk) * s_ref[:, 0:128]
        qd_ref[:, 256:384] = yk.astype(BF16)
        qd_ref[:, 384:512] = p_ref[:, 2176:2304].astype(BF16)

    row = lambda width: pl.BlockSpec((tm, width), lambda b, i: (b * nt + i, 0))
    tab = pl.BlockSpec((tm, 256), lambda b, i: (i, 0))
    return pl.pallas_call(
        body, name=name, grid=(bl, nt),
        in_specs=[row(IN_WIDTH), pl.BlockSpec((8, 256), lambda b, i: (0, 0)), tab, tab],
        out_specs=_residue_specs(tm, 768, nt) + [row(512), row(512)],
        out_shape=_residue_shapes(bl, 768, BF16) + [_sds((t, 512), BF16), _sds((t, 512), BF16)],
        scratch_shapes=[pltpu.VMEM((6, tm, LANES), F32)],
        compiler_params=_cparams(PAR, PAR),
    )(proj, gains, cos, sins)


def _qkprep_bwd(proj, da, db, dd, dcu, dcv, gains, cos, sins, *, tm, name):
    t = proj.shape[0]
    bl = t // SEQ
    nt = SEQ // tm
    flat = [a for cfg in da for a in cfg] + list(db) + list(dd) + [dcu, dcv]

    def body(*refs):
        p_ref, g_ref, c_ref, s_ref = refs[:4]
        d_refs = refs[4:4 + len(flat)]
        dp_ref, dg_ref, scr = refs[4 + len(flat):]
        a_refs = d_refs[:9]
        dqb_ref, dkb_ref, dvb_ref, dqd_ref, dkd_ref, dvd_ref, dcu_ref, dcv_ref = d_refs[9:]
        e = _group_sum_matrix(256, True)
        first = (pl.program_id(0) == 0) & (pl.program_id(1) == 0)
        last = (pl.program_id(0) == bl - 1) & (pl.program_id(1) == nt - 1)

        @pl.when(first)
        def _():
            dg_ref[...] = jnp.zeros_like(dg_ref)

        def hn_bwd(x, dy, row):
            wd = x.shape[1]
            ee = e[:wd, :wd]
            g = g_ref[row:row + 1, :wd]
            r = lax.rsqrt(_seg_sum(x * x, ee) * (1.0 / HEAD_DIM) + EPS)
            gdy = dy * g
            dx = r * gdy - x * (r * r * r * (_seg_sum(gdy * x, ee) * (1.0 / HEAD_DIM)))
            dg_ref[row:row + 1, :wd] += jnp.sum(dy * x * r, axis=0, keepdims=True)
            return dx

        def rope_bwd(dy, wd):
            return dy * c_ref[:, :wd] + _rope_partner(dy * s_ref[:, :wd])

        dqkv = jnp.concatenate([a_refs[0][...], a_refs[1][...], a_refs[2][...]], axis=1)
        for ci, d in ((1, 4), (2, 16)):
            for r in range(d):
                part = jnp.concatenate([a_refs[3 * ci + m][r] for m in range(3)], axis=1)
                _write_residue(scr, 0, r, d, part)
            dqkv = dqkv + _gather_cols(scr, 0, 6)
        dp_ref[:, 0:256] = hn_bwd(p_ref[:, 0:256], dqkv[:, 0:256] * ATTN_SCALE, 0).astype(BF16)
        dp_ref[:, 256:512] = hn_bwd(p_ref[:, 256:512], dqkv[:, 256:512], 1).astype(BF16)
        dp_ref[:, 512:768] = dqkv[:, 512:768].astype(BF16)
        dp_ref[:, 768:1024] = hn_bwd(p_ref[:, 768:1024], dqb_ref[...] * ATTN_SCALE, 2).astype(BF16)
        dp_ref[:, 1024:1152] = hn_bwd(p_ref[:, 1024:1152], dkb_ref[...], 3).astype(BF16)
        dp_ref[:, 1152:1280] = dvb_ref[...].astype(BF16)
        dp_ref[:, 1280:1536] = dcu_ref[...].astype(BF16)
        dp_ref[:, 1536:1792] = dcv_ref[...].astype(BF16)
        dp_ref[:, 1792:2048] = hn_bwd(p_ref[:, 1792:2048], rope_bwd(dqd_ref[...] * ATTN_SCALE, 256), 4).astype(BF16)
        dp_ref[:, 2048:2176] = hn_bwd(p_ref[:, 2048:2176], rope_bwd(dkd_ref[...], 128), 5).astype(BF16)
        dp_ref[:, 2176:2304] = dvd_ref[...].astype(BF16)

        @pl.when(last)
        def _():
            dg_ref[...] = _seg_sum(dg_ref[...], _group_sum_matrix(256, False))

    row = lambda width: pl.BlockSpec((tm, width), lambda b, i: (b * nt + i, 0))
    tab = pl.BlockSpec((tm, 256), lambda b, i: (i, 0))
    in_specs = [row(IN_WIDTH), pl.BlockSpec((8, 256), lambda b, i: (0, 0)), tab, tab]
    res_specs = _residue_specs(tm, 256, nt)
    in_specs += [res_specs[ci] for ci in range(3) for _ in range(3)]
    in_specs += [row(a.shape[1]) for a in flat[9:]]
    return pl.pallas_call(
        body, name=name, grid=(bl, nt), in_specs=in_specs,
        out_specs=[row(IN_WIDTH), pl.BlockSpec((8, 256), lambda b, i: (0, 0))],
        out_shape=[_sds((t, IN_WIDTH), BF16), _sds((8, 256), F32)],
        scratch_shapes=[pltpu.VMEM((6, tm, LANES), F32)],
        compiler_params=_cparams(ARB, ARB),
    )(proj, gains, cos, sins, *flat)


def _band_spec(seq_len, spec):
    width, idx = spec
    return pl.BlockSpec((None, None, seq_len, width), lambda b, r: (b, r, 0, idx))


def _fill_padded(dst, src_ref, rad, seq_len):
    z = jnp.zeros((rad, dst.shape[1]), dst.dtype)
    dst[0:rad, :] = z
    dst[rad + seq_len:rad + seq_len + rad, :] = z
    dst[rad:rad + seq_len, :] = src_ref[...]


def _band_fwd(src, qs, ks, vs, bias, sink, *, rad, nh, nkv, name):
    bl, dil, sl, _ = src.shape
    blk = bias.shape[1]
    kw = blk + 2 * rad
    nb = sl // blk
    rep = nh // nkv
    has_sink = sink is not None

    def body(*refs):
        q_ref, k_ref, v_ref, b_ref = refs[:4]
        s_ref = refs[4] if has_sink else None
        o_ref, l_ref, kp, vp = refs[4 + has_sink:]
        _fill_padded(kp, k_ref, rad, sl)
        _fill_padded(vp, v_ref, rad, sl)

        def blk_body(i, carry):
            r0 = pl.multiple_of(i * blk, blk)
            qb = q_ref[pl.ds(r0, blk), :]
            kwin = kp[pl.ds(r0, kw), :]
            vwin = vp[pl.ds(r0, kw), :]
            col = r0 - rad + lax.broadcasted_iota(jnp.int32, (blk, kw), 1)
            neg = jnp.where((col >= 0) & (col < sl), 0.0, NEG_INF).astype(F32)
            for h in range(nh):
                g = h // rep
                hs = slice(h * HEAD_DIM, (h + 1) * HEAD_DIM)
                gs = slice(g * HEAD_DIM, (g + 1) * HEAD_DIM)
                s = lax.dot_general(qb[:, hs], kwin[:, gs], NT, preferred_element_type=F32)
                s = s + b_ref[h] + neg
                m = jnp.max(s, axis=1, keepdims=True)
                if has_sink:
                    sk = s_ref[h][0:1, 0:1]
                    m = jnp.maximum(m, sk)
                p = jnp.exp(s - m)
                den = jnp.sum(p, axis=1, keepdims=True)
                if has_sink:
                    den = den + jnp.exp(sk - m)
                o = jnp.dot(p.astype(BF16), vwin[:, gs], preferred_element_type=F32) / den
                o_ref[pl.ds(r0, blk), hs] = o
                l_ref[pl.ds(r0, blk), hs] = jnp.broadcast_to(m + jnp.log(den), (blk, HEAD_DIM))
            return carry

        lax.fori_loop(0, nb, blk_body, 0)

    in_specs = [_band_spec(sl, qs), _band_spec(sl, ks), _band_spec(sl, vs),
                pl.BlockSpec((nh, blk, kw), lambda b, r: (0, 0, 0))]
    args = [src] * 3 + [bias]
    if has_sink:
        in_specs.append(pl.BlockSpec((nh, 8, 128), lambda b, r: (0, 0, 0)))
        args.append(sink)
    return pl.pallas_call(
        body, name=name, grid=(bl, dil), in_specs=in_specs,
        out_specs=[_band_spec(sl, (256, 0))] * 2,
        out_shape=[_sds((bl, dil, sl, 256), F32)] * 2,
        scratch_shapes=[pltpu.VMEM((sl + 2 * rad, ks[0]), BF16), pltpu.VMEM((sl + 2 * rad, vs[0]), BF16)],
        compiler_params=_cparams(PAR, PAR),
    )(*args)


def _band_bwd(src, qs, ks, vs, bias, sink, dy, dcol, lse, delta, *, rad, nh, nkv, name):
    bl, dil, sl, _ = src.shape
    blk = bias.shape[1]
    kw = blk + 2 * rad
    nb = sl // blk
    rep = nh // nkv
    has_sink = sink is not None
    wk, wv = ks[0], vs[0]

    def body(*refs):
        q_ref, k_ref, v_ref, b_ref = refs[:4]
        s_ref = refs[4] if has_sink else None
        do_ref, l_ref, dl_ref = refs[4 + has_sink:7 + has_sink]
        outs = refs[7 + has_sink:]
        if has_sink:
            dq_ref, dk_ref, dv_ref, db_ref, dsk_ref, kp, vp, dka, dva = outs
        else:
            dq_ref, dk_ref, dv_ref, db_ref, kp, vp, dka, dva = outs

        @pl.when((pl.program_id(0) == 0) & (pl.program_id(1) == 0))
        def _():
            db_ref[...] = jnp.zeros_like(db_ref)
            if has_sink:
                dsk_ref[...] = jnp.zeros_like(dsk_ref)

        _fill_padded(kp, k_ref, rad, sl)
        _fill_padded(vp, v_ref, rad, sl)
        dka[...] = jnp.zeros_like(dka)
        dva[...] = jnp.zeros_like(dva)

        def blk_body(i, carry):
            r0 = pl.multiple_of(i * blk, blk)
            qb = q_ref[pl.ds(r0, blk), :]
            kwin = kp[pl.ds(r0, kw), :]
            vwin = vp[pl.ds(r0, kw), :]
            dob = do_ref[pl.ds(r0, blk), :].astype(BF16)
            lb = l_ref[pl.ds(r0, blk), :]
            dlb = dl_ref[pl.ds(r0, blk), :]
            col = r0 - rad + lax.broadcasted_iota(jnp.int32, (blk, kw), 1)
            neg = jnp.where((col >= 0) & (col < sl), 0.0, NEG_INF).astype(F32)
            for h in range(nh):
                g = h // rep
                hs = slice(h * HEAD_DIM, (h + 1) * HEAD_DIM)
                gs = slice(g * HEAD_DIM, (g + 1) * HEAD_DIM)
                qh, kh, vh, doh = qb[:, hs], kwin[:, gs], vwin[:, gs], dob[:, hs]
                lh = lb[:, h * HEAD_DIM:h * HEAD_DIM + 1]
                dlh = dlb[:, h * HEAD_DIM:h * HEAD_DIM + 1]
                s = lax.dot_general(qh, kh, NT, preferred_element_type=F32) + b_ref[h] + neg
                p = jnp.exp(s - lh)
                dp = lax.dot_general(doh, vh, NT, preferred_element_type=F32)
                ds = p * (dp - dlh)
                dsb = ds.astype(BF16)
                dq_ref[pl.ds(r0, blk), hs] = jnp.dot(dsb, kh, preferred_element_type=F32)
                dka[pl.ds(r0, kw), gs] += lax.dot_general(dsb, qh, TN, preferred_element_type=F32)
                dva[pl.ds(r0, kw), gs] += lax.dot_general(p.astype(BF16), doh, TN, preferred_element_type=F32)
                db_ref[h] += ds
                if has_sink:
                    ps = jnp.exp(s_ref[h][0:1, 0:1] - lh)
                    dsk_ref[h] += jnp.broadcast_to(-jnp.sum(ps * dlh, axis=0, keepdims=True), (8, 128))
            return carry

        lax.fori_loop(0, nb, blk_body, 0)
        dk_ref[...] = dka[rad:rad + sl, :]
        dv_ref[...] = dva[rad:rad + sl, :]

    const3 = lambda b, r: (0, 0, 0)
    in_specs = [_band_spec(sl, qs), _band_spec(sl, ks), _band_spec(sl, vs), pl.BlockSpec((nh, blk, kw), const3)]
    args = [src] * 3 + [bias]
    if has_sink:
        in_specs.append(pl.BlockSpec((nh, 8, 128), const3))
        args.append(sink)
    row = _band_spec(sl, (256, 0))
    in_specs += [_band_spec(sl, (256, dcol)), row, row]
    args += [dy, lse, delta]
    out_specs = [row, _band_spec(sl, (wk, 0)), _band_spec(sl, (wv, 0)), pl.BlockSpec((nh, blk, kw), const3)]
    out_shape = [_sds((bl, dil, sl, 256), F32), _sds((bl, dil, sl, wk), F32), _sds((bl, dil, sl, wv), F32),
                 _sds((nh, blk, kw), F32)]
    if has_sink:
        out_specs.append(pl.BlockSpec((nh, 8, 128), const3))
        out_shape.append(_sds((nh, 8, 128), F32))
    return pl.pallas_call(
        body, name=name, grid=(bl, dil), in_specs=in_specs, out_specs=out_specs, out_shape=out_shape,
        scratch_shapes=[pltpu.VMEM((sl + 2 * rad, wk), BF16), pltpu.VMEM((sl + 2 * rad, wv), BF16),
                        pltpu.VMEM((sl + 2 * rad, wk), F32), pltpu.VMEM((sl + 2 * rad, wv), F32)],
        compiler_params=_cparams(ARB, ARB),
    )(*args)


def _combine_a(os_, ls_, *, tm, name):
    bl = os_[1].shape[0]
    t = bl * SEQ
    nt = SEQ // tm

    def body(o1, o4, o16, l1, l4, l16, y_ref, lt_ref, scr):
        for k, (d, ref) in enumerate(((4, o4), (16, o16), (4, l4), (16, l16))):
            for r in range(d):
                _write_residue(scr, 2 * k, r, d, ref[r])
        o2, o3, b, c = (_gather_cols(scr, 2 * k, 2) for k in range(4))
        a = l1[...]
        m = jnp.maximum(jnp.maximum(a, b), c)
        ea, eb, ec = jnp.exp(a - m), jnp.exp(b - m), jnp.exp(c - m)
        den = ea + eb + ec
        y_ref[...] = (ea / den) * o1[...] + (eb / den) * o2 + (ec / den) * o3
        lt_ref[...] = m + jnp.log(den)

    specs = _residue_specs(tm, 256, nt)
    return pl.pallas_call(
        body, name=name, grid=(bl, nt), in_specs=specs * 2, out_specs=[specs[0]] * 2,
        out_shape=[_sds((t, 256), F32)] * 2, scratch_shapes=[pltpu.VMEM((8, tm, LANES), F32)],
        compiler_params=_cparams(PAR, PAR),
    )(*os_, *ls_)


def _deltas(dycat, ya, yb, yd, lse_a, *, tm, name):
    t = ya.shape[0]
    bl = t // SEQ
    nt = SEQ // tm

    def body(dy_ref, ya_ref, yb_ref, yd_ref, la_ref, dy4, dy16, l4, l16, da1, da4, da16, db_ref, dd_ref, scr):
        e = _group_sum_matrix(256, True)
        dya = dy_ref[:, 0:256]
        dla = _seg_sum(dya * ya_ref[...], e)
        da1[...] = dla
        db_ref[...] = _seg_sum(dy_ref[:, 256:512] * yb_ref[...], e)
        dd_ref[...] = _seg_sum(dy_ref[:, 768:1024] * yd_ref[...], e)
        for k, (val, r4, r16) in enumerate(((dya, dy4, dy16), (la_ref[...], l4, l16), (dla, da4, da16))):
            _scatter_cols(scr, 2 * k, val)
            for d, ref in ((4, r4), (16, r16)):
                for r in range(d):
                    ref[r] = _read_residue(scr, 2 * k, 2, r, d)

    specs = _residue_specs(tm, 256, nt)
    nat = specs[0]
    shapes = _residue_shapes(bl, 256, F32)
    outs = pl.pallas_call(
        body, name=name, grid=(bl, nt),
        in_specs=[pl.BlockSpec((tm, 1024), lambda b, i: (b * nt + i, 0)), nat, nat, nat, nat],
        out_specs=specs[1:] + specs[1:] + specs + [nat, nat],
        out_shape=shapes[1:] + shapes[1:] + shapes + [shapes[0], shapes[0]],
        scratch_shapes=[pltpu.VMEM((6, tm, LANES), F32)],
        compiler_params=_cparams(PAR, PAR),
    )(dycat, ya, yb, yd, lse_a)
    return outs[0:2], outs[2:4], outs[4:7], outs[7], outs[8]


def _dense_fwd(qd, *, tq, name):
    t = qd.shape[0]
    bl = t // SEQ
    nq = SEQ // tq

    def body(q_ref, k_ref, v_ref, o_ref, l_ref):
        q = q_ref[...]
        for g in range(2):
            h0, h1 = 2 * g, 2 * g + 1
            q2 = jnp.concatenate([q[:, h0 * 64:(h0 + 1) * 64], q[:, h1 * 64:(h1 + 1) * 64]], axis=0)
            kg = k_ref[:, g * 64:(g + 1) * 64]
            vg = v_ref[:, g * 64:(g + 1) * 64]
            s = lax.dot_general(q2, kg, NT, preferred_element_type=F32)
            m = jnp.max(s, axis=1, keepdims=True)
            p = jnp.exp(s - m)
            den = jnp.sum(p, axis=1, keepdims=True)
            o2 = jnp.dot(p.astype(BF16), vg, preferred_element_type=F32) / den
            l2 = jnp.broadcast_to(m + jnp.log(den), (2 * tq, 64))
            o_ref[:, h0 * 64:(h0 + 1) * 64] = o2[:tq]
            o_ref[:, h1 * 64:(h1 + 1) * 64] = o2[tq:]
            l_ref[:, h0 * 64:(h0 + 1) * 64] = l2[:tq]
            l_ref[:, h1 * 64:(h1 + 1) * 64] = l2[tq:]

    q3 = qd.reshape(bl, SEQ, 512)
    o, lse = pl.pallas_call(
        body, name=name, grid=(bl, nq),
        in_specs=[pl.BlockSpec((None, tq, 256), lambda b, i: (b, i, 0)),
                  pl.BlockSpec((None, SEQ, 128), lambda b, i: (b, 0, 2)),
                  pl.BlockSpec((None, SEQ, 128), lambda b, i: (b, 0, 3))],
        out_specs=[pl.BlockSpec((None, tq, 256), lambda b, i: (b, i, 0))] * 2,
        out_shape=[_sds((bl, SEQ, 256), F32)] * 2,
        compiler_params=_cparams(PAR, PAR),
    )(q3, q3, q3)
    return o.reshape(t, 256), lse.reshape(t, 256)


def _dense_bwd(qd, dycat, lse, delta, *, tq, name):
    t = qd.shape[0]
    bl = t // SEQ
    nq = SEQ // tq

    def body(q_ref, k_ref, v_ref, do_ref, l_ref, dl_ref, dq_ref, dk_ref, dv_ref):
        @pl.when(pl.program_id(1) == 0)
        def _():
            dk_ref[...] = jnp.zeros_like(dk_ref)
            dv_ref[...] = jnp.zeros_like(dv_ref)

        q = q_ref[...]
        do = do_ref[...].astype(BF16)
        lv = l_ref[...]
        dlv = dl_ref[...]
        for g in range(2):
            h0, h1 = 2 * g, 2 * g + 1
            q2 = jnp.concatenate([q[:, h0 * 64:(h0 + 1) * 64], q[:, h1 * 64:(h1 + 1) * 64]], axis=0)
            do2 = jnp.concatenate([do[:, h0 * 64:(h0 + 1) * 64], do[:, h1 * 64:(h1 + 1) * 64]], axis=0)
            l2 = jnp.concatenate([lv[:, h0 * 64:h0 * 64 + 1], lv[:, h1 * 64:h1 * 64 + 1]], axis=0)
            dl2 = jnp.concatenate([dlv[:, h0 * 64:h0 * 64 + 1], dlv[:, h1 * 64:h1 * 64 + 1]], axis=0)
            kg = k_ref[:, g * 64:(g + 1) * 64]
            vg = v_ref[:, g * 64:(g + 1) * 64]
            s = lax.dot_general(q2, kg, NT, preferred_element_type=F32)
            p = jnp.exp(s - l2)
            dp = lax.dot_general(do2, vg, NT, preferred_element_type=F32)
            ds = (p * (dp - dl2)).astype(BF16)
            dq2 = jnp.dot(ds, kg, preferred_element_type=F32)
            dq_ref[:, h0 * 64:(h0 + 1) * 64] = dq2[:tq]
            dq_ref[:, h1 * 64:(h1 + 1) * 64] = dq2[tq:]
            dk_ref[:, g * 64:(g + 1) * 64] += lax.dot_general(ds, q2, TN, preferred_element_type=F32)
            dv_ref[:, g * 64:(g + 1) * 64] += lax.dot_general(p.astype(BF16), do2, TN, preferred_element_type=F32)

    q3 = qd.reshape(bl, SEQ, 512)
    tile = pl.BlockSpec((None, tq, 256), lambda b, i: (b, i, 0))
    full = pl.BlockSpec((None, SEQ, 128), lambda b, i: (b, 0, 0))
    dq, dk, dv = pl.pallas_call(
        body, name=name, grid=(bl, nq),
        in_specs=[tile, pl.BlockSpec((None, SEQ, 128), lambda b, i: (b, 0, 2)),
                  pl.BlockSpec((None, SEQ, 128), lambda b, i: (b, 0, 3)),
                  pl.BlockSpec((None, tq, 256), lambda b, i: (b, i, 3)), tile, tile],
        out_specs=[tile, full, full],
        out_shape=[_sds((bl, SEQ, 256), F32), _sds((bl, SEQ, 128), F32), _sds((bl, SEQ, 128), F32)],
        compiler_params=_cparams(PAR, ARB),
    )(q3, q3, q3, dycat.reshape(bl, SEQ, 1024), lse.reshape(bl, SEQ, 256), delta.reshape(bl, SEQ, 256))
    return dq.reshape(t, 256), dk.reshape(t, 128), dv.reshape(t, 128)


def _c_norm(cv, gam, bet):
    vg = _gelu(cv)
    mu = jnp.mean(vg, axis=-1, keepdims=True)
    xc = vg - mu
    r = lax.rsqrt(jnp.mean(xc * xc, axis=-1, keepdims=True) + EPS)
    xhat = xc * r
    return xhat * gam + bet, xhat, r


def _c_fwd(proj, gam, bet, ws, bst, *, tm, name):
    t = proj.shape[0]
    nch = tm // C_CHUNK

    def body(u_ref, v_ref, g_ref, b_ref, ws_ref, bs_ref, y_ref):
        vn, _, _ = _c_norm(v_ref[...], g_ref[...], b_ref[...])
        vnb = vn.astype(BF16)
        for c in range(nch):
            rows = slice(c * C_CHUNK, (c + 1) * C_CHUNK)
            for g in range(C_GROUPS):
                gs = slice(g * 64, (g + 1) * 64)
                mixed = jnp.dot(ws_ref[g], vnb[rows, gs], preferred_element_type=F32) + bs_ref[:, gs]
                y_ref[rows, gs] = _gelu(u_ref[rows, gs]) * mixed

    vec = pl.BlockSpec((1, 256), lambda i: (0, 0))
    return pl.pallas_call(
        body, name=name, grid=(t // tm,),
        in_specs=[pl.BlockSpec((tm, 256), lambda i: (i, 5)), pl.BlockSpec((tm, 256), lambda i: (i, 6)), vec, vec,
                  pl.BlockSpec((C_GROUPS, C_CHUNK, C_CHUNK), lambda i: (0, 0, 0)),
                  pl.BlockSpec((C_CHUNK, 256), lambda i: (0, 0))],
        out_specs=pl.BlockSpec((tm, 256), lambda i: (i, 0)), out_shape=_sds((t, 256), F32),
        compiler_params=_cparams(PAR),
    )(proj, proj, gam, bet, ws, bst)


def _c_bwd(proj, dycat, gam, bet, ws, wst, bst, *, tm, name):
    t = proj.shape[0]
    nch = tm // C_CHUNK
    nstep = t // tm

    def body(u_ref, v_ref, dy_ref, g_ref, b_ref, ws_ref, wst_ref, bs_ref,
             du_ref, dv_ref, dws_ref, dbs_ref, dg_ref, db_ref, dvn_s):
        step = pl.program_id(0)

        @pl.when(step == 0)
        def _():
            dws_ref[...] = jnp.zeros_like(dws_ref)
            dbs_ref[...] = jnp.zeros_like(dbs_ref)
            dg_ref[...] = jnp.zeros_like(dg_ref)
            db_ref[...] = jnp.zeros_like(db_ref)

        cv = v_ref[...]
        gam_v = g_ref[...]
        vn, xhat, r = _c_norm(cv, gam_v, b_ref[...])
        vnb = vn.astype(BF16)
        for c in range(nch):
            rows = slice(c * C_CHUNK, (c + 1) * C_CHUNK)
            for g in range(C_GROUPS):
                gs = slice(g * 64, (g + 1) * 64)
                cu = u_ref[rows, gs]
                dy = dy_ref[rows, gs]
                mixed = jnp.dot(ws_ref[g], vnb[rows, gs], preferred_element_type=F32) + bs_ref[:, gs]
                du_ref[rows, gs] = dy * mixed * _gelu_grad(cu)
                dmix = dy * _gelu(cu)
                dbs_ref[:, gs] += dmix
                dmb = dmix.astype(BF16)
                dws_ref[g] += lax.dot_general(dmb, vnb[rows, gs], NT, preferred_element_type=F32)
                dvn_s[rows, gs] = jnp.dot(wst_ref[g], dmb, preferred_element_type=F32)
        dvn = dvn_s[...]
        dg_ref[...] += jnp.sum(dvn * xhat, axis=0, keepdims=True)
        db_ref[...] += jnp.sum(dvn, axis=0, keepdims=True)
        dxh = dvn * gam_v
        dvg = r * (dxh - jnp.mean(dxh, axis=-1, keepdims=True) - xhat * jnp.mean(dxh * xhat, axis=-1, keepdims=True))
        dv_ref[...] = dvg * _gelu_grad(cv)

        @pl.when(step == nstep - 1)
        def _():
            dbs_ref[...] = _seg_sum(dbs_ref[...], _group_sum_matrix(256, True))

    vec = pl.BlockSpec((1, 256), lambda i: (0, 0))
    mat = pl.BlockSpec((C_GROUPS, C_CHUNK, C_CHUNK), lambda i: (0, 0, 0))
    bsp = pl.BlockSpec((C_CHUNK, 256), lambda i: (0, 0))
    tile = pl.BlockSpec((tm, 256), lambda i: (i, 0))
    return pl.pallas_call(
        body, name=name, grid=(nstep,),
        in_specs=[pl.BlockSpec((tm, 256), lambda i: (i, 5)), pl.BlockSpec((tm, 256), lambda i: (i, 6)),
                  pl.BlockSpec((tm, 256), lambda i: (i, 2)), vec, vec, mat, mat, bsp],
        out_specs=[tile, tile, mat, bsp, vec, vec],
        out_shape=[_sds((t, 256), F32), _sds((t, 256), F32), _sds((C_GROUPS, C_CHUNK, C_CHUNK), F32),
                   _sds((C_CHUNK, 256), F32), _sds((1, 256), F32), _sds((1, 256), F32)],
        scratch_shapes=[pltpu.VMEM((tm, 256), F32)],
        compiler_params=_cparams(ARB),
    )(proj, proj, dycat, gam, bet, ws, wst, bst)


FF_TC = 128
FF_NB = D_FF // FF_TC
FF_CH = 64
FF_HALO = 16


def _edge_taps(ref, first):
    if first:
        ext = ref[0:FF_CH + FF_HALO, :].astype(F32)
        body = slice(0, FF_CH)
    else:
        ext = ref[SEQ - FF_CH - FF_HALO:SEQ, :].astype(F32)
        body = slice(FF_HALO, FF_HALO + FF_CH)
    n = ext.shape[0]
    row = lax.broadcasted_iota(jnp.int32, ext.shape, 0)
    dn = pltpu.roll(ext, 1, 0)
    up = pltpu.roll(ext, n - 1, 0)
    if first:
        dn = jnp.where(row == 0, 0.0, dn)
    else:
        up = jnp.where(row == n - 1, 0.0, up)
    return dn[body], ext[body], up[body]


def _mid_taps(ref, r0):
    ext = ref[pl.ds(pl.multiple_of(r0 - FF_HALO, FF_HALO), FF_CH + 2 * FF_HALO), :].astype(F32)
    n = ext.shape[0]
    body = slice(FF_HALO, FF_HALO + FF_CH)
    return pltpu.roll(ext, 1, 0)[body], ext[body], pltpu.roll(ext, n - 1, 0)[body]


def _chunk_loop(step):
    step(0, lambda ref: _edge_taps(ref, True))

    def mid(i, carry):
        r0 = pl.multiple_of(i * FF_CH, FF_CH)
        step(r0, lambda ref: _mid_taps(ref, r0))
        return carry

    lax.fori_loop(1, SEQ // FF_CH - 1, mid, 0)
    step(SEQ - FF_CH, lambda ref: _edge_taps(ref, False))


def _conv3(taps, w_ref, b_ref):
    dn, md, up = taps
    return w_ref[0:1, :] * dn + w_ref[1:2, :] * md + w_ref[2:3, :] * up + b_ref[...]


def _ff_specs(order):
    def at(fn):
        return (lambda b, j: fn(b, j)) if order == "bj" else (lambda j, b: fn(b, j))
    hs = [pl.BlockSpec((None, SEQ, FF_TC), at(lambda b, j, o=o: (b, 0, j + o))) for o in (0, FF_NB)]
    ws = [pl.BlockSpec((3, FF_TC), at(lambda b, j, o=o: (0, j + o))) for o in (0, FF_NB)]
    bs = [pl.BlockSpec((1, FF_TC), at(lambda b, j, o=o: (0, j + o))) for o in (0, FF_NB)]
    return hs, ws, bs


def _conv_gate_fwd(h, cw, cb, *, name):
    t = h.shape[0]
    bl = t // SEQ

    def body(hg_ref, hu_ref, wg_ref, wu_ref, bg_ref, bu_ref, a_ref):
        def step(r0, taps):
            cg = _conv3(taps(hg_ref), wg_ref, bg_ref)
            cu = _conv3(taps(hu_ref), wu_ref, bu_ref)
            a_ref[pl.ds(r0, FF_CH), :] = (cg * _sigmoid(cg) * cu).astype(BF16)

        _chunk_loop(step)

    hs, ws, bs = _ff_specs("bj")
    h3 = h.reshape(bl, SEQ, 2 * D_FF)
    act = pl.pallas_call(
        body, name=name, grid=(bl, FF_NB), in_specs=hs + ws + bs,
        out_specs=pl.BlockSpec((None, SEQ, FF_TC), lambda b, j: (b, 0, j)),
        out_shape=_sds((bl, SEQ, D_FF), BF16),
        compiler_params=_cparams(PAR, PAR),
    )(h3, h3, cw, cw, cb, cb)
    return act.reshape(t, D_FF)


def _conv_gate_bwd(h, dact, cw, cb, *, name):
    t = h.shape[0]
    bl = t // SEQ

    def body(hg_ref, hu_ref, wg_ref, wu_ref, bg_ref, bu_ref, da_ref,
             dhg_ref, dhu_ref, dwg_ref, dwu_ref, dbg_ref, dbu_ref, dg_s, du_s):
        @pl.when(pl.program_id(1) == 0)
        def _():
            for ref in (dwg_ref, dwu_ref, dbg_ref, dbu_ref):
                ref[...] = jnp.zeros_like(ref)

        red = lambda x: jnp.sum(x, axis=0, keepdims=True)

        def pass1(r0, taps):
            tg, tu = taps(hg_ref), taps(hu_ref)
            cg = _conv3(tg, wg_ref, bg_ref)
            cu = _conv3(tu, wu_ref, bu_ref)
            da = da_ref[pl.ds(r0, FF_CH), :].astype(F32)
            sg = _sigmoid(cg)
            dcg = da * cu * (sg * (1.0 + cg * (1.0 - sg)))
            dcu = da * (cg * sg)
            dg_s[pl.ds(r0, FF_CH), :] = dcg
            du_s[pl.ds(r0, FF_CH), :] = dcu
            for d, tp, dw_ref, db_ref in ((dcg, tg, dwg_ref, dbg_ref), (dcu, tu, dwu_ref, dbu_ref)):
                for k in range(3):
                    dw_ref[k:k + 1, :] += red(d * tp[k])
                db_ref[...] += red(d)

        _chunk_loop(pass1)

        def pass2(r0, taps):
            for s, w_ref, o_ref in ((dg_s, wg_ref, dhg_ref), (du_s, wu_ref, dhu_ref)):
                dn, md, up = taps(s)
                o_ref[pl.ds(r0, FF_CH), :] = (w_ref[0:1, :] * up + w_ref[1:2, :] * md + w_ref[2:3, :] * dn).astype(BF16)

        _chunk_loop(pass2)

    hs, ws, bs = _ff_specs("jb")
    half = pl.BlockSpec((None, SEQ, FF_TC), lambda j, b: (b, 0, j))
    wsp = pl.BlockSpec((3, FF_TC), lambda j, b: (0, j))
    bsp = pl.BlockSpec((1, FF_TC), lambda j, b: (0, j))
    h3 = h.reshape(bl, SEQ, 2 * D_FF)
    dhg, dhu, dwg, dwu, dbg, dbu = pl.pallas_call(
        body, name=name, grid=(FF_NB, bl), in_specs=hs + ws + bs + [half],
        out_specs=[half, half, wsp, wsp, bsp, bsp],
        out_shape=[_sds((bl, SEQ, D_FF), BF16), _sds((bl, SEQ, D_FF), BF16), _sds((3, D_FF), F32), _sds((3, D_FF), F32),
                   _sds((1, D_FF), F32), _sds((1, D_FF), F32)],
        scratch_shapes=[pltpu.VMEM((SEQ, FF_TC), F32), pltpu.VMEM((SEQ, FF_TC), F32)],
        compiler_params=_cparams(PAR, ARB),
    )(h3, h3, cw, cw, cb, cb, dact.reshape(bl, SEQ, D_FF))
    return (dhg.reshape(t, D_FF), dhu.reshape(t, D_FF), jnp.concatenate([dwg, dwu], axis=1),
            jnp.concatenate([dbg, dbu], axis=1))


def _ple_fwd(x2, gain, wg, pe, pe_blk, wp, *, tm, tn, name):
    t, k = x2.shape
    n = wg.shape[1]

    def body(x_ref, g_ref, wg_ref, pe_ref, wp_ref, xr_ref, hn_ref, x3_ref, gt_ref, pp_ref, hn_s):
        @pl.when(pl.program_id(1) == 0)
        def _():
            x = x_ref[...]
            r = lax.rsqrt(jnp.mean(x * x, axis=-1, keepdims=True) + EPS)
            hn_s[...] = (x * r * g_ref[...]).astype(BF16)
            hn_ref[...] = hn_s[...]

        gate = _sigmoid(jnp.dot(hn_s[...], wg_ref[...], preferred_element_type=F32))
        pp = jnp.dot(pe_ref[...].astype(BF16), wp_ref[...], preferred_element_type=F32)
        gt_ref[...] = gate
        pp_ref[...] = pp
        x3_ref[...] = xr_ref[...] + pp * gate

    tile = pl.BlockSpec((tm, tn), lambda i, j: (i, j))
    return pl.pallas_call(
        body, name=name, grid=(t // tm, n // tn),
        in_specs=[pl.BlockSpec((tm, k), lambda i, j: (i, 0)), pl.BlockSpec((1, k), lambda i, j: (0, 0)),
                  pl.BlockSpec((k, tn), lambda i, j: (0, j)), pl.BlockSpec((tm, PLE_DIM), lambda i, j: (pe_blk + i, 0)),
                  pl.BlockSpec((PLE_DIM, tn), lambda i, j: (0, j)), tile],
        out_specs=[pl.BlockSpec((tm, k), lambda i, j: (i, 0)), tile, tile, tile],
        out_shape=[_sds((t, k), BF16), _sds((t, n), F32), _sds((t, n), F32), _sds((t, n), F32)],
        scratch_shapes=[pltpu.VMEM((tm, k), BF16)],
        compiler_params=_cparams(PAR, ARB),
    )(x2, gain, wg, pe, wp, x2)


def _ple_bwd_ew(dx3, gate, pp, *, tm, name):
    t, n = dx3.shape

    def body(d_ref, g_ref, p_ref, dz_ref, dpp_ref):
        d, g = d_ref[...], g_ref[...]
        dz_ref[...] = (d * p_ref[...] * g * (1.0 - g)).astype(BF16)
        dpp_ref[...] = (d * g).astype(BF16)

    spec = pl.BlockSpec((tm, n), lambda i: (i, 0))
    return pl.pallas_call(
        body, name=name, grid=(t // tm,), in_specs=[spec] * 3, out_specs=[spec] * 2,
        out_shape=[_sds((t, n), BF16)] * 2, compiler_params=_cparams(PAR),
    )(dx3, gate, pp)


def _loss_head(y, tgt, *, tm, name):
    t, d = y.shape

    def body(y_ref, t_ref, l_ref, dy_ref):
        @pl.when(pl.program_id(0) == 0)
        def _():
            l_ref[...] = jnp.zeros_like(l_ref)

        e = y_ref[...] - t_ref[...]
        dy_ref[...] = e * (1.0 / d)
        s = jnp.sum(jnp.sum(e * e, axis=1, keepdims=True), axis=0, keepdims=True)
        l_ref[...] += jnp.broadcast_to(s * (0.5 / d), (8, 128))

    spec = pl.BlockSpec((tm, d), lambda i: (i, 0))
    return pl.pallas_call(
        body, name=name, grid=(t // tm,), in_specs=[spec, spec],
        out_specs=[pl.BlockSpec((8, 128), lambda i: (0, 0)), spec],
        out_shape=[_sds((8, 128), F32), _sds((t, d), F32)], compiler_params=_cparams(ARB),
    )(y, tgt)


BIAS_PC = 8192


def _onehot(bucket_row):
    rows = lax.broadcasted_iota(jnp.int32, (REL_BUCKETS, bucket_row.shape[1]), 0)
    return (rows == bucket_row).astype(F32)


def _bias_lookup(table_t, bucket, *, name):
    h = table_t.shape[0]
    p = bucket.shape[1]

    def body(t_ref, b_ref, o_ref):
        bk = b_ref[...]
        val = jnp.dot(t_ref[...], _onehot(bk), precision=HI, preferred_element_type=F32)
        o_ref[...] = jnp.where(bk >= 0, val, NEG_INF)

    return pl.pallas_call(
        body, name=name, grid=(p // BIAS_PC,),
        in_specs=[pl.BlockSpec((h, REL_BUCKETS), lambda i: (0, 0)), pl.BlockSpec((1, BIAS_PC), lambda i: (0, i))],
        out_specs=pl.BlockSpec((h, BIAS_PC), lambda i: (0, i)), out_shape=_sds((h, p), F32),
        compiler_params=_cparams(PAR),
    )(table_t, bucket)


def _bucket_reduce(dbias, bucket, *, name):
    h, p = dbias.shape

    def body(d_ref, b_ref, o_ref):
        @pl.when(pl.program_id(0) == 0)
        def _():
            o_ref[...] = jnp.zeros_like(o_ref)

        o_ref[...] += lax.dot_general(d_ref[...], _onehot(b_ref[...]), NT, precision=HI, preferred_element_type=F32)

    return pl.pallas_call(
        body, name=name, grid=(p // BIAS_PC,),
        in_specs=[pl.BlockSpec((h, BIAS_PC), lambda i: (0, i)), pl.BlockSpec((1, BIAS_PC), lambda i: (0, i))],
        out_specs=pl.BlockSpec((h, REL_BUCKETS), lambda i: (0, 0)), out_shape=_sds((h, REL_BUCKETS), F32),
        compiler_params=_cparams(ARB),
    )(dbias, bucket)


def _adamw_math(w, g, m, v):
    m = ADAM_B1 * m + (1.0 - ADAM_B1) * g
    v = ADAM_B2 * v + (1.0 - ADAM_B2) * (g * g)
    m_hat = m / (1.0 - ADAM_B1 ** ADAM_STEP)
    v_hat = v / (1.0 - ADAM_B2 ** ADAM_STEP)
    delta = -ADAM_LR * (m_hat / (jnp.sqrt(v_hat) + ADAM_EPS) + ADAM_WD * w)
    return delta, m, v


def _adamw_reduce(parts, w, m, v, *, tr, name):
    r, c = w.shape

    def body(p_ref, w_ref, m_ref, v_ref, g_ref, d_ref, nm_ref, nv_ref):
        g = p_ref[0].astype(F32)
        for k in range(1, N_DEV):
            g = g + p_ref[k].astype(F32)
        d, nm, nv = _adamw_math(w_ref[...], g, m_ref[...], v_ref[...])
        g_ref[...] = g
        d_ref[...] = d
        nm_ref[...] = nm
        nv_ref[...] = nv

    spec = pl.BlockSpec((tr, c), lambda i: (i, 0))
    return pl.pallas_call(
        body, name=name, grid=(r // tr,),
        in_specs=[pl.BlockSpec((N_DEV, tr, c), lambda i: (0, i, 0)), spec, spec, spec],
        out_specs=[spec] * 4, out_shape=[_sds((r, c), F32)] * 4, compiler_params=_cparams(PAR),
    )(parts, w, m, v)


def _adamw_plain(g, w, m, v, *, name):
    def body(g_ref, w_ref, m_ref, v_ref, d_ref, nm_ref, nv_ref):
        d, nm, nv = _adamw_math(w_ref[...], g_ref[...], m_ref[...], v_ref[...])
        d_ref[...] = d
        nm_ref[...] = nm
        nv_ref[...] = nv

    return pl.pallas_call(body, name=name, out_shape=[_sds(w.shape, F32)] * 3)(g, w, m, v)


def _mesh_pos():
    return lax.axis_index("x"), lax.axis_index("y"), lax.axis_index("c")


def _allgather_body(x_refs, out_refs, send_sems, recv_sems, local_sems, slot):
    x, y, c = _mesh_pos()
    me, sibling = (x, y, c), (x, y, 1 - c)
    chips = [(1 - x, y), (x, 1 - y), (1 - x, 1 - y)]
    waits = []
    for a, (x_ref, out_ref) in enumerate(zip(x_refs, out_refs)):
        def copy(k, block, to, src=None, out_ref=out_ref, a=a):
            return pltpu.make_async_remote_copy(
                src_ref=slot(out_ref, block) if src is None else src, dst_ref=slot(out_ref, block),
                send_sem=send_sems.at[a, k], recv_sem=recv_sems.at[a, k], device_id=to, device_id_type=MESH)

        mine = pltpu.make_async_copy(x_ref, slot(out_ref, me), local_sems.at[a])
        mine.start()
        first = [copy(0, me, sibling, src=x_ref)]
        first += [copy(1 + j, me, (*chip, c), src=x_ref) for j, chip in enumerate(chips)]
        for cp in first:
            cp.start()
        waits.append((copy, mine, first))
    sends = []
    for copy, mine, first in waits:
        passed = [copy(4 + j, (*chip, c), sibling) for j, chip in enumerate(chips)]
        for j, chip in enumerate(chips):
            copy(1 + j, (*chip, c), me).wait_recv()
            passed[j].start()
        sends.append(passed)
    for (copy, mine, first), passed in zip(waits, sends):
        copy(0, sibling, me).wait_recv()
        for j, chip in enumerate(chips):
            copy(4 + j, (*chip, 1 - c), me).wait_recv()
        for cp in first + passed:
            cp.wait_send()
        mine.wait()


def _allgather_hbm(xs, *, name):
    na = len(xs)

    def body(*refs):
        x_refs, out_refs = refs[:na], refs[na:2 * na]
        send_sems, recv_sems, local_sems = refs[2 * na:]
        _allgather_body(x_refs, out_refs, send_sems, recv_sems, local_sems,
                        lambda ref, pos: ref.at[4 * pos[0] + 2 * pos[1] + pos[2]])

    hbm = pl.BlockSpec(memory_space=pltpu.HBM)
    return pl.pallas_call(
        body, name=name, in_specs=[hbm] * na, out_specs=[hbm] * na,
        out_shape=[_sds((N_DEV,) + x.shape, x.dtype) for x in xs],
        scratch_shapes=[pltpu.SemaphoreType.DMA((na, 7)), pltpu.SemaphoreType.DMA((na, 7)),
                        pltpu.SemaphoreType.DMA((na,))],
    )(*xs)


def _allgather_vmem(x, *, reduce, name):
    r, c = x.shape

    def body(x_ref, out_ref, *rest):
        if reduce:
            gath, send_sems, recv_sems, local_sems = rest
        else:
            send_sems, recv_sems, local_sems = rest
            gath = out_ref
        _allgather_body([x_ref], [gath], send_sems, recv_sems, local_sems,
                        lambda ref, pos: ref.at[pl.ds((4 * pos[0] + 2 * pos[1] + pos[2]) * r, r), :])
        if reduce:
            acc = gath[0:r, :]
            for k in range(1, N_DEV):
                acc = acc + gath[k * r:(k + 1) * r, :]
            out_ref[...] = acc

    vm = pl.BlockSpec(memory_space=pltpu.VMEM)
    scratch = [pltpu.SemaphoreType.DMA((1, 7)), pltpu.SemaphoreType.DMA((1, 7)), pltpu.SemaphoreType.DMA((1,))]
    if reduce:
        scratch = [pltpu.VMEM((N_DEV * r, c), x.dtype)] + scratch
    return pl.pallas_call(
        body, name=name, in_specs=[vm], out_specs=vm,
        out_shape=_sds((r, c) if reduce else (N_DEV * r, c), x.dtype), scratch_shapes=scratch,
    )(x)


def _all_to_all_hbm(xs, *, name):
    na = len(xs)

    def body(*refs):
        x_refs, out_refs = refs[:na], refs[na:2 * na]
        send_sems, recv_sems, local_sems = refs[2 * na:]
        x, y, c = _mesh_pos()
        me = 4 * x + 2 * y + c
        rel = [(0, 0, 1), (1, 0, 0), (0, 1, 0), (1, 1, 0), (1, 0, 1), (0, 1, 1), (1, 1, 1)]
        copies = []
        for a, (x_ref, out_ref) in enumerate(zip(x_refs, out_refs)):
            own = pltpu.make_async_copy(x_ref.at[me], out_ref.at[me], local_sems.at[a])
            own.start()
            copies.append(own)
            for k, (fx, fy, fc) in enumerate(rel):
                px, py, pc = x ^ fx, y ^ fy, c ^ fc
                peer = 4 * px + 2 * py + pc
                cp = pltpu.make_async_remote_copy(
                    src_ref=x_ref.at[peer], dst_ref=out_ref.at[me], send_sem=send_sems.at[a, k],
                    recv_sem=recv_sems.at[a, k], device_id=(px, py, pc), device_id_type=MESH)
                cp.start()
                copies.append(cp)
        for cp in copies:
            cp.wait()

    hbm = pl.BlockSpec(memory_space=pltpu.HBM)
    return pl.pallas_call(
        body, name=name, in_specs=[hbm] * na, out_specs=[hbm] * na,
        out_shape=[_sds(x.shape, x.dtype) for x in xs],
        scratch_shapes=[pltpu.SemaphoreType.DMA((na, 7)), pltpu.SemaphoreType.DMA((na, 7)),
                        pltpu.SemaphoreType.DMA((na,))],
    )(*xs)


def _t5_bucket(rel):
    nb = REL_BUCKETS // 2
    ret = jnp.where(rel > 0, nb, 0)
    n = jnp.abs(rel)
    max_exact = nb // 2
    nf = jnp.maximum(n, 1).astype(F32)
    large = max_exact + (jnp.log(nf / max_exact) / math.log(REL_MAX_DIST / max_exact)
                         * (nb - max_exact)).astype(jnp.int32)
    large = jnp.minimum(large, nb - 1)
    return ret + jnp.where(n < max_exact, n, large)


def _band_pattern(block, radius, dil):
    kw = block + 2 * radius
    rel = jnp.arange(kw)[None, :] - radius - jnp.arange(block)[:, None]
    return jnp.where(jnp.abs(rel) <= radius, _t5_bucket(rel * dil), -1).astype(jnp.int32).reshape(1, block * kw)


def _rope_tables():
    lane = np.arange(64)
    seg, j = lane // 32, lane % 32
    inv = ROPE_THETA ** (-jnp.arange(0, 32, 2, dtype=F32) / 32)
    tpos = jnp.arange(SEQ)
    pos = jnp.where(jnp.asarray(seg)[None, :] == 0, (tpos // GRID_W)[:, None], (tpos % GRID_W)[:, None])
    ang = pos.astype(F32) * inv[jnp.asarray(j % 16)][None, :]
    cos = jnp.cos(ang)
    sins = jnp.where(jnp.asarray(j)[None, :] < 16, -jnp.sin(ang), jnp.sin(ang))
    return jnp.tile(cos, (1, 4)), jnp.tile(sins, (1, 4))


A_Q, A_K, A_V = (256, 0), (256, 1), (256, 2)
B_Q, B_K, B_V = (256, 0), (128, 2), (128, 3)
A_HEADS = dict(rad=A_RADIUS, nh=4, nkv=4)
B_HEADS = dict(rad=SWA_RADIUS, nh=4, nkv=2)


def _local_step(x, pe, tgt, rel_bias, wts):
    t = x.shape[0]
    bl = t // SEQ
    cos, sins = _rope_tables()
    blocks_a = [min(BAND_BLOCK, SEQ // d) for d in DILATIONS]
    pats_a = [_band_pattern(blk, A_RADIUS, d) for blk, d in zip(blocks_a, DILATIONS)]
    pat_b = _band_pattern(BAND_BLOCK, SWA_RADIUS, 1)
    table_t = rel_bias.T
    bias_a = [_bias_lookup(table_t[:4], pt, name=f"bias_a{ci}").reshape(4, blk, blk + 2 * A_RADIUS)
              for ci, (pt, blk) in enumerate(zip(pats_a, blocks_a))]
    bias_b = _bias_lookup(table_t[4:], pat_b, name="bias_b").reshape(4, BAND_BLOCK, BAND_BLOCK + 2 * SWA_RADIUS)
    nat4 = lambda a: a.reshape(bl, 1, SEQ, a.shape[-1])

    saved = []
    for li in range(DEPTH):
        w = wts[li]
        hn0, proj = _norm_mm((x,), w["g_mix"], w["w_in"], None, tm=1024, tn=1152, name="mix_in_fwd")
        qa1, qa4, qa16, qb, qd = _qkprep_fwd(proj, w["qk_gains"], cos, sins, tm=512, name="qkprep_fwd")
        qa = (nat4(qa1), qa4, qa16)
        oa, la = [], []
        for ci in range(3):
            o, l = _band_fwd(qa[ci], A_Q, A_K, A_V, bias_a[ci], None, name=f"band_a{ci}_fwd", **A_HEADS)
            oa.append(o)
            la.append(l)
        oa[0], la[0] = oa[0].reshape(t, 256), la[0].reshape(t, 256)
        ya, lse_a = _combine_a(oa, la, tm=512, name="combine_a")
        yb, lse_b = _band_fwd(nat4(qb), B_Q, B_K, B_V, bias_b, w["sink_t"], name="band_b_fwd", **B_HEADS)
        yb = yb.reshape(t, 256)
        yc = _c_fwd(proj, w["c_g"], w["c_b"], w["c_ws"], w["c_bst"], tm=512, name="c_fwd")
        yd, lse_d = _dense_fwd(qd, tq=128, name="dense_fwd")
        mixed, x1 = _norm_mm((ya, yb, yc, yd), w["out_gain"], w["w_out"], x, tm=1024, tn=1024, name="mix_out_fwd")
        hn1, h = _norm_mm((x1,), w["g_ffn"], w["w_up"], None, tm=1024, tn=1408, name="ffn_up_fwd", out_dtype=BF16)
        act = _conv_gate_fwd(h, w["conv_w"], w["conv_b"], name="conv_gate_fwd")
        x2 = _mm(act, w["w_down"], "nn", x1, tm=1024, tn=1024, out_dtype=F32, name="ffn_down_fwd")
        hn2, x3, gate, pp = _ple_fwd(x2, w["g_ple"], w["w_gate"], pe, li * (t // 1024), w["w_proj"], tm=1024, tn=512,
                                     name="ple_fwd")
        saved.append(dict(x0=x, hn0=hn0, proj=proj, qa=qa, qb=qb, qd=qd, ya=ya, lse_a=lse_a, yb=yb, lse_b=lse_b,
                          yc=yc, yd=yd, lse_d=lse_d, mixed=mixed, x1=x1, hn1=hn1, h=h, act=act, x2=x2, hn2=hn2,
                          gate=gate, pp=pp))
        x = x3

    loss_tile, dx = _loss_head(x, tgt, tm=512, name="loss_head")
    grads = [None] * DEPTH
    d_table_a = jnp.zeros((4, REL_BUCKETS), F32)
    d_table_b = jnp.zeros((4, REL_BUCKETS), F32)
    for li in reversed(range(DEPTH)):
        w, s = wts[li], saved[li]
        g = {}
        dz, dpp = _ple_bwd_ew(dx, s["gate"], s["pp"], tm=512, name="ple_bwd_ew")
        g["w_gate"] = _mm(s["hn2"], dz, "tn", None, tm=1024, tn=512, out_dtype=BF16, name="dw_gate")
        g["w_proj"] = _mm(pe, dpp, "tn", None, tm=256, tn=1024, out_dtype=BF16, name="dw_proj", a_rows=(li, t))
        dx2, dx2b, g["g_ple"] = _mm_bt_normbwd((dz,), w["w_gate"], (s["x2"],), w["g_ple"], dx, tm=1024, tn=1024,
                                               name="ple_bwd", emit_bf16=True)
        g["w_down"] = _mm(s["act"], dx2b, "tn", None, tm=1408, tn=512, out_dtype=BF16, name="dw_down")
        dact = _mm(dx2b, w["w_down"], "nt", None, tm=1024, tn=1408, out_dtype=BF16, name="ffn_down_bwd")
        dhg, dhu, g["conv_w"], g["conv_b"] = _conv_gate_bwd(s["h"], dact, w["conv_w"], w["conv_b"], name="conv_gate_bwd")
        g["w_up"] = jnp.concatenate(
            [_mm(s["hn1"], dhalf, "tn", None, tm=1024, tn=1408, out_dtype=BF16, name=f"dw_up_{nm}")
             for nm, dhalf in (("gate", dhg), ("up", dhu))], axis=1)
        dx1, dx1b, g["g_ffn"] = _mm_bt_normbwd((dhg, dhu), w["w_up"], (s["x1"],), w["g_ffn"], dx2, tm=1024, tn=1408,
                                               name="ffn_up_bwd", emit_bf16=True)
        g["w_out"] = _mm(s["mixed"], dx1b, "tn", None, tm=1024, tn=512, out_dtype=BF16, name="dw_out")
        dycat, g["out_gain"] = _mm_bt_normbwd((dx1b,), w["w_out"], (s["ya"], s["yb"], s["yc"], s["yd"]), w["out_gain"],
                                              None, tm=1024, tn=1024, name="mix_out_bwd")
        dy_r, lse_r, dl_a, dl_b, dl_d = _deltas(dycat, s["ya"], s["yb"], s["yd"], s["lse_a"], tm=512, name="deltas")
        dy_a = (nat4(dycat),) + tuple(dy_r)
        lse_a = (nat4(s["lse_a"]),) + tuple(lse_r)
        dl_a = (nat4(dl_a[0]),) + tuple(dl_a[1:])
        da = []
        for ci in range(3):
            dq, dk, dv, dbias = _band_bwd(s["qa"][ci], A_Q, A_K, A_V, bias_a[ci], None, dy_a[ci], 0, lse_a[ci],
                                          dl_a[ci], name=f"band_a{ci}_bwd", **A_HEADS)
            if ci == 0:
                dq, dk, dv = (a.reshape(t, 256) for a in (dq, dk, dv))
            da.append((dq, dk, dv))
            d_table_a = d_table_a + _bucket_reduce(dbias.reshape(4, -1), pats_a[ci], name=f"bucket_a{ci}")
        dqb, dkb, dvb, dbias_b, dsink = _band_bwd(nat4(s["qb"]), B_Q, B_K, B_V, bias_b, w["sink_t"], nat4(dycat), 1,
                                                  nat4(s["lse_b"]), nat4(dl_b), name="band_b_bwd", **B_HEADS)
        d_table_b = d_table_b + _bucket_reduce(dbias_b.reshape(4, -1), pat_b, name="bucket_b")
        g["sink"] = dsink[:, 0, 0]
        dd = _dense_bwd(s["qd"], dycat, s["lse_d"], dl_d, tq=128, name="dense_bwd")
        dcu, dcv, g["c_ws"], dbs, g["c_g"], g["c_b"] = _c_bwd(s["proj"], dycat, w["c_g"], w["c_b"], w["c_ws"],
                                                               w["c_wst"], w["c_bst"], tm=512, name="c_bwd")
        g["c_bs"] = dbs[:, ::64].T
        db = (dqb.reshape(t, 256), dkb.reshape(t, 128), dvb.reshape(t, 128))
        dproj, dgains = _qkprep_bwd(s["proj"], da, db, dd, dcu, dcv, w["qk_gains"], cos, sins, tm=512, name="qkprep_bwd")
        g["qk_gain"] = dgains[:6, :64].reshape(3, 2, HEAD_DIM)
        g["w_in"] = _mm(s["hn0"], dproj, "tn", None, tm=1024, tn=1152, out_dtype=BF16, name="dw_in")
        dx, g["g_mix"] = _mm_bt_normbwd((dproj,), w["w_in"], (s["x0"],), w["g_mix"], dx1, tm=1024, tn=1152,
                                        name="mix_in_bwd")
        grads[li] = g
    d_rel_bias = jnp.concatenate([d_table_a, d_table_b], axis=0).T
    return loss_tile[0, 0], dx, grads, d_rel_bias


WEIGHT_NAMES = ("rel_bias", "ln_mix_g", "w_in", "qk_gain", "sink", "c_norm_g", "c_norm_b", "c_ws", "c_bs", "out_gain",
                "w_out", "ln_ffn_g", "w_up", "conv_w", "conv_b", "w_down", "ln_ple_g", "w_ple_gate", "w_ple_proj")
COL_SHARDED = ("w_in", "w_up", "w_ple_proj")
ROW_SHARDED = ("w_out", "w_down", "w_ple_gate")
SMALL_SHARDED = ("conv_w", "out_gain")
REPLICATED = tuple(n for n in WEIGHT_NAMES if n not in COL_SHARDED + ROW_SHARDED + SMALL_SHARDED)
LOCAL_GRAD_KEY = {"ln_mix_g": "g_mix", "ln_ffn_g": "g_ffn", "ln_ple_g": "g_ple", "c_norm_g": "c_g", "c_norm_b": "c_b",
                  "w_ple_gate": "w_gate", "w_ple_proj": "w_proj"}


def _full_from_gathered(name, gathered):
    _, r, c = gathered.shape
    if name in ROW_SHARDED:
        return gathered.reshape(N_DEV * r, c)
    return jnp.transpose(gathered, (1, 0, 2)).reshape(r, N_DEV * c)


def _slots_from_full(name, full, shard_shape):
    d, r, c = shard_shape
    if name in ROW_SHARDED:
        g = jnp.transpose(full.reshape(d, N_DEV, r, c), (1, 0, 2, 3))
    else:
        g = jnp.transpose(full.reshape(d, r, N_DEV, c), (2, 0, 1, 3))
    return g.reshape(N_DEV, d * r, c)


def _piece_rows(shape):
    return -(-int(np.prod(shape)) // 1024) * 8


def _pack_rows(arrays):
    pieces = []
    for a in arrays:
        n, rows = int(np.prod(a.shape)), _piece_rows(a.shape)
        flat = a.astype(F32).reshape(-1)
        if n != rows * LANES:
            flat = jnp.pad(flat, (0, rows * LANES - n))
        pieces.append(flat.reshape(rows, LANES))
    return jnp.concatenate(pieces, axis=0)


def _unpack_rows(packed, shapes):
    out, off = [], 0
    for shp in shapes:
        n, rows = int(np.prod(shp)), _piece_rows(shp)
        piece = packed[off:off + rows]
        out.append((piece if n == rows * LANES else piece.reshape(-1)[:n]).reshape(shp))
        off += rows
    return out


def kernel(x, p, rel_bias, ln_mix_g, w_in, qk_gain, sink, c_norm_g, c_norm_b, c_ws, c_bs, out_gain, w_out, ln_ffn_g, w_up, conv_w, conv_b, w_down, ln_ple_g, w_ple_gate, w_ple_proj, loss_target, m_rel_bias, m_ln_mix_g, m_w_in, m_qk_gain, m_sink, m_c_norm_g, m_c_norm_b, m_c_ws, m_c_bs, m_out_gain, m_w_out, m_ln_ffn_g, m_w_up, m_conv_w, m_conv_b, m_w_down, m_ln_ple_g, m_w_ple_gate, m_w_ple_proj, v_rel_bias, v_ln_mix_g, v_w_in, v_qk_gain, v_sink, v_c_norm_g, v_c_norm_b, v_c_ws, v_c_bs, v_out_gain, v_w_out, v_ln_ffn_g, v_w_up, v_conv_w, v_conv_b, v_w_down, v_ln_ple_g, v_w_ple_gate, v_w_ple_proj):
    env = dict(locals())
    wt = {n: env[n] for n in WEIGHT_NAMES}
    mom_m = {n: env["m_" + n] for n in WEIGHT_NAMES}
    mom_v = {n: env["v_" + n] for n in WEIGHT_NAMES}
    bl = x.shape[0]
    t = bl * SEQ
    me = 4 * lax.axis_index("x") + 2 * lax.axis_index("y") + lax.axis_index("c")

    big = COL_SHARDED + ROW_SHARDED
    big_layers = [(n, li) for n in big for li in range(DEPTH)]
    gathered = _allgather_hbm([wt[n][li].astype(BF16) for n, li in big_layers], name="gather_weights")
    full = {nl: _full_from_gathered(nl[0], g) for nl, g in zip(big_layers, gathered)}
    small_shapes = [wt[n].shape for n in SMALL_SHARDED]
    small = _allgather_vmem(_pack_rows([wt[n] for n in SMALL_SHARDED]), reduce=False, name="gather_small")
    small = small.reshape(N_DEV, -1)
    off = 0
    for n, shp in zip(SMALL_SHARDED, small_shapes):
        cnt = int(np.prod(shp))
        g = small[:, off:off + cnt].reshape((N_DEV,) + tuple(shp))
        full[n] = jnp.transpose(g, (1, 2, 0, 3)).reshape(shp[0], shp[1], N_DEV * shp[2])
        off += _piece_rows(shp) * LANES

    def head_gain(li, a, b, reps):
        g = jnp.tile(qk_gain[li, a, b], reps)
        return jnp.pad(g, (0, 256 - g.shape[0]))

    wts = []
    for li in range(DEPTH):
        rows = [head_gain(li, 0, 0, 4), head_gain(li, 0, 1, 4), head_gain(li, 1, 0, 4), head_gain(li, 1, 1, 2),
                head_gain(li, 2, 0, 4), head_gain(li, 2, 1, 2), jnp.zeros((256,), F32), jnp.zeros((256,), F32)]
        wts.append(dict(
            g_mix=ln_mix_g[li].reshape(1, -1), w_in=full["w_in", li], qk_gains=jnp.stack(rows),
            sink_t=jnp.broadcast_to(sink[li][:, None, None], (4, 8, 128)),
            c_g=c_norm_g[li].reshape(1, -1), c_b=c_norm_b[li].reshape(1, -1), c_ws=c_ws[li].astype(BF16),
            c_wst=jnp.transpose(c_ws[li], (0, 2, 1)).astype(BF16), c_bst=jnp.repeat(c_bs[li].T, 64, axis=1),
            out_gain=full["out_gain"][li].reshape(1, -1), w_out=full["w_out", li],
            g_ffn=ln_ffn_g[li].reshape(1, -1), w_up=full["w_up", li],
            conv_w=full["conv_w"][li], conv_b=conv_b[li].reshape(1, -1),
            w_down=full["w_down", li], g_ple=ln_ple_g[li].reshape(1, -1),
            w_gate=full["w_ple_gate", li], w_proj=full["w_ple_proj", li]))

    loss_part, dx, grads, d_rel_bias = _local_step(
        x.reshape(t, D_MODEL), p.reshape(DEPTH * t, PLE_DIM), loss_target.reshape(t, D_MODEL), rel_bias, wts)
    loss = lax.psum(loss_part, ("x", "y", "c"))

    def local_grad(n):
        if n == "rel_bias":
            return d_rel_bias
        key = LOCAL_GRAD_KEY.get(n, n)
        return jnp.stack([grads[li][key].reshape(wt[n].shape[1:]) if n in REPLICATED else grads[li][key]
                          for li in range(DEPTH)])

    slots = [_slots_from_full(n, local_grad(n), wt[n].shape) for n in big]
    landed = _all_to_all_hbm(slots, name="exchange_grads")
    out_g, out_d, out_m, out_v = {}, {}, {}, {}
    for n, parts in zip(big, landed):
        shp = wt[n].shape
        two_d = lambda a: a.reshape(-1, shp[-1])
        res = _adamw_reduce(parts, two_d(wt[n]), two_d(mom_m[n]), two_d(mom_v[n]),
                            tr=64 if n == "w_down" else 128, name="adamw_" + n)
        out_g[n], out_d[n], out_m[n], out_v[n] = [r.reshape(shp) for r in res]

    small_names = REPLICATED + SMALL_SHARDED
    small_full_shapes = [wt[n].shape if n in REPLICATED else full[n].shape for n in small_names]
    reduced = _allgather_vmem(_pack_rows([local_grad(n) for n in small_names]), reduce=True, name="allreduce_small")
    reduced = dict(zip(small_names, _unpack_rows(reduced, small_full_shapes)))
    rep_shapes = [wt[n].shape for n in REPLICATED]
    upd = _adamw_plain(_pack_rows([reduced[n] for n in REPLICATED]), _pack_rows([wt[n] for n in REPLICATED]),
                       _pack_rows([mom_m[n] for n in REPLICATED]), _pack_rows([mom_v[n] for n in REPLICATED]),
                       name="adamw_replicated")
    for dst, packed in zip((out_d, out_m, out_v), upd):
        dst.update(zip(REPLICATED, _unpack_rows(packed, rep_shapes)))
    for n in REPLICATED:
        out_g[n] = reduced[n]
    for n in SMALL_SHARDED:
        shp = wt[n].shape
        g = reduced[n].reshape(shp[0], shp[1], N_DEV, shp[2])
        g = lax.dynamic_index_in_dim(g, me, axis=2, keepdims=False)
        two_d = lambda a: a.reshape(-1, shp[-1])
        res = _adamw_plain(two_d(g), two_d(wt[n]), two_d(mom_m[n]), two_d(mom_v[n]), name="adamw_" + n)
        out_g[n] = g
        out_d[n], out_m[n], out_v[n] = [r.reshape(shp) for r in res]

    return (loss, dx.reshape(bl, SEQ, D_MODEL), *[out_g[n] for n in WEIGHT_NAMES], *[out_d[n] for n in WEIGHT_NAMES],
            *[out_m[n] for n in WEIGHT_NAMES], *[out_v[n] for n in WEIGHT_NAMES])
```

```python
import math

import jax
import jax.numpy as jnp
import numpy as np
from jax import lax
from jax.experimental import pallas as pl
from jax.experimental.pallas import tpu as pltpu

F32 = jnp.float32
BF16 = jnp.bfloat16
HI = lax.Precision.HIGHEST

N_DEV = 8
D_MODEL = 1024
SEQ = 2048
DEPTH = 2
HEAD_DIM = 64
IN_WIDTH = 2304
D_FF = 2816
PLE_DIM = 256
C_CHUNK = 128
C_GROUPS = 4
DILATED_CFGS = ((128, 1), (512, 4), (2048, 16))
DILATIONS = tuple(d for _, d in DILATED_CFGS)
A_RADIUS = 64
SWA_RADIUS = 128
BAND_BLOCK = 256
GRID_W = 64
ROPE_THETA = 10000.0
REL_BUCKETS = 32
REL_MAX_DIST = 1024
EPS = 1e-6
NEG_INF = -1e30
ATTN_SCALE = HEAD_DIM ** -0.5
LANES = 128

ADAM_LR = 0.001
ADAM_B1 = 0.9
ADAM_B2 = 0.999
ADAM_EPS = 1e-08
ADAM_WD = 0.01
ADAM_STEP = 10

MESH = pl.DeviceIdType.MESH
NT = (((1,), (1,)), ((), ()))
TN = (((0,), (0,)), ((), ()))
ARB = "arbitrary"
PAR = "parallel"


def _cparams(*sem):
    return pltpu.CompilerParams(dimension_semantics=tuple(sem))


def _sds(shape, dtype):
    return jax.ShapeDtypeStruct(tuple(shape), dtype)


def _group_sum_matrix(n, same_group):
    r = lax.broadcasted_iota(jnp.int32, (n, n), 0)
    c = lax.broadcasted_iota(jnp.int32, (n, n), 1)
    if same_group:
        return ((r >> 6) == (c >> 6)).astype(F32)
    return ((r & 63) == (c & 63)).astype(F32)


def _seg_sum(x, e):
    return jnp.dot(x, e, precision=HI, preferred_element_type=F32)


def _gelu(x):
    c = math.sqrt(2.0 / math.pi)
    return 0.5 * x * (1.0 + jnp.tanh(c * (x + 0.044715 * (x * x * x))))


def _gelu_grad(x):
    c = math.sqrt(2.0 / math.pi)
    t = jnp.tanh(c * (x + 0.044715 * (x * x * x)))
    return 0.5 * (1.0 + t) + 0.5 * x * (1.0 - t * t) * c * (1.0 + 3.0 * 0.044715 * (x * x))


def _sigmoid(x):
    return 1.0 / (1.0 + jnp.exp(-x))


def _scatter_cols(scratch, first, val):
    for c in range(val.shape[1] // LANES):
        scratch[first + c] = val[:, c * LANES:(c + 1) * LANES]


def _gather_cols(scratch, first, ncol):
    return jnp.concatenate([scratch[first + c] for c in range(ncol)], axis=1)


def _read_residue(scratch, first, ncol, r, d):
    n = scratch.shape[1] // d
    return jnp.concatenate([scratch.at[first + c][pl.ds(r, n, stride=d), :] for c in range(ncol)], axis=1)


def _write_residue(scratch, first, r, d, val):
    n = scratch.shape[1] // d
    for c in range(val.shape[1] // LANES):
        scratch.at[first + c][pl.ds(r, n, stride=d), :] = val[:, c * LANES:(c + 1) * LANES]


def _norm_mm(xs, gain, w, res, *, tm, tn, name, out_dtype=F32):
    t = xs[0].shape[0]
    k = sum(x.shape[1] for x in xs)
    n = w.shape[1]
    ng = len(xs)
    has_res = res is not None

    def body(*refs):
        x_refs = refs[:ng]
        g_ref, w_ref = refs[ng], refs[ng + 1]
        res_ref = refs[ng + 2] if has_res else None
        hn_ref, o_ref, hn_s = refs[ng + 2 + has_res:]

        @pl.when(pl.program_id(1) == 0)
        def _():
            off = 0
            for xr in x_refs:
                x = xr[...]
                wd = x.shape[1]
                r = lax.rsqrt(jnp.mean(x * x, axis=-1, keepdims=True) + EPS)
                hn_s[:, off:off + wd] = (x * r * g_ref[:, off:off + wd]).astype(BF16)
                off += wd
            hn_ref[...] = hn_s[...]

        acc = jnp.dot(hn_s[...], w_ref[...], preferred_element_type=F32)
        if has_res:
            acc = acc + res_ref[...]
        o_ref[...] = acc.astype(out_dtype)

    in_specs = [pl.BlockSpec((tm, x.shape[1]), lambda i, j: (i, 0)) for x in xs]
    in_specs += [pl.BlockSpec((1, k), lambda i, j: (0, 0)), pl.BlockSpec((k, tn), lambda i, j: (0, j))]
    args = list(xs) + [gain, w]
    if has_res:
        in_specs.append(pl.BlockSpec((tm, tn), lambda i, j: (i, j)))
        args.append(res)
    return pl.pallas_call(
        body, name=name, grid=(t // tm, n // tn), in_specs=in_specs,
        out_specs=[pl.BlockSpec((tm, k), lambda i, j: (i, 0)), pl.BlockSpec((tm, tn), lambda i, j: (i, j))],
        out_shape=[_sds((t, k), BF16), _sds((t, n), out_dtype)],
        scratch_shapes=[pltpu.VMEM((tm, k), BF16)],
        compiler_params=_cparams(PAR, ARB),
    )(*args)


def _mm(a, b, mode, res, *, tm, tn, out_dtype, name, a_rows=None):
    if mode == "tn":
        kk, m = a.shape
        blk_a = 0
        if a_rows is not None:
            blk_a, kk = a_rows
        a_spec = pl.BlockSpec((kk, tm), lambda i, j: (blk_a, i))
    else:
        m, kk = a.shape
        a_spec = pl.BlockSpec((tm, kk), lambda i, j: (i, 0))
    if mode == "nt":
        n = b.shape[0]
        b_spec = pl.BlockSpec((tn, kk), lambda i, j: (j, 0))
    else:
        n = b.shape[1]
        b_spec = pl.BlockSpec((kk, tn), lambda i, j: (0, j))
    has_res = res is not None

    def body(*refs):
        a_ref, b_ref = refs[0], refs[1]
        o_ref = refs[-1]
        av = a_ref[...].astype(BF16)
        bv = b_ref[...].astype(BF16)
        if mode == "nn":
            acc = jnp.dot(av, bv, preferred_element_type=F32)
        elif mode == "nt":
            acc = lax.dot_general(av, bv, NT, preferred_element_type=F32)
        else:
            acc = lax.dot_general(av, bv, TN, preferred_element_type=F32)
        if has_res:
            acc = acc + refs[2][...]
        o_ref[...] = acc.astype(out_dtype)

    in_specs = [a_spec, b_spec]
    args = [a, b]
    if has_res:
        in_specs.append(pl.BlockSpec((tm, tn), lambda i, j: (i, j)))
        args.append(res)
    return pl.pallas_call(
        body, name=name, grid=(m // tm, n // tn), in_specs=in_specs,
        out_specs=pl.BlockSpec((tm, tn), lambda i, j: (i, j)),
        out_shape=_sds((m, n), out_dtype),
        compiler_params=_cparams(PAR, PAR),
    )(*args)


def _mm_bt_normbwd(dys, w, xs, gain, dres, *, tm, tn, name, emit_bf16=False):
    t, wd_each = dys[0].shape
    nd = len(dys)
    per = wd_each // tn
    nj = nd * per
    k = w.shape[0]
    ng = len(xs)
    has_res = dres is not None

    def body(*refs):
        dy_refs = refs[:nd]
        w_ref = refs[nd]
        x_refs = refs[nd + 1:nd + 1 + ng]
        g_ref = refs[nd + 1 + ng]
        dres_ref = refs[nd + 2 + ng] if has_res else None
        outs = refs[nd + 2 + ng + has_res:]
        dx_ref = outs[0]
        dxb_ref = outs[1] if emit_bf16 else None
        dg_ref, acc = outs[1 + emit_bf16:]
        i, j = pl.program_id(0), pl.program_id(1)

        @pl.when(j == 0)
        def _():
            acc[...] = jnp.zeros_like(acc)

        for d, dy_ref in enumerate(dy_refs):
            @pl.when((j >= d * per) & (j < (d + 1) * per))
            def _(dy_ref=dy_ref):
                acc[...] += lax.dot_general(dy_ref[...].astype(BF16), w_ref[...], NT, preferred_element_type=F32)

        @pl.when(j == nj - 1)
        def _():
            @pl.when(i == 0)
            def _():
                dg_ref[...] = jnp.zeros_like(dg_ref)

            off = 0
            for xr in x_refs:
                x = xr[...]
                wd = x.shape[1]
                g = g_ref[:, off:off + wd]
                dyn = acc[:, off:off + wd]
                r = lax.rsqrt(jnp.mean(x * x, axis=-1, keepdims=True) + EPS)
                gdy = dyn * g
                dx = r * gdy - x * (r * r * r * jnp.mean(gdy * x, axis=-1, keepdims=True))
                if has_res:
                    dx = dx + dres_ref[:, off:off + wd]
                dx_ref[:, off:off + wd] = dx
                if emit_bf16:
                    dxb_ref[:, off:off + wd] = dx.astype(BF16)
                dg_ref[:, off:off + wd] += jnp.sum(dyn * x * r, axis=0, keepdims=True)
                off += wd

    def dy_map(d):
        return lambda i, j: (i, jnp.clip(j - d * per, 0, per - 1))

    in_specs = [pl.BlockSpec((tm, tn), dy_map(d)) for d in range(nd)]
    in_specs.append(pl.BlockSpec((k, tn), lambda i, j: (0, j)))
    in_specs += [pl.BlockSpec((tm, x.shape[1]), lambda i, j: (i, 0)) for x in xs]
    in_specs.append(pl.BlockSpec((1, k), lambda i, j: (0, 0)))
    args = list(dys) + [w] + list(xs) + [gain]
    if has_res:
        in_specs.append(pl.BlockSpec((tm, k), lambda i, j: (i, 0)))
        args.append(dres)
    row = pl.BlockSpec((tm, k), lambda i, j: (i, 0))
    out_specs = [row] + ([row] if emit_bf16 else []) + [pl.BlockSpec((1, k), lambda i, j: (0, 0))]
    out_shape = [_sds((t, k), F32)] + ([_sds((t, k), BF16)] if emit_bf16 else []) + [_sds((1, k), F32)]
    return pl.pallas_call(
        body, name=name, grid=(t // tm, nj), in_specs=in_specs, out_specs=out_specs, out_shape=out_shape,
        scratch_shapes=[pltpu.VMEM((tm, k), F32)],
        compiler_params=_cparams(ARB, ARB),
    )(*args)


def _rope_partner(y):
    n = y.shape[1]
    lane = lax.broadcasted_iota(jnp.int32, y.shape, 1)
    return jnp.where((lane & 31) < 16, pltpu.roll(y, n - 16, 1), pltpu.roll(y, 16, 1))


def _residue_specs(tm, width, nt):
    specs = [pl.BlockSpec((tm, width), lambda b, i: (b * nt + i, 0))]
    for d in DILATIONS[1:]:
        specs.append(pl.BlockSpec((None, d, tm // d, width), lambda b, i: (b, 0, i, 0)))
    return specs


def _residue_shapes(bl, width, dtype):
    return [_sds((bl * SEQ, width), dtype)] + [_sds((bl, d, SEQ // d, width), dtype) for d in DILATIONS[1:]]


def _qkprep_fwd(proj, gains, cos, sins, *, tm, name):
    t = proj.shape[0]
    bl = t // SEQ
    nt = SEQ // tm

    def body(p_ref, g_ref, c_ref, s_ref, qa1_ref, qa4_ref, qa16_ref, qb_ref, qd_ref, scr):
        e = _group_sum_matrix(256, True)

        def hn(x, row):
            wd = x.shape[1]
            ms = _seg_sum(x * x, e[:wd, :wd]) * (1.0 / HEAD_DIM)
            return x * lax.rsqrt(ms + EPS) * g_ref[row:row + 1, :wd]

        qa = jnp.concatenate([hn(p_ref[:, 0:256], 0) * ATTN_SCALE, hn(p_ref[:, 256:512], 1), p_ref[:, 512:768]], axis=1)
        qa1_ref[...] = qa.astype(BF16)
        _scatter_cols(scr, 0, qa)
        for d, ref in ((4, qa4_ref), (16, qa16_ref)):
            for r in range(d):
                ref[r] = _read_residue(scr, 0, 6, r, d).astype(BF16)
        qb_ref[:, 0:256] = (hn(p_ref[:, 768:1024], 2) * ATTN_SCALE).astype(BF16)
        qb_ref[:, 256:384] = hn(p_ref[:, 1024:1152], 3).astype(BF16)
        qb_ref[:, 384:512] = p_ref[:, 1152:1280].astype(BF16)
        yq = hn(p_ref[:, 1792:2048], 4)
        yq = yq * c_ref[...] + _rope_partner(yq) * s_ref[...]
        qd_ref[:, 0:256] = (yq * ATTN_SCALE).astype(BF16)
        yk = hn(p_ref[:, 2048:2176], 5)
        yk = yk * c_ref[:, 0:128] + _rope_partner(yk) * s_ref[:, 0:128]
        qd_ref[:, 256:384] = yk.astype(BF16)
        qd_ref[:, 384:512] = p_ref[:, 2176:2304].astype(BF16)

    row = lambda width: pl.BlockSpec((tm, width), lambda b, i: (b * nt + i, 0))
    tab = pl.BlockSpec((tm, 256), lambda b, i: (i, 0))
    return pl.pallas_call(
        body, name=name, grid=(bl, nt),
        in_specs=[row(IN_WIDTH), pl.BlockSpec((8, 256), lambda b, i: (0, 0)), tab, tab],
        out_specs=_residue_specs(tm, 768, nt) + [row(512), row(512)],
        out_shape=_residue_shapes(bl, 768, BF16) + [_sds((t, 512), BF16), _sds((t, 512), BF16)],
        scratch_shapes=[pltpu.VMEM((6, tm, LANES), F32)],
        compiler_params=_cparams(PAR, PAR),
    )(proj, gains, cos, sins)


def _qkprep_bwd(proj, da, db, dd, dcu, dcv, gains, cos, sins, *, tm, name):
    t = proj.shape[0]
    bl = t // SEQ
    nt = SEQ // tm
    flat = [a for cfg in da for a in cfg] + list(db) + list(dd) + [dcu, dcv]

    def body(*refs):
        p_ref, g_ref, c_ref, s_ref = refs[:4]
        d_refs = refs[4:4 + len(flat)]
        dp_ref, dg_ref, scr = refs[4 + len(flat):]
        a_refs = d_refs[:9]
        dqb_ref, dkb_ref, dvb_ref, dqd_ref, dkd_ref, dvd_ref, dcu_ref, dcv_ref = d_refs[9:]
        e = _group_sum_matrix(256, True)
        first = (pl.program_id(0) == 0) & (pl.program_id(1) == 0)
        last = (pl.program_id(0) == bl - 1) & (pl.program_id(1) == nt - 1)

        @pl.when(first)
        def _():
            dg_ref[...] = jnp.zeros_like(dg_ref)

        def hn_bwd(x, dy, row):
            wd = x.shape[1]
            ee = e[:wd, :wd]
            g = g_ref[row:row + 1, :wd]
            r = lax.rsqrt(_seg_sum(x * x, ee) * (1.0 / HEAD_DIM) + EPS)
            gdy = dy * g
            dx = r * gdy - x * (r * r * r * (_seg_sum(gdy * x, ee) * (1.0 / HEAD_DIM)))
            dg_ref[row:row + 1, :wd] += jnp.sum(dy * x * r, axis=0, keepdims=True)
            return dx

        def rope_bwd(dy, wd):
            return dy * c_ref[:, :wd] + _rope_partner(dy * s_ref[:, :wd])

        dqkv = jnp.concatenate([a_refs[0][...], a_refs[1][...], a_refs[2][...]], axis=1)
        for ci, d in ((1, 4), (2, 16)):
            for r in range(d):
                part = jnp.concatenate([a_refs[3 * ci + m][r] for m in range(3)], axis=1)
                _write_residue(scr, 0, r, d, part)
            dqkv = dqkv + _gather_cols(scr, 0, 6)
        dp_ref[:, 0:256] = hn_bwd(p_ref[:, 0:256], dqkv[:, 0:256] * ATTN_SCALE, 0).astype(BF16)
        dp_ref[:, 256:512] = hn_bwd(p_ref[:, 256:512], dqkv[:, 256:512], 1).astype(BF16)
        dp_ref[:, 512:768] = dqkv[:, 512:768].astype(BF16)
        dp_ref[:, 768:1024] = hn_bwd(p_ref[:, 768:1024], dqb_ref[...] * ATTN_SCALE, 2).astype(BF16)
        dp_ref[:, 1024:1152] = hn_bwd(p_ref[:, 1024:1152], dkb_ref[...], 3).astype(BF16)
        dp_ref[:, 1152:1280] = dvb_ref[...].astype(BF16)
        dp_ref[:, 1280:1536] = dcu_ref[...].astype(BF16)
        dp_ref[:, 1536:1792] = dcv_ref[...].astype(BF16)
        dp_ref[:, 1792:2048] = hn_bwd(p_ref[:, 1792:2048], rope_bwd(dqd_ref[...] * ATTN_SCALE, 256), 4).astype(BF16)
        dp_ref[:, 2048:2176] = hn_bwd(p_ref[:, 2048:2176], rope_bwd(dkd_ref[...], 128), 5).astype(BF16)
        dp_ref[:, 2176:2304] = dvd_ref[...].astype(BF16)

        @pl.when(last)
        def _():
            dg_ref[...] = _seg_sum(dg_ref[...], _group_sum_matrix(256, False))

    row = lambda width: pl.BlockSpec((tm, width), lambda b, i: (b * nt + i, 0))
    tab = pl.BlockSpec((tm, 256), lambda b, i: (i, 0))
    in_specs = [row(IN_WIDTH), pl.BlockSpec((8, 256), lambda b, i: (0, 0)), tab, tab]
    res_specs = _residue_specs(tm, 256, nt)
    in_specs += [res_specs[ci] for ci in range(3) for _ in range(3)]
    in_specs += [row(a.shape[1]) for a in flat[9:]]
    return pl.pallas_call(
        body, name=name, grid=(bl, nt), in_specs=in_specs,
        out_specs=[row(IN_WIDTH), pl.BlockSpec((8, 256), lambda b, i: (0, 0))],
        out_shape=[_sds((t, IN_WIDTH), BF16), _sds((8, 256), F32)],
        scratch_shapes=[pltpu.VMEM((6, tm, LANES), F32)],
        compiler_params=_cparams(ARB, ARB),
    )(proj, gains, cos, sins, *flat)


def _band_spec(seq_len, spec):
    width, idx = spec
    return pl.BlockSpec((None, None, seq_len, width), lambda b, r: (b, r, 0, idx))


def _fill_padded(dst, src_ref, rad, seq_len):
    z = jnp.zeros((rad, dst.shape[1]), dst.dtype)
    dst[0:rad, :] = z
    dst[rad + seq_len:rad + seq_len + rad, :] = z
    dst[rad:rad + seq_len, :] = src_ref[...]


def _band_fwd(src, qs, ks, vs, bias, sink, *, rad, nh, nkv, name):
    bl, dil, sl, _ = src.shape
    blk = bias.shape[1]
    kw = blk + 2 * rad
    nb = sl // blk
    rep = nh // nkv
    has_sink = sink is not None

    def body(*refs):
        q_ref, k_ref, v_ref, b_ref = refs[:4]
        s_ref = refs[4] if has_sink else None
        o_ref, l_ref, kp, vp = refs[4 + has_sink:]
        _fill_padded(kp, k_ref, rad, sl)
        _fill_padded(vp, v_ref, rad, sl)

        def blk_body(i, carry):
            r0 = pl.multiple_of(i * blk, blk)
            qb = q_ref[pl.ds(r0, blk), :]
            kwin = kp[pl.ds(r0, kw), :]
            vwin = vp[pl.ds(r0, kw), :]
            col = r0 - rad + lax.broadcasted_iota(jnp.int32, (blk, kw), 1)
            neg = jnp.where((col >= 0) & (col < sl), 0.0, NEG_INF).astype(F32)
            for h in range(nh):
                g = h // rep
                hs = slice(h * HEAD_DIM, (h + 1) * HEAD_DIM)
                gs = slice(g * HEAD_DIM, (g + 1) * HEAD_DIM)
                s = lax.dot_general(qb[:, hs], kwin[:, gs], NT, preferred_element_type=F32)
                s = s + b_ref[h] + neg
                m = jnp.max(s, axis=1, keepdims=True)
                if has_sink:
                    sk = s_ref[h][0:1, 0:1]
                    m = jnp.maximum(m, sk)
                p = jnp.exp(s - m)
                den = jnp.sum(p, axis=1, keepdims=True)
                if has_sink:
                    den = den + jnp.exp(sk - m)
                o = jnp.dot(p.astype(BF16), vwin[:, gs], preferred_element_type=F32) / den
                o_ref[pl.ds(r0, blk), hs] = o
                l_ref[pl.ds(r0, blk), hs] = jnp.broadcast_to(m + jnp.log(den), (blk, HEAD_DIM))
            return carry

        lax.fori_loop(0, nb, blk_body, 0)

    in_specs = [_band_spec(sl, qs), _band_spec(sl, ks), _band_spec(sl, vs),
                pl.BlockSpec((nh, blk, kw), lambda b, r: (0, 0, 0))]
    args = [src] * 3 + [bias]
    if has_sink:
        in_specs.append(pl.BlockSpec((nh, 8, 128), lambda b, r: (0, 0, 0)))
        args.append(sink)
    return pl.pallas_call(
        body, name=name, grid=(bl, dil), in_specs=in_specs,
        out_specs=[_band_spec(sl, (256, 0))] * 2,
        out_shape=[_sds((bl, dil, sl, 256), F32)] * 2,
        scratch_shapes=[pltpu.VMEM((sl + 2 * rad, ks[0]), BF16), pltpu.VMEM((sl + 2 * rad, vs[0]), BF16)],
        compiler_params=_cparams(PAR, PAR),
    )(*args)


def _band_bwd(src, qs, ks, vs, bias, sink, dy, dcol, lse, delta, *, rad, nh, nkv, name):
    bl, dil, sl, _ = src.shape
    blk = bias.shape[1]
    kw = blk + 2 * rad
    nb = sl // blk
    rep = nh // nkv
    has_sink = sink is not None
    wk, wv = ks[0], vs[0]

    def body(*refs):
        q_ref, k_ref, v_ref, b_ref = refs[:4]
        s_ref = refs[4] if has_sink else None
        do_ref, l_ref, dl_ref = refs[4 + has_sink:7 + has_sink]
        outs = refs[7 + has_sink:]
        if has_sink:
            dq_ref, dk_ref, dv_ref, db_ref, dsk_ref, kp, vp, dka, dva = outs
        else:
            dq_ref, dk_ref, dv_ref, db_ref, kp, vp, dka, dva = outs

        @pl.when((pl.program_id(0) == 0) & (pl.program_id(1) == 0))
        def _():
            db_ref[...] = jnp.zeros_like(db_ref)
            if has_sink:
                dsk_ref[...] = jnp.zeros_like(dsk_ref)

        _fill_padded(kp, k_ref, rad, sl)
        _fill_padded(vp, v_ref, rad, sl)
        dka[...] = jnp.zeros_like(dka)
        dva[...] = jnp.zeros_like(dva)

        def blk_body(i, carry):
            r0 = pl.multiple_of(i * blk, blk)
            qb = q_ref[pl.ds(r0, blk), :]
            kwin = kp[pl.ds(r0, kw), :]
            vwin = vp[pl.ds(r0, kw), :]
            dob = do_ref[pl.ds(r0, blk), :].astype(BF16)
            lb = l_ref[pl.ds(r0, blk), :]
            dlb = dl_ref[pl.ds(r0, blk), :]
            col = r0 - rad + lax.broadcasted_iota(jnp.int32, (blk, kw), 1)
            neg = jnp.where((col >= 0) & (col < sl), 0.0, NEG_INF).astype(F32)
            for h in range(nh):
                g = h // rep
                hs = slice(h * HEAD_DIM, (h + 1) * HEAD_DIM)
                gs = slice(g * HEAD_DIM, (g + 1) * HEAD_DIM)
                qh, kh, vh, doh = qb[:, hs], kwin[:, gs], vwin[:, gs], dob[:, hs]
                lh = lb[:, h * HEAD_DIM:h * HEAD_DIM + 1]
                dlh = dlb[:, h * HEAD_DIM:h * HEAD_DIM + 1]
                s = lax.dot_general(qh, kh, NT, preferred_element_type=F32) + b_ref[h] + neg
                p = jnp.exp(s - lh)
                dp = lax.dot_general(doh, vh, NT, preferred_element_type=F32)
                ds = p * (dp - dlh)
                dsb = ds.astype(BF16)
                dq_ref[pl.ds(r0, blk), hs] = jnp.dot(dsb, kh, preferred_element_type=F32)
                dka[pl.ds(r0, kw), gs] += lax.dot_general(dsb, qh, TN, preferred_element_type=F32)
                dva[pl.ds(r0, kw), gs] += lax.dot_general(p.astype(BF16), doh, TN, preferred_element_type=F32)
                db_ref[h] += ds
                if has_sink:
                    ps = jnp.exp(s_ref[h][0:1, 0:1] - lh)
                    dsk_ref[h] += jnp.broadcast_to(-jnp.sum(ps * dlh, axis=0, keepdims=True), (8, 128))
            return carry

        lax.fori_loop(0, nb, blk_body, 0)
        dk_ref[...] = dka[rad:rad + sl, :]
        dv_ref[...] = dva[rad:rad + sl, :]

    const3 = lambda b, r: (0, 0, 0)
    in_specs = [_band_spec(sl, qs), _band_spec(sl, ks), _band_spec(sl, vs), pl.BlockSpec((nh, blk, kw), const3)]
    args = [src] * 3 + [bias]
    if has_sink:
        in_specs.append(pl.BlockSpec((nh, 8, 128), const3))
        args.append(sink)
    row = _band_spec(sl, (256, 0))
    in_specs += [_band_spec(sl, (256, dcol)), row, row]
    args += [dy, lse, delta]
    out_specs = [row, _band_spec(sl, (wk, 0)), _band_spec(sl, (wv, 0)), pl.BlockSpec((nh, blk, kw), const3)]
    out_shape = [_sds((bl, dil, sl, 256), F32), _sds((bl, dil, sl, wk), F32), _sds((bl, dil, sl, wv), F32),
                 _sds((nh, blk, kw), F32)]
    if has_sink:
        out_specs.append(pl.BlockSpec((nh, 8, 128), const3))
        out_shape.append(_sds((nh, 8, 128), F32))
    return pl.pallas_call(
        body, name=name, grid=(bl, dil), in_specs=in_specs, out_specs=out_specs, out_shape=out_shape,
        scratch_shapes=[pltpu.VMEM((sl + 2 * rad, wk), BF16), pltpu.VMEM((sl + 2 * rad, wv), BF16),
                        pltpu.VMEM((sl + 2 * rad, wk), F32), pltpu.VMEM((sl + 2 * rad, wv), F32)],
        compiler_params=_cparams(ARB, ARB),
    )(*args)


def _combine_a(os_, ls_, *, tm, name):
    bl = os_[1].shape[0]
    t = bl * SEQ
    nt = SEQ // tm

    def body(o1, o4, o16, l1, l4, l16, y_ref, lt_ref, scr):
        for k, (d, ref) in enumerate(((4, o4), (16, o16), (4, l4), (16, l16))):
            for r in range(d):
                _write_residue(scr, 2 * k, r, d, ref[r])
        o2, o3, b, c = (_gather_cols(scr, 2 * k, 2) for k in range(4))
        a = l1[...]
        m = jnp.maximum(jnp.maximum(a, b), c)
        ea, eb, ec = jnp.exp(a - m), jnp.exp(b - m), jnp.exp(c - m)
        den = ea + eb + ec
        y_ref[...] = (ea / den) * o1[...] + (eb / den) * o2 + (ec / den) * o3
        lt_ref[...] = m + jnp.log(den)

    specs = _residue_specs(tm, 256, nt)
    return pl.pallas_call(
        body, name=name, grid=(bl, nt), in_specs=specs * 2, out_specs=[specs[0]] * 2,
        out_shape=[_sds((t, 256), F32)] * 2, scratch_shapes=[pltpu.VMEM((8, tm, LANES), F32)],
        compiler_params=_cparams(PAR, PAR),
    )(*os_, *ls_)


def _deltas(dycat, ya, yb, yd, lse_a, *, tm, name):
    t = ya.shape[0]
    bl = t // SEQ
    nt = SEQ // tm

    def body(dy_ref, ya_ref, yb_ref, yd_ref, la_ref, dy4, dy16, l4, l16, da1, da4, da16, db_ref, dd_ref, scr):
        e = _group_sum_matrix(256, True)
        dya = dy_ref[:, 0:256]
        dla = _seg_sum(dya * ya_ref[...], e)
        da1[...] = dla
        db_ref[...] = _seg_sum(dy_ref[:, 256:512] * yb_ref[...], e)
        dd_ref[...] = _seg_sum(dy_ref[:, 768:1024] * yd_ref[...], e)
        for k, (val, r4, r16) in enumerate(((dya, dy4, dy16), (la_ref[...], l4, l16), (dla, da4, da16))):
            _scatter_cols(scr, 2 * k, val)
            for d, ref in ((4, r4), (16, r16)):
                for r in range(d):
                    ref[r] = _read_residue(scr, 2 * k, 2, r, d)

    specs = _residue_specs(tm, 256, nt)
    nat = specs[0]
    shapes = _residue_shapes(bl, 256, F32)
    outs = pl.pallas_call(
        body, name=name, grid=(bl, nt),
        in_specs=[pl.BlockSpec((tm, 1024), lambda b, i: (b * nt + i, 0)), nat, nat, nat, nat],
        out_specs=specs[1:] + specs[1:] + specs + [nat, nat],
        out_shape=shapes[1:] + shapes[1:] + shapes + [shapes[0], shapes[0]],
        scratch_shapes=[pltpu.VMEM((6, tm, LANES), F32)],
        compiler_params=_cparams(PAR, PAR),
    )(dycat, ya, yb, yd, lse_a)
    return outs[0:2], outs[2:4], outs[4:7], outs[7], outs[8]


def _dense_fwd(qd, *, tq, name):
    t = qd.shape[0]
    bl = t // SEQ
    nq = SEQ // tq

    def body(q_ref, k_ref, v_ref, o_ref, l_ref):
        q = q_ref[...]
        for g in range(2):
            h0, h1 = 2 * g, 2 * g + 1
            q2 = jnp.concatenate([q[:, h0 * 64:(h0 + 1) * 64], q[:, h1 * 64:(h1 + 1) * 64]], axis=0)
            kg = k_ref[:, g * 64:(g + 1) * 64]
            vg = v_ref[:, g * 64:(g + 1) * 64]
            s = lax.dot_general(q2, kg, NT, preferred_element_type=F32)
            m = jnp.max(s, axis=1, keepdims=True)
            p = jnp.exp(s - m)
            den = jnp.sum(p, axis=1, keepdims=True)
            o2 = jnp.dot(p.astype(BF16), vg, preferred_element_type=F32) / den
            l2 = jnp.broadcast_to(m + jnp.log(den), (2 * tq, 64))
            o_ref[:, h0 * 64:(h0 + 1) * 64] = o2[:tq]
            o_ref[:, h1 * 64:(h1 + 1) * 64] = o2[tq:]
            l_ref[:, h0 * 64:(h0 + 1) * 64] = l2[:tq]
            l_ref[:, h1 * 64:(h1 + 1) * 64] = l2[tq:]

    q3 = qd.reshape(bl, SEQ, 512)
    o, lse = pl.pallas_call(
        body, name=name, grid=(bl, nq),
        in_specs=[pl.BlockSpec((None, tq, 256), lambda b, i: (b, i, 0)),
                  pl.BlockSpec((None, SEQ, 128), lambda b, i: (b, 0, 2)),
                  pl.BlockSpec((None, SEQ, 128), lambda b, i: (b, 0, 3))],
        out_specs=[pl.BlockSpec((None, tq, 256), lambda b, i: (b, i, 0))] * 2,
        out_shape=[_sds((bl, SEQ, 256), F32)] * 2,
        compiler_params=_cparams(PAR, PAR),
    )(q3, q3, q3)
    return o.reshape(t, 256), lse.reshape(t, 256)


def _dense_bwd(qd, dycat, lse, delta, *, tq, name):
    t = qd.shape[0]
    bl = t // SEQ
    nq = SEQ // tq

    def body(q_ref, k_ref, v_ref, do_ref, l_ref, dl_ref, dq_ref, dk_ref, dv_ref):
        @pl.when(pl.program_id(1) == 0)
        def _():
            dk_ref[...] = jnp.zeros_like(dk_ref)
            dv_ref[...] = jnp.zeros_like(dv_ref)

        q = q_ref[...]
        do = do_ref[...].astype(BF16)
        lv = l_ref[...]
        dlv = dl_ref[...]
        for g in range(2):
            h0, h1 = 2 * g, 2 * g + 1
            q2 = jnp.concatenate([q[:, h0 * 64:(h0 + 1) * 64], q[:, h1 * 64:(h1 + 1) * 64]], axis=0)
            do2 = jnp.concatenate([do[:, h0 * 64:(h0 + 1) * 64], do[:, h1 * 64:(h1 + 1) * 64]], axis=0)
            l2 = jnp.concatenate([lv[:, h0 * 64:h0 * 64 + 1], lv[:, h1 * 64:h1 * 64 + 1]], axis=0)
            dl2 = jnp.concatenate([dlv[:, h0 * 64:h0 * 64 + 1], dlv[:, h1 * 64:h1 * 64 + 1]], axis=0)
            kg = k_ref[:, g * 64:(g + 1) * 64]
            vg = v_ref[:, g * 64:(g + 1) * 64]
            s = lax.dot_general(q2, kg, NT, preferred_element_type=F32)
            p = jnp.exp(s - l2)
            dp = lax.dot_general(do2, vg, NT, preferred_element_type=F32)
            ds = (p * (dp - dl2)).astype(BF16)
            dq2 = jnp.dot(ds, kg, preferred_element_type=F32)
            dq_ref[:, h0 * 64:(h0 + 1) * 64] = dq2[:tq]
            dq_ref[:, h1 * 64:(h1 + 1) * 64] = dq2[tq:]
            dk_ref[:, g * 64:(g + 1) * 64] += lax.dot_general(ds, q2, TN, preferred_element_type=F32)
            dv_ref[:, g * 64:(g + 1) * 64] += lax.dot_general(p.astype(BF16), do2, TN, preferred_element_type=F32)

    q3 = qd.reshape(bl, SEQ, 512)
    tile = pl.BlockSpec((None, tq, 256), lambda b, i: (b, i, 0))
    full = pl.BlockSpec((None, SEQ, 128), lambda b, i: (b, 0, 0))
    dq, dk, dv = pl.pallas_call(
        body, name=name, grid=(bl, nq),
        in_specs=[tile, pl.BlockSpec((None, SEQ, 128), lambda b, i: (b, 0, 2)),
                  pl.BlockSpec((None, SEQ, 128), lambda b, i: (b, 0, 3)),
                  pl.BlockSpec((None, tq, 256), lambda b, i: (b, i, 3)), tile, tile],
        out_specs=[tile, full, full],
        out_shape=[_sds((bl, SEQ, 256), F32), _sds((bl, SEQ, 128), F32), _sds((bl, SEQ, 128), F32)],
        compiler_params=_cparams(PAR, ARB),
    )(q3, q3, q3, dycat.reshape(bl, SEQ, 1024), lse.reshape(bl, SEQ, 256), delta.reshape(bl, SEQ, 256))
    return dq.reshape(t, 256), dk.reshape(t, 128), dv.reshape(t, 128)


def _c_norm(cv, gam, bet):
    vg = _gelu(cv)
    mu = jnp.mean(vg, axis=-1, keepdims=True)
    xc = vg - mu
    r = lax.rsqrt(jnp.mean(xc * xc, axis=-1, keepdims=True) + EPS)
    xhat = xc * r
    return xhat * gam + bet, xhat, r


def _c_fwd(proj, gam, bet, ws, bst, *, tm, name):
    t = proj.shape[0]
    nch = tm // C_CHUNK

    def body(u_ref, v_ref, g_ref, b_ref, ws_ref, bs_ref, y_ref):
        vn, _, _ = _c_norm(v_ref[...], g_ref[...], b_ref[...])
        vnb = vn.astype(BF16)
        for c in range(nch):
            rows = slice(c * C_CHUNK, (c + 1) * C_CHUNK)
            for g in range(C_GROUPS):
                gs = slice(g * 64, (g + 1) * 64)
                mixed = jnp.dot(ws_ref[g], vnb[rows, gs], preferred_element_type=F32) + bs_ref[:, gs]
                y_ref[rows, gs] = _gelu(u_ref[rows, gs]) * mixed

    vec = pl.BlockSpec((1, 256), lambda i: (0, 0))
    return pl.pallas_call(
        body, name=name, grid=(t // tm,),
        in_specs=[pl.BlockSpec((tm, 256), lambda i: (i, 5)), pl.BlockSpec((tm, 256), lambda i: (i, 6)), vec, vec,
                  pl.BlockSpec((C_GROUPS, C_CHUNK, C_CHUNK), lambda i: (0, 0, 0)),
                  pl.BlockSpec((C_CHUNK, 256), lambda i: (0, 0))],
        out_specs=pl.BlockSpec((tm, 256), lambda i: (i, 0)), out_shape=_sds((t, 256), F32),
        compiler_params=_cparams(PAR),
    )(proj, proj, gam, bet, ws, bst)


def _c_bwd(proj, dycat, gam, bet, ws, wst, bst, *, tm, name):
    t = proj.shape[0]
    nch = tm // C_CHUNK
    nstep = t // tm

    def body(u_ref, v_ref, dy_ref, g_ref, b_ref, ws_ref, wst_ref, bs_ref,
             du_ref, dv_ref, dws_ref, dbs_ref, dg_ref, db_ref, dvn_s):
        step = pl.program_id(0)

        @pl.when(step == 0)
        def _():
            dws_ref[...] = jnp.zeros_like(dws_ref)
            dbs_ref[...] = jnp.zeros_like(dbs_ref)
            dg_ref[...] = jnp.zeros_like(dg_ref)
            db_ref[...] = jnp.zeros_like(db_ref)

        cv = v_ref[...]
        gam_v = g_ref[...]
        vn, xhat, r = _c_norm(cv, gam_v, b_ref[...])
        vnb = vn.astype(BF16)
        for c in range(nch):
            rows = slice(c * C_CHUNK, (c + 1) * C_CHUNK)
            for g in range(C_GROUPS):
                gs = slice(g * 64, (g + 1) * 64)
                cu = u_ref[rows, gs]
                dy = dy_ref[rows, gs]
                mixed = jnp.dot(ws_ref[g], vnb[rows, gs], preferred_element_type=F32) + bs_ref[:, gs]
                du_ref[rows, gs] = dy * mixed * _gelu_grad(cu)
                dmix = dy * _gelu(cu)
                dbs_ref[:, gs] += dmix
                dmb = dmix.astype(BF16)
                dws_ref[g] += lax.dot_general(dmb, vnb[rows, gs], NT, preferred_element_type=F32)
                dvn_s[rows, gs] = jnp.dot(wst_ref[g], dmb, preferred_element_type=F32)
        dvn = dvn_s[...]
        dg_ref[...] += jnp.sum(dvn * xhat, axis=0, keepdims=True)
        db_ref[...] += jnp.sum(dvn, axis=0, keepdims=True)
        dxh = dvn * gam_v
        dvg = r * (dxh - jnp.mean(dxh, axis=-1, keepdims=True) - xhat * jnp.mean(dxh * xhat, axis=-1, keepdims=True))
        dv_ref[...] = dvg * _gelu_grad(cv)

        @pl.when(step == nstep - 1)
        def _():
            dbs_ref[...] = _seg_sum(dbs_ref[...], _group_sum_matrix(256, True))

    vec = pl.BlockSpec((1, 256), lambda i: (0, 0))
    mat = pl.BlockSpec((C_GROUPS, C_CHUNK, C_CHUNK), lambda i: (0, 0, 0))
    bsp = pl.BlockSpec((C_CHUNK, 256), lambda i: (0, 0))
    tile = pl.BlockSpec((tm, 256), lambda i: (i, 0))
    return pl.pallas_call(
        body, name=name, grid=(nstep,),
        in_specs=[pl.BlockSpec((tm, 256), lambda i: (i, 5)), pl.BlockSpec((tm, 256), lambda i: (i, 6)),
                  pl.BlockSpec((tm, 256), lambda i: (i, 2)), vec, vec, mat, mat, bsp],
        out_specs=[tile, tile, mat, bsp, vec, vec],
        out_shape=[_sds((t, 256), F32), _sds((t, 256), F32), _sds((C_GROUPS, C_CHUNK, C_CHUNK), F32),
                   _sds((C_CHUNK, 256), F32), _sds((1, 256), F32), _sds((1, 256), F32)],
        scratch_shapes=[pltpu.VMEM((tm, 256), F32)],
        compiler_params=_cparams(ARB),
    )(proj, proj, dycat, gam, bet, ws, wst, bst)


FF_TC = 128
FF_NB = D_FF // FF_TC
FF_CH = 64
FF_HALO = 16


def _edge_taps(ref, first):
    if first:
        ext = ref[0:FF_CH + FF_HALO, :].astype(F32)
        body = slice(0, FF_CH)
    else:
        ext = ref[SEQ - FF_CH - FF_HALO:SEQ, :].astype(F32)
        body = slice(FF_HALO, FF_HALO + FF_CH)
    n = ext.shape[0]
    row = lax.broadcasted_iota(jnp.int32, ext.shape, 0)
    dn = pltpu.roll(ext, 1, 0)
    up = pltpu.roll(ext, n - 1, 0)
    if first:
        dn = jnp.where(row == 0, 0.0, dn)
    else:
        up = jnp.where(row == n - 1, 0.0, up)
    return dn[body], ext[body], up[body]


def _mid_taps(ref, r0):
    ext = ref[pl.ds(pl.multiple_of(r0 - FF_HALO, FF_HALO), FF_CH + 2 * FF_HALO), :].astype(F32)
    n = ext.shape[0]
    body = slice(FF_HALO, FF_HALO + FF_CH)
    return pltpu.roll(ext, 1, 0)[body], ext[body], pltpu.roll(ext, n - 1, 0)[body]


def _chunk_loop(step):
    step(0, lambda ref: _edge_taps(ref, True))

    def mid(i, carry):
        r0 = pl.multiple_of(i * FF_CH, FF_CH)
        step(r0, lambda ref: _mid_taps(ref, r0))
        return carry

    lax.fori_loop(1, SEQ // FF_CH - 1, mid, 0)
    step(SEQ - FF_CH, lambda ref: _edge_taps(ref, False))


def _conv3(taps, w_ref, b_ref):
    dn, md, up = taps
    return w_ref[0:1, :] * dn + w_ref[1:2, :] * md + w_ref[2:3, :] * up + b_ref[...]


def _ff_specs(order):
    def at(fn):
        return (lambda b, j: fn(b, j)) if order == "bj" else (lambda j, b: fn(b, j))
    hs = [pl.BlockSpec((None, SEQ, FF_TC), at(lambda b, j, o=o: (b, 0, j + o))) for o in (0, FF_NB)]
    ws = [pl.BlockSpec((3, FF_TC), at(lambda b, j, o=o: (0, j + o))) for o in (0, FF_NB)]
    bs = [pl.BlockSpec((1, FF_TC), at(lambda b, j, o=o: (0, j + o))) for o in (0, FF_NB)]
    return hs, ws, bs


def _conv_gate_fwd(h, cw, cb, *, name):
    t = h.shape[0]
    bl = t // SEQ

    def body(hg_ref, hu_ref, wg_ref, wu_ref, bg_ref, bu_ref, a_ref):
        def step(r0, taps):
            cg = _conv3(taps(hg_ref), wg_ref, bg_ref)
            cu = _conv3(taps(hu_ref), wu_ref, bu_ref)
            a_ref[pl.ds(r0, FF_CH), :] = (cg * _sigmoid(cg) * cu).astype(BF16)

        _chunk_loop(step)

    hs, ws, bs = _ff_specs("bj")
    h3 = h.reshape(bl, SEQ, 2 * D_FF)
    act = pl.pallas_call(
        body, name=name, grid=(bl, FF_NB), in_specs=hs + ws + bs,
        out_specs=pl.BlockSpec((None, SEQ, FF_TC), lambda b, j: (b, 0, j)),
        out_shape=_sds((bl, SEQ, D_FF), BF16),
        compiler_params=_cparams(PAR, PAR),
    )(h3, h3, cw, cw, cb, cb)
    return act.reshape(t, D_FF)


def _conv_gate_bwd(h, dact, cw, cb, *, name):
    t = h.shape[0]
    bl = t // SEQ

    def body(hg_ref, hu_ref, wg_ref, wu_ref, bg_ref, bu_ref, da_ref,
             dhg_ref, dhu_ref, dwg_ref, dwu_ref, dbg_ref, dbu_ref, dg_s, du_s):
        @pl.when(pl.program_id(1) == 0)
        def _():
            for ref in (dwg_ref, dwu_ref, dbg_ref, dbu_ref):
                ref[...] = jnp.zeros_like(ref)

        red = lambda x: jnp.sum(x, axis=0, keepdims=True)

        def pass1(r0, taps):
            tg, tu = taps(hg_ref), taps(hu_ref)
            cg = _conv3(tg, wg_ref, bg_ref)
            cu = _conv3(tu, wu_ref, bu_ref)
            da = da_ref[pl.ds(r0, FF_CH), :].astype(F32)
            sg = _sigmoid(cg)
            dcg = da * cu * (sg * (1.0 + cg * (1.0 - sg)))
            dcu = da * (cg * sg)
            dg_s[pl.ds(r0, FF_CH), :] = dcg
            du_s[pl.ds(r0, FF_CH), :] = dcu
            for d, tp, dw_ref, db_ref in ((dcg, tg, dwg_ref, dbg_ref), (dcu, tu, dwu_ref, dbu_ref)):
                for k in range(3):
                    dw_ref[k:k + 1, :] += red(d * tp[k])
                db_ref[...] += red(d)

        _chunk_loop(pass1)

        def pass2(r0, taps):
            for s, w_ref, o_ref in ((dg_s, wg_ref, dhg_ref), (du_s, wu_ref, dhu_ref)):
                dn, md, up = taps(s)
                o_ref[pl.ds(r0, FF_CH), :] = (w_ref[0:1, :] * up + w_ref[1:2, :] * md + w_ref[2:3, :] * dn).astype(BF16)

        _chunk_loop(pass2)

    hs, ws, bs = _ff_specs("jb")
    half = pl.BlockSpec((None, SEQ, FF_TC), lambda j, b: (b, 0, j))
    wsp = pl.BlockSpec((3, FF_TC), lambda j, b: (0, j))
    bsp = pl.BlockSpec((1, FF_TC), lambda j, b: (0, j))
    h3 = h.reshape(bl, SEQ, 2 * D_FF)
    dhg, dhu, dwg, dwu, dbg, dbu = pl.pallas_call(
        body, name=name, grid=(FF_NB, bl), in_specs=hs + ws + bs + [half],
        out_specs=[half, half, wsp, wsp, bsp, bsp],
        out_shape=[_sds((bl, SEQ, D_FF), BF16), _sds((bl, SEQ, D_FF), BF16), _sds((3, D_FF), F32), _sds((3, D_FF), F32),
                   _sds((1, D_FF), F32), _sds((1, D_FF), F32)],
        scratch_shapes=[pltpu.VMEM((SEQ, FF_TC), F32), pltpu.VMEM((SEQ, FF_TC), F32)],
        compiler_params=_cparams(PAR, ARB),
    )(h3, h3, cw, cw, cb, cb, dact.reshape(bl, SEQ, D_FF))
    return (dhg.reshape(t, D_FF), dhu.reshape(t, D_FF), jnp.concatenate([dwg, dwu], axis=1),
            jnp.concatenate([dbg, dbu], axis=1))


def _ple_fwd(x2, gain, wg, pe, pe_blk, wp, *, tm, tn, name):
    t, k = x2.shape
    n = wg.shape[1]

    def body(x_ref, g_ref, wg_ref, pe_ref, wp_ref, xr_ref, hn_ref, x3_ref, gt_ref, pp_ref, hn_s):
        @pl.when(pl.program_id(1) == 0)
        def _():
            x = x_ref[...]
            r = lax.rsqrt(jnp.mean(x * x, axis=-1, keepdims=True) + EPS)
            hn_s[...] = (x * r * g_ref[...]).astype(BF16)
            hn_ref[...] = hn_s[...]

        gate = _sigmoid(jnp.dot(hn_s[...], wg_ref[...], preferred_element_type=F32))
        pp = jnp.dot(pe_ref[...].astype(BF16), wp_ref[...], preferred_element_type=F32)
        gt_ref[...] = gate
        pp_ref[...] = pp
        x3_ref[...] = xr_ref[...] + pp * gate

    tile = pl.BlockSpec((tm, tn), lambda i, j: (i, j))
    return pl.pallas_call(
        body, name=name, grid=(t // tm, n // tn),
        in_specs=[pl.BlockSpec((tm, k), lambda i, j: (i, 0)), pl.BlockSpec((1, k), lambda i, j: (0, 0)),
                  pl.BlockSpec((k, tn), lambda i, j: (0, j)), pl.BlockSpec((tm, PLE_DIM), lambda i, j: (pe_blk + i, 0)),
                  pl.BlockSpec((PLE_DIM, tn), lambda i, j: (0, j)), tile],
        out_specs=[pl.BlockSpec((tm, k), lambda i, j: (i, 0)), tile, tile, tile],
        out_shape=[_sds((t, k), BF16), _sds((t, n), F32), _sds((t, n), F32), _sds((t, n), F32)],
        scratch_shapes=[pltpu.VMEM((tm, k), BF16)],
        compiler_params=_cparams(PAR, ARB),
    )(x2, gain, wg, pe, wp, x2)


def _ple_bwd_ew(dx3, gate, pp, *, tm, name):
    t, n = dx3.shape

    def body(d_ref, g_ref, p_ref, dz_ref, dpp_ref):
        d, g = d_ref[...], g_ref[...]
        dz_ref[...] = (d * p_ref[...] * g * (1.0 - g)).astype(BF16)
        dpp_ref[...] = (d * g).astype(BF16)

    spec = pl.BlockSpec((tm, n), lambda i: (i, 0))
    return pl.pallas_call(
        body, name=name, grid=(t // tm,), in_specs=[spec] * 3, out_specs=[spec] * 2,
        out_shape=[_sds((t, n), BF16)] * 2, compiler_params=_cparams(PAR),
    )(dx3, gate, pp)


def _loss_head(y, tgt, *, tm, name):
    t, d = y.shape

    def body(y_ref, t_ref, l_ref, dy_ref):
        @pl.when(pl.program_id(0) == 0)
        def _():
            l_ref[...] = jnp.zeros_like(l_ref)

        e = y_ref[...] - t_ref[...]
        dy_ref[...] = e * (1.0 / d)
        s = jnp.sum(jnp.sum(e * e, axis=1, keepdims=True), axis=0, keepdims=True)
        l_ref[...] += jnp.broadcast_to(s * (0.5 / d), (8, 128))

    spec = pl.BlockSpec((tm, d), lambda i: (i, 0))
    return pl.pallas_call(
        body, name=name, grid=(t // tm,), in_specs=[spec, spec],
        out_specs=[pl.BlockSpec((8, 128), lambda i: (0, 0)), spec],
        out_shape=[_sds((8, 128), F32), _sds((t, d), F32)], compiler_params=_cparams(ARB),
    )(y, tgt)


BIAS_PC = 8192


def _onehot(bucket_row):
    rows = lax.broadcasted_iota(jnp.int32, (REL_BUCKETS, bucket_row.shape[1]), 0)
    return (rows == bucket_row).astype(F32)


def _bias_lookup(table_t, bucket, *, name):
    h = table_t.shape[0]
    p = bucket.shape[1]

    def body(t_ref, b_ref, o_ref):
        bk = b_ref[...]
        val = jnp.dot(t_ref[...], _onehot(bk), precision=HI, preferred_element_type=F32)
        o_ref[...] = jnp.where(bk >= 0, val, NEG_INF)

    return pl.pallas_call(
        body, name=name, grid=(p // BIAS_PC,),
        in_specs=[pl.BlockSpec((h, REL_BUCKETS), lambda i: (0, 0)), pl.BlockSpec((1, BIAS_PC), lambda i: (0, i))],
        out_specs=pl.BlockSpec((h, BIAS_PC), lambda i: (0, i)), out_shape=_sds((h, p), F32),
        compiler_params=_cparams(PAR),
    )(table_t, bucket)


def _bucket_reduce(dbias, bucket, *, name):
    h, p = dbias.shape

    def body(d_ref, b_ref, o_ref):
        @pl.when(pl.program_id(0) == 0)
        def _():
            o_ref[...] = jnp.zeros_like(o_ref)

        o_ref[...] += lax.dot_general(d_ref[...], _onehot(b_ref[...]), NT, precision=HI, preferred_element_type=F32)

    return pl.pallas_call(
        body, name=name, grid=(p // BIAS_PC,),
        in_specs=[pl.BlockSpec((h, BIAS_PC), lambda i: (0, i)), pl.BlockSpec((1, BIAS_PC), lambda i: (0, i))],
        out_specs=pl.BlockSpec((h, REL_BUCKETS), lambda i: (0, 0)), out_shape=_sds((h, REL_BUCKETS), F32),
        compiler_params=_cparams(ARB),
    )(dbias, bucket)


def _adamw_math(w, g, m, v):
    m = ADAM_B1 * m + (1.0 - ADAM_B1) * g
    v = ADAM_B2 * v + (1.0 - ADAM_B2) * (g * g)
    m_hat = m / (1.0 - ADAM_B1 ** ADAM_STEP)
    v_hat = v / (1.0 - ADAM_B2 ** ADAM_STEP)
    delta = -ADAM_LR * (m_hat / (jnp.sqrt(v_hat) + ADAM_EPS) + ADAM_WD * w)
    return delta, m, v


def _adamw_reduce(parts, w, m, v, *, tr, name):
    nl = len(parts)
    rows, c = w.shape
    r = rows // nl
    nt = r // tr

    def body(*refs):
        p_refs = refs[:nl]
        w_ref, m_ref, v_ref, g_ref, d_ref, nm_ref, nv_ref = refs[nl:]
        for li, p_ref in enumerate(p_refs):
            @pl.when(pl.program_id(0) == li)
            def _(p_ref=p_ref):
                g = p_ref[0].astype(F32)
                for k in range(1, N_DEV):
                    g = g + p_ref[k].astype(F32)
                d, nm, nv = _adamw_math(w_ref[...], g, m_ref[...], v_ref[...])
                g_ref[...] = g
                d_ref[...] = d
                nm_ref[...] = nm
                nv_ref[...] = nv

    def part_map(li):
        return lambda l, i: (0, jnp.where(l == li, i, jnp.where(l < li, 0, nt - 1)), 0)

    spec = pl.BlockSpec((tr, c), lambda l, i: (l * nt + i, 0))
    return pl.pallas_call(
        body, name=name, grid=(nl, nt),
        in_specs=[pl.BlockSpec((N_DEV, tr, c), part_map(li)) for li in range(nl)] + [spec, spec, spec],
        out_specs=[spec] * 4, out_shape=[_sds((rows, c), F32)] * 4, compiler_params=_cparams(ARB, ARB),
    )(*parts, w, m, v)


def _adamw_plain(g, w, m, v, *, name):
    def body(g_ref, w_ref, m_ref, v_ref, d_ref, nm_ref, nv_ref):
        d, nm, nv = _adamw_math(w_ref[...], g_ref[...], m_ref[...], v_ref[...])
        d_ref[...] = d
        nm_ref[...] = nm
        nv_ref[...] = nv

    return pl.pallas_call(body, name=name, out_shape=[_sds(w.shape, F32)] * 3)(g, w, m, v)


def _mesh_pos():
    return lax.axis_index("x"), lax.axis_index("y"), lax.axis_index("c")


def _allgather_body(x_refs, out_refs, send_sems, recv_sems, local_sems, slot):
    x, y, c = _mesh_pos()
    me, sibling = (x, y, c), (x, y, 1 - c)
    chips = [(1 - x, y), (x, 1 - y), (1 - x, 1 - y)]
    waits = []
    for a, (x_ref, out_ref) in enumerate(zip(x_refs, out_refs)):
        def copy(k, block, to, src=None, out_ref=out_ref, a=a):
            return pltpu.make_async_remote_copy(
                src_ref=slot(out_ref, block) if src is None else src, dst_ref=slot(out_ref, block),
                send_sem=send_sems.at[a, k], recv_sem=recv_sems.at[a, k], device_id=to, device_id_type=MESH)

        mine = pltpu.make_async_copy(x_ref, slot(out_ref, me), local_sems.at[a])
        mine.start()
        first = [copy(0, me, sibling, src=x_ref)]
        first += [copy(1 + j, me, (*chip, c), src=x_ref) for j, chip in enumerate(chips)]
        for cp in first:
            cp.start()
        waits.append((copy, mine, first))
    sends = []
    for copy, mine, first in waits:
        passed = [copy(4 + j, (*chip, c), sibling) for j, chip in enumerate(chips)]
        for j, chip in enumerate(chips):
            copy(1 + j, (*chip, c), me).wait_recv()
            passed[j].start()
        sends.append(passed)
    for (copy, mine, first), passed in zip(waits, sends):
        copy(0, sibling, me).wait_recv()
        for j, chip in enumerate(chips):
            copy(4 + j, (*chip, 1 - c), me).wait_recv()
        for cp in first + passed:
            cp.wait_send()
        mine.wait()


PEER_FLIPS = ((0, 0, 1), (1, 0, 0), (0, 1, 0), (1, 1, 0), (1, 0, 1), (0, 1, 1), (1, 1, 1))
HBM_SPEC = pl.BlockSpec(memory_space=pltpu.HBM)
SEM_SPEC = pl.BlockSpec(memory_space=pltpu.SEMAPHORE)
DATAFLOW = pltpu.SideEffectType.DATAFLOW_SIDE_EFFECTING


def _peer_copies(x_refs, land_refs, send_sem, recv_sem, scatter):
    x, y, c = _mesh_pos()
    me = 4 * x + 2 * y + c
    copies = []
    for x_ref, land_ref in zip(x_refs, land_refs):
        for fx, fy, fc in PEER_FLIPS:
            px, py, pc = x ^ fx, y ^ fy, c ^ fc
            src = x_ref.at[4 * px + 2 * py + pc] if scatter else x_ref
            copies.append(pltpu.make_async_remote_copy(
                src_ref=src, dst_ref=land_ref.at[me], send_sem=send_sem, recv_sem=recv_sem,
                device_id=(px, py, pc), device_id_type=MESH))
    return copies


def _exchange_start(groups, *, scatter, name):
    xs = [x for grp in groups for x in grp]
    na = len(xs)
    ngrp = len(groups)
    firsts = np.cumsum([0] + [len(grp) for grp in groups])
    land_shapes = [x.shape if scatter else (N_DEV,) + x.shape for x in xs]

    def body(*refs):
        x_refs, land_refs = refs[:na], refs[na:2 * na]
        sems = refs[2 * na:2 * na + 2 * ngrp]
        token, local_sem = refs[-2], refs[-1]
        xm, ym, cm = _mesh_pos()
        me = 4 * xm + 2 * ym + cm
        for gi in range(ngrp):
            lo, hi = firsts[gi], firsts[gi + 1]
            for cp in _peer_copies(x_refs[lo:hi], land_refs[lo:hi], sems[2 * gi], sems[2 * gi + 1], scatter):
                cp.start()
        for x_ref, land_ref in zip(x_refs, land_refs):
            own = pltpu.make_async_copy(x_ref.at[me] if scatter else x_ref, land_ref.at[me], local_sem)
            own.start()
            own.wait()
        token[...] = jnp.zeros_like(token)

    sem_shapes = [pltpu.SemaphoreType.DMA(())] * (2 * ngrp)
    lands = [pltpu.with_memory_space_constraint(lax.empty(s, x.dtype), pltpu.HBM) for s, x in zip(land_shapes, xs)]
    outs = pl.pallas_call(
        body, name=name,
        in_specs=[HBM_SPEC] * (2 * na),
        out_specs=[SEM_SPEC] * (2 * ngrp) + [HBM_SPEC] * (2 * na) + [pl.BlockSpec(memory_space=pltpu.VMEM)],
        out_shape=sem_shapes + [pltpu.HBM(x.shape, x.dtype) for x in xs]
        + [pltpu.HBM(s, x.dtype) for s, x in zip(land_shapes, xs)] + [_sds((8, 128), F32)],
        input_output_aliases={i: 2 * ngrp + i for i in range(2 * na)},
        scratch_shapes=[pltpu.SemaphoreType.DMA],
        compiler_params=pltpu.CompilerParams(has_side_effects=DATAFLOW),
    )(*[pltpu.with_memory_space_constraint(x, pltpu.HBM) for x in xs], *lands)
    sems, thru, token = outs[:2 * ngrp], outs[2 * ngrp:2 * ngrp + 2 * na], outs[-1]
    handles = []
    for gi in range(ngrp):
        lo, hi = firsts[gi], firsts[gi + 1]
        handles.append((sems[2 * gi], sems[2 * gi + 1], thru[lo:hi], thru[na + lo:na + hi]))
    return handles, token


def _exchange_wait(handle, after, *, scatter, name):
    send_sems, recv_sems, x_thru, land_thru = handle
    na = len(x_thru)

    def body(*refs):
        x_refs, land_refs = refs[:na], refs[na:2 * na]
        send_ref, recv_ref = refs[2 * na], refs[2 * na + 1]
        for cp in _peer_copies(x_refs, land_refs, send_ref, recv_ref, scatter):
            cp.wait_send()
            cp.wait_recv()

    outs = pl.pallas_call(
        body, name=name,
        in_specs=[HBM_SPEC] * (2 * na) + [SEM_SPEC, SEM_SPEC, pl.BlockSpec(memory_space=pl.ANY)],
        out_specs=[HBM_SPEC] * (2 * na),
        out_shape=[pltpu.HBM(a.shape, a.dtype) for a in list(x_thru) + list(land_thru)],
        input_output_aliases={i: i for i in range(2 * na)},
        compiler_params=pltpu.CompilerParams(has_side_effects=DATAFLOW),
    )(*x_thru, *land_thru, send_sems, recv_sems, after)
    return outs[na:]


def _allgather_vmem(x, *, reduce, name):
    r, c = x.shape

    def body(x_ref, out_ref, *rest):
        if reduce:
            gath, send_sems, recv_sems, local_sems = rest
        else:
            send_sems, recv_sems, local_sems = rest
            gath = out_ref
        _allgather_body([x_ref], [gath], send_sems, recv_sems, local_sems,
                        lambda ref, pos: ref.at[pl.ds((4 * pos[0] + 2 * pos[1] + pos[2]) * r, r), :])
        if reduce:
            acc = gath[0:r, :]
            for k in range(1, N_DEV):
                acc = acc + gath[k * r:(k + 1) * r, :]
            out_ref[...] = acc

    vm = pl.BlockSpec(memory_space=pltpu.VMEM)
    scratch = [pltpu.SemaphoreType.DMA((1, 7)), pltpu.SemaphoreType.DMA((1, 7)), pltpu.SemaphoreType.DMA((1,))]
    if reduce:
        scratch = [pltpu.VMEM((N_DEV * r, c), x.dtype)] + scratch
    return pl.pallas_call(
        body, name=name, in_specs=[vm], out_specs=vm,
        out_shape=_sds((r, c) if reduce else (N_DEV * r, c), x.dtype), scratch_shapes=scratch,
    )(x)


def _t5_bucket(rel):
    nb = REL_BUCKETS // 2
    ret = jnp.where(rel > 0, nb, 0)
    n = jnp.abs(rel)
    max_exact = nb // 2
    nf = jnp.maximum(n, 1).astype(F32)
    large = max_exact + (jnp.log(nf / max_exact) / math.log(REL_MAX_DIST / max_exact)
                         * (nb - max_exact)).astype(jnp.int32)
    large = jnp.minimum(large, nb - 1)
    return ret + jnp.where(n < max_exact, n, large)


def _band_pattern(block, radius, dil):
    kw = block + 2 * radius
    rel = jnp.arange(kw)[None, :] - radius - jnp.arange(block)[:, None]
    return jnp.where(jnp.abs(rel) <= radius, _t5_bucket(rel * dil), -1).astype(jnp.int32).reshape(1, block * kw)


def _rope_tables():
    lane = np.arange(64)
    seg, j = lane // 32, lane % 32
    inv = ROPE_THETA ** (-jnp.arange(0, 32, 2, dtype=F32) / 32)
    tpos = jnp.arange(SEQ)
    pos = jnp.where(jnp.asarray(seg)[None, :] == 0, (tpos // GRID_W)[:, None], (tpos % GRID_W)[:, None])
    ang = pos.astype(F32) * inv[jnp.asarray(j % 16)][None, :]
    cos = jnp.cos(ang)
    sins = jnp.where(jnp.asarray(j)[None, :] < 16, -jnp.sin(ang), jnp.sin(ang))
    return jnp.tile(cos, (1, 4)), jnp.tile(sins, (1, 4))


A_Q, A_K, A_V = (256, 0), (256, 1), (256, 2)
B_Q, B_K, B_V = (256, 0), (128, 2), (128, 3)
A_HEADS = dict(rad=A_RADIUS, nh=4, nkv=4)
B_HEADS = dict(rad=SWA_RADIUS, nh=4, nkv=2)


def _pin(arr, token):
    return arr if token is None else arr + token[0:1, 0:1]


def _local_step(x, pe, tgt, rel_bias, wts, matmul_weights, grads_ready):
    t = x.shape[0]
    bl = t // SEQ
    cos, sins = _rope_tables()
    blocks_a = [min(BAND_BLOCK, SEQ // d) for d in DILATIONS]
    pats_a = [_band_pattern(blk, A_RADIUS, d) for blk, d in zip(blocks_a, DILATIONS)]
    pat_b = _band_pattern(BAND_BLOCK, SWA_RADIUS, 1)
    table_t = rel_bias.T
    bias_a = [_bias_lookup(table_t[:4], pt, name=f"bias_a{ci}").reshape(4, blk, blk + 2 * A_RADIUS)
              for ci, (pt, blk) in enumerate(zip(pats_a, blocks_a))]
    bias_b = _bias_lookup(table_t[4:], pat_b, name="bias_b").reshape(4, BAND_BLOCK, BAND_BLOCK + 2 * SWA_RADIUS)
    nat4 = lambda a: a.reshape(bl, 1, SEQ, a.shape[-1])

    saved = []
    for li in range(DEPTH):
        w = dict(wts[li])
        w.update(matmul_weights(li, "in", x))
        hn0, proj = _norm_mm((x,), w["g_mix"], w["w_in"], None, tm=1024, tn=1152, name="mix_in_fwd")
        w.update(matmul_weights(li, "rest", proj))
        qa1, qa4, qa16, qb, qd = _qkprep_fwd(proj, w["qk_gains"], cos, sins, tm=512, name="qkprep_fwd")
        qa = (nat4(qa1), qa4, qa16)
        oa, la = [], []
        for ci in range(3):
            o, l = _band_fwd(qa[ci], A_Q, A_K, A_V, bias_a[ci], None, name=f"band_a{ci}_fwd", **A_HEADS)
            oa.append(o)
            la.append(l)
        oa[0], la[0] = oa[0].reshape(t, 256), la[0].reshape(t, 256)
        ya, lse_a = _combine_a(oa, la, tm=512, name="combine_a")
        yb, lse_b = _band_fwd(nat4(qb), B_Q, B_K, B_V, bias_b, w["sink_t"], name="band_b_fwd", **B_HEADS)
        yb = yb.reshape(t, 256)
        yc = _c_fwd(proj, w["c_g"], w["c_b"], w["c_ws"], w["c_bst"], tm=512, name="c_fwd")
        yd, lse_d = _dense_fwd(qd, tq=128, name="dense_fwd")
        mixed, x1 = _norm_mm((ya, yb, yc, yd), w["out_gain"], w["w_out"], x, tm=1024, tn=1024, name="mix_out_fwd")
        hn1, h = _norm_mm((x1,), w["g_ffn"], w["w_up"], None, tm=1024, tn=1408, name="ffn_up_fwd", out_dtype=BF16)
        act = _conv_gate_fwd(h, w["conv_w"], w["conv_b"], name="conv_gate_fwd")
        x2 = _mm(act, w["w_down"], "nn", x1, tm=1024, tn=1024, out_dtype=F32, name="ffn_down_fwd")
        hn2, x3, gate, pp = _ple_fwd(x2, w["g_ple"], w["w_gate"], pe, li * (t // 1024), w["w_proj"], tm=1024, tn=512,
                                     name="ple_fwd")
        saved.append(dict(w=w, x0=x, hn0=hn0, proj=proj, qa=qa, qb=qb, qd=qd, ya=ya, lse_a=lse_a, yb=yb, lse_b=lse_b,
                          yc=yc, yd=yd, lse_d=lse_d, mixed=mixed, x1=x1, hn1=hn1, h=h, act=act, x2=x2, hn2=hn2,
                          gate=gate, pp=pp))
        x = x3

    loss_tile, dx = _loss_head(x, tgt, tm=512, name="loss_head")
    grads = [None] * DEPTH
    d_table_a = jnp.zeros((4, REL_BUCKETS), F32)
    d_table_b = jnp.zeros((4, REL_BUCKETS), F32)
    token = None
    for li in reversed(range(DEPTH)):
        s = saved[li]
        w = s["w"]
        g = {}
        w["g_ple"] = _pin(w["g_ple"], token)
        dz, dpp = _ple_bwd_ew(dx, s["gate"], s["pp"], tm=512, name="ple_bwd_ew")
        g["w_gate"] = _mm(s["hn2"], dz, "tn", None, tm=1024, tn=512, out_dtype=BF16, name="dw_gate")
        g["w_proj"] = _mm(pe, dpp, "tn", None, tm=256, tn=1024, out_dtype=BF16, name="dw_proj", a_rows=(li, t))
        dx2, dx2b, g["g_ple"] = _mm_bt_normbwd((dz,), w["w_gate"], (s["x2"],), w["g_ple"], dx, tm=1024, tn=1024,
                                               name="ple_bwd", emit_bf16=True)
        g["w_down"] = _mm(s["act"], dx2b, "tn", None, tm=1408, tn=512, out_dtype=BF16, name="dw_down")
        dact = _mm(dx2b, w["w_down"], "nt", None, tm=1024, tn=1408, out_dtype=BF16, name="ffn_down_bwd")
        dhg, dhu, g["conv_w"], g["conv_b"] = _conv_gate_bwd(s["h"], dact, w["conv_w"], w["conv_b"], name="conv_gate_bwd")
        g["w_up"] = jnp.concatenate(
            [_mm(s["hn1"], dhalf, "tn", None, tm=1024, tn=1408, out_dtype=BF16, name=f"dw_up_{nm}")
             for nm, dhalf in (("gate", dhg), ("up", dhu))], axis=1)
        g_ffn = _pin(w["g_ffn"], grads_ready(li, "mid", g))
        dx1, dx1b, g["g_ffn"] = _mm_bt_normbwd((dhg, dhu), w["w_up"], (s["x1"],), g_ffn, dx2, tm=1024, tn=1408,
                                               name="ffn_up_bwd", emit_bf16=True)
        g["w_out"] = _mm(s["mixed"], dx1b, "tn", None, tm=1024, tn=512, out_dtype=BF16, name="dw_out")
        dycat, g["out_gain"] = _mm_bt_normbwd((dx1b,), w["w_out"], (s["ya"], s["yb"], s["yc"], s["yd"]), w["out_gain"],
                                              None, tm=1024, tn=1024, name="mix_out_bwd")
        dy_r, lse_r, dl_a, dl_b, dl_d = _deltas(dycat, s["ya"], s["yb"], s["yd"], s["lse_a"], tm=512, name="deltas")
        dy_a = (nat4(dycat),) + tuple(dy_r)
        lse_a = (nat4(s["lse_a"]),) + tuple(lse_r)
        dl_a = (nat4(dl_a[0]),) + tuple(dl_a[1:])
        da = []
        for ci in range(3):
            dq, dk, dv, dbias = _band_bwd(s["qa"][ci], A_Q, A_K, A_V, bias_a[ci], None, dy_a[ci], 0, lse_a[ci],
                                          dl_a[ci], name=f"band_a{ci}_bwd", **A_HEADS)
            if ci == 0:
                dq, dk, dv = (a.reshape(t, 256) for a in (dq, dk, dv))
            da.append((dq, dk, dv))
            d_table_a = d_table_a + _bucket_reduce(dbias.reshape(4, -1), pats_a[ci], name=f"bucket_a{ci}")
        dqb, dkb, dvb, dbias_b, dsink = _band_bwd(nat4(s["qb"]), B_Q, B_K, B_V, bias_b, w["sink_t"], nat4(dycat), 1,
                                                  nat4(s["lse_b"]), nat4(dl_b), name="band_b_bwd", **B_HEADS)
        d_table_b = d_table_b + _bucket_reduce(dbias_b.reshape(4, -1), pat_b, name="bucket_b")
        g["sink"] = dsink[:, 0, 0]
        dd = _dense_bwd(s["qd"], dycat, s["lse_d"], dl_d, tq=128, name="dense_bwd")
        dcu, dcv, g["c_ws"], dbs, g["c_g"], g["c_b"] = _c_bwd(s["proj"], dycat, w["c_g"], w["c_b"], w["c_ws"],
                                                               w["c_wst"], w["c_bst"], tm=512, name="c_bwd")
        g["c_bs"] = dbs[:, ::64].T
        db = (dqb.reshape(t, 256), dkb.reshape(t, 128), dvb.reshape(t, 128))
        dproj, dgains = _qkprep_bwd(s["proj"], da, db, dd, dcu, dcv, w["qk_gains"], cos, sins, tm=512, name="qkprep_bwd")
        g["qk_gain"] = dgains[:6, :64].reshape(3, 2, HEAD_DIM)
        g["w_in"] = _mm(s["hn0"], dproj, "tn", None, tm=1024, tn=1152, out_dtype=BF16, name="dw_in")
        dx, g["g_mix"] = _mm_bt_normbwd((dproj,), w["w_in"], (s["x0"],), w["g_mix"], dx1, tm=1024, tn=1152,
                                        name="mix_in_bwd")
        grads[li] = g
        token = grads_ready(li, "end", g)
    d_rel_bias = jnp.concatenate([d_table_a, d_table_b], axis=0).T
    return loss_tile[0, 0], dx, grads, d_rel_bias


WEIGHT_NAMES = ("rel_bias", "ln_mix_g", "w_in", "qk_gain", "sink", "c_norm_g", "c_norm_b", "c_ws", "c_bs", "out_gain",
                "w_out", "ln_ffn_g", "w_up", "conv_w", "conv_b", "w_down", "ln_ple_g", "w_ple_gate", "w_ple_proj")
COL_SHARDED = ("w_in", "w_up", "w_ple_proj")
ROW_SHARDED = ("w_out", "w_down", "w_ple_gate")
SMALL_SHARDED = ("conv_w", "out_gain")
REPLICATED = tuple(n for n in WEIGHT_NAMES if n not in COL_SHARDED + ROW_SHARDED + SMALL_SHARDED)
LOCAL_GRAD_KEY = {"ln_mix_g": "g_mix", "ln_ffn_g": "g_ffn", "ln_ple_g": "g_ple", "c_norm_g": "c_g", "c_norm_b": "c_b",
                  "w_ple_gate": "w_gate", "w_ple_proj": "w_proj"}


def _full_from_gathered(name, gathered):
    _, r, c = gathered.shape
    if name in ROW_SHARDED:
        return gathered.reshape(N_DEV * r, c)
    return jnp.transpose(gathered, (1, 0, 2)).reshape(r, N_DEV * c)


def _slots_from_full(name, full):
    rows, cols = full.shape
    if name in ROW_SHARDED:
        return full.reshape(N_DEV, rows // N_DEV, cols)
    return jnp.transpose(full.reshape(rows, N_DEV, cols // N_DEV), (1, 0, 2))


def _piece_rows(shape):
    return -(-int(np.prod(shape)) // 1024) * 8


def _pack_rows(arrays):
    pieces = []
    for a in arrays:
        n, rows = int(np.prod(a.shape)), _piece_rows(a.shape)
        flat = a.astype(F32).reshape(-1)
        if n != rows * LANES:
            flat = jnp.pad(flat, (0, rows * LANES - n))
        pieces.append(flat.reshape(rows, LANES))
    return jnp.concatenate(pieces, axis=0)


def _unpack_rows(packed, shapes):
    out, off = [], 0
    for shp in shapes:
        n, rows = int(np.prod(shp)), _piece_rows(shp)
        piece = packed[off:off + rows]
        out.append((piece if n == rows * LANES else piece.reshape(-1)[:n]).reshape(shp))
        off += rows
    return out


def kernel(x, p, rel_bias, ln_mix_g, w_in, qk_gain, sink, c_norm_g, c_norm_b, c_ws, c_bs, out_gain, w_out, ln_ffn_g, w_up, conv_w, conv_b, w_down, ln_ple_g, w_ple_gate, w_ple_proj, loss_target, m_rel_bias, m_ln_mix_g, m_w_in, m_qk_gain, m_sink, m_c_norm_g, m_c_norm_b, m_c_ws, m_c_bs, m_out_gain, m_w_out, m_ln_ffn_g, m_w_up, m_conv_w, m_conv_b, m_w_down, m_ln_ple_g, m_w_ple_gate, m_w_ple_proj, v_rel_bias, v_ln_mix_g, v_w_in, v_qk_gain, v_sink, v_c_norm_g, v_c_norm_b, v_c_ws, v_c_bs, v_out_gain, v_w_out, v_ln_ffn_g, v_w_up, v_conv_w, v_conv_b, v_w_down, v_ln_ple_g, v_w_ple_gate, v_w_ple_proj):
    env = dict(locals())
    wt = {n: env[n] for n in WEIGHT_NAMES}
    mom_m = {n: env["m_" + n] for n in WEIGHT_NAMES}
    mom_v = {n: env["v_" + n] for n in WEIGHT_NAMES}
    bl = x.shape[0]
    t = bl * SEQ
    me = 4 * lax.axis_index("x") + 2 * lax.axis_index("y") + lax.axis_index("c")

    big = COL_SHARDED + ROW_SHARDED
    rest = tuple(n for n in big if n != "w_in")
    gather_groups = []
    for li in range(DEPTH):
        gather_groups += [[("w_in", li)], [(n, li) for n in rest]]
    gather_handles, _ = _exchange_start([[wt[n][li].astype(BF16) for n, li in grp] for grp in gather_groups],
                                        scatter=False, name="gather_start")
    full = {}
    small_shapes = [wt[n].shape for n in SMALL_SHARDED]
    small = _allgather_vmem(_pack_rows([wt[n] for n in SMALL_SHARDED]), reduce=False, name="gather_small")
    small = small.reshape(N_DEV, -1)
    off = 0
    for n, shp in zip(SMALL_SHARDED, small_shapes):
        cnt = int(np.prod(shp))
        g = small[:, off:off + cnt].reshape((N_DEV,) + tuple(shp))
        full[n] = jnp.transpose(g, (1, 2, 0, 3)).reshape(shp[0], shp[1], N_DEV * shp[2])
        off += _piece_rows(shp) * LANES

    def head_gain(li, a, b, reps):
        g = jnp.tile(qk_gain[li, a, b], reps)
        return jnp.pad(g, (0, 256 - g.shape[0]))

    wts = []
    for li in range(DEPTH):
        rows = [head_gain(li, 0, 0, 4), head_gain(li, 0, 1, 4), head_gain(li, 1, 0, 4), head_gain(li, 1, 1, 2),
                head_gain(li, 2, 0, 4), head_gain(li, 2, 1, 2), jnp.zeros((256,), F32), jnp.zeros((256,), F32)]
        wts.append(dict(
            g_mix=ln_mix_g[li].reshape(1, -1), qk_gains=jnp.stack(rows),
            sink_t=jnp.broadcast_to(sink[li][:, None, None], (4, 8, 128)),
            c_g=c_norm_g[li].reshape(1, -1), c_b=c_norm_b[li].reshape(1, -1), c_ws=c_ws[li].astype(BF16),
            c_wst=jnp.transpose(c_ws[li], (0, 2, 1)).astype(BF16), c_bst=jnp.repeat(c_bs[li].T, 64, axis=1),
            out_gain=full["out_gain"][li].reshape(1, -1), g_ffn=ln_ffn_g[li].reshape(1, -1),
            conv_w=full["conv_w"][li], conv_b=conv_b[li].reshape(1, -1), g_ple=ln_ple_g[li].reshape(1, -1)))

    local_key = {"w_ple_gate": "w_gate", "w_ple_proj": "w_proj"}

    def matmul_weights(li, part, after):
        gi = 2 * li + (part == "rest")
        lands = _exchange_wait(gather_handles[gi], after, scatter=False, name=f"gather_wait_{li}_{part}")
        return {local_key.get(n, n): _full_from_gathered(n, land) for (n, _), land in zip(gather_groups[gi], lands)}

    mid_names = ("w_ple_gate", "w_ple_proj", "w_down", "w_up")
    end_names = ("w_out", "w_in")
    pending = []

    def start_exchange(li, names, g, tag):
        slots = [_slots_from_full(n, g[local_key.get(n, n)]) for n in names]
        (handle,), token = _exchange_start([slots], scatter=True, name=f"grads_start_{li}_{tag}")
        pending.append((li, names, handle, tag))
        return token

    def grads_ready(li, stage, g):
        if li == 0:
            return start_exchange(li, mid_names if stage == "mid" else end_names, g, stage)
        if stage == "end":
            return start_exchange(li, mid_names + end_names, g, stage)
        return None

    loss_part, dx, grads, d_rel_bias = _local_step(
        x.reshape(t, D_MODEL), p.reshape(DEPTH * t, PLE_DIM), loss_target.reshape(t, D_MODEL), rel_bias, wts,
        matmul_weights, grads_ready)
    loss = lax.psum(loss_part, ("x", "y", "c"))

    def local_grad(n):
        if n == "rel_bias":
            return d_rel_bias
        key = LOCAL_GRAD_KEY.get(n, n)
        return jnp.stack([grads[li][key].reshape(wt[n].shape[1:]) if n in REPLICATED else grads[li][key]
                          for li in range(DEPTH)])

    landed = {}
    for li, names, handle, tag in pending:
        for n, land in zip(names, _exchange_wait(handle, dx, scatter=True, name=f"grads_wait_{li}_{tag}")):
            landed[n, li] = land
    out_g, out_d, out_m, out_v = {}, {}, {}, {}
    for n in big:
        shp = wt[n].shape
        two_d = lambda a: a.reshape(-1, shp[-1])
        res = _adamw_reduce([landed[n, li] for li in range(DEPTH)], two_d(wt[n]), two_d(mom_m[n]), two_d(mom_v[n]),
                            tr=32 if n == "w_down" else 128, name="adamw_" + n)
        out_g[n], out_d[n], out_m[n], out_v[n] = [r.reshape(shp) for r in res]

    small_names = REPLICATED + SMALL_SHARDED
    small_full_shapes = [wt[n].shape if n in REPLICATED else full[n].shape for n in small_names]
    reduced = _allgather_vmem(_pack_rows([local_grad(n) for n in small_names]), reduce=True, name="allreduce_small")
    reduced = dict(zip(small_names, _unpack_rows(reduced, small_full_shapes)))
    rep_shapes = [wt[n].shape for n in REPLICATED]
    upd = _adamw_plain(_pack_rows([reduced[n] for n in REPLICATED]), _pack_rows([wt[n] for n in REPLICATED]),
                       _pack_rows([mom_m[n] for n in REPLICATED]), _pack_rows([mom_v[n] for n in REPLICATED]),
                       name="adamw_replicated")
    for dst, packed in zip((out_d, out_m, out_v), upd):
        dst.update(zip(REPLICATED, _unpack_rows(packed, rep_shapes)))
    for n in REPLICATED:
        out_g[n] = reduced[n]
    for n in SMALL_SHARDED:
        shp = wt[n].shape
        g = reduced[n].reshape(shp[0], shp[1], N_DEV, shp[2])
        g = lax.dynamic_index_in_dim(g, me, axis=2, keepdims=False)
        two_d = lambda a: a.reshape(-1, shp[-1])
        res = _adamw_plain(two_d(g), two_d(wt[n]), two_d(mom_m[n]), two_d(mom_v[n]), name="adamw_" + n)
        out_g[n] = g
        out_d[n], out_m[n], out_v[n] = [r.reshape(shp) for r in res]

    return (loss, dx.reshape(bl, SEQ, D_MODEL), *[out_g[n] for n in WEIGHT_NAMES], *[out_d[n] for n in WEIGHT_NAMES],
            *[out_m[n] for n in WEIGHT_NAMES], *[out_v[n] for n in WEIGHT_NAMES])
```

```python
import math

import jax
import jax.numpy as jnp
import numpy as np
from jax import lax
from jax.experimental import pallas as pl
from jax.experimental.pallas import tpu as pltpu

F32 = jnp.float32
BF16 = jnp.bfloat16
HI = lax.Precision.HIGHEST

N_DEV = 8
D_MODEL = 1024
SEQ = 2048
DEPTH = 2
HEAD_DIM = 64
IN_WIDTH = 2304
D_FF = 2816
PLE_DIM = 256
C_CHUNK = 128
C_GROUPS = 4
DILATED_CFGS = ((128, 1), (512, 4), (2048, 16))
DILATIONS = tuple(d for _, d in DILATED_CFGS)
A_RADIUS = 64
SWA_RADIUS = 128
BAND_BLOCK = 256
GRID_W = 64
ROPE_THETA = 10000.0
REL_BUCKETS = 32
REL_MAX_DIST = 1024
EPS = 1e-6
NEG_INF = -1e30
ATTN_SCALE = HEAD_DIM ** -0.5
LANES = 128

ADAM_LR = 0.001
ADAM_B1 = 0.9
ADAM_B2 = 0.999
ADAM_EPS = 1e-08
ADAM_WD = 0.01
ADAM_STEP = 10

MESH = pl.DeviceIdType.MESH
NT = (((1,), (1,)), ((), ()))
TN = (((0,), (0,)), ((), ()))
ARB = "arbitrary"
PAR = "parallel"


def _cparams(*sem):
    return pltpu.CompilerParams(dimension_semantics=tuple(sem))


def _sds(shape, dtype):
    return jax.ShapeDtypeStruct(tuple(shape), dtype)


def _group_sum_matrix(n, same_group):
    r = lax.broadcasted_iota(jnp.int32, (n, n), 0)
    c = lax.broadcasted_iota(jnp.int32, (n, n), 1)
    if same_group:
        return ((r >> 6) == (c >> 6)).astype(F32)
    return ((r & 63) == (c & 63)).astype(F32)


def _seg_sum(x, e):
    return jnp.dot(x, e, precision=HI, preferred_element_type=F32)


def _gelu(x):
    c = math.sqrt(2.0 / math.pi)
    return 0.5 * x * (1.0 + jnp.tanh(c * (x + 0.044715 * (x * x * x))))


def _gelu_grad(x):
    c = math.sqrt(2.0 / math.pi)
    t = jnp.tanh(c * (x + 0.044715 * (x * x * x)))
    return 0.5 * (1.0 + t) + 0.5 * x * (1.0 - t * t) * c * (1.0 + 3.0 * 0.044715 * (x * x))


def _sigmoid(x):
    return 1.0 / (1.0 + jnp.exp(-x))


def _scatter_cols(scratch, first, val):
    for c in range(val.shape[1] // LANES):
        scratch[first + c] = val[:, c * LANES:(c + 1) * LANES]


def _gather_cols(scratch, first, ncol):
    return jnp.concatenate([scratch[first + c] for c in range(ncol)], axis=1)


def _read_residue(scratch, first, ncol, r, d):
    n = scratch.shape[1] // d
    return jnp.concatenate([scratch.at[first + c][pl.ds(r, n, stride=d), :] for c in range(ncol)], axis=1)


def _write_residue(scratch, first, r, d, val):
    n = scratch.shape[1] // d
    for c in range(val.shape[1] // LANES):
        scratch.at[first + c][pl.ds(r, n, stride=d), :] = val[:, c * LANES:(c + 1) * LANES]


def _norm_mm(xs, gain, w, res, *, tm, tn, name, out_dtype=F32):
    t = xs[0].shape[0]
    k = sum(x.shape[1] for x in xs)
    n = w.shape[1]
    ng = len(xs)
    has_res = res is not None

    def body(*refs):
        x_refs = refs[:ng]
        g_ref, w_ref = refs[ng], refs[ng + 1]
        res_ref = refs[ng + 2] if has_res else None
        hn_ref, o_ref, hn_s = refs[ng + 2 + has_res:]

        @pl.when(pl.program_id(1) == 0)
        def _():
            off = 0
            for xr in x_refs:
                x = xr[...]
                wd = x.shape[1]
                r = lax.rsqrt(jnp.mean(x * x, axis=-1, keepdims=True) + EPS)
                hn_s[:, off:off + wd] = (x * r * g_ref[:, off:off + wd]).astype(BF16)
                off += wd
            hn_ref[...] = hn_s[...]

        acc = jnp.dot(hn_s[...], w_ref[...], preferred_element_type=F32)
        if has_res:
            acc = acc + res_ref[...]
        o_ref[...] = acc.astype(out_dtype)

    in_specs = [pl.BlockSpec((tm, x.shape[1]), lambda i, j: (i, 0)) for x in xs]
    in_specs += [pl.BlockSpec((1, k), lambda i, j: (0, 0)), pl.BlockSpec((k, tn), lambda i, j: (0, j))]
    args = list(xs) + [gain, w]
    if has_res:
        in_specs.append(pl.BlockSpec((tm, tn), lambda i, j: (i, j)))
        args.append(res)
    return pl.pallas_call(
        body, name=name, grid=(t // tm, n // tn), in_specs=in_specs,
        out_specs=[pl.BlockSpec((tm, k), lambda i, j: (i, 0)), pl.BlockSpec((tm, tn), lambda i, j: (i, j))],
        out_shape=[_sds((t, k), BF16), _sds((t, n), out_dtype)],
        scratch_shapes=[pltpu.VMEM((tm, k), BF16)],
        compiler_params=_cparams(PAR, ARB),
    )(*args)


def _mm(a, b, mode, res, *, tm, tn, out_dtype, name, a_rows=None):
    if mode == "tn":
        kk, m = a.shape
        blk_a = 0
        if a_rows is not None:
            blk_a, kk = a_rows
        a_spec = pl.BlockSpec((kk, tm), lambda i, j: (blk_a, i))
    else:
        m, kk = a.shape
        a_spec = pl.BlockSpec((tm, kk), lambda i, j: (i, 0))
    if mode == "nt":
        n = b.shape[0]
        b_spec = pl.BlockSpec((tn, kk), lambda i, j: (j, 0))
    else:
        n = b.shape[1]
        b_spec = pl.BlockSpec((kk, tn), lambda i, j: (0, j))
    has_res = res is not None

    def body(*refs):
        a_ref, b_ref = refs[0], refs[1]
        o_ref = refs[-1]
        av = a_ref[...].astype(BF16)
        bv = b_ref[...].astype(BF16)
        if mode == "nn":
            acc = jnp.dot(av, bv, preferred_element_type=F32)
        elif mode == "nt":
            acc = lax.dot_general(av, bv, NT, preferred_element_type=F32)
        else:
            acc = lax.dot_general(av, bv, TN, preferred_element_type=F32)
        if has_res:
            acc = acc + refs[2][...]
        o_ref[...] = acc.astype(out_dtype)

    in_specs = [a_spec, b_spec]
    args = [a, b]
    if has_res:
        in_specs.append(pl.BlockSpec((tm, tn), lambda i, j: (i, j)))
        args.append(res)
    return pl.pallas_call(
        body, name=name, grid=(m // tm, n // tn), in_specs=in_specs,
        out_specs=pl.BlockSpec((tm, tn), lambda i, j: (i, j)),
        out_shape=_sds((m, n), out_dtype),
        compiler_params=_cparams(PAR, PAR),
    )(*args)


def _mm_bt_normbwd(dys, w, xs, gain, dres, *, tm, tn, name, emit_bf16=False):
    t, wd_each = dys[0].shape
    nd = len(dys)
    per = wd_each // tn
    nj = nd * per
    k = w.shape[0]
    ng = len(xs)
    has_res = dres is not None

    def body(*refs):
        dy_refs = refs[:nd]
        w_ref = refs[nd]
        x_refs = refs[nd + 1:nd + 1 + ng]
        g_ref = refs[nd + 1 + ng]
        dres_ref = refs[nd + 2 + ng] if has_res else None
        outs = refs[nd + 2 + ng + has_res:]
        dx_ref = outs[0]
        dxb_ref = outs[1] if emit_bf16 else None
        dg_ref, acc = outs[1 + emit_bf16:]
        i, j = pl.program_id(0), pl.program_id(1)

        @pl.when(j == 0)
        def _():
            acc[...] = jnp.zeros_like(acc)

        for d, dy_ref in enumerate(dy_refs):
            @pl.when((j >= d * per) & (j < (d + 1) * per))
            def _(dy_ref=dy_ref):
                acc[...] += lax.dot_general(dy_ref[...].astype(BF16), w_ref[...], NT, preferred_element_type=F32)

        @pl.when(j == nj - 1)
        def _():
            @pl.when(i == 0)
            def _():
                dg_ref[...] = jnp.zeros_like(dg_ref)

            off = 0
            for xr in x_refs:
                x = xr[...]
                wd = x.shape[1]
                g = g_ref[:, off:off + wd]
                dyn = acc[:, off:off + wd]
                r = lax.rsqrt(jnp.mean(x * x, axis=-1, keepdims=True) + EPS)
                gdy = dyn * g
                dx = r * gdy - x * (r * r * r * jnp.mean(gdy * x, axis=-1, keepdims=True))
                if has_res:
                    dx = dx + dres_ref[:, off:off + wd]
                dx_ref[:, off:off + wd] = dx
                if emit_bf16:
                    dxb_ref[:, off:off + wd] = dx.astype(BF16)
                dg_ref[:, off:off + wd] += jnp.sum(dyn * x * r, axis=0, keepdims=True)
                off += wd

    def dy_map(d):
        return lambda i, j: (i, jnp.clip(j - d * per, 0, per - 1))

    in_specs = [pl.BlockSpec((tm, tn), dy_map(d)) for d in range(nd)]
    in_specs.append(pl.BlockSpec((k, tn), lambda i, j: (0, j)))
    in_specs += [pl.BlockSpec((tm, x.shape[1]), lambda i, j: (i, 0)) for x in xs]
    in_specs.append(pl.BlockSpec((1, k), lambda i, j: (0, 0)))
    args = list(dys) + [w] + list(xs) + [gain]
    if has_res:
        in_specs.append(pl.BlockSpec((tm, k), lambda i, j: (i, 0)))
        args.append(dres)
    row = pl.BlockSpec((tm, k), lambda i, j: (i, 0))
    out_specs = [row] + ([row] if emit_bf16 else []) + [pl.BlockSpec((1, k), lambda i, j: (0, 0))]
    out_shape = [_sds((t, k), F32)] + ([_sds((t, k), BF16)] if emit_bf16 else []) + [_sds((1, k), F32)]
    return pl.pallas_call(
        body, name=name, grid=(t // tm, nj), in_specs=in_specs, out_specs=out_specs, out_shape=out_shape,
        scratch_shapes=[pltpu.VMEM((tm, k), F32)],
        compiler_params=_cparams(ARB, ARB),
    )(*args)


def _rope_partner(y):
    n = y.shape[1]
    lane = lax.broadcasted_iota(jnp.int32, y.shape, 1)
    return jnp.where((lane & 31) < 16, pltpu.roll(y, n - 16, 1), pltpu.roll(y, 16, 1))


def _residue_specs(tm, width, nt):
    specs = [pl.BlockSpec((tm, width), lambda b, i: (b * nt + i, 0))]
    for d in DILATIONS[1:]:
        specs.append(pl.BlockSpec((None, d, tm // d, width), lambda b, i: (b, 0, i, 0)))
    return specs


def _residue_shapes(bl, width, dtype):
    return [_sds((bl * SEQ, width), dtype)] + [_sds((bl, d, SEQ // d, width), dtype) for d in DILATIONS[1:]]


def _qkprep_fwd(proj, gains, cos, sins, *, tm, name):
    t = proj.shape[0]
    bl = t // SEQ
    nt = SEQ // tm

    def body(p_ref, g_ref, c_ref, s_ref, qa1_ref, qa4_ref, qa16_ref, qb_ref, qd_ref, scr):
        e = _group_sum_matrix(256, True)

        def hn(x, row):
            wd = x.shape[1]
            ms = _seg_sum(x * x, e[:wd, :wd]) * (1.0 / HEAD_DIM)
            return x * lax.rsqrt(ms + EPS) * g_ref[row:row + 1, :wd]

        qa = jnp.concatenate([hn(p_ref[:, 0:256], 0) * ATTN_SCALE, hn(p_ref[:, 256:512], 1), p_ref[:, 512:768]], axis=1)
        qa1_ref[...] = qa.astype(BF16)
        _scatter_cols(scr, 0, qa)
        for d, ref in ((4, qa4_ref), (16, qa16_ref)):
            for r in range(d):
                ref[r] = _read_residue(scr, 0, 6, r, d).astype(BF16)
        qb_ref[:, 0:256] = (hn(p_ref[:, 768:1024], 2) * ATTN_SCALE).astype(BF16)
        qb_ref[:, 256:384] = hn(p_ref[:, 1024:1152], 3).astype(BF16)
        qb_ref[:, 384:512] = p_ref[:, 1152:1280].astype(BF16)
        yq = hn(p_ref[:, 1792:2048], 4)
        yq = yq * c_ref[...] + _rope_partner(yq) * s_ref[...]
        qd_ref[:, 0:256] = (yq * ATTN_SCALE).astype(BF16)
        yk = hn(p_ref[:, 2048:2176], 5)
        yk = yk * c_ref[:, 0:128] + _rope_partner(yk) * s_ref[:, 0:128]
        qd_ref[:, 256:384] = yk.astype(BF16)
        qd_ref[:, 384:512] = p_ref[:, 2176:2304].astype(BF16)

    row = lambda width: pl.BlockSpec((tm, width), lambda b, i: (b * nt + i, 0))
    tab = pl.BlockSpec((tm, 256), lambda b, i: (i, 0))
    return pl.pallas_call(
        body, name=name, grid=(bl, nt),
        in_specs=[row(IN_WIDTH), pl.BlockSpec((8, 256), lambda b, i: (0, 0)), tab, tab],
        out_specs=_residue_specs(tm, 768, nt) + [row(512), row(512)],
        out_shape=_residue_shapes(bl, 768, BF16) + [_sds((t, 512), BF16), _sds((t, 512), BF16)],
        scratch_shapes=[pltpu.VMEM((6, tm, LANES), F32)],
        compiler_params=_cparams(PAR, PAR),
    )(proj, gains, cos, sins)


def _qkprep_bwd(proj, da, db, dd, dcu, dcv, gains, cos, sins, *, tm, name):
    t = proj.shape[0]
    bl = t // SEQ
    nt = SEQ // tm
    flat = [a for cfg in da for a in cfg] + list(db) + list(dd) + [dcu, dcv]

    def body(*refs):
        p_ref, g_ref, c_ref, s_ref = refs[:4]
        d_refs = refs[4:4 + len(flat)]
        dp_ref, dg_ref, scr = refs[4 + len(flat):]
        a_refs = d_refs[:9]
        dqb_ref, dkb_ref, dvb_ref, dqd_ref, dkd_ref, dvd_ref, dcu_ref, dcv_ref = d_refs[9:]
        e = _group_sum_matrix(256, True)
        first = (pl.program_id(0) == 0) & (pl.program_id(1) == 0)
        last = (pl.program_id(0) == bl - 1) & (pl.program_id(1) == nt - 1)

        @pl.when(first)
        def _():
            dg_ref[...] = jnp.zeros_like(dg_ref)

        def hn_bwd(x, dy, row):
            wd = x.shape[1]
            ee = e[:wd, :wd]
            g = g_ref[row:row + 1, :wd]
            r = lax.rsqrt(_seg_sum(x * x, ee) * (1.0 / HEAD_DIM) + EPS)
            gdy = dy * g
            dx = r * gdy - x * (r * r * r * (_seg_sum(gdy * x, ee) * (1.0 / HEAD_DIM)))
            dg_ref[row:row + 1, :wd] += jnp.sum(dy * x * r, axis=0, keepdims=True)
            return dx

        def rope_bwd(dy, wd):
            return dy * c_ref[:, :wd] + _rope_partner(dy * s_ref[:, :wd])

        dqkv = jnp.concatenate([a_refs[0][...], a_refs[1][...], a_refs[2][...]], axis=1)
        for ci, d in ((1, 4), (2, 16)):
            for r in range(d):
                part = jnp.concatenate([a_refs[3 * ci + m][r] for m in range(3)], axis=1)
                _write_residue(scr, 0, r, d, part)
            dqkv = dqkv + _gather_cols(scr, 0, 6)
        dp_ref[:, 0:256] = hn_bwd(p_ref[:, 0:256], dqkv[:, 0:256] * ATTN_SCALE, 0).astype(BF16)
        dp_ref[:, 256:512] = hn_bwd(p_ref[:, 256:512], dqkv[:, 256:512], 1).astype(BF16)
        dp_ref[:, 512:768] = dqkv[:, 512:768].astype(BF16)
        dp_ref[:, 768:1024] = hn_bwd(p_ref[:, 768:1024], dqb_ref[...] * ATTN_SCALE, 2).astype(BF16)
        dp_ref[:, 1024:1152] = hn_bwd(p_ref[:, 1024:1152], dkb_ref[...], 3).astype(BF16)
        dp_ref[:, 1152:1280] = dvb_ref[...].astype(BF16)
        dp_ref[:, 1280:1536] = dcu_ref[...].astype(BF16)
        dp_ref[:, 1536:1792] = dcv_ref[...].astype(BF16)
        dp_ref[:, 1792:2048] = hn_bwd(p_ref[:, 1792:2048], rope_bwd(dqd_ref[...] * ATTN_SCALE, 256), 4).astype(BF16)
        dp_ref[:, 2048:2176] = hn_bwd(p_ref[:, 2048:2176], rope_bwd(dkd_ref[...], 128), 5).astype(BF16)
        dp_ref[:, 2176:2304] = dvd_ref[...].astype(BF16)

        @pl.when(last)
        def _():
            dg_ref[...] = _seg_sum(dg_ref[...], _group_sum_matrix(256, False))

    row = lambda width: pl.BlockSpec((tm, width), lambda b, i: (b * nt + i, 0))
    tab = pl.BlockSpec((tm, 256), lambda b, i: (i, 0))
    in_specs = [row(IN_WIDTH), pl.BlockSpec((8, 256), lambda b, i: (0, 0)), tab, tab]
    res_specs = _residue_specs(tm, 256, nt)
    in_specs += [res_specs[ci] for ci in range(3) for _ in range(3)]
    in_specs += [row(a.shape[1]) for a in flat[9:]]
    return pl.pallas_call(
        body, name=name, grid=(bl, nt), in_specs=in_specs,
        out_specs=[row(IN_WIDTH), pl.BlockSpec((8, 256), lambda b, i: (0, 0))],
        out_shape=[_sds((t, IN_WIDTH), BF16), _sds((8, 256), F32)],
        scratch_shapes=[pltpu.VMEM((6, tm, LANES), F32)],
        compiler_params=_cparams(ARB, ARB),
    )(proj, gains, cos, sins, *flat)


def _band_spec(seq_len, spec):
    width, idx = spec
    return pl.BlockSpec((None, None, seq_len, width), lambda b, r: (b, r, 0, idx))


def _fill_padded(dst, src_ref, rad, seq_len):
    z = jnp.zeros((rad, dst.shape[1]), dst.dtype)
    dst[0:rad, :] = z
    dst[rad + seq_len:rad + seq_len + rad, :] = z
    dst[rad:rad + seq_len, :] = src_ref[...]


def _band_fwd(src, qs, ks, vs, bias, sink, *, rad, nh, nkv, name):
    bl, dil, sl, _ = src.shape
    blk = bias.shape[1]
    kw = blk + 2 * rad
    nb = sl // blk
    rep = nh // nkv
    has_sink = sink is not None

    def body(*refs):
        q_ref, k_ref, v_ref, b_ref = refs[:4]
        s_ref = refs[4] if has_sink else None
        o_ref, l_ref, kp, vp = refs[4 + has_sink:]
        _fill_padded(kp, k_ref, rad, sl)
        _fill_padded(vp, v_ref, rad, sl)

        def blk_body(i, carry):
            r0 = pl.multiple_of(i * blk, blk)
            qb = q_ref[pl.ds(r0, blk), :]
            kwin = kp[pl.ds(r0, kw), :]
            vwin = vp[pl.ds(r0, kw), :]
            col = r0 - rad + lax.broadcasted_iota(jnp.int32, (blk, kw), 1)
            neg = jnp.where((col >= 0) & (col < sl), 0.0, NEG_INF).astype(F32)
            for h in range(nh):
                g = h // rep
                hs = slice(h * HEAD_DIM, (h + 1) * HEAD_DIM)
                gs = slice(g * HEAD_DIM, (g + 1) * HEAD_DIM)
                s = lax.dot_general(qb[:, hs], kwin[:, gs], NT, preferred_element_type=F32)
                s = s + b_ref[h] + neg
                m = jnp.max(s, axis=1, keepdims=True)
                if has_sink:
                    sk = s_ref[h][0:1, 0:1]
                    m = jnp.maximum(m, sk)
                p = jnp.exp(s - m)
                den = jnp.sum(p, axis=1, keepdims=True)
                if has_sink:
                    den = den + jnp.exp(sk - m)
                o = jnp.dot(p.astype(BF16), vwin[:, gs], preferred_element_type=F32) / den
                o_ref[pl.ds(r0, blk), hs] = o
                l_ref[pl.ds(r0, blk), hs] = jnp.broadcast_to(m + jnp.log(den), (blk, HEAD_DIM))
            return carry

        lax.fori_loop(0, nb, blk_body, 0)

    in_specs = [_band_spec(sl, qs), _band_spec(sl, ks), _band_spec(sl, vs),
                pl.BlockSpec((nh, blk, kw), lambda b, r: (0, 0, 0))]
    args = [src] * 3 + [bias]
    if has_sink:
        in_specs.append(pl.BlockSpec((nh, 8, 128), lambda b, r: (0, 0, 0)))
        args.append(sink)
    return pl.pallas_call(
        body, name=name, grid=(bl, dil), in_specs=in_specs,
        out_specs=[_band_spec(sl, (256, 0))] * 2,
        out_shape=[_sds((bl, dil, sl, 256), F32)] * 2,
        scratch_shapes=[pltpu.VMEM((sl + 2 * rad, ks[0]), BF16), pltpu.VMEM((sl + 2 * rad, vs[0]), BF16)],
        compiler_params=_cparams(PAR, PAR),
    )(*args)


def _band_bwd(src, qs, ks, vs, bias, sink, dy, dcol, lse, delta, *, rad, nh, nkv, name):
    bl, dil, sl, _ = src.shape
    blk = bias.shape[1]
    kw = blk + 2 * rad
    nb = sl // blk
    rep = nh // nkv
    has_sink = sink is not None
    wk, wv = ks[0], vs[0]

    def body(*refs):
        q_ref, k_ref, v_ref, b_ref = refs[:4]
        s_ref = refs[4] if has_sink else None
        do_ref, l_ref, dl_ref = refs[4 + has_sink:7 + has_sink]
        outs = refs[7 + has_sink:]
        if has_sink:
            dq_ref, dk_ref, dv_ref, db_ref, dsk_ref, kp, vp, dka, dva = outs
        else:
            dq_ref, dk_ref, dv_ref, db_ref, kp, vp, dka, dva = outs

        @pl.when((pl.program_id(0) == 0) & (pl.program_id(1) == 0))
        def _():
            db_ref[...] = jnp.zeros_like(db_ref)
            if has_sink:
                dsk_ref[...] = jnp.zeros_like(dsk_ref)

        _fill_padded(kp, k_ref, rad, sl)
        _fill_padded(vp, v_ref, rad, sl)
        dka[...] = jnp.zeros_like(dka)
        dva[...] = jnp.zeros_like(dva)

        def blk_body(i, carry):
            r0 = pl.multiple_of(i * blk, blk)
            qb = q_ref[pl.ds(r0, blk), :]
            kwin = kp[pl.ds(r0, kw), :]
            vwin = vp[pl.ds(r0, kw), :]
            dob = do_ref[pl.ds(r0, blk), :].astype(BF16)
            lb = l_ref[pl.ds(r0, blk), :]
            dlb = dl_ref[pl.ds(r0, blk), :]
            col = r0 - rad + lax.broadcasted_iota(jnp.int32, (blk, kw), 1)
            neg = jnp.where((col >= 0) & (col < sl), 0.0, NEG_INF).astype(F32)
            for h in range(nh):
                g = h // rep
                hs = slice(h * HEAD_DIM, (h + 1) * HEAD_DIM)
                gs = slice(g * HEAD_DIM, (g + 1) * HEAD_DIM)
                qh, kh, vh, doh = qb[:, hs], kwin[:, gs], vwin[:, gs], dob[:, hs]
                lh = lb[:, h * HEAD_DIM:h * HEAD_DIM + 1]
                dlh = dlb[:, h * HEAD_DIM:h * HEAD_DIM + 1]
                s = lax.dot_general(qh, kh, NT, preferred_element_type=F32) + b_ref[h] + neg
                p = jnp.exp(s - lh)
                dp = lax.dot_general(doh, vh, NT, preferred_element_type=F32)
                ds = p * (dp - dlh)
                dsb = ds.astype(BF16)
                dq_ref[pl.ds(r0, blk), hs] = jnp.dot(dsb, kh, preferred_element_type=F32)
                dka[pl.ds(r0, kw), gs] += lax.dot_general(dsb, qh, TN, preferred_element_type=F32)
                dva[pl.ds(r0, kw), gs] += lax.dot_general(p.astype(BF16), doh, TN, preferred_element_type=F32)
                db_ref[h] += ds
                if has_sink:
                    ps = jnp.exp(s_ref[h][0:1, 0:1] - lh)
                    dsk_ref[h] += jnp.broadcast_to(-jnp.sum(ps * dlh, axis=0, keepdims=True), (8, 128))
            return carry

        lax.fori_loop(0, nb, blk_body, 0)
        dk_ref[...] = dka[rad:rad + sl, :]
        dv_ref[...] = dva[rad:rad + sl, :]

    const3 = lambda b, r: (0, 0, 0)
    in_specs = [_band_spec(sl, qs), _band_spec(sl, ks), _band_spec(sl, vs), pl.BlockSpec((nh, blk, kw), const3)]
    args = [src] * 3 + [bias]
    if has_sink:
        in_specs.append(pl.BlockSpec((nh, 8, 128), const3))
        args.append(sink)
    row = _band_spec(sl, (256, 0))
    in_specs += [_band_spec(sl, (256, dcol)), row, row]
    args += [dy, lse, delta]
    out_specs = [row, _band_spec(sl, (wk, 0)), _band_spec(sl, (wv, 0)), pl.BlockSpec((nh, blk, kw), const3)]
    out_shape = [_sds((bl, dil, sl, 256), F32), _sds((bl, dil, sl, wk), F32), _sds((bl, dil, sl, wv), F32),
                 _sds((nh, blk, kw), F32)]
    if has_sink:
        out_specs.append(pl.BlockSpec((nh, 8, 128), const3))
        out_shape.append(_sds((nh, 8, 128), F32))
    return pl.pallas_call(
        body, name=name, grid=(bl, dil), in_specs=in_specs, out_specs=out_specs, out_shape=out_shape,
        scratch_shapes=[pltpu.VMEM((sl + 2 * rad, wk), BF16), pltpu.VMEM((sl + 2 * rad, wv), BF16),
                        pltpu.VMEM((sl + 2 * rad, wk), F32), pltpu.VMEM((sl + 2 * rad, wv), F32)],
        compiler_params=_cparams(ARB, ARB),
    )(*args)


def _combine_a(os_, ls_, *, tm, name):
    bl = os_[1].shape[0]
    t = bl * SEQ
    nt = SEQ // tm

    def body(o1, o4, o16, l1, l4, l16, y_ref, lt_ref, scr):
        for k, (d, ref) in enumerate(((4, o4), (16, o16), (4, l4), (16, l16))):
            for r in range(d):
                _write_residue(scr, 2 * k, r, d, ref[r])
        o2, o3, b, c = (_gather_cols(scr, 2 * k, 2) for k in range(4))
        a = l1[...]
        m = jnp.maximum(jnp.maximum(a, b), c)
        ea, eb, ec = jnp.exp(a - m), jnp.exp(b - m), jnp.exp(c - m)
        den = ea + eb + ec
        y_ref[...] = (ea / den) * o1[...] + (eb / den) * o2 + (ec / den) * o3
        lt_ref[...] = m + jnp.log(den)

    specs = _residue_specs(tm, 256, nt)
    return pl.pallas_call(
        body, name=name, grid=(bl, nt), in_specs=specs * 2, out_specs=[specs[0]] * 2,
        out_shape=[_sds((t, 256), F32)] * 2, scratch_shapes=[pltpu.VMEM((8, tm, LANES), F32)],
        compiler_params=_cparams(PAR, PAR),
    )(*os_, *ls_)


def _deltas(dycat, ya, yb, yd, lse_a, *, tm, name):
    t = ya.shape[0]
    bl = t // SEQ
    nt = SEQ // tm

    def body(dy_ref, ya_ref, yb_ref, yd_ref, la_ref, dy4, dy16, l4, l16, da1, da4, da16, db_ref, dd_ref, scr):
        e = _group_sum_matrix(256, True)
        dya = dy_ref[:, 0:256]
        dla = _seg_sum(dya * ya_ref[...], e)
        da1[...] = dla
        db_ref[...] = _seg_sum(dy_ref[:, 256:512] * yb_ref[...], e)
        dd_ref[...] = _seg_sum(dy_ref[:, 768:1024] * yd_ref[...], e)
        for k, (val, r4, r16) in enumerate(((dya, dy4, dy16), (la_ref[...], l4, l16), (dla, da4, da16))):
            _scatter_cols(scr, 2 * k, val)
            for d, ref in ((4, r4), (16, r16)):
                for r in range(d):
                    ref[r] = _read_residue(scr, 2 * k, 2, r, d)

    specs = _residue_specs(tm, 256, nt)
    nat = specs[0]
    shapes = _residue_shapes(bl, 256, F32)
    outs = pl.pallas_call(
        body, name=name, grid=(bl, nt),
        in_specs=[pl.BlockSpec((tm, 1024), lambda b, i: (b * nt + i, 0)), nat, nat, nat, nat],
        out_specs=specs[1:] + specs[1:] + specs + [nat, nat],
        out_shape=shapes[1:] + shapes[1:] + shapes + [shapes[0], shapes[0]],
        scratch_shapes=[pltpu.VMEM((6, tm, LANES), F32)],
        compiler_params=_cparams(PAR, PAR),
    )(dycat, ya, yb, yd, lse_a)
    return outs[0:2], outs[2:4], outs[4:7], outs[7], outs[8]


def _dense_fwd(qd, *, tq, name):
    t = qd.shape[0]
    bl = t // SEQ
    nq = SEQ // tq

    def body(q_ref, k_ref, v_ref, o_ref, l_ref):
        q = q_ref[...]
        for g in range(2):
            h0, h1 = 2 * g, 2 * g + 1
            q2 = jnp.concatenate([q[:, h0 * 64:(h0 + 1) * 64], q[:, h1 * 64:(h1 + 1) * 64]], axis=0)
            kg = k_ref[:, g * 64:(g + 1) * 64]
            vg = v_ref[:, g * 64:(g + 1) * 64]
            s = lax.dot_general(q2, kg, NT, preferred_element_type=F32)
            m = jnp.max(s, axis=1, keepdims=True)
            p = jnp.exp(s - m)
            den = jnp.sum(p, axis=1, keepdims=True)
            o2 = jnp.dot(p.astype(BF16), vg, preferred_element_type=F32) / den
            l2 = jnp.broadcast_to(m + jnp.log(den), (2 * tq, 64))
            o_ref[:, h0 * 64:(h0 + 1) * 64] = o2[:tq]
            o_ref[:, h1 * 64:(h1 + 1) * 64] = o2[tq:]
            l_ref[:, h0 * 64:(h0 + 1) * 64] = l2[:tq]
            l_ref[:, h1 * 64:(h1 + 1) * 64] = l2[tq:]

    q3 = qd.reshape(bl, SEQ, 512)
    o, lse = pl.pallas_call(
        body, name=name, grid=(bl, nq),
        in_specs=[pl.BlockSpec((None, tq, 256), lambda b, i: (b, i, 0)),
                  pl.BlockSpec((None, SEQ, 128), lambda b, i: (b, 0, 2)),
                  pl.BlockSpec((None, SEQ, 128), lambda b, i: (b, 0, 3))],
        out_specs=[pl.BlockSpec((None, tq, 256), lambda b, i: (b, i, 0))] * 2,
        out_shape=[_sds((bl, SEQ, 256), F32)] * 2,
        compiler_params=_cparams(PAR, PAR),
    )(q3, q3, q3)
    return o.reshape(t, 256), lse.reshape(t, 256)


def _dense_bwd(qd, dycat, lse, delta, *, tq, name):
    t = qd.shape[0]
    bl = t // SEQ
    nq = SEQ // tq

    def body(q_ref, k_ref, v_ref, do_ref, l_ref, dl_ref, dq_ref, dk_ref, dv_ref):
        @pl.when(pl.program_id(1) == 0)
        def _():
            dk_ref[...] = jnp.zeros_like(dk_ref)
            dv_ref[...] = jnp.zeros_like(dv_ref)

        q = q_ref[...]
        do = do_ref[...].astype(BF16)
        lv = l_ref[...]
        dlv = dl_ref[...]
        for g in range(2):
            h0, h1 = 2 * g, 2 * g + 1
            q2 = jnp.concatenate([q[:, h0 * 64:(h0 + 1) * 64], q[:, h1 * 64:(h1 + 1) * 64]], axis=0)
            do2 = jnp.concatenate([do[:, h0 * 64:(h0 + 1) * 64], do[:, h1 * 64:(h1 + 1) * 64]], axis=0)
            l2 = jnp.concatenate([lv[:, h0 * 64:h0 * 64 + 1], lv[:, h1 * 64:h1 * 64 + 1]], axis=0)
            dl2 = jnp.concatenate([dlv[:, h0 * 64:h0 * 64 + 1], dlv[:, h1 * 64:h1 * 64 + 1]], axis=0)
            kg = k_ref[:, g * 64:(g + 1) * 64]
            vg = v_ref[:, g * 64:(g + 1) * 64]
            s = lax.dot_general(q2, kg, NT, preferred_element_type=F32)
            p = jnp.exp(s - l2)
            dp = lax.dot_general(do2, vg, NT, preferred_element_type=F32)
            ds = (p * (dp - dl2)).astype(BF16)
            dq2 = jnp.dot(ds, kg, preferred_element_type=F32)
            dq_ref[:, h0 * 64:(h0 + 1) * 64] = dq2[:tq]
            dq_ref[:, h1 * 64:(h1 + 1) * 64] = dq2[tq:]
            dk_ref[:, g * 64:(g + 1) * 64] += lax.dot_general(ds, q2, TN, preferred_element_type=F32)
            dv_ref[:, g * 64:(g + 1) * 64] += lax.dot_general(p.astype(BF16), do2, TN, preferred_element_type=F32)

    q3 = qd.reshape(bl, SEQ, 512)
    tile = pl.BlockSpec((None, tq, 256), lambda b, i: (b, i, 0))
    full = pl.BlockSpec((None, SEQ, 128), lambda b, i: (b, 0, 0))
    dq, dk, dv = pl.pallas_call(
        body, name=name, grid=(bl, nq),
        in_specs=[tile, pl.BlockSpec((None, SEQ, 128), lambda b, i: (b, 0, 2)),
                  pl.BlockSpec((None, SEQ, 128), lambda b, i: (b, 0, 3)),
                  pl.BlockSpec((None, tq, 256), lambda b, i: (b, i, 3)), tile, tile],
        out_specs=[tile, full, full],
        out_shape=[_sds((bl, SEQ, 256), F32), _sds((bl, SEQ, 128), F32), _sds((bl, SEQ, 128), F32)],
        compiler_params=_cparams(PAR, ARB),
    )(q3, q3, q3, dycat.reshape(bl, SEQ, 1024), lse.reshape(bl, SEQ, 256), delta.reshape(bl, SEQ, 256))
    return dq.reshape(t, 256), dk.reshape(t, 128), dv.reshape(t, 128)


def _c_norm(cv, gam, bet):
    vg = _gelu(cv)
    mu = jnp.mean(vg, axis=-1, keepdims=True)
    xc = vg - mu
    r = lax.rsqrt(jnp.mean(xc * xc, axis=-1, keepdims=True) + EPS)
    xhat = xc * r
    return xhat * gam + bet, xhat, r


def _c_fwd(proj, gam, bet, ws, bst, *, tm, name):
    t = proj.shape[0]
    nch = tm // C_CHUNK

    def body(u_ref, v_ref, g_ref, b_ref, ws_ref, bs_ref, y_ref):
        vn, _, _ = _c_norm(v_ref[...], g_ref[...], b_ref[...])
        vnb = vn.astype(BF16)
        for c in range(nch):
            rows = slice(c * C_CHUNK, (c + 1) * C_CHUNK)
            for g in range(C_GROUPS):
                gs = slice(g * 64, (g + 1) * 64)
                mixed = jnp.dot(ws_ref[g], vnb[rows, gs], preferred_element_type=F32) + bs_ref[:, gs]
                y_ref[rows, gs] = _gelu(u_ref[rows, gs]) * mixed

    vec = pl.BlockSpec((1, 256), lambda i: (0, 0))
    return pl.pallas_call(
        body, name=name, grid=(t // tm,),
        in_specs=[pl.BlockSpec((tm, 256), lambda i: (i, 5)), pl.BlockSpec((tm, 256), lambda i: (i, 6)), vec, vec,
                  pl.BlockSpec((C_GROUPS, C_CHUNK, C_CHUNK), lambda i: (0, 0, 0)),
                  pl.BlockSpec((C_CHUNK, 256), lambda i: (0, 0))],
        out_specs=pl.BlockSpec((tm, 256), lambda i: (i, 0)), out_shape=_sds((t, 256), F32),
        compiler_params=_cparams(PAR),
    )(proj, proj, gam, bet, ws, bst)


def _c_bwd(proj, dycat, gam, bet, ws, wst, bst, *, tm, name):
    t = proj.shape[0]
    nch = tm // C_CHUNK
    nstep = t // tm

    def body(u_ref, v_ref, dy_ref, g_ref, b_ref, ws_ref, wst_ref, bs_ref,
             du_ref, dv_ref, dws_ref, dbs_ref, dg_ref, db_ref, dvn_s):
        step = pl.program_id(0)

        @pl.when(step == 0)
        def _():
            dws_ref[...] = jnp.zeros_like(dws_ref)
            dbs_ref[...] = jnp.zeros_like(dbs_ref)
            dg_ref[...] = jnp.zeros_like(dg_ref)
            db_ref[...] = jnp.zeros_like(db_ref)

        cv = v_ref[...]
        gam_v = g_ref[...]
        vn, xhat, r = _c_norm(cv, gam_v, b_ref[...])
        vnb = vn.astype(BF16)
        for c in range(nch):
            rows = slice(c * C_CHUNK, (c + 1) * C_CHUNK)
            for g in range(C_GROUPS):
                gs = slice(g * 64, (g + 1) * 64)
                cu = u_ref[rows, gs]
                dy = dy_ref[rows, gs]
                mixed = jnp.dot(ws_ref[g], vnb[rows, gs], preferred_element_type=F32) + bs_ref[:, gs]
                du_ref[rows, gs] = dy * mixed * _gelu_grad(cu)
                dmix = dy * _gelu(cu)
                dbs_ref[:, gs] += dmix
                dmb = dmix.astype(BF16)
                dws_ref[g] += lax.dot_general(dmb, vnb[rows, gs], NT, preferred_element_type=F32)
                dvn_s[rows, gs] = jnp.dot(wst_ref[g], dmb, preferred_element_type=F32)
        dvn = dvn_s[...]
        dg_ref[...] += jnp.sum(dvn * xhat, axis=0, keepdims=True)
        db_ref[...] += jnp.sum(dvn, axis=0, keepdims=True)
        dxh = dvn * gam_v
        dvg = r * (dxh - jnp.mean(dxh, axis=-1, keepdims=True) - xhat * jnp.mean(dxh * xhat, axis=-1, keepdims=True))
        dv_ref[...] = dvg * _gelu_grad(cv)

        @pl.when(step == nstep - 1)
        def _():
            dbs_ref[...] = _seg_sum(dbs_ref[...], _group_sum_matrix(256, True))

    vec = pl.BlockSpec((1, 256), lambda i: (0, 0))
    mat = pl.BlockSpec((C_GROUPS, C_CHUNK, C_CHUNK), lambda i: (0, 0, 0))
    bsp = pl.BlockSpec((C_CHUNK, 256), lambda i: (0, 0))
    tile = pl.BlockSpec((tm, 256), lambda i: (i, 0))
    return pl.pallas_call(
        body, name=name, grid=(nstep,),
        in_specs=[pl.BlockSpec((tm, 256), lambda i: (i, 5)), pl.BlockSpec((tm, 256), lambda i: (i, 6)),
                  pl.BlockSpec((tm, 256), lambda i: (i, 2)), vec, vec, mat, mat, bsp],
        out_specs=[tile, tile, mat, bsp, vec, vec],
        out_shape=[_sds((t, 256), F32), _sds((t, 256), F32), _sds((C_GROUPS, C_CHUNK, C_CHUNK), F32),
                   _sds((C_CHUNK, 256), F32), _sds((1, 256), F32), _sds((1, 256), F32)],
        scratch_shapes=[pltpu.VMEM((tm, 256), F32)],
        compiler_params=_cparams(ARB),
    )(proj, proj, dycat, gam, bet, ws, wst, bst)


FF_TC = 128
FF_NB = D_FF // FF_TC
FF_CH = 64
FF_HALO = 16


def _edge_taps(ref, first):
    if first:
        ext = ref[0:FF_CH + FF_HALO, :].astype(F32)
        body = slice(0, FF_CH)
    else:
        ext = ref[SEQ - FF_CH - FF_HALO:SEQ, :].astype(F32)
        body = slice(FF_HALO, FF_HALO + FF_CH)
    n = ext.shape[0]
    row = lax.broadcasted_iota(jnp.int32, ext.shape, 0)
    dn = pltpu.roll(ext, 1, 0)
    up = pltpu.roll(ext, n - 1, 0)
    if first:
        dn = jnp.where(row == 0, 0.0, dn)
    else:
        up = jnp.where(row == n - 1, 0.0, up)
    return dn[body], ext[body], up[body]


def _mid_taps(ref, r0):
    ext = ref[pl.ds(pl.multiple_of(r0 - FF_HALO, FF_HALO), FF_CH + 2 * FF_HALO), :].astype(F32)
    n = ext.shape[0]
    body = slice(FF_HALO, FF_HALO + FF_CH)
    return pltpu.roll(ext, 1, 0)[body], ext[body], pltpu.roll(ext, n - 1, 0)[body]


def _chunk_loop(step):
    step(0, lambda ref: _edge_taps(ref, True))

    def mid(i, carry):
        r0 = pl.multiple_of(i * FF_CH, FF_CH)
        step(r0, lambda ref: _mid_taps(ref, r0))
        return carry

    lax.fori_loop(1, SEQ // FF_CH - 1, mid, 0)
    step(SEQ - FF_CH, lambda ref: _edge_taps(ref, False))


def _conv3(taps, w_ref, b_ref):
    dn, md, up = taps
    return w_ref[0:1, :] * dn + w_ref[1:2, :] * md + w_ref[2:3, :] * up + b_ref[...]


def _ff_specs(order):
    def at(fn):
        return (lambda b, j: fn(b, j)) if order == "bj" else (lambda j, b: fn(b, j))
    hs = [pl.BlockSpec((None, SEQ, FF_TC), at(lambda b, j, o=o: (b, 0, j + o))) for o in (0, FF_NB)]
    ws = [pl.BlockSpec((3, FF_TC), at(lambda b, j, o=o: (0, j + o))) for o in (0, FF_NB)]
    bs = [pl.BlockSpec((1, FF_TC), at(lambda b, j, o=o: (0, j + o))) for o in (0, FF_NB)]
    return hs, ws, bs


def _conv_gate_fwd(h, cw, cb, *, name):
    t = h.shape[0]
    bl = t // SEQ

    def body(hg_ref, hu_ref, wg_ref, wu_ref, bg_ref, bu_ref, a_ref):
        def step(r0, taps):
            cg = _conv3(taps(hg_ref), wg_ref, bg_ref)
            cu = _conv3(taps(hu_ref), wu_ref, bu_ref)
            a_ref[pl.ds(r0, FF_CH), :] = (cg * _sigmoid(cg) * cu).astype(BF16)

        _chunk_loop(step)

    hs, ws, bs = _ff_specs("bj")
    h3 = h.reshape(bl, SEQ, 2 * D_FF)
    act = pl.pallas_call(
        body, name=name, grid=(bl, FF_NB), in_specs=hs + ws + bs,
        out_specs=pl.BlockSpec((None, SEQ, FF_TC), lambda b, j: (b, 0, j)),
        out_shape=_sds((bl, SEQ, D_FF), BF16),
        compiler_params=_cparams(PAR, PAR),
    )(h3, h3, cw, cw, cb, cb)
    return act.reshape(t, D_FF)


def _conv_gate_bwd(h, dact, cw, cb, *, name):
    t = h.shape[0]
    bl = t // SEQ

    def body(hg_ref, hu_ref, wg_ref, wu_ref, bg_ref, bu_ref, da_ref,
             dhg_ref, dhu_ref, dwg_ref, dwu_ref, dbg_ref, dbu_ref, dg_s, du_s):
        @pl.when(pl.program_id(1) == 0)
        def _():
            for ref in (dwg_ref, dwu_ref, dbg_ref, dbu_ref):
                ref[...] = jnp.zeros_like(ref)

        red = lambda x: jnp.sum(x, axis=0, keepdims=True)

        def pass1(r0, taps):
            tg, tu = taps(hg_ref), taps(hu_ref)
            cg = _conv3(tg, wg_ref, bg_ref)
            cu = _conv3(tu, wu_ref, bu_ref)
            da = da_ref[pl.ds(r0, FF_CH), :].astype(F32)
            sg = _sigmoid(cg)
            dcg = da * cu * (sg * (1.0 + cg * (1.0 - sg)))
            dcu = da * (cg * sg)
            dg_s[pl.ds(r0, FF_CH), :] = dcg
            du_s[pl.ds(r0, FF_CH), :] = dcu
            for d, tp, dw_ref, db_ref in ((dcg, tg, dwg_ref, dbg_ref), (dcu, tu, dwu_ref, dbu_ref)):
                for k in range(3):
                    dw_ref[k:k + 1, :] += red(d * tp[k])
                db_ref[...] += red(d)

        _chunk_loop(pass1)

        def pass2(r0, taps):
            for s, w_ref, o_ref in ((dg_s, wg_ref, dhg_ref), (du_s, wu_ref, dhu_ref)):
                dn, md, up = taps(s)
                o_ref[pl.ds(r0, FF_CH), :] = (w_ref[0:1, :] * up + w_ref[1:2, :] * md + w_ref[2:3, :] * dn).astype(BF16)

        _chunk_loop(pass2)

    hs, ws, bs = _ff_specs("jb")
    half = pl.BlockSpec((None, SEQ, FF_TC), lambda j, b: (b, 0, j))
    wsp = pl.BlockSpec((3, FF_TC), lambda j, b: (0, j))
    bsp = pl.BlockSpec((1, FF_TC), lambda j, b: (0, j))
    h3 = h.reshape(bl, SEQ, 2 * D_FF)
    dhg, dhu, dwg, dwu, dbg, dbu = pl.pallas_call(
        body, name=name, grid=(FF_NB, bl), in_specs=hs + ws + bs + [half],
        out_specs=[half, half, wsp, wsp, bsp, bsp],
        out_shape=[_sds((bl, SEQ, D_FF), BF16), _sds((bl, SEQ, D_FF), BF16), _sds((3, D_FF), F32), _sds((3, D_FF), F32),
                   _sds((1, D_FF), F32), _sds((1, D_FF), F32)],
        scratch_shapes=[pltpu.VMEM((SEQ, FF_TC), F32), pltpu.VMEM((SEQ, FF_TC), F32)],
        compiler_params=_cparams(PAR, ARB),
    )(h3, h3, cw, cw, cb, cb, dact.reshape(bl, SEQ, D_FF))
    return (dhg.reshape(t, D_FF), dhu.reshape(t, D_FF), jnp.concatenate([dwg, dwu], axis=1),
            jnp.concatenate([dbg, dbu], axis=1))


def _ple_fwd(x2, gain, wg, pe, pe_blk, wp, *, tm, tn, name):
    t, k = x2.shape
    n = wg.shape[1]

    def body(x_ref, g_ref, wg_ref, pe_ref, wp_ref, xr_ref, hn_ref, x3_ref, gt_ref, pp_ref, hn_s):
        @pl.when(pl.program_id(1) == 0)
        def _():
            x = x_ref[...]
            r = lax.rsqrt(jnp.mean(x * x, axis=-1, keepdims=True) + EPS)
            hn_s[...] = (x * r * g_ref[...]).astype(BF16)
            hn_ref[...] = hn_s[...]

        gate = _sigmoid(jnp.dot(hn_s[...], wg_ref[...], preferred_element_type=F32))
        pp = jnp.dot(pe_ref[...].astype(BF16), wp_ref[...], preferred_element_type=F32)
        gt_ref[...] = gate
        pp_ref[...] = pp
        x3_ref[...] = xr_ref[...] + pp * gate

    tile = pl.BlockSpec((tm, tn), lambda i, j: (i, j))
    return pl.pallas_call(
        body, name=name, grid=(t // tm, n // tn),
        in_specs=[pl.BlockSpec((tm, k), lambda i, j: (i, 0)), pl.BlockSpec((1, k), lambda i, j: (0, 0)),
                  pl.BlockSpec((k, tn), lambda i, j: (0, j)), pl.BlockSpec((tm, PLE_DIM), lambda i, j: (pe_blk + i, 0)),
                  pl.BlockSpec((PLE_DIM, tn), lambda i, j: (0, j)), tile],
        out_specs=[pl.BlockSpec((tm, k), lambda i, j: (i, 0)), tile, tile, tile],
        out_shape=[_sds((t, k), BF16), _sds((t, n), F32), _sds((t, n), F32), _sds((t, n), F32)],
        scratch_shapes=[pltpu.VMEM((tm, k), BF16)],
        compiler_params=_cparams(PAR, ARB),
    )(x2, gain, wg, pe, wp, x2)


def _ple_bwd_ew(dx3, gate, pp, *, tm, name):
    t, n = dx3.shape

    def body(d_ref, g_ref, p_ref, dz_ref, dpp_ref):
        d, g = d_ref[...], g_ref[...]
        dz_ref[...] = (d * p_ref[...] * g * (1.0 - g)).astype(BF16)
        dpp_ref[...] = (d * g).astype(BF16)

    spec = pl.BlockSpec((tm, n), lambda i: (i, 0))
    return pl.pallas_call(
        body, name=name, grid=(t // tm,), in_specs=[spec] * 3, out_specs=[spec] * 2,
        out_shape=[_sds((t, n), BF16)] * 2, compiler_params=_cparams(PAR),
    )(dx3, gate, pp)


def _loss_head(y, tgt, *, tm, name):
    t, d = y.shape

    def body(y_ref, t_ref, l_ref, dy_ref):
        @pl.when(pl.program_id(0) == 0)
        def _():
            l_ref[...] = jnp.zeros_like(l_ref)

        e = y_ref[...] - t_ref[...]
        dy_ref[...] = e * (1.0 / d)
        s = jnp.sum(jnp.sum(e * e, axis=1, keepdims=True), axis=0, keepdims=True)
        l_ref[...] += jnp.broadcast_to(s * (0.5 / d), (8, 128))

    spec = pl.BlockSpec((tm, d), lambda i: (i, 0))
    return pl.pallas_call(
        body, name=name, grid=(t // tm,), in_specs=[spec, spec],
        out_specs=[pl.BlockSpec((8, 128), lambda i: (0, 0)), spec],
        out_shape=[_sds((8, 128), F32), _sds((t, d), F32)], compiler_params=_cparams(ARB),
    )(y, tgt)


BIAS_PC = 8192


def _onehot(bucket_row):
    rows = lax.broadcasted_iota(jnp.int32, (REL_BUCKETS, bucket_row.shape[1]), 0)
    return (rows == bucket_row).astype(F32)


def _bias_lookup(table_t, bucket, *, name):
    h = table_t.shape[0]
    p = bucket.shape[1]

    def body(t_ref, b_ref, o_ref):
        bk = b_ref[...]
        val = jnp.dot(t_ref[...], _onehot(bk), precision=HI, preferred_element_type=F32)
        o_ref[...] = jnp.where(bk >= 0, val, NEG_INF)

    return pl.pallas_call(
        body, name=name, grid=(p // BIAS_PC,),
        in_specs=[pl.BlockSpec((h, REL_BUCKETS), lambda i: (0, 0)), pl.BlockSpec((1, BIAS_PC), lambda i: (0, i))],
        out_specs=pl.BlockSpec((h, BIAS_PC), lambda i: (0, i)), out_shape=_sds((h, p), F32),
        compiler_params=_cparams(PAR),
    )(table_t, bucket)


def _bucket_reduce(dbias, bucket, *, name):
    h, p = dbias.shape

    def body(d_ref, b_ref, o_ref):
        @pl.when(pl.program_id(0) == 0)
        def _():
            o_ref[...] = jnp.zeros_like(o_ref)

        o_ref[...] += lax.dot_general(d_ref[...], _onehot(b_ref[...]), NT, precision=HI, preferred_element_type=F32)

    return pl.pallas_call(
        body, name=name, grid=(p // BIAS_PC,),
        in_specs=[pl.BlockSpec((h, BIAS_PC), lambda i: (0, i)), pl.BlockSpec((1, BIAS_PC), lambda i: (0, i))],
        out_specs=pl.BlockSpec((h, REL_BUCKETS), lambda i: (0, 0)), out_shape=_sds((h, REL_BUCKETS), F32),
        compiler_params=_cparams(ARB),
    )(dbias, bucket)


def _adamw_math(w, g, m, v):
    m = ADAM_B1 * m + (1.0 - ADAM_B1) * g
    v = ADAM_B2 * v + (1.0 - ADAM_B2) * (g * g)
    m_hat = m / (1.0 - ADAM_B1 ** ADAM_STEP)
    v_hat = v / (1.0 - ADAM_B2 ** ADAM_STEP)
    delta = -ADAM_LR * (m_hat / (jnp.sqrt(v_hat) + ADAM_EPS) + ADAM_WD * w)
    return delta, m, v


def _adamw_reduce(parts, w, m, v, *, tr, name):
    nl = len(parts)
    rows, c = w.shape
    r = rows // nl
    nt = r // tr

    def body(*refs):
        p_refs = refs[:nl]
        w_ref, m_ref, v_ref, g_ref, d_ref, nm_ref, nv_ref = refs[nl:]
        for li, p_ref in enumerate(p_refs):
            @pl.when(pl.program_id(0) == li)
            def _(p_ref=p_ref):
                g = p_ref[0].astype(F32)
                for k in range(1, N_DEV):
                    g = g + p_ref[k].astype(F32)
                d, nm, nv = _adamw_math(w_ref[...], g, m_ref[...], v_ref[...])
                g_ref[...] = g
                d_ref[...] = d
                nm_ref[...] = nm
                nv_ref[...] = nv

    def part_map(li):
        return lambda l, i: (0, jnp.where(l == li, i, jnp.where(l < li, 0, nt - 1)), 0)

    spec = pl.BlockSpec((tr, c), lambda l, i: (l * nt + i, 0))
    return pl.pallas_call(
        body, name=name, grid=(nl, nt),
        in_specs=[pl.BlockSpec((N_DEV, tr, c), part_map(li)) for li in range(nl)] + [spec, spec, spec],
        out_specs=[spec] * 4, out_shape=[_sds((rows, c), F32)] * 4, compiler_params=_cparams(ARB, ARB),
    )(*parts, w, m, v)


def _adamw_plain(g, w, m, v, *, name):
    def body(g_ref, w_ref, m_ref, v_ref, d_ref, nm_ref, nv_ref):
        d, nm, nv = _adamw_math(w_ref[...], g_ref[...], m_ref[...], v_ref[...])
        d_ref[...] = d
        nm_ref[...] = nm
        nv_ref[...] = nv

    return pl.pallas_call(body, name=name, out_shape=[_sds(w.shape, F32)] * 3)(g, w, m, v)


def _mesh_pos():
    return lax.axis_index("x"), lax.axis_index("y"), lax.axis_index("c")


def _allgather_body(x_refs, out_refs, send_sems, recv_sems, local_sems, slot):
    x, y, c = _mesh_pos()
    me, sibling = (x, y, c), (x, y, 1 - c)
    chips = [(1 - x, y), (x, 1 - y), (1 - x, 1 - y)]
    waits = []
    for a, (x_ref, out_ref) in enumerate(zip(x_refs, out_refs)):
        def copy(k, block, to, src=None, out_ref=out_ref, a=a):
            return pltpu.make_async_remote_copy(
                src_ref=slot(out_ref, block) if src is None else src, dst_ref=slot(out_ref, block),
                send_sem=send_sems.at[a, k], recv_sem=recv_sems.at[a, k], device_id=to, device_id_type=MESH)

        mine = pltpu.make_async_copy(x_ref, slot(out_ref, me), local_sems.at[a])
        mine.start()
        first = [copy(0, me, sibling, src=x_ref)]
        first += [copy(1 + j, me, (*chip, c), src=x_ref) for j, chip in enumerate(chips)]
        for cp in first:
            cp.start()
        waits.append((copy, mine, first))
    sends = []
    for copy, mine, first in waits:
        passed = [copy(4 + j, (*chip, c), sibling) for j, chip in enumerate(chips)]
        for j, chip in enumerate(chips):
            copy(1 + j, (*chip, c), me).wait_recv()
            passed[j].start()
        sends.append(passed)
    for (copy, mine, first), passed in zip(waits, sends):
        copy(0, sibling, me).wait_recv()
        for j, chip in enumerate(chips):
            copy(4 + j, (*chip, 1 - c), me).wait_recv()
        for cp in first + passed:
            cp.wait_send()
        mine.wait()


PEER_FLIPS = ((0, 0, 1), (1, 0, 0), (0, 1, 0), (1, 1, 0), (1, 0, 1), (0, 1, 1), (1, 1, 1))
HBM_SPEC = pl.BlockSpec(memory_space=pltpu.HBM)
SEM_SPEC = pl.BlockSpec(memory_space=pltpu.SEMAPHORE)
DATAFLOW = pltpu.SideEffectType.DATAFLOW_SIDE_EFFECTING


def _peer_copies(x_refs, land_refs, send_sem, recv_sem, scatter):
    x, y, c = _mesh_pos()
    me = 4 * x + 2 * y + c
    copies = []
    for x_ref, land_ref in zip(x_refs, land_refs):
        for fx, fy, fc in PEER_FLIPS:
            px, py, pc = x ^ fx, y ^ fy, c ^ fc
            src = x_ref.at[4 * px + 2 * py + pc] if scatter else x_ref
            copies.append(pltpu.make_async_remote_copy(
                src_ref=src, dst_ref=land_ref.at[me], send_sem=send_sem, recv_sem=recv_sem,
                device_id=(px, py, pc), device_id_type=MESH))
    return copies


def _exchange_start(groups, after, *, scatter, name):
    xs = [x for grp in groups for x in grp]
    na = len(xs)
    ngrp = len(groups)
    firsts = np.cumsum([0] + [len(grp) for grp in groups])
    land_shapes = [x.shape if scatter else (N_DEV,) + x.shape for x in xs]
    extra = [] if after is None else [after]

    def body(*refs):
        x_refs, land_refs = refs[:na], refs[na:2 * na]
        sems = refs[2 * na + len(extra):2 * na + len(extra) + 2 * ngrp]
        token, local_sem = refs[-2], refs[-1]
        xm, ym, cm = _mesh_pos()
        me = 4 * xm + 2 * ym + cm
        for gi in range(ngrp):
            lo, hi = firsts[gi], firsts[gi + 1]
            for cp in _peer_copies(x_refs[lo:hi], land_refs[lo:hi], sems[2 * gi], sems[2 * gi + 1], scatter):
                cp.start()
        for x_ref, land_ref in zip(x_refs, land_refs):
            own = pltpu.make_async_copy(x_ref.at[me] if scatter else x_ref, land_ref.at[me], local_sem)
            own.start()
            own.wait()
        token[...] = jnp.zeros_like(token)

    sem_shapes = [pltpu.SemaphoreType.DMA(())] * (2 * ngrp)
    lands = [pltpu.with_memory_space_constraint(lax.empty(s, x.dtype), pltpu.HBM) for s, x in zip(land_shapes, xs)]
    outs = pl.pallas_call(
        body, name=name,
        in_specs=[HBM_SPEC] * (2 * na) + [pl.BlockSpec(memory_space=pl.ANY)] * len(extra),
        out_specs=[SEM_SPEC] * (2 * ngrp) + [HBM_SPEC] * (2 * na) + [pl.BlockSpec(memory_space=pltpu.VMEM)],
        out_shape=sem_shapes + [pltpu.HBM(x.shape, x.dtype) for x in xs]
        + [pltpu.HBM(s, x.dtype) for s, x in zip(land_shapes, xs)] + [_sds((8, 128), F32)],
        input_output_aliases={i: 2 * ngrp + i for i in range(2 * na)},
        scratch_shapes=[pltpu.SemaphoreType.DMA],
        compiler_params=pltpu.CompilerParams(has_side_effects=DATAFLOW),
    )(*[pltpu.with_memory_space_constraint(x, pltpu.HBM) for x in xs], *lands, *extra)
    sems, thru, token = outs[:2 * ngrp], outs[2 * ngrp:2 * ngrp + 2 * na], outs[-1]
    handles = []
    for gi in range(ngrp):
        lo, hi = firsts[gi], firsts[gi + 1]
        handles.append((sems[2 * gi], sems[2 * gi + 1], thru[lo:hi], thru[na + lo:na + hi]))
    return handles, token


def _exchange_wait(handle, after, *, scatter, name):
    send_sems, recv_sems, x_thru, land_thru = handle
    na = len(x_thru)

    def body(*refs):
        x_refs, land_refs = refs[:na], refs[na:2 * na]
        send_ref, recv_ref = refs[2 * na], refs[2 * na + 1]
        for cp in _peer_copies(x_refs, land_refs, send_ref, recv_ref, scatter):
            cp.wait_send()
            cp.wait_recv()

    outs = pl.pallas_call(
        body, name=name,
        in_specs=[HBM_SPEC] * (2 * na) + [SEM_SPEC, SEM_SPEC, pl.BlockSpec(memory_space=pl.ANY)],
        out_specs=[HBM_SPEC] * (2 * na),
        out_shape=[pltpu.HBM(a.shape, a.dtype) for a in list(x_thru) + list(land_thru)],
        input_output_aliases={i: i for i in range(2 * na)},
        compiler_params=pltpu.CompilerParams(has_side_effects=DATAFLOW),
    )(*x_thru, *land_thru, send_sems, recv_sems, after)
    return outs[na:]


def _allgather_vmem(x, *, reduce, name):
    r, c = x.shape

    def body(x_ref, out_ref, *rest):
        if reduce:
            gath, send_sems, recv_sems, local_sems = rest
        else:
            send_sems, recv_sems, local_sems = rest
            gath = out_ref
        _allgather_body([x_ref], [gath], send_sems, recv_sems, local_sems,
                        lambda ref, pos: ref.at[pl.ds((4 * pos[0] + 2 * pos[1] + pos[2]) * r, r), :])
        if reduce:
            acc = gath[0:r, :]
            for k in range(1, N_DEV):
                acc = acc + gath[k * r:(k + 1) * r, :]
            out_ref[...] = acc

    vm = pl.BlockSpec(memory_space=pltpu.VMEM)
    scratch = [pltpu.SemaphoreType.DMA((1, 7)), pltpu.SemaphoreType.DMA((1, 7)), pltpu.SemaphoreType.DMA((1,))]
    if reduce:
        scratch = [pltpu.VMEM((N_DEV * r, c), x.dtype)] + scratch
    return pl.pallas_call(
        body, name=name, in_specs=[vm], out_specs=vm,
        out_shape=_sds((r, c) if reduce else (N_DEV * r, c), x.dtype), scratch_shapes=scratch,
    )(x)


def _t5_bucket(rel):
    nb = REL_BUCKETS // 2
    ret = jnp.where(rel > 0, nb, 0)
    n = jnp.abs(rel)
    max_exact = nb // 2
    nf = jnp.maximum(n, 1).astype(F32)
    large = max_exact + (jnp.log(nf / max_exact) / math.log(REL_MAX_DIST / max_exact)
                         * (nb - max_exact)).astype(jnp.int32)
    large = jnp.minimum(large, nb - 1)
    return ret + jnp.where(n < max_exact, n, large)


def _band_pattern(block, radius, dil):
    kw = block + 2 * radius
    rel = jnp.arange(kw)[None, :] - radius - jnp.arange(block)[:, None]
    return jnp.where(jnp.abs(rel) <= radius, _t5_bucket(rel * dil), -1).astype(jnp.int32).reshape(1, block * kw)


def _rope_tables():
    lane = np.arange(64)
    seg, j = lane // 32, lane % 32
    inv = ROPE_THETA ** (-jnp.arange(0, 32, 2, dtype=F32) / 32)
    tpos = jnp.arange(SEQ)
    pos = jnp.where(jnp.asarray(seg)[None, :] == 0, (tpos // GRID_W)[:, None], (tpos % GRID_W)[:, None])
    ang = pos.astype(F32) * inv[jnp.asarray(j % 16)][None, :]
    cos = jnp.cos(ang)
    sins = jnp.where(jnp.asarray(j)[None, :] < 16, -jnp.sin(ang), jnp.sin(ang))
    return jnp.tile(cos, (1, 4)), jnp.tile(sins, (1, 4))


A_Q, A_K, A_V = (256, 0), (256, 1), (256, 2)
B_Q, B_K, B_V = (256, 0), (128, 2), (128, 3)
A_HEADS = dict(rad=A_RADIUS, nh=4, nkv=4)
B_HEADS = dict(rad=SWA_RADIUS, nh=4, nkv=2)


def _pin(arr, token):
    return arr if token is None else arr + token[0:1, 0:1]


def _local_step(x, pe, tgt, rel_bias, wts, matmul_weights, grads_ready):
    t = x.shape[0]
    bl = t // SEQ
    cos, sins = _rope_tables()
    blocks_a = [min(BAND_BLOCK, SEQ // d) for d in DILATIONS]
    pats_a = [_band_pattern(blk, A_RADIUS, d) for blk, d in zip(blocks_a, DILATIONS)]
    pat_b = _band_pattern(BAND_BLOCK, SWA_RADIUS, 1)
    table_t = rel_bias.T
    bias_a = [_bias_lookup(table_t[:4], pt, name=f"bias_a{ci}").reshape(4, blk, blk + 2 * A_RADIUS)
              for ci, (pt, blk) in enumerate(zip(pats_a, blocks_a))]
    bias_b = _bias_lookup(table_t[4:], pat_b, name="bias_b").reshape(4, BAND_BLOCK, BAND_BLOCK + 2 * SWA_RADIUS)
    nat4 = lambda a: a.reshape(bl, 1, SEQ, a.shape[-1])

    saved = []
    for li in range(DEPTH):
        w = dict(wts[li])
        w.update(matmul_weights(li, "in", x)[0])
        hn0, proj = _norm_mm((x,), w["g_mix"], w["w_in"], None, tm=1024, tn=1152, name="mix_in_fwd")
        more, started = matmul_weights(li, "rest", proj)
        w.update(more)
        w["out_gain"] = _pin(w["out_gain"], started)
        qa1, qa4, qa16, qb, qd = _qkprep_fwd(proj, w["qk_gains"], cos, sins, tm=512, name="qkprep_fwd")
        qa = (nat4(qa1), qa4, qa16)
        oa, la = [], []
        for ci in range(3):
            o, l = _band_fwd(qa[ci], A_Q, A_K, A_V, bias_a[ci], None, name=f"band_a{ci}_fwd", **A_HEADS)
            oa.append(o)
            la.append(l)
        oa[0], la[0] = oa[0].reshape(t, 256), la[0].reshape(t, 256)
        ya, lse_a = _combine_a(oa, la, tm=512, name="combine_a")
        yb, lse_b = _band_fwd(nat4(qb), B_Q, B_K, B_V, bias_b, w["sink_t"], name="band_b_fwd", **B_HEADS)
        yb = yb.reshape(t, 256)
        yc = _c_fwd(proj, w["c_g"], w["c_b"], w["c_ws"], w["c_bst"], tm=512, name="c_fwd")
        yd, lse_d = _dense_fwd(qd, tq=128, name="dense_fwd")
        mixed, x1 = _norm_mm((ya, yb, yc, yd), w["out_gain"], w["w_out"], x, tm=1024, tn=1024, name="mix_out_fwd")
        hn1, h = _norm_mm((x1,), w["g_ffn"], w["w_up"], None, tm=1024, tn=1408, name="ffn_up_fwd", out_dtype=BF16)
        act = _conv_gate_fwd(h, w["conv_w"], w["conv_b"], name="conv_gate_fwd")
        x2 = _mm(act, w["w_down"], "nn", x1, tm=1024, tn=1024, out_dtype=F32, name="ffn_down_fwd")
        hn2, x3, gate, pp = _ple_fwd(x2, w["g_ple"], w["w_gate"], pe, li * (t // 1024), w["w_proj"], tm=1024, tn=512,
                                     name="ple_fwd")
        saved.append(dict(w=w, x0=x, hn0=hn0, proj=proj, qa=qa, qb=qb, qd=qd, ya=ya, lse_a=lse_a, yb=yb, lse_b=lse_b,
                          yc=yc, yd=yd, lse_d=lse_d, mixed=mixed, x1=x1, hn1=hn1, h=h, act=act, x2=x2, hn2=hn2,
                          gate=gate, pp=pp))
        x = x3

    loss_tile, dx = _loss_head(x, tgt, tm=512, name="loss_head")
    grads = [None] * DEPTH
    d_table_a = jnp.zeros((4, REL_BUCKETS), F32)
    d_table_b = jnp.zeros((4, REL_BUCKETS), F32)
    token = None
    for li in reversed(range(DEPTH)):
        s = saved[li]
        w = s["w"]
        g = {}
        w["g_ple"] = _pin(w["g_ple"], token)
        dz, dpp = _ple_bwd_ew(dx, s["gate"], s["pp"], tm=512, name="ple_bwd_ew")
        g["w_gate"] = _mm(s["hn2"], dz, "tn", None, tm=1024, tn=512, out_dtype=BF16, name="dw_gate")
        g["w_proj"] = _mm(pe, dpp, "tn", None, tm=256, tn=1024, out_dtype=BF16, name="dw_proj", a_rows=(li, t))
        dx2, dx2b, g["g_ple"] = _mm_bt_normbwd((dz,), w["w_gate"], (s["x2"],), w["g_ple"], dx, tm=1024, tn=1024,
                                               name="ple_bwd", emit_bf16=True)
        g["w_down"] = _mm(s["act"], dx2b, "tn", None, tm=1408, tn=512, out_dtype=BF16, name="dw_down")
        dact = _mm(dx2b, w["w_down"], "nt", None, tm=1024, tn=1408, out_dtype=BF16, name="ffn_down_bwd")
        dhg, dhu, g["conv_w"], g["conv_b"] = _conv_gate_bwd(s["h"], dact, w["conv_w"], w["conv_b"], name="conv_gate_bwd")
        g["w_up"] = jnp.concatenate(
            [_mm(s["hn1"], dhalf, "tn", None, tm=1024, tn=1408, out_dtype=BF16, name=f"dw_up_{nm}")
             for nm, dhalf in (("gate", dhg), ("up", dhu))], axis=1)
        g_ffn = _pin(w["g_ffn"], grads_ready(li, "mid", g))
        dx1, dx1b, g["g_ffn"] = _mm_bt_normbwd((dhg, dhu), w["w_up"], (s["x1"],), g_ffn, dx2, tm=1024, tn=1408,
                                               name="ffn_up_bwd", emit_bf16=True)
        g["w_out"] = _mm(s["mixed"], dx1b, "tn", None, tm=1024, tn=512, out_dtype=BF16, name="dw_out")
        dycat, g["out_gain"] = _mm_bt_normbwd((dx1b,), w["w_out"], (s["ya"], s["yb"], s["yc"], s["yd"]), w["out_gain"],
                                              None, tm=1024, tn=1024, name="mix_out_bwd")
        dy_r, lse_r, dl_a, dl_b, dl_d = _deltas(dycat, s["ya"], s["yb"], s["yd"], s["lse_a"], tm=512, name="deltas")
        dy_a = (nat4(dycat),) + tuple(dy_r)
        lse_a = (nat4(s["lse_a"]),) + tuple(lse_r)
        dl_a = (nat4(dl_a[0]),) + tuple(dl_a[1:])
        da = []
        for ci in range(3):
            dq, dk, dv, dbias = _band_bwd(s["qa"][ci], A_Q, A_K, A_V, bias_a[ci], None, dy_a[ci], 0, lse_a[ci],
                                          dl_a[ci], name=f"band_a{ci}_bwd", **A_HEADS)
            if ci == 0:
                dq, dk, dv = (a.reshape(t, 256) for a in (dq, dk, dv))
            da.append((dq, dk, dv))
            d_table_a = d_table_a + _bucket_reduce(dbias.reshape(4, -1), pats_a[ci], name=f"bucket_a{ci}")
        dqb, dkb, dvb, dbias_b, dsink = _band_bwd(nat4(s["qb"]), B_Q, B_K, B_V, bias_b, w["sink_t"], nat4(dycat), 1,
                                                  nat4(s["lse_b"]), nat4(dl_b), name="band_b_bwd", **B_HEADS)
        d_table_b = d_table_b + _bucket_reduce(dbias_b.reshape(4, -1), pat_b, name="bucket_b")
        g["sink"] = dsink[:, 0, 0]
        dd = _dense_bwd(s["qd"], dycat, s["lse_d"], dl_d, tq=128, name="dense_bwd")
        dcu, dcv, g["c_ws"], dbs, g["c_g"], g["c_b"] = _c_bwd(s["proj"], dycat, w["c_g"], w["c_b"], w["c_ws"],
                                                               w["c_wst"], w["c_bst"], tm=512, name="c_bwd")
        g["c_bs"] = dbs[:, ::64].T
        db = (dqb.reshape(t, 256), dkb.reshape(t, 128), dvb.reshape(t, 128))
        dproj, dgains = _qkprep_bwd(s["proj"], da, db, dd, dcu, dcv, w["qk_gains"], cos, sins, tm=512, name="qkprep_bwd")
        g["qk_gain"] = dgains[:6, :64].reshape(3, 2, HEAD_DIM)
        g["w_in"] = _mm(s["hn0"], dproj, "tn", None, tm=1024, tn=1152, out_dtype=BF16, name="dw_in")
        dx, g["g_mix"] = _mm_bt_normbwd((dproj,), w["w_in"], (s["x0"],), w["g_mix"], dx1, tm=1024, tn=1152,
                                        name="mix_in_bwd")
        grads[li] = g
        token = grads_ready(li, "end", g)
    d_rel_bias = jnp.concatenate([d_table_a, d_table_b], axis=0).T
    return loss_tile[0, 0], dx, grads, d_rel_bias


WEIGHT_NAMES = ("rel_bias", "ln_mix_g", "w_in", "qk_gain", "sink", "c_norm_g", "c_norm_b", "c_ws", "c_bs", "out_gain",
                "w_out", "ln_ffn_g", "w_up", "conv_w", "conv_b", "w_down", "ln_ple_g", "w_ple_gate", "w_ple_proj")
COL_SHARDED = ("w_in", "w_up", "w_ple_proj")
ROW_SHARDED = ("w_out", "w_down", "w_ple_gate")
SMALL_SHARDED = ("conv_w", "out_gain")
REPLICATED = tuple(n for n in WEIGHT_NAMES if n not in COL_SHARDED + ROW_SHARDED + SMALL_SHARDED)
LOCAL_GRAD_KEY = {"ln_mix_g": "g_mix", "ln_ffn_g": "g_ffn", "ln_ple_g": "g_ple", "c_norm_g": "c_g", "c_norm_b": "c_b",
                  "w_ple_gate": "w_gate", "w_ple_proj": "w_proj"}


def _full_from_gathered(name, gathered):
    _, r, c = gathered.shape
    if name in ROW_SHARDED:
        return gathered.reshape(N_DEV * r, c)
    return jnp.transpose(gathered, (1, 0, 2)).reshape(r, N_DEV * c)


def _slots_from_full(name, full):
    rows, cols = full.shape
    if name in ROW_SHARDED:
        return full.reshape(N_DEV, rows // N_DEV, cols)
    return jnp.transpose(full.reshape(rows, N_DEV, cols // N_DEV), (1, 0, 2))


def _piece_rows(shape):
    return -(-int(np.prod(shape)) // 1024) * 8


def _pack_rows(arrays):
    pieces = []
    for a in arrays:
        n, rows = int(np.prod(a.shape)), _piece_rows(a.shape)
        flat = a.astype(F32).reshape(-1)
        if n != rows * LANES:
            flat = jnp.pad(flat, (0, rows * LANES - n))
        pieces.append(flat.reshape(rows, LANES))
    return jnp.concatenate(pieces, axis=0)


def _unpack_rows(packed, shapes):
    out, off = [], 0
    for shp in shapes:
        n, rows = int(np.prod(shp)), _piece_rows(shp)
        piece = packed[off:off + rows]
        out.append((piece if n == rows * LANES else piece.reshape(-1)[:n]).reshape(shp))
        off += rows
    return out


def kernel(x, p, rel_bias, ln_mix_g, w_in, qk_gain, sink, c_norm_g, c_norm_b, c_ws, c_bs, out_gain, w_out, ln_ffn_g, w_up, conv_w, conv_b, w_down, ln_ple_g, w_ple_gate, w_ple_proj, loss_target, m_rel_bias, m_ln_mix_g, m_w_in, m_qk_gain, m_sink, m_c_norm_g, m_c_norm_b, m_c_ws, m_c_bs, m_out_gain, m_w_out, m_ln_ffn_g, m_w_up, m_conv_w, m_conv_b, m_w_down, m_ln_ple_g, m_w_ple_gate, m_w_ple_proj, v_rel_bias, v_ln_mix_g, v_w_in, v_qk_gain, v_sink, v_c_norm_g, v_c_norm_b, v_c_ws, v_c_bs, v_out_gain, v_w_out, v_ln_ffn_g, v_w_up, v_conv_w, v_conv_b, v_w_down, v_ln_ple_g, v_w_ple_gate, v_w_ple_proj):
    env = dict(locals())
    wt = {n: env[n] for n in WEIGHT_NAMES}
    mom_m = {n: env["m_" + n] for n in WEIGHT_NAMES}
    mom_v = {n: env["v_" + n] for n in WEIGHT_NAMES}
    bl = x.shape[0]
    t = bl * SEQ
    me = 4 * lax.axis_index("x") + 2 * lax.axis_index("y") + lax.axis_index("c")

    big = COL_SHARDED + ROW_SHARDED
    full = {}
    small_shapes = [wt[n].shape for n in SMALL_SHARDED]
    small = _allgather_vmem(_pack_rows([wt[n] for n in SMALL_SHARDED]), reduce=False, name="gather_small")
    small = small.reshape(N_DEV, -1)
    off = 0
    for n, shp in zip(SMALL_SHARDED, small_shapes):
        cnt = int(np.prod(shp))
        g = small[:, off:off + cnt].reshape((N_DEV,) + tuple(shp))
        full[n] = jnp.transpose(g, (1, 2, 0, 3)).reshape(shp[0], shp[1], N_DEV * shp[2])
        off += _piece_rows(shp) * LANES

    def head_gain(li, a, b, reps):
        g = jnp.tile(qk_gain[li, a, b], reps)
        return jnp.pad(g, (0, 256 - g.shape[0]))

    wts = []
    for li in range(DEPTH):
        rows = [head_gain(li, 0, 0, 4), head_gain(li, 0, 1, 4), head_gain(li, 1, 0, 4), head_gain(li, 1, 1, 2),
                head_gain(li, 2, 0, 4), head_gain(li, 2, 1, 2), jnp.zeros((256,), F32), jnp.zeros((256,), F32)]
        wts.append(dict(
            g_mix=ln_mix_g[li].reshape(1, -1), qk_gains=jnp.stack(rows),
            sink_t=jnp.broadcast_to(sink[li][:, None, None], (4, 8, 128)),
            c_g=c_norm_g[li].reshape(1, -1), c_b=c_norm_b[li].reshape(1, -1), c_ws=c_ws[li].astype(BF16),
            c_wst=jnp.transpose(c_ws[li], (0, 2, 1)).astype(BF16), c_bst=jnp.repeat(c_bs[li].T, 64, axis=1),
            out_gain=full["out_gain"][li].reshape(1, -1), g_ffn=ln_ffn_g[li].reshape(1, -1),
            conv_w=full["conv_w"][li], conv_b=conv_b[li].reshape(1, -1), g_ple=ln_ple_g[li].reshape(1, -1)))

    local_key = {"w_ple_gate": "w_gate", "w_ple_proj": "w_proj"}
    pack_w = D_MODEL

    gather_names = {"in": ("w_in",), "rest": tuple(n for n in big if n != "w_in")}
    gather_handles = {}

    def start_gather(li, after):
        packed = [[jnp.concatenate([wt[n][li].astype(BF16).reshape(-1, pack_w) for n in gather_names[part]], axis=0)]
                  for part in ("in", "rest")]
        handles, token = _exchange_start(packed, after, scatter=False, name=f"gather_start_{li}")
        gather_handles[li, "in"], gather_handles[li, "rest"] = handles
        return token

    def unpack(land, names):
        out, off = [], 0
        for n in names:
            r, c = wt[n].shape[1:]
            rows = r * c // pack_w
            out.append(land[:, off:off + rows].reshape(N_DEV, r, c))
            off += rows
        return out

    def matmul_weights(li, part, after):
        started = start_gather(li + 1, after) if part == "rest" and li + 1 < DEPTH else None
        (land,) = _exchange_wait(gather_handles[li, part], after, scatter=False, name=f"gather_wait_{li}_{part}")
        names = gather_names[part]
        return {local_key.get(n, n): _full_from_gathered(n, g) for n, g in zip(names, unpack(land, names))}, started

    start_gather(0, None)

    mid_names = ("w_ple_gate", "w_ple_proj", "w_down", "w_up")
    end_names = ("w_out", "w_in")
    pending = []

    def start_exchange(li, names, g, tag):
        slots = jnp.concatenate([_slots_from_full(n, g[local_key.get(n, n)]).reshape(N_DEV, -1, pack_w) for n in names],
                                axis=1)
        (handle,), token = _exchange_start([[slots]], None, scatter=True, name=f"grads_start_{li}_{tag}")
        pending.append((li, names, handle, tag))
        return token

    def grads_ready(li, stage, g):
        if li == 0:
            return start_exchange(li, mid_names if stage == "mid" else end_names, g, stage)
        if stage == "end":
            return start_exchange(li, mid_names + end_names, g, stage)
        return None

    loss_part, dx, grads, d_rel_bias = _local_step(
        x.reshape(t, D_MODEL), p.reshape(DEPTH * t, PLE_DIM), loss_target.reshape(t, D_MODEL), rel_bias, wts,
        matmul_weights, grads_ready)
    loss = lax.psum(loss_part, ("x", "y", "c"))

    def local_grad(n):
        if n == "rel_bias":
            return d_rel_bias
        key = LOCAL_GRAD_KEY.get(n, n)
        return jnp.stack([grads[li][key].reshape(wt[n].shape[1:]) if n in REPLICATED else grads[li][key]
                          for li in range(DEPTH)])

    landed = {}
    for li, names, handle, tag in pending:
        (land,) = _exchange_wait(handle, dx, scatter=True, name=f"grads_wait_{li}_{tag}")
        for n, parts in zip(names, unpack(land, names)):
            landed[n, li] = parts
    out_g, out_d, out_m, out_v = {}, {}, {}, {}
    for n in big:
        shp = wt[n].shape
        two_d = lambda a: a.reshape(-1, shp[-1])
        res = _adamw_reduce([landed[n, li] for li in range(DEPTH)], two_d(wt[n]), two_d(mom_m[n]), two_d(mom_v[n]),
                            tr=32 if n == "w_down" else 128, name="adamw_" + n)
        out_g[n], out_d[n], out_m[n], out_v[n] = [r.reshape(shp) for r in res]

    small_names = REPLICATED + SMALL_SHARDED
    small_full_shapes = [wt[n].shape if n in REPLICATED else full[n].shape for n in small_names]
    reduced = _allgather_vmem(_pack_rows([local_grad(n) for n in small_names]), reduce=True, name="allreduce_small")
    reduced = dict(zip(small_names, _unpack_rows(reduced, small_full_shapes)))
    rep_shapes = [wt[n].shape for n in REPLICATED]
    upd = _adamw_plain(_pack_rows([reduced[n] for n in REPLICATED]), _pack_rows([wt[n] for n in REPLICATED]),
                       _pack_rows([mom_m[n] for n in REPLICATED]), _pack_rows([mom_v[n] for n in REPLICATED]),
                       name="adamw_replicated")
    for dst, packed in zip((out_d, out_m, out_v), upd):
        dst.update(zip(REPLICATED, _unpack_rows(packed, rep_shapes)))
    for n in REPLICATED:
        out_g[n] = reduced[n]
    for n in SMALL_SHARDED:
        shp = wt[n].shape
        g = reduced[n].reshape(shp[0], shp[1], N_DEV, shp[2])
        g = lax.dynamic_index_in_dim(g, me, axis=2, keepdims=False)
        two_d = lambda a: a.reshape(-1, shp[-1])
        res = _adamw_plain(two_d(g), two_d(wt[n]), two_d(mom_m[n]), two_d(mom_v[n]), name="adamw_" + n)
        out_g[n] = g
        out_d[n], out_m[n], out_v[n] = [r.reshape(shp) for r in res]

    return (loss, dx.reshape(bl, SEQ, D_MODEL), *[out_g[n] for n in WEIGHT_NAMES], *[out_d[n] for n in WEIGHT_NAMES],
            *[out_m[n] for n in WEIGHT_NAMES], *[out_v[n] for n in WEIGHT_NAMES])
```

```python
import math

import jax
import jax.numpy as jnp
import numpy as np
from jax import lax
from jax.experimental import pallas as pl
from jax.experimental.pallas import tpu as pltpu

F32 = jnp.float32
BF16 = jnp.bfloat16
HI = lax.Precision.HIGHEST

N_DEV = 8
D_MODEL = 1024
SEQ = 2048
DEPTH = 2
HEAD_DIM = 64
IN_WIDTH = 2304
D_FF = 2816
PLE_DIM = 256
C_CHUNK = 128
C_GROUPS = 4
DILATED_CFGS = ((128, 1), (512, 4), (2048, 16))
DILATIONS = tuple(d for _, d in DILATED_CFGS)
A_RADIUS = 64
SWA_RADIUS = 128
BAND_BLOCK = 256
GRID_W = 64
ROPE_THETA = 10000.0
REL_BUCKETS = 32
REL_MAX_DIST = 1024
EPS = 1e-6
NEG_INF = -1e30
ATTN_SCALE = HEAD_DIM ** -0.5
LANES = 128

ADAM_LR = 0.001
ADAM_B1 = 0.9
ADAM_B2 = 0.999
ADAM_EPS = 1e-08
ADAM_WD = 0.01
ADAM_STEP = 10

MESH = pl.DeviceIdType.MESH
NT = (((1,), (1,)), ((), ()))
TN = (((0,), (0,)), ((), ()))
ARB = "arbitrary"
PAR = "parallel"


def _cparams(*sem):
    return pltpu.CompilerParams(dimension_semantics=tuple(sem))


def _sds(shape, dtype):
    return jax.ShapeDtypeStruct(tuple(shape), dtype)


def _group_sum_matrix(n, same_group):
    r = lax.broadcasted_iota(jnp.int32, (n, n), 0)
    c = lax.broadcasted_iota(jnp.int32, (n, n), 1)
    if same_group:
        return ((r >> 6) == (c >> 6)).astype(F32)
    return ((r & 63) == (c & 63)).astype(F32)


def _seg_sum(x, e):
    return jnp.dot(x, e, precision=HI, preferred_element_type=F32)


def _gelu(x):
    c = math.sqrt(2.0 / math.pi)
    return 0.5 * x * (1.0 + jnp.tanh(c * (x + 0.044715 * (x * x * x))))


def _gelu_grad(x):
    c = math.sqrt(2.0 / math.pi)
    t = jnp.tanh(c * (x + 0.044715 * (x * x * x)))
    return 0.5 * (1.0 + t) + 0.5 * x * (1.0 - t * t) * c * (1.0 + 3.0 * 0.044715 * (x * x))


def _sigmoid(x):
    return 1.0 / (1.0 + jnp.exp(-x))


def _scatter_cols(scratch, first, val):
    for c in range(val.shape[1] // LANES):
        scratch[first + c] = val[:, c * LANES:(c + 1) * LANES]


def _gather_cols(scratch, first, ncol):
    return jnp.concatenate([scratch[first + c] for c in range(ncol)], axis=1)


def _read_residue(scratch, first, ncol, r, d):
    n = scratch.shape[1] // d
    return jnp.concatenate([scratch.at[first + c][pl.ds(r, n, stride=d), :] for c in range(ncol)], axis=1)


def _write_residue(scratch, first, r, d, val):
    n = scratch.shape[1] // d
    for c in range(val.shape[1] // LANES):
        scratch.at[first + c][pl.ds(r, n, stride=d), :] = val[:, c * LANES:(c + 1) * LANES]


def _norm_mm(xs, gain, w, res, *, tm, tn, name, out_dtype=F32):
    t = xs[0].shape[0]
    k = sum(x.shape[1] for x in xs)
    n = w.shape[1]
    ng = len(xs)
    has_res = res is not None

    def body(*refs):
        x_refs = refs[:ng]
        g_ref, w_ref = refs[ng], refs[ng + 1]
        res_ref = refs[ng + 2] if has_res else None
        hn_ref, o_ref, hn_s = refs[ng + 2 + has_res:]

        @pl.when(pl.program_id(1) == 0)
        def _():
            off = 0
            for xr in x_refs:
                x = xr[...]
                wd = x.shape[1]
                r = lax.rsqrt(jnp.mean(x * x, axis=-1, keepdims=True) + EPS)
                hn_s[:, off:off + wd] = (x * r * g_ref[:, off:off + wd]).astype(BF16)
                off += wd
            hn_ref[...] = hn_s[...]

        acc = jnp.dot(hn_s[...], w_ref[...], preferred_element_type=F32)
        if has_res:
            acc = acc + res_ref[...]
        o_ref[...] = acc.astype(out_dtype)

    in_specs = [pl.BlockSpec((tm, x.shape[1]), lambda i, j: (i, 0)) for x in xs]
    in_specs += [pl.BlockSpec((1, k), lambda i, j: (0, 0)), pl.BlockSpec((k, tn), lambda i, j: (0, j))]
    args = list(xs) + [gain, w]
    if has_res:
        in_specs.append(pl.BlockSpec((tm, tn), lambda i, j: (i, j)))
        args.append(res)
    return pl.pallas_call(
        body, name=name, grid=(t // tm, n // tn), in_specs=in_specs,
        out_specs=[pl.BlockSpec((tm, k), lambda i, j: (i, 0)), pl.BlockSpec((tm, tn), lambda i, j: (i, j))],
        out_shape=[_sds((t, k), BF16), _sds((t, n), out_dtype)],
        scratch_shapes=[pltpu.VMEM((tm, k), BF16)],
        compiler_params=_cparams(PAR, ARB),
    )(*args)


def _mm(a, b, mode, res, *, tm, tn, out_dtype, name, a_rows=None):
    if mode == "tn":
        kk, m = a.shape
        blk_a = 0
        if a_rows is not None:
            blk_a, kk = a_rows
        a_spec = pl.BlockSpec((kk, tm), lambda i, j: (blk_a, i))
    else:
        m, kk = a.shape
        a_spec = pl.BlockSpec((tm, kk), lambda i, j: (i, 0))
    if mode == "nt":
        n = b.shape[0]
        b_spec = pl.BlockSpec((tn, kk), lambda i, j: (j, 0))
    else:
        n = b.shape[1]
        b_spec = pl.BlockSpec((kk, tn), lambda i, j: (0, j))
    has_res = res is not None

    def body(*refs):
        a_ref, b_ref = refs[0], refs[1]
        o_ref = refs[-1]
        av = a_ref[...].astype(BF16)
        bv = b_ref[...].astype(BF16)
        if mode == "nn":
            acc = jnp.dot(av, bv, preferred_element_type=F32)
        elif mode == "nt":
            acc = lax.dot_general(av, bv, NT, preferred_element_type=F32)
        else:
            acc = lax.dot_general(av, bv, TN, preferred_element_type=F32)
        if has_res:
            acc = acc + refs[2][...]
        o_ref[...] = acc.astype(out_dtype)

    in_specs = [a_spec, b_spec]
    args = [a, b]
    if has_res:
        in_specs.append(pl.BlockSpec((tm, tn), lambda i, j: (i, j)))
        args.append(res)
    return pl.pallas_call(
        body, name=name, grid=(m // tm, n // tn), in_specs=in_specs,
        out_specs=pl.BlockSpec((tm, tn), lambda i, j: (i, j)),
        out_shape=_sds((m, n), out_dtype),
        compiler_params=_cparams(PAR, PAR),
    )(*args)


def _mm_bt_normbwd(dys, w, xs, gain, dres, *, tm, tn, name, emit_bf16=False):
    t, wd_each = dys[0].shape
    nd = len(dys)
    per = wd_each // tn
    nj = nd * per
    k = w.shape[0]
    ng = len(xs)
    has_res = dres is not None

    def body(*refs):
        dy_refs = refs[:nd]
        w_ref = refs[nd]
        x_refs = refs[nd + 1:nd + 1 + ng]
        g_ref = refs[nd + 1 + ng]
        dres_ref = refs[nd + 2 + ng] if has_res else None
        outs = refs[nd + 2 + ng + has_res:]
        dx_ref = outs[0]
        dxb_ref = outs[1] if emit_bf16 else None
        dg_ref, acc = outs[1 + emit_bf16:]
        i, j = pl.program_id(0), pl.program_id(1)

        @pl.when(j == 0)
        def _():
            acc[...] = jnp.zeros_like(acc)

        for d, dy_ref in enumerate(dy_refs):
            @pl.when((j >= d * per) & (j < (d + 1) * per))
            def _(dy_ref=dy_ref):
                acc[...] += lax.dot_general(dy_ref[...].astype(BF16), w_ref[...], NT, preferred_element_type=F32)

        @pl.when(j == nj - 1)
        def _():
            @pl.when(i == 0)
            def _():
                dg_ref[...] = jnp.zeros_like(dg_ref)

            off = 0
            for xr in x_refs:
                x = xr[...]
                wd = x.shape[1]
                g = g_ref[:, off:off + wd]
                dyn = acc[:, off:off + wd]
                r = lax.rsqrt(jnp.mean(x * x, axis=-1, keepdims=True) + EPS)
                gdy = dyn * g
                dx = r * gdy - x * (r * r * r * jnp.mean(gdy * x, axis=-1, keepdims=True))
                if has_res:
                    dx = dx + dres_ref[:, off:off + wd]
                dx_ref[:, off:off + wd] = dx
                if emit_bf16:
                    dxb_ref[:, off:off + wd] = dx.astype(BF16)
                dg_ref[:, off:off + wd] += jnp.sum(dyn * x * r, axis=0, keepdims=True)
                off += wd

    def dy_map(d):
        return lambda i, j: (i, jnp.clip(j - d * per, 0, per - 1))

    in_specs = [pl.BlockSpec((tm, tn), dy_map(d)) for d in range(nd)]
    in_specs.append(pl.BlockSpec((k, tn), lambda i, j: (0, j)))
    in_specs += [pl.BlockSpec((tm, x.shape[1]), lambda i, j: (i, 0)) for x in xs]
    in_specs.append(pl.BlockSpec((1, k), lambda i, j: (0, 0)))
    args = list(dys) + [w] + list(xs) + [gain]
    if has_res:
        in_specs.append(pl.BlockSpec((tm, k), lambda i, j: (i, 0)))
        args.append(dres)
    row = pl.BlockSpec((tm, k), lambda i, j: (i, 0))
    out_specs = [row] + ([row] if emit_bf16 else []) + [pl.BlockSpec((1, k), lambda i, j: (0, 0))]
    out_shape = [_sds((t, k), F32)] + ([_sds((t, k), BF16)] if emit_bf16 else []) + [_sds((1, k), F32)]
    return pl.pallas_call(
        body, name=name, grid=(t // tm, nj), in_specs=in_specs, out_specs=out_specs, out_shape=out_shape,
        scratch_shapes=[pltpu.VMEM((tm, k), F32)],
        compiler_params=_cparams(ARB, ARB),
    )(*args)


def _rope_partner(y):
    n = y.shape[1]
    lane = lax.broadcasted_iota(jnp.int32, y.shape, 1)
    return jnp.where((lane & 31) < 16, pltpu.roll(y, n - 16, 1), pltpu.roll(y, 16, 1))


def _residue_specs(tm, width, nt):
    specs = [pl.BlockSpec((tm, width), lambda b, i: (b * nt + i, 0))]
    for d in DILATIONS[1:]:
        specs.append(pl.BlockSpec((None, d, tm // d, width), lambda b, i: (b, 0, i, 0)))
    return specs


def _residue_shapes(bl, width, dtype):
    return [_sds((bl * SEQ, width), dtype)] + [_sds((bl, d, SEQ // d, width), dtype) for d in DILATIONS[1:]]


def _qkprep_fwd(proj, gains, cos, sins, *, tm, name):
    t = proj.shape[0]
    bl = t // SEQ
    nt = SEQ // tm

    def body(p_ref, g_ref, c_ref, s_ref, qa1_ref, qa4_ref, qa16_ref, qb_ref, qd_ref, scr):
        e = _group_sum_matrix(256, True)

        def hn(x, row):
            wd = x.shape[1]
            ms = _seg_sum(x * x, e[:wd, :wd]) * (1.0 / HEAD_DIM)
            return x * lax.rsqrt(ms + EPS) * g_ref[row:row + 1, :wd]

        qa = jnp.concatenate([hn(p_ref[:, 0:256], 0) * ATTN_SCALE, hn(p_ref[:, 256:512], 1), p_ref[:, 512:768]], axis=1)
        qa1_ref[...] = qa.astype(BF16)
        _scatter_cols(scr, 0, qa)
        for d, ref in ((4, qa4_ref), (16, qa16_ref)):
            for r in range(d):
                ref[r] = _read_residue(scr, 0, 6, r, d).astype(BF16)
        qb_ref[:, 0:256] = (hn(p_ref[:, 768:1024], 2) * ATTN_SCALE).astype(BF16)
        qb_ref[:, 256:384] = hn(p_ref[:, 1024:1152], 3).astype(BF16)
        qb_ref[:, 384:512] = p_ref[:, 1152:1280].astype(BF16)
        yq = hn(p_ref[:, 1792:2048], 4)
        yq = yq * c_ref[...] + _rope_partner(yq) * s_ref[...]
        qd_ref[:, 0:256] = (yq * ATTN_SCALE).astype(BF16)
        yk = hn(p_ref[:, 2048:2176], 5)
        yk = yk * c_ref[:, 0:128] + _rope_partner(yk) * s_ref[:, 0:128]
        qd_ref[:, 256:384] = yk.astype(BF16)
        qd_ref[:, 384:512] = p_ref[:, 2176:2304].astype(BF16)

    row = lambda width: pl.BlockSpec((tm, width), lambda b, i: (b * nt + i, 0))
    tab = pl.BlockSpec((tm, 256), lambda b, i: (i, 0))
    return pl.pallas_call(
        body, name=name, grid=(bl, nt),
        in_specs=[row(IN_WIDTH), pl.BlockSpec((8, 256), lambda b, i: (0, 0)), tab, tab],
        out_specs=_residue_specs(tm, 768, nt) + [row(512), row(512)],
        out_shape=_residue_shapes(bl, 768, BF16) + [_sds((t, 512), BF16), _sds((t, 512), BF16)],
        scratch_shapes=[pltpu.VMEM((6, tm, LANES), F32)],
        compiler_params=_cparams(PAR, PAR),
    )(proj, gains, cos, sins)


def _qkprep_bwd(proj, da, db, dd, dcu, dcv, gains, cos, sins, *, tm, name):
    t = proj.shape[0]
    bl = t // SEQ
    nt = SEQ // tm
    flat = [a for cfg in da for a in cfg] + list(db) + list(dd) + [dcu, dcv]

    def body(*refs):
        p_ref, g_ref, c_ref, s_ref = refs[:4]
        d_refs = refs[4:4 + len(flat)]
        dp_ref, dg_ref, scr = refs[4 + len(flat):]
        a_refs = d_refs[:9]
        dqb_ref, dkb_ref, dvb_ref, dqd_ref, dkd_ref, dvd_ref, dcu_ref, dcv_ref = d_refs[9:]
        e = _group_sum_matrix(256, True)
        first = (pl.program_id(0) == 0) & (pl.program_id(1) == 0)
        last = (pl.program_id(0) == bl - 1) & (pl.program_id(1) == nt - 1)

        @pl.when(first)
        def _():
            dg_ref[...] = jnp.zeros_like(dg_ref)

        def hn_bwd(x, dy, row):
            wd = x.shape[1]
            ee = e[:wd, :wd]
            g = g_ref[row:row + 1, :wd]
            r = lax.rsqrt(_seg_sum(x * x, ee) * (1.0 / HEAD_DIM) + EPS)
            gdy = dy * g
            dx = r * gdy - x * (r * r * r * (_seg_sum(gdy * x, ee) * (1.0 / HEAD_DIM)))
            dg_ref[row:row + 1, :wd] += jnp.sum(dy * x * r, axis=0, keepdims=True)
            return dx

        def rope_bwd(dy, wd):
            return dy * c_ref[:, :wd] + _rope_partner(dy * s_ref[:, :wd])

        dqkv = jnp.concatenate([a_refs[0][...], a_refs[1][...], a_refs[2][...]], axis=1)
        for ci, d in ((1, 4), (2, 16)):
            for r in range(d):
                part = jnp.concatenate([a_refs[3 * ci + m][r] for m in range(3)], axis=1)
                _write_residue(scr, 0, r, d, part)
            dqkv = dqkv + _gather_cols(scr, 0, 6)
        dp_ref[:, 0:256] = hn_bwd(p_ref[:, 0:256], dqkv[:, 0:256] * ATTN_SCALE, 0).astype(BF16)
        dp_ref[:, 256:512] = hn_bwd(p_ref[:, 256:512], dqkv[:, 256:512], 1).astype(BF16)
        dp_ref[:, 512:768] = dqkv[:, 512:768].astype(BF16)
        dp_ref[:, 768:1024] = hn_bwd(p_ref[:, 768:1024], dqb_ref[...] * ATTN_SCALE, 2).astype(BF16)
        dp_ref[:, 1024:1152] = hn_bwd(p_ref[:, 1024:1152], dkb_ref[...], 3).astype(BF16)
        dp_ref[:, 1152:1280] = dvb_ref[...].astype(BF16)
        dp_ref[:, 1280:1536] = dcu_ref[...].astype(BF16)
        dp_ref[:, 1536:1792] = dcv_ref[...].astype(BF16)
        dp_ref[:, 1792:2048] = hn_bwd(p_ref[:, 1792:2048], rope_bwd(dqd_ref[...] * ATTN_SCALE, 256), 4).astype(BF16)
        dp_ref[:, 2048:2176] = hn_bwd(p_ref[:, 2048:2176], rope_bwd(dkd_ref[...], 128), 5).astype(BF16)
        dp_ref[:, 2176:2304] = dvd_ref[...].astype(BF16)

        @pl.when(last)
        def _():
            dg_ref[...] = _seg_sum(dg_ref[...], _group_sum_matrix(256, False))

    row = lambda width: pl.BlockSpec((tm, width), lambda b, i: (b * nt + i, 0))
    tab = pl.BlockSpec((tm, 256), lambda b, i: (i, 0))
    in_specs = [row(IN_WIDTH), pl.BlockSpec((8, 256), lambda b, i: (0, 0)), tab, tab]
    res_specs = _residue_specs(tm, 256, nt)
    in_specs += [res_specs[ci] for ci in range(3) for _ in range(3)]
    in_specs += [row(a.shape[1]) for a in flat[9:]]
    return pl.pallas_call(
        body, name=name, grid=(bl, nt), in_specs=in_specs,
        out_specs=[row(IN_WIDTH), pl.BlockSpec((8, 256), lambda b, i: (0, 0))],
        out_shape=[_sds((t, IN_WIDTH), BF16), _sds((8, 256), F32)],
        scratch_shapes=[pltpu.VMEM((6, tm, LANES), F32)],
        compiler_params=_cparams(ARB, ARB),
    )(proj, gains, cos, sins, *flat)


def _band_spec(seq_len, spec):
    width, idx = spec
    return pl.BlockSpec((None, None, seq_len, width), lambda b, r: (b, r, 0, idx))


def _fill_padded(dst, src_ref, rad, seq_len):
    z = jnp.zeros((rad, dst.shape[1]), dst.dtype)
    dst[0:rad, :] = z
    dst[rad + seq_len:rad + seq_len + rad, :] = z
    dst[rad:rad + seq_len, :] = src_ref[...]


def _band_fwd(src, qs, ks, vs, bias, sink, *, rad, nh, nkv, name):
    bl, dil, sl, _ = src.shape
    blk = bias.shape[1]
    kw = blk + 2 * rad
    nb = sl // blk
    rep = nh // nkv
    has_sink = sink is not None

    def body(*refs):
        q_ref, k_ref, v_ref, b_ref = refs[:4]
        s_ref = refs[4] if has_sink else None
        o_ref, l_ref, kp, vp = refs[4 + has_sink:]
        _fill_padded(kp, k_ref, rad, sl)
        _fill_padded(vp, v_ref, rad, sl)

        def blk_body(i, carry):
            r0 = pl.multiple_of(i * blk, blk)
            qb = q_ref[pl.ds(r0, blk), :]
            kwin = kp[pl.ds(r0, kw), :]
            vwin = vp[pl.ds(r0, kw), :]
            col = r0 - rad + lax.broadcasted_iota(jnp.int32, (blk, kw), 1)
            neg = jnp.where((col >= 0) & (col < sl), 0.0, NEG_INF).astype(F32)
            for h in range(nh):
                g = h // rep
                hs = slice(h * HEAD_DIM, (h + 1) * HEAD_DIM)
                gs = slice(g * HEAD_DIM, (g + 1) * HEAD_DIM)
                s = lax.dot_general(qb[:, hs], kwin[:, gs], NT, preferred_element_type=F32)
                s = s + b_ref[h] + neg
                m = jnp.max(s, axis=1, keepdims=True)
                if has_sink:
                    sk = s_ref[h][0:1, 0:1]
                    m = jnp.maximum(m, sk)
                p = jnp.exp(s - m)
                den = jnp.sum(p, axis=1, keepdims=True)
                if has_sink:
                    den = den + jnp.exp(sk - m)
                o = jnp.dot(p.astype(BF16), vwin[:, gs], preferred_element_type=F32) / den
                o_ref[pl.ds(r0, blk), hs] = o
                l_ref[pl.ds(r0, blk), hs] = jnp.broadcast_to(m + jnp.log(den), (blk, HEAD_DIM))
            return carry

        lax.fori_loop(0, nb, blk_body, 0)

    in_specs = [_band_spec(sl, qs), _band_spec(sl, ks), _band_spec(sl, vs),
                pl.BlockSpec((nh, blk, kw), lambda b, r: (0, 0, 0))]
    args = [src] * 3 + [bias]
    if has_sink:
        in_specs.append(pl.BlockSpec((nh, 8, 128), lambda b, r: (0, 0, 0)))
        args.append(sink)
    return pl.pallas_call(
        body, name=name, grid=(bl, dil), in_specs=in_specs,
        out_specs=[_band_spec(sl, (256, 0))] * 2,
        out_shape=[_sds((bl, dil, sl, 256), F32)] * 2,
        scratch_shapes=[pltpu.VMEM((sl + 2 * rad, ks[0]), BF16), pltpu.VMEM((sl + 2 * rad, vs[0]), BF16)],
        compiler_params=_cparams(PAR, PAR),
    )(*args)


def _band_bwd(src, qs, ks, vs, bias, sink, dy, dcol, lse, delta, *, rad, nh, nkv, name):
    bl, dil, sl, _ = src.shape
    blk = bias.shape[1]
    kw = blk + 2 * rad
    nb = sl // blk
    rep = nh // nkv
    has_sink = sink is not None
    wk, wv = ks[0], vs[0]

    def body(*refs):
        q_ref, k_ref, v_ref, b_ref = refs[:4]
        s_ref = refs[4] if has_sink else None
        do_ref, l_ref, dl_ref = refs[4 + has_sink:7 + has_sink]
        outs = refs[7 + has_sink:]
        if has_sink:
            dq_ref, dk_ref, dv_ref, db_ref, dsk_ref, kp, vp, dka, dva = outs
        else:
            dq_ref, dk_ref, dv_ref, db_ref, kp, vp, dka, dva = outs

        @pl.when((pl.program_id(0) == 0) & (pl.program_id(1) == 0))
        def _():
            db_ref[...] = jnp.zeros_like(db_ref)
            if has_sink:
                dsk_ref[...] = jnp.zeros_like(dsk_ref)

        _fill_padded(kp, k_ref, rad, sl)
        _fill_padded(vp, v_ref, rad, sl)
        dka[...] = jnp.zeros_like(dka)
        dva[...] = jnp.zeros_like(dva)

        def blk_body(i, carry):
            r0 = pl.multiple_of(i * blk, blk)
            qb = q_ref[pl.ds(r0, blk), :]
            kwin = kp[pl.ds(r0, kw), :]
            vwin = vp[pl.ds(r0, kw), :]
            dob = do_ref[pl.ds(r0, blk), :].astype(BF16)
            lb = l_ref[pl.ds(r0, blk), :]
            dlb = dl_ref[pl.ds(r0, blk), :]
            col = r0 - rad + lax.broadcasted_iota(jnp.int32, (blk, kw), 1)
            neg = jnp.where((col >= 0) & (col < sl), 0.0, NEG_INF).astype(F32)
            for h in range(nh):
                g = h // rep
                hs = slice(h * HEAD_DIM, (h + 1) * HEAD_DIM)
                gs = slice(g * HEAD_DIM, (g + 1) * HEAD_DIM)
                qh, kh, vh, doh = qb[:, hs], kwin[:, gs], vwin[:, gs], dob[:, hs]
                lh = lb[:, h * HEAD_DIM:h * HEAD_DIM + 1]
                dlh = dlb[:, h * HEAD_DIM:h * HEAD_DIM + 1]
                s = lax.dot_general(qh, kh, NT, preferred_element_type=F32) + b_ref[h] + neg
                p = jnp.exp(s - lh)
                dp = lax.dot_general(doh, vh, NT, preferred_element_type=F32)
                ds = p * (dp - dlh)
                dsb = ds.astype(BF16)
                dq_ref[pl.ds(r0, blk), hs] = jnp.dot(dsb, kh, preferred_element_type=F32)
                dka[pl.ds(r0, kw), gs] += lax.dot_general(dsb, qh, TN, preferred_element_type=F32)
                dva[pl.ds(r0, kw), gs] += lax.dot_general(p.astype(BF16), doh, TN, preferred_element_type=F32)
                db_ref[h] += ds
                if has_sink:
                    ps = jnp.exp(s_ref[h][0:1, 0:1] - lh)
                    dsk_ref[h] += jnp.broadcast_to(-jnp.sum(ps * dlh, axis=0, keepdims=True), (8, 128))
            return carry

        lax.fori_loop(0, nb, blk_body, 0)
        dk_ref[...] = dka[rad:rad + sl, :]
        dv_ref[...] = dva[rad:rad + sl, :]

    const3 = lambda b, r: (0, 0, 0)
    in_specs = [_band_spec(sl, qs), _band_spec(sl, ks), _band_spec(sl, vs), pl.BlockSpec((nh, blk, kw), const3)]
    args = [src] * 3 + [bias]
    if has_sink:
        in_specs.append(pl.BlockSpec((nh, 8, 128), const3))
        args.append(sink)
    row = _band_spec(sl, (256, 0))
    in_specs += [_band_spec(sl, (256, dcol)), row, row]
    args += [dy, lse, delta]
    out_specs = [row, _band_spec(sl, (wk, 0)), _band_spec(sl, (wv, 0)), pl.BlockSpec((nh, blk, kw), const3)]
    out_shape = [_sds((bl, dil, sl, 256), F32), _sds((bl, dil, sl, wk), F32), _sds((bl, dil, sl, wv), F32),
                 _sds((nh, blk, kw), F32)]
    if has_sink:
        out_specs.append(pl.BlockSpec((nh, 8, 128), const3))
        out_shape.append(_sds((nh, 8, 128), F32))
    return pl.pallas_call(
        body, name=name, grid=(bl, dil), in_specs=in_specs, out_specs=out_specs, out_shape=out_shape,
        scratch_shapes=[pltpu.VMEM((sl + 2 * rad, wk), BF16), pltpu.VMEM((sl + 2 * rad, wv), BF16),
                        pltpu.VMEM((sl + 2 * rad, wk), F32), pltpu.VMEM((sl + 2 * rad, wv), F32)],
        compiler_params=_cparams(ARB, ARB),
    )(*args)


def _combine_a(os_, ls_, *, tm, name):
    bl = os_[1].shape[0]
    t = bl * SEQ
    nt = SEQ // tm

    def body(o1, o4, o16, l1, l4, l16, y_ref, lt_ref, scr):
        for k, (d, ref) in enumerate(((4, o4), (16, o16), (4, l4), (16, l16))):
            for r in range(d):
                _write_residue(scr, 2 * k, r, d, ref[r])
        o2, o3, b, c = (_gather_cols(scr, 2 * k, 2) for k in range(4))
        a = l1[...]
        m = jnp.maximum(jnp.maximum(a, b), c)
        ea, eb, ec = jnp.exp(a - m), jnp.exp(b - m), jnp.exp(c - m)
        den = ea + eb + ec
        y_ref[...] = (ea / den) * o1[...] + (eb / den) * o2 + (ec / den) * o3
        lt_ref[...] = m + jnp.log(den)

    specs = _residue_specs(tm, 256, nt)
    return pl.pallas_call(
        body, name=name, grid=(bl, nt), in_specs=specs * 2, out_specs=[specs[0]] * 2,
        out_shape=[_sds((t, 256), F32)] * 2, scratch_shapes=[pltpu.VMEM((8, tm, LANES), F32)],
        compiler_params=_cparams(PAR, PAR),
    )(*os_, *ls_)


def _deltas(dycat, ya, yb, yd, lse_a, *, tm, name):
    t = ya.shape[0]
    bl = t // SEQ
    nt = SEQ // tm

    def body(dy_ref, ya_ref, yb_ref, yd_ref, la_ref, dy4, dy16, l4, l16, da1, da4, da16, db_ref, dd_ref, scr):
        e = _group_sum_matrix(256, True)
        dya = dy_ref[:, 0:256]
        dla = _seg_sum(dya * ya_ref[...], e)
        da1[...] = dla
        db_ref[...] = _seg_sum(dy_ref[:, 256:512] * yb_ref[...], e)
        dd_ref[...] = _seg_sum(dy_ref[:, 768:1024] * yd_ref[...], e)
        for k, (val, r4, r16) in enumerate(((dya, dy4, dy16), (la_ref[...], l4, l16), (dla, da4, da16))):
            _scatter_cols(scr, 2 * k, val)
            for d, ref in ((4, r4), (16, r16)):
                for r in range(d):
                    ref[r] = _read_residue(scr, 2 * k, 2, r, d)

    specs = _residue_specs(tm, 256, nt)
    nat = specs[0]
    shapes = _residue_shapes(bl, 256, F32)
    outs = pl.pallas_call(
        body, name=name, grid=(bl, nt),
        in_specs=[pl.BlockSpec((tm, 1024), lambda b, i: (b * nt + i, 0)), nat, nat, nat, nat],
        out_specs=specs[1:] + specs[1:] + specs + [nat, nat],
        out_shape=shapes[1:] + shapes[1:] + shapes + [shapes[0], shapes[0]],
        scratch_shapes=[pltpu.VMEM((6, tm, LANES), F32)],
        compiler_params=_cparams(PAR, PAR),
    )(dycat, ya, yb, yd, lse_a)
    return outs[0:2], outs[2:4], outs[4:7], outs[7], outs[8]


def _dense_fwd(qd, *, tq, name):
    t = qd.shape[0]
    bl = t // SEQ
    nq = SEQ // tq

    def body(q_ref, k_ref, v_ref, o_ref, l_ref):
        q = q_ref[...]
        for g in range(2):
            h0, h1 = 2 * g, 2 * g + 1
            q2 = jnp.concatenate([q[:, h0 * 64:(h0 + 1) * 64], q[:, h1 * 64:(h1 + 1) * 64]], axis=0)
            kg = k_ref[:, g * 64:(g + 1) * 64]
            vg = v_ref[:, g * 64:(g + 1) * 64]
            s = lax.dot_general(q2, kg, NT, preferred_element_type=F32)
            m = jnp.max(s, axis=1, keepdims=True)
            p = jnp.exp(s - m)
            den = jnp.sum(p, axis=1, keepdims=True)
            o2 = jnp.dot(p.astype(BF16), vg, preferred_element_type=F32) / den
            l2 = jnp.broadcast_to(m + jnp.log(den), (2 * tq, 64))
            o_ref[:, h0 * 64:(h0 + 1) * 64] = o2[:tq]
            o_ref[:, h1 * 64:(h1 + 1) * 64] = o2[tq:]
            l_ref[:, h0 * 64:(h0 + 1) * 64] = l2[:tq]
            l_ref[:, h1 * 64:(h1 + 1) * 64] = l2[tq:]

    q3 = qd.reshape(bl, SEQ, 512)
    o, lse = pl.pallas_call(
        body, name=name, grid=(bl, nq),
        in_specs=[pl.BlockSpec((None, tq, 256), lambda b, i: (b, i, 0)),
                  pl.BlockSpec((None, SEQ, 128), lambda b, i: (b, 0, 2)),
                  pl.BlockSpec((None, SEQ, 128), lambda b, i: (b, 0, 3))],
        out_specs=[pl.BlockSpec((None, tq, 256), lambda b, i: (b, i, 0))] * 2,
        out_shape=[_sds((bl, SEQ, 256), F32)] * 2,
        compiler_params=_cparams(PAR, PAR),
    )(q3, q3, q3)
    return o.reshape(t, 256), lse.reshape(t, 256)


def _dense_bwd(qd, dycat, lse, delta, *, tq, name):
    t = qd.shape[0]
    bl = t // SEQ
    nq = SEQ // tq

    def body(q_ref, k_ref, v_ref, do_ref, l_ref, dl_ref, dq_ref, dk_ref, dv_ref):
        @pl.when(pl.program_id(1) == 0)
        def _():
            dk_ref[...] = jnp.zeros_like(dk_ref)
            dv_ref[...] = jnp.zeros_like(dv_ref)

        q = q_ref[...]
        do = do_ref[...].astype(BF16)
        lv = l_ref[...]
        dlv = dl_ref[...]
        for g in range(2):
            h0, h1 = 2 * g, 2 * g + 1
            q2 = jnp.concatenate([q[:, h0 * 64:(h0 + 1) * 64], q[:, h1 * 64:(h1 + 1) * 64]], axis=0)
            do2 = jnp.concatenate([do[:, h0 * 64:(h0 + 1) * 64], do[:, h1 * 64:(h1 + 1) * 64]], axis=0)
            l2 = jnp.concatenate([lv[:, h0 * 64:h0 * 64 + 1], lv[:, h1 * 64:h1 * 64 + 1]], axis=0)
            dl2 = jnp.concatenate([dlv[:, h0 * 64:h0 * 64 + 1], dlv[:, h1 * 64:h1 * 64 + 1]], axis=0)
            kg = k_ref[:, g * 64:(g + 1) * 64]
            vg = v_ref[:, g * 64:(g + 1) * 64]
            s = lax.dot_general(q2, kg, NT, preferred_element_type=F32)
            p = jnp.exp(s - l2)
            dp = lax.dot_general(do2, vg, NT, preferred_element_type=F32)
            ds = (p * (dp - dl2)).astype(BF16)
            dq2 = jnp.dot(ds, kg, preferred_element_type=F32)
            dq_ref[:, h0 * 64:(h0 + 1) * 64] = dq2[:tq]
            dq_ref[:, h1 * 64:(h1 + 1) * 64] = dq2[tq:]
            dk_ref[:, g * 64:(g + 1) * 64] += lax.dot_general(ds, q2, TN, preferred_element_type=F32)
            dv_ref[:, g * 64:(g + 1) * 64] += lax.dot_general(p.astype(BF16), do2, TN, preferred_element_type=F32)

    q3 = qd.reshape(bl, SEQ, 512)
    tile = pl.BlockSpec((None, tq, 256), lambda b, i: (b, i, 0))
    full = pl.BlockSpec((None, SEQ, 128), lambda b, i: (b, 0, 0))
    dq, dk, dv = pl.pallas_call(
        body, name=name, grid=(bl, nq),
        in_specs=[tile, pl.BlockSpec((None, SEQ, 128), lambda b, i: (b, 0, 2)),
                  pl.BlockSpec((None, SEQ, 128), lambda b, i: (b, 0, 3)),
                  pl.BlockSpec((None, tq, 256), lambda b, i: (b, i, 3)), tile, tile],
        out_specs=[tile, full, full],
        out_shape=[_sds((bl, SEQ, 256), F32), _sds((bl, SEQ, 128), F32), _sds((bl, SEQ, 128), F32)],
        compiler_params=_cparams(PAR, ARB),
    )(q3, q3, q3, dycat.reshape(bl, SEQ, 1024), lse.reshape(bl, SEQ, 256), delta.reshape(bl, SEQ, 256))
    return dq.reshape(t, 256), dk.reshape(t, 128), dv.reshape(t, 128)


def _c_norm(cv, gam, bet):
    vg = _gelu(cv)
    mu = jnp.mean(vg, axis=-1, keepdims=True)
    xc = vg - mu
    r = lax.rsqrt(jnp.mean(xc * xc, axis=-1, keepdims=True) + EPS)
    xhat = xc * r
    return xhat * gam + bet, xhat, r


def _c_fwd(proj, gam, bet, ws, bst, *, tm, name):
    t = proj.shape[0]
    nch = tm // C_CHUNK

    def body(u_ref, v_ref, g_ref, b_ref, ws_ref, bs_ref, y_ref):
        vn, _, _ = _c_norm(v_ref[...], g_ref[...], b_ref[...])
        vnb = vn.astype(BF16)
        for c in range(nch):
            rows = slice(c * C_CHUNK, (c + 1) * C_CHUNK)
            for g in range(C_GROUPS):
                gs = slice(g * 64, (g + 1) * 64)
                mixed = jnp.dot(ws_ref[g], vnb[rows, gs], preferred_element_type=F32) + bs_ref[:, gs]
                y_ref[rows, gs] = _gelu(u_ref[rows, gs]) * mixed

    vec = pl.BlockSpec((1, 256), lambda i: (0, 0))
    return pl.pallas_call(
        body, name=name, grid=(t // tm,),
        in_specs=[pl.BlockSpec((tm, 256), lambda i: (i, 5)), pl.BlockSpec((tm, 256), lambda i: (i, 6)), vec, vec,
                  pl.BlockSpec((C_GROUPS, C_CHUNK, C_CHUNK), lambda i: (0, 0, 0)),
                  pl.BlockSpec((C_CHUNK, 256), lambda i: (0, 0))],
        out_specs=pl.BlockSpec((tm, 256), lambda i: (i, 0)), out_shape=_sds((t, 256), F32),
        compiler_params=_cparams(PAR),
    )(proj, proj, gam, bet, ws, bst)


def _c_bwd(proj, dycat, gam, bet, ws, wst, bst, *, tm, name):
    t = proj.shape[0]
    nch = tm // C_CHUNK
    nstep = t // tm

    def body(u_ref, v_ref, dy_ref, g_ref, b_ref, ws_ref, wst_ref, bs_ref,
             du_ref, dv_ref, dws_ref, dbs_ref, dg_ref, db_ref, dvn_s):
        step = pl.program_id(0)

        @pl.when(step == 0)
        def _():
            dws_ref[...] = jnp.zeros_like(dws_ref)
            dbs_ref[...] = jnp.zeros_like(dbs_ref)
            dg_ref[...] = jnp.zeros_like(dg_ref)
            db_ref[...] = jnp.zeros_like(db_ref)

        cv = v_ref[...]
        gam_v = g_ref[...]
        vn, xhat, r = _c_norm(cv, gam_v, b_ref[...])
        vnb = vn.astype(BF16)
        for c in range(nch):
            rows = slice(c * C_CHUNK, (c + 1) * C_CHUNK)
            for g in range(C_GROUPS):
                gs = slice(g * 64, (g + 1) * 64)
                cu = u_ref[rows, gs]
                dy = dy_ref[rows, gs]
                mixed = jnp.dot(ws_ref[g], vnb[rows, gs], preferred_element_type=F32) + bs_ref[:, gs]
                du_ref[rows, gs] = dy * mixed * _gelu_grad(cu)
                dmix = dy * _gelu(cu)
                dbs_ref[:, gs] += dmix
                dmb = dmix.astype(BF16)
                dws_ref[g] += lax.dot_general(dmb, vnb[rows, gs], NT, preferred_element_type=F32)
                dvn_s[rows, gs] = jnp.dot(wst_ref[g], dmb, preferred_element_type=F32)
        dvn = dvn_s[...]
        dg_ref[...] += jnp.sum(dvn * xhat, axis=0, keepdims=True)
        db_ref[...] += jnp.sum(dvn, axis=0, keepdims=True)
        dxh = dvn * gam_v
        dvg = r * (dxh - jnp.mean(dxh, axis=-1, keepdims=True) - xhat * jnp.mean(dxh * xhat, axis=-1, keepdims=True))
        dv_ref[...] = dvg * _gelu_grad(cv)

        @pl.when(step == nstep - 1)
        def _():
            dbs_ref[...] = _seg_sum(dbs_ref[...], _group_sum_matrix(256, True))

    vec = pl.BlockSpec((1, 256), lambda i: (0, 0))
    mat = pl.BlockSpec((C_GROUPS, C_CHUNK, C_CHUNK), lambda i: (0, 0, 0))
    bsp = pl.BlockSpec((C_CHUNK, 256), lambda i: (0, 0))
    tile = pl.BlockSpec((tm, 256), lambda i: (i, 0))
    return pl.pallas_call(
        body, name=name, grid=(nstep,),
        in_specs=[pl.BlockSpec((tm, 256), lambda i: (i, 5)), pl.BlockSpec((tm, 256), lambda i: (i, 6)),
                  pl.BlockSpec((tm, 256), lambda i: (i, 2)), vec, vec, mat, mat, bsp],
        out_specs=[tile, tile, mat, bsp, vec, vec],
        out_shape=[_sds((t, 256), F32), _sds((t, 256), F32), _sds((C_GROUPS, C_CHUNK, C_CHUNK), F32),
                   _sds((C_CHUNK, 256), F32), _sds((1, 256), F32), _sds((1, 256), F32)],
        scratch_shapes=[pltpu.VMEM((tm, 256), F32)],
        compiler_params=_cparams(ARB),
    )(proj, proj, dycat, gam, bet, ws, wst, bst)


FF_TC = 128
FF_NB = D_FF // FF_TC
FF_CH = 64
FF_HALO = 16


def _edge_taps(ref, first):
    if first:
        ext = ref[0:FF_CH + FF_HALO, :].astype(F32)
        body = slice(0, FF_CH)
    else:
        ext = ref[SEQ - FF_CH - FF_HALO:SEQ, :].astype(F32)
        body = slice(FF_HALO, FF_HALO + FF_CH)
    n = ext.shape[0]
    row = lax.broadcasted_iota(jnp.int32, ext.shape, 0)
    dn = pltpu.roll(ext, 1, 0)
    up = pltpu.roll(ext, n - 1, 0)
    if first:
        dn = jnp.where(row == 0, 0.0, dn)
    else:
        up = jnp.where(row == n - 1, 0.0, up)
    return dn[body], ext[body], up[body]


def _mid_taps(ref, r0):
    ext = ref[pl.ds(pl.multiple_of(r0 - FF_HALO, FF_HALO), FF_CH + 2 * FF_HALO), :].astype(F32)
    n = ext.shape[0]
    body = slice(FF_HALO, FF_HALO + FF_CH)
    return pltpu.roll(ext, 1, 0)[body], ext[body], pltpu.roll(ext, n - 1, 0)[body]


def _chunk_loop(step):
    step(0, lambda ref: _edge_taps(ref, True))

    def mid(i, carry):
        r0 = pl.multiple_of(i * FF_CH, FF_CH)
        step(r0, lambda ref: _mid_taps(ref, r0))
        return carry

    lax.fori_loop(1, SEQ // FF_CH - 1, mid, 0)
    step(SEQ - FF_CH, lambda ref: _edge_taps(ref, False))


def _conv3(taps, w_ref, b_ref):
    dn, md, up = taps
    return w_ref[0:1, :] * dn + w_ref[1:2, :] * md + w_ref[2:3, :] * up + b_ref[...]


def _ff_specs(order):
    def at(fn):
        return (lambda b, j: fn(b, j)) if order == "bj" else (lambda j, b: fn(b, j))
    hs = [pl.BlockSpec((None, SEQ, FF_TC), at(lambda b, j, o=o: (b, 0, j + o))) for o in (0, FF_NB)]
    ws = [pl.BlockSpec((3, FF_TC), at(lambda b, j, o=o: (0, j + o))) for o in (0, FF_NB)]
    bs = [pl.BlockSpec((1, FF_TC), at(lambda b, j, o=o: (0, j + o))) for o in (0, FF_NB)]
    return hs, ws, bs


def _conv_gate_fwd(h, cw, cb, *, name):
    t = h.shape[0]
    bl = t // SEQ

    def body(hg_ref, hu_ref, wg_ref, wu_ref, bg_ref, bu_ref, a_ref):
        def step(r0, taps):
            cg = _conv3(taps(hg_ref), wg_ref, bg_ref)
            cu = _conv3(taps(hu_ref), wu_ref, bu_ref)
            a_ref[pl.ds(r0, FF_CH), :] = (cg * _sigmoid(cg) * cu).astype(BF16)

        _chunk_loop(step)

    hs, ws, bs = _ff_specs("bj")
    h3 = h.reshape(bl, SEQ, 2 * D_FF)
    act = pl.pallas_call(
        body, name=name, grid=(bl, FF_NB), in_specs=hs + ws + bs,
        out_specs=pl.BlockSpec((None, SEQ, FF_TC), lambda b, j: (b, 0, j)),
        out_shape=_sds((bl, SEQ, D_FF), BF16),
        compiler_params=_cparams(PAR, PAR),
    )(h3, h3, cw, cw, cb, cb)
    return act.reshape(t, D_FF)


def _conv_gate_bwd(h, dact, cw, cb, *, name):
    t = h.shape[0]
    bl = t // SEQ

    def body(hg_ref, hu_ref, wg_ref, wu_ref, bg_ref, bu_ref, da_ref,
             dhg_ref, dhu_ref, dwg_ref, dwu_ref, dbg_ref, dbu_ref, dg_s, du_s):
        @pl.when(pl.program_id(1) == 0)
        def _():
            for ref in (dwg_ref, dwu_ref, dbg_ref, dbu_ref):
                ref[...] = jnp.zeros_like(ref)

        red = lambda x: jnp.sum(x, axis=0, keepdims=True)

        def pass1(r0, taps):
            tg, tu = taps(hg_ref), taps(hu_ref)
            cg = _conv3(tg, wg_ref, bg_ref)
            cu = _conv3(tu, wu_ref, bu_ref)
            da = da_ref[pl.ds(r0, FF_CH), :].astype(F32)
            sg = _sigmoid(cg)
            dcg = da * cu * (sg * (1.0 + cg * (1.0 - sg)))
            dcu = da * (cg * sg)
            dg_s[pl.ds(r0, FF_CH), :] = dcg
            du_s[pl.ds(r0, FF_CH), :] = dcu
            for d, tp, dw_ref, db_ref in ((dcg, tg, dwg_ref, dbg_ref), (dcu, tu, dwu_ref, dbu_ref)):
                for k in range(3):
                    dw_ref[k:k + 1, :] += red(d * tp[k])
                db_ref[...] += red(d)

        _chunk_loop(pass1)

        def pass2(r0, taps):
            for s, w_ref, o_ref in ((dg_s, wg_ref, dhg_ref), (du_s, wu_ref, dhu_ref)):
                dn, md, up = taps(s)
                o_ref[pl.ds(r0, FF_CH), :] = (w_ref[0:1, :] * up + w_ref[1:2, :] * md + w_ref[2:3, :] * dn).astype(BF16)

        _chunk_loop(pass2)

    hs, ws, bs = _ff_specs("jb")
    half = pl.BlockSpec((None, SEQ, FF_TC), lambda j, b: (b, 0, j))
    wsp = pl.BlockSpec((3, FF_TC), lambda j, b: (0, j))
    bsp = pl.BlockSpec((1, FF_TC), lambda j, b: (0, j))
    h3 = h.reshape(bl, SEQ, 2 * D_FF)
    dhg, dhu, dwg, dwu, dbg, dbu = pl.pallas_call(
        body, name=name, grid=(FF_NB, bl), in_specs=hs + ws + bs + [half],
        out_specs=[half, half, wsp, wsp, bsp, bsp],
        out_shape=[_sds((bl, SEQ, D_FF), BF16), _sds((bl, SEQ, D_FF), BF16), _sds((3, D_FF), F32), _sds((3, D_FF), F32),
                   _sds((1, D_FF), F32), _sds((1, D_FF), F32)],
        scratch_shapes=[pltpu.VMEM((SEQ, FF_TC), F32), pltpu.VMEM((SEQ, FF_TC), F32)],
        compiler_params=_cparams(PAR, ARB),
    )(h3, h3, cw, cw, cb, cb, dact.reshape(bl, SEQ, D_FF))
    return (dhg.reshape(t, D_FF), dhu.reshape(t, D_FF), jnp.concatenate([dwg, dwu], axis=1),
            jnp.concatenate([dbg, dbu], axis=1))


def _ple_fwd(x2, gain, wg, pe, pe_blk, wp, *, tm, tn, name):
    t, k = x2.shape
    n = wg.shape[1]

    def body(x_ref, g_ref, wg_ref, pe_ref, wp_ref, xr_ref, hn_ref, x3_ref, gt_ref, pp_ref, hn_s):
        @pl.when(pl.program_id(1) == 0)
        def _():
            x = x_ref[...]
            r = lax.rsqrt(jnp.mean(x * x, axis=-1, keepdims=True) + EPS)
            hn_s[...] = (x * r * g_ref[...]).astype(BF16)
            hn_ref[...] = hn_s[...]

        gate = _sigmoid(jnp.dot(hn_s[...], wg_ref[...], preferred_element_type=F32))
        pp = jnp.dot(pe_ref[...].astype(BF16), wp_ref[...], preferred_element_type=F32)
        gt_ref[...] = gate
        pp_ref[...] = pp
        x3_ref[...] = xr_ref[...] + pp * gate

    tile = pl.BlockSpec((tm, tn), lambda i, j: (i, j))
    return pl.pallas_call(
        body, name=name, grid=(t // tm, n // tn),
        in_specs=[pl.BlockSpec((tm, k), lambda i, j: (i, 0)), pl.BlockSpec((1, k), lambda i, j: (0, 0)),
                  pl.BlockSpec((k, tn), lambda i, j: (0, j)), pl.BlockSpec((tm, PLE_DIM), lambda i, j: (pe_blk + i, 0)),
                  pl.BlockSpec((PLE_DIM, tn), lambda i, j: (0, j)), tile],
        out_specs=[pl.BlockSpec((tm, k), lambda i, j: (i, 0)), tile, tile, tile],
        out_shape=[_sds((t, k), BF16), _sds((t, n), F32), _sds((t, n), F32), _sds((t, n), F32)],
        scratch_shapes=[pltpu.VMEM((tm, k), BF16)],
        compiler_params=_cparams(PAR, ARB),
    )(x2, gain, wg, pe, wp, x2)


def _ple_bwd_ew(dx3, gate, pp, *, tm, name):
    t, n = dx3.shape

    def body(d_ref, g_ref, p_ref, dz_ref, dpp_ref):
        d, g = d_ref[...], g_ref[...]
        dz_ref[...] = (d * p_ref[...] * g * (1.0 - g)).astype(BF16)
        dpp_ref[...] = (d * g).astype(BF16)

    spec = pl.BlockSpec((tm, n), lambda i: (i, 0))
    return pl.pallas_call(
        body, name=name, grid=(t // tm,), in_specs=[spec] * 3, out_specs=[spec] * 2,
        out_shape=[_sds((t, n), BF16)] * 2, compiler_params=_cparams(PAR),
    )(dx3, gate, pp)


def _loss_head(y, tgt, *, tm, name):
    t, d = y.shape

    def body(y_ref, t_ref, l_ref, dy_ref):
        @pl.when(pl.program_id(0) == 0)
        def _():
            l_ref[...] = jnp.zeros_like(l_ref)

        e = y_ref[...] - t_ref[...]
        dy_ref[...] = e * (1.0 / d)
        s = jnp.sum(jnp.sum(e * e, axis=1, keepdims=True), axis=0, keepdims=True)
        l_ref[...] += jnp.broadcast_to(s * (0.5 / d), (8, 128))

    spec = pl.BlockSpec((tm, d), lambda i: (i, 0))
    return pl.pallas_call(
        body, name=name, grid=(t // tm,), in_specs=[spec, spec],
        out_specs=[pl.BlockSpec((8, 128), lambda i: (0, 0)), spec],
        out_shape=[_sds((8, 128), F32), _sds((t, d), F32)], compiler_params=_cparams(ARB),
    )(y, tgt)


BIAS_PC = 8192


def _onehot(bucket_row):
    rows = lax.broadcasted_iota(jnp.int32, (REL_BUCKETS, bucket_row.shape[1]), 0)
    return (rows == bucket_row).astype(F32)


def _bias_lookup(table_t, bucket, *, name):
    h = table_t.shape[0]
    p = bucket.shape[1]

    def body(t_ref, b_ref, o_ref):
        bk = b_ref[...]
        val = jnp.dot(t_ref[...], _onehot(bk), precision=HI, preferred_element_type=F32)
        o_ref[...] = jnp.where(bk >= 0, val, NEG_INF)

    return pl.pallas_call(
        body, name=name, grid=(p // BIAS_PC,),
        in_specs=[pl.BlockSpec((h, REL_BUCKETS), lambda i: (0, 0)), pl.BlockSpec((1, BIAS_PC), lambda i: (0, i))],
        out_specs=pl.BlockSpec((h, BIAS_PC), lambda i: (0, i)), out_shape=_sds((h, p), F32),
        compiler_params=_cparams(PAR),
    )(table_t, bucket)


def _bucket_reduce(dbias, bucket, *, name):
    h, p = dbias.shape

    def body(d_ref, b_ref, o_ref):
        @pl.when(pl.program_id(0) == 0)
        def _():
            o_ref[...] = jnp.zeros_like(o_ref)

        o_ref[...] += lax.dot_general(d_ref[...], _onehot(b_ref[...]), NT, precision=HI, preferred_element_type=F32)

    return pl.pallas_call(
        body, name=name, grid=(p // BIAS_PC,),
        in_specs=[pl.BlockSpec((h, BIAS_PC), lambda i: (0, i)), pl.BlockSpec((1, BIAS_PC), lambda i: (0, i))],
        out_specs=pl.BlockSpec((h, REL_BUCKETS), lambda i: (0, 0)), out_shape=_sds((h, REL_BUCKETS), F32),
        compiler_params=_cparams(ARB),
    )(dbias, bucket)


def _adamw_math(w, g, m, v):
    m = ADAM_B1 * m + (1.0 - ADAM_B1) * g
    v = ADAM_B2 * v + (1.0 - ADAM_B2) * (g * g)
    m_hat = m / (1.0 - ADAM_B1 ** ADAM_STEP)
    v_hat = v / (1.0 - ADAM_B2 ** ADAM_STEP)
    delta = -ADAM_LR * (m_hat / (jnp.sqrt(v_hat) + ADAM_EPS) + ADAM_WD * w)
    return delta, m, v


def _adamw_reduce(parts, w, m, v, *, tr, name):
    nl = len(parts)
    rows, c = w.shape
    r = rows // nl
    nt = r // tr

    def body(*refs):
        p_refs = refs[:nl]
        w_ref, m_ref, v_ref, g_ref, d_ref, nm_ref, nv_ref = refs[nl:]
        for li, p_ref in enumerate(p_refs):
            @pl.when(pl.program_id(0) == li)
            def _(p_ref=p_ref):
                g = p_ref[0].astype(F32)
                for k in range(1, N_DEV):
                    g = g + p_ref[k].astype(F32)
                d, nm, nv = _adamw_math(w_ref[...], g, m_ref[...], v_ref[...])
                g_ref[...] = g
                d_ref[...] = d
                nm_ref[...] = nm
                nv_ref[...] = nv

    def part_map(li):
        return lambda l, i: (0, jnp.where(l == li, i, jnp.where(l < li, 0, nt - 1)), 0)

    spec = pl.BlockSpec((tr, c), lambda l, i: (l * nt + i, 0))
    return pl.pallas_call(
        body, name=name, grid=(nl, nt),
        in_specs=[pl.BlockSpec((N_DEV, tr, c), part_map(li)) for li in range(nl)] + [spec, spec, spec],
        out_specs=[spec] * 4, out_shape=[_sds((rows, c), F32)] * 4, compiler_params=_cparams(ARB, ARB),
    )(*parts, w, m, v)


def _adamw_plain(g, w, m, v, *, name):
    def body(g_ref, w_ref, m_ref, v_ref, d_ref, nm_ref, nv_ref):
        d, nm, nv = _adamw_math(w_ref[...], g_ref[...], m_ref[...], v_ref[...])
        d_ref[...] = d
        nm_ref[...] = nm
        nv_ref[...] = nv

    return pl.pallas_call(body, name=name, out_shape=[_sds(w.shape, F32)] * 3)(g, w, m, v)


def _mesh_pos():
    return lax.axis_index("x"), lax.axis_index("y"), lax.axis_index("c")


def _allgather_body(x_refs, out_refs, send_sems, recv_sems, local_sems, slot):
    x, y, c = _mesh_pos()
    me, sibling = (x, y, c), (x, y, 1 - c)
    chips = [(1 - x, y), (x, 1 - y), (1 - x, 1 - y)]
    waits = []
    for a, (x_ref, out_ref) in enumerate(zip(x_refs, out_refs)):
        def copy(k, block, to, src=None, out_ref=out_ref, a=a):
            return pltpu.make_async_remote_copy(
                src_ref=slot(out_ref, block) if src is None else src, dst_ref=slot(out_ref, block),
                send_sem=send_sems.at[a, k], recv_sem=recv_sems.at[a, k], device_id=to, device_id_type=MESH)

        mine = pltpu.make_async_copy(x_ref, slot(out_ref, me), local_sems.at[a])
        mine.start()
        first = [copy(0, me, sibling, src=x_ref)]
        first += [copy(1 + j, me, (*chip, c), src=x_ref) for j, chip in enumerate(chips)]
        for cp in first:
            cp.start()
        waits.append((copy, mine, first))
    sends = []
    for copy, mine, first in waits:
        passed = [copy(4 + j, (*chip, c), sibling) for j, chip in enumerate(chips)]
        for j, chip in enumerate(chips):
            copy(1 + j, (*chip, c), me).wait_recv()
            passed[j].start()
        sends.append(passed)
    for (copy, mine, first), passed in zip(waits, sends):
        copy(0, sibling, me).wait_recv()
        for j, chip in enumerate(chips):
            copy(4 + j, (*chip, 1 - c), me).wait_recv()
        for cp in first + passed:
            cp.wait_send()
        mine.wait()


PEER_FLIPS = ((0, 0, 1), (1, 0, 0), (0, 1, 0), (1, 1, 0), (1, 0, 1), (0, 1, 1), (1, 1, 1))
HBM_SPEC = pl.BlockSpec(memory_space=pltpu.HBM)
SEM_SPEC = pl.BlockSpec(memory_space=pltpu.SEMAPHORE)
DATAFLOW = pltpu.SideEffectType.DATAFLOW_SIDE_EFFECTING


def _peer_copies(x_refs, land_refs, send_sem, recv_sem, scatter):
    x, y, c = _mesh_pos()
    me = 4 * x + 2 * y + c
    copies = []
    for x_ref, land_ref in zip(x_refs, land_refs):
        for fx, fy, fc in PEER_FLIPS:
            px, py, pc = x ^ fx, y ^ fy, c ^ fc
            src = x_ref.at[4 * px + 2 * py + pc] if scatter else x_ref
            copies.append(pltpu.make_async_remote_copy(
                src_ref=src, dst_ref=land_ref.at[me], send_sem=send_sem, recv_sem=recv_sem,
                device_id=(px, py, pc), device_id_type=MESH))
    return copies


def _exchange_start(groups, after, *, scatter, name):
    xs = [x for grp in groups for x in grp]
    na = len(xs)
    ngrp = len(groups)
    firsts = np.cumsum([0] + [len(grp) for grp in groups])
    land_shapes = [x.shape if scatter else (N_DEV,) + x.shape for x in xs]
    extra = [] if after is None else [after]

    def body(*refs):
        x_refs, land_refs = refs[:na], refs[na:2 * na]
        sems = refs[2 * na + len(extra):2 * na + len(extra) + 2 * ngrp]
        token, local_sem = refs[-2], refs[-1]
        xm, ym, cm = _mesh_pos()
        me = 4 * xm + 2 * ym + cm
        for x_ref, land_ref in zip(x_refs, land_refs):
            own = pltpu.make_async_copy(x_ref.at[me] if scatter else x_ref, land_ref.at[me], local_sem)
            own.start()
            own.wait()
        for gi in range(ngrp):
            lo, hi = firsts[gi], firsts[gi + 1]
            for cp in _peer_copies(x_refs[lo:hi], land_refs[lo:hi], sems[2 * gi], sems[2 * gi + 1], scatter):
                cp.start()
        token[...] = jnp.zeros_like(token)

    sem_shapes = [pltpu.SemaphoreType.DMA(())] * (2 * ngrp)
    lands = [pltpu.with_memory_space_constraint(lax.empty(s, x.dtype), pltpu.HBM) for s, x in zip(land_shapes, xs)]
    outs = pl.pallas_call(
        body, name=name,
        in_specs=[HBM_SPEC] * (2 * na) + [pl.BlockSpec(memory_space=pl.ANY)] * len(extra),
        out_specs=[SEM_SPEC] * (2 * ngrp) + [HBM_SPEC] * (2 * na) + [pl.BlockSpec(memory_space=pltpu.VMEM)],
        out_shape=sem_shapes + [pltpu.HBM(x.shape, x.dtype) for x in xs]
        + [pltpu.HBM(s, x.dtype) for s, x in zip(land_shapes, xs)] + [_sds((8, 128), F32)],
        input_output_aliases={i: 2 * ngrp + i for i in range(2 * na)},
        scratch_shapes=[pltpu.SemaphoreType.DMA],
        compiler_params=pltpu.CompilerParams(has_side_effects=DATAFLOW),
    )(*[pltpu.with_memory_space_constraint(x, pltpu.HBM) for x in xs], *lands, *extra)
    sems, thru, token = outs[:2 * ngrp], outs[2 * ngrp:2 * ngrp + 2 * na], outs[-1]
    handles = []
    for gi in range(ngrp):
        lo, hi = firsts[gi], firsts[gi + 1]
        handles.append((sems[2 * gi], sems[2 * gi + 1], thru[lo:hi], thru[na + lo:na + hi]))
    return handles, token


def _exchange_wait(handle, after, *, scatter, name):
    send_sems, recv_sems, x_thru, land_thru = handle
    na = len(x_thru)

    def body(*refs):
        x_refs, land_refs = refs[:na], refs[na:2 * na]
        send_ref, recv_ref = refs[2 * na], refs[2 * na + 1]
        for cp in _peer_copies(x_refs, land_refs, send_ref, recv_ref, scatter):
            cp.wait_send()
            cp.wait_recv()

    outs = pl.pallas_call(
        body, name=name,
        in_specs=[HBM_SPEC] * (2 * na) + [SEM_SPEC, SEM_SPEC, pl.BlockSpec(memory_space=pl.ANY)],
        out_specs=[HBM_SPEC] * (2 * na),
        out_shape=[pltpu.HBM(a.shape, a.dtype) for a in list(x_thru) + list(land_thru)],
        input_output_aliases={i: i for i in range(2 * na)},
        compiler_params=pltpu.CompilerParams(has_side_effects=DATAFLOW),
    )(*x_thru, *land_thru, send_sems, recv_sems, after)
    return outs[na:]


def _allgather_vmem(x, *, reduce, name):
    r, c = x.shape

    def body(x_ref, out_ref, *rest):
        if reduce:
            gath, send_sems, recv_sems, local_sems = rest
        else:
            send_sems, recv_sems, local_sems = rest
            gath = out_ref
        _allgather_body([x_ref], [gath], send_sems, recv_sems, local_sems,
                        lambda ref, pos: ref.at[pl.ds((4 * pos[0] + 2 * pos[1] + pos[2]) * r, r), :])
        if reduce:
            acc = gath[0:r, :]
            for k in range(1, N_DEV):
                acc = acc + gath[k * r:(k + 1) * r, :]
            out_ref[...] = acc

    vm = pl.BlockSpec(memory_space=pltpu.VMEM)
    scratch = [pltpu.SemaphoreType.DMA((1, 7)), pltpu.SemaphoreType.DMA((1, 7)), pltpu.SemaphoreType.DMA((1,))]
    if reduce:
        scratch = [pltpu.VMEM((N_DEV * r, c), x.dtype)] + scratch
    return pl.pallas_call(
        body, name=name, in_specs=[vm], out_specs=vm,
        out_shape=_sds((r, c) if reduce else (N_DEV * r, c), x.dtype), scratch_shapes=scratch,
    )(x)


def _t5_bucket(rel):
    nb = REL_BUCKETS // 2
    ret = jnp.where(rel > 0, nb, 0)
    n = jnp.abs(rel)
    max_exact = nb // 2
    nf = jnp.maximum(n, 1).astype(F32)
    large = max_exact + (jnp.log(nf / max_exact) / math.log(REL_MAX_DIST / max_exact)
                         * (nb - max_exact)).astype(jnp.int32)
    large = jnp.minimum(large, nb - 1)
    return ret + jnp.where(n < max_exact, n, large)


def _band_pattern(block, radius, dil):
    kw = block + 2 * radius
    rel = jnp.arange(kw)[None, :] - radius - jnp.arange(block)[:, None]
    return jnp.where(jnp.abs(rel) <= radius, _t5_bucket(rel * dil), -1).astype(jnp.int32).reshape(1, block * kw)


def _rope_tables():
    lane = np.arange(64)
    seg, j = lane // 32, lane % 32
    inv = ROPE_THETA ** (-jnp.arange(0, 32, 2, dtype=F32) / 32)
    tpos = jnp.arange(SEQ)
    pos = jnp.where(jnp.asarray(seg)[None, :] == 0, (tpos // GRID_W)[:, None], (tpos % GRID_W)[:, None])
    ang = pos.astype(F32) * inv[jnp.asarray(j % 16)][None, :]
    cos = jnp.cos(ang)
    sins = jnp.where(jnp.asarray(j)[None, :] < 16, -jnp.sin(ang), jnp.sin(ang))
    return jnp.tile(cos, (1, 4)), jnp.tile(sins, (1, 4))


A_Q, A_K, A_V = (256, 0), (256, 1), (256, 2)
B_Q, B_K, B_V = (256, 0), (128, 2), (128, 3)
A_HEADS = dict(rad=A_RADIUS, nh=4, nkv=4)
B_HEADS = dict(rad=SWA_RADIUS, nh=4, nkv=2)


def _pin(arr, token):
    return arr if token is None else arr + token[0:1, 0:1]


def _local_step(x, pe, tgt, rel_bias, wts, matmul_weights, grads_ready):
    t = x.shape[0]
    bl = t // SEQ
    cos, sins = _rope_tables()
    blocks_a = [min(BAND_BLOCK, SEQ // d) for d in DILATIONS]
    pats_a = [_band_pattern(blk, A_RADIUS, d) for blk, d in zip(blocks_a, DILATIONS)]
    pat_b = _band_pattern(BAND_BLOCK, SWA_RADIUS, 1)
    table_t = rel_bias.T
    bias_a = [_bias_lookup(table_t[:4], pt, name=f"bias_a{ci}").reshape(4, blk, blk + 2 * A_RADIUS)
              for ci, (pt, blk) in enumerate(zip(pats_a, blocks_a))]
    bias_b = _bias_lookup(table_t[4:], pat_b, name="bias_b").reshape(4, BAND_BLOCK, BAND_BLOCK + 2 * SWA_RADIUS)
    nat4 = lambda a: a.reshape(bl, 1, SEQ, a.shape[-1])

    saved = []
    for li in range(DEPTH):
        w = dict(wts[li])
        w.update(matmul_weights(li, "in", x)[0])
        hn0, proj = _norm_mm((x,), w["g_mix"], w["w_in"], None, tm=1024, tn=1152, name="mix_in_fwd")
        more, started = matmul_weights(li, "rest", proj)
        w.update(more)
        w["out_gain"] = _pin(w["out_gain"], started)
        qa1, qa4, qa16, qb, qd = _qkprep_fwd(proj, w["qk_gains"], cos, sins, tm=512, name="qkprep_fwd")
        qa = (nat4(qa1), qa4, qa16)
        oa, la = [], []
        for ci in range(3):
            o, l = _band_fwd(qa[ci], A_Q, A_K, A_V, bias_a[ci], None, name=f"band_a{ci}_fwd", **A_HEADS)
            oa.append(o)
            la.append(l)
        oa[0], la[0] = oa[0].reshape(t, 256), la[0].reshape(t, 256)
        ya, lse_a = _combine_a(oa, la, tm=512, name="combine_a")
        yb, lse_b = _band_fwd(nat4(qb), B_Q, B_K, B_V, bias_b, w["sink_t"], name="band_b_fwd", **B_HEADS)
        yb = yb.reshape(t, 256)
        yc = _c_fwd(proj, w["c_g"], w["c_b"], w["c_ws"], w["c_bst"], tm=512, name="c_fwd")
        yd, lse_d = _dense_fwd(qd, tq=128, name="dense_fwd")
        mixed, x1 = _norm_mm((ya, yb, yc, yd), w["out_gain"], w["w_out"], x, tm=1024, tn=1024, name="mix_out_fwd")
        hn1, h = _norm_mm((x1,), w["g_ffn"], w["w_up"], None, tm=1024, tn=1408, name="ffn_up_fwd", out_dtype=BF16)
        act = _conv_gate_fwd(h, w["conv_w"], w["conv_b"], name="conv_gate_fwd")
        x2 = _mm(act, w["w_down"], "nn", x1, tm=1024, tn=1024, out_dtype=F32, name="ffn_down_fwd")
        hn2, x3, gate, pp = _ple_fwd(x2, w["g_ple"], w["w_gate"], pe, li * (t // 1024), w["w_proj"], tm=1024, tn=512,
                                     name="ple_fwd")
        saved.append(dict(w=w, x0=x, hn0=hn0, proj=proj, qa=qa, qb=qb, qd=qd, ya=ya, lse_a=lse_a, yb=yb, lse_b=lse_b,
                          yc=yc, yd=yd, lse_d=lse_d, mixed=mixed, x1=x1, hn1=hn1, h=h, act=act, x2=x2, hn2=hn2,
                          gate=gate, pp=pp))
        x = x3

    loss_tile, dx = _loss_head(x, tgt, tm=512, name="loss_head")
    grads = [None] * DEPTH
    d_table_a = jnp.zeros((4, REL_BUCKETS), F32)
    d_table_b = jnp.zeros((4, REL_BUCKETS), F32)
    token = None
    for li in reversed(range(DEPTH)):
        s = saved[li]
        w = s["w"]
        g = {}
        w["g_ple"] = _pin(w["g_ple"], token)
        dz, dpp = _ple_bwd_ew(dx, s["gate"], s["pp"], tm=512, name="ple_bwd_ew")
        g["w_gate"] = _mm(s["hn2"], dz, "tn", None, tm=1024, tn=512, out_dtype=BF16, name="dw_gate")
        g["w_proj"] = _mm(pe, dpp, "tn", None, tm=256, tn=1024, out_dtype=BF16, name="dw_proj", a_rows=(li, t))
        dx2, dx2b, g["g_ple"] = _mm_bt_normbwd((dz,), w["w_gate"], (s["x2"],), w["g_ple"], dx, tm=1024, tn=1024,
                                               name="ple_bwd", emit_bf16=True)
        g["w_down"] = _mm(s["act"], dx2b, "tn", None, tm=1408, tn=512, out_dtype=BF16, name="dw_down")
        dact = _mm(dx2b, w["w_down"], "nt", None, tm=1024, tn=1408, out_dtype=BF16, name="ffn_down_bwd")
        dhg, dhu, g["conv_w"], g["conv_b"] = _conv_gate_bwd(s["h"], dact, w["conv_w"], w["conv_b"], name="conv_gate_bwd")
        g["w_up"] = jnp.concatenate(
            [_mm(s["hn1"], dhalf, "tn", None, tm=1024, tn=1408, out_dtype=BF16, name=f"dw_up_{nm}")
             for nm, dhalf in (("gate", dhg), ("up", dhu))], axis=1)
        g_ffn = _pin(w["g_ffn"], grads_ready(li, "mid", g))
        dx1, dx1b, g["g_ffn"] = _mm_bt_normbwd((dhg, dhu), w["w_up"], (s["x1"],), g_ffn, dx2, tm=1024, tn=1408,
                                               name="ffn_up_bwd", emit_bf16=True)
        g["w_out"] = _mm(s["mixed"], dx1b, "tn", None, tm=1024, tn=512, out_dtype=BF16, name="dw_out")
        dycat, g["out_gain"] = _mm_bt_normbwd((dx1b,), w["w_out"], (s["ya"], s["yb"], s["yc"], s["yd"]), w["out_gain"],
                                              None, tm=1024, tn=1024, name="mix_out_bwd")
        dy_r, lse_r, dl_a, dl_b, dl_d = _deltas(dycat, s["ya"], s["yb"], s["yd"], s["lse_a"], tm=512, name="deltas")
        dy_a = (nat4(dycat),) + tuple(dy_r)
        lse_a = (nat4(s["lse_a"]),) + tuple(lse_r)
        dl_a = (nat4(dl_a[0]),) + tuple(dl_a[1:])
        da = []
        for ci in range(3):
            dq, dk, dv, dbias = _band_bwd(s["qa"][ci], A_Q, A_K, A_V, bias_a[ci], None, dy_a[ci], 0, lse_a[ci],
                                          dl_a[ci], name=f"band_a{ci}_bwd", **A_HEADS)
            if ci == 0:
                dq, dk, dv = (a.reshape(t, 256) for a in (dq, dk, dv))
            da.append((dq, dk, dv))
            d_table_a = d_table_a + _bucket_reduce(dbias.reshape(4, -1), pats_a[ci], name=f"bucket_a{ci}")
        dqb, dkb, dvb, dbias_b, dsink = _band_bwd(nat4(s["qb"]), B_Q, B_K, B_V, bias_b, w["sink_t"], nat4(dycat), 1,
                                                  nat4(s["lse_b"]), nat4(dl_b), name="band_b_bwd", **B_HEADS)
        d_table_b = d_table_b + _bucket_reduce(dbias_b.reshape(4, -1), pat_b, name="bucket_b")
        g["sink"] = dsink[:, 0, 0]
        dd = _dense_bwd(s["qd"], dycat, s["lse_d"], dl_d, tq=128, name="dense_bwd")
        dcu, dcv, g["c_ws"], dbs, g["c_g"], g["c_b"] = _c_bwd(s["proj"], dycat, w["c_g"], w["c_b"], w["c_ws"],
                                                               w["c_wst"], w["c_bst"], tm=512, name="c_bwd")
        g["c_bs"] = dbs[:, ::64].T
        db = (dqb.reshape(t, 256), dkb.reshape(t, 128), dvb.reshape(t, 128))
        dproj, dgains = _qkprep_bwd(s["proj"], da, db, dd, dcu, dcv, w["qk_gains"], cos, sins, tm=512, name="qkprep_bwd")
        g["qk_gain"] = dgains[:6, :64].reshape(3, 2, HEAD_DIM)
        g["w_in"] = _mm(s["hn0"], dproj, "tn", None, tm=1024, tn=1152, out_dtype=BF16, name="dw_in")
        dx, g["g_mix"] = _mm_bt_normbwd((dproj,), w["w_in"], (s["x0"],), w["g_mix"], dx1, tm=1024, tn=1152,
                                        name="mix_in_bwd")
        grads[li] = g
        token = grads_ready(li, "end", g)
    d_rel_bias = jnp.concatenate([d_table_a, d_table_b], axis=0).T
    return loss_tile[0, 0], dx, grads, d_rel_bias


WEIGHT_NAMES = ("rel_bias", "ln_mix_g", "w_in", "qk_gain", "sink", "c_norm_g", "c_norm_b", "c_ws", "c_bs", "out_gain",
                "w_out", "ln_ffn_g", "w_up", "conv_w", "conv_b", "w_down", "ln_ple_g", "w_ple_gate", "w_ple_proj")
COL_SHARDED = ("w_in", "w_up", "w_ple_proj")
ROW_SHARDED = ("w_out", "w_down", "w_ple_gate")
SMALL_SHARDED = ("conv_w", "out_gain")
REPLICATED = tuple(n for n in WEIGHT_NAMES if n not in COL_SHARDED + ROW_SHARDED + SMALL_SHARDED)
LOCAL_GRAD_KEY = {"ln_mix_g": "g_mix", "ln_ffn_g": "g_ffn", "ln_ple_g": "g_ple", "c_norm_g": "c_g", "c_norm_b": "c_b",
                  "w_ple_gate": "w_gate", "w_ple_proj": "w_proj"}


def _full_from_gathered(name, gathered):
    _, r, c = gathered.shape
    if name in ROW_SHARDED:
        return gathered.reshape(N_DEV * r, c)
    return jnp.transpose(gathered, (1, 0, 2)).reshape(r, N_DEV * c)


def _slots_from_full(name, full):
    rows, cols = full.shape
    if name in ROW_SHARDED:
        return full.reshape(N_DEV, rows // N_DEV, cols)
    return jnp.transpose(full.reshape(rows, N_DEV, cols // N_DEV), (1, 0, 2))


def _piece_rows(shape):
    return -(-int(np.prod(shape)) // 1024) * 8


def _pack_rows(arrays):
    pieces = []
    for a in arrays:
        n, rows = int(np.prod(a.shape)), _piece_rows(a.shape)
        flat = a.astype(F32).reshape(-1)
        if n != rows * LANES:
            flat = jnp.pad(flat, (0, rows * LANES - n))
        pieces.append(flat.reshape(rows, LANES))
    return jnp.concatenate(pieces, axis=0)


def _unpack_rows(packed, shapes):
    out, off = [], 0
    for shp in shapes:
        n, rows = int(np.prod(shp)), _piece_rows(shp)
        piece = packed[off:off + rows]
        out.append((piece if n == rows * LANES else piece.reshape(-1)[:n]).reshape(shp))
        off += rows
    return out


def kernel(x, p, rel_bias, ln_mix_g, w_in, qk_gain, sink, c_norm_g, c_norm_b, c_ws, c_bs, out_gain, w_out, ln_ffn_g, w_up, conv_w, conv_b, w_down, ln_ple_g, w_ple_gate, w_ple_proj, loss_target, m_rel_bias, m_ln_mix_g, m_w_in, m_qk_gain, m_sink, m_c_norm_g, m_c_norm_b, m_c_ws, m_c_bs, m_out_gain, m_w_out, m_ln_ffn_g, m_w_up, m_conv_w, m_conv_b, m_w_down, m_ln_ple_g, m_w_ple_gate, m_w_ple_proj, v_rel_bias, v_ln_mix_g, v_w_in, v_qk_gain, v_sink, v_c_norm_g, v_c_norm_b, v_c_ws, v_c_bs, v_out_gain, v_w_out, v_ln_ffn_g, v_w_up, v_conv_w, v_conv_b, v_w_down, v_ln_ple_g, v_w_ple_gate, v_w_ple_proj):
    env = dict(locals())
    wt = {n: env[n] for n in WEIGHT_NAMES}
    mom_m = {n: env["m_" + n] for n in WEIGHT_NAMES}
    mom_v = {n: env["v_" + n] for n in WEIGHT_NAMES}
    bl = x.shape[0]
    t = bl * SEQ
    me = 4 * lax.axis_index("x") + 2 * lax.axis_index("y") + lax.axis_index("c")

    big = COL_SHARDED + ROW_SHARDED
    full = {}
    small_shapes = [wt[n].shape for n in SMALL_SHARDED]
    small = _allgather_vmem(_pack_rows([wt[n] for n in SMALL_SHARDED]), reduce=False, name="gather_small")
    small = small.reshape(N_DEV, -1)
    off = 0
    for n, shp in zip(SMALL_SHARDED, small_shapes):
        cnt = int(np.prod(shp))
        g = small[:, off:off + cnt].reshape((N_DEV,) + tuple(shp))
        full[n] = jnp.transpose(g, (1, 2, 0, 3)).reshape(shp[0], shp[1], N_DEV * shp[2])
        off += _piece_rows(shp) * LANES

    def head_gain(li, a, b, reps):
        g = jnp.tile(qk_gain[li, a, b], reps)
        return jnp.pad(g, (0, 256 - g.shape[0]))

    wts = []
    for li in range(DEPTH):
        rows = [head_gain(li, 0, 0, 4), head_gain(li, 0, 1, 4), head_gain(li, 1, 0, 4), head_gain(li, 1, 1, 2),
                head_gain(li, 2, 0, 4), head_gain(li, 2, 1, 2), jnp.zeros((256,), F32), jnp.zeros((256,), F32)]
        wts.append(dict(
            g_mix=ln_mix_g[li].reshape(1, -1), qk_gains=jnp.stack(rows),
            sink_t=jnp.broadcast_to(sink[li][:, None, None], (4, 8, 128)),
            c_g=c_norm_g[li].reshape(1, -1), c_b=c_norm_b[li].reshape(1, -1), c_ws=c_ws[li].astype(BF16),
            c_wst=jnp.transpose(c_ws[li], (0, 2, 1)).astype(BF16), c_bst=jnp.repeat(c_bs[li].T, 64, axis=1),
            out_gain=full["out_gain"][li].reshape(1, -1), g_ffn=ln_ffn_g[li].reshape(1, -1),
            conv_w=full["conv_w"][li], conv_b=conv_b[li].reshape(1, -1), g_ple=ln_ple_g[li].reshape(1, -1)))

    local_key = {"w_ple_gate": "w_gate", "w_ple_proj": "w_proj"}

    gather_names = {"in": ("w_in",), "rest": tuple(n for n in big if n != "w_in")}
    gather_handles = {}

    def start_gather(li, after):
        shards = [[wt[n][li].astype(BF16) for n in gather_names[part]] for part in ("in", "rest")]
        handles, token = _exchange_start(shards, after, scatter=False, name=f"gather_start_{li}")
        gather_handles[li, "in"], gather_handles[li, "rest"] = handles
        return token

    def matmul_weights(li, part, after):
        started = start_gather(li + 1, after) if part == "rest" and li + 1 < DEPTH else None
        lands = _exchange_wait(gather_handles[li, part], after, scatter=False, name=f"gather_wait_{li}_{part}")
        names = gather_names[part]
        return {local_key.get(n, n): _full_from_gathered(n, g) for n, g in zip(names, lands)}, started

    start_gather(0, None)

    mid_names = ("w_ple_gate", "w_ple_proj", "w_down", "w_up")
    end_names = ("w_out", "w_in")
    pending = []

    def start_exchange(li, names, g, tag):
        slots = [_slots_from_full(n, g[local_key.get(n, n)]) for n in names]
        (handle,), token = _exchange_start([slots], None, scatter=True, name=f"grads_start_{li}_{tag}")
        pending.append((li, names, handle, tag))
        return token

    def grads_ready(li, stage, g):
        if li == 0:
            return start_exchange(li, mid_names if stage == "mid" else end_names, g, stage)
        if stage == "end":
            return start_exchange(li, mid_names + end_names, g, stage)
        return None

    loss_part, dx, grads, d_rel_bias = _local_step(
        x.reshape(t, D_MODEL), p.reshape(DEPTH * t, PLE_DIM), loss_target.reshape(t, D_MODEL), rel_bias, wts,
        matmul_weights, grads_ready)
    loss = lax.psum(loss_part, ("x", "y", "c"))

    def local_grad(n):
        if n == "rel_bias":
            return d_rel_bias
        key = LOCAL_GRAD_KEY.get(n, n)
        return jnp.stack([grads[li][key].reshape(wt[n].shape[1:]) if n in REPLICATED else grads[li][key]
                          for li in range(DEPTH)])

    landed = {}
    for li, names, handle, tag in pending:
        for n, land in zip(names, _exchange_wait(handle, dx, scatter=True, name=f"grads_wait_{li}_{tag}")):
            landed[n, li] = land
    out_g, out_d, out_m, out_v = {}, {}, {}, {}
    for n in big:
        shp = wt[n].shape
        two_d = lambda a: a.reshape(-1, shp[-1])
        res = _adamw_reduce([landed[n, li] for li in range(DEPTH)], two_d(wt[n]), two_d(mom_m[n]), two_d(mom_v[n]),
                            tr=32 if n == "w_down" else 128, name="adamw_" + n)
        out_g[n], out_d[n], out_m[n], out_v[n] = [r.reshape(shp) for r in res]

    small_names = REPLICATED + SMALL_SHARDED
    small_full_shapes = [wt[n].shape if n in REPLICATED else full[n].shape for n in small_names]
    reduced = _allgather_vmem(_pack_rows([local_grad(n) for n in small_names]), reduce=True, name="allreduce_small")
    reduced = dict(zip(small_names, _unpack_rows(reduced, small_full_shapes)))
    rep_shapes = [wt[n].shape for n in REPLICATED]
    upd = _adamw_plain(_pack_rows([reduced[n] for n in REPLICATED]), _pack_rows([wt[n] for n in REPLICATED]),
                       _pack_rows([mom_m[n] for n in REPLICATED]), _pack_rows([mom_v[n] for n in REPLICATED]),
                       name="adamw_replicated")
    for dst, packed in zip((out_d, out_m, out_v), upd):
        dst.update(zip(REPLICATED, _unpack_rows(packed, rep_shapes)))
    for n in REPLICATED:
        out_g[n] = reduced[n]
    for n in SMALL_SHARDED:
        shp = wt[n].shape
        g = reduced[n].reshape(shp[0], shp[1], N_DEV, shp[2])
        g = lax.dynamic_index_in_dim(g, me, axis=2, keepdims=False)
        two_d = lambda a: a.reshape(-1, shp[-1])
        res = _adamw_plain(two_d(g), two_d(wt[n]), two_d(mom_m[n]), two_d(mom_v[n]), name="adamw_" + n)
        out_g[n] = g
        out_d[n], out_m[n], out_v[n] = [r.reshape(shp) for r in res]

    return (loss, dx.reshape(bl, SEQ, D_MODEL), *[out_g[n] for n in WEIGHT_NAMES], *[out_d[n] for n in WEIGHT_NAMES],
            *[out_m[n] for n in WEIGHT_NAMES], *[out_v[n] for n in WEIGHT_NAMES])
```

```python
import math

import jax
import jax.numpy as jnp
import numpy as np
from jax import lax
from jax.experimental import pallas as pl
from jax.experimental.pallas import tpu as pltpu

F32 = jnp.float32
BF16 = jnp.bfloat16
HI = lax.Precision.HIGHEST

N_DEV = 8
D_MODEL = 1024
SEQ = 2048
DEPTH = 2
HEAD_DIM = 64
IN_WIDTH = 2304
D_FF = 2816
PLE_DIM = 256
C_CHUNK = 128
C_GROUPS = 4
DILATED_CFGS = ((128, 1), (512, 4), (2048, 16))
DILATIONS = tuple(d for _, d in DILATED_CFGS)
A_RADIUS = 64
SWA_RADIUS = 128
BAND_BLOCK = 256
GRID_W = 64
ROPE_THETA = 10000.0
REL_BUCKETS = 32
REL_MAX_DIST = 1024
EPS = 1e-6
NEG_INF = -1e30
ATTN_SCALE = HEAD_DIM ** -0.5
LANES = 128

ADAM_LR = 0.001
ADAM_B1 = 0.9
ADAM_B2 = 0.999
ADAM_EPS = 1e-08
ADAM_WD = 0.01
ADAM_STEP = 10

MESH = pl.DeviceIdType.MESH
NT = (((1,), (1,)), ((), ()))
TN = (((0,), (0,)), ((), ()))
ARB = "arbitrary"
PAR = "parallel"


def _cparams(*sem):
    return pltpu.CompilerParams(dimension_semantics=tuple(sem))


def _sds(shape, dtype):
    return jax.ShapeDtypeStruct(tuple(shape), dtype)


def _group_sum_matrix(n, same_group):
    r = lax.broadcasted_iota(jnp.int32, (n, n), 0)
    c = lax.broadcasted_iota(jnp.int32, (n, n), 1)
    if same_group:
        return ((r >> 6) == (c >> 6)).astype(F32)
    return ((r & 63) == (c & 63)).astype(F32)


def _seg_sum(x, e):
    return jnp.dot(x, e, precision=HI, preferred_element_type=F32)


def _gelu(x):
    c = math.sqrt(2.0 / math.pi)
    return 0.5 * x * (1.0 + jnp.tanh(c * (x + 0.044715 * (x * x * x))))


def _gelu_grad(x):
    c = math.sqrt(2.0 / math.pi)
    t = jnp.tanh(c * (x + 0.044715 * (x * x * x)))
    return 0.5 * (1.0 + t) + 0.5 * x * (1.0 - t * t) * c * (1.0 + 3.0 * 0.044715 * (x * x))


def _sigmoid(x):
    return 1.0 / (1.0 + jnp.exp(-x))


def _scatter_cols(scratch, first, val):
    for c in range(val.shape[1] // LANES):
        scratch[first + c] = val[:, c * LANES:(c + 1) * LANES]


def _gather_cols(scratch, first, ncol):
    return jnp.concatenate([scratch[first + c] for c in range(ncol)], axis=1)


def _read_residue(scratch, first, ncol, r, d):
    n = scratch.shape[1] // d
    return jnp.concatenate([scratch.at[first + c][pl.ds(r, n, stride=d), :] for c in range(ncol)], axis=1)


def _write_residue(scratch, first, r, d, val):
    n = scratch.shape[1] // d
    for c in range(val.shape[1] // LANES):
        scratch.at[first + c][pl.ds(r, n, stride=d), :] = val[:, c * LANES:(c + 1) * LANES]


def _norm_mm(xs, gain, w, res, *, tm, tn, name, out_dtype=F32):
    t = xs[0].shape[0]
    k = sum(x.shape[1] for x in xs)
    n = w.shape[1]
    ng = len(xs)
    has_res = res is not None

    def body(*refs):
        x_refs = refs[:ng]
        g_ref, w_ref = refs[ng], refs[ng + 1]
        res_ref = refs[ng + 2] if has_res else None
        hn_ref, o_ref, hn_s = refs[ng + 2 + has_res:]

        @pl.when(pl.program_id(1) == 0)
        def _():
            off = 0
            for xr in x_refs:
                x = xr[...]
                wd = x.shape[1]
                r = lax.rsqrt(jnp.mean(x * x, axis=-1, keepdims=True) + EPS)
                hn_s[:, off:off + wd] = (x * r * g_ref[:, off:off + wd]).astype(BF16)
                off += wd
            hn_ref[...] = hn_s[...]

        acc = jnp.dot(hn_s[...], w_ref[...], preferred_element_type=F32)
        if has_res:
            acc = acc + res_ref[...]
        o_ref[...] = acc.astype(out_dtype)

    in_specs = [pl.BlockSpec((tm, x.shape[1]), lambda i, j: (i, 0)) for x in xs]
    in_specs += [pl.BlockSpec((1, k), lambda i, j: (0, 0)), pl.BlockSpec((k, tn), lambda i, j: (0, j))]
    args = list(xs) + [gain, w]
    if has_res:
        in_specs.append(pl.BlockSpec((tm, tn), lambda i, j: (i, j)))
        args.append(res)
    return pl.pallas_call(
        body, name=name, grid=(t // tm, n // tn), in_specs=in_specs,
        out_specs=[pl.BlockSpec((tm, k), lambda i, j: (i, 0)), pl.BlockSpec((tm, tn), lambda i, j: (i, j))],
        out_shape=[_sds((t, k), BF16), _sds((t, n), out_dtype)],
        scratch_shapes=[pltpu.VMEM((tm, k), BF16)],
        compiler_params=_cparams(PAR, ARB),
    )(*args)


def _mm(a, b, mode, res, *, tm, tn, out_dtype, name, a_rows=None):
    if mode == "tn":
        kk, m = a.shape
        blk_a = 0
        if a_rows is not None:
            blk_a, kk = a_rows
        a_spec = pl.BlockSpec((kk, tm), lambda i, j: (blk_a, i))
    else:
        m, kk = a.shape
        a_spec = pl.BlockSpec((tm, kk), lambda i, j: (i, 0))
    if mode == "nt":
        n = b.shape[0]
        b_spec = pl.BlockSpec((tn, kk), lambda i, j: (j, 0))
    else:
        n = b.shape[1]
        b_spec = pl.BlockSpec((kk, tn), lambda i, j: (0, j))
    has_res = res is not None

    def body(*refs):
        a_ref, b_ref = refs[0], refs[1]
        o_ref = refs[-1]
        av = a_ref[...].astype(BF16)
        bv = b_ref[...].astype(BF16)
        if mode == "nn":
            acc = jnp.dot(av, bv, preferred_element_type=F32)
        elif mode == "nt":
            acc = lax.dot_general(av, bv, NT, preferred_element_type=F32)
        else:
            acc = lax.dot_general(av, bv, TN, preferred_element_type=F32)
        if has_res:
            acc = acc + refs[2][...]
        o_ref[...] = acc.astype(out_dtype)

    in_specs = [a_spec, b_spec]
    args = [a, b]
    if has_res:
        in_specs.append(pl.BlockSpec((tm, tn), lambda i, j: (i, j)))
        args.append(res)
    return pl.pallas_call(
        body, name=name, grid=(m // tm, n // tn), in_specs=in_specs,
        out_specs=pl.BlockSpec((tm, tn), lambda i, j: (i, j)),
        out_shape=_sds((m, n), out_dtype),
        compiler_params=_cparams(PAR, PAR),
    )(*args)


def _mm_bt_normbwd(dys, w, xs, gain, dres, *, tm, tn, name, emit_bf16=False):
    t, wd_each = dys[0].shape
    nd = len(dys)
    per = wd_each // tn
    nj = nd * per
    k = w.shape[0]
    ng = len(xs)
    has_res = dres is not None

    def body(*refs):
        dy_refs = refs[:nd]
        w_ref = refs[nd]
        x_refs = refs[nd + 1:nd + 1 + ng]
        g_ref = refs[nd + 1 + ng]
        dres_ref = refs[nd + 2 + ng] if has_res else None
        outs = refs[nd + 2 + ng + has_res:]
        dx_ref = outs[0]
        dxb_ref = outs[1] if emit_bf16 else None
        dg_ref, acc = outs[1 + emit_bf16:]
        i, j = pl.program_id(0), pl.program_id(1)

        @pl.when(j == 0)
        def _():
            acc[...] = jnp.zeros_like(acc)

        for d, dy_ref in enumerate(dy_refs):
            @pl.when((j >= d * per) & (j < (d + 1) * per))
            def _(dy_ref=dy_ref):
                acc[...] += lax.dot_general(dy_ref[...].astype(BF16), w_ref[...], NT, preferred_element_type=F32)

        @pl.when(j == nj - 1)
        def _():
            @pl.when(i == 0)
            def _():
                dg_ref[...] = jnp.zeros_like(dg_ref)

            off = 0
            for xr in x_refs:
                x = xr[...]
                wd = x.shape[1]
                g = g_ref[:, off:off + wd]
                dyn = acc[:, off:off + wd]
                r = lax.rsqrt(jnp.mean(x * x, axis=-1, keepdims=True) + EPS)
                gdy = dyn * g
                dx = r * gdy - x * (r * r * r * jnp.mean(gdy * x, axis=-1, keepdims=True))
                if has_res:
                    dx = dx + dres_ref[:, off:off + wd]
                dx_ref[:, off:off + wd] = dx
                if emit_bf16:
                    dxb_ref[:, off:off + wd] = dx.astype(BF16)
                dg_ref[:, off:off + wd] += jnp.sum(dyn * x * r, axis=0, keepdims=True)
                off += wd

    def dy_map(d):
        return lambda i, j: (i, jnp.clip(j - d * per, 0, per - 1))

    in_specs = [pl.BlockSpec((tm, tn), dy_map(d)) for d in range(nd)]
    in_specs.append(pl.BlockSpec((k, tn), lambda i, j: (0, j)))
    in_specs += [pl.BlockSpec((tm, x.shape[1]), lambda i, j: (i, 0)) for x in xs]
    in_specs.append(pl.BlockSpec((1, k), lambda i, j: (0, 0)))
    args = list(dys) + [w] + list(xs) + [gain]
    if has_res:
        in_specs.append(pl.BlockSpec((tm, k), lambda i, j: (i, 0)))
        args.append(dres)
    row = pl.BlockSpec((tm, k), lambda i, j: (i, 0))
    out_specs = [row] + ([row] if emit_bf16 else []) + [pl.BlockSpec((1, k), lambda i, j: (0, 0))]
    out_shape = [_sds((t, k), F32)] + ([_sds((t, k), BF16)] if emit_bf16 else []) + [_sds((1, k), F32)]
    return pl.pallas_call(
        body, name=name, grid=(t // tm, nj), in_specs=in_specs, out_specs=out_specs, out_shape=out_shape,
        scratch_shapes=[pltpu.VMEM((tm, k), F32)],
        compiler_params=_cparams(ARB, ARB),
    )(*args)


def _rope_partner(y):
    n = y.shape[1]
    lane = lax.broadcasted_iota(jnp.int32, y.shape, 1)
    return jnp.where((lane & 31) < 16, pltpu.roll(y, n - 16, 1), pltpu.roll(y, 16, 1))


def _residue_specs(tm, width, nt):
    specs = [pl.BlockSpec((tm, width), lambda b, i: (b * nt + i, 0))]
    for d in DILATIONS[1:]:
        specs.append(pl.BlockSpec((None, d, tm // d, width), lambda b, i: (b, 0, i, 0)))
    return specs


def _residue_shapes(bl, width, dtype):
    return [_sds((bl * SEQ, width), dtype)] + [_sds((bl, d, SEQ // d, width), dtype) for d in DILATIONS[1:]]


def _qkprep_fwd(proj, gains, cos, sins, *, tm, name):
    t = proj.shape[0]
    bl = t // SEQ
    nt = SEQ // tm

    def body(p_ref, g_ref, c_ref, s_ref, qa1_ref, qa4_ref, qa16_ref, qb_ref, qd_ref, scr):
        e = _group_sum_matrix(256, True)

        def hn(x, row):
            wd = x.shape[1]
            ms = _seg_sum(x * x, e[:wd, :wd]) * (1.0 / HEAD_DIM)
            return x * lax.rsqrt(ms + EPS) * g_ref[row:row + 1, :wd]

        qa = jnp.concatenate([hn(p_ref[:, 0:256], 0) * ATTN_SCALE, hn(p_ref[:, 256:512], 1), p_ref[:, 512:768]], axis=1)
        qa1_ref[...] = qa.astype(BF16)
        _scatter_cols(scr, 0, qa)
        for d, ref in ((4, qa4_ref), (16, qa16_ref)):
            for r in range(d):
                ref[r] = _read_residue(scr, 0, 6, r, d).astype(BF16)
        qb_ref[:, 0:256] = (hn(p_ref[:, 768:1024], 2) * ATTN_SCALE).astype(BF16)
        qb_ref[:, 256:384] = hn(p_ref[:, 1024:1152], 3).astype(BF16)
        qb_ref[:, 384:512] = p_ref[:, 1152:1280].astype(BF16)
        yq = hn(p_ref[:, 1792:2048], 4)
        yq = yq * c_ref[...] + _rope_partner(yq) * s_ref[...]
        qd_ref[:, 0:256] = (yq * ATTN_SCALE).astype(BF16)
        yk = hn(p_ref[:, 2048:2176], 5)
        yk = yk * c_ref[:, 0:128] + _rope_partner(yk) * s_ref[:, 0:128]
        qd_ref[:, 256:384] = yk.astype(BF16)
        qd_ref[:, 384:512] = p_ref[:, 2176:2304].astype(BF16)

    row = lambda width: pl.BlockSpec((tm, width), lambda b, i: (b * nt + i, 0))
    tab = pl.BlockSpec((tm, 256), lambda b, i: (i, 0))
    return pl.pallas_call(
        body, name=name, grid=(bl, nt),
        in_specs=[row(IN_WIDTH), pl.BlockSpec((8, 256), lambda b, i: (0, 0)), tab, tab],
        out_specs=_residue_specs(tm, 768, nt) + [row(512), row(512)],
        out_shape=_residue_shapes(bl, 768, BF16) + [_sds((t, 512), BF16), _sds((t, 512), BF16)],
        scratch_shapes=[pltpu.VMEM((6, tm, LANES), F32)],
        compiler_params=_cparams(PAR, PAR),
    )(proj, gains, cos, sins)


def _qkprep_bwd(proj, da, db, dd, dcu, dcv, gains, cos, sins, *, tm, name):
    t = proj.shape[0]
    bl = t // SEQ
    nt = SEQ // tm
    flat = [a for cfg in da for a in cfg] + list(db) + list(dd) + [dcu, dcv]

    def body(*refs):
        p_ref, g_ref, c_ref, s_ref = refs[:4]
        d_refs = refs[4:4 + len(flat)]
        dp_ref, dg_ref, scr = refs[4 + len(flat):]
        a_refs = d_refs[:9]
        dqb_ref, dkb_ref, dvb_ref, dqd_ref, dkd_ref, dvd_ref, dcu_ref, dcv_ref = d_refs[9:]
        e = _group_sum_matrix(256, True)
        first = (pl.program_id(0) == 0) & (pl.program_id(1) == 0)
        last = (pl.program_id(0) == bl - 1) & (pl.program_id(1) == nt - 1)

        @pl.when(first)
        def _():
            dg_ref[...] = jnp.zeros_like(dg_ref)

        def hn_bwd(x, dy, row):
            wd = x.shape[1]
            ee = e[:wd, :wd]
            g = g_ref[row:row + 1, :wd]
            r = lax.rsqrt(_seg_sum(x * x, ee) * (1.0 / HEAD_DIM) + EPS)
            gdy = dy * g
            dx = r * gdy - x * (r * r * r * (_seg_sum(gdy * x, ee) * (1.0 / HEAD_DIM)))
            dg_ref[row:row + 1, :wd] += jnp.sum(dy * x * r, axis=0, keepdims=True)
            return dx

        def rope_bwd(dy, wd):
            return dy * c_ref[:, :wd] + _rope_partner(dy * s_ref[:, :wd])

        dqkv = jnp.concatenate([a_refs[0][...], a_refs[1][...], a_refs[2][...]], axis=1)
        for ci, d in ((1, 4), (2, 16)):
            for r in range(d):
                part = jnp.concatenate([a_refs[3 * ci + m][r] for m in range(3)], axis=1)
                _write_residue(scr, 0, r, d, part)
            dqkv = dqkv + _gather_cols(scr, 0, 6)
        dp_ref[:, 0:256] = hn_bwd(p_ref[:, 0:256], dqkv[:, 0:256] * ATTN_SCALE, 0).astype(BF16)
        dp_ref[:, 256:512] = hn_bwd(p_ref[:, 256:512], dqkv[:, 256:512], 1).astype(BF16)
        dp_ref[:, 512:768] = dqkv[:, 512:768].astype(BF16)
        dp_ref[:, 768:1024] = hn_bwd(p_ref[:, 768:1024], dqb_ref[...] * ATTN_SCALE, 2).astype(BF16)
        dp_ref[:, 1024:1152] = hn_bwd(p_ref[:, 1024:1152], dkb_ref[...], 3).astype(BF16)
        dp_ref[:, 1152:1280] = dvb_ref[...].astype(BF16)
        dp_ref[:, 1280:1536] = dcu_ref[...].astype(BF16)
        dp_ref[:, 1536:1792] = dcv_ref[...].astype(BF16)
        dp_ref[:, 1792:2048] = hn_bwd(p_ref[:, 1792:2048], rope_bwd(dqd_ref[...] * ATTN_SCALE, 256), 4).astype(BF16)
        dp_ref[:, 2048:2176] = hn_bwd(p_ref[:, 2048:2176], rope_bwd(dkd_ref[...], 128), 5).astype(BF16)
        dp_ref[:, 2176:2304] = dvd_ref[...].astype(BF16)

        @pl.when(last)
        def _():
            dg_ref[...] = _seg_sum(dg_ref[...], _group_sum_matrix(256, False))

    row = lambda width: pl.BlockSpec((tm, width), lambda b, i: (b * nt + i, 0))
    tab = pl.BlockSpec((tm, 256), lambda b, i: (i, 0))
    in_specs = [row(IN_WIDTH), pl.BlockSpec((8, 256), lambda b, i: (0, 0)), tab, tab]
    res_specs = _residue_specs(tm, 256, nt)
    in_specs += [res_specs[ci] for ci in range(3) for _ in range(3)]
    in_specs += [row(a.shape[1]) for a in flat[9:]]
    return pl.pallas_call(
        body, name=name, grid=(bl, nt), in_specs=in_specs,
        out_specs=[row(IN_WIDTH), pl.BlockSpec((8, 256), lambda b, i: (0, 0))],
        out_shape=[_sds((t, IN_WIDTH), BF16), _sds((8, 256), F32)],
        scratch_shapes=[pltpu.VMEM((6, tm, LANES), F32)],
        compiler_params=_cparams(ARB, ARB),
    )(proj, gains, cos, sins, *flat)


def _band_spec(seq_len, spec):
    width, idx = spec
    return pl.BlockSpec((None, None, seq_len, width), lambda b, r: (b, r, 0, idx))


def _fill_padded(dst, src_ref, rad, seq_len):
    z = jnp.zeros((rad, dst.shape[1]), dst.dtype)
    dst[0:rad, :] = z
    dst[rad + seq_len:rad + seq_len + rad, :] = z
    dst[rad:rad + seq_len, :] = src_ref[...]


def _band_fwd(src, qs, ks, vs, bias, sink, *, rad, nh, nkv, name):
    bl, dil, sl, _ = src.shape
    blk = bias.shape[1]
    kw = blk + 2 * rad
    nb = sl // blk
    rep = nh // nkv
    has_sink = sink is not None

    def body(*refs):
        q_ref, k_ref, v_ref, b_ref = refs[:4]
        s_ref = refs[4] if has_sink else None
        o_ref, l_ref, kp, vp = refs[4 + has_sink:]
        _fill_padded(kp, k_ref, rad, sl)
        _fill_padded(vp, v_ref, rad, sl)

        def blk_body(i, carry):
            r0 = pl.multiple_of(i * blk, blk)
            qb = q_ref[pl.ds(r0, blk), :]
            kwin = kp[pl.ds(r0, kw), :]
            vwin = vp[pl.ds(r0, kw), :]
            col = r0 - rad + lax.broadcasted_iota(jnp.int32, (blk, kw), 1)
            neg = jnp.where((col >= 0) & (col < sl), 0.0, NEG_INF).astype(F32)
            for h in range(nh):
                g = h // rep
                hs = slice(h * HEAD_DIM, (h + 1) * HEAD_DIM)
                gs = slice(g * HEAD_DIM, (g + 1) * HEAD_DIM)
                s = lax.dot_general(qb[:, hs], kwin[:, gs], NT, preferred_element_type=F32)
                s = s + b_ref[h] + neg
                m = jnp.max(s, axis=1, keepdims=True)
                if has_sink:
                    sk = s_ref[h][0:1, 0:1]
                    m = jnp.maximum(m, sk)
                p = jnp.exp(s - m)
                den = jnp.sum(p, axis=1, keepdims=True)
                if has_sink:
                    den = den + jnp.exp(sk - m)
                o = jnp.dot(p.astype(BF16), vwin[:, gs], preferred_element_type=F32) / den
                o_ref[pl.ds(r0, blk), hs] = o
                l_ref[pl.ds(r0, blk), hs] = jnp.broadcast_to(m + jnp.log(den), (blk, HEAD_DIM))
            return carry

        lax.fori_loop(0, nb, blk_body, 0)

    in_specs = [_band_spec(sl, qs), _band_spec(sl, ks), _band_spec(sl, vs),
                pl.BlockSpec((nh, blk, kw), lambda b, r: (0, 0, 0))]
    args = [src] * 3 + [bias]
    if has_sink:
        in_specs.append(pl.BlockSpec((nh, 8, 128), lambda b, r: (0, 0, 0)))
        args.append(sink)
    return pl.pallas_call(
        body, name=name, grid=(bl, dil), in_specs=in_specs,
        out_specs=[_band_spec(sl, (256, 0))] * 2,
        out_shape=[_sds((bl, dil, sl, 256), F32)] * 2,
        scratch_shapes=[pltpu.VMEM((sl + 2 * rad, ks[0]), BF16), pltpu.VMEM((sl + 2 * rad, vs[0]), BF16)],
        compiler_params=_cparams(PAR, PAR),
    )(*args)


def _band_bwd(src, qs, ks, vs, bias, sink, dy, dcol, lse, delta, *, rad, nh, nkv, name):
    bl, dil, sl, _ = src.shape
    blk = bias.shape[1]
    kw = blk + 2 * rad
    nb = sl // blk
    rep = nh // nkv
    has_sink = sink is not None
    wk, wv = ks[0], vs[0]

    def body(*refs):
        q_ref, k_ref, v_ref, b_ref = refs[:4]
        s_ref = refs[4] if has_sink else None
        do_ref, l_ref, dl_ref = refs[4 + has_sink:7 + has_sink]
        outs = refs[7 + has_sink:]
        if has_sink:
            dq_ref, dk_ref, dv_ref, db_ref, dsk_ref, kp, vp, dka, dva = outs
        else:
            dq_ref, dk_ref, dv_ref, db_ref, kp, vp, dka, dva = outs

        @pl.when((pl.program_id(0) == 0) & (pl.program_id(1) == 0))
        def _():
            db_ref[...] = jnp.zeros_like(db_ref)
            if has_sink:
                dsk_ref[...] = jnp.zeros_like(dsk_ref)

        _fill_padded(kp, k_ref, rad, sl)
        _fill_padded(vp, v_ref, rad, sl)
        dka[...] = jnp.zeros_like(dka)
        dva[...] = jnp.zeros_like(dva)

        def blk_body(i, carry):
            r0 = pl.multiple_of(i * blk, blk)
            qb = q_ref[pl.ds(r0, blk), :]
            kwin = kp[pl.ds(r0, kw), :]
            vwin = vp[pl.ds(r0, kw), :]
            dob = do_ref[pl.ds(r0, blk), :].astype(BF16)
            lb = l_ref[pl.ds(r0, blk), :]
            dlb = dl_ref[pl.ds(r0, blk), :]
            col = r0 - rad + lax.broadcasted_iota(jnp.int32, (blk, kw), 1)
            neg = jnp.where((col >= 0) & (col < sl), 0.0, NEG_INF).astype(F32)
            for h in range(nh):
                g = h // rep
                hs = slice(h * HEAD_DIM, (h + 1) * HEAD_DIM)
                gs = slice(g * HEAD_DIM, (g + 1) * HEAD_DIM)
                qh, kh, vh, doh = qb[:, hs], kwin[:, gs], vwin[:, gs], dob[:, hs]
                lh = lb[:, h * HEAD_DIM:h * HEAD_DIM + 1]
                dlh = dlb[:, h * HEAD_DIM:h * HEAD_DIM + 1]
                s = lax.dot_general(qh, kh, NT, preferred_element_type=F32) + b_ref[h] + neg
                p = jnp.exp(s - lh)
                dp = lax.dot_general(doh, vh, NT, preferred_element_type=F32)
                ds = p * (dp - dlh)
                dsb = ds.astype(BF16)
                dq_ref[pl.ds(r0, blk), hs] = jnp.dot(dsb, kh, preferred_element_type=F32)
                dka[pl.ds(r0, kw), gs] += lax.dot_general(dsb, qh, TN, preferred_element_type=F32)
                dva[pl.ds(r0, kw), gs] += lax.dot_general(p.astype(BF16), doh, TN, preferred_element_type=F32)
                db_ref[h] += ds
                if has_sink:
                    ps = jnp.exp(s_ref[h][0:1, 0:1] - lh)
                    dsk_ref[h] += jnp.broadcast_to(-jnp.sum(ps * dlh, axis=0, keepdims=True), (8, 128))
            return carry

        lax.fori_loop(0, nb, blk_body, 0)
        dk_ref[...] = dka[rad:rad + sl, :]
        dv_ref[...] = dva[rad:rad + sl, :]

    const3 = lambda b, r: (0, 0, 0)
    in_specs = [_band_spec(sl, qs), _band_spec(sl, ks), _band_spec(sl, vs), pl.BlockSpec((nh, blk, kw), const3)]
    args = [src] * 3 + [bias]
    if has_sink:
        in_specs.append(pl.BlockSpec((nh, 8, 128), const3))
        args.append(sink)
    row = _band_spec(sl, (256, 0))
    in_specs += [_band_spec(sl, (256, dcol)), row, row]
    args += [dy, lse, delta]
    out_specs = [row, _band_spec(sl, (wk, 0)), _band_spec(sl, (wv, 0)), pl.BlockSpec((nh, blk, kw), const3)]
    out_shape = [_sds((bl, dil, sl, 256), F32), _sds((bl, dil, sl, wk), F32), _sds((bl, dil, sl, wv), F32),
                 _sds((nh, blk, kw), F32)]
    if has_sink:
        out_specs.append(pl.BlockSpec((nh, 8, 128), const3))
        out_shape.append(_sds((nh, 8, 128), F32))
    return pl.pallas_call(
        body, name=name, grid=(bl, dil), in_specs=in_specs, out_specs=out_specs, out_shape=out_shape,
        scratch_shapes=[pltpu.VMEM((sl + 2 * rad, wk), BF16), pltpu.VMEM((sl + 2 * rad, wv), BF16),
                        pltpu.VMEM((sl + 2 * rad, wk), F32), pltpu.VMEM((sl + 2 * rad, wv), F32)],
        compiler_params=_cparams(ARB, ARB),
    )(*args)


def _combine_a(os_, ls_, *, tm, name):
    bl = os_[1].shape[0]
    t = bl * SEQ
    nt = SEQ // tm

    def body(o1, o4, o16, l1, l4, l16, y_ref, lt_ref, scr):
        for k, (d, ref) in enumerate(((4, o4), (16, o16), (4, l4), (16, l16))):
            for r in range(d):
                _write_residue(scr, 2 * k, r, d, ref[r])
        o2, o3, b, c = (_gather_cols(scr, 2 * k, 2) for k in range(4))
        a = l1[...]
        m = jnp.maximum(jnp.maximum(a, b), c)
        ea, eb, ec = jnp.exp(a - m), jnp.exp(b - m), jnp.exp(c - m)
        den = ea + eb + ec
        y_ref[...] = (ea / den) * o1[...] + (eb / den) * o2 + (ec / den) * o3
        lt_ref[...] = m + jnp.log(den)

    specs = _residue_specs(tm, 256, nt)
    return pl.pallas_call(
        body, name=name, grid=(bl, nt), in_specs=specs * 2, out_specs=[specs[0]] * 2,
        out_shape=[_sds((t, 256), F32)] * 2, scratch_shapes=[pltpu.VMEM((8, tm, LANES), F32)],
        compiler_params=_cparams(PAR, PAR),
    )(*os_, *ls_)


def _deltas(dycat, ya, yb, yd, lse_a, *, tm, name):
    t = ya.shape[0]
    bl = t // SEQ
    nt = SEQ // tm

    def body(dy_ref, ya_ref, yb_ref, yd_ref, la_ref, dy4, dy16, l4, l16, da1, da4, da16, db_ref, dd_ref, scr):
        e = _group_sum_matrix(256, True)
        dya = dy_ref[:, 0:256]
        dla = _seg_sum(dya * ya_ref[...], e)
        da1[...] = dla
        db_ref[...] = _seg_sum(dy_ref[:, 256:512] * yb_ref[...], e)
        dd_ref[...] = _seg_sum(dy_ref[:, 768:1024] * yd_ref[...], e)
        for k, (val, r4, r16) in enumerate(((dya, dy4, dy16), (la_ref[...], l4, l16), (dla, da4, da16))):
            _scatter_cols(scr, 2 * k, val)
            for d, ref in ((4, r4), (16, r16)):
                for r in range(d):
                    ref[r] = _read_residue(scr, 2 * k, 2, r, d)

    specs = _residue_specs(tm, 256, nt)
    nat = specs[0]
    shapes = _residue_shapes(bl, 256, F32)
    outs = pl.pallas_call(
        body, name=name, grid=(bl, nt),
        in_specs=[pl.BlockSpec((tm, 1024), lambda b, i: (b * nt + i, 0)), nat, nat, nat, nat],
        out_specs=specs[1:] + specs[1:] + specs + [nat, nat],
        out_shape=shapes[1:] + shapes[1:] + shapes + [shapes[0], shapes[0]],
        scratch_shapes=[pltpu.VMEM((6, tm, LANES), F32)],
        compiler_params=_cparams(PAR, PAR),
    )(dycat, ya, yb, yd, lse_a)
    return outs[0:2], outs[2:4], outs[4:7], outs[7], outs[8]


def _dense_fwd(qd, *, tq, name):
    t = qd.shape[0]
    bl = t // SEQ
    nq = SEQ // tq

    def body(q_ref, k_ref, v_ref, o_ref, l_ref):
        q = q_ref[...]
        for g in range(2):
            h0, h1 = 2 * g, 2 * g + 1
            q2 = jnp.concatenate([q[:, h0 * 64:(h0 + 1) * 64], q[:, h1 * 64:(h1 + 1) * 64]], axis=0)
            kg = k_ref[:, g * 64:(g + 1) * 64]
            vg = v_ref[:, g * 64:(g + 1) * 64]
            s = lax.dot_general(q2, kg, NT, preferred_element_type=F32)
            m = jnp.max(s, axis=1, keepdims=True)
            p = jnp.exp(s - m)
            den = jnp.sum(p, axis=1, keepdims=True)
            o2 = jnp.dot(p.astype(BF16), vg, preferred_element_type=F32) / den
            l2 = jnp.broadcast_to(m + jnp.log(den), (2 * tq, 64))
            o_ref[:, h0 * 64:(h0 + 1) * 64] = o2[:tq]
            o_ref[:, h1 * 64:(h1 + 1) * 64] = o2[tq:]
            l_ref[:, h0 * 64:(h0 + 1) * 64] = l2[:tq]
            l_ref[:, h1 * 64:(h1 + 1) * 64] = l2[tq:]

    q3 = qd.reshape(bl, SEQ, 512)
    o, lse = pl.pallas_call(
        body, name=name, grid=(bl, nq),
        in_specs=[pl.BlockSpec((None, tq, 256), lambda b, i: (b, i, 0)),
                  pl.BlockSpec((None, SEQ, 128), lambda b, i: (b, 0, 2)),
                  pl.BlockSpec((None, SEQ, 128), lambda b, i: (b, 0, 3))],
        out_specs=[pl.BlockSpec((None, tq, 256), lambda b, i: (b, i, 0))] * 2,
        out_shape=[_sds((bl, SEQ, 256), F32)] * 2,
        compiler_params=_cparams(PAR, PAR),
    )(q3, q3, q3)
    return o.reshape(t, 256), lse.reshape(t, 256)


def _dense_bwd(qd, dycat, lse, delta, *, tq, name):
    t = qd.shape[0]
    bl = t // SEQ
    nq = SEQ // tq

    def body(q_ref, k_ref, v_ref, do_ref, l_ref, dl_ref, dq_ref, dk_ref, dv_ref):
        @pl.when(pl.program_id(1) == 0)
        def _():
            dk_ref[...] = jnp.zeros_like(dk_ref)
            dv_ref[...] = jnp.zeros_like(dv_ref)

        q = q_ref[...]
        do = do_ref[...].astype(BF16)
        lv = l_ref[...]
        dlv = dl_ref[...]
        for g in range(2):
            h0, h1 = 2 * g, 2 * g + 1
            q2 = jnp.concatenate([q[:, h0 * 64:(h0 + 1) * 64], q[:, h1 * 64:(h1 + 1) * 64]], axis=0)
            do2 = jnp.concatenate([do[:, h0 * 64:(h0 + 1) * 64], do[:, h1 * 64:(h1 + 1) * 64]], axis=0)
            l2 = jnp.concatenate([lv[:, h0 * 64:h0 * 64 + 1], lv[:, h1 * 64:h1 * 64 + 1]], axis=0)
            dl2 = jnp.concatenate([dlv[:, h0 * 64:h0 * 64 + 1], dlv[:, h1 * 64:h1 * 64 + 1]], axis=0)
            kg = k_ref[:, g * 64:(g + 1) * 64]
            vg = v_ref[:, g * 64:(g + 1) * 64]
            s = lax.dot_general(q2, kg, NT, preferred_element_type=F32)
            p = jnp.exp(s - l2)
            dp = lax.dot_general(do2, vg, NT, preferred_element_type=F32)
            ds = (p * (dp - dl2)).astype(BF16)
            dq2 = jnp.dot(ds, kg, preferred_element_type=F32)
            dq_ref[:, h0 * 64:(h0 + 1) * 64] = dq2[:tq]
            dq_ref[:, h1 * 64:(h1 + 1) * 64] = dq2[tq:]
            dk_ref[:, g * 64:(g + 1) * 64] += lax.dot_general(ds, q2, TN, preferred_element_type=F32)
            dv_ref[:, g * 64:(g + 1) * 64] += lax.dot_general(p.astype(BF16), do2, TN, preferred_element_type=F32)

    q3 = qd.reshape(bl, SEQ, 512)
    tile = pl.BlockSpec((None, tq, 256), lambda b, i: (b, i, 0))
    full = pl.BlockSpec((None, SEQ, 128), lambda b, i: (b, 0, 0))
    dq, dk, dv = pl.pallas_call(
        body, name=name, grid=(bl, nq),
        in_specs=[tile, pl.BlockSpec((None, SEQ, 128), lambda b, i: (b, 0, 2)),
                  pl.BlockSpec((None, SEQ, 128), lambda b, i: (b, 0, 3)),
                  pl.BlockSpec((None, tq, 256), lambda b, i: (b, i, 3)), tile, tile],
        out_specs=[tile, full, full],
        out_shape=[_sds((bl, SEQ, 256), F32), _sds((bl, SEQ, 128), F32), _sds((bl, SEQ, 128), F32)],
        compiler_params=_cparams(PAR, ARB),
    )(q3, q3, q3, dycat.reshape(bl, SEQ, 1024), lse.reshape(bl, SEQ, 256), delta.reshape(bl, SEQ, 256))
    return dq.reshape(t, 256), dk.reshape(t, 128), dv.reshape(t, 128)


def _c_norm(cv, gam, bet):
    vg = _gelu(cv)
    mu = jnp.mean(vg, axis=-1, keepdims=True)
    xc = vg - mu
    r = lax.rsqrt(jnp.mean(xc * xc, axis=-1, keepdims=True) + EPS)
    xhat = xc * r
    return xhat * gam + bet, xhat, r


def _c_fwd(proj, gam, bet, ws, bst, *, tm, name):
    t = proj.shape[0]
    nch = tm // C_CHUNK

    def body(u_ref, v_ref, g_ref, b_ref, ws_ref, bs_ref, y_ref):
        vn, _, _ = _c_norm(v_ref[...], g_ref[...], b_ref[...])
        vnb = vn.astype(BF16)
        for c in range(nch):
            rows = slice(c * C_CHUNK, (c + 1) * C_CHUNK)
            for g in range(C_GROUPS):
                gs = slice(g * 64, (g + 1) * 64)
                mixed = jnp.dot(ws_ref[g], vnb[rows, gs], preferred_element_type=F32) + bs_ref[:, gs]
                y_ref[rows, gs] = _gelu(u_ref[rows, gs]) * mixed

    vec = pl.BlockSpec((1, 256), lambda i: (0, 0))
    return pl.pallas_call(
        body, name=name, grid=(t // tm,),
        in_specs=[pl.BlockSpec((tm, 256), lambda i: (i, 5)), pl.BlockSpec((tm, 256), lambda i: (i, 6)), vec, vec,
                  pl.BlockSpec((C_GROUPS, C_CHUNK, C_CHUNK), lambda i: (0, 0, 0)),
                  pl.BlockSpec((C_CHUNK, 256), lambda i: (0, 0))],
        out_specs=pl.BlockSpec((tm, 256), lambda i: (i, 0)), out_shape=_sds((t, 256), F32),
        compiler_params=_cparams(PAR),
    )(proj, proj, gam, bet, ws, bst)


def _c_bwd(proj, dycat, gam, bet, ws, wst, bst, *, tm, name):
    t = proj.shape[0]
    nch = tm // C_CHUNK
    nstep = t // tm

    def body(u_ref, v_ref, dy_ref, g_ref, b_ref, ws_ref, wst_ref, bs_ref,
             du_ref, dv_ref, dws_ref, dbs_ref, dg_ref, db_ref, dvn_s):
        step = pl.program_id(0)

        @pl.when(step == 0)
        def _():
            dws_ref[...] = jnp.zeros_like(dws_ref)
            dbs_ref[...] = jnp.zeros_like(dbs_ref)
            dg_ref[...] = jnp.zeros_like(dg_ref)
            db_ref[...] = jnp.zeros_like(db_ref)

        cv = v_ref[...]
        gam_v = g_ref[...]
        vn, xhat, r = _c_norm(cv, gam_v, b_ref[...])
        vnb = vn.astype(BF16)
        for c in range(nch):
            rows = slice(c * C_CHUNK, (c + 1) * C_CHUNK)
            for g in range(C_GROUPS):
                gs = slice(g * 64, (g + 1) * 64)
                cu = u_ref[rows, gs]
                dy = dy_ref[rows, gs]
                mixed = jnp.dot(ws_ref[g], vnb[rows, gs], preferred_element_type=F32) + bs_ref[:, gs]
                du_ref[rows, gs] = dy * mixed * _gelu_grad(cu)
                dmix = dy * _gelu(cu)
                dbs_ref[:, gs] += dmix
                dmb = dmix.astype(BF16)
                dws_ref[g] += lax.dot_general(dmb, vnb[rows, gs], NT, preferred_element_type=F32)
                dvn_s[rows, gs] = jnp.dot(wst_ref[g], dmb, preferred_element_type=F32)
        dvn = dvn_s[...]
        dg_ref[...] += jnp.sum(dvn * xhat, axis=0, keepdims=True)
        db_ref[...] += jnp.sum(dvn, axis=0, keepdims=True)
        dxh = dvn * gam_v
        dvg = r * (dxh - jnp.mean(dxh, axis=-1, keepdims=True) - xhat * jnp.mean(dxh * xhat, axis=-1, keepdims=True))
        dv_ref[...] = dvg * _gelu_grad(cv)

        @pl.when(step == nstep - 1)
        def _():
            dbs_ref[...] = _seg_sum(dbs_ref[...], _group_sum_matrix(256, True))

    vec = pl.BlockSpec((1, 256), lambda i: (0, 0))
    mat = pl.BlockSpec((C_GROUPS, C_CHUNK, C_CHUNK), lambda i: (0, 0, 0))
    bsp = pl.BlockSpec((C_CHUNK, 256), lambda i: (0, 0))
    tile = pl.BlockSpec((tm, 256), lambda i: (i, 0))
    return pl.pallas_call(
        body, name=name, grid=(nstep,),
        in_specs=[pl.BlockSpec((tm, 256), lambda i: (i, 5)), pl.BlockSpec((tm, 256), lambda i: (i, 6)),
                  pl.BlockSpec((tm, 256), lambda i: (i, 2)), vec, vec, mat, mat, bsp],
        out_specs=[tile, tile, mat, bsp, vec, vec],
        out_shape=[_sds((t, 256), F32), _sds((t, 256), F32), _sds((C_GROUPS, C_CHUNK, C_CHUNK), F32),
                   _sds((C_CHUNK, 256), F32), _sds((1, 256), F32), _sds((1, 256), F32)],
        scratch_shapes=[pltpu.VMEM((tm, 256), F32)],
        compiler_params=_cparams(ARB),
    )(proj, proj, dycat, gam, bet, ws, wst, bst)


FF_TC = 128
FF_NB = D_FF // FF_TC
FF_CH = 64
FF_HALO = 16


def _edge_taps(ref, first):
    if first:
        ext = ref[0:FF_CH + FF_HALO, :].astype(F32)
        body = slice(0, FF_CH)
    else:
        ext = ref[SEQ - FF_CH - FF_HALO:SEQ, :].astype(F32)
        body = slice(FF_HALO, FF_HALO + FF_CH)
    n = ext.shape[0]
    row = lax.broadcasted_iota(jnp.int32, ext.shape, 0)
    dn = pltpu.roll(ext, 1, 0)
    up = pltpu.roll(ext, n - 1, 0)
    if first:
        dn = jnp.where(row == 0, 0.0, dn)
    else:
        up = jnp.where(row == n - 1, 0.0, up)
    return dn[body], ext[body], up[body]


def _mid_taps(ref, r0):
    ext = ref[pl.ds(pl.multiple_of(r0 - FF_HALO, FF_HALO), FF_CH + 2 * FF_HALO), :].astype(F32)
    n = ext.shape[0]
    body = slice(FF_HALO, FF_HALO + FF_CH)
    return pltpu.roll(ext, 1, 0)[body], ext[body], pltpu.roll(ext, n - 1, 0)[body]


def _chunk_loop(step):
    step(0, lambda ref: _edge_taps(ref, True))

    def mid(i, carry):
        r0 = pl.multiple_of(i * FF_CH, FF_CH)
        step(r0, lambda ref: _mid_taps(ref, r0))
        return carry

    lax.fori_loop(1, SEQ // FF_CH - 1, mid, 0)
    step(SEQ - FF_CH, lambda ref: _edge_taps(ref, False))


def _conv3(taps, w_ref, b_ref):
    dn, md, up = taps
    return w_ref[0:1, :] * dn + w_ref[1:2, :] * md + w_ref[2:3, :] * up + b_ref[...]


def _ff_specs(order):
    def at(fn):
        return (lambda b, j: fn(b, j)) if order == "bj" else (lambda j, b: fn(b, j))
    hs = [pl.BlockSpec((None, SEQ, FF_TC), at(lambda b, j, o=o: (b, 0, j + o))) for o in (0, FF_NB)]
    ws = [pl.BlockSpec((3, FF_TC), at(lambda b, j, o=o: (0, j + o))) for o in (0, FF_NB)]
    bs = [pl.BlockSpec((1, FF_TC), at(lambda b, j, o=o: (0, j + o))) for o in (0, FF_NB)]
    return hs, ws, bs


def _conv_gate_fwd(h, cw, cb, *, name):
    t = h.shape[0]
    bl = t // SEQ

    def body(hg_ref, hu_ref, wg_ref, wu_ref, bg_ref, bu_ref, a_ref):
        def step(r0, taps):
            cg = _conv3(taps(hg_ref), wg_ref, bg_ref)
            cu = _conv3(taps(hu_ref), wu_ref, bu_ref)
            a_ref[pl.ds(r0, FF_CH), :] = (cg * _sigmoid(cg) * cu).astype(BF16)

        _chunk_loop(step)

    hs, ws, bs = _ff_specs("bj")
    h3 = h.reshape(bl, SEQ, 2 * D_FF)
    act = pl.pallas_call(
        body, name=name, grid=(bl, FF_NB), in_specs=hs + ws + bs,
        out_specs=pl.BlockSpec((None, SEQ, FF_TC), lambda b, j: (b, 0, j)),
        out_shape=_sds((bl, SEQ, D_FF), BF16),
        compiler_params=_cparams(PAR, PAR),
    )(h3, h3, cw, cw, cb, cb)
    return act.reshape(t, D_FF)


def _conv_gate_bwd(h, dact, cw, cb, *, name):
    t = h.shape[0]
    bl = t // SEQ

    def body(hg_ref, hu_ref, wg_ref, wu_ref, bg_ref, bu_ref, da_ref,
             dhg_ref, dhu_ref, dwg_ref, dwu_ref, dbg_ref, dbu_ref, dg_s, du_s):
        @pl.when(pl.program_id(1) == 0)
        def _():
            for ref in (dwg_ref, dwu_ref, dbg_ref, dbu_ref):
                ref[...] = jnp.zeros_like(ref)

        red = lambda x: jnp.sum(x, axis=0, keepdims=True)

        def pass1(r0, taps):
            tg, tu = taps(hg_ref), taps(hu_ref)
            cg = _conv3(tg, wg_ref, bg_ref)
            cu = _conv3(tu, wu_ref, bu_ref)
            da = da_ref[pl.ds(r0, FF_CH), :].astype(F32)
            sg = _sigmoid(cg)
            dcg = da * cu * (sg * (1.0 + cg * (1.0 - sg)))
            dcu = da * (cg * sg)
            dg_s[pl.ds(r0, FF_CH), :] = dcg
            du_s[pl.ds(r0, FF_CH), :] = dcu
            for d, tp, dw_ref, db_ref in ((dcg, tg, dwg_ref, dbg_ref), (dcu, tu, dwu_ref, dbu_ref)):
                for k in range(3):
                    dw_ref[k:k + 1, :] += red(d * tp[k])
                db_ref[...] += red(d)

        _chunk_loop(pass1)

        def pass2(r0, taps):
            for s, w_ref, o_ref in ((dg_s, wg_ref, dhg_ref), (du_s, wu_ref, dhu_ref)):
                dn, md, up = taps(s)
                o_ref[pl.ds(r0, FF_CH), :] = (w_ref[0:1, :] * up + w_ref[1:2, :] * md + w_ref[2:3, :] * dn).astype(BF16)

        _chunk_loop(pass2)

    hs, ws, bs = _ff_specs("jb")
    half = pl.BlockSpec((None, SEQ, FF_TC), lambda j, b: (b, 0, j))
    wsp = pl.BlockSpec((3, FF_TC), lambda j, b: (0, j))
    bsp = pl.BlockSpec((1, FF_TC), lambda j, b: (0, j))
    h3 = h.reshape(bl, SEQ, 2 * D_FF)
    dhg, dhu, dwg, dwu, dbg, dbu = pl.pallas_call(
        body, name=name, grid=(FF_NB, bl), in_specs=hs + ws + bs + [half],
        out_specs=[half, half, wsp, wsp, bsp, bsp],
        out_shape=[_sds((bl, SEQ, D_FF), BF16), _sds((bl, SEQ, D_FF), BF16), _sds((3, D_FF), F32), _sds((3, D_FF), F32),
                   _sds((1, D_FF), F32), _sds((1, D_FF), F32)],
        scratch_shapes=[pltpu.VMEM((SEQ, FF_TC), F32), pltpu.VMEM((SEQ, FF_TC), F32)],
        compiler_params=_cparams(PAR, ARB),
    )(h3, h3, cw, cw, cb, cb, dact.reshape(bl, SEQ, D_FF))
    return (dhg.reshape(t, D_FF), dhu.reshape(t, D_FF), jnp.concatenate([dwg, dwu], axis=1),
            jnp.concatenate([dbg, dbu], axis=1))


def _ple_fwd(x2, gain, wg, pe, pe_blk, wp, *, tm, tn, name):
    t, k = x2.shape
    n = wg.shape[1]

    def body(x_ref, g_ref, wg_ref, pe_ref, wp_ref, xr_ref, hn_ref, x3_ref, gt_ref, pp_ref, hn_s):
        @pl.when(pl.program_id(1) == 0)
        def _():
            x = x_ref[...]
            r = lax.rsqrt(jnp.mean(x * x, axis=-1, keepdims=True) + EPS)
            hn_s[...] = (x * r * g_ref[...]).astype(BF16)
            hn_ref[...] = hn_s[...]

        gate = _sigmoid(jnp.dot(hn_s[...], wg_ref[...], preferred_element_type=F32))
        pp = jnp.dot(pe_ref[...].astype(BF16), wp_ref[...], preferred_element_type=F32)
        gt_ref[...] = gate
        pp_ref[...] = pp
        x3_ref[...] = xr_ref[...] + pp * gate

    tile = pl.BlockSpec((tm, tn), lambda i, j: (i, j))
    return pl.pallas_call(
        body, name=name, grid=(t // tm, n // tn),
        in_specs=[pl.BlockSpec((tm, k), lambda i, j: (i, 0)), pl.BlockSpec((1, k), lambda i, j: (0, 0)),
                  pl.BlockSpec((k, tn), lambda i, j: (0, j)), pl.BlockSpec((tm, PLE_DIM), lambda i, j: (pe_blk + i, 0)),
                  pl.BlockSpec((PLE_DIM, tn), lambda i, j: (0, j)), tile],
        out_specs=[pl.BlockSpec((tm, k), lambda i, j: (i, 0)), tile, tile, tile],
        out_shape=[_sds((t, k), BF16), _sds((t, n), F32), _sds((t, n), F32), _sds((t, n), F32)],
        scratch_shapes=[pltpu.VMEM((tm, k), BF16)],
        compiler_params=_cparams(PAR, ARB),
    )(x2, gain, wg, pe, wp, x2)


def _ple_bwd_ew(dx3, gate, pp, *, tm, name):
    t, n = dx3.shape

    def body(d_ref, g_ref, p_ref, dz_ref, dpp_ref):
        d, g = d_ref[...], g_ref[...]
        dz_ref[...] = (d * p_ref[...] * g * (1.0 - g)).astype(BF16)
        dpp_ref[...] = (d * g).astype(BF16)

    spec = pl.BlockSpec((tm, n), lambda i: (i, 0))
    return pl.pallas_call(
        body, name=name, grid=(t // tm,), in_specs=[spec] * 3, out_specs=[spec] * 2,
        out_shape=[_sds((t, n), BF16)] * 2, compiler_params=_cparams(PAR),
    )(dx3, gate, pp)


def _loss_head(y, tgt, *, tm, name):
    t, d = y.shape

    def body(y_ref, t_ref, l_ref, dy_ref):
        @pl.when(pl.program_id(0) == 0)
        def _():
            l_ref[...] = jnp.zeros_like(l_ref)

        e = y_ref[...] - t_ref[...]
        dy_ref[...] = e * (1.0 / d)
        s = jnp.sum(jnp.sum(e * e, axis=1, keepdims=True), axis=0, keepdims=True)
        l_ref[...] += jnp.broadcast_to(s * (0.5 / d), (8, 128))

    spec = pl.BlockSpec((tm, d), lambda i: (i, 0))
    return pl.pallas_call(
        body, name=name, grid=(t // tm,), in_specs=[spec, spec],
        out_specs=[pl.BlockSpec((8, 128), lambda i: (0, 0)), spec],
        out_shape=[_sds((8, 128), F32), _sds((t, d), F32)], compiler_params=_cparams(ARB),
    )(y, tgt)


BIAS_PC = 8192


def _onehot(bucket_row):
    rows = lax.broadcasted_iota(jnp.int32, (REL_BUCKETS, bucket_row.shape[1]), 0)
    return (rows == bucket_row).astype(F32)


def _bias_lookup(table_t, bucket, *, name):
    h = table_t.shape[0]
    p = bucket.shape[1]

    def body(t_ref, b_ref, o_ref):
        bk = b_ref[...]
        val = jnp.dot(t_ref[...], _onehot(bk), precision=HI, preferred_element_type=F32)
        o_ref[...] = jnp.where(bk >= 0, val, NEG_INF)

    return pl.pallas_call(
        body, name=name, grid=(p // BIAS_PC,),
        in_specs=[pl.BlockSpec((h, REL_BUCKETS), lambda i: (0, 0)), pl.BlockSpec((1, BIAS_PC), lambda i: (0, i))],
        out_specs=pl.BlockSpec((h, BIAS_PC), lambda i: (0, i)), out_shape=_sds((h, p), F32),
        compiler_params=_cparams(PAR),
    )(table_t, bucket)


def _bucket_reduce(dbias, bucket, *, name):
    h, p = dbias.shape

    def body(d_ref, b_ref, o_ref):
        @pl.when(pl.program_id(0) == 0)
        def _():
            o_ref[...] = jnp.zeros_like(o_ref)

        o_ref[...] += lax.dot_general(d_ref[...], _onehot(b_ref[...]), NT, precision=HI, preferred_element_type=F32)

    return pl.pallas_call(
        body, name=name, grid=(p // BIAS_PC,),
        in_specs=[pl.BlockSpec((h, BIAS_PC), lambda i: (0, i)), pl.BlockSpec((1, BIAS_PC), lambda i: (0, i))],
        out_specs=pl.BlockSpec((h, REL_BUCKETS), lambda i: (0, 0)), out_shape=_sds((h, REL_BUCKETS), F32),
        compiler_params=_cparams(ARB),
    )(dbias, bucket)


def _adamw_math(w, g, m, v):
    m = ADAM_B1 * m + (1.0 - ADAM_B1) * g
    v = ADAM_B2 * v + (1.0 - ADAM_B2) * (g * g)
    m_hat = m / (1.0 - ADAM_B1 ** ADAM_STEP)
    v_hat = v / (1.0 - ADAM_B2 ** ADAM_STEP)
    delta = -ADAM_LR * (m_hat / (jnp.sqrt(v_hat) + ADAM_EPS) + ADAM_WD * w)
    return delta, m, v


def _adamw_reduce(parts, w, m, v, *, tr, name):
    nl = len(parts)
    rows, c = w.shape
    r = rows // nl
    nt = r // tr

    def body(*refs):
        p_refs = refs[:nl]
        w_ref, m_ref, v_ref, g_ref, d_ref, nm_ref, nv_ref = refs[nl:]
        for li, p_ref in enumerate(p_refs):
            @pl.when(pl.program_id(0) == li)
            def _(p_ref=p_ref):
                g = p_ref[0].astype(F32)
                for k in range(1, N_DEV):
                    g = g + p_ref[k].astype(F32)
                d, nm, nv = _adamw_math(w_ref[...], g, m_ref[...], v_ref[...])
                g_ref[...] = g
                d_ref[...] = d
                nm_ref[...] = nm
                nv_ref[...] = nv

    def part_map(li):
        return lambda l, i: (0, jnp.where(l == li, i, jnp.where(l < li, 0, nt - 1)), 0)

    spec = pl.BlockSpec((tr, c), lambda l, i: (l * nt + i, 0))
    return pl.pallas_call(
        body, name=name, grid=(nl, nt),
        in_specs=[pl.BlockSpec((N_DEV, tr, c), part_map(li)) for li in range(nl)] + [spec, spec, spec],
        out_specs=[spec] * 4, out_shape=[_sds((rows, c), F32)] * 4, compiler_params=_cparams(ARB, ARB),
    )(*parts, w, m, v)


def _adamw_plain(g, w, m, v, *, name):
    def body(g_ref, w_ref, m_ref, v_ref, d_ref, nm_ref, nv_ref):
        d, nm, nv = _adamw_math(w_ref[...], g_ref[...], m_ref[...], v_ref[...])
        d_ref[...] = d
        nm_ref[...] = nm
        nv_ref[...] = nv

    return pl.pallas_call(body, name=name, out_shape=[_sds(w.shape, F32)] * 3)(g, w, m, v)


def _mesh_pos():
    return lax.axis_index("x"), lax.axis_index("y"), lax.axis_index("c")


def _allgather_body(x_refs, out_refs, send_sems, recv_sems, local_sems, slot):
    x, y, c = _mesh_pos()
    me, sibling = (x, y, c), (x, y, 1 - c)
    chips = [(1 - x, y), (x, 1 - y), (1 - x, 1 - y)]
    waits = []
    for a, (x_ref, out_ref) in enumerate(zip(x_refs, out_refs)):
        def copy(k, block, to, src=None, out_ref=out_ref, a=a):
            return pltpu.make_async_remote_copy(
                src_ref=slot(out_ref, block) if src is None else src, dst_ref=slot(out_ref, block),
                send_sem=send_sems.at[a, k], recv_sem=recv_sems.at[a, k], device_id=to, device_id_type=MESH)

        mine = pltpu.make_async_copy(x_ref, slot(out_ref, me), local_sems.at[a])
        mine.start()
        first = [copy(0, me, sibling, src=x_ref)]
        first += [copy(1 + j, me, (*chip, c), src=x_ref) for j, chip in enumerate(chips)]
        for cp in first:
            cp.start()
        waits.append((copy, mine, first))
    sends = []
    for copy, mine, first in waits:
        passed = [copy(4 + j, (*chip, c), sibling) for j, chip in enumerate(chips)]
        for j, chip in enumerate(chips):
            copy(1 + j, (*chip, c), me).wait_recv()
            passed[j].start()
        sends.append(passed)
    for (copy, mine, first), passed in zip(waits, sends):
        copy(0, sibling, me).wait_recv()
        for j, chip in enumerate(chips):
            copy(4 + j, (*chip, 1 - c), me).wait_recv()
        for cp in first + passed:
            cp.wait_send()
        mine.wait()


PEER_FLIPS = ((0, 0, 1), (1, 0, 0), (0, 1, 0), (1, 1, 0), (1, 0, 1), (0, 1, 1), (1, 1, 1))
HBM_SPEC = pl.BlockSpec(memory_space=pltpu.HBM)
SEM_SPEC = pl.BlockSpec(memory_space=pltpu.SEMAPHORE)
DATAFLOW = pltpu.SideEffectType.DATAFLOW_SIDE_EFFECTING


def _peer_copies(x_refs, land_refs, send_sem, recv_sem, scatter):
    x, y, c = _mesh_pos()
    me = 4 * x + 2 * y + c
    copies = []
    for x_ref, land_ref in zip(x_refs, land_refs):
        for fx, fy, fc in PEER_FLIPS:
            px, py, pc = x ^ fx, y ^ fy, c ^ fc
            src = x_ref.at[4 * px + 2 * py + pc] if scatter else x_ref
            copies.append(pltpu.make_async_remote_copy(
                src_ref=src, dst_ref=land_ref.at[me], send_sem=send_sem, recv_sem=recv_sem,
                device_id=(px, py, pc), device_id_type=MESH))
    return copies


def _exchange_start(groups, after, *, scatter, name):
    xs = [x for grp in groups for x in grp]
    na = len(xs)
    ngrp = len(groups)
    firsts = np.cumsum([0] + [len(grp) for grp in groups])
    land_shapes = [x.shape if scatter else (N_DEV,) + x.shape for x in xs]
    extra = [] if after is None else [after]

    def body(*refs):
        x_refs, land_refs = refs[:na], refs[na:2 * na]
        sems = refs[2 * na + len(extra):2 * na + len(extra) + 2 * ngrp]
        token, zeros, local_sem = refs[-3], refs[-2], refs[-1]
        xm, ym, cm = _mesh_pos()
        me = 4 * xm + 2 * ym + cm
        zeros[...] = jnp.zeros_like(zeros)
        locals_ = [pltpu.make_async_copy(zeros, token, local_sem)]
        locals_ += [pltpu.make_async_copy(x_ref.at[me] if scatter else x_ref, land_ref.at[me], local_sem)
                    for x_ref, land_ref in zip(x_refs, land_refs)]
        for own in locals_:
            own.start()
            own.wait()
        for gi in range(ngrp):
            lo, hi = firsts[gi], firsts[gi + 1]
            for cp in _peer_copies(x_refs[lo:hi], land_refs[lo:hi], sems[2 * gi], sems[2 * gi + 1], scatter):
                cp.start()

    sem_shapes = [pltpu.SemaphoreType.DMA(())] * (2 * ngrp)
    lands = [pltpu.with_memory_space_constraint(lax.empty(s, x.dtype), pltpu.HBM) for s, x in zip(land_shapes, xs)]
    outs = pl.pallas_call(
        body, name=name,
        in_specs=[HBM_SPEC] * (2 * na) + [pl.BlockSpec(memory_space=pl.ANY)] * len(extra),
        out_specs=[SEM_SPEC] * (2 * ngrp) + [HBM_SPEC] * (2 * na + 1),
        out_shape=sem_shapes + [pltpu.HBM(x.shape, x.dtype) for x in xs]
        + [pltpu.HBM(s, x.dtype) for s, x in zip(land_shapes, xs)] + [pltpu.HBM((8, 128), F32)],
        input_output_aliases={i: 2 * ngrp + i for i in range(2 * na)},
        scratch_shapes=[pltpu.VMEM((8, 128), F32), pltpu.SemaphoreType.DMA],
        compiler_params=pltpu.CompilerParams(has_side_effects=DATAFLOW),
    )(*[pltpu.with_memory_space_constraint(x, pltpu.HBM) for x in xs], *lands, *extra)
    sems, thru, token = outs[:2 * ngrp], outs[2 * ngrp:2 * ngrp + 2 * na], outs[-1]
    handles = []
    for gi in range(ngrp):
        lo, hi = firsts[gi], firsts[gi + 1]
        handles.append((sems[2 * gi], sems[2 * gi + 1], thru[lo:hi], thru[na + lo:na + hi]))
    return handles, token


def _exchange_wait(handle, after, *, scatter, name):
    send_sems, recv_sems, x_thru, land_thru = handle
    na = len(x_thru)

    def body(*refs):
        x_refs, land_refs = refs[:na], refs[na:2 * na]
        send_ref, recv_ref = refs[2 * na], refs[2 * na + 1]
        for cp in _peer_copies(x_refs, land_refs, send_ref, recv_ref, scatter):
            cp.wait_send()
            cp.wait_recv()

    outs = pl.pallas_call(
        body, name=name,
        in_specs=[HBM_SPEC] * (2 * na) + [SEM_SPEC, SEM_SPEC, pl.BlockSpec(memory_space=pl.ANY)],
        out_specs=[HBM_SPEC] * (2 * na),
        out_shape=[pltpu.HBM(a.shape, a.dtype) for a in list(x_thru) + list(land_thru)],
        input_output_aliases={i: i for i in range(2 * na)},
        compiler_params=pltpu.CompilerParams(has_side_effects=DATAFLOW),
    )(*x_thru, *land_thru, send_sems, recv_sems, after)
    return outs[na:]


def _allgather_vmem(x, *, reduce, name):
    r, c = x.shape

    def body(x_ref, out_ref, *rest):
        if reduce:
            gath, send_sems, recv_sems, local_sems = rest
        else:
            send_sems, recv_sems, local_sems = rest
            gath = out_ref
        _allgather_body([x_ref], [gath], send_sems, recv_sems, local_sems,
                        lambda ref, pos: ref.at[pl.ds((4 * pos[0] + 2 * pos[1] + pos[2]) * r, r), :])
        if reduce:
            acc = gath[0:r, :]
            for k in range(1, N_DEV):
                acc = acc + gath[k * r:(k + 1) * r, :]
            out_ref[...] = acc

    vm = pl.BlockSpec(memory_space=pltpu.VMEM)
    scratch = [pltpu.SemaphoreType.DMA((1, 7)), pltpu.SemaphoreType.DMA((1, 7)), pltpu.SemaphoreType.DMA((1,))]
    if reduce:
        scratch = [pltpu.VMEM((N_DEV * r, c), x.dtype)] + scratch
    return pl.pallas_call(
        body, name=name, in_specs=[vm], out_specs=vm,
        out_shape=_sds((r, c) if reduce else (N_DEV * r, c), x.dtype), scratch_shapes=scratch,
    )(x)


def _t5_bucket(rel):
    nb = REL_BUCKETS // 2
    ret = jnp.where(rel > 0, nb, 0)
    n = jnp.abs(rel)
    max_exact = nb // 2
    nf = jnp.maximum(n, 1).astype(F32)
    large = max_exact + (jnp.log(nf / max_exact) / math.log(REL_MAX_DIST / max_exact)
                         * (nb - max_exact)).astype(jnp.int32)
    large = jnp.minimum(large, nb - 1)
    return ret + jnp.where(n < max_exact, n, large)


def _band_pattern(block, radius, dil):
    kw = block + 2 * radius
    rel = jnp.arange(kw)[None, :] - radius - jnp.arange(block)[:, None]
    return jnp.where(jnp.abs(rel) <= radius, _t5_bucket(rel * dil), -1).astype(jnp.int32).reshape(1, block * kw)


def _rope_tables():
    lane = np.arange(64)
    seg, j = lane // 32, lane % 32
    inv = ROPE_THETA ** (-jnp.arange(0, 32, 2, dtype=F32) / 32)
    tpos = jnp.arange(SEQ)
    pos = jnp.where(jnp.asarray(seg)[None, :] == 0, (tpos // GRID_W)[:, None], (tpos % GRID_W)[:, None])
    ang = pos.astype(F32) * inv[jnp.asarray(j % 16)][None, :]
    cos = jnp.cos(ang)
    sins = jnp.where(jnp.asarray(j)[None, :] < 16, -jnp.sin(ang), jnp.sin(ang))
    return jnp.tile(cos, (1, 4)), jnp.tile(sins, (1, 4))


A_Q, A_K, A_V = (256, 0), (256, 1), (256, 2)
B_Q, B_K, B_V = (256, 0), (128, 2), (128, 3)
A_HEADS = dict(rad=A_RADIUS, nh=4, nkv=4)
B_HEADS = dict(rad=SWA_RADIUS, nh=4, nkv=2)


def _pin(arr, token):
    return arr if token is None else arr + token[0:1, 0:1]


def _local_step(x, pe, tgt, rel_bias, wts, matmul_weights, grads_ready):
    t = x.shape[0]
    bl = t // SEQ
    cos, sins = _rope_tables()
    blocks_a = [min(BAND_BLOCK, SEQ // d) for d in DILATIONS]
    pats_a = [_band_pattern(blk, A_RADIUS, d) for blk, d in zip(blocks_a, DILATIONS)]
    pat_b = _band_pattern(BAND_BLOCK, SWA_RADIUS, 1)
    table_t = rel_bias.T
    bias_a = [_bias_lookup(table_t[:4], pt, name=f"bias_a{ci}").reshape(4, blk, blk + 2 * A_RADIUS)
              for ci, (pt, blk) in enumerate(zip(pats_a, blocks_a))]
    bias_b = _bias_lookup(table_t[4:], pat_b, name="bias_b").reshape(4, BAND_BLOCK, BAND_BLOCK + 2 * SWA_RADIUS)
    nat4 = lambda a: a.reshape(bl, 1, SEQ, a.shape[-1])

    saved = []
    for li in range(DEPTH):
        w = dict(wts[li])
        w.update(matmul_weights(li, "in", x)[0])
        hn0, proj = _norm_mm((x,), w["g_mix"], w["w_in"], None, tm=1024, tn=1152, name="mix_in_fwd")
        more, started = matmul_weights(li, "rest", proj)
        w.update(more)
        w["out_gain"] = _pin(w["out_gain"], started)
        qa1, qa4, qa16, qb, qd = _qkprep_fwd(proj, w["qk_gains"], cos, sins, tm=512, name="qkprep_fwd")
        qa = (nat4(qa1), qa4, qa16)
        oa, la = [], []
        for ci in range(3):
            o, l = _band_fwd(qa[ci], A_Q, A_K, A_V, bias_a[ci], None, name=f"band_a{ci}_fwd", **A_HEADS)
            oa.append(o)
            la.append(l)
        oa[0], la[0] = oa[0].reshape(t, 256), la[0].reshape(t, 256)
        ya, lse_a = _combine_a(oa, la, tm=512, name="combine_a")
        yb, lse_b = _band_fwd(nat4(qb), B_Q, B_K, B_V, bias_b, w["sink_t"], name="band_b_fwd", **B_HEADS)
        yb = yb.reshape(t, 256)
        yc = _c_fwd(proj, w["c_g"], w["c_b"], w["c_ws"], w["c_bst"], tm=512, name="c_fwd")
        yd, lse_d = _dense_fwd(qd, tq=128, name="dense_fwd")
        mixed, x1 = _norm_mm((ya, yb, yc, yd), w["out_gain"], w["w_out"], x, tm=1024, tn=1024, name="mix_out_fwd")
        hn1, h = _norm_mm((x1,), w["g_ffn"], w["w_up"], None, tm=1024, tn=1408, name="ffn_up_fwd", out_dtype=BF16)
        act = _conv_gate_fwd(h, w["conv_w"], w["conv_b"], name="conv_gate_fwd")
        x2 = _mm(act, w["w_down"], "nn", x1, tm=1024, tn=1024, out_dtype=F32, name="ffn_down_fwd")
        hn2, x3, gate, pp = _ple_fwd(x2, w["g_ple"], w["w_gate"], pe, li * (t // 1024), w["w_proj"], tm=1024, tn=512,
                                     name="ple_fwd")
        saved.append(dict(w=w, x0=x, hn0=hn0, proj=proj, qa=qa, qb=qb, qd=qd, ya=ya, lse_a=lse_a, yb=yb, lse_b=lse_b,
                          yc=yc, yd=yd, lse_d=lse_d, mixed=mixed, x1=x1, hn1=hn1, h=h, act=act, x2=x2, hn2=hn2,
                          gate=gate, pp=pp))
        x = x3

    loss_tile, dx = _loss_head(x, tgt, tm=512, name="loss_head")
    grads = [None] * DEPTH
    d_table_a = jnp.zeros((4, REL_BUCKETS), F32)
    d_table_b = jnp.zeros((4, REL_BUCKETS), F32)
    token = None
    for li in reversed(range(DEPTH)):
        s = saved[li]
        w = s["w"]
        g = {}
        w["g_ple"] = _pin(w["g_ple"], token)
        dz, dpp = _ple_bwd_ew(dx, s["gate"], s["pp"], tm=512, name="ple_bwd_ew")
        g["w_gate"] = _mm(s["hn2"], dz, "tn", None, tm=1024, tn=512, out_dtype=BF16, name="dw_gate")
        g["w_proj"] = _mm(pe, dpp, "tn", None, tm=256, tn=1024, out_dtype=BF16, name="dw_proj", a_rows=(li, t))
        dx2, dx2b, g["g_ple"] = _mm_bt_normbwd((dz,), w["w_gate"], (s["x2"],), w["g_ple"], dx, tm=1024, tn=1024,
                                               name="ple_bwd", emit_bf16=True)
        g["w_down"] = _mm(s["act"], dx2b, "tn", None, tm=1408, tn=512, out_dtype=BF16, name="dw_down")
        dact = _mm(dx2b, w["w_down"], "nt", None, tm=1024, tn=1408, out_dtype=BF16, name="ffn_down_bwd")
        dhg, dhu, g["conv_w"], g["conv_b"] = _conv_gate_bwd(s["h"], dact, w["conv_w"], w["conv_b"], name="conv_gate_bwd")
        g["w_up"] = jnp.concatenate(
            [_mm(s["hn1"], dhalf, "tn", None, tm=1024, tn=1408, out_dtype=BF16, name=f"dw_up_{nm}")
             for nm, dhalf in (("gate", dhg), ("up", dhu))], axis=1)
        g_ffn = _pin(w["g_ffn"], grads_ready(li, "mid", g))
        dx1, dx1b, g["g_ffn"] = _mm_bt_normbwd((dhg, dhu), w["w_up"], (s["x1"],), g_ffn, dx2, tm=1024, tn=1408,
                                               name="ffn_up_bwd", emit_bf16=True)
        g["w_out"] = _mm(s["mixed"], dx1b, "tn", None, tm=1024, tn=512, out_dtype=BF16, name="dw_out")
        dycat, g["out_gain"] = _mm_bt_normbwd((dx1b,), w["w_out"], (s["ya"], s["yb"], s["yc"], s["yd"]), w["out_gain"],
                                              None, tm=1024, tn=1024, name="mix_out_bwd")
        dy_r, lse_r, dl_a, dl_b, dl_d = _deltas(dycat, s["ya"], s["yb"], s["yd"], s["lse_a"], tm=512, name="deltas")
        dy_a = (nat4(dycat),) + tuple(dy_r)
        lse_a = (nat4(s["lse_a"]),) + tuple(lse_r)
        dl_a = (nat4(dl_a[0]),) + tuple(dl_a[1:])
        da = []
        for ci in range(3):
            dq, dk, dv, dbias = _band_bwd(s["qa"][ci], A_Q, A_K, A_V, bias_a[ci], None, dy_a[ci], 0, lse_a[ci],
                                          dl_a[ci], name=f"band_a{ci}_bwd", **A_HEADS)
            if ci == 0:
                dq, dk, dv = (a.reshape(t, 256) for a in (dq, dk, dv))
            da.append((dq, dk, dv))
            d_table_a = d_table_a + _bucket_reduce(dbias.reshape(4, -1), pats_a[ci], name=f"bucket_a{ci}")
        dqb, dkb, dvb, dbias_b, dsink = _band_bwd(nat4(s["qb"]), B_Q, B_K, B_V, bias_b, w["sink_t"], nat4(dycat), 1,
                                                  nat4(s["lse_b"]), nat4(dl_b), name="band_b_bwd", **B_HEADS)
        d_table_b = d_table_b + _bucket_reduce(dbias_b.reshape(4, -1), pat_b, name="bucket_b")
        g["sink"] = dsink[:, 0, 0]
        dd = _dense_bwd(s["qd"], dycat, s["lse_d"], dl_d, tq=128, name="dense_bwd")
        dcu, dcv, g["c_ws"], dbs, g["c_g"], g["c_b"] = _c_bwd(s["proj"], dycat, w["c_g"], w["c_b"], w["c_ws"],
                                                               w["c_wst"], w["c_bst"], tm=512, name="c_bwd")
        g["c_bs"] = dbs[:, ::64].T
        db = (dqb.reshape(t, 256), dkb.reshape(t, 128), dvb.reshape(t, 128))
        dproj, dgains = _qkprep_bwd(s["proj"], da, db, dd, dcu, dcv, w["qk_gains"], cos, sins, tm=512, name="qkprep_bwd")
        g["qk_gain"] = dgains[:6, :64].reshape(3, 2, HEAD_DIM)
        g["w_in"] = _mm(s["hn0"], dproj, "tn", None, tm=1024, tn=1152, out_dtype=BF16, name="dw_in")
        dx, g["g_mix"] = _mm_bt_normbwd((dproj,), w["w_in"], (s["x0"],), w["g_mix"], dx1, tm=1024, tn=1152,
                                        name="mix_in_bwd")
        grads[li] = g
        token = grads_ready(li, "end", g)
    d_rel_bias = jnp.concatenate([d_table_a, d_table_b], axis=0).T
    return loss_tile[0, 0], dx, grads, d_rel_bias


WEIGHT_NAMES = ("rel_bias", "ln_mix_g", "w_in", "qk_gain", "sink", "c_norm_g", "c_norm_b", "c_ws", "c_bs", "out_gain",
                "w_out", "ln_ffn_g", "w_up", "conv_w", "conv_b", "w_down", "ln_ple_g", "w_ple_gate", "w_ple_proj")
COL_SHARDED = ("w_in", "w_up", "w_ple_proj")
ROW_SHARDED = ("w_out", "w_down", "w_ple_gate")
SMALL_SHARDED = ("conv_w", "out_gain")
REPLICATED = tuple(n for n in WEIGHT_NAMES if n not in COL_SHARDED + ROW_SHARDED + SMALL_SHARDED)
LOCAL_GRAD_KEY = {"ln_mix_g": "g_mix", "ln_ffn_g": "g_ffn", "ln_ple_g": "g_ple", "c_norm_g": "c_g", "c_norm_b": "c_b",
                  "w_ple_gate": "w_gate", "w_ple_proj": "w_proj"}


def _full_from_gathered(name, gathered):
    _, r, c = gathered.shape
    if name in ROW_SHARDED:
        return gathered.reshape(N_DEV * r, c)
    return jnp.transpose(gathered, (1, 0, 2)).reshape(r, N_DEV * c)


def _slots_from_full(name, full):
    rows, cols = full.shape
    if name in ROW_SHARDED:
        return full.reshape(N_DEV, rows // N_DEV, cols)
    return jnp.transpose(full.reshape(rows, N_DEV, cols // N_DEV), (1, 0, 2))


def _piece_rows(shape):
    return -(-int(np.prod(shape)) // 1024) * 8


def _pack_rows(arrays):
    pieces = []
    for a in arrays:
        n, rows = int(np.prod(a.shape)), _piece_rows(a.shape)
        flat = a.astype(F32).reshape(-1)
        if n != rows * LANES:
            flat = jnp.pad(flat, (0, rows * LANES - n))
        pieces.append(flat.reshape(rows, LANES))
    return jnp.concatenate(pieces, axis=0)


def _unpack_rows(packed, shapes):
    out, off = [], 0
    for shp in shapes:
        n, rows = int(np.prod(shp)), _piece_rows(shp)
        piece = packed[off:off + rows]
        out.append((piece if n == rows * LANES else piece.reshape(-1)[:n]).reshape(shp))
        off += rows
    return out


def kernel(x, p, rel_bias, ln_mix_g, w_in, qk_gain, sink, c_norm_g, c_norm_b, c_ws, c_bs, out_gain, w_out, ln_ffn_g, w_up, conv_w, conv_b, w_down, ln_ple_g, w_ple_gate, w_ple_proj, loss_target, m_rel_bias, m_ln_mix_g, m_w_in, m_qk_gain, m_sink, m_c_norm_g, m_c_norm_b, m_c_ws, m_c_bs, m_out_gain, m_w_out, m_ln_ffn_g, m_w_up, m_conv_w, m_conv_b, m_w_down, m_ln_ple_g, m_w_ple_gate, m_w_ple_proj, v_rel_bias, v_ln_mix_g, v_w_in, v_qk_gain, v_sink, v_c_norm_g, v_c_norm_b, v_c_ws, v_c_bs, v_out_gain, v_w_out, v_ln_ffn_g, v_w_up, v_conv_w, v_conv_b, v_w_down, v_ln_ple_g, v_w_ple_gate, v_w_ple_proj):
    env = dict(locals())
    wt = {n: env[n] for n in WEIGHT_NAMES}
    mom_m = {n: env["m_" + n] for n in WEIGHT_NAMES}
    mom_v = {n: env["v_" + n] for n in WEIGHT_NAMES}
    bl = x.shape[0]
    t = bl * SEQ
    me = 4 * lax.axis_index("x") + 2 * lax.axis_index("y") + lax.axis_index("c")

    big = COL_SHARDED + ROW_SHARDED
    full = {}
    small_shapes = [wt[n].shape for n in SMALL_SHARDED]
    small = _allgather_vmem(_pack_rows([wt[n] for n in SMALL_SHARDED]), reduce=False, name="gather_small")
    small = small.reshape(N_DEV, -1)
    off = 0
    for n, shp in zip(SMALL_SHARDED, small_shapes):
        cnt = int(np.prod(shp))
        g = small[:, off:off + cnt].reshape((N_DEV,) + tuple(shp))
        full[n] = jnp.transpose(g, (1, 2, 0, 3)).reshape(shp[0], shp[1], N_DEV * shp[2])
        off += _piece_rows(shp) * LANES

    def head_gain(li, a, b, reps):
        g = jnp.tile(qk_gain[li, a, b], reps)
        return jnp.pad(g, (0, 256 - g.shape[0]))

    wts = []
    for li in range(DEPTH):
        rows = [head_gain(li, 0, 0, 4), head_gain(li, 0, 1, 4), head_gain(li, 1, 0, 4), head_gain(li, 1, 1, 2),
                head_gain(li, 2, 0, 4), head_gain(li, 2, 1, 2), jnp.zeros((256,), F32), jnp.zeros((256,), F32)]
        wts.append(dict(
            g_mix=ln_mix_g[li].reshape(1, -1), qk_gains=jnp.stack(rows),
            sink_t=jnp.broadcast_to(sink[li][:, None, None], (4, 8, 128)),
            c_g=c_norm_g[li].reshape(1, -1), c_b=c_norm_b[li].reshape(1, -1), c_ws=c_ws[li].astype(BF16),
            c_wst=jnp.transpose(c_ws[li], (0, 2, 1)).astype(BF16), c_bst=jnp.repeat(c_bs[li].T, 64, axis=1),
            out_gain=full["out_gain"][li].reshape(1, -1), g_ffn=ln_ffn_g[li].reshape(1, -1),
            conv_w=full["conv_w"][li], conv_b=conv_b[li].reshape(1, -1), g_ple=ln_ple_g[li].reshape(1, -1)))

    local_key = {"w_ple_gate": "w_gate", "w_ple_proj": "w_proj"}

    gather_names = {"in": ("w_in",), "rest": tuple(n for n in big if n != "w_in")}
    gather_handles = {}

    def start_gather(li, after):
        shards = [[wt[n][li].astype(BF16) for n in gather_names[part]] for part in ("in", "rest")]
        handles, token = _exchange_start(shards, after, scatter=False, name=f"gather_start_{li}")
        gather_handles[li, "in"], gather_handles[li, "rest"] = handles
        return token

    def matmul_weights(li, part, after):
        started = start_gather(li + 1, after) if part == "rest" and li + 1 < DEPTH else None
        lands = _exchange_wait(gather_handles[li, part], after, scatter=False, name=f"gather_wait_{li}_{part}")
        names = gather_names[part]
        return {local_key.get(n, n): _full_from_gathered(n, g) for n, g in zip(names, lands)}, started

    start_gather(0, None)

    mid_names = ("w_ple_gate", "w_ple_proj", "w_down", "w_up")
    end_names = ("w_out", "w_in")
    pending = []

    def start_exchange(li, names, g, tag):
        slots = [_slots_from_full(n, g[local_key.get(n, n)]) for n in names]
        (handle,), token = _exchange_start([slots], None, scatter=True, name=f"grads_start_{li}_{tag}")
        pending.append((li, names, handle, tag))
        return token

    def grads_ready(li, stage, g):
        if li == 0:
            return start_exchange(li, mid_names if stage == "mid" else end_names, g, stage)
        if stage == "end":
            return start_exchange(li, mid_names + end_names, g, stage)
        return None

    loss_part, dx, grads, d_rel_bias = _local_step(
        x.reshape(t, D_MODEL), p.reshape(DEPTH * t, PLE_DIM), loss_target.reshape(t, D_MODEL), rel_bias, wts,
        matmul_weights, grads_ready)
    loss = lax.psum(loss_part, ("x", "y", "c"))

    def local_grad(n):
        if n == "rel_bias":
            return d_rel_bias
        key = LOCAL_GRAD_KEY.get(n, n)
        return jnp.stack([grads[li][key].reshape(wt[n].shape[1:]) if n in REPLICATED else grads[li][key]
                          for li in range(DEPTH)])

    landed = {}
    for li, names, handle, tag in pending:
        for n, land in zip(names, _exchange_wait(handle, dx, scatter=True, name=f"grads_wait_{li}_{tag}")):
            landed[n, li] = land
    out_g, out_d, out_m, out_v = {}, {}, {}, {}
    for n in big:
        shp = wt[n].shape
        two_d = lambda a: a.reshape(-1, shp[-1])
        res = _adamw_reduce([landed[n, li] for li in range(DEPTH)], two_d(wt[n]), two_d(mom_m[n]), two_d(mom_v[n]),
                            tr=32 if n == "w_down" else 128, name="adamw_" + n)
        out_g[n], out_d[n], out_m[n], out_v[n] = [r.reshape(shp) for r in res]

    small_names = REPLICATED + SMALL_SHARDED
    small_full_shapes = [wt[n].shape if n in REPLICATED else full[n].shape for n in small_names]
    reduced = _allgather_vmem(_pack_rows([local_grad(n) for n in small_names]), reduce=True, name="allreduce_small")
    reduced = dict(zip(small_names, _unpack_rows(reduced, small_full_shapes)))
    rep_shapes = [wt[n].shape for n in REPLICATED]
    upd = _adamw_plain(_pack_rows([reduced[n] for n in REPLICATED]), _pack_rows([wt[n] for n in REPLICATED]),
                       _pack_rows([mom_m[n] for n in REPLICATED]), _pack_rows([mom_v[n] for n in REPLICATED]),
                       name="adamw_replicated")
    for dst, packed in zip((out_d, out_m, out_v), upd):
        dst.update(zip(REPLICATED, _unpack_rows(packed, rep_shapes)))
    for n in REPLICATED:
        out_g[n] = reduced[n]
    for n in SMALL_SHARDED:
        shp = wt[n].shape
        g = reduced[n].reshape(shp[0], shp[1], N_DEV, shp[2])
        g = lax.dynamic_index_in_dim(g, me, axis=2, keepdims=False)
        two_d = lambda a: a.reshape(-1, shp[-1])
        res = _adamw_plain(two_d(g), two_d(wt[n]), two_d(mom_m[n]), two_d(mom_v[n]), name="adamw_" + n)
        out_g[n] = g
        out_d[n], out_m[n], out_v[n] = [r.reshape(shp) for r in res]

    return (loss, dx.reshape(bl, SEQ, D_MODEL), *[out_g[n] for n in WEIGHT_NAMES], *[out_d[n] for n in WEIGHT_NAMES],
            *[out_m[n] for n in WEIGHT_NAMES], *[out_v[n] for n in WEIGHT_NAMES])
```

```python
import math

import jax
import jax.numpy as jnp
import numpy as np
from jax import lax
from jax.experimental import pallas as pl
from jax.experimental.pallas import tpu as pltpu
from jax.experimental.pallas import tpu_sc as plsc

F32 = jnp.float32
BF16 = jnp.bfloat16
HI = lax.Precision.HIGHEST

N_DEV = 8
D_MODEL = 1024
SEQ = 2048
DEPTH = 2
HEAD_DIM = 64
IN_WIDTH = 2304
D_FF = 2816
PLE_DIM = 256
C_CHUNK = 128
C_GROUPS = 4
DILATED_CFGS = ((128, 1), (512, 4), (2048, 16))
DILATIONS = tuple(d for _, d in DILATED_CFGS)
A_RADIUS = 64
SWA_RADIUS = 128
BAND_BLOCK = 256
GRID_W = 64
ROPE_THETA = 10000.0
REL_BUCKETS = 32
REL_MAX_DIST = 1024
EPS = 1e-6
NEG_INF = -1e30
ATTN_SCALE = HEAD_DIM ** -0.5
LANES = 128

ADAM_LR = 0.001
ADAM_B1 = 0.9
ADAM_B2 = 0.999
ADAM_EPS = 1e-08
ADAM_WD = 0.01
ADAM_STEP = 10

MESH = pl.DeviceIdType.MESH
NT = (((1,), (1,)), ((), ()))
TN = (((0,), (0,)), ((), ()))
ARB = "arbitrary"
PAR = "parallel"


def _cparams(*sem):
    return pltpu.CompilerParams(dimension_semantics=tuple(sem))


def _sds(shape, dtype):
    return jax.ShapeDtypeStruct(tuple(shape), dtype)


def _group_sum_matrix(n, same_group):
    r = lax.broadcasted_iota(jnp.int32, (n, n), 0)
    c = lax.broadcasted_iota(jnp.int32, (n, n), 1)
    if same_group:
        return ((r >> 6) == (c >> 6)).astype(F32)
    return ((r & 63) == (c & 63)).astype(F32)


def _seg_sum(x, e):
    return jnp.dot(x, e, precision=HI, preferred_element_type=F32)


def _gelu(x):
    c = math.sqrt(2.0 / math.pi)
    return 0.5 * x * (1.0 + jnp.tanh(c * (x + 0.044715 * (x * x * x))))


def _gelu_grad(x):
    c = math.sqrt(2.0 / math.pi)
    t = jnp.tanh(c * (x + 0.044715 * (x * x * x)))
    return 0.5 * (1.0 + t) + 0.5 * x * (1.0 - t * t) * c * (1.0 + 3.0 * 0.044715 * (x * x))


def _sigmoid(x):
    return 1.0 / (1.0 + jnp.exp(-x))


def _scatter_cols(scratch, first, val):
    for c in range(val.shape[1] // LANES):
        scratch[first + c] = val[:, c * LANES:(c + 1) * LANES]


def _gather_cols(scratch, first, ncol):
    return jnp.concatenate([scratch[first + c] for c in range(ncol)], axis=1)


def _read_residue(scratch, first, ncol, r, d):
    n = scratch.shape[1] // d
    return jnp.concatenate([scratch.at[first + c][pl.ds(r, n, stride=d), :] for c in range(ncol)], axis=1)


def _write_residue(scratch, first, r, d, val):
    n = scratch.shape[1] // d
    for c in range(val.shape[1] // LANES):
        scratch.at[first + c][pl.ds(r, n, stride=d), :] = val[:, c * LANES:(c + 1) * LANES]


def _norm_mm(xs, gain, w, res, *, tm, tn, name, out_dtype=F32):
    t = xs[0].shape[0]
    k = sum(x.shape[1] for x in xs)
    n = w.shape[1]
    ng = len(xs)
    has_res = res is not None

    def body(*refs):
        x_refs = refs[:ng]
        g_ref, w_ref = refs[ng], refs[ng + 1]
        res_ref = refs[ng + 2] if has_res else None
        hn_ref, o_ref, hn_s = refs[ng + 2 + has_res:]

        @pl.when(pl.program_id(1) == 0)
        def _():
            off = 0
            for xr in x_refs:
                x = xr[...]
                wd = x.shape[1]
                r = lax.rsqrt(jnp.mean(x * x, axis=-1, keepdims=True) + EPS)
                hn_s[:, off:off + wd] = (x * r * g_ref[:, off:off + wd]).astype(BF16)
                off += wd
            hn_ref[...] = hn_s[...]

        acc = jnp.dot(hn_s[...], w_ref[...], preferred_element_type=F32)
        if has_res:
            acc = acc + res_ref[...]
        o_ref[...] = acc.astype(out_dtype)

    in_specs = [pl.BlockSpec((tm, x.shape[1]), lambda i, j: (i, 0)) for x in xs]
    in_specs += [pl.BlockSpec((1, k), lambda i, j: (0, 0)), pl.BlockSpec((k, tn), lambda i, j: (0, j))]
    args = list(xs) + [gain, w]
    if has_res:
        in_specs.append(pl.BlockSpec((tm, tn), lambda i, j: (i, j)))
        args.append(res)
    return pl.pallas_call(
        body, name=name, grid=(t // tm, n // tn), in_specs=in_specs,
        out_specs=[pl.BlockSpec((tm, k), lambda i, j: (i, 0)), pl.BlockSpec((tm, tn), lambda i, j: (i, j))],
        out_shape=[_sds((t, k), BF16), _sds((t, n), out_dtype)],
        scratch_shapes=[pltpu.VMEM((tm, k), BF16)],
        compiler_params=_cparams(PAR, ARB),
    )(*args)


def _mm(a, b, mode, res, *, tm, tn, out_dtype, name, a_rows=None):
    if mode == "tn":
        kk, m = a.shape
        blk_a = 0
        if a_rows is not None:
            blk_a, kk = a_rows
        a_spec = pl.BlockSpec((kk, tm), lambda i, j: (blk_a, i))
    else:
        m, kk = a.shape
        a_spec = pl.BlockSpec((tm, kk), lambda i, j: (i, 0))
    if mode == "nt":
        n = b.shape[0]
        b_spec = pl.BlockSpec((tn, kk), lambda i, j: (j, 0))
    else:
        n = b.shape[1]
        b_spec = pl.BlockSpec((kk, tn), lambda i, j: (0, j))
    has_res = res is not None

    def body(*refs):
        a_ref, b_ref = refs[0], refs[1]
        o_ref = refs[-1]
        av = a_ref[...].astype(BF16)
        bv = b_ref[...].astype(BF16)
        if mode == "nn":
            acc = jnp.dot(av, bv, preferred_element_type=F32)
        elif mode == "nt":
            acc = lax.dot_general(av, bv, NT, preferred_element_type=F32)
        else:
            acc = lax.dot_general(av, bv, TN, preferred_element_type=F32)
        if has_res:
            acc = acc + refs[2][...]
        o_ref[...] = acc.astype(out_dtype)

    in_specs = [a_spec, b_spec]
    args = [a, b]
    if has_res:
        in_specs.append(pl.BlockSpec((tm, tn), lambda i, j: (i, j)))
        args.append(res)
    return pl.pallas_call(
        body, name=name, grid=(m // tm, n // tn), in_specs=in_specs,
        out_specs=pl.BlockSpec((tm, tn), lambda i, j: (i, j)),
        out_shape=_sds((m, n), out_dtype),
        compiler_params=_cparams(PAR, PAR),
    )(*args)


def _mm_bt_normbwd(dys, w, xs, gain, dres, *, tm, tn, name, emit_bf16=False):
    t, wd_each = dys[0].shape
    nd = len(dys)
    per = wd_each // tn
    nj = nd * per
    k = w.shape[0]
    ng = len(xs)
    has_res = dres is not None

    def body(*refs):
        dy_refs = refs[:nd]
        w_ref = refs[nd]
        x_refs = refs[nd + 1:nd + 1 + ng]
        g_ref = refs[nd + 1 + ng]
        dres_ref = refs[nd + 2 + ng] if has_res else None
        outs = refs[nd + 2 + ng + has_res:]
        dx_ref = outs[0]
        dxb_ref = outs[1] if emit_bf16 else None
        dg_ref, acc = outs[1 + emit_bf16:]
        i, j = pl.program_id(0), pl.program_id(1)

        @pl.when(j == 0)
        def _():
            acc[...] = jnp.zeros_like(acc)

        for d, dy_ref in enumerate(dy_refs):
            @pl.when((j >= d * per) & (j < (d + 1) * per))
            def _(dy_ref=dy_ref):
                acc[...] += lax.dot_general(dy_ref[...].astype(BF16), w_ref[...], NT, preferred_element_type=F32)

        @pl.when(j == nj - 1)
        def _():
            @pl.when(i == 0)
            def _():
                dg_ref[...] = jnp.zeros_like(dg_ref)

            off = 0
            for xr in x_refs:
                x = xr[...]
                wd = x.shape[1]
                g = g_ref[:, off:off + wd]
                dyn = acc[:, off:off + wd]
                r = lax.rsqrt(jnp.mean(x * x, axis=-1, keepdims=True) + EPS)
                gdy = dyn * g
                dx = r * gdy - x * (r * r * r * jnp.mean(gdy * x, axis=-1, keepdims=True))
                if has_res:
                    dx = dx + dres_ref[:, off:off + wd]
                dx_ref[:, off:off + wd] = dx
                if emit_bf16:
                    dxb_ref[:, off:off + wd] = dx.astype(BF16)
                dg_ref[:, off:off + wd] += jnp.sum(dyn * x * r, axis=0, keepdims=True)
                off += wd

    def dy_map(d):
        return lambda i, j: (i, jnp.clip(j - d * per, 0, per - 1))

    in_specs = [pl.BlockSpec((tm, tn), dy_map(d)) for d in range(nd)]
    in_specs.append(pl.BlockSpec((k, tn), lambda i, j: (0, j)))
    in_specs += [pl.BlockSpec((tm, x.shape[1]), lambda i, j: (i, 0)) for x in xs]
    in_specs.append(pl.BlockSpec((1, k), lambda i, j: (0, 0)))
    args = list(dys) + [w] + list(xs) + [gain]
    if has_res:
        in_specs.append(pl.BlockSpec((tm, k), lambda i, j: (i, 0)))
        args.append(dres)
    row = pl.BlockSpec((tm, k), lambda i, j: (i, 0))
    out_specs = [row] + ([row] if emit_bf16 else []) + [pl.BlockSpec((1, k), lambda i, j: (0, 0))]
    out_shape = [_sds((t, k), F32)] + ([_sds((t, k), BF16)] if emit_bf16 else []) + [_sds((1, k), F32)]
    return pl.pallas_call(
        body, name=name, grid=(t // tm, nj), in_specs=in_specs, out_specs=out_specs, out_shape=out_shape,
        scratch_shapes=[pltpu.VMEM((tm, k), F32)],
        compiler_params=_cparams(ARB, ARB),
    )(*args)


def _rope_partner(y):
    n = y.shape[1]
    lane = lax.broadcasted_iota(jnp.int32, y.shape, 1)
    return jnp.where((lane & 31) < 16, pltpu.roll(y, n - 16, 1), pltpu.roll(y, 16, 1))


def _residue_specs(tm, width, nt):
    specs = [pl.BlockSpec((tm, width), lambda b, i: (b * nt + i, 0))]
    for d in DILATIONS[1:]:
        specs.append(pl.BlockSpec((None, d, tm // d, width), lambda b, i: (b, 0, i, 0)))
    return specs


def _residue_shapes(bl, width, dtype):
    return [_sds((bl * SEQ, width), dtype)] + [_sds((bl, d, SEQ // d, width), dtype) for d in DILATIONS[1:]]


def _qkprep_fwd(proj, gains, cos, sins, *, tm, name):
    t = proj.shape[0]
    bl = t // SEQ
    nt = SEQ // tm

    def body(p_ref, g_ref, c_ref, s_ref, qa1_ref, qa4_ref, qa16_ref, qb_ref, qd_ref, scr):
        e = _group_sum_matrix(256, True)

        def hn(x, row):
            wd = x.shape[1]
            ms = _seg_sum(x * x, e[:wd, :wd]) * (1.0 / HEAD_DIM)
            return x * lax.rsqrt(ms + EPS) * g_ref[row:row + 1, :wd]

        qa = jnp.concatenate([hn(p_ref[:, 0:256], 0) * ATTN_SCALE, hn(p_ref[:, 256:512], 1), p_ref[:, 512:768]], axis=1)
        qa1_ref[...] = qa.astype(BF16)
        _scatter_cols(scr, 0, qa)
        for d, ref in ((4, qa4_ref), (16, qa16_ref)):
            for r in range(d):
                ref[r] = _read_residue(scr, 0, 6, r, d).astype(BF16)
        qb_ref[:, 0:256] = (hn(p_ref[:, 768:1024], 2) * ATTN_SCALE).astype(BF16)
        qb_ref[:, 256:384] = hn(p_ref[:, 1024:1152], 3).astype(BF16)
        qb_ref[:, 384:512] = p_ref[:, 1152:1280].astype(BF16)
        yq = hn(p_ref[:, 1792:2048], 4)
        yq = yq * c_ref[...] + _rope_partner(yq) * s_ref[...]
        qd_ref[:, 0:256] = (yq * ATTN_SCALE).astype(BF16)
        yk = hn(p_ref[:, 2048:2176], 5)
        yk = yk * c_ref[:, 0:128] + _rope_partner(yk) * s_ref[:, 0:128]
        qd_ref[:, 256:384] = yk.astype(BF16)
        qd_ref[:, 384:512] = p_ref[:, 2176:2304].astype(BF16)

    row = lambda width: pl.BlockSpec((tm, width), lambda b, i: (b * nt + i, 0))
    tab = pl.BlockSpec((tm, 256), lambda b, i: (i, 0))
    return pl.pallas_call(
        body, name=name, grid=(bl, nt),
        in_specs=[row(IN_WIDTH), pl.BlockSpec((8, 256), lambda b, i: (0, 0)), tab, tab],
        out_specs=_residue_specs(tm, 768, nt) + [row(512), row(512)],
        out_shape=_residue_shapes(bl, 768, BF16) + [_sds((t, 512), BF16), _sds((t, 512), BF16)],
        scratch_shapes=[pltpu.VMEM((6, tm, LANES), F32)],
        compiler_params=_cparams(PAR, PAR),
    )(proj, gains, cos, sins)


def _qkprep_bwd(proj, da, db, dd, dcu, dcv, gains, cos, sins, *, tm, name):
    t = proj.shape[0]
    bl = t // SEQ
    nt = SEQ // tm
    flat = [a for cfg in da for a in cfg] + list(db) + list(dd) + [dcu, dcv]

    def body(*refs):
        p_ref, g_ref, c_ref, s_ref = refs[:4]
        d_refs = refs[4:4 + len(flat)]
        dp_ref, dg_ref, scr = refs[4 + len(flat):]
        a_refs = d_refs[:9]
        dqb_ref, dkb_ref, dvb_ref, dqd_ref, dkd_ref, dvd_ref, dcu_ref, dcv_ref = d_refs[9:]
        e = _group_sum_matrix(256, True)
        first = (pl.program_id(0) == 0) & (pl.program_id(1) == 0)
        last = (pl.program_id(0) == bl - 1) & (pl.program_id(1) == nt - 1)

        @pl.when(first)
        def _():
            dg_ref[...] = jnp.zeros_like(dg_ref)

        def hn_bwd(x, dy, row):
            wd = x.shape[1]
            ee = e[:wd, :wd]
            g = g_ref[row:row + 1, :wd]
            r = lax.rsqrt(_seg_sum(x * x, ee) * (1.0 / HEAD_DIM) + EPS)
            gdy = dy * g
            dx = r * gdy - x * (r * r * r * (_seg_sum(gdy * x, ee) * (1.0 / HEAD_DIM)))
            dg_ref[row:row + 1, :wd] += jnp.sum(dy * x * r, axis=0, keepdims=True)
            return dx

        def rope_bwd(dy, wd):
            return dy * c_ref[:, :wd] + _rope_partner(dy * s_ref[:, :wd])

        dqkv = jnp.concatenate([a_refs[0][...], a_refs[1][...], a_refs[2][...]], axis=1)
        for ci, d in ((1, 4), (2, 16)):
            for r in range(d):
                part = jnp.concatenate([a_refs[3 * ci + m][r] for m in range(3)], axis=1)
                _write_residue(scr, 0, r, d, part)
            dqkv = dqkv + _gather_cols(scr, 0, 6)
        dp_ref[:, 0:256] = hn_bwd(p_ref[:, 0:256], dqkv[:, 0:256] * ATTN_SCALE, 0).astype(BF16)
        dp_ref[:, 256:512] = hn_bwd(p_ref[:, 256:512], dqkv[:, 256:512], 1).astype(BF16)
        dp_ref[:, 512:768] = dqkv[:, 512:768].astype(BF16)
        dp_ref[:, 768:1024] = hn_bwd(p_ref[:, 768:1024], dqb_ref[...] * ATTN_SCALE, 2).astype(BF16)
        dp_ref[:, 1024:1152] = hn_bwd(p_ref[:, 1024:1152], dkb_ref[...], 3).astype(BF16)
        dp_ref[:, 1152:1280] = dvb_ref[...].astype(BF16)
        dp_ref[:, 1280:1536] = dcu_ref[...].astype(BF16)
        dp_ref[:, 1536:1792] = dcv_ref[...].astype(BF16)
        dp_ref[:, 1792:2048] = hn_bwd(p_ref[:, 1792:2048], rope_bwd(dqd_ref[...] * ATTN_SCALE, 256), 4).astype(BF16)
        dp_ref[:, 2048:2176] = hn_bwd(p_ref[:, 2048:2176], rope_bwd(dkd_ref[...], 128), 5).astype(BF16)
        dp_ref[:, 2176:2304] = dvd_ref[...].astype(BF16)

        @pl.when(last)
        def _():
            dg_ref[...] = _seg_sum(dg_ref[...], _group_sum_matrix(256, False))

    row = lambda width: pl.BlockSpec((tm, width), lambda b, i: (b * nt + i, 0))
    tab = pl.BlockSpec((tm, 256), lambda b, i: (i, 0))
    in_specs = [row(IN_WIDTH), pl.BlockSpec((8, 256), lambda b, i: (0, 0)), tab, tab]
    res_specs = _residue_specs(tm, 256, nt)
    in_specs += [res_specs[ci] for ci in range(3) for _ in range(3)]
    in_specs += [row(a.shape[1]) for a in flat[9:]]
    return pl.pallas_call(
        body, name=name, grid=(bl, nt), in_specs=in_specs,
        out_specs=[row(IN_WIDTH), pl.BlockSpec((8, 256), lambda b, i: (0, 0))],
        out_shape=[_sds((t, IN_WIDTH), BF16), _sds((8, 256), F32)],
        scratch_shapes=[pltpu.VMEM((6, tm, LANES), F32)],
        compiler_params=_cparams(ARB, ARB),
    )(proj, gains, cos, sins, *flat)


def _band_spec(seq_len, spec):
    width, idx = spec
    return pl.BlockSpec((None, None, seq_len, width), lambda b, r: (b, r, 0, idx))


def _fill_padded(dst, src_ref, rad, seq_len):
    z = jnp.zeros((rad, dst.shape[1]), dst.dtype)
    dst[0:rad, :] = z
    dst[rad + seq_len:rad + seq_len + rad, :] = z
    dst[rad:rad + seq_len, :] = src_ref[...]


def _band_fwd(src, qs, ks, vs, bias, sink, *, rad, nh, nkv, name):
    bl, dil, sl, _ = src.shape
    blk = bias.shape[1]
    kw = blk + 2 * rad
    nb = sl // blk
    rep = nh // nkv
    has_sink = sink is not None

    def body(*refs):
        q_ref, k_ref, v_ref, b_ref = refs[:4]
        s_ref = refs[4] if has_sink else None
        o_ref, l_ref, kp, vp = refs[4 + has_sink:]
        _fill_padded(kp, k_ref, rad, sl)
        _fill_padded(vp, v_ref, rad, sl)

        def blk_body(i, carry):
            r0 = pl.multiple_of(i * blk, blk)
            qb = q_ref[pl.ds(r0, blk), :]
            kwin = kp[pl.ds(r0, kw), :]
            vwin = vp[pl.ds(r0, kw), :]
            col = r0 - rad + lax.broadcasted_iota(jnp.int32, (blk, kw), 1)
            neg = jnp.where((col >= 0) & (col < sl), 0.0, NEG_INF).astype(F32)
            for h in range(nh):
                g = h // rep
                hs = slice(h * HEAD_DIM, (h + 1) * HEAD_DIM)
                gs = slice(g * HEAD_DIM, (g + 1) * HEAD_DIM)
                s = lax.dot_general(qb[:, hs], kwin[:, gs], NT, preferred_element_type=F32)
                s = s + b_ref[h] + neg
                m = jnp.max(s, axis=1, keepdims=True)
                if has_sink:
                    sk = s_ref[h][0:1, 0:1]
                    m = jnp.maximum(m, sk)
                p = jnp.exp(s - m)
                den = jnp.sum(p, axis=1, keepdims=True)
                if has_sink:
                    den = den + jnp.exp(sk - m)
                o = jnp.dot(p.astype(BF16), vwin[:, gs], preferred_element_type=F32) / den
                o_ref[pl.ds(r0, blk), hs] = o
                l_ref[pl.ds(r0, blk), hs] = jnp.broadcast_to(m + jnp.log(den), (blk, HEAD_DIM))
            return carry

        lax.fori_loop(0, nb, blk_body, 0)

    in_specs = [_band_spec(sl, qs), _band_spec(sl, ks), _band_spec(sl, vs),
                pl.BlockSpec((nh, blk, kw), lambda b, r: (0, 0, 0))]
    args = [src] * 3 + [bias]
    if has_sink:
        in_specs.append(pl.BlockSpec((nh, 8, 128), lambda b, r: (0, 0, 0)))
        args.append(sink)
    return pl.pallas_call(
        body, name=name, grid=(bl, dil), in_specs=in_specs,
        out_specs=[_band_spec(sl, (256, 0))] * 2,
        out_shape=[_sds((bl, dil, sl, 256), F32)] * 2,
        scratch_shapes=[pltpu.VMEM((sl + 2 * rad, ks[0]), BF16), pltpu.VMEM((sl + 2 * rad, vs[0]), BF16)],
        compiler_params=_cparams(PAR, PAR),
    )(*args)


def _band_bwd(src, qs, ks, vs, bias, sink, dy, dcol, lse, delta, *, rad, nh, nkv, name):
    bl, dil, sl, _ = src.shape
    blk = bias.shape[1]
    kw = blk + 2 * rad
    nb = sl // blk
    rep = nh // nkv
    has_sink = sink is not None
    wk, wv = ks[0], vs[0]

    def body(*refs):
        q_ref, k_ref, v_ref, b_ref = refs[:4]
        s_ref = refs[4] if has_sink else None
        do_ref, l_ref, dl_ref = refs[4 + has_sink:7 + has_sink]
        outs = refs[7 + has_sink:]
        if has_sink:
            dq_ref, dk_ref, dv_ref, db_ref, dsk_ref, kp, vp, dka, dva = outs
        else:
            dq_ref, dk_ref, dv_ref, db_ref, kp, vp, dka, dva = outs

        @pl.when((pl.program_id(0) == 0) & (pl.program_id(1) == 0))
        def _():
            db_ref[...] = jnp.zeros_like(db_ref)
            if has_sink:
                dsk_ref[...] = jnp.zeros_like(dsk_ref)

        _fill_padded(kp, k_ref, rad, sl)
        _fill_padded(vp, v_ref, rad, sl)
        dka[...] = jnp.zeros_like(dka)
        dva[...] = jnp.zeros_like(dva)

        def blk_body(i, carry):
            r0 = pl.multiple_of(i * blk, blk)
            qb = q_ref[pl.ds(r0, blk), :]
            kwin = kp[pl.ds(r0, kw), :]
            vwin = vp[pl.ds(r0, kw), :]
            dob = do_ref[pl.ds(r0, blk), :].astype(BF16)
            lb = l_ref[pl.ds(r0, blk), :]
            dlb = dl_ref[pl.ds(r0, blk), :]
            col = r0 - rad + lax.broadcasted_iota(jnp.int32, (blk, kw), 1)
            neg = jnp.where((col >= 0) & (col < sl), 0.0, NEG_INF).astype(F32)
            for h in range(nh):
                g = h // rep
                hs = slice(h * HEAD_DIM, (h + 1) * HEAD_DIM)
                gs = slice(g * HEAD_DIM, (g + 1) * HEAD_DIM)
                qh, kh, vh, doh = qb[:, hs], kwin[:, gs], vwin[:, gs], dob[:, hs]
                lh = lb[:, h * HEAD_DIM:h * HEAD_DIM + 1]
                dlh = dlb[:, h * HEAD_DIM:h * HEAD_DIM + 1]
                s = lax.dot_general(qh, kh, NT, preferred_element_type=F32) + b_ref[h] + neg
                p = jnp.exp(s - lh)
                dp = lax.dot_general(doh, vh, NT, preferred_element_type=F32)
                ds = p * (dp - dlh)
                dsb = ds.astype(BF16)
                dq_ref[pl.ds(r0, blk), hs] = jnp.dot(dsb, kh, preferred_element_type=F32)
                dka[pl.ds(r0, kw), gs] += lax.dot_general(dsb, qh, TN, preferred_element_type=F32)
                dva[pl.ds(r0, kw), gs] += lax.dot_general(p.astype(BF16), doh, TN, preferred_element_type=F32)
                db_ref[h] += ds
                if has_sink:
                    ps = jnp.exp(s_ref[h][0:1, 0:1] - lh)
                    dsk_ref[h] += jnp.broadcast_to(-jnp.sum(ps * dlh, axis=0, keepdims=True), (8, 128))
            return carry

        lax.fori_loop(0, nb, blk_body, 0)
        dk_ref[...] = dka[rad:rad + sl, :]
        dv_ref[...] = dva[rad:rad + sl, :]

    const3 = lambda b, r: (0, 0, 0)
    in_specs = [_band_spec(sl, qs), _band_spec(sl, ks), _band_spec(sl, vs), pl.BlockSpec((nh, blk, kw), const3)]
    args = [src] * 3 + [bias]
    if has_sink:
        in_specs.append(pl.BlockSpec((nh, 8, 128), const3))
        args.append(sink)
    row = _band_spec(sl, (256, 0))
    in_specs += [_band_spec(sl, (256, dcol)), row, row]
    args += [dy, lse, delta]
    out_specs = [row, _band_spec(sl, (wk, 0)), _band_spec(sl, (wv, 0)), pl.BlockSpec((nh, blk, kw), const3)]
    out_shape = [_sds((bl, dil, sl, 256), F32), _sds((bl, dil, sl, wk), F32), _sds((bl, dil, sl, wv), F32),
                 _sds((nh, blk, kw), F32)]
    if has_sink:
        out_specs.append(pl.BlockSpec((nh, 8, 128), const3))
        out_shape.append(_sds((nh, 8, 128), F32))
    return pl.pallas_call(
        body, name=name, grid=(bl, dil), in_specs=in_specs, out_specs=out_specs, out_shape=out_shape,
        scratch_shapes=[pltpu.VMEM((sl + 2 * rad, wk), BF16), pltpu.VMEM((sl + 2 * rad, wv), BF16),
                        pltpu.VMEM((sl + 2 * rad, wk), F32), pltpu.VMEM((sl + 2 * rad, wv), F32)],
        compiler_params=_cparams(ARB, ARB),
    )(*args)


def _combine_a(os_, ls_, *, tm, name):
    bl = os_[1].shape[0]
    t = bl * SEQ
    nt = SEQ // tm

    def body(o1, o4, o16, l1, l4, l16, y_ref, lt_ref, scr):
        for k, (d, ref) in enumerate(((4, o4), (16, o16), (4, l4), (16, l16))):
            for r in range(d):
                _write_residue(scr, 2 * k, r, d, ref[r])
        o2, o3, b, c = (_gather_cols(scr, 2 * k, 2) for k in range(4))
        a = l1[...]
        m = jnp.maximum(jnp.maximum(a, b), c)
        ea, eb, ec = jnp.exp(a - m), jnp.exp(b - m), jnp.exp(c - m)
        den = ea + eb + ec
        y_ref[...] = (ea / den) * o1[...] + (eb / den) * o2 + (ec / den) * o3
        lt_ref[...] = m + jnp.log(den)

    specs = _residue_specs(tm, 256, nt)
    return pl.pallas_call(
        body, name=name, grid=(bl, nt), in_specs=specs * 2, out_specs=[specs[0]] * 2,
        out_shape=[_sds((t, 256), F32)] * 2, scratch_shapes=[pltpu.VMEM((8, tm, LANES), F32)],
        compiler_params=_cparams(PAR, PAR),
    )(*os_, *ls_)


def _deltas(dycat, ya, yb, yd, lse_a, *, tm, name):
    t = ya.shape[0]
    bl = t // SEQ
    nt = SEQ // tm

    def body(dy_ref, ya_ref, yb_ref, yd_ref, la_ref, dy4, dy16, l4, l16, da1, da4, da16, db_ref, dd_ref, scr):
        e = _group_sum_matrix(256, True)
        dya = dy_ref[:, 0:256]
        dla = _seg_sum(dya * ya_ref[...], e)
        da1[...] = dla
        db_ref[...] = _seg_sum(dy_ref[:, 256:512] * yb_ref[...], e)
        dd_ref[...] = _seg_sum(dy_ref[:, 768:1024] * yd_ref[...], e)
        for k, (val, r4, r16) in enumerate(((dya, dy4, dy16), (la_ref[...], l4, l16), (dla, da4, da16))):
            _scatter_cols(scr, 2 * k, val)
            for d, ref in ((4, r4), (16, r16)):
                for r in range(d):
                    ref[r] = _read_residue(scr, 2 * k, 2, r, d)

    specs = _residue_specs(tm, 256, nt)
    nat = specs[0]
    shapes = _residue_shapes(bl, 256, F32)
    outs = pl.pallas_call(
        body, name=name, grid=(bl, nt),
        in_specs=[pl.BlockSpec((tm, 1024), lambda b, i: (b * nt + i, 0)), nat, nat, nat, nat],
        out_specs=specs[1:] + specs[1:] + specs + [nat, nat],
        out_shape=shapes[1:] + shapes[1:] + shapes + [shapes[0], shapes[0]],
        scratch_shapes=[pltpu.VMEM((6, tm, LANES), F32)],
        compiler_params=_cparams(PAR, PAR),
    )(dycat, ya, yb, yd, lse_a)
    return outs[0:2], outs[2:4], outs[4:7], outs[7], outs[8]


def _dense_fwd(qd, *, tq, name):
    t = qd.shape[0]
    bl = t // SEQ
    nq = SEQ // tq

    def body(q_ref, k_ref, v_ref, o_ref, l_ref):
        q = q_ref[...]
        for g in range(2):
            h0, h1 = 2 * g, 2 * g + 1
            q2 = jnp.concatenate([q[:, h0 * 64:(h0 + 1) * 64], q[:, h1 * 64:(h1 + 1) * 64]], axis=0)
            kg = k_ref[:, g * 64:(g + 1) * 64]
            vg = v_ref[:, g * 64:(g + 1) * 64]
            s = lax.dot_general(q2, kg, NT, preferred_element_type=F32)
            m = jnp.max(s, axis=1, keepdims=True)
            p = jnp.exp(s - m)
            den = jnp.sum(p, axis=1, keepdims=True)
            o2 = jnp.dot(p.astype(BF16), vg, preferred_element_type=F32) / den
            l2 = jnp.broadcast_to(m + jnp.log(den), (2 * tq, 64))
            o_ref[:, h0 * 64:(h0 + 1) * 64] = o2[:tq]
            o_ref[:, h1 * 64:(h1 + 1) * 64] = o2[tq:]
            l_ref[:, h0 * 64:(h0 + 1) * 64] = l2[:tq]
            l_ref[:, h1 * 64:(h1 + 1) * 64] = l2[tq:]

    q3 = qd.reshape(bl, SEQ, 512)
    o, lse = pl.pallas_call(
        body, name=name, grid=(bl, nq),
        in_specs=[pl.BlockSpec((None, tq, 256), lambda b, i: (b, i, 0)),
                  pl.BlockSpec((None, SEQ, 128), lambda b, i: (b, 0, 2)),
                  pl.BlockSpec((None, SEQ, 128), lambda b, i: (b, 0, 3))],
        out_specs=[pl.BlockSpec((None, tq, 256), lambda b, i: (b, i, 0))] * 2,
        out_shape=[_sds((bl, SEQ, 256), F32)] * 2,
        compiler_params=_cparams(PAR, PAR),
    )(q3, q3, q3)
    return o.reshape(t, 256), lse.reshape(t, 256)


def _dense_bwd(qd, dycat, lse, delta, *, tq, name):
    t = qd.shape[0]
    bl = t // SEQ
    nq = SEQ // tq

    def body(q_ref, k_ref, v_ref, do_ref, l_ref, dl_ref, dq_ref, dk_ref, dv_ref):
        @pl.when(pl.program_id(1) == 0)
        def _():
            dk_ref[...] = jnp.zeros_like(dk_ref)
            dv_ref[...] = jnp.zeros_like(dv_ref)

        q = q_ref[...]
        do = do_ref[...].astype(BF16)
        lv = l_ref[...]
        dlv = dl_ref[...]
        for g in range(2):
            h0, h1 = 2 * g, 2 * g + 1
            q2 = jnp.concatenate([q[:, h0 * 64:(h0 + 1) * 64], q[:, h1 * 64:(h1 + 1) * 64]], axis=0)
            do2 = jnp.concatenate([do[:, h0 * 64:(h0 + 1) * 64], do[:, h1 * 64:(h1 + 1) * 64]], axis=0)
            l2 = jnp.concatenate([lv[:, h0 * 64:h0 * 64 + 1], lv[:, h1 * 64:h1 * 64 + 1]], axis=0)
            dl2 = jnp.concatenate([dlv[:, h0 * 64:h0 * 64 + 1], dlv[:, h1 * 64:h1 * 64 + 1]], axis=0)
            kg = k_ref[:, g * 64:(g + 1) * 64]
            vg = v_ref[:, g * 64:(g + 1) * 64]
            s = lax.dot_general(q2, kg, NT, preferred_element_type=F32)
            p = jnp.exp(s - l2)
            dp = lax.dot_general(do2, vg, NT, preferred_element_type=F32)
            ds = (p * (dp - dl2)).astype(BF16)
            dq2 = jnp.dot(ds, kg, preferred_element_type=F32)
            dq_ref[:, h0 * 64:(h0 + 1) * 64] = dq2[:tq]
            dq_ref[:, h1 * 64:(h1 + 1) * 64] = dq2[tq:]
            dk_ref[:, g * 64:(g + 1) * 64] += lax.dot_general(ds, q2, TN, preferred_element_type=F32)
            dv_ref[:, g * 64:(g + 1) * 64] += lax.dot_general(p.astype(BF16), do2, TN, preferred_element_type=F32)

    q3 = qd.reshape(bl, SEQ, 512)
    tile = pl.BlockSpec((None, tq, 256), lambda b, i: (b, i, 0))
    full = pl.BlockSpec((None, SEQ, 128), lambda b, i: (b, 0, 0))
    dq, dk, dv = pl.pallas_call(
        body, name=name, grid=(bl, nq),
        in_specs=[tile, pl.BlockSpec((None, SEQ, 128), lambda b, i: (b, 0, 2)),
                  pl.BlockSpec((None, SEQ, 128), lambda b, i: (b, 0, 3)),
                  pl.BlockSpec((None, tq, 256), lambda b, i: (b, i, 3)), tile, tile],
        out_specs=[tile, full, full],
        out_shape=[_sds((bl, SEQ, 256), F32), _sds((bl, SEQ, 128), F32), _sds((bl, SEQ, 128), F32)],
        compiler_params=_cparams(PAR, ARB),
    )(q3, q3, q3, dycat.reshape(bl, SEQ, 1024), lse.reshape(bl, SEQ, 256), delta.reshape(bl, SEQ, 256))
    return dq.reshape(t, 256), dk.reshape(t, 128), dv.reshape(t, 128)


def _c_norm(cv, gam, bet):
    vg = _gelu(cv)
    mu = jnp.mean(vg, axis=-1, keepdims=True)
    xc = vg - mu
    r = lax.rsqrt(jnp.mean(xc * xc, axis=-1, keepdims=True) + EPS)
    xhat = xc * r
    return xhat * gam + bet, xhat, r


def _c_fwd(proj, gam, bet, ws, bst, *, tm, name):
    t = proj.shape[0]
    nch = tm // C_CHUNK

    def body(u_ref, v_ref, g_ref, b_ref, ws_ref, bs_ref, y_ref):
        vn, _, _ = _c_norm(v_ref[...], g_ref[...], b_ref[...])
        vnb = vn.astype(BF16)
        for c in range(nch):
            rows = slice(c * C_CHUNK, (c + 1) * C_CHUNK)
            for g in range(C_GROUPS):
                gs = slice(g * 64, (g + 1) * 64)
                mixed = jnp.dot(ws_ref[g], vnb[rows, gs], preferred_element_type=F32) + bs_ref[:, gs]
                y_ref[rows, gs] = _gelu(u_ref[rows, gs]) * mixed

    vec = pl.BlockSpec((1, 256), lambda i: (0, 0))
    return pl.pallas_call(
        body, name=name, grid=(t // tm,),
        in_specs=[pl.BlockSpec((tm, 256), lambda i: (i, 5)), pl.BlockSpec((tm, 256), lambda i: (i, 6)), vec, vec,
                  pl.BlockSpec((C_GROUPS, C_CHUNK, C_CHUNK), lambda i: (0, 0, 0)),
                  pl.BlockSpec((C_CHUNK, 256), lambda i: (0, 0))],
        out_specs=pl.BlockSpec((tm, 256), lambda i: (i, 0)), out_shape=_sds((t, 256), F32),
        compiler_params=_cparams(PAR),
    )(proj, proj, gam, bet, ws, bst)


def _c_bwd(proj, dycat, gam, bet, ws, wst, bst, *, tm, name):
    t = proj.shape[0]
    nch = tm // C_CHUNK
    nstep = t // tm

    def body(u_ref, v_ref, dy_ref, g_ref, b_ref, ws_ref, wst_ref, bs_ref,
             du_ref, dv_ref, dws_ref, dbs_ref, dg_ref, db_ref, dvn_s):
        step = pl.program_id(0)

        @pl.when(step == 0)
        def _():
            dws_ref[...] = jnp.zeros_like(dws_ref)
            dbs_ref[...] = jnp.zeros_like(dbs_ref)
            dg_ref[...] = jnp.zeros_like(dg_ref)
            db_ref[...] = jnp.zeros_like(db_ref)

        cv = v_ref[...]
        gam_v = g_ref[...]
        vn, xhat, r = _c_norm(cv, gam_v, b_ref[...])
        vnb = vn.astype(BF16)
        for c in range(nch):
            rows = slice(c * C_CHUNK, (c + 1) * C_CHUNK)
            for g in range(C_GROUPS):
                gs = slice(g * 64, (g + 1) * 64)
                cu = u_ref[rows, gs]
                dy = dy_ref[rows, gs]
                mixed = jnp.dot(ws_ref[g], vnb[rows, gs], preferred_element_type=F32) + bs_ref[:, gs]
                du_ref[rows, gs] = dy * mixed * _gelu_grad(cu)
                dmix = dy * _gelu(cu)
                dbs_ref[:, gs] += dmix
                dmb = dmix.astype(BF16)
                dws_ref[g] += lax.dot_general(dmb, vnb[rows, gs], NT, preferred_element_type=F32)
                dvn_s[rows, gs] = jnp.dot(wst_ref[g], dmb, preferred_element_type=F32)
        dvn = dvn_s[...]
        dg_ref[...] += jnp.sum(dvn * xhat, axis=0, keepdims=True)
        db_ref[...] += jnp.sum(dvn, axis=0, keepdims=True)
        dxh = dvn * gam_v
        dvg = r * (dxh - jnp.mean(dxh, axis=-1, keepdims=True) - xhat * jnp.mean(dxh * xhat, axis=-1, keepdims=True))
        dv_ref[...] = dvg * _gelu_grad(cv)

        @pl.when(step == nstep - 1)
        def _():
            dbs_ref[...] = _seg_sum(dbs_ref[...], _group_sum_matrix(256, True))

    vec = pl.BlockSpec((1, 256), lambda i: (0, 0))
    mat = pl.BlockSpec((C_GROUPS, C_CHUNK, C_CHUNK), lambda i: (0, 0, 0))
    bsp = pl.BlockSpec((C_CHUNK, 256), lambda i: (0, 0))
    tile = pl.BlockSpec((tm, 256), lambda i: (i, 0))
    return pl.pallas_call(
        body, name=name, grid=(nstep,),
        in_specs=[pl.BlockSpec((tm, 256), lambda i: (i, 5)), pl.BlockSpec((tm, 256), lambda i: (i, 6)),
                  pl.BlockSpec((tm, 256), lambda i: (i, 2)), vec, vec, mat, mat, bsp],
        out_specs=[tile, tile, mat, bsp, vec, vec],
        out_shape=[_sds((t, 256), F32), _sds((t, 256), F32), _sds((C_GROUPS, C_CHUNK, C_CHUNK), F32),
                   _sds((C_CHUNK, 256), F32), _sds((1, 256), F32), _sds((1, 256), F32)],
        scratch_shapes=[pltpu.VMEM((tm, 256), F32)],
        compiler_params=_cparams(ARB),
    )(proj, proj, dycat, gam, bet, ws, wst, bst)


FF_TC = 128
FF_NB = D_FF // FF_TC
FF_CH = 64
FF_HALO = 16


def _edge_taps(ref, first):
    if first:
        ext = ref[0:FF_CH + FF_HALO, :].astype(F32)
        body = slice(0, FF_CH)
    else:
        ext = ref[SEQ - FF_CH - FF_HALO:SEQ, :].astype(F32)
        body = slice(FF_HALO, FF_HALO + FF_CH)
    n = ext.shape[0]
    row = lax.broadcasted_iota(jnp.int32, ext.shape, 0)
    dn = pltpu.roll(ext, 1, 0)
    up = pltpu.roll(ext, n - 1, 0)
    if first:
        dn = jnp.where(row == 0, 0.0, dn)
    else:
        up = jnp.where(row == n - 1, 0.0, up)
    return dn[body], ext[body], up[body]


def _mid_taps(ref, r0):
    ext = ref[pl.ds(pl.multiple_of(r0 - FF_HALO, FF_HALO), FF_CH + 2 * FF_HALO), :].astype(F32)
    n = ext.shape[0]
    body = slice(FF_HALO, FF_HALO + FF_CH)
    return pltpu.roll(ext, 1, 0)[body], ext[body], pltpu.roll(ext, n - 1, 0)[body]


def _chunk_loop(step):
    step(0, lambda ref: _edge_taps(ref, True))

    def mid(i, carry):
        r0 = pl.multiple_of(i * FF_CH, FF_CH)
        step(r0, lambda ref: _mid_taps(ref, r0))
        return carry

    lax.fori_loop(1, SEQ // FF_CH - 1, mid, 0)
    step(SEQ - FF_CH, lambda ref: _edge_taps(ref, False))


def _conv3(taps, w_ref, b_ref):
    dn, md, up = taps
    return w_ref[0:1, :] * dn + w_ref[1:2, :] * md + w_ref[2:3, :] * up + b_ref[...]


def _ff_specs(order):
    def at(fn):
        return (lambda b, j: fn(b, j)) if order == "bj" else (lambda j, b: fn(b, j))
    hs = [pl.BlockSpec((None, SEQ, FF_TC), at(lambda b, j, o=o: (b, 0, j + o))) for o in (0, FF_NB)]
    ws = [pl.BlockSpec((3, FF_TC), at(lambda b, j, o=o: (0, j + o))) for o in (0, FF_NB)]
    bs = [pl.BlockSpec((1, FF_TC), at(lambda b, j, o=o: (0, j + o))) for o in (0, FF_NB)]
    return hs, ws, bs


def _conv_gate_fwd(h, cw, cb, *, name):
    t = h.shape[0]
    bl = t // SEQ

    def body(hg_ref, hu_ref, wg_ref, wu_ref, bg_ref, bu_ref, a_ref):
        def step(r0, taps):
            cg = _conv3(taps(hg_ref), wg_ref, bg_ref)
            cu = _conv3(taps(hu_ref), wu_ref, bu_ref)
            a_ref[pl.ds(r0, FF_CH), :] = (cg * _sigmoid(cg) * cu).astype(BF16)

        _chunk_loop(step)

    hs, ws, bs = _ff_specs("bj")
    h3 = h.reshape(bl, SEQ, 2 * D_FF)
    act = pl.pallas_call(
        body, name=name, grid=(bl, FF_NB), in_specs=hs + ws + bs,
        out_specs=pl.BlockSpec((None, SEQ, FF_TC), lambda b, j: (b, 0, j)),
        out_shape=_sds((bl, SEQ, D_FF), BF16),
        compiler_params=_cparams(PAR, PAR),
    )(h3, h3, cw, cw, cb, cb)
    return act.reshape(t, D_FF)


def _conv_gate_bwd(h, dact, cw, cb, *, name):
    t = h.shape[0]
    bl = t // SEQ

    def body(hg_ref, hu_ref, wg_ref, wu_ref, bg_ref, bu_ref, da_ref,
             dhg_ref, dhu_ref, dwg_ref, dwu_ref, dbg_ref, dbu_ref, dg_s, du_s):
        @pl.when(pl.program_id(1) == 0)
        def _():
            for ref in (dwg_ref, dwu_ref, dbg_ref, dbu_ref):
                ref[...] = jnp.zeros_like(ref)

        red = lambda x: jnp.sum(x, axis=0, keepdims=True)

        def pass1(r0, taps):
            tg, tu = taps(hg_ref), taps(hu_ref)
            cg = _conv3(tg, wg_ref, bg_ref)
            cu = _conv3(tu, wu_ref, bu_ref)
            da = da_ref[pl.ds(r0, FF_CH), :].astype(F32)
            sg = _sigmoid(cg)
            dcg = da * cu * (sg * (1.0 + cg * (1.0 - sg)))
            dcu = da * (cg * sg)
            dg_s[pl.ds(r0, FF_CH), :] = dcg
            du_s[pl.ds(r0, FF_CH), :] = dcu
            for d, tp, dw_ref, db_ref in ((dcg, tg, dwg_ref, dbg_ref), (dcu, tu, dwu_ref, dbu_ref)):
                for k in range(3):
                    dw_ref[k:k + 1, :] += red(d * tp[k])
                db_ref[...] += red(d)

        _chunk_loop(pass1)

        def pass2(r0, taps):
            for s, w_ref, o_ref in ((dg_s, wg_ref, dhg_ref), (du_s, wu_ref, dhu_ref)):
                dn, md, up = taps(s)
                o_ref[pl.ds(r0, FF_CH), :] = (w_ref[0:1, :] * up + w_ref[1:2, :] * md + w_ref[2:3, :] * dn).astype(BF16)

        _chunk_loop(pass2)

    hs, ws, bs = _ff_specs("jb")
    half = pl.BlockSpec((None, SEQ, FF_TC), lambda j, b: (b, 0, j))
    wsp = pl.BlockSpec((3, FF_TC), lambda j, b: (0, j))
    bsp = pl.BlockSpec((1, FF_TC), lambda j, b: (0, j))
    h3 = h.reshape(bl, SEQ, 2 * D_FF)
    dhg, dhu, dwg, dwu, dbg, dbu = pl.pallas_call(
        body, name=name, grid=(FF_NB, bl), in_specs=hs + ws + bs + [half],
        out_specs=[half, half, wsp, wsp, bsp, bsp],
        out_shape=[_sds((bl, SEQ, D_FF), BF16), _sds((bl, SEQ, D_FF), BF16), _sds((3, D_FF), F32), _sds((3, D_FF), F32),
                   _sds((1, D_FF), F32), _sds((1, D_FF), F32)],
        scratch_shapes=[pltpu.VMEM((SEQ, FF_TC), F32), pltpu.VMEM((SEQ, FF_TC), F32)],
        compiler_params=_cparams(PAR, ARB),
    )(h3, h3, cw, cw, cb, cb, dact.reshape(bl, SEQ, D_FF))
    return (dhg.reshape(t, D_FF), dhu.reshape(t, D_FF), jnp.concatenate([dwg, dwu], axis=1),
            jnp.concatenate([dbg, dbu], axis=1))


def _ple_fwd(x2, gain, wg, pe, pe_blk, wp, *, tm, tn, name):
    t, k = x2.shape
    n = wg.shape[1]

    def body(x_ref, g_ref, wg_ref, pe_ref, wp_ref, xr_ref, hn_ref, x3_ref, gt_ref, pp_ref, hn_s):
        @pl.when(pl.program_id(1) == 0)
        def _():
            x = x_ref[...]
            r = lax.rsqrt(jnp.mean(x * x, axis=-1, keepdims=True) + EPS)
            hn_s[...] = (x * r * g_ref[...]).astype(BF16)
            hn_ref[...] = hn_s[...]

        gate = _sigmoid(jnp.dot(hn_s[...], wg_ref[...], preferred_element_type=F32))
        pp = jnp.dot(pe_ref[...].astype(BF16), wp_ref[...], preferred_element_type=F32)
        gt_ref[...] = gate
        pp_ref[...] = pp
        x3_ref[...] = xr_ref[...] + pp * gate

    tile = pl.BlockSpec((tm, tn), lambda i, j: (i, j))
    return pl.pallas_call(
        body, name=name, grid=(t // tm, n // tn),
        in_specs=[pl.BlockSpec((tm, k), lambda i, j: (i, 0)), pl.BlockSpec((1, k), lambda i, j: (0, 0)),
                  pl.BlockSpec((k, tn), lambda i, j: (0, j)), pl.BlockSpec((tm, PLE_DIM), lambda i, j: (pe_blk + i, 0)),
                  pl.BlockSpec((PLE_DIM, tn), lambda i, j: (0, j)), tile],
        out_specs=[pl.BlockSpec((tm, k), lambda i, j: (i, 0)), tile, tile, tile],
        out_shape=[_sds((t, k), BF16), _sds((t, n), F32), _sds((t, n), F32), _sds((t, n), F32)],
        scratch_shapes=[pltpu.VMEM((tm, k), BF16)],
        compiler_params=_cparams(PAR, ARB),
    )(x2, gain, wg, pe, wp, x2)


def _ple_bwd_ew(dx3, gate, pp, *, tm, name):
    t, n = dx3.shape

    def body(d_ref, g_ref, p_ref, dz_ref, dpp_ref):
        d, g = d_ref[...], g_ref[...]
        dz_ref[...] = (d * p_ref[...] * g * (1.0 - g)).astype(BF16)
        dpp_ref[...] = (d * g).astype(BF16)

    spec = pl.BlockSpec((tm, n), lambda i: (i, 0))
    return pl.pallas_call(
        body, name=name, grid=(t // tm,), in_specs=[spec] * 3, out_specs=[spec] * 2,
        out_shape=[_sds((t, n), BF16)] * 2, compiler_params=_cparams(PAR),
    )(dx3, gate, pp)


def _loss_head(y, tgt, *, tm, name):
    t, d = y.shape

    def body(y_ref, t_ref, l_ref, dy_ref):
        @pl.when(pl.program_id(0) == 0)
        def _():
            l_ref[...] = jnp.zeros_like(l_ref)

        e = y_ref[...] - t_ref[...]
        dy_ref[...] = e * (1.0 / d)
        s = jnp.sum(jnp.sum(e * e, axis=1, keepdims=True), axis=0, keepdims=True)
        l_ref[...] += jnp.broadcast_to(s * (0.5 / d), (8, 128))

    spec = pl.BlockSpec((tm, d), lambda i: (i, 0))
    return pl.pallas_call(
        body, name=name, grid=(t // tm,), in_specs=[spec, spec],
        out_specs=[pl.BlockSpec((8, 128), lambda i: (0, 0)), spec],
        out_shape=[_sds((8, 128), F32), _sds((t, d), F32)], compiler_params=_cparams(ARB),
    )(y, tgt)


BIAS_PC = 8192


def _onehot(bucket_row):
    rows = lax.broadcasted_iota(jnp.int32, (REL_BUCKETS, bucket_row.shape[1]), 0)
    return (rows == bucket_row).astype(F32)


def _bias_lookup(table_t, bucket, *, name):
    h = table_t.shape[0]
    p = bucket.shape[1]

    def body(t_ref, b_ref, o_ref):
        bk = b_ref[...]
        val = jnp.dot(t_ref[...], _onehot(bk), precision=HI, preferred_element_type=F32)
        o_ref[...] = jnp.where(bk >= 0, val, NEG_INF)

    return pl.pallas_call(
        body, name=name, grid=(p // BIAS_PC,),
        in_specs=[pl.BlockSpec((h, REL_BUCKETS), lambda i: (0, 0)), pl.BlockSpec((1, BIAS_PC), lambda i: (0, i))],
        out_specs=pl.BlockSpec((h, BIAS_PC), lambda i: (0, i)), out_shape=_sds((h, p), F32),
        compiler_params=_cparams(PAR),
    )(table_t, bucket)


def _bucket_reduce(dbias, bucket, *, name):
    h, p = dbias.shape

    def body(d_ref, b_ref, o_ref):
        @pl.when(pl.program_id(0) == 0)
        def _():
            o_ref[...] = jnp.zeros_like(o_ref)

        o_ref[...] += lax.dot_general(d_ref[...], _onehot(b_ref[...]), NT, precision=HI, preferred_element_type=F32)

    return pl.pallas_call(
        body, name=name, grid=(p // BIAS_PC,),
        in_specs=[pl.BlockSpec((h, BIAS_PC), lambda i: (0, i)), pl.BlockSpec((1, BIAS_PC), lambda i: (0, i))],
        out_specs=pl.BlockSpec((h, REL_BUCKETS), lambda i: (0, 0)), out_shape=_sds((h, REL_BUCKETS), F32),
        compiler_params=_cparams(ARB),
    )(dbias, bucket)


def _adamw_math(w, g, m, v):
    m = ADAM_B1 * m + (1.0 - ADAM_B1) * g
    v = ADAM_B2 * v + (1.0 - ADAM_B2) * (g * g)
    m_hat = m / (1.0 - ADAM_B1 ** ADAM_STEP)
    v_hat = v / (1.0 - ADAM_B2 ** ADAM_STEP)
    delta = -ADAM_LR * (m_hat / (jnp.sqrt(v_hat) + ADAM_EPS) + ADAM_WD * w)
    return delta, m, v


def _adamw_reduce(parts, w, m, v, *, tr, name):
    nl = len(parts)
    rows, c = w.shape
    r = rows // nl
    nt = r // tr

    def body(*refs):
        p_refs = refs[:nl]
        w_ref, m_ref, v_ref, g_ref, d_ref, nm_ref, nv_ref = refs[nl:]
        for li, p_ref in enumerate(p_refs):
            @pl.when(pl.program_id(0) == li)
            def _(p_ref=p_ref):
                g = p_ref[0].astype(F32)
                for k in range(1, N_DEV):
                    g = g + p_ref[k].astype(F32)
                d, nm, nv = _adamw_math(w_ref[...], g, m_ref[...], v_ref[...])
                g_ref[...] = g
                d_ref[...] = d
                nm_ref[...] = nm
                nv_ref[...] = nv

    def part_map(li):
        return lambda l, i: (0, jnp.where(l == li, i, jnp.where(l < li, 0, nt - 1)), 0)

    spec = pl.BlockSpec((tr, c), lambda l, i: (l * nt + i, 0))
    return pl.pallas_call(
        body, name=name, grid=(nl, nt),
        in_specs=[pl.BlockSpec((N_DEV, tr, c), part_map(li)) for li in range(nl)] + [spec, spec, spec],
        out_specs=[spec] * 4, out_shape=[_sds((rows, c), F32)] * 4, compiler_params=_cparams(ARB, ARB),
    )(*parts, w, m, v)


def _adamw_plain(g, w, m, v, *, name):
    def body(g_ref, w_ref, m_ref, v_ref, d_ref, nm_ref, nv_ref):
        d, nm, nv = _adamw_math(w_ref[...], g_ref[...], m_ref[...], v_ref[...])
        d_ref[...] = d
        nm_ref[...] = nm
        nv_ref[...] = nv

    return pl.pallas_call(body, name=name, out_shape=[_sds(w.shape, F32)] * 3)(g, w, m, v)


def _mesh_pos():
    return lax.axis_index("x"), lax.axis_index("y"), lax.axis_index("c")


def _allgather_body(x_refs, out_refs, send_sems, recv_sems, local_sems, slot):
    x, y, c = _mesh_pos()
    me, sibling = (x, y, c), (x, y, 1 - c)
    chips = [(1 - x, y), (x, 1 - y), (1 - x, 1 - y)]
    waits = []
    for a, (x_ref, out_ref) in enumerate(zip(x_refs, out_refs)):
        def copy(k, block, to, src=None, out_ref=out_ref, a=a):
            return pltpu.make_async_remote_copy(
                src_ref=slot(out_ref, block) if src is None else src, dst_ref=slot(out_ref, block),
                send_sem=send_sems.at[a, k], recv_sem=recv_sems.at[a, k], device_id=to, device_id_type=MESH)

        mine = pltpu.make_async_copy(x_ref, slot(out_ref, me), local_sems.at[a])
        mine.start()
        first = [copy(0, me, sibling, src=x_ref)]
        first += [copy(1 + j, me, (*chip, c), src=x_ref) for j, chip in enumerate(chips)]
        for cp in first:
            cp.start()
        waits.append((copy, mine, first))
    sends = []
    for copy, mine, first in waits:
        passed = [copy(4 + j, (*chip, c), sibling) for j, chip in enumerate(chips)]
        for j, chip in enumerate(chips):
            copy(1 + j, (*chip, c), me).wait_recv()
            passed[j].start()
        sends.append(passed)
    for (copy, mine, first), passed in zip(waits, sends):
        copy(0, sibling, me).wait_recv()
        for j, chip in enumerate(chips):
            copy(4 + j, (*chip, 1 - c), me).wait_recv()
        for cp in first + passed:
            cp.wait_send()
        mine.wait()


PEER_FLIPS = ((0, 0, 1), (1, 0, 0), (0, 1, 0), (1, 1, 0), (1, 0, 1), (0, 1, 1), (1, 1, 1))
HBM_SPEC = pl.BlockSpec(memory_space=pltpu.HBM)
SEM_SPEC = pl.BlockSpec(memory_space=pltpu.SEMAPHORE)
DATAFLOW = pltpu.SideEffectType.DATAFLOW_SIDE_EFFECTING


def _peer_copies(x_refs, land_refs, send_sem, recv_sem, scatter):
    x, y, c = _mesh_pos()
    me = 4 * x + 2 * y + c
    copies = []
    for x_ref, land_ref in zip(x_refs, land_refs):
        for fx, fy, fc in PEER_FLIPS:
            px, py, pc = x ^ fx, y ^ fy, c ^ fc
            src = x_ref.at[4 * px + 2 * py + pc] if scatter else x_ref
            copies.append(pltpu.make_async_remote_copy(
                src_ref=src, dst_ref=land_ref.at[me], send_sem=send_sem, recv_sem=recv_sem,
                device_id=(px, py, pc), device_id_type=MESH))
    return copies


def _exchange_start(groups, after, *, scatter, name):
    xs = [x for grp in groups for x in grp]
    na = len(xs)
    ngrp = len(groups)
    firsts = np.cumsum([0] + [len(grp) for grp in groups])
    land_shapes = [x.shape if scatter else (N_DEV,) + x.shape for x in xs]
    extra = [] if after is None else [after]

    def body(*refs):
        x_refs, land_refs = refs[:na], refs[na:2 * na]
        sems = refs[2 * na + len(extra):2 * na + len(extra) + 2 * ngrp]
        token, zeros, local_sem = refs[-3], refs[-2], refs[-1]
        xm, ym, cm = _mesh_pos()
        me = 4 * xm + 2 * ym + cm
        zeros[...] = jnp.zeros_like(zeros)
        locals_ = [pltpu.make_async_copy(zeros, token, local_sem)]
        locals_ += [pltpu.make_async_copy(x_ref.at[me] if scatter else x_ref, land_ref.at[me], local_sem)
                    for x_ref, land_ref in zip(x_refs, land_refs)]
        for own in locals_:
            own.start()
            own.wait()
        for gi in range(ngrp):
            lo, hi = firsts[gi], firsts[gi + 1]
            for cp in _peer_copies(x_refs[lo:hi], land_refs[lo:hi], sems[2 * gi], sems[2 * gi + 1], scatter):
                cp.start()

    sem_shapes = [pltpu.SemaphoreType.DMA(())] * (2 * ngrp)
    lands = [pltpu.with_memory_space_constraint(lax.empty(s, x.dtype), pltpu.HBM) for s, x in zip(land_shapes, xs)]
    outs = pl.pallas_call(
        body, name=name,
        in_specs=[HBM_SPEC] * (2 * na) + [pl.BlockSpec(memory_space=pl.ANY)] * len(extra),
        out_specs=[SEM_SPEC] * (2 * ngrp) + [HBM_SPEC] * (2 * na + 1),
        out_shape=sem_shapes + [pltpu.HBM(x.shape, x.dtype) for x in xs]
        + [pltpu.HBM(s, x.dtype) for s, x in zip(land_shapes, xs)] + [pltpu.HBM((8, 128), F32)],
        input_output_aliases={i: 2 * ngrp + i for i in range(2 * na)},
        scratch_shapes=[pltpu.VMEM((8, 128), F32), pltpu.SemaphoreType.DMA],
        compiler_params=pltpu.CompilerParams(has_side_effects=DATAFLOW),
    )(*[pltpu.with_memory_space_constraint(x, pltpu.HBM) for x in xs], *lands, *extra)
    sems, thru, token = outs[:2 * ngrp], outs[2 * ngrp:2 * ngrp + 2 * na], outs[-1]
    handles = []
    for gi in range(ngrp):
        lo, hi = firsts[gi], firsts[gi + 1]
        handles.append((sems[2 * gi], sems[2 * gi + 1], thru[lo:hi], thru[na + lo:na + hi]))
    return handles, token


def _exchange_wait(handle, after, *, scatter, name):
    send_sems, recv_sems, x_thru, land_thru = handle
    na = len(x_thru)

    def body(*refs):
        x_refs, land_refs = refs[:na], refs[na:2 * na]
        send_ref, recv_ref = refs[2 * na], refs[2 * na + 1]
        for cp in _peer_copies(x_refs, land_refs, send_ref, recv_ref, scatter):
            cp.wait_send()
            cp.wait_recv()

    outs = pl.pallas_call(
        body, name=name,
        in_specs=[HBM_SPEC] * (2 * na) + [SEM_SPEC, SEM_SPEC, pl.BlockSpec(memory_space=pl.ANY)],
        out_specs=[HBM_SPEC] * (2 * na),
        out_shape=[pltpu.HBM(a.shape, a.dtype) for a in list(x_thru) + list(land_thru)],
        input_output_aliases={i: i for i in range(2 * na)},
        compiler_params=pltpu.CompilerParams(has_side_effects=DATAFLOW),
    )(*x_thru, *land_thru, send_sems, recv_sems, after)
    return outs[na:]


def _sc_exchange(xs, *, scatter, collective_id, name):
    na = len(xs)
    land_shapes = [x.shape if scatter else (N_DEV,) + x.shape for x in xs]

    def body(*refs):
        x_refs, land_refs = refs[:na], refs[na:2 * na]
        send_sem, recv_sem, local_sem = refs[2 * na:]
        x, y, c = _mesh_pos()
        me = 4 * x + 2 * y + c
        barrier = pltpu.get_barrier_semaphore()
        for fx, fy, fc in PEER_FLIPS:
            pl.semaphore_signal(barrier, inc=1, device_id=(x ^ fx, y ^ fy, c ^ fc), device_id_type=MESH)
        pl.semaphore_wait(barrier, len(PEER_FLIPS))
        for x_ref, land_ref in zip(x_refs, land_refs):
            own = pltpu.make_async_copy(x_ref.at[me] if scatter else x_ref, land_ref.at[me], local_sem)
            own.start()
            own.wait()
        copies = _peer_copies(x_refs, land_refs, send_sem, recv_sem, scatter)
        for cp in copies:
            cp.start()
        for cp in copies:
            cp.wait()

    return pl.kernel(
        body, name=name, out_type=[_sds(s, x.dtype) for s, x in zip(land_shapes, xs)],
        mesh=plsc.ScalarSubcoreMesh(axis_name="sequencer", num_cores=1),
        scratch_types=[pltpu.SemaphoreType.DMA, pltpu.SemaphoreType.DMA, pltpu.SemaphoreType.DMA],
        compiler_params=pltpu.CompilerParams(collective_id=collective_id),
    )(*xs)


def _allgather_vmem(x, *, reduce, name):
    r, c = x.shape

    def body(x_ref, out_ref, *rest):
        if reduce:
            gath, send_sems, recv_sems, local_sems = rest
        else:
            send_sems, recv_sems, local_sems = rest
            gath = out_ref
        _allgather_body([x_ref], [gath], send_sems, recv_sems, local_sems,
                        lambda ref, pos: ref.at[pl.ds((4 * pos[0] + 2 * pos[1] + pos[2]) * r, r), :])
        if reduce:
            acc = gath[0:r, :]
            for k in range(1, N_DEV):
                acc = acc + gath[k * r:(k + 1) * r, :]
            out_ref[...] = acc

    vm = pl.BlockSpec(memory_space=pltpu.VMEM)
    scratch = [pltpu.SemaphoreType.DMA((1, 7)), pltpu.SemaphoreType.DMA((1, 7)), pltpu.SemaphoreType.DMA((1,))]
    if reduce:
        scratch = [pltpu.VMEM((N_DEV * r, c), x.dtype)] + scratch
    return pl.pallas_call(
        body, name=name, in_specs=[vm], out_specs=vm,
        out_shape=_sds((r, c) if reduce else (N_DEV * r, c), x.dtype), scratch_shapes=scratch,
    )(x)


def _t5_bucket(rel):
    nb = REL_BUCKETS // 2
    ret = jnp.where(rel > 0, nb, 0)
    n = jnp.abs(rel)
    max_exact = nb // 2
    nf = jnp.maximum(n, 1).astype(F32)
    large = max_exact + (jnp.log(nf / max_exact) / math.log(REL_MAX_DIST / max_exact)
                         * (nb - max_exact)).astype(jnp.int32)
    large = jnp.minimum(large, nb - 1)
    return ret + jnp.where(n < max_exact, n, large)


def _band_pattern(block, radius, dil):
    kw = block + 2 * radius
    rel = jnp.arange(kw)[None, :] - radius - jnp.arange(block)[:, None]
    return jnp.where(jnp.abs(rel) <= radius, _t5_bucket(rel * dil), -1).astype(jnp.int32).reshape(1, block * kw)


def _rope_tables():
    lane = np.arange(64)
    seg, j = lane // 32, lane % 32
    inv = ROPE_THETA ** (-jnp.arange(0, 32, 2, dtype=F32) / 32)
    tpos = jnp.arange(SEQ)
    pos = jnp.where(jnp.asarray(seg)[None, :] == 0, (tpos // GRID_W)[:, None], (tpos % GRID_W)[:, None])
    ang = pos.astype(F32) * inv[jnp.asarray(j % 16)][None, :]
    cos = jnp.cos(ang)
    sins = jnp.where(jnp.asarray(j)[None, :] < 16, -jnp.sin(ang), jnp.sin(ang))
    return jnp.tile(cos, (1, 4)), jnp.tile(sins, (1, 4))


A_Q, A_K, A_V = (256, 0), (256, 1), (256, 2)
B_Q, B_K, B_V = (256, 0), (128, 2), (128, 3)
A_HEADS = dict(rad=A_RADIUS, nh=4, nkv=4)
B_HEADS = dict(rad=SWA_RADIUS, nh=4, nkv=2)


def _pin(arr, token):
    return arr if token is None else arr + token[0:1, 0:1]


def _local_step(x, pe, tgt, rel_bias, wts, matmul_weights, grads_ready):
    t = x.shape[0]
    bl = t // SEQ
    cos, sins = _rope_tables()
    blocks_a = [min(BAND_BLOCK, SEQ // d) for d in DILATIONS]
    pats_a = [_band_pattern(blk, A_RADIUS, d) for blk, d in zip(blocks_a, DILATIONS)]
    pat_b = _band_pattern(BAND_BLOCK, SWA_RADIUS, 1)
    table_t = rel_bias.T
    bias_a = [_bias_lookup(table_t[:4], pt, name=f"bias_a{ci}").reshape(4, blk, blk + 2 * A_RADIUS)
              for ci, (pt, blk) in enumerate(zip(pats_a, blocks_a))]
    bias_b = _bias_lookup(table_t[4:], pat_b, name="bias_b").reshape(4, BAND_BLOCK, BAND_BLOCK + 2 * SWA_RADIUS)
    nat4 = lambda a: a.reshape(bl, 1, SEQ, a.shape[-1])

    saved = []
    for li in range(DEPTH):
        w = dict(wts[li])
        w.update(matmul_weights(li, "in", x)[0])
        hn0, proj = _norm_mm((x,), w["g_mix"], w["w_in"], None, tm=1024, tn=1152, name="mix_in_fwd")
        more, started = matmul_weights(li, "rest", proj)
        w.update(more)
        w["out_gain"] = _pin(w["out_gain"], started)
        qa1, qa4, qa16, qb, qd = _qkprep_fwd(proj, w["qk_gains"], cos, sins, tm=512, name="qkprep_fwd")
        qa = (nat4(qa1), qa4, qa16)
        oa, la = [], []
        for ci in range(3):
            o, l = _band_fwd(qa[ci], A_Q, A_K, A_V, bias_a[ci], None, name=f"band_a{ci}_fwd", **A_HEADS)
            oa.append(o)
            la.append(l)
        oa[0], la[0] = oa[0].reshape(t, 256), la[0].reshape(t, 256)
        ya, lse_a = _combine_a(oa, la, tm=512, name="combine_a")
        yb, lse_b = _band_fwd(nat4(qb), B_Q, B_K, B_V, bias_b, w["sink_t"], name="band_b_fwd", **B_HEADS)
        yb = yb.reshape(t, 256)
        yc = _c_fwd(proj, w["c_g"], w["c_b"], w["c_ws"], w["c_bst"], tm=512, name="c_fwd")
        yd, lse_d = _dense_fwd(qd, tq=128, name="dense_fwd")
        mixed, x1 = _norm_mm((ya, yb, yc, yd), w["out_gain"], w["w_out"], x, tm=1024, tn=1024, name="mix_out_fwd")
        hn1, h = _norm_mm((x1,), w["g_ffn"], w["w_up"], None, tm=1024, tn=1408, name="ffn_up_fwd", out_dtype=BF16)
        act = _conv_gate_fwd(h, w["conv_w"], w["conv_b"], name="conv_gate_fwd")
        x2 = _mm(act, w["w_down"], "nn", x1, tm=1024, tn=1024, out_dtype=F32, name="ffn_down_fwd")
        hn2, x3, gate, pp = _ple_fwd(x2, w["g_ple"], w["w_gate"], pe, li * (t // 1024), w["w_proj"], tm=1024, tn=512,
                                     name="ple_fwd")
        saved.append(dict(w=w, x0=x, hn0=hn0, proj=proj, qa=qa, qb=qb, qd=qd, ya=ya, lse_a=lse_a, yb=yb, lse_b=lse_b,
                          yc=yc, yd=yd, lse_d=lse_d, mixed=mixed, x1=x1, hn1=hn1, h=h, act=act, x2=x2, hn2=hn2,
                          gate=gate, pp=pp))
        x = x3

    loss_tile, dx = _loss_head(x, tgt, tm=512, name="loss_head")
    grads = [None] * DEPTH
    d_table_a = jnp.zeros((4, REL_BUCKETS), F32)
    d_table_b = jnp.zeros((4, REL_BUCKETS), F32)
    token = None
    for li in reversed(range(DEPTH)):
        s = saved[li]
        w = s["w"]
        g = {}
        w["g_ple"] = _pin(w["g_ple"], token)
        dz, dpp = _ple_bwd_ew(dx, s["gate"], s["pp"], tm=512, name="ple_bwd_ew")
        g["w_gate"] = _mm(s["hn2"], dz, "tn", None, tm=1024, tn=512, out_dtype=BF16, name="dw_gate")
        g["w_proj"] = _mm(pe, dpp, "tn", None, tm=256, tn=1024, out_dtype=BF16, name="dw_proj", a_rows=(li, t))
        dx2, dx2b, g["g_ple"] = _mm_bt_normbwd((dz,), w["w_gate"], (s["x2"],), w["g_ple"], dx, tm=1024, tn=1024,
                                               name="ple_bwd", emit_bf16=True)
        g["w_down"] = _mm(s["act"], dx2b, "tn", None, tm=1408, tn=512, out_dtype=BF16, name="dw_down")
        dact = _mm(dx2b, w["w_down"], "nt", None, tm=1024, tn=1408, out_dtype=BF16, name="ffn_down_bwd")
        dhg, dhu, g["conv_w"], g["conv_b"] = _conv_gate_bwd(s["h"], dact, w["conv_w"], w["conv_b"], name="conv_gate_bwd")
        g["w_up"] = jnp.concatenate(
            [_mm(s["hn1"], dhalf, "tn", None, tm=1024, tn=1408, out_dtype=BF16, name=f"dw_up_{nm}")
             for nm, dhalf in (("gate", dhg), ("up", dhu))], axis=1)
        g_ffn = _pin(w["g_ffn"], grads_ready(li, "mid", g))
        dx1, dx1b, g["g_ffn"] = _mm_bt_normbwd((dhg, dhu), w["w_up"], (s["x1"],), g_ffn, dx2, tm=1024, tn=1408,
                                               name="ffn_up_bwd", emit_bf16=True)
        g["w_out"] = _mm(s["mixed"], dx1b, "tn", None, tm=1024, tn=512, out_dtype=BF16, name="dw_out")
        dycat, g["out_gain"] = _mm_bt_normbwd((dx1b,), w["w_out"], (s["ya"], s["yb"], s["yc"], s["yd"]), w["out_gain"],
                                              None, tm=1024, tn=1024, name="mix_out_bwd")
        dy_r, lse_r, dl_a, dl_b, dl_d = _deltas(dycat, s["ya"], s["yb"], s["yd"], s["lse_a"], tm=512, name="deltas")
        dy_a = (nat4(dycat),) + tuple(dy_r)
        lse_a = (nat4(s["lse_a"]),) + tuple(lse_r)
        dl_a = (nat4(dl_a[0]),) + tuple(dl_a[1:])
        da = []
        for ci in range(3):
            dq, dk, dv, dbias = _band_bwd(s["qa"][ci], A_Q, A_K, A_V, bias_a[ci], None, dy_a[ci], 0, lse_a[ci],
                                          dl_a[ci], name=f"band_a{ci}_bwd", **A_HEADS)
            if ci == 0:
                dq, dk, dv = (a.reshape(t, 256) for a in (dq, dk, dv))
            da.append((dq, dk, dv))
            d_table_a = d_table_a + _bucket_reduce(dbias.reshape(4, -1), pats_a[ci], name=f"bucket_a{ci}")
        dqb, dkb, dvb, dbias_b, dsink = _band_bwd(nat4(s["qb"]), B_Q, B_K, B_V, bias_b, w["sink_t"], nat4(dycat), 1,
                                                  nat4(s["lse_b"]), nat4(dl_b), name="band_b_bwd", **B_HEADS)
        d_table_b = d_table_b + _bucket_reduce(dbias_b.reshape(4, -1), pat_b, name="bucket_b")
        g["sink"] = dsink[:, 0, 0]
        dd = _dense_bwd(s["qd"], dycat, s["lse_d"], dl_d, tq=128, name="dense_bwd")
        dcu, dcv, g["c_ws"], dbs, g["c_g"], g["c_b"] = _c_bwd(s["proj"], dycat, w["c_g"], w["c_b"], w["c_ws"],
                                                               w["c_wst"], w["c_bst"], tm=512, name="c_bwd")
        g["c_bs"] = dbs[:, ::64].T
        db = (dqb.reshape(t, 256), dkb.reshape(t, 128), dvb.reshape(t, 128))
        dproj, dgains = _qkprep_bwd(s["proj"], da, db, dd, dcu, dcv, w["qk_gains"], cos, sins, tm=512, name="qkprep_bwd")
        g["qk_gain"] = dgains[:6, :64].reshape(3, 2, HEAD_DIM)
        g["w_in"] = _mm(s["hn0"], dproj, "tn", None, tm=1024, tn=1152, out_dtype=BF16, name="dw_in")
        dx, g["g_mix"] = _mm_bt_normbwd((dproj,), w["w_in"], (s["x0"],), w["g_mix"], dx1, tm=1024, tn=1152,
                                        name="mix_in_bwd")
        grads[li] = g
        token = grads_ready(li, "end", g)
    d_rel_bias = jnp.concatenate([d_table_a, d_table_b], axis=0).T
    return loss_tile[0, 0], dx, grads, d_rel_bias


WEIGHT_NAMES = ("rel_bias", "ln_mix_g", "w_in", "qk_gain", "sink", "c_norm_g", "c_norm_b", "c_ws", "c_bs", "out_gain",
                "w_out", "ln_ffn_g", "w_up", "conv_w", "conv_b", "w_down", "ln_ple_g", "w_ple_gate", "w_ple_proj")
COL_SHARDED = ("w_in", "w_up", "w_ple_proj")
ROW_SHARDED = ("w_out", "w_down", "w_ple_gate")
SMALL_SHARDED = ("conv_w", "out_gain")
REPLICATED = tuple(n for n in WEIGHT_NAMES if n not in COL_SHARDED + ROW_SHARDED + SMALL_SHARDED)
LOCAL_GRAD_KEY = {"ln_mix_g": "g_mix", "ln_ffn_g": "g_ffn", "ln_ple_g": "g_ple", "c_norm_g": "c_g", "c_norm_b": "c_b",
                  "w_ple_gate": "w_gate", "w_ple_proj": "w_proj"}


def _full_from_gathered(name, gathered):
    _, r, c = gathered.shape
    if name in ROW_SHARDED:
        return gathered.reshape(N_DEV * r, c)
    return jnp.transpose(gathered, (1, 0, 2)).reshape(r, N_DEV * c)


def _slots_from_full(name, full):
    rows, cols = full.shape
    if name in ROW_SHARDED:
        return full.reshape(N_DEV, rows // N_DEV, cols)
    return jnp.transpose(full.reshape(rows, N_DEV, cols // N_DEV), (1, 0, 2))


def _piece_rows(shape):
    return -(-int(np.prod(shape)) // 1024) * 8


def _pack_rows(arrays):
    pieces = []
    for a in arrays:
        n, rows = int(np.prod(a.shape)), _piece_rows(a.shape)
        flat = a.astype(F32).reshape(-1)
        if n != rows * LANES:
            flat = jnp.pad(flat, (0, rows * LANES - n))
        pieces.append(flat.reshape(rows, LANES))
    return jnp.concatenate(pieces, axis=0)


def _unpack_rows(packed, shapes):
    out, off = [], 0
    for shp in shapes:
        n, rows = int(np.prod(shp)), _piece_rows(shp)
        piece = packed[off:off + rows]
        out.append((piece if n == rows * LANES else piece.reshape(-1)[:n]).reshape(shp))
        off += rows
    return out


def kernel(x, p, rel_bias, ln_mix_g, w_in, qk_gain, sink, c_norm_g, c_norm_b, c_ws, c_bs, out_gain, w_out, ln_ffn_g, w_up, conv_w, conv_b, w_down, ln_ple_g, w_ple_gate, w_ple_proj, loss_target, m_rel_bias, m_ln_mix_g, m_w_in, m_qk_gain, m_sink, m_c_norm_g, m_c_norm_b, m_c_ws, m_c_bs, m_out_gain, m_w_out, m_ln_ffn_g, m_w_up, m_conv_w, m_conv_b, m_w_down, m_ln_ple_g, m_w_ple_gate, m_w_ple_proj, v_rel_bias, v_ln_mix_g, v_w_in, v_qk_gain, v_sink, v_c_norm_g, v_c_norm_b, v_c_ws, v_c_bs, v_out_gain, v_w_out, v_ln_ffn_g, v_w_up, v_conv_w, v_conv_b, v_w_down, v_ln_ple_g, v_w_ple_gate, v_w_ple_proj):
    env = dict(locals())
    wt = {n: env[n] for n in WEIGHT_NAMES}
    mom_m = {n: env["m_" + n] for n in WEIGHT_NAMES}
    mom_v = {n: env["v_" + n] for n in WEIGHT_NAMES}
    bl = x.shape[0]
    t = bl * SEQ
    me = 4 * lax.axis_index("x") + 2 * lax.axis_index("y") + lax.axis_index("c")

    big = COL_SHARDED + ROW_SHARDED
    full = {}
    small_shapes = [wt[n].shape for n in SMALL_SHARDED]
    small = _allgather_vmem(_pack_rows([wt[n] for n in SMALL_SHARDED]), reduce=False, name="gather_small")
    small = small.reshape(N_DEV, -1)
    off = 0
    for n, shp in zip(SMALL_SHARDED, small_shapes):
        cnt = int(np.prod(shp))
        g = small[:, off:off + cnt].reshape((N_DEV,) + tuple(shp))
        full[n] = jnp.transpose(g, (1, 2, 0, 3)).reshape(shp[0], shp[1], N_DEV * shp[2])
        off += _piece_rows(shp) * LANES

    def head_gain(li, a, b, reps):
        g = jnp.tile(qk_gain[li, a, b], reps)
        return jnp.pad(g, (0, 256 - g.shape[0]))

    wts = []
    for li in range(DEPTH):
        rows = [head_gain(li, 0, 0, 4), head_gain(li, 0, 1, 4), head_gain(li, 1, 0, 4), head_gain(li, 1, 1, 2),
                head_gain(li, 2, 0, 4), head_gain(li, 2, 1, 2), jnp.zeros((256,), F32), jnp.zeros((256,), F32)]
        wts.append(dict(
            g_mix=ln_mix_g[li].reshape(1, -1), qk_gains=jnp.stack(rows),
            sink_t=jnp.broadcast_to(sink[li][:, None, None], (4, 8, 128)),
            c_g=c_norm_g[li].reshape(1, -1), c_b=c_norm_b[li].reshape(1, -1), c_ws=c_ws[li].astype(BF16),
            c_wst=jnp.transpose(c_ws[li], (0, 2, 1)).astype(BF16), c_bst=jnp.repeat(c_bs[li].T, 64, axis=1),
            out_gain=full["out_gain"][li].reshape(1, -1), g_ffn=ln_ffn_g[li].reshape(1, -1),
            conv_w=full["conv_w"][li], conv_b=conv_b[li].reshape(1, -1), g_ple=ln_ple_g[li].reshape(1, -1)))

    local_key = {"w_ple_gate": "w_gate", "w_ple_proj": "w_proj"}

    gather_names = {"in": ("w_in",), "rest": tuple(n for n in big if n != "w_in")}
    gathered = {}
    for cid, (li, part) in enumerate(((0, "in"), (0, "rest"), (1, "in"), (1, "rest"))):
        lands = _sc_exchange([wt[n][li].astype(BF16) for n in gather_names[part]], scatter=False,
                             collective_id=cid, name=f"gather_{li}_{part}")
        gathered[li, part] = dict(zip(gather_names[part], lands))

    def matmul_weights(li, part, after):
        return {local_key.get(n, n): _full_from_gathered(n, g) for n, g in gathered[li, part].items()}, None

    mid_names = ("w_ple_gate", "w_ple_proj", "w_down", "w_up")
    end_names = ("w_out", "w_in")
    landed = {}

    def start_exchange(li, names, g, tag, cid):
        slots = [_slots_from_full(n, g[local_key.get(n, n)]) for n in names]
        lands = _sc_exchange(slots, scatter=True, collective_id=cid, name=f"grads_{li}_{tag}")
        landed.update({(n, li): land for n, land in zip(names, lands)})

    def grads_ready(li, stage, g):
        if li == 0:
            start_exchange(li, mid_names if stage == "mid" else end_names, g, stage, 5 if stage == "mid" else 6)
        elif stage == "end":
            start_exchange(li, mid_names + end_names, g, stage, 4)
        return None

    loss_part, dx, grads, d_rel_bias = _local_step(
        x.reshape(t, D_MODEL), p.reshape(DEPTH * t, PLE_DIM), loss_target.reshape(t, D_MODEL), rel_bias, wts,
        matmul_weights, grads_ready)
    loss = lax.psum(loss_part, ("x", "y", "c"))

    def local_grad(n):
        if n == "rel_bias":
            return d_rel_bias
        key = LOCAL_GRAD_KEY.get(n, n)
        return jnp.stack([grads[li][key].reshape(wt[n].shape[1:]) if n in REPLICATED else grads[li][key]
                          for li in range(DEPTH)])

    out_g, out_d, out_m, out_v = {}, {}, {}, {}
    for n in big:
        shp = wt[n].shape
        two_d = lambda a: a.reshape(-1, shp[-1])
        res = _adamw_reduce([landed[n, li] for li in range(DEPTH)], two_d(wt[n]), two_d(mom_m[n]), two_d(mom_v[n]),
                            tr=32 if n == "w_down" else 128, name="adamw_" + n)
        out_g[n], out_d[n], out_m[n], out_v[n] = [r.reshape(shp) for r in res]

    small_names = REPLICATED + SMALL_SHARDED
    small_full_shapes = [wt[n].shape if n in REPLICATED else full[n].shape for n in small_names]
    reduced = _allgather_vmem(_pack_rows([local_grad(n) for n in small_names]), reduce=True, name="allreduce_small")
    reduced = dict(zip(small_names, _unpack_rows(reduced, small_full_shapes)))
    rep_shapes = [wt[n].shape for n in REPLICATED]
    upd = _adamw_plain(_pack_rows([reduced[n] for n in REPLICATED]), _pack_rows([wt[n] for n in REPLICATED]),
                       _pack_rows([mom_m[n] for n in REPLICATED]), _pack_rows([mom_v[n] for n in REPLICATED]),
                       name="adamw_replicated")
    for dst, packed in zip((out_d, out_m, out_v), upd):
        dst.update(zip(REPLICATED, _unpack_rows(packed, rep_shapes)))
    for n in REPLICATED:
        out_g[n] = reduced[n]
    for n in SMALL_SHARDED:
        shp = wt[n].shape
        g = reduced[n].reshape(shp[0], shp[1], N_DEV, shp[2])
        g = lax.dynamic_index_in_dim(g, me, axis=2, keepdims=False)
        two_d = lambda a: a.reshape(-1, shp[-1])
        res = _adamw_plain(two_d(g), two_d(wt[n]), two_d(mom_m[n]), two_d(mom_v[n]), name="adamw_" + n)
        out_g[n] = g
        out_d[n], out_m[n], out_v[n] = [r.reshape(shp) for r in res]

    return (loss, dx.reshape(bl, SEQ, D_MODEL), *[out_g[n] for n in WEIGHT_NAMES], *[out_d[n] for n in WEIGHT_NAMES],
            *[out_m[n] for n in WEIGHT_NAMES], *[out_v[n] for n in WEIGHT_NAMES])
```

```python
import math

import jax
import jax.numpy as jnp
import numpy as np
from jax import lax
from jax.experimental import pallas as pl
from jax.experimental.pallas import tpu as pltpu
from jax.experimental.pallas import tpu_sc as plsc

F32 = jnp.float32
BF16 = jnp.bfloat16
HI = lax.Precision.HIGHEST

N_DEV = 8
D_MODEL = 1024
SEQ = 2048
DEPTH = 2
HEAD_DIM = 64
IN_WIDTH = 2304
D_FF = 2816
PLE_DIM = 256
C_CHUNK = 128
C_GROUPS = 4
DILATED_CFGS = ((128, 1), (512, 4), (2048, 16))
DILATIONS = tuple(d for _, d in DILATED_CFGS)
A_RADIUS = 64
SWA_RADIUS = 128
BAND_BLOCK = 256
GRID_W = 64
ROPE_THETA = 10000.0
REL_BUCKETS = 32
REL_MAX_DIST = 1024
EPS = 1e-6
NEG_INF = -1e30
ATTN_SCALE = HEAD_DIM ** -0.5
LANES = 128

ADAM_LR = 0.001
ADAM_B1 = 0.9
ADAM_B2 = 0.999
ADAM_EPS = 1e-08
ADAM_WD = 0.01
ADAM_STEP = 10

MESH = pl.DeviceIdType.MESH
NT = (((1,), (1,)), ((), ()))
TN = (((0,), (0,)), ((), ()))
ARB = "arbitrary"
PAR = "parallel"


def _cparams(*sem):
    return pltpu.CompilerParams(dimension_semantics=tuple(sem))


def _sds(shape, dtype):
    return jax.ShapeDtypeStruct(tuple(shape), dtype)


def _group_sum_matrix(n, same_group):
    r = lax.broadcasted_iota(jnp.int32, (n, n), 0)
    c = lax.broadcasted_iota(jnp.int32, (n, n), 1)
    if same_group:
        return ((r >> 6) == (c >> 6)).astype(F32)
    return ((r & 63) == (c & 63)).astype(F32)


def _seg_sum(x, e):
    return jnp.dot(x, e, precision=HI, preferred_element_type=F32)


def _gelu(x):
    c = math.sqrt(2.0 / math.pi)
    return 0.5 * x * (1.0 + jnp.tanh(c * (x + 0.044715 * (x * x * x))))


def _gelu_grad(x):
    c = math.sqrt(2.0 / math.pi)
    t = jnp.tanh(c * (x + 0.044715 * (x * x * x)))
    return 0.5 * (1.0 + t) + 0.5 * x * (1.0 - t * t) * c * (1.0 + 3.0 * 0.044715 * (x * x))


def _sigmoid(x):
    return 1.0 / (1.0 + jnp.exp(-x))


def _scatter_cols(scratch, first, val):
    for c in range(val.shape[1] // LANES):
        scratch[first + c] = val[:, c * LANES:(c + 1) * LANES]


def _gather_cols(scratch, first, ncol):
    return jnp.concatenate([scratch[first + c] for c in range(ncol)], axis=1)


def _read_residue(scratch, first, ncol, r, d):
    n = scratch.shape[1] // d
    return jnp.concatenate([scratch.at[first + c][pl.ds(r, n, stride=d), :] for c in range(ncol)], axis=1)


def _write_residue(scratch, first, r, d, val):
    n = scratch.shape[1] // d
    for c in range(val.shape[1] // LANES):
        scratch.at[first + c][pl.ds(r, n, stride=d), :] = val[:, c * LANES:(c + 1) * LANES]


def _norm_mm(xs, gain, w, res, *, tm, tn, name, out_dtype=F32):
    t = xs[0].shape[0]
    k = sum(x.shape[1] for x in xs)
    n = w.shape[1]
    ng = len(xs)
    has_res = res is not None

    def body(*refs):
        x_refs = refs[:ng]
        g_ref, w_ref = refs[ng], refs[ng + 1]
        res_ref = refs[ng + 2] if has_res else None
        hn_ref, o_ref, hn_s = refs[ng + 2 + has_res:]

        @pl.when(pl.program_id(1) == 0)
        def _():
            off = 0
            for xr in x_refs:
                x = xr[...]
                wd = x.shape[1]
                r = lax.rsqrt(jnp.mean(x * x, axis=-1, keepdims=True) + EPS)
                hn_s[:, off:off + wd] = (x * r * g_ref[:, off:off + wd]).astype(BF16)
                off += wd
            hn_ref[...] = hn_s[...]

        acc = jnp.dot(hn_s[...], w_ref[...], preferred_element_type=F32)
        if has_res:
            acc = acc + res_ref[...]
        o_ref[...] = acc.astype(out_dtype)

    in_specs = [pl.BlockSpec((tm, x.shape[1]), lambda i, j: (i, 0)) for x in xs]
    in_specs += [pl.BlockSpec((1, k), lambda i, j: (0, 0)), pl.BlockSpec((k, tn), lambda i, j: (0, j))]
    args = list(xs) + [gain, w]
    if has_res:
        in_specs.append(pl.BlockSpec((tm, tn), lambda i, j: (i, j)))
        args.append(res)
    return pl.pallas_call(
        body, name=name, grid=(t // tm, n // tn), in_specs=in_specs,
        out_specs=[pl.BlockSpec((tm, k), lambda i, j: (i, 0)), pl.BlockSpec((tm, tn), lambda i, j: (i, j))],
        out_shape=[_sds((t, k), BF16), _sds((t, n), out_dtype)],
        scratch_shapes=[pltpu.VMEM((tm, k), BF16)],
        compiler_params=_cparams(PAR, ARB),
    )(*args)


def _mm(a, b, mode, res, *, tm, tn, out_dtype, name, a_rows=None):
    if mode == "tn":
        kk, m = a.shape
        blk_a = 0
        if a_rows is not None:
            blk_a, kk = a_rows
        a_spec = pl.BlockSpec((kk, tm), lambda i, j: (blk_a, i))
    else:
        m, kk = a.shape
        a_spec = pl.BlockSpec((tm, kk), lambda i, j: (i, 0))
    if mode == "nt":
        n = b.shape[0]
        b_spec = pl.BlockSpec((tn, kk), lambda i, j: (j, 0))
    else:
        n = b.shape[1]
        b_spec = pl.BlockSpec((kk, tn), lambda i, j: (0, j))
    has_res = res is not None

    def body(*refs):
        a_ref, b_ref = refs[0], refs[1]
        o_ref = refs[-1]
        av = a_ref[...].astype(BF16)
        bv = b_ref[...].astype(BF16)
        if mode == "nn":
            acc = jnp.dot(av, bv, preferred_element_type=F32)
        elif mode == "nt":
            acc = lax.dot_general(av, bv, NT, preferred_element_type=F32)
        else:
            acc = lax.dot_general(av, bv, TN, preferred_element_type=F32)
        if has_res:
            acc = acc + refs[2][...]
        o_ref[...] = acc.astype(out_dtype)

    in_specs = [a_spec, b_spec]
    args = [a, b]
    if has_res:
        in_specs.append(pl.BlockSpec((tm, tn), lambda i, j: (i, j)))
        args.append(res)
    return pl.pallas_call(
        body, name=name, grid=(m // tm, n // tn), in_specs=in_specs,
        out_specs=pl.BlockSpec((tm, tn), lambda i, j: (i, j)),
        out_shape=_sds((m, n), out_dtype),
        compiler_params=_cparams(PAR, PAR),
    )(*args)


def _mm_bt_normbwd(dys, w, xs, gain, dres, *, tm, tn, name, emit_bf16=False):
    t, wd_each = dys[0].shape
    nd = len(dys)
    per = wd_each // tn
    nj = nd * per
    k = w.shape[0]
    ng = len(xs)
    has_res = dres is not None

    def body(*refs):
        dy_refs = refs[:nd]
        w_ref = refs[nd]
        x_refs = refs[nd + 1:nd + 1 + ng]
        g_ref = refs[nd + 1 + ng]
        dres_ref = refs[nd + 2 + ng] if has_res else None
        outs = refs[nd + 2 + ng + has_res:]
        dx_ref = outs[0]
        dxb_ref = outs[1] if emit_bf16 else None
        dg_ref, acc = outs[1 + emit_bf16:]
        i, j = pl.program_id(0), pl.program_id(1)

        @pl.when(j == 0)
        def _():
            acc[...] = jnp.zeros_like(acc)

        for d, dy_ref in enumerate(dy_refs):
            @pl.when((j >= d * per) & (j < (d + 1) * per))
            def _(dy_ref=dy_ref):
                acc[...] += lax.dot_general(dy_ref[...].astype(BF16), w_ref[...], NT, preferred_element_type=F32)

        @pl.when(j == nj - 1)
        def _():
            @pl.when(i == 0)
            def _():
                dg_ref[...] = jnp.zeros_like(dg_ref)

            off = 0
            for xr in x_refs:
                x = xr[...]
                wd = x.shape[1]
                g = g_ref[:, off:off + wd]
                dyn = acc[:, off:off + wd]
                r = lax.rsqrt(jnp.mean(x * x, axis=-1, keepdims=True) + EPS)
                gdy = dyn * g
                dx = r * gdy - x * (r * r * r * jnp.mean(gdy * x, axis=-1, keepdims=True))
                if has_res:
                    dx = dx + dres_ref[:, off:off + wd]
                dx_ref[:, off:off + wd] = dx
                if emit_bf16:
                    dxb_ref[:, off:off + wd] = dx.astype(BF16)
                dg_ref[:, off:off + wd] += jnp.sum(dyn * x * r, axis=0, keepdims=True)
                off += wd

    def dy_map(d):
        return lambda i, j: (i, jnp.clip(j - d * per, 0, per - 1))

    in_specs = [pl.BlockSpec((tm, tn), dy_map(d)) for d in range(nd)]
    in_specs.append(pl.BlockSpec((k, tn), lambda i, j: (0, j)))
    in_specs += [pl.BlockSpec((tm, x.shape[1]), lambda i, j: (i, 0)) for x in xs]
    in_specs.append(pl.BlockSpec((1, k), lambda i, j: (0, 0)))
    args = list(dys) + [w] + list(xs) + [gain]
    if has_res:
        in_specs.append(pl.BlockSpec((tm, k), lambda i, j: (i, 0)))
        args.append(dres)
    row = pl.BlockSpec((tm, k), lambda i, j: (i, 0))
    out_specs = [row] + ([row] if emit_bf16 else []) + [pl.BlockSpec((1, k), lambda i, j: (0, 0))]
    out_shape = [_sds((t, k), F32)] + ([_sds((t, k), BF16)] if emit_bf16 else []) + [_sds((1, k), F32)]
    return pl.pallas_call(
        body, name=name, grid=(t // tm, nj), in_specs=in_specs, out_specs=out_specs, out_shape=out_shape,
        scratch_shapes=[pltpu.VMEM((tm, k), F32)],
        compiler_params=_cparams(ARB, ARB),
    )(*args)


def _rope_partner(y):
    n = y.shape[1]
    lane = lax.broadcasted_iota(jnp.int32, y.shape, 1)
    return jnp.where((lane & 31) < 16, pltpu.roll(y, n - 16, 1), pltpu.roll(y, 16, 1))


def _residue_specs(tm, width, nt):
    specs = [pl.BlockSpec((tm, width), lambda b, i: (b * nt + i, 0))]
    for d in DILATIONS[1:]:
        specs.append(pl.BlockSpec((None, d, tm // d, width), lambda b, i: (b, 0, i, 0)))
    return specs


def _residue_shapes(bl, width, dtype):
    return [_sds((bl * SEQ, width), dtype)] + [_sds((bl, d, SEQ // d, width), dtype) for d in DILATIONS[1:]]


def _qkprep_fwd(proj, gains, cos, sins, *, tm, name):
    t = proj.shape[0]
    bl = t // SEQ
    nt = SEQ // tm

    def body(p_ref, g_ref, c_ref, s_ref, qa1_ref, qa4_ref, qa16_ref, qb_ref, qd_ref, scr):
        e = _group_sum_matrix(256, True)

        def hn(x, row):
            wd = x.shape[1]
            ms = _seg_sum(x * x, e[:wd, :wd]) * (1.0 / HEAD_DIM)
            return x * lax.rsqrt(ms + EPS) * g_ref[row:row + 1, :wd]

        qa = jnp.concatenate([hn(p_ref[:, 0:256], 0) * ATTN_SCALE, hn(p_ref[:, 256:512], 1), p_ref[:, 512:768]], axis=1)
        qa1_ref[...] = qa.astype(BF16)
        _scatter_cols(scr, 0, qa)
        for d, ref in ((4, qa4_ref), (16, qa16_ref)):
            for r in range(d):
                ref[r] = _read_residue(scr, 0, 6, r, d).astype(BF16)
        qb_ref[:, 0:256] = (hn(p_ref[:, 768:1024], 2) * ATTN_SCALE).astype(BF16)
        qb_ref[:, 256:384] = hn(p_ref[:, 1024:1152], 3).astype(BF16)
        qb_ref[:, 384:512] = p_ref[:, 1152:1280].astype(BF16)
        yq = hn(p_ref[:, 1792:2048], 4)
        yq = yq * c_ref[...] + _rope_partner(yq) * s_ref[...]
        qd_ref[:, 0:256] = (yq * ATTN_SCALE).astype(BF16)
        yk = hn(p_ref[:, 2048:2176], 5)
        yk = yk * c_ref[:, 0:128] + _rope_partner(yk) * s_ref[:, 0:128]
        qd_ref[:, 256:384] = yk.astype(BF16)
        qd_ref[:, 384:512] = p_ref[:, 2176:2304].astype(BF16)

    row = lambda width: pl.BlockSpec((tm, width), lambda b, i: (b * nt + i, 0))
    tab = pl.BlockSpec((tm, 256), lambda b, i: (i, 0))
    return pl.pallas_call(
        body, name=name, grid=(bl, nt),
        in_specs=[row(IN_WIDTH), pl.BlockSpec((8, 256), lambda b, i: (0, 0)), tab, tab],
        out_specs=_residue_specs(tm, 768, nt) + [row(512), row(512)],
        out_shape=_residue_shapes(bl, 768, BF16) + [_sds((t, 512), BF16), _sds((t, 512), BF16)],
        scratch_shapes=[pltpu.VMEM((6, tm, LANES), F32)],
        compiler_params=_cparams(PAR, PAR),
    )(proj, gains, cos, sins)


def _qkprep_bwd(proj, da, db, dd, dcu, dcv, gains, cos, sins, *, tm, name):
    t = proj.shape[0]
    bl = t // SEQ
    nt = SEQ // tm
    flat = [a for cfg in da for a in cfg] + list(db) + list(dd) + [dcu, dcv]

    def body(*refs):
        p_ref, g_ref, c_ref, s_ref = refs[:4]
        d_refs = refs[4:4 + len(flat)]
        dp_ref, dg_ref, scr = refs[4 + len(flat):]
        a_refs = d_refs[:9]
        dqb_ref, dkb_ref, dvb_ref, dqd_ref, dkd_ref, dvd_ref, dcu_ref, dcv_ref = d_refs[9:]
        e = _group_sum_matrix(256, True)
        first = (pl.program_id(0) == 0) & (pl.program_id(1) == 0)
        last = (pl.program_id(0) == bl - 1) & (pl.program_id(1) == nt - 1)

        @pl.when(first)
        def _():
            dg_ref[...] = jnp.zeros_like(dg_ref)

        def hn_bwd(x, dy, row):
            wd = x.shape[1]
            ee = e[:wd, :wd]
            g = g_ref[row:row + 1, :wd]
            r = lax.rsqrt(_seg_sum(x * x, ee) * (1.0 / HEAD_DIM) + EPS)
            gdy = dy * g
            dx = r * gdy - x * (r * r * r * (_seg_sum(gdy * x, ee) * (1.0 / HEAD_DIM)))
            dg_ref[row:row + 1, :wd] += jnp.sum(dy * x * r, axis=0, keepdims=True)
            return dx

        def rope_bwd(dy, wd):
            return dy * c_ref[:, :wd] + _rope_partner(dy * s_ref[:, :wd])

        dqkv = jnp.concatenate([a_refs[0][...], a_refs[1][...], a_refs[2][...]], axis=1)
        for ci, d in ((1, 4), (2, 16)):
            for r in range(d):
                part = jnp.concatenate([a_refs[3 * ci + m][r] for m in range(3)], axis=1)
                _write_residue(scr, 0, r, d, part)
            dqkv = dqkv + _gather_cols(scr, 0, 6)
        dp_ref[:, 0:256] = hn_bwd(p_ref[:, 0:256], dqkv[:, 0:256] * ATTN_SCALE, 0).astype(BF16)
        dp_ref[:, 256:512] = hn_bwd(p_ref[:, 256:512], dqkv[:, 256:512], 1).astype(BF16)
        dp_ref[:, 512:768] = dqkv[:, 512:768].astype(BF16)
        dp_ref[:, 768:1024] = hn_bwd(p_ref[:, 768:1024], dqb_ref[...] * ATTN_SCALE, 2).astype(BF16)
        dp_ref[:, 1024:1152] = hn_bwd(p_ref[:, 1024:1152], dkb_ref[...], 3).astype(BF16)
        dp_ref[:, 1152:1280] = dvb_ref[...].astype(BF16)
        dp_ref[:, 1280:1536] = dcu_ref[...].astype(BF16)
        dp_ref[:, 1536:1792] = dcv_ref[...].astype(BF16)
        dp_ref[:, 1792:2048] = hn_bwd(p_ref[:, 1792:2048], rope_bwd(dqd_ref[...] * ATTN_SCALE, 256), 4).astype(BF16)
        dp_ref[:, 2048:2176] = hn_bwd(p_ref[:, 2048:2176], rope_bwd(dkd_ref[...], 128), 5).astype(BF16)
        dp_ref[:, 2176:2304] = dvd_ref[...].astype(BF16)

        @pl.when(last)
        def _():
            dg_ref[...] = _seg_sum(dg_ref[...], _group_sum_matrix(256, False))

    row = lambda width: pl.BlockSpec((tm, width), lambda b, i: (b * nt + i, 0))
    tab = pl.BlockSpec((tm, 256), lambda b, i: (i, 0))
    in_specs = [row(IN_WIDTH), pl.BlockSpec((8, 256), lambda b, i: (0, 0)), tab, tab]
    res_specs = _residue_specs(tm, 256, nt)
    in_specs += [res_specs[ci] for ci in range(3) for _ in range(3)]
    in_specs += [row(a.shape[1]) for a in flat[9:]]
    return pl.pallas_call(
        body, name=name, grid=(bl, nt), in_specs=in_specs,
        out_specs=[row(IN_WIDTH), pl.BlockSpec((8, 256), lambda b, i: (0, 0))],
        out_shape=[_sds((t, IN_WIDTH), BF16), _sds((8, 256), F32)],
        scratch_shapes=[pltpu.VMEM((6, tm, LANES), F32)],
        compiler_params=_cparams(ARB, ARB),
    )(proj, gains, cos, sins, *flat)


def _band_spec(seq_len, spec):
    width, idx = spec
    return pl.BlockSpec((None, None, seq_len, width), lambda b, r: (b, r, 0, idx))


def _fill_padded(dst, src_ref, rad, seq_len):
    z = jnp.zeros((rad, dst.shape[1]), dst.dtype)
    dst[0:rad, :] = z
    dst[rad + seq_len:rad + seq_len + rad, :] = z
    dst[rad:rad + seq_len, :] = src_ref[...]


def _band_fwd(src, qs, ks, vs, bias, sink, *, rad, nh, nkv, name):
    bl, dil, sl, _ = src.shape
    blk = bias.shape[1]
    kw = blk + 2 * rad
    nb = sl // blk
    rep = nh // nkv
    has_sink = sink is not None

    def body(*refs):
        q_ref, k_ref, v_ref, b_ref = refs[:4]
        s_ref = refs[4] if has_sink else None
        o_ref, l_ref, kp, vp = refs[4 + has_sink:]
        _fill_padded(kp, k_ref, rad, sl)
        _fill_padded(vp, v_ref, rad, sl)

        def blk_body(i, carry):
            r0 = pl.multiple_of(i * blk, blk)
            qb = q_ref[pl.ds(r0, blk), :]
            kwin = kp[pl.ds(r0, kw), :]
            vwin = vp[pl.ds(r0, kw), :]
            col = r0 - rad + lax.broadcasted_iota(jnp.int32, (blk, kw), 1)
            neg = jnp.where((col >= 0) & (col < sl), 0.0, NEG_INF).astype(F32)
            for h in range(nh):
                g = h // rep
                hs = slice(h * HEAD_DIM, (h + 1) * HEAD_DIM)
                gs = slice(g * HEAD_DIM, (g + 1) * HEAD_DIM)
                s = lax.dot_general(qb[:, hs], kwin[:, gs], NT, preferred_element_type=F32)
                s = s + b_ref[h] + neg
                m = jnp.max(s, axis=1, keepdims=True)
                if has_sink:
                    sk = s_ref[h][0:1, 0:1]
                    m = jnp.maximum(m, sk)
                p = jnp.exp(s - m)
                den = jnp.sum(p, axis=1, keepdims=True)
                if has_sink:
                    den = den + jnp.exp(sk - m)
                o = jnp.dot(p.astype(BF16), vwin[:, gs], preferred_element_type=F32) / den
                o_ref[pl.ds(r0, blk), hs] = o
                l_ref[pl.ds(r0, blk), hs] = jnp.broadcast_to(m + jnp.log(den), (blk, HEAD_DIM))
            return carry

        lax.fori_loop(0, nb, blk_body, 0)

    in_specs = [_band_spec(sl, qs), _band_spec(sl, ks), _band_spec(sl, vs),
                pl.BlockSpec((nh, blk, kw), lambda b, r: (0, 0, 0))]
    args = [src] * 3 + [bias]
    if has_sink:
        in_specs.append(pl.BlockSpec((nh, 8, 128), lambda b, r: (0, 0, 0)))
        args.append(sink)
    return pl.pallas_call(
        body, name=name, grid=(bl, dil), in_specs=in_specs,
        out_specs=[_band_spec(sl, (256, 0))] * 2,
        out_shape=[_sds((bl, dil, sl, 256), F32)] * 2,
        scratch_shapes=[pltpu.VMEM((sl + 2 * rad, ks[0]), BF16), pltpu.VMEM((sl + 2 * rad, vs[0]), BF16)],
        compiler_params=_cparams(PAR, PAR),
    )(*args)


def _band_bwd(src, qs, ks, vs, bias, sink, dy, dcol, lse, delta, *, rad, nh, nkv, name):
    bl, dil, sl, _ = src.shape
    blk = bias.shape[1]
    kw = blk + 2 * rad
    nb = sl // blk
    rep = nh // nkv
    has_sink = sink is not None
    wk, wv = ks[0], vs[0]

    def body(*refs):
        q_ref, k_ref, v_ref, b_ref = refs[:4]
        s_ref = refs[4] if has_sink else None
        do_ref, l_ref, dl_ref = refs[4 + has_sink:7 + has_sink]
        outs = refs[7 + has_sink:]
        if has_sink:
            dq_ref, dk_ref, dv_ref, db_ref, dsk_ref, kp, vp, dka, dva = outs
        else:
            dq_ref, dk_ref, dv_ref, db_ref, kp, vp, dka, dva = outs

        @pl.when((pl.program_id(0) == 0) & (pl.program_id(1) == 0))
        def _():
            db_ref[...] = jnp.zeros_like(db_ref)
            if has_sink:
                dsk_ref[...] = jnp.zeros_like(dsk_ref)

        _fill_padded(kp, k_ref, rad, sl)
        _fill_padded(vp, v_ref, rad, sl)
        dka[...] = jnp.zeros_like(dka)
        dva[...] = jnp.zeros_like(dva)

        def blk_body(i, carry):
            r0 = pl.multiple_of(i * blk, blk)
            qb = q_ref[pl.ds(r0, blk), :]
            kwin = kp[pl.ds(r0, kw), :]
            vwin = vp[pl.ds(r0, kw), :]
            dob = do_ref[pl.ds(r0, blk), :].astype(BF16)
            lb = l_ref[pl.ds(r0, blk), :]
            dlb = dl_ref[pl.ds(r0, blk), :]
            col = r0 - rad + lax.broadcasted_iota(jnp.int32, (blk, kw), 1)
            neg = jnp.where((col >= 0) & (col < sl), 0.0, NEG_INF).astype(F32)
            for h in range(nh):
                g = h // rep
                hs = slice(h * HEAD_DIM, (h + 1) * HEAD_DIM)
                gs = slice(g * HEAD_DIM, (g + 1) * HEAD_DIM)
                qh, kh, vh, doh = qb[:, hs], kwin[:, gs], vwin[:, gs], dob[:, hs]
                lh = lb[:, h * HEAD_DIM:h * HEAD_DIM + 1]
                dlh = dlb[:, h * HEAD_DIM:h * HEAD_DIM + 1]
                s = lax.dot_general(qh, kh, NT, preferred_element_type=F32) + b_ref[h] + neg
                p = jnp.exp(s - lh)
                dp = lax.dot_general(doh, vh, NT, preferred_element_type=F32)
                ds = p * (dp - dlh)
                dsb = ds.astype(BF16)
                dq_ref[pl.ds(r0, blk), hs] = jnp.dot(dsb, kh, preferred_element_type=F32)
                dka[pl.ds(r0, kw), gs] += lax.dot_general(dsb, qh, TN, preferred_element_type=F32)
                dva[pl.ds(r0, kw), gs] += lax.dot_general(p.astype(BF16), doh, TN, preferred_element_type=F32)
                db_ref[h] += ds
                if has_sink:
                    ps = jnp.exp(s_ref[h][0:1, 0:1] - lh)
                    dsk_ref[h] += jnp.broadcast_to(-jnp.sum(ps * dlh, axis=0, keepdims=True), (8, 128))
            return carry

        lax.fori_loop(0, nb, blk_body, 0)
        dk_ref[...] = dka[rad:rad + sl, :]
        dv_ref[...] = dva[rad:rad + sl, :]

    const3 = lambda b, r: (0, 0, 0)
    in_specs = [_band_spec(sl, qs), _band_spec(sl, ks), _band_spec(sl, vs), pl.BlockSpec((nh, blk, kw), const3)]
    args = [src] * 3 + [bias]
    if has_sink:
        in_specs.append(pl.BlockSpec((nh, 8, 128), const3))
        args.append(sink)
    row = _band_spec(sl, (256, 0))
    in_specs += [_band_spec(sl, (256, dcol)), row, row]
    args += [dy, lse, delta]
    out_specs = [row, _band_spec(sl, (wk, 0)), _band_spec(sl, (wv, 0)), pl.BlockSpec((nh, blk, kw), const3)]
    out_shape = [_sds((bl, dil, sl, 256), F32), _sds((bl, dil, sl, wk), F32), _sds((bl, dil, sl, wv), F32),
                 _sds((nh, blk, kw), F32)]
    if has_sink:
        out_specs.append(pl.BlockSpec((nh, 8, 128), const3))
        out_shape.append(_sds((nh, 8, 128), F32))
    return pl.pallas_call(
        body, name=name, grid=(bl, dil), in_specs=in_specs, out_specs=out_specs, out_shape=out_shape,
        scratch_shapes=[pltpu.VMEM((sl + 2 * rad, wk), BF16), pltpu.VMEM((sl + 2 * rad, wv), BF16),
                        pltpu.VMEM((sl + 2 * rad, wk), F32), pltpu.VMEM((sl + 2 * rad, wv), F32)],
        compiler_params=_cparams(ARB, ARB),
    )(*args)


def _combine_a(os_, ls_, *, tm, name):
    bl = os_[1].shape[0]
    t = bl * SEQ
    nt = SEQ // tm

    def body(o1, o4, o16, l1, l4, l16, y_ref, lt_ref, scr):
        for k, (d, ref) in enumerate(((4, o4), (16, o16), (4, l4), (16, l16))):
            for r in range(d):
                _write_residue(scr, 2 * k, r, d, ref[r])
        o2, o3, b, c = (_gather_cols(scr, 2 * k, 2) for k in range(4))
        a = l1[...]
        m = jnp.maximum(jnp.maximum(a, b), c)
        ea, eb, ec = jnp.exp(a - m), jnp.exp(b - m), jnp.exp(c - m)
        den = ea + eb + ec
        y_ref[...] = (ea / den) * o1[...] + (eb / den) * o2 + (ec / den) * o3
        lt_ref[...] = m + jnp.log(den)

    specs = _residue_specs(tm, 256, nt)
    return pl.pallas_call(
        body, name=name, grid=(bl, nt), in_specs=specs * 2, out_specs=[specs[0]] * 2,
        out_shape=[_sds((t, 256), F32)] * 2, scratch_shapes=[pltpu.VMEM((8, tm, LANES), F32)],
        compiler_params=_cparams(PAR, PAR),
    )(*os_, *ls_)


def _deltas(dycat, ya, yb, yd, lse_a, *, tm, name):
    t = ya.shape[0]
    bl = t // SEQ
    nt = SEQ // tm

    def body(dy_ref, ya_ref, yb_ref, yd_ref, la_ref, dy4, dy16, l4, l16, da1, da4, da16, db_ref, dd_ref, scr):
        e = _group_sum_matrix(256, True)
        dya = dy_ref[:, 0:256]
        dla = _seg_sum(dya * ya_ref[...], e)
        da1[...] = dla
        db_ref[...] = _seg_sum(dy_ref[:, 256:512] * yb_ref[...], e)
        dd_ref[...] = _seg_sum(dy_ref[:, 768:1024] * yd_ref[...], e)
        for k, (val, r4, r16) in enumerate(((dya, dy4, dy16), (la_ref[...], l4, l16), (dla, da4, da16))):
            _scatter_cols(scr, 2 * k, val)
            for d, ref in ((4, r4), (16, r16)):
                for r in range(d):
                    ref[r] = _read_residue(scr, 2 * k, 2, r, d)

    specs = _residue_specs(tm, 256, nt)
    nat = specs[0]
    shapes = _residue_shapes(bl, 256, F32)
    outs = pl.pallas_call(
        body, name=name, grid=(bl, nt),
        in_specs=[pl.BlockSpec((tm, 1024), lambda b, i: (b * nt + i, 0)), nat, nat, nat, nat],
        out_specs=specs[1:] + specs[1:] + specs + [nat, nat],
        out_shape=shapes[1:] + shapes[1:] + shapes + [shapes[0], shapes[0]],
        scratch_shapes=[pltpu.VMEM((6, tm, LANES), F32)],
        compiler_params=_cparams(PAR, PAR),
    )(dycat, ya, yb, yd, lse_a)
    return outs[0:2], outs[2:4], outs[4:7], outs[7], outs[8]


def _dense_fwd(qd, *, tq, name):
    t = qd.shape[0]
    bl = t // SEQ
    nq = SEQ // tq

    def body(q_ref, k_ref, v_ref, o_ref, l_ref):
        q = q_ref[...]
        for g in range(2):
            h0, h1 = 2 * g, 2 * g + 1
            q2 = jnp.concatenate([q[:, h0 * 64:(h0 + 1) * 64], q[:, h1 * 64:(h1 + 1) * 64]], axis=0)
            kg = k_ref[:, g * 64:(g + 1) * 64]
            vg = v_ref[:, g * 64:(g + 1) * 64]
            s = lax.dot_general(q2, kg, NT, preferred_element_type=F32)
            m = jnp.max(s, axis=1, keepdims=True)
            p = jnp.exp(s - m)
            den = jnp.sum(p, axis=1, keepdims=True)
            o2 = jnp.dot(p.astype(BF16), vg, preferred_element_type=F32) / den
            l2 = jnp.broadcast_to(m + jnp.log(den), (2 * tq, 64))
            o_ref[:, h0 * 64:(h0 + 1) * 64] = o2[:tq]
            o_ref[:, h1 * 64:(h1 + 1) * 64] = o2[tq:]
            l_ref[:, h0 * 64:(h0 + 1) * 64] = l2[:tq]
            l_ref[:, h1 * 64:(h1 + 1) * 64] = l2[tq:]

    q3 = qd.reshape(bl, SEQ, 512)
    o, lse = pl.pallas_call(
        body, name=name, grid=(bl, nq),
        in_specs=[pl.BlockSpec((None, tq, 256), lambda b, i: (b, i, 0)),
                  pl.BlockSpec((None, SEQ, 128), lambda b, i: (b, 0, 2)),
                  pl.BlockSpec((None, SEQ, 128), lambda b, i: (b, 0, 3))],
        out_specs=[pl.BlockSpec((None, tq, 256), lambda b, i: (b, i, 0))] * 2,
        out_shape=[_sds((bl, SEQ, 256), F32)] * 2,
        compiler_params=_cparams(PAR, PAR),
    )(q3, q3, q3)
    return o.reshape(t, 256), lse.reshape(t, 256)


def _dense_bwd(qd, dycat, lse, delta, *, tq, name):
    t = qd.shape[0]
    bl = t // SEQ
    nq = SEQ // tq

    def body(q_ref, k_ref, v_ref, do_ref, l_ref, dl_ref, dq_ref, dk_ref, dv_ref):
        @pl.when(pl.program_id(1) == 0)
        def _():
            dk_ref[...] = jnp.zeros_like(dk_ref)
            dv_ref[...] = jnp.zeros_like(dv_ref)

        q = q_ref[...]
        do = do_ref[...].astype(BF16)
        lv = l_ref[...]
        dlv = dl_ref[...]
        for g in range(2):
            h0, h1 = 2 * g, 2 * g + 1
            q2 = jnp.concatenate([q[:, h0 * 64:(h0 + 1) * 64], q[:, h1 * 64:(h1 + 1) * 64]], axis=0)
            do2 = jnp.concatenate([do[:, h0 * 64:(h0 + 1) * 64], do[:, h1 * 64:(h1 + 1) * 64]], axis=0)
            l2 = jnp.concatenate([lv[:, h0 * 64:h0 * 64 + 1], lv[:, h1 * 64:h1 * 64 + 1]], axis=0)
            dl2 = jnp.concatenate([dlv[:, h0 * 64:h0 * 64 + 1], dlv[:, h1 * 64:h1 * 64 + 1]], axis=0)
            kg = k_ref[:, g * 64:(g + 1) * 64]
            vg = v_ref[:, g * 64:(g + 1) * 64]
            s = lax.dot_general(q2, kg, NT, preferred_element_type=F32)
            p = jnp.exp(s - l2)
            dp = lax.dot_general(do2, vg, NT, preferred_element_type=F32)
            ds = (p * (dp - dl2)).astype(BF16)
            dq2 = jnp.dot(ds, kg, preferred_element_type=F32)
            dq_ref[:, h0 * 64:(h0 + 1) * 64] = dq2[:tq]
            dq_ref[:, h1 * 64:(h1 + 1) * 64] = dq2[tq:]
            dk_ref[:, g * 64:(g + 1) * 64] += lax.dot_general(ds, q2, TN, preferred_element_type=F32)
            dv_ref[:, g * 64:(g + 1) * 64] += lax.dot_general(p.astype(BF16), do2, TN, preferred_element_type=F32)

    q3 = qd.reshape(bl, SEQ, 512)
    tile = pl.BlockSpec((None, tq, 256), lambda b, i: (b, i, 0))
    full = pl.BlockSpec((None, SEQ, 128), lambda b, i: (b, 0, 0))
    dq, dk, dv = pl.pallas_call(
        body, name=name, grid=(bl, nq),
        in_specs=[tile, pl.BlockSpec((None, SEQ, 128), lambda b, i: (b, 0, 2)),
                  pl.BlockSpec((None, SEQ, 128), lambda b, i: (b, 0, 3)),
                  pl.BlockSpec((None, tq, 256), lambda b, i: (b, i, 3)), tile, tile],
        out_specs=[tile, full, full],
        out_shape=[_sds((bl, SEQ, 256), F32), _sds((bl, SEQ, 128), F32), _sds((bl, SEQ, 128), F32)],
        compiler_params=_cparams(PAR, ARB),
    )(q3, q3, q3, dycat.reshape(bl, SEQ, 1024), lse.reshape(bl, SEQ, 256), delta.reshape(bl, SEQ, 256))
    return dq.reshape(t, 256), dk.reshape(t, 128), dv.reshape(t, 128)


def _c_norm(cv, gam, bet):
    vg = _gelu(cv)
    mu = jnp.mean(vg, axis=-1, keepdims=True)
    xc = vg - mu
    r = lax.rsqrt(jnp.mean(xc * xc, axis=-1, keepdims=True) + EPS)
    xhat = xc * r
    return xhat * gam + bet, xhat, r


def _c_fwd(proj, gam, bet, ws, bst, *, tm, name):
    t = proj.shape[0]
    nch = tm // C_CHUNK

    def body(u_ref, v_ref, g_ref, b_ref, ws_ref, bs_ref, y_ref):
        vn, _, _ = _c_norm(v_ref[...], g_ref[...], b_ref[...])
        vnb = vn.astype(BF16)
        for c in range(nch):
            rows = slice(c * C_CHUNK, (c + 1) * C_CHUNK)
            for g in range(C_GROUPS):
                gs = slice(g * 64, (g + 1) * 64)
                mixed = jnp.dot(ws_ref[g], vnb[rows, gs], preferred_element_type=F32) + bs_ref[:, gs]
                y_ref[rows, gs] = _gelu(u_ref[rows, gs]) * mixed

    vec = pl.BlockSpec((1, 256), lambda i: (0, 0))
    return pl.pallas_call(
        body, name=name, grid=(t // tm,),
        in_specs=[pl.BlockSpec((tm, 256), lambda i: (i, 5)), pl.BlockSpec((tm, 256), lambda i: (i, 6)), vec, vec,
                  pl.BlockSpec((C_GROUPS, C_CHUNK, C_CHUNK), lambda i: (0, 0, 0)),
                  pl.BlockSpec((C_CHUNK, 256), lambda i: (0, 0))],
        out_specs=pl.BlockSpec((tm, 256), lambda i: (i, 0)), out_shape=_sds((t, 256), F32),
        compiler_params=_cparams(PAR),
    )(proj, proj, gam, bet, ws, bst)


def _c_bwd(proj, dycat, gam, bet, ws, wst, bst, *, tm, name):
    t = proj.shape[0]
    nch = tm // C_CHUNK
    nstep = t // tm

    def body(u_ref, v_ref, dy_ref, g_ref, b_ref, ws_ref, wst_ref, bs_ref,
             du_ref, dv_ref, dws_ref, dbs_ref, dg_ref, db_ref, dvn_s):
        step = pl.program_id(0)

        @pl.when(step == 0)
        def _():
            dws_ref[...] = jnp.zeros_like(dws_ref)
            dbs_ref[...] = jnp.zeros_like(dbs_ref)
            dg_ref[...] = jnp.zeros_like(dg_ref)
            db_ref[...] = jnp.zeros_like(db_ref)

        cv = v_ref[...]
        gam_v = g_ref[...]
        vn, xhat, r = _c_norm(cv, gam_v, b_ref[...])
        vnb = vn.astype(BF16)
        for c in range(nch):
            rows = slice(c * C_CHUNK, (c + 1) * C_CHUNK)
            for g in range(C_GROUPS):
                gs = slice(g * 64, (g + 1) * 64)
                cu = u_ref[rows, gs]
                dy = dy_ref[rows, gs]
                mixed = jnp.dot(ws_ref[g], vnb[rows, gs], preferred_element_type=F32) + bs_ref[:, gs]
                du_ref[rows, gs] = dy * mixed * _gelu_grad(cu)
                dmix = dy * _gelu(cu)
                dbs_ref[:, gs] += dmix
                dmb = dmix.astype(BF16)
                dws_ref[g] += lax.dot_general(dmb, vnb[rows, gs], NT, preferred_element_type=F32)
                dvn_s[rows, gs] = jnp.dot(wst_ref[g], dmb, preferred_element_type=F32)
        dvn = dvn_s[...]
        dg_ref[...] += jnp.sum(dvn * xhat, axis=0, keepdims=True)
        db_ref[...] += jnp.sum(dvn, axis=0, keepdims=True)
        dxh = dvn * gam_v
        dvg = r * (dxh - jnp.mean(dxh, axis=-1, keepdims=True) - xhat * jnp.mean(dxh * xhat, axis=-1, keepdims=True))
        dv_ref[...] = dvg * _gelu_grad(cv)

        @pl.when(step == nstep - 1)
        def _():
            dbs_ref[...] = _seg_sum(dbs_ref[...], _group_sum_matrix(256, True))

    vec = pl.BlockSpec((1, 256), lambda i: (0, 0))
    mat = pl.BlockSpec((C_GROUPS, C_CHUNK, C_CHUNK), lambda i: (0, 0, 0))
    bsp = pl.BlockSpec((C_CHUNK, 256), lambda i: (0, 0))
    tile = pl.BlockSpec((tm, 256), lambda i: (i, 0))
    return pl.pallas_call(
        body, name=name, grid=(nstep,),
        in_specs=[pl.BlockSpec((tm, 256), lambda i: (i, 5)), pl.BlockSpec((tm, 256), lambda i: (i, 6)),
                  pl.BlockSpec((tm, 256), lambda i: (i, 2)), vec, vec, mat, mat, bsp],
        out_specs=[tile, tile, mat, bsp, vec, vec],
        out_shape=[_sds((t, 256), F32), _sds((t, 256), F32), _sds((C_GROUPS, C_CHUNK, C_CHUNK), F32),
                   _sds((C_CHUNK, 256), F32), _sds((1, 256), F32), _sds((1, 256), F32)],
        scratch_shapes=[pltpu.VMEM((tm, 256), F32)],
        compiler_params=_cparams(ARB),
    )(proj, proj, dycat, gam, bet, ws, wst, bst)


FF_TC = 128
FF_NB = D_FF // FF_TC
FF_CH = 64
FF_HALO = 16


def _edge_taps(ref, first):
    if first:
        ext = ref[0:FF_CH + FF_HALO, :].astype(F32)
        body = slice(0, FF_CH)
    else:
        ext = ref[SEQ - FF_CH - FF_HALO:SEQ, :].astype(F32)
        body = slice(FF_HALO, FF_HALO + FF_CH)
    n = ext.shape[0]
    row = lax.broadcasted_iota(jnp.int32, ext.shape, 0)
    dn = pltpu.roll(ext, 1, 0)
    up = pltpu.roll(ext, n - 1, 0)
    if first:
        dn = jnp.where(row == 0, 0.0, dn)
    else:
        up = jnp.where(row == n - 1, 0.0, up)
    return dn[body], ext[body], up[body]


def _mid_taps(ref, r0):
    ext = ref[pl.ds(pl.multiple_of(r0 - FF_HALO, FF_HALO), FF_CH + 2 * FF_HALO), :].astype(F32)
    n = ext.shape[0]
    body = slice(FF_HALO, FF_HALO + FF_CH)
    return pltpu.roll(ext, 1, 0)[body], ext[body], pltpu.roll(ext, n - 1, 0)[body]


def _chunk_loop(step):
    step(0, lambda ref: _edge_taps(ref, True))

    def mid(i, carry):
        r0 = pl.multiple_of(i * FF_CH, FF_CH)
        step(r0, lambda ref: _mid_taps(ref, r0))
        return carry

    lax.fori_loop(1, SEQ // FF_CH - 1, mid, 0)
    step(SEQ - FF_CH, lambda ref: _edge_taps(ref, False))


def _conv3(taps, w_ref, b_ref):
    dn, md, up = taps
    return w_ref[0:1, :] * dn + w_ref[1:2, :] * md + w_ref[2:3, :] * up + b_ref[...]


def _ff_specs(order):
    def at(fn):
        return (lambda b, j: fn(b, j)) if order == "bj" else (lambda j, b: fn(b, j))
    hs = [pl.BlockSpec((None, SEQ, FF_TC), at(lambda b, j, o=o: (b, 0, j + o))) for o in (0, FF_NB)]
    ws = [pl.BlockSpec((3, FF_TC), at(lambda b, j, o=o: (0, j + o))) for o in (0, FF_NB)]
    bs = [pl.BlockSpec((1, FF_TC), at(lambda b, j, o=o: (0, j + o))) for o in (0, FF_NB)]
    return hs, ws, bs


def _conv_gate_fwd(h, cw, cb, *, name):
    t = h.shape[0]
    bl = t // SEQ

    def body(hg_ref, hu_ref, wg_ref, wu_ref, bg_ref, bu_ref, a_ref):
        def step(r0, taps):
            cg = _conv3(taps(hg_ref), wg_ref, bg_ref)
            cu = _conv3(taps(hu_ref), wu_ref, bu_ref)
            a_ref[pl.ds(r0, FF_CH), :] = (cg * _sigmoid(cg) * cu).astype(BF16)

        _chunk_loop(step)

    hs, ws, bs = _ff_specs("bj")
    h3 = h.reshape(bl, SEQ, 2 * D_FF)
    act = pl.pallas_call(
        body, name=name, grid=(bl, FF_NB), in_specs=hs + ws + bs,
        out_specs=pl.BlockSpec((None, SEQ, FF_TC), lambda b, j: (b, 0, j)),
        out_shape=_sds((bl, SEQ, D_FF), BF16),
        compiler_params=_cparams(PAR, PAR),
    )(h3, h3, cw, cw, cb, cb)
    return act.reshape(t, D_FF)


def _conv_gate_bwd(h, dact, cw, cb, *, name):
    t = h.shape[0]
    bl = t // SEQ

    def body(hg_ref, hu_ref, wg_ref, wu_ref, bg_ref, bu_ref, da_ref,
             dhg_ref, dhu_ref, dwg_ref, dwu_ref, dbg_ref, dbu_ref, dg_s, du_s):
        @pl.when(pl.program_id(1) == 0)
        def _():
            for ref in (dwg_ref, dwu_ref, dbg_ref, dbu_ref):
                ref[...] = jnp.zeros_like(ref)

        red = lambda x: jnp.sum(x, axis=0, keepdims=True)

        def pass1(r0, taps):
            tg, tu = taps(hg_ref), taps(hu_ref)
            cg = _conv3(tg, wg_ref, bg_ref)
            cu = _conv3(tu, wu_ref, bu_ref)
            da = da_ref[pl.ds(r0, FF_CH), :].astype(F32)
            sg = _sigmoid(cg)
            dcg = da * cu * (sg * (1.0 + cg * (1.0 - sg)))
            dcu = da * (cg * sg)
            dg_s[pl.ds(r0, FF_CH), :] = dcg
            du_s[pl.ds(r0, FF_CH), :] = dcu
            for d, tp, dw_ref, db_ref in ((dcg, tg, dwg_ref, dbg_ref), (dcu, tu, dwu_ref, dbu_ref)):
                for k in range(3):
                    dw_ref[k:k + 1, :] += red(d * tp[k])
                db_ref[...] += red(d)

        _chunk_loop(pass1)

        def pass2(r0, taps):
            for s, w_ref, o_ref in ((dg_s, wg_ref, dhg_ref), (du_s, wu_ref, dhu_ref)):
                dn, md, up = taps(s)
                o_ref[pl.ds(r0, FF_CH), :] = (w_ref[0:1, :] * up + w_ref[1:2, :] * md + w_ref[2:3, :] * dn).astype(BF16)

        _chunk_loop(pass2)

    hs, ws, bs = _ff_specs("jb")
    half = pl.BlockSpec((None, SEQ, FF_TC), lambda j, b: (b, 0, j))
    wsp = pl.BlockSpec((3, FF_TC), lambda j, b: (0, j))
    bsp = pl.BlockSpec((1, FF_TC), lambda j, b: (0, j))
    h3 = h.reshape(bl, SEQ, 2 * D_FF)
    dhg, dhu, dwg, dwu, dbg, dbu = pl.pallas_call(
        body, name=name, grid=(FF_NB, bl), in_specs=hs + ws + bs + [half],
        out_specs=[half, half, wsp, wsp, bsp, bsp],
        out_shape=[_sds((bl, SEQ, D_FF), BF16), _sds((bl, SEQ, D_FF), BF16), _sds((3, D_FF), F32), _sds((3, D_FF), F32),
                   _sds((1, D_FF), F32), _sds((1, D_FF), F32)],
        scratch_shapes=[pltpu.VMEM((SEQ, FF_TC), F32), pltpu.VMEM((SEQ, FF_TC), F32)],
        compiler_params=_cparams(PAR, ARB),
    )(h3, h3, cw, cw, cb, cb, dact.reshape(bl, SEQ, D_FF))
    return (dhg.reshape(t, D_FF), dhu.reshape(t, D_FF), jnp.concatenate([dwg, dwu], axis=1),
            jnp.concatenate([dbg, dbu], axis=1))


def _ple_fwd(x2, gain, wg, pe, pe_blk, wp, *, tm, tn, name):
    t, k = x2.shape
    n = wg.shape[1]

    def body(x_ref, g_ref, wg_ref, pe_ref, wp_ref, xr_ref, hn_ref, x3_ref, gt_ref, pp_ref, hn_s):
        @pl.when(pl.program_id(1) == 0)
        def _():
            x = x_ref[...]
            r = lax.rsqrt(jnp.mean(x * x, axis=-1, keepdims=True) + EPS)
            hn_s[...] = (x * r * g_ref[...]).astype(BF16)
            hn_ref[...] = hn_s[...]

        gate = _sigmoid(jnp.dot(hn_s[...], wg_ref[...], preferred_element_type=F32))
        pp = jnp.dot(pe_ref[...].astype(BF16), wp_ref[...], preferred_element_type=F32)
        gt_ref[...] = gate
        pp_ref[...] = pp
        x3_ref[...] = xr_ref[...] + pp * gate

    tile = pl.BlockSpec((tm, tn), lambda i, j: (i, j))
    return pl.pallas_call(
        body, name=name, grid=(t // tm, n // tn),
        in_specs=[pl.BlockSpec((tm, k), lambda i, j: (i, 0)), pl.BlockSpec((1, k), lambda i, j: (0, 0)),
                  pl.BlockSpec((k, tn), lambda i, j: (0, j)), pl.BlockSpec((tm, PLE_DIM), lambda i, j: (pe_blk + i, 0)),
                  pl.BlockSpec((PLE_DIM, tn), lambda i, j: (0, j)), tile],
        out_specs=[pl.BlockSpec((tm, k), lambda i, j: (i, 0)), tile, tile, tile],
        out_shape=[_sds((t, k), BF16), _sds((t, n), F32), _sds((t, n), F32), _sds((t, n), F32)],
        scratch_shapes=[pltpu.VMEM((tm, k), BF16)],
        compiler_params=_cparams(PAR, ARB),
    )(x2, gain, wg, pe, wp, x2)


def _ple_bwd_ew(dx3, gate, pp, *, tm, name):
    t, n = dx3.shape

    def body(d_ref, g_ref, p_ref, dz_ref, dpp_ref):
        d, g = d_ref[...], g_ref[...]
        dz_ref[...] = (d * p_ref[...] * g * (1.0 - g)).astype(BF16)
        dpp_ref[...] = (d * g).astype(BF16)

    spec = pl.BlockSpec((tm, n), lambda i: (i, 0))
    return pl.pallas_call(
        body, name=name, grid=(t // tm,), in_specs=[spec] * 3, out_specs=[spec] * 2,
        out_shape=[_sds((t, n), BF16)] * 2, compiler_params=_cparams(PAR),
    )(dx3, gate, pp)


def _loss_head(y, tgt, *, tm, name):
    t, d = y.shape

    def body(y_ref, t_ref, l_ref, dy_ref):
        @pl.when(pl.program_id(0) == 0)
        def _():
            l_ref[...] = jnp.zeros_like(l_ref)

        e = y_ref[...] - t_ref[...]
        dy_ref[...] = e * (1.0 / d)
        s = jnp.sum(jnp.sum(e * e, axis=1, keepdims=True), axis=0, keepdims=True)
        l_ref[...] += jnp.broadcast_to(s * (0.5 / d), (8, 128))

    spec = pl.BlockSpec((tm, d), lambda i: (i, 0))
    return pl.pallas_call(
        body, name=name, grid=(t // tm,), in_specs=[spec, spec],
        out_specs=[pl.BlockSpec((8, 128), lambda i: (0, 0)), spec],
        out_shape=[_sds((8, 128), F32), _sds((t, d), F32)], compiler_params=_cparams(ARB),
    )(y, tgt)


BIAS_PC = 8192


def _onehot(bucket_row):
    rows = lax.broadcasted_iota(jnp.int32, (REL_BUCKETS, bucket_row.shape[1]), 0)
    return (rows == bucket_row).astype(F32)


def _bias_lookup(table_t, bucket, *, name):
    h = table_t.shape[0]
    p = bucket.shape[1]

    def body(t_ref, b_ref, o_ref):
        bk = b_ref[...]
        val = jnp.dot(t_ref[...], _onehot(bk), precision=HI, preferred_element_type=F32)
        o_ref[...] = jnp.where(bk >= 0, val, NEG_INF)

    return pl.pallas_call(
        body, name=name, grid=(p // BIAS_PC,),
        in_specs=[pl.BlockSpec((h, REL_BUCKETS), lambda i: (0, 0)), pl.BlockSpec((1, BIAS_PC), lambda i: (0, i))],
        out_specs=pl.BlockSpec((h, BIAS_PC), lambda i: (0, i)), out_shape=_sds((h, p), F32),
        compiler_params=_cparams(PAR),
    )(table_t, bucket)


def _bucket_reduce(dbias, bucket, *, name):
    h, p = dbias.shape

    def body(d_ref, b_ref, o_ref):
        @pl.when(pl.program_id(0) == 0)
        def _():
            o_ref[...] = jnp.zeros_like(o_ref)

        o_ref[...] += lax.dot_general(d_ref[...], _onehot(b_ref[...]), NT, precision=HI, preferred_element_type=F32)

    return pl.pallas_call(
        body, name=name, grid=(p // BIAS_PC,),
        in_specs=[pl.BlockSpec((h, BIAS_PC), lambda i: (0, i)), pl.BlockSpec((1, BIAS_PC), lambda i: (0, i))],
        out_specs=pl.BlockSpec((h, REL_BUCKETS), lambda i: (0, 0)), out_shape=_sds((h, REL_BUCKETS), F32),
        compiler_params=_cparams(ARB),
    )(dbias, bucket)


def _adamw_math(w, g, m, v):
    m = ADAM_B1 * m + (1.0 - ADAM_B1) * g
    v = ADAM_B2 * v + (1.0 - ADAM_B2) * (g * g)
    m_hat = m / (1.0 - ADAM_B1 ** ADAM_STEP)
    v_hat = v / (1.0 - ADAM_B2 ** ADAM_STEP)
    delta = -ADAM_LR * (m_hat / (jnp.sqrt(v_hat) + ADAM_EPS) + ADAM_WD * w)
    return delta, m, v


def _adamw_reduce(parts, w, m, v, *, tr, name):
    nl = len(parts)
    rows, c = w.shape
    r = rows // nl
    nt = r // tr

    def body(*refs):
        p_refs = refs[:nl]
        w_ref, m_ref, v_ref, g_ref, d_ref, nm_ref, nv_ref = refs[nl:]
        for li, p_ref in enumerate(p_refs):
            @pl.when(pl.program_id(0) == li)
            def _(p_ref=p_ref):
                g = p_ref[0].astype(F32)
                for k in range(1, N_DEV):
                    g = g + p_ref[k].astype(F32)
                d, nm, nv = _adamw_math(w_ref[...], g, m_ref[...], v_ref[...])
                g_ref[...] = g
                d_ref[...] = d
                nm_ref[...] = nm
                nv_ref[...] = nv

    def part_map(li):
        return lambda l, i: (0, jnp.where(l == li, i, jnp.where(l < li, 0, nt - 1)), 0)

    spec = pl.BlockSpec((tr, c), lambda l, i: (l * nt + i, 0))
    return pl.pallas_call(
        body, name=name, grid=(nl, nt),
        in_specs=[pl.BlockSpec((N_DEV, tr, c), part_map(li)) for li in range(nl)] + [spec, spec, spec],
        out_specs=[spec] * 4, out_shape=[_sds((rows, c), F32)] * 4, compiler_params=_cparams(ARB, ARB),
    )(*parts, w, m, v)


def _adamw_plain(g, w, m, v, *, name):
    def body(g_ref, w_ref, m_ref, v_ref, d_ref, nm_ref, nv_ref):
        d, nm, nv = _adamw_math(w_ref[...], g_ref[...], m_ref[...], v_ref[...])
        d_ref[...] = d
        nm_ref[...] = nm
        nv_ref[...] = nv

    return pl.pallas_call(body, name=name, out_shape=[_sds(w.shape, F32)] * 3)(g, w, m, v)


def _mesh_pos():
    return lax.axis_index("x"), lax.axis_index("y"), lax.axis_index("c")


def _allgather_body(x_refs, out_refs, send_sems, recv_sems, local_sems, slot):
    x, y, c = _mesh_pos()
    me, sibling = (x, y, c), (x, y, 1 - c)
    chips = [(1 - x, y), (x, 1 - y), (1 - x, 1 - y)]
    waits = []
    for a, (x_ref, out_ref) in enumerate(zip(x_refs, out_refs)):
        def copy(k, block, to, src=None, out_ref=out_ref, a=a):
            return pltpu.make_async_remote_copy(
                src_ref=slot(out_ref, block) if src is None else src, dst_ref=slot(out_ref, block),
                send_sem=send_sems.at[a, k], recv_sem=recv_sems.at[a, k], device_id=to, device_id_type=MESH)

        mine = pltpu.make_async_copy(x_ref, slot(out_ref, me), local_sems.at[a])
        mine.start()
        first = [copy(0, me, sibling, src=x_ref)]
        first += [copy(1 + j, me, (*chip, c), src=x_ref) for j, chip in enumerate(chips)]
        for cp in first:
            cp.start()
        waits.append((copy, mine, first))
    sends = []
    for copy, mine, first in waits:
        passed = [copy(4 + j, (*chip, c), sibling) for j, chip in enumerate(chips)]
        for j, chip in enumerate(chips):
            copy(1 + j, (*chip, c), me).wait_recv()
            passed[j].start()
        sends.append(passed)
    for (copy, mine, first), passed in zip(waits, sends):
        copy(0, sibling, me).wait_recv()
        for j, chip in enumerate(chips):
            copy(4 + j, (*chip, 1 - c), me).wait_recv()
        for cp in first + passed:
            cp.wait_send()
        mine.wait()


PEER_FLIPS = ((0, 0, 1), (1, 0, 0), (0, 1, 0), (1, 1, 0), (1, 0, 1), (0, 1, 1), (1, 1, 1))
HBM_SPEC = pl.BlockSpec(memory_space=pltpu.HBM)
SEM_SPEC = pl.BlockSpec(memory_space=pltpu.SEMAPHORE)
DATAFLOW = pltpu.SideEffectType.DATAFLOW_SIDE_EFFECTING


def _peer_copies(x_refs, land_refs, send_sem, recv_sem, scatter):
    x, y, c = _mesh_pos()
    me = 4 * x + 2 * y + c
    copies = []
    for x_ref, land_ref in zip(x_refs, land_refs):
        for fx, fy, fc in PEER_FLIPS:
            px, py, pc = x ^ fx, y ^ fy, c ^ fc
            src = x_ref.at[4 * px + 2 * py + pc] if scatter else x_ref
            copies.append(pltpu.make_async_remote_copy(
                src_ref=src, dst_ref=land_ref.at[me], send_sem=send_sem, recv_sem=recv_sem,
                device_id=(px, py, pc), device_id_type=MESH))
    return copies


def _exchange_start(groups, after, *, scatter, name):
    xs = [x for grp in groups for x in grp]
    na = len(xs)
    ngrp = len(groups)
    firsts = np.cumsum([0] + [len(grp) for grp in groups])
    land_shapes = [x.shape if scatter else (N_DEV,) + x.shape for x in xs]
    extra = [] if after is None else [after]

    def body(*refs):
        x_refs, land_refs = refs[:na], refs[na:2 * na]
        sems = refs[2 * na + len(extra):2 * na + len(extra) + 2 * ngrp]
        token, zeros, local_sem = refs[-3], refs[-2], refs[-1]
        xm, ym, cm = _mesh_pos()
        me = 4 * xm + 2 * ym + cm
        zeros[...] = jnp.zeros_like(zeros)
        locals_ = [pltpu.make_async_copy(zeros, token, local_sem)]
        locals_ += [pltpu.make_async_copy(x_ref.at[me] if scatter else x_ref, land_ref.at[me], local_sem)
                    for x_ref, land_ref in zip(x_refs, land_refs)]
        for own in locals_:
            own.start()
            own.wait()
        for gi in range(ngrp):
            lo, hi = firsts[gi], firsts[gi + 1]
            for cp in _peer_copies(x_refs[lo:hi], land_refs[lo:hi], sems[2 * gi], sems[2 * gi + 1], scatter):
                cp.start()

    sem_shapes = [pltpu.SemaphoreType.DMA(())] * (2 * ngrp)
    lands = [pltpu.with_memory_space_constraint(lax.empty(s, x.dtype), pltpu.HBM) for s, x in zip(land_shapes, xs)]
    outs = pl.pallas_call(
        body, name=name,
        in_specs=[HBM_SPEC] * (2 * na) + [pl.BlockSpec(memory_space=pl.ANY)] * len(extra),
        out_specs=[SEM_SPEC] * (2 * ngrp) + [HBM_SPEC] * (2 * na + 1),
        out_shape=sem_shapes + [pltpu.HBM(x.shape, x.dtype) for x in xs]
        + [pltpu.HBM(s, x.dtype) for s, x in zip(land_shapes, xs)] + [pltpu.HBM((8, 128), F32)],
        input_output_aliases={i: 2 * ngrp + i for i in range(2 * na)},
        scratch_shapes=[pltpu.VMEM((8, 128), F32), pltpu.SemaphoreType.DMA],
        compiler_params=pltpu.CompilerParams(has_side_effects=DATAFLOW),
    )(*[pltpu.with_memory_space_constraint(x, pltpu.HBM) for x in xs], *lands, *extra)
    sems, thru, token = outs[:2 * ngrp], outs[2 * ngrp:2 * ngrp + 2 * na], outs[-1]
    handles = []
    for gi in range(ngrp):
        lo, hi = firsts[gi], firsts[gi + 1]
        handles.append((sems[2 * gi], sems[2 * gi + 1], thru[lo:hi], thru[na + lo:na + hi]))
    return handles, token


def _exchange_wait(handle, after, *, scatter, name):
    send_sems, recv_sems, x_thru, land_thru = handle
    na = len(x_thru)

    def body(*refs):
        x_refs, land_refs = refs[:na], refs[na:2 * na]
        send_ref, recv_ref = refs[2 * na], refs[2 * na + 1]
        for cp in _peer_copies(x_refs, land_refs, send_ref, recv_ref, scatter):
            cp.wait_send()
            cp.wait_recv()

    outs = pl.pallas_call(
        body, name=name,
        in_specs=[HBM_SPEC] * (2 * na) + [SEM_SPEC, SEM_SPEC, pl.BlockSpec(memory_space=pl.ANY)],
        out_specs=[HBM_SPEC] * (2 * na),
        out_shape=[pltpu.HBM(a.shape, a.dtype) for a in list(x_thru) + list(land_thru)],
        input_output_aliases={i: i for i in range(2 * na)},
        compiler_params=pltpu.CompilerParams(has_side_effects=DATAFLOW),
    )(*x_thru, *land_thru, send_sems, recv_sems, after)
    return outs[na:]


def _sc_exchange(xs, *, scatter, collective_id, name):
    na = len(xs)
    land_shapes = [x.shape if scatter else (N_DEV,) + x.shape for x in xs]

    def body(*refs):
        x_refs, land_refs = refs[:na], refs[na:2 * na]
        send_sem, recv_sem, local_sem = refs[2 * na:]
        x, y, c = _mesh_pos()
        me = 4 * x + 2 * y + c
        barrier = pltpu.get_barrier_semaphore()
        for fx, fy, fc in PEER_FLIPS:
            pl.semaphore_signal(barrier, inc=1, device_id=(x ^ fx, y ^ fy, c ^ fc), device_id_type=MESH)
        pl.semaphore_wait(barrier, len(PEER_FLIPS))
        for x_ref, land_ref in zip(x_refs, land_refs):
            own = pltpu.make_async_copy(x_ref.at[me] if scatter else x_ref, land_ref.at[me], local_sem)
            own.start()
            own.wait()
        copies = _peer_copies(x_refs, land_refs, send_sem, recv_sem, scatter)
        for cp in copies:
            cp.start()
        for cp in copies:
            cp.wait()

    return pl.kernel(
        body, name=name, out_type=[_sds(s, x.dtype) for s, x in zip(land_shapes, xs)],
        mesh=plsc.ScalarSubcoreMesh(axis_name="sequencer", num_cores=1),
        scratch_types=[pltpu.SemaphoreType.DMA, pltpu.SemaphoreType.DMA, pltpu.SemaphoreType.DMA],
        compiler_params=pltpu.CompilerParams(collective_id=collective_id),
    )(*xs)


def _sc_allgather(xs, *, collective_id, name):
    na = len(xs)

    def body(*refs):
        x_refs, out_refs = refs[:na], refs[na:2 * na]
        send_sems, recv_sems, local_sems = refs[2 * na:]
        x, y, c = _mesh_pos()
        barrier = pltpu.get_barrier_semaphore()
        for fx, fy, fc in PEER_FLIPS:
            pl.semaphore_signal(barrier, inc=1, device_id=(x ^ fx, y ^ fy, c ^ fc), device_id_type=MESH)
        pl.semaphore_wait(barrier, len(PEER_FLIPS))
        _allgather_body(x_refs, out_refs, send_sems, recv_sems, local_sems,
                        lambda ref, pos: ref.at[4 * pos[0] + 2 * pos[1] + pos[2]])

    return pl.kernel(
        body, name=name, out_type=[_sds((N_DEV,) + x.shape, x.dtype) for x in xs],
        mesh=plsc.ScalarSubcoreMesh(axis_name="sequencer", num_cores=1),
        scratch_types=[pltpu.SemaphoreType.DMA((na, 7)), pltpu.SemaphoreType.DMA((na, 7)),
                       pltpu.SemaphoreType.DMA((na,))],
        compiler_params=pltpu.CompilerParams(collective_id=collective_id),
    )(*xs)


def _allgather_vmem(x, *, reduce, name):
    r, c = x.shape

    def body(x_ref, out_ref, *rest):
        if reduce:
            gath, send_sems, recv_sems, local_sems = rest
        else:
            send_sems, recv_sems, local_sems = rest
            gath = out_ref
        _allgather_body([x_ref], [gath], send_sems, recv_sems, local_sems,
                        lambda ref, pos: ref.at[pl.ds((4 * pos[0] + 2 * pos[1] + pos[2]) * r, r), :])
        if reduce:
            acc = gath[0:r, :]
            for k in range(1, N_DEV):
                acc = acc + gath[k * r:(k + 1) * r, :]
            out_ref[...] = acc

    vm = pl.BlockSpec(memory_space=pltpu.VMEM)
    scratch = [pltpu.SemaphoreType.DMA((1, 7)), pltpu.SemaphoreType.DMA((1, 7)), pltpu.SemaphoreType.DMA((1,))]
    if reduce:
        scratch = [pltpu.VMEM((N_DEV * r, c), x.dtype)] + scratch
    return pl.pallas_call(
        body, name=name, in_specs=[vm], out_specs=vm,
        out_shape=_sds((r, c) if reduce else (N_DEV * r, c), x.dtype), scratch_shapes=scratch,
    )(x)


def _t5_bucket(rel):
    nb = REL_BUCKETS // 2
    ret = jnp.where(rel > 0, nb, 0)
    n = jnp.abs(rel)
    max_exact = nb // 2
    nf = jnp.maximum(n, 1).astype(F32)
    large = max_exact + (jnp.log(nf / max_exact) / math.log(REL_MAX_DIST / max_exact)
                         * (nb - max_exact)).astype(jnp.int32)
    large = jnp.minimum(large, nb - 1)
    return ret + jnp.where(n < max_exact, n, large)


def _band_pattern(block, radius, dil):
    kw = block + 2 * radius
    rel = jnp.arange(kw)[None, :] - radius - jnp.arange(block)[:, None]
    return jnp.where(jnp.abs(rel) <= radius, _t5_bucket(rel * dil), -1).astype(jnp.int32).reshape(1, block * kw)


def _rope_tables():
    lane = np.arange(64)
    seg, j = lane // 32, lane % 32
    inv = ROPE_THETA ** (-jnp.arange(0, 32, 2, dtype=F32) / 32)
    tpos = jnp.arange(SEQ)
    pos = jnp.where(jnp.asarray(seg)[None, :] == 0, (tpos // GRID_W)[:, None], (tpos % GRID_W)[:, None])
    ang = pos.astype(F32) * inv[jnp.asarray(j % 16)][None, :]
    cos = jnp.cos(ang)
    sins = jnp.where(jnp.asarray(j)[None, :] < 16, -jnp.sin(ang), jnp.sin(ang))
    return jnp.tile(cos, (1, 4)), jnp.tile(sins, (1, 4))


A_Q, A_K, A_V = (256, 0), (256, 1), (256, 2)
B_Q, B_K, B_V = (256, 0), (128, 2), (128, 3)
A_HEADS = dict(rad=A_RADIUS, nh=4, nkv=4)
B_HEADS = dict(rad=SWA_RADIUS, nh=4, nkv=2)


def _pin(arr, token):
    return arr if token is None else arr + token[0:1, 0:1]


def _local_step(x, pe, tgt, rel_bias, wts, matmul_weights, grads_ready):
    t = x.shape[0]
    bl = t // SEQ
    cos, sins = _rope_tables()
    blocks_a = [min(BAND_BLOCK, SEQ // d) for d in DILATIONS]
    pats_a = [_band_pattern(blk, A_RADIUS, d) for blk, d in zip(blocks_a, DILATIONS)]
    pat_b = _band_pattern(BAND_BLOCK, SWA_RADIUS, 1)
    table_t = rel_bias.T
    bias_a = [_bias_lookup(table_t[:4], pt, name=f"bias_a{ci}").reshape(4, blk, blk + 2 * A_RADIUS)
              for ci, (pt, blk) in enumerate(zip(pats_a, blocks_a))]
    bias_b = _bias_lookup(table_t[4:], pat_b, name="bias_b").reshape(4, BAND_BLOCK, BAND_BLOCK + 2 * SWA_RADIUS)
    nat4 = lambda a: a.reshape(bl, 1, SEQ, a.shape[-1])

    saved = []
    for li in range(DEPTH):
        w = dict(wts[li])
        w.update(matmul_weights(li, "in", x)[0])
        hn0, proj = _norm_mm((x,), w["g_mix"], w["w_in"], None, tm=1024, tn=1152, name="mix_in_fwd")
        more, started = matmul_weights(li, "rest", proj)
        w.update(more)
        w["out_gain"] = _pin(w["out_gain"], started)
        qa1, qa4, qa16, qb, qd = _qkprep_fwd(proj, w["qk_gains"], cos, sins, tm=512, name="qkprep_fwd")
        qa = (nat4(qa1), qa4, qa16)
        oa, la = [], []
        for ci in range(3):
            o, l = _band_fwd(qa[ci], A_Q, A_K, A_V, bias_a[ci], None, name=f"band_a{ci}_fwd", **A_HEADS)
            oa.append(o)
            la.append(l)
        oa[0], la[0] = oa[0].reshape(t, 256), la[0].reshape(t, 256)
        ya, lse_a = _combine_a(oa, la, tm=512, name="combine_a")
        yb, lse_b = _band_fwd(nat4(qb), B_Q, B_K, B_V, bias_b, w["sink_t"], name="band_b_fwd", **B_HEADS)
        yb = yb.reshape(t, 256)
        yc = _c_fwd(proj, w["c_g"], w["c_b"], w["c_ws"], w["c_bst"], tm=512, name="c_fwd")
        yd, lse_d = _dense_fwd(qd, tq=128, name="dense_fwd")
        mixed, x1 = _norm_mm((ya, yb, yc, yd), w["out_gain"], w["w_out"], x, tm=1024, tn=1024, name="mix_out_fwd")
        hn1, h = _norm_mm((x1,), w["g_ffn"], w["w_up"], None, tm=1024, tn=1408, name="ffn_up_fwd", out_dtype=BF16)
        act = _conv_gate_fwd(h, w["conv_w"], w["conv_b"], name="conv_gate_fwd")
        x2 = _mm(act, w["w_down"], "nn", x1, tm=1024, tn=1024, out_dtype=F32, name="ffn_down_fwd")
        hn2, x3, gate, pp = _ple_fwd(x2, w["g_ple"], w["w_gate"], pe, li * (t // 1024), w["w_proj"], tm=1024, tn=512,
                                     name="ple_fwd")
        saved.append(dict(w=w, x0=x, hn0=hn0, proj=proj, qa=qa, qb=qb, qd=qd, ya=ya, lse_a=lse_a, yb=yb, lse_b=lse_b,
                          yc=yc, yd=yd, lse_d=lse_d, mixed=mixed, x1=x1, hn1=hn1, h=h, act=act, x2=x2, hn2=hn2,
                          gate=gate, pp=pp))
        x = x3

    loss_tile, dx = _loss_head(x, tgt, tm=512, name="loss_head")
    grads = [None] * DEPTH
    d_table_a = jnp.zeros((4, REL_BUCKETS), F32)
    d_table_b = jnp.zeros((4, REL_BUCKETS), F32)
    token = None
    for li in reversed(range(DEPTH)):
        s = saved[li]
        w = s["w"]
        g = {}
        w["g_ple"] = _pin(w["g_ple"], token)
        dz, dpp = _ple_bwd_ew(dx, s["gate"], s["pp"], tm=512, name="ple_bwd_ew")
        g["w_gate"] = _mm(s["hn2"], dz, "tn", None, tm=1024, tn=512, out_dtype=BF16, name="dw_gate")
        g["w_proj"] = _mm(pe, dpp, "tn", None, tm=256, tn=1024, out_dtype=BF16, name="dw_proj", a_rows=(li, t))
        dx2, dx2b, g["g_ple"] = _mm_bt_normbwd((dz,), w["w_gate"], (s["x2"],), w["g_ple"], dx, tm=1024, tn=1024,
                                               name="ple_bwd", emit_bf16=True)
        g["w_down"] = _mm(s["act"], dx2b, "tn", None, tm=1408, tn=512, out_dtype=BF16, name="dw_down")
        dact = _mm(dx2b, w["w_down"], "nt", None, tm=1024, tn=1408, out_dtype=BF16, name="ffn_down_bwd")
        dhg, dhu, g["conv_w"], g["conv_b"] = _conv_gate_bwd(s["h"], dact, w["conv_w"], w["conv_b"], name="conv_gate_bwd")
        g["w_up"] = jnp.concatenate(
            [_mm(s["hn1"], dhalf, "tn", None, tm=1024, tn=1408, out_dtype=BF16, name=f"dw_up_{nm}")
             for nm, dhalf in (("gate", dhg), ("up", dhu))], axis=1)
        g_ffn = _pin(w["g_ffn"], grads_ready(li, "mid", g))
        dx1, dx1b, g["g_ffn"] = _mm_bt_normbwd((dhg, dhu), w["w_up"], (s["x1"],), g_ffn, dx2, tm=1024, tn=1408,
                                               name="ffn_up_bwd", emit_bf16=True)
        g["w_out"] = _mm(s["mixed"], dx1b, "tn", None, tm=1024, tn=512, out_dtype=BF16, name="dw_out")
        dycat, g["out_gain"] = _mm_bt_normbwd((dx1b,), w["w_out"], (s["ya"], s["yb"], s["yc"], s["yd"]), w["out_gain"],
                                              None, tm=1024, tn=1024, name="mix_out_bwd")
        dy_r, lse_r, dl_a, dl_b, dl_d = _deltas(dycat, s["ya"], s["yb"], s["yd"], s["lse_a"], tm=512, name="deltas")
        dy_a = (nat4(dycat),) + tuple(dy_r)
        lse_a = (nat4(s["lse_a"]),) + tuple(lse_r)
        dl_a = (nat4(dl_a[0]),) + tuple(dl_a[1:])
        da = []
        for ci in range(3):
            dq, dk, dv, dbias = _band_bwd(s["qa"][ci], A_Q, A_K, A_V, bias_a[ci], None, dy_a[ci], 0, lse_a[ci],
                                          dl_a[ci], name=f"band_a{ci}_bwd", **A_HEADS)
            if ci == 0:
                dq, dk, dv = (a.reshape(t, 256) for a in (dq, dk, dv))
            da.append((dq, dk, dv))
            d_table_a = d_table_a + _bucket_reduce(dbias.reshape(4, -1), pats_a[ci], name=f"bucket_a{ci}")
        dqb, dkb, dvb, dbias_b, dsink = _band_bwd(nat4(s["qb"]), B_Q, B_K, B_V, bias_b, w["sink_t"], nat4(dycat), 1,
                                                  nat4(s["lse_b"]), nat4(dl_b), name="band_b_bwd", **B_HEADS)
        d_table_b = d_table_b + _bucket_reduce(dbias_b.reshape(4, -1), pat_b, name="bucket_b")
        g["sink"] = dsink[:, 0, 0]
        dd = _dense_bwd(s["qd"], dycat, s["lse_d"], dl_d, tq=128, name="dense_bwd")
        dcu, dcv, g["c_ws"], dbs, g["c_g"], g["c_b"] = _c_bwd(s["proj"], dycat, w["c_g"], w["c_b"], w["c_ws"],
                                                               w["c_wst"], w["c_bst"], tm=512, name="c_bwd")
        g["c_bs"] = dbs[:, ::64].T
        db = (dqb.reshape(t, 256), dkb.reshape(t, 128), dvb.reshape(t, 128))
        dproj, dgains = _qkprep_bwd(s["proj"], da, db, dd, dcu, dcv, w["qk_gains"], cos, sins, tm=512, name="qkprep_bwd")
        g["qk_gain"] = dgains[:6, :64].reshape(3, 2, HEAD_DIM)
        g["w_in"] = _mm(s["hn0"], dproj, "tn", None, tm=1024, tn=1152, out_dtype=BF16, name="dw_in")
        dx, g["g_mix"] = _mm_bt_normbwd((dproj,), w["w_in"], (s["x0"],), w["g_mix"], dx1, tm=1024, tn=1152,
                                        name="mix_in_bwd")
        grads[li] = g
        token = grads_ready(li, "end", g)
    d_rel_bias = jnp.concatenate([d_table_a, d_table_b], axis=0).T
    return loss_tile[0, 0], dx, grads, d_rel_bias


WEIGHT_NAMES = ("rel_bias", "ln_mix_g", "w_in", "qk_gain", "sink", "c_norm_g", "c_norm_b", "c_ws", "c_bs", "out_gain",
                "w_out", "ln_ffn_g", "w_up", "conv_w", "conv_b", "w_down", "ln_ple_g", "w_ple_gate", "w_ple_proj")
COL_SHARDED = ("w_in", "w_up", "w_ple_proj")
ROW_SHARDED = ("w_out", "w_down", "w_ple_gate")
SMALL_SHARDED = ("conv_w", "out_gain")
REPLICATED = tuple(n for n in WEIGHT_NAMES if n not in COL_SHARDED + ROW_SHARDED + SMALL_SHARDED)
LOCAL_GRAD_KEY = {"ln_mix_g": "g_mix", "ln_ffn_g": "g_ffn", "ln_ple_g": "g_ple", "c_norm_g": "c_g", "c_norm_b": "c_b",
                  "w_ple_gate": "w_gate", "w_ple_proj": "w_proj"}


def _full_from_gathered(name, gathered):
    _, r, c = gathered.shape
    if name in ROW_SHARDED:
        return gathered.reshape(N_DEV * r, c)
    return jnp.transpose(gathered, (1, 0, 2)).reshape(r, N_DEV * c)


def _slots_from_full(name, full):
    rows, cols = full.shape
    if name in ROW_SHARDED:
        return full.reshape(N_DEV, rows // N_DEV, cols)
    return jnp.transpose(full.reshape(rows, N_DEV, cols // N_DEV), (1, 0, 2))


def _piece_rows(shape):
    return -(-int(np.prod(shape)) // 1024) * 8


def _pack_rows(arrays):
    pieces = []
    for a in arrays:
        n, rows = int(np.prod(a.shape)), _piece_rows(a.shape)
        flat = a.astype(F32).reshape(-1)
        if n != rows * LANES:
            flat = jnp.pad(flat, (0, rows * LANES - n))
        pieces.append(flat.reshape(rows, LANES))
    return jnp.concatenate(pieces, axis=0)


def _unpack_rows(packed, shapes):
    out, off = [], 0
    for shp in shapes:
        n, rows = int(np.prod(shp)), _piece_rows(shp)
        piece = packed[off:off + rows]
        out.append((piece if n == rows * LANES else piece.reshape(-1)[:n]).reshape(shp))
        off += rows
    return out


def kernel(x, p, rel_bias, ln_mix_g, w_in, qk_gain, sink, c_norm_g, c_norm_b, c_ws, c_bs, out_gain, w_out, ln_ffn_g, w_up, conv_w, conv_b, w_down, ln_ple_g, w_ple_gate, w_ple_proj, loss_target, m_rel_bias, m_ln_mix_g, m_w_in, m_qk_gain, m_sink, m_c_norm_g, m_c_norm_b, m_c_ws, m_c_bs, m_out_gain, m_w_out, m_ln_ffn_g, m_w_up, m_conv_w, m_conv_b, m_w_down, m_ln_ple_g, m_w_ple_gate, m_w_ple_proj, v_rel_bias, v_ln_mix_g, v_w_in, v_qk_gain, v_sink, v_c_norm_g, v_c_norm_b, v_c_ws, v_c_bs, v_out_gain, v_w_out, v_ln_ffn_g, v_w_up, v_conv_w, v_conv_b, v_w_down, v_ln_ple_g, v_w_ple_gate, v_w_ple_proj):
    env = dict(locals())
    wt = {n: env[n] for n in WEIGHT_NAMES}
    mom_m = {n: env["m_" + n] for n in WEIGHT_NAMES}
    mom_v = {n: env["v_" + n] for n in WEIGHT_NAMES}
    bl = x.shape[0]
    t = bl * SEQ
    me = 4 * lax.axis_index("x") + 2 * lax.axis_index("y") + lax.axis_index("c")

    big = COL_SHARDED + ROW_SHARDED
    full = {}
    small_shapes = [wt[n].shape for n in SMALL_SHARDED]
    small = _allgather_vmem(_pack_rows([wt[n] for n in SMALL_SHARDED]), reduce=False, name="gather_small")
    small = small.reshape(N_DEV, -1)
    off = 0
    for n, shp in zip(SMALL_SHARDED, small_shapes):
        cnt = int(np.prod(shp))
        g = small[:, off:off + cnt].reshape((N_DEV,) + tuple(shp))
        full[n] = jnp.transpose(g, (1, 2, 0, 3)).reshape(shp[0], shp[1], N_DEV * shp[2])
        off += _piece_rows(shp) * LANES

    def head_gain(li, a, b, reps):
        g = jnp.tile(qk_gain[li, a, b], reps)
        return jnp.pad(g, (0, 256 - g.shape[0]))

    wts = []
    for li in range(DEPTH):
        rows = [head_gain(li, 0, 0, 4), head_gain(li, 0, 1, 4), head_gain(li, 1, 0, 4), head_gain(li, 1, 1, 2),
                head_gain(li, 2, 0, 4), head_gain(li, 2, 1, 2), jnp.zeros((256,), F32), jnp.zeros((256,), F32)]
        wts.append(dict(
            g_mix=ln_mix_g[li].reshape(1, -1), qk_gains=jnp.stack(rows),
            sink_t=jnp.broadcast_to(sink[li][:, None, None], (4, 8, 128)),
            c_g=c_norm_g[li].reshape(1, -1), c_b=c_norm_b[li].reshape(1, -1), c_ws=c_ws[li].astype(BF16),
            c_wst=jnp.transpose(c_ws[li], (0, 2, 1)).astype(BF16), c_bst=jnp.repeat(c_bs[li].T, 64, axis=1),
            out_gain=full["out_gain"][li].reshape(1, -1), g_ffn=ln_ffn_g[li].reshape(1, -1),
            conv_w=full["conv_w"][li], conv_b=conv_b[li].reshape(1, -1), g_ple=ln_ple_g[li].reshape(1, -1)))

    local_key = {"w_ple_gate": "w_gate", "w_ple_proj": "w_proj"}

    gather_names = {"in": ("w_in",), "rest": tuple(n for n in big if n != "w_in")}
    gathered = {}
    for li in range(DEPTH):
        lands = _sc_allgather([wt[n][li].astype(BF16) for n in big], collective_id=li, name=f"gather_{li}")
        gathered[li] = dict(zip(big, lands))

    def matmul_weights(li, part, after):
        return {local_key.get(n, n): _full_from_gathered(n, gathered[li][n]) for n in gather_names[part]}, None

    mid_names = ("w_ple_gate", "w_ple_proj", "w_down", "w_up")
    end_names = ("w_out", "w_in")
    landed = {}

    def start_exchange(li, names, g, tag, cid):
        slots = [_slots_from_full(n, g[local_key.get(n, n)]) for n in names]
        lands = _sc_exchange(slots, scatter=True, collective_id=cid, name=f"grads_{li}_{tag}")
        landed.update({(n, li): land for n, land in zip(names, lands)})

    def grads_ready(li, stage, g):
        if li == 0:
            start_exchange(li, mid_names if stage == "mid" else end_names, g, stage, 5 if stage == "mid" else 6)
        elif stage == "end":
            start_exchange(li, mid_names + end_names, g, stage, 4)
        return None

    loss_part, dx, grads, d_rel_bias = _local_step(
        x.reshape(t, D_MODEL), p.reshape(DEPTH * t, PLE_DIM), loss_target.reshape(t, D_MODEL), rel_bias, wts,
        matmul_weights, grads_ready)
    loss = lax.psum(loss_part, ("x", "y", "c"))

    def local_grad(n):
        if n == "rel_bias":
            return d_rel_bias
        key = LOCAL_GRAD_KEY.get(n, n)
        return jnp.stack([grads[li][key].reshape(wt[n].shape[1:]) if n in REPLICATED else grads[li][key]
                          for li in range(DEPTH)])

    out_g, out_d, out_m, out_v = {}, {}, {}, {}
    for n in big:
        shp = wt[n].shape
        two_d = lambda a: a.reshape(-1, shp[-1])
        res = _adamw_reduce([landed[n, li] for li in range(DEPTH)], two_d(wt[n]), two_d(mom_m[n]), two_d(mom_v[n]),
                            tr=32 if n == "w_down" else 128, name="adamw_" + n)
        out_g[n], out_d[n], out_m[n], out_v[n] = [r.reshape(shp) for r in res]

    small_names = REPLICATED + SMALL_SHARDED
    small_full_shapes = [wt[n].shape if n in REPLICATED else full[n].shape for n in small_names]
    reduced = _allgather_vmem(_pack_rows([local_grad(n) for n in small_names]), reduce=True, name="allreduce_small")
    reduced = dict(zip(small_names, _unpack_rows(reduced, small_full_shapes)))
    rep_shapes = [wt[n].shape for n in REPLICATED]
    upd = _adamw_plain(_pack_rows([reduced[n] for n in REPLICATED]), _pack_rows([wt[n] for n in REPLICATED]),
                       _pack_rows([mom_m[n] for n in REPLICATED]), _pack_rows([mom_v[n] for n in REPLICATED]),
                       name="adamw_replicated")
    for dst, packed in zip((out_d, out_m, out_v), upd):
        dst.update(zip(REPLICATED, _unpack_rows(packed, rep_shapes)))
    for n in REPLICATED:
        out_g[n] = reduced[n]
    for n in SMALL_SHARDED:
        shp = wt[n].shape
        g = reduced[n].reshape(shp[0], shp[1], N_DEV, shp[2])
        g = lax.dynamic_index_in_dim(g, me, axis=2, keepdims=False)
        two_d = lambda a: a.reshape(-1, shp[-1])
        res = _adamw_plain(two_d(g), two_d(wt[n]), two_d(mom_m[n]), two_d(mom_v[n]), name="adamw_" + n)
        out_g[n] = g
        out_d[n], out_m[n], out_v[n] = [r.reshape(shp) for r in res]

    return (loss, dx.reshape(bl, SEQ, D_MODEL), *[out_g[n] for n in WEIGHT_NAMES], *[out_d[n] for n in WEIGHT_NAMES],
            *[out_m[n] for n in WEIGHT_NAMES], *[out_v[n] for n in WEIGHT_NAMES])
```

```python
import math

import jax
import jax.numpy as jnp
import numpy as np
from jax import lax
from jax.experimental import pallas as pl
from jax.experimental.pallas import tpu as pltpu
from jax.experimental.pallas import tpu_sc as plsc

F32 = jnp.float32
BF16 = jnp.bfloat16
HI = lax.Precision.HIGHEST

N_DEV = 8
D_MODEL = 1024
SEQ = 2048
DEPTH = 2
HEAD_DIM = 64
IN_WIDTH = 2304
D_FF = 2816
PLE_DIM = 256
C_CHUNK = 128
C_GROUPS = 4
DILATED_CFGS = ((128, 1), (512, 4), (2048, 16))
DILATIONS = tuple(d for _, d in DILATED_CFGS)
A_RADIUS = 64
SWA_RADIUS = 128
BAND_BLOCK = 256
GRID_W = 64
ROPE_THETA = 10000.0
REL_BUCKETS = 32
REL_MAX_DIST = 1024
EPS = 1e-6
NEG_INF = -1e30
ATTN_SCALE = HEAD_DIM ** -0.5
LANES = 128

ADAM_LR = 0.001
ADAM_B1 = 0.9
ADAM_B2 = 0.999
ADAM_EPS = 1e-08
ADAM_WD = 0.01
ADAM_STEP = 10

MESH = pl.DeviceIdType.MESH
NT = (((1,), (1,)), ((), ()))
TN = (((0,), (0,)), ((), ()))
ARB = "arbitrary"
PAR = "parallel"


def _cparams(*sem):
    return pltpu.CompilerParams(dimension_semantics=tuple(sem))


def _sds(shape, dtype):
    return jax.ShapeDtypeStruct(tuple(shape), dtype)


def _group_sum_matrix(n, same_group):
    r = lax.broadcasted_iota(jnp.int32, (n, n), 0)
    c = lax.broadcasted_iota(jnp.int32, (n, n), 1)
    if same_group:
        return ((r >> 6) == (c >> 6)).astype(F32)
    return ((r & 63) == (c & 63)).astype(F32)


def _seg_sum(x, e):
    return jnp.dot(x, e, precision=HI, preferred_element_type=F32)


def _gelu(x):
    c = math.sqrt(2.0 / math.pi)
    return 0.5 * x * (1.0 + jnp.tanh(c * (x + 0.044715 * (x * x * x))))


def _gelu_grad(x):
    c = math.sqrt(2.0 / math.pi)
    t = jnp.tanh(c * (x + 0.044715 * (x * x * x)))
    return 0.5 * (1.0 + t) + 0.5 * x * (1.0 - t * t) * c * (1.0 + 3.0 * 0.044715 * (x * x))


def _sigmoid(x):
    return 1.0 / (1.0 + jnp.exp(-x))


def _scatter_cols(scratch, first, val):
    for c in range(val.shape[1] // LANES):
        scratch[first + c] = val[:, c * LANES:(c + 1) * LANES]


def _gather_cols(scratch, first, ncol):
    return jnp.concatenate([scratch[first + c] for c in range(ncol)], axis=1)


def _read_residue(scratch, first, ncol, r, d):
    n = scratch.shape[1] // d
    return jnp.concatenate([scratch.at[first + c][pl.ds(r, n, stride=d), :] for c in range(ncol)], axis=1)


def _write_residue(scratch, first, r, d, val):
    n = scratch.shape[1] // d
    for c in range(val.shape[1] // LANES):
        scratch.at[first + c][pl.ds(r, n, stride=d), :] = val[:, c * LANES:(c + 1) * LANES]


def _norm_mm(xs, gain, w, res, *, tm, tn, name, out_dtype=F32):
    t = xs[0].shape[0]
    k = sum(x.shape[1] for x in xs)
    n = w.shape[1]
    ng = len(xs)
    has_res = res is not None

    def body(*refs):
        x_refs = refs[:ng]
        g_ref, w_ref = refs[ng], refs[ng + 1]
        res_ref = refs[ng + 2] if has_res else None
        hn_ref, o_ref, hn_s = refs[ng + 2 + has_res:]

        @pl.when(pl.program_id(1) == 0)
        def _():
            off = 0
            for xr in x_refs:
                x = xr[...]
                wd = x.shape[1]
                r = lax.rsqrt(jnp.mean(x * x, axis=-1, keepdims=True) + EPS)
                hn_s[:, off:off + wd] = (x * r * g_ref[:, off:off + wd]).astype(BF16)
                off += wd
            hn_ref[...] = hn_s[...]

        acc = jnp.dot(hn_s[...], w_ref[...], preferred_element_type=F32)
        if has_res:
            acc = acc + res_ref[...]
        o_ref[...] = acc.astype(out_dtype)

    in_specs = [pl.BlockSpec((tm, x.shape[1]), lambda i, j: (i, 0)) for x in xs]
    in_specs += [pl.BlockSpec((1, k), lambda i, j: (0, 0)), pl.BlockSpec((k, tn), lambda i, j: (0, j))]
    args = list(xs) + [gain, w]
    if has_res:
        in_specs.append(pl.BlockSpec((tm, tn), lambda i, j: (i, j)))
        args.append(res)
    return pl.pallas_call(
        body, name=name, grid=(t // tm, n // tn), in_specs=in_specs,
        out_specs=[pl.BlockSpec((tm, k), lambda i, j: (i, 0)), pl.BlockSpec((tm, tn), lambda i, j: (i, j))],
        out_shape=[_sds((t, k), BF16), _sds((t, n), out_dtype)],
        scratch_shapes=[pltpu.VMEM((tm, k), BF16)],
        compiler_params=_cparams(PAR, ARB),
    )(*args)


def _mm(a, b, mode, res, *, tm, tn, out_dtype, name, a_rows=None):
    if mode == "tn":
        kk, m = a.shape
        blk_a = 0
        if a_rows is not None:
            blk_a, kk = a_rows
        a_spec = pl.BlockSpec((kk, tm), lambda i, j: (blk_a, i))
    else:
        m, kk = a.shape
        a_spec = pl.BlockSpec((tm, kk), lambda i, j: (i, 0))
    if mode == "nt":
        n = b.shape[0]
        b_spec = pl.BlockSpec((tn, kk), lambda i, j: (j, 0))
    else:
        n = b.shape[1]
        b_spec = pl.BlockSpec((kk, tn), lambda i, j: (0, j))
    has_res = res is not None

    def body(*refs):
        a_ref, b_ref = refs[0], refs[1]
        o_ref = refs[-1]
        av = a_ref[...].astype(BF16)
        bv = b_ref[...].astype(BF16)
        if mode == "nn":
            acc = jnp.dot(av, bv, preferred_element_type=F32)
        elif mode == "nt":
            acc = lax.dot_general(av, bv, NT, preferred_element_type=F32)
        else:
            acc = lax.dot_general(av, bv, TN, preferred_element_type=F32)
        if has_res:
            acc = acc + refs[2][...]
        o_ref[...] = acc.astype(out_dtype)

    in_specs = [a_spec, b_spec]
    args = [a, b]
    if has_res:
        in_specs.append(pl.BlockSpec((tm, tn), lambda i, j: (i, j)))
        args.append(res)
    return pl.pallas_call(
        body, name=name, grid=(m // tm, n // tn), in_specs=in_specs,
        out_specs=pl.BlockSpec((tm, tn), lambda i, j: (i, j)),
        out_shape=_sds((m, n), out_dtype),
        compiler_params=_cparams(PAR, PAR),
    )(*args)


def _mm_bt_normbwd(dys, w, xs, gain, dres, *, tm, tn, name, emit_bf16=False):
    t, wd_each = dys[0].shape
    nd = len(dys)
    per = wd_each // tn
    nj = nd * per
    k = w.shape[0]
    ng = len(xs)
    has_res = dres is not None

    def body(*refs):
        dy_refs = refs[:nd]
        w_ref = refs[nd]
        x_refs = refs[nd + 1:nd + 1 + ng]
        g_ref = refs[nd + 1 + ng]
        dres_ref = refs[nd + 2 + ng] if has_res else None
        outs = refs[nd + 2 + ng + has_res:]
        dx_ref = outs[0]
        dxb_ref = outs[1] if emit_bf16 else None
        dg_ref, acc = outs[1 + emit_bf16:]
        i, j = pl.program_id(0), pl.program_id(1)

        @pl.when(j == 0)
        def _():
            acc[...] = jnp.zeros_like(acc)

        for d, dy_ref in enumerate(dy_refs):
            @pl.when((j >= d * per) & (j < (d + 1) * per))
            def _(dy_ref=dy_ref):
                acc[...] += lax.dot_general(dy_ref[...].astype(BF16), w_ref[...], NT, preferred_element_type=F32)

        @pl.when(j == nj - 1)
        def _():
            @pl.when(i == 0)
            def _():
                dg_ref[...] = jnp.zeros_like(dg_ref)

            off = 0
            for xr in x_refs:
                x = xr[...]
                wd = x.shape[1]
                g = g_ref[:, off:off + wd]
                dyn = acc[:, off:off + wd]
                r = lax.rsqrt(jnp.mean(x * x, axis=-1, keepdims=True) + EPS)
                gdy = dyn * g
                dx = r * gdy - x * (r * r * r * jnp.mean(gdy * x, axis=-1, keepdims=True))
                if has_res:
                    dx = dx + dres_ref[:, off:off + wd]
                dx_ref[:, off:off + wd] = dx
                if emit_bf16:
                    dxb_ref[:, off:off + wd] = dx.astype(BF16)
                dg_ref[:, off:off + wd] += jnp.sum(dyn * x * r, axis=0, keepdims=True)
                off += wd

    def dy_map(d):
        return lambda i, j: (i, jnp.clip(j - d * per, 0, per - 1))

    in_specs = [pl.BlockSpec((tm, tn), dy_map(d)) for d in range(nd)]
    in_specs.append(pl.BlockSpec((k, tn), lambda i, j: (0, j)))
    in_specs += [pl.BlockSpec((tm, x.shape[1]), lambda i, j: (i, 0)) for x in xs]
    in_specs.append(pl.BlockSpec((1, k), lambda i, j: (0, 0)))
    args = list(dys) + [w] + list(xs) + [gain]
    if has_res:
        in_specs.append(pl.BlockSpec((tm, k), lambda i, j: (i, 0)))
        args.append(dres)
    row = pl.BlockSpec((tm, k), lambda i, j: (i, 0))
    out_specs = [row] + ([row] if emit_bf16 else []) + [pl.BlockSpec((1, k), lambda i, j: (0, 0))]
    out_shape = [_sds((t, k), F32)] + ([_sds((t, k), BF16)] if emit_bf16 else []) + [_sds((1, k), F32)]
    return pl.pallas_call(
        body, name=name, grid=(t // tm, nj), in_specs=in_specs, out_specs=out_specs, out_shape=out_shape,
        scratch_shapes=[pltpu.VMEM((tm, k), F32)],
        compiler_params=_cparams(ARB, ARB),
    )(*args)


def _rope_partner(y):
    n = y.shape[1]
    lane = lax.broadcasted_iota(jnp.int32, y.shape, 1)
    return jnp.where((lane & 31) < 16, pltpu.roll(y, n - 16, 1), pltpu.roll(y, 16, 1))


def _residue_specs(tm, width, nt):
    specs = [pl.BlockSpec((tm, width), lambda b, i: (b * nt + i, 0))]
    for d in DILATIONS[1:]:
        specs.append(pl.BlockSpec((None, d, tm // d, width), lambda b, i: (b, 0, i, 0)))
    return specs


def _residue_shapes(bl, width, dtype):
    return [_sds((bl * SEQ, width), dtype)] + [_sds((bl, d, SEQ // d, width), dtype) for d in DILATIONS[1:]]


def _qkprep_fwd(proj, gains, cos, sins, *, tm, name):
    t = proj.shape[0]
    bl = t // SEQ
    nt = SEQ // tm

    def body(p_ref, g_ref, c_ref, s_ref, qa1_ref, qa4_ref, qa16_ref, qb_ref, qd_ref, scr):
        e = _group_sum_matrix(256, True)

        def hn(x, row):
            wd = x.shape[1]
            ms = _seg_sum(x * x, e[:wd, :wd]) * (1.0 / HEAD_DIM)
            return x * lax.rsqrt(ms + EPS) * g_ref[row:row + 1, :wd]

        qa = jnp.concatenate([hn(p_ref[:, 0:256], 0) * ATTN_SCALE, hn(p_ref[:, 256:512], 1), p_ref[:, 512:768]], axis=1)
        qa1_ref[...] = qa.astype(BF16)
        _scatter_cols(scr, 0, qa)
        for d, ref in ((4, qa4_ref), (16, qa16_ref)):
            for r in range(d):
                ref[r] = _read_residue(scr, 0, 6, r, d).astype(BF16)
        qb_ref[:, 0:256] = (hn(p_ref[:, 768:1024], 2) * ATTN_SCALE).astype(BF16)
        qb_ref[:, 256:384] = hn(p_ref[:, 1024:1152], 3).astype(BF16)
        qb_ref[:, 384:512] = p_ref[:, 1152:1280].astype(BF16)
        yq = hn(p_ref[:, 1792:2048], 4)
        yq = yq * c_ref[...] + _rope_partner(yq) * s_ref[...]
        qd_ref[:, 0:256] = (yq * ATTN_SCALE).astype(BF16)
        yk = hn(p_ref[:, 2048:2176], 5)
        yk = yk * c_ref[:, 0:128] + _rope_partner(yk) * s_ref[:, 0:128]
        qd_ref[:, 256:384] = yk.astype(BF16)
        qd_ref[:, 384:512] = p_ref[:, 2176:2304].astype(BF16)

    row = lambda width: pl.BlockSpec((tm, width), lambda b, i: (b * nt + i, 0))
    tab = pl.BlockSpec((tm, 256), lambda b, i: (i, 0))
    return pl.pallas_call(
        body, name=name, grid=(bl, nt),
        in_specs=[row(IN_WIDTH), pl.BlockSpec((8, 256), lambda b, i: (0, 0)), tab, tab],
        out_specs=_residue_specs(tm, 768, nt) + [row(512), row(512)],
        out_shape=_residue_shapes(bl, 768, BF16) + [_sds((t, 512), BF16), _sds((t, 512), BF16)],
        scratch_shapes=[pltpu.VMEM((6, tm, LANES), F32)],
        compiler_params=_cparams(PAR, PAR),
    )(proj, gains, cos, sins)


def _qkprep_bwd(proj, da, db, dd, dcu, dcv, gains, cos, sins, *, tm, name):
    t = proj.shape[0]
    bl = t // SEQ
    nt = SEQ // tm
    flat = [a for cfg in da for a in cfg] + list(db) + list(dd) + [dcu, dcv]

    def body(*refs):
        p_ref, g_ref, c_ref, s_ref = refs[:4]
        d_refs = refs[4:4 + len(flat)]
        dp_ref, dg_ref, scr = refs[4 + len(flat):]
        a_refs = d_refs[:9]
        dqb_ref, dkb_ref, dvb_ref, dqd_ref, dkd_ref, dvd_ref, dcu_ref, dcv_ref = d_refs[9:]
        e = _group_sum_matrix(256, True)
        first = (pl.program_id(0) == 0) & (pl.program_id(1) == 0)
        last = (pl.program_id(0) == bl - 1) & (pl.program_id(1) == nt - 1)

        @pl.when(first)
        def _():
            dg_ref[...] = jnp.zeros_like(dg_ref)

        def hn_bwd(x, dy, row):
            wd = x.shape[1]
            ee = e[:wd, :wd]
            g = g_ref[row:row + 1, :wd]
            r = lax.rsqrt(_seg_sum(x * x, ee) * (1.0 / HEAD_DIM) + EPS)
            gdy = dy * g
            dx = r * gdy - x * (r * r * r * (_seg_sum(gdy * x, ee) * (1.0 / HEAD_DIM)))
            dg_ref[row:row + 1, :wd] += jnp.sum(dy * x * r, axis=0, keepdims=True)
            return dx

        def rope_bwd(dy, wd):
            return dy * c_ref[:, :wd] + _rope_partner(dy * s_ref[:, :wd])

        dqkv = jnp.concatenate([a_refs[0][...], a_refs[1][...], a_refs[2][...]], axis=1)
        for ci, d in ((1, 4), (2, 16)):
            for r in range(d):
                part = jnp.concatenate([a_refs[3 * ci + m][r] for m in range(3)], axis=1)
                _write_residue(scr, 0, r, d, part)
            dqkv = dqkv + _gather_cols(scr, 0, 6)
        dp_ref[:, 0:256] = hn_bwd(p_ref[:, 0:256], dqkv[:, 0:256] * ATTN_SCALE, 0).astype(BF16)
        dp_ref[:, 256:512] = hn_bwd(p_ref[:, 256:512], dqkv[:, 256:512], 1).astype(BF16)
        dp_ref[:, 512:768] = dqkv[:, 512:768].astype(BF16)
        dp_ref[:, 768:1024] = hn_bwd(p_ref[:, 768:1024], dqb_ref[...] * ATTN_SCALE, 2).astype(BF16)
        dp_ref[:, 1024:1152] = hn_bwd(p_ref[:, 1024:1152], dkb_ref[...], 3).astype(BF16)
        dp_ref[:, 1152:1280] = dvb_ref[...].astype(BF16)
        dp_ref[:, 1280:1536] = dcu_ref[...].astype(BF16)
        dp_ref[:, 1536:1792] = dcv_ref[...].astype(BF16)
        dp_ref[:, 1792:2048] = hn_bwd(p_ref[:, 1792:2048], rope_bwd(dqd_ref[...] * ATTN_SCALE, 256), 4).astype(BF16)
        dp_ref[:, 2048:2176] = hn_bwd(p_ref[:, 2048:2176], rope_bwd(dkd_ref[...], 128), 5).astype(BF16)
        dp_ref[:, 2176:2304] = dvd_ref[...].astype(BF16)

        @pl.when(last)
        def _():
            dg_ref[...] = _seg_sum(dg_ref[...], _group_sum_matrix(256, False))

    row = lambda width: pl.BlockSpec((tm, width), lambda b, i: (b * nt + i, 0))
    tab = pl.BlockSpec((tm, 256), lambda b, i: (i, 0))
    in_specs = [row(IN_WIDTH), pl.BlockSpec((8, 256), lambda b, i: (0, 0)), tab, tab]
    res_specs = _residue_specs(tm, 256, nt)
    in_specs += [res_specs[ci] for ci in range(3) for _ in range(3)]
    in_specs += [row(a.shape[1]) for a in flat[9:]]
    return pl.pallas_call(
        body, name=name, grid=(bl, nt), in_specs=in_specs,
        out_specs=[row(IN_WIDTH), pl.BlockSpec((8, 256), lambda b, i: (0, 0))],
        out_shape=[_sds((t, IN_WIDTH), BF16), _sds((8, 256), F32)],
        scratch_shapes=[pltpu.VMEM((6, tm, LANES), F32)],
        compiler_params=_cparams(ARB, ARB),
    )(proj, gains, cos, sins, *flat)


def _band_spec(seq_len, spec):
    width, idx = spec
    return pl.BlockSpec((None, None, seq_len, width), lambda b, r: (b, r, 0, idx))


def _fill_padded(dst, src_ref, rad, seq_len):
    z = jnp.zeros((rad, dst.shape[1]), dst.dtype)
    dst[0:rad, :] = z
    dst[rad + seq_len:rad + seq_len + rad, :] = z
    dst[rad:rad + seq_len, :] = src_ref[...]


def _band_fwd(src, qs, ks, vs, bias, sink, *, rad, nh, nkv, name):
    bl, dil, sl, _ = src.shape
    blk = bias.shape[1]
    kw = blk + 2 * rad
    nb = sl // blk
    rep = nh // nkv
    has_sink = sink is not None

    def body(*refs):
        q_ref, k_ref, v_ref, b_ref = refs[:4]
        s_ref = refs[4] if has_sink else None
        o_ref, l_ref, kp, vp = refs[4 + has_sink:]
        _fill_padded(kp, k_ref, rad, sl)
        _fill_padded(vp, v_ref, rad, sl)

        def blk_body(i, carry):
            r0 = pl.multiple_of(i * blk, blk)
            qb = q_ref[pl.ds(r0, blk), :]
            kwin = kp[pl.ds(r0, kw), :]
            vwin = vp[pl.ds(r0, kw), :]
            col = r0 - rad + lax.broadcasted_iota(jnp.int32, (blk, kw), 1)
            neg = jnp.where((col >= 0) & (col < sl), 0.0, NEG_INF).astype(F32)
            for h in range(nh):
                g = h // rep
                hs = slice(h * HEAD_DIM, (h + 1) * HEAD_DIM)
                gs = slice(g * HEAD_DIM, (g + 1) * HEAD_DIM)
                s = lax.dot_general(qb[:, hs], kwin[:, gs], NT, preferred_element_type=F32)
                s = s + b_ref[h] + neg
                m = jnp.max(s, axis=1, keepdims=True)
                if has_sink:
                    sk = s_ref[h][0:1, 0:1]
                    m = jnp.maximum(m, sk)
                p = jnp.exp(s - m)
                den = jnp.sum(p, axis=1, keepdims=True)
                if has_sink:
                    den = den + jnp.exp(sk - m)
                o = jnp.dot(p.astype(BF16), vwin[:, gs], preferred_element_type=F32) / den
                o_ref[pl.ds(r0, blk), hs] = o
                l_ref[pl.ds(r0, blk), hs] = jnp.broadcast_to(m + jnp.log(den), (blk, HEAD_DIM))
            return carry

        lax.fori_loop(0, nb, blk_body, 0)

    in_specs = [_band_spec(sl, qs), _band_spec(sl, ks), _band_spec(sl, vs),
                pl.BlockSpec((nh, blk, kw), lambda b, r: (0, 0, 0))]
    args = [src] * 3 + [bias]
    if has_sink:
        in_specs.append(pl.BlockSpec((nh, 8, 128), lambda b, r: (0, 0, 0)))
        args.append(sink)
    return pl.pallas_call(
        body, name=name, grid=(bl, dil), in_specs=in_specs,
        out_specs=[_band_spec(sl, (256, 0))] * 2,
        out_shape=[_sds((bl, dil, sl, 256), F32)] * 2,
        scratch_shapes=[pltpu.VMEM((sl + 2 * rad, ks[0]), BF16), pltpu.VMEM((sl + 2 * rad, vs[0]), BF16)],
        compiler_params=_cparams(PAR, PAR),
    )(*args)


def _band_bwd(src, qs, ks, vs, bias, sink, dy, dcol, lse, delta, *, rad, nh, nkv, name):
    bl, dil, sl, _ = src.shape
    blk = bias.shape[1]
    kw = blk + 2 * rad
    nb = sl // blk
    rep = nh // nkv
    has_sink = sink is not None
    wk, wv = ks[0], vs[0]

    def body(*refs):
        q_ref, k_ref, v_ref, b_ref = refs[:4]
        s_ref = refs[4] if has_sink else None
        do_ref, l_ref, dl_ref = refs[4 + has_sink:7 + has_sink]
        outs = refs[7 + has_sink:]
        if has_sink:
            dq_ref, dk_ref, dv_ref, db_ref, dsk_ref, kp, vp, dka, dva = outs
        else:
            dq_ref, dk_ref, dv_ref, db_ref, kp, vp, dka, dva = outs

        @pl.when((pl.program_id(0) == 0) & (pl.program_id(1) == 0))
        def _():
            db_ref[...] = jnp.zeros_like(db_ref)
            if has_sink:
                dsk_ref[...] = jnp.zeros_like(dsk_ref)

        _fill_padded(kp, k_ref, rad, sl)
        _fill_padded(vp, v_ref, rad, sl)
        dka[...] = jnp.zeros_like(dka)
        dva[...] = jnp.zeros_like(dva)

        def blk_body(i, carry):
            r0 = pl.multiple_of(i * blk, blk)
            qb = q_ref[pl.ds(r0, blk), :]
            kwin = kp[pl.ds(r0, kw), :]
            vwin = vp[pl.ds(r0, kw), :]
            dob = do_ref[pl.ds(r0, blk), :].astype(BF16)
            lb = l_ref[pl.ds(r0, blk), :]
            dlb = dl_ref[pl.ds(r0, blk), :]
            col = r0 - rad + lax.broadcasted_iota(jnp.int32, (blk, kw), 1)
            neg = jnp.where((col >= 0) & (col < sl), 0.0, NEG_INF).astype(F32)
            for h in range(nh):
                g = h // rep
                hs = slice(h * HEAD_DIM, (h + 1) * HEAD_DIM)
                gs = slice(g * HEAD_DIM, (g + 1) * HEAD_DIM)
                qh, kh, vh, doh = qb[:, hs], kwin[:, gs], vwin[:, gs], dob[:, hs]
                lh = lb[:, h * HEAD_DIM:h * HEAD_DIM + 1]
                dlh = dlb[:, h * HEAD_DIM:h * HEAD_DIM + 1]
                s = lax.dot_general(qh, kh, NT, preferred_element_type=F32) + b_ref[h] + neg
                p = jnp.exp(s - lh)
                dp = lax.dot_general(doh, vh, NT, preferred_element_type=F32)
                ds = p * (dp - dlh)
                dsb = ds.astype(BF16)
                dq_ref[pl.ds(r0, blk), hs] = jnp.dot(dsb, kh, preferred_element_type=F32)
                dka[pl.ds(r0, kw), gs] += lax.dot_general(dsb, qh, TN, preferred_element_type=F32)
                dva[pl.ds(r0, kw), gs] += lax.dot_general(p.astype(BF16), doh, TN, preferred_element_type=F32)
                db_ref[h] += ds
                if has_sink:
                    ps = jnp.exp(s_ref[h][0:1, 0:1] - lh)
                    dsk_ref[h] += jnp.broadcast_to(-jnp.sum(ps * dlh, axis=0, keepdims=True), (8, 128))
            return carry

        lax.fori_loop(0, nb, blk_body, 0)
        dk_ref[...] = dka[rad:rad + sl, :]
        dv_ref[...] = dva[rad:rad + sl, :]

    const3 = lambda b, r: (0, 0, 0)
    in_specs = [_band_spec(sl, qs), _band_spec(sl, ks), _band_spec(sl, vs), pl.BlockSpec((nh, blk, kw), const3)]
    args = [src] * 3 + [bias]
    if has_sink:
        in_specs.append(pl.BlockSpec((nh, 8, 128), const3))
        args.append(sink)
    row = _band_spec(sl, (256, 0))
    in_specs += [_band_spec(sl, (256, dcol)), row, row]
    args += [dy, lse, delta]
    out_specs = [row, _band_spec(sl, (wk, 0)), _band_spec(sl, (wv, 0)), pl.BlockSpec((nh, blk, kw), const3)]
    out_shape = [_sds((bl, dil, sl, 256), F32), _sds((bl, dil, sl, wk), F32), _sds((bl, dil, sl, wv), F32),
                 _sds((nh, blk, kw), F32)]
    if has_sink:
        out_specs.append(pl.BlockSpec((nh, 8, 128), const3))
        out_shape.append(_sds((nh, 8, 128), F32))
    return pl.pallas_call(
        body, name=name, grid=(bl, dil), in_specs=in_specs, out_specs=out_specs, out_shape=out_shape,
        scratch_shapes=[pltpu.VMEM((sl + 2 * rad, wk), BF16), pltpu.VMEM((sl + 2 * rad, wv), BF16),
                        pltpu.VMEM((sl + 2 * rad, wk), F32), pltpu.VMEM((sl + 2 * rad, wv), F32)],
        compiler_params=_cparams(ARB, ARB),
    )(*args)


def _combine_a(os_, ls_, *, tm, name):
    bl = os_[1].shape[0]
    t = bl * SEQ
    nt = SEQ // tm

    def body(o1, o4, o16, l1, l4, l16, y_ref, lt_ref, scr):
        for k, (d, ref) in enumerate(((4, o4), (16, o16), (4, l4), (16, l16))):
            for r in range(d):
                _write_residue(scr, 2 * k, r, d, ref[r])
        o2, o3, b, c = (_gather_cols(scr, 2 * k, 2) for k in range(4))
        a = l1[...]
        m = jnp.maximum(jnp.maximum(a, b), c)
        ea, eb, ec = jnp.exp(a - m), jnp.exp(b - m), jnp.exp(c - m)
        den = ea + eb + ec
        y_ref[...] = (ea / den) * o1[...] + (eb / den) * o2 + (ec / den) * o3
        lt_ref[...] = m + jnp.log(den)

    specs = _residue_specs(tm, 256, nt)
    return pl.pallas_call(
        body, name=name, grid=(bl, nt), in_specs=specs * 2, out_specs=[specs[0]] * 2,
        out_shape=[_sds((t, 256), F32)] * 2, scratch_shapes=[pltpu.VMEM((8, tm, LANES), F32)],
        compiler_params=_cparams(PAR, PAR),
    )(*os_, *ls_)


def _deltas(dycat, ya, yb, yd, lse_a, *, tm, name):
    t = ya.shape[0]
    bl = t // SEQ
    nt = SEQ // tm

    def body(dy_ref, ya_ref, yb_ref, yd_ref, la_ref, dy4, dy16, l4, l16, da1, da4, da16, db_ref, dd_ref, scr):
        e = _group_sum_matrix(256, True)
        dya = dy_ref[:, 0:256]
        dla = _seg_sum(dya * ya_ref[...], e)
        da1[...] = dla
        db_ref[...] = _seg_sum(dy_ref[:, 256:512] * yb_ref[...], e)
        dd_ref[...] = _seg_sum(dy_ref[:, 768:1024] * yd_ref[...], e)
        for k, (val, r4, r16) in enumerate(((dya, dy4, dy16), (la_ref[...], l4, l16), (dla, da4, da16))):
            _scatter_cols(scr, 2 * k, val)
            for d, ref in ((4, r4), (16, r16)):
                for r in range(d):
                    ref[r] = _read_residue(scr, 2 * k, 2, r, d)

    specs = _residue_specs(tm, 256, nt)
    nat = specs[0]
    shapes = _residue_shapes(bl, 256, F32)
    outs = pl.pallas_call(
        body, name=name, grid=(bl, nt),
        in_specs=[pl.BlockSpec((tm, 1024), lambda b, i: (b * nt + i, 0)), nat, nat, nat, nat],
        out_specs=specs[1:] + specs[1:] + specs + [nat, nat],
        out_shape=shapes[1:] + shapes[1:] + shapes + [shapes[0], shapes[0]],
        scratch_shapes=[pltpu.VMEM((6, tm, LANES), F32)],
        compiler_params=_cparams(PAR, PAR),
    )(dycat, ya, yb, yd, lse_a)
    return outs[0:2], outs[2:4], outs[4:7], outs[7], outs[8]


def _dense_fwd(qd, *, tq, name):
    t = qd.shape[0]
    bl = t // SEQ
    nq = SEQ // tq

    def body(q_ref, k_ref, v_ref, o_ref, l_ref):
        q = q_ref[...]
        for g in range(2):
            h0, h1 = 2 * g, 2 * g + 1
            q2 = jnp.concatenate([q[:, h0 * 64:(h0 + 1) * 64], q[:, h1 * 64:(h1 + 1) * 64]], axis=0)
            kg = k_ref[:, g * 64:(g + 1) * 64]
            vg = v_ref[:, g * 64:(g + 1) * 64]
            s = lax.dot_general(q2, kg, NT, preferred_element_type=F32)
            m = jnp.max(s, axis=1, keepdims=True)
            p = jnp.exp(s - m)
            den = jnp.sum(p, axis=1, keepdims=True)
            o2 = jnp.dot(p.astype(BF16), vg, preferred_element_type=F32) / den
            l2 = jnp.broadcast_to(m + jnp.log(den), (2 * tq, 64))
            o_ref[:, h0 * 64:(h0 + 1) * 64] = o2[:tq]
            o_ref[:, h1 * 64:(h1 + 1) * 64] = o2[tq:]
            l_ref[:, h0 * 64:(h0 + 1) * 64] = l2[:tq]
            l_ref[:, h1 * 64:(h1 + 1) * 64] = l2[tq:]

    q3 = qd.reshape(bl, SEQ, 512)
    o, lse = pl.pallas_call(
        body, name=name, grid=(bl, nq),
        in_specs=[pl.BlockSpec((None, tq, 256), lambda b, i: (b, i, 0)),
                  pl.BlockSpec((None, SEQ, 128), lambda b, i: (b, 0, 2)),
                  pl.BlockSpec((None, SEQ, 128), lambda b, i: (b, 0, 3))],
        out_specs=[pl.BlockSpec((None, tq, 256), lambda b, i: (b, i, 0))] * 2,
        out_shape=[_sds((bl, SEQ, 256), F32)] * 2,
        compiler_params=_cparams(PAR, PAR),
    )(q3, q3, q3)
    return o.reshape(t, 256), lse.reshape(t, 256)


def _dense_bwd(qd, dycat, lse, delta, *, tq, name):
    t = qd.shape[0]
    bl = t // SEQ
    nq = SEQ // tq

    def body(q_ref, k_ref, v_ref, do_ref, l_ref, dl_ref, dq_ref, dk_ref, dv_ref):
        @pl.when(pl.program_id(1) == 0)
        def _():
            dk_ref[...] = jnp.zeros_like(dk_ref)
            dv_ref[...] = jnp.zeros_like(dv_ref)

        q = q_ref[...]
        do = do_ref[...].astype(BF16)
        lv = l_ref[...]
        dlv = dl_ref[...]
        for g in range(2):
            h0, h1 = 2 * g, 2 * g + 1
            q2 = jnp.concatenate([q[:, h0 * 64:(h0 + 1) * 64], q[:, h1 * 64:(h1 + 1) * 64]], axis=0)
            do2 = jnp.concatenate([do[:, h0 * 64:(h0 + 1) * 64], do[:, h1 * 64:(h1 + 1) * 64]], axis=0)
            l2 = jnp.concatenate([lv[:, h0 * 64:h0 * 64 + 1], lv[:, h1 * 64:h1 * 64 + 1]], axis=0)
            dl2 = jnp.concatenate([dlv[:, h0 * 64:h0 * 64 + 1], dlv[:, h1 * 64:h1 * 64 + 1]], axis=0)
            kg = k_ref[:, g * 64:(g + 1) * 64]
            vg = v_ref[:, g * 64:(g + 1) * 64]
            s = lax.dot_general(q2, kg, NT, preferred_element_type=F32)
            p = jnp.exp(s - l2)
            dp = lax.dot_general(do2, vg, NT, preferred_element_type=F32)
            ds = (p * (dp - dl2)).astype(BF16)
            dq2 = jnp.dot(ds, kg, preferred_element_type=F32)
            dq_ref[:, h0 * 64:(h0 + 1) * 64] = dq2[:tq]
            dq_ref[:, h1 * 64:(h1 + 1) * 64] = dq2[tq:]
            dk_ref[:, g * 64:(g + 1) * 64] += lax.dot_general(ds, q2, TN, preferred_element_type=F32)
            dv_ref[:, g * 64:(g + 1) * 64] += lax.dot_general(p.astype(BF16), do2, TN, preferred_element_type=F32)

    q3 = qd.reshape(bl, SEQ, 512)
    tile = pl.BlockSpec((None, tq, 256), lambda b, i: (b, i, 0))
    full = pl.BlockSpec((None, SEQ, 128), lambda b, i: (b, 0, 0))
    dq, dk, dv = pl.pallas_call(
        body, name=name, grid=(bl, nq),
        in_specs=[tile, pl.BlockSpec((None, SEQ, 128), lambda b, i: (b, 0, 2)),
                  pl.BlockSpec((None, SEQ, 128), lambda b, i: (b, 0, 3)),
                  pl.BlockSpec((None, tq, 256), lambda b, i: (b, i, 3)), tile, tile],
        out_specs=[tile, full, full],
        out_shape=[_sds((bl, SEQ, 256), F32), _sds((bl, SEQ, 128), F32), _sds((bl, SEQ, 128), F32)],
        compiler_params=_cparams(PAR, ARB),
    )(q3, q3, q3, dycat.reshape(bl, SEQ, 1024), lse.reshape(bl, SEQ, 256), delta.reshape(bl, SEQ, 256))
    return dq.reshape(t, 256), dk.reshape(t, 128), dv.reshape(t, 128)


def _c_norm(cv, gam, bet):
    vg = _gelu(cv)
    mu = jnp.mean(vg, axis=-1, keepdims=True)
    xc = vg - mu
    r = lax.rsqrt(jnp.mean(xc * xc, axis=-1, keepdims=True) + EPS)
    xhat = xc * r
    return xhat * gam + bet, xhat, r


def _c_fwd(proj, gam, bet, ws, bst, *, tm, name):
    t = proj.shape[0]
    nch = tm // C_CHUNK

    def body(u_ref, v_ref, g_ref, b_ref, ws_ref, bs_ref, y_ref):
        vn, _, _ = _c_norm(v_ref[...], g_ref[...], b_ref[...])
        vnb = vn.astype(BF16)
        for c in range(nch):
            rows = slice(c * C_CHUNK, (c + 1) * C_CHUNK)
            for g in range(C_GROUPS):
                gs = slice(g * 64, (g + 1) * 64)
                mixed = jnp.dot(ws_ref[g], vnb[rows, gs], preferred_element_type=F32) + bs_ref[:, gs]
                y_ref[rows, gs] = _gelu(u_ref[rows, gs]) * mixed

    vec = pl.BlockSpec((1, 256), lambda i: (0, 0))
    return pl.pallas_call(
        body, name=name, grid=(t // tm,),
        in_specs=[pl.BlockSpec((tm, 256), lambda i: (i, 5)), pl.BlockSpec((tm, 256), lambda i: (i, 6)), vec, vec,
                  pl.BlockSpec((C_GROUPS, C_CHUNK, C_CHUNK), lambda i: (0, 0, 0)),
                  pl.BlockSpec((C_CHUNK, 256), lambda i: (0, 0))],
        out_specs=pl.BlockSpec((tm, 256), lambda i: (i, 0)), out_shape=_sds((t, 256), F32),
        compiler_params=_cparams(PAR),
    )(proj, proj, gam, bet, ws, bst)


def _c_bwd(proj, dycat, gam, bet, ws, wst, bst, *, tm, name):
    t = proj.shape[0]
    nch = tm // C_CHUNK
    nstep = t // tm

    def body(u_ref, v_ref, dy_ref, g_ref, b_ref, ws_ref, wst_ref, bs_ref,
             du_ref, dv_ref, dws_ref, dbs_ref, dg_ref, db_ref, dvn_s):
        step = pl.program_id(0)

        @pl.when(step == 0)
        def _():
            dws_ref[...] = jnp.zeros_like(dws_ref)
            dbs_ref[...] = jnp.zeros_like(dbs_ref)
            dg_ref[...] = jnp.zeros_like(dg_ref)
            db_ref[...] = jnp.zeros_like(db_ref)

        cv = v_ref[...]
        gam_v = g_ref[...]
        vn, xhat, r = _c_norm(cv, gam_v, b_ref[...])
        vnb = vn.astype(BF16)
        for c in range(nch):
            rows = slice(c * C_CHUNK, (c + 1) * C_CHUNK)
            for g in range(C_GROUPS):
                gs = slice(g * 64, (g + 1) * 64)
                cu = u_ref[rows, gs]
                dy = dy_ref[rows, gs]
                mixed = jnp.dot(ws_ref[g], vnb[rows, gs], preferred_element_type=F32) + bs_ref[:, gs]
                du_ref[rows, gs] = dy * mixed * _gelu_grad(cu)
                dmix = dy * _gelu(cu)
                dbs_ref[:, gs] += dmix
                dmb = dmix.astype(BF16)
                dws_ref[g] += lax.dot_general(dmb, vnb[rows, gs], NT, preferred_element_type=F32)
                dvn_s[rows, gs] = jnp.dot(wst_ref[g], dmb, preferred_element_type=F32)
        dvn = dvn_s[...]
        dg_ref[...] += jnp.sum(dvn * xhat, axis=0, keepdims=True)
        db_ref[...] += jnp.sum(dvn, axis=0, keepdims=True)
        dxh = dvn * gam_v
        dvg = r * (dxh - jnp.mean(dxh, axis=-1, keepdims=True) - xhat * jnp.mean(dxh * xhat, axis=-1, keepdims=True))
        dv_ref[...] = dvg * _gelu_grad(cv)

        @pl.when(step == nstep - 1)
        def _():
            dbs_ref[...] = _seg_sum(dbs_ref[...], _group_sum_matrix(256, True))

    vec = pl.BlockSpec((1, 256), lambda i: (0, 0))
    mat = pl.BlockSpec((C_GROUPS, C_CHUNK, C_CHUNK), lambda i: (0, 0, 0))
    bsp = pl.BlockSpec((C_CHUNK, 256), lambda i: (0, 0))
    tile = pl.BlockSpec((tm, 256), lambda i: (i, 0))
    return pl.pallas_call(
        body, name=name, grid=(nstep,),
        in_specs=[pl.BlockSpec((tm, 256), lambda i: (i, 5)), pl.BlockSpec((tm, 256), lambda i: (i, 6)),
                  pl.BlockSpec((tm, 256), lambda i: (i, 2)), vec, vec, mat, mat, bsp],
        out_specs=[tile, tile, mat, bsp, vec, vec],
        out_shape=[_sds((t, 256), F32), _sds((t, 256), F32), _sds((C_GROUPS, C_CHUNK, C_CHUNK), F32),
                   _sds((C_CHUNK, 256), F32), _sds((1, 256), F32), _sds((1, 256), F32)],
        scratch_shapes=[pltpu.VMEM((tm, 256), F32)],
        compiler_params=_cparams(ARB),
    )(proj, proj, dycat, gam, bet, ws, wst, bst)


FF_TC = 128
FF_NB = D_FF // FF_TC
FF_CH = 64
FF_HALO = 16


def _edge_taps(ref, first):
    if first:
        ext = ref[0:FF_CH + FF_HALO, :].astype(F32)
        body = slice(0, FF_CH)
    else:
        ext = ref[SEQ - FF_CH - FF_HALO:SEQ, :].astype(F32)
        body = slice(FF_HALO, FF_HALO + FF_CH)
    n = ext.shape[0]
    row = lax.broadcasted_iota(jnp.int32, ext.shape, 0)
    dn = pltpu.roll(ext, 1, 0)
    up = pltpu.roll(ext, n - 1, 0)
    if first:
        dn = jnp.where(row == 0, 0.0, dn)
    else:
        up = jnp.where(row == n - 1, 0.0, up)
    return dn[body], ext[body], up[body]


def _mid_taps(ref, r0):
    ext = ref[pl.ds(pl.multiple_of(r0 - FF_HALO, FF_HALO), FF_CH + 2 * FF_HALO), :].astype(F32)
    n = ext.shape[0]
    body = slice(FF_HALO, FF_HALO + FF_CH)
    return pltpu.roll(ext, 1, 0)[body], ext[body], pltpu.roll(ext, n - 1, 0)[body]


def _chunk_loop(step):
    step(0, lambda ref: _edge_taps(ref, True))

    def mid(i, carry):
        r0 = pl.multiple_of(i * FF_CH, FF_CH)
        step(r0, lambda ref: _mid_taps(ref, r0))
        return carry

    lax.fori_loop(1, SEQ // FF_CH - 1, mid, 0)
    step(SEQ - FF_CH, lambda ref: _edge_taps(ref, False))


def _conv3(taps, w_ref, b_ref):
    dn, md, up = taps
    return w_ref[0:1, :] * dn + w_ref[1:2, :] * md + w_ref[2:3, :] * up + b_ref[...]


def _ff_specs(order):
    def at(fn):
        return (lambda b, j: fn(b, j)) if order == "bj" else (lambda j, b: fn(b, j))
    hs = [pl.BlockSpec((None, SEQ, FF_TC), at(lambda b, j, o=o: (b, 0, j + o))) for o in (0, FF_NB)]
    ws = [pl.BlockSpec((3, FF_TC), at(lambda b, j, o=o: (0, j + o))) for o in (0, FF_NB)]
    bs = [pl.BlockSpec((1, FF_TC), at(lambda b, j, o=o: (0, j + o))) for o in (0, FF_NB)]
    return hs, ws, bs


def _conv_gate_fwd(h, cw, cb, *, name):
    t = h.shape[0]
    bl = t // SEQ

    def body(hg_ref, hu_ref, wg_ref, wu_ref, bg_ref, bu_ref, a_ref):
        def step(r0, taps):
            cg = _conv3(taps(hg_ref), wg_ref, bg_ref)
            cu = _conv3(taps(hu_ref), wu_ref, bu_ref)
            a_ref[pl.ds(r0, FF_CH), :] = (cg * _sigmoid(cg) * cu).astype(BF16)

        _chunk_loop(step)

    hs, ws, bs = _ff_specs("bj")
    h3 = h.reshape(bl, SEQ, 2 * D_FF)
    act = pl.pallas_call(
        body, name=name, grid=(bl, FF_NB), in_specs=hs + ws + bs,
        out_specs=pl.BlockSpec((None, SEQ, FF_TC), lambda b, j: (b, 0, j)),
        out_shape=_sds((bl, SEQ, D_FF), BF16),
        compiler_params=_cparams(PAR, PAR),
    )(h3, h3, cw, cw, cb, cb)
    return act.reshape(t, D_FF)


def _conv_gate_bwd(h, dact, cw, cb, *, name):
    t = h.shape[0]
    bl = t // SEQ

    def body(hg_ref, hu_ref, wg_ref, wu_ref, bg_ref, bu_ref, da_ref,
             dhg_ref, dhu_ref, dwg_ref, dwu_ref, dbg_ref, dbu_ref, dg_s, du_s):
        @pl.when(pl.program_id(1) == 0)
        def _():
            for ref in (dwg_ref, dwu_ref, dbg_ref, dbu_ref):
                ref[...] = jnp.zeros_like(ref)

        red = lambda x: jnp.sum(x, axis=0, keepdims=True)

        def pass1(r0, taps):
            tg, tu = taps(hg_ref), taps(hu_ref)
            cg = _conv3(tg, wg_ref, bg_ref)
            cu = _conv3(tu, wu_ref, bu_ref)
            da = da_ref[pl.ds(r0, FF_CH), :].astype(F32)
            sg = _sigmoid(cg)
            dcg = da * cu * (sg * (1.0 + cg * (1.0 - sg)))
            dcu = da * (cg * sg)
            dg_s[pl.ds(r0, FF_CH), :] = dcg
            du_s[pl.ds(r0, FF_CH), :] = dcu
            for d, tp, dw_ref, db_ref in ((dcg, tg, dwg_ref, dbg_ref), (dcu, tu, dwu_ref, dbu_ref)):
                for k in range(3):
                    dw_ref[k:k + 1, :] += red(d * tp[k])
                db_ref[...] += red(d)

        _chunk_loop(pass1)

        def pass2(r0, taps):
            for s, w_ref, o_ref in ((dg_s, wg_ref, dhg_ref), (du_s, wu_ref, dhu_ref)):
                dn, md, up = taps(s)
                o_ref[pl.ds(r0, FF_CH), :] = (w_ref[0:1, :] * up + w_ref[1:2, :] * md + w_ref[2:3, :] * dn).astype(BF16)

        _chunk_loop(pass2)

    hs, ws, bs = _ff_specs("jb")
    half = pl.BlockSpec((None, SEQ, FF_TC), lambda j, b: (b, 0, j))
    wsp = pl.BlockSpec((3, FF_TC), lambda j, b: (0, j))
    bsp = pl.BlockSpec((1, FF_TC), lambda j, b: (0, j))
    h3 = h.reshape(bl, SEQ, 2 * D_FF)
    dhg, dhu, dwg, dwu, dbg, dbu = pl.pallas_call(
        body, name=name, grid=(FF_NB, bl), in_specs=hs + ws + bs + [half],
        out_specs=[half, half, wsp, wsp, bsp, bsp],
        out_shape=[_sds((bl, SEQ, D_FF), BF16), _sds((bl, SEQ, D_FF), BF16), _sds((3, D_FF), F32), _sds((3, D_FF), F32),
                   _sds((1, D_FF), F32), _sds((1, D_FF), F32)],
        scratch_shapes=[pltpu.VMEM((SEQ, FF_TC), F32), pltpu.VMEM((SEQ, FF_TC), F32)],
        compiler_params=_cparams(PAR, ARB),
    )(h3, h3, cw, cw, cb, cb, dact.reshape(bl, SEQ, D_FF))
    return (dhg.reshape(t, D_FF), dhu.reshape(t, D_FF), jnp.concatenate([dwg, dwu], axis=1),
            jnp.concatenate([dbg, dbu], axis=1))


def _ple_fwd(x2, gain, wg, pe, pe_blk, wp, *, tm, tn, name):
    t, k = x2.shape
    n = wg.shape[1]

    def body(x_ref, g_ref, wg_ref, pe_ref, wp_ref, xr_ref, hn_ref, x3_ref, gt_ref, pp_ref, hn_s):
        @pl.when(pl.program_id(1) == 0)
        def _():
            x = x_ref[...]
            r = lax.rsqrt(jnp.mean(x * x, axis=-1, keepdims=True) + EPS)
            hn_s[...] = (x * r * g_ref[...]).astype(BF16)
            hn_ref[...] = hn_s[...]

        gate = _sigmoid(jnp.dot(hn_s[...], wg_ref[...], preferred_element_type=F32))
        pp = jnp.dot(pe_ref[...].astype(BF16), wp_ref[...], preferred_element_type=F32)
        gt_ref[...] = gate
        pp_ref[...] = pp
        x3_ref[...] = xr_ref[...] + pp * gate

    tile = pl.BlockSpec((tm, tn), lambda i, j: (i, j))
    return pl.pallas_call(
        body, name=name, grid=(t // tm, n // tn),
        in_specs=[pl.BlockSpec((tm, k), lambda i, j: (i, 0)), pl.BlockSpec((1, k), lambda i, j: (0, 0)),
                  pl.BlockSpec((k, tn), lambda i, j: (0, j)), pl.BlockSpec((tm, PLE_DIM), lambda i, j: (pe_blk + i, 0)),
                  pl.BlockSpec((PLE_DIM, tn), lambda i, j: (0, j)), tile],
        out_specs=[pl.BlockSpec((tm, k), lambda i, j: (i, 0)), tile, tile, tile],
        out_shape=[_sds((t, k), BF16), _sds((t, n), F32), _sds((t, n), F32), _sds((t, n), F32)],
        scratch_shapes=[pltpu.VMEM((tm, k), BF16)],
        compiler_params=_cparams(PAR, ARB),
    )(x2, gain, wg, pe, wp, x2)


def _ple_bwd_ew(dx3, gate, pp, *, tm, name):
    t, n = dx3.shape

    def body(d_ref, g_ref, p_ref, dz_ref, dpp_ref):
        d, g = d_ref[...], g_ref[...]
        dz_ref[...] = (d * p_ref[...] * g * (1.0 - g)).astype(BF16)
        dpp_ref[...] = (d * g).astype(BF16)

    spec = pl.BlockSpec((tm, n), lambda i: (i, 0))
    return pl.pallas_call(
        body, name=name, grid=(t // tm,), in_specs=[spec] * 3, out_specs=[spec] * 2,
        out_shape=[_sds((t, n), BF16)] * 2, compiler_params=_cparams(PAR),
    )(dx3, gate, pp)


def _loss_head(y, tgt, *, tm, name):
    t, d = y.shape

    def body(y_ref, t_ref, l_ref, dy_ref):
        @pl.when(pl.program_id(0) == 0)
        def _():
            l_ref[...] = jnp.zeros_like(l_ref)

        e = y_ref[...] - t_ref[...]
        dy_ref[...] = e * (1.0 / d)
        s = jnp.sum(jnp.sum(e * e, axis=1, keepdims=True), axis=0, keepdims=True)
        l_ref[...] += jnp.broadcast_to(s * (0.5 / d), (8, 128))

    spec = pl.BlockSpec((tm, d), lambda i: (i, 0))
    return pl.pallas_call(
        body, name=name, grid=(t // tm,), in_specs=[spec, spec],
        out_specs=[pl.BlockSpec((8, 128), lambda i: (0, 0)), spec],
        out_shape=[_sds((8, 128), F32), _sds((t, d), F32)], compiler_params=_cparams(ARB),
    )(y, tgt)


BIAS_PC = 8192


def _onehot(bucket_row):
    rows = lax.broadcasted_iota(jnp.int32, (REL_BUCKETS, bucket_row.shape[1]), 0)
    return (rows == bucket_row).astype(F32)


def _bias_lookup(table_t, bucket, *, name):
    h = table_t.shape[0]
    p = bucket.shape[1]

    def body(t_ref, b_ref, o_ref):
        bk = b_ref[...]
        val = jnp.dot(t_ref[...], _onehot(bk), precision=HI, preferred_element_type=F32)
        o_ref[...] = jnp.where(bk >= 0, val, NEG_INF)

    return pl.pallas_call(
        body, name=name, grid=(p // BIAS_PC,),
        in_specs=[pl.BlockSpec((h, REL_BUCKETS), lambda i: (0, 0)), pl.BlockSpec((1, BIAS_PC), lambda i: (0, i))],
        out_specs=pl.BlockSpec((h, BIAS_PC), lambda i: (0, i)), out_shape=_sds((h, p), F32),
        compiler_params=_cparams(PAR),
    )(table_t, bucket)


def _bucket_reduce(dbias, bucket, *, name):
    h, p = dbias.shape

    def body(d_ref, b_ref, o_ref):
        @pl.when(pl.program_id(0) == 0)
        def _():
            o_ref[...] = jnp.zeros_like(o_ref)

        o_ref[...] += lax.dot_general(d_ref[...], _onehot(b_ref[...]), NT, precision=HI, preferred_element_type=F32)

    return pl.pallas_call(
        body, name=name, grid=(p // BIAS_PC,),
        in_specs=[pl.BlockSpec((h, BIAS_PC), lambda i: (0, i)), pl.BlockSpec((1, BIAS_PC), lambda i: (0, i))],
        out_specs=pl.BlockSpec((h, REL_BUCKETS), lambda i: (0, 0)), out_shape=_sds((h, REL_BUCKETS), F32),
        compiler_params=_cparams(ARB),
    )(dbias, bucket)


def _adamw_math(w, g, m, v):
    m = ADAM_B1 * m + (1.0 - ADAM_B1) * g
    v = ADAM_B2 * v + (1.0 - ADAM_B2) * (g * g)
    m_hat = m / (1.0 - ADAM_B1 ** ADAM_STEP)
    v_hat = v / (1.0 - ADAM_B2 ** ADAM_STEP)
    delta = -ADAM_LR * (m_hat / (jnp.sqrt(v_hat) + ADAM_EPS) + ADAM_WD * w)
    return delta, m, v


def _adamw_reduce(parts, w, m, v, *, tr, name):
    nl = len(parts)
    rows, c = w.shape
    r = rows // nl
    nt = r // tr

    def body(*refs):
        p_refs = refs[:nl]
        w_ref, m_ref, v_ref, g_ref, d_ref, nm_ref, nv_ref = refs[nl:]
        for li, p_ref in enumerate(p_refs):
            @pl.when(pl.program_id(0) == li)
            def _(p_ref=p_ref):
                g = p_ref[0].astype(F32)
                for k in range(1, N_DEV):
                    g = g + p_ref[k].astype(F32)
                d, nm, nv = _adamw_math(w_ref[...], g, m_ref[...], v_ref[...])
                g_ref[...] = g
                d_ref[...] = d
                nm_ref[...] = nm
                nv_ref[...] = nv

    def part_map(li):
        return lambda l, i: (0, jnp.where(l == li, i, jnp.where(l < li, 0, nt - 1)), 0)

    spec = pl.BlockSpec((tr, c), lambda l, i: (l * nt + i, 0))
    return pl.pallas_call(
        body, name=name, grid=(nl, nt),
        in_specs=[pl.BlockSpec((N_DEV, tr, c), part_map(li)) for li in range(nl)] + [spec, spec, spec],
        out_specs=[spec] * 4, out_shape=[_sds((rows, c), F32)] * 4, compiler_params=_cparams(ARB, ARB),
    )(*parts, w, m, v)


def _adamw_plain(g, w, m, v, *, name):
    def body(g_ref, w_ref, m_ref, v_ref, d_ref, nm_ref, nv_ref):
        d, nm, nv = _adamw_math(w_ref[...], g_ref[...], m_ref[...], v_ref[...])
        d_ref[...] = d
        nm_ref[...] = nm
        nv_ref[...] = nv

    return pl.pallas_call(body, name=name, out_shape=[_sds(w.shape, F32)] * 3)(g, w, m, v)


def _mesh_pos():
    return lax.axis_index("x"), lax.axis_index("y"), lax.axis_index("c")


def _allgather_body(x_refs, out_refs, send_sems, recv_sems, local_sems, slot):
    x, y, c = _mesh_pos()
    me, sibling = (x, y, c), (x, y, 1 - c)
    chips = [(1 - x, y), (x, 1 - y), (1 - x, 1 - y)]
    waits = []
    for a, (x_ref, out_ref) in enumerate(zip(x_refs, out_refs)):
        def copy(k, block, to, src=None, out_ref=out_ref, a=a):
            return pltpu.make_async_remote_copy(
                src_ref=slot(out_ref, block) if src is None else src, dst_ref=slot(out_ref, block),
                send_sem=send_sems.at[a, k], recv_sem=recv_sems.at[a, k], device_id=to, device_id_type=MESH)

        mine = pltpu.make_async_copy(x_ref, slot(out_ref, me), local_sems.at[a])
        mine.start()
        first = [copy(0, me, sibling, src=x_ref)]
        first += [copy(1 + j, me, (*chip, c), src=x_ref) for j, chip in enumerate(chips)]
        for cp in first:
            cp.start()
        waits.append((copy, mine, first))
    sends = []
    for copy, mine, first in waits:
        passed = [copy(4 + j, (*chip, c), sibling) for j, chip in enumerate(chips)]
        for j, chip in enumerate(chips):
            copy(1 + j, (*chip, c), me).wait_recv()
            passed[j].start()
        sends.append(passed)
    for (copy, mine, first), passed in zip(waits, sends):
        copy(0, sibling, me).wait_recv()
        for j, chip in enumerate(chips):
            copy(4 + j, (*chip, 1 - c), me).wait_recv()
        for cp in first + passed:
            cp.wait_send()
        mine.wait()


PEER_FLIPS = ((0, 0, 1), (1, 0, 0), (0, 1, 0), (1, 1, 0), (1, 0, 1), (0, 1, 1), (1, 1, 1))


def _peer_copies(x_refs, land_refs, send_sem, recv_sem, scatter):
    x, y, c = _mesh_pos()
    me = 4 * x + 2 * y + c
    copies = []
    for x_ref, land_ref in zip(x_refs, land_refs):
        for fx, fy, fc in PEER_FLIPS:
            px, py, pc = x ^ fx, y ^ fy, c ^ fc
            src = x_ref.at[4 * px + 2 * py + pc] if scatter else x_ref
            copies.append(pltpu.make_async_remote_copy(
                src_ref=src, dst_ref=land_ref.at[me], send_sem=send_sem, recv_sem=recv_sem,
                device_id=(px, py, pc), device_id_type=MESH))
    return copies


def _sc_exchange(xs, *, scatter, collective_id, name):
    na = len(xs)
    land_shapes = [x.shape if scatter else (N_DEV,) + x.shape for x in xs]

    def body(*refs):
        x_refs, land_refs = refs[:na], refs[na:2 * na]
        send_sem, recv_sem, local_sem = refs[2 * na:]
        x, y, c = _mesh_pos()
        me = 4 * x + 2 * y + c
        barrier = pltpu.get_barrier_semaphore()
        for fx, fy, fc in PEER_FLIPS:
            pl.semaphore_signal(barrier, inc=1, device_id=(x ^ fx, y ^ fy, c ^ fc), device_id_type=MESH)
        pl.semaphore_wait(barrier, len(PEER_FLIPS))
        for x_ref, land_ref in zip(x_refs, land_refs):
            own = pltpu.make_async_copy(x_ref.at[me] if scatter else x_ref, land_ref.at[me], local_sem)
            own.start()
            own.wait()
        copies = _peer_copies(x_refs, land_refs, send_sem, recv_sem, scatter)
        for cp in copies:
            cp.start()
        for cp in copies:
            cp.wait()

    return pl.kernel(
        body, name=name, out_type=[_sds(s, x.dtype) for s, x in zip(land_shapes, xs)],
        mesh=plsc.ScalarSubcoreMesh(axis_name="sequencer", num_cores=1),
        scratch_types=[pltpu.SemaphoreType.DMA, pltpu.SemaphoreType.DMA, pltpu.SemaphoreType.DMA],
        compiler_params=pltpu.CompilerParams(collective_id=collective_id),
    )(*xs)


def _sc_allgather(xs, *, collective_id, name):
    na = len(xs)

    def body(*refs):
        x_refs, out_refs = refs[:na], refs[na:2 * na]
        send_sems, recv_sems, local_sems = refs[2 * na:]
        x, y, c = _mesh_pos()
        barrier = pltpu.get_barrier_semaphore()
        for fx, fy, fc in PEER_FLIPS:
            pl.semaphore_signal(barrier, inc=1, device_id=(x ^ fx, y ^ fy, c ^ fc), device_id_type=MESH)
        pl.semaphore_wait(barrier, len(PEER_FLIPS))
        _allgather_body(x_refs, out_refs, send_sems, recv_sems, local_sems,
                        lambda ref, pos: ref.at[4 * pos[0] + 2 * pos[1] + pos[2]])

    return pl.kernel(
        body, name=name, out_type=[_sds((N_DEV,) + x.shape, x.dtype) for x in xs],
        mesh=plsc.ScalarSubcoreMesh(axis_name="sequencer", num_cores=1),
        scratch_types=[pltpu.SemaphoreType.DMA((na, 7)), pltpu.SemaphoreType.DMA((na, 7)),
                       pltpu.SemaphoreType.DMA((na,))],
        compiler_params=pltpu.CompilerParams(collective_id=collective_id),
    )(*xs)


def _allgather_vmem(x, *, reduce, name):
    r, c = x.shape

    def body(x_ref, out_ref, *rest):
        if reduce:
            gath, send_sems, recv_sems, local_sems = rest
        else:
            send_sems, recv_sems, local_sems = rest
            gath = out_ref
        _allgather_body([x_ref], [gath], send_sems, recv_sems, local_sems,
                        lambda ref, pos: ref.at[pl.ds((4 * pos[0] + 2 * pos[1] + pos[2]) * r, r), :])
        if reduce:
            acc = gath[0:r, :]
            for k in range(1, N_DEV):
                acc = acc + gath[k * r:(k + 1) * r, :]
            out_ref[...] = acc

    vm = pl.BlockSpec(memory_space=pltpu.VMEM)
    scratch = [pltpu.SemaphoreType.DMA((1, 7)), pltpu.SemaphoreType.DMA((1, 7)), pltpu.SemaphoreType.DMA((1,))]
    if reduce:
        scratch = [pltpu.VMEM((N_DEV * r, c), x.dtype)] + scratch
    return pl.pallas_call(
        body, name=name, in_specs=[vm], out_specs=vm,
        out_shape=_sds((r, c) if reduce else (N_DEV * r, c), x.dtype), scratch_shapes=scratch,
    )(x)


def _t5_bucket(rel):
    nb = REL_BUCKETS // 2
    ret = jnp.where(rel > 0, nb, 0)
    n = jnp.abs(rel)
    max_exact = nb // 2
    nf = jnp.maximum(n, 1).astype(F32)
    large = max_exact + (jnp.log(nf / max_exact) / math.log(REL_MAX_DIST / max_exact)
                         * (nb - max_exact)).astype(jnp.int32)
    large = jnp.minimum(large, nb - 1)
    return ret + jnp.where(n < max_exact, n, large)


def _band_pattern(block, radius, dil):
    kw = block + 2 * radius
    rel = jnp.arange(kw)[None, :] - radius - jnp.arange(block)[:, None]
    return jnp.where(jnp.abs(rel) <= radius, _t5_bucket(rel * dil), -1).astype(jnp.int32).reshape(1, block * kw)


def _rope_tables():
    lane = np.arange(64)
    seg, j = lane // 32, lane % 32
    inv = ROPE_THETA ** (-jnp.arange(0, 32, 2, dtype=F32) / 32)
    tpos = jnp.arange(SEQ)
    pos = jnp.where(jnp.asarray(seg)[None, :] == 0, (tpos // GRID_W)[:, None], (tpos % GRID_W)[:, None])
    ang = pos.astype(F32) * inv[jnp.asarray(j % 16)][None, :]
    cos = jnp.cos(ang)
    sins = jnp.where(jnp.asarray(j)[None, :] < 16, -jnp.sin(ang), jnp.sin(ang))
    return jnp.tile(cos, (1, 4)), jnp.tile(sins, (1, 4))


A_Q, A_K, A_V = (256, 0), (256, 1), (256, 2)
B_Q, B_K, B_V = (256, 0), (128, 2), (128, 3)
A_HEADS = dict(rad=A_RADIUS, nh=4, nkv=4)
B_HEADS = dict(rad=SWA_RADIUS, nh=4, nkv=2)


def _pin(arr, token):
    return arr if token is None else arr + token[0:1, 0:1]


def _local_step(x, pe, tgt, rel_bias, wts, matmul_weights, grads_ready):
    t = x.shape[0]
    bl = t // SEQ
    cos, sins = _rope_tables()
    blocks_a = [min(BAND_BLOCK, SEQ // d) for d in DILATIONS]
    pats_a = [_band_pattern(blk, A_RADIUS, d) for blk, d in zip(blocks_a, DILATIONS)]
    pat_b = _band_pattern(BAND_BLOCK, SWA_RADIUS, 1)
    table_t = rel_bias.T
    bias_a = [_bias_lookup(table_t[:4], pt, name=f"bias_a{ci}").reshape(4, blk, blk + 2 * A_RADIUS)
              for ci, (pt, blk) in enumerate(zip(pats_a, blocks_a))]
    bias_b = _bias_lookup(table_t[4:], pat_b, name="bias_b").reshape(4, BAND_BLOCK, BAND_BLOCK + 2 * SWA_RADIUS)
    nat4 = lambda a: a.reshape(bl, 1, SEQ, a.shape[-1])

    saved = []
    for li in range(DEPTH):
        w = dict(wts[li])
        w.update(matmul_weights(li, "in", x)[0])
        hn0, proj = _norm_mm((x,), w["g_mix"], w["w_in"], None, tm=1024, tn=1152, name="mix_in_fwd")
        more, started = matmul_weights(li, "rest", proj)
        w.update(more)
        w["out_gain"] = _pin(w["out_gain"], started)
        qa1, qa4, qa16, qb, qd = _qkprep_fwd(proj, w["qk_gains"], cos, sins, tm=512, name="qkprep_fwd")
        qa = (nat4(qa1), qa4, qa16)
        oa, la = [], []
        for ci in range(3):
            o, l = _band_fwd(qa[ci], A_Q, A_K, A_V, bias_a[ci], None, name=f"band_a{ci}_fwd", **A_HEADS)
            oa.append(o)
            la.append(l)
        oa[0], la[0] = oa[0].reshape(t, 256), la[0].reshape(t, 256)
        ya, lse_a = _combine_a(oa, la, tm=512, name="combine_a")
        yb, lse_b = _band_fwd(nat4(qb), B_Q, B_K, B_V, bias_b, w["sink_t"], name="band_b_fwd", **B_HEADS)
        yb = yb.reshape(t, 256)
        yc = _c_fwd(proj, w["c_g"], w["c_b"], w["c_ws"], w["c_bst"], tm=512, name="c_fwd")
        yd, lse_d = _dense_fwd(qd, tq=128, name="dense_fwd")
        mixed, x1 = _norm_mm((ya, yb, yc, yd), w["out_gain"], w["w_out"], x, tm=1024, tn=1024, name="mix_out_fwd")
        hn1, h = _norm_mm((x1,), w["g_ffn"], w["w_up"], None, tm=1024, tn=1408, name="ffn_up_fwd", out_dtype=BF16)
        act = _conv_gate_fwd(h, w["conv_w"], w["conv_b"], name="conv_gate_fwd")
        x2 = _mm(act, w["w_down"], "nn", x1, tm=1024, tn=1024, out_dtype=F32, name="ffn_down_fwd")
        hn2, x3, gate, pp = _ple_fwd(x2, w["g_ple"], w["w_gate"], pe, li * (t // 1024), w["w_proj"], tm=1024, tn=512,
                                     name="ple_fwd")
        saved.append(dict(w=w, x0=x, hn0=hn0, proj=proj, qa=qa, qb=qb, qd=qd, ya=ya, lse_a=lse_a, yb=yb, lse_b=lse_b,
                          yc=yc, yd=yd, lse_d=lse_d, mixed=mixed, x1=x1, hn1=hn1, h=h, act=act, x2=x2, hn2=hn2,
                          gate=gate, pp=pp))
        x = x3

    loss_tile, dx = _loss_head(x, tgt, tm=512, name="loss_head")
    grads = [None] * DEPTH
    d_table_a = jnp.zeros((4, REL_BUCKETS), F32)
    d_table_b = jnp.zeros((4, REL_BUCKETS), F32)
    token = None
    for li in reversed(range(DEPTH)):
        s = saved[li]
        w = s["w"]
        g = {}
        w["g_ple"] = _pin(w["g_ple"], token)
        dz, dpp = _ple_bwd_ew(dx, s["gate"], s["pp"], tm=512, name="ple_bwd_ew")
        g["w_gate"] = _mm(s["hn2"], dz, "tn", None, tm=1024, tn=512, out_dtype=BF16, name="dw_gate")
        g["w_proj"] = _mm(pe, dpp, "tn", None, tm=256, tn=1024, out_dtype=BF16, name="dw_proj", a_rows=(li, t))
        dx2, dx2b, g["g_ple"] = _mm_bt_normbwd((dz,), w["w_gate"], (s["x2"],), w["g_ple"], dx, tm=1024, tn=1024,
                                               name="ple_bwd", emit_bf16=True)
        g["w_down"] = _mm(s["act"], dx2b, "tn", None, tm=1408, tn=512, out_dtype=BF16, name="dw_down")
        dact = _mm(dx2b, w["w_down"], "nt", None, tm=1024, tn=1408, out_dtype=BF16, name="ffn_down_bwd")
        dhg, dhu, g["conv_w"], g["conv_b"] = _conv_gate_bwd(s["h"], dact, w["conv_w"], w["conv_b"], name="conv_gate_bwd")
        g["w_up"] = jnp.concatenate(
            [_mm(s["hn1"], dhalf, "tn", None, tm=1024, tn=1408, out_dtype=BF16, name=f"dw_up_{nm}")
             for nm, dhalf in (("gate", dhg), ("up", dhu))], axis=1)
        g_ffn = _pin(w["g_ffn"], grads_ready(li, "mid", g))
        dx1, dx1b, g["g_ffn"] = _mm_bt_normbwd((dhg, dhu), w["w_up"], (s["x1"],), g_ffn, dx2, tm=1024, tn=1408,
                                               name="ffn_up_bwd", emit_bf16=True)
        g["w_out"] = _mm(s["mixed"], dx1b, "tn", None, tm=1024, tn=512, out_dtype=BF16, name="dw_out")
        dycat, g["out_gain"] = _mm_bt_normbwd((dx1b,), w["w_out"], (s["ya"], s["yb"], s["yc"], s["yd"]), w["out_gain"],
                                              None, tm=1024, tn=1024, name="mix_out_bwd")
        dy_r, lse_r, dl_a, dl_b, dl_d = _deltas(dycat, s["ya"], s["yb"], s["yd"], s["lse_a"], tm=512, name="deltas")
        dy_a = (nat4(dycat),) + tuple(dy_r)
        lse_a = (nat4(s["lse_a"]),) + tuple(lse_r)
        dl_a = (nat4(dl_a[0]),) + tuple(dl_a[1:])
        da = []
        for ci in range(3):
            dq, dk, dv, dbias = _band_bwd(s["qa"][ci], A_Q, A_K, A_V, bias_a[ci], None, dy_a[ci], 0, lse_a[ci],
                                          dl_a[ci], name=f"band_a{ci}_bwd", **A_HEADS)
            if ci == 0:
                dq, dk, dv = (a.reshape(t, 256) for a in (dq, dk, dv))
            da.append((dq, dk, dv))
            d_table_a = d_table_a + _bucket_reduce(dbias.reshape(4, -1), pats_a[ci], name=f"bucket_a{ci}")
        dqb, dkb, dvb, dbias_b, dsink = _band_bwd(nat4(s["qb"]), B_Q, B_K, B_V, bias_b, w["sink_t"], nat4(dycat), 1,
                                                  nat4(s["lse_b"]), nat4(dl_b), name="band_b_bwd", **B_HEADS)
        d_table_b = d_table_b + _bucket_reduce(dbias_b.reshape(4, -1), pat_b, name="bucket_b")
        g["sink"] = dsink[:, 0, 0]
        dd = _dense_bwd(s["qd"], dycat, s["lse_d"], dl_d, tq=128, name="dense_bwd")
        dcu, dcv, g["c_ws"], dbs, g["c_g"], g["c_b"] = _c_bwd(s["proj"], dycat, w["c_g"], w["c_b"], w["c_ws"],
                                                               w["c_wst"], w["c_bst"], tm=512, name="c_bwd")
        g["c_bs"] = dbs[:, ::64].T
        db = (dqb.reshape(t, 256), dkb.reshape(t, 128), dvb.reshape(t, 128))
        dproj, dgains = _qkprep_bwd(s["proj"], da, db, dd, dcu, dcv, w["qk_gains"], cos, sins, tm=512, name="qkprep_bwd")
        g["qk_gain"] = dgains[:6, :64].reshape(3, 2, HEAD_DIM)
        g["w_in"] = _mm(s["hn0"], dproj, "tn", None, tm=1024, tn=1152, out_dtype=BF16, name="dw_in")
        dx, g["g_mix"] = _mm_bt_normbwd((dproj,), w["w_in"], (s["x0"],), w["g_mix"], dx1, tm=1024, tn=1152,
                                        name="mix_in_bwd")
        grads[li] = g
        token = grads_ready(li, "end", g)
    d_rel_bias = jnp.concatenate([d_table_a, d_table_b], axis=0).T
    return loss_tile[0, 0], dx, grads, d_rel_bias


WEIGHT_NAMES = ("rel_bias", "ln_mix_g", "w_in", "qk_gain", "sink", "c_norm_g", "c_norm_b", "c_ws", "c_bs", "out_gain",
                "w_out", "ln_ffn_g", "w_up", "conv_w", "conv_b", "w_down", "ln_ple_g", "w_ple_gate", "w_ple_proj")
COL_SHARDED = ("w_in", "w_up", "w_ple_proj")
ROW_SHARDED = ("w_out", "w_down", "w_ple_gate")
SMALL_SHARDED = ("conv_w", "out_gain")
REPLICATED = tuple(n for n in WEIGHT_NAMES if n not in COL_SHARDED + ROW_SHARDED + SMALL_SHARDED)
LOCAL_GRAD_KEY = {"ln_mix_g": "g_mix", "ln_ffn_g": "g_ffn", "ln_ple_g": "g_ple", "c_norm_g": "c_g", "c_norm_b": "c_b",
                  "w_ple_gate": "w_gate", "w_ple_proj": "w_proj"}


def _full_from_gathered(name, gathered):
    _, r, c = gathered.shape
    if name in ROW_SHARDED:
        return gathered.reshape(N_DEV * r, c)
    return jnp.transpose(gathered, (1, 0, 2)).reshape(r, N_DEV * c)


def _slots_from_full(name, full):
    rows, cols = full.shape
    if name in ROW_SHARDED:
        return full.reshape(N_DEV, rows // N_DEV, cols)
    return jnp.transpose(full.reshape(rows, N_DEV, cols // N_DEV), (1, 0, 2))


def _piece_rows(shape):
    return -(-int(np.prod(shape)) // 1024) * 8


def _pack_rows(arrays):
    pieces = []
    for a in arrays:
        n, rows = int(np.prod(a.shape)), _piece_rows(a.shape)
        flat = a.astype(F32).reshape(-1)
        if n != rows * LANES:
            flat = jnp.pad(flat, (0, rows * LANES - n))
        pieces.append(flat.reshape(rows, LANES))
    return jnp.concatenate(pieces, axis=0)


def _unpack_rows(packed, shapes):
    out, off = [], 0
    for shp in shapes:
        n, rows = int(np.prod(shp)), _piece_rows(shp)
        piece = packed[off:off + rows]
        out.append((piece if n == rows * LANES else piece.reshape(-1)[:n]).reshape(shp))
        off += rows
    return out


def kernel(x, p, rel_bias, ln_mix_g, w_in, qk_gain, sink, c_norm_g, c_norm_b, c_ws, c_bs, out_gain, w_out, ln_ffn_g, w_up, conv_w, conv_b, w_down, ln_ple_g, w_ple_gate, w_ple_proj, loss_target, m_rel_bias, m_ln_mix_g, m_w_in, m_qk_gain, m_sink, m_c_norm_g, m_c_norm_b, m_c_ws, m_c_bs, m_out_gain, m_w_out, m_ln_ffn_g, m_w_up, m_conv_w, m_conv_b, m_w_down, m_ln_ple_g, m_w_ple_gate, m_w_ple_proj, v_rel_bias, v_ln_mix_g, v_w_in, v_qk_gain, v_sink, v_c_norm_g, v_c_norm_b, v_c_ws, v_c_bs, v_out_gain, v_w_out, v_ln_ffn_g, v_w_up, v_conv_w, v_conv_b, v_w_down, v_ln_ple_g, v_w_ple_gate, v_w_ple_proj):
    env = dict(locals())
    wt = {n: env[n] for n in WEIGHT_NAMES}
    mom_m = {n: env["m_" + n] for n in WEIGHT_NAMES}
    mom_v = {n: env["v_" + n] for n in WEIGHT_NAMES}
    bl = x.shape[0]
    t = bl * SEQ
    me = 4 * lax.axis_index("x") + 2 * lax.axis_index("y") + lax.axis_index("c")

    big = COL_SHARDED + ROW_SHARDED
    full = {}
    small_shapes = [wt[n].shape for n in SMALL_SHARDED]
    small = _allgather_vmem(_pack_rows([wt[n] for n in SMALL_SHARDED]), reduce=False, name="gather_small")
    small = small.reshape(N_DEV, -1)
    off = 0
    for n, shp in zip(SMALL_SHARDED, small_shapes):
        cnt = int(np.prod(shp))
        g = small[:, off:off + cnt].reshape((N_DEV,) + tuple(shp))
        full[n] = jnp.transpose(g, (1, 2, 0, 3)).reshape(shp[0], shp[1], N_DEV * shp[2])
        off += _piece_rows(shp) * LANES

    def head_gain(li, a, b, reps):
        g = jnp.tile(qk_gain[li, a, b], reps)
        return jnp.pad(g, (0, 256 - g.shape[0]))

    wts = []
    for li in range(DEPTH):
        rows = [head_gain(li, 0, 0, 4), head_gain(li, 0, 1, 4), head_gain(li, 1, 0, 4), head_gain(li, 1, 1, 2),
                head_gain(li, 2, 0, 4), head_gain(li, 2, 1, 2), jnp.zeros((256,), F32), jnp.zeros((256,), F32)]
        wts.append(dict(
            g_mix=ln_mix_g[li].reshape(1, -1), qk_gains=jnp.stack(rows),
            sink_t=jnp.broadcast_to(sink[li][:, None, None], (4, 8, 128)),
            c_g=c_norm_g[li].reshape(1, -1), c_b=c_norm_b[li].reshape(1, -1), c_ws=c_ws[li].astype(BF16),
            c_wst=jnp.transpose(c_ws[li], (0, 2, 1)).astype(BF16), c_bst=jnp.repeat(c_bs[li].T, 64, axis=1),
            out_gain=full["out_gain"][li].reshape(1, -1), g_ffn=ln_ffn_g[li].reshape(1, -1),
            conv_w=full["conv_w"][li], conv_b=conv_b[li].reshape(1, -1), g_ple=ln_ple_g[li].reshape(1, -1)))

    local_key = {"w_ple_gate": "w_gate", "w_ple_proj": "w_proj"}

    gather_names = {"in": ("w_in",), "rest": tuple(n for n in big if n != "w_in")}
    gathered = {}
    for li in range(DEPTH):
        lands = _sc_allgather([wt[n][li].astype(BF16) for n in big], collective_id=li, name=f"gather_{li}")
        gathered[li] = dict(zip(big, lands))

    def matmul_weights(li, part, after):
        out = {}
        for n in gather_names[part]:
            g, _ = lax.optimization_barrier((gathered[li][n], after))
            out[local_key.get(n, n)] = _full_from_gathered(n, g)
        return out, None

    mid_names = ("w_ple_gate", "w_ple_proj", "w_down", "w_up")
    end_names = ("w_out", "w_in")
    landed = {}

    def start_exchange(li, names, g, tag, cid):
        slots = [_slots_from_full(n, g[local_key.get(n, n)]) for n in names]
        lands = _sc_exchange(slots, scatter=True, collective_id=cid, name=f"grads_{li}_{tag}")
        landed.update({(n, li): land for n, land in zip(names, lands)})

    def grads_ready(li, stage, g):
        if li == 0:
            start_exchange(li, mid_names if stage == "mid" else end_names, g, stage, 5 if stage == "mid" else 6)
        elif stage == "end":
            start_exchange(li, mid_names + end_names, g, stage, 4)
        return None

    loss_part, dx, grads, d_rel_bias = _local_step(
        x.reshape(t, D_MODEL), p.reshape(DEPTH * t, PLE_DIM), loss_target.reshape(t, D_MODEL), rel_bias, wts,
        matmul_weights, grads_ready)
    loss = lax.psum(loss_part, ("x", "y", "c"))

    def local_grad(n):
        if n == "rel_bias":
            return d_rel_bias
        key = LOCAL_GRAD_KEY.get(n, n)
        return jnp.stack([grads[li][key].reshape(wt[n].shape[1:]) if n in REPLICATED else grads[li][key]
                          for li in range(DEPTH)])

    out_g, out_d, out_m, out_v = {}, {}, {}, {}
    for n in big:
        shp = wt[n].shape
        two_d = lambda a: a.reshape(-1, shp[-1])
        res = _adamw_reduce([landed[n, li] for li in range(DEPTH)], two_d(wt[n]), two_d(mom_m[n]), two_d(mom_v[n]),
                            tr=32 if n == "w_down" else 128, name="adamw_" + n)
        out_g[n], out_d[n], out_m[n], out_v[n] = [r.reshape(shp) for r in res]

    small_names = REPLICATED + SMALL_SHARDED
    small_full_shapes = [wt[n].shape if n in REPLICATED else full[n].shape for n in small_names]
    reduced = _allgather_vmem(_pack_rows([local_grad(n) for n in small_names]), reduce=True, name="allreduce_small")
    reduced = dict(zip(small_names, _unpack_rows(reduced, small_full_shapes)))
    rep_shapes = [wt[n].shape for n in REPLICATED]
    upd = _adamw_plain(_pack_rows([reduced[n] for n in REPLICATED]), _pack_rows([wt[n] for n in REPLICATED]),
                       _pack_rows([mom_m[n] for n in REPLICATED]), _pack_rows([mom_v[n] for n in REPLICATED]),
                       name="adamw_replicated")
    for dst, packed in zip((out_d, out_m, out_v), upd):
        dst.update(zip(REPLICATED, _unpack_rows(packed, rep_shapes)))
    for n in REPLICATED:
        out_g[n] = reduced[n]
    for n in SMALL_SHARDED:
        shp = wt[n].shape
        g = reduced[n].reshape(shp[0], shp[1], N_DEV, shp[2])
        g = lax.dynamic_index_in_dim(g, me, axis=2, keepdims=False)
        two_d = lambda a: a.reshape(-1, shp[-1])
        res = _adamw_plain(two_d(g), two_d(wt[n]), two_d(mom_m[n]), two_d(mom_v[n]), name="adamw_" + n)
        out_g[n] = g
        out_d[n], out_m[n], out_v[n] = [r.reshape(shp) for r in res]

    return (loss, dx.reshape(bl, SEQ, D_MODEL), *[out_g[n] for n in WEIGHT_NAMES], *[out_d[n] for n in WEIGHT_NAMES],
            *[out_m[n] for n in WEIGHT_NAMES], *[out_v[n] for n in WEIGHT_NAMES])
```

```python
import math

import jax
import jax.numpy as jnp
import numpy as np
from jax import lax
from jax.experimental import pallas as pl
from jax.experimental.pallas import tpu as pltpu
from jax.experimental.pallas import tpu_sc as plsc

F32 = jnp.float32
BF16 = jnp.bfloat16
HI = lax.Precision.HIGHEST

N_DEV = 8
D_MODEL = 1024
SEQ = 2048
DEPTH = 2
HEAD_DIM = 64
IN_WIDTH = 2304
D_FF = 2816
PLE_DIM = 256
C_CHUNK = 128
C_GROUPS = 4
DILATED_CFGS = ((128, 1), (512, 4), (2048, 16))
DILATIONS = tuple(d for _, d in DILATED_CFGS)
A_RADIUS = 64
SWA_RADIUS = 128
BAND_BLOCK = 256
GRID_W = 64
ROPE_THETA = 10000.0
REL_BUCKETS = 32
REL_MAX_DIST = 1024
EPS = 1e-6
NEG_INF = -1e30
ATTN_SCALE = HEAD_DIM ** -0.5
LANES = 128

ADAM_LR = 0.001
ADAM_B1 = 0.9
ADAM_B2 = 0.999
ADAM_EPS = 1e-08
ADAM_WD = 0.01
ADAM_STEP = 10

MESH = pl.DeviceIdType.MESH
NT = (((1,), (1,)), ((), ()))
TN = (((0,), (0,)), ((), ()))
ARB = "arbitrary"
PAR = "parallel"


def _cparams(*sem):
    return pltpu.CompilerParams(dimension_semantics=tuple(sem))


def _sds(shape, dtype):
    return jax.ShapeDtypeStruct(tuple(shape), dtype)


def _group_sum_matrix(n, same_group):
    r = lax.broadcasted_iota(jnp.int32, (n, n), 0)
    c = lax.broadcasted_iota(jnp.int32, (n, n), 1)
    if same_group:
        return ((r >> 6) == (c >> 6)).astype(F32)
    return ((r & 63) == (c & 63)).astype(F32)


def _seg_sum(x, e):
    eb = e.astype(BF16)
    hi = x.astype(BF16)
    lo = (x - hi.astype(F32)).astype(BF16)
    return jnp.dot(hi, eb, preferred_element_type=F32) + jnp.dot(lo, eb, preferred_element_type=F32)


def _gelu(x):
    c = math.sqrt(2.0 / math.pi)
    return 0.5 * x * (1.0 + jnp.tanh(c * (x + 0.044715 * (x * x * x))))


def _gelu_grad(x):
    c = math.sqrt(2.0 / math.pi)
    t = jnp.tanh(c * (x + 0.044715 * (x * x * x)))
    return 0.5 * (1.0 + t) + 0.5 * x * (1.0 - t * t) * c * (1.0 + 3.0 * 0.044715 * (x * x))


def _sigmoid(x):
    return 1.0 / (1.0 + jnp.exp(-x))


def _scatter_cols(scratch, first, val):
    for c in range(val.shape[1] // LANES):
        scratch[first + c] = val[:, c * LANES:(c + 1) * LANES]


def _gather_cols(scratch, first, ncol):
    return jnp.concatenate([scratch[first + c] for c in range(ncol)], axis=1)


def _read_residue(scratch, first, ncol, r, d):
    n = scratch.shape[1] // d
    return jnp.concatenate([scratch.at[first + c][pl.ds(r, n, stride=d), :] for c in range(ncol)], axis=1)


def _write_residue(scratch, first, r, d, val):
    n = scratch.shape[1] // d
    for c in range(val.shape[1] // LANES):
        scratch.at[first + c][pl.ds(r, n, stride=d), :] = val[:, c * LANES:(c + 1) * LANES]


def _norm_mm(xs, gain, w, res, *, tm, tn, name, out_dtype=F32):
    t = xs[0].shape[0]
    k = sum(x.shape[1] for x in xs)
    n = w.shape[1]
    ng = len(xs)
    has_res = res is not None

    def body(*refs):
        x_refs = refs[:ng]
        g_ref, w_ref = refs[ng], refs[ng + 1]
        res_ref = refs[ng + 2] if has_res else None
        hn_ref, o_ref, hn_s = refs[ng + 2 + has_res:]

        @pl.when(pl.program_id(1) == 0)
        def _():
            off = 0
            for xr in x_refs:
                x = xr[...]
                wd = x.shape[1]
                r = lax.rsqrt(jnp.mean(x * x, axis=-1, keepdims=True) + EPS)
                hn_s[:, off:off + wd] = (x * r * g_ref[:, off:off + wd]).astype(BF16)
                off += wd
            hn_ref[...] = hn_s[...]

        acc = jnp.dot(hn_s[...], w_ref[...], preferred_element_type=F32)
        if has_res:
            acc = acc + res_ref[...]
        o_ref[...] = acc.astype(out_dtype)

    in_specs = [pl.BlockSpec((tm, x.shape[1]), lambda i, j: (i, 0)) for x in xs]
    in_specs += [pl.BlockSpec((1, k), lambda i, j: (0, 0)), pl.BlockSpec((k, tn), lambda i, j: (0, j))]
    args = list(xs) + [gain, w]
    if has_res:
        in_specs.append(pl.BlockSpec((tm, tn), lambda i, j: (i, j)))
        args.append(res)
    return pl.pallas_call(
        body, name=name, grid=(t // tm, n // tn), in_specs=in_specs,
        out_specs=[pl.BlockSpec((tm, k), lambda i, j: (i, 0)), pl.BlockSpec((tm, tn), lambda i, j: (i, j))],
        out_shape=[_sds((t, k), BF16), _sds((t, n), out_dtype)],
        scratch_shapes=[pltpu.VMEM((tm, k), BF16)],
        compiler_params=_cparams(PAR, ARB),
    )(*args)


def _mm(a, b, mode, res, *, tm, tn, out_dtype, name, a_rows=None):
    if mode == "tn":
        kk, m = a.shape
        blk_a = 0
        if a_rows is not None:
            blk_a, kk = a_rows
        a_spec = pl.BlockSpec((kk, tm), lambda i, j: (blk_a, i))
    else:
        m, kk = a.shape
        a_spec = pl.BlockSpec((tm, kk), lambda i, j: (i, 0))
    if mode == "nt":
        n = b.shape[0]
        b_spec = pl.BlockSpec((tn, kk), lambda i, j: (j, 0))
    else:
        n = b.shape[1]
        b_spec = pl.BlockSpec((kk, tn), lambda i, j: (0, j))
    has_res = res is not None

    def body(*refs):
        a_ref, b_ref = refs[0], refs[1]
        o_ref = refs[-1]
        av = a_ref[...].astype(BF16)
        bv = b_ref[...].astype(BF16)
        if mode == "nn":
            acc = jnp.dot(av, bv, preferred_element_type=F32)
        elif mode == "nt":
            acc = lax.dot_general(av, bv, NT, preferred_element_type=F32)
        else:
            acc = lax.dot_general(av, bv, TN, preferred_element_type=F32)
        if has_res:
            acc = acc + refs[2][...]
        o_ref[...] = acc.astype(out_dtype)

    in_specs = [a_spec, b_spec]
    args = [a, b]
    if has_res:
        in_specs.append(pl.BlockSpec((tm, tn), lambda i, j: (i, j)))
        args.append(res)
    return pl.pallas_call(
        body, name=name, grid=(m // tm, n // tn), in_specs=in_specs,
        out_specs=pl.BlockSpec((tm, tn), lambda i, j: (i, j)),
        out_shape=_sds((m, n), out_dtype),
        compiler_params=_cparams(PAR, PAR),
    )(*args)


def _mm_bt_normbwd(dys, w, xs, gain, dres, *, tm, tn, name, emit_bf16=False):
    t, wd_each = dys[0].shape
    nd = len(dys)
    per = wd_each // tn
    nj = nd * per
    k = w.shape[0]
    ng = len(xs)
    has_res = dres is not None

    def body(*refs):
        dy_refs = refs[:nd]
        w_ref = refs[nd]
        x_refs = refs[nd + 1:nd + 1 + ng]
        g_ref = refs[nd + 1 + ng]
        dres_ref = refs[nd + 2 + ng] if has_res else None
        outs = refs[nd + 2 + ng + has_res:]
        dx_ref = outs[0]
        dxb_ref = outs[1] if emit_bf16 else None
        dg_ref, acc = outs[1 + emit_bf16:]
        i, j = pl.program_id(0), pl.program_id(1)

        @pl.when(j == 0)
        def _():
            acc[...] = jnp.zeros_like(acc)

        for d, dy_ref in enumerate(dy_refs):
            @pl.when((j >= d * per) & (j < (d + 1) * per))
            def _(dy_ref=dy_ref):
                acc[...] += lax.dot_general(dy_ref[...].astype(BF16), w_ref[...], NT, preferred_element_type=F32)

        @pl.when(j == nj - 1)
        def _():
            @pl.when(i == 0)
            def _():
                dg_ref[...] = jnp.zeros_like(dg_ref)

            off = 0
            for xr in x_refs:
                x = xr[...]
                wd = x.shape[1]
                g = g_ref[:, off:off + wd]
                dyn = acc[:, off:off + wd]
                r = lax.rsqrt(jnp.mean(x * x, axis=-1, keepdims=True) + EPS)
                gdy = dyn * g
                dx = r * gdy - x * (r * r * r * jnp.mean(gdy * x, axis=-1, keepdims=True))
                if has_res:
                    dx = dx + dres_ref[:, off:off + wd]
                dx_ref[:, off:off + wd] = dx
                if emit_bf16:
                    dxb_ref[:, off:off + wd] = dx.astype(BF16)
                dg_ref[:, off:off + wd] += jnp.sum(dyn * x * r, axis=0, keepdims=True)
                off += wd

    def dy_map(d):
        return lambda i, j: (i, jnp.clip(j - d * per, 0, per - 1))

    in_specs = [pl.BlockSpec((tm, tn), dy_map(d)) for d in range(nd)]
    in_specs.append(pl.BlockSpec((k, tn), lambda i, j: (0, j)))
    in_specs += [pl.BlockSpec((tm, x.shape[1]), lambda i, j: (i, 0)) for x in xs]
    in_specs.append(pl.BlockSpec((1, k), lambda i, j: (0, 0)))
    args = list(dys) + [w] + list(xs) + [gain]
    if has_res:
        in_specs.append(pl.BlockSpec((tm, k), lambda i, j: (i, 0)))
        args.append(dres)
    row = pl.BlockSpec((tm, k), lambda i, j: (i, 0))
    out_specs = [row] + ([row] if emit_bf16 else []) + [pl.BlockSpec((1, k), lambda i, j: (0, 0))]
    out_shape = [_sds((t, k), F32)] + ([_sds((t, k), BF16)] if emit_bf16 else []) + [_sds((1, k), F32)]
    return pl.pallas_call(
        body, name=name, grid=(t // tm, nj), in_specs=in_specs, out_specs=out_specs, out_shape=out_shape,
        scratch_shapes=[pltpu.VMEM((tm, k), F32)],
        compiler_params=_cparams(ARB, ARB),
    )(*args)


def _rope_partner(y):
    n = y.shape[1]
    lane = lax.broadcasted_iota(jnp.int32, y.shape, 1)
    return jnp.where((lane & 31) < 16, pltpu.roll(y, n - 16, 1), pltpu.roll(y, 16, 1))


def _residue_specs(tm, width, nt):
    specs = [pl.BlockSpec((tm, width), lambda b, i: (b * nt + i, 0))]
    for d in DILATIONS[1:]:
        specs.append(pl.BlockSpec((None, d, tm // d, width), lambda b, i: (b, 0, i, 0)))
    return specs


def _residue_shapes(bl, width, dtype):
    return [_sds((bl * SEQ, width), dtype)] + [_sds((bl, d, SEQ // d, width), dtype) for d in DILATIONS[1:]]


def _qkprep_fwd(proj, gains, cos, sins, *, tm, name):
    t = proj.shape[0]
    bl = t // SEQ
    nt = SEQ // tm

    def body(p_ref, g_ref, c_ref, s_ref, qa1_ref, qa4_ref, qa16_ref, qb_ref, qd_ref, scr):
        e = _group_sum_matrix(256, True)

        def hn(x, row):
            wd = x.shape[1]
            ms = _seg_sum(x * x, e[:wd, :wd]) * (1.0 / HEAD_DIM)
            return x * lax.rsqrt(ms + EPS) * g_ref[row:row + 1, :wd]

        qa = jnp.concatenate([hn(p_ref[:, 0:256], 0) * ATTN_SCALE, hn(p_ref[:, 256:512], 1), p_ref[:, 512:768]], axis=1)
        qa1_ref[...] = qa.astype(BF16)
        _scatter_cols(scr, 0, qa)
        for d, ref in ((4, qa4_ref), (16, qa16_ref)):
            for r in range(d):
                ref[r] = _read_residue(scr, 0, 6, r, d).astype(BF16)
        qb_ref[:, 0:256] = (hn(p_ref[:, 768:1024], 2) * ATTN_SCALE).astype(BF16)
        qb_ref[:, 256:384] = hn(p_ref[:, 1024:1152], 3).astype(BF16)
        qb_ref[:, 384:512] = p_ref[:, 1152:1280].astype(BF16)
        yq = hn(p_ref[:, 1792:2048], 4)
        yq = yq * c_ref[...] + _rope_partner(yq) * s_ref[...]
        qd_ref[:, 0:256] = (yq * ATTN_SCALE).astype(BF16)
        yk = hn(p_ref[:, 2048:2176], 5)
        yk = yk * c_ref[:, 0:128] + _rope_partner(yk) * s_ref[:, 0:128]
        qd_ref[:, 256:384] = yk.astype(BF16)
        qd_ref[:, 384:512] = p_ref[:, 2176:2304].astype(BF16)

    row = lambda width: pl.BlockSpec((tm, width), lambda b, i: (b * nt + i, 0))
    tab = pl.BlockSpec((tm, 256), lambda b, i: (i, 0))
    return pl.pallas_call(
        body, name=name, grid=(bl, nt),
        in_specs=[row(IN_WIDTH), pl.BlockSpec((8, 256), lambda b, i: (0, 0)), tab, tab],
        out_specs=_residue_specs(tm, 768, nt) + [row(512), row(512)],
        out_shape=_residue_shapes(bl, 768, BF16) + [_sds((t, 512), BF16), _sds((t, 512), BF16)],
        scratch_shapes=[pltpu.VMEM((6, tm, LANES), F32)],
        compiler_params=_cparams(PAR, PAR),
    )(proj, gains, cos, sins)


def _qkprep_bwd(proj, da, db, dd, dcu, dcv, gains, cos, sins, *, tm, name):
    t = proj.shape[0]
    bl = t // SEQ
    nt = SEQ // tm
    flat = [a for cfg in da for a in cfg] + list(db) + list(dd) + [dcu, dcv]

    def body(*refs):
        p_ref, g_ref, c_ref, s_ref = refs[:4]
        d_refs = refs[4:4 + len(flat)]
        dp_ref, dg_ref, scr = refs[4 + len(flat):]
        a_refs = d_refs[:9]
        dqb_ref, dkb_ref, dvb_ref, dqd_ref, dkd_ref, dvd_ref, dcu_ref, dcv_ref = d_refs[9:]
        e = _group_sum_matrix(256, True)
        first = (pl.program_id(0) == 0) & (pl.program_id(1) == 0)
        last = (pl.program_id(0) == bl - 1) & (pl.program_id(1) == nt - 1)

        @pl.when(first)
        def _():
            dg_ref[...] = jnp.zeros_like(dg_ref)

        def hn_bwd(x, dy, row):
            wd = x.shape[1]
            ee = e[:wd, :wd]
            g = g_ref[row:row + 1, :wd]
            r = lax.rsqrt(_seg_sum(x * x, ee) * (1.0 / HEAD_DIM) + EPS)
            gdy = dy * g
            dx = r * gdy - x * (r * r * r * (_seg_sum(gdy * x, ee) * (1.0 / HEAD_DIM)))
            dg_ref[row:row + 1, :wd] += jnp.sum(dy * x * r, axis=0, keepdims=True)
            return dx

        def rope_bwd(dy, wd):
            return dy * c_ref[:, :wd] + _rope_partner(dy * s_ref[:, :wd])

        dqkv = jnp.concatenate([a_refs[0][...], a_refs[1][...], a_refs[2][...]], axis=1)
        for ci, d in ((1, 4), (2, 16)):
            for r in range(d):
                part = jnp.concatenate([a_refs[3 * ci + m][r] for m in range(3)], axis=1)
                _write_residue(scr, 0, r, d, part)
            dqkv = dqkv + _gather_cols(scr, 0, 6)
        dp_ref[:, 0:256] = hn_bwd(p_ref[:, 0:256], dqkv[:, 0:256] * ATTN_SCALE, 0).astype(BF16)
        dp_ref[:, 256:512] = hn_bwd(p_ref[:, 256:512], dqkv[:, 256:512], 1).astype(BF16)
        dp_ref[:, 512:768] = dqkv[:, 512:768].astype(BF16)
        dp_ref[:, 768:1024] = hn_bwd(p_ref[:, 768:1024], dqb_ref[...] * ATTN_SCALE, 2).astype(BF16)
        dp_ref[:, 1024:1152] = hn_bwd(p_ref[:, 1024:1152], dkb_ref[...], 3).astype(BF16)
        dp_ref[:, 1152:1280] = dvb_ref[...].astype(BF16)
        dp_ref[:, 1280:1536] = dcu_ref[...].astype(BF16)
        dp_ref[:, 1536:1792] = dcv_ref[...].astype(BF16)
        dp_ref[:, 1792:2048] = hn_bwd(p_ref[:, 1792:2048], rope_bwd(dqd_ref[...] * ATTN_SCALE, 256), 4).astype(BF16)
        dp_ref[:, 2048:2176] = hn_bwd(p_ref[:, 2048:2176], rope_bwd(dkd_ref[...], 128), 5).astype(BF16)
        dp_ref[:, 2176:2304] = dvd_ref[...].astype(BF16)

        @pl.when(last)
        def _():
            dg_ref[...] = _seg_sum(dg_ref[...], _group_sum_matrix(256, False))

    row = lambda width: pl.BlockSpec((tm, width), lambda b, i: (b * nt + i, 0))
    tab = pl.BlockSpec((tm, 256), lambda b, i: (i, 0))
    in_specs = [row(IN_WIDTH), pl.BlockSpec((8, 256), lambda b, i: (0, 0)), tab, tab]
    res_specs = _residue_specs(tm, 256, nt)
    in_specs += [res_specs[ci] for ci in range(3) for _ in range(3)]
    in_specs += [row(a.shape[1]) for a in flat[9:]]
    return pl.pallas_call(
        body, name=name, grid=(bl, nt), in_specs=in_specs,
        out_specs=[row(IN_WIDTH), pl.BlockSpec((8, 256), lambda b, i: (0, 0))],
        out_shape=[_sds((t, IN_WIDTH), BF16), _sds((8, 256), F32)],
        scratch_shapes=[pltpu.VMEM((6, tm, LANES), F32)],
        compiler_params=_cparams(ARB, ARB),
    )(proj, gains, cos, sins, *flat)


def _band_spec(seq_len, spec):
    width, idx = spec
    return pl.BlockSpec((None, None, seq_len, width), lambda b, r: (b, r, 0, idx))


def _fill_padded(dst, src_ref, rad, seq_len):
    z = jnp.zeros((rad, dst.shape[1]), dst.dtype)
    dst[0:rad, :] = z
    dst[rad + seq_len:rad + seq_len + rad, :] = z
    dst[rad:rad + seq_len, :] = src_ref[...]


def _band_fwd(src, qs, ks, vs, bias, sink, *, rad, nh, nkv, name):
    bl, dil, sl, _ = src.shape
    blk = bias.shape[1]
    kw = blk + 2 * rad
    nb = sl // blk
    rep = nh // nkv
    has_sink = sink is not None

    def body(*refs):
        q_ref, k_ref, v_ref, b_ref = refs[:4]
        s_ref = refs[4] if has_sink else None
        o_ref, l_ref, kp, vp = refs[4 + has_sink:]
        _fill_padded(kp, k_ref, rad, sl)
        _fill_padded(vp, v_ref, rad, sl)

        def blk_body(i, carry):
            r0 = pl.multiple_of(i * blk, blk)
            qb = q_ref[pl.ds(r0, blk), :]
            kwin = kp[pl.ds(r0, kw), :]
            vwin = vp[pl.ds(r0, kw), :]
            col = r0 - rad + lax.broadcasted_iota(jnp.int32, (blk, kw), 1)
            neg = jnp.where((col >= 0) & (col < sl), 0.0, NEG_INF).astype(F32)
            for h in range(nh):
                g = h // rep
                hs = slice(h * HEAD_DIM, (h + 1) * HEAD_DIM)
                gs = slice(g * HEAD_DIM, (g + 1) * HEAD_DIM)
                s = lax.dot_general(qb[:, hs], kwin[:, gs], NT, preferred_element_type=F32)
                s = s + b_ref[h] + neg
                m = jnp.max(s, axis=1, keepdims=True)
                if has_sink:
                    sk = s_ref[h][0:1, 0:1]
                    m = jnp.maximum(m, sk)
                p = jnp.exp(s - m)
                den = jnp.sum(p, axis=1, keepdims=True)
                if has_sink:
                    den = den + jnp.exp(sk - m)
                o = jnp.dot(p.astype(BF16), vwin[:, gs], preferred_element_type=F32) / den
                o_ref[pl.ds(r0, blk), hs] = o
                l_ref[pl.ds(r0, blk), hs] = jnp.broadcast_to(m + jnp.log(den), (blk, HEAD_DIM))
            return carry

        lax.fori_loop(0, nb, blk_body, 0)

    in_specs = [_band_spec(sl, qs), _band_spec(sl, ks), _band_spec(sl, vs),
                pl.BlockSpec((nh, blk, kw), lambda b, r: (0, 0, 0))]
    args = [src] * 3 + [bias]
    if has_sink:
        in_specs.append(pl.BlockSpec((nh, 8, 128), lambda b, r: (0, 0, 0)))
        args.append(sink)
    return pl.pallas_call(
        body, name=name, grid=(bl, dil), in_specs=in_specs,
        out_specs=[_band_spec(sl, (256, 0))] * 2,
        out_shape=[_sds((bl, dil, sl, 256), F32)] * 2,
        scratch_shapes=[pltpu.VMEM((sl + 2 * rad, ks[0]), BF16), pltpu.VMEM((sl + 2 * rad, vs[0]), BF16)],
        compiler_params=_cparams(PAR, PAR),
    )(*args)


def _band_bwd(src, qs, ks, vs, bias, sink, dy, dcol, lse, delta, *, rad, nh, nkv, name):
    bl, dil, sl, _ = src.shape
    blk = bias.shape[1]
    kw = blk + 2 * rad
    nb = sl // blk
    rep = nh // nkv
    has_sink = sink is not None
    wk, wv = ks[0], vs[0]

    def body(*refs):
        q_ref, k_ref, v_ref, b_ref = refs[:4]
        s_ref = refs[4] if has_sink else None
        do_ref, l_ref, dl_ref = refs[4 + has_sink:7 + has_sink]
        outs = refs[7 + has_sink:]
        if has_sink:
            dq_ref, dk_ref, dv_ref, db_ref, dsk_ref, kp, vp, dka, dva = outs
        else:
            dq_ref, dk_ref, dv_ref, db_ref, kp, vp, dka, dva = outs

        @pl.when((pl.program_id(0) == 0) & (pl.program_id(1) == 0))
        def _():
            db_ref[...] = jnp.zeros_like(db_ref)
            if has_sink:
                dsk_ref[...] = jnp.zeros_like(dsk_ref)

        _fill_padded(kp, k_ref, rad, sl)
        _fill_padded(vp, v_ref, rad, sl)
        dka[...] = jnp.zeros_like(dka)
        dva[...] = jnp.zeros_like(dva)

        def blk_body(i, carry):
            r0 = pl.multiple_of(i * blk, blk)
            qb = q_ref[pl.ds(r0, blk), :]
            kwin = kp[pl.ds(r0, kw), :]
            vwin = vp[pl.ds(r0, kw), :]
            dob = do_ref[pl.ds(r0, blk), :].astype(BF16)
            lb = l_ref[pl.ds(r0, blk), :]
            dlb = dl_ref[pl.ds(r0, blk), :]
            col = r0 - rad + lax.broadcasted_iota(jnp.int32, (blk, kw), 1)
            neg = jnp.where((col >= 0) & (col < sl), 0.0, NEG_INF).astype(F32)
            for h in range(nh):
                g = h // rep
                hs = slice(h * HEAD_DIM, (h + 1) * HEAD_DIM)
                gs = slice(g * HEAD_DIM, (g + 1) * HEAD_DIM)
                qh, kh, vh, doh = qb[:, hs], kwin[:, gs], vwin[:, gs], dob[:, hs]
                lh = lb[:, h * HEAD_DIM:h * HEAD_DIM + 1]
                dlh = dlb[:, h * HEAD_DIM:h * HEAD_DIM + 1]
                s = lax.dot_general(qh, kh, NT, preferred_element_type=F32) + b_ref[h] + neg
                p = jnp.exp(s - lh)
                dp = lax.dot_general(doh, vh, NT, preferred_element_type=F32)
                ds = p * (dp - dlh)
                dsb = ds.astype(BF16)
                dq_ref[pl.ds(r0, blk), hs] = jnp.dot(dsb, kh, preferred_element_type=F32)
                dka[pl.ds(r0, kw), gs] += lax.dot_general(dsb, qh, TN, preferred_element_type=F32)
                dva[pl.ds(r0, kw), gs] += lax.dot_general(p.astype(BF16), doh, TN, preferred_element_type=F32)
                db_ref[h] += ds
                if has_sink:
                    ps = jnp.exp(s_ref[h][0:1, 0:1] - lh)
                    dsk_ref[h] += jnp.broadcast_to(-jnp.sum(ps * dlh, axis=0, keepdims=True), (8, 128))
            return carry

        lax.fori_loop(0, nb, blk_body, 0)
        dk_ref[...] = dka[rad:rad + sl, :]
        dv_ref[...] = dva[rad:rad + sl, :]

    const3 = lambda b, r: (0, 0, 0)
    in_specs = [_band_spec(sl, qs), _band_spec(sl, ks), _band_spec(sl, vs), pl.BlockSpec((nh, blk, kw), const3)]
    args = [src] * 3 + [bias]
    if has_sink:
        in_specs.append(pl.BlockSpec((nh, 8, 128), const3))
        args.append(sink)
    row = _band_spec(sl, (256, 0))
    in_specs += [_band_spec(sl, (256, dcol)), row, row]
    args += [dy, lse, delta]
    out_specs = [row, _band_spec(sl, (wk, 0)), _band_spec(sl, (wv, 0)), pl.BlockSpec((nh, blk, kw), const3)]
    out_shape = [_sds((bl, dil, sl, 256), F32), _sds((bl, dil, sl, wk), F32), _sds((bl, dil, sl, wv), F32),
                 _sds((nh, blk, kw), F32)]
    if has_sink:
        out_specs.append(pl.BlockSpec((nh, 8, 128), const3))
        out_shape.append(_sds((nh, 8, 128), F32))
    return pl.pallas_call(
        body, name=name, grid=(bl, dil), in_specs=in_specs, out_specs=out_specs, out_shape=out_shape,
        scratch_shapes=[pltpu.VMEM((sl + 2 * rad, wk), BF16), pltpu.VMEM((sl + 2 * rad, wv), BF16),
                        pltpu.VMEM((sl + 2 * rad, wk), F32), pltpu.VMEM((sl + 2 * rad, wv), F32)],
        compiler_params=_cparams(ARB, ARB),
    )(*args)


def _combine_a(os_, ls_, *, tm, name):
    bl = os_[1].shape[0]
    t = bl * SEQ
    nt = SEQ // tm

    def body(o1, o4, o16, l1, l4, l16, y_ref, lt_ref, scr):
        for k, (d, ref) in enumerate(((4, o4), (16, o16), (4, l4), (16, l16))):
            for r in range(d):
                _write_residue(scr, 2 * k, r, d, ref[r])
        o2, o3, b, c = (_gather_cols(scr, 2 * k, 2) for k in range(4))
        a = l1[...]
        m = jnp.maximum(jnp.maximum(a, b), c)
        ea, eb, ec = jnp.exp(a - m), jnp.exp(b - m), jnp.exp(c - m)
        den = ea + eb + ec
        y_ref[...] = (ea / den) * o1[...] + (eb / den) * o2 + (ec / den) * o3
        lt_ref[...] = m + jnp.log(den)

    specs = _residue_specs(tm, 256, nt)
    return pl.pallas_call(
        body, name=name, grid=(bl, nt), in_specs=specs * 2, out_specs=[specs[0]] * 2,
        out_shape=[_sds((t, 256), F32)] * 2, scratch_shapes=[pltpu.VMEM((8, tm, LANES), F32)],
        compiler_params=_cparams(PAR, PAR),
    )(*os_, *ls_)


def _deltas(dycat, ya, yb, yd, lse_a, *, tm, name):
    t = ya.shape[0]
    bl = t // SEQ
    nt = SEQ // tm

    def body(dy_ref, ya_ref, yb_ref, yd_ref, la_ref, dy4, dy16, l4, l16, da1, da4, da16, db_ref, dd_ref, scr):
        e = _group_sum_matrix(256, True)
        dya = dy_ref[:, 0:256]
        dla = _seg_sum(dya * ya_ref[...], e)
        da1[...] = dla
        db_ref[...] = _seg_sum(dy_ref[:, 256:512] * yb_ref[...], e)
        dd_ref[...] = _seg_sum(dy_ref[:, 768:1024] * yd_ref[...], e)
        for k, (val, r4, r16) in enumerate(((dya, dy4, dy16), (la_ref[...], l4, l16), (dla, da4, da16))):
            _scatter_cols(scr, 2 * k, val)
            for d, ref in ((4, r4), (16, r16)):
                for r in range(d):
                    ref[r] = _read_residue(scr, 2 * k, 2, r, d)

    specs = _residue_specs(tm, 256, nt)
    nat = specs[0]
    shapes = _residue_shapes(bl, 256, F32)
    outs = pl.pallas_call(
        body, name=name, grid=(bl, nt),
        in_specs=[pl.BlockSpec((tm, 1024), lambda b, i: (b * nt + i, 0)), nat, nat, nat, nat],
        out_specs=specs[1:] + specs[1:] + specs + [nat, nat],
        out_shape=shapes[1:] + shapes[1:] + shapes + [shapes[0], shapes[0]],
        scratch_shapes=[pltpu.VMEM((6, tm, LANES), F32)],
        compiler_params=_cparams(PAR, PAR),
    )(dycat, ya, yb, yd, lse_a)
    return outs[0:2], outs[2:4], outs[4:7], outs[7], outs[8]


def _dense_fwd(qd, *, tq, name):
    t = qd.shape[0]
    bl = t // SEQ
    nq = SEQ // tq

    def body(q_ref, k_ref, v_ref, o_ref, l_ref):
        q = q_ref[...]
        for g in range(2):
            h0, h1 = 2 * g, 2 * g + 1
            q2 = jnp.concatenate([q[:, h0 * 64:(h0 + 1) * 64], q[:, h1 * 64:(h1 + 1) * 64]], axis=0)
            kg = k_ref[:, g * 64:(g + 1) * 64]
            vg = v_ref[:, g * 64:(g + 1) * 64]
            s = lax.dot_general(q2, kg, NT, preferred_element_type=F32)
            m = jnp.max(s, axis=1, keepdims=True)
            p = jnp.exp(s - m)
            den = jnp.sum(p, axis=1, keepdims=True)
            o2 = jnp.dot(p.astype(BF16), vg, preferred_element_type=F32) / den
            l2 = jnp.broadcast_to(m + jnp.log(den), (2 * tq, 64))
            o_ref[:, h0 * 64:(h0 + 1) * 64] = o2[:tq]
            o_ref[:, h1 * 64:(h1 + 1) * 64] = o2[tq:]
            l_ref[:, h0 * 64:(h0 + 1) * 64] = l2[:tq]
            l_ref[:, h1 * 64:(h1 + 1) * 64] = l2[tq:]

    q3 = qd.reshape(bl, SEQ, 512)
    o, lse = pl.pallas_call(
        body, name=name, grid=(bl, nq),
        in_specs=[pl.BlockSpec((None, tq, 256), lambda b, i: (b, i, 0)),
                  pl.BlockSpec((None, SEQ, 128), lambda b, i: (b, 0, 2)),
                  pl.BlockSpec((None, SEQ, 128), lambda b, i: (b, 0, 3))],
        out_specs=[pl.BlockSpec((None, tq, 256), lambda b, i: (b, i, 0))] * 2,
        out_shape=[_sds((bl, SEQ, 256), F32)] * 2,
        compiler_params=_cparams(PAR, PAR),
    )(q3, q3, q3)
    return o.reshape(t, 256), lse.reshape(t, 256)


def _dense_bwd(qd, dycat, lse, delta, *, tq, name):
    t = qd.shape[0]
    bl = t // SEQ
    nq = SEQ // tq

    def body(q_ref, k_ref, v_ref, do_ref, l_ref, dl_ref, dq_ref, dk_ref, dv_ref):
        @pl.when(pl.program_id(1) == 0)
        def _():
            dk_ref[...] = jnp.zeros_like(dk_ref)
            dv_ref[...] = jnp.zeros_like(dv_ref)

        q = q_ref[...]
        do = do_ref[...].astype(BF16)
        lv = l_ref[...]
        dlv = dl_ref[...]
        for g in range(2):
            h0, h1 = 2 * g, 2 * g + 1
            q2 = jnp.concatenate([q[:, h0 * 64:(h0 + 1) * 64], q[:, h1 * 64:(h1 + 1) * 64]], axis=0)
            do2 = jnp.concatenate([do[:, h0 * 64:(h0 + 1) * 64], do[:, h1 * 64:(h1 + 1) * 64]], axis=0)
            l2 = jnp.concatenate([lv[:, h0 * 64:h0 * 64 + 1], lv[:, h1 * 64:h1 * 64 + 1]], axis=0)
            dl2 = jnp.concatenate([dlv[:, h0 * 64:h0 * 64 + 1], dlv[:, h1 * 64:h1 * 64 + 1]], axis=0)
            kg = k_ref[:, g * 64:(g + 1) * 64]
            vg = v_ref[:, g * 64:(g + 1) * 64]
            s = lax.dot_general(q2, kg, NT, preferred_element_type=F32)
            p = jnp.exp(s - l2)
            dp = lax.dot_general(do2, vg, NT, preferred_element_type=F32)
            ds = (p * (dp - dl2)).astype(BF16)
            dq2 = jnp.dot(ds, kg, preferred_element_type=F32)
            dq_ref[:, h0 * 64:(h0 + 1) * 64] = dq2[:tq]
            dq_ref[:, h1 * 64:(h1 + 1) * 64] = dq2[tq:]
            dk_ref[:, g * 64:(g + 1) * 64] += lax.dot_general(ds, q2, TN, preferred_element_type=F32)
            dv_ref[:, g * 64:(g + 1) * 64] += lax.dot_general(p.astype(BF16), do2, TN, preferred_element_type=F32)

    q3 = qd.reshape(bl, SEQ, 512)
    tile = pl.BlockSpec((None, tq, 256), lambda b, i: (b, i, 0))
    full = pl.BlockSpec((None, SEQ, 128), lambda b, i: (b, 0, 0))
    dq, dk, dv = pl.pallas_call(
        body, name=name, grid=(bl, nq),
        in_specs=[tile, pl.BlockSpec((None, SEQ, 128), lambda b, i: (b, 0, 2)),
                  pl.BlockSpec((None, SEQ, 128), lambda b, i: (b, 0, 3)),
                  pl.BlockSpec((None, tq, 256), lambda b, i: (b, i, 3)), tile, tile],
        out_specs=[tile, full, full],
        out_shape=[_sds((bl, SEQ, 256), F32), _sds((bl, SEQ, 128), F32), _sds((bl, SEQ, 128), F32)],
        compiler_params=_cparams(PAR, ARB),
    )(q3, q3, q3, dycat.reshape(bl, SEQ, 1024), lse.reshape(bl, SEQ, 256), delta.reshape(bl, SEQ, 256))
    return dq.reshape(t, 256), dk.reshape(t, 128), dv.reshape(t, 128)


def _c_norm(cv, gam, bet):
    vg = _gelu(cv)
    mu = jnp.mean(vg, axis=-1, keepdims=True)
    xc = vg - mu
    r = lax.rsqrt(jnp.mean(xc * xc, axis=-1, keepdims=True) + EPS)
    xhat = xc * r
    return xhat * gam + bet, xhat, r


def _c_fwd(proj, gam, bet, ws, bst, *, tm, name):
    t = proj.shape[0]
    nch = tm // C_CHUNK

    def body(u_ref, v_ref, g_ref, b_ref, ws_ref, bs_ref, y_ref):
        vn, _, _ = _c_norm(v_ref[...], g_ref[...], b_ref[...])
        vnb = vn.astype(BF16)
        for c in range(nch):
            rows = slice(c * C_CHUNK, (c + 1) * C_CHUNK)
            for g in range(C_GROUPS):
                gs = slice(g * 64, (g + 1) * 64)
                mixed = jnp.dot(ws_ref[g], vnb[rows, gs], preferred_element_type=F32) + bs_ref[:, gs]
                y_ref[rows, gs] = _gelu(u_ref[rows, gs]) * mixed

    vec = pl.BlockSpec((1, 256), lambda i: (0, 0))
    return pl.pallas_call(
        body, name=name, grid=(t // tm,),
        in_specs=[pl.BlockSpec((tm, 256), lambda i: (i, 5)), pl.BlockSpec((tm, 256), lambda i: (i, 6)), vec, vec,
                  pl.BlockSpec((C_GROUPS, C_CHUNK, C_CHUNK), lambda i: (0, 0, 0)),
                  pl.BlockSpec((C_CHUNK, 256), lambda i: (0, 0))],
        out_specs=pl.BlockSpec((tm, 256), lambda i: (i, 0)), out_shape=_sds((t, 256), F32),
        compiler_params=_cparams(PAR),
    )(proj, proj, gam, bet, ws, bst)


def _c_bwd(proj, dycat, gam, bet, ws, wst, bst, *, tm, name):
    t = proj.shape[0]
    nch = tm // C_CHUNK
    nstep = t // tm

    def body(u_ref, v_ref, dy_ref, g_ref, b_ref, ws_ref, wst_ref, bs_ref,
             du_ref, dv_ref, dws_ref, dbs_ref, dg_ref, db_ref, dvn_s):
        step = pl.program_id(0)

        @pl.when(step == 0)
        def _():
            dws_ref[...] = jnp.zeros_like(dws_ref)
            dbs_ref[...] = jnp.zeros_like(dbs_ref)
            dg_ref[...] = jnp.zeros_like(dg_ref)
            db_ref[...] = jnp.zeros_like(db_ref)

        cv = v_ref[...]
        gam_v = g_ref[...]
        vn, xhat, r = _c_norm(cv, gam_v, b_ref[...])
        vnb = vn.astype(BF16)
        for c in range(nch):
            rows = slice(c * C_CHUNK, (c + 1) * C_CHUNK)
            for g in range(C_GROUPS):
                gs = slice(g * 64, (g + 1) * 64)
                cu = u_ref[rows, gs]
                dy = dy_ref[rows, gs]
                mixed = jnp.dot(ws_ref[g], vnb[rows, gs], preferred_element_type=F32) + bs_ref[:, gs]
                du_ref[rows, gs] = dy * mixed * _gelu_grad(cu)
                dmix = dy * _gelu(cu)
                dbs_ref[:, gs] += dmix
                dmb = dmix.astype(BF16)
                dws_ref[g] += lax.dot_general(dmb, vnb[rows, gs], NT, preferred_element_type=F32)
                dvn_s[rows, gs] = jnp.dot(wst_ref[g], dmb, preferred_element_type=F32)
        dvn = dvn_s[...]
        dg_ref[...] += jnp.sum(dvn * xhat, axis=0, keepdims=True)
        db_ref[...] += jnp.sum(dvn, axis=0, keepdims=True)
        dxh = dvn * gam_v
        dvg = r * (dxh - jnp.mean(dxh, axis=-1, keepdims=True) - xhat * jnp.mean(dxh * xhat, axis=-1, keepdims=True))
        dv_ref[...] = dvg * _gelu_grad(cv)

        @pl.when(step == nstep - 1)
        def _():
            dbs_ref[...] = _seg_sum(dbs_ref[...], _group_sum_matrix(256, True))

    vec = pl.BlockSpec((1, 256), lambda i: (0, 0))
    mat = pl.BlockSpec((C_GROUPS, C_CHUNK, C_CHUNK), lambda i: (0, 0, 0))
    bsp = pl.BlockSpec((C_CHUNK, 256), lambda i: (0, 0))
    tile = pl.BlockSpec((tm, 256), lambda i: (i, 0))
    return pl.pallas_call(
        body, name=name, grid=(nstep,),
        in_specs=[pl.BlockSpec((tm, 256), lambda i: (i, 5)), pl.BlockSpec((tm, 256), lambda i: (i, 6)),
                  pl.BlockSpec((tm, 256), lambda i: (i, 2)), vec, vec, mat, mat, bsp],
        out_specs=[tile, tile, mat, bsp, vec, vec],
        out_shape=[_sds((t, 256), F32), _sds((t, 256), F32), _sds((C_GROUPS, C_CHUNK, C_CHUNK), F32),
                   _sds((C_CHUNK, 256), F32), _sds((1, 256), F32), _sds((1, 256), F32)],
        scratch_shapes=[pltpu.VMEM((tm, 256), F32)],
        compiler_params=_cparams(ARB),
    )(proj, proj, dycat, gam, bet, ws, wst, bst)


FF_TC = 128
FF_NB = D_FF // FF_TC
FF_CH = 64
FF_HALO = 16


def _edge_taps(ref, first):
    if first:
        ext = ref[0:FF_CH + FF_HALO, :].astype(F32)
        body = slice(0, FF_CH)
    else:
        ext = ref[SEQ - FF_CH - FF_HALO:SEQ, :].astype(F32)
        body = slice(FF_HALO, FF_HALO + FF_CH)
    n = ext.shape[0]
    row = lax.broadcasted_iota(jnp.int32, ext.shape, 0)
    dn = pltpu.roll(ext, 1, 0)
    up = pltpu.roll(ext, n - 1, 0)
    if first:
        dn = jnp.where(row == 0, 0.0, dn)
    else:
        up = jnp.where(row == n - 1, 0.0, up)
    return dn[body], ext[body], up[body]


def _mid_taps(ref, r0):
    ext = ref[pl.ds(pl.multiple_of(r0 - FF_HALO, FF_HALO), FF_CH + 2 * FF_HALO), :].astype(F32)
    n = ext.shape[0]
    body = slice(FF_HALO, FF_HALO + FF_CH)
    return pltpu.roll(ext, 1, 0)[body], ext[body], pltpu.roll(ext, n - 1, 0)[body]


def _chunk_loop(step):
    step(0, lambda ref: _edge_taps(ref, True))

    def mid(i, carry):
        r0 = pl.multiple_of(i * FF_CH, FF_CH)
        step(r0, lambda ref: _mid_taps(ref, r0))
        return carry

    lax.fori_loop(1, SEQ // FF_CH - 1, mid, 0)
    step(SEQ - FF_CH, lambda ref: _edge_taps(ref, False))


def _conv3(taps, w_ref, b_ref):
    dn, md, up = taps
    return w_ref[0:1, :] * dn + w_ref[1:2, :] * md + w_ref[2:3, :] * up + b_ref[...]


def _ff_specs(order):
    def at(fn):
        return (lambda b, j: fn(b, j)) if order == "bj" else (lambda j, b: fn(b, j))
    hs = [pl.BlockSpec((None, SEQ, FF_TC), at(lambda b, j, o=o: (b, 0, j + o))) for o in (0, FF_NB)]
    ws = [pl.BlockSpec((3, FF_TC), at(lambda b, j, o=o: (0, j + o))) for o in (0, FF_NB)]
    bs = [pl.BlockSpec((1, FF_TC), at(lambda b, j, o=o: (0, j + o))) for o in (0, FF_NB)]
    return hs, ws, bs


def _conv_gate_fwd(h, cw, cb, *, name):
    t = h.shape[0]
    bl = t // SEQ

    def body(hg_ref, hu_ref, wg_ref, wu_ref, bg_ref, bu_ref, a_ref):
        def step(r0, taps):
            cg = _conv3(taps(hg_ref), wg_ref, bg_ref)
            cu = _conv3(taps(hu_ref), wu_ref, bu_ref)
            a_ref[pl.ds(r0, FF_CH), :] = (cg * _sigmoid(cg) * cu).astype(BF16)

        _chunk_loop(step)

    hs, ws, bs = _ff_specs("bj")
    h3 = h.reshape(bl, SEQ, 2 * D_FF)
    act = pl.pallas_call(
        body, name=name, grid=(bl, FF_NB), in_specs=hs + ws + bs,
        out_specs=pl.BlockSpec((None, SEQ, FF_TC), lambda b, j: (b, 0, j)),
        out_shape=_sds((bl, SEQ, D_FF), BF16),
        compiler_params=_cparams(PAR, PAR),
    )(h3, h3, cw, cw, cb, cb)
    return act.reshape(t, D_FF)


def _conv_gate_bwd(h, dact, cw, cb, *, name):
    t = h.shape[0]
    bl = t // SEQ

    def body(hg_ref, hu_ref, wg_ref, wu_ref, bg_ref, bu_ref, da_ref,
             dhg_ref, dhu_ref, dwg_ref, dwu_ref, dbg_ref, dbu_ref, dg_s, du_s):
        @pl.when(pl.program_id(1) == 0)
        def _():
            for ref in (dwg_ref, dwu_ref, dbg_ref, dbu_ref):
                ref[...] = jnp.zeros_like(ref)

        red = lambda x: jnp.sum(x, axis=0, keepdims=True)

        def pass1(r0, taps):
            tg, tu = taps(hg_ref), taps(hu_ref)
            cg = _conv3(tg, wg_ref, bg_ref)
            cu = _conv3(tu, wu_ref, bu_ref)
            da = da_ref[pl.ds(r0, FF_CH), :].astype(F32)
            sg = _sigmoid(cg)
            dcg = da * cu * (sg * (1.0 + cg * (1.0 - sg)))
            dcu = da * (cg * sg)
            dg_s[pl.ds(r0, FF_CH), :] = dcg
            du_s[pl.ds(r0, FF_CH), :] = dcu
            for d, tp, dw_ref, db_ref in ((dcg, tg, dwg_ref, dbg_ref), (dcu, tu, dwu_ref, dbu_ref)):
                for k in range(3):
                    dw_ref[k:k + 1, :] += red(d * tp[k])
                db_ref[...] += red(d)

        _chunk_loop(pass1)

        def pass2(r0, taps):
            for s, w_ref, o_ref in ((dg_s, wg_ref, dhg_ref), (du_s, wu_ref, dhu_ref)):
                dn, md, up = taps(s)
                o_ref[pl.ds(r0, FF_CH), :] = (w_ref[0:1, :] * up + w_ref[1:2, :] * md + w_ref[2:3, :] * dn).astype(BF16)

        _chunk_loop(pass2)

    hs, ws, bs = _ff_specs("jb")
    half = pl.BlockSpec((None, SEQ, FF_TC), lambda j, b: (b, 0, j))
    wsp = pl.BlockSpec((3, FF_TC), lambda j, b: (0, j))
    bsp = pl.BlockSpec((1, FF_TC), lambda j, b: (0, j))
    h3 = h.reshape(bl, SEQ, 2 * D_FF)
    dhg, dhu, dwg, dwu, dbg, dbu = pl.pallas_call(
        body, name=name, grid=(FF_NB, bl), in_specs=hs + ws + bs + [half],
        out_specs=[half, half, wsp, wsp, bsp, bsp],
        out_shape=[_sds((bl, SEQ, D_FF), BF16), _sds((bl, SEQ, D_FF), BF16), _sds((3, D_FF), F32), _sds((3, D_FF), F32),
                   _sds((1, D_FF), F32), _sds((1, D_FF), F32)],
        scratch_shapes=[pltpu.VMEM((SEQ, FF_TC), F32), pltpu.VMEM((SEQ, FF_TC), F32)],
        compiler_params=_cparams(PAR, ARB),
    )(h3, h3, cw, cw, cb, cb, dact.reshape(bl, SEQ, D_FF))
    return (dhg.reshape(t, D_FF), dhu.reshape(t, D_FF), jnp.concatenate([dwg, dwu], axis=1),
            jnp.concatenate([dbg, dbu], axis=1))


def _ple_fwd(x2, gain, wg, pe, pe_blk, wp, *, tm, tn, name):
    t, k = x2.shape
    n = wg.shape[1]

    def body(x_ref, g_ref, wg_ref, pe_ref, wp_ref, xr_ref, hn_ref, x3_ref, gt_ref, pp_ref, hn_s):
        @pl.when(pl.program_id(1) == 0)
        def _():
            x = x_ref[...]
            r = lax.rsqrt(jnp.mean(x * x, axis=-1, keepdims=True) + EPS)
            hn_s[...] = (x * r * g_ref[...]).astype(BF16)
            hn_ref[...] = hn_s[...]

        gate = _sigmoid(jnp.dot(hn_s[...], wg_ref[...], preferred_element_type=F32))
        pp = jnp.dot(pe_ref[...].astype(BF16), wp_ref[...], preferred_element_type=F32)
        gt_ref[...] = gate
        pp_ref[...] = pp
        x3_ref[...] = xr_ref[...] + pp * gate

    tile = pl.BlockSpec((tm, tn), lambda i, j: (i, j))
    return pl.pallas_call(
        body, name=name, grid=(t // tm, n // tn),
        in_specs=[pl.BlockSpec((tm, k), lambda i, j: (i, 0)), pl.BlockSpec((1, k), lambda i, j: (0, 0)),
                  pl.BlockSpec((k, tn), lambda i, j: (0, j)), pl.BlockSpec((tm, PLE_DIM), lambda i, j: (pe_blk + i, 0)),
                  pl.BlockSpec((PLE_DIM, tn), lambda i, j: (0, j)), tile],
        out_specs=[pl.BlockSpec((tm, k), lambda i, j: (i, 0)), tile, tile, tile],
        out_shape=[_sds((t, k), BF16), _sds((t, n), F32), _sds((t, n), F32), _sds((t, n), F32)],
        scratch_shapes=[pltpu.VMEM((tm, k), BF16)],
        compiler_params=_cparams(PAR, ARB),
    )(x2, gain, wg, pe, wp, x2)


def _ple_bwd_ew(dx3, gate, pp, *, tm, name):
    t, n = dx3.shape

    def body(d_ref, g_ref, p_ref, dz_ref, dpp_ref):
        d, g = d_ref[...], g_ref[...]
        dz_ref[...] = (d * p_ref[...] * g * (1.0 - g)).astype(BF16)
        dpp_ref[...] = (d * g).astype(BF16)

    spec = pl.BlockSpec((tm, n), lambda i: (i, 0))
    return pl.pallas_call(
        body, name=name, grid=(t // tm,), in_specs=[spec] * 3, out_specs=[spec] * 2,
        out_shape=[_sds((t, n), BF16)] * 2, compiler_params=_cparams(PAR),
    )(dx3, gate, pp)


def _loss_head(y, tgt, *, tm, name):
    t, d = y.shape

    def body(y_ref, t_ref, l_ref, dy_ref):
        @pl.when(pl.program_id(0) == 0)
        def _():
            l_ref[...] = jnp.zeros_like(l_ref)

        e = y_ref[...] - t_ref[...]
        dy_ref[...] = e * (1.0 / d)
        s = jnp.sum(jnp.sum(e * e, axis=1, keepdims=True), axis=0, keepdims=True)
        l_ref[...] += jnp.broadcast_to(s * (0.5 / d), (8, 128))

    spec = pl.BlockSpec((tm, d), lambda i: (i, 0))
    return pl.pallas_call(
        body, name=name, grid=(t // tm,), in_specs=[spec, spec],
        out_specs=[pl.BlockSpec((8, 128), lambda i: (0, 0)), spec],
        out_shape=[_sds((8, 128), F32), _sds((t, d), F32)], compiler_params=_cparams(ARB),
    )(y, tgt)


BIAS_PC = 8192


def _onehot(bucket_row):
    rows = lax.broadcasted_iota(jnp.int32, (REL_BUCKETS, bucket_row.shape[1]), 0)
    return (rows == bucket_row).astype(F32)


def _bias_lookup(table_t, bucket, *, name):
    h = table_t.shape[0]
    p = bucket.shape[1]

    def body(t_ref, b_ref, o_ref):
        bk = b_ref[...]
        val = jnp.dot(t_ref[...], _onehot(bk), precision=HI, preferred_element_type=F32)
        o_ref[...] = jnp.where(bk >= 0, val, NEG_INF)

    return pl.pallas_call(
        body, name=name, grid=(p // BIAS_PC,),
        in_specs=[pl.BlockSpec((h, REL_BUCKETS), lambda i: (0, 0)), pl.BlockSpec((1, BIAS_PC), lambda i: (0, i))],
        out_specs=pl.BlockSpec((h, BIAS_PC), lambda i: (0, i)), out_shape=_sds((h, p), F32),
        compiler_params=_cparams(PAR),
    )(table_t, bucket)


def _bucket_reduce(dbias, bucket, *, name):
    h, p = dbias.shape

    def body(d_ref, b_ref, o_ref):
        @pl.when(pl.program_id(0) == 0)
        def _():
            o_ref[...] = jnp.zeros_like(o_ref)

        o_ref[...] += lax.dot_general(d_ref[...], _onehot(b_ref[...]), NT, precision=HI, preferred_element_type=F32)

    return pl.pallas_call(
        body, name=name, grid=(p // BIAS_PC,),
        in_specs=[pl.BlockSpec((h, BIAS_PC), lambda i: (0, i)), pl.BlockSpec((1, BIAS_PC), lambda i: (0, i))],
        out_specs=pl.BlockSpec((h, REL_BUCKETS), lambda i: (0, 0)), out_shape=_sds((h, REL_BUCKETS), F32),
        compiler_params=_cparams(ARB),
    )(dbias, bucket)


def _adamw_math(w, g, m, v):
    m = ADAM_B1 * m + (1.0 - ADAM_B1) * g
    v = ADAM_B2 * v + (1.0 - ADAM_B2) * (g * g)
    m_hat = m / (1.0 - ADAM_B1 ** ADAM_STEP)
    v_hat = v / (1.0 - ADAM_B2 ** ADAM_STEP)
    delta = -ADAM_LR * (m_hat / (jnp.sqrt(v_hat) + ADAM_EPS) + ADAM_WD * w)
    return delta, m, v


def _adamw_reduce(parts, w, m, v, *, tr, name):
    nl = len(parts)
    rows, c = w.shape
    r = rows // nl
    nt = r // tr

    def body(*refs):
        p_refs = refs[:nl]
        w_ref, m_ref, v_ref, g_ref, d_ref, nm_ref, nv_ref = refs[nl:]
        for li, p_ref in enumerate(p_refs):
            @pl.when(pl.program_id(0) == li)
            def _(p_ref=p_ref):
                g = p_ref[0].astype(F32)
                for k in range(1, N_DEV):
                    g = g + p_ref[k].astype(F32)
                d, nm, nv = _adamw_math(w_ref[...], g, m_ref[...], v_ref[...])
                g_ref[...] = g
                d_ref[...] = d
                nm_ref[...] = nm
                nv_ref[...] = nv

    def part_map(li):
        return lambda l, i: (0, jnp.where(l == li, i, jnp.where(l < li, 0, nt - 1)), 0)

    spec = pl.BlockSpec((tr, c), lambda l, i: (l * nt + i, 0))
    return pl.pallas_call(
        body, name=name, grid=(nl, nt),
        in_specs=[pl.BlockSpec((N_DEV, tr, c), part_map(li)) for li in range(nl)] + [spec, spec, spec],
        out_specs=[spec] * 4, out_shape=[_sds((rows, c), F32)] * 4, compiler_params=_cparams(ARB, ARB),
    )(*parts, w, m, v)


def _adamw_plain(g, w, m, v, *, name):
    def body(g_ref, w_ref, m_ref, v_ref, d_ref, nm_ref, nv_ref):
        d, nm, nv = _adamw_math(w_ref[...], g_ref[...], m_ref[...], v_ref[...])
        d_ref[...] = d
        nm_ref[...] = nm
        nv_ref[...] = nv

    return pl.pallas_call(body, name=name, out_shape=[_sds(w.shape, F32)] * 3)(g, w, m, v)


def _mesh_pos():
    return lax.axis_index("x"), lax.axis_index("y"), lax.axis_index("c")


def _allgather_body(x_refs, out_refs, send_sems, recv_sems, local_sems, slot):
    x, y, c = _mesh_pos()
    me, sibling = (x, y, c), (x, y, 1 - c)
    chips = [(1 - x, y), (x, 1 - y), (1 - x, 1 - y)]
    waits = []
    for a, (x_ref, out_ref) in enumerate(zip(x_refs, out_refs)):
        def copy(k, block, to, src=None, out_ref=out_ref, a=a):
            return pltpu.make_async_remote_copy(
                src_ref=slot(out_ref, block) if src is None else src, dst_ref=slot(out_ref, block),
                send_sem=send_sems.at[a, k], recv_sem=recv_sems.at[a, k], device_id=to, device_id_type=MESH)

        mine = pltpu.make_async_copy(x_ref, slot(out_ref, me), local_sems.at[a])
        mine.start()
        first = [copy(0, me, sibling, src=x_ref)]
        first += [copy(1 + j, me, (*chip, c), src=x_ref) for j, chip in enumerate(chips)]
        for cp in first:
            cp.start()
        waits.append((copy, mine, first))
    sends = []
    for copy, mine, first in waits:
        passed = [copy(4 + j, (*chip, c), sibling) for j, chip in enumerate(chips)]
        for j, chip in enumerate(chips):
            copy(1 + j, (*chip, c), me).wait_recv()
            passed[j].start()
        sends.append(passed)
    for (copy, mine, first), passed in zip(waits, sends):
        copy(0, sibling, me).wait_recv()
        for j, chip in enumerate(chips):
            copy(4 + j, (*chip, 1 - c), me).wait_recv()
        for cp in first + passed:
            cp.wait_send()
        mine.wait()


PEER_FLIPS = ((0, 0, 1), (1, 0, 0), (0, 1, 0), (1, 1, 0), (1, 0, 1), (0, 1, 1), (1, 1, 1))


def _peer_copies(x_refs, land_refs, send_sem, recv_sem, scatter):
    x, y, c = _mesh_pos()
    me = 4 * x + 2 * y + c
    copies = []
    for x_ref, land_ref in zip(x_refs, land_refs):
        for fx, fy, fc in PEER_FLIPS:
            px, py, pc = x ^ fx, y ^ fy, c ^ fc
            src = x_ref.at[4 * px + 2 * py + pc] if scatter else x_ref
            copies.append(pltpu.make_async_remote_copy(
                src_ref=src, dst_ref=land_ref.at[me], send_sem=send_sem, recv_sem=recv_sem,
                device_id=(px, py, pc), device_id_type=MESH))
    return copies


def _sc_exchange(xs, *, scatter, collective_id, name):
    na = len(xs)
    land_shapes = [x.shape if scatter else (N_DEV,) + x.shape for x in xs]

    def body(*refs):
        x_refs, land_refs = refs[:na], refs[na:2 * na]
        send_sem, recv_sem, local_sem = refs[2 * na:]
        x, y, c = _mesh_pos()
        me = 4 * x + 2 * y + c
        barrier = pltpu.get_barrier_semaphore()
        for fx, fy, fc in PEER_FLIPS:
            pl.semaphore_signal(barrier, inc=1, device_id=(x ^ fx, y ^ fy, c ^ fc), device_id_type=MESH)
        pl.semaphore_wait(barrier, len(PEER_FLIPS))
        for x_ref, land_ref in zip(x_refs, land_refs):
            own = pltpu.make_async_copy(x_ref.at[me] if scatter else x_ref, land_ref.at[me], local_sem)
            own.start()
            own.wait()
        copies = _peer_copies(x_refs, land_refs, send_sem, recv_sem, scatter)
        for cp in copies:
            cp.start()
        for cp in copies:
            cp.wait()

    return pl.kernel(
        body, name=name, out_type=[_sds(s, x.dtype) for s, x in zip(land_shapes, xs)],
        mesh=plsc.ScalarSubcoreMesh(axis_name="sequencer", num_cores=1),
        scratch_types=[pltpu.SemaphoreType.DMA, pltpu.SemaphoreType.DMA, pltpu.SemaphoreType.DMA],
        compiler_params=pltpu.CompilerParams(collective_id=collective_id),
    )(*xs)


def _sc_allgather(xs, *, collective_id, name):
    na = len(xs)

    def body(*refs):
        x_refs, out_refs = refs[:na], refs[na:2 * na]
        send_sems, recv_sems, local_sems = refs[2 * na:]
        x, y, c = _mesh_pos()
        barrier = pltpu.get_barrier_semaphore()
        for fx, fy, fc in PEER_FLIPS:
            pl.semaphore_signal(barrier, inc=1, device_id=(x ^ fx, y ^ fy, c ^ fc), device_id_type=MESH)
        pl.semaphore_wait(barrier, len(PEER_FLIPS))
        _allgather_body(x_refs, out_refs, send_sems, recv_sems, local_sems,
                        lambda ref, pos: ref.at[4 * pos[0] + 2 * pos[1] + pos[2]])

    return pl.kernel(
        body, name=name, out_type=[_sds((N_DEV,) + x.shape, x.dtype) for x in xs],
        mesh=plsc.ScalarSubcoreMesh(axis_name="sequencer", num_cores=1),
        scratch_types=[pltpu.SemaphoreType.DMA((na, 7)), pltpu.SemaphoreType.DMA((na, 7)),
                       pltpu.SemaphoreType.DMA((na,))],
        compiler_params=pltpu.CompilerParams(collective_id=collective_id),
    )(*xs)


def _allgather_vmem(x, *, reduce, name):
    r, c = x.shape

    def body(x_ref, out_ref, *rest):
        if reduce:
            gath, send_sems, recv_sems, local_sems = rest
        else:
            send_sems, recv_sems, local_sems = rest
            gath = out_ref
        _allgather_body([x_ref], [gath], send_sems, recv_sems, local_sems,
                        lambda ref, pos: ref.at[pl.ds((4 * pos[0] + 2 * pos[1] + pos[2]) * r, r), :])
        if reduce:
            acc = gath[0:r, :]
            for k in range(1, N_DEV):
                acc = acc + gath[k * r:(k + 1) * r, :]
            out_ref[...] = acc

    vm = pl.BlockSpec(memory_space=pltpu.VMEM)
    scratch = [pltpu.SemaphoreType.DMA((1, 7)), pltpu.SemaphoreType.DMA((1, 7)), pltpu.SemaphoreType.DMA((1,))]
    if reduce:
        scratch = [pltpu.VMEM((N_DEV * r, c), x.dtype)] + scratch
    return pl.pallas_call(
        body, name=name, in_specs=[vm], out_specs=vm,
        out_shape=_sds((r, c) if reduce else (N_DEV * r, c), x.dtype), scratch_shapes=scratch,
    )(x)


def _t5_bucket(rel):
    nb = REL_BUCKETS // 2
    ret = jnp.where(rel > 0, nb, 0)
    n = jnp.abs(rel)
    max_exact = nb // 2
    nf = jnp.maximum(n, 1).astype(F32)
    large = max_exact + (jnp.log(nf / max_exact) / math.log(REL_MAX_DIST / max_exact)
                         * (nb - max_exact)).astype(jnp.int32)
    large = jnp.minimum(large, nb - 1)
    return ret + jnp.where(n < max_exact, n, large)


def _band_pattern(block, radius, dil):
    kw = block + 2 * radius
    rel = jnp.arange(kw)[None, :] - radius - jnp.arange(block)[:, None]
    return jnp.where(jnp.abs(rel) <= radius, _t5_bucket(rel * dil), -1).astype(jnp.int32).reshape(1, block * kw)


def _rope_tables():
    lane = np.arange(64)
    seg, j = lane // 32, lane % 32
    inv = ROPE_THETA ** (-jnp.arange(0, 32, 2, dtype=F32) / 32)
    tpos = jnp.arange(SEQ)
    pos = jnp.where(jnp.asarray(seg)[None, :] == 0, (tpos // GRID_W)[:, None], (tpos % GRID_W)[:, None])
    ang = pos.astype(F32) * inv[jnp.asarray(j % 16)][None, :]
    cos = jnp.cos(ang)
    sins = jnp.where(jnp.asarray(j)[None, :] < 16, -jnp.sin(ang), jnp.sin(ang))
    return jnp.tile(cos, (1, 4)), jnp.tile(sins, (1, 4))


A_Q, A_K, A_V = (256, 0), (256, 1), (256, 2)
B_Q, B_K, B_V = (256, 0), (128, 2), (128, 3)
A_HEADS = dict(rad=A_RADIUS, nh=4, nkv=4)
B_HEADS = dict(rad=SWA_RADIUS, nh=4, nkv=2)


def _pin(arr, token):
    return arr if token is None else arr + token[0:1, 0:1]


def _local_step(x, pe, tgt, rel_bias, wts, matmul_weights, grads_ready):
    t = x.shape[0]
    bl = t // SEQ
    cos, sins = _rope_tables()
    blocks_a = [min(BAND_BLOCK, SEQ // d) for d in DILATIONS]
    pats_a = [_band_pattern(blk, A_RADIUS, d) for blk, d in zip(blocks_a, DILATIONS)]
    pat_b = _band_pattern(BAND_BLOCK, SWA_RADIUS, 1)
    table_t = rel_bias.T
    bias_a = [_bias_lookup(table_t[:4], pt, name=f"bias_a{ci}").reshape(4, blk, blk + 2 * A_RADIUS)
              for ci, (pt, blk) in enumerate(zip(pats_a, blocks_a))]
    bias_b = _bias_lookup(table_t[4:], pat_b, name="bias_b").reshape(4, BAND_BLOCK, BAND_BLOCK + 2 * SWA_RADIUS)
    nat4 = lambda a: a.reshape(bl, 1, SEQ, a.shape[-1])

    saved = []
    for li in range(DEPTH):
        w = dict(wts[li])
        w.update(matmul_weights(li, "in", x)[0])
        hn0, proj = _norm_mm((x,), w["g_mix"], w["w_in"], None, tm=1024, tn=1152, name="mix_in_fwd")
        qa1, qa4, qa16, qb, qd = _qkprep_fwd(proj, w["qk_gains"], cos, sins, tm=512, name="qkprep_fwd")
        qa = (nat4(qa1), qa4, qa16)
        oa, la = [], []
        for ci in range(3):
            o, l = _band_fwd(qa[ci], A_Q, A_K, A_V, bias_a[ci], None, name=f"band_a{ci}_fwd", **A_HEADS)
            oa.append(o)
            la.append(l)
        oa[0], la[0] = oa[0].reshape(t, 256), la[0].reshape(t, 256)
        ya, lse_a = _combine_a(oa, la, tm=512, name="combine_a")
        yb, lse_b = _band_fwd(nat4(qb), B_Q, B_K, B_V, bias_b, w["sink_t"], name="band_b_fwd", **B_HEADS)
        yb = yb.reshape(t, 256)
        yc = _c_fwd(proj, w["c_g"], w["c_b"], w["c_ws"], w["c_bst"], tm=512, name="c_fwd")
        yd, lse_d = _dense_fwd(qd, tq=128, name="dense_fwd")
        more, started = matmul_weights(li, "rest", yd)
        w.update(more)
        w["out_gain"] = _pin(w["out_gain"], started)
        mixed, x1 = _norm_mm((ya, yb, yc, yd), w["out_gain"], w["w_out"], x, tm=1024, tn=1024, name="mix_out_fwd")
        hn1, h = _norm_mm((x1,), w["g_ffn"], w["w_up"], None, tm=1024, tn=1408, name="ffn_up_fwd", out_dtype=BF16)
        act = _conv_gate_fwd(h, w["conv_w"], w["conv_b"], name="conv_gate_fwd")
        x2 = _mm(act, w["w_down"], "nn", x1, tm=1024, tn=1024, out_dtype=F32, name="ffn_down_fwd")
        hn2, x3, gate, pp = _ple_fwd(x2, w["g_ple"], w["w_gate"], pe, li * (t // 1024), w["w_proj"], tm=1024, tn=512,
                                     name="ple_fwd")
        saved.append(dict(w=w, x0=x, hn0=hn0, proj=proj, qa=qa, qb=qb, qd=qd, ya=ya, lse_a=lse_a, yb=yb, lse_b=lse_b,
                          yc=yc, yd=yd, lse_d=lse_d, mixed=mixed, x1=x1, hn1=hn1, h=h, act=act, x2=x2, hn2=hn2,
                          gate=gate, pp=pp))
        x = x3

    loss_tile, dx = _loss_head(x, tgt, tm=512, name="loss_head")
    grads = [None] * DEPTH
    d_table_a = jnp.zeros((4, REL_BUCKETS), F32)
    d_table_b = jnp.zeros((4, REL_BUCKETS), F32)
    token = None
    for li in reversed(range(DEPTH)):
        s = saved[li]
        w = s["w"]
        g = {}
        w["g_ple"] = _pin(w["g_ple"], token)
        dz, dpp = _ple_bwd_ew(dx, s["gate"], s["pp"], tm=512, name="ple_bwd_ew")
        g["w_gate"] = _mm(s["hn2"], dz, "tn", None, tm=1024, tn=512, out_dtype=BF16, name="dw_gate")
        g["w_proj"] = _mm(pe, dpp, "tn", None, tm=256, tn=1024, out_dtype=BF16, name="dw_proj", a_rows=(li, t))
        dx2, dx2b, g["g_ple"] = _mm_bt_normbwd((dz,), w["w_gate"], (s["x2"],), w["g_ple"], dx, tm=1024, tn=1024,
                                               name="ple_bwd", emit_bf16=True)
        g["w_down"] = _mm(s["act"], dx2b, "tn", None, tm=1408, tn=512, out_dtype=BF16, name="dw_down")
        dact = _mm(dx2b, w["w_down"], "nt", None, tm=1024, tn=1408, out_dtype=BF16, name="ffn_down_bwd")
        dhg, dhu, g["conv_w"], g["conv_b"] = _conv_gate_bwd(s["h"], dact, w["conv_w"], w["conv_b"], name="conv_gate_bwd")
        g["w_up"] = jnp.concatenate(
            [_mm(s["hn1"], dhalf, "tn", None, tm=1024, tn=1408, out_dtype=BF16, name=f"dw_up_{nm}")
             for nm, dhalf in (("gate", dhg), ("up", dhu))], axis=1)
        g_ffn = _pin(w["g_ffn"], grads_ready(li, "mid", g))
        dx1, dx1b, g["g_ffn"] = _mm_bt_normbwd((dhg, dhu), w["w_up"], (s["x1"],), g_ffn, dx2, tm=1024, tn=1408,
                                               name="ffn_up_bwd", emit_bf16=True)
        g["w_out"] = _mm(s["mixed"], dx1b, "tn", None, tm=1024, tn=512, out_dtype=BF16, name="dw_out")
        dycat, g["out_gain"] = _mm_bt_normbwd((dx1b,), w["w_out"], (s["ya"], s["yb"], s["yc"], s["yd"]), w["out_gain"],
                                              None, tm=1024, tn=1024, name="mix_out_bwd")
        dy_r, lse_r, dl_a, dl_b, dl_d = _deltas(dycat, s["ya"], s["yb"], s["yd"], s["lse_a"], tm=512, name="deltas")
        dy_a = (nat4(dycat),) + tuple(dy_r)
        lse_a = (nat4(s["lse_a"]),) + tuple(lse_r)
        dl_a = (nat4(dl_a[0]),) + tuple(dl_a[1:])
        da = []
        for ci in range(3):
            dq, dk, dv, dbias = _band_bwd(s["qa"][ci], A_Q, A_K, A_V, bias_a[ci], None, dy_a[ci], 0, lse_a[ci],
                                          dl_a[ci], name=f"band_a{ci}_bwd", **A_HEADS)
            if ci == 0:
                dq, dk, dv = (a.reshape(t, 256) for a in (dq, dk, dv))
            da.append((dq, dk, dv))
            d_table_a = d_table_a + _bucket_reduce(dbias.reshape(4, -1), pats_a[ci], name=f"bucket_a{ci}")
        dqb, dkb, dvb, dbias_b, dsink = _band_bwd(nat4(s["qb"]), B_Q, B_K, B_V, bias_b, w["sink_t"], nat4(dycat), 1,
                                                  nat4(s["lse_b"]), nat4(dl_b), name="band_b_bwd", **B_HEADS)
        d_table_b = d_table_b + _bucket_reduce(dbias_b.reshape(4, -1), pat_b, name="bucket_b")
        g["sink"] = dsink[:, 0, 0]
        dd = _dense_bwd(s["qd"], dycat, s["lse_d"], dl_d, tq=128, name="dense_bwd")
        dcu, dcv, g["c_ws"], dbs, g["c_g"], g["c_b"] = _c_bwd(s["proj"], dycat, w["c_g"], w["c_b"], w["c_ws"],
                                                               w["c_wst"], w["c_bst"], tm=512, name="c_bwd")
        g["c_bs"] = dbs[:, ::64].T
        db = (dqb.reshape(t, 256), dkb.reshape(t, 128), dvb.reshape(t, 128))
        dproj, dgains = _qkprep_bwd(s["proj"], da, db, dd, dcu, dcv, w["qk_gains"], cos, sins, tm=512, name="qkprep_bwd")
        g["qk_gain"] = dgains[:6, :64].reshape(3, 2, HEAD_DIM)
        g["w_in"] = _mm(s["hn0"], dproj, "tn", None, tm=1024, tn=1152, out_dtype=BF16, name="dw_in")
        dx, g["g_mix"] = _mm_bt_normbwd((dproj,), w["w_in"], (s["x0"],), w["g_mix"], dx1, tm=1024, tn=1152,
                                        name="mix_in_bwd")
        grads[li] = g
        token = grads_ready(li, "end", g)
    d_rel_bias = jnp.concatenate([d_table_a, d_table_b], axis=0).T
    return loss_tile[0, 0], dx, grads, d_rel_bias


WEIGHT_NAMES = ("rel_bias", "ln_mix_g", "w_in", "qk_gain", "sink", "c_norm_g", "c_norm_b", "c_ws", "c_bs", "out_gain",
                "w_out", "ln_ffn_g", "w_up", "conv_w", "conv_b", "w_down", "ln_ple_g", "w_ple_gate", "w_ple_proj")
COL_SHARDED = ("w_in", "w_up", "w_ple_proj")
ROW_SHARDED = ("w_out", "w_down", "w_ple_gate")
SMALL_SHARDED = ("conv_w", "out_gain")
REPLICATED = tuple(n for n in WEIGHT_NAMES if n not in COL_SHARDED + ROW_SHARDED + SMALL_SHARDED)
LOCAL_GRAD_KEY = {"ln_mix_g": "g_mix", "ln_ffn_g": "g_ffn", "ln_ple_g": "g_ple", "c_norm_g": "c_g", "c_norm_b": "c_b",
                  "w_ple_gate": "w_gate", "w_ple_proj": "w_proj"}


def _full_from_gathered(name, gathered):
    _, r, c = gathered.shape
    if name in ROW_SHARDED:
        return gathered.reshape(N_DEV * r, c)
    return jnp.transpose(gathered, (1, 0, 2)).reshape(r, N_DEV * c)


def _slots_from_full(name, full):
    rows, cols = full.shape
    if name in ROW_SHARDED:
        return full.reshape(N_DEV, rows // N_DEV, cols)
    return jnp.transpose(full.reshape(rows, N_DEV, cols // N_DEV), (1, 0, 2))


def _piece_rows(shape):
    return -(-int(np.prod(shape)) // 1024) * 8


def _pack_rows(arrays):
    pieces = []
    for a in arrays:
        n, rows = int(np.prod(a.shape)), _piece_rows(a.shape)
        flat = a.astype(F32).reshape(-1)
        if n != rows * LANES:
            flat = jnp.pad(flat, (0, rows * LANES - n))
        pieces.append(flat.reshape(rows, LANES))
    return jnp.concatenate(pieces, axis=0)


def _unpack_rows(packed, shapes):
    out, off = [], 0
    for shp in shapes:
        n, rows = int(np.prod(shp)), _piece_rows(shp)
        piece = packed[off:off + rows]
        out.append((piece if n == rows * LANES else piece.reshape(-1)[:n]).reshape(shp))
        off += rows
    return out


def kernel(x, p, rel_bias, ln_mix_g, w_in, qk_gain, sink, c_norm_g, c_norm_b, c_ws, c_bs, out_gain, w_out, ln_ffn_g, w_up, conv_w, conv_b, w_down, ln_ple_g, w_ple_gate, w_ple_proj, loss_target, m_rel_bias, m_ln_mix_g, m_w_in, m_qk_gain, m_sink, m_c_norm_g, m_c_norm_b, m_c_ws, m_c_bs, m_out_gain, m_w_out, m_ln_ffn_g, m_w_up, m_conv_w, m_conv_b, m_w_down, m_ln_ple_g, m_w_ple_gate, m_w_ple_proj, v_rel_bias, v_ln_mix_g, v_w_in, v_qk_gain, v_sink, v_c_norm_g, v_c_norm_b, v_c_ws, v_c_bs, v_out_gain, v_w_out, v_ln_ffn_g, v_w_up, v_conv_w, v_conv_b, v_w_down, v_ln_ple_g, v_w_ple_gate, v_w_ple_proj):
    env = dict(locals())
    wt = {n: env[n] for n in WEIGHT_NAMES}
    mom_m = {n: env["m_" + n] for n in WEIGHT_NAMES}
    mom_v = {n: env["v_" + n] for n in WEIGHT_NAMES}
    bl = x.shape[0]
    t = bl * SEQ
    me = 4 * lax.axis_index("x") + 2 * lax.axis_index("y") + lax.axis_index("c")

    big = COL_SHARDED + ROW_SHARDED
    full = {}
    small_shapes = [wt[n].shape for n in SMALL_SHARDED]
    small = _allgather_vmem(_pack_rows([wt[n] for n in SMALL_SHARDED]), reduce=False, name="gather_small")
    small = small.reshape(N_DEV, -1)
    off = 0
    for n, shp in zip(SMALL_SHARDED, small_shapes):
        cnt = int(np.prod(shp))
        g = small[:, off:off + cnt].reshape((N_DEV,) + tuple(shp))
        full[n] = jnp.transpose(g, (1, 2, 0, 3)).reshape(shp[0], shp[1], N_DEV * shp[2])
        off += _piece_rows(shp) * LANES

    def head_gain(li, a, b, reps):
        g = jnp.tile(qk_gain[li, a, b], reps)
        return jnp.pad(g, (0, 256 - g.shape[0]))

    wts = []
    for li in range(DEPTH):
        rows = [head_gain(li, 0, 0, 4), head_gain(li, 0, 1, 4), head_gain(li, 1, 0, 4), head_gain(li, 1, 1, 2),
                head_gain(li, 2, 0, 4), head_gain(li, 2, 1, 2), jnp.zeros((256,), F32), jnp.zeros((256,), F32)]
        wts.append(dict(
            g_mix=ln_mix_g[li].reshape(1, -1), qk_gains=jnp.stack(rows),
            sink_t=jnp.broadcast_to(sink[li][:, None, None], (4, 8, 128)),
            c_g=c_norm_g[li].reshape(1, -1), c_b=c_norm_b[li].reshape(1, -1), c_ws=c_ws[li].astype(BF16),
            c_wst=jnp.transpose(c_ws[li], (0, 2, 1)).astype(BF16), c_bst=jnp.repeat(c_bs[li].T, 64, axis=1),
            out_gain=full["out_gain"][li].reshape(1, -1), g_ffn=ln_ffn_g[li].reshape(1, -1),
            conv_w=full["conv_w"][li], conv_b=conv_b[li].reshape(1, -1), g_ple=ln_ple_g[li].reshape(1, -1)))

    local_key = {"w_ple_gate": "w_gate", "w_ple_proj": "w_proj"}

    gather_names = {"in": ("w_in",), "rest": tuple(n for n in big if n != "w_in")}
    gathered = {}
    for cid, (li, names) in enumerate(((0, gather_names["in"]), (0, gather_names["rest"]), (1, big))):
        lands = _sc_allgather([wt[n][li].astype(BF16) for n in names], collective_id=cid,
                              name=f"gather_{li}_{len(names)}")
        gathered.setdefault(li, {}).update(zip(names, lands))

    def matmul_weights(li, part, after):
        out = {}
        for n in gather_names[part]:
            g, _ = lax.optimization_barrier((gathered[li][n], after))
            out[local_key.get(n, n)] = _full_from_gathered(n, g)
        return out, None

    mid_names = ("w_ple_gate", "w_ple_proj", "w_down", "w_up")
    end_names = ("w_out", "w_in")
    landed = {}

    def start_exchange(li, names, g, tag, cid):
        slots = [_slots_from_full(n, g[local_key.get(n, n)]) for n in names]
        lands = _sc_exchange(slots, scatter=True, collective_id=cid, name=f"grads_{li}_{tag}")
        landed.update({(n, li): land for n, land in zip(names, lands)})

    def grads_ready(li, stage, g):
        if li == 0:
            start_exchange(li, mid_names if stage == "mid" else end_names, g, stage, 5 if stage == "mid" else 6)
        elif stage == "end":
            start_exchange(li, mid_names + end_names, g, stage, 4)
        return None

    loss_part, dx, grads, d_rel_bias = _local_step(
        x.reshape(t, D_MODEL), p.reshape(DEPTH * t, PLE_DIM), loss_target.reshape(t, D_MODEL), rel_bias, wts,
        matmul_weights, grads_ready)
    loss = lax.psum(loss_part, ("x", "y", "c"))

    def local_grad(n):
        if n == "rel_bias":
            return d_rel_bias
        key = LOCAL_GRAD_KEY.get(n, n)
        return jnp.stack([grads[li][key].reshape(wt[n].shape[1:]) if n in REPLICATED else grads[li][key]
                          for li in range(DEPTH)])

    out_g, out_d, out_m, out_v = {}, {}, {}, {}
    for n in big:
        shp = wt[n].shape
        two_d = lambda a: a.reshape(-1, shp[-1])
        res = _adamw_reduce([landed[n, li] for li in range(DEPTH)], two_d(wt[n]), two_d(mom_m[n]), two_d(mom_v[n]),
                            tr=32 if n == "w_down" else 128, name="adamw_" + n)
        out_g[n], out_d[n], out_m[n], out_v[n] = [r.reshape(shp) for r in res]

    small_names = REPLICATED + SMALL_SHARDED
    small_full_shapes = [wt[n].shape if n in REPLICATED else full[n].shape for n in small_names]
    reduced = _allgather_vmem(_pack_rows([local_grad(n) for n in small_names]), reduce=True, name="allreduce_small")
    reduced = dict(zip(small_names, _unpack_rows(reduced, small_full_shapes)))
    rep_shapes = [wt[n].shape for n in REPLICATED]
    upd = _adamw_plain(_pack_rows([reduced[n] for n in REPLICATED]), _pack_rows([wt[n] for n in REPLICATED]),
                       _pack_rows([mom_m[n] for n in REPLICATED]), _pack_rows([mom_v[n] for n in REPLICATED]),
                       name="adamw_replicated")
    for dst, packed in zip((out_d, out_m, out_v), upd):
        dst.update(zip(REPLICATED, _unpack_rows(packed, rep_shapes)))
    for n in REPLICATED:
        out_g[n] = reduced[n]
    for n in SMALL_SHARDED:
        shp = wt[n].shape
        g = reduced[n].reshape(shp[0], shp[1], N_DEV, shp[2])
        g = lax.dynamic_index_in_dim(g, me, axis=2, keepdims=False)
        two_d = lambda a: a.reshape(-1, shp[-1])
        res = _adamw_plain(two_d(g), two_d(wt[n]), two_d(mom_m[n]), two_d(mom_v[n]), name="adamw_" + n)
        out_g[n] = g
        out_d[n], out_m[n], out_v[n] = [r.reshape(shp) for r in res]

    return (loss, dx.reshape(bl, SEQ, D_MODEL), *[out_g[n] for n in WEIGHT_NAMES], *[out_d[n] for n in WEIGHT_NAMES],
            *[out_m[n] for n in WEIGHT_NAMES], *[out_v[n] for n in WEIGHT_NAMES])
```

```python
import math

import jax
import jax.numpy as jnp
import numpy as np
from jax import lax
from jax.experimental import pallas as pl
from jax.experimental.pallas import tpu as pltpu
from jax.experimental.pallas import tpu_sc as plsc

F32 = jnp.float32
BF16 = jnp.bfloat16
HI = lax.Precision.HIGHEST

N_DEV = 8
D_MODEL = 1024
SEQ = 2048
DEPTH = 2
HEAD_DIM = 64
IN_WIDTH = 2304
D_FF = 2816
PLE_DIM = 256
C_CHUNK = 128
C_GROUPS = 4
DILATED_CFGS = ((128, 1), (512, 4), (2048, 16))
DILATIONS = tuple(d for _, d in DILATED_CFGS)
A_RADIUS = 64
SWA_RADIUS = 128
BAND_BLOCK = 256
GRID_W = 64
ROPE_THETA = 10000.0
REL_BUCKETS = 32
REL_MAX_DIST = 1024
EPS = 1e-6
NEG_INF = -1e30
ATTN_SCALE = HEAD_DIM ** -0.5
LANES = 128

ADAM_LR = 0.001
ADAM_B1 = 0.9
ADAM_B2 = 0.999
ADAM_EPS = 1e-08
ADAM_WD = 0.01
ADAM_STEP = 10

MESH = pl.DeviceIdType.MESH
NT = (((1,), (1,)), ((), ()))
TN = (((0,), (0,)), ((), ()))
ARB = "arbitrary"
PAR = "parallel"


def _cparams(*sem):
    return pltpu.CompilerParams(dimension_semantics=tuple(sem))


def _sds(shape, dtype):
    return jax.ShapeDtypeStruct(tuple(shape), dtype)


def _group_sum_matrix(n, same_group):
    r = lax.broadcasted_iota(jnp.int32, (n, n), 0)
    c = lax.broadcasted_iota(jnp.int32, (n, n), 1)
    if same_group:
        return ((r >> 6) == (c >> 6)).astype(F32)
    return ((r & 63) == (c & 63)).astype(F32)


def _seg_sum(x, e):
    eb = e.astype(BF16)
    hi = x.astype(BF16)
    lo = (x - hi.astype(F32)).astype(BF16)
    return jnp.dot(hi, eb, preferred_element_type=F32) + jnp.dot(lo, eb, preferred_element_type=F32)


def _gelu(x):
    c = math.sqrt(2.0 / math.pi)
    return 0.5 * x * (1.0 + jnp.tanh(c * (x + 0.044715 * (x * x * x))))


def _gelu_grad(x):
    c = math.sqrt(2.0 / math.pi)
    t = jnp.tanh(c * (x + 0.044715 * (x * x * x)))
    return 0.5 * (1.0 + t) + 0.5 * x * (1.0 - t * t) * c * (1.0 + 3.0 * 0.044715 * (x * x))


def _sigmoid(x):
    return 1.0 / (1.0 + jnp.exp(-x))


def _scatter_cols(scratch, first, val):
    for c in range(val.shape[1] // LANES):
        scratch[first + c] = val[:, c * LANES:(c + 1) * LANES]


def _gather_cols(scratch, first, ncol):
    return jnp.concatenate([scratch[first + c] for c in range(ncol)], axis=1)


def _read_residue(scratch, first, ncol, r, d):
    n = scratch.shape[1] // d
    return jnp.concatenate([scratch.at[first + c][pl.ds(r, n, stride=d), :] for c in range(ncol)], axis=1)


def _write_residue(scratch, first, r, d, val):
    n = scratch.shape[1] // d
    for c in range(val.shape[1] // LANES):
        scratch.at[first + c][pl.ds(r, n, stride=d), :] = val[:, c * LANES:(c + 1) * LANES]


def _norm_mm(xs, gain, w, res, *, tm, tn, name, out_dtype=F32):
    t = xs[0].shape[0]
    k = sum(x.shape[1] for x in xs)
    n = w.shape[1]
    ng = len(xs)
    has_res = res is not None

    def body(*refs):
        x_refs = refs[:ng]
        g_ref, w_ref = refs[ng], refs[ng + 1]
        res_ref = refs[ng + 2] if has_res else None
        hn_ref, o_ref, hn_s = refs[ng + 2 + has_res:]

        @pl.when(pl.program_id(1) == 0)
        def _():
            off = 0
            for xr in x_refs:
                x = xr[...]
                wd = x.shape[1]
                r = lax.rsqrt(jnp.mean(x * x, axis=-1, keepdims=True) + EPS)
                hn_s[:, off:off + wd] = (x * r * g_ref[:, off:off + wd]).astype(BF16)
                off += wd
            hn_ref[...] = hn_s[...]

        acc = jnp.dot(hn_s[...], w_ref[...], preferred_element_type=F32)
        if has_res:
            acc = acc + res_ref[...]
        o_ref[...] = acc.astype(out_dtype)

    in_specs = [pl.BlockSpec((tm, x.shape[1]), lambda i, j: (i, 0)) for x in xs]
    in_specs += [pl.BlockSpec((1, k), lambda i, j: (0, 0)), pl.BlockSpec((k, tn), lambda i, j: (0, j))]
    args = list(xs) + [gain, w]
    if has_res:
        in_specs.append(pl.BlockSpec((tm, tn), lambda i, j: (i, j)))
        args.append(res)
    return pl.pallas_call(
        body, name=name, grid=(t // tm, n // tn), in_specs=in_specs,
        out_specs=[pl.BlockSpec((tm, k), lambda i, j: (i, 0)), pl.BlockSpec((tm, tn), lambda i, j: (i, j))],
        out_shape=[_sds((t, k), BF16), _sds((t, n), out_dtype)],
        scratch_shapes=[pltpu.VMEM((tm, k), BF16)],
        compiler_params=_cparams(PAR, ARB),
    )(*args)


def _mm(a, b, mode, res, *, tm, tn, out_dtype, name, a_rows=None):
    if mode == "tn":
        kk, m = a.shape
        blk_a = 0
        if a_rows is not None:
            blk_a, kk = a_rows
        a_spec = pl.BlockSpec((kk, tm), lambda i, j: (blk_a, i))
    else:
        m, kk = a.shape
        a_spec = pl.BlockSpec((tm, kk), lambda i, j: (i, 0))
    if mode == "nt":
        n = b.shape[0]
        b_spec = pl.BlockSpec((tn, kk), lambda i, j: (j, 0))
    else:
        n = b.shape[1]
        b_spec = pl.BlockSpec((kk, tn), lambda i, j: (0, j))
    has_res = res is not None

    def body(*refs):
        a_ref, b_ref = refs[0], refs[1]
        o_ref = refs[-1]
        av = a_ref[...].astype(BF16)
        bv = b_ref[...].astype(BF16)
        if mode == "nn":
            acc = jnp.dot(av, bv, preferred_element_type=F32)
        elif mode == "nt":
            acc = lax.dot_general(av, bv, NT, preferred_element_type=F32)
        else:
            acc = lax.dot_general(av, bv, TN, preferred_element_type=F32)
        if has_res:
            acc = acc + refs[2][...]
        o_ref[...] = acc.astype(out_dtype)

    in_specs = [a_spec, b_spec]
    args = [a, b]
    if has_res:
        in_specs.append(pl.BlockSpec((tm, tn), lambda i, j: (i, j)))
        args.append(res)
    return pl.pallas_call(
        body, name=name, grid=(m // tm, n // tn), in_specs=in_specs,
        out_specs=pl.BlockSpec((tm, tn), lambda i, j: (i, j)),
        out_shape=_sds((m, n), out_dtype),
        compiler_params=_cparams(PAR, PAR),
    )(*args)


def _mm_bt_normbwd(dys, w, xs, gain, dres, *, tm, tn, name, emit_bf16=False):
    t, wd_each = dys[0].shape
    nd = len(dys)
    per = wd_each // tn
    nj = nd * per
    k = w.shape[0]
    ng = len(xs)
    has_res = dres is not None

    def body(*refs):
        dy_refs = refs[:nd]
        w_ref = refs[nd]
        x_refs = refs[nd + 1:nd + 1 + ng]
        g_ref = refs[nd + 1 + ng]
        dres_ref = refs[nd + 2 + ng] if has_res else None
        outs = refs[nd + 2 + ng + has_res:]
        dx_ref = outs[0]
        dxb_ref = outs[1] if emit_bf16 else None
        dg_ref, acc = outs[1 + emit_bf16:]
        i, j = pl.program_id(0), pl.program_id(1)

        @pl.when(j == 0)
        def _():
            acc[...] = jnp.zeros_like(acc)

        for d, dy_ref in enumerate(dy_refs):
            @pl.when((j >= d * per) & (j < (d + 1) * per))
            def _(dy_ref=dy_ref):
                acc[...] += lax.dot_general(dy_ref[...].astype(BF16), w_ref[...], NT, preferred_element_type=F32)

        @pl.when(j == nj - 1)
        def _():
            @pl.when(i == 0)
            def _():
                dg_ref[...] = jnp.zeros_like(dg_ref)

            off = 0
            for xr in x_refs:
                x = xr[...]
                wd = x.shape[1]
                g = g_ref[:, off:off + wd]
                dyn = acc[:, off:off + wd]
                r = lax.rsqrt(jnp.mean(x * x, axis=-1, keepdims=True) + EPS)
                gdy = dyn * g
                dx = r * gdy - x * (r * r * r * jnp.mean(gdy * x, axis=-1, keepdims=True))
                if has_res:
                    dx = dx + dres_ref[:, off:off + wd]
                dx_ref[:, off:off + wd] = dx
                if emit_bf16:
                    dxb_ref[:, off:off + wd] = dx.astype(BF16)
                dg_ref[:, off:off + wd] += jnp.sum(dyn * x * r, axis=0, keepdims=True)
                off += wd

    def dy_map(d):
        return lambda i, j: (i, jnp.clip(j - d * per, 0, per - 1))

    in_specs = [pl.BlockSpec((tm, tn), dy_map(d)) for d in range(nd)]
    in_specs.append(pl.BlockSpec((k, tn), lambda i, j: (0, j)))
    in_specs += [pl.BlockSpec((tm, x.shape[1]), lambda i, j: (i, 0)) for x in xs]
    in_specs.append(pl.BlockSpec((1, k), lambda i, j: (0, 0)))
    args = list(dys) + [w] + list(xs) + [gain]
    if has_res:
        in_specs.append(pl.BlockSpec((tm, k), lambda i, j: (i, 0)))
        args.append(dres)
    row = pl.BlockSpec((tm, k), lambda i, j: (i, 0))
    out_specs = [row] + ([row] if emit_bf16 else []) + [pl.BlockSpec((1, k), lambda i, j: (0, 0))]
    out_shape = [_sds((t, k), F32)] + ([_sds((t, k), BF16)] if emit_bf16 else []) + [_sds((1, k), F32)]
    return pl.pallas_call(
        body, name=name, grid=(t // tm, nj), in_specs=in_specs, out_specs=out_specs, out_shape=out_shape,
        scratch_shapes=[pltpu.VMEM((tm, k), F32)],
        compiler_params=_cparams(ARB, ARB),
    )(*args)


def _rope_partner(y):
    n = y.shape[1]
    lane = lax.broadcasted_iota(jnp.int32, y.shape, 1)
    return jnp.where((lane & 31) < 16, pltpu.roll(y, n - 16, 1), pltpu.roll(y, 16, 1))


def _residue_specs(tm, width, nt):
    specs = [pl.BlockSpec((tm, width), lambda b, i: (b * nt + i, 0))]
    for d in DILATIONS[1:]:
        specs.append(pl.BlockSpec((None, d, tm // d, width), lambda b, i: (b, 0, i, 0)))
    return specs


def _residue_shapes(bl, width, dtype):
    return [_sds((bl * SEQ, width), dtype)] + [_sds((bl, d, SEQ // d, width), dtype) for d in DILATIONS[1:]]


def _qkprep_fwd(proj, gains, cos, sins, *, tm, name):
    t = proj.shape[0]
    bl = t // SEQ
    nt = SEQ // tm

    def body(p_ref, g_ref, c_ref, s_ref, qa1_ref, qa4_ref, qa16_ref, qb_ref, qd_ref, scr):
        e = _group_sum_matrix(256, True)

        def hn(x, row):
            wd = x.shape[1]
            ms = _seg_sum(x * x, e[:wd, :wd]) * (1.0 / HEAD_DIM)
            return x * lax.rsqrt(ms + EPS) * g_ref[row:row + 1, :wd]

        qa = jnp.concatenate([hn(p_ref[:, 0:256], 0) * ATTN_SCALE, hn(p_ref[:, 256:512], 1), p_ref[:, 512:768]], axis=1)
        qa1_ref[...] = qa.astype(BF16)
        _scatter_cols(scr, 0, qa)
        for d, ref in ((4, qa4_ref), (16, qa16_ref)):
            for r in range(d):
                ref[r] = _read_residue(scr, 0, 6, r, d).astype(BF16)
        qb_ref[:, 0:256] = (hn(p_ref[:, 768:1024], 2) * ATTN_SCALE).astype(BF16)
        qb_ref[:, 256:384] = hn(p_ref[:, 1024:1152], 3).astype(BF16)
        qb_ref[:, 384:512] = p_ref[:, 1152:1280].astype(BF16)
        yq = hn(p_ref[:, 1792:2048], 4)
        yq = yq * c_ref[...] + _rope_partner(yq) * s_ref[...]
        qd_ref[:, 0:256] = (yq * ATTN_SCALE).astype(BF16)
        yk = hn(p_ref[:, 2048:2176], 5)
        yk = yk * c_ref[:, 0:128] + _rope_partner(yk) * s_ref[:, 0:128]
        qd_ref[:, 256:384] = yk.astype(BF16)
        qd_ref[:, 384:512] = p_ref[:, 2176:2304].astype(BF16)

    row = lambda width: pl.BlockSpec((tm, width), lambda b, i: (b * nt + i, 0))
    tab = pl.BlockSpec((tm, 256), lambda b, i: (i, 0))
    return pl.pallas_call(
        body, name=name, grid=(bl, nt),
        in_specs=[row(IN_WIDTH), pl.BlockSpec((8, 256), lambda b, i: (0, 0)), tab, tab],
        out_specs=_residue_specs(tm, 768, nt) + [row(512), row(512)],
        out_shape=_residue_shapes(bl, 768, BF16) + [_sds((t, 512), BF16), _sds((t, 512), BF16)],
        scratch_shapes=[pltpu.VMEM((6, tm, LANES), F32)],
        compiler_params=_cparams(PAR, PAR),
    )(proj, gains, cos, sins)


def _qkprep_bwd(proj, da, db, dd, dcu, dcv, gains, cos, sins, *, tm, name):
    t = proj.shape[0]
    bl = t // SEQ
    nt = SEQ // tm
    flat = [a for cfg in da for a in cfg] + list(db) + list(dd) + [dcu, dcv]

    def body(*refs):
        p_ref, g_ref, c_ref, s_ref = refs[:4]
        d_refs = refs[4:4 + len(flat)]
        dp_ref, dg_ref, scr = refs[4 + len(flat):]
        a_refs = d_refs[:9]
        dqb_ref, dkb_ref, dvb_ref, dqd_ref, dkd_ref, dvd_ref, dcu_ref, dcv_ref = d_refs[9:]
        e = _group_sum_matrix(256, True)
        first = (pl.program_id(0) == 0) & (pl.program_id(1) == 0)
        last = (pl.program_id(0) == bl - 1) & (pl.program_id(1) == nt - 1)

        @pl.when(first)
        def _():
            dg_ref[...] = jnp.zeros_like(dg_ref)

        def hn_bwd(x, dy, row):
            wd = x.shape[1]
            ee = e[:wd, :wd]
            g = g_ref[row:row + 1, :wd]
            r = lax.rsqrt(_seg_sum(x * x, ee) * (1.0 / HEAD_DIM) + EPS)
            gdy = dy * g
            dx = r * gdy - x * (r * r * r * (_seg_sum(gdy * x, ee) * (1.0 / HEAD_DIM)))
            dg_ref[row:row + 1, :wd] += jnp.sum(dy * x * r, axis=0, keepdims=True)
            return dx

        def rope_bwd(dy, wd):
            return dy * c_ref[:, :wd] + _rope_partner(dy * s_ref[:, :wd])

        dqkv = jnp.concatenate([a_refs[0][...], a_refs[1][...], a_refs[2][...]], axis=1)
        for ci, d in ((1, 4), (2, 16)):
            for r in range(d):
                part = jnp.concatenate([a_refs[3 * ci + m][r] for m in range(3)], axis=1)
                _write_residue(scr, 0, r, d, part)
            dqkv = dqkv + _gather_cols(scr, 0, 6)
        dp_ref[:, 0:256] = hn_bwd(p_ref[:, 0:256], dqkv[:, 0:256] * ATTN_SCALE, 0).astype(BF16)
        dp_ref[:, 256:512] = hn_bwd(p_ref[:, 256:512], dqkv[:, 256:512], 1).astype(BF16)
        dp_ref[:, 512:768] = dqkv[:, 512:768].astype(BF16)
        dp_ref[:, 768:1024] = hn_bwd(p_ref[:, 768:1024], dqb_ref[...] * ATTN_SCALE, 2).astype(BF16)
        dp_ref[:, 1024:1152] = hn_bwd(p_ref[:, 1024:1152], dkb_ref[...], 3).astype(BF16)
        dp_ref[:, 1152:1280] = dvb_ref[...].astype(BF16)
        dp_ref[:, 1280:1536] = dcu_ref[...].astype(BF16)
        dp_ref[:, 1536:1792] = dcv_ref[...].astype(BF16)
        dp_ref[:, 1792:2048] = hn_bwd(p_ref[:, 1792:2048], rope_bwd(dqd_ref[...] * ATTN_SCALE, 256), 4).astype(BF16)
        dp_ref[:, 2048:2176] = hn_bwd(p_ref[:, 2048:2176], rope_bwd(dkd_ref[...], 128), 5).astype(BF16)
        dp_ref[:, 2176:2304] = dvd_ref[...].astype(BF16)

        @pl.when(last)
        def _():
            dg_ref[...] = _seg_sum(dg_ref[...], _group_sum_matrix(256, False))

    row = lambda width: pl.BlockSpec((tm, width), lambda b, i: (b * nt + i, 0))
    tab = pl.BlockSpec((tm, 256), lambda b, i: (i, 0))
    in_specs = [row(IN_WIDTH), pl.BlockSpec((8, 256), lambda b, i: (0, 0)), tab, tab]
    res_specs = _residue_specs(tm, 256, nt)
    in_specs += [res_specs[ci] for ci in range(3) for _ in range(3)]
    in_specs += [row(a.shape[1]) for a in flat[9:]]
    return pl.pallas_call(
        body, name=name, grid=(bl, nt), in_specs=in_specs,
        out_specs=[row(IN_WIDTH), pl.BlockSpec((8, 256), lambda b, i: (0, 0))],
        out_shape=[_sds((t, IN_WIDTH), BF16), _sds((8, 256), F32)],
        scratch_shapes=[pltpu.VMEM((6, tm, LANES), F32)],
        compiler_params=_cparams(ARB, ARB),
    )(proj, gains, cos, sins, *flat)


def _band_spec(seq_len, spec):
    width, idx = spec
    return pl.BlockSpec((None, None, seq_len, width), lambda b, r: (b, r, 0, idx))


def _fill_padded(dst, src_ref, rad, seq_len):
    z = jnp.zeros((rad, dst.shape[1]), dst.dtype)
    dst[0:rad, :] = z
    dst[rad + seq_len:rad + seq_len + rad, :] = z
    dst[rad:rad + seq_len, :] = src_ref[...]


def _band_fwd(src, qs, ks, vs, bias, sink, *, rad, nh, nkv, name):
    bl, dil, sl, _ = src.shape
    blk = bias.shape[1]
    kw = blk + 2 * rad
    nb = sl // blk
    rep = nh // nkv
    has_sink = sink is not None

    def body(*refs):
        q_ref, k_ref, v_ref, b_ref = refs[:4]
        s_ref = refs[4] if has_sink else None
        o_ref, l_ref, kp, vp = refs[4 + has_sink:]
        _fill_padded(kp, k_ref, rad, sl)
        _fill_padded(vp, v_ref, rad, sl)

        def blk_body(i, carry):
            r0 = pl.multiple_of(i * blk, blk)
            qb = q_ref[pl.ds(r0, blk), :]
            kwin = kp[pl.ds(r0, kw), :]
            vwin = vp[pl.ds(r0, kw), :]
            col = r0 - rad + lax.broadcasted_iota(jnp.int32, (blk, kw), 1)
            neg = jnp.where((col >= 0) & (col < sl), 0.0, NEG_INF).astype(F32)
            for h in range(nh):
                g = h // rep
                hs = slice(h * HEAD_DIM, (h + 1) * HEAD_DIM)
                gs = slice(g * HEAD_DIM, (g + 1) * HEAD_DIM)
                s = lax.dot_general(qb[:, hs], kwin[:, gs], NT, preferred_element_type=F32)
                s = s + b_ref[h] + neg
                m = jnp.max(s, axis=1, keepdims=True)
                if has_sink:
                    sk = s_ref[h][0:1, 0:1]
                    m = jnp.maximum(m, sk)
                p = jnp.exp(s - m)
                den = jnp.sum(p, axis=1, keepdims=True)
                if has_sink:
                    den = den + jnp.exp(sk - m)
                o = jnp.dot(p.astype(BF16), vwin[:, gs], preferred_element_type=F32) / den
                o_ref[pl.ds(r0, blk), hs] = o
                l_ref[pl.ds(r0, blk), hs] = jnp.broadcast_to(m + jnp.log(den), (blk, HEAD_DIM))
            return carry

        lax.fori_loop(0, nb, blk_body, 0)

    in_specs = [_band_spec(sl, qs), _band_spec(sl, ks), _band_spec(sl, vs),
                pl.BlockSpec((nh, blk, kw), lambda b, r: (0, 0, 0))]
    args = [src] * 3 + [bias]
    if has_sink:
        in_specs.append(pl.BlockSpec((nh, 8, 128), lambda b, r: (0, 0, 0)))
        args.append(sink)
    return pl.pallas_call(
        body, name=name, grid=(bl, dil), in_specs=in_specs,
        out_specs=[_band_spec(sl, (256, 0))] * 2,
        out_shape=[_sds((bl, dil, sl, 256), F32)] * 2,
        scratch_shapes=[pltpu.VMEM((sl + 2 * rad, ks[0]), BF16), pltpu.VMEM((sl + 2 * rad, vs[0]), BF16)],
        compiler_params=_cparams(PAR, PAR),
    )(*args)


def _band_bwd(src, qs, ks, vs, bias, sink, dy, dcol, lse, delta, *, rad, nh, nkv, name):
    bl, dil, sl, _ = src.shape
    blk = bias.shape[1]
    kw = blk + 2 * rad
    nb = sl // blk
    rep = nh // nkv
    has_sink = sink is not None
    wk, wv = ks[0], vs[0]

    def body(*refs):
        q_ref, k_ref, v_ref, b_ref = refs[:4]
        s_ref = refs[4] if has_sink else None
        do_ref, l_ref, dl_ref = refs[4 + has_sink:7 + has_sink]
        outs = refs[7 + has_sink:]
        if has_sink:
            dq_ref, dk_ref, dv_ref, db_ref, dsk_ref, kp, vp, dka, dva = outs
        else:
            dq_ref, dk_ref, dv_ref, db_ref, kp, vp, dka, dva = outs

        @pl.when((pl.program_id(0) == 0) & (pl.program_id(1) == 0))
        def _():
            db_ref[...] = jnp.zeros_like(db_ref)
            if has_sink:
                dsk_ref[...] = jnp.zeros_like(dsk_ref)

        _fill_padded(kp, k_ref, rad, sl)
        _fill_padded(vp, v_ref, rad, sl)
        dka[...] = jnp.zeros_like(dka)
        dva[...] = jnp.zeros_like(dva)

        def blk_body(i, carry):
            r0 = pl.multiple_of(i * blk, blk)
            qb = q_ref[pl.ds(r0, blk), :]
            kwin = kp[pl.ds(r0, kw), :]
            vwin = vp[pl.ds(r0, kw), :]
            dob = do_ref[pl.ds(r0, blk), :].astype(BF16)
            lb = l_ref[pl.ds(r0, blk), :]
            dlb = dl_ref[pl.ds(r0, blk), :]
            col = r0 - rad + lax.broadcasted_iota(jnp.int32, (blk, kw), 1)
            neg = jnp.where((col >= 0) & (col < sl), 0.0, NEG_INF).astype(F32)
            for h in range(nh):
                g = h // rep
                hs = slice(h * HEAD_DIM, (h + 1) * HEAD_DIM)
                gs = slice(g * HEAD_DIM, (g + 1) * HEAD_DIM)
                qh, kh, vh, doh = qb[:, hs], kwin[:, gs], vwin[:, gs], dob[:, hs]
                lh = lb[:, h * HEAD_DIM:h * HEAD_DIM + 1]
                dlh = dlb[:, h * HEAD_DIM:h * HEAD_DIM + 1]
                s = lax.dot_general(qh, kh, NT, preferred_element_type=F32) + b_ref[h] + neg
                p = jnp.exp(s - lh)
                dp = lax.dot_general(doh, vh, NT, preferred_element_type=F32)
                ds = p * (dp - dlh)
                dsb = ds.astype(BF16)
                dq_ref[pl.ds(r0, blk), hs] = jnp.dot(dsb, kh, preferred_element_type=F32)
                dka[pl.ds(r0, kw), gs] += lax.dot_general(dsb, qh, TN, preferred_element_type=F32)
                dva[pl.ds(r0, kw), gs] += lax.dot_general(p.astype(BF16), doh, TN, preferred_element_type=F32)
                db_ref[h] += ds
                if has_sink:
                    ps = jnp.exp(s_ref[h][0:1, 0:1] - lh)
                    dsk_ref[h] += jnp.broadcast_to(-jnp.sum(ps * dlh, axis=0, keepdims=True), (8, 128))
            return carry

        lax.fori_loop(0, nb, blk_body, 0)
        dk_ref[...] = dka[rad:rad + sl, :]
        dv_ref[...] = dva[rad:rad + sl, :]

    const3 = lambda b, r: (0, 0, 0)
    in_specs = [_band_spec(sl, qs), _band_spec(sl, ks), _band_spec(sl, vs), pl.BlockSpec((nh, blk, kw), const3)]
    args = [src] * 3 + [bias]
    if has_sink:
        in_specs.append(pl.BlockSpec((nh, 8, 128), const3))
        args.append(sink)
    row = _band_spec(sl, (256, 0))
    in_specs += [_band_spec(sl, (256, dcol)), row, row]
    args += [dy, lse, delta]
    out_specs = [row, _band_spec(sl, (wk, 0)), _band_spec(sl, (wv, 0)), pl.BlockSpec((nh, blk, kw), const3)]
    out_shape = [_sds((bl, dil, sl, 256), F32), _sds((bl, dil, sl, wk), F32), _sds((bl, dil, sl, wv), F32),
                 _sds((nh, blk, kw), F32)]
    if has_sink:
        out_specs.append(pl.BlockSpec((nh, 8, 128), const3))
        out_shape.append(_sds((nh, 8, 128), F32))
    return pl.pallas_call(
        body, name=name, grid=(bl, dil), in_specs=in_specs, out_specs=out_specs, out_shape=out_shape,
        scratch_shapes=[pltpu.VMEM((sl + 2 * rad, wk), BF16), pltpu.VMEM((sl + 2 * rad, wv), BF16),
                        pltpu.VMEM((sl + 2 * rad, wk), F32), pltpu.VMEM((sl + 2 * rad, wv), F32)],
        compiler_params=_cparams(ARB, ARB),
    )(*args)


def _combine_a(os_, ls_, *, tm, name):
    bl = os_[1].shape[0]
    t = bl * SEQ
    nt = SEQ // tm

    def body(o1, o4, o16, l1, l4, l16, y_ref, lt_ref, scr):
        for k, (d, ref) in enumerate(((4, o4), (16, o16), (4, l4), (16, l16))):
            for r in range(d):
                _write_residue(scr, 2 * k, r, d, ref[r])
        o2, o3, b, c = (_gather_cols(scr, 2 * k, 2) for k in range(4))
        a = l1[...]
        m = jnp.maximum(jnp.maximum(a, b), c)
        ea, eb, ec = jnp.exp(a - m), jnp.exp(b - m), jnp.exp(c - m)
        den = ea + eb + ec
        y_ref[...] = (ea / den) * o1[...] + (eb / den) * o2 + (ec / den) * o3
        lt_ref[...] = m + jnp.log(den)

    specs = _residue_specs(tm, 256, nt)
    return pl.pallas_call(
        body, name=name, grid=(bl, nt), in_specs=specs * 2, out_specs=[specs[0]] * 2,
        out_shape=[_sds((t, 256), F32)] * 2, scratch_shapes=[pltpu.VMEM((8, tm, LANES), F32)],
        compiler_params=_cparams(PAR, PAR),
    )(*os_, *ls_)


def _deltas(dycat, ya, yb, yd, lse_a, *, tm, name):
    t = ya.shape[0]
    bl = t // SEQ
    nt = SEQ // tm

    def body(dy_ref, ya_ref, yb_ref, yd_ref, la_ref, dy4, dy16, l4, l16, da1, da4, da16, db_ref, dd_ref, scr):
        e = _group_sum_matrix(256, True)
        dya = dy_ref[:, 0:256]
        dla = _seg_sum(dya * ya_ref[...], e)
        da1[...] = dla
        db_ref[...] = _seg_sum(dy_ref[:, 256:512] * yb_ref[...], e)
        dd_ref[...] = _seg_sum(dy_ref[:, 768:1024] * yd_ref[...], e)
        for k, (val, r4, r16) in enumerate(((dya, dy4, dy16), (la_ref[...], l4, l16), (dla, da4, da16))):
            _scatter_cols(scr, 2 * k, val)
            for d, ref in ((4, r4), (16, r16)):
                for r in range(d):
                    ref[r] = _read_residue(scr, 2 * k, 2, r, d)

    specs = _residue_specs(tm, 256, nt)
    nat = specs[0]
    shapes = _residue_shapes(bl, 256, F32)
    outs = pl.pallas_call(
        body, name=name, grid=(bl, nt),
        in_specs=[pl.BlockSpec((tm, 1024), lambda b, i: (b * nt + i, 0)), nat, nat, nat, nat],
        out_specs=specs[1:] + specs[1:] + specs + [nat, nat],
        out_shape=shapes[1:] + shapes[1:] + shapes + [shapes[0], shapes[0]],
        scratch_shapes=[pltpu.VMEM((6, tm, LANES), F32)],
        compiler_params=_cparams(PAR, PAR),
    )(dycat, ya, yb, yd, lse_a)
    return outs[0:2], outs[2:4], outs[4:7], outs[7], outs[8]


def _dense_fwd(qd, *, tq, name):
    t = qd.shape[0]
    bl = t // SEQ
    nq = SEQ // tq

    def body(q_ref, k_ref, v_ref, o_ref, l_ref):
        q = q_ref[...]
        for g in range(2):
            h0, h1 = 2 * g, 2 * g + 1
            q2 = jnp.concatenate([q[:, h0 * 64:(h0 + 1) * 64], q[:, h1 * 64:(h1 + 1) * 64]], axis=0)
            kg = k_ref[:, g * 64:(g + 1) * 64]
            vg = v_ref[:, g * 64:(g + 1) * 64]
            s = lax.dot_general(q2, kg, NT, preferred_element_type=F32)
            m = jnp.max(s, axis=1, keepdims=True)
            p = jnp.exp(s - m)
            den = jnp.sum(p, axis=1, keepdims=True)
            o2 = jnp.dot(p.astype(BF16), vg, preferred_element_type=F32) / den
            l2 = jnp.broadcast_to(m + jnp.log(den), (2 * tq, 64))
            o_ref[:, h0 * 64:(h0 + 1) * 64] = o2[:tq]
            o_ref[:, h1 * 64:(h1 + 1) * 64] = o2[tq:]
            l_ref[:, h0 * 64:(h0 + 1) * 64] = l2[:tq]
            l_ref[:, h1 * 64:(h1 + 1) * 64] = l2[tq:]

    q3 = qd.reshape(bl, SEQ, 512)
    o, lse = pl.pallas_call(
        body, name=name, grid=(bl, nq),
        in_specs=[pl.BlockSpec((None, tq, 256), lambda b, i: (b, i, 0)),
                  pl.BlockSpec((None, SEQ, 128), lambda b, i: (b, 0, 2)),
                  pl.BlockSpec((None, SEQ, 128), lambda b, i: (b, 0, 3))],
        out_specs=[pl.BlockSpec((None, tq, 256), lambda b, i: (b, i, 0))] * 2,
        out_shape=[_sds((bl, SEQ, 256), F32)] * 2,
        compiler_params=_cparams(PAR, PAR),
    )(q3, q3, q3)
    return o.reshape(t, 256), lse.reshape(t, 256)


def _dense_bwd(qd, dycat, lse, delta, *, tq, name):
    t = qd.shape[0]
    bl = t // SEQ
    nq = SEQ // tq

    def body(q_ref, k_ref, v_ref, do_ref, l_ref, dl_ref, dq_ref, dk_ref, dv_ref):
        @pl.when(pl.program_id(1) == 0)
        def _():
            dk_ref[...] = jnp.zeros_like(dk_ref)
            dv_ref[...] = jnp.zeros_like(dv_ref)

        q = q_ref[...]
        do = do_ref[...].astype(BF16)
        lv = l_ref[...]
        dlv = dl_ref[...]
        for g in range(2):
            h0, h1 = 2 * g, 2 * g + 1
            q2 = jnp.concatenate([q[:, h0 * 64:(h0 + 1) * 64], q[:, h1 * 64:(h1 + 1) * 64]], axis=0)
            do2 = jnp.concatenate([do[:, h0 * 64:(h0 + 1) * 64], do[:, h1 * 64:(h1 + 1) * 64]], axis=0)
            l2 = jnp.concatenate([lv[:, h0 * 64:h0 * 64 + 1], lv[:, h1 * 64:h1 * 64 + 1]], axis=0)
            dl2 = jnp.concatenate([dlv[:, h0 * 64:h0 * 64 + 1], dlv[:, h1 * 64:h1 * 64 + 1]], axis=0)
            kg = k_ref[:, g * 64:(g + 1) * 64]
            vg = v_ref[:, g * 64:(g + 1) * 64]
            s = lax.dot_general(q2, kg, NT, preferred_element_type=F32)
            p = jnp.exp(s - l2)
            dp = lax.dot_general(do2, vg, NT, preferred_element_type=F32)
            ds = (p * (dp - dl2)).astype(BF16)
            dq2 = jnp.dot(ds, kg, preferred_element_type=F32)
            dq_ref[:, h0 * 64:(h0 + 1) * 64] = dq2[:tq]
            dq_ref[:, h1 * 64:(h1 + 1) * 64] = dq2[tq:]
            dk_ref[:, g * 64:(g + 1) * 64] += lax.dot_general(ds, q2, TN, preferred_element_type=F32)
            dv_ref[:, g * 64:(g + 1) * 64] += lax.dot_general(p.astype(BF16), do2, TN, preferred_element_type=F32)

    q3 = qd.reshape(bl, SEQ, 512)
    tile = pl.BlockSpec((None, tq, 256), lambda b, i: (b, i, 0))
    full = pl.BlockSpec((None, SEQ, 128), lambda b, i: (b, 0, 0))
    dq, dk, dv = pl.pallas_call(
        body, name=name, grid=(bl, nq),
        in_specs=[tile, pl.BlockSpec((None, SEQ, 128), lambda b, i: (b, 0, 2)),
                  pl.BlockSpec((None, SEQ, 128), lambda b, i: (b, 0, 3)),
                  pl.BlockSpec((None, tq, 256), lambda b, i: (b, i, 3)), tile, tile],
        out_specs=[tile, full, full],
        out_shape=[_sds((bl, SEQ, 256), F32), _sds((bl, SEQ, 128), F32), _sds((bl, SEQ, 128), F32)],
        compiler_params=_cparams(PAR, ARB),
    )(q3, q3, q3, dycat.reshape(bl, SEQ, 1024), lse.reshape(bl, SEQ, 256), delta.reshape(bl, SEQ, 256))
    return dq.reshape(t, 256), dk.reshape(t, 128), dv.reshape(t, 128)


def _c_norm(cv, gam, bet):
    vg = _gelu(cv)
    mu = jnp.mean(vg, axis=-1, keepdims=True)
    xc = vg - mu
    r = lax.rsqrt(jnp.mean(xc * xc, axis=-1, keepdims=True) + EPS)
    xhat = xc * r
    return xhat * gam + bet, xhat, r


def _c_fwd(proj, gam, bet, ws, bst, *, tm, name):
    t = proj.shape[0]
    nch = tm // C_CHUNK

    def body(u_ref, v_ref, g_ref, b_ref, ws_ref, bs_ref, y_ref):
        vn, _, _ = _c_norm(v_ref[...], g_ref[...], b_ref[...])
        vnb = vn.astype(BF16)
        for c in range(nch):
            rows = slice(c * C_CHUNK, (c + 1) * C_CHUNK)
            for g in range(C_GROUPS):
                gs = slice(g * 64, (g + 1) * 64)
                mixed = jnp.dot(ws_ref[g], vnb[rows, gs], preferred_element_type=F32) + bs_ref[:, gs]
                y_ref[rows, gs] = _gelu(u_ref[rows, gs]) * mixed

    vec = pl.BlockSpec((1, 256), lambda i: (0, 0))
    return pl.pallas_call(
        body, name=name, grid=(t // tm,),
        in_specs=[pl.BlockSpec((tm, 256), lambda i: (i, 5)), pl.BlockSpec((tm, 256), lambda i: (i, 6)), vec, vec,
                  pl.BlockSpec((C_GROUPS, C_CHUNK, C_CHUNK), lambda i: (0, 0, 0)),
                  pl.BlockSpec((C_CHUNK, 256), lambda i: (0, 0))],
        out_specs=pl.BlockSpec((tm, 256), lambda i: (i, 0)), out_shape=_sds((t, 256), F32),
        compiler_params=_cparams(PAR),
    )(proj, proj, gam, bet, ws, bst)


def _c_bwd(proj, dycat, gam, bet, ws, wst, bst, *, tm, name):
    t = proj.shape[0]
    nch = tm // C_CHUNK
    nstep = t // tm

    def body(u_ref, v_ref, dy_ref, g_ref, b_ref, ws_ref, wst_ref, bs_ref,
             du_ref, dv_ref, dws_ref, dbs_ref, dg_ref, db_ref, dvn_s):
        step = pl.program_id(0)

        @pl.when(step == 0)
        def _():
            dws_ref[...] = jnp.zeros_like(dws_ref)
            dbs_ref[...] = jnp.zeros_like(dbs_ref)
            dg_ref[...] = jnp.zeros_like(dg_ref)
            db_ref[...] = jnp.zeros_like(db_ref)

        cv = v_ref[...]
        gam_v = g_ref[...]
        vn, xhat, r = _c_norm(cv, gam_v, b_ref[...])
        vnb = vn.astype(BF16)
        for c in range(nch):
            rows = slice(c * C_CHUNK, (c + 1) * C_CHUNK)
            for g in range(C_GROUPS):
                gs = slice(g * 64, (g + 1) * 64)
                cu = u_ref[rows, gs]
                dy = dy_ref[rows, gs]
                mixed = jnp.dot(ws_ref[g], vnb[rows, gs], preferred_element_type=F32) + bs_ref[:, gs]
                du_ref[rows, gs] = dy * mixed * _gelu_grad(cu)
                dmix = dy * _gelu(cu)
                dbs_ref[:, gs] += dmix
                dmb = dmix.astype(BF16)
                dws_ref[g] += lax.dot_general(dmb, vnb[rows, gs], NT, preferred_element_type=F32)
                dvn_s[rows, gs] = jnp.dot(wst_ref[g], dmb, preferred_element_type=F32)
        dvn = dvn_s[...]
        dg_ref[...] += jnp.sum(dvn * xhat, axis=0, keepdims=True)
        db_ref[...] += jnp.sum(dvn, axis=0, keepdims=True)
        dxh = dvn * gam_v
        dvg = r * (dxh - jnp.mean(dxh, axis=-1, keepdims=True) - xhat * jnp.mean(dxh * xhat, axis=-1, keepdims=True))
        dv_ref[...] = dvg * _gelu_grad(cv)

        @pl.when(step == nstep - 1)
        def _():
            dbs_ref[...] = _seg_sum(dbs_ref[...], _group_sum_matrix(256, True))

    vec = pl.BlockSpec((1, 256), lambda i: (0, 0))
    mat = pl.BlockSpec((C_GROUPS, C_CHUNK, C_CHUNK), lambda i: (0, 0, 0))
    bsp = pl.BlockSpec((C_CHUNK, 256), lambda i: (0, 0))
    tile = pl.BlockSpec((tm, 256), lambda i: (i, 0))
    return pl.pallas_call(
        body, name=name, grid=(nstep,),
        in_specs=[pl.BlockSpec((tm, 256), lambda i: (i, 5)), pl.BlockSpec((tm, 256), lambda i: (i, 6)),
                  pl.BlockSpec((tm, 256), lambda i: (i, 2)), vec, vec, mat, mat, bsp],
        out_specs=[tile, tile, mat, bsp, vec, vec],
        out_shape=[_sds((t, 256), F32), _sds((t, 256), F32), _sds((C_GROUPS, C_CHUNK, C_CHUNK), F32),
                   _sds((C_CHUNK, 256), F32), _sds((1, 256), F32), _sds((1, 256), F32)],
        scratch_shapes=[pltpu.VMEM((tm, 256), F32)],
        compiler_params=_cparams(ARB),
    )(proj, proj, dycat, gam, bet, ws, wst, bst)


FF_TC = 128
FF_NB = D_FF // FF_TC
FF_CH = 64
FF_HALO = 16


def _edge_taps(ref, first):
    if first:
        ext = ref[0:FF_CH + FF_HALO, :].astype(F32)
        body = slice(0, FF_CH)
    else:
        ext = ref[SEQ - FF_CH - FF_HALO:SEQ, :].astype(F32)
        body = slice(FF_HALO, FF_HALO + FF_CH)
    n = ext.shape[0]
    row = lax.broadcasted_iota(jnp.int32, ext.shape, 0)
    dn = pltpu.roll(ext, 1, 0)
    up = pltpu.roll(ext, n - 1, 0)
    if first:
        dn = jnp.where(row == 0, 0.0, dn)
    else:
        up = jnp.where(row == n - 1, 0.0, up)
    return dn[body], ext[body], up[body]


def _mid_taps(ref, r0):
    ext = ref[pl.ds(pl.multiple_of(r0 - FF_HALO, FF_HALO), FF_CH + 2 * FF_HALO), :].astype(F32)
    n = ext.shape[0]
    body = slice(FF_HALO, FF_HALO + FF_CH)
    return pltpu.roll(ext, 1, 0)[body], ext[body], pltpu.roll(ext, n - 1, 0)[body]


def _chunk_loop(step):
    step(0, lambda ref: _edge_taps(ref, True))

    def mid(i, carry):
        r0 = pl.multiple_of(i * FF_CH, FF_CH)
        step(r0, lambda ref: _mid_taps(ref, r0))
        return carry

    lax.fori_loop(1, SEQ // FF_CH - 1, mid, 0)
    step(SEQ - FF_CH, lambda ref: _edge_taps(ref, False))


def _conv3(taps, w_ref, b_ref):
    dn, md, up = taps
    return w_ref[0:1, :] * dn + w_ref[1:2, :] * md + w_ref[2:3, :] * up + b_ref[...]


def _ff_specs(order):
    def at(fn):
        return (lambda b, j: fn(b, j)) if order == "bj" else (lambda j, b: fn(b, j))
    hs = [pl.BlockSpec((None, SEQ, FF_TC), at(lambda b, j, o=o: (b, 0, j + o))) for o in (0, FF_NB)]
    ws = [pl.BlockSpec((3, FF_TC), at(lambda b, j, o=o: (0, j + o))) for o in (0, FF_NB)]
    bs = [pl.BlockSpec((1, FF_TC), at(lambda b, j, o=o: (0, j + o))) for o in (0, FF_NB)]
    return hs, ws, bs


def _conv_gate_fwd(h, cw, cb, *, name):
    t = h.shape[0]
    bl = t // SEQ

    def body(hg_ref, hu_ref, wg_ref, wu_ref, bg_ref, bu_ref, a_ref):
        def step(r0, taps):
            cg = _conv3(taps(hg_ref), wg_ref, bg_ref)
            cu = _conv3(taps(hu_ref), wu_ref, bu_ref)
            a_ref[pl.ds(r0, FF_CH), :] = (cg * _sigmoid(cg) * cu).astype(BF16)

        _chunk_loop(step)

    hs, ws, bs = _ff_specs("bj")
    h3 = h.reshape(bl, SEQ, 2 * D_FF)
    act = pl.pallas_call(
        body, name=name, grid=(bl, FF_NB), in_specs=hs + ws + bs,
        out_specs=pl.BlockSpec((None, SEQ, FF_TC), lambda b, j: (b, 0, j)),
        out_shape=_sds((bl, SEQ, D_FF), BF16),
        compiler_params=_cparams(PAR, PAR),
    )(h3, h3, cw, cw, cb, cb)
    return act.reshape(t, D_FF)


def _conv_gate_bwd(h, dact, cw, cb, *, name):
    t = h.shape[0]
    bl = t // SEQ

    def body(hg_ref, hu_ref, wg_ref, wu_ref, bg_ref, bu_ref, da_ref,
             dhg_ref, dhu_ref, dwg_ref, dwu_ref, dbg_ref, dbu_ref, dg_s, du_s):
        @pl.when(pl.program_id(1) == 0)
        def _():
            for ref in (dwg_ref, dwu_ref, dbg_ref, dbu_ref):
                ref[...] = jnp.zeros_like(ref)

        red = lambda x: jnp.sum(x, axis=0, keepdims=True)

        def pass1(r0, taps):
            tg, tu = taps(hg_ref), taps(hu_ref)
            cg = _conv3(tg, wg_ref, bg_ref)
            cu = _conv3(tu, wu_ref, bu_ref)
            da = da_ref[pl.ds(r0, FF_CH), :].astype(F32)
            sg = _sigmoid(cg)
            dcg = da * cu * (sg * (1.0 + cg * (1.0 - sg)))
            dcu = da * (cg * sg)
            dg_s[pl.ds(r0, FF_CH), :] = dcg
            du_s[pl.ds(r0, FF_CH), :] = dcu
            for d, tp, dw_ref, db_ref in ((dcg, tg, dwg_ref, dbg_ref), (dcu, tu, dwu_ref, dbu_ref)):
                for k in range(3):
                    dw_ref[k:k + 1, :] += red(d * tp[k])
                db_ref[...] += red(d)

        _chunk_loop(pass1)

        def pass2(r0, taps):
            for s, w_ref, o_ref in ((dg_s, wg_ref, dhg_ref), (du_s, wu_ref, dhu_ref)):
                dn, md, up = taps(s)
                o_ref[pl.ds(r0, FF_CH), :] = (w_ref[0:1, :] * up + w_ref[1:2, :] * md + w_ref[2:3, :] * dn).astype(BF16)

        _chunk_loop(pass2)

    hs, ws, bs = _ff_specs("jb")
    half = pl.BlockSpec((None, SEQ, FF_TC), lambda j, b: (b, 0, j))
    wsp = pl.BlockSpec((3, FF_TC), lambda j, b: (0, j))
    bsp = pl.BlockSpec((1, FF_TC), lambda j, b: (0, j))
    h3 = h.reshape(bl, SEQ, 2 * D_FF)
    dhg, dhu, dwg, dwu, dbg, dbu = pl.pallas_call(
        body, name=name, grid=(FF_NB, bl), in_specs=hs + ws + bs + [half],
        out_specs=[half, half, wsp, wsp, bsp, bsp],
        out_shape=[_sds((bl, SEQ, D_FF), BF16), _sds((bl, SEQ, D_FF), BF16), _sds((3, D_FF), F32), _sds((3, D_FF), F32),
                   _sds((1, D_FF), F32), _sds((1, D_FF), F32)],
        scratch_shapes=[pltpu.VMEM((SEQ, FF_TC), F32), pltpu.VMEM((SEQ, FF_TC), F32)],
        compiler_params=_cparams(PAR, ARB),
    )(h3, h3, cw, cw, cb, cb, dact.reshape(bl, SEQ, D_FF))
    return (dhg.reshape(t, D_FF), dhu.reshape(t, D_FF), jnp.concatenate([dwg, dwu], axis=1),
            jnp.concatenate([dbg, dbu], axis=1))


def _ple_fwd(x2, gain, wg, pe, pe_blk, wp, *, tm, tn, name):
    t, k = x2.shape
    n = wg.shape[1]

    def body(x_ref, g_ref, wg_ref, pe_ref, wp_ref, xr_ref, hn_ref, x3_ref, gt_ref, pp_ref, hn_s):
        @pl.when(pl.program_id(1) == 0)
        def _():
            x = x_ref[...]
            r = lax.rsqrt(jnp.mean(x * x, axis=-1, keepdims=True) + EPS)
            hn_s[...] = (x * r * g_ref[...]).astype(BF16)
            hn_ref[...] = hn_s[...]

        gate = _sigmoid(jnp.dot(hn_s[...], wg_ref[...], preferred_element_type=F32))
        pp = jnp.dot(pe_ref[...].astype(BF16), wp_ref[...], preferred_element_type=F32)
        gt_ref[...] = gate.astype(BF16)
        pp_ref[...] = pp.astype(BF16)
        x3_ref[...] = xr_ref[...] + pp * gate

    tile = pl.BlockSpec((tm, tn), lambda i, j: (i, j))
    return pl.pallas_call(
        body, name=name, grid=(t // tm, n // tn),
        in_specs=[pl.BlockSpec((tm, k), lambda i, j: (i, 0)), pl.BlockSpec((1, k), lambda i, j: (0, 0)),
                  pl.BlockSpec((k, tn), lambda i, j: (0, j)), pl.BlockSpec((tm, PLE_DIM), lambda i, j: (pe_blk + i, 0)),
                  pl.BlockSpec((PLE_DIM, tn), lambda i, j: (0, j)), tile],
        out_specs=[pl.BlockSpec((tm, k), lambda i, j: (i, 0)), tile, tile, tile],
        out_shape=[_sds((t, k), BF16), _sds((t, n), F32), _sds((t, n), BF16), _sds((t, n), BF16)],
        scratch_shapes=[pltpu.VMEM((tm, k), BF16)],
        compiler_params=_cparams(PAR, ARB),
    )(x2, gain, wg, pe, wp, x2)


def _ple_bwd_ew(dx3, gate, pp, *, tm, name):
    t, n = dx3.shape

    def body(d_ref, g_ref, p_ref, dz_ref, dpp_ref):
        d, g = d_ref[...], g_ref[...]
        dz_ref[...] = (d * p_ref[...] * g * (1.0 - g)).astype(BF16)
        dpp_ref[...] = (d * g).astype(BF16)

    spec = pl.BlockSpec((tm, n), lambda i: (i, 0))
    return pl.pallas_call(
        body, name=name, grid=(t // tm,), in_specs=[spec] * 3, out_specs=[spec] * 2,
        out_shape=[_sds((t, n), BF16)] * 2, compiler_params=_cparams(PAR),
    )(dx3, gate, pp)


def _loss_head(y, tgt, *, tm, name):
    t, d = y.shape

    def body(y_ref, t_ref, l_ref, dy_ref):
        @pl.when(pl.program_id(0) == 0)
        def _():
            l_ref[...] = jnp.zeros_like(l_ref)

        e = y_ref[...] - t_ref[...]
        dy_ref[...] = e * (1.0 / d)
        s = jnp.sum(jnp.sum(e * e, axis=1, keepdims=True), axis=0, keepdims=True)
        l_ref[...] += jnp.broadcast_to(s * (0.5 / d), (8, 128))

    spec = pl.BlockSpec((tm, d), lambda i: (i, 0))
    return pl.pallas_call(
        body, name=name, grid=(t // tm,), in_specs=[spec, spec],
        out_specs=[pl.BlockSpec((8, 128), lambda i: (0, 0)), spec],
        out_shape=[_sds((8, 128), F32), _sds((t, d), F32)], compiler_params=_cparams(ARB),
    )(y, tgt)


BIAS_PC = 8192


def _onehot(bucket_row):
    rows = lax.broadcasted_iota(jnp.int32, (REL_BUCKETS, bucket_row.shape[1]), 0)
    return (rows == bucket_row).astype(BF16)


def _dot3(x, onehot, dims):
    acc = None
    for _ in range(3):
        term = x.astype(BF16)
        part = lax.dot_general(term, onehot, dims, preferred_element_type=F32)
        acc = part if acc is None else acc + part
        x = x - term.astype(F32)
    return acc


def _bias_lookup(table_t, bucket, *, name):
    h = table_t.shape[0]
    p = bucket.shape[1]

    def body(t_ref, b_ref, o_ref):
        bk = b_ref[...]
        val = _dot3(t_ref[...], _onehot(bk), (((1,), (0,)), ((), ())))
        o_ref[...] = jnp.where(bk >= 0, val, NEG_INF)

    return pl.pallas_call(
        body, name=name, grid=(p // BIAS_PC,),
        in_specs=[pl.BlockSpec((h, REL_BUCKETS), lambda i: (0, 0)), pl.BlockSpec((1, BIAS_PC), lambda i: (0, i))],
        out_specs=pl.BlockSpec((h, BIAS_PC), lambda i: (0, i)), out_shape=_sds((h, p), F32),
        compiler_params=_cparams(PAR),
    )(table_t, bucket)


def _bucket_reduce(dbiases, bucket, *, name):
    h, p = dbiases[0].shape
    nl = len(dbiases)

    def body(*refs):
        b_ref, o_ref = refs[nl], refs[nl + 1]

        @pl.when(pl.program_id(0) == 0)
        def _():
            o_ref[...] = jnp.zeros_like(o_ref)

        d = refs[0][...]
        for d_ref in refs[1:nl]:
            d = d + d_ref[...]
        o_ref[...] += _dot3(d, _onehot(b_ref[...]), NT)

    return pl.pallas_call(
        body, name=name, grid=(p // BIAS_PC,),
        in_specs=[pl.BlockSpec((h, BIAS_PC), lambda i: (0, i))] * nl + [pl.BlockSpec((1, BIAS_PC), lambda i: (0, i))],
        out_specs=pl.BlockSpec((h, REL_BUCKETS), lambda i: (0, 0)), out_shape=_sds((h, REL_BUCKETS), F32),
        compiler_params=_cparams(ARB),
    )(*dbiases, bucket)


def _adamw_math(w, g, m, v):
    m = ADAM_B1 * m + (1.0 - ADAM_B1) * g
    v = ADAM_B2 * v + (1.0 - ADAM_B2) * (g * g)
    m_hat = m / (1.0 - ADAM_B1 ** ADAM_STEP)
    v_hat = v / (1.0 - ADAM_B2 ** ADAM_STEP)
    delta = -ADAM_LR * (m_hat / (jnp.sqrt(v_hat) + ADAM_EPS) + ADAM_WD * w)
    return delta, m, v


def _adamw_reduce(parts, w, m, v, *, tr, name):
    nl = len(parts)
    rows, c = w.shape
    r = rows // nl
    nt = r // tr

    def body(*refs):
        p_refs = refs[:nl]
        w_ref, m_ref, v_ref, g_ref, d_ref, nm_ref, nv_ref = refs[nl:]
        for li, p_ref in enumerate(p_refs):
            @pl.when(pl.program_id(0) == li)
            def _(p_ref=p_ref):
                g = p_ref[0].astype(F32)
                for k in range(1, N_DEV):
                    g = g + p_ref[k].astype(F32)
                d, nm, nv = _adamw_math(w_ref[...], g, m_ref[...], v_ref[...])
                g_ref[...] = g
                d_ref[...] = d
                nm_ref[...] = nm
                nv_ref[...] = nv

    def part_map(li):
        return lambda l, i: (0, jnp.where(l == li, i, jnp.where(l < li, 0, nt - 1)), 0)

    spec = pl.BlockSpec((tr, c), lambda l, i: (l * nt + i, 0))
    return pl.pallas_call(
        body, name=name, grid=(nl, nt),
        in_specs=[pl.BlockSpec((N_DEV, tr, c), part_map(li)) for li in range(nl)] + [spec, spec, spec],
        out_specs=[spec] * 4, out_shape=[_sds((rows, c), F32)] * 4, compiler_params=_cparams(ARB, ARB),
    )(*parts, w, m, v)


def _adamw_plain(g, w, m, v, *, name):
    def body(g_ref, w_ref, m_ref, v_ref, d_ref, nm_ref, nv_ref):
        d, nm, nv = _adamw_math(w_ref[...], g_ref[...], m_ref[...], v_ref[...])
        d_ref[...] = d
        nm_ref[...] = nm
        nv_ref[...] = nv

    return pl.pallas_call(body, name=name, out_shape=[_sds(w.shape, F32)] * 3)(g, w, m, v)


def _mesh_pos():
    return lax.axis_index("x"), lax.axis_index("y"), lax.axis_index("c")


def _allgather_body(x_refs, out_refs, send_sems, recv_sems, local_sems, slot):
    x, y, c = _mesh_pos()
    me, sibling = (x, y, c), (x, y, 1 - c)
    chips = [(1 - x, y), (x, 1 - y), (1 - x, 1 - y)]
    waits = []
    for a, (x_ref, out_ref) in enumerate(zip(x_refs, out_refs)):
        def copy(k, block, to, src=None, out_ref=out_ref, a=a):
            return pltpu.make_async_remote_copy(
                src_ref=slot(out_ref, block) if src is None else src, dst_ref=slot(out_ref, block),
                send_sem=send_sems.at[a, k], recv_sem=recv_sems.at[a, k], device_id=to, device_id_type=MESH)

        mine = pltpu.make_async_copy(x_ref, slot(out_ref, me), local_sems.at[a])
        mine.start()
        first = [copy(0, me, sibling, src=x_ref)]
        first += [copy(1 + j, me, (*chip, c), src=x_ref) for j, chip in enumerate(chips)]
        for cp in first:
            cp.start()
        waits.append((copy, mine, first))
    sends = []
    for copy, mine, first in waits:
        passed = [copy(4 + j, (*chip, c), sibling) for j, chip in enumerate(chips)]
        for j, chip in enumerate(chips):
            copy(1 + j, (*chip, c), me).wait_recv()
            passed[j].start()
        sends.append(passed)
    for (copy, mine, first), passed in zip(waits, sends):
        copy(0, sibling, me).wait_recv()
        for j, chip in enumerate(chips):
            copy(4 + j, (*chip, 1 - c), me).wait_recv()
        for cp in first + passed:
            cp.wait_send()
        mine.wait()


PEER_FLIPS = ((0, 0, 1), (1, 0, 0), (0, 1, 0), (1, 1, 0), (1, 0, 1), (0, 1, 1), (1, 1, 1))


def _peer_copies(x_refs, land_refs, send_sem, recv_sem, scatter):
    x, y, c = _mesh_pos()
    me = 4 * x + 2 * y + c
    copies = []
    for x_ref, land_ref in zip(x_refs, land_refs):
        for fx, fy, fc in PEER_FLIPS:
            px, py, pc = x ^ fx, y ^ fy, c ^ fc
            src = x_ref.at[4 * px + 2 * py + pc] if scatter else x_ref
            copies.append(pltpu.make_async_remote_copy(
                src_ref=src, dst_ref=land_ref.at[me], send_sem=send_sem, recv_sem=recv_sem,
                device_id=(px, py, pc), device_id_type=MESH))
    return copies


def _sc_exchange(xs, *, scatter, collective_id, name):
    na = len(xs)
    land_shapes = [x.shape if scatter else (N_DEV,) + x.shape for x in xs]

    def body(*refs):
        x_refs, land_refs = refs[:na], refs[na:2 * na]
        send_sem, recv_sem, local_sem = refs[2 * na:]
        x, y, c = _mesh_pos()
        me = 4 * x + 2 * y + c
        barrier = pltpu.get_barrier_semaphore()
        for fx, fy, fc in PEER_FLIPS:
            pl.semaphore_signal(barrier, inc=1, device_id=(x ^ fx, y ^ fy, c ^ fc), device_id_type=MESH)
        pl.semaphore_wait(barrier, len(PEER_FLIPS))
        for x_ref, land_ref in zip(x_refs, land_refs):
            own = pltpu.make_async_copy(x_ref.at[me] if scatter else x_ref, land_ref.at[me], local_sem)
            own.start()
            own.wait()
        copies = _peer_copies(x_refs, land_refs, send_sem, recv_sem, scatter)
        for cp in copies:
            cp.start()
        for cp in copies:
            cp.wait()

    return pl.kernel(
        body, name=name, out_type=[_sds(s, x.dtype) for s, x in zip(land_shapes, xs)],
        mesh=plsc.ScalarSubcoreMesh(axis_name="sequencer", num_cores=1),
        scratch_types=[pltpu.SemaphoreType.DMA, pltpu.SemaphoreType.DMA, pltpu.SemaphoreType.DMA],
        compiler_params=pltpu.CompilerParams(collective_id=collective_id),
    )(*xs)


def _sc_allgather(xs, *, collective_id, name):
    na = len(xs)

    def body(*refs):
        x_refs, out_refs = refs[:na], refs[na:2 * na]
        send_sems, recv_sems, local_sems = refs[2 * na:]
        x, y, c = _mesh_pos()
        barrier = pltpu.get_barrier_semaphore()
        for fx, fy, fc in PEER_FLIPS:
            pl.semaphore_signal(barrier, inc=1, device_id=(x ^ fx, y ^ fy, c ^ fc), device_id_type=MESH)
        pl.semaphore_wait(barrier, len(PEER_FLIPS))
        _allgather_body(x_refs, out_refs, send_sems, recv_sems, local_sems,
                        lambda ref, pos: ref.at[4 * pos[0] + 2 * pos[1] + pos[2]])

    return pl.kernel(
        body, name=name, out_type=[_sds((N_DEV,) + x.shape, x.dtype) for x in xs],
        mesh=plsc.ScalarSubcoreMesh(axis_name="sequencer", num_cores=1),
        scratch_types=[pltpu.SemaphoreType.DMA((na, 7)), pltpu.SemaphoreType.DMA((na, 7)),
                       pltpu.SemaphoreType.DMA((na,))],
        compiler_params=pltpu.CompilerParams(collective_id=collective_id),
    )(*xs)


def _allgather_vmem(x, *, reduce, name):
    r, c = x.shape

    def body(x_ref, out_ref, *rest):
        if reduce:
            gath, send_sems, recv_sems, local_sems = rest
        else:
            send_sems, recv_sems, local_sems = rest
            gath = out_ref
        _allgather_body([x_ref], [gath], send_sems, recv_sems, local_sems,
                        lambda ref, pos: ref.at[pl.ds((4 * pos[0] + 2 * pos[1] + pos[2]) * r, r), :])
        if reduce:
            acc = gath[0:r, :]
            for k in range(1, N_DEV):
                acc = acc + gath[k * r:(k + 1) * r, :]
            out_ref[...] = acc

    vm = pl.BlockSpec(memory_space=pltpu.VMEM)
    scratch = [pltpu.SemaphoreType.DMA((1, 7)), pltpu.SemaphoreType.DMA((1, 7)), pltpu.SemaphoreType.DMA((1,))]
    if reduce:
        scratch = [pltpu.VMEM((N_DEV * r, c), x.dtype)] + scratch
    return pl.pallas_call(
        body, name=name, in_specs=[vm], out_specs=vm,
        out_shape=_sds((r, c) if reduce else (N_DEV * r, c), x.dtype), scratch_shapes=scratch,
    )(x)


def _t5_bucket(rel):
    nb = REL_BUCKETS // 2
    ret = jnp.where(rel > 0, nb, 0)
    n = jnp.abs(rel)
    max_exact = nb // 2
    nf = jnp.maximum(n, 1).astype(F32)
    large = max_exact + (jnp.log(nf / max_exact) / math.log(REL_MAX_DIST / max_exact)
                         * (nb - max_exact)).astype(jnp.int32)
    large = jnp.minimum(large, nb - 1)
    return ret + jnp.where(n < max_exact, n, large)


def _band_pattern(block, radius, dil):
    kw = block + 2 * radius
    rel = jnp.arange(kw)[None, :] - radius - jnp.arange(block)[:, None]
    return jnp.where(jnp.abs(rel) <= radius, _t5_bucket(rel * dil), -1).astype(jnp.int32).reshape(1, block * kw)


def _rope_tables():
    lane = np.arange(64)
    seg, j = lane // 32, lane % 32
    inv = ROPE_THETA ** (-jnp.arange(0, 32, 2, dtype=F32) / 32)
    tpos = jnp.arange(SEQ)
    pos = jnp.where(jnp.asarray(seg)[None, :] == 0, (tpos // GRID_W)[:, None], (tpos % GRID_W)[:, None])
    ang = pos.astype(F32) * inv[jnp.asarray(j % 16)][None, :]
    cos = jnp.cos(ang)
    sins = jnp.where(jnp.asarray(j)[None, :] < 16, -jnp.sin(ang), jnp.sin(ang))
    return jnp.tile(cos, (1, 4)), jnp.tile(sins, (1, 4))


A_Q, A_K, A_V = (256, 0), (256, 1), (256, 2)
B_Q, B_K, B_V = (256, 0), (128, 2), (128, 3)
A_HEADS = dict(rad=A_RADIUS, nh=4, nkv=4)
B_HEADS = dict(rad=SWA_RADIUS, nh=4, nkv=2)


def _pin(arr, token):
    return arr if token is None else arr + token[0:1, 0:1]


def _local_step(x, pe, tgt, rel_bias, wts, matmul_weights, grads_ready):
    t = x.shape[0]
    bl = t // SEQ
    cos, sins = _rope_tables()
    blocks_a = [min(BAND_BLOCK, SEQ // d) for d in DILATIONS]
    pats_a = [_band_pattern(blk, A_RADIUS, d) for blk, d in zip(blocks_a, DILATIONS)]
    pat_b = _band_pattern(BAND_BLOCK, SWA_RADIUS, 1)
    table_t = rel_bias.T
    bias_a = [_bias_lookup(table_t[:4], pt, name=f"bias_a{ci}").reshape(4, blk, blk + 2 * A_RADIUS)
              for ci, (pt, blk) in enumerate(zip(pats_a, blocks_a))]
    bias_b = _bias_lookup(table_t[4:], pat_b, name="bias_b").reshape(4, BAND_BLOCK, BAND_BLOCK + 2 * SWA_RADIUS)
    nat4 = lambda a: a.reshape(bl, 1, SEQ, a.shape[-1])

    saved = []
    for li in range(DEPTH):
        w = dict(wts[li])
        w.update(matmul_weights(li, "in", x)[0])
        hn0, proj = _norm_mm((x,), w["g_mix"], w["w_in"], None, tm=1024, tn=1152, name="mix_in_fwd")
        qa1, qa4, qa16, qb, qd = _qkprep_fwd(proj, w["qk_gains"], cos, sins, tm=512, name="qkprep_fwd")
        qa = (nat4(qa1), qa4, qa16)
        oa, la = [], []
        for ci in range(3):
            o, l = _band_fwd(qa[ci], A_Q, A_K, A_V, bias_a[ci], None, name=f"band_a{ci}_fwd", **A_HEADS)
            oa.append(o)
            la.append(l)
        oa[0], la[0] = oa[0].reshape(t, 256), la[0].reshape(t, 256)
        ya, lse_a = _combine_a(oa, la, tm=512, name="combine_a")
        yb, lse_b = _band_fwd(nat4(qb), B_Q, B_K, B_V, bias_b, w["sink_t"], name="band_b_fwd", **B_HEADS)
        yb = yb.reshape(t, 256)
        yc = _c_fwd(proj, w["c_g"], w["c_b"], w["c_ws"], w["c_bst"], tm=512, name="c_fwd")
        yd, lse_d = _dense_fwd(qd, tq=128, name="dense_fwd")
        more, started = matmul_weights(li, "rest", yd)
        w.update(more)
        w["out_gain"] = _pin(w["out_gain"], started)
        mixed, x1 = _norm_mm((ya, yb, yc, yd), w["out_gain"], w["w_out"], x, tm=1024, tn=1024, name="mix_out_fwd")
        hn1, h = _norm_mm((x1,), w["g_ffn"], w["w_up"], None, tm=1024, tn=1408, name="ffn_up_fwd", out_dtype=BF16)
        act = _conv_gate_fwd(h, w["conv_w"], w["conv_b"], name="conv_gate_fwd")
        x2 = _mm(act, w["w_down"], "nn", x1, tm=1024, tn=1024, out_dtype=F32, name="ffn_down_fwd")
        hn2, x3, gate, pp = _ple_fwd(x2, w["g_ple"], w["w_gate"], pe, li * (t // 1024), w["w_proj"], tm=1024, tn=512,
                                     name="ple_fwd")
        saved.append(dict(w=w, x0=x, hn0=hn0, proj=proj, qa=qa, qb=qb, qd=qd, ya=ya, lse_a=lse_a, yb=yb, lse_b=lse_b,
                          yc=yc, yd=yd, lse_d=lse_d, mixed=mixed, x1=x1, hn1=hn1, h=h, act=act, x2=x2, hn2=hn2,
                          gate=gate, pp=pp))
        x = x3

    loss_tile, dx = _loss_head(x, tgt, tm=512, name="loss_head")
    grads = [None] * DEPTH
    dbias_a, dbias_bs = [[], [], []], []
    token = None
    for li in reversed(range(DEPTH)):
        s = saved[li]
        w = s["w"]
        g = {}
        w["g_ple"] = _pin(w["g_ple"], token)
        dz, dpp = _ple_bwd_ew(dx, s["gate"], s["pp"], tm=512, name="ple_bwd_ew")
        g["w_gate"] = _mm(s["hn2"], dz, "tn", None, tm=1024, tn=512, out_dtype=BF16, name="dw_gate")
        g["w_proj"] = _mm(pe, dpp, "tn", None, tm=256, tn=1024, out_dtype=BF16, name="dw_proj", a_rows=(li, t))
        dx2, dx2b, g["g_ple"] = _mm_bt_normbwd((dz,), w["w_gate"], (s["x2"],), w["g_ple"], dx, tm=1024, tn=1024,
                                               name="ple_bwd", emit_bf16=True)
        g["w_down"] = _mm(s["act"], dx2b, "tn", None, tm=1408, tn=512, out_dtype=BF16, name="dw_down")
        dact = _mm(dx2b, w["w_down"], "nt", None, tm=1024, tn=1408, out_dtype=BF16, name="ffn_down_bwd")
        dhg, dhu, g["conv_w"], g["conv_b"] = _conv_gate_bwd(s["h"], dact, w["conv_w"], w["conv_b"], name="conv_gate_bwd")
        g["w_up"] = jnp.concatenate(
            [_mm(s["hn1"], dhalf, "tn", None, tm=1024, tn=1408, out_dtype=BF16, name=f"dw_up_{nm}")
             for nm, dhalf in (("gate", dhg), ("up", dhu))], axis=1)
        dx1, dx1b, g["g_ffn"] = _mm_bt_normbwd((dhg, dhu), w["w_up"], (s["x1"],), w["g_ffn"], dx2, tm=1024, tn=1408,
                                               name="ffn_up_bwd", emit_bf16=True)
        g["w_out"] = _mm(s["mixed"], dx1b, "tn", None, tm=1024, tn=512, out_dtype=BF16, name="dw_out")
        out_gain = _pin(w["out_gain"], grads_ready(li, "mid", g))
        dycat, g["out_gain"] = _mm_bt_normbwd((dx1b,), w["w_out"], (s["ya"], s["yb"], s["yc"], s["yd"]), out_gain,
                                              None, tm=1024, tn=1024, name="mix_out_bwd")
        dy_r, lse_r, dl_a, dl_b, dl_d = _deltas(dycat, s["ya"], s["yb"], s["yd"], s["lse_a"], tm=512, name="deltas")
        dy_a = (nat4(dycat),) + tuple(dy_r)
        lse_a = (nat4(s["lse_a"]),) + tuple(lse_r)
        dl_a = (nat4(dl_a[0]),) + tuple(dl_a[1:])
        da = []
        for ci in range(3):
            dq, dk, dv, dbias = _band_bwd(s["qa"][ci], A_Q, A_K, A_V, bias_a[ci], None, dy_a[ci], 0, lse_a[ci],
                                          dl_a[ci], name=f"band_a{ci}_bwd", **A_HEADS)
            if ci == 0:
                dq, dk, dv = (a.reshape(t, 256) for a in (dq, dk, dv))
            da.append((dq, dk, dv))
            dbias_a[ci].append(dbias.reshape(4, -1))
        dqb, dkb, dvb, dbias_b, dsink = _band_bwd(nat4(s["qb"]), B_Q, B_K, B_V, bias_b, w["sink_t"], nat4(dycat), 1,
                                                  nat4(s["lse_b"]), nat4(dl_b), name="band_b_bwd", **B_HEADS)
        dbias_bs.append(dbias_b.reshape(4, -1))
        g["sink"] = dsink[:, 0, 0]
        dd = _dense_bwd(s["qd"], dycat, s["lse_d"], dl_d, tq=128, name="dense_bwd")
        dcu, dcv, g["c_ws"], dbs, g["c_g"], g["c_b"] = _c_bwd(s["proj"], dycat, w["c_g"], w["c_b"], w["c_ws"],
                                                               w["c_wst"], w["c_bst"], tm=512, name="c_bwd")
        g["c_bs"] = dbs[:, ::64].T
        db = (dqb.reshape(t, 256), dkb.reshape(t, 128), dvb.reshape(t, 128))
        dproj, dgains = _qkprep_bwd(s["proj"], da, db, dd, dcu, dcv, w["qk_gains"], cos, sins, tm=512, name="qkprep_bwd")
        g["qk_gain"] = dgains[:6, :64].reshape(3, 2, HEAD_DIM)
        g["w_in"] = _mm(s["hn0"], dproj, "tn", None, tm=1024, tn=1152, out_dtype=BF16, name="dw_in")
        dx, g["g_mix"] = _mm_bt_normbwd((dproj,), w["w_in"], (s["x0"],), w["g_mix"], dx1, tm=1024, tn=1152,
                                        name="mix_in_bwd")
        grads[li] = g
        token = grads_ready(li, "end", g)
    d_table_a = sum(_bucket_reduce(dbias_a[ci], pats_a[ci], name=f"bucket_a{ci}") for ci in range(3))
    d_table_b = _bucket_reduce(dbias_bs, pat_b, name="bucket_b")
    d_rel_bias = jnp.concatenate([d_table_a, d_table_b], axis=0).T
    return loss_tile[0, 0], dx, grads, d_rel_bias


WEIGHT_NAMES = ("rel_bias", "ln_mix_g", "w_in", "qk_gain", "sink", "c_norm_g", "c_norm_b", "c_ws", "c_bs", "out_gain",
                "w_out", "ln_ffn_g", "w_up", "conv_w", "conv_b", "w_down", "ln_ple_g", "w_ple_gate", "w_ple_proj")
COL_SHARDED = ("w_in", "w_up", "w_ple_proj")
ROW_SHARDED = ("w_out", "w_down", "w_ple_gate")
SMALL_SHARDED = ("conv_w", "out_gain")
REPLICATED = tuple(n for n in WEIGHT_NAMES if n not in COL_SHARDED + ROW_SHARDED + SMALL_SHARDED)
LOCAL_GRAD_KEY = {"ln_mix_g": "g_mix", "ln_ffn_g": "g_ffn", "ln_ple_g": "g_ple", "c_norm_g": "c_g", "c_norm_b": "c_b",
                  "w_ple_gate": "w_gate", "w_ple_proj": "w_proj"}


def _full_from_gathered(name, gathered):
    _, r, c = gathered.shape
    if name in ROW_SHARDED:
        return gathered.reshape(N_DEV * r, c)
    return jnp.transpose(gathered, (1, 0, 2)).reshape(r, N_DEV * c)


def _slots_from_full(name, full):
    rows, cols = full.shape
    if name in ROW_SHARDED:
        return full.reshape(N_DEV, rows // N_DEV, cols)
    return jnp.transpose(full.reshape(rows, N_DEV, cols // N_DEV), (1, 0, 2))


def _piece_rows(shape):
    return -(-int(np.prod(shape)) // 1024) * 8


def _pack_rows(arrays):
    pieces = []
    for a in arrays:
        n, rows = int(np.prod(a.shape)), _piece_rows(a.shape)
        flat = a.astype(F32).reshape(-1)
        if n != rows * LANES:
            flat = jnp.pad(flat, (0, rows * LANES - n))
        pieces.append(flat.reshape(rows, LANES))
    return jnp.concatenate(pieces, axis=0)


def _unpack_rows(packed, shapes):
    out, off = [], 0
    for shp in shapes:
        n, rows = int(np.prod(shp)), _piece_rows(shp)
        piece = packed[off:off + rows]
        out.append((piece if n == rows * LANES else piece.reshape(-1)[:n]).reshape(shp))
        off += rows
    return out


def kernel(x, p, rel_bias, ln_mix_g, w_in, qk_gain, sink, c_norm_g, c_norm_b, c_ws, c_bs, out_gain, w_out, ln_ffn_g, w_up, conv_w, conv_b, w_down, ln_ple_g, w_ple_gate, w_ple_proj, loss_target, m_rel_bias, m_ln_mix_g, m_w_in, m_qk_gain, m_sink, m_c_norm_g, m_c_norm_b, m_c_ws, m_c_bs, m_out_gain, m_w_out, m_ln_ffn_g, m_w_up, m_conv_w, m_conv_b, m_w_down, m_ln_ple_g, m_w_ple_gate, m_w_ple_proj, v_rel_bias, v_ln_mix_g, v_w_in, v_qk_gain, v_sink, v_c_norm_g, v_c_norm_b, v_c_ws, v_c_bs, v_out_gain, v_w_out, v_ln_ffn_g, v_w_up, v_conv_w, v_conv_b, v_w_down, v_ln_ple_g, v_w_ple_gate, v_w_ple_proj):
    env = dict(locals())
    wt = {n: env[n] for n in WEIGHT_NAMES}
    mom_m = {n: env["m_" + n] for n in WEIGHT_NAMES}
    mom_v = {n: env["v_" + n] for n in WEIGHT_NAMES}
    bl = x.shape[0]
    t = bl * SEQ
    me = 4 * lax.axis_index("x") + 2 * lax.axis_index("y") + lax.axis_index("c")

    big = COL_SHARDED + ROW_SHARDED
    full = {}
    small_shapes = [wt[n].shape for n in SMALL_SHARDED]
    small = _allgather_vmem(_pack_rows([wt[n] for n in SMALL_SHARDED]), reduce=False, name="gather_small")
    small = small.reshape(N_DEV, -1)
    off = 0
    for n, shp in zip(SMALL_SHARDED, small_shapes):
        cnt = int(np.prod(shp))
        g = small[:, off:off + cnt].reshape((N_DEV,) + tuple(shp))
        full[n] = jnp.transpose(g, (1, 2, 0, 3)).reshape(shp[0], shp[1], N_DEV * shp[2])
        off += _piece_rows(shp) * LANES

    def head_gain(li, a, b, reps):
        g = jnp.tile(qk_gain[li, a, b], reps)
        return jnp.pad(g, (0, 256 - g.shape[0]))

    wts = []
    for li in range(DEPTH):
        rows = [head_gain(li, 0, 0, 4), head_gain(li, 0, 1, 4), head_gain(li, 1, 0, 4), head_gain(li, 1, 1, 2),
                head_gain(li, 2, 0, 4), head_gain(li, 2, 1, 2), jnp.zeros((256,), F32), jnp.zeros((256,), F32)]
        wts.append(dict(
            g_mix=ln_mix_g[li].reshape(1, -1), qk_gains=jnp.stack(rows),
            sink_t=jnp.broadcast_to(sink[li][:, None, None], (4, 8, 128)),
            c_g=c_norm_g[li].reshape(1, -1), c_b=c_norm_b[li].reshape(1, -1), c_ws=c_ws[li].astype(BF16),
            c_wst=jnp.transpose(c_ws[li], (0, 2, 1)).astype(BF16), c_bst=jnp.repeat(c_bs[li].T, 64, axis=1),
            out_gain=full["out_gain"][li].reshape(1, -1), g_ffn=ln_ffn_g[li].reshape(1, -1),
            conv_w=full["conv_w"][li], conv_b=conv_b[li].reshape(1, -1), g_ple=ln_ple_g[li].reshape(1, -1)))

    local_key = {"w_ple_gate": "w_gate", "w_ple_proj": "w_proj"}

    gather_names = {"in": ("w_in",), "rest": tuple(n for n in big if n != "w_in")}
    gathered = {}
    for cid, (li, names) in enumerate(((0, gather_names["in"]), (0, gather_names["rest"]), (1, big))):
        lands = _sc_allgather([wt[n][li].astype(BF16) for n in names], collective_id=cid,
                              name=f"gather_{li}_{len(names)}")
        gathered.setdefault(li, {}).update(zip(names, lands))

    def matmul_weights(li, part, after):
        out = {}
        for n in gather_names[part]:
            g, _ = lax.optimization_barrier((gathered[li][n], after))
            out[local_key.get(n, n)] = _full_from_gathered(n, g)
        return out, None

    mid_names = ("w_ple_gate", "w_ple_proj", "w_down", "w_up", "w_out")
    end_names = ("w_in",)
    landed = {}

    def start_exchange(li, names, g, tag, cid):
        slots = [_slots_from_full(n, g[local_key.get(n, n)]) for n in names]
        lands = _sc_exchange(slots, scatter=True, collective_id=cid, name=f"grads_{li}_{tag}")
        landed.update({(n, li): land for n, land in zip(names, lands)})

    def grads_ready(li, stage, g):
        if li == 0:
            start_exchange(li, mid_names if stage == "mid" else end_names, g, stage, 5 if stage == "mid" else 6)
        elif stage == "end":
            start_exchange(li, mid_names + end_names, g, stage, 4)
        return None

    loss_part, dx, grads, d_rel_bias = _local_step(
        x.reshape(t, D_MODEL), p.reshape(DEPTH * t, PLE_DIM), loss_target.reshape(t, D_MODEL), rel_bias, wts,
        matmul_weights, grads_ready)
    loss = lax.psum(loss_part, ("x", "y", "c"))

    def local_grad(n):
        if n == "rel_bias":
            return d_rel_bias
        key = LOCAL_GRAD_KEY.get(n, n)
        return jnp.stack([grads[li][key].reshape(wt[n].shape[1:]) if n in REPLICATED else grads[li][key]
                          for li in range(DEPTH)])

    out_g, out_d, out_m, out_v = {}, {}, {}, {}
    for n in big:
        shp = wt[n].shape
        two_d = lambda a: a.reshape(-1, shp[-1])
        res = _adamw_reduce([landed[n, li] for li in range(DEPTH)], two_d(wt[n]), two_d(mom_m[n]), two_d(mom_v[n]),
                            tr=32 if n == "w_down" else 128, name="adamw_" + n)
        out_g[n], out_d[n], out_m[n], out_v[n] = [r.reshape(shp) for r in res]

    small_names = REPLICATED + SMALL_SHARDED
    small_full_shapes = [wt[n].shape if n in REPLICATED else full[n].shape for n in small_names]
    reduced = _allgather_vmem(_pack_rows([local_grad(n) for n in small_names]), reduce=True, name="allreduce_small")
    reduced = dict(zip(small_names, _unpack_rows(reduced, small_full_shapes)))
    rep_shapes = [wt[n].shape for n in REPLICATED]
    upd = _adamw_plain(_pack_rows([reduced[n] for n in REPLICATED]), _pack_rows([wt[n] for n in REPLICATED]),
                       _pack_rows([mom_m[n] for n in REPLICATED]), _pack_rows([mom_v[n] for n in REPLICATED]),
                       name="adamw_replicated")
    for dst, packed in zip((out_d, out_m, out_v), upd):
        dst.update(zip(REPLICATED, _unpack_rows(packed, rep_shapes)))
    for n in REPLICATED:
        out_g[n] = reduced[n]
    for n in SMALL_SHARDED:
        shp = wt[n].shape
        g = reduced[n].reshape(shp[0], shp[1], N_DEV, shp[2])
        g = lax.dynamic_index_in_dim(g, me, axis=2, keepdims=False)
        two_d = lambda a: a.reshape(-1, shp[-1])
        res = _adamw_plain(two_d(g), two_d(wt[n]), two_d(mom_m[n]), two_d(mom_v[n]), name="adamw_" + n)
        out_g[n] = g
        out_d[n], out_m[n], out_v[n] = [r.reshape(shp) for r in res]

    return (loss, dx.reshape(bl, SEQ, D_MODEL), *[out_g[n] for n in WEIGHT_NAMES], *[out_d[n] for n in WEIGHT_NAMES],
            *[out_m[n] for n in WEIGHT_NAMES], *[out_v[n] for n in WEIGHT_NAMES])
```

```python
import math

import jax
import jax.numpy as jnp
import numpy as np
from jax import lax
from jax.experimental import pallas as pl
from jax.experimental.pallas import tpu as pltpu
from jax.experimental.pallas import tpu_sc as plsc

F32 = jnp.float32
BF16 = jnp.bfloat16
HI = lax.Precision.HIGHEST

N_DEV = 8
D_MODEL = 1024
SEQ = 2048
DEPTH = 2
HEAD_DIM = 64
IN_WIDTH = 2304
D_FF = 2816
PLE_DIM = 256
C_CHUNK = 128
C_GROUPS = 4
DILATED_CFGS = ((128, 1), (512, 4), (2048, 16))
DILATIONS = tuple(d for _, d in DILATED_CFGS)
A_RADIUS = 64
SWA_RADIUS = 128
BAND_BLOCK = 128
GRID_W = 64
ROPE_THETA = 10000.0
REL_BUCKETS = 32
REL_MAX_DIST = 1024
EPS = 1e-6
NEG_INF = -1e30
ATTN_SCALE = HEAD_DIM ** -0.5
LANES = 128

ADAM_LR = 0.001
ADAM_B1 = 0.9
ADAM_B2 = 0.999
ADAM_EPS = 1e-08
ADAM_WD = 0.01
ADAM_STEP = 10

MESH = pl.DeviceIdType.MESH
NT = (((1,), (1,)), ((), ()))
TN = (((0,), (0,)), ((), ()))
ARB = "arbitrary"
PAR = "parallel"


def _cparams(*sem):
    return pltpu.CompilerParams(dimension_semantics=tuple(sem))


def _sds(shape, dtype):
    return jax.ShapeDtypeStruct(tuple(shape), dtype)


def _group_sum_matrix(n, same_group):
    r = lax.broadcasted_iota(jnp.int32, (n, n), 0)
    c = lax.broadcasted_iota(jnp.int32, (n, n), 1)
    if same_group:
        return ((r >> 6) == (c >> 6)).astype(F32)
    return ((r & 63) == (c & 63)).astype(F32)


def _seg_sum(x, e):
    eb = e.astype(BF16)
    hi = x.astype(BF16)
    lo = (x - hi.astype(F32)).astype(BF16)
    return jnp.dot(hi, eb, preferred_element_type=F32) + jnp.dot(lo, eb, preferred_element_type=F32)


def _gelu(x):
    c = math.sqrt(2.0 / math.pi)
    return 0.5 * x * (1.0 + jnp.tanh(c * (x + 0.044715 * (x * x * x))))


def _gelu_grad(x):
    c = math.sqrt(2.0 / math.pi)
    t = jnp.tanh(c * (x + 0.044715 * (x * x * x)))
    return 0.5 * (1.0 + t) + 0.5 * x * (1.0 - t * t) * c * (1.0 + 3.0 * 0.044715 * (x * x))


def _sigmoid(x):
    return 1.0 / (1.0 + jnp.exp(-x))


def _scatter_cols(scratch, first, val):
    for c in range(val.shape[1] // LANES):
        scratch[first + c] = val[:, c * LANES:(c + 1) * LANES]


def _gather_cols(scratch, first, ncol):
    return jnp.concatenate([scratch[first + c] for c in range(ncol)], axis=1)


def _read_residue(scratch, first, ncol, r, d):
    n = scratch.shape[1] // d
    return jnp.concatenate([scratch.at[first + c][pl.ds(r, n, stride=d), :] for c in range(ncol)], axis=1)


def _write_residue(scratch, first, r, d, val):
    n = scratch.shape[1] // d
    for c in range(val.shape[1] // LANES):
        scratch.at[first + c][pl.ds(r, n, stride=d), :] = val[:, c * LANES:(c + 1) * LANES]


def _norm_mm(xs, gain, w, res, *, tm, tn, name, out_dtype=F32):
    t = xs[0].shape[0]
    k = sum(x.shape[1] for x in xs)
    n = w.shape[1]
    ng = len(xs)
    has_res = res is not None

    def body(*refs):
        x_refs = refs[:ng]
        g_ref, w_ref = refs[ng], refs[ng + 1]
        res_ref = refs[ng + 2] if has_res else None
        hn_ref, o_ref, hn_s = refs[ng + 2 + has_res:]

        @pl.when(pl.program_id(1) == 0)
        def _():
            off = 0
            for xr in x_refs:
                x = xr[...]
                wd = x.shape[1]
                r = lax.rsqrt(jnp.mean(x * x, axis=-1, keepdims=True) + EPS)
                hn_s[:, off:off + wd] = (x * r * g_ref[:, off:off + wd]).astype(BF16)
                off += wd
            hn_ref[...] = hn_s[...]

        acc = jnp.dot(hn_s[...], w_ref[...], preferred_element_type=F32)
        if has_res:
            acc = acc + res_ref[...]
        o_ref[...] = acc.astype(out_dtype)

    in_specs = [pl.BlockSpec((tm, x.shape[1]), lambda i, j: (i, 0)) for x in xs]
    in_specs += [pl.BlockSpec((1, k), lambda i, j: (0, 0)), pl.BlockSpec((k, tn), lambda i, j: (0, j))]
    args = list(xs) + [gain, w]
    if has_res:
        in_specs.append(pl.BlockSpec((tm, tn), lambda i, j: (i, j)))
        args.append(res)
    return pl.pallas_call(
        body, name=name, grid=(t // tm, n // tn), in_specs=in_specs,
        out_specs=[pl.BlockSpec((tm, k), lambda i, j: (i, 0)), pl.BlockSpec((tm, tn), lambda i, j: (i, j))],
        out_shape=[_sds((t, k), BF16), _sds((t, n), out_dtype)],
        scratch_shapes=[pltpu.VMEM((tm, k), BF16)],
        compiler_params=_cparams(PAR, ARB),
    )(*args)


def _mm(a, b, mode, res, *, tm, tn, out_dtype, name, a_rows=None):
    if mode == "tn":
        kk, m = a.shape
        blk_a = 0
        if a_rows is not None:
            blk_a, kk = a_rows
        a_spec = pl.BlockSpec((kk, tm), lambda i, j: (blk_a, i))
    else:
        m, kk = a.shape
        a_spec = pl.BlockSpec((tm, kk), lambda i, j: (i, 0))
    if mode == "nt":
        n = b.shape[0]
        b_spec = pl.BlockSpec((tn, kk), lambda i, j: (j, 0))
    else:
        n = b.shape[1]
        b_spec = pl.BlockSpec((kk, tn), lambda i, j: (0, j))
    has_res = res is not None

    def body(*refs):
        a_ref, b_ref = refs[0], refs[1]
        o_ref = refs[-1]
        av = a_ref[...].astype(BF16)
        bv = b_ref[...].astype(BF16)
        if mode == "nn":
            acc = jnp.dot(av, bv, preferred_element_type=F32)
        elif mode == "nt":
            acc = lax.dot_general(av, bv, NT, preferred_element_type=F32)
        else:
            acc = lax.dot_general(av, bv, TN, preferred_element_type=F32)
        if has_res:
            acc = acc + refs[2][...]
        o_ref[...] = acc.astype(out_dtype)

    in_specs = [a_spec, b_spec]
    args = [a, b]
    if has_res:
        in_specs.append(pl.BlockSpec((tm, tn), lambda i, j: (i, j)))
        args.append(res)
    return pl.pallas_call(
        body, name=name, grid=(m // tm, n // tn), in_specs=in_specs,
        out_specs=pl.BlockSpec((tm, tn), lambda i, j: (i, j)),
        out_shape=_sds((m, n), out_dtype),
        compiler_params=_cparams(PAR, PAR),
    )(*args)


def _mm_bt_normbwd(dys, w, xs, gain, dres, *, tm, tn, name, emit_bf16=False):
    t, wd_each = dys[0].shape
    nd = len(dys)
    per = wd_each // tn
    nj = nd * per
    k = w.shape[0]
    ng = len(xs)
    has_res = dres is not None

    def body(*refs):
        dy_refs = refs[:nd]
        w_ref = refs[nd]
        x_refs = refs[nd + 1:nd + 1 + ng]
        g_ref = refs[nd + 1 + ng]
        dres_ref = refs[nd + 2 + ng] if has_res else None
        outs = refs[nd + 2 + ng + has_res:]
        dx_ref = outs[0]
        dxb_ref = outs[1] if emit_bf16 else None
        dg_ref, acc = outs[1 + emit_bf16:]
        i, j = pl.program_id(0), pl.program_id(1)

        @pl.when(j == 0)
        def _():
            acc[...] = jnp.zeros_like(acc)

        for d, dy_ref in enumerate(dy_refs):
            @pl.when((j >= d * per) & (j < (d + 1) * per))
            def _(dy_ref=dy_ref):
                acc[...] += lax.dot_general(dy_ref[...].astype(BF16), w_ref[...], NT, preferred_element_type=F32)

        @pl.when(j == nj - 1)
        def _():
            @pl.when(i == 0)
            def _():
                dg_ref[...] = jnp.zeros_like(dg_ref)

            off = 0
            for xr in x_refs:
                x = xr[...]
                wd = x.shape[1]
                g = g_ref[:, off:off + wd]
                dyn = acc[:, off:off + wd]
                r = lax.rsqrt(jnp.mean(x * x, axis=-1, keepdims=True) + EPS)
                gdy = dyn * g
                dx = r * gdy - x * (r * r * r * jnp.mean(gdy * x, axis=-1, keepdims=True))
                if has_res:
                    dx = dx + dres_ref[:, off:off + wd]
                dx_ref[:, off:off + wd] = dx
                if emit_bf16:
                    dxb_ref[:, off:off + wd] = dx.astype(BF16)
                dg_ref[:, off:off + wd] += jnp.sum(dyn * x * r, axis=0, keepdims=True)
                off += wd

    def dy_map(d):
        return lambda i, j: (i, jnp.clip(j - d * per, 0, per - 1))

    in_specs = [pl.BlockSpec((tm, tn), dy_map(d)) for d in range(nd)]
    in_specs.append(pl.BlockSpec((k, tn), lambda i, j: (0, j)))
    in_specs += [pl.BlockSpec((tm, x.shape[1]), lambda i, j: (i, 0)) for x in xs]
    in_specs.append(pl.BlockSpec((1, k), lambda i, j: (0, 0)))
    args = list(dys) + [w] + list(xs) + [gain]
    if has_res:
        in_specs.append(pl.BlockSpec((tm, k), lambda i, j: (i, 0)))
        args.append(dres)
    row = pl.BlockSpec((tm, k), lambda i, j: (i, 0))
    out_specs = [row] + ([row] if emit_bf16 else []) + [pl.BlockSpec((1, k), lambda i, j: (0, 0))]
    out_shape = [_sds((t, k), F32)] + ([_sds((t, k), BF16)] if emit_bf16 else []) + [_sds((1, k), F32)]
    return pl.pallas_call(
        body, name=name, grid=(t // tm, nj), in_specs=in_specs, out_specs=out_specs, out_shape=out_shape,
        scratch_shapes=[pltpu.VMEM((tm, k), F32)],
        compiler_params=_cparams(ARB, ARB),
    )(*args)


def _rope_partner(y):
    n = y.shape[1]
    lane = lax.broadcasted_iota(jnp.int32, y.shape, 1)
    return jnp.where((lane & 31) < 16, pltpu.roll(y, n - 16, 1), pltpu.roll(y, 16, 1))


def _residue_specs(tm, width, nt):
    specs = [pl.BlockSpec((tm, width), lambda b, i: (b * nt + i, 0))]
    for d in DILATIONS[1:]:
        specs.append(pl.BlockSpec((None, d, tm // d, width), lambda b, i: (b, 0, i, 0)))
    return specs


def _residue_shapes(bl, width, dtype):
    return [_sds((bl * SEQ, width), dtype)] + [_sds((bl, d, SEQ // d, width), dtype) for d in DILATIONS[1:]]


def _qkprep_fwd(proj, gains, cos, sins, *, tm, name):
    t = proj.shape[0]
    bl = t // SEQ
    nt = SEQ // tm

    def body(p_ref, g_ref, c_ref, s_ref, qa1_ref, qa4_ref, qa16_ref, qb_ref, qd_ref, scr):
        e = _group_sum_matrix(256, True)

        def hn(x, row):
            wd = x.shape[1]
            ms = _seg_sum(x * x, e[:wd, :wd]) * (1.0 / HEAD_DIM)
            return x * lax.rsqrt(ms + EPS) * g_ref[row:row + 1, :wd]

        qa = jnp.concatenate([hn(p_ref[:, 0:256], 0) * ATTN_SCALE, hn(p_ref[:, 256:512], 1), p_ref[:, 512:768]], axis=1)
        qa1_ref[...] = qa.astype(BF16)
        _scatter_cols(scr, 0, qa)
        for d, ref in ((4, qa4_ref), (16, qa16_ref)):
            for r in range(d):
                ref[r] = _read_residue(scr, 0, 6, r, d).astype(BF16)
        qb_ref[:, 0:256] = (hn(p_ref[:, 768:1024], 2) * ATTN_SCALE).astype(BF16)
        qb_ref[:, 256:384] = hn(p_ref[:, 1024:1152], 3).astype(BF16)
        qb_ref[:, 384:512] = p_ref[:, 1152:1280].astype(BF16)
        yq = hn(p_ref[:, 1792:2048], 4)
        yq = yq * c_ref[...] + _rope_partner(yq) * s_ref[...]
        qd_ref[:, 0:256] = (yq * ATTN_SCALE).astype(BF16)
        yk = hn(p_ref[:, 2048:2176], 5)
        yk = yk * c_ref[:, 0:128] + _rope_partner(yk) * s_ref[:, 0:128]
        qd_ref[:, 256:384] = yk.astype(BF16)
        qd_ref[:, 384:512] = p_ref[:, 2176:2304].astype(BF16)

    row = lambda width: pl.BlockSpec((tm, width), lambda b, i: (b * nt + i, 0))
    tab = pl.BlockSpec((tm, 256), lambda b, i: (i, 0))
    return pl.pallas_call(
        body, name=name, grid=(bl, nt),
        in_specs=[row(IN_WIDTH), pl.BlockSpec((8, 256), lambda b, i: (0, 0)), tab, tab],
        out_specs=_residue_specs(tm, 768, nt) + [row(512), row(512)],
        out_shape=_residue_shapes(bl, 768, BF16) + [_sds((t, 512), BF16), _sds((t, 512), BF16)],
        scratch_shapes=[pltpu.VMEM((6, tm, LANES), F32)],
        compiler_params=_cparams(PAR, PAR),
    )(proj, gains, cos, sins)


def _qkprep_bwd(proj, da, db, dd, dcu, dcv, gains, cos, sins, *, tm, name):
    t = proj.shape[0]
    bl = t // SEQ
    nt = SEQ // tm
    flat = [a for cfg in da for a in cfg] + list(db) + list(dd) + [dcu, dcv]

    def body(*refs):
        p_ref, g_ref, c_ref, s_ref = refs[:4]
        d_refs = refs[4:4 + len(flat)]
        dp_ref, dg_ref, scr = refs[4 + len(flat):]
        a_refs = d_refs[:9]
        dqb_ref, dkb_ref, dvb_ref, dqd_ref, dkd_ref, dvd_ref, dcu_ref, dcv_ref = d_refs[9:]
        e = _group_sum_matrix(256, True)
        first = (pl.program_id(0) == 0) & (pl.program_id(1) == 0)
        last = (pl.program_id(0) == bl - 1) & (pl.program_id(1) == nt - 1)

        @pl.when(first)
        def _():
            dg_ref[...] = jnp.zeros_like(dg_ref)

        def hn_bwd(x, dy, row):
            wd = x.shape[1]
            ee = e[:wd, :wd]
            g = g_ref[row:row + 1, :wd]
            r = lax.rsqrt(_seg_sum(x * x, ee) * (1.0 / HEAD_DIM) + EPS)
            gdy = dy * g
            dx = r * gdy - x * (r * r * r * (_seg_sum(gdy * x, ee) * (1.0 / HEAD_DIM)))
            dg_ref[row:row + 1, :wd] += jnp.sum(dy * x * r, axis=0, keepdims=True)
            return dx

        def rope_bwd(dy, wd):
            return dy * c_ref[:, :wd] + _rope_partner(dy * s_ref[:, :wd])

        dqkv = jnp.concatenate([a_refs[0][...], a_refs[1][...], a_refs[2][...]], axis=1)
        for ci, d in ((1, 4), (2, 16)):
            for r in range(d):
                part = jnp.concatenate([a_refs[3 * ci + m][r] for m in range(3)], axis=1)
                _write_residue(scr, 0, r, d, part)
            dqkv = dqkv + _gather_cols(scr, 0, 6)
        dp_ref[:, 0:256] = hn_bwd(p_ref[:, 0:256], dqkv[:, 0:256] * ATTN_SCALE, 0).astype(BF16)
        dp_ref[:, 256:512] = hn_bwd(p_ref[:, 256:512], dqkv[:, 256:512], 1).astype(BF16)
        dp_ref[:, 512:768] = dqkv[:, 512:768].astype(BF16)
        dp_ref[:, 768:1024] = hn_bwd(p_ref[:, 768:1024], dqb_ref[...] * ATTN_SCALE, 2).astype(BF16)
        dp_ref[:, 1024:1152] = hn_bwd(p_ref[:, 1024:1152], dkb_ref[...], 3).astype(BF16)
        dp_ref[:, 1152:1280] = dvb_ref[...].astype(BF16)
        dp_ref[:, 1280:1536] = dcu_ref[...].astype(BF16)
        dp_ref[:, 1536:1792] = dcv_ref[...].astype(BF16)
        dp_ref[:, 1792:2048] = hn_bwd(p_ref[:, 1792:2048], rope_bwd(dqd_ref[...] * ATTN_SCALE, 256), 4).astype(BF16)
        dp_ref[:, 2048:2176] = hn_bwd(p_ref[:, 2048:2176], rope_bwd(dkd_ref[...], 128), 5).astype(BF16)
        dp_ref[:, 2176:2304] = dvd_ref[...].astype(BF16)

        @pl.when(last)
        def _():
            dg_ref[...] = _seg_sum(dg_ref[...], _group_sum_matrix(256, False))

    row = lambda width: pl.BlockSpec((tm, width), lambda b, i: (b * nt + i, 0))
    tab = pl.BlockSpec((tm, 256), lambda b, i: (i, 0))
    in_specs = [row(IN_WIDTH), pl.BlockSpec((8, 256), lambda b, i: (0, 0)), tab, tab]
    res_specs = _residue_specs(tm, 256, nt)
    in_specs += [res_specs[ci] for ci in range(3) for _ in range(3)]
    in_specs += [row(a.shape[1]) for a in flat[9:]]
    return pl.pallas_call(
        body, name=name, grid=(bl, nt), in_specs=in_specs,
        out_specs=[row(IN_WIDTH), pl.BlockSpec((8, 256), lambda b, i: (0, 0))],
        out_shape=[_sds((t, IN_WIDTH), BF16), _sds((8, 256), F32)],
        scratch_shapes=[pltpu.VMEM((6, tm, LANES), F32)],
        compiler_params=_cparams(ARB, ARB),
    )(proj, gains, cos, sins, *flat)


def _band_spec(seq_len, spec):
    width, idx = spec
    return pl.BlockSpec((None, None, seq_len, width), lambda b, r: (b, r, 0, idx))


def _fill_padded(dst, src_ref, rad, seq_len):
    z = jnp.zeros((rad, dst.shape[1]), dst.dtype)
    dst[0:rad, :] = z
    dst[rad + seq_len:rad + seq_len + rad, :] = z
    dst[rad:rad + seq_len, :] = src_ref[...]


def _band_fwd(src, qs, ks, vs, bias, sink, *, rad, nh, nkv, name):
    bl, dil, sl, _ = src.shape
    blk = bias.shape[1]
    kw = blk + 2 * rad
    nb = sl // blk
    rep = nh // nkv
    has_sink = sink is not None

    def body(*refs):
        q_ref, k_ref, v_ref, b_ref = refs[:4]
        s_ref = refs[4] if has_sink else None
        o_ref, l_ref, kp, vp = refs[4 + has_sink:]
        _fill_padded(kp, k_ref, rad, sl)
        _fill_padded(vp, v_ref, rad, sl)

        def blk_body(i, carry):
            r0 = pl.multiple_of(i * blk, blk)
            qb = q_ref[pl.ds(r0, blk), :]
            kwin = kp[pl.ds(r0, kw), :]
            vwin = vp[pl.ds(r0, kw), :]
            col = r0 - rad + lax.broadcasted_iota(jnp.int32, (blk, kw), 1)
            neg = jnp.where((col >= 0) & (col < sl), 0.0, NEG_INF).astype(F32)
            for h in range(nh):
                g = h // rep
                hs = slice(h * HEAD_DIM, (h + 1) * HEAD_DIM)
                gs = slice(g * HEAD_DIM, (g + 1) * HEAD_DIM)
                s = lax.dot_general(qb[:, hs], kwin[:, gs], NT, preferred_element_type=F32)
                s = s + b_ref[h] + neg
                m = jnp.max(s, axis=1, keepdims=True)
                if has_sink:
                    sk = s_ref[h][0:1, 0:1]
                    m = jnp.maximum(m, sk)
                p = jnp.exp(s - m)
                den = jnp.sum(p, axis=1, keepdims=True)
                if has_sink:
                    den = den + jnp.exp(sk - m)
                o = jnp.dot(p.astype(BF16), vwin[:, gs], preferred_element_type=F32) / den
                o_ref[pl.ds(r0, blk), hs] = o
                l_ref[pl.ds(r0, blk), hs] = jnp.broadcast_to(m + jnp.log(den), (blk, HEAD_DIM))
            return carry

        lax.fori_loop(0, nb, blk_body, 0)

    in_specs = [_band_spec(sl, qs), _band_spec(sl, ks), _band_spec(sl, vs),
                pl.BlockSpec((nh, blk, kw), lambda b, r: (0, 0, 0))]
    args = [src] * 3 + [bias]
    if has_sink:
        in_specs.append(pl.BlockSpec((nh, 8, 128), lambda b, r: (0, 0, 0)))
        args.append(sink)
    return pl.pallas_call(
        body, name=name, grid=(bl, dil), in_specs=in_specs,
        out_specs=[_band_spec(sl, (256, 0))] * 2,
        out_shape=[_sds((bl, dil, sl, 256), F32)] * 2,
        scratch_shapes=[pltpu.VMEM((sl + 2 * rad, ks[0]), BF16), pltpu.VMEM((sl + 2 * rad, vs[0]), BF16)],
        compiler_params=_cparams(PAR, PAR),
    )(*args)


def _band_bwd(src, qs, ks, vs, bias, sink, dy, dcol, lse, delta, *, rad, nh, nkv, name):
    bl, dil, sl, _ = src.shape
    blk = bias.shape[1]
    kw = blk + 2 * rad
    nb = sl // blk
    rep = nh // nkv
    has_sink = sink is not None
    wk, wv = ks[0], vs[0]

    def body(*refs):
        q_ref, k_ref, v_ref, b_ref = refs[:4]
        s_ref = refs[4] if has_sink else None
        do_ref, l_ref, dl_ref = refs[4 + has_sink:7 + has_sink]
        outs = refs[7 + has_sink:]
        if has_sink:
            dq_ref, dk_ref, dv_ref, db_ref, dsk_ref, kp, vp, dka, dva = outs
        else:
            dq_ref, dk_ref, dv_ref, db_ref, kp, vp, dka, dva = outs

        @pl.when((pl.program_id(0) == 0) & (pl.program_id(1) == 0))
        def _():
            db_ref[...] = jnp.zeros_like(db_ref)
            if has_sink:
                dsk_ref[...] = jnp.zeros_like(dsk_ref)

        _fill_padded(kp, k_ref, rad, sl)
        _fill_padded(vp, v_ref, rad, sl)
        dka[...] = jnp.zeros_like(dka)
        dva[...] = jnp.zeros_like(dva)

        def blk_body(i, carry):
            r0 = pl.multiple_of(i * blk, blk)
            qb = q_ref[pl.ds(r0, blk), :]
            kwin = kp[pl.ds(r0, kw), :]
            vwin = vp[pl.ds(r0, kw), :]
            dob = do_ref[pl.ds(r0, blk), :].astype(BF16)
            lb = l_ref[pl.ds(r0, blk), :]
            dlb = dl_ref[pl.ds(r0, blk), :]
            col = r0 - rad + lax.broadcasted_iota(jnp.int32, (blk, kw), 1)
            neg = jnp.where((col >= 0) & (col < sl), 0.0, NEG_INF).astype(F32)
            for h in range(nh):
                g = h // rep
                hs = slice(h * HEAD_DIM, (h + 1) * HEAD_DIM)
                gs = slice(g * HEAD_DIM, (g + 1) * HEAD_DIM)
                qh, kh, vh, doh = qb[:, hs], kwin[:, gs], vwin[:, gs], dob[:, hs]
                lh = lb[:, h * HEAD_DIM:h * HEAD_DIM + 1]
                dlh = dlb[:, h * HEAD_DIM:h * HEAD_DIM + 1]
                s = lax.dot_general(qh, kh, NT, preferred_element_type=F32) + b_ref[h] + neg
                p = jnp.exp(s - lh)
                dp = lax.dot_general(doh, vh, NT, preferred_element_type=F32)
                ds = p * (dp - dlh)
                dsb = ds.astype(BF16)
                dq_ref[pl.ds(r0, blk), hs] = jnp.dot(dsb, kh, preferred_element_type=F32)
                dka[pl.ds(r0, kw), gs] += lax.dot_general(dsb, qh, TN, preferred_element_type=F32)
                dva[pl.ds(r0, kw), gs] += lax.dot_general(p.astype(BF16), doh, TN, preferred_element_type=F32)
                db_ref[h] += ds
                if has_sink:
                    ps = jnp.exp(s_ref[h][0:1, 0:1] - lh)
                    dsk_ref[h] += jnp.broadcast_to(-jnp.sum(ps * dlh, axis=0, keepdims=True), (8, 128))
            return carry

        lax.fori_loop(0, nb, blk_body, 0)
        dk_ref[...] = dka[rad:rad + sl, :]
        dv_ref[...] = dva[rad:rad + sl, :]

    const3 = lambda b, r: (0, 0, 0)
    in_specs = [_band_spec(sl, qs), _band_spec(sl, ks), _band_spec(sl, vs), pl.BlockSpec((nh, blk, kw), const3)]
    args = [src] * 3 + [bias]
    if has_sink:
        in_specs.append(pl.BlockSpec((nh, 8, 128), const3))
        args.append(sink)
    row = _band_spec(sl, (256, 0))
    in_specs += [_band_spec(sl, (256, dcol)), row, row]
    args += [dy, lse, delta]
    out_specs = [row, _band_spec(sl, (wk, 0)), _band_spec(sl, (wv, 0)), pl.BlockSpec((nh, blk, kw), const3)]
    out_shape = [_sds((bl, dil, sl, 256), F32), _sds((bl, dil, sl, wk), F32), _sds((bl, dil, sl, wv), F32),
                 _sds((nh, blk, kw), F32)]
    if has_sink:
        out_specs.append(pl.BlockSpec((nh, 8, 128), const3))
        out_shape.append(_sds((nh, 8, 128), F32))
    return pl.pallas_call(
        body, name=name, grid=(bl, dil), in_specs=in_specs, out_specs=out_specs, out_shape=out_shape,
        scratch_shapes=[pltpu.VMEM((sl + 2 * rad, wk), BF16), pltpu.VMEM((sl + 2 * rad, wv), BF16),
                        pltpu.VMEM((sl + 2 * rad, wk), F32), pltpu.VMEM((sl + 2 * rad, wv), F32)],
        compiler_params=_cparams(ARB, ARB),
    )(*args)


def _combine_a(os_, ls_, *, tm, name):
    bl = os_[1].shape[0]
    t = bl * SEQ
    nt = SEQ // tm

    def body(o1, o4, o16, l1, l4, l16, y_ref, lt_ref, scr):
        for k, (d, ref) in enumerate(((4, o4), (16, o16), (4, l4), (16, l16))):
            for r in range(d):
                _write_residue(scr, 2 * k, r, d, ref[r])
        o2, o3, b, c = (_gather_cols(scr, 2 * k, 2) for k in range(4))
        a = l1[...]
        m = jnp.maximum(jnp.maximum(a, b), c)
        ea, eb, ec = jnp.exp(a - m), jnp.exp(b - m), jnp.exp(c - m)
        den = ea + eb + ec
        y_ref[...] = (ea / den) * o1[...] + (eb / den) * o2 + (ec / den) * o3
        lt_ref[...] = m + jnp.log(den)

    specs = _residue_specs(tm, 256, nt)
    return pl.pallas_call(
        body, name=name, grid=(bl, nt), in_specs=specs * 2, out_specs=[specs[0]] * 2,
        out_shape=[_sds((t, 256), F32)] * 2, scratch_shapes=[pltpu.VMEM((8, tm, LANES), F32)],
        compiler_params=_cparams(PAR, PAR),
    )(*os_, *ls_)


def _deltas(dycat, ya, yb, yd, lse_a, *, tm, name):
    t = ya.shape[0]
    bl = t // SEQ
    nt = SEQ // tm

    def body(dy_ref, ya_ref, yb_ref, yd_ref, la_ref, dy4, dy16, l4, l16, da1, da4, da16, db_ref, dd_ref, scr):
        e = _group_sum_matrix(256, True)
        dya = dy_ref[:, 0:256]
        dla = _seg_sum(dya * ya_ref[...], e)
        da1[...] = dla
        db_ref[...] = _seg_sum(dy_ref[:, 256:512] * yb_ref[...], e)
        dd_ref[...] = _seg_sum(dy_ref[:, 768:1024] * yd_ref[...], e)
        for k, (val, r4, r16) in enumerate(((dya, dy4, dy16), (la_ref[...], l4, l16), (dla, da4, da16))):
            _scatter_cols(scr, 2 * k, val)
            for d, ref in ((4, r4), (16, r16)):
                for r in range(d):
                    ref[r] = _read_residue(scr, 2 * k, 2, r, d)

    specs = _residue_specs(tm, 256, nt)
    nat = specs[0]
    shapes = _residue_shapes(bl, 256, F32)
    outs = pl.pallas_call(
        body, name=name, grid=(bl, nt),
        in_specs=[pl.BlockSpec((tm, 1024), lambda b, i: (b * nt + i, 0)), nat, nat, nat, nat],
        out_specs=specs[1:] + specs[1:] + specs + [nat, nat],
        out_shape=shapes[1:] + shapes[1:] + shapes + [shapes[0], shapes[0]],
        scratch_shapes=[pltpu.VMEM((6, tm, LANES), F32)],
        compiler_params=_cparams(PAR, PAR),
    )(dycat, ya, yb, yd, lse_a)
    return outs[0:2], outs[2:4], outs[4:7], outs[7], outs[8]


def _dense_fwd(qd, *, tq, name):
    t = qd.shape[0]
    bl = t // SEQ
    nq = SEQ // tq

    def body(q_ref, k_ref, v_ref, o_ref, l_ref):
        q = q_ref[...]
        for g in range(2):
            h0, h1 = 2 * g, 2 * g + 1
            q2 = jnp.concatenate([q[:, h0 * 64:(h0 + 1) * 64], q[:, h1 * 64:(h1 + 1) * 64]], axis=0)
            kg = k_ref[:, g * 64:(g + 1) * 64]
            vg = v_ref[:, g * 64:(g + 1) * 64]
            s = lax.dot_general(q2, kg, NT, preferred_element_type=F32)
            m = jnp.max(s, axis=1, keepdims=True)
            p = jnp.exp(s - m)
            den = jnp.sum(p, axis=1, keepdims=True)
            o2 = jnp.dot(p.astype(BF16), vg, preferred_element_type=F32) / den
            l2 = jnp.broadcast_to(m + jnp.log(den), (2 * tq, 64))
            o_ref[:, h0 * 64:(h0 + 1) * 64] = o2[:tq]
            o_ref[:, h1 * 64:(h1 + 1) * 64] = o2[tq:]
            l_ref[:, h0 * 64:(h0 + 1) * 64] = l2[:tq]
            l_ref[:, h1 * 64:(h1 + 1) * 64] = l2[tq:]

    q3 = qd.reshape(bl, SEQ, 512)
    o, lse = pl.pallas_call(
        body, name=name, grid=(bl, nq),
        in_specs=[pl.BlockSpec((None, tq, 256), lambda b, i: (b, i, 0)),
                  pl.BlockSpec((None, SEQ, 128), lambda b, i: (b, 0, 2)),
                  pl.BlockSpec((None, SEQ, 128), lambda b, i: (b, 0, 3))],
        out_specs=[pl.BlockSpec((None, tq, 256), lambda b, i: (b, i, 0))] * 2,
        out_shape=[_sds((bl, SEQ, 256), F32)] * 2,
        compiler_params=_cparams(PAR, PAR),
    )(q3, q3, q3)
    return o.reshape(t, 256), lse.reshape(t, 256)


def _dense_bwd(qd, dycat, lse, delta, *, tq, name):
    t = qd.shape[0]
    bl = t // SEQ
    nq = SEQ // tq

    def body(q_ref, k_ref, v_ref, do_ref, l_ref, dl_ref, dq_ref, dk_ref, dv_ref):
        @pl.when(pl.program_id(1) == 0)
        def _():
            dk_ref[...] = jnp.zeros_like(dk_ref)
            dv_ref[...] = jnp.zeros_like(dv_ref)

        q = q_ref[...]
        do = do_ref[...].astype(BF16)
        lv = l_ref[...]
        dlv = dl_ref[...]
        for g in range(2):
            h0, h1 = 2 * g, 2 * g + 1
            q2 = jnp.concatenate([q[:, h0 * 64:(h0 + 1) * 64], q[:, h1 * 64:(h1 + 1) * 64]], axis=0)
            do2 = jnp.concatenate([do[:, h0 * 64:(h0 + 1) * 64], do[:, h1 * 64:(h1 + 1) * 64]], axis=0)
            l2 = jnp.concatenate([lv[:, h0 * 64:h0 * 64 + 1], lv[:, h1 * 64:h1 * 64 + 1]], axis=0)
            dl2 = jnp.concatenate([dlv[:, h0 * 64:h0 * 64 + 1], dlv[:, h1 * 64:h1 * 64 + 1]], axis=0)
            kg = k_ref[:, g * 64:(g + 1) * 64]
            vg = v_ref[:, g * 64:(g + 1) * 64]
            s = lax.dot_general(q2, kg, NT, preferred_element_type=F32)
            p = jnp.exp(s - l2)
            dp = lax.dot_general(do2, vg, NT, preferred_element_type=F32)
            ds = (p * (dp - dl2)).astype(BF16)
            dq2 = jnp.dot(ds, kg, preferred_element_type=F32)
            dq_ref[:, h0 * 64:(h0 + 1) * 64] = dq2[:tq]
            dq_ref[:, h1 * 64:(h1 + 1) * 64] = dq2[tq:]
            dk_ref[:, g * 64:(g + 1) * 64] += lax.dot_general(ds, q2, TN, preferred_element_type=F32)
            dv_ref[:, g * 64:(g + 1) * 64] += lax.dot_general(p.astype(BF16), do2, TN, preferred_element_type=F32)

    q3 = qd.reshape(bl, SEQ, 512)
    tile = pl.BlockSpec((None, tq, 256), lambda b, i: (b, i, 0))
    full = pl.BlockSpec((None, SEQ, 128), lambda b, i: (b, 0, 0))
    dq, dk, dv = pl.pallas_call(
        body, name=name, grid=(bl, nq),
        in_specs=[tile, pl.BlockSpec((None, SEQ, 128), lambda b, i: (b, 0, 2)),
                  pl.BlockSpec((None, SEQ, 128), lambda b, i: (b, 0, 3)),
                  pl.BlockSpec((None, tq, 256), lambda b, i: (b, i, 3)), tile, tile],
        out_specs=[tile, full, full],
        out_shape=[_sds((bl, SEQ, 256), F32), _sds((bl, SEQ, 128), F32), _sds((bl, SEQ, 128), F32)],
        compiler_params=_cparams(PAR, ARB),
    )(q3, q3, q3, dycat.reshape(bl, SEQ, 1024), lse.reshape(bl, SEQ, 256), delta.reshape(bl, SEQ, 256))
    return dq.reshape(t, 256), dk.reshape(t, 128), dv.reshape(t, 128)


def _c_norm(cv, gam, bet):
    vg = _gelu(cv)
    mu = jnp.mean(vg, axis=-1, keepdims=True)
    xc = vg - mu
    r = lax.rsqrt(jnp.mean(xc * xc, axis=-1, keepdims=True) + EPS)
    xhat = xc * r
    return xhat * gam + bet, xhat, r


def _c_fwd(proj, gam, bet, ws, bst, *, tm, name):
    t = proj.shape[0]
    nch = tm // C_CHUNK

    def body(u_ref, v_ref, g_ref, b_ref, ws_ref, bs_ref, y_ref):
        vn, _, _ = _c_norm(v_ref[...], g_ref[...], b_ref[...])
        vnb = vn.astype(BF16)
        for c in range(nch):
            rows = slice(c * C_CHUNK, (c + 1) * C_CHUNK)
            for g in range(C_GROUPS):
                gs = slice(g * 64, (g + 1) * 64)
                mixed = jnp.dot(ws_ref[g], vnb[rows, gs], preferred_element_type=F32) + bs_ref[:, gs]
                y_ref[rows, gs] = _gelu(u_ref[rows, gs]) * mixed

    vec = pl.BlockSpec((1, 256), lambda i: (0, 0))
    return pl.pallas_call(
        body, name=name, grid=(t // tm,),
        in_specs=[pl.BlockSpec((tm, 256), lambda i: (i, 5)), pl.BlockSpec((tm, 256), lambda i: (i, 6)), vec, vec,
                  pl.BlockSpec((C_GROUPS, C_CHUNK, C_CHUNK), lambda i: (0, 0, 0)),
                  pl.BlockSpec((C_CHUNK, 256), lambda i: (0, 0))],
        out_specs=pl.BlockSpec((tm, 256), lambda i: (i, 0)), out_shape=_sds((t, 256), F32),
        compiler_params=_cparams(PAR),
    )(proj, proj, gam, bet, ws, bst)


def _c_bwd(proj, dycat, gam, bet, ws, wst, bst, *, tm, name):
    t = proj.shape[0]
    nch = tm // C_CHUNK
    nstep = t // tm

    def body(u_ref, v_ref, dy_ref, g_ref, b_ref, ws_ref, wst_ref, bs_ref,
             du_ref, dv_ref, dws_ref, dbs_ref, dg_ref, db_ref, dvn_s):
        step = pl.program_id(0)

        @pl.when(step == 0)
        def _():
            dws_ref[...] = jnp.zeros_like(dws_ref)
            dbs_ref[...] = jnp.zeros_like(dbs_ref)
            dg_ref[...] = jnp.zeros_like(dg_ref)
            db_ref[...] = jnp.zeros_like(db_ref)

        cv = v_ref[...]
        gam_v = g_ref[...]
        vn, xhat, r = _c_norm(cv, gam_v, b_ref[...])
        vnb = vn.astype(BF16)
        for c in range(nch):
            rows = slice(c * C_CHUNK, (c + 1) * C_CHUNK)
            for g in range(C_GROUPS):
                gs = slice(g * 64, (g + 1) * 64)
                cu = u_ref[rows, gs]
                dy = dy_ref[rows, gs]
                mixed = jnp.dot(ws_ref[g], vnb[rows, gs], preferred_element_type=F32) + bs_ref[:, gs]
                du_ref[rows, gs] = dy * mixed * _gelu_grad(cu)
                dmix = dy * _gelu(cu)
                dbs_ref[:, gs] += dmix
                dmb = dmix.astype(BF16)
                dws_ref[g] += lax.dot_general(dmb, vnb[rows, gs], NT, preferred_element_type=F32)
                dvn_s[rows, gs] = jnp.dot(wst_ref[g], dmb, preferred_element_type=F32)
        dvn = dvn_s[...]
        dg_ref[...] += jnp.sum(dvn * xhat, axis=0, keepdims=True)
        db_ref[...] += jnp.sum(dvn, axis=0, keepdims=True)
        dxh = dvn * gam_v
        dvg = r * (dxh - jnp.mean(dxh, axis=-1, keepdims=True) - xhat * jnp.mean(dxh * xhat, axis=-1, keepdims=True))
        dv_ref[...] = dvg * _gelu_grad(cv)

        @pl.when(step == nstep - 1)
        def _():
            dbs_ref[...] = _seg_sum(dbs_ref[...], _group_sum_matrix(256, True))

    vec = pl.BlockSpec((1, 256), lambda i: (0, 0))
    mat = pl.BlockSpec((C_GROUPS, C_CHUNK, C_CHUNK), lambda i: (0, 0, 0))
    bsp = pl.BlockSpec((C_CHUNK, 256), lambda i: (0, 0))
    tile = pl.BlockSpec((tm, 256), lambda i: (i, 0))
    return pl.pallas_call(
        body, name=name, grid=(nstep,),
        in_specs=[pl.BlockSpec((tm, 256), lambda i: (i, 5)), pl.BlockSpec((tm, 256), lambda i: (i, 6)),
                  pl.BlockSpec((tm, 256), lambda i: (i, 2)), vec, vec, mat, mat, bsp],
        out_specs=[tile, tile, mat, bsp, vec, vec],
        out_shape=[_sds((t, 256), F32), _sds((t, 256), F32), _sds((C_GROUPS, C_CHUNK, C_CHUNK), F32),
                   _sds((C_CHUNK, 256), F32), _sds((1, 256), F32), _sds((1, 256), F32)],
        scratch_shapes=[pltpu.VMEM((tm, 256), F32)],
        compiler_params=_cparams(ARB),
    )(proj, proj, dycat, gam, bet, ws, wst, bst)


FF_TC = 128
FF_NB = D_FF // FF_TC
FF_CH = 64
FF_HALO = 16


def _taps(ref, r0, win, where):
    z = jnp.zeros((FF_HALO, win.shape[1]), F32)
    if where == "first":
        win[0:FF_HALO, :] = z
        win[FF_HALO:, :] = ref[0:FF_CH + FF_HALO, :].astype(F32)
    elif where == "last":
        win[0:FF_CH + FF_HALO, :] = ref[SEQ - FF_CH - FF_HALO:SEQ, :].astype(F32)
        win[FF_CH + FF_HALO:, :] = z
    else:
        win[...] = ref[pl.ds(pl.multiple_of(r0 - FF_HALO, FF_HALO), FF_CH + 2 * FF_HALO), :].astype(F32)
    return tuple(win[FF_HALO + o:FF_HALO + o + FF_CH, :] for o in (-1, 0, 1))


def _chunk_loop(step):
    step(0, lambda ref, win: _taps(ref, 0, win, "first"))

    def mid(i, carry):
        r0 = pl.multiple_of(i * FF_CH, FF_CH)
        step(r0, lambda ref, win: _taps(ref, r0, win, "mid"))
        return carry

    lax.fori_loop(1, SEQ // FF_CH - 1, mid, 0)
    step(SEQ - FF_CH, lambda ref, win: _taps(ref, SEQ - FF_CH, win, "last"))


def _conv3(taps, w_ref, b_ref):
    dn, md, up = taps
    return w_ref[0:1, :] * dn + w_ref[1:2, :] * md + w_ref[2:3, :] * up + b_ref[...]


def _ff_specs(order):
    def at(fn):
        return (lambda b, j: fn(b, j)) if order == "bj" else (lambda j, b: fn(b, j))
    hs = [pl.BlockSpec((None, SEQ, FF_TC), at(lambda b, j, o=o: (b, 0, j + o))) for o in (0, FF_NB)]
    ws = [pl.BlockSpec((3, FF_TC), at(lambda b, j, o=o: (0, j + o))) for o in (0, FF_NB)]
    bs = [pl.BlockSpec((1, FF_TC), at(lambda b, j, o=o: (0, j + o))) for o in (0, FF_NB)]
    return hs, ws, bs


def _conv_gate_fwd(h, cw, cb, *, name):
    t = h.shape[0]
    bl = t // SEQ

    def body(hg_ref, hu_ref, wg_ref, wu_ref, bg_ref, bu_ref, a_ref, win):
        def step(r0, taps):
            cg = _conv3(taps(hg_ref, win.at[0]), wg_ref, bg_ref)
            cu = _conv3(taps(hu_ref, win.at[1]), wu_ref, bu_ref)
            a_ref[pl.ds(r0, FF_CH), :] = (cg * _sigmoid(cg) * cu).astype(BF16)

        _chunk_loop(step)

    hs, ws, bs = _ff_specs("bj")
    h3 = h.reshape(bl, SEQ, 2 * D_FF)
    act = pl.pallas_call(
        body, name=name, grid=(bl, FF_NB), in_specs=hs + ws + bs,
        out_specs=pl.BlockSpec((None, SEQ, FF_TC), lambda b, j: (b, 0, j)),
        out_shape=_sds((bl, SEQ, D_FF), BF16),
        scratch_shapes=[pltpu.VMEM((2, FF_CH + 2 * FF_HALO, FF_TC), F32)],
        compiler_params=_cparams(PAR, PAR),
    )(h3, h3, cw, cw, cb, cb)
    return act.reshape(t, D_FF)


def _conv_gate_bwd(h, dact, cw, cb, *, name):
    t = h.shape[0]
    bl = t // SEQ

    def body(hg_ref, hu_ref, wg_ref, wu_ref, bg_ref, bu_ref, da_ref,
             dhg_ref, dhu_ref, dwg_ref, dwu_ref, dbg_ref, dbu_ref, dg_s, du_s, win):
        @pl.when(pl.program_id(1) == 0)
        def _():
            for ref in (dwg_ref, dwu_ref, dbg_ref, dbu_ref):
                ref[...] = jnp.zeros_like(ref)

        red = lambda x: jnp.sum(x, axis=0, keepdims=True)

        def pass1(r0, taps):
            tg, tu = taps(hg_ref, win.at[0]), taps(hu_ref, win.at[1])
            cg = _conv3(tg, wg_ref, bg_ref)
            cu = _conv3(tu, wu_ref, bu_ref)
            da = da_ref[pl.ds(r0, FF_CH), :].astype(F32)
            sg = _sigmoid(cg)
            dcg = da * cu * (sg * (1.0 + cg * (1.0 - sg)))
            dcu = da * (cg * sg)
            dg_s[pl.ds(r0, FF_CH), :] = dcg
            du_s[pl.ds(r0, FF_CH), :] = dcu
            for d, tp, dw_ref, db_ref in ((dcg, tg, dwg_ref, dbg_ref), (dcu, tu, dwu_ref, dbu_ref)):
                for k in range(3):
                    dw_ref[k:k + 1, :] += red(d * tp[k])
                db_ref[...] += red(d)

        _chunk_loop(pass1)

        def pass2(r0, taps):
            for k, (s, w_ref, o_ref) in enumerate(((dg_s, wg_ref, dhg_ref), (du_s, wu_ref, dhu_ref))):
                dn, md, up = taps(s, win.at[k])
                o_ref[pl.ds(r0, FF_CH), :] = (w_ref[0:1, :] * up + w_ref[1:2, :] * md + w_ref[2:3, :] * dn).astype(BF16)

        _chunk_loop(pass2)

    hs, ws, bs = _ff_specs("jb")
    half = pl.BlockSpec((None, SEQ, FF_TC), lambda j, b: (b, 0, j))
    wsp = pl.BlockSpec((3, FF_TC), lambda j, b: (0, j))
    bsp = pl.BlockSpec((1, FF_TC), lambda j, b: (0, j))
    h3 = h.reshape(bl, SEQ, 2 * D_FF)
    dhg, dhu, dwg, dwu, dbg, dbu = pl.pallas_call(
        body, name=name, grid=(FF_NB, bl), in_specs=hs + ws + bs + [half],
        out_specs=[half, half, wsp, wsp, bsp, bsp],
        out_shape=[_sds((bl, SEQ, D_FF), BF16), _sds((bl, SEQ, D_FF), BF16), _sds((3, D_FF), F32), _sds((3, D_FF), F32),
                   _sds((1, D_FF), F32), _sds((1, D_FF), F32)],
        scratch_shapes=[pltpu.VMEM((SEQ, FF_TC), F32), pltpu.VMEM((SEQ, FF_TC), F32),
                        pltpu.VMEM((2, FF_CH + 2 * FF_HALO, FF_TC), F32)],
        compiler_params=_cparams(PAR, ARB),
    )(h3, h3, cw, cw, cb, cb, dact.reshape(bl, SEQ, D_FF))
    return (dhg.reshape(t, D_FF), dhu.reshape(t, D_FF), jnp.concatenate([dwg, dwu], axis=1),
            jnp.concatenate([dbg, dbu], axis=1))


def _ple_fwd(x2, gain, wg, pe, pe_blk, wp, *, tm, tn, name):
    t, k = x2.shape
    n = wg.shape[1]

    def body(x_ref, g_ref, wg_ref, pe_ref, wp_ref, xr_ref, hn_ref, x3_ref, gt_ref, pp_ref, hn_s):
        @pl.when(pl.program_id(1) == 0)
        def _():
            x = x_ref[...]
            r = lax.rsqrt(jnp.mean(x * x, axis=-1, keepdims=True) + EPS)
            hn_s[...] = (x * r * g_ref[...]).astype(BF16)
            hn_ref[...] = hn_s[...]

        gate = _sigmoid(jnp.dot(hn_s[...], wg_ref[...], preferred_element_type=F32))
        pp = jnp.dot(pe_ref[...].astype(BF16), wp_ref[...], preferred_element_type=F32)
        gt_ref[...] = gate.astype(BF16)
        pp_ref[...] = pp.astype(BF16)
        x3_ref[...] = xr_ref[...] + pp * gate

    tile = pl.BlockSpec((tm, tn), lambda i, j: (i, j))
    return pl.pallas_call(
        body, name=name, grid=(t // tm, n // tn),
        in_specs=[pl.BlockSpec((tm, k), lambda i, j: (i, 0)), pl.BlockSpec((1, k), lambda i, j: (0, 0)),
                  pl.BlockSpec((k, tn), lambda i, j: (0, j)), pl.BlockSpec((tm, PLE_DIM), lambda i, j: (pe_blk + i, 0)),
                  pl.BlockSpec((PLE_DIM, tn), lambda i, j: (0, j)), tile],
        out_specs=[pl.BlockSpec((tm, k), lambda i, j: (i, 0)), tile, tile, tile],
        out_shape=[_sds((t, k), BF16), _sds((t, n), F32), _sds((t, n), BF16), _sds((t, n), BF16)],
        scratch_shapes=[pltpu.VMEM((tm, k), BF16)],
        compiler_params=_cparams(PAR, ARB),
    )(x2, gain, wg, pe, wp, x2)


def _ple_bwd_ew(dx3, gate, pp, *, tm, name):
    t, n = dx3.shape

    def body(d_ref, g_ref, p_ref, dz_ref, dpp_ref):
        d, g = d_ref[...], g_ref[...]
        dz_ref[...] = (d * p_ref[...] * g * (1.0 - g)).astype(BF16)
        dpp_ref[...] = (d * g).astype(BF16)

    spec = pl.BlockSpec((tm, n), lambda i: (i, 0))
    return pl.pallas_call(
        body, name=name, grid=(t // tm,), in_specs=[spec] * 3, out_specs=[spec] * 2,
        out_shape=[_sds((t, n), BF16)] * 2, compiler_params=_cparams(PAR),
    )(dx3, gate, pp)


def _loss_head(y, tgt, *, tm, name):
    t, d = y.shape

    def body(y_ref, t_ref, l_ref, dy_ref):
        @pl.when(pl.program_id(0) == 0)
        def _():
            l_ref[...] = jnp.zeros_like(l_ref)

        e = y_ref[...] - t_ref[...]
        dy_ref[...] = e * (1.0 / d)
        s = jnp.sum(jnp.sum(e * e, axis=1, keepdims=True), axis=0, keepdims=True)
        l_ref[...] += jnp.broadcast_to(s * (0.5 / d), (8, 128))

    spec = pl.BlockSpec((tm, d), lambda i: (i, 0))
    return pl.pallas_call(
        body, name=name, grid=(t // tm,), in_specs=[spec, spec],
        out_specs=[pl.BlockSpec((8, 128), lambda i: (0, 0)), spec],
        out_shape=[_sds((8, 128), F32), _sds((t, d), F32)], compiler_params=_cparams(ARB),
    )(y, tgt)


BIAS_PC = 8192


def _onehot(bucket_row):
    rows = lax.broadcasted_iota(jnp.int32, (REL_BUCKETS, bucket_row.shape[1]), 0)
    return (rows == bucket_row).astype(BF16)


def _dot3(x, onehot, dims):
    acc = None
    for _ in range(3):
        term = x.astype(BF16)
        part = lax.dot_general(term, onehot, dims, preferred_element_type=F32)
        acc = part if acc is None else acc + part
        x = x - term.astype(F32)
    return acc


def _bias_lookup(table_t, bucket, *, name):
    h = table_t.shape[0]
    p = bucket.shape[1]

    def body(t_ref, b_ref, o_ref):
        bk = b_ref[...]
        val = _dot3(t_ref[...], _onehot(bk), (((1,), (0,)), ((), ())))
        o_ref[...] = jnp.where(bk >= 0, val, NEG_INF)

    return pl.pallas_call(
        body, name=name, grid=(p // BIAS_PC,),
        in_specs=[pl.BlockSpec((h, REL_BUCKETS), lambda i: (0, 0)), pl.BlockSpec((1, BIAS_PC), lambda i: (0, i))],
        out_specs=pl.BlockSpec((h, BIAS_PC), lambda i: (0, i)), out_shape=_sds((h, p), F32),
        compiler_params=_cparams(PAR),
    )(table_t, bucket)


def _bucket_reduce(dbiases, bucket, *, name):
    h, p = dbiases[0].shape
    nl = len(dbiases)

    def body(*refs):
        b_ref, o_ref = refs[nl], refs[nl + 1]

        @pl.when(pl.program_id(0) == 0)
        def _():
            o_ref[...] = jnp.zeros_like(o_ref)

        d = refs[0][...]
        for d_ref in refs[1:nl]:
            d = d + d_ref[...]
        o_ref[...] += _dot3(d, _onehot(b_ref[...]), NT)

    return pl.pallas_call(
        body, name=name, grid=(p // BIAS_PC,),
        in_specs=[pl.BlockSpec((h, BIAS_PC), lambda i: (0, i))] * nl + [pl.BlockSpec((1, BIAS_PC), lambda i: (0, i))],
        out_specs=pl.BlockSpec((h, REL_BUCKETS), lambda i: (0, 0)), out_shape=_sds((h, REL_BUCKETS), F32),
        compiler_params=_cparams(ARB),
    )(*dbiases, bucket)


def _adamw_math(w, g, m, v):
    m = ADAM_B1 * m + (1.0 - ADAM_B1) * g
    v = ADAM_B2 * v + (1.0 - ADAM_B2) * (g * g)
    m_hat = m / (1.0 - ADAM_B1 ** ADAM_STEP)
    v_hat = v / (1.0 - ADAM_B2 ** ADAM_STEP)
    delta = -ADAM_LR * (m_hat / (jnp.sqrt(v_hat) + ADAM_EPS) + ADAM_WD * w)
    return delta, m, v


def _adamw_reduce(parts, w, m, v, *, tr, name):
    nl = len(parts)
    rows, c = w.shape
    r = rows // nl
    nt = r // tr

    def body(*refs):
        p_refs = refs[:nl]
        w_ref, m_ref, v_ref, g_ref, d_ref, nm_ref, nv_ref = refs[nl:]
        for li, p_ref in enumerate(p_refs):
            @pl.when(pl.program_id(0) == li)
            def _(p_ref=p_ref):
                g = p_ref[0].astype(F32)
                for k in range(1, N_DEV):
                    g = g + p_ref[k].astype(F32)
                d, nm, nv = _adamw_math(w_ref[...], g, m_ref[...], v_ref[...])
                g_ref[...] = g
                d_ref[...] = d
                nm_ref[...] = nm
                nv_ref[...] = nv

    def part_map(li):
        return lambda l, i: (0, jnp.where(l == li, i, jnp.where(l < li, 0, nt - 1)), 0)

    spec = pl.BlockSpec((tr, c), lambda l, i: (l * nt + i, 0))
    return pl.pallas_call(
        body, name=name, grid=(nl, nt),
        in_specs=[pl.BlockSpec((N_DEV, tr, c), part_map(li)) for li in range(nl)] + [spec, spec, spec],
        out_specs=[spec] * 4, out_shape=[_sds((rows, c), F32)] * 4, compiler_params=_cparams(ARB, ARB),
    )(*parts, w, m, v)


def _adamw_plain(g, w, m, v, *, name):
    def body(g_ref, w_ref, m_ref, v_ref, d_ref, nm_ref, nv_ref):
        d, nm, nv = _adamw_math(w_ref[...], g_ref[...], m_ref[...], v_ref[...])
        d_ref[...] = d
        nm_ref[...] = nm
        nv_ref[...] = nv

    return pl.pallas_call(body, name=name, out_shape=[_sds(w.shape, F32)] * 3)(g, w, m, v)


def _mesh_pos():
    return lax.axis_index("x"), lax.axis_index("y"), lax.axis_index("c")


def _allgather_body(x_refs, out_refs, send_sems, recv_sems, local_sems, slot):
    x, y, c = _mesh_pos()
    me, sibling = (x, y, c), (x, y, 1 - c)
    chips = [(1 - x, y), (x, 1 - y), (1 - x, 1 - y)]
    waits = []
    for a, (x_ref, out_ref) in enumerate(zip(x_refs, out_refs)):
        def copy(k, block, to, src=None, out_ref=out_ref, a=a):
            return pltpu.make_async_remote_copy(
                src_ref=slot(out_ref, block) if src is None else src, dst_ref=slot(out_ref, block),
                send_sem=send_sems.at[a, k], recv_sem=recv_sems.at[a, k], device_id=to, device_id_type=MESH)

        mine = pltpu.make_async_copy(x_ref, slot(out_ref, me), local_sems.at[a])
        mine.start()
        first = [copy(0, me, sibling, src=x_ref)]
        first += [copy(1 + j, me, (*chip, c), src=x_ref) for j, chip in enumerate(chips)]
        for cp in first:
            cp.start()
        waits.append((copy, mine, first))
    sends = []
    for copy, mine, first in waits:
        passed = [copy(4 + j, (*chip, c), sibling) for j, chip in enumerate(chips)]
        for j, chip in enumerate(chips):
            copy(1 + j, (*chip, c), me).wait_recv()
            passed[j].start()
        sends.append(passed)
    for (copy, mine, first), passed in zip(waits, sends):
        copy(0, sibling, me).wait_recv()
        for j, chip in enumerate(chips):
            copy(4 + j, (*chip, 1 - c), me).wait_recv()
        for cp in first + passed:
            cp.wait_send()
        mine.wait()


PEER_FLIPS = ((0, 0, 1), (1, 0, 0), (0, 1, 0), (1, 1, 0), (1, 0, 1), (0, 1, 1), (1, 1, 1))


def _peer_copies(x_refs, land_refs, send_sem, recv_sem, scatter):
    x, y, c = _mesh_pos()
    me = 4 * x + 2 * y + c
    copies = []
    for x_ref, land_ref in zip(x_refs, land_refs):
        for fx, fy, fc in PEER_FLIPS:
            px, py, pc = x ^ fx, y ^ fy, c ^ fc
            src = x_ref.at[4 * px + 2 * py + pc] if scatter else x_ref
            copies.append(pltpu.make_async_remote_copy(
                src_ref=src, dst_ref=land_ref.at[me], send_sem=send_sem, recv_sem=recv_sem,
                device_id=(px, py, pc), device_id_type=MESH))
    return copies


def _sc_exchange(xs, *, scatter, collective_id, name):
    na = len(xs)
    land_shapes = [x.shape if scatter else (N_DEV,) + x.shape for x in xs]

    def body(*refs):
        x_refs, land_refs = refs[:na], refs[na:2 * na]
        send_sem, recv_sem, local_sem = refs[2 * na:]
        x, y, c = _mesh_pos()
        me = 4 * x + 2 * y + c
        barrier = pltpu.get_barrier_semaphore()
        for fx, fy, fc in PEER_FLIPS:
            pl.semaphore_signal(barrier, inc=1, device_id=(x ^ fx, y ^ fy, c ^ fc), device_id_type=MESH)
        pl.semaphore_wait(barrier, len(PEER_FLIPS))
        for x_ref, land_ref in zip(x_refs, land_refs):
            own = pltpu.make_async_copy(x_ref.at[me] if scatter else x_ref, land_ref.at[me], local_sem)
            own.start()
            own.wait()
        copies = _peer_copies(x_refs, land_refs, send_sem, recv_sem, scatter)
        for cp in copies:
            cp.start()
        for cp in copies:
            cp.wait()

    return pl.kernel(
        body, name=name, out_type=[_sds(s, x.dtype) for s, x in zip(land_shapes, xs)],
        mesh=plsc.ScalarSubcoreMesh(axis_name="sequencer", num_cores=1),
        scratch_types=[pltpu.SemaphoreType.DMA, pltpu.SemaphoreType.DMA, pltpu.SemaphoreType.DMA],
        compiler_params=pltpu.CompilerParams(collective_id=collective_id),
    )(*xs)


def _sc_allgather(xs, *, collective_id, name):
    na = len(xs)

    def body(*refs):
        x_refs, out_refs = refs[:na], refs[na:2 * na]
        send_sems, recv_sems, local_sems = refs[2 * na:]
        x, y, c = _mesh_pos()
        barrier = pltpu.get_barrier_semaphore()
        for fx, fy, fc in PEER_FLIPS:
            pl.semaphore_signal(barrier, inc=1, device_id=(x ^ fx, y ^ fy, c ^ fc), device_id_type=MESH)
        pl.semaphore_wait(barrier, len(PEER_FLIPS))
        _allgather_body(x_refs, out_refs, send_sems, recv_sems, local_sems,
                        lambda ref, pos: ref.at[4 * pos[0] + 2 * pos[1] + pos[2]])

    return pl.kernel(
        body, name=name, out_type=[_sds((N_DEV,) + x.shape, x.dtype) for x in xs],
        mesh=plsc.ScalarSubcoreMesh(axis_name="sequencer", num_cores=1),
        scratch_types=[pltpu.SemaphoreType.DMA((na, 7)), pltpu.SemaphoreType.DMA((na, 7)),
                       pltpu.SemaphoreType.DMA((na,))],
        compiler_params=pltpu.CompilerParams(collective_id=collective_id),
    )(*xs)


def _allgather_vmem(x, *, reduce, name):
    r, c = x.shape

    def body(x_ref, out_ref, *rest):
        if reduce:
            gath, send_sems, recv_sems, local_sems = rest
        else:
            send_sems, recv_sems, local_sems = rest
            gath = out_ref
        _allgather_body([x_ref], [gath], send_sems, recv_sems, local_sems,
                        lambda ref, pos: ref.at[pl.ds((4 * pos[0] + 2 * pos[1] + pos[2]) * r, r), :])
        if reduce:
            acc = gath[0:r, :]
            for k in range(1, N_DEV):
                acc = acc + gath[k * r:(k + 1) * r, :]
            out_ref[...] = acc

    vm = pl.BlockSpec(memory_space=pltpu.VMEM)
    scratch = [pltpu.SemaphoreType.DMA((1, 7)), pltpu.SemaphoreType.DMA((1, 7)), pltpu.SemaphoreType.DMA((1,))]
    if reduce:
        scratch = [pltpu.VMEM((N_DEV * r, c), x.dtype)] + scratch
    return pl.pallas_call(
        body, name=name, in_specs=[vm], out_specs=vm,
        out_shape=_sds((r, c) if reduce else (N_DEV * r, c), x.dtype), scratch_shapes=scratch,
    )(x)


def _t5_bucket(rel):
    nb = REL_BUCKETS // 2
    ret = jnp.where(rel > 0, nb, 0)
    n = jnp.abs(rel)
    max_exact = nb // 2
    nf = jnp.maximum(n, 1).astype(F32)
    large = max_exact + (jnp.log(nf / max_exact) / math.log(REL_MAX_DIST / max_exact)
                         * (nb - max_exact)).astype(jnp.int32)
    large = jnp.minimum(large, nb - 1)
    return ret + jnp.where(n < max_exact, n, large)


def _band_pattern(block, radius, dil):
    kw = block + 2 * radius
    rel = jnp.arange(kw)[None, :] - radius - jnp.arange(block)[:, None]
    return jnp.where(jnp.abs(rel) <= radius, _t5_bucket(rel * dil), -1).astype(jnp.int32).reshape(1, block * kw)


def _rope_tables():
    lane = np.arange(64)
    seg, j = lane // 32, lane % 32
    inv = ROPE_THETA ** (-jnp.arange(0, 32, 2, dtype=F32) / 32)
    tpos = jnp.arange(SEQ)
    pos = jnp.where(jnp.asarray(seg)[None, :] == 0, (tpos // GRID_W)[:, None], (tpos % GRID_W)[:, None])
    ang = pos.astype(F32) * inv[jnp.asarray(j % 16)][None, :]
    cos = jnp.cos(ang)
    sins = jnp.where(jnp.asarray(j)[None, :] < 16, -jnp.sin(ang), jnp.sin(ang))
    return jnp.tile(cos, (1, 4)), jnp.tile(sins, (1, 4))


A_Q, A_K, A_V = (256, 0), (256, 1), (256, 2)
B_Q, B_K, B_V = (256, 0), (128, 2), (128, 3)
A_HEADS = dict(rad=A_RADIUS, nh=4, nkv=4)
B_HEADS = dict(rad=SWA_RADIUS, nh=4, nkv=2)


def _pin(arr, token):
    return arr if token is None else arr + token[0:1, 0:1]


def _local_step(x, pe, tgt, rel_bias, wts, matmul_weights, grads_ready):
    t = x.shape[0]
    bl = t // SEQ
    cos, sins = _rope_tables()
    blocks_a = [min(BAND_BLOCK, SEQ // d) for d in DILATIONS]
    pats_a = [_band_pattern(blk, A_RADIUS, d) for blk, d in zip(blocks_a, DILATIONS)]
    pat_b = _band_pattern(BAND_BLOCK, SWA_RADIUS, 1)
    table_t = rel_bias.T
    bias_a = [_bias_lookup(table_t[:4], pt, name=f"bias_a{ci}").reshape(4, blk, blk + 2 * A_RADIUS)
              for ci, (pt, blk) in enumerate(zip(pats_a, blocks_a))]
    bias_b = _bias_lookup(table_t[4:], pat_b, name="bias_b").reshape(4, BAND_BLOCK, BAND_BLOCK + 2 * SWA_RADIUS)
    nat4 = lambda a: a.reshape(bl, 1, SEQ, a.shape[-1])

    saved = []
    for li in range(DEPTH):
        w = dict(wts[li])
        w.update(matmul_weights(li, "in", x)[0])
        hn0, proj = _norm_mm((x,), w["g_mix"], w["w_in"], None, tm=1024, tn=1152, name="mix_in_fwd")
        qa1, qa4, qa16, qb, qd = _qkprep_fwd(proj, w["qk_gains"], cos, sins, tm=512, name="qkprep_fwd")
        qa = (nat4(qa1), qa4, qa16)
        oa, la = [], []
        for ci in range(3):
            o, l = _band_fwd(qa[ci], A_Q, A_K, A_V, bias_a[ci], None, name=f"band_a{ci}_fwd", **A_HEADS)
            oa.append(o)
            la.append(l)
        oa[0], la[0] = oa[0].reshape(t, 256), la[0].reshape(t, 256)
        ya, lse_a = _combine_a(oa, la, tm=512, name="combine_a")
        yb, lse_b = _band_fwd(nat4(qb), B_Q, B_K, B_V, bias_b, w["sink_t"], name="band_b_fwd", **B_HEADS)
        yb = yb.reshape(t, 256)
        yc = _c_fwd(proj, w["c_g"], w["c_b"], w["c_ws"], w["c_bst"], tm=512, name="c_fwd")
        yd, lse_d = _dense_fwd(qd, tq=128, name="dense_fwd")
        more, started = matmul_weights(li, "rest", yd)
        w.update(more)
        w["out_gain"] = _pin(w["out_gain"], started)
        mixed, x1 = _norm_mm((ya, yb, yc, yd), w["out_gain"], w["w_out"], x, tm=1024, tn=1024, name="mix_out_fwd")
        hn1, h = _norm_mm((x1,), w["g_ffn"], w["w_up"], None, tm=1024, tn=1408, name="ffn_up_fwd", out_dtype=BF16)
        act = _conv_gate_fwd(h, w["conv_w"], w["conv_b"], name="conv_gate_fwd")
        x2 = _mm(act, w["w_down"], "nn", x1, tm=1024, tn=1024, out_dtype=F32, name="ffn_down_fwd")
        hn2, x3, gate, pp = _ple_fwd(x2, w["g_ple"], w["w_gate"], pe, li * (t // 1024), w["w_proj"], tm=1024, tn=512,
                                     name="ple_fwd")
        saved.append(dict(w=w, x0=x, hn0=hn0, proj=proj, qa=qa, qb=qb, qd=qd, ya=ya, lse_a=lse_a, yb=yb, lse_b=lse_b,
                          yc=yc, yd=yd, lse_d=lse_d, mixed=mixed, x1=x1, hn1=hn1, h=h, act=act, x2=x2, hn2=hn2,
                          gate=gate, pp=pp))
        x = x3

    loss_tile, dx = _loss_head(x, tgt, tm=512, name="loss_head")
    grads = [None] * DEPTH
    dbias_a, dbias_bs = [[], [], []], []
    token = None
    for li in reversed(range(DEPTH)):
        s = saved[li]
        w = s["w"]
        g = {}
        w["g_ple"] = _pin(w["g_ple"], token)
        dz, dpp = _ple_bwd_ew(dx, s["gate"], s["pp"], tm=512, name="ple_bwd_ew")
        g["w_gate"] = _mm(s["hn2"], dz, "tn", None, tm=1024, tn=512, out_dtype=BF16, name="dw_gate")
        g["w_proj"] = _mm(pe, dpp, "tn", None, tm=256, tn=1024, out_dtype=BF16, name="dw_proj", a_rows=(li, t))
        dx2, dx2b, g["g_ple"] = _mm_bt_normbwd((dz,), w["w_gate"], (s["x2"],), w["g_ple"], dx, tm=1024, tn=1024,
                                               name="ple_bwd", emit_bf16=True)
        g["w_down"] = _mm(s["act"], dx2b, "tn", None, tm=1408, tn=512, out_dtype=BF16, name="dw_down")
        dact = _mm(dx2b, w["w_down"], "nt", None, tm=1024, tn=1408, out_dtype=BF16, name="ffn_down_bwd")
        dhg, dhu, g["conv_w"], g["conv_b"] = _conv_gate_bwd(s["h"], dact, w["conv_w"], w["conv_b"], name="conv_gate_bwd")
        g["w_up"] = jnp.concatenate(
            [_mm(s["hn1"], dhalf, "tn", None, tm=1024, tn=1408, out_dtype=BF16, name=f"dw_up_{nm}")
             for nm, dhalf in (("gate", dhg), ("up", dhu))], axis=1)
        dx1, dx1b, g["g_ffn"] = _mm_bt_normbwd((dhg, dhu), w["w_up"], (s["x1"],), w["g_ffn"], dx2, tm=1024, tn=1408,
                                               name="ffn_up_bwd", emit_bf16=True)
        g["w_out"] = _mm(s["mixed"], dx1b, "tn", None, tm=1024, tn=512, out_dtype=BF16, name="dw_out")
        out_gain = _pin(w["out_gain"], grads_ready(li, "mid", g))
        dycat, g["out_gain"] = _mm_bt_normbwd((dx1b,), w["w_out"], (s["ya"], s["yb"], s["yc"], s["yd"]), out_gain,
                                              None, tm=1024, tn=1024, name="mix_out_bwd")
        dy_r, lse_r, dl_a, dl_b, dl_d = _deltas(dycat, s["ya"], s["yb"], s["yd"], s["lse_a"], tm=512, name="deltas")
        dy_a = (nat4(dycat),) + tuple(dy_r)
        lse_a = (nat4(s["lse_a"]),) + tuple(lse_r)
        dl_a = (nat4(dl_a[0]),) + tuple(dl_a[1:])
        da = []
        for ci in range(3):
            dq, dk, dv, dbias = _band_bwd(s["qa"][ci], A_Q, A_K, A_V, bias_a[ci], None, dy_a[ci], 0, lse_a[ci],
                                          dl_a[ci], name=f"band_a{ci}_bwd", **A_HEADS)
            if ci == 0:
                dq, dk, dv = (a.reshape(t, 256) for a in (dq, dk, dv))
            da.append((dq, dk, dv))
            dbias_a[ci].append(dbias.reshape(4, -1))
        dqb, dkb, dvb, dbias_b, dsink = _band_bwd(nat4(s["qb"]), B_Q, B_K, B_V, bias_b, w["sink_t"], nat4(dycat), 1,
                                                  nat4(s["lse_b"]), nat4(dl_b), name="band_b_bwd", **B_HEADS)
        dbias_bs.append(dbias_b.reshape(4, -1))
        g["sink"] = dsink[:, 0, 0]
        dd = _dense_bwd(s["qd"], dycat, s["lse_d"], dl_d, tq=128, name="dense_bwd")
        dcu, dcv, g["c_ws"], dbs, g["c_g"], g["c_b"] = _c_bwd(s["proj"], dycat, w["c_g"], w["c_b"], w["c_ws"],
                                                               w["c_wst"], w["c_bst"], tm=512, name="c_bwd")
        g["c_bs"] = dbs[:, ::64].T
        db = (dqb.reshape(t, 256), dkb.reshape(t, 128), dvb.reshape(t, 128))
        dproj, dgains = _qkprep_bwd(s["proj"], da, db, dd, dcu, dcv, w["qk_gains"], cos, sins, tm=512, name="qkprep_bwd")
        g["qk_gain"] = dgains[:6, :64].reshape(3, 2, HEAD_DIM)
        g["w_in"] = _mm(s["hn0"], dproj, "tn", None, tm=1024, tn=1152, out_dtype=BF16, name="dw_in")
        dx, g["g_mix"] = _mm_bt_normbwd((dproj,), w["w_in"], (s["x0"],), w["g_mix"], dx1, tm=1024, tn=1152,
                                        name="mix_in_bwd")
        grads[li] = g
        token = grads_ready(li, "end", g)
    d_table_a = sum(_bucket_reduce(dbias_a[ci], pats_a[ci], name=f"bucket_a{ci}") for ci in range(3))
    d_table_b = _bucket_reduce(dbias_bs, pat_b, name="bucket_b")
    d_rel_bias = jnp.concatenate([d_table_a, d_table_b], axis=0).T
    return loss_tile[0, 0], dx, grads, d_rel_bias


WEIGHT_NAMES = ("rel_bias", "ln_mix_g", "w_in", "qk_gain", "sink", "c_norm_g", "c_norm_b", "c_ws", "c_bs", "out_gain",
                "w_out", "ln_ffn_g", "w_up", "conv_w", "conv_b", "w_down", "ln_ple_g", "w_ple_gate", "w_ple_proj")
COL_SHARDED = ("w_in", "w_up", "w_ple_proj")
ROW_SHARDED = ("w_out", "w_down", "w_ple_gate")
SMALL_SHARDED = ("conv_w", "out_gain")
REPLICATED = tuple(n for n in WEIGHT_NAMES if n not in COL_SHARDED + ROW_SHARDED + SMALL_SHARDED)
LOCAL_GRAD_KEY = {"ln_mix_g": "g_mix", "ln_ffn_g": "g_ffn", "ln_ple_g": "g_ple", "c_norm_g": "c_g", "c_norm_b": "c_b",
                  "w_ple_gate": "w_gate", "w_ple_proj": "w_proj"}


def _full_from_gathered(name, gathered):
    _, r, c = gathered.shape
    if name in ROW_SHARDED:
        return gathered.reshape(N_DEV * r, c)
    return jnp.transpose(gathered, (1, 0, 2)).reshape(r, N_DEV * c)


def _slots_from_full(name, full):
    rows, cols = full.shape
    if name in ROW_SHARDED:
        return full.reshape(N_DEV, rows // N_DEV, cols)
    return jnp.transpose(full.reshape(rows, N_DEV, cols // N_DEV), (1, 0, 2))


def _piece_rows(shape):
    return -(-int(np.prod(shape)) // 1024) * 8


def _pack_rows(arrays):
    pieces = []
    for a in arrays:
        n, rows = int(np.prod(a.shape)), _piece_rows(a.shape)
        flat = a.astype(F32).reshape(-1)
        if n != rows * LANES:
            flat = jnp.pad(flat, (0, rows * LANES - n))
        pieces.append(flat.reshape(rows, LANES))
    return jnp.concatenate(pieces, axis=0)


def _unpack_rows(packed, shapes):
    out, off = [], 0
    for shp in shapes:
        n, rows = int(np.prod(shp)), _piece_rows(shp)
        piece = packed[off:off + rows]
        out.append((piece if n == rows * LANES else piece.reshape(-1)[:n]).reshape(shp))
        off += rows
    return out


def kernel(x, p, rel_bias, ln_mix_g, w_in, qk_gain, sink, c_norm_g, c_norm_b, c_ws, c_bs, out_gain, w_out, ln_ffn_g, w_up, conv_w, conv_b, w_down, ln_ple_g, w_ple_gate, w_ple_proj, loss_target, m_rel_bias, m_ln_mix_g, m_w_in, m_qk_gain, m_sink, m_c_norm_g, m_c_norm_b, m_c_ws, m_c_bs, m_out_gain, m_w_out, m_ln_ffn_g, m_w_up, m_conv_w, m_conv_b, m_w_down, m_ln_ple_g, m_w_ple_gate, m_w_ple_proj, v_rel_bias, v_ln_mix_g, v_w_in, v_qk_gain, v_sink, v_c_norm_g, v_c_norm_b, v_c_ws, v_c_bs, v_out_gain, v_w_out, v_ln_ffn_g, v_w_up, v_conv_w, v_conv_b, v_w_down, v_ln_ple_g, v_w_ple_gate, v_w_ple_proj):
    env = dict(locals())
    wt = {n: env[n] for n in WEIGHT_NAMES}
    mom_m = {n: env["m_" + n] for n in WEIGHT_NAMES}
    mom_v = {n: env["v_" + n] for n in WEIGHT_NAMES}
    bl = x.shape[0]
    t = bl * SEQ
    me = 4 * lax.axis_index("x") + 2 * lax.axis_index("y") + lax.axis_index("c")

    big = COL_SHARDED + ROW_SHARDED
    full = {}
    small_shapes = [wt[n].shape for n in SMALL_SHARDED]
    small = _allgather_vmem(_pack_rows([wt[n] for n in SMALL_SHARDED]), reduce=False, name="gather_small")
    small = small.reshape(N_DEV, -1)
    off = 0
    for n, shp in zip(SMALL_SHARDED, small_shapes):
        cnt = int(np.prod(shp))
        g = small[:, off:off + cnt].reshape((N_DEV,) + tuple(shp))
        full[n] = jnp.transpose(g, (1, 2, 0, 3)).reshape(shp[0], shp[1], N_DEV * shp[2])
        off += _piece_rows(shp) * LANES

    def head_gain(li, a, b, reps):
        g = jnp.tile(qk_gain[li, a, b], reps)
        return jnp.pad(g, (0, 256 - g.shape[0]))

    wts = []
    for li in range(DEPTH):
        rows = [head_gain(li, 0, 0, 4), head_gain(li, 0, 1, 4), head_gain(li, 1, 0, 4), head_gain(li, 1, 1, 2),
                head_gain(li, 2, 0, 4), head_gain(li, 2, 1, 2), jnp.zeros((256,), F32), jnp.zeros((256,), F32)]
        wts.append(dict(
            g_mix=ln_mix_g[li].reshape(1, -1), qk_gains=jnp.stack(rows),
            sink_t=jnp.broadcast_to(sink[li][:, None, None], (4, 8, 128)),
            c_g=c_norm_g[li].reshape(1, -1), c_b=c_norm_b[li].reshape(1, -1), c_ws=c_ws[li].astype(BF16),
            c_wst=jnp.transpose(c_ws[li], (0, 2, 1)).astype(BF16), c_bst=jnp.repeat(c_bs[li].T, 64, axis=1),
            out_gain=full["out_gain"][li].reshape(1, -1), g_ffn=ln_ffn_g[li].reshape(1, -1),
            conv_w=full["conv_w"][li], conv_b=conv_b[li].reshape(1, -1), g_ple=ln_ple_g[li].reshape(1, -1)))

    local_key = {"w_ple_gate": "w_gate", "w_ple_proj": "w_proj"}

    gather_names = {"in": ("w_in",), "rest": tuple(n for n in big if n != "w_in")}
    gathered = {}
    for cid, (li, names) in enumerate(((0, gather_names["in"]), (0, gather_names["rest"]), (1, big))):
        lands = _sc_allgather([wt[n][li].astype(BF16) for n in names], collective_id=cid,
                              name=f"gather_{li}_{len(names)}")
        gathered.setdefault(li, {}).update(zip(names, lands))

    def matmul_weights(li, part, after):
        out = {}
        for n in gather_names[part]:
            g, _ = lax.optimization_barrier((gathered[li][n], after))
            out[local_key.get(n, n)] = _full_from_gathered(n, g)
        return out, None

    mid_names = ("w_ple_gate", "w_ple_proj", "w_down", "w_up", "w_out")
    end_names = ("w_in",)
    landed = {}

    def start_exchange(li, names, g, tag, cid):
        slots = [_slots_from_full(n, g[local_key.get(n, n)]) for n in names]
        lands = _sc_exchange(slots, scatter=True, collective_id=cid, name=f"grads_{li}_{tag}")
        landed.update({(n, li): land for n, land in zip(names, lands)})

    def grads_ready(li, stage, g):
        if li == 0:
            start_exchange(li, mid_names if stage == "mid" else end_names, g, stage, 5 if stage == "mid" else 6)
        elif stage == "end":
            start_exchange(li, mid_names + end_names, g, stage, 4)
        return None

    loss_part, dx, grads, d_rel_bias = _local_step(
        x.reshape(t, D_MODEL), p.reshape(DEPTH * t, PLE_DIM), loss_target.reshape(t, D_MODEL), rel_bias, wts,
        matmul_weights, grads_ready)
    loss = lax.psum(loss_part, ("x", "y", "c"))

    def local_grad(n):
        if n == "rel_bias":
            return d_rel_bias
        key = LOCAL_GRAD_KEY.get(n, n)
        return jnp.stack([grads[li][key].reshape(wt[n].shape[1:]) if n in REPLICATED else grads[li][key]
                          for li in range(DEPTH)])

    out_g, out_d, out_m, out_v = {}, {}, {}, {}
    for n in big:
        shp = wt[n].shape
        two_d = lambda a: a.reshape(-1, shp[-1])
        res = _adamw_reduce([landed[n, li] for li in range(DEPTH)], two_d(wt[n]), two_d(mom_m[n]), two_d(mom_v[n]),
                            tr=32 if n == "w_down" else 128, name="adamw_" + n)
        out_g[n], out_d[n], out_m[n], out_v[n] = [r.reshape(shp) for r in res]

    small_names = REPLICATED + SMALL_SHARDED
    small_full_shapes = [wt[n].shape if n in REPLICATED else full[n].shape for n in small_names]
    reduced = _allgather_vmem(_pack_rows([local_grad(n) for n in small_names]), reduce=True, name="allreduce_small")
    reduced = dict(zip(small_names, _unpack_rows(reduced, small_full_shapes)))
    rep_shapes = [wt[n].shape for n in REPLICATED]
    upd = _adamw_plain(_pack_rows([reduced[n] for n in REPLICATED]), _pack_rows([wt[n] for n in REPLICATED]),
                       _pack_rows([mom_m[n] for n in REPLICATED]), _pack_rows([mom_v[n] for n in REPLICATED]),
                       name="adamw_replicated")
    for dst, packed in zip((out_d, out_m, out_v), upd):
        dst.update(zip(REPLICATED, _unpack_rows(packed, rep_shapes)))
    for n in REPLICATED:
        out_g[n] = reduced[n]
    for n in SMALL_SHARDED:
        shp = wt[n].shape
        g = reduced[n].reshape(shp[0], shp[1], N_DEV, shp[2])
        g = lax.dynamic_index_in_dim(g, me, axis=2, keepdims=False)
        two_d = lambda a: a.reshape(-1, shp[-1])
        res = _adamw_plain(two_d(g), two_d(wt[n]), two_d(mom_m[n]), two_d(mom_v[n]), name="adamw_" + n)
        out_g[n] = g
        out_d[n], out_m[n], out_v[n] = [r.reshape(shp) for r in res]

    return (loss, dx.reshape(bl, SEQ, D_MODEL), *[out_g[n] for n in WEIGHT_NAMES], *[out_d[n] for n in WEIGHT_NAMES],
            *[out_m[n] for n in WEIGHT_NAMES], *[out_v[n] for n in WEIGHT_NAMES])
```

```python
import math

import jax
import jax.numpy as jnp
import numpy as np
from jax import lax
from jax.experimental import pallas as pl
from jax.experimental.pallas import tpu as pltpu
from jax.experimental.pallas import tpu_sc as plsc

F32 = jnp.float32
BF16 = jnp.bfloat16
HI = lax.Precision.HIGHEST

N_DEV = 8
D_MODEL = 1024
SEQ = 2048
DEPTH = 2
HEAD_DIM = 64
IN_WIDTH = 2304
D_FF = 2816
PLE_DIM = 256
C_CHUNK = 128
C_GROUPS = 4
DILATED_CFGS = ((128, 1), (512, 4), (2048, 16))
DILATIONS = tuple(d for _, d in DILATED_CFGS)
A_RADIUS = 64
SWA_RADIUS = 128
BAND_BLOCK = 256
GRID_W = 64
ROPE_THETA = 10000.0
REL_BUCKETS = 32
REL_MAX_DIST = 1024
EPS = 1e-6
NEG_INF = -1e30
ATTN_SCALE = HEAD_DIM ** -0.5
LANES = 128

ADAM_LR = 0.001
ADAM_B1 = 0.9
ADAM_B2 = 0.999
ADAM_EPS = 1e-08
ADAM_WD = 0.01
ADAM_STEP = 10

MESH = pl.DeviceIdType.MESH
NT = (((1,), (1,)), ((), ()))
TN = (((0,), (0,)), ((), ()))
ARB = "arbitrary"
PAR = "parallel"


def _cparams(*sem):
    return pltpu.CompilerParams(dimension_semantics=tuple(sem))


def _sds(shape, dtype):
    return jax.ShapeDtypeStruct(tuple(shape), dtype)


def _group_sum_matrix(n, same_group):
    r = lax.broadcasted_iota(jnp.int32, (n, n), 0)
    c = lax.broadcasted_iota(jnp.int32, (n, n), 1)
    if same_group:
        return ((r >> 6) == (c >> 6)).astype(F32)
    return ((r & 63) == (c & 63)).astype(F32)


def _seg_sum(x, e):
    eb = e.astype(BF16)
    hi = x.astype(BF16)
    lo = (x - hi.astype(F32)).astype(BF16)
    return jnp.dot(hi, eb, preferred_element_type=F32) + jnp.dot(lo, eb, preferred_element_type=F32)


def _gelu(x):
    c = math.sqrt(2.0 / math.pi)
    return 0.5 * x * (1.0 + jnp.tanh(c * (x + 0.044715 * (x * x * x))))


def _gelu_grad(x):
    c = math.sqrt(2.0 / math.pi)
    t = jnp.tanh(c * (x + 0.044715 * (x * x * x)))
    return 0.5 * (1.0 + t) + 0.5 * x * (1.0 - t * t) * c * (1.0 + 3.0 * 0.044715 * (x * x))


def _sigmoid(x):
    return 1.0 / (1.0 + jnp.exp(-x))


def _scatter_cols(scratch, first, val):
    for c in range(val.shape[1] // LANES):
        scratch[first + c] = val[:, c * LANES:(c + 1) * LANES]


def _gather_cols(scratch, first, ncol):
    return jnp.concatenate([scratch[first + c] for c in range(ncol)], axis=1)


def _read_residue(scratch, first, ncol, r, d):
    n = scratch.shape[1] // d
    return jnp.concatenate([scratch.at[first + c][pl.ds(r, n, stride=d), :] for c in range(ncol)], axis=1)


def _write_residue(scratch, first, r, d, val):
    n = scratch.shape[1] // d
    for c in range(val.shape[1] // LANES):
        scratch.at[first + c][pl.ds(r, n, stride=d), :] = val[:, c * LANES:(c + 1) * LANES]


def _norm_mm(xs, gain, w, res, *, tm, tn, name, out_dtype=F32):
    t = xs[0].shape[0]
    k = sum(x.shape[1] for x in xs)
    n = w.shape[1]
    ng = len(xs)
    has_res = res is not None

    def body(*refs):
        x_refs = refs[:ng]
        g_ref, w_ref = refs[ng], refs[ng + 1]
        res_ref = refs[ng + 2] if has_res else None
        hn_ref, o_ref, hn_s = refs[ng + 2 + has_res:]

        @pl.when(pl.program_id(1) == 0)
        def _():
            off = 0
            for xr in x_refs:
                x = xr[...]
                wd = x.shape[1]
                r = lax.rsqrt(jnp.mean(x * x, axis=-1, keepdims=True) + EPS)
                hn_s[:, off:off + wd] = (x * r * g_ref[:, off:off + wd]).astype(BF16)
                off += wd
            hn_ref[...] = hn_s[...]

        acc = jnp.dot(hn_s[...], w_ref[...], preferred_element_type=F32)
        if has_res:
            acc = acc + res_ref[...]
        o_ref[...] = acc.astype(out_dtype)

    in_specs = [pl.BlockSpec((tm, x.shape[1]), lambda i, j: (i, 0)) for x in xs]
    in_specs += [pl.BlockSpec((1, k), lambda i, j: (0, 0)), pl.BlockSpec((k, tn), lambda i, j: (0, j))]
    args = list(xs) + [gain, w]
    if has_res:
        in_specs.append(pl.BlockSpec((tm, tn), lambda i, j: (i, j)))
        args.append(res)
    return pl.pallas_call(
        body, name=name, grid=(t // tm, n // tn), in_specs=in_specs,
        out_specs=[pl.BlockSpec((tm, k), lambda i, j: (i, 0)), pl.BlockSpec((tm, tn), lambda i, j: (i, j))],
        out_shape=[_sds((t, k), BF16), _sds((t, n), out_dtype)],
        scratch_shapes=[pltpu.VMEM((tm, k), BF16)],
        compiler_params=_cparams(PAR, ARB),
    )(*args)


def _mm(a, b, mode, res, *, tm, tn, out_dtype, name, a_rows=None):
    if mode == "tn":
        kk, m = a.shape
        blk_a = 0
        if a_rows is not None:
            blk_a, kk = a_rows
        a_spec = pl.BlockSpec((kk, tm), lambda i, j: (blk_a, i))
    else:
        m, kk = a.shape
        a_spec = pl.BlockSpec((tm, kk), lambda i, j: (i, 0))
    if mode == "nt":
        n = b.shape[0]
        b_spec = pl.BlockSpec((tn, kk), lambda i, j: (j, 0))
    else:
        n = b.shape[1]
        b_spec = pl.BlockSpec((kk, tn), lambda i, j: (0, j))
    has_res = res is not None

    def body(*refs):
        a_ref, b_ref = refs[0], refs[1]
        o_ref = refs[-1]
        av = a_ref[...].astype(BF16)
        bv = b_ref[...].astype(BF16)
        if mode == "nn":
            acc = jnp.dot(av, bv, preferred_element_type=F32)
        elif mode == "nt":
            acc = lax.dot_general(av, bv, NT, preferred_element_type=F32)
        else:
            acc = lax.dot_general(av, bv, TN, preferred_element_type=F32)
        if has_res:
            acc = acc + refs[2][...]
        o_ref[...] = acc.astype(out_dtype)

    in_specs = [a_spec, b_spec]
    args = [a, b]
    if has_res:
        in_specs.append(pl.BlockSpec((tm, tn), lambda i, j: (i, j)))
        args.append(res)
    return pl.pallas_call(
        body, name=name, grid=(m // tm, n // tn), in_specs=in_specs,
        out_specs=pl.BlockSpec((tm, tn), lambda i, j: (i, j)),
        out_shape=_sds((m, n), out_dtype),
        compiler_params=_cparams(PAR, PAR),
    )(*args)


def _mm_bt_normbwd(dys, w, xs, gain, dres, *, tm, tn, name, emit_bf16=False):
    t, wd_each = dys[0].shape
    nd = len(dys)
    per = wd_each // tn
    nj = nd * per
    k = w.shape[0]
    ng = len(xs)
    has_res = dres is not None

    def body(*refs):
        dy_refs = refs[:nd]
        w_ref = refs[nd]
        x_refs = refs[nd + 1:nd + 1 + ng]
        g_ref = refs[nd + 1 + ng]
        dres_ref = refs[nd + 2 + ng] if has_res else None
        outs = refs[nd + 2 + ng + has_res:]
        dx_ref = outs[0]
        dxb_ref = outs[1] if emit_bf16 else None
        dg_ref, acc = outs[1 + emit_bf16:]
        i, j = pl.program_id(0), pl.program_id(1)

        @pl.when(j == 0)
        def _():
            acc[...] = jnp.zeros_like(acc)

        for d, dy_ref in enumerate(dy_refs):
            @pl.when((j >= d * per) & (j < (d + 1) * per))
            def _(dy_ref=dy_ref):
                acc[...] += lax.dot_general(dy_ref[...].astype(BF16), w_ref[...], NT, preferred_element_type=F32)

        @pl.when(j == nj - 1)
        def _():
            @pl.when(i == 0)
            def _():
                dg_ref[...] = jnp.zeros_like(dg_ref)

            off = 0
            for xr in x_refs:
                x = xr[...]
                wd = x.shape[1]
                g = g_ref[:, off:off + wd]
                dyn = acc[:, off:off + wd]
                r = lax.rsqrt(jnp.mean(x * x, axis=-1, keepdims=True) + EPS)
                gdy = dyn * g
                dx = r * gdy - x * (r * r * r * jnp.mean(gdy * x, axis=-1, keepdims=True))
                if has_res:
                    dx = dx + dres_ref[:, off:off + wd]
                dx_ref[:, off:off + wd] = dx
                if emit_bf16:
                    dxb_ref[:, off:off + wd] = dx.astype(BF16)
                dg_ref[:, off:off + wd] += jnp.sum(dyn * x * r, axis=0, keepdims=True)
                off += wd

    def dy_map(d):
        return lambda i, j: (i, jnp.clip(j - d * per, 0, per - 1))

    in_specs = [pl.BlockSpec((tm, tn), dy_map(d)) for d in range(nd)]
    in_specs.append(pl.BlockSpec((k, tn), lambda i, j: (0, j)))
    in_specs += [pl.BlockSpec((tm, x.shape[1]), lambda i, j: (i, 0)) for x in xs]
    in_specs.append(pl.BlockSpec((1, k), lambda i, j: (0, 0)))
    args = list(dys) + [w] + list(xs) + [gain]
    if has_res:
        in_specs.append(pl.BlockSpec((tm, k), lambda i, j: (i, 0)))
        args.append(dres)
    row = pl.BlockSpec((tm, k), lambda i, j: (i, 0))
    out_specs = [row] + ([row] if emit_bf16 else []) + [pl.BlockSpec((1, k), lambda i, j: (0, 0))]
    out_shape = [_sds((t, k), F32)] + ([_sds((t, k), BF16)] if emit_bf16 else []) + [_sds((1, k), F32)]
    return pl.pallas_call(
        body, name=name, grid=(t // tm, nj), in_specs=in_specs, out_specs=out_specs, out_shape=out_shape,
        scratch_shapes=[pltpu.VMEM((tm, k), F32)],
        compiler_params=_cparams(ARB, ARB),
    )(*args)


def _rope_partner(y):
    n = y.shape[1]
    lane = lax.broadcasted_iota(jnp.int32, y.shape, 1)
    return jnp.where((lane & 31) < 16, pltpu.roll(y, n - 16, 1), pltpu.roll(y, 16, 1))


def _residue_specs(tm, width, nt):
    specs = [pl.BlockSpec((tm, width), lambda b, i: (b * nt + i, 0))]
    for d in DILATIONS[1:]:
        specs.append(pl.BlockSpec((None, d, tm // d, width), lambda b, i: (b, 0, i, 0)))
    return specs


def _residue_shapes(bl, width, dtype):
    return [_sds((bl * SEQ, width), dtype)] + [_sds((bl, d, SEQ // d, width), dtype) for d in DILATIONS[1:]]


def _qkprep_fwd(proj, gains, cos, sins, *, tm, name):
    t = proj.shape[0]
    bl = t // SEQ
    nt = SEQ // tm

    def body(p_ref, g_ref, c_ref, s_ref, qa1_ref, qa4_ref, qa16_ref, qb_ref, qd_ref, scr):
        e = _group_sum_matrix(256, True)

        def hn(x, row):
            wd = x.shape[1]
            ms = _seg_sum(x * x, e[:wd, :wd]) * (1.0 / HEAD_DIM)
            return x * lax.rsqrt(ms + EPS) * g_ref[row:row + 1, :wd]

        qa = jnp.concatenate([hn(p_ref[:, 0:256], 0) * ATTN_SCALE, hn(p_ref[:, 256:512], 1), p_ref[:, 512:768]], axis=1)
        qa1_ref[...] = qa.astype(BF16)
        _scatter_cols(scr, 0, qa)
        for d, ref in ((4, qa4_ref), (16, qa16_ref)):
            for r in range(d):
                ref[r] = _read_residue(scr, 0, 6, r, d).astype(BF16)
        qb_ref[:, 0:256] = (hn(p_ref[:, 768:1024], 2) * ATTN_SCALE).astype(BF16)
        qb_ref[:, 256:384] = hn(p_ref[:, 1024:1152], 3).astype(BF16)
        qb_ref[:, 384:512] = p_ref[:, 1152:1280].astype(BF16)
        yq = hn(p_ref[:, 1792:2048], 4)
        yq = yq * c_ref[...] + _rope_partner(yq) * s_ref[...]
        qd_ref[:, 0:256] = (yq * ATTN_SCALE).astype(BF16)
        yk = hn(p_ref[:, 2048:2176], 5)
        yk = yk * c_ref[:, 0:128] + _rope_partner(yk) * s_ref[:, 0:128]
        qd_ref[:, 256:384] = yk.astype(BF16)
        qd_ref[:, 384:512] = p_ref[:, 2176:2304].astype(BF16)

    row = lambda width: pl.BlockSpec((tm, width), lambda b, i: (b * nt + i, 0))
    tab = pl.BlockSpec((tm, 256), lambda b, i: (i, 0))
    return pl.pallas_call(
        body, name=name, grid=(bl, nt),
        in_specs=[row(IN_WIDTH), pl.BlockSpec((8, 256), lambda b, i: (0, 0)), tab, tab],
        out_specs=_residue_specs(tm, 768, nt) + [row(512), row(512)],
        out_shape=_residue_shapes(bl, 768, BF16) + [_sds((t, 512), BF16), _sds((t, 512), BF16)],
        scratch_shapes=[pltpu.VMEM((6, tm, LANES), F32)],
        compiler_params=_cparams(PAR, PAR),
    )(proj, gains, cos, sins)


def _qkprep_bwd(proj, da, db, dd, dcu, dcv, gains, cos, sins, *, tm, name):
    t = proj.shape[0]
    bl = t // SEQ
    nt = SEQ // tm
    flat = [a for cfg in da for a in cfg] + list(db) + list(dd) + [dcu, dcv]

    def body(*refs):
        p_ref, g_ref, c_ref, s_ref = refs[:4]
        d_refs = refs[4:4 + len(flat)]
        dp_ref, dg_ref, scr = refs[4 + len(flat):]
        a_refs = d_refs[:9]
        dqb_ref, dkb_ref, dvb_ref, dqd_ref, dkd_ref, dvd_ref, dcu_ref, dcv_ref = d_refs[9:]
        e = _group_sum_matrix(256, True)
        first = (pl.program_id(0) == 0) & (pl.program_id(1) == 0)
        last = (pl.program_id(0) == bl - 1) & (pl.program_id(1) == nt - 1)

        @pl.when(first)
        def _():
            dg_ref[...] = jnp.zeros_like(dg_ref)

        def hn_bwd(x, dy, row):
            wd = x.shape[1]
            ee = e[:wd, :wd]
            g = g_ref[row:row + 1, :wd]
            r = lax.rsqrt(_seg_sum(x * x, ee) * (1.0 / HEAD_DIM) + EPS)
            gdy = dy * g
            dx = r * gdy - x * (r * r * r * (_seg_sum(gdy * x, ee) * (1.0 / HEAD_DIM)))
            dg_ref[row:row + 1, :wd] += jnp.sum(dy * x * r, axis=0, keepdims=True)
            return dx

        def rope_bwd(dy, wd):
            return dy * c_ref[:, :wd] + _rope_partner(dy * s_ref[:, :wd])

        dqkv = jnp.concatenate([a_refs[0][...], a_refs[1][...], a_refs[2][...]], axis=1)
        for ci, d in ((1, 4), (2, 16)):
            for r in range(d):
                part = jnp.concatenate([a_refs[3 * ci + m][r] for m in range(3)], axis=1)
                _write_residue(scr, 0, r, d, part)
            dqkv = dqkv + _gather_cols(scr, 0, 6)
        dp_ref[:, 0:256] = hn_bwd(p_ref[:, 0:256], dqkv[:, 0:256] * ATTN_SCALE, 0).astype(BF16)
        dp_ref[:, 256:512] = hn_bwd(p_ref[:, 256:512], dqkv[:, 256:512], 1).astype(BF16)
        dp_ref[:, 512:768] = dqkv[:, 512:768].astype(BF16)
        dp_ref[:, 768:1024] = hn_bwd(p_ref[:, 768:1024], dqb_ref[...] * ATTN_SCALE, 2).astype(BF16)
        dp_ref[:, 1024:1152] = hn_bwd(p_ref[:, 1024:1152], dkb_ref[...], 3).astype(BF16)
        dp_ref[:, 1152:1280] = dvb_ref[...].astype(BF16)
        dp_ref[:, 1280:1536] = dcu_ref[...].astype(BF16)
        dp_ref[:, 1536:1792] = dcv_ref[...].astype(BF16)
        dp_ref[:, 1792:2048] = hn_bwd(p_ref[:, 1792:2048], rope_bwd(dqd_ref[...] * ATTN_SCALE, 256), 4).astype(BF16)
        dp_ref[:, 2048:2176] = hn_bwd(p_ref[:, 2048:2176], rope_bwd(dkd_ref[...], 128), 5).astype(BF16)
        dp_ref[:, 2176:2304] = dvd_ref[...].astype(BF16)

        @pl.when(last)
        def _():
            dg_ref[...] = _seg_sum(dg_ref[...], _group_sum_matrix(256, False))

    row = lambda width: pl.BlockSpec((tm, width), lambda b, i: (b * nt + i, 0))
    tab = pl.BlockSpec((tm, 256), lambda b, i: (i, 0))
    in_specs = [row(IN_WIDTH), pl.BlockSpec((8, 256), lambda b, i: (0, 0)), tab, tab]
    res_specs = _residue_specs(tm, 256, nt)
    in_specs += [res_specs[ci] for ci in range(3) for _ in range(3)]
    in_specs += [row(a.shape[1]) for a in flat[9:]]
    return pl.pallas_call(
        body, name=name, grid=(bl, nt), in_specs=in_specs,
        out_specs=[row(IN_WIDTH), pl.BlockSpec((8, 256), lambda b, i: (0, 0))],
        out_shape=[_sds((t, IN_WIDTH), BF16), _sds((8, 256), F32)],
        scratch_shapes=[pltpu.VMEM((6, tm, LANES), F32)],
        compiler_params=_cparams(ARB, ARB),
    )(proj, gains, cos, sins, *flat)


def _band_spec(seq_len, spec):
    width, idx = spec
    return pl.BlockSpec((None, None, seq_len, width), lambda b, r: (b, r, 0, idx))


def _fill_padded(dst, src_ref, rad, seq_len):
    z = jnp.zeros((rad, dst.shape[1]), dst.dtype)
    dst[0:rad, :] = z
    dst[rad + seq_len:rad + seq_len + rad, :] = z
    dst[rad:rad + seq_len, :] = src_ref[...]


def _band_fwd(src, qs, ks, vs, bias, sink, *, rad, nh, nkv, name):
    bl, dil, sl, _ = src.shape
    blk = bias.shape[1]
    kw = blk + 2 * rad
    nb = sl // blk
    rep = nh // nkv
    has_sink = sink is not None

    def body(*refs):
        q_ref, k_ref, v_ref, b_ref = refs[:4]
        s_ref = refs[4] if has_sink else None
        o_ref, l_ref, kp, vp = refs[4 + has_sink:]
        _fill_padded(kp, k_ref, rad, sl)
        _fill_padded(vp, v_ref, rad, sl)

        def blk_body(i, carry):
            r0 = pl.multiple_of(i * blk, blk)
            qb = q_ref[pl.ds(r0, blk), :]
            kwin = kp[pl.ds(r0, kw), :]
            vwin = vp[pl.ds(r0, kw), :]
            col = r0 - rad + lax.broadcasted_iota(jnp.int32, (blk, kw), 1)
            neg = jnp.where((col >= 0) & (col < sl), 0.0, NEG_INF).astype(F32)
            for h in range(nh):
                g = h // rep
                hs = slice(h * HEAD_DIM, (h + 1) * HEAD_DIM)
                gs = slice(g * HEAD_DIM, (g + 1) * HEAD_DIM)
                s = lax.dot_general(qb[:, hs], kwin[:, gs], NT, preferred_element_type=F32)
                s = s + b_ref[h] + neg
                m = jnp.max(s, axis=1, keepdims=True)
                if has_sink:
                    sk = s_ref[h][0:1, 0:1]
                    m = jnp.maximum(m, sk)
                p = jnp.exp(s - m)
                den = jnp.sum(p, axis=1, keepdims=True)
                if has_sink:
                    den = den + jnp.exp(sk - m)
                o = jnp.dot(p.astype(BF16), vwin[:, gs], preferred_element_type=F32) / den
                o_ref[pl.ds(r0, blk), hs] = o
                l_ref[pl.ds(r0, blk), hs] = jnp.broadcast_to(m + jnp.log(den), (blk, HEAD_DIM))
            return carry

        lax.fori_loop(0, nb, blk_body, 0)

    in_specs = [_band_spec(sl, qs), _band_spec(sl, ks), _band_spec(sl, vs),
                pl.BlockSpec((nh, blk, kw), lambda b, r: (0, 0, 0))]
    args = [src] * 3 + [bias]
    if has_sink:
        in_specs.append(pl.BlockSpec((nh, 8, 128), lambda b, r: (0, 0, 0)))
        args.append(sink)
    return pl.pallas_call(
        body, name=name, grid=(bl, dil), in_specs=in_specs,
        out_specs=[_band_spec(sl, (256, 0))] * 2,
        out_shape=[_sds((bl, dil, sl, 256), F32)] * 2,
        scratch_shapes=[pltpu.VMEM((sl + 2 * rad, ks[0]), BF16), pltpu.VMEM((sl + 2 * rad, vs[0]), BF16)],
        compiler_params=_cparams(PAR, PAR),
    )(*args)


def _band_bwd(src, qs, ks, vs, bias, sink, dy, dcol, lse, delta, *, rad, nh, nkv, name):
    bl, dil, sl, _ = src.shape
    blk = bias.shape[1]
    kw = blk + 2 * rad
    nb = sl // blk
    rep = nh // nkv
    has_sink = sink is not None
    wk, wv = ks[0], vs[0]

    def body(*refs):
        q_ref, k_ref, v_ref, b_ref = refs[:4]
        s_ref = refs[4] if has_sink else None
        do_ref, l_ref, dl_ref = refs[4 + has_sink:7 + has_sink]
        outs = refs[7 + has_sink:]
        if has_sink:
            dq_ref, dk_ref, dv_ref, db_ref, dsk_ref, kp, vp, dka, dva = outs
        else:
            dq_ref, dk_ref, dv_ref, db_ref, kp, vp, dka, dva = outs

        @pl.when((pl.program_id(0) == 0) & (pl.program_id(1) == 0))
        def _():
            db_ref[...] = jnp.zeros_like(db_ref)
            if has_sink:
                dsk_ref[...] = jnp.zeros_like(dsk_ref)

        _fill_padded(kp, k_ref, rad, sl)
        _fill_padded(vp, v_ref, rad, sl)
        dka[...] = jnp.zeros_like(dka)
        dva[...] = jnp.zeros_like(dva)

        def blk_body(i, carry):
            r0 = pl.multiple_of(i * blk, blk)
            qb = q_ref[pl.ds(r0, blk), :]
            kwin = kp[pl.ds(r0, kw), :]
            vwin = vp[pl.ds(r0, kw), :]
            dob = do_ref[pl.ds(r0, blk), :].astype(BF16)
            lb = l_ref[pl.ds(r0, blk), :]
            dlb = dl_ref[pl.ds(r0, blk), :]
            col = r0 - rad + lax.broadcasted_iota(jnp.int32, (blk, kw), 1)
            neg = jnp.where((col >= 0) & (col < sl), 0.0, NEG_INF).astype(F32)
            for h in range(nh):
                g = h // rep
                hs = slice(h * HEAD_DIM, (h + 1) * HEAD_DIM)
                gs = slice(g * HEAD_DIM, (g + 1) * HEAD_DIM)
                qh, kh, vh, doh = qb[:, hs], kwin[:, gs], vwin[:, gs], dob[:, hs]
                lh = lb[:, h * HEAD_DIM:h * HEAD_DIM + 1]
                dlh = dlb[:, h * HEAD_DIM:h * HEAD_DIM + 1]
                s = lax.dot_general(qh, kh, NT, preferred_element_type=F32) + b_ref[h] + neg
                p = jnp.exp(s - lh)
                dp = lax.dot_general(doh, vh, NT, preferred_element_type=F32)
                ds = p * (dp - dlh)
                dsb = ds.astype(BF16)
                dq_ref[pl.ds(r0, blk), hs] = jnp.dot(dsb, kh, preferred_element_type=F32)
                dka[pl.ds(r0, kw), gs] += lax.dot_general(dsb, qh, TN, preferred_element_type=F32)
                dva[pl.ds(r0, kw), gs] += lax.dot_general(p.astype(BF16), doh, TN, preferred_element_type=F32)
                db_ref[h] += ds
                if has_sink:
                    ps = jnp.exp(s_ref[h][0:1, 0:1] - lh)
                    dsk_ref[h] += jnp.broadcast_to(-jnp.sum(ps * dlh, axis=0, keepdims=True), (8, 128))
            return carry

        lax.fori_loop(0, nb, blk_body, 0)
        dk_ref[...] = dka[rad:rad + sl, :]
        dv_ref[...] = dva[rad:rad + sl, :]

    const3 = lambda b, r: (0, 0, 0)
    in_specs = [_band_spec(sl, qs), _band_spec(sl, ks), _band_spec(sl, vs), pl.BlockSpec((nh, blk, kw), const3)]
    args = [src] * 3 + [bias]
    if has_sink:
        in_specs.append(pl.BlockSpec((nh, 8, 128), const3))
        args.append(sink)
    row = _band_spec(sl, (256, 0))
    in_specs += [_band_spec(sl, (256, dcol)), row, row]
    args += [dy, lse, delta]
    out_specs = [row, _band_spec(sl, (wk, 0)), _band_spec(sl, (wv, 0)), pl.BlockSpec((nh, blk, kw), const3)]
    out_shape = [_sds((bl, dil, sl, 256), F32), _sds((bl, dil, sl, wk), F32), _sds((bl, dil, sl, wv), F32),
                 _sds((nh, blk, kw), F32)]
    if has_sink:
        out_specs.append(pl.BlockSpec((nh, 8, 128), const3))
        out_shape.append(_sds((nh, 8, 128), F32))
    return pl.pallas_call(
        body, name=name, grid=(bl, dil), in_specs=in_specs, out_specs=out_specs, out_shape=out_shape,
        scratch_shapes=[pltpu.VMEM((sl + 2 * rad, wk), BF16), pltpu.VMEM((sl + 2 * rad, wv), BF16),
                        pltpu.VMEM((sl + 2 * rad, wk), F32), pltpu.VMEM((sl + 2 * rad, wv), F32)],
        compiler_params=_cparams(ARB, ARB),
    )(*args)


def _combine_a(os_, ls_, *, tm, name):
    bl = os_[1].shape[0]
    t = bl * SEQ
    nt = SEQ // tm

    def body(o1, o4, o16, l1, l4, l16, y_ref, lt_ref, scr):
        for k, (d, ref) in enumerate(((4, o4), (16, o16), (4, l4), (16, l16))):
            for r in range(d):
                _write_residue(scr, 2 * k, r, d, ref[r])
        o2, o3, b, c = (_gather_cols(scr, 2 * k, 2) for k in range(4))
        a = l1[...]
        m = jnp.maximum(jnp.maximum(a, b), c)
        ea, eb, ec = jnp.exp(a - m), jnp.exp(b - m), jnp.exp(c - m)
        den = ea + eb + ec
        y_ref[...] = (ea / den) * o1[...] + (eb / den) * o2 + (ec / den) * o3
        lt_ref[...] = m + jnp.log(den)

    specs = _residue_specs(tm, 256, nt)
    return pl.pallas_call(
        body, name=name, grid=(bl, nt), in_specs=specs * 2, out_specs=[specs[0]] * 2,
        out_shape=[_sds((t, 256), F32)] * 2, scratch_shapes=[pltpu.VMEM((8, tm, LANES), F32)],
        compiler_params=_cparams(PAR, PAR),
    )(*os_, *ls_)


def _deltas(dycat, ya, yb, yd, lse_a, *, tm, name):
    t = ya.shape[0]
    bl = t // SEQ
    nt = SEQ // tm

    def body(dy_ref, ya_ref, yb_ref, yd_ref, la_ref, dy4, dy16, l4, l16, da1, da4, da16, db_ref, dd_ref, scr):
        e = _group_sum_matrix(256, True)
        dya = dy_ref[:, 0:256]
        dla = _seg_sum(dya * ya_ref[...], e)
        da1[...] = dla
        db_ref[...] = _seg_sum(dy_ref[:, 256:512] * yb_ref[...], e)
        dd_ref[...] = _seg_sum(dy_ref[:, 768:1024] * yd_ref[...], e)
        for k, (val, r4, r16) in enumerate(((dya, dy4, dy16), (la_ref[...], l4, l16), (dla, da4, da16))):
            _scatter_cols(scr, 2 * k, val)
            for d, ref in ((4, r4), (16, r16)):
                for r in range(d):
                    ref[r] = _read_residue(scr, 2 * k, 2, r, d)

    specs = _residue_specs(tm, 256, nt)
    nat = specs[0]
    shapes = _residue_shapes(bl, 256, F32)
    outs = pl.pallas_call(
        body, name=name, grid=(bl, nt),
        in_specs=[pl.BlockSpec((tm, 1024), lambda b, i: (b * nt + i, 0)), nat, nat, nat, nat],
        out_specs=specs[1:] + specs[1:] + specs + [nat, nat],
        out_shape=shapes[1:] + shapes[1:] + shapes + [shapes[0], shapes[0]],
        scratch_shapes=[pltpu.VMEM((6, tm, LANES), F32)],
        compiler_params=_cparams(PAR, PAR),
    )(dycat, ya, yb, yd, lse_a)
    return outs[0:2], outs[2:4], outs[4:7], outs[7], outs[8]


def _dense_fwd(qd, *, tq, name):
    t = qd.shape[0]
    bl = t // SEQ
    nq = SEQ // tq

    def body(q_ref, k_ref, v_ref, o_ref, l_ref):
        q = q_ref[...]
        for g in range(2):
            h0, h1 = 2 * g, 2 * g + 1
            q2 = jnp.concatenate([q[:, h0 * 64:(h0 + 1) * 64], q[:, h1 * 64:(h1 + 1) * 64]], axis=0)
            kg = k_ref[:, g * 64:(g + 1) * 64]
            vg = v_ref[:, g * 64:(g + 1) * 64]
            s = lax.dot_general(q2, kg, NT, preferred_element_type=F32)
            m = jnp.max(s, axis=1, keepdims=True)
            p = jnp.exp(s - m)
            den = jnp.sum(p, axis=1, keepdims=True)
            o2 = jnp.dot(p.astype(BF16), vg, preferred_element_type=F32) / den
            l2 = jnp.broadcast_to(m + jnp.log(den), (2 * tq, 64))
            o_ref[:, h0 * 64:(h0 + 1) * 64] = o2[:tq]
            o_ref[:, h1 * 64:(h1 + 1) * 64] = o2[tq:]
            l_ref[:, h0 * 64:(h0 + 1) * 64] = l2[:tq]
            l_ref[:, h1 * 64:(h1 + 1) * 64] = l2[tq:]

    q3 = qd.reshape(bl, SEQ, 512)
    o, lse = pl.pallas_call(
        body, name=name, grid=(bl, nq),
        in_specs=[pl.BlockSpec((None, tq, 256), lambda b, i: (b, i, 0)),
                  pl.BlockSpec((None, SEQ, 128), lambda b, i: (b, 0, 2)),
                  pl.BlockSpec((None, SEQ, 128), lambda b, i: (b, 0, 3))],
        out_specs=[pl.BlockSpec((None, tq, 256), lambda b, i: (b, i, 0))] * 2,
        out_shape=[_sds((bl, SEQ, 256), F32)] * 2,
        compiler_params=_cparams(PAR, PAR),
    )(q3, q3, q3)
    return o.reshape(t, 256), lse.reshape(t, 256)


def _dense_bwd(qd, dycat, lse, delta, *, tq, name):
    t = qd.shape[0]
    bl = t // SEQ
    nq = SEQ // tq

    def body(q_ref, k_ref, v_ref, do_ref, l_ref, dl_ref, dq_ref, dk_ref, dv_ref):
        @pl.when(pl.program_id(1) == 0)
        def _():
            dk_ref[...] = jnp.zeros_like(dk_ref)
            dv_ref[...] = jnp.zeros_like(dv_ref)

        q = q_ref[...]
        do = do_ref[...].astype(BF16)
        lv = l_ref[...]
        dlv = dl_ref[...]
        for g in range(2):
            h0, h1 = 2 * g, 2 * g + 1
            q2 = jnp.concatenate([q[:, h0 * 64:(h0 + 1) * 64], q[:, h1 * 64:(h1 + 1) * 64]], axis=0)
            do2 = jnp.concatenate([do[:, h0 * 64:(h0 + 1) * 64], do[:, h1 * 64:(h1 + 1) * 64]], axis=0)
            l2 = jnp.concatenate([lv[:, h0 * 64:h0 * 64 + 1], lv[:, h1 * 64:h1 * 64 + 1]], axis=0)
            dl2 = jnp.concatenate([dlv[:, h0 * 64:h0 * 64 + 1], dlv[:, h1 * 64:h1 * 64 + 1]], axis=0)
            kg = k_ref[:, g * 64:(g + 1) * 64]
            vg = v_ref[:, g * 64:(g + 1) * 64]
            s = lax.dot_general(q2, kg, NT, preferred_element_type=F32)
            p = jnp.exp(s - l2)
            dp = lax.dot_general(do2, vg, NT, preferred_element_type=F32)
            ds = (p * (dp - dl2)).astype(BF16)
            dq2 = jnp.dot(ds, kg, preferred_element_type=F32)
            dq_ref[:, h0 * 64:(h0 + 1) * 64] = dq2[:tq]
            dq_ref[:, h1 * 64:(h1 + 1) * 64] = dq2[tq:]
            dk_ref[:, g * 64:(g + 1) * 64] += lax.dot_general(ds, q2, TN, preferred_element_type=F32)
            dv_ref[:, g * 64:(g + 1) * 64] += lax.dot_general(p.astype(BF16), do2, TN, preferred_element_type=F32)

    q3 = qd.reshape(bl, SEQ, 512)
    tile = pl.BlockSpec((None, tq, 256), lambda b, i: (b, i, 0))
    full = pl.BlockSpec((None, SEQ, 128), lambda b, i: (b, 0, 0))
    dq, dk, dv = pl.pallas_call(
        body, name=name, grid=(bl, nq),
        in_specs=[tile, pl.BlockSpec((None, SEQ, 128), lambda b, i: (b, 0, 2)),
                  pl.BlockSpec((None, SEQ, 128), lambda b, i: (b, 0, 3)),
                  pl.BlockSpec((None, tq, 256), lambda b, i: (b, i, 3)), tile, tile],
        out_specs=[tile, full, full],
        out_shape=[_sds((bl, SEQ, 256), F32), _sds((bl, SEQ, 128), F32), _sds((bl, SEQ, 128), F32)],
        compiler_params=_cparams(PAR, ARB),
    )(q3, q3, q3, dycat.reshape(bl, SEQ, 1024), lse.reshape(bl, SEQ, 256), delta.reshape(bl, SEQ, 256))
    return dq.reshape(t, 256), dk.reshape(t, 128), dv.reshape(t, 128)


def _c_norm(cv, gam, bet):
    vg = _gelu(cv)
    mu = jnp.mean(vg, axis=-1, keepdims=True)
    xc = vg - mu
    r = lax.rsqrt(jnp.mean(xc * xc, axis=-1, keepdims=True) + EPS)
    xhat = xc * r
    return xhat * gam + bet, xhat, r


def _c_fwd(proj, gam, bet, ws, bst, *, tm, name):
    t = proj.shape[0]
    nch = tm // C_CHUNK

    def body(u_ref, v_ref, g_ref, b_ref, ws_ref, bs_ref, y_ref):
        vn, _, _ = _c_norm(v_ref[...], g_ref[...], b_ref[...])
        vnb = vn.astype(BF16)
        for c in range(nch):
            rows = slice(c * C_CHUNK, (c + 1) * C_CHUNK)
            for g in range(C_GROUPS):
                gs = slice(g * 64, (g + 1) * 64)
                mixed = jnp.dot(ws_ref[g], vnb[rows, gs], preferred_element_type=F32) + bs_ref[:, gs]
                y_ref[rows, gs] = _gelu(u_ref[rows, gs]) * mixed

    vec = pl.BlockSpec((1, 256), lambda i: (0, 0))
    return pl.pallas_call(
        body, name=name, grid=(t // tm,),
        in_specs=[pl.BlockSpec((tm, 256), lambda i: (i, 5)), pl.BlockSpec((tm, 256), lambda i: (i, 6)), vec, vec,
                  pl.BlockSpec((C_GROUPS, C_CHUNK, C_CHUNK), lambda i: (0, 0, 0)),
                  pl.BlockSpec((C_CHUNK, 256), lambda i: (0, 0))],
        out_specs=pl.BlockSpec((tm, 256), lambda i: (i, 0)), out_shape=_sds((t, 256), F32),
        compiler_params=_cparams(PAR),
    )(proj, proj, gam, bet, ws, bst)


def _c_bwd(proj, dycat, gam, bet, ws, wst, bst, *, tm, name):
    t = proj.shape[0]
    nch = tm // C_CHUNK
    nstep = t // tm

    def body(u_ref, v_ref, dy_ref, g_ref, b_ref, ws_ref, wst_ref, bs_ref,
             du_ref, dv_ref, dws_ref, dbs_ref, dg_ref, db_ref, dvn_s):
        step = pl.program_id(0)

        @pl.when(step == 0)
        def _():
            dws_ref[...] = jnp.zeros_like(dws_ref)
            dbs_ref[...] = jnp.zeros_like(dbs_ref)
            dg_ref[...] = jnp.zeros_like(dg_ref)
            db_ref[...] = jnp.zeros_like(db_ref)

        cv = v_ref[...]
        gam_v = g_ref[...]
        vn, xhat, r = _c_norm(cv, gam_v, b_ref[...])
        vnb = vn.astype(BF16)
        for c in range(nch):
            rows = slice(c * C_CHUNK, (c + 1) * C_CHUNK)
            for g in range(C_GROUPS):
                gs = slice(g * 64, (g + 1) * 64)
                cu = u_ref[rows, gs]
                dy = dy_ref[rows, gs]
                mixed = jnp.dot(ws_ref[g], vnb[rows, gs], preferred_element_type=F32) + bs_ref[:, gs]
                du_ref[rows, gs] = dy * mixed * _gelu_grad(cu)
                dmix = dy * _gelu(cu)
                dbs_ref[:, gs] += dmix
                dmb = dmix.astype(BF16)
                dws_ref[g] += lax.dot_general(dmb, vnb[rows, gs], NT, preferred_element_type=F32)
                dvn_s[rows, gs] = jnp.dot(wst_ref[g], dmb, preferred_element_type=F32)
        dvn = dvn_s[...]
        dg_ref[...] += jnp.sum(dvn * xhat, axis=0, keepdims=True)
        db_ref[...] += jnp.sum(dvn, axis=0, keepdims=True)
        dxh = dvn * gam_v
        dvg = r * (dxh - jnp.mean(dxh, axis=-1, keepdims=True) - xhat * jnp.mean(dxh * xhat, axis=-1, keepdims=True))
        dv_ref[...] = dvg * _gelu_grad(cv)

        @pl.when(step == nstep - 1)
        def _():
            dbs_ref[...] = _seg_sum(dbs_ref[...], _group_sum_matrix(256, True))

    vec = pl.BlockSpec((1, 256), lambda i: (0, 0))
    mat = pl.BlockSpec((C_GROUPS, C_CHUNK, C_CHUNK), lambda i: (0, 0, 0))
    bsp = pl.BlockSpec((C_CHUNK, 256), lambda i: (0, 0))
    tile = pl.BlockSpec((tm, 256), lambda i: (i, 0))
    return pl.pallas_call(
        body, name=name, grid=(nstep,),
        in_specs=[pl.BlockSpec((tm, 256), lambda i: (i, 5)), pl.BlockSpec((tm, 256), lambda i: (i, 6)),
                  pl.BlockSpec((tm, 256), lambda i: (i, 2)), vec, vec, mat, mat, bsp],
        out_specs=[tile, tile, mat, bsp, vec, vec],
        out_shape=[_sds((t, 256), F32), _sds((t, 256), F32), _sds((C_GROUPS, C_CHUNK, C_CHUNK), F32),
                   _sds((C_CHUNK, 256), F32), _sds((1, 256), F32), _sds((1, 256), F32)],
        scratch_shapes=[pltpu.VMEM((tm, 256), F32)],
        compiler_params=_cparams(ARB),
    )(proj, proj, dycat, gam, bet, ws, wst, bst)


FF_TC = 128
FF_NB = D_FF // FF_TC
FF_CH = 64
FF_HALO = 16


def _taps(ref, r0, win, where):
    z = jnp.zeros((FF_HALO, win.shape[1]), F32)
    if where == "first":
        win[0:FF_HALO, :] = z
        win[FF_HALO:, :] = ref[0:FF_CH + FF_HALO, :].astype(F32)
    elif where == "last":
        win[0:FF_CH + FF_HALO, :] = ref[SEQ - FF_CH - FF_HALO:SEQ, :].astype(F32)
        win[FF_CH + FF_HALO:, :] = z
    else:
        win[...] = ref[pl.ds(pl.multiple_of(r0 - FF_HALO, FF_HALO), FF_CH + 2 * FF_HALO), :].astype(F32)
    return tuple(win[FF_HALO + o:FF_HALO + o + FF_CH, :] for o in (-1, 0, 1))


def _chunk_loop(step):
    step(0, lambda ref, win: _taps(ref, 0, win, "first"))

    def mid(i, carry):
        r0 = pl.multiple_of(i * FF_CH, FF_CH)
        step(r0, lambda ref, win: _taps(ref, r0, win, "mid"))
        return carry

    lax.fori_loop(1, SEQ // FF_CH - 1, mid, 0)
    step(SEQ - FF_CH, lambda ref, win: _taps(ref, SEQ - FF_CH, win, "last"))


def _conv3(taps, w_ref, b_ref):
    dn, md, up = taps
    return w_ref[0:1, :] * dn + w_ref[1:2, :] * md + w_ref[2:3, :] * up + b_ref[...]


def _ff_specs(order):
    def at(fn):
        return (lambda b, j: fn(b, j)) if order == "bj" else (lambda j, b: fn(b, j))
    hs = [pl.BlockSpec((None, SEQ, FF_TC), at(lambda b, j, o=o: (b, 0, j + o))) for o in (0, FF_NB)]
    ws = [pl.BlockSpec((3, FF_TC), at(lambda b, j, o=o: (0, j + o))) for o in (0, FF_NB)]
    bs = [pl.BlockSpec((1, FF_TC), at(lambda b, j, o=o: (0, j + o))) for o in (0, FF_NB)]
    return hs, ws, bs


def _conv_gate_fwd(h, cw, cb, *, name):
    t = h.shape[0]
    bl = t // SEQ

    def body(hg_ref, hu_ref, wg_ref, wu_ref, bg_ref, bu_ref, a_ref, win):
        def step(r0, taps):
            cg = _conv3(taps(hg_ref, win.at[0]), wg_ref, bg_ref)
            cu = _conv3(taps(hu_ref, win.at[1]), wu_ref, bu_ref)
            a_ref[pl.ds(r0, FF_CH), :] = (cg * _sigmoid(cg) * cu).astype(BF16)

        _chunk_loop(step)

    hs, ws, bs = _ff_specs("bj")
    h3 = h.reshape(bl, SEQ, 2 * D_FF)
    act = pl.pallas_call(
        body, name=name, grid=(bl, FF_NB), in_specs=hs + ws + bs,
        out_specs=pl.BlockSpec((None, SEQ, FF_TC), lambda b, j: (b, 0, j)),
        out_shape=_sds((bl, SEQ, D_FF), BF16),
        scratch_shapes=[pltpu.VMEM((2, FF_CH + 2 * FF_HALO, FF_TC), F32)],
        compiler_params=_cparams(PAR, PAR),
    )(h3, h3, cw, cw, cb, cb)
    return act.reshape(t, D_FF)


def _conv_gate_bwd(h, dact, cw, cb, *, name):
    t = h.shape[0]
    bl = t // SEQ

    def body(hg_ref, hu_ref, wg_ref, wu_ref, bg_ref, bu_ref, da_ref,
             dhg_ref, dhu_ref, dwg_ref, dwu_ref, dbg_ref, dbu_ref, dg_s, du_s, win, sums):
        @pl.when(pl.program_id(1) == 0)
        def _():
            for ref in (dwg_ref, dwu_ref, dbg_ref, dbu_ref):
                ref[...] = jnp.zeros_like(ref)

        sums[...] = jnp.zeros_like(sums)
        red = lambda x: jnp.sum(x.reshape(FF_CH // 8, 8, x.shape[1]), axis=0)

        def pass1(r0, taps):
            tg, tu = taps(hg_ref, win.at[0]), taps(hu_ref, win.at[1])
            cg = _conv3(tg, wg_ref, bg_ref)
            cu = _conv3(tu, wu_ref, bu_ref)
            da = da_ref[pl.ds(r0, FF_CH), :].astype(F32)
            sg = _sigmoid(cg)
            dcg = da * cu * (sg * (1.0 + cg * (1.0 - sg)))
            dcu = da * (cg * sg)
            dg_s[pl.ds(r0, FF_CH), :] = dcg
            du_s[pl.ds(r0, FF_CH), :] = dcu
            for half, (d, tp) in enumerate(((dcg, tg), (dcu, tu))):
                for k in range(3):
                    sums[4 * half + k] += red(d * tp[k])
                sums[4 * half + 3] += red(d)

        _chunk_loop(pass1)
        for half, (dw_ref, db_ref) in enumerate(((dwg_ref, dbg_ref), (dwu_ref, dbu_ref))):
            for k in range(3):
                dw_ref[k:k + 1, :] += jnp.sum(sums[4 * half + k], axis=0, keepdims=True)
            db_ref[...] += jnp.sum(sums[4 * half + 3], axis=0, keepdims=True)

        def pass2(r0, taps):
            for k, (s, w_ref, o_ref) in enumerate(((dg_s, wg_ref, dhg_ref), (du_s, wu_ref, dhu_ref))):
                dn, md, up = taps(s, win.at[k])
                o_ref[pl.ds(r0, FF_CH), :] = (w_ref[0:1, :] * up + w_ref[1:2, :] * md + w_ref[2:3, :] * dn).astype(BF16)

        _chunk_loop(pass2)

    hs, ws, bs = _ff_specs("jb")
    half = pl.BlockSpec((None, SEQ, FF_TC), lambda j, b: (b, 0, j))
    wsp = pl.BlockSpec((3, FF_TC), lambda j, b: (0, j))
    bsp = pl.BlockSpec((1, FF_TC), lambda j, b: (0, j))
    h3 = h.reshape(bl, SEQ, 2 * D_FF)
    dhg, dhu, dwg, dwu, dbg, dbu = pl.pallas_call(
        body, name=name, grid=(FF_NB, bl), in_specs=hs + ws + bs + [half],
        out_specs=[half, half, wsp, wsp, bsp, bsp],
        out_shape=[_sds((bl, SEQ, D_FF), BF16), _sds((bl, SEQ, D_FF), BF16), _sds((3, D_FF), F32), _sds((3, D_FF), F32),
                   _sds((1, D_FF), F32), _sds((1, D_FF), F32)],
        scratch_shapes=[pltpu.VMEM((SEQ, FF_TC), F32), pltpu.VMEM((SEQ, FF_TC), F32),
                        pltpu.VMEM((2, FF_CH + 2 * FF_HALO, FF_TC), F32), pltpu.VMEM((8, 8, FF_TC), F32)],
        compiler_params=_cparams(PAR, ARB),
    )(h3, h3, cw, cw, cb, cb, dact.reshape(bl, SEQ, D_FF))
    return (dhg.reshape(t, D_FF), dhu.reshape(t, D_FF), jnp.concatenate([dwg, dwu], axis=1),
            jnp.concatenate([dbg, dbu], axis=1))


def _ple_fwd(x2, gain, wg, pe, pe_blk, wp, *, tm, tn, name):
    t, k = x2.shape
    n = wg.shape[1]

    def body(x_ref, g_ref, wg_ref, pe_ref, wp_ref, xr_ref, hn_ref, x3_ref, gt_ref, pp_ref, hn_s):
        @pl.when(pl.program_id(1) == 0)
        def _():
            x = x_ref[...]
            r = lax.rsqrt(jnp.mean(x * x, axis=-1, keepdims=True) + EPS)
            hn_s[...] = (x * r * g_ref[...]).astype(BF16)
            hn_ref[...] = hn_s[...]

        gate = _sigmoid(jnp.dot(hn_s[...], wg_ref[...], preferred_element_type=F32))
        pp = jnp.dot(pe_ref[...].astype(BF16), wp_ref[...], preferred_element_type=F32)
        gt_ref[...] = gate.astype(BF16)
        pp_ref[...] = pp.astype(BF16)
        x3_ref[...] = xr_ref[...] + pp * gate

    tile = pl.BlockSpec((tm, tn), lambda i, j: (i, j))
    return pl.pallas_call(
        body, name=name, grid=(t // tm, n // tn),
        in_specs=[pl.BlockSpec((tm, k), lambda i, j: (i, 0)), pl.BlockSpec((1, k), lambda i, j: (0, 0)),
                  pl.BlockSpec((k, tn), lambda i, j: (0, j)), pl.BlockSpec((tm, PLE_DIM), lambda i, j: (pe_blk + i, 0)),
                  pl.BlockSpec((PLE_DIM, tn), lambda i, j: (0, j)), tile],
        out_specs=[pl.BlockSpec((tm, k), lambda i, j: (i, 0)), tile, tile, tile],
        out_shape=[_sds((t, k), BF16), _sds((t, n), F32), _sds((t, n), BF16), _sds((t, n), BF16)],
        scratch_shapes=[pltpu.VMEM((tm, k), BF16)],
        compiler_params=_cparams(PAR, ARB),
    )(x2, gain, wg, pe, wp, x2)


def _ple_bwd_ew(dx3, gate, pp, *, tm, name):
    t, n = dx3.shape

    def body(d_ref, g_ref, p_ref, dz_ref, dpp_ref):
        d, g = d_ref[...], g_ref[...]
        dz_ref[...] = (d * p_ref[...] * g * (1.0 - g)).astype(BF16)
        dpp_ref[...] = (d * g).astype(BF16)

    spec = pl.BlockSpec((tm, n), lambda i: (i, 0))
    return pl.pallas_call(
        body, name=name, grid=(t // tm,), in_specs=[spec] * 3, out_specs=[spec] * 2,
        out_shape=[_sds((t, n), BF16)] * 2, compiler_params=_cparams(PAR),
    )(dx3, gate, pp)


def _loss_head(y, tgt, *, tm, name):
    t, d = y.shape

    def body(y_ref, t_ref, l_ref, dy_ref):
        @pl.when(pl.program_id(0) == 0)
        def _():
            l_ref[...] = jnp.zeros_like(l_ref)

        e = y_ref[...] - t_ref[...]
        dy_ref[...] = e * (1.0 / d)
        s = jnp.sum(jnp.sum(e * e, axis=1, keepdims=True), axis=0, keepdims=True)
        l_ref[...] += jnp.broadcast_to(s * (0.5 / d), (8, 128))

    spec = pl.BlockSpec((tm, d), lambda i: (i, 0))
    return pl.pallas_call(
        body, name=name, grid=(t // tm,), in_specs=[spec, spec],
        out_specs=[pl.BlockSpec((8, 128), lambda i: (0, 0)), spec],
        out_shape=[_sds((8, 128), F32), _sds((t, d), F32)], compiler_params=_cparams(ARB),
    )(y, tgt)


BIAS_PC = 8192


def _onehot(bucket_row):
    rows = lax.broadcasted_iota(jnp.int32, (REL_BUCKETS, bucket_row.shape[1]), 0)
    return (rows == bucket_row).astype(BF16)


def _dot3(x, onehot, dims):
    acc = None
    for _ in range(3):
        term = x.astype(BF16)
        part = lax.dot_general(term, onehot, dims, preferred_element_type=F32)
        acc = part if acc is None else acc + part
        x = x - term.astype(F32)
    return acc


def _bias_lookup(table_t, bucket, *, name):
    h = table_t.shape[0]
    p = bucket.shape[1]

    def body(t_ref, b_ref, o_ref):
        bk = b_ref[...]
        val = _dot3(t_ref[...], _onehot(bk), (((1,), (0,)), ((), ())))
        o_ref[...] = jnp.where(bk >= 0, val, NEG_INF)

    return pl.pallas_call(
        body, name=name, grid=(p // BIAS_PC,),
        in_specs=[pl.BlockSpec((h, REL_BUCKETS), lambda i: (0, 0)), pl.BlockSpec((1, BIAS_PC), lambda i: (0, i))],
        out_specs=pl.BlockSpec((h, BIAS_PC), lambda i: (0, i)), out_shape=_sds((h, p), F32),
        compiler_params=_cparams(PAR),
    )(table_t, bucket)


def _bucket_reduce(dbiases, bucket, *, name):
    h, p = dbiases[0].shape
    nl = len(dbiases)

    def body(*refs):
        b_ref, o_ref = refs[nl], refs[nl + 1]

        @pl.when(pl.program_id(0) == 0)
        def _():
            o_ref[...] = jnp.zeros_like(o_ref)

        d = refs[0][...]
        for d_ref in refs[1:nl]:
            d = d + d_ref[...]
        o_ref[...] += _dot3(d, _onehot(b_ref[...]), NT)

    return pl.pallas_call(
        body, name=name, grid=(p // BIAS_PC,),
        in_specs=[pl.BlockSpec((h, BIAS_PC), lambda i: (0, i))] * nl + [pl.BlockSpec((1, BIAS_PC), lambda i: (0, i))],
        out_specs=pl.BlockSpec((h, REL_BUCKETS), lambda i: (0, 0)), out_shape=_sds((h, REL_BUCKETS), F32),
        compiler_params=_cparams(ARB),
    )(*dbiases, bucket)


def _adamw_math(w, g, m, v):
    m = ADAM_B1 * m + (1.0 - ADAM_B1) * g
    v = ADAM_B2 * v + (1.0 - ADAM_B2) * (g * g)
    m_hat = m / (1.0 - ADAM_B1 ** ADAM_STEP)
    v_hat = v / (1.0 - ADAM_B2 ** ADAM_STEP)
    delta = -ADAM_LR * (m_hat / (jnp.sqrt(v_hat) + ADAM_EPS) + ADAM_WD * w)
    return delta, m, v


def _adamw_reduce(parts, w, m, v, *, tr, name):
    nl = len(parts)
    rows, c = w.shape
    r = rows // nl
    nt = r // tr

    def body(*refs):
        p_refs = refs[:nl]
        w_ref, m_ref, v_ref, g_ref, d_ref, nm_ref, nv_ref = refs[nl:]
        for li, p_ref in enumerate(p_refs):
            @pl.when(pl.program_id(0) == li)
            def _(p_ref=p_ref):
                g = p_ref[0].astype(F32)
                for k in range(1, N_DEV):
                    g = g + p_ref[k].astype(F32)
                d, nm, nv = _adamw_math(w_ref[...], g, m_ref[...], v_ref[...])
                g_ref[...] = g
                d_ref[...] = d
                nm_ref[...] = nm
                nv_ref[...] = nv

    def part_map(li):
        return lambda l, i: (0, jnp.where(l == li, i, jnp.where(l < li, 0, nt - 1)), 0)

    spec = pl.BlockSpec((tr, c), lambda l, i: (l * nt + i, 0))
    return pl.pallas_call(
        body, name=name, grid=(nl, nt),
        in_specs=[pl.BlockSpec((N_DEV, tr, c), part_map(li)) for li in range(nl)] + [spec, spec, spec],
        out_specs=[spec] * 4, out_shape=[_sds((rows, c), F32)] * 4, compiler_params=_cparams(ARB, ARB),
    )(*parts, w, m, v)


def _adamw_plain(g, w, m, v, *, name):
    def body(g_ref, w_ref, m_ref, v_ref, d_ref, nm_ref, nv_ref):
        d, nm, nv = _adamw_math(w_ref[...], g_ref[...], m_ref[...], v_ref[...])
        d_ref[...] = d
        nm_ref[...] = nm
        nv_ref[...] = nv

    return pl.pallas_call(body, name=name, out_shape=[_sds(w.shape, F32)] * 3)(g, w, m, v)


def _mesh_pos():
    return lax.axis_index("x"), lax.axis_index("y"), lax.axis_index("c")


def _allgather_body(x_refs, out_refs, send_sems, recv_sems, local_sems, slot):
    x, y, c = _mesh_pos()
    me, sibling = (x, y, c), (x, y, 1 - c)
    chips = [(1 - x, y), (x, 1 - y), (1 - x, 1 - y)]
    waits = []
    for a, (x_ref, out_ref) in enumerate(zip(x_refs, out_refs)):
        def copy(k, block, to, src=None, out_ref=out_ref, a=a):
            return pltpu.make_async_remote_copy(
                src_ref=slot(out_ref, block) if src is None else src, dst_ref=slot(out_ref, block),
                send_sem=send_sems.at[a, k], recv_sem=recv_sems.at[a, k], device_id=to, device_id_type=MESH)

        mine = pltpu.make_async_copy(x_ref, slot(out_ref, me), local_sems.at[a])
        mine.start()
        first = [copy(0, me, sibling, src=x_ref)]
        first += [copy(1 + j, me, (*chip, c), src=x_ref) for j, chip in enumerate(chips)]
        for cp in first:
            cp.start()
        waits.append((copy, mine, first))
    sends = []
    for copy, mine, first in waits:
        passed = [copy(4 + j, (*chip, c), sibling) for j, chip in enumerate(chips)]
        for j, chip in enumerate(chips):
            copy(1 + j, (*chip, c), me).wait_recv()
            passed[j].start()
        sends.append(passed)
    for (copy, mine, first), passed in zip(waits, sends):
        copy(0, sibling, me).wait_recv()
        for j, chip in enumerate(chips):
            copy(4 + j, (*chip, 1 - c), me).wait_recv()
        for cp in first + passed:
            cp.wait_send()
        mine.wait()


PEER_FLIPS = ((0, 0, 1), (1, 0, 0), (0, 1, 0), (1, 1, 0), (1, 0, 1), (0, 1, 1), (1, 1, 1))


def _peer_copies(x_refs, land_refs, send_sem, recv_sem, scatter):
    x, y, c = _mesh_pos()
    me = 4 * x + 2 * y + c
    copies = []
    for x_ref, land_ref in zip(x_refs, land_refs):
        for fx, fy, fc in PEER_FLIPS:
            px, py, pc = x ^ fx, y ^ fy, c ^ fc
            src = x_ref.at[4 * px + 2 * py + pc] if scatter else x_ref
            copies.append(pltpu.make_async_remote_copy(
                src_ref=src, dst_ref=land_ref.at[me], send_sem=send_sem, recv_sem=recv_sem,
                device_id=(px, py, pc), device_id_type=MESH))
    return copies


def _sc_exchange(xs, *, scatter, collective_id, name):
    na = len(xs)
    land_shapes = [x.shape if scatter else (N_DEV,) + x.shape for x in xs]

    def body(*refs):
        x_refs, land_refs = refs[:na], refs[na:2 * na]
        send_sem, recv_sem, local_sem = refs[2 * na:]
        x, y, c = _mesh_pos()
        me = 4 * x + 2 * y + c
        barrier = pltpu.get_barrier_semaphore()
        for fx, fy, fc in PEER_FLIPS:
            pl.semaphore_signal(barrier, inc=1, device_id=(x ^ fx, y ^ fy, c ^ fc), device_id_type=MESH)
        pl.semaphore_wait(barrier, len(PEER_FLIPS))
        for x_ref, land_ref in zip(x_refs, land_refs):
            own = pltpu.make_async_copy(x_ref.at[me] if scatter else x_ref, land_ref.at[me], local_sem)
            own.start()
            own.wait()
        copies = _peer_copies(x_refs, land_refs, send_sem, recv_sem, scatter)
        for cp in copies:
            cp.start()
        for cp in copies:
            cp.wait()

    return pl.kernel(
        body, name=name, out_type=[_sds(s, x.dtype) for s, x in zip(land_shapes, xs)],
        mesh=plsc.ScalarSubcoreMesh(axis_name="sequencer", num_cores=1),
        scratch_types=[pltpu.SemaphoreType.DMA, pltpu.SemaphoreType.DMA, pltpu.SemaphoreType.DMA],
        compiler_params=pltpu.CompilerParams(collective_id=collective_id),
    )(*xs)


def _sc_allgather(xs, *, collective_id, name):
    na = len(xs)

    def body(*refs):
        x_refs, out_refs = refs[:na], refs[na:2 * na]
        send_sems, recv_sems, local_sems = refs[2 * na:]
        x, y, c = _mesh_pos()
        barrier = pltpu.get_barrier_semaphore()
        for fx, fy, fc in PEER_FLIPS:
            pl.semaphore_signal(barrier, inc=1, device_id=(x ^ fx, y ^ fy, c ^ fc), device_id_type=MESH)
        pl.semaphore_wait(barrier, len(PEER_FLIPS))
        _allgather_body(x_refs, out_refs, send_sems, recv_sems, local_sems,
                        lambda ref, pos: ref.at[4 * pos[0] + 2 * pos[1] + pos[2]])

    return pl.kernel(
        body, name=name, out_type=[_sds((N_DEV,) + x.shape, x.dtype) for x in xs],
        mesh=plsc.ScalarSubcoreMesh(axis_name="sequencer", num_cores=1),
        scratch_types=[pltpu.SemaphoreType.DMA((na, 7)), pltpu.SemaphoreType.DMA((na, 7)),
                       pltpu.SemaphoreType.DMA((na,))],
        compiler_params=pltpu.CompilerParams(collective_id=collective_id),
    )(*xs)


def _allgather_vmem(x, *, reduce, name):
    r, c = x.shape

    def body(x_ref, out_ref, *rest):
        if reduce:
            gath, send_sems, recv_sems, local_sems = rest
        else:
            send_sems, recv_sems, local_sems = rest
            gath = out_ref
        _allgather_body([x_ref], [gath], send_sems, recv_sems, local_sems,
                        lambda ref, pos: ref.at[pl.ds((4 * pos[0] + 2 * pos[1] + pos[2]) * r, r), :])
        if reduce:
            acc = gath[0:r, :]
            for k in range(1, N_DEV):
                acc = acc + gath[k * r:(k + 1) * r, :]
            out_ref[...] = acc

    vm = pl.BlockSpec(memory_space=pltpu.VMEM)
    scratch = [pltpu.SemaphoreType.DMA((1, 7)), pltpu.SemaphoreType.DMA((1, 7)), pltpu.SemaphoreType.DMA((1,))]
    if reduce:
        scratch = [pltpu.VMEM((N_DEV * r, c), x.dtype)] + scratch
    return pl.pallas_call(
        body, name=name, in_specs=[vm], out_specs=vm,
        out_shape=_sds((r, c) if reduce else (N_DEV * r, c), x.dtype), scratch_shapes=scratch,
    )(x)


def _t5_bucket(rel):
    nb = REL_BUCKETS // 2
    ret = jnp.where(rel > 0, nb, 0)
    n = jnp.abs(rel)
    max_exact = nb // 2
    nf = jnp.maximum(n, 1).astype(F32)
    large = max_exact + (jnp.log(nf / max_exact) / math.log(REL_MAX_DIST / max_exact)
                         * (nb - max_exact)).astype(jnp.int32)
    large = jnp.minimum(large, nb - 1)
    return ret + jnp.where(n < max_exact, n, large)


def _band_pattern(block, radius, dil):
    kw = block + 2 * radius
    rel = jnp.arange(kw)[None, :] - radius - jnp.arange(block)[:, None]
    return jnp.where(jnp.abs(rel) <= radius, _t5_bucket(rel * dil), -1).astype(jnp.int32).reshape(1, block * kw)


def _rope_tables():
    lane = np.arange(64)
    seg, j = lane // 32, lane % 32
    inv = ROPE_THETA ** (-jnp.arange(0, 32, 2, dtype=F32) / 32)
    tpos = jnp.arange(SEQ)
    pos = jnp.where(jnp.asarray(seg)[None, :] == 0, (tpos // GRID_W)[:, None], (tpos % GRID_W)[:, None])
    ang = pos.astype(F32) * inv[jnp.asarray(j % 16)][None, :]
    cos = jnp.cos(ang)
    sins = jnp.where(jnp.asarray(j)[None, :] < 16, -jnp.sin(ang), jnp.sin(ang))
    return jnp.tile(cos, (1, 4)), jnp.tile(sins, (1, 4))


A_Q, A_K, A_V = (256, 0), (256, 1), (256, 2)
B_Q, B_K, B_V = (256, 0), (128, 2), (128, 3)
A_HEADS = dict(rad=A_RADIUS, nh=4, nkv=4)
B_HEADS = dict(rad=SWA_RADIUS, nh=4, nkv=2)


def _pin(arr, token):
    return arr if token is None else arr + token[0:1, 0:1]


def _local_step(x, pe, tgt, rel_bias, wts, matmul_weights, grads_ready):
    t = x.shape[0]
    bl = t // SEQ
    cos, sins = _rope_tables()
    blocks_a = [min(BAND_BLOCK, SEQ // d) for d in DILATIONS]
    pats_a = [_band_pattern(blk, A_RADIUS, d) for blk, d in zip(blocks_a, DILATIONS)]
    pat_b = _band_pattern(BAND_BLOCK, SWA_RADIUS, 1)
    table_t = rel_bias.T
    bias_a = [_bias_lookup(table_t[:4], pt, name=f"bias_a{ci}").reshape(4, blk, blk + 2 * A_RADIUS)
              for ci, (pt, blk) in enumerate(zip(pats_a, blocks_a))]
    bias_b = _bias_lookup(table_t[4:], pat_b, name="bias_b").reshape(4, BAND_BLOCK, BAND_BLOCK + 2 * SWA_RADIUS)
    nat4 = lambda a: a.reshape(bl, 1, SEQ, a.shape[-1])

    saved = []
    for li in range(DEPTH):
        w = dict(wts[li])
        w.update(matmul_weights(li, "in", x)[0])
        hn0, proj = _norm_mm((x,), w["g_mix"], w["w_in"], None, tm=1024, tn=1152, name="mix_in_fwd")
        qa1, qa4, qa16, qb, qd = _qkprep_fwd(proj, w["qk_gains"], cos, sins, tm=512, name="qkprep_fwd")
        qa = (nat4(qa1), qa4, qa16)
        oa, la = [], []
        for ci in range(3):
            o, l = _band_fwd(qa[ci], A_Q, A_K, A_V, bias_a[ci], None, name=f"band_a{ci}_fwd", **A_HEADS)
            oa.append(o)
            la.append(l)
        oa[0], la[0] = oa[0].reshape(t, 256), la[0].reshape(t, 256)
        ya, lse_a = _combine_a(oa, la, tm=512, name="combine_a")
        yb, lse_b = _band_fwd(nat4(qb), B_Q, B_K, B_V, bias_b, w["sink_t"], name="band_b_fwd", **B_HEADS)
        yb = yb.reshape(t, 256)
        yc = _c_fwd(proj, w["c_g"], w["c_b"], w["c_ws"], w["c_bst"], tm=512, name="c_fwd")
        yd, lse_d = _dense_fwd(qd, tq=256, name="dense_fwd")
        more, started = matmul_weights(li, "rest", yd)
        w.update(more)
        w["out_gain"] = _pin(w["out_gain"], started)
        mixed, x1 = _norm_mm((ya, yb, yc, yd), w["out_gain"], w["w_out"], x, tm=1024, tn=1024, name="mix_out_fwd")
        hn1, h = _norm_mm((x1,), w["g_ffn"], w["w_up"], None, tm=1024, tn=1408, name="ffn_up_fwd", out_dtype=BF16)
        act = _conv_gate_fwd(h, w["conv_w"], w["conv_b"], name="conv_gate_fwd")
        x2 = _mm(act, w["w_down"], "nn", x1, tm=1024, tn=1024, out_dtype=F32, name="ffn_down_fwd")
        hn2, x3, gate, pp = _ple_fwd(x2, w["g_ple"], w["w_gate"], pe, li * (t // 1024), w["w_proj"], tm=1024, tn=512,
                                     name="ple_fwd")
        saved.append(dict(w=w, x0=x, hn0=hn0, proj=proj, qa=qa, qb=qb, qd=qd, ya=ya, lse_a=lse_a, yb=yb, lse_b=lse_b,
                          yc=yc, yd=yd, lse_d=lse_d, mixed=mixed, x1=x1, hn1=hn1, h=h, act=act, x2=x2, hn2=hn2,
                          gate=gate, pp=pp))
        x = x3

    loss_tile, dx = _loss_head(x, tgt, tm=512, name="loss_head")
    grads = [None] * DEPTH
    dbias_a, dbias_bs = [[], [], []], []
    token = None
    for li in reversed(range(DEPTH)):
        s = saved[li]
        w = s["w"]
        g = {}
        w["g_ple"] = _pin(w["g_ple"], token)
        dz, dpp = _ple_bwd_ew(dx, s["gate"], s["pp"], tm=512, name="ple_bwd_ew")
        g["w_gate"] = _mm(s["hn2"], dz, "tn", None, tm=1024, tn=512, out_dtype=BF16, name="dw_gate")
        g["w_proj"] = _mm(pe, dpp, "tn", None, tm=256, tn=1024, out_dtype=BF16, name="dw_proj", a_rows=(li, t))
        dx2, dx2b, g["g_ple"] = _mm_bt_normbwd((dz,), w["w_gate"], (s["x2"],), w["g_ple"], dx, tm=1024, tn=1024,
                                               name="ple_bwd", emit_bf16=True)
        g["w_down"] = _mm(s["act"], dx2b, "tn", None, tm=1408, tn=512, out_dtype=BF16, name="dw_down")
        dact = _mm(dx2b, w["w_down"], "nt", None, tm=1024, tn=1408, out_dtype=BF16, name="ffn_down_bwd")
        dhg, dhu, g["conv_w"], g["conv_b"] = _conv_gate_bwd(s["h"], dact, w["conv_w"], w["conv_b"], name="conv_gate_bwd")
        g["w_up"] = jnp.concatenate(
            [_mm(s["hn1"], dhalf, "tn", None, tm=1024, tn=1408, out_dtype=BF16, name=f"dw_up_{nm}")
             for nm, dhalf in (("gate", dhg), ("up", dhu))], axis=1)
        dx1, dx1b, g["g_ffn"] = _mm_bt_normbwd((dhg, dhu), w["w_up"], (s["x1"],), w["g_ffn"], dx2, tm=1024, tn=1408,
                                               name="ffn_up_bwd", emit_bf16=True)
        g["w_out"] = _mm(s["mixed"], dx1b, "tn", None, tm=1024, tn=512, out_dtype=BF16, name="dw_out")
        out_gain = _pin(w["out_gain"], grads_ready(li, "mid", g))
        dycat, g["out_gain"] = _mm_bt_normbwd((dx1b,), w["w_out"], (s["ya"], s["yb"], s["yc"], s["yd"]), out_gain,
                                              None, tm=1024, tn=1024, name="mix_out_bwd")
        dy_r, lse_r, dl_a, dl_b, dl_d = _deltas(dycat, s["ya"], s["yb"], s["yd"], s["lse_a"], tm=512, name="deltas")
        dy_a = (nat4(dycat),) + tuple(dy_r)
        lse_a = (nat4(s["lse_a"]),) + tuple(lse_r)
        dl_a = (nat4(dl_a[0]),) + tuple(dl_a[1:])
        da = []
        for ci in range(3):
            dq, dk, dv, dbias = _band_bwd(s["qa"][ci], A_Q, A_K, A_V, bias_a[ci], None, dy_a[ci], 0, lse_a[ci],
                                          dl_a[ci], name=f"band_a{ci}_bwd", **A_HEADS)
            if ci == 0:
                dq, dk, dv = (a.reshape(t, 256) for a in (dq, dk, dv))
            da.append((dq, dk, dv))
            dbias_a[ci].append(dbias.reshape(4, -1))
        dqb, dkb, dvb, dbias_b, dsink = _band_bwd(nat4(s["qb"]), B_Q, B_K, B_V, bias_b, w["sink_t"], nat4(dycat), 1,
                                                  nat4(s["lse_b"]), nat4(dl_b), name="band_b_bwd", **B_HEADS)
        dbias_bs.append(dbias_b.reshape(4, -1))
        g["sink"] = dsink[:, 0, 0]
        dd = _dense_bwd(s["qd"], dycat, s["lse_d"], dl_d, tq=256, name="dense_bwd")
        dcu, dcv, g["c_ws"], dbs, g["c_g"], g["c_b"] = _c_bwd(s["proj"], dycat, w["c_g"], w["c_b"], w["c_ws"],
                                                               w["c_wst"], w["c_bst"], tm=512, name="c_bwd")
        g["c_bs"] = dbs[:, ::64].T
        db = (dqb.reshape(t, 256), dkb.reshape(t, 128), dvb.reshape(t, 128))
        dproj, dgains = _qkprep_bwd(s["proj"], da, db, dd, dcu, dcv, w["qk_gains"], cos, sins, tm=512, name="qkprep_bwd")
        g["qk_gain"] = dgains[:6, :64].reshape(3, 2, HEAD_DIM)
        g["w_in"] = _mm(s["hn0"], dproj, "tn", None, tm=1024, tn=1152, out_dtype=BF16, name="dw_in")
        dx, g["g_mix"] = _mm_bt_normbwd((dproj,), w["w_in"], (s["x0"],), w["g_mix"], dx1, tm=1024, tn=1152,
                                        name="mix_in_bwd")
        grads[li] = g
        token = grads_ready(li, "end", g)
    d_table_a = sum(_bucket_reduce(dbias_a[ci], pats_a[ci], name=f"bucket_a{ci}") for ci in range(3))
    d_table_b = _bucket_reduce(dbias_bs, pat_b, name="bucket_b")
    d_rel_bias = jnp.concatenate([d_table_a, d_table_b], axis=0).T
    return loss_tile[0, 0], dx, grads, d_rel_bias


WEIGHT_NAMES = ("rel_bias", "ln_mix_g", "w_in", "qk_gain", "sink", "c_norm_g", "c_norm_b", "c_ws", "c_bs", "out_gain",
                "w_out", "ln_ffn_g", "w_up", "conv_w", "conv_b", "w_down", "ln_ple_g", "w_ple_gate", "w_ple_proj")
COL_SHARDED = ("w_in", "w_up", "w_ple_proj")
ROW_SHARDED = ("w_out", "w_down", "w_ple_gate")
SMALL_SHARDED = ("conv_w", "out_gain")
REPLICATED = tuple(n for n in WEIGHT_NAMES if n not in COL_SHARDED + ROW_SHARDED + SMALL_SHARDED)
LOCAL_GRAD_KEY = {"ln_mix_g": "g_mix", "ln_ffn_g": "g_ffn", "ln_ple_g": "g_ple", "c_norm_g": "c_g", "c_norm_b": "c_b",
                  "w_ple_gate": "w_gate", "w_ple_proj": "w_proj"}


def _full_from_gathered(name, gathered):
    _, r, c = gathered.shape
    if name in ROW_SHARDED:
        return gathered.reshape(N_DEV * r, c)
    return jnp.transpose(gathered, (1, 0, 2)).reshape(r, N_DEV * c)


def _slots_from_full(name, full):
    rows, cols = full.shape
    if name in ROW_SHARDED:
        return full.reshape(N_DEV, rows // N_DEV, cols)
    return jnp.transpose(full.reshape(rows, N_DEV, cols // N_DEV), (1, 0, 2))


def _piece_rows(shape):
    return -(-int(np.prod(shape)) // 1024) * 8


def _pack_rows(arrays):
    pieces = []
    for a in arrays:
        n, rows = int(np.prod(a.shape)), _piece_rows(a.shape)
        flat = a.astype(F32).reshape(-1)
        if n != rows * LANES:
            flat = jnp.pad(flat, (0, rows * LANES - n))
        pieces.append(flat.reshape(rows, LANES))
    return jnp.concatenate(pieces, axis=0)


def _unpack_rows(packed, shapes):
    out, off = [], 0
    for shp in shapes:
        n, rows = int(np.prod(shp)), _piece_rows(shp)
        piece = packed[off:off + rows]
        out.append((piece if n == rows * LANES else piece.reshape(-1)[:n]).reshape(shp))
        off += rows
    return out


def kernel(x, p, rel_bias, ln_mix_g, w_in, qk_gain, sink, c_norm_g, c_norm_b, c_ws, c_bs, out_gain, w_out, ln_ffn_g, w_up, conv_w, conv_b, w_down, ln_ple_g, w_ple_gate, w_ple_proj, loss_target, m_rel_bias, m_ln_mix_g, m_w_in, m_qk_gain, m_sink, m_c_norm_g, m_c_norm_b, m_c_ws, m_c_bs, m_out_gain, m_w_out, m_ln_ffn_g, m_w_up, m_conv_w, m_conv_b, m_w_down, m_ln_ple_g, m_w_ple_gate, m_w_ple_proj, v_rel_bias, v_ln_mix_g, v_w_in, v_qk_gain, v_sink, v_c_norm_g, v_c_norm_b, v_c_ws, v_c_bs, v_out_gain, v_w_out, v_ln_ffn_g, v_w_up, v_conv_w, v_conv_b, v_w_down, v_ln_ple_g, v_w_ple_gate, v_w_ple_proj):
    env = dict(locals())
    wt = {n: env[n] for n in WEIGHT_NAMES}
    mom_m = {n: env["m_" + n] for n in WEIGHT_NAMES}
    mom_v = {n: env["v_" + n] for n in WEIGHT_NAMES}
    bl = x.shape[0]
    t = bl * SEQ
    me = 4 * lax.axis_index("x") + 2 * lax.axis_index("y") + lax.axis_index("c")

    big = COL_SHARDED + ROW_SHARDED
    full = {}
    small_shapes = [wt[n].shape for n in SMALL_SHARDED]
    small = _allgather_vmem(_pack_rows([wt[n] for n in SMALL_SHARDED]), reduce=False, name="gather_small")
    small = small.reshape(N_DEV, -1)
    off = 0
    for n, shp in zip(SMALL_SHARDED, small_shapes):
        cnt = int(np.prod(shp))
        g = small[:, off:off + cnt].reshape((N_DEV,) + tuple(shp))
        full[n] = jnp.transpose(g, (1, 2, 0, 3)).reshape(shp[0], shp[1], N_DEV * shp[2])
        off += _piece_rows(shp) * LANES

    def head_gain(li, a, b, reps):
        g = jnp.tile(qk_gain[li, a, b], reps)
        return jnp.pad(g, (0, 256 - g.shape[0]))

    wts = []
    for li in range(DEPTH):
        rows = [head_gain(li, 0, 0, 4), head_gain(li, 0, 1, 4), head_gain(li, 1, 0, 4), head_gain(li, 1, 1, 2),
                head_gain(li, 2, 0, 4), head_gain(li, 2, 1, 2), jnp.zeros((256,), F32), jnp.zeros((256,), F32)]
        wts.append(dict(
            g_mix=ln_mix_g[li].reshape(1, -1), qk_gains=jnp.stack(rows),
            sink_t=jnp.broadcast_to(sink[li][:, None, None], (4, 8, 128)),
            c_g=c_norm_g[li].reshape(1, -1), c_b=c_norm_b[li].reshape(1, -1), c_ws=c_ws[li].astype(BF16),
            c_wst=jnp.transpose(c_ws[li], (0, 2, 1)).astype(BF16), c_bst=jnp.repeat(c_bs[li].T, 64, axis=1),
            out_gain=full["out_gain"][li].reshape(1, -1), g_ffn=ln_ffn_g[li].reshape(1, -1),
            conv_w=full["conv_w"][li], conv_b=conv_b[li].reshape(1, -1), g_ple=ln_ple_g[li].reshape(1, -1)))

    local_key = {"w_ple_gate": "w_gate", "w_ple_proj": "w_proj"}

    gather_names = {"in": ("w_in",), "rest": tuple(n for n in big if n != "w_in")}
    gathered = {}
    for cid, (li, names) in enumerate(((0, gather_names["in"]), (0, gather_names["rest"]), (1, big))):
        lands = _sc_allgather([wt[n][li].astype(BF16) for n in names], collective_id=cid,
                              name=f"gather_{li}_{len(names)}")
        gathered.setdefault(li, {}).update(zip(names, lands))

    def matmul_weights(li, part, after):
        out = {}
        for n in gather_names[part]:
            g, _ = lax.optimization_barrier((gathered[li][n], after))
            out[local_key.get(n, n)] = _full_from_gathered(n, g)
        return out, None

    mid_names = ("w_ple_gate", "w_ple_proj", "w_down", "w_up", "w_out")
    end_names = ("w_in",)
    landed = {}

    def start_exchange(li, names, g, tag, cid):
        slots = [_slots_from_full(n, g[local_key.get(n, n)]) for n in names]
        lands = _sc_exchange(slots, scatter=True, collective_id=cid, name=f"grads_{li}_{tag}")
        landed.update({(n, li): land for n, land in zip(names, lands)})

    def grads_ready(li, stage, g):
        if li == 0:
            start_exchange(li, mid_names if stage == "mid" else end_names, g, stage, 5 if stage == "mid" else 6)
        elif stage == "end":
            start_exchange(li, mid_names + end_names, g, stage, 4)
        return None

    loss_part, dx, grads, d_rel_bias = _local_step(
        x.reshape(t, D_MODEL), p.reshape(DEPTH * t, PLE_DIM), loss_target.reshape(t, D_MODEL), rel_bias, wts,
        matmul_weights, grads_ready)
    loss = lax.psum(loss_part, ("x", "y", "c"))

    def local_grad(n):
        if n == "rel_bias":
            return d_rel_bias
        key = LOCAL_GRAD_KEY.get(n, n)
        return jnp.stack([grads[li][key].reshape(wt[n].shape[1:]) if n in REPLICATED else grads[li][key]
                          for li in range(DEPTH)])

    out_g, out_d, out_m, out_v = {}, {}, {}, {}
    for n in big:
        shp = wt[n].shape
        two_d = lambda a: a.reshape(-1, shp[-1])
        res = _adamw_reduce([landed[n, li] for li in range(DEPTH)], two_d(wt[n]), two_d(mom_m[n]), two_d(mom_v[n]),
                            tr=32 if n == "w_down" else 128, name="adamw_" + n)
        out_g[n], out_d[n], out_m[n], out_v[n] = [r.reshape(shp) for r in res]

    small_names = REPLICATED + SMALL_SHARDED
    small_full_shapes = [wt[n].shape if n in REPLICATED else full[n].shape for n in small_names]
    reduced = _allgather_vmem(_pack_rows([local_grad(n) for n in small_names]), reduce=True, name="allreduce_small")
    reduced = dict(zip(small_names, _unpack_rows(reduced, small_full_shapes)))
    rep_shapes = [wt[n].shape for n in REPLICATED]
    upd = _adamw_plain(_pack_rows([reduced[n] for n in REPLICATED]), _pack_rows([wt[n] for n in REPLICATED]),
                       _pack_rows([mom_m[n] for n in REPLICATED]), _pack_rows([mom_v[n] for n in REPLICATED]),
                       name="adamw_replicated")
    for dst, packed in zip((out_d, out_m, out_v), upd):
        dst.update(zip(REPLICATED, _unpack_rows(packed, rep_shapes)))
    for n in REPLICATED:
        out_g[n] = reduced[n]
    for n in SMALL_SHARDED:
        shp = wt[n].shape
        g = reduced[n].reshape(shp[0], shp[1], N_DEV, shp[2])
        g = lax.dynamic_index_in_dim(g, me, axis=2, keepdims=False)
        two_d = lambda a: a.reshape(-1, shp[-1])
        res = _adamw_plain(two_d(g), two_d(wt[n]), two_d(mom_m[n]), two_d(mom_v[n]), name="adamw_" + n)
        out_g[n] = g
        out_d[n], out_m[n], out_v[n] = [r.reshape(shp) for r in res]

    return (loss, dx.reshape(bl, SEQ, D_MODEL), *[out_g[n] for n in WEIGHT_NAMES], *[out_d[n] for n in WEIGHT_NAMES],
            *[out_m[n] for n in WEIGHT_NAMES], *[out_v[n] for n in WEIGHT_NAMES])
```

```python
import math

import jax
import jax.numpy as jnp
import numpy as np
from jax import lax
from jax.experimental import pallas as pl
from jax.experimental.pallas import tpu as pltpu
from jax.experimental.pallas import tpu_sc as plsc

F32 = jnp.float32
BF16 = jnp.bfloat16
HI = lax.Precision.HIGHEST

N_DEV = 8
D_MODEL = 1024
SEQ = 2048
DEPTH = 2
HEAD_DIM = 64
IN_WIDTH = 2304
D_FF = 2816
PLE_DIM = 256
C_CHUNK = 128
C_GROUPS = 4
DILATED_CFGS = ((128, 1), (512, 4), (2048, 16))
DILATIONS = tuple(d for _, d in DILATED_CFGS)
A_RADIUS = 64
SWA_RADIUS = 128
BAND_BLOCK = 256
GRID_W = 64
ROPE_THETA = 10000.0
REL_BUCKETS = 32
REL_MAX_DIST = 1024
EPS = 1e-6
NEG_INF = -1e30
ATTN_SCALE = HEAD_DIM ** -0.5
LANES = 128

ADAM_LR = 0.001
ADAM_B1 = 0.9
ADAM_B2 = 0.999
ADAM_EPS = 1e-08
ADAM_WD = 0.01
ADAM_STEP = 10

MESH = pl.DeviceIdType.MESH
NT = (((1,), (1,)), ((), ()))
TN = (((0,), (0,)), ((), ()))
ARB = "arbitrary"
PAR = "parallel"


def _cparams(*sem):
    return pltpu.CompilerParams(dimension_semantics=tuple(sem))


def _sds(shape, dtype):
    return jax.ShapeDtypeStruct(tuple(shape), dtype)


def _group_sum_matrix(n, same_group):
    r = lax.broadcasted_iota(jnp.int32, (n, n), 0)
    c = lax.broadcasted_iota(jnp.int32, (n, n), 1)
    if same_group:
        return ((r >> 6) == (c >> 6)).astype(F32)
    return ((r & 63) == (c & 63)).astype(F32)


def _seg_sum(x, e):
    eb = e.astype(BF16)
    hi = x.astype(BF16)
    lo = (x - hi.astype(F32)).astype(BF16)
    return jnp.dot(hi, eb, preferred_element_type=F32) + jnp.dot(lo, eb, preferred_element_type=F32)


def _gelu(x):
    c = math.sqrt(2.0 / math.pi)
    return 0.5 * x * (1.0 + jnp.tanh(c * (x + 0.044715 * (x * x * x))))


def _gelu_grad(x):
    c = math.sqrt(2.0 / math.pi)
    t = jnp.tanh(c * (x + 0.044715 * (x * x * x)))
    return 0.5 * (1.0 + t) + 0.5 * x * (1.0 - t * t) * c * (1.0 + 3.0 * 0.044715 * (x * x))


def _sigmoid(x):
    return 1.0 / (1.0 + jnp.exp(-x))


def _scatter_cols(scratch, first, val):
    for c in range(val.shape[1] // LANES):
        scratch[first + c] = val[:, c * LANES:(c + 1) * LANES]


def _gather_cols(scratch, first, ncol):
    return jnp.concatenate([scratch[first + c] for c in range(ncol)], axis=1)


def _read_residue(scratch, first, ncol, r, d):
    n = scratch.shape[1] // d
    return jnp.concatenate([scratch.at[first + c][pl.ds(r, n, stride=d), :] for c in range(ncol)], axis=1)


def _write_residue(scratch, first, r, d, val):
    n = scratch.shape[1] // d
    for c in range(val.shape[1] // LANES):
        scratch.at[first + c][pl.ds(r, n, stride=d), :] = val[:, c * LANES:(c + 1) * LANES]


def _norm_mm(xs, gain, w, res, *, tm, tn, name, out_dtype=F32):
    t = xs[0].shape[0]
    k = sum(x.shape[1] for x in xs)
    n = w.shape[1]
    ng = len(xs)
    has_res = res is not None

    def body(*refs):
        x_refs = refs[:ng]
        g_ref, w_ref = refs[ng], refs[ng + 1]
        res_ref = refs[ng + 2] if has_res else None
        hn_ref, o_ref, hn_s = refs[ng + 2 + has_res:]

        @pl.when(pl.program_id(1) == 0)
        def _():
            off = 0
            for xr in x_refs:
                x = xr[...]
                wd = x.shape[1]
                r = lax.rsqrt(jnp.mean(x * x, axis=-1, keepdims=True) + EPS)
                hn_s[:, off:off + wd] = (x * r * g_ref[:, off:off + wd]).astype(BF16)
                off += wd
            hn_ref[...] = hn_s[...]

        acc = jnp.dot(hn_s[...], w_ref[...], preferred_element_type=F32)
        if has_res:
            acc = acc + res_ref[...]
        o_ref[...] = acc.astype(out_dtype)

    in_specs = [pl.BlockSpec((tm, x.shape[1]), lambda i, j: (i, 0)) for x in xs]
    in_specs += [pl.BlockSpec((1, k), lambda i, j: (0, 0)), pl.BlockSpec((k, tn), lambda i, j: (0, j))]
    args = list(xs) + [gain, w]
    if has_res:
        in_specs.append(pl.BlockSpec((tm, tn), lambda i, j: (i, j)))
        args.append(res)
    return pl.pallas_call(
        body, name=name, grid=(t // tm, n // tn), in_specs=in_specs,
        out_specs=[pl.BlockSpec((tm, k), lambda i, j: (i, 0)), pl.BlockSpec((tm, tn), lambda i, j: (i, j))],
        out_shape=[_sds((t, k), BF16), _sds((t, n), out_dtype)],
        scratch_shapes=[pltpu.VMEM((tm, k), BF16)],
        compiler_params=_cparams(PAR, ARB),
    )(*args)


def _mm(a, b, mode, res, *, tm, tn, out_dtype, name, a_rows=None):
    if mode == "tn":
        kk, m = a.shape
        blk_a = 0
        if a_rows is not None:
            blk_a, kk = a_rows
        a_spec = pl.BlockSpec((kk, tm), lambda i, j: (blk_a, i))
    else:
        m, kk = a.shape
        a_spec = pl.BlockSpec((tm, kk), lambda i, j: (i, 0))
    if mode == "nt":
        n = b.shape[0]
        b_spec = pl.BlockSpec((tn, kk), lambda i, j: (j, 0))
    else:
        n = b.shape[1]
        b_spec = pl.BlockSpec((kk, tn), lambda i, j: (0, j))
    has_res = res is not None

    def body(*refs):
        a_ref, b_ref = refs[0], refs[1]
        o_ref = refs[-1]
        av = a_ref[...].astype(BF16)
        bv = b_ref[...].astype(BF16)
        if mode == "nn":
            acc = jnp.dot(av, bv, preferred_element_type=F32)
        elif mode == "nt":
            acc = lax.dot_general(av, bv, NT, preferred_element_type=F32)
        else:
            acc = lax.dot_general(av, bv, TN, preferred_element_type=F32)
        if has_res:
            acc = acc + refs[2][...]
        o_ref[...] = acc.astype(out_dtype)

    in_specs = [a_spec, b_spec]
    args = [a, b]
    if has_res:
        in_specs.append(pl.BlockSpec((tm, tn), lambda i, j: (i, j)))
        args.append(res)
    return pl.pallas_call(
        body, name=name, grid=(m // tm, n // tn), in_specs=in_specs,
        out_specs=pl.BlockSpec((tm, tn), lambda i, j: (i, j)),
        out_shape=_sds((m, n), out_dtype),
        compiler_params=_cparams(PAR, PAR),
    )(*args)


def _mm_bt_normbwd(dys, w, xs, gain, dres, *, tm, tn, name, emit_bf16=False):
    t, wd_each = dys[0].shape
    nd = len(dys)
    per = wd_each // tn
    nj = nd * per
    k = w.shape[0]
    ng = len(xs)
    has_res = dres is not None

    def body(*refs):
        dy_refs = refs[:nd]
        w_ref = refs[nd]
        x_refs = refs[nd + 1:nd + 1 + ng]
        g_ref = refs[nd + 1 + ng]
        dres_ref = refs[nd + 2 + ng] if has_res else None
        outs = refs[nd + 2 + ng + has_res:]
        dx_ref = outs[0]
        dxb_ref = outs[1] if emit_bf16 else None
        dg_ref, acc = outs[1 + emit_bf16:]
        i, j = pl.program_id(0), pl.program_id(1)

        @pl.when(j == 0)
        def _():
            acc[...] = jnp.zeros_like(acc)

        for d, dy_ref in enumerate(dy_refs):
            @pl.when((j >= d * per) & (j < (d + 1) * per))
            def _(dy_ref=dy_ref):
                acc[...] += lax.dot_general(dy_ref[...].astype(BF16), w_ref[...], NT, preferred_element_type=F32)

        @pl.when(j == nj - 1)
        def _():
            @pl.when(i == 0)
            def _():
                dg_ref[...] = jnp.zeros_like(dg_ref)

            off = 0
            for xr in x_refs:
                x = xr[...]
                wd = x.shape[1]
                g = g_ref[:, off:off + wd]
                dyn = acc[:, off:off + wd]
                r = lax.rsqrt(jnp.mean(x * x, axis=-1, keepdims=True) + EPS)
                gdy = dyn * g
                dx = r * gdy - x * (r * r * r * jnp.mean(gdy * x, axis=-1, keepdims=True))
                if has_res:
                    dx = dx + dres_ref[:, off:off + wd]
                dx_ref[:, off:off + wd] = dx
                if emit_bf16:
                    dxb_ref[:, off:off + wd] = dx.astype(BF16)
                dg_ref[:, off:off + wd] += jnp.sum(dyn * x * r, axis=0, keepdims=True)
                off += wd

    def dy_map(d):
        return lambda i, j: (i, jnp.clip(j - d * per, 0, per - 1))

    in_specs = [pl.BlockSpec((tm, tn), dy_map(d)) for d in range(nd)]
    in_specs.append(pl.BlockSpec((k, tn), lambda i, j: (0, j)))
    in_specs += [pl.BlockSpec((tm, x.shape[1]), lambda i, j: (i, 0)) for x in xs]
    in_specs.append(pl.BlockSpec((1, k), lambda i, j: (0, 0)))
    args = list(dys) + [w] + list(xs) + [gain]
    if has_res:
        in_specs.append(pl.BlockSpec((tm, k), lambda i, j: (i, 0)))
        args.append(dres)
    row = pl.BlockSpec((tm, k), lambda i, j: (i, 0))
    out_specs = [row] + ([row] if emit_bf16 else []) + [pl.BlockSpec((1, k), lambda i, j: (0, 0))]
    out_shape = [_sds((t, k), F32)] + ([_sds((t, k), BF16)] if emit_bf16 else []) + [_sds((1, k), F32)]
    return pl.pallas_call(
        body, name=name, grid=(t // tm, nj), in_specs=in_specs, out_specs=out_specs, out_shape=out_shape,
        scratch_shapes=[pltpu.VMEM((tm, k), F32)],
        compiler_params=_cparams(ARB, ARB),
    )(*args)


def _rope_partner(y):
    n = y.shape[1]
    lane = lax.broadcasted_iota(jnp.int32, y.shape, 1)
    return jnp.where((lane & 31) < 16, pltpu.roll(y, n - 16, 1), pltpu.roll(y, 16, 1))


def _residue_specs(tm, width, nt):
    specs = [pl.BlockSpec((tm, width), lambda b, i: (b * nt + i, 0))]
    for d in DILATIONS[1:]:
        specs.append(pl.BlockSpec((None, d, tm // d, width), lambda b, i: (b, 0, i, 0)))
    return specs


def _residue_shapes(bl, width, dtype):
    return [_sds((bl * SEQ, width), dtype)] + [_sds((bl, d, SEQ // d, width), dtype) for d in DILATIONS[1:]]


def _qkprep_fwd(proj, gains, cos, sins, *, tm, name):
    t = proj.shape[0]
    bl = t // SEQ
    nt = SEQ // tm

    def body(p_ref, g_ref, c_ref, s_ref, qa1_ref, qa4_ref, qa16_ref, qb_ref, qd_ref, scr):
        e = _group_sum_matrix(256, True)

        def hn(x, row):
            wd = x.shape[1]
            ms = _seg_sum(x * x, e[:wd, :wd]) * (1.0 / HEAD_DIM)
            return x * lax.rsqrt(ms + EPS) * g_ref[row:row + 1, :wd]

        qa = jnp.concatenate([hn(p_ref[:, 0:256], 0) * ATTN_SCALE, hn(p_ref[:, 256:512], 1), p_ref[:, 512:768]], axis=1)
        qa1_ref[...] = qa.astype(BF16)
        _scatter_cols(scr, 0, qa)
        for d, ref in ((4, qa4_ref), (16, qa16_ref)):
            for r in range(d):
                ref[r] = _read_residue(scr, 0, 6, r, d).astype(BF16)
        qb_ref[:, 0:256] = (hn(p_ref[:, 768:1024], 2) * ATTN_SCALE).astype(BF16)
        qb_ref[:, 256:384] = hn(p_ref[:, 1024:1152], 3).astype(BF16)
        qb_ref[:, 384:512] = p_ref[:, 1152:1280].astype(BF16)
        yq = hn(p_ref[:, 1792:2048], 4)
        yq = yq * c_ref[...] + _rope_partner(yq) * s_ref[...]
        qd_ref[:, 0:256] = (yq * ATTN_SCALE).astype(BF16)
        yk = hn(p_ref[:, 2048:2176], 5)
        yk = yk * c_ref[:, 0:128] + _rope_partner(yk) * s_ref[:, 0:128]
        qd_ref[:, 256:384] = yk.astype(BF16)
        qd_ref[:, 384:512] = p_ref[:, 2176:2304].astype(BF16)

    row = lambda width: pl.BlockSpec((tm, width), lambda b, i: (b * nt + i, 0))
    tab = pl.BlockSpec((tm, 256), lambda b, i: (i, 0))
    return pl.pallas_call(
        body, name=name, grid=(bl, nt),
        in_specs=[row(IN_WIDTH), pl.BlockSpec((8, 256), lambda b, i: (0, 0)), tab, tab],
        out_specs=_residue_specs(tm, 768, nt) + [row(512), row(512)],
        out_shape=_residue_shapes(bl, 768, BF16) + [_sds((t, 512), BF16), _sds((t, 512), BF16)],
        scratch_shapes=[pltpu.VMEM((6, tm, LANES), F32)],
        compiler_params=_cparams(PAR, PAR),
    )(proj, gains, cos, sins)


def _qkprep_bwd(proj, da, db, dd, dcu, dcv, gains, cos, sins, *, tm, name):
    t = proj.shape[0]
    bl = t // SEQ
    nt = SEQ // tm
    flat = [a for cfg in da for a in cfg] + list(db) + list(dd) + [dcu, dcv]

    def body(*refs):
        p_ref, g_ref, c_ref, s_ref = refs[:4]
        d_refs = refs[4:4 + len(flat)]
        dp_ref, dg_ref, scr = refs[4 + len(flat):]
        a_refs = d_refs[:9]
        dqb_ref, dkb_ref, dvb_ref, dqd_ref, dkd_ref, dvd_ref, dcu_ref, dcv_ref = d_refs[9:]
        e = _group_sum_matrix(256, True)
        first = (pl.program_id(0) == 0) & (pl.program_id(1) == 0)
        last = (pl.program_id(0) == bl - 1) & (pl.program_id(1) == nt - 1)

        @pl.when(first)
        def _():
            dg_ref[...] = jnp.zeros_like(dg_ref)

        def hn_bwd(x, dy, row):
            wd = x.shape[1]
            ee = e[:wd, :wd]
            g = g_ref[row:row + 1, :wd]
            r = lax.rsqrt(_seg_sum(x * x, ee) * (1.0 / HEAD_DIM) + EPS)
            gdy = dy * g
            dx = r * gdy - x * (r * r * r * (_seg_sum(gdy * x, ee) * (1.0 / HEAD_DIM)))
            dg_ref[row:row + 1, :wd] += jnp.sum(dy * x * r, axis=0, keepdims=True)
            return dx

        def rope_bwd(dy, wd):
            return dy * c_ref[:, :wd] + _rope_partner(dy * s_ref[:, :wd])

        dqkv = jnp.concatenate([a_refs[0][...], a_refs[1][...], a_refs[2][...]], axis=1)
        for ci, d in ((1, 4), (2, 16)):
            for r in range(d):
                part = jnp.concatenate([a_refs[3 * ci + m][r] for m in range(3)], axis=1)
                _write_residue(scr, 0, r, d, part)
            dqkv = dqkv + _gather_cols(scr, 0, 6)
        dp_ref[:, 0:256] = hn_bwd(p_ref[:, 0:256], dqkv[:, 0:256] * ATTN_SCALE, 0).astype(BF16)
        dp_ref[:, 256:512] = hn_bwd(p_ref[:, 256:512], dqkv[:, 256:512], 1).astype(BF16)
        dp_ref[:, 512:768] = dqkv[:, 512:768].astype(BF16)
        dp_ref[:, 768:1024] = hn_bwd(p_ref[:, 768:1024], dqb_ref[...] * ATTN_SCALE, 2).astype(BF16)
        dp_ref[:, 1024:1152] = hn_bwd(p_ref[:, 1024:1152], dkb_ref[...], 3).astype(BF16)
        dp_ref[:, 1152:1280] = dvb_ref[...].astype(BF16)
        dp_ref[:, 1280:1536] = dcu_ref[...].astype(BF16)
        dp_ref[:, 1536:1792] = dcv_ref[...].astype(BF16)
        dp_ref[:, 1792:2048] = hn_bwd(p_ref[:, 1792:2048], rope_bwd(dqd_ref[...] * ATTN_SCALE, 256), 4).astype(BF16)
        dp_ref[:, 2048:2176] = hn_bwd(p_ref[:, 2048:2176], rope_bwd(dkd_ref[...], 128), 5).astype(BF16)
        dp_ref[:, 2176:2304] = dvd_ref[...].astype(BF16)

        @pl.when(last)
        def _():
            dg_ref[...] = _seg_sum(dg_ref[...], _group_sum_matrix(256, False))

    row = lambda width: pl.BlockSpec((tm, width), lambda b, i: (b * nt + i, 0))
    tab = pl.BlockSpec((tm, 256), lambda b, i: (i, 0))
    in_specs = [row(IN_WIDTH), pl.BlockSpec((8, 256), lambda b, i: (0, 0)), tab, tab]
    res_specs = _residue_specs(tm, 256, nt)
    in_specs += [res_specs[ci] for ci in range(3) for _ in range(3)]
    in_specs += [row(a.shape[1]) for a in flat[9:]]
    return pl.pallas_call(
        body, name=name, grid=(bl, nt), in_specs=in_specs,
        out_specs=[row(IN_WIDTH), pl.BlockSpec((8, 256), lambda b, i: (0, 0))],
        out_shape=[_sds((t, IN_WIDTH), BF16), _sds((8, 256), F32)],
        scratch_shapes=[pltpu.VMEM((6, tm, LANES), F32)],
        compiler_params=_cparams(ARB, ARB),
    )(proj, gains, cos, sins, *flat)


def _band_spec(seq_len, spec):
    width, idx = spec
    return pl.BlockSpec((None, None, seq_len, width), lambda b, r: (b, r, 0, idx))


def _fill_padded(dst, src_ref, rad, seq_len):
    z = jnp.zeros((rad, dst.shape[1]), dst.dtype)
    dst[0:rad, :] = z
    dst[rad + seq_len:rad + seq_len + rad, :] = z
    dst[rad:rad + seq_len, :] = src_ref[...]


def _band_fwd(src, qs, ks, vs, bias, sink, *, rad, nh, nkv, name):
    bl, dil, sl, _ = src.shape
    blk = bias.shape[1]
    kw = blk + 2 * rad
    nb = sl // blk
    rep = nh // nkv
    has_sink = sink is not None

    def body(*refs):
        q_ref, k_ref, v_ref, b_ref = refs[:4]
        s_ref = refs[4] if has_sink else None
        o_ref, l_ref, kp, vp = refs[4 + has_sink:]
        _fill_padded(kp, k_ref, rad, sl)
        _fill_padded(vp, v_ref, rad, sl)

        def blk_body(i, carry):
            r0 = pl.multiple_of(i * blk, blk)
            qb = q_ref[pl.ds(r0, blk), :]
            kwin = kp[pl.ds(r0, kw), :]
            vwin = vp[pl.ds(r0, kw), :]
            col = r0 - rad + lax.broadcasted_iota(jnp.int32, (blk, kw), 1)
            neg = jnp.where((col >= 0) & (col < sl), 0.0, NEG_INF).astype(F32)
            for h in range(nh):
                g = h // rep
                hs = slice(h * HEAD_DIM, (h + 1) * HEAD_DIM)
                gs = slice(g * HEAD_DIM, (g + 1) * HEAD_DIM)
                s = lax.dot_general(qb[:, hs], kwin[:, gs], NT, preferred_element_type=F32)
                s = s + b_ref[h] + neg
                m = jnp.max(s, axis=1, keepdims=True)
                if has_sink:
                    sk = s_ref[h][0:1, 0:1]
                    m = jnp.maximum(m, sk)
                p = jnp.exp(s - m)
                den = jnp.sum(p, axis=1, keepdims=True)
                if has_sink:
                    den = den + jnp.exp(sk - m)
                o = jnp.dot(p.astype(BF16), vwin[:, gs], preferred_element_type=F32) / den
                o_ref[pl.ds(r0, blk), hs] = o
                l_ref[pl.ds(r0, blk), hs] = jnp.broadcast_to(m + jnp.log(den), (blk, HEAD_DIM))
            return carry

        lax.fori_loop(0, nb, blk_body, 0)

    in_specs = [_band_spec(sl, qs), _band_spec(sl, ks), _band_spec(sl, vs),
                pl.BlockSpec((nh, blk, kw), lambda b, r: (0, 0, 0))]
    args = [src] * 3 + [bias]
    if has_sink:
        in_specs.append(pl.BlockSpec((nh, 8, 128), lambda b, r: (0, 0, 0)))
        args.append(sink)
    return pl.pallas_call(
        body, name=name, grid=(bl, dil), in_specs=in_specs,
        out_specs=[_band_spec(sl, (256, 0))] * 2,
        out_shape=[_sds((bl, dil, sl, 256), F32)] * 2,
        scratch_shapes=[pltpu.VMEM((sl + 2 * rad, ks[0]), BF16), pltpu.VMEM((sl + 2 * rad, vs[0]), BF16)],
        compiler_params=_cparams(PAR, PAR),
    )(*args)


def _band_bwd(src, qs, ks, vs, bias, sink, dy, dcol, lse, delta, *, rad, nh, nkv, name):
    bl, dil, sl, _ = src.shape
    blk = bias.shape[1]
    kw = blk + 2 * rad
    nb = sl // blk
    rep = nh // nkv
    has_sink = sink is not None
    wk, wv = ks[0], vs[0]

    def body(*refs):
        q_ref, k_ref, v_ref, b_ref = refs[:4]
        s_ref = refs[4] if has_sink else None
        do_ref, l_ref, dl_ref = refs[4 + has_sink:7 + has_sink]
        outs = refs[7 + has_sink:]
        if has_sink:
            dq_ref, dk_ref, dv_ref, db_ref, dsk_ref, kp, vp, dka, dva = outs
        else:
            dq_ref, dk_ref, dv_ref, db_ref, kp, vp, dka, dva = outs

        @pl.when((pl.program_id(0) == 0) & (pl.program_id(1) == 0))
        def _():
            db_ref[...] = jnp.zeros_like(db_ref)
            if has_sink:
                dsk_ref[...] = jnp.zeros_like(dsk_ref)

        _fill_padded(kp, k_ref, rad, sl)
        _fill_padded(vp, v_ref, rad, sl)
        dka[...] = jnp.zeros_like(dka)
        dva[...] = jnp.zeros_like(dva)

        def blk_body(i, carry):
            r0 = pl.multiple_of(i * blk, blk)
            qb = q_ref[pl.ds(r0, blk), :]
            kwin = kp[pl.ds(r0, kw), :]
            vwin = vp[pl.ds(r0, kw), :]
            dob = do_ref[pl.ds(r0, blk), :].astype(BF16)
            lb = l_ref[pl.ds(r0, blk), :]
            dlb = dl_ref[pl.ds(r0, blk), :]
            col = r0 - rad + lax.broadcasted_iota(jnp.int32, (blk, kw), 1)
            neg = jnp.where((col >= 0) & (col < sl), 0.0, NEG_INF).astype(F32)
            for h in range(nh):
                g = h // rep
                hs = slice(h * HEAD_DIM, (h + 1) * HEAD_DIM)
                gs = slice(g * HEAD_DIM, (g + 1) * HEAD_DIM)
                qh, kh, vh, doh = qb[:, hs], kwin[:, gs], vwin[:, gs], dob[:, hs]
                lh = lb[:, h * HEAD_DIM:h * HEAD_DIM + 1]
                dlh = dlb[:, h * HEAD_DIM:h * HEAD_DIM + 1]
                s = lax.dot_general(qh, kh, NT, preferred_element_type=F32) + b_ref[h] + neg
                p = jnp.exp(s - lh)
                dp = lax.dot_general(doh, vh, NT, preferred_element_type=F32)
                ds = p * (dp - dlh)
                dsb = ds.astype(BF16)
                dq_ref[pl.ds(r0, blk), hs] = jnp.dot(dsb, kh, preferred_element_type=F32)
                dka[pl.ds(r0, kw), gs] += lax.dot_general(qh, dsb, TN, preferred_element_type=F32).T
                dva[pl.ds(r0, kw), gs] += lax.dot_general(doh, p.astype(BF16), TN, preferred_element_type=F32).T
                db_ref[h] += ds
                if has_sink:
                    ps = jnp.exp(s_ref[h][0:1, 0:1] - lh)
                    dsk_ref[h] += jnp.broadcast_to(-jnp.sum(ps * dlh, axis=0, keepdims=True), (8, 128))
            return carry

        lax.fori_loop(0, nb, blk_body, 0)
        dk_ref[...] = dka[rad:rad + sl, :]
        dv_ref[...] = dva[rad:rad + sl, :]

    const3 = lambda b, r: (0, 0, 0)
    in_specs = [_band_spec(sl, qs), _band_spec(sl, ks), _band_spec(sl, vs), pl.BlockSpec((nh, blk, kw), const3)]
    args = [src] * 3 + [bias]
    if has_sink:
        in_specs.append(pl.BlockSpec((nh, 8, 128), const3))
        args.append(sink)
    row = _band_spec(sl, (256, 0))
    in_specs += [_band_spec(sl, (256, dcol)), row, row]
    args += [dy, lse, delta]
    out_specs = [row, _band_spec(sl, (wk, 0)), _band_spec(sl, (wv, 0)), pl.BlockSpec((nh, blk, kw), const3)]
    out_shape = [_sds((bl, dil, sl, 256), F32), _sds((bl, dil, sl, wk), F32), _sds((bl, dil, sl, wv), F32),
                 _sds((nh, blk, kw), F32)]
    if has_sink:
        out_specs.append(pl.BlockSpec((nh, 8, 128), const3))
        out_shape.append(_sds((nh, 8, 128), F32))
    return pl.pallas_call(
        body, name=name, grid=(bl, dil), in_specs=in_specs, out_specs=out_specs, out_shape=out_shape,
        scratch_shapes=[pltpu.VMEM((sl + 2 * rad, wk), BF16), pltpu.VMEM((sl + 2 * rad, wv), BF16),
                        pltpu.VMEM((sl + 2 * rad, wk), F32), pltpu.VMEM((sl + 2 * rad, wv), F32)],
        compiler_params=_cparams(ARB, ARB),
    )(*args)


def _combine_a(os_, ls_, *, tm, name):
    bl = os_[1].shape[0]
    t = bl * SEQ
    nt = SEQ // tm

    def body(o1, o4, o16, l1, l4, l16, y_ref, lt_ref, scr):
        for k, (d, ref) in enumerate(((4, o4), (16, o16), (4, l4), (16, l16))):
            for r in range(d):
                _write_residue(scr, 2 * k, r, d, ref[r])
        o2, o3, b, c = (_gather_cols(scr, 2 * k, 2) for k in range(4))
        a = l1[...]
        m = jnp.maximum(jnp.maximum(a, b), c)
        ea, eb, ec = jnp.exp(a - m), jnp.exp(b - m), jnp.exp(c - m)
        den = ea + eb + ec
        y_ref[...] = (ea / den) * o1[...] + (eb / den) * o2 + (ec / den) * o3
        lt_ref[...] = m + jnp.log(den)

    specs = _residue_specs(tm, 256, nt)
    return pl.pallas_call(
        body, name=name, grid=(bl, nt), in_specs=specs * 2, out_specs=[specs[0]] * 2,
        out_shape=[_sds((t, 256), F32)] * 2, scratch_shapes=[pltpu.VMEM((8, tm, LANES), F32)],
        compiler_params=_cparams(PAR, PAR),
    )(*os_, *ls_)


def _deltas(dycat, ya, yb, yd, lse_a, *, tm, name):
    t = ya.shape[0]
    bl = t // SEQ
    nt = SEQ // tm

    def body(dy_ref, ya_ref, yb_ref, yd_ref, la_ref, dy4, dy16, l4, l16, da1, da4, da16, db_ref, dd_ref, scr):
        e = _group_sum_matrix(256, True)
        dya = dy_ref[:, 0:256]
        dla = _seg_sum(dya * ya_ref[...], e)
        da1[...] = dla
        db_ref[...] = _seg_sum(dy_ref[:, 256:512] * yb_ref[...], e)
        dd_ref[...] = _seg_sum(dy_ref[:, 768:1024] * yd_ref[...], e)
        for k, (val, r4, r16) in enumerate(((dya, dy4, dy16), (la_ref[...], l4, l16), (dla, da4, da16))):
            _scatter_cols(scr, 2 * k, val)
            for d, ref in ((4, r4), (16, r16)):
                for r in range(d):
                    ref[r] = _read_residue(scr, 2 * k, 2, r, d)

    specs = _residue_specs(tm, 256, nt)
    nat = specs[0]
    shapes = _residue_shapes(bl, 256, F32)
    outs = pl.pallas_call(
        body, name=name, grid=(bl, nt),
        in_specs=[pl.BlockSpec((tm, 1024), lambda b, i: (b * nt + i, 0)), nat, nat, nat, nat],
        out_specs=specs[1:] + specs[1:] + specs + [nat, nat],
        out_shape=shapes[1:] + shapes[1:] + shapes + [shapes[0], shapes[0]],
        scratch_shapes=[pltpu.VMEM((6, tm, LANES), F32)],
        compiler_params=_cparams(PAR, PAR),
    )(dycat, ya, yb, yd, lse_a)
    return outs[0:2], outs[2:4], outs[4:7], outs[7], outs[8]


def _dense_fwd(qd, *, tq, name):
    t = qd.shape[0]
    bl = t // SEQ
    nq = SEQ // tq

    def body(q_ref, k_ref, v_ref, o_ref, l_ref):
        q = q_ref[...]
        for g in range(2):
            h0, h1 = 2 * g, 2 * g + 1
            q2 = jnp.concatenate([q[:, h0 * 64:(h0 + 1) * 64], q[:, h1 * 64:(h1 + 1) * 64]], axis=0)
            kg = k_ref[:, g * 64:(g + 1) * 64]
            vg = v_ref[:, g * 64:(g + 1) * 64]
            s = lax.dot_general(q2, kg, NT, preferred_element_type=F32)
            m = jnp.max(s, axis=1, keepdims=True)
            p = jnp.exp(s - m)
            den = jnp.sum(p, axis=1, keepdims=True)
            o2 = jnp.dot(p.astype(BF16), vg, preferred_element_type=F32) / den
            l2 = jnp.broadcast_to(m + jnp.log(den), (2 * tq, 64))
            o_ref[:, h0 * 64:(h0 + 1) * 64] = o2[:tq]
            o_ref[:, h1 * 64:(h1 + 1) * 64] = o2[tq:]
            l_ref[:, h0 * 64:(h0 + 1) * 64] = l2[:tq]
            l_ref[:, h1 * 64:(h1 + 1) * 64] = l2[tq:]

    q3 = qd.reshape(bl, SEQ, 512)
    o, lse = pl.pallas_call(
        body, name=name, grid=(bl, nq),
        in_specs=[pl.BlockSpec((None, tq, 256), lambda b, i: (b, i, 0)),
                  pl.BlockSpec((None, SEQ, 128), lambda b, i: (b, 0, 2)),
                  pl.BlockSpec((None, SEQ, 128), lambda b, i: (b, 0, 3))],
        out_specs=[pl.BlockSpec((None, tq, 256), lambda b, i: (b, i, 0))] * 2,
        out_shape=[_sds((bl, SEQ, 256), F32)] * 2,
        compiler_params=_cparams(PAR, PAR),
    )(q3, q3, q3)
    return o.reshape(t, 256), lse.reshape(t, 256)


def _dense_bwd(qd, dycat, lse, delta, *, tq, name):
    t = qd.shape[0]
    bl = t // SEQ
    nq = SEQ // tq

    def body(q_ref, k_ref, v_ref, do_ref, l_ref, dl_ref, dq_ref, dk_ref, dv_ref, dkt, dvt):
        @pl.when(pl.program_id(1) == 0)
        def _():
            dkt[...] = jnp.zeros_like(dkt)
            dvt[...] = jnp.zeros_like(dvt)

        q = q_ref[...]
        do = do_ref[...].astype(BF16)
        lv = l_ref[...]
        dlv = dl_ref[...]
        for g in range(2):
            h0, h1 = 2 * g, 2 * g + 1
            q2 = jnp.concatenate([q[:, h0 * 64:(h0 + 1) * 64], q[:, h1 * 64:(h1 + 1) * 64]], axis=0)
            do2 = jnp.concatenate([do[:, h0 * 64:(h0 + 1) * 64], do[:, h1 * 64:(h1 + 1) * 64]], axis=0)
            l2 = jnp.concatenate([lv[:, h0 * 64:h0 * 64 + 1], lv[:, h1 * 64:h1 * 64 + 1]], axis=0)
            dl2 = jnp.concatenate([dlv[:, h0 * 64:h0 * 64 + 1], dlv[:, h1 * 64:h1 * 64 + 1]], axis=0)
            kg = k_ref[:, g * 64:(g + 1) * 64]
            vg = v_ref[:, g * 64:(g + 1) * 64]
            s = lax.dot_general(q2, kg, NT, preferred_element_type=F32)
            p = jnp.exp(s - l2)
            dp = lax.dot_general(do2, vg, NT, preferred_element_type=F32)
            ds = (p * (dp - dl2)).astype(BF16)
            dq2 = jnp.dot(ds, kg, preferred_element_type=F32)
            dq_ref[:, h0 * 64:(h0 + 1) * 64] = dq2[:tq]
            dq_ref[:, h1 * 64:(h1 + 1) * 64] = dq2[tq:]
            dkt[g * 64:(g + 1) * 64, :] += lax.dot_general(q2, ds, TN, preferred_element_type=F32)
            dvt[g * 64:(g + 1) * 64, :] += lax.dot_general(do2, p.astype(BF16), TN, preferred_element_type=F32)

        @pl.when(pl.program_id(1) == nq - 1)
        def _():
            dk_ref[...] = dkt[...].T
            dv_ref[...] = dvt[...].T

    q3 = qd.reshape(bl, SEQ, 512)
    tile = pl.BlockSpec((None, tq, 256), lambda b, i: (b, i, 0))
    full = pl.BlockSpec((None, SEQ, 128), lambda b, i: (b, 0, 0))
    dq, dk, dv = pl.pallas_call(
        body, name=name, grid=(bl, nq),
        in_specs=[tile, pl.BlockSpec((None, SEQ, 128), lambda b, i: (b, 0, 2)),
                  pl.BlockSpec((None, SEQ, 128), lambda b, i: (b, 0, 3)),
                  pl.BlockSpec((None, tq, 256), lambda b, i: (b, i, 3)), tile, tile],
        out_specs=[tile, full, full],
        out_shape=[_sds((bl, SEQ, 256), F32), _sds((bl, SEQ, 128), F32), _sds((bl, SEQ, 128), F32)],
        scratch_shapes=[pltpu.VMEM((128, SEQ), F32), pltpu.VMEM((128, SEQ), F32)],
        compiler_params=_cparams(PAR, ARB),
    )(q3, q3, q3, dycat.reshape(bl, SEQ, 1024), lse.reshape(bl, SEQ, 256), delta.reshape(bl, SEQ, 256))
    return dq.reshape(t, 256), dk.reshape(t, 128), dv.reshape(t, 128)


def _c_norm(cv, gam, bet):
    vg = _gelu(cv)
    mu = jnp.mean(vg, axis=-1, keepdims=True)
    xc = vg - mu
    r = lax.rsqrt(jnp.mean(xc * xc, axis=-1, keepdims=True) + EPS)
    xhat = xc * r
    return xhat * gam + bet, xhat, r


def _c_fwd(proj, gam, bet, ws, bst, *, tm, name):
    t = proj.shape[0]
    nch = tm // C_CHUNK

    def body(u_ref, v_ref, g_ref, b_ref, ws_ref, bs_ref, y_ref):
        vn, _, _ = _c_norm(v_ref[...], g_ref[...], b_ref[...])
        vnb = vn.astype(BF16)
        for c in range(nch):
            rows = slice(c * C_CHUNK, (c + 1) * C_CHUNK)
            for g in range(C_GROUPS):
                gs = slice(g * 64, (g + 1) * 64)
                mixed = jnp.dot(ws_ref[g], vnb[rows, gs], preferred_element_type=F32) + bs_ref[:, gs]
                y_ref[rows, gs] = _gelu(u_ref[rows, gs]) * mixed

    vec = pl.BlockSpec((1, 256), lambda i: (0, 0))
    return pl.pallas_call(
        body, name=name, grid=(t // tm,),
        in_specs=[pl.BlockSpec((tm, 256), lambda i: (i, 5)), pl.BlockSpec((tm, 256), lambda i: (i, 6)), vec, vec,
                  pl.BlockSpec((C_GROUPS, C_CHUNK, C_CHUNK), lambda i: (0, 0, 0)),
                  pl.BlockSpec((C_CHUNK, 256), lambda i: (0, 0))],
        out_specs=pl.BlockSpec((tm, 256), lambda i: (i, 0)), out_shape=_sds((t, 256), F32),
        compiler_params=_cparams(PAR),
    )(proj, proj, gam, bet, ws, bst)


def _c_bwd(proj, dycat, gam, bet, ws, wst, bst, *, tm, name):
    t = proj.shape[0]
    nch = tm // C_CHUNK
    nstep = t // tm

    def body(u_ref, v_ref, dy_ref, g_ref, b_ref, ws_ref, wst_ref, bs_ref,
             du_ref, dv_ref, dws_ref, dbs_ref, dg_ref, db_ref, dvn_s):
        step = pl.program_id(0)

        @pl.when(step == 0)
        def _():
            dws_ref[...] = jnp.zeros_like(dws_ref)
            dbs_ref[...] = jnp.zeros_like(dbs_ref)
            dg_ref[...] = jnp.zeros_like(dg_ref)
            db_ref[...] = jnp.zeros_like(db_ref)

        cv = v_ref[...]
        gam_v = g_ref[...]
        vn, xhat, r = _c_norm(cv, gam_v, b_ref[...])
        vnb = vn.astype(BF16)
        for c in range(nch):
            rows = slice(c * C_CHUNK, (c + 1) * C_CHUNK)
            for g in range(C_GROUPS):
                gs = slice(g * 64, (g + 1) * 64)
                cu = u_ref[rows, gs]
                dy = dy_ref[rows, gs]
                mixed = jnp.dot(ws_ref[g], vnb[rows, gs], preferred_element_type=F32) + bs_ref[:, gs]
                du_ref[rows, gs] = dy * mixed * _gelu_grad(cu)
                dmix = dy * _gelu(cu)
                dbs_ref[:, gs] += dmix
                dmb = dmix.astype(BF16)
                dws_ref[g] += lax.dot_general(dmb, vnb[rows, gs], NT, preferred_element_type=F32)
                dvn_s[rows, gs] = jnp.dot(wst_ref[g], dmb, preferred_element_type=F32)
        dvn = dvn_s[...]
        dg_ref[...] += jnp.sum(dvn * xhat, axis=0, keepdims=True)
        db_ref[...] += jnp.sum(dvn, axis=0, keepdims=True)
        dxh = dvn * gam_v
        dvg = r * (dxh - jnp.mean(dxh, axis=-1, keepdims=True) - xhat * jnp.mean(dxh * xhat, axis=-1, keepdims=True))
        dv_ref[...] = dvg * _gelu_grad(cv)

        @pl.when(step == nstep - 1)
        def _():
            dbs_ref[...] = _seg_sum(dbs_ref[...], _group_sum_matrix(256, True))

    vec = pl.BlockSpec((1, 256), lambda i: (0, 0))
    mat = pl.BlockSpec((C_GROUPS, C_CHUNK, C_CHUNK), lambda i: (0, 0, 0))
    bsp = pl.BlockSpec((C_CHUNK, 256), lambda i: (0, 0))
    tile = pl.BlockSpec((tm, 256), lambda i: (i, 0))
    return pl.pallas_call(
        body, name=name, grid=(nstep,),
        in_specs=[pl.BlockSpec((tm, 256), lambda i: (i, 5)), pl.BlockSpec((tm, 256), lambda i: (i, 6)),
                  pl.BlockSpec((tm, 256), lambda i: (i, 2)), vec, vec, mat, mat, bsp],
        out_specs=[tile, tile, mat, bsp, vec, vec],
        out_shape=[_sds((t, 256), F32), _sds((t, 256), F32), _sds((C_GROUPS, C_CHUNK, C_CHUNK), F32),
                   _sds((C_CHUNK, 256), F32), _sds((1, 256), F32), _sds((1, 256), F32)],
        scratch_shapes=[pltpu.VMEM((tm, 256), F32)],
        compiler_params=_cparams(ARB),
    )(proj, proj, dycat, gam, bet, ws, wst, bst)


FF_TC = 128
FF_NB = D_FF // FF_TC
FF_CH = 64
FF_HALO = 16


def _taps(ref, r0, win, where):
    z = jnp.zeros((FF_HALO, win.shape[1]), F32)
    if where == "first":
        win[0:FF_HALO, :] = z
        win[FF_HALO:, :] = ref[0:FF_CH + FF_HALO, :].astype(F32)
    elif where == "last":
        win[0:FF_CH + FF_HALO, :] = ref[SEQ - FF_CH - FF_HALO:SEQ, :].astype(F32)
        win[FF_CH + FF_HALO:, :] = z
    else:
        win[...] = ref[pl.ds(pl.multiple_of(r0 - FF_HALO, FF_HALO), FF_CH + 2 * FF_HALO), :].astype(F32)
    return tuple(win[FF_HALO + o:FF_HALO + o + FF_CH, :] for o in (-1, 0, 1))


def _chunk_loop(step):
    step(0, lambda ref, win: _taps(ref, 0, win, "first"))

    def mid(i, carry):
        r0 = pl.multiple_of(i * FF_CH, FF_CH)
        step(r0, lambda ref, win: _taps(ref, r0, win, "mid"))
        return carry

    lax.fori_loop(1, SEQ // FF_CH - 1, mid, 0)
    step(SEQ - FF_CH, lambda ref, win: _taps(ref, SEQ - FF_CH, win, "last"))


def _conv3(taps, w_ref, b_ref):
    dn, md, up = taps
    return w_ref[0:1, :] * dn + w_ref[1:2, :] * md + w_ref[2:3, :] * up + b_ref[...]


def _ff_specs(order):
    def at(fn):
        return (lambda b, j: fn(b, j)) if order == "bj" else (lambda j, b: fn(b, j))
    hs = [pl.BlockSpec((None, SEQ, FF_TC), at(lambda b, j, o=o: (b, 0, j + o))) for o in (0, FF_NB)]
    ws = [pl.BlockSpec((3, FF_TC), at(lambda b, j, o=o: (0, j + o))) for o in (0, FF_NB)]
    bs = [pl.BlockSpec((1, FF_TC), at(lambda b, j, o=o: (0, j + o))) for o in (0, FF_NB)]
    return hs, ws, bs


def _conv_gate_fwd(h, cw, cb, *, name):
    t = h.shape[0]
    bl = t // SEQ

    def body(hg_ref, hu_ref, wg_ref, wu_ref, bg_ref, bu_ref, a_ref, win):
        def step(r0, taps):
            cg = _conv3(taps(hg_ref, win.at[0]), wg_ref, bg_ref)
            cu = _conv3(taps(hu_ref, win.at[1]), wu_ref, bu_ref)
            a_ref[pl.ds(r0, FF_CH), :] = (cg * _sigmoid(cg) * cu).astype(BF16)

        _chunk_loop(step)

    hs, ws, bs = _ff_specs("bj")
    h3 = h.reshape(bl, SEQ, 2 * D_FF)
    act = pl.pallas_call(
        body, name=name, grid=(bl, FF_NB), in_specs=hs + ws + bs,
        out_specs=pl.BlockSpec((None, SEQ, FF_TC), lambda b, j: (b, 0, j)),
        out_shape=_sds((bl, SEQ, D_FF), BF16),
        scratch_shapes=[pltpu.VMEM((2, FF_CH + 2 * FF_HALO, FF_TC), F32)],
        compiler_params=_cparams(PAR, PAR),
    )(h3, h3, cw, cw, cb, cb)
    return act.reshape(t, D_FF)


def _conv_gate_bwd(h, dact, cw, cb, *, name):
    t = h.shape[0]
    bl = t // SEQ

    def body(hg_ref, hu_ref, wg_ref, wu_ref, bg_ref, bu_ref, da_ref,
             dhg_ref, dhu_ref, dwg_ref, dwu_ref, dbg_ref, dbu_ref, dg_s, du_s, win, sums):
        @pl.when(pl.program_id(1) == 0)
        def _():
            for ref in (dwg_ref, dwu_ref, dbg_ref, dbu_ref):
                ref[...] = jnp.zeros_like(ref)

        sums[...] = jnp.zeros_like(sums)
        red = lambda x: jnp.sum(x.reshape(FF_CH // 8, 8, x.shape[1]), axis=0)

        def pass1(r0, taps):
            tg, tu = taps(hg_ref, win.at[0]), taps(hu_ref, win.at[1])
            cg = _conv3(tg, wg_ref, bg_ref)
            cu = _conv3(tu, wu_ref, bu_ref)
            da = da_ref[pl.ds(r0, FF_CH), :].astype(F32)
            sg = _sigmoid(cg)
            dcg = da * cu * (sg * (1.0 + cg * (1.0 - sg)))
            dcu = da * (cg * sg)
            dg_s[pl.ds(r0, FF_CH), :] = dcg
            du_s[pl.ds(r0, FF_CH), :] = dcu
            for half, (d, tp) in enumerate(((dcg, tg), (dcu, tu))):
                for k in range(3):
                    sums[4 * half + k] += red(d * tp[k])
                sums[4 * half + 3] += red(d)

        _chunk_loop(pass1)
        for half, (dw_ref, db_ref) in enumerate(((dwg_ref, dbg_ref), (dwu_ref, dbu_ref))):
            for k in range(3):
                dw_ref[k:k + 1, :] += jnp.sum(sums[4 * half + k], axis=0, keepdims=True)
            db_ref[...] += jnp.sum(sums[4 * half + 3], axis=0, keepdims=True)

        def pass2(r0, taps):
            for k, (s, w_ref, o_ref) in enumerate(((dg_s, wg_ref, dhg_ref), (du_s, wu_ref, dhu_ref))):
                dn, md, up = taps(s, win.at[k])
                o_ref[pl.ds(r0, FF_CH), :] = (w_ref[0:1, :] * up + w_ref[1:2, :] * md + w_ref[2:3, :] * dn).astype(BF16)

        _chunk_loop(pass2)

    hs, ws, bs = _ff_specs("jb")
    half = pl.BlockSpec((None, SEQ, FF_TC), lambda j, b: (b, 0, j))
    wsp = pl.BlockSpec((3, FF_TC), lambda j, b: (0, j))
    bsp = pl.BlockSpec((1, FF_TC), lambda j, b: (0, j))
    h3 = h.reshape(bl, SEQ, 2 * D_FF)
    dhg, dhu, dwg, dwu, dbg, dbu = pl.pallas_call(
        body, name=name, grid=(FF_NB, bl), in_specs=hs + ws + bs + [half],
        out_specs=[half, half, wsp, wsp, bsp, bsp],
        out_shape=[_sds((bl, SEQ, D_FF), BF16), _sds((bl, SEQ, D_FF), BF16), _sds((3, D_FF), F32), _sds((3, D_FF), F32),
                   _sds((1, D_FF), F32), _sds((1, D_FF), F32)],
        scratch_shapes=[pltpu.VMEM((SEQ, FF_TC), F32), pltpu.VMEM((SEQ, FF_TC), F32),
                        pltpu.VMEM((2, FF_CH + 2 * FF_HALO, FF_TC), F32), pltpu.VMEM((8, 8, FF_TC), F32)],
        compiler_params=_cparams(PAR, ARB),
    )(h3, h3, cw, cw, cb, cb, dact.reshape(bl, SEQ, D_FF))
    return (dhg.reshape(t, D_FF), dhu.reshape(t, D_FF), jnp.concatenate([dwg, dwu], axis=1),
            jnp.concatenate([dbg, dbu], axis=1))


def _ple_fwd(x2, gain, wg, pe, pe_blk, wp, *, tm, tn, name):
    t, k = x2.shape
    n = wg.shape[1]

    def body(x_ref, g_ref, wg_ref, pe_ref, wp_ref, xr_ref, hn_ref, x3_ref, gt_ref, pp_ref, hn_s):
        @pl.when(pl.program_id(1) == 0)
        def _():
            x = x_ref[...]
            r = lax.rsqrt(jnp.mean(x * x, axis=-1, keepdims=True) + EPS)
            hn_s[...] = (x * r * g_ref[...]).astype(BF16)
            hn_ref[...] = hn_s[...]

        gate = _sigmoid(jnp.dot(hn_s[...], wg_ref[...], preferred_element_type=F32))
        pp = jnp.dot(pe_ref[...].astype(BF16), wp_ref[...], preferred_element_type=F32)
        gt_ref[...] = gate.astype(BF16)
        pp_ref[...] = pp.astype(BF16)
        x3_ref[...] = xr_ref[...] + pp * gate

    tile = pl.BlockSpec((tm, tn), lambda i, j: (i, j))
    return pl.pallas_call(
        body, name=name, grid=(t // tm, n // tn),
        in_specs=[pl.BlockSpec((tm, k), lambda i, j: (i, 0)), pl.BlockSpec((1, k), lambda i, j: (0, 0)),
                  pl.BlockSpec((k, tn), lambda i, j: (0, j)), pl.BlockSpec((tm, PLE_DIM), lambda i, j: (pe_blk + i, 0)),
                  pl.BlockSpec((PLE_DIM, tn), lambda i, j: (0, j)), tile],
        out_specs=[pl.BlockSpec((tm, k), lambda i, j: (i, 0)), tile, tile, tile],
        out_shape=[_sds((t, k), BF16), _sds((t, n), F32), _sds((t, n), BF16), _sds((t, n), BF16)],
        scratch_shapes=[pltpu.VMEM((tm, k), BF16)],
        compiler_params=_cparams(PAR, ARB),
    )(x2, gain, wg, pe, wp, x2)


def _ple_bwd_ew(dx3, gate, pp, *, tm, name):
    t, n = dx3.shape

    def body(d_ref, g_ref, p_ref, dz_ref, dpp_ref):
        d, g = d_ref[...], g_ref[...]
        dz_ref[...] = (d * p_ref[...] * g * (1.0 - g)).astype(BF16)
        dpp_ref[...] = (d * g).astype(BF16)

    spec = pl.BlockSpec((tm, n), lambda i: (i, 0))
    return pl.pallas_call(
        body, name=name, grid=(t // tm,), in_specs=[spec] * 3, out_specs=[spec] * 2,
        out_shape=[_sds((t, n), BF16)] * 2, compiler_params=_cparams(PAR),
    )(dx3, gate, pp)


def _loss_head(y, tgt, *, tm, name):
    t, d = y.shape

    def body(y_ref, t_ref, l_ref, dy_ref):
        @pl.when(pl.program_id(0) == 0)
        def _():
            l_ref[...] = jnp.zeros_like(l_ref)

        e = y_ref[...] - t_ref[...]
        dy_ref[...] = e * (1.0 / d)
        s = jnp.sum(jnp.sum(e * e, axis=1, keepdims=True), axis=0, keepdims=True)
        l_ref[...] += jnp.broadcast_to(s * (0.5 / d), (8, 128))

    spec = pl.BlockSpec((tm, d), lambda i: (i, 0))
    return pl.pallas_call(
        body, name=name, grid=(t // tm,), in_specs=[spec, spec],
        out_specs=[pl.BlockSpec((8, 128), lambda i: (0, 0)), spec],
        out_shape=[_sds((8, 128), F32), _sds((t, d), F32)], compiler_params=_cparams(ARB),
    )(y, tgt)


BIAS_PC = 8192


def _onehot(bucket_row):
    rows = lax.broadcasted_iota(jnp.int32, (REL_BUCKETS, bucket_row.shape[1]), 0)
    return (rows == bucket_row).astype(BF16)


def _dot3(x, onehot, dims):
    acc = None
    for _ in range(3):
        term = x.astype(BF16)
        part = lax.dot_general(term, onehot, dims, preferred_element_type=F32)
        acc = part if acc is None else acc + part
        x = x - term.astype(F32)
    return acc


def _bias_lookup(table_t, bucket, *, name):
    h = table_t.shape[0]
    p = bucket.shape[1]

    def body(t_ref, b_ref, o_ref):
        bk = b_ref[...]
        val = _dot3(t_ref[...], _onehot(bk), (((1,), (0,)), ((), ())))
        o_ref[...] = jnp.where(bk >= 0, val, NEG_INF)

    return pl.pallas_call(
        body, name=name, grid=(p // BIAS_PC,),
        in_specs=[pl.BlockSpec((h, REL_BUCKETS), lambda i: (0, 0)), pl.BlockSpec((1, BIAS_PC), lambda i: (0, i))],
        out_specs=pl.BlockSpec((h, BIAS_PC), lambda i: (0, i)), out_shape=_sds((h, p), F32),
        compiler_params=_cparams(PAR),
    )(table_t, bucket)


def _bucket_reduce(dbiases, bucket, *, name):
    h, p = dbiases[0].shape
    nl = len(dbiases)

    def body(*refs):
        b_ref, o_ref = refs[nl], refs[nl + 1]

        @pl.when(pl.program_id(0) == 0)
        def _():
            o_ref[...] = jnp.zeros_like(o_ref)

        d = refs[0][...]
        for d_ref in refs[1:nl]:
            d = d + d_ref[...]
        o_ref[...] += _dot3(d, _onehot(b_ref[...]), NT)

    return pl.pallas_call(
        body, name=name, grid=(p // BIAS_PC,),
        in_specs=[pl.BlockSpec((h, BIAS_PC), lambda i: (0, i))] * nl + [pl.BlockSpec((1, BIAS_PC), lambda i: (0, i))],
        out_specs=pl.BlockSpec((h, REL_BUCKETS), lambda i: (0, 0)), out_shape=_sds((h, REL_BUCKETS), F32),
        compiler_params=_cparams(ARB),
    )(*dbiases, bucket)


def _adamw_math(w, g, m, v):
    m = ADAM_B1 * m + (1.0 - ADAM_B1) * g
    v = ADAM_B2 * v + (1.0 - ADAM_B2) * (g * g)
    m_hat = m / (1.0 - ADAM_B1 ** ADAM_STEP)
    v_hat = v / (1.0 - ADAM_B2 ** ADAM_STEP)
    delta = -ADAM_LR * (m_hat / (jnp.sqrt(v_hat) + ADAM_EPS) + ADAM_WD * w)
    return delta, m, v


def _adamw_reduce(parts, w, m, v, *, tr, name):
    nl = len(parts)
    rows, c = w.shape
    r = rows // nl
    nt = r // tr

    def body(*refs):
        p_refs = refs[:nl]
        w_ref, m_ref, v_ref, g_ref, d_ref, nm_ref, nv_ref = refs[nl:]
        for li, p_ref in enumerate(p_refs):
            @pl.when(pl.program_id(0) == li)
            def _(p_ref=p_ref):
                g = p_ref[0].astype(F32)
                for k in range(1, N_DEV):
                    g = g + p_ref[k].astype(F32)
                d, nm, nv = _adamw_math(w_ref[...], g, m_ref[...], v_ref[...])
                g_ref[...] = g
                d_ref[...] = d
                nm_ref[...] = nm
                nv_ref[...] = nv

    def part_map(li):
        return lambda l, i: (0, jnp.where(l == li, i, jnp.where(l < li, 0, nt - 1)), 0)

    spec = pl.BlockSpec((tr, c), lambda l, i: (l * nt + i, 0))
    return pl.pallas_call(
        body, name=name, grid=(nl, nt),
        in_specs=[pl.BlockSpec((N_DEV, tr, c), part_map(li)) for li in range(nl)] + [spec, spec, spec],
        out_specs=[spec] * 4, out_shape=[_sds((rows, c), F32)] * 4, compiler_params=_cparams(ARB, ARB),
    )(*parts, w, m, v)


def _adamw_plain(g, w, m, v, *, name):
    def body(g_ref, w_ref, m_ref, v_ref, d_ref, nm_ref, nv_ref):
        d, nm, nv = _adamw_math(w_ref[...], g_ref[...], m_ref[...], v_ref[...])
        d_ref[...] = d
        nm_ref[...] = nm
        nv_ref[...] = nv

    return pl.pallas_call(body, name=name, out_shape=[_sds(w.shape, F32)] * 3)(g, w, m, v)


def _mesh_pos():
    return lax.axis_index("x"), lax.axis_index("y"), lax.axis_index("c")


def _allgather_body(x_refs, out_refs, send_sems, recv_sems, local_sems, slot):
    x, y, c = _mesh_pos()
    me, sibling = (x, y, c), (x, y, 1 - c)
    chips = [(1 - x, y), (x, 1 - y), (1 - x, 1 - y)]
    waits = []
    for a, (x_ref, out_ref) in enumerate(zip(x_refs, out_refs)):
        def copy(k, block, to, src=None, out_ref=out_ref, a=a):
            return pltpu.make_async_remote_copy(
                src_ref=slot(out_ref, block) if src is None else src, dst_ref=slot(out_ref, block),
                send_sem=send_sems.at[a, k], recv_sem=recv_sems.at[a, k], device_id=to, device_id_type=MESH)

        mine = pltpu.make_async_copy(x_ref, slot(out_ref, me), local_sems.at[a])
        mine.start()
        first = [copy(0, me, sibling, src=x_ref)]
        first += [copy(1 + j, me, (*chip, c), src=x_ref) for j, chip in enumerate(chips)]
        for cp in first:
            cp.start()
        waits.append((copy, mine, first))
    sends = []
    for copy, mine, first in waits:
        passed = [copy(4 + j, (*chip, c), sibling) for j, chip in enumerate(chips)]
        for j, chip in enumerate(chips):
            copy(1 + j, (*chip, c), me).wait_recv()
            passed[j].start()
        sends.append(passed)
    for (copy, mine, first), passed in zip(waits, sends):
        copy(0, sibling, me).wait_recv()
        for j, chip in enumerate(chips):
            copy(4 + j, (*chip, 1 - c), me).wait_recv()
        for cp in first + passed:
            cp.wait_send()
        mine.wait()


PEER_FLIPS = ((0, 0, 1), (1, 0, 0), (0, 1, 0), (1, 1, 0), (1, 0, 1), (0, 1, 1), (1, 1, 1))


def _peer_copies(x_refs, land_refs, send_sem, recv_sem, scatter):
    x, y, c = _mesh_pos()
    me = 4 * x + 2 * y + c
    copies = []
    for x_ref, land_ref in zip(x_refs, land_refs):
        for fx, fy, fc in PEER_FLIPS:
            px, py, pc = x ^ fx, y ^ fy, c ^ fc
            src = x_ref.at[4 * px + 2 * py + pc] if scatter else x_ref
            copies.append(pltpu.make_async_remote_copy(
                src_ref=src, dst_ref=land_ref.at[me], send_sem=send_sem, recv_sem=recv_sem,
                device_id=(px, py, pc), device_id_type=MESH))
    return copies


def _sc_exchange(xs, *, scatter, collective_id, name):
    na = len(xs)
    land_shapes = [x.shape if scatter else (N_DEV,) + x.shape for x in xs]

    def body(*refs):
        x_refs, land_refs = refs[:na], refs[na:2 * na]
        send_sem, recv_sem, local_sem = refs[2 * na:]
        x, y, c = _mesh_pos()
        me = 4 * x + 2 * y + c
        barrier = pltpu.get_barrier_semaphore()
        for fx, fy, fc in PEER_FLIPS:
            pl.semaphore_signal(barrier, inc=1, device_id=(x ^ fx, y ^ fy, c ^ fc), device_id_type=MESH)
        pl.semaphore_wait(barrier, len(PEER_FLIPS))
        for x_ref, land_ref in zip(x_refs, land_refs):
            own = pltpu.make_async_copy(x_ref.at[me] if scatter else x_ref, land_ref.at[me], local_sem)
            own.start()
            own.wait()
        copies = _peer_copies(x_refs, land_refs, send_sem, recv_sem, scatter)
        for cp in copies:
            cp.start()
        for cp in copies:
            cp.wait()

    return pl.kernel(
        body, name=name, out_type=[_sds(s, x.dtype) for s, x in zip(land_shapes, xs)],
        mesh=plsc.ScalarSubcoreMesh(axis_name="sequencer", num_cores=1),
        scratch_types=[pltpu.SemaphoreType.DMA, pltpu.SemaphoreType.DMA, pltpu.SemaphoreType.DMA],
        compiler_params=pltpu.CompilerParams(collective_id=collective_id),
    )(*xs)


def _sc_allgather(xs, *, collective_id, name):
    na = len(xs)

    def body(*refs):
        x_refs, out_refs = refs[:na], refs[na:2 * na]
        send_sems, recv_sems, local_sems = refs[2 * na:]
        x, y, c = _mesh_pos()
        barrier = pltpu.get_barrier_semaphore()
        for fx, fy, fc in PEER_FLIPS:
            pl.semaphore_signal(barrier, inc=1, device_id=(x ^ fx, y ^ fy, c ^ fc), device_id_type=MESH)
        pl.semaphore_wait(barrier, len(PEER_FLIPS))
        _allgather_body(x_refs, out_refs, send_sems, recv_sems, local_sems,
                        lambda ref, pos: ref.at[4 * pos[0] + 2 * pos[1] + pos[2]])

    return pl.kernel(
        body, name=name, out_type=[_sds((N_DEV,) + x.shape, x.dtype) for x in xs],
        mesh=plsc.ScalarSubcoreMesh(axis_name="sequencer", num_cores=1),
        scratch_types=[pltpu.SemaphoreType.DMA((na, 7)), pltpu.SemaphoreType.DMA((na, 7)),
                       pltpu.SemaphoreType.DMA((na,))],
        compiler_params=pltpu.CompilerParams(collective_id=collective_id),
    )(*xs)


def _allgather_vmem(x, *, reduce, name):
    r, c = x.shape

    def body(x_ref, out_ref, *rest):
        if reduce:
            gath, send_sems, recv_sems, local_sems = rest
        else:
            send_sems, recv_sems, local_sems = rest
            gath = out_ref
        _allgather_body([x_ref], [gath], send_sems, recv_sems, local_sems,
                        lambda ref, pos: ref.at[pl.ds((4 * pos[0] + 2 * pos[1] + pos[2]) * r, r), :])
        if reduce:
            acc = gath[0:r, :]
            for k in range(1, N_DEV):
                acc = acc + gath[k * r:(k + 1) * r, :]
            out_ref[...] = acc

    vm = pl.BlockSpec(memory_space=pltpu.VMEM)
    scratch = [pltpu.SemaphoreType.DMA((1, 7)), pltpu.SemaphoreType.DMA((1, 7)), pltpu.SemaphoreType.DMA((1,))]
    if reduce:
        scratch = [pltpu.VMEM((N_DEV * r, c), x.dtype)] + scratch
    return pl.pallas_call(
        body, name=name, in_specs=[vm], out_specs=vm,
        out_shape=_sds((r, c) if reduce else (N_DEV * r, c), x.dtype), scratch_shapes=scratch,
    )(x)


def _t5_bucket(rel):
    nb = REL_BUCKETS // 2
    ret = jnp.where(rel > 0, nb, 0)
    n = jnp.abs(rel)
    max_exact = nb // 2
    nf = jnp.maximum(n, 1).astype(F32)
    large = max_exact + (jnp.log(nf / max_exact) / math.log(REL_MAX_DIST / max_exact)
                         * (nb - max_exact)).astype(jnp.int32)
    large = jnp.minimum(large, nb - 1)
    return ret + jnp.where(n < max_exact, n, large)


def _band_pattern(block, radius, dil):
    kw = block + 2 * radius
    rel = jnp.arange(kw)[None, :] - radius - jnp.arange(block)[:, None]
    return jnp.where(jnp.abs(rel) <= radius, _t5_bucket(rel * dil), -1).astype(jnp.int32).reshape(1, block * kw)


def _rope_tables():
    lane = np.arange(64)
    seg, j = lane // 32, lane % 32
    inv = ROPE_THETA ** (-jnp.arange(0, 32, 2, dtype=F32) / 32)
    tpos = jnp.arange(SEQ)
    pos = jnp.where(jnp.asarray(seg)[None, :] == 0, (tpos // GRID_W)[:, None], (tpos % GRID_W)[:, None])
    ang = pos.astype(F32) * inv[jnp.asarray(j % 16)][None, :]
    cos = jnp.cos(ang)
    sins = jnp.where(jnp.asarray(j)[None, :] < 16, -jnp.sin(ang), jnp.sin(ang))
    return jnp.tile(cos, (1, 4)), jnp.tile(sins, (1, 4))


A_Q, A_K, A_V = (256, 0), (256, 1), (256, 2)
B_Q, B_K, B_V = (256, 0), (128, 2), (128, 3)
A_HEADS = dict(rad=A_RADIUS, nh=4, nkv=4)
B_HEADS = dict(rad=SWA_RADIUS, nh=4, nkv=2)


def _pin(arr, token):
    return arr if token is None else arr + token[0:1, 0:1]


def _local_step(x, pe, tgt, rel_bias, wts, matmul_weights, grads_ready):
    t = x.shape[0]
    bl = t // SEQ
    cos, sins = _rope_tables()
    blocks_a = [min(BAND_BLOCK, SEQ // d) for d in DILATIONS]
    pats_a = [_band_pattern(blk, A_RADIUS, d) for blk, d in zip(blocks_a, DILATIONS)]
    pat_b = _band_pattern(BAND_BLOCK, SWA_RADIUS, 1)
    table_t = rel_bias.T
    bias_a = [_bias_lookup(table_t[:4], pt, name=f"bias_a{ci}").reshape(4, blk, blk + 2 * A_RADIUS)
              for ci, (pt, blk) in enumerate(zip(pats_a, blocks_a))]
    bias_b = _bias_lookup(table_t[4:], pat_b, name="bias_b").reshape(4, BAND_BLOCK, BAND_BLOCK + 2 * SWA_RADIUS)
    nat4 = lambda a: a.reshape(bl, 1, SEQ, a.shape[-1])

    saved = []
    for li in range(DEPTH):
        w = dict(wts[li])
        w.update(matmul_weights(li, "in", x)[0])
        hn0, proj = _norm_mm((x,), w["g_mix"], w["w_in"], None, tm=1024, tn=1152, name="mix_in_fwd")
        qa1, qa4, qa16, qb, qd = _qkprep_fwd(proj, w["qk_gains"], cos, sins, tm=512, name="qkprep_fwd")
        qa = (nat4(qa1), qa4, qa16)
        oa, la = [], []
        for ci in range(3):
            o, l = _band_fwd(qa[ci], A_Q, A_K, A_V, bias_a[ci], None, name=f"band_a{ci}_fwd", **A_HEADS)
            oa.append(o)
            la.append(l)
        oa[0], la[0] = oa[0].reshape(t, 256), la[0].reshape(t, 256)
        ya, lse_a = _combine_a(oa, la, tm=512, name="combine_a")
        yb, lse_b = _band_fwd(nat4(qb), B_Q, B_K, B_V, bias_b, w["sink_t"], name="band_b_fwd", **B_HEADS)
        yb = yb.reshape(t, 256)
        yc = _c_fwd(proj, w["c_g"], w["c_b"], w["c_ws"], w["c_bst"], tm=512, name="c_fwd")
        yd, lse_d = _dense_fwd(qd, tq=256, name="dense_fwd")
        more, started = matmul_weights(li, "rest", yd)
        w.update(more)
        w["out_gain"] = _pin(w["out_gain"], started)
        mixed, x1 = _norm_mm((ya, yb, yc, yd), w["out_gain"], w["w_out"], x, tm=1024, tn=1024, name="mix_out_fwd")
        hn1, h = _norm_mm((x1,), w["g_ffn"], w["w_up"], None, tm=1024, tn=1408, name="ffn_up_fwd", out_dtype=BF16)
        act = _conv_gate_fwd(h, w["conv_w"], w["conv_b"], name="conv_gate_fwd")
        x2 = _mm(act, w["w_down"], "nn", x1, tm=1024, tn=1024, out_dtype=F32, name="ffn_down_fwd")
        hn2, x3, gate, pp = _ple_fwd(x2, w["g_ple"], w["w_gate"], pe, li * (t // 1024), w["w_proj"], tm=1024, tn=512,
                                     name="ple_fwd")
        saved.append(dict(w=w, x0=x, hn0=hn0, proj=proj, qa=qa, qb=qb, qd=qd, ya=ya, lse_a=lse_a, yb=yb, lse_b=lse_b,
                          yc=yc, yd=yd, lse_d=lse_d, mixed=mixed, x1=x1, hn1=hn1, h=h, act=act, x2=x2, hn2=hn2,
                          gate=gate, pp=pp))
        x = x3

    loss_tile, dx = _loss_head(x, tgt, tm=512, name="loss_head")
    grads = [None] * DEPTH
    dbias_a, dbias_bs = [[], [], []], []
    token = None
    for li in reversed(range(DEPTH)):
        s = saved[li]
        w = s["w"]
        g = {}
        w["g_ple"] = _pin(w["g_ple"], token)
        dz, dpp = _ple_bwd_ew(dx, s["gate"], s["pp"], tm=512, name="ple_bwd_ew")
        g["w_gate"] = _mm(s["hn2"], dz, "tn", None, tm=1024, tn=512, out_dtype=BF16, name="dw_gate")
        g["w_proj"] = _mm(pe, dpp, "tn", None, tm=256, tn=1024, out_dtype=BF16, name="dw_proj", a_rows=(li, t))
        dx2, dx2b, g["g_ple"] = _mm_bt_normbwd((dz,), w["w_gate"], (s["x2"],), w["g_ple"], dx, tm=1024, tn=1024,
                                               name="ple_bwd", emit_bf16=True)
        g["w_down"] = _mm(s["act"], dx2b, "tn", None, tm=1408, tn=512, out_dtype=BF16, name="dw_down")
        dact = _mm(dx2b, w["w_down"], "nt", None, tm=1024, tn=1408, out_dtype=BF16, name="ffn_down_bwd")
        dhg, dhu, g["conv_w"], g["conv_b"] = _conv_gate_bwd(s["h"], dact, w["conv_w"], w["conv_b"], name="conv_gate_bwd")
        g["w_up"] = jnp.concatenate(
            [_mm(s["hn1"], dhalf, "tn", None, tm=1024, tn=1408, out_dtype=BF16, name=f"dw_up_{nm}")
             for nm, dhalf in (("gate", dhg), ("up", dhu))], axis=1)
        dx1, dx1b, g["g_ffn"] = _mm_bt_normbwd((dhg, dhu), w["w_up"], (s["x1"],), w["g_ffn"], dx2, tm=1024, tn=1408,
                                               name="ffn_up_bwd", emit_bf16=True)
        g["w_out"] = _mm(s["mixed"], dx1b, "tn", None, tm=1024, tn=512, out_dtype=BF16, name="dw_out")
        out_gain = _pin(w["out_gain"], grads_ready(li, "mid", g))
        dycat, g["out_gain"] = _mm_bt_normbwd((dx1b,), w["w_out"], (s["ya"], s["yb"], s["yc"], s["yd"]), out_gain,
                                              None, tm=1024, tn=1024, name="mix_out_bwd")
        dy_r, lse_r, dl_a, dl_b, dl_d = _deltas(dycat, s["ya"], s["yb"], s["yd"], s["lse_a"], tm=512, name="deltas")
        dy_a = (nat4(dycat),) + tuple(dy_r)
        lse_a = (nat4(s["lse_a"]),) + tuple(lse_r)
        dl_a = (nat4(dl_a[0]),) + tuple(dl_a[1:])
        da = []
        for ci in range(3):
            dq, dk, dv, dbias = _band_bwd(s["qa"][ci], A_Q, A_K, A_V, bias_a[ci], None, dy_a[ci], 0, lse_a[ci],
                                          dl_a[ci], name=f"band_a{ci}_bwd", **A_HEADS)
            if ci == 0:
                dq, dk, dv = (a.reshape(t, 256) for a in (dq, dk, dv))
            da.append((dq, dk, dv))
            dbias_a[ci].append(dbias.reshape(4, -1))
        dqb, dkb, dvb, dbias_b, dsink = _band_bwd(nat4(s["qb"]), B_Q, B_K, B_V, bias_b, w["sink_t"], nat4(dycat), 1,
                                                  nat4(s["lse_b"]), nat4(dl_b), name="band_b_bwd", **B_HEADS)
        dbias_bs.append(dbias_b.reshape(4, -1))
        g["sink"] = dsink[:, 0, 0]
        dd = _dense_bwd(s["qd"], dycat, s["lse_d"], dl_d, tq=256, name="dense_bwd")
        dcu, dcv, g["c_ws"], dbs, g["c_g"], g["c_b"] = _c_bwd(s["proj"], dycat, w["c_g"], w["c_b"], w["c_ws"],
                                                               w["c_wst"], w["c_bst"], tm=512, name="c_bwd")
        g["c_bs"] = dbs[:, ::64].T
        db = (dqb.reshape(t, 256), dkb.reshape(t, 128), dvb.reshape(t, 128))
        dproj, dgains = _qkprep_bwd(s["proj"], da, db, dd, dcu, dcv, w["qk_gains"], cos, sins, tm=512, name="qkprep_bwd")
        g["qk_gain"] = dgains[:6, :64].reshape(3, 2, HEAD_DIM)
        g["w_in"] = _mm(s["hn0"], dproj, "tn", None, tm=1024, tn=1152, out_dtype=BF16, name="dw_in")
        dx, g["g_mix"] = _mm_bt_normbwd((dproj,), w["w_in"], (s["x0"],), w["g_mix"], dx1, tm=1024, tn=1152,
                                        name="mix_in_bwd")
        grads[li] = g
        token = grads_ready(li, "end", g)
    d_table_a = sum(_bucket_reduce(dbias_a[ci], pats_a[ci], name=f"bucket_a{ci}") for ci in range(3))
    d_table_b = _bucket_reduce(dbias_bs, pat_b, name="bucket_b")
    d_rel_bias = jnp.concatenate([d_table_a, d_table_b], axis=0).T
    return loss_tile[0, 0], dx, grads, d_rel_bias


WEIGHT_NAMES = ("rel_bias", "ln_mix_g", "w_in", "qk_gain", "sink", "c_norm_g", "c_norm_b", "c_ws", "c_bs", "out_gain",
                "w_out", "ln_ffn_g", "w_up", "conv_w", "conv_b", "w_down", "ln_ple_g", "w_ple_gate", "w_ple_proj")
COL_SHARDED = ("w_in", "w_up", "w_ple_proj")
ROW_SHARDED = ("w_out", "w_down", "w_ple_gate")
SMALL_SHARDED = ("conv_w", "out_gain")
REPLICATED = tuple(n for n in WEIGHT_NAMES if n not in COL_SHARDED + ROW_SHARDED + SMALL_SHARDED)
LOCAL_GRAD_KEY = {"ln_mix_g": "g_mix", "ln_ffn_g": "g_ffn", "ln_ple_g": "g_ple", "c_norm_g": "c_g", "c_norm_b": "c_b",
                  "w_ple_gate": "w_gate", "w_ple_proj": "w_proj"}


def _full_from_gathered(name, gathered):
    _, r, c = gathered.shape
    if name in ROW_SHARDED:
        return gathered.reshape(N_DEV * r, c)
    return jnp.transpose(gathered, (1, 0, 2)).reshape(r, N_DEV * c)


def _slots_from_full(name, full):
    rows, cols = full.shape
    if name in ROW_SHARDED:
        return full.reshape(N_DEV, rows // N_DEV, cols)
    return jnp.transpose(full.reshape(rows, N_DEV, cols // N_DEV), (1, 0, 2))


def _piece_rows(shape):
    return -(-int(np.prod(shape)) // 1024) * 8


def _pack_rows(arrays):
    pieces = []
    for a in arrays:
        n, rows = int(np.prod(a.shape)), _piece_rows(a.shape)
        flat = a.astype(F32).reshape(-1)
        if n != rows * LANES:
            flat = jnp.pad(flat, (0, rows * LANES - n))
        pieces.append(flat.reshape(rows, LANES))
    return jnp.concatenate(pieces, axis=0)


def _unpack_rows(packed, shapes):
    out, off = [], 0
    for shp in shapes:
        n, rows = int(np.prod(shp)), _piece_rows(shp)
        piece = packed[off:off + rows]
        out.append((piece if n == rows * LANES else piece.reshape(-1)[:n]).reshape(shp))
        off += rows
    return out


def kernel(x, p, rel_bias, ln_mix_g, w_in, qk_gain, sink, c_norm_g, c_norm_b, c_ws, c_bs, out_gain, w_out, ln_ffn_g, w_up, conv_w, conv_b, w_down, ln_ple_g, w_ple_gate, w_ple_proj, loss_target, m_rel_bias, m_ln_mix_g, m_w_in, m_qk_gain, m_sink, m_c_norm_g, m_c_norm_b, m_c_ws, m_c_bs, m_out_gain, m_w_out, m_ln_ffn_g, m_w_up, m_conv_w, m_conv_b, m_w_down, m_ln_ple_g, m_w_ple_gate, m_w_ple_proj, v_rel_bias, v_ln_mix_g, v_w_in, v_qk_gain, v_sink, v_c_norm_g, v_c_norm_b, v_c_ws, v_c_bs, v_out_gain, v_w_out, v_ln_ffn_g, v_w_up, v_conv_w, v_conv_b, v_w_down, v_ln_ple_g, v_w_ple_gate, v_w_ple_proj):
    env = dict(locals())
    wt = {n: env[n] for n in WEIGHT_NAMES}
    mom_m = {n: env["m_" + n] for n in WEIGHT_NAMES}
    mom_v = {n: env["v_" + n] for n in WEIGHT_NAMES}
    bl = x.shape[0]
    t = bl * SEQ
    me = 4 * lax.axis_index("x") + 2 * lax.axis_index("y") + lax.axis_index("c")

    big = COL_SHARDED + ROW_SHARDED
    full = {}
    small_shapes = [wt[n].shape for n in SMALL_SHARDED]
    small = _allgather_vmem(_pack_rows([wt[n] for n in SMALL_SHARDED]), reduce=False, name="gather_small")
    small = small.reshape(N_DEV, -1)
    off = 0
    for n, shp in zip(SMALL_SHARDED, small_shapes):
        cnt = int(np.prod(shp))
        g = small[:, off:off + cnt].reshape((N_DEV,) + tuple(shp))
        full[n] = jnp.transpose(g, (1, 2, 0, 3)).reshape(shp[0], shp[1], N_DEV * shp[2])
        off += _piece_rows(shp) * LANES

    def head_gain(li, a, b, reps):
        g = jnp.tile(qk_gain[li, a, b], reps)
        return jnp.pad(g, (0, 256 - g.shape[0]))

    wts = []
    for li in range(DEPTH):
        rows = [head_gain(li, 0, 0, 4), head_gain(li, 0, 1, 4), head_gain(li, 1, 0, 4), head_gain(li, 1, 1, 2),
                head_gain(li, 2, 0, 4), head_gain(li, 2, 1, 2), jnp.zeros((256,), F32), jnp.zeros((256,), F32)]
        wts.append(dict(
            g_mix=ln_mix_g[li].reshape(1, -1), qk_gains=jnp.stack(rows),
            sink_t=jnp.broadcast_to(sink[li][:, None, None], (4, 8, 128)),
            c_g=c_norm_g[li].reshape(1, -1), c_b=c_norm_b[li].reshape(1, -1), c_ws=c_ws[li].astype(BF16),
            c_wst=jnp.transpose(c_ws[li], (0, 2, 1)).astype(BF16), c_bst=jnp.repeat(c_bs[li].T, 64, axis=1),
            out_gain=full["out_gain"][li].reshape(1, -1), g_ffn=ln_ffn_g[li].reshape(1, -1),
            conv_w=full["conv_w"][li], conv_b=conv_b[li].reshape(1, -1), g_ple=ln_ple_g[li].reshape(1, -1)))

    local_key = {"w_ple_gate": "w_gate", "w_ple_proj": "w_proj"}

    gather_names = {"in": ("w_in",), "rest": tuple(n for n in big if n != "w_in")}
    gathered = {}
    for cid, (li, names) in enumerate(((0, gather_names["in"]), (0, gather_names["rest"]), (1, big))):
        lands = _sc_allgather([wt[n][li].astype(BF16) for n in names], collective_id=cid,
                              name=f"gather_{li}_{len(names)}")
        gathered.setdefault(li, {}).update(zip(names, lands))

    def matmul_weights(li, part, after):
        out = {}
        for n in gather_names[part]:
            g, _ = lax.optimization_barrier((gathered[li][n], after))
            out[local_key.get(n, n)] = _full_from_gathered(n, g)
        return out, None

    mid_names = ("w_ple_gate", "w_ple_proj", "w_down", "w_up", "w_out")
    end_names = ("w_in",)
    landed = {}

    def start_exchange(li, names, g, tag, cid):
        slots = [_slots_from_full(n, g[local_key.get(n, n)]) for n in names]
        lands = _sc_exchange(slots, scatter=True, collective_id=cid, name=f"grads_{li}_{tag}")
        landed.update({(n, li): land for n, land in zip(names, lands)})

    def grads_ready(li, stage, g):
        if li == 0:
            start_exchange(li, mid_names if stage == "mid" else end_names, g, stage, 5 if stage == "mid" else 6)
        elif stage == "end":
            start_exchange(li, mid_names + end_names, g, stage, 4)
        return None

    loss_part, dx, grads, d_rel_bias = _local_step(
        x.reshape(t, D_MODEL), p.reshape(DEPTH * t, PLE_DIM), loss_target.reshape(t, D_MODEL), rel_bias, wts,
        matmul_weights, grads_ready)
    loss = lax.psum(loss_part, ("x", "y", "c"))

    def local_grad(n):
        if n == "rel_bias":
            return d_rel_bias
        key = LOCAL_GRAD_KEY.get(n, n)
        return jnp.stack([grads[li][key].reshape(wt[n].shape[1:]) if n in REPLICATED else grads[li][key]
                          for li in range(DEPTH)])

    out_g, out_d, out_m, out_v = {}, {}, {}, {}
    for n in big:
        shp = wt[n].shape
        two_d = lambda a: a.reshape(-1, shp[-1])
        res = _adamw_reduce([landed[n, li] for li in range(DEPTH)], two_d(wt[n]), two_d(mom_m[n]), two_d(mom_v[n]),
                            tr=32 if n == "w_down" else 128, name="adamw_" + n)
        out_g[n], out_d[n], out_m[n], out_v[n] = [r.reshape(shp) for r in res]

    small_names = REPLICATED + SMALL_SHARDED
    small_full_shapes = [wt[n].shape if n in REPLICATED else full[n].shape for n in small_names]
    reduced = _allgather_vmem(_pack_rows([local_grad(n) for n in small_names]), reduce=True, name="allreduce_small")
    reduced = dict(zip(small_names, _unpack_rows(reduced, small_full_shapes)))
    rep_shapes = [wt[n].shape for n in REPLICATED]
    upd = _adamw_plain(_pack_rows([reduced[n] for n in REPLICATED]), _pack_rows([wt[n] for n in REPLICATED]),
                       _pack_rows([mom_m[n] for n in REPLICATED]), _pack_rows([mom_v[n] for n in REPLICATED]),
                       name="adamw_replicated")
    for dst, packed in zip((out_d, out_m, out_v), upd):
        dst.update(zip(REPLICATED, _unpack_rows(packed, rep_shapes)))
    for n in REPLICATED:
        out_g[n] = reduced[n]
    for n in SMALL_SHARDED:
        shp = wt[n].shape
        g = reduced[n].reshape(shp[0], shp[1], N_DEV, shp[2])
        g = lax.dynamic_index_in_dim(g, me, axis=2, keepdims=False)
        two_d = lambda a: a.reshape(-1, shp[-1])
        res = _adamw_plain(two_d(g), two_d(wt[n]), two_d(mom_m[n]), two_d(mom_v[n]), name="adamw_" + n)
        out_g[n] = g
        out_d[n], out_m[n], out_v[n] = [r.reshape(shp) for r in res]

    return (loss, dx.reshape(bl, SEQ, D_MODEL), *[out_g[n] for n in WEIGHT_NAMES], *[out_d[n] for n in WEIGHT_NAMES],
            *[out_m[n] for n in WEIGHT_NAMES], *[out_v[n] for n in WEIGHT_NAMES])
```

```python
import math

import jax
import jax.numpy as jnp
import numpy as np
from jax import lax
from jax.experimental import pallas as pl
from jax.experimental.pallas import tpu as pltpu
from jax.experimental.pallas import tpu_sc as plsc

F32 = jnp.float32
BF16 = jnp.bfloat16
HI = lax.Precision.HIGHEST

N_DEV = 8
D_MODEL = 1024
SEQ = 2048
DEPTH = 2
HEAD_DIM = 64
IN_WIDTH = 2304
D_FF = 2816
PLE_DIM = 256
C_CHUNK = 128
C_GROUPS = 4
DILATED_CFGS = ((128, 1), (512, 4), (2048, 16))
DILATIONS = tuple(d for _, d in DILATED_CFGS)
A_RADIUS = 64
SWA_RADIUS = 128
BAND_BLOCK = 256
GRID_W = 64
ROPE_THETA = 10000.0
REL_BUCKETS = 32
REL_MAX_DIST = 1024
EPS = 1e-6
NEG_INF = -1e30
ATTN_SCALE = HEAD_DIM ** -0.5
LANES = 128

ADAM_LR = 0.001
ADAM_B1 = 0.9
ADAM_B2 = 0.999
ADAM_EPS = 1e-08
ADAM_WD = 0.01
ADAM_STEP = 10

MESH = pl.DeviceIdType.MESH
NT = (((1,), (1,)), ((), ()))
TN = (((0,), (0,)), ((), ()))
ARB = "arbitrary"
PAR = "parallel"


def _cparams(*sem):
    return pltpu.CompilerParams(dimension_semantics=tuple(sem))


def _sds(shape, dtype):
    return jax.ShapeDtypeStruct(tuple(shape), dtype)


def _group_sum_matrix(n, same_group):
    r = lax.broadcasted_iota(jnp.int32, (n, n), 0)
    c = lax.broadcasted_iota(jnp.int32, (n, n), 1)
    if same_group:
        return ((r >> 6) == (c >> 6)).astype(F32)
    return ((r & 63) == (c & 63)).astype(F32)


def _seg_sum(x, e):
    eb = e.astype(BF16)
    hi = x.astype(BF16)
    lo = (x - hi.astype(F32)).astype(BF16)
    return jnp.dot(hi, eb, preferred_element_type=F32) + jnp.dot(lo, eb, preferred_element_type=F32)


def _gelu(x):
    c = math.sqrt(2.0 / math.pi)
    return 0.5 * x * (1.0 + jnp.tanh(c * (x + 0.044715 * (x * x * x))))


def _gelu_grad(x):
    c = math.sqrt(2.0 / math.pi)
    t = jnp.tanh(c * (x + 0.044715 * (x * x * x)))
    return 0.5 * (1.0 + t) + 0.5 * x * (1.0 - t * t) * c * (1.0 + 3.0 * 0.044715 * (x * x))


def _sigmoid(x):
    return 1.0 / (1.0 + jnp.exp(-x))


def _scatter_cols(scratch, first, val):
    for c in range(val.shape[1] // LANES):
        scratch[first + c] = val[:, c * LANES:(c + 1) * LANES]


def _gather_cols(scratch, first, ncol):
    return jnp.concatenate([scratch[first + c] for c in range(ncol)], axis=1)


def _read_residue(scratch, first, ncol, r, d):
    n = scratch.shape[1] // d
    return jnp.concatenate([scratch.at[first + c][pl.ds(r, n, stride=d), :] for c in range(ncol)], axis=1)


def _write_residue(scratch, first, r, d, val):
    n = scratch.shape[1] // d
    for c in range(val.shape[1] // LANES):
        scratch.at[first + c][pl.ds(r, n, stride=d), :] = val[:, c * LANES:(c + 1) * LANES]


def _norm_mm(xs, gain, w, res, *, tm, tn, name, out_dtype=F32):
    t = xs[0].shape[0]
    k = sum(x.shape[1] for x in xs)
    n = w.shape[1]
    ng = len(xs)
    has_res = res is not None

    def body(*refs):
        x_refs = refs[:ng]
        g_ref, w_ref = refs[ng], refs[ng + 1]
        res_ref = refs[ng + 2] if has_res else None
        hn_ref, o_ref, hn_s = refs[ng + 2 + has_res:]

        @pl.when(pl.program_id(1) == 0)
        def _():
            off = 0
            for xr in x_refs:
                x = xr[...]
                wd = x.shape[1]
                r = lax.rsqrt(jnp.mean(x * x, axis=-1, keepdims=True) + EPS)
                hn_s[:, off:off + wd] = (x * r * g_ref[:, off:off + wd]).astype(BF16)
                off += wd
            hn_ref[...] = hn_s[...]

        acc = jnp.dot(hn_s[...], w_ref[...], preferred_element_type=F32)
        if has_res:
            acc = acc + res_ref[...]
        o_ref[...] = acc.astype(out_dtype)

    in_specs = [pl.BlockSpec((tm, x.shape[1]), lambda i, j: (i, 0)) for x in xs]
    in_specs += [pl.BlockSpec((1, k), lambda i, j: (0, 0)), pl.BlockSpec((k, tn), lambda i, j: (0, j))]
    args = list(xs) + [gain, w]
    if has_res:
        in_specs.append(pl.BlockSpec((tm, tn), lambda i, j: (i, j)))
        args.append(res)
    return pl.pallas_call(
        body, name=name, grid=(t // tm, n // tn), in_specs=in_specs,
        out_specs=[pl.BlockSpec((tm, k), lambda i, j: (i, 0)), pl.BlockSpec((tm, tn), lambda i, j: (i, j))],
        out_shape=[_sds((t, k), BF16), _sds((t, n), out_dtype)],
        scratch_shapes=[pltpu.VMEM((tm, k), BF16)],
        compiler_params=_cparams(PAR, ARB),
    )(*args)


def _mm(a, b, mode, res, *, tm, tn, out_dtype, name, a_rows=None):
    if mode == "tn":
        kk, m = a.shape
        blk_a = 0
        if a_rows is not None:
            blk_a, kk = a_rows
        a_spec = pl.BlockSpec((kk, tm), lambda i, j: (blk_a, i))
    else:
        m, kk = a.shape
        a_spec = pl.BlockSpec((tm, kk), lambda i, j: (i, 0))
    if mode == "nt":
        n = b.shape[0]
        b_spec = pl.BlockSpec((tn, kk), lambda i, j: (j, 0))
    else:
        n = b.shape[1]
        b_spec = pl.BlockSpec((kk, tn), lambda i, j: (0, j))
    has_res = res is not None

    def body(*refs):
        a_ref, b_ref = refs[0], refs[1]
        o_ref = refs[-1]
        av = a_ref[...].astype(BF16)
        bv = b_ref[...].astype(BF16)
        if mode == "nn":
            acc = jnp.dot(av, bv, preferred_element_type=F32)
        elif mode == "nt":
            acc = lax.dot_general(av, bv, NT, preferred_element_type=F32)
        else:
            acc = lax.dot_general(av, bv, TN, preferred_element_type=F32)
        if has_res:
            acc = acc + refs[2][...]
        o_ref[...] = acc.astype(out_dtype)

    in_specs = [a_spec, b_spec]
    args = [a, b]
    if has_res:
        in_specs.append(pl.BlockSpec((tm, tn), lambda i, j: (i, j)))
        args.append(res)
    return pl.pallas_call(
        body, name=name, grid=(m // tm, n // tn), in_specs=in_specs,
        out_specs=pl.BlockSpec((tm, tn), lambda i, j: (i, j)),
        out_shape=_sds((m, n), out_dtype),
        compiler_params=_cparams(PAR, PAR),
    )(*args)


def _mm_bt_normbwd(dys, w, xs, gain, dres, *, tm, tn, name, emit_bf16=False):
    t, wd_each = dys[0].shape
    nd = len(dys)
    per = wd_each // tn
    nj = nd * per
    k = w.shape[0]
    ng = len(xs)
    has_res = dres is not None

    def body(*refs):
        dy_refs = refs[:nd]
        w_ref = refs[nd]
        x_refs = refs[nd + 1:nd + 1 + ng]
        g_ref = refs[nd + 1 + ng]
        dres_ref = refs[nd + 2 + ng] if has_res else None
        outs = refs[nd + 2 + ng + has_res:]
        dx_ref = outs[0]
        dxb_ref = outs[1] if emit_bf16 else None
        dg_ref, acc = outs[1 + emit_bf16:]
        i, j = pl.program_id(0), pl.program_id(1)

        @pl.when(j == 0)
        def _():
            acc[...] = jnp.zeros_like(acc)

        for d, dy_ref in enumerate(dy_refs):
            @pl.when((j >= d * per) & (j < (d + 1) * per))
            def _(dy_ref=dy_ref):
                acc[...] += lax.dot_general(dy_ref[...].astype(BF16), w_ref[...], NT, preferred_element_type=F32)

        @pl.when(j == nj - 1)
        def _():
            @pl.when(i == 0)
            def _():
                dg_ref[...] = jnp.zeros_like(dg_ref)

            off = 0
            for xr in x_refs:
                x = xr[...]
                wd = x.shape[1]
                g = g_ref[:, off:off + wd]
                dyn = acc[:, off:off + wd]
                r = lax.rsqrt(jnp.mean(x * x, axis=-1, keepdims=True) + EPS)
                gdy = dyn * g
                dx = r * gdy - x * (r * r * r * jnp.mean(gdy * x, axis=-1, keepdims=True))
                if has_res:
                    dx = dx + dres_ref[:, off:off + wd]
                dx_ref[:, off:off + wd] = dx
                if emit_bf16:
                    dxb_ref[:, off:off + wd] = dx.astype(BF16)
                dg_ref[:, off:off + wd] += jnp.sum(dyn * x * r, axis=0, keepdims=True)
                off += wd

    def dy_map(d):
        return lambda i, j: (i, jnp.clip(j - d * per, 0, per - 1))

    in_specs = [pl.BlockSpec((tm, tn), dy_map(d)) for d in range(nd)]
    in_specs.append(pl.BlockSpec((k, tn), lambda i, j: (0, j)))
    in_specs += [pl.BlockSpec((tm, x.shape[1]), lambda i, j: (i, 0)) for x in xs]
    in_specs.append(pl.BlockSpec((1, k), lambda i, j: (0, 0)))
    args = list(dys) + [w] + list(xs) + [gain]
    if has_res:
        in_specs.append(pl.BlockSpec((tm, k), lambda i, j: (i, 0)))
        args.append(dres)
    row = pl.BlockSpec((tm, k), lambda i, j: (i, 0))
    out_specs = [row] + ([row] if emit_bf16 else []) + [pl.BlockSpec((1, k), lambda i, j: (0, 0))]
    out_shape = [_sds((t, k), F32)] + ([_sds((t, k), BF16)] if emit_bf16 else []) + [_sds((1, k), F32)]
    return pl.pallas_call(
        body, name=name, grid=(t // tm, nj), in_specs=in_specs, out_specs=out_specs, out_shape=out_shape,
        scratch_shapes=[pltpu.VMEM((tm, k), F32)],
        compiler_params=_cparams(ARB, ARB),
    )(*args)


def _rope_partner(y):
    n = y.shape[1]
    lane = lax.broadcasted_iota(jnp.int32, y.shape, 1)
    return jnp.where((lane & 31) < 16, pltpu.roll(y, n - 16, 1), pltpu.roll(y, 16, 1))


def _residue_specs(tm, width, nt):
    specs = [pl.BlockSpec((tm, width), lambda b, i: (b * nt + i, 0))]
    for d in DILATIONS[1:]:
        specs.append(pl.BlockSpec((None, d, tm // d, width), lambda b, i: (b, 0, i, 0)))
    return specs


def _residue_shapes(bl, width, dtype):
    return [_sds((bl * SEQ, width), dtype)] + [_sds((bl, d, SEQ // d, width), dtype) for d in DILATIONS[1:]]


def _qkprep_fwd(proj, gains, cos, sins, *, tm, name):
    t = proj.shape[0]
    bl = t // SEQ
    nt = SEQ // tm

    def body(p_ref, g_ref, c_ref, s_ref, qa1_ref, qa4_ref, qa16_ref, qb_ref, qd_ref, scr):
        e = _group_sum_matrix(256, True)

        def hn(x, row):
            wd = x.shape[1]
            ms = _seg_sum(x * x, e[:wd, :wd]) * (1.0 / HEAD_DIM)
            return x * lax.rsqrt(ms + EPS) * g_ref[row:row + 1, :wd]

        qa = jnp.concatenate([hn(p_ref[:, 0:256], 0) * ATTN_SCALE, hn(p_ref[:, 256:512], 1), p_ref[:, 512:768]], axis=1)
        qa1_ref[...] = qa.astype(BF16)
        _scatter_cols(scr, 0, qa)
        for d, ref in ((4, qa4_ref), (16, qa16_ref)):
            for r in range(d):
                ref[r] = _read_residue(scr, 0, 6, r, d).astype(BF16)
        qb_ref[:, 0:256] = (hn(p_ref[:, 768:1024], 2) * ATTN_SCALE).astype(BF16)
        qb_ref[:, 256:384] = hn(p_ref[:, 1024:1152], 3).astype(BF16)
        qb_ref[:, 384:512] = p_ref[:, 1152:1280].astype(BF16)
        yq = hn(p_ref[:, 1792:2048], 4)
        yq = yq * c_ref[...] + _rope_partner(yq) * s_ref[...]
        qd_ref[:, 0:256] = (yq * ATTN_SCALE).astype(BF16)
        yk = hn(p_ref[:, 2048:2176], 5)
        yk = yk * c_ref[:, 0:128] + _rope_partner(yk) * s_ref[:, 0:128]
        qd_ref[:, 256:384] = yk.astype(BF16)
        qd_ref[:, 384:512] = p_ref[:, 2176:2304].astype(BF16)

    row = lambda width: pl.BlockSpec((tm, width), lambda b, i: (b * nt + i, 0))
    tab = pl.BlockSpec((tm, 256), lambda b, i: (i, 0))
    return pl.pallas_call(
        body, name=name, grid=(bl, nt),
        in_specs=[row(IN_WIDTH), pl.BlockSpec((8, 256), lambda b, i: (0, 0)), tab, tab],
        out_specs=_residue_specs(tm, 768, nt) + [row(512), row(512)],
        out_shape=_residue_shapes(bl, 768, BF16) + [_sds((t, 512), BF16), _sds((t, 512), BF16)],
        scratch_shapes=[pltpu.VMEM((6, tm, LANES), F32)],
        compiler_params=_cparams(PAR, PAR),
    )(proj, gains, cos, sins)


def _qkprep_bwd(proj, da, db, dd, dcu, dcv, gains, cos, sins, *, tm, name):
    t = proj.shape[0]
    bl = t // SEQ
    nt = SEQ // tm
    flat = [a for cfg in da for a in cfg] + list(db) + list(dd) + [dcu, dcv]

    def body(*refs):
        p_ref, g_ref, c_ref, s_ref = refs[:4]
        d_refs = refs[4:4 + len(flat)]
        dp_ref, dg_ref, scr = refs[4 + len(flat):]
        a_refs = d_refs[:9]
        dqb_ref, dkb_ref, dvb_ref, dqd_ref, dkd_ref, dvd_ref, dcu_ref, dcv_ref = d_refs[9:]
        e = _group_sum_matrix(256, True)
        first = (pl.program_id(0) == 0) & (pl.program_id(1) == 0)
        last = (pl.program_id(0) == bl - 1) & (pl.program_id(1) == nt - 1)

        @pl.when(first)
        def _():
            dg_ref[...] = jnp.zeros_like(dg_ref)

        def hn_bwd(x, dy, row):
            wd = x.shape[1]
            ee = e[:wd, :wd]
            g = g_ref[row:row + 1, :wd]
            r = lax.rsqrt(_seg_sum(x * x, ee) * (1.0 / HEAD_DIM) + EPS)
            gdy = dy * g
            dx = r * gdy - x * (r * r * r * (_seg_sum(gdy * x, ee) * (1.0 / HEAD_DIM)))
            dg_ref[row:row + 1, :wd] += jnp.sum(dy * x * r, axis=0, keepdims=True)
            return dx

        def rope_bwd(dy, wd):
            return dy * c_ref[:, :wd] + _rope_partner(dy * s_ref[:, :wd])

        dqkv = jnp.concatenate([a_refs[0][...], a_refs[1][...], a_refs[2][...]], axis=1)
        for ci, d in ((1, 4), (2, 16)):
            for r in range(d):
                part = jnp.concatenate([a_refs[3 * ci + m][r] for m in range(3)], axis=1)
                _write_residue(scr, 0, r, d, part)
            dqkv = dqkv + _gather_cols(scr, 0, 6)
        dp_ref[:, 0:256] = hn_bwd(p_ref[:, 0:256], dqkv[:, 0:256] * ATTN_SCALE, 0).astype(BF16)
        dp_ref[:, 256:512] = hn_bwd(p_ref[:, 256:512], dqkv[:, 256:512], 1).astype(BF16)
        dp_ref[:, 512:768] = dqkv[:, 512:768].astype(BF16)
        dp_ref[:, 768:1024] = hn_bwd(p_ref[:, 768:1024], dqb_ref[...] * ATTN_SCALE, 2).astype(BF16)
        dp_ref[:, 1024:1152] = hn_bwd(p_ref[:, 1024:1152], dkb_ref[...], 3).astype(BF16)
        dp_ref[:, 1152:1280] = dvb_ref[...].astype(BF16)
        dp_ref[:, 1280:1536] = dcu_ref[...].astype(BF16)
        dp_ref[:, 1536:1792] = dcv_ref[...].astype(BF16)
        dp_ref[:, 1792:2048] = hn_bwd(p_ref[:, 1792:2048], rope_bwd(dqd_ref[...] * ATTN_SCALE, 256), 4).astype(BF16)
        dp_ref[:, 2048:2176] = hn_bwd(p_ref[:, 2048:2176], rope_bwd(dkd_ref[...], 128), 5).astype(BF16)
        dp_ref[:, 2176:2304] = dvd_ref[...].astype(BF16)

        @pl.when(last)
        def _():
            dg_ref[...] = _seg_sum(dg_ref[...], _group_sum_matrix(256, False))

    row = lambda width: pl.BlockSpec((tm, width), lambda b, i: (b * nt + i, 0))
    tab = pl.BlockSpec((tm, 256), lambda b, i: (i, 0))
    in_specs = [row(IN_WIDTH), pl.BlockSpec((8, 256), lambda b, i: (0, 0)), tab, tab]
    res_specs = _residue_specs(tm, 256, nt)
    in_specs += [res_specs[ci] for ci in range(3) for _ in range(3)]
    in_specs += [row(a.shape[1]) for a in flat[9:]]
    return pl.pallas_call(
        body, name=name, grid=(bl, nt), in_specs=in_specs,
        out_specs=[row(IN_WIDTH), pl.BlockSpec((8, 256), lambda b, i: (0, 0))],
        out_shape=[_sds((t, IN_WIDTH), BF16), _sds((8, 256), F32)],
        scratch_shapes=[pltpu.VMEM((6, tm, LANES), F32)],
        compiler_params=_cparams(ARB, ARB),
    )(proj, gains, cos, sins, *flat)


BAND_RESIDUES = 4


def _band_spec(seq_len, spec, rb):
    width, idx = spec
    return pl.BlockSpec((None, rb, seq_len, width), lambda b, r: (b, r, 0, idx))


def _fill_padded(dst, src_ref, rad, seq_len):
    z = jnp.zeros((rad, dst.shape[1]), dst.dtype)
    dst[0:rad, :] = z
    dst[rad + seq_len:rad + seq_len + rad, :] = z
    dst[rad:rad + seq_len, :] = src_ref[...]


def _band_fwd(src, qs, ks, vs, bias, sink, *, rad, nh, nkv, name):
    bl, dil, sl, _ = src.shape
    blk = bias.shape[1]
    kw = blk + 2 * rad
    nb = sl // blk
    rep = nh // nkv
    has_sink = sink is not None
    rb = min(dil, BAND_RESIDUES)

    def body(*refs):
        q_all, k_all, v_all, b_ref = refs[:4]
        s_ref = refs[4] if has_sink else None
        o_all, l_all, kp, vp = refs[4 + has_sink:]
        for ri in range(rb):
            one_sequence(q_all.at[ri], k_all.at[ri], v_all.at[ri], b_ref, s_ref, o_all.at[ri], l_all.at[ri], kp, vp)

    def one_sequence(q_ref, k_ref, v_ref, b_ref, s_ref, o_ref, l_ref, kp, vp):
        _fill_padded(kp, k_ref, rad, sl)
        _fill_padded(vp, v_ref, rad, sl)

        def blk_body(i, carry):
            r0 = pl.multiple_of(i * blk, blk)
            qb = q_ref[pl.ds(r0, blk), :]
            kwin = kp[pl.ds(r0, kw), :]
            vwin = vp[pl.ds(r0, kw), :]
            col = r0 - rad + lax.broadcasted_iota(jnp.int32, (blk, kw), 1)
            neg = jnp.where((col >= 0) & (col < sl), 0.0, NEG_INF).astype(F32)
            for h in range(nh):
                g = h // rep
                hs = slice(h * HEAD_DIM, (h + 1) * HEAD_DIM)
                gs = slice(g * HEAD_DIM, (g + 1) * HEAD_DIM)
                s = lax.dot_general(qb[:, hs], kwin[:, gs], NT, preferred_element_type=F32)
                s = s + b_ref[h] + neg
                m = jnp.max(s, axis=1, keepdims=True)
                if has_sink:
                    sk = s_ref[h][0:1, 0:1]
                    m = jnp.maximum(m, sk)
                p = jnp.exp(s - m)
                den = jnp.sum(p, axis=1, keepdims=True)
                if has_sink:
                    den = den + jnp.exp(sk - m)
                o = jnp.dot(p.astype(BF16), vwin[:, gs], preferred_element_type=F32) / den
                o_ref[pl.ds(r0, blk), hs] = o
                l_ref[pl.ds(r0, blk), hs] = jnp.broadcast_to(m + jnp.log(den), (blk, HEAD_DIM))
            return carry

        lax.fori_loop(0, nb, blk_body, 0)

    in_specs = [_band_spec(sl, qs, rb), _band_spec(sl, ks, rb), _band_spec(sl, vs, rb),
                pl.BlockSpec((nh, blk, kw), lambda b, r: (0, 0, 0))]
    args = [src] * 3 + [bias]
    if has_sink:
        in_specs.append(pl.BlockSpec((nh, 8, 128), lambda b, r: (0, 0, 0)))
        args.append(sink)
    return pl.pallas_call(
        body, name=name, grid=(bl, dil // rb), in_specs=in_specs,
        out_specs=[_band_spec(sl, (256, 0), rb)] * 2,
        out_shape=[_sds((bl, dil, sl, 256), F32)] * 2,
        scratch_shapes=[pltpu.VMEM((sl + 2 * rad, ks[0]), BF16), pltpu.VMEM((sl + 2 * rad, vs[0]), BF16)],
        compiler_params=_cparams(PAR, PAR),
    )(*args)


def _band_bwd(src, qs, ks, vs, bias, sink, dy, dcol, lse, delta, *, rad, nh, nkv, name):
    bl, dil, sl, _ = src.shape
    blk = bias.shape[1]
    kw = blk + 2 * rad
    nb = sl // blk
    rep = nh // nkv
    has_sink = sink is not None
    rb = min(dil, BAND_RESIDUES)
    wk, wv = ks[0], vs[0]

    def body(*refs):
        q_all, k_all, v_all, b_ref = refs[:4]
        s_ref = refs[4] if has_sink else None
        do_all, l_all, dl_all = refs[4 + has_sink:7 + has_sink]
        outs = refs[7 + has_sink:]
        dsk_ref = None
        if has_sink:
            dq_all, dk_all, dv_all, db_ref, dsk_ref, kp, vp, dka, dva = outs
        else:
            dq_all, dk_all, dv_all, db_ref, kp, vp, dka, dva = outs

        @pl.when((pl.program_id(0) == 0) & (pl.program_id(1) == 0))
        def _():
            db_ref[...] = jnp.zeros_like(db_ref)
            if has_sink:
                dsk_ref[...] = jnp.zeros_like(dsk_ref)

        for ri in range(rb):
            one_sequence(q_all.at[ri], k_all.at[ri], v_all.at[ri], b_ref, s_ref, do_all.at[ri], l_all.at[ri],
                         dl_all.at[ri], dq_all.at[ri], dk_all.at[ri], dv_all.at[ri], db_ref, dsk_ref, kp, vp, dka, dva)

    def one_sequence(q_ref, k_ref, v_ref, b_ref, s_ref, do_ref, l_ref, dl_ref, dq_ref, dk_ref, dv_ref, db_ref, dsk_ref,
                     kp, vp, dka, dva):
        _fill_padded(kp, k_ref, rad, sl)
        _fill_padded(vp, v_ref, rad, sl)
        dka[...] = jnp.zeros_like(dka)
        dva[...] = jnp.zeros_like(dva)

        def blk_body(i, carry):
            r0 = pl.multiple_of(i * blk, blk)
            qb = q_ref[pl.ds(r0, blk), :]
            kwin = kp[pl.ds(r0, kw), :]
            vwin = vp[pl.ds(r0, kw), :]
            dob = do_ref[pl.ds(r0, blk), :].astype(BF16)
            lb = l_ref[pl.ds(r0, blk), :]
            dlb = dl_ref[pl.ds(r0, blk), :]
            col = r0 - rad + lax.broadcasted_iota(jnp.int32, (blk, kw), 1)
            neg = jnp.where((col >= 0) & (col < sl), 0.0, NEG_INF).astype(F32)
            for h in range(nh):
                g = h // rep
                hs = slice(h * HEAD_DIM, (h + 1) * HEAD_DIM)
                gs = slice(g * HEAD_DIM, (g + 1) * HEAD_DIM)
                qh, kh, vh, doh = qb[:, hs], kwin[:, gs], vwin[:, gs], dob[:, hs]
                lh = lb[:, h * HEAD_DIM:h * HEAD_DIM + 1]
                dlh = dlb[:, h * HEAD_DIM:h * HEAD_DIM + 1]
                s = lax.dot_general(qh, kh, NT, preferred_element_type=F32) + b_ref[h] + neg
                p = jnp.exp(s - lh)
                dp = lax.dot_general(doh, vh, NT, preferred_element_type=F32)
                ds = p * (dp - dlh)
                dsb = ds.astype(BF16)
                dq_ref[pl.ds(r0, blk), hs] = jnp.dot(dsb, kh, preferred_element_type=F32)
                dka[pl.ds(r0, kw), gs] += lax.dot_general(dsb, qh, TN, preferred_element_type=F32)
                dva[pl.ds(r0, kw), gs] += lax.dot_general(p.astype(BF16), doh, TN, preferred_element_type=F32)
                db_ref[h] += ds
                if has_sink:
                    ps = jnp.exp(s_ref[h][0:1, 0:1] - lh)
                    dsk_ref[h] += jnp.broadcast_to(-jnp.sum(ps * dlh, axis=0, keepdims=True), (8, 128))
            return carry

        lax.fori_loop(0, nb, blk_body, 0)
        dk_ref[...] = dka[rad:rad + sl, :]
        dv_ref[...] = dva[rad:rad + sl, :]

    const3 = lambda b, r: (0, 0, 0)
    in_specs = [_band_spec(sl, qs, rb), _band_spec(sl, ks, rb), _band_spec(sl, vs, rb),
                pl.BlockSpec((nh, blk, kw), const3)]
    args = [src] * 3 + [bias]
    if has_sink:
        in_specs.append(pl.BlockSpec((nh, 8, 128), const3))
        args.append(sink)
    row = _band_spec(sl, (256, 0), rb)
    in_specs += [_band_spec(sl, (256, dcol), rb), row, row]
    args += [dy, lse, delta]
    out_specs = [row, _band_spec(sl, (wk, 0), rb), _band_spec(sl, (wv, 0), rb), pl.BlockSpec((nh, blk, kw), const3)]
    out_shape = [_sds((bl, dil, sl, 256), F32), _sds((bl, dil, sl, wk), F32), _sds((bl, dil, sl, wv), F32),
                 _sds((nh, blk, kw), F32)]
    if has_sink:
        out_specs.append(pl.BlockSpec((nh, 8, 128), const3))
        out_shape.append(_sds((nh, 8, 128), F32))
    return pl.pallas_call(
        body, name=name, grid=(bl, dil // rb), in_specs=in_specs, out_specs=out_specs, out_shape=out_shape,
        scratch_shapes=[pltpu.VMEM((sl + 2 * rad, wk), BF16), pltpu.VMEM((sl + 2 * rad, wv), BF16),
                        pltpu.VMEM((sl + 2 * rad, wk), F32), pltpu.VMEM((sl + 2 * rad, wv), F32)],
        compiler_params=_cparams(ARB, ARB),
    )(*args)


def _combine_a(os_, ls_, *, tm, name):
    bl = os_[1].shape[0]
    t = bl * SEQ
    nt = SEQ // tm

    def body(o1, o4, o16, l1, l4, l16, y_ref, lt_ref, scr):
        for k, (d, ref) in enumerate(((4, o4), (16, o16), (4, l4), (16, l16))):
            for r in range(d):
                _write_residue(scr, 2 * k, r, d, ref[r])
        o2, o3, b, c = (_gather_cols(scr, 2 * k, 2) for k in range(4))
        a = l1[...]
        m = jnp.maximum(jnp.maximum(a, b), c)
        ea, eb, ec = jnp.exp(a - m), jnp.exp(b - m), jnp.exp(c - m)
        den = ea + eb + ec
        y_ref[...] = (ea / den) * o1[...] + (eb / den) * o2 + (ec / den) * o3
        lt_ref[...] = m + jnp.log(den)

    specs = _residue_specs(tm, 256, nt)
    return pl.pallas_call(
        body, name=name, grid=(bl, nt), in_specs=specs * 2, out_specs=[specs[0]] * 2,
        out_shape=[_sds((t, 256), F32)] * 2, scratch_shapes=[pltpu.VMEM((8, tm, LANES), F32)],
        compiler_params=_cparams(PAR, PAR),
    )(*os_, *ls_)


def _deltas(dycat, ya, yb, yd, lse_a, *, tm, name):
    t = ya.shape[0]
    bl = t // SEQ
    nt = SEQ // tm

    def body(dy_ref, ya_ref, yb_ref, yd_ref, la_ref, dy4, dy16, l4, l16, da1, da4, da16, db_ref, dd_ref, scr):
        e = _group_sum_matrix(256, True)
        dya = dy_ref[:, 0:256]
        dla = _seg_sum(dya * ya_ref[...], e)
        da1[...] = dla
        db_ref[...] = _seg_sum(dy_ref[:, 256:512] * yb_ref[...], e)
        dd_ref[...] = _seg_sum(dy_ref[:, 768:1024] * yd_ref[...], e)
        for k, (val, r4, r16) in enumerate(((dya, dy4, dy16), (la_ref[...], l4, l16), (dla, da4, da16))):
            _scatter_cols(scr, 2 * k, val)
            for d, ref in ((4, r4), (16, r16)):
                for r in range(d):
                    ref[r] = _read_residue(scr, 2 * k, 2, r, d)

    specs = _residue_specs(tm, 256, nt)
    nat = specs[0]
    shapes = _residue_shapes(bl, 256, F32)
    outs = pl.pallas_call(
        body, name=name, grid=(bl, nt),
        in_specs=[pl.BlockSpec((tm, 1024), lambda b, i: (b * nt + i, 0)), nat, nat, nat, nat],
        out_specs=specs[1:] + specs[1:] + specs + [nat, nat],
        out_shape=shapes[1:] + shapes[1:] + shapes + [shapes[0], shapes[0]],
        scratch_shapes=[pltpu.VMEM((6, tm, LANES), F32)],
        compiler_params=_cparams(PAR, PAR),
    )(dycat, ya, yb, yd, lse_a)
    return outs[0:2], outs[2:4], outs[4:7], outs[7], outs[8]


def _dense_fwd(qd, *, tq, name):
    t = qd.shape[0]
    bl = t // SEQ
    nq = SEQ // tq

    def body(q_ref, k_ref, v_ref, o_ref, l_ref):
        q = q_ref[...]
        for g in range(2):
            h0, h1 = 2 * g, 2 * g + 1
            q2 = jnp.concatenate([q[:, h0 * 64:(h0 + 1) * 64], q[:, h1 * 64:(h1 + 1) * 64]], axis=0)
            kg = k_ref[:, g * 64:(g + 1) * 64]
            vg = v_ref[:, g * 64:(g + 1) * 64]
            s = lax.dot_general(q2, kg, NT, preferred_element_type=F32)
            m = jnp.max(s, axis=1, keepdims=True)
            p = jnp.exp(s - m)
            den = jnp.sum(p, axis=1, keepdims=True)
            o2 = jnp.dot(p.astype(BF16), vg, preferred_element_type=F32) / den
            l2 = jnp.broadcast_to(m + jnp.log(den), (2 * tq, 64))
            o_ref[:, h0 * 64:(h0 + 1) * 64] = o2[:tq]
            o_ref[:, h1 * 64:(h1 + 1) * 64] = o2[tq:]
            l_ref[:, h0 * 64:(h0 + 1) * 64] = l2[:tq]
            l_ref[:, h1 * 64:(h1 + 1) * 64] = l2[tq:]

    q3 = qd.reshape(bl, SEQ, 512)
    o, lse = pl.pallas_call(
        body, name=name, grid=(bl, nq),
        in_specs=[pl.BlockSpec((None, tq, 256), lambda b, i: (b, i, 0)),
                  pl.BlockSpec((None, SEQ, 128), lambda b, i: (b, 0, 2)),
                  pl.BlockSpec((None, SEQ, 128), lambda b, i: (b, 0, 3))],
        out_specs=[pl.BlockSpec((None, tq, 256), lambda b, i: (b, i, 0))] * 2,
        out_shape=[_sds((bl, SEQ, 256), F32)] * 2,
        compiler_params=_cparams(PAR, PAR),
    )(q3, q3, q3)
    return o.reshape(t, 256), lse.reshape(t, 256)


def _dense_bwd(qd, dycat, lse, delta, *, tq, name):
    t = qd.shape[0]
    bl = t // SEQ
    nq = SEQ // tq

    def body(q_ref, k_ref, v_ref, do_ref, l_ref, dl_ref, dq_ref, dk_ref, dv_ref, dkt, dvt):
        @pl.when(pl.program_id(1) == 0)
        def _():
            dkt[...] = jnp.zeros_like(dkt)
            dvt[...] = jnp.zeros_like(dvt)

        q = q_ref[...]
        do = do_ref[...].astype(BF16)
        lv = l_ref[...]
        dlv = dl_ref[...]
        for g in range(2):
            h0, h1 = 2 * g, 2 * g + 1
            q2 = jnp.concatenate([q[:, h0 * 64:(h0 + 1) * 64], q[:, h1 * 64:(h1 + 1) * 64]], axis=0)
            do2 = jnp.concatenate([do[:, h0 * 64:(h0 + 1) * 64], do[:, h1 * 64:(h1 + 1) * 64]], axis=0)
            l2 = jnp.concatenate([lv[:, h0 * 64:h0 * 64 + 1], lv[:, h1 * 64:h1 * 64 + 1]], axis=0)
            dl2 = jnp.concatenate([dlv[:, h0 * 64:h0 * 64 + 1], dlv[:, h1 * 64:h1 * 64 + 1]], axis=0)
            kg = k_ref[:, g * 64:(g + 1) * 64]
            vg = v_ref[:, g * 64:(g + 1) * 64]
            s = lax.dot_general(q2, kg, NT, preferred_element_type=F32)
            p = jnp.exp(s - l2)
            dp = lax.dot_general(do2, vg, NT, preferred_element_type=F32)
            ds = (p * (dp - dl2)).astype(BF16)
            dq2 = jnp.dot(ds, kg, preferred_element_type=F32)
            dq_ref[:, h0 * 64:(h0 + 1) * 64] = dq2[:tq]
            dq_ref[:, h1 * 64:(h1 + 1) * 64] = dq2[tq:]
            dkt[g * 64:(g + 1) * 64, :] += lax.dot_general(q2, ds, TN, preferred_element_type=F32)
            dvt[g * 64:(g + 1) * 64, :] += lax.dot_general(do2, p.astype(BF16), TN, preferred_element_type=F32)

        @pl.when(pl.program_id(1) == nq - 1)
        def _():
            dk_ref[...] = dkt[...].T
            dv_ref[...] = dvt[...].T

    q3 = qd.reshape(bl, SEQ, 512)
    tile = pl.BlockSpec((None, tq, 256), lambda b, i: (b, i, 0))
    full = pl.BlockSpec((None, SEQ, 128), lambda b, i: (b, 0, 0))
    dq, dk, dv = pl.pallas_call(
        body, name=name, grid=(bl, nq),
        in_specs=[tile, pl.BlockSpec((None, SEQ, 128), lambda b, i: (b, 0, 2)),
                  pl.BlockSpec((None, SEQ, 128), lambda b, i: (b, 0, 3)),
                  pl.BlockSpec((None, tq, 256), lambda b, i: (b, i, 3)), tile, tile],
        out_specs=[tile, full, full],
        out_shape=[_sds((bl, SEQ, 256), F32), _sds((bl, SEQ, 128), F32), _sds((bl, SEQ, 128), F32)],
        scratch_shapes=[pltpu.VMEM((128, SEQ), F32), pltpu.VMEM((128, SEQ), F32)],
        compiler_params=_cparams(PAR, ARB),
    )(q3, q3, q3, dycat.reshape(bl, SEQ, 1024), lse.reshape(bl, SEQ, 256), delta.reshape(bl, SEQ, 256))
    return dq.reshape(t, 256), dk.reshape(t, 128), dv.reshape(t, 128)


def _c_norm(cv, gam, bet):
    vg = _gelu(cv)
    mu = jnp.mean(vg, axis=-1, keepdims=True)
    xc = vg - mu
    r = lax.rsqrt(jnp.mean(xc * xc, axis=-1, keepdims=True) + EPS)
    xhat = xc * r
    return xhat * gam + bet, xhat, r


def _c_fwd(proj, gam, bet, ws, bst, *, tm, name):
    t = proj.shape[0]
    nch = tm // C_CHUNK

    def body(u_ref, v_ref, g_ref, b_ref, ws_ref, bs_ref, y_ref):
        vn, _, _ = _c_norm(v_ref[...], g_ref[...], b_ref[...])
        vnb = vn.astype(BF16)
        for c in range(nch):
            rows = slice(c * C_CHUNK, (c + 1) * C_CHUNK)
            for g in range(C_GROUPS):
                gs = slice(g * 64, (g + 1) * 64)
                mixed = jnp.dot(ws_ref[g], vnb[rows, gs], preferred_element_type=F32) + bs_ref[:, gs]
                y_ref[rows, gs] = _gelu(u_ref[rows, gs]) * mixed

    vec = pl.BlockSpec((1, 256), lambda i: (0, 0))
    return pl.pallas_call(
        body, name=name, grid=(t // tm,),
        in_specs=[pl.BlockSpec((tm, 256), lambda i: (i, 5)), pl.BlockSpec((tm, 256), lambda i: (i, 6)), vec, vec,
                  pl.BlockSpec((C_GROUPS, C_CHUNK, C_CHUNK), lambda i: (0, 0, 0)),
                  pl.BlockSpec((C_CHUNK, 256), lambda i: (0, 0))],
        out_specs=pl.BlockSpec((tm, 256), lambda i: (i, 0)), out_shape=_sds((t, 256), F32),
        compiler_params=_cparams(PAR),
    )(proj, proj, gam, bet, ws, bst)


def _c_bwd(proj, dycat, gam, bet, ws, wst, bst, *, tm, name):
    t = proj.shape[0]
    nch = tm // C_CHUNK
    nstep = t // tm

    def body(u_ref, v_ref, dy_ref, g_ref, b_ref, ws_ref, wst_ref, bs_ref,
             du_ref, dv_ref, dws_ref, dbs_ref, dg_ref, db_ref, dvn_s):
        step = pl.program_id(0)

        @pl.when(step == 0)
        def _():
            dws_ref[...] = jnp.zeros_like(dws_ref)
            dbs_ref[...] = jnp.zeros_like(dbs_ref)
            dg_ref[...] = jnp.zeros_like(dg_ref)
            db_ref[...] = jnp.zeros_like(db_ref)

        cv = v_ref[...]
        gam_v = g_ref[...]
        vn, xhat, r = _c_norm(cv, gam_v, b_ref[...])
        vnb = vn.astype(BF16)
        for c in range(nch):
            rows = slice(c * C_CHUNK, (c + 1) * C_CHUNK)
            for g in range(C_GROUPS):
                gs = slice(g * 64, (g + 1) * 64)
                cu = u_ref[rows, gs]
                dy = dy_ref[rows, gs]
                mixed = jnp.dot(ws_ref[g], vnb[rows, gs], preferred_element_type=F32) + bs_ref[:, gs]
                du_ref[rows, gs] = dy * mixed * _gelu_grad(cu)
                dmix = dy * _gelu(cu)
                dbs_ref[:, gs] += dmix
                dmb = dmix.astype(BF16)
                dws_ref[g] += lax.dot_general(dmb, vnb[rows, gs], NT, preferred_element_type=F32)
                dvn_s[rows, gs] = jnp.dot(wst_ref[g], dmb, preferred_element_type=F32)
        dvn = dvn_s[...]
        dg_ref[...] += jnp.sum(dvn * xhat, axis=0, keepdims=True)
        db_ref[...] += jnp.sum(dvn, axis=0, keepdims=True)
        dxh = dvn * gam_v
        dvg = r * (dxh - jnp.mean(dxh, axis=-1, keepdims=True) - xhat * jnp.mean(dxh * xhat, axis=-1, keepdims=True))
        dv_ref[...] = dvg * _gelu_grad(cv)

        @pl.when(step == nstep - 1)
        def _():
            dbs_ref[...] = _seg_sum(dbs_ref[...], _group_sum_matrix(256, True))

    vec = pl.BlockSpec((1, 256), lambda i: (0, 0))
    mat = pl.BlockSpec((C_GROUPS, C_CHUNK, C_CHUNK), lambda i: (0, 0, 0))
    bsp = pl.BlockSpec((C_CHUNK, 256), lambda i: (0, 0))
    tile = pl.BlockSpec((tm, 256), lambda i: (i, 0))
    return pl.pallas_call(
        body, name=name, grid=(nstep,),
        in_specs=[pl.BlockSpec((tm, 256), lambda i: (i, 5)), pl.BlockSpec((tm, 256), lambda i: (i, 6)),
                  pl.BlockSpec((tm, 256), lambda i: (i, 2)), vec, vec, mat, mat, bsp],
        out_specs=[tile, tile, mat, bsp, vec, vec],
        out_shape=[_sds((t, 256), F32), _sds((t, 256), F32), _sds((C_GROUPS, C_CHUNK, C_CHUNK), F32),
                   _sds((C_CHUNK, 256), F32), _sds((1, 256), F32), _sds((1, 256), F32)],
        scratch_shapes=[pltpu.VMEM((tm, 256), F32)],
        compiler_params=_cparams(ARB),
    )(proj, proj, dycat, gam, bet, ws, wst, bst)


FF_TC = 128
FF_NB = D_FF // FF_TC
FF_CH = 64
FF_HALO = 16


def _taps(ref, r0, win, where):
    z = jnp.zeros((FF_HALO, win.shape[1]), F32)
    if where == "first":
        win[0:FF_HALO, :] = z
        win[FF_HALO:, :] = ref[0:FF_CH + FF_HALO, :].astype(F32)
    elif where == "last":
        win[0:FF_CH + FF_HALO, :] = ref[SEQ - FF_CH - FF_HALO:SEQ, :].astype(F32)
        win[FF_CH + FF_HALO:, :] = z
    else:
        win[...] = ref[pl.ds(pl.multiple_of(r0 - FF_HALO, FF_HALO), FF_CH + 2 * FF_HALO), :].astype(F32)
    return tuple(win[FF_HALO + o:FF_HALO + o + FF_CH, :] for o in (-1, 0, 1))


def _chunk_loop(step):
    step(0, lambda ref, win: _taps(ref, 0, win, "first"))

    def mid(i, carry):
        r0 = pl.multiple_of(i * FF_CH, FF_CH)
        step(r0, lambda ref, win: _taps(ref, r0, win, "mid"))
        return carry

    lax.fori_loop(1, SEQ // FF_CH - 1, mid, 0)
    step(SEQ - FF_CH, lambda ref, win: _taps(ref, SEQ - FF_CH, win, "last"))


def _conv3(taps, w_ref, b_ref):
    dn, md, up = taps
    return w_ref[0:1, :] * dn + w_ref[1:2, :] * md + w_ref[2:3, :] * up + b_ref[...]


def _ff_specs(order):
    def at(fn):
        return (lambda b, j: fn(b, j)) if order == "bj" else (lambda j, b: fn(b, j))
    hs = [pl.BlockSpec((None, SEQ, FF_TC), at(lambda b, j, o=o: (b, 0, j + o))) for o in (0, FF_NB)]
    ws = [pl.BlockSpec((3, FF_TC), at(lambda b, j, o=o: (0, j + o))) for o in (0, FF_NB)]
    bs = [pl.BlockSpec((1, FF_TC), at(lambda b, j, o=o: (0, j + o))) for o in (0, FF_NB)]
    return hs, ws, bs


def _conv_gate_fwd(h, cw, cb, *, name):
    t = h.shape[0]
    bl = t // SEQ

    def body(hg_ref, hu_ref, wg_ref, wu_ref, bg_ref, bu_ref, a_ref, win):
        def step(r0, taps):
            cg = _conv3(taps(hg_ref, win.at[0]), wg_ref, bg_ref)
            cu = _conv3(taps(hu_ref, win.at[1]), wu_ref, bu_ref)
            a_ref[pl.ds(r0, FF_CH), :] = (cg * _sigmoid(cg) * cu).astype(BF16)

        _chunk_loop(step)

    hs, ws, bs = _ff_specs("bj")
    h3 = h.reshape(bl, SEQ, 2 * D_FF)
    act = pl.pallas_call(
        body, name=name, grid=(bl, FF_NB), in_specs=hs + ws + bs,
        out_specs=pl.BlockSpec((None, SEQ, FF_TC), lambda b, j: (b, 0, j)),
        out_shape=_sds((bl, SEQ, D_FF), BF16),
        scratch_shapes=[pltpu.VMEM((2, FF_CH + 2 * FF_HALO, FF_TC), F32)],
        compiler_params=_cparams(PAR, PAR),
    )(h3, h3, cw, cw, cb, cb)
    return act.reshape(t, D_FF)


def _conv_gate_bwd(h, dact, cw, cb, *, name):
    t = h.shape[0]
    bl = t // SEQ

    def body(hg_ref, hu_ref, wg_ref, wu_ref, bg_ref, bu_ref, da_ref,
             dhg_ref, dhu_ref, dwg_ref, dwu_ref, dbg_ref, dbu_ref, dg_s, du_s, win, sums):
        @pl.when(pl.program_id(1) == 0)
        def _():
            for ref in (dwg_ref, dwu_ref, dbg_ref, dbu_ref):
                ref[...] = jnp.zeros_like(ref)

        sums[...] = jnp.zeros_like(sums)
        red = lambda x: jnp.sum(x.reshape(FF_CH // 8, 8, x.shape[1]), axis=0)

        def pass1(r0, taps):
            tg, tu = taps(hg_ref, win.at[0]), taps(hu_ref, win.at[1])
            cg = _conv3(tg, wg_ref, bg_ref)
            cu = _conv3(tu, wu_ref, bu_ref)
            da = da_ref[pl.ds(r0, FF_CH), :].astype(F32)
            sg = _sigmoid(cg)
            dcg = da * cu * (sg * (1.0 + cg * (1.0 - sg)))
            dcu = da * (cg * sg)
            dg_s[pl.ds(r0, FF_CH), :] = dcg
            du_s[pl.ds(r0, FF_CH), :] = dcu
            for half, (d, tp) in enumerate(((dcg, tg), (dcu, tu))):
                for k in range(3):
                    sums[4 * half + k] += red(d * tp[k])
                sums[4 * half + 3] += red(d)

        _chunk_loop(pass1)
        for half, (dw_ref, db_ref) in enumerate(((dwg_ref, dbg_ref), (dwu_ref, dbu_ref))):
            for k in range(3):
                dw_ref[k:k + 1, :] += jnp.sum(sums[4 * half + k], axis=0, keepdims=True)
            db_ref[...] += jnp.sum(sums[4 * half + 3], axis=0, keepdims=True)

        def pass2(r0, taps):
            for k, (s, w_ref, o_ref) in enumerate(((dg_s, wg_ref, dhg_ref), (du_s, wu_ref, dhu_ref))):
                dn, md, up = taps(s, win.at[k])
                o_ref[pl.ds(r0, FF_CH), :] = (w_ref[0:1, :] * up + w_ref[1:2, :] * md + w_ref[2:3, :] * dn).astype(BF16)

        _chunk_loop(pass2)

    hs, ws, bs = _ff_specs("jb")
    half = pl.BlockSpec((None, SEQ, FF_TC), lambda j, b: (b, 0, j))
    wsp = pl.BlockSpec((3, FF_TC), lambda j, b: (0, j))
    bsp = pl.BlockSpec((1, FF_TC), lambda j, b: (0, j))
    h3 = h.reshape(bl, SEQ, 2 * D_FF)
    dhg, dhu, dwg, dwu, dbg, dbu = pl.pallas_call(
        body, name=name, grid=(FF_NB, bl), in_specs=hs + ws + bs + [half],
        out_specs=[half, half, wsp, wsp, bsp, bsp],
        out_shape=[_sds((bl, SEQ, D_FF), BF16), _sds((bl, SEQ, D_FF), BF16), _sds((3, D_FF), F32), _sds((3, D_FF), F32),
                   _sds((1, D_FF), F32), _sds((1, D_FF), F32)],
        scratch_shapes=[pltpu.VMEM((SEQ, FF_TC), F32), pltpu.VMEM((SEQ, FF_TC), F32),
                        pltpu.VMEM((2, FF_CH + 2 * FF_HALO, FF_TC), F32), pltpu.VMEM((8, 8, FF_TC), F32)],
        compiler_params=_cparams(PAR, ARB),
    )(h3, h3, cw, cw, cb, cb, dact.reshape(bl, SEQ, D_FF))
    return (dhg.reshape(t, D_FF), dhu.reshape(t, D_FF), jnp.concatenate([dwg, dwu], axis=1),
            jnp.concatenate([dbg, dbu], axis=1))


def _ple_fwd(x2, gain, wg, pe, pe_blk, wp, *, tm, tn, name):
    t, k = x2.shape
    n = wg.shape[1]

    def body(x_ref, g_ref, wg_ref, pe_ref, wp_ref, xr_ref, hn_ref, x3_ref, gt_ref, pp_ref, hn_s):
        @pl.when(pl.program_id(1) == 0)
        def _():
            x = x_ref[...]
            r = lax.rsqrt(jnp.mean(x * x, axis=-1, keepdims=True) + EPS)
            hn_s[...] = (x * r * g_ref[...]).astype(BF16)
            hn_ref[...] = hn_s[...]

        gate = _sigmoid(jnp.dot(hn_s[...], wg_ref[...], preferred_element_type=F32))
        pp = jnp.dot(pe_ref[...].astype(BF16), wp_ref[...], preferred_element_type=F32)
        gt_ref[...] = gate.astype(BF16)
        pp_ref[...] = pp.astype(BF16)
        x3_ref[...] = xr_ref[...] + pp * gate

    tile = pl.BlockSpec((tm, tn), lambda i, j: (i, j))
    return pl.pallas_call(
        body, name=name, grid=(t // tm, n // tn),
        in_specs=[pl.BlockSpec((tm, k), lambda i, j: (i, 0)), pl.BlockSpec((1, k), lambda i, j: (0, 0)),
                  pl.BlockSpec((k, tn), lambda i, j: (0, j)), pl.BlockSpec((tm, PLE_DIM), lambda i, j: (pe_blk + i, 0)),
                  pl.BlockSpec((PLE_DIM, tn), lambda i, j: (0, j)), tile],
        out_specs=[pl.BlockSpec((tm, k), lambda i, j: (i, 0)), tile, tile, tile],
        out_shape=[_sds((t, k), BF16), _sds((t, n), F32), _sds((t, n), BF16), _sds((t, n), BF16)],
        scratch_shapes=[pltpu.VMEM((tm, k), BF16)],
        compiler_params=_cparams(PAR, ARB),
    )(x2, gain, wg, pe, wp, x2)


def _ple_bwd_ew(dx3, gate, pp, *, tm, name):
    t, n = dx3.shape

    def body(d_ref, g_ref, p_ref, dz_ref, dpp_ref):
        d, g = d_ref[...], g_ref[...]
        dz_ref[...] = (d * p_ref[...] * g * (1.0 - g)).astype(BF16)
        dpp_ref[...] = (d * g).astype(BF16)

    spec = pl.BlockSpec((tm, n), lambda i: (i, 0))
    return pl.pallas_call(
        body, name=name, grid=(t // tm,), in_specs=[spec] * 3, out_specs=[spec] * 2,
        out_shape=[_sds((t, n), BF16)] * 2, compiler_params=_cparams(PAR),
    )(dx3, gate, pp)


def _loss_head(y, tgt, *, tm, name):
    t, d = y.shape

    def body(y_ref, t_ref, l_ref, dy_ref):
        @pl.when(pl.program_id(0) == 0)
        def _():
            l_ref[...] = jnp.zeros_like(l_ref)

        e = y_ref[...] - t_ref[...]
        dy_ref[...] = e * (1.0 / d)
        s = jnp.sum(jnp.sum(e * e, axis=1, keepdims=True), axis=0, keepdims=True)
        l_ref[...] += jnp.broadcast_to(s * (0.5 / d), (8, 128))

    spec = pl.BlockSpec((tm, d), lambda i: (i, 0))
    return pl.pallas_call(
        body, name=name, grid=(t // tm,), in_specs=[spec, spec],
        out_specs=[pl.BlockSpec((8, 128), lambda i: (0, 0)), spec],
        out_shape=[_sds((8, 128), F32), _sds((t, d), F32)], compiler_params=_cparams(ARB),
    )(y, tgt)


BIAS_PC = 8192


def _onehot(bucket_row):
    rows = lax.broadcasted_iota(jnp.int32, (REL_BUCKETS, bucket_row.shape[1]), 0)
    return (rows == bucket_row).astype(BF16)


def _dot3(x, onehot, dims):
    acc = None
    for _ in range(3):
        term = x.astype(BF16)
        part = lax.dot_general(term, onehot, dims, preferred_element_type=F32)
        acc = part if acc is None else acc + part
        x = x - term.astype(F32)
    return acc


def _bias_lookup(table_t, bucket, *, name):
    h = table_t.shape[0]
    p = bucket.shape[1]

    def body(t_ref, b_ref, o_ref):
        bk = b_ref[...]
        val = _dot3(t_ref[...], _onehot(bk), (((1,), (0,)), ((), ())))
        o_ref[...] = jnp.where(bk >= 0, val, NEG_INF)

    return pl.pallas_call(
        body, name=name, grid=(p // BIAS_PC,),
        in_specs=[pl.BlockSpec((h, REL_BUCKETS), lambda i: (0, 0)), pl.BlockSpec((1, BIAS_PC), lambda i: (0, i))],
        out_specs=pl.BlockSpec((h, BIAS_PC), lambda i: (0, i)), out_shape=_sds((h, p), F32),
        compiler_params=_cparams(PAR),
    )(table_t, bucket)


def _bucket_reduce(dbiases, bucket, *, name):
    h, p = dbiases[0].shape
    nl = len(dbiases)

    def body(*refs):
        b_ref, o_ref = refs[nl], refs[nl + 1]

        @pl.when(pl.program_id(0) == 0)
        def _():
            o_ref[...] = jnp.zeros_like(o_ref)

        d = refs[0][...]
        for d_ref in refs[1:nl]:
            d = d + d_ref[...]
        o_ref[...] += _dot3(d, _onehot(b_ref[...]), NT)

    return pl.pallas_call(
        body, name=name, grid=(p // BIAS_PC,),
        in_specs=[pl.BlockSpec((h, BIAS_PC), lambda i: (0, i))] * nl + [pl.BlockSpec((1, BIAS_PC), lambda i: (0, i))],
        out_specs=pl.BlockSpec((h, REL_BUCKETS), lambda i: (0, 0)), out_shape=_sds((h, REL_BUCKETS), F32),
        compiler_params=_cparams(ARB),
    )(*dbiases, bucket)


def _adamw_math(w, g, m, v):
    m = ADAM_B1 * m + (1.0 - ADAM_B1) * g
    v = ADAM_B2 * v + (1.0 - ADAM_B2) * (g * g)
    m_hat = m / (1.0 - ADAM_B1 ** ADAM_STEP)
    v_hat = v / (1.0 - ADAM_B2 ** ADAM_STEP)
    delta = -ADAM_LR * (m_hat / (jnp.sqrt(v_hat) + ADAM_EPS) + ADAM_WD * w)
    return delta, m, v


def _adamw_reduce(parts, w, m, v, *, tr, name):
    nl = len(parts)
    rows, c = w.shape
    r = rows // nl
    nt = r // tr

    def body(*refs):
        p_refs = refs[:nl]
        w_ref, m_ref, v_ref, g_ref, d_ref, nm_ref, nv_ref = refs[nl:]
        for li, p_ref in enumerate(p_refs):
            @pl.when(pl.program_id(0) == li)
            def _(p_ref=p_ref):
                g = p_ref[0].astype(F32)
                for k in range(1, N_DEV):
                    g = g + p_ref[k].astype(F32)
                d, nm, nv = _adamw_math(w_ref[...], g, m_ref[...], v_ref[...])
                g_ref[...] = g
                d_ref[...] = d
                nm_ref[...] = nm
                nv_ref[...] = nv

    def part_map(li):
        return lambda l, i: (0, jnp.where(l == li, i, jnp.where(l < li, 0, nt - 1)), 0)

    spec = pl.BlockSpec((tr, c), lambda l, i: (l * nt + i, 0))
    return pl.pallas_call(
        body, name=name, grid=(nl, nt),
        in_specs=[pl.BlockSpec((N_DEV, tr, c), part_map(li)) for li in range(nl)] + [spec, spec, spec],
        out_specs=[spec] * 4, out_shape=[_sds((rows, c), F32)] * 4, compiler_params=_cparams(ARB, ARB),
    )(*parts, w, m, v)


def _adamw_plain(g, w, m, v, *, name):
    def body(g_ref, w_ref, m_ref, v_ref, d_ref, nm_ref, nv_ref):
        d, nm, nv = _adamw_math(w_ref[...], g_ref[...], m_ref[...], v_ref[...])
        d_ref[...] = d
        nm_ref[...] = nm
        nv_ref[...] = nv

    return pl.pallas_call(body, name=name, out_shape=[_sds(w.shape, F32)] * 3)(g, w, m, v)


def _mesh_pos():
    return lax.axis_index("x"), lax.axis_index("y"), lax.axis_index("c")


def _allgather_body(x_refs, out_refs, send_sems, recv_sems, local_sems, slot):
    x, y, c = _mesh_pos()
    me, sibling = (x, y, c), (x, y, 1 - c)
    chips = [(1 - x, y), (x, 1 - y), (1 - x, 1 - y)]
    waits = []
    for a, (x_ref, out_ref) in enumerate(zip(x_refs, out_refs)):
        def copy(k, block, to, src=None, out_ref=out_ref, a=a):
            return pltpu.make_async_remote_copy(
                src_ref=slot(out_ref, block) if src is None else src, dst_ref=slot(out_ref, block),
                send_sem=send_sems.at[a, k], recv_sem=recv_sems.at[a, k], device_id=to, device_id_type=MESH)

        mine = pltpu.make_async_copy(x_ref, slot(out_ref, me), local_sems.at[a])
        mine.start()
        first = [copy(0, me, sibling, src=x_ref)]
        first += [copy(1 + j, me, (*chip, c), src=x_ref) for j, chip in enumerate(chips)]
        for cp in first:
            cp.start()
        waits.append((copy, mine, first))
    sends = []
    for copy, mine, first in waits:
        passed = [copy(4 + j, (*chip, c), sibling) for j, chip in enumerate(chips)]
        for j, chip in enumerate(chips):
            copy(1 + j, (*chip, c), me).wait_recv()
            passed[j].start()
        sends.append(passed)
    for (copy, mine, first), passed in zip(waits, sends):
        copy(0, sibling, me).wait_recv()
        for j, chip in enumerate(chips):
            copy(4 + j, (*chip, 1 - c), me).wait_recv()
        for cp in first + passed:
            cp.wait_send()
        mine.wait()


PEER_FLIPS = ((0, 0, 1), (1, 0, 0), (0, 1, 0), (1, 1, 0), (1, 0, 1), (0, 1, 1), (1, 1, 1))


def _peer_copies(x_refs, land_refs, send_sem, recv_sem, scatter):
    x, y, c = _mesh_pos()
    me = 4 * x + 2 * y + c
    copies = []
    for x_ref, land_ref in zip(x_refs, land_refs):
        for fx, fy, fc in PEER_FLIPS:
            px, py, pc = x ^ fx, y ^ fy, c ^ fc
            src = x_ref.at[4 * px + 2 * py + pc] if scatter else x_ref
            copies.append(pltpu.make_async_remote_copy(
                src_ref=src, dst_ref=land_ref.at[me], send_sem=send_sem, recv_sem=recv_sem,
                device_id=(px, py, pc), device_id_type=MESH))
    return copies


def _sc_exchange(xs, *, scatter, collective_id, name):
    na = len(xs)
    land_shapes = [x.shape if scatter else (N_DEV,) + x.shape for x in xs]

    def body(*refs):
        x_refs, land_refs = refs[:na], refs[na:2 * na]
        send_sem, recv_sem, local_sem = refs[2 * na:]
        x, y, c = _mesh_pos()
        me = 4 * x + 2 * y + c
        barrier = pltpu.get_barrier_semaphore()
        for fx, fy, fc in PEER_FLIPS:
            pl.semaphore_signal(barrier, inc=1, device_id=(x ^ fx, y ^ fy, c ^ fc), device_id_type=MESH)
        pl.semaphore_wait(barrier, len(PEER_FLIPS))
        for x_ref, land_ref in zip(x_refs, land_refs):
            own = pltpu.make_async_copy(x_ref.at[me] if scatter else x_ref, land_ref.at[me], local_sem)
            own.start()
            own.wait()
        copies = _peer_copies(x_refs, land_refs, send_sem, recv_sem, scatter)
        for cp in copies:
            cp.start()
        for cp in copies:
            cp.wait()

    return pl.kernel(
        body, name=name, out_type=[_sds(s, x.dtype) for s, x in zip(land_shapes, xs)],
        mesh=plsc.ScalarSubcoreMesh(axis_name="sequencer", num_cores=1),
        scratch_types=[pltpu.SemaphoreType.DMA, pltpu.SemaphoreType.DMA, pltpu.SemaphoreType.DMA],
        compiler_params=pltpu.CompilerParams(collective_id=collective_id),
    )(*xs)


def _sc_allgather(xs, *, collective_id, name):
    na = len(xs)

    def body(*refs):
        x_refs, out_refs = refs[:na], refs[na:2 * na]
        send_sems, recv_sems, local_sems = refs[2 * na:]
        x, y, c = _mesh_pos()
        barrier = pltpu.get_barrier_semaphore()
        for fx, fy, fc in PEER_FLIPS:
            pl.semaphore_signal(barrier, inc=1, device_id=(x ^ fx, y ^ fy, c ^ fc), device_id_type=MESH)
        pl.semaphore_wait(barrier, len(PEER_FLIPS))
        _allgather_body(x_refs, out_refs, send_sems, recv_sems, local_sems,
                        lambda ref, pos: ref.at[4 * pos[0] + 2 * pos[1] + pos[2]])

    return pl.kernel(
        body, name=name, out_type=[_sds((N_DEV,) + x.shape, x.dtype) for x in xs],
        mesh=plsc.ScalarSubcoreMesh(axis_name="sequencer", num_cores=1),
        scratch_types=[pltpu.SemaphoreType.DMA((na, 7)), pltpu.SemaphoreType.DMA((na, 7)),
                       pltpu.SemaphoreType.DMA((na,))],
        compiler_params=pltpu.CompilerParams(collective_id=collective_id),
    )(*xs)


def _allgather_vmem(x, *, reduce, name):
    r, c = x.shape

    def body(x_ref, out_ref, *rest):
        if reduce:
            gath, send_sems, recv_sems, local_sems = rest
        else:
            send_sems, recv_sems, local_sems = rest
            gath = out_ref
        _allgather_body([x_ref], [gath], send_sems, recv_sems, local_sems,
                        lambda ref, pos: ref.at[pl.ds((4 * pos[0] + 2 * pos[1] + pos[2]) * r, r), :])
        if reduce:
            acc = gath[0:r, :]
            for k in range(1, N_DEV):
                acc = acc + gath[k * r:(k + 1) * r, :]
            out_ref[...] = acc

    vm = pl.BlockSpec(memory_space=pltpu.VMEM)
    scratch = [pltpu.SemaphoreType.DMA((1, 7)), pltpu.SemaphoreType.DMA((1, 7)), pltpu.SemaphoreType.DMA((1,))]
    if reduce:
        scratch = [pltpu.VMEM((N_DEV * r, c), x.dtype)] + scratch
    return pl.pallas_call(
        body, name=name, in_specs=[vm], out_specs=vm,
        out_shape=_sds((r, c) if reduce else (N_DEV * r, c), x.dtype), scratch_shapes=scratch,
    )(x)


def _t5_bucket(rel):
    nb = REL_BUCKETS // 2
    ret = jnp.where(rel > 0, nb, 0)
    n = jnp.abs(rel)
    max_exact = nb // 2
    nf = jnp.maximum(n, 1).astype(F32)
    large = max_exact + (jnp.log(nf / max_exact) / math.log(REL_MAX_DIST / max_exact)
                         * (nb - max_exact)).astype(jnp.int32)
    large = jnp.minimum(large, nb - 1)
    return ret + jnp.where(n < max_exact, n, large)


def _band_pattern(block, radius, dil):
    kw = block + 2 * radius
    rel = jnp.arange(kw)[None, :] - radius - jnp.arange(block)[:, None]
    return jnp.where(jnp.abs(rel) <= radius, _t5_bucket(rel * dil), -1).astype(jnp.int32).reshape(1, block * kw)


def _rope_tables():
    lane = np.arange(64)
    seg, j = lane // 32, lane % 32
    inv = ROPE_THETA ** (-jnp.arange(0, 32, 2, dtype=F32) / 32)
    tpos = jnp.arange(SEQ)
    pos = jnp.where(jnp.asarray(seg)[None, :] == 0, (tpos // GRID_W)[:, None], (tpos % GRID_W)[:, None])
    ang = pos.astype(F32) * inv[jnp.asarray(j % 16)][None, :]
    cos = jnp.cos(ang)
    sins = jnp.where(jnp.asarray(j)[None, :] < 16, -jnp.sin(ang), jnp.sin(ang))
    return jnp.tile(cos, (1, 4)), jnp.tile(sins, (1, 4))


A_Q, A_K, A_V = (256, 0), (256, 1), (256, 2)
B_Q, B_K, B_V = (256, 0), (128, 2), (128, 3)
A_HEADS = dict(rad=A_RADIUS, nh=4, nkv=4)
B_HEADS = dict(rad=SWA_RADIUS, nh=4, nkv=2)


def _pin(arr, token):
    return arr if token is None else arr + token[0:1, 0:1]


def _local_step(x, pe, tgt, rel_bias, wts, matmul_weights, grads_ready):
    t = x.shape[0]
    bl = t // SEQ
    cos, sins = _rope_tables()
    blocks_a = [min(BAND_BLOCK, SEQ // d) for d in DILATIONS]
    pats_a = [_band_pattern(blk, A_RADIUS, d) for blk, d in zip(blocks_a, DILATIONS)]
    pat_b = _band_pattern(BAND_BLOCK, SWA_RADIUS, 1)
    table_t = rel_bias.T
    bias_a = [_bias_lookup(table_t[:4], pt, name=f"bias_a{ci}").reshape(4, blk, blk + 2 * A_RADIUS)
              for ci, (pt, blk) in enumerate(zip(pats_a, blocks_a))]
    bias_b = _bias_lookup(table_t[4:], pat_b, name="bias_b").reshape(4, BAND_BLOCK, BAND_BLOCK + 2 * SWA_RADIUS)
    nat4 = lambda a: a.reshape(bl, 1, SEQ, a.shape[-1])

    saved = []
    for li in range(DEPTH):
        w = dict(wts[li])
        w.update(matmul_weights(li, "in", x)[0])
        hn0, proj = _norm_mm((x,), w["g_mix"], w["w_in"], None, tm=1024, tn=1152, name="mix_in_fwd")
        qa1, qa4, qa16, qb, qd = _qkprep_fwd(proj, w["qk_gains"], cos, sins, tm=512, name="qkprep_fwd")
        qa = (nat4(qa1), qa4, qa16)
        oa, la = [], []
        for ci in range(3):
            o, l = _band_fwd(qa[ci], A_Q, A_K, A_V, bias_a[ci], None, name=f"band_a{ci}_fwd", **A_HEADS)
            oa.append(o)
            la.append(l)
        oa[0], la[0] = oa[0].reshape(t, 256), la[0].reshape(t, 256)
        ya, lse_a = _combine_a(oa, la, tm=512, name="combine_a")
        yb, lse_b = _band_fwd(nat4(qb), B_Q, B_K, B_V, bias_b, w["sink_t"], name="band_b_fwd", **B_HEADS)
        yb = yb.reshape(t, 256)
        yc = _c_fwd(proj, w["c_g"], w["c_b"], w["c_ws"], w["c_bst"], tm=512, name="c_fwd")
        yd, lse_d = _dense_fwd(qd, tq=256, name="dense_fwd")
        more, started = matmul_weights(li, "rest", yd)
        w.update(more)
        w["out_gain"] = _pin(w["out_gain"], started)
        mixed, x1 = _norm_mm((ya, yb, yc, yd), w["out_gain"], w["w_out"], x, tm=1024, tn=1024, name="mix_out_fwd")
        hn1, h = _norm_mm((x1,), w["g_ffn"], w["w_up"], None, tm=1024, tn=1408, name="ffn_up_fwd", out_dtype=BF16)
        act = _conv_gate_fwd(h, w["conv_w"], w["conv_b"], name="conv_gate_fwd")
        x2 = _mm(act, w["w_down"], "nn", x1, tm=1024, tn=1024, out_dtype=F32, name="ffn_down_fwd")
        hn2, x3, gate, pp = _ple_fwd(x2, w["g_ple"], w["w_gate"], pe, li * (t // 1024), w["w_proj"], tm=1024, tn=512,
                                     name="ple_fwd")
        saved.append(dict(w=w, x0=x, hn0=hn0, proj=proj, qa=qa, qb=qb, qd=qd, ya=ya, lse_a=lse_a, yb=yb, lse_b=lse_b,
                          yc=yc, yd=yd, lse_d=lse_d, mixed=mixed, x1=x1, hn1=hn1, h=h, act=act, x2=x2, hn2=hn2,
                          gate=gate, pp=pp))
        x = x3

    loss_tile, dx = _loss_head(x, tgt, tm=512, name="loss_head")
    grads = [None] * DEPTH
    dbias_a, dbias_bs = [[], [], []], []
    token = None
    for li in reversed(range(DEPTH)):
        s = saved[li]
        w = s["w"]
        g = {}
        w["g_ple"] = _pin(w["g_ple"], token)
        dz, dpp = _ple_bwd_ew(dx, s["gate"], s["pp"], tm=512, name="ple_bwd_ew")
        g["w_gate"] = _mm(s["hn2"], dz, "tn", None, tm=1024, tn=512, out_dtype=BF16, name="dw_gate")
        g["w_proj"] = _mm(pe, dpp, "tn", None, tm=256, tn=1024, out_dtype=BF16, name="dw_proj", a_rows=(li, t))
        dx2, dx2b, g["g_ple"] = _mm_bt_normbwd((dz,), w["w_gate"], (s["x2"],), w["g_ple"], dx, tm=1024, tn=1024,
                                               name="ple_bwd", emit_bf16=True)
        g["w_down"] = _mm(s["act"], dx2b, "tn", None, tm=1408, tn=512, out_dtype=BF16, name="dw_down")
        dact = _mm(dx2b, w["w_down"], "nt", None, tm=1024, tn=1408, out_dtype=BF16, name="ffn_down_bwd")
        dhg, dhu, g["conv_w"], g["conv_b"] = _conv_gate_bwd(s["h"], dact, w["conv_w"], w["conv_b"], name="conv_gate_bwd")
        g["w_up"] = jnp.concatenate(
            [_mm(s["hn1"], dhalf, "tn", None, tm=1024, tn=1408, out_dtype=BF16, name=f"dw_up_{nm}")
             for nm, dhalf in (("gate", dhg), ("up", dhu))], axis=1)
        dx1, dx1b, g["g_ffn"] = _mm_bt_normbwd((dhg, dhu), w["w_up"], (s["x1"],), w["g_ffn"], dx2, tm=1024, tn=1408,
                                               name="ffn_up_bwd", emit_bf16=True)
        g["w_out"] = _mm(s["mixed"], dx1b, "tn", None, tm=1024, tn=512, out_dtype=BF16, name="dw_out")
        out_gain = _pin(w["out_gain"], grads_ready(li, "mid", g))
        dycat, g["out_gain"] = _mm_bt_normbwd((dx1b,), w["w_out"], (s["ya"], s["yb"], s["yc"], s["yd"]), out_gain,
                                              None, tm=1024, tn=1024, name="mix_out_bwd")
        dy_r, lse_r, dl_a, dl_b, dl_d = _deltas(dycat, s["ya"], s["yb"], s["yd"], s["lse_a"], tm=512, name="deltas")
        dy_a = (nat4(dycat),) + tuple(dy_r)
        lse_a = (nat4(s["lse_a"]),) + tuple(lse_r)
        dl_a = (nat4(dl_a[0]),) + tuple(dl_a[1:])
        da = []
        for ci in range(3):
            dq, dk, dv, dbias = _band_bwd(s["qa"][ci], A_Q, A_K, A_V, bias_a[ci], None, dy_a[ci], 0, lse_a[ci],
                                          dl_a[ci], name=f"band_a{ci}_bwd", **A_HEADS)
            if ci == 0:
                dq, dk, dv = (a.reshape(t, 256) for a in (dq, dk, dv))
            da.append((dq, dk, dv))
            dbias_a[ci].append(dbias.reshape(4, -1))
        dqb, dkb, dvb, dbias_b, dsink = _band_bwd(nat4(s["qb"]), B_Q, B_K, B_V, bias_b, w["sink_t"], nat4(dycat), 1,
                                                  nat4(s["lse_b"]), nat4(dl_b), name="band_b_bwd", **B_HEADS)
        dbias_bs.append(dbias_b.reshape(4, -1))
        g["sink"] = dsink[:, 0, 0]
        dd = _dense_bwd(s["qd"], dycat, s["lse_d"], dl_d, tq=256, name="dense_bwd")
        dcu, dcv, g["c_ws"], dbs, g["c_g"], g["c_b"] = _c_bwd(s["proj"], dycat, w["c_g"], w["c_b"], w["c_ws"],
                                                               w["c_wst"], w["c_bst"], tm=512, name="c_bwd")
        g["c_bs"] = dbs[:, ::64].T
        db = (dqb.reshape(t, 256), dkb.reshape(t, 128), dvb.reshape(t, 128))
        dproj, dgains = _qkprep_bwd(s["proj"], da, db, dd, dcu, dcv, w["qk_gains"], cos, sins, tm=512, name="qkprep_bwd")
        g["qk_gain"] = dgains[:6, :64].reshape(3, 2, HEAD_DIM)
        g["w_in"] = _mm(s["hn0"], dproj, "tn", None, tm=1024, tn=1152, out_dtype=BF16, name="dw_in")
        dx, g["g_mix"] = _mm_bt_normbwd((dproj,), w["w_in"], (s["x0"],), w["g_mix"], dx1, tm=1024, tn=1152,
                                        name="mix_in_bwd")
        grads[li] = g
        token = grads_ready(li, "end", g)
    d_table_a = sum(_bucket_reduce(dbias_a[ci], pats_a[ci], name=f"bucket_a{ci}") for ci in range(3))
    d_table_b = _bucket_reduce(dbias_bs, pat_b, name="bucket_b")
    d_rel_bias = jnp.concatenate([d_table_a, d_table_b], axis=0).T
    return loss_tile[0, 0], dx, grads, d_rel_bias


WEIGHT_NAMES = ("rel_bias", "ln_mix_g", "w_in", "qk_gain", "sink", "c_norm_g", "c_norm_b", "c_ws", "c_bs", "out_gain",
                "w_out", "ln_ffn_g", "w_up", "conv_w", "conv_b", "w_down", "ln_ple_g", "w_ple_gate", "w_ple_proj")
COL_SHARDED = ("w_in", "w_up", "w_ple_proj")
ROW_SHARDED = ("w_out", "w_down", "w_ple_gate")
SMALL_SHARDED = ("conv_w", "out_gain")
REPLICATED = tuple(n for n in WEIGHT_NAMES if n not in COL_SHARDED + ROW_SHARDED + SMALL_SHARDED)
LOCAL_GRAD_KEY = {"ln_mix_g": "g_mix", "ln_ffn_g": "g_ffn", "ln_ple_g": "g_ple", "c_norm_g": "c_g", "c_norm_b": "c_b",
                  "w_ple_gate": "w_gate", "w_ple_proj": "w_proj"}


def _full_from_gathered(name, gathered):
    _, r, c = gathered.shape
    if name in ROW_SHARDED:
        return gathered.reshape(N_DEV * r, c)
    return jnp.transpose(gathered, (1, 0, 2)).reshape(r, N_DEV * c)


def _slots_from_full(name, full):
    rows, cols = full.shape
    if name in ROW_SHARDED:
        return full.reshape(N_DEV, rows // N_DEV, cols)
    return jnp.transpose(full.reshape(rows, N_DEV, cols // N_DEV), (1, 0, 2))


def _piece_rows(shape):
    return -(-int(np.prod(shape)) // 1024) * 8


def _pack_rows(arrays):
    pieces = []
    for a in arrays:
        n, rows = int(np.prod(a.shape)), _piece_rows(a.shape)
        flat = a.astype(F32).reshape(-1)
        if n != rows * LANES:
            flat = jnp.pad(flat, (0, rows * LANES - n))
        pieces.append(flat.reshape(rows, LANES))
    return jnp.concatenate(pieces, axis=0)


def _unpack_rows(packed, shapes):
    out, off = [], 0
    for shp in shapes:
        n, rows = int(np.prod(shp)), _piece_rows(shp)
        piece = packed[off:off + rows]
        out.append((piece if n == rows * LANES else piece.reshape(-1)[:n]).reshape(shp))
        off += rows
    return out


def kernel(x, p, rel_bias, ln_mix_g, w_in, qk_gain, sink, c_norm_g, c_norm_b, c_ws, c_bs, out_gain, w_out, ln_ffn_g, w_up, conv_w, conv_b, w_down, ln_ple_g, w_ple_gate, w_ple_proj, loss_target, m_rel_bias, m_ln_mix_g, m_w_in, m_qk_gain, m_sink, m_c_norm_g, m_c_norm_b, m_c_ws, m_c_bs, m_out_gain, m_w_out, m_ln_ffn_g, m_w_up, m_conv_w, m_conv_b, m_w_down, m_ln_ple_g, m_w_ple_gate, m_w_ple_proj, v_rel_bias, v_ln_mix_g, v_w_in, v_qk_gain, v_sink, v_c_norm_g, v_c_norm_b, v_c_ws, v_c_bs, v_out_gain, v_w_out, v_ln_ffn_g, v_w_up, v_conv_w, v_conv_b, v_w_down, v_ln_ple_g, v_w_ple_gate, v_w_ple_proj):
    env = dict(locals())
    wt = {n: env[n] for n in WEIGHT_NAMES}
    mom_m = {n: env["m_" + n] for n in WEIGHT_NAMES}
    mom_v = {n: env["v_" + n] for n in WEIGHT_NAMES}
    bl = x.shape[0]
    t = bl * SEQ
    me = 4 * lax.axis_index("x") + 2 * lax.axis_index("y") + lax.axis_index("c")

    big = COL_SHARDED + ROW_SHARDED
    full = {}
    small_shapes = [wt[n].shape for n in SMALL_SHARDED]
    small = _allgather_vmem(_pack_rows([wt[n] for n in SMALL_SHARDED]), reduce=False, name="gather_small")
    small = small.reshape(N_DEV, -1)
    off = 0
    for n, shp in zip(SMALL_SHARDED, small_shapes):
        cnt = int(np.prod(shp))
        g = small[:, off:off + cnt].reshape((N_DEV,) + tuple(shp))
        full[n] = jnp.transpose(g, (1, 2, 0, 3)).reshape(shp[0], shp[1], N_DEV * shp[2])
        off += _piece_rows(shp) * LANES

    def head_gain(li, a, b, reps):
        g = jnp.tile(qk_gain[li, a, b], reps)
        return jnp.pad(g, (0, 256 - g.shape[0]))

    wts = []
    for li in range(DEPTH):
        rows = [head_gain(li, 0, 0, 4), head_gain(li, 0, 1, 4), head_gain(li, 1, 0, 4), head_gain(li, 1, 1, 2),
                head_gain(li, 2, 0, 4), head_gain(li, 2, 1, 2), jnp.zeros((256,), F32), jnp.zeros((256,), F32)]
        wts.append(dict(
            g_mix=ln_mix_g[li].reshape(1, -1), qk_gains=jnp.stack(rows),
            sink_t=jnp.broadcast_to(sink[li][:, None, None], (4, 8, 128)),
            c_g=c_norm_g[li].reshape(1, -1), c_b=c_norm_b[li].reshape(1, -1), c_ws=c_ws[li].astype(BF16),
            c_wst=jnp.transpose(c_ws[li], (0, 2, 1)).astype(BF16), c_bst=jnp.repeat(c_bs[li].T, 64, axis=1),
            out_gain=full["out_gain"][li].reshape(1, -1), g_ffn=ln_ffn_g[li].reshape(1, -1),
            conv_w=full["conv_w"][li], conv_b=conv_b[li].reshape(1, -1), g_ple=ln_ple_g[li].reshape(1, -1)))

    local_key = {"w_ple_gate": "w_gate", "w_ple_proj": "w_proj"}

    gather_names = {"in": ("w_in",), "rest": tuple(n for n in big if n != "w_in")}
    gathered = {}
    for cid, (li, names) in enumerate(((0, gather_names["in"]), (0, gather_names["rest"]), (1, big))):
        lands = _sc_allgather([wt[n][li].astype(BF16) for n in names], collective_id=cid,
                              name=f"gather_{li}_{len(names)}")
        gathered.setdefault(li, {}).update(zip(names, lands))

    def matmul_weights(li, part, after):
        out = {}
        for n in gather_names[part]:
            g, _ = lax.optimization_barrier((gathered[li][n], after))
            out[local_key.get(n, n)] = _full_from_gathered(n, g)
        return out, None

    mid_names = ("w_ple_gate", "w_ple_proj", "w_down", "w_up", "w_out")
    end_names = ("w_in",)
    landed = {}

    def start_exchange(li, names, g, tag, cid):
        slots = [_slots_from_full(n, g[local_key.get(n, n)]) for n in names]
        lands = _sc_exchange(slots, scatter=True, collective_id=cid, name=f"grads_{li}_{tag}")
        landed.update({(n, li): land for n, land in zip(names, lands)})

    def grads_ready(li, stage, g):
        if li == 0:
            start_exchange(li, mid_names if stage == "mid" else end_names, g, stage, 5 if stage == "mid" else 6)
        elif stage == "end":
            start_exchange(li, mid_names + end_names, g, stage, 4)
        return None

    loss_part, dx, grads, d_rel_bias = _local_step(
        x.reshape(t, D_MODEL), p.reshape(DEPTH * t, PLE_DIM), loss_target.reshape(t, D_MODEL), rel_bias, wts,
        matmul_weights, grads_ready)
    loss = lax.psum(loss_part, ("x", "y", "c"))

    def local_grad(n):
        if n == "rel_bias":
            return d_rel_bias
        key = LOCAL_GRAD_KEY.get(n, n)
        return jnp.stack([grads[li][key].reshape(wt[n].shape[1:]) if n in REPLICATED else grads[li][key]
                          for li in range(DEPTH)])

    out_g, out_d, out_m, out_v = {}, {}, {}, {}
    for n in big:
        shp = wt[n].shape
        two_d = lambda a: a.reshape(-1, shp[-1])
        res = _adamw_reduce([landed[n, li] for li in range(DEPTH)], two_d(wt[n]), two_d(mom_m[n]), two_d(mom_v[n]),
                            tr=32 if n == "w_down" else 128, name="adamw_" + n)
        out_g[n], out_d[n], out_m[n], out_v[n] = [r.reshape(shp) for r in res]

    small_names = REPLICATED + SMALL_SHARDED
    small_full_shapes = [wt[n].shape if n in REPLICATED else full[n].shape for n in small_names]
    reduced = _allgather_vmem(_pack_rows([local_grad(n) for n in small_names]), reduce=True, name="allreduce_small")
    reduced = dict(zip(small_names, _unpack_rows(reduced, small_full_shapes)))
    rep_shapes = [wt[n].shape for n in REPLICATED]
    upd = _adamw_plain(_pack_rows([reduced[n] for n in REPLICATED]), _pack_rows([wt[n] for n in REPLICATED]),
                       _pack_rows([mom_m[n] for n in REPLICATED]), _pack_rows([mom_v[n] for n in REPLICATED]),
                       name="adamw_replicated")
    for dst, packed in zip((out_d, out_m, out_v), upd):
        dst.update(zip(REPLICATED, _unpack_rows(packed, rep_shapes)))
    for n in REPLICATED:
        out_g[n] = reduced[n]
    for n in SMALL_SHARDED:
        shp = wt[n].shape
        g = reduced[n].reshape(shp[0], shp[1], N_DEV, shp[2])
        g = lax.dynamic_index_in_dim(g, me, axis=2, keepdims=False)
        two_d = lambda a: a.reshape(-1, shp[-1])
        res = _adamw_plain(two_d(g), two_d(wt[n]), two_d(mom_m[n]), two_d(mom_v[n]), name="adamw_" + n)
        out_g[n] = g
        out_d[n], out_m[n], out_v[n] = [r.reshape(shp) for r in res]

    return (loss, dx.reshape(bl, SEQ, D_MODEL), *[out_g[n] for n in WEIGHT_NAMES], *[out_d[n] for n in WEIGHT_NAMES],
            *[out_m[n] for n in WEIGHT_NAMES], *[out_v[n] for n in WEIGHT_NAMES])
```

```python
import math

import jax
import jax.numpy as jnp
import numpy as np
from jax import lax
from jax.experimental import pallas as pl
from jax.experimental.pallas import tpu as pltpu
from jax.experimental.pallas import tpu_sc as plsc

F32 = jnp.float32
BF16 = jnp.bfloat16
HI = lax.Precision.HIGHEST

N_DEV = 8
D_MODEL = 1024
SEQ = 2048
DEPTH = 2
HEAD_DIM = 64
IN_WIDTH = 2304
D_FF = 2816
PLE_DIM = 256
C_CHUNK = 128
C_GROUPS = 4
DILATED_CFGS = ((128, 1), (512, 4), (2048, 16))
DILATIONS = tuple(d for _, d in DILATED_CFGS)
A_RADIUS = 64
SWA_RADIUS = 128
BAND_BLOCK = 256
GRID_W = 64
ROPE_THETA = 10000.0
REL_BUCKETS = 32
REL_MAX_DIST = 1024
EPS = 1e-6
NEG_INF = -1e30
ATTN_SCALE = HEAD_DIM ** -0.5
LANES = 128

ADAM_LR = 0.001
ADAM_B1 = 0.9
ADAM_B2 = 0.999
ADAM_EPS = 1e-08
ADAM_WD = 0.01
ADAM_STEP = 10

MESH = pl.DeviceIdType.MESH
NT = (((1,), (1,)), ((), ()))
TN = (((0,), (0,)), ((), ()))
ARB = "arbitrary"
PAR = "parallel"


def _cparams(*sem):
    return pltpu.CompilerParams(dimension_semantics=tuple(sem))


def _sds(shape, dtype):
    return jax.ShapeDtypeStruct(tuple(shape), dtype)


def _group_sum_matrix(n, same_group):
    r = lax.broadcasted_iota(jnp.int32, (n, n), 0)
    c = lax.broadcasted_iota(jnp.int32, (n, n), 1)
    if same_group:
        return ((r >> 6) == (c >> 6)).astype(F32)
    return ((r & 63) == (c & 63)).astype(F32)


def _seg_sum(x, e):
    eb = e.astype(BF16)
    hi = x.astype(BF16)
    lo = (x - hi.astype(F32)).astype(BF16)
    return jnp.dot(hi, eb, preferred_element_type=F32) + jnp.dot(lo, eb, preferred_element_type=F32)


def _gelu(x):
    c = math.sqrt(2.0 / math.pi)
    return 0.5 * x * (1.0 + jnp.tanh(c * (x + 0.044715 * (x * x * x))))


def _gelu_grad(x):
    c = math.sqrt(2.0 / math.pi)
    t = jnp.tanh(c * (x + 0.044715 * (x * x * x)))
    return 0.5 * (1.0 + t) + 0.5 * x * (1.0 - t * t) * c * (1.0 + 3.0 * 0.044715 * (x * x))


def _sigmoid(x):
    return 1.0 / (1.0 + jnp.exp(-x))


def _scatter_cols(scratch, first, val):
    for c in range(val.shape[1] // LANES):
        scratch[first + c] = val[:, c * LANES:(c + 1) * LANES]


def _gather_cols(scratch, first, ncol):
    return jnp.concatenate([scratch[first + c] for c in range(ncol)], axis=1)


def _read_residue(scratch, first, ncol, r, d):
    n = scratch.shape[1] // d
    return jnp.concatenate([scratch.at[first + c][pl.ds(r, n, stride=d), :] for c in range(ncol)], axis=1)


def _write_residue(scratch, first, r, d, val):
    n = scratch.shape[1] // d
    for c in range(val.shape[1] // LANES):
        scratch.at[first + c][pl.ds(r, n, stride=d), :] = val[:, c * LANES:(c + 1) * LANES]


def _norm_mm(xs, gain, w, res, *, tm, tn, name, out_dtype=F32):
    t = xs[0].shape[0]
    k = sum(x.shape[1] for x in xs)
    n = w.shape[1]
    ng = len(xs)
    has_res = res is not None

    def body(*refs):
        x_refs = refs[:ng]
        g_ref, w_ref = refs[ng], refs[ng + 1]
        res_ref = refs[ng + 2] if has_res else None
        hn_ref, o_ref, hn_s = refs[ng + 2 + has_res:]

        @pl.when(pl.program_id(1) == 0)
        def _():
            off = 0
            for xr in x_refs:
                x = xr[...]
                wd = x.shape[1]
                r = lax.rsqrt(jnp.mean(x * x, axis=-1, keepdims=True) + EPS)
                hn_s[:, off:off + wd] = (x * r * g_ref[:, off:off + wd]).astype(BF16)
                off += wd
            hn_ref[...] = hn_s[...]

        acc = jnp.dot(hn_s[...], w_ref[...], preferred_element_type=F32)
        if has_res:
            acc = acc + res_ref[...]
        o_ref[...] = acc.astype(out_dtype)

    in_specs = [pl.BlockSpec((tm, x.shape[1]), lambda i, j: (i, 0)) for x in xs]
    in_specs += [pl.BlockSpec((1, k), lambda i, j: (0, 0)), pl.BlockSpec((k, tn), lambda i, j: (0, j))]
    args = list(xs) + [gain, w]
    if has_res:
        in_specs.append(pl.BlockSpec((tm, tn), lambda i, j: (i, j)))
        args.append(res)
    return pl.pallas_call(
        body, name=name, grid=(t // tm, n // tn), in_specs=in_specs,
        out_specs=[pl.BlockSpec((tm, k), lambda i, j: (i, 0)), pl.BlockSpec((tm, tn), lambda i, j: (i, j))],
        out_shape=[_sds((t, k), BF16), _sds((t, n), out_dtype)],
        scratch_shapes=[pltpu.VMEM((tm, k), BF16)],
        compiler_params=_cparams(PAR, ARB),
    )(*args)


def _mm(a, b, mode, res, *, tm, tn, out_dtype, name, a_rows=None):
    if mode == "tn":
        kk, m = a.shape
        blk_a = 0
        if a_rows is not None:
            blk_a, kk = a_rows
        a_spec = pl.BlockSpec((kk, tm), lambda i, j: (blk_a, i))
    else:
        m, kk = a.shape
        a_spec = pl.BlockSpec((tm, kk), lambda i, j: (i, 0))
    if mode == "nt":
        n = b.shape[0]
        b_spec = pl.BlockSpec((tn, kk), lambda i, j: (j, 0))
    else:
        n = b.shape[1]
        b_spec = pl.BlockSpec((kk, tn), lambda i, j: (0, j))
    has_res = res is not None

    def body(*refs):
        a_ref, b_ref = refs[0], refs[1]
        o_ref = refs[-1]
        av = a_ref[...].astype(BF16)
        bv = b_ref[...].astype(BF16)
        if mode == "nn":
            acc = jnp.dot(av, bv, preferred_element_type=F32)
        elif mode == "nt":
            acc = lax.dot_general(av, bv, NT, preferred_element_type=F32)
        else:
            acc = lax.dot_general(av, bv, TN, preferred_element_type=F32)
        if has_res:
            acc = acc + refs[2][...]
        o_ref[...] = acc.astype(out_dtype)

    in_specs = [a_spec, b_spec]
    args = [a, b]
    if has_res:
        in_specs.append(pl.BlockSpec((tm, tn), lambda i, j: (i, j)))
        args.append(res)
    return pl.pallas_call(
        body, name=name, grid=(m // tm, n // tn), in_specs=in_specs,
        out_specs=pl.BlockSpec((tm, tn), lambda i, j: (i, j)),
        out_shape=_sds((m, n), out_dtype),
        compiler_params=_cparams(PAR, PAR),
    )(*args)


def _mm_bt_normbwd(dys, w, xs, gain, dres, *, tm, tn, name, emit_bf16=False):
    t, wd_each = dys[0].shape
    nd = len(dys)
    per = wd_each // tn
    nj = nd * per
    k = w.shape[0]
    ng = len(xs)
    has_res = dres is not None

    def body(*refs):
        dy_refs = refs[:nd]
        w_ref = refs[nd]
        x_refs = refs[nd + 1:nd + 1 + ng]
        g_ref = refs[nd + 1 + ng]
        dres_ref = refs[nd + 2 + ng] if has_res else None
        outs = refs[nd + 2 + ng + has_res:]
        dx_ref = outs[0]
        dxb_ref = outs[1] if emit_bf16 else None
        dg_ref, acc = outs[1 + emit_bf16:]
        i, j = pl.program_id(0), pl.program_id(1)

        @pl.when(j == 0)
        def _():
            acc[...] = jnp.zeros_like(acc)

        for d, dy_ref in enumerate(dy_refs):
            @pl.when((j >= d * per) & (j < (d + 1) * per))
            def _(dy_ref=dy_ref):
                acc[...] += lax.dot_general(dy_ref[...].astype(BF16), w_ref[...], NT, preferred_element_type=F32)

        @pl.when(j == nj - 1)
        def _():
            @pl.when(i == 0)
            def _():
                dg_ref[...] = jnp.zeros_like(dg_ref)

            off = 0
            for xr in x_refs:
                x = xr[...]
                wd = x.shape[1]
                g = g_ref[:, off:off + wd]
                dyn = acc[:, off:off + wd]
                r = lax.rsqrt(jnp.mean(x * x, axis=-1, keepdims=True) + EPS)
                gdy = dyn * g
                dx = r * gdy - x * (r * r * r * jnp.mean(gdy * x, axis=-1, keepdims=True))
                if has_res:
                    dx = dx + dres_ref[:, off:off + wd]
                dx_ref[:, off:off + wd] = dx
                if emit_bf16:
                    dxb_ref[:, off:off + wd] = dx.astype(BF16)
                dg_ref[:, off:off + wd] += jnp.sum(dyn * x * r, axis=0, keepdims=True)
                off += wd

    def dy_map(d):
        return lambda i, j: (i, jnp.clip(j - d * per, 0, per - 1))

    in_specs = [pl.BlockSpec((tm, tn), dy_map(d)) for d in range(nd)]
    in_specs.append(pl.BlockSpec((k, tn), lambda i, j: (0, j)))
    in_specs += [pl.BlockSpec((tm, x.shape[1]), lambda i, j: (i, 0)) for x in xs]
    in_specs.append(pl.BlockSpec((1, k), lambda i, j: (0, 0)))
    args = list(dys) + [w] + list(xs) + [gain]
    if has_res:
        in_specs.append(pl.BlockSpec((tm, k), lambda i, j: (i, 0)))
        args.append(dres)
    row = pl.BlockSpec((tm, k), lambda i, j: (i, 0))
    out_specs = [row] + ([row] if emit_bf16 else []) + [pl.BlockSpec((1, k), lambda i, j: (0, 0))]
    out_shape = [_sds((t, k), F32)] + ([_sds((t, k), BF16)] if emit_bf16 else []) + [_sds((1, k), F32)]
    return pl.pallas_call(
        body, name=name, grid=(t // tm, nj), in_specs=in_specs, out_specs=out_specs, out_shape=out_shape,
        scratch_shapes=[pltpu.VMEM((tm, k), F32)],
        compiler_params=_cparams(ARB, ARB),
    )(*args)


def _rope_partner(y):
    n = y.shape[1]
    lane = lax.broadcasted_iota(jnp.int32, y.shape, 1)
    return jnp.where((lane & 31) < 16, pltpu.roll(y, n - 16, 1), pltpu.roll(y, 16, 1))


def _residue_specs(tm, width, nt):
    specs = [pl.BlockSpec((tm, width), lambda b, i: (b * nt + i, 0))]
    for d in DILATIONS[1:]:
        specs.append(pl.BlockSpec((None, d, tm // d, width), lambda b, i: (b, 0, i, 0)))
    return specs


def _residue_shapes(bl, width, dtype):
    return [_sds((bl * SEQ, width), dtype)] + [_sds((bl, d, SEQ // d, width), dtype) for d in DILATIONS[1:]]


def _qkprep_fwd(proj, gains, cos, sins, *, tm, name):
    t = proj.shape[0]
    bl = t // SEQ
    nt = SEQ // tm

    def body(p_ref, g_ref, c_ref, s_ref, qa1_ref, qa4_ref, qa16_ref, qb_ref, qd_ref, scr):
        e = _group_sum_matrix(256, True)

        def hn(x, row):
            wd = x.shape[1]
            ms = _seg_sum(x * x, e[:wd, :wd]) * (1.0 / HEAD_DIM)
            return x * lax.rsqrt(ms + EPS) * g_ref[row:row + 1, :wd]

        qa = jnp.concatenate([hn(p_ref[:, 0:256], 0) * ATTN_SCALE, hn(p_ref[:, 256:512], 1), p_ref[:, 512:768]], axis=1)
        qa1_ref[...] = qa.astype(BF16)
        _scatter_cols(scr, 0, qa)
        for d, ref in ((4, qa4_ref), (16, qa16_ref)):
            for r in range(d):
                ref[r] = _read_residue(scr, 0, 6, r, d).astype(BF16)
        qb_ref[:, 0:256] = (hn(p_ref[:, 768:1024], 2) * ATTN_SCALE).astype(BF16)
        qb_ref[:, 256:384] = hn(p_ref[:, 1024:1152], 3).astype(BF16)
        qb_ref[:, 384:512] = p_ref[:, 1152:1280].astype(BF16)
        yq = hn(p_ref[:, 1792:2048], 4)
        yq = yq * c_ref[...] + _rope_partner(yq) * s_ref[...]
        qd_ref[:, 0:256] = (yq * ATTN_SCALE).astype(BF16)
        yk = hn(p_ref[:, 2048:2176], 5)
        yk = yk * c_ref[:, 0:128] + _rope_partner(yk) * s_ref[:, 0:128]
        qd_ref[:, 256:384] = yk.astype(BF16)
        qd_ref[:, 384:512] = p_ref[:, 2176:2304].astype(BF16)

    row = lambda width: pl.BlockSpec((tm, width), lambda b, i: (b * nt + i, 0))
    tab = pl.BlockSpec((tm, 256), lambda b, i: (i, 0))
    return pl.pallas_call(
        body, name=name, grid=(bl, nt),
        in_specs=[row(IN_WIDTH), pl.BlockSpec((8, 256), lambda b, i: (0, 0)), tab, tab],
        out_specs=_residue_specs(tm, 768, nt) + [row(512), row(512)],
        out_shape=_residue_shapes(bl, 768, BF16) + [_sds((t, 512), BF16), _sds((t, 512), BF16)],
        scratch_shapes=[pltpu.VMEM((6, tm, LANES), F32)],
        compiler_params=_cparams(PAR, PAR),
    )(proj, gains, cos, sins)


def _qkprep_bwd(proj, da, db, dd, dcu, dcv, gains, cos, sins, *, tm, name):
    t = proj.shape[0]
    bl = t // SEQ
    nt = SEQ // tm
    flat = [a for cfg in da for a in cfg] + list(db) + list(dd) + [dcu, dcv]

    def body(*refs):
        p_ref, g_ref, c_ref, s_ref = refs[:4]
        d_refs = refs[4:4 + len(flat)]
        dp_ref, dg_ref, scr = refs[4 + len(flat):]
        a_refs = d_refs[:9]
        dqb_ref, dkb_ref, dvb_ref, dqd_ref, dkd_ref, dvd_ref, dcu_ref, dcv_ref = d_refs[9:]
        e = _group_sum_matrix(256, True)
        first = (pl.program_id(0) == 0) & (pl.program_id(1) == 0)
        last = (pl.program_id(0) == bl - 1) & (pl.program_id(1) == nt - 1)

        @pl.when(first)
        def _():
            dg_ref[...] = jnp.zeros_like(dg_ref)

        def hn_bwd(x, dy, row):
            wd = x.shape[1]
            ee = e[:wd, :wd]
            g = g_ref[row:row + 1, :wd]
            r = lax.rsqrt(_seg_sum(x * x, ee) * (1.0 / HEAD_DIM) + EPS)
            gdy = dy * g
            dx = r * gdy - x * (r * r * r * (_seg_sum(gdy * x, ee) * (1.0 / HEAD_DIM)))
            dg_ref[row:row + 1, :wd] += jnp.sum(dy * x * r, axis=0, keepdims=True)
            return dx

        def rope_bwd(dy, wd):
            return dy * c_ref[:, :wd] + _rope_partner(dy * s_ref[:, :wd])

        dqkv = jnp.concatenate([a_refs[0][...], a_refs[1][...], a_refs[2][...]], axis=1)
        for ci, d in ((1, 4), (2, 16)):
            for r in range(d):
                part = jnp.concatenate([a_refs[3 * ci + m][r] for m in range(3)], axis=1)
                _write_residue(scr, 0, r, d, part)
            dqkv = dqkv + _gather_cols(scr, 0, 6)
        dp_ref[:, 0:256] = hn_bwd(p_ref[:, 0:256], dqkv[:, 0:256] * ATTN_SCALE, 0).astype(BF16)
        dp_ref[:, 256:512] = hn_bwd(p_ref[:, 256:512], dqkv[:, 256:512], 1).astype(BF16)
        dp_ref[:, 512:768] = dqkv[:, 512:768].astype(BF16)
        dp_ref[:, 768:1024] = hn_bwd(p_ref[:, 768:1024], dqb_ref[...] * ATTN_SCALE, 2).astype(BF16)
        dp_ref[:, 1024:1152] = hn_bwd(p_ref[:, 1024:1152], dkb_ref[...], 3).astype(BF16)
        dp_ref[:, 1152:1280] = dvb_ref[...].astype(BF16)
        dp_ref[:, 1280:1536] = dcu_ref[...].astype(BF16)
        dp_ref[:, 1536:1792] = dcv_ref[...].astype(BF16)
        dp_ref[:, 1792:2048] = hn_bwd(p_ref[:, 1792:2048], rope_bwd(dqd_ref[...] * ATTN_SCALE, 256), 4).astype(BF16)
        dp_ref[:, 2048:2176] = hn_bwd(p_ref[:, 2048:2176], rope_bwd(dkd_ref[...], 128), 5).astype(BF16)
        dp_ref[:, 2176:2304] = dvd_ref[...].astype(BF16)

        @pl.when(last)
        def _():
            dg_ref[...] = _seg_sum(dg_ref[...], _group_sum_matrix(256, False))

    row = lambda width: pl.BlockSpec((tm, width), lambda b, i: (b * nt + i, 0))
    tab = pl.BlockSpec((tm, 256), lambda b, i: (i, 0))
    in_specs = [row(IN_WIDTH), pl.BlockSpec((8, 256), lambda b, i: (0, 0)), tab, tab]
    res_specs = _residue_specs(tm, 256, nt)
    in_specs += [res_specs[ci] for ci in range(3) for _ in range(3)]
    in_specs += [row(a.shape[1]) for a in flat[9:]]
    return pl.pallas_call(
        body, name=name, grid=(bl, nt), in_specs=in_specs,
        out_specs=[row(IN_WIDTH), pl.BlockSpec((8, 256), lambda b, i: (0, 0))],
        out_shape=[_sds((t, IN_WIDTH), BF16), _sds((8, 256), F32)],
        scratch_shapes=[pltpu.VMEM((6, tm, LANES), F32)],
        compiler_params=_cparams(ARB, ARB),
    )(proj, gains, cos, sins, *flat)


BAND_RESIDUES = 4


def _band_spec(seq_len, spec, rb):
    width, idx = spec
    return pl.BlockSpec((None, rb, seq_len, width), lambda b, r: (b, r, 0, idx))


def _fill_padded(dst, src_ref, rad, seq_len):
    z = jnp.zeros((rad, dst.shape[1]), dst.dtype)
    dst[0:rad, :] = z
    dst[rad + seq_len:rad + seq_len + rad, :] = z
    dst[rad:rad + seq_len, :] = src_ref[...]


def _band_fwd(src, qs, ks, vs, bias, sink, *, rad, nh, nkv, name):
    bl, dil, sl, _ = src.shape
    blk = bias.shape[1]
    kw = blk + 2 * rad
    nb = sl // blk
    rep = nh // nkv
    has_sink = sink is not None
    rb = min(dil, BAND_RESIDUES)

    def body(*refs):
        q_all, k_all, v_all, b_ref = refs[:4]
        s_ref = refs[4] if has_sink else None
        o_all, l_all, kp, vp = refs[4 + has_sink:]
        for ri in range(rb):
            one_sequence(q_all.at[ri], k_all.at[ri], v_all.at[ri], b_ref, s_ref, o_all.at[ri], l_all.at[ri], kp, vp)

    def one_sequence(q_ref, k_ref, v_ref, b_ref, s_ref, o_ref, l_ref, kp, vp):
        _fill_padded(kp, k_ref, rad, sl)
        _fill_padded(vp, v_ref, rad, sl)

        def blk_body(i, carry):
            r0 = pl.multiple_of(i * blk, blk)
            qb = q_ref[pl.ds(r0, blk), :]
            kwin = kp[pl.ds(r0, kw), :]
            vwin = vp[pl.ds(r0, kw), :]
            col = r0 - rad + lax.broadcasted_iota(jnp.int32, (blk, kw), 1)
            neg = jnp.where((col >= 0) & (col < sl), 0.0, NEG_INF).astype(F32)
            for h in range(nh):
                g = h // rep
                hs = slice(h * HEAD_DIM, (h + 1) * HEAD_DIM)
                gs = slice(g * HEAD_DIM, (g + 1) * HEAD_DIM)
                s = lax.dot_general(qb[:, hs], kwin[:, gs], NT, preferred_element_type=F32)
                s = s + b_ref[h] + neg
                m = jnp.max(s, axis=1, keepdims=True)
                if has_sink:
                    sk = s_ref[h][0:1, 0:1]
                    m = jnp.maximum(m, sk)
                p = jnp.exp(s - m)
                den = jnp.sum(p, axis=1, keepdims=True)
                if has_sink:
                    den = den + jnp.exp(sk - m)
                o = jnp.dot(p.astype(BF16), vwin[:, gs], preferred_element_type=F32) / den
                o_ref[pl.ds(r0, blk), hs] = o
                l_ref[pl.ds(r0, blk), hs] = jnp.broadcast_to(m + jnp.log(den), (blk, HEAD_DIM))
            return carry

        lax.fori_loop(0, nb, blk_body, 0)

    in_specs = [_band_spec(sl, qs, rb), _band_spec(sl, ks, rb), _band_spec(sl, vs, rb),
                pl.BlockSpec((nh, blk, kw), lambda b, r: (0, 0, 0))]
    args = [src] * 3 + [bias]
    if has_sink:
        in_specs.append(pl.BlockSpec((nh, 8, 128), lambda b, r: (0, 0, 0)))
        args.append(sink)
    return pl.pallas_call(
        body, name=name, grid=(bl, dil // rb), in_specs=in_specs,
        out_specs=[_band_spec(sl, (256, 0), rb)] * 2,
        out_shape=[_sds((bl, dil, sl, 256), F32)] * 2,
        scratch_shapes=[pltpu.VMEM((sl + 2 * rad, ks[0]), BF16), pltpu.VMEM((sl + 2 * rad, vs[0]), BF16)],
        compiler_params=_cparams(PAR, PAR),
    )(*args)


def _band_bwd(src, qs, ks, vs, bias, sink, dy, dcol, lse, delta, *, rad, nh, nkv, name):
    bl, dil, sl, _ = src.shape
    blk = bias.shape[1]
    kw = blk + 2 * rad
    nb = sl // blk
    rep = nh // nkv
    has_sink = sink is not None
    rb = min(dil, BAND_RESIDUES)
    wk, wv = ks[0], vs[0]

    def body(*refs):
        q_all, k_all, v_all, b_ref = refs[:4]
        s_ref = refs[4] if has_sink else None
        do_all, l_all, dl_all = refs[4 + has_sink:7 + has_sink]
        outs = refs[7 + has_sink:]
        dsk_ref = None
        if has_sink:
            dq_all, dk_all, dv_all, db_ref, dsk_ref, kp, vp, dka, dva = outs
        else:
            dq_all, dk_all, dv_all, db_ref, kp, vp, dka, dva = outs

        @pl.when((pl.program_id(0) == 0) & (pl.program_id(1) == 0))
        def _():
            db_ref[...] = jnp.zeros_like(db_ref)
            if has_sink:
                dsk_ref[...] = jnp.zeros_like(dsk_ref)

        for ri in range(rb):
            one_sequence(q_all.at[ri], k_all.at[ri], v_all.at[ri], b_ref, s_ref, do_all.at[ri], l_all.at[ri],
                         dl_all.at[ri], dq_all.at[ri], dk_all.at[ri], dv_all.at[ri], db_ref, dsk_ref, kp, vp, dka, dva)

    def one_sequence(q_ref, k_ref, v_ref, b_ref, s_ref, do_ref, l_ref, dl_ref, dq_ref, dk_ref, dv_ref, db_ref, dsk_ref,
                     kp, vp, dka, dva):
        _fill_padded(kp, k_ref, rad, sl)
        _fill_padded(vp, v_ref, rad, sl)
        dka[...] = jnp.zeros_like(dka)
        dva[...] = jnp.zeros_like(dva)

        def blk_body(i, carry):
            r0 = pl.multiple_of(i * blk, blk)
            qb = q_ref[pl.ds(r0, blk), :]
            kwin = kp[pl.ds(r0, kw), :]
            vwin = vp[pl.ds(r0, kw), :]
            dob = do_ref[pl.ds(r0, blk), :].astype(BF16)
            lb = l_ref[pl.ds(r0, blk), :]
            dlb = dl_ref[pl.ds(r0, blk), :]
            col = r0 - rad + lax.broadcasted_iota(jnp.int32, (blk, kw), 1)
            neg = jnp.where((col >= 0) & (col < sl), 0.0, NEG_INF).astype(F32)
            for h in range(nh):
                g = h // rep
                hs = slice(h * HEAD_DIM, (h + 1) * HEAD_DIM)
                gs = slice(g * HEAD_DIM, (g + 1) * HEAD_DIM)
                qh, kh, vh, doh = qb[:, hs], kwin[:, gs], vwin[:, gs], dob[:, hs]
                lh = lb[:, h * HEAD_DIM:h * HEAD_DIM + 1]
                dlh = dlb[:, h * HEAD_DIM:h * HEAD_DIM + 1]
                s = lax.dot_general(qh, kh, NT, preferred_element_type=F32) + b_ref[h] + neg
                p = jnp.exp(s - lh)
                dp = lax.dot_general(doh, vh, NT, preferred_element_type=F32)
                ds = p * (dp - dlh)
                dsb = ds.astype(BF16)
                dq_ref[pl.ds(r0, blk), hs] = jnp.dot(dsb, kh, preferred_element_type=F32)
                dka[pl.ds(r0, kw), gs] += lax.dot_general(dsb, qh, TN, preferred_element_type=F32)
                dva[pl.ds(r0, kw), gs] += lax.dot_general(p.astype(BF16), doh, TN, preferred_element_type=F32)
                db_ref[h] += ds
                if has_sink:
                    ps = jnp.exp(s_ref[h][0:1, 0:1] - lh)
                    dsk_ref[h] += jnp.broadcast_to(-jnp.sum(ps * dlh, axis=0, keepdims=True), (8, 128))
            return carry

        lax.fori_loop(0, nb, blk_body, 0)
        dk_ref[...] = dka[rad:rad + sl, :]
        dv_ref[...] = dva[rad:rad + sl, :]

    const3 = lambda b, r: (0, 0, 0)
    in_specs = [_band_spec(sl, qs, rb), _band_spec(sl, ks, rb), _band_spec(sl, vs, rb),
                pl.BlockSpec((nh, blk, kw), const3)]
    args = [src] * 3 + [bias]
    if has_sink:
        in_specs.append(pl.BlockSpec((nh, 8, 128), const3))
        args.append(sink)
    row = _band_spec(sl, (256, 0), rb)
    in_specs += [_band_spec(sl, (256, dcol), rb), row, row]
    args += [dy, lse, delta]
    out_specs = [row, _band_spec(sl, (wk, 0), rb), _band_spec(sl, (wv, 0), rb), pl.BlockSpec((nh, blk, kw), const3)]
    out_shape = [_sds((bl, dil, sl, 256), F32), _sds((bl, dil, sl, wk), F32), _sds((bl, dil, sl, wv), F32),
                 _sds((nh, blk, kw), F32)]
    if has_sink:
        out_specs.append(pl.BlockSpec((nh, 8, 128), const3))
        out_shape.append(_sds((nh, 8, 128), F32))
    return pl.pallas_call(
        body, name=name, grid=(bl, dil // rb), in_specs=in_specs, out_specs=out_specs, out_shape=out_shape,
        scratch_shapes=[pltpu.VMEM((sl + 2 * rad, wk), BF16), pltpu.VMEM((sl + 2 * rad, wv), BF16),
                        pltpu.VMEM((sl + 2 * rad, wk), F32), pltpu.VMEM((sl + 2 * rad, wv), F32)],
        compiler_params=_cparams(ARB, ARB),
    )(*args)


def _combine_a(os_, ls_, *, tm, name):
    bl = os_[1].shape[0]
    t = bl * SEQ
    nt = SEQ // tm

    def body(o1, o4, o16, l1, l4, l16, y_ref, lt_ref, scr):
        for k, (d, ref) in enumerate(((4, o4), (16, o16), (4, l4), (16, l16))):
            for r in range(d):
                _write_residue(scr, 2 * k, r, d, ref[r])
        o2, o3, b, c = (_gather_cols(scr, 2 * k, 2) for k in range(4))
        a = l1[...]
        m = jnp.maximum(jnp.maximum(a, b), c)
        ea, eb, ec = jnp.exp(a - m), jnp.exp(b - m), jnp.exp(c - m)
        den = ea + eb + ec
        y_ref[...] = (ea / den) * o1[...] + (eb / den) * o2 + (ec / den) * o3
        lt_ref[...] = m + jnp.log(den)

    specs = _residue_specs(tm, 256, nt)
    return pl.pallas_call(
        body, name=name, grid=(bl, nt), in_specs=specs * 2, out_specs=[specs[0]] * 2,
        out_shape=[_sds((t, 256), F32)] * 2, scratch_shapes=[pltpu.VMEM((8, tm, LANES), F32)],
        compiler_params=_cparams(PAR, PAR),
    )(*os_, *ls_)


def _deltas(dycat, ya, yb, yd, lse_a, *, tm, name):
    t = ya.shape[0]
    bl = t // SEQ
    nt = SEQ // tm

    def body(dy_ref, ya_ref, yb_ref, yd_ref, la_ref, dy4, dy16, l4, l16, da1, da4, da16, db_ref, dd_ref, scr):
        e = _group_sum_matrix(256, True)
        dya = dy_ref[:, 0:256]
        dla = _seg_sum(dya * ya_ref[...], e)
        da1[...] = dla
        db_ref[...] = _seg_sum(dy_ref[:, 256:512] * yb_ref[...], e)
        dd_ref[...] = _seg_sum(dy_ref[:, 768:1024] * yd_ref[...], e)
        for k, (val, r4, r16) in enumerate(((dya, dy4, dy16), (la_ref[...], l4, l16), (dla, da4, da16))):
            _scatter_cols(scr, 2 * k, val)
            for d, ref in ((4, r4), (16, r16)):
                for r in range(d):
                    ref[r] = _read_residue(scr, 2 * k, 2, r, d)

    specs = _residue_specs(tm, 256, nt)
    nat = specs[0]
    shapes = _residue_shapes(bl, 256, F32)
    outs = pl.pallas_call(
        body, name=name, grid=(bl, nt),
        in_specs=[pl.BlockSpec((tm, 1024), lambda b, i: (b * nt + i, 0)), nat, nat, nat, nat],
        out_specs=specs[1:] + specs[1:] + specs + [nat, nat],
        out_shape=shapes[1:] + shapes[1:] + shapes + [shapes[0], shapes[0]],
        scratch_shapes=[pltpu.VMEM((6, tm, LANES), F32)],
        compiler_params=_cparams(PAR, PAR),
    )(dycat, ya, yb, yd, lse_a)
    return outs[0:2], outs[2:4], outs[4:7], outs[7], outs[8]


def _dense_fwd(qd, *, tq, name):
    t = qd.shape[0]
    bl = t // SEQ
    nq = SEQ // tq

    def body(q_ref, k_ref, v_ref, o_ref, l_ref):
        q = q_ref[...]
        for g in range(2):
            h0, h1 = 2 * g, 2 * g + 1
            q2 = jnp.concatenate([q[:, h0 * 64:(h0 + 1) * 64], q[:, h1 * 64:(h1 + 1) * 64]], axis=0)
            kg = k_ref[:, g * 64:(g + 1) * 64]
            vg = v_ref[:, g * 64:(g + 1) * 64]
            s = lax.dot_general(q2, kg, NT, preferred_element_type=F32)
            m = jnp.max(s, axis=1, keepdims=True)
            p = jnp.exp(s - m)
            den = jnp.sum(p, axis=1, keepdims=True)
            o2 = jnp.dot(p.astype(BF16), vg, preferred_element_type=F32) / den
            l2 = jnp.broadcast_to(m + jnp.log(den), (2 * tq, 64))
            o_ref[:, h0 * 64:(h0 + 1) * 64] = o2[:tq]
            o_ref[:, h1 * 64:(h1 + 1) * 64] = o2[tq:]
            l_ref[:, h0 * 64:(h0 + 1) * 64] = l2[:tq]
            l_ref[:, h1 * 64:(h1 + 1) * 64] = l2[tq:]

    q3 = qd.reshape(bl, SEQ, 512)
    o, lse = pl.pallas_call(
        body, name=name, grid=(bl, nq),
        in_specs=[pl.BlockSpec((None, tq, 256), lambda b, i: (b, i, 0)),
                  pl.BlockSpec((None, SEQ, 128), lambda b, i: (b, 0, 2)),
                  pl.BlockSpec((None, SEQ, 128), lambda b, i: (b, 0, 3))],
        out_specs=[pl.BlockSpec((None, tq, 256), lambda b, i: (b, i, 0))] * 2,
        out_shape=[_sds((bl, SEQ, 256), F32)] * 2,
        compiler_params=_cparams(PAR, PAR),
    )(q3, q3, q3)
    return o.reshape(t, 256), lse.reshape(t, 256)


def _dense_bwd(qd, dycat, lse, delta, *, tq, name):
    t = qd.shape[0]
    bl = t // SEQ
    nq = SEQ // tq

    def body(q_ref, k_ref, v_ref, do_ref, l_ref, dl_ref, dq_ref, dk_ref, dv_ref, dkt, dvt):
        @pl.when(pl.program_id(1) == 0)
        def _():
            dkt[...] = jnp.zeros_like(dkt)
            dvt[...] = jnp.zeros_like(dvt)

        q = q_ref[...]
        do = do_ref[...].astype(BF16)
        lv = l_ref[...]
        dlv = dl_ref[...]
        for g in range(2):
            h0, h1 = 2 * g, 2 * g + 1
            q2 = jnp.concatenate([q[:, h0 * 64:(h0 + 1) * 64], q[:, h1 * 64:(h1 + 1) * 64]], axis=0)
            do2 = jnp.concatenate([do[:, h0 * 64:(h0 + 1) * 64], do[:, h1 * 64:(h1 + 1) * 64]], axis=0)
            l2 = jnp.concatenate([lv[:, h0 * 64:h0 * 64 + 1], lv[:, h1 * 64:h1 * 64 + 1]], axis=0)
            dl2 = jnp.concatenate([dlv[:, h0 * 64:h0 * 64 + 1], dlv[:, h1 * 64:h1 * 64 + 1]], axis=0)
            kg = k_ref[:, g * 64:(g + 1) * 64]
            vg = v_ref[:, g * 64:(g + 1) * 64]
            s = lax.dot_general(q2, kg, NT, preferred_element_type=F32)
            p = jnp.exp(s - l2)
            dp = lax.dot_general(do2, vg, NT, preferred_element_type=F32)
            ds = (p * (dp - dl2)).astype(BF16)
            dq2 = jnp.dot(ds, kg, preferred_element_type=F32)
            dq_ref[:, h0 * 64:(h0 + 1) * 64] = dq2[:tq]
            dq_ref[:, h1 * 64:(h1 + 1) * 64] = dq2[tq:]
            dkt[g * 64:(g + 1) * 64, :] += lax.dot_general(q2, ds, TN, preferred_element_type=F32)
            dvt[g * 64:(g + 1) * 64, :] += lax.dot_general(do2, p.astype(BF16), TN, preferred_element_type=F32)

        @pl.when(pl.program_id(1) == nq - 1)
        def _():
            dk_ref[...] = dkt[...].T
            dv_ref[...] = dvt[...].T

    q3 = qd.reshape(bl, SEQ, 512)
    tile = pl.BlockSpec((None, tq, 256), lambda b, i: (b, i, 0))
    full = pl.BlockSpec((None, SEQ, 128), lambda b, i: (b, 0, 0))
    dq, dk, dv = pl.pallas_call(
        body, name=name, grid=(bl, nq),
        in_specs=[tile, pl.BlockSpec((None, SEQ, 128), lambda b, i: (b, 0, 2)),
                  pl.BlockSpec((None, SEQ, 128), lambda b, i: (b, 0, 3)),
                  pl.BlockSpec((None, tq, 256), lambda b, i: (b, i, 3)), tile, tile],
        out_specs=[tile, full, full],
        out_shape=[_sds((bl, SEQ, 256), F32), _sds((bl, SEQ, 128), F32), _sds((bl, SEQ, 128), F32)],
        scratch_shapes=[pltpu.VMEM((128, SEQ), F32), pltpu.VMEM((128, SEQ), F32)],
        compiler_params=_cparams(PAR, ARB),
    )(q3, q3, q3, dycat.reshape(bl, SEQ, 1024), lse.reshape(bl, SEQ, 256), delta.reshape(bl, SEQ, 256))
    return dq.reshape(t, 256), dk.reshape(t, 128), dv.reshape(t, 128)


def _c_norm(cv, gam, bet):
    vg = _gelu(cv)
    mu = jnp.mean(vg, axis=-1, keepdims=True)
    xc = vg - mu
    r = lax.rsqrt(jnp.mean(xc * xc, axis=-1, keepdims=True) + EPS)
    xhat = xc * r
    return xhat * gam + bet, xhat, r


def _c_fwd(proj, gam, bet, ws, bst, *, tm, name):
    t = proj.shape[0]
    nch = tm // C_CHUNK

    def body(u_ref, v_ref, g_ref, b_ref, ws_ref, bs_ref, y_ref):
        vn, _, _ = _c_norm(v_ref[...], g_ref[...], b_ref[...])
        vnb = vn.astype(BF16)
        for c in range(nch):
            rows = slice(c * C_CHUNK, (c + 1) * C_CHUNK)
            for g in range(C_GROUPS):
                gs = slice(g * 64, (g + 1) * 64)
                mixed = jnp.dot(ws_ref[g], vnb[rows, gs], preferred_element_type=F32) + bs_ref[:, gs]
                y_ref[rows, gs] = _gelu(u_ref[rows, gs]) * mixed

    vec = pl.BlockSpec((1, 256), lambda i: (0, 0))
    return pl.pallas_call(
        body, name=name, grid=(t // tm,),
        in_specs=[pl.BlockSpec((tm, 256), lambda i: (i, 5)), pl.BlockSpec((tm, 256), lambda i: (i, 6)), vec, vec,
                  pl.BlockSpec((C_GROUPS, C_CHUNK, C_CHUNK), lambda i: (0, 0, 0)),
                  pl.BlockSpec((C_CHUNK, 256), lambda i: (0, 0))],
        out_specs=pl.BlockSpec((tm, 256), lambda i: (i, 0)), out_shape=_sds((t, 256), F32),
        compiler_params=_cparams(PAR),
    )(proj, proj, gam, bet, ws, bst)


def _c_bwd(proj, dycat, gam, bet, ws, wst, bst, *, tm, name):
    t = proj.shape[0]
    nch = tm // C_CHUNK
    nstep = t // tm

    def body(u_ref, v_ref, dy_ref, g_ref, b_ref, ws_ref, wst_ref, bs_ref,
             du_ref, dv_ref, dws_ref, dbs_ref, dg_ref, db_ref, dvn_s):
        step = pl.program_id(0)

        @pl.when(step == 0)
        def _():
            dws_ref[...] = jnp.zeros_like(dws_ref)
            dbs_ref[...] = jnp.zeros_like(dbs_ref)
            dg_ref[...] = jnp.zeros_like(dg_ref)
            db_ref[...] = jnp.zeros_like(db_ref)

        cv = v_ref[...]
        gam_v = g_ref[...]
        vn, xhat, r = _c_norm(cv, gam_v, b_ref[...])
        vnb = vn.astype(BF16)
        for c in range(nch):
            rows = slice(c * C_CHUNK, (c + 1) * C_CHUNK)
            for g in range(C_GROUPS):
                gs = slice(g * 64, (g + 1) * 64)
                cu = u_ref[rows, gs]
                dy = dy_ref[rows, gs]
                mixed = jnp.dot(ws_ref[g], vnb[rows, gs], preferred_element_type=F32) + bs_ref[:, gs]
                du_ref[rows, gs] = dy * mixed * _gelu_grad(cu)
                dmix = dy * _gelu(cu)
                dbs_ref[:, gs] += dmix
                dmb = dmix.astype(BF16)
                dws_ref[g] += lax.dot_general(dmb, vnb[rows, gs], NT, preferred_element_type=F32)
                dvn_s[rows, gs] = jnp.dot(wst_ref[g], dmb, preferred_element_type=F32)
        dvn = dvn_s[...]
        dg_ref[...] += jnp.sum(dvn * xhat, axis=0, keepdims=True)
        db_ref[...] += jnp.sum(dvn, axis=0, keepdims=True)
        dxh = dvn * gam_v
        dvg = r * (dxh - jnp.mean(dxh, axis=-1, keepdims=True) - xhat * jnp.mean(dxh * xhat, axis=-1, keepdims=True))
        dv_ref[...] = dvg * _gelu_grad(cv)

        @pl.when(step == nstep - 1)
        def _():
            dbs_ref[...] = _seg_sum(dbs_ref[...], _group_sum_matrix(256, True))

    vec = pl.BlockSpec((1, 256), lambda i: (0, 0))
    mat = pl.BlockSpec((C_GROUPS, C_CHUNK, C_CHUNK), lambda i: (0, 0, 0))
    bsp = pl.BlockSpec((C_CHUNK, 256), lambda i: (0, 0))
    tile = pl.BlockSpec((tm, 256), lambda i: (i, 0))
    return pl.pallas_call(
        body, name=name, grid=(nstep,),
        in_specs=[pl.BlockSpec((tm, 256), lambda i: (i, 5)), pl.BlockSpec((tm, 256), lambda i: (i, 6)),
                  pl.BlockSpec((tm, 256), lambda i: (i, 2)), vec, vec, mat, mat, bsp],
        out_specs=[tile, tile, mat, bsp, vec, vec],
        out_shape=[_sds((t, 256), F32), _sds((t, 256), F32), _sds((C_GROUPS, C_CHUNK, C_CHUNK), F32),
                   _sds((C_CHUNK, 256), F32), _sds((1, 256), F32), _sds((1, 256), F32)],
        scratch_shapes=[pltpu.VMEM((tm, 256), F32)],
        compiler_params=_cparams(ARB),
    )(proj, proj, dycat, gam, bet, ws, wst, bst)


FF_TC = 128
FF_NB = D_FF // FF_TC
FF_CH = 64
FF_HALO = 16


def _taps(ref, r0, win, where):
    z = jnp.zeros((FF_HALO, win.shape[1]), F32)
    if where == "first":
        win[0:FF_HALO, :] = z
        win[FF_HALO:, :] = ref[0:FF_CH + FF_HALO, :].astype(F32)
    elif where == "last":
        win[0:FF_CH + FF_HALO, :] = ref[SEQ - FF_CH - FF_HALO:SEQ, :].astype(F32)
        win[FF_CH + FF_HALO:, :] = z
    else:
        win[...] = ref[pl.ds(pl.multiple_of(r0 - FF_HALO, FF_HALO), FF_CH + 2 * FF_HALO), :].astype(F32)
    return tuple(win[FF_HALO + o:FF_HALO + o + FF_CH, :] for o in (-1, 0, 1))


def _chunk_loop(step):
    step(0, lambda ref, win: _taps(ref, 0, win, "first"))

    def mid(i, carry):
        r0 = pl.multiple_of(i * FF_CH, FF_CH)
        step(r0, lambda ref, win: _taps(ref, r0, win, "mid"))
        return carry

    lax.fori_loop(1, SEQ // FF_CH - 1, mid, 0)
    step(SEQ - FF_CH, lambda ref, win: _taps(ref, SEQ - FF_CH, win, "last"))


def _conv3(taps, w_ref, b_ref):
    dn, md, up = taps
    return w_ref[0:1, :] * dn + w_ref[1:2, :] * md + w_ref[2:3, :] * up + b_ref[...]


def _ff_specs(order):
    def at(fn):
        return (lambda b, j: fn(b, j)) if order == "bj" else (lambda j, b: fn(b, j))
    hs = [pl.BlockSpec((None, SEQ, FF_TC), at(lambda b, j, o=o: (b, 0, j + o))) for o in (0, FF_NB)]
    ws = [pl.BlockSpec((3, FF_TC), at(lambda b, j, o=o: (0, j + o))) for o in (0, FF_NB)]
    bs = [pl.BlockSpec((1, FF_TC), at(lambda b, j, o=o: (0, j + o))) for o in (0, FF_NB)]
    return hs, ws, bs


def _conv_gate_fwd(h, cw, cb, *, name):
    t = h.shape[0]
    bl = t // SEQ

    def body(hg_ref, hu_ref, wg_ref, wu_ref, bg_ref, bu_ref, a_ref, win):
        def step(r0, taps):
            cg = _conv3(taps(hg_ref, win.at[0]), wg_ref, bg_ref)
            cu = _conv3(taps(hu_ref, win.at[1]), wu_ref, bu_ref)
            a_ref[pl.ds(r0, FF_CH), :] = (cg * _sigmoid(cg) * cu).astype(BF16)

        _chunk_loop(step)

    hs, ws, bs = _ff_specs("bj")
    h3 = h.reshape(bl, SEQ, 2 * D_FF)
    act = pl.pallas_call(
        body, name=name, grid=(bl, FF_NB), in_specs=hs + ws + bs,
        out_specs=pl.BlockSpec((None, SEQ, FF_TC), lambda b, j: (b, 0, j)),
        out_shape=_sds((bl, SEQ, D_FF), BF16),
        scratch_shapes=[pltpu.VMEM((2, FF_CH + 2 * FF_HALO, FF_TC), F32)],
        compiler_params=_cparams(PAR, PAR),
    )(h3, h3, cw, cw, cb, cb)
    return act.reshape(t, D_FF)


def _conv_gate_bwd(h, dact, cw, cb, *, name):
    t = h.shape[0]
    bl = t // SEQ

    def body(hg_ref, hu_ref, wg_ref, wu_ref, bg_ref, bu_ref, da_ref,
             dhg_ref, dhu_ref, dwg_ref, dwu_ref, dbg_ref, dbu_ref, dg_s, du_s, win, sums):
        @pl.when(pl.program_id(1) == 0)
        def _():
            for ref in (dwg_ref, dwu_ref, dbg_ref, dbu_ref):
                ref[...] = jnp.zeros_like(ref)

        sums[...] = jnp.zeros_like(sums)
        red = lambda x: jnp.sum(x.reshape(FF_CH // 8, 8, x.shape[1]), axis=0)

        def pass1(r0, taps):
            tg, tu = taps(hg_ref, win.at[0]), taps(hu_ref, win.at[1])
            cg = _conv3(tg, wg_ref, bg_ref)
            cu = _conv3(tu, wu_ref, bu_ref)
            da = da_ref[pl.ds(r0, FF_CH), :].astype(F32)
            sg = _sigmoid(cg)
            dcg = da * cu * (sg * (1.0 + cg * (1.0 - sg)))
            dcu = da * (cg * sg)
            dg_s[pl.ds(r0, FF_CH), :] = dcg
            du_s[pl.ds(r0, FF_CH), :] = dcu
            for half, (d, tp) in enumerate(((dcg, tg), (dcu, tu))):
                for k in range(3):
                    sums[4 * half + k] += red(d * tp[k])
                sums[4 * half + 3] += red(d)

        _chunk_loop(pass1)
        for half, (dw_ref, db_ref) in enumerate(((dwg_ref, dbg_ref), (dwu_ref, dbu_ref))):
            for k in range(3):
                dw_ref[k:k + 1, :] += jnp.sum(sums[4 * half + k], axis=0, keepdims=True)
            db_ref[...] += jnp.sum(sums[4 * half + 3], axis=0, keepdims=True)

        def pass2(r0, taps):
            for k, (s, w_ref, o_ref) in enumerate(((dg_s, wg_ref, dhg_ref), (du_s, wu_ref, dhu_ref))):
                dn, md, up = taps(s, win.at[k])
                o_ref[pl.ds(r0, FF_CH), :] = (w_ref[0:1, :] * up + w_ref[1:2, :] * md + w_ref[2:3, :] * dn).astype(BF16)

        _chunk_loop(pass2)

    hs, ws, bs = _ff_specs("jb")
    half = pl.BlockSpec((None, SEQ, FF_TC), lambda j, b: (b, 0, j))
    wsp = pl.BlockSpec((3, FF_TC), lambda j, b: (0, j))
    bsp = pl.BlockSpec((1, FF_TC), lambda j, b: (0, j))
    h3 = h.reshape(bl, SEQ, 2 * D_FF)
    dhg, dhu, dwg, dwu, dbg, dbu = pl.pallas_call(
        body, name=name, grid=(FF_NB, bl), in_specs=hs + ws + bs + [half],
        out_specs=[half, half, wsp, wsp, bsp, bsp],
        out_shape=[_sds((bl, SEQ, D_FF), BF16), _sds((bl, SEQ, D_FF), BF16), _sds((3, D_FF), F32), _sds((3, D_FF), F32),
                   _sds((1, D_FF), F32), _sds((1, D_FF), F32)],
        scratch_shapes=[pltpu.VMEM((SEQ, FF_TC), F32), pltpu.VMEM((SEQ, FF_TC), F32),
                        pltpu.VMEM((2, FF_CH + 2 * FF_HALO, FF_TC), F32), pltpu.VMEM((8, 8, FF_TC), F32)],
        compiler_params=_cparams(PAR, ARB),
    )(h3, h3, cw, cw, cb, cb, dact.reshape(bl, SEQ, D_FF))
    return (dhg.reshape(t, D_FF), dhu.reshape(t, D_FF), jnp.concatenate([dwg, dwu], axis=1),
            jnp.concatenate([dbg, dbu], axis=1))


def _ple_fwd(x2, gain, wg, pe, pe_blk, wp, *, tm, name):
    t, k = x2.shape

    def body(x_ref, g_ref, wg_ref, pe_ref, wp_ref, hn_ref, x3_ref, gt_ref, pp_ref):
        x = x_ref[...]
        r = lax.rsqrt(jnp.mean(x * x, axis=-1, keepdims=True) + EPS)
        hn = (x * r * g_ref[...]).astype(BF16)
        hn_ref[...] = hn
        gate = _sigmoid(jnp.dot(hn, wg_ref[...], preferred_element_type=F32))
        pp = jnp.dot(pe_ref[...].astype(BF16), wp_ref[...], preferred_element_type=F32)
        gt_ref[...] = gate.astype(BF16)
        pp_ref[...] = pp.astype(BF16)
        x3_ref[...] = x + pp * gate

    row = pl.BlockSpec((tm, k), lambda i: (i, 0))
    return pl.pallas_call(
        body, name=name, grid=(t // tm,),
        in_specs=[row, pl.BlockSpec((1, k), lambda i: (0, 0)), pl.BlockSpec((k, k), lambda i: (0, 0)),
                  pl.BlockSpec((tm, PLE_DIM), lambda i: (pe_blk + i, 0)), pl.BlockSpec((PLE_DIM, k), lambda i: (0, 0))],
        out_specs=[row, row, row, row],
        out_shape=[_sds((t, k), BF16), _sds((t, k), F32), _sds((t, k), BF16), _sds((t, k), BF16)],
        compiler_params=_cparams(PAR),
    )(x2, gain, wg, pe, wp)


def _ple_bwd_ew(dx3, gate, pp, *, tm, name):
    t, n = dx3.shape

    def body(d_ref, g_ref, p_ref, dz_ref, dpp_ref):
        d, g = d_ref[...], g_ref[...]
        dz_ref[...] = (d * p_ref[...] * g * (1.0 - g)).astype(BF16)
        dpp_ref[...] = (d * g).astype(BF16)

    spec = pl.BlockSpec((tm, n), lambda i: (i, 0))
    return pl.pallas_call(
        body, name=name, grid=(t // tm,), in_specs=[spec] * 3, out_specs=[spec] * 2,
        out_shape=[_sds((t, n), BF16)] * 2, compiler_params=_cparams(PAR),
    )(dx3, gate, pp)


def _loss_head(y, tgt, *, tm, name):
    t, d = y.shape

    def body(y_ref, t_ref, l_ref, dy_ref):
        @pl.when(pl.program_id(0) == 0)
        def _():
            l_ref[...] = jnp.zeros_like(l_ref)

        e = y_ref[...] - t_ref[...]
        dy_ref[...] = e * (1.0 / d)
        s = jnp.sum(jnp.sum(e * e, axis=1, keepdims=True), axis=0, keepdims=True)
        l_ref[...] += jnp.broadcast_to(s * (0.5 / d), (8, 128))

    spec = pl.BlockSpec((tm, d), lambda i: (i, 0))
    return pl.pallas_call(
        body, name=name, grid=(t // tm,), in_specs=[spec, spec],
        out_specs=[pl.BlockSpec((8, 128), lambda i: (0, 0)), spec],
        out_shape=[_sds((8, 128), F32), _sds((t, d), F32)], compiler_params=_cparams(ARB),
    )(y, tgt)


BIAS_PC = 8192


def _onehot(bucket_row):
    rows = lax.broadcasted_iota(jnp.int32, (REL_BUCKETS, bucket_row.shape[1]), 0)
    return (rows == bucket_row).astype(BF16)


def _dot3(x, onehot, dims):
    acc = None
    for _ in range(3):
        term = x.astype(BF16)
        part = lax.dot_general(term, onehot, dims, preferred_element_type=F32)
        acc = part if acc is None else acc + part
        x = x - term.astype(F32)
    return acc


def _bias_lookup(table_t, bucket, *, name):
    h = table_t.shape[0]
    p = bucket.shape[1]

    def body(t_ref, b_ref, o_ref):
        bk = b_ref[...]
        val = _dot3(t_ref[...], _onehot(bk), (((1,), (0,)), ((), ())))
        o_ref[...] = jnp.where(bk >= 0, val, NEG_INF)

    return pl.pallas_call(
        body, name=name, grid=(p // BIAS_PC,),
        in_specs=[pl.BlockSpec((h, REL_BUCKETS), lambda i: (0, 0)), pl.BlockSpec((1, BIAS_PC), lambda i: (0, i))],
        out_specs=pl.BlockSpec((h, BIAS_PC), lambda i: (0, i)), out_shape=_sds((h, p), F32),
        compiler_params=_cparams(PAR),
    )(table_t, bucket)


def _bucket_reduce(dbiases, bucket, *, name):
    h, p = dbiases[0].shape
    nl = len(dbiases)

    def body(*refs):
        b_ref, o_ref = refs[nl], refs[nl + 1]

        @pl.when(pl.program_id(0) == 0)
        def _():
            o_ref[...] = jnp.zeros_like(o_ref)

        d = refs[0][...]
        for d_ref in refs[1:nl]:
            d = d + d_ref[...]
        o_ref[...] += _dot3(d, _onehot(b_ref[...]), NT)

    return pl.pallas_call(
        body, name=name, grid=(p // BIAS_PC,),
        in_specs=[pl.BlockSpec((h, BIAS_PC), lambda i: (0, i))] * nl + [pl.BlockSpec((1, BIAS_PC), lambda i: (0, i))],
        out_specs=pl.BlockSpec((h, REL_BUCKETS), lambda i: (0, 0)), out_shape=_sds((h, REL_BUCKETS), F32),
        compiler_params=_cparams(ARB),
    )(*dbiases, bucket)


def _adamw_math(w, g, m, v):
    m = ADAM_B1 * m + (1.0 - ADAM_B1) * g
    v = ADAM_B2 * v + (1.0 - ADAM_B2) * (g * g)
    m_hat = m / (1.0 - ADAM_B1 ** ADAM_STEP)
    v_hat = v / (1.0 - ADAM_B2 ** ADAM_STEP)
    delta = -ADAM_LR * (m_hat / (jnp.sqrt(v_hat) + ADAM_EPS) + ADAM_WD * w)
    return delta, m, v


def _adamw_reduce(parts, w, m, v, *, tr, name):
    nl = len(parts)
    rows, c = w.shape
    r = rows // nl
    nt = r // tr

    def body(*refs):
        p_refs = refs[:nl]
        w_ref, m_ref, v_ref, g_ref, d_ref, nm_ref, nv_ref = refs[nl:]
        for li, p_ref in enumerate(p_refs):
            @pl.when(pl.program_id(0) == li)
            def _(p_ref=p_ref):
                g = p_ref[0].astype(F32)
                for k in range(1, N_DEV):
                    g = g + p_ref[k].astype(F32)
                d, nm, nv = _adamw_math(w_ref[...], g, m_ref[...], v_ref[...])
                g_ref[...] = g
                d_ref[...] = d
                nm_ref[...] = nm
                nv_ref[...] = nv

    def part_map(li):
        return lambda l, i: (0, jnp.where(l == li, i, jnp.where(l < li, 0, nt - 1)), 0)

    spec = pl.BlockSpec((tr, c), lambda l, i: (l * nt + i, 0))
    return pl.pallas_call(
        body, name=name, grid=(nl, nt),
        in_specs=[pl.BlockSpec((N_DEV, tr, c), part_map(li)) for li in range(nl)] + [spec, spec, spec],
        out_specs=[spec] * 4, out_shape=[_sds((rows, c), F32)] * 4, compiler_params=_cparams(ARB, ARB),
    )(*parts, w, m, v)


def _adamw_plain(g, w, m, v, *, name):
    def body(g_ref, w_ref, m_ref, v_ref, d_ref, nm_ref, nv_ref):
        d, nm, nv = _adamw_math(w_ref[...], g_ref[...], m_ref[...], v_ref[...])
        d_ref[...] = d
        nm_ref[...] = nm
        nv_ref[...] = nv

    return pl.pallas_call(body, name=name, out_shape=[_sds(w.shape, F32)] * 3)(g, w, m, v)


def _mesh_pos():
    return lax.axis_index("x"), lax.axis_index("y"), lax.axis_index("c")


def _allgather_body(x_refs, out_refs, send_sems, recv_sems, local_sems, slot):
    x, y, c = _mesh_pos()
    me, sibling = (x, y, c), (x, y, 1 - c)
    chips = [(1 - x, y), (x, 1 - y), (1 - x, 1 - y)]
    waits = []
    for a, (x_ref, out_ref) in enumerate(zip(x_refs, out_refs)):
        def copy(k, block, to, src=None, out_ref=out_ref, a=a):
            return pltpu.make_async_remote_copy(
                src_ref=slot(out_ref, block) if src is None else src, dst_ref=slot(out_ref, block),
                send_sem=send_sems.at[a, k], recv_sem=recv_sems.at[a, k], device_id=to, device_id_type=MESH)

        mine = pltpu.make_async_copy(x_ref, slot(out_ref, me), local_sems.at[a])
        mine.start()
        first = [copy(0, me, sibling, src=x_ref)]
        first += [copy(1 + j, me, (*chip, c), src=x_ref) for j, chip in enumerate(chips)]
        for cp in first:
            cp.start()
        waits.append((copy, mine, first))
    sends = []
    for copy, mine, first in waits:
        passed = [copy(4 + j, (*chip, c), sibling) for j, chip in enumerate(chips)]
        for j, chip in enumerate(chips):
            copy(1 + j, (*chip, c), me).wait_recv()
            passed[j].start()
        sends.append(passed)
    for (copy, mine, first), passed in zip(waits, sends):
        copy(0, sibling, me).wait_recv()
        for j, chip in enumerate(chips):
            copy(4 + j, (*chip, 1 - c), me).wait_recv()
        for cp in first + passed:
            cp.wait_send()
        mine.wait()


PEER_FLIPS = ((0, 0, 1), (1, 0, 0), (0, 1, 0), (1, 1, 0), (1, 0, 1), (0, 1, 1), (1, 1, 1))


def _peer_copies(x_refs, land_refs, send_sem, recv_sem, scatter):
    x, y, c = _mesh_pos()
    me = 4 * x + 2 * y + c
    copies = []
    for x_ref, land_ref in zip(x_refs, land_refs):
        for fx, fy, fc in PEER_FLIPS:
            px, py, pc = x ^ fx, y ^ fy, c ^ fc
            src = x_ref.at[4 * px + 2 * py + pc] if scatter else x_ref
            copies.append(pltpu.make_async_remote_copy(
                src_ref=src, dst_ref=land_ref.at[me], send_sem=send_sem, recv_sem=recv_sem,
                device_id=(px, py, pc), device_id_type=MESH))
    return copies


def _sc_exchange(xs, *, scatter, collective_id, name):
    na = len(xs)
    land_shapes = [x.shape if scatter else (N_DEV,) + x.shape for x in xs]

    def body(*refs):
        x_refs, land_refs = refs[:na], refs[na:2 * na]
        send_sem, recv_sem, local_sem = refs[2 * na:]
        x, y, c = _mesh_pos()
        me = 4 * x + 2 * y + c
        barrier = pltpu.get_barrier_semaphore()
        for fx, fy, fc in PEER_FLIPS:
            pl.semaphore_signal(barrier, inc=1, device_id=(x ^ fx, y ^ fy, c ^ fc), device_id_type=MESH)
        pl.semaphore_wait(barrier, len(PEER_FLIPS))
        for x_ref, land_ref in zip(x_refs, land_refs):
            own = pltpu.make_async_copy(x_ref.at[me] if scatter else x_ref, land_ref.at[me], local_sem)
            own.start()
            own.wait()
        copies = _peer_copies(x_refs, land_refs, send_sem, recv_sem, scatter)
        for cp in copies:
            cp.start()
        for cp in copies:
            cp.wait()

    return pl.kernel(
        body, name=name, out_type=[_sds(s, x.dtype) for s, x in zip(land_shapes, xs)],
        mesh=plsc.ScalarSubcoreMesh(axis_name="sequencer", num_cores=1),
        scratch_types=[pltpu.SemaphoreType.DMA, pltpu.SemaphoreType.DMA, pltpu.SemaphoreType.DMA],
        compiler_params=pltpu.CompilerParams(collective_id=collective_id),
    )(*xs)


def _sc_allgather(xs, *, collective_id, name):
    na = len(xs)

    def body(*refs):
        x_refs, out_refs = refs[:na], refs[na:2 * na]
        send_sems, recv_sems, local_sems = refs[2 * na:]
        x, y, c = _mesh_pos()
        barrier = pltpu.get_barrier_semaphore()
        for fx, fy, fc in PEER_FLIPS:
            pl.semaphore_signal(barrier, inc=1, device_id=(x ^ fx, y ^ fy, c ^ fc), device_id_type=MESH)
        pl.semaphore_wait(barrier, len(PEER_FLIPS))
        _allgather_body(x_refs, out_refs, send_sems, recv_sems, local_sems,
                        lambda ref, pos: ref.at[4 * pos[0] + 2 * pos[1] + pos[2]])

    return pl.kernel(
        body, name=name, out_type=[_sds((N_DEV,) + x.shape, x.dtype) for x in xs],
        mesh=plsc.ScalarSubcoreMesh(axis_name="sequencer", num_cores=1),
        scratch_types=[pltpu.SemaphoreType.DMA((na, 7)), pltpu.SemaphoreType.DMA((na, 7)),
                       pltpu.SemaphoreType.DMA((na,))],
        compiler_params=pltpu.CompilerParams(collective_id=collective_id),
    )(*xs)


def _allgather_vmem(x, *, reduce, name):
    r, c = x.shape

    def body(x_ref, out_ref, *rest):
        if reduce:
            gath, send_sems, recv_sems, local_sems = rest
        else:
            send_sems, recv_sems, local_sems = rest
            gath = out_ref
        _allgather_body([x_ref], [gath], send_sems, recv_sems, local_sems,
                        lambda ref, pos: ref.at[pl.ds((4 * pos[0] + 2 * pos[1] + pos[2]) * r, r), :])
        if reduce:
            acc = gath[0:r, :]
            for k in range(1, N_DEV):
                acc = acc + gath[k * r:(k + 1) * r, :]
            out_ref[...] = acc

    vm = pl.BlockSpec(memory_space=pltpu.VMEM)
    scratch = [pltpu.SemaphoreType.DMA((1, 7)), pltpu.SemaphoreType.DMA((1, 7)), pltpu.SemaphoreType.DMA((1,))]
    if reduce:
        scratch = [pltpu.VMEM((N_DEV * r, c), x.dtype)] + scratch
    return pl.pallas_call(
        body, name=name, in_specs=[vm], out_specs=vm,
        out_shape=_sds((r, c) if reduce else (N_DEV * r, c), x.dtype), scratch_shapes=scratch,
    )(x)


def _t5_bucket(rel):
    nb = REL_BUCKETS // 2
    ret = jnp.where(rel > 0, nb, 0)
    n = jnp.abs(rel)
    max_exact = nb // 2
    nf = jnp.maximum(n, 1).astype(F32)
    large = max_exact + (jnp.log(nf / max_exact) / math.log(REL_MAX_DIST / max_exact)
                         * (nb - max_exact)).astype(jnp.int32)
    large = jnp.minimum(large, nb - 1)
    return ret + jnp.where(n < max_exact, n, large)


def _band_pattern(block, radius, dil):
    kw = block + 2 * radius
    rel = jnp.arange(kw)[None, :] - radius - jnp.arange(block)[:, None]
    return jnp.where(jnp.abs(rel) <= radius, _t5_bucket(rel * dil), -1).astype(jnp.int32).reshape(1, block * kw)


def _rope_tables():
    lane = np.arange(64)
    seg, j = lane // 32, lane % 32
    inv = ROPE_THETA ** (-jnp.arange(0, 32, 2, dtype=F32) / 32)
    tpos = jnp.arange(SEQ)
    pos = jnp.where(jnp.asarray(seg)[None, :] == 0, (tpos // GRID_W)[:, None], (tpos % GRID_W)[:, None])
    ang = pos.astype(F32) * inv[jnp.asarray(j % 16)][None, :]
    cos = jnp.cos(ang)
    sins = jnp.where(jnp.asarray(j)[None, :] < 16, -jnp.sin(ang), jnp.sin(ang))
    return jnp.tile(cos, (1, 4)), jnp.tile(sins, (1, 4))


A_Q, A_K, A_V = (256, 0), (256, 1), (256, 2)
B_Q, B_K, B_V = (256, 0), (128, 2), (128, 3)
A_HEADS = dict(rad=A_RADIUS, nh=4, nkv=4)
B_HEADS = dict(rad=SWA_RADIUS, nh=4, nkv=2)


def _local_step(x, pe, tgt, rel_bias, wts, matmul_weights, grads_ready):
    t = x.shape[0]
    bl = t // SEQ
    cos, sins = _rope_tables()
    blocks_a = [min(BAND_BLOCK, SEQ // d) for d in DILATIONS]
    pats_a = [_band_pattern(blk, A_RADIUS, d) for blk, d in zip(blocks_a, DILATIONS)]
    pat_b = _band_pattern(BAND_BLOCK, SWA_RADIUS, 1)
    table_t = rel_bias.T
    bias_a = [_bias_lookup(table_t[:4], pt, name=f"bias_a{ci}").reshape(4, blk, blk + 2 * A_RADIUS)
              for ci, (pt, blk) in enumerate(zip(pats_a, blocks_a))]
    bias_b = _bias_lookup(table_t[4:], pat_b, name="bias_b").reshape(4, BAND_BLOCK, BAND_BLOCK + 2 * SWA_RADIUS)
    nat4 = lambda a: a.reshape(bl, 1, SEQ, a.shape[-1])

    saved = []
    for li in range(DEPTH):
        w = dict(wts[li])
        w.update(matmul_weights(li, "in", x))
        hn0, proj = _norm_mm((x,), w["g_mix"], w["w_in"], None, tm=1024, tn=1152, name="mix_in_fwd")
        qa1, qa4, qa16, qb, qd = _qkprep_fwd(proj, w["qk_gains"], cos, sins, tm=512, name="qkprep_fwd")
        qa = (nat4(qa1), qa4, qa16)
        oa, la = [], []
        for ci in range(3):
            o, l = _band_fwd(qa[ci], A_Q, A_K, A_V, bias_a[ci], None, name=f"band_a{ci}_fwd", **A_HEADS)
            oa.append(o)
            la.append(l)
        oa[0], la[0] = oa[0].reshape(t, 256), la[0].reshape(t, 256)
        ya, lse_a = _combine_a(oa, la, tm=512, name="combine_a")
        yb, lse_b = _band_fwd(nat4(qb), B_Q, B_K, B_V, bias_b, w["sink_t"], name="band_b_fwd", **B_HEADS)
        yb = yb.reshape(t, 256)
        yc = _c_fwd(proj, w["c_g"], w["c_b"], w["c_ws"], w["c_bst"], tm=512, name="c_fwd")
        yd, lse_d = _dense_fwd(qd, tq=256, name="dense_fwd")
        w.update(matmul_weights(li, "rest", yd))
        mixed, x1 = _norm_mm((ya, yb, yc, yd), w["out_gain"], w["w_out"], x, tm=1024, tn=1024, name="mix_out_fwd")
        hn1, h = _norm_mm((x1,), w["g_ffn"], w["w_up"], None, tm=1024, tn=1408, name="ffn_up_fwd", out_dtype=BF16)
        act = _conv_gate_fwd(h, w["conv_w"], w["conv_b"], name="conv_gate_fwd")
        x2 = _mm(act, w["w_down"], "nn", x1, tm=1024, tn=1024, out_dtype=F32, name="ffn_down_fwd")
        hn2, x3, gate, pp = _ple_fwd(x2, w["g_ple"], w["w_gate"], pe, li * (t // 1024), w["w_proj"], tm=1024,
                                     name="ple_fwd")
        saved.append(dict(w=w, x0=x, hn0=hn0, proj=proj, qa=qa, qb=qb, qd=qd, ya=ya, lse_a=lse_a, yb=yb, lse_b=lse_b,
                          yc=yc, yd=yd, lse_d=lse_d, mixed=mixed, x1=x1, hn1=hn1, h=h, act=act, x2=x2, hn2=hn2,
                          gate=gate, pp=pp))
        x = x3

    loss_tile, dx = _loss_head(x, tgt, tm=512, name="loss_head")
    grads = [None] * DEPTH
    dbias_a, dbias_bs = [[], [], []], []
    for li in reversed(range(DEPTH)):
        s = saved[li]
        w = s["w"]
        g = {}
        dz, dpp = _ple_bwd_ew(dx, s["gate"], s["pp"], tm=512, name="ple_bwd_ew")
        g["w_gate"] = _mm(s["hn2"], dz, "tn", None, tm=1024, tn=512, out_dtype=BF16, name="dw_gate")
        g["w_proj"] = _mm(pe, dpp, "tn", None, tm=256, tn=1024, out_dtype=BF16, name="dw_proj", a_rows=(li, t))
        dx2, dx2b, g["g_ple"] = _mm_bt_normbwd((dz,), w["w_gate"], (s["x2"],), w["g_ple"], dx, tm=1024, tn=1024,
                                               name="ple_bwd", emit_bf16=True)
        g["w_down"] = _mm(s["act"], dx2b, "tn", None, tm=1408, tn=512, out_dtype=BF16, name="dw_down")
        dact = _mm(dx2b, w["w_down"], "nt", None, tm=1024, tn=1408, out_dtype=BF16, name="ffn_down_bwd")
        dhg, dhu, g["conv_w"], g["conv_b"] = _conv_gate_bwd(s["h"], dact, w["conv_w"], w["conv_b"], name="conv_gate_bwd")
        g["w_up"] = jnp.concatenate(
            [_mm(s["hn1"], dhalf, "tn", None, tm=1024, tn=1408, out_dtype=BF16, name=f"dw_up_{nm}")
             for nm, dhalf in (("gate", dhg), ("up", dhu))], axis=1)
        dx1, dx1b, g["g_ffn"] = _mm_bt_normbwd((dhg, dhu), w["w_up"], (s["x1"],), w["g_ffn"], dx2, tm=1024, tn=1408,
                                               name="ffn_up_bwd", emit_bf16=True)
        g["w_out"] = _mm(s["mixed"], dx1b, "tn", None, tm=1024, tn=512, out_dtype=BF16, name="dw_out")
        grads_ready(li, "mid", g)
        dycat, g["out_gain"] = _mm_bt_normbwd((dx1b,), w["w_out"], (s["ya"], s["yb"], s["yc"], s["yd"]), w["out_gain"],
                                              None, tm=1024, tn=1024, name="mix_out_bwd")
        dy_r, lse_r, dl_a, dl_b, dl_d = _deltas(dycat, s["ya"], s["yb"], s["yd"], s["lse_a"], tm=512, name="deltas")
        dy_a = (nat4(dycat),) + tuple(dy_r)
        lse_a = (nat4(s["lse_a"]),) + tuple(lse_r)
        dl_a = (nat4(dl_a[0]),) + tuple(dl_a[1:])
        da = []
        for ci in range(3):
            dq, dk, dv, dbias = _band_bwd(s["qa"][ci], A_Q, A_K, A_V, bias_a[ci], None, dy_a[ci], 0, lse_a[ci],
                                          dl_a[ci], name=f"band_a{ci}_bwd", **A_HEADS)
            if ci == 0:
                dq, dk, dv = (a.reshape(t, 256) for a in (dq, dk, dv))
            da.append((dq, dk, dv))
            dbias_a[ci].append(dbias.reshape(4, -1))
        dqb, dkb, dvb, dbias_b, dsink = _band_bwd(nat4(s["qb"]), B_Q, B_K, B_V, bias_b, w["sink_t"], nat4(dycat), 1,
                                                  nat4(s["lse_b"]), nat4(dl_b), name="band_b_bwd", **B_HEADS)
        dbias_bs.append(dbias_b.reshape(4, -1))
        g["sink"] = dsink[:, 0, 0]
        dd = _dense_bwd(s["qd"], dycat, s["lse_d"], dl_d, tq=256, name="dense_bwd")
        dcu, dcv, g["c_ws"], dbs, g["c_g"], g["c_b"] = _c_bwd(s["proj"], dycat, w["c_g"], w["c_b"], w["c_ws"],
                                                               w["c_wst"], w["c_bst"], tm=512, name="c_bwd")
        g["c_bs"] = dbs[:, ::64].T
        db = (dqb.reshape(t, 256), dkb.reshape(t, 128), dvb.reshape(t, 128))
        dproj, dgains = _qkprep_bwd(s["proj"], da, db, dd, dcu, dcv, w["qk_gains"], cos, sins, tm=512, name="qkprep_bwd")
        g["qk_gain"] = dgains[:6, :64].reshape(3, 2, HEAD_DIM)
        g["w_in"] = _mm(s["hn0"], dproj, "tn", None, tm=1024, tn=1152, out_dtype=BF16, name="dw_in")
        dx, g["g_mix"] = _mm_bt_normbwd((dproj,), w["w_in"], (s["x0"],), w["g_mix"], dx1, tm=1024, tn=1152,
                                        name="mix_in_bwd")
        grads[li] = g
        grads_ready(li, "end", g)
    d_table_a = sum(_bucket_reduce(dbias_a[ci], pats_a[ci], name=f"bucket_a{ci}") for ci in range(3))
    d_table_b = _bucket_reduce(dbias_bs, pat_b, name="bucket_b")
    d_rel_bias = jnp.concatenate([d_table_a, d_table_b], axis=0).T
    return loss_tile[0, 0], dx, grads, d_rel_bias


WEIGHT_NAMES = ("rel_bias", "ln_mix_g", "w_in", "qk_gain", "sink", "c_norm_g", "c_norm_b", "c_ws", "c_bs", "out_gain",
                "w_out", "ln_ffn_g", "w_up", "conv_w", "conv_b", "w_down", "ln_ple_g", "w_ple_gate", "w_ple_proj")
COL_SHARDED = ("w_in", "w_up", "w_ple_proj")
ROW_SHARDED = ("w_out", "w_down", "w_ple_gate")
SMALL_SHARDED = ("conv_w", "out_gain")
REPLICATED = tuple(n for n in WEIGHT_NAMES if n not in COL_SHARDED + ROW_SHARDED + SMALL_SHARDED)
LOCAL_GRAD_KEY = {"ln_mix_g": "g_mix", "ln_ffn_g": "g_ffn", "ln_ple_g": "g_ple", "c_norm_g": "c_g", "c_norm_b": "c_b",
                  "w_ple_gate": "w_gate", "w_ple_proj": "w_proj"}


def _full_from_gathered(name, gathered):
    _, r, c = gathered.shape
    if name in ROW_SHARDED:
        return gathered.reshape(N_DEV * r, c)
    return jnp.transpose(gathered, (1, 0, 2)).reshape(r, N_DEV * c)


def _slots_from_full(name, full):
    rows, cols = full.shape
    if name in ROW_SHARDED:
        return full.reshape(N_DEV, rows // N_DEV, cols)
    return jnp.transpose(full.reshape(rows, N_DEV, cols // N_DEV), (1, 0, 2))


def _piece_rows(shape):
    return -(-int(np.prod(shape)) // 1024) * 8


def _pack_rows(arrays):
    pieces = []
    for a in arrays:
        n, rows = int(np.prod(a.shape)), _piece_rows(a.shape)
        flat = a.astype(F32).reshape(-1)
        if n != rows * LANES:
            flat = jnp.pad(flat, (0, rows * LANES - n))
        pieces.append(flat.reshape(rows, LANES))
    return jnp.concatenate(pieces, axis=0)


def _unpack_rows(packed, shapes):
    out, off = [], 0
    for shp in shapes:
        n, rows = int(np.prod(shp)), _piece_rows(shp)
        piece = packed[off:off + rows]
        out.append((piece if n == rows * LANES else piece.reshape(-1)[:n]).reshape(shp))
        off += rows
    return out


def kernel(x, p, rel_bias, ln_mix_g, w_in, qk_gain, sink, c_norm_g, c_norm_b, c_ws, c_bs, out_gain, w_out, ln_ffn_g, w_up, conv_w, conv_b, w_down, ln_ple_g, w_ple_gate, w_ple_proj, loss_target, m_rel_bias, m_ln_mix_g, m_w_in, m_qk_gain, m_sink, m_c_norm_g, m_c_norm_b, m_c_ws, m_c_bs, m_out_gain, m_w_out, m_ln_ffn_g, m_w_up, m_conv_w, m_conv_b, m_w_down, m_ln_ple_g, m_w_ple_gate, m_w_ple_proj, v_rel_bias, v_ln_mix_g, v_w_in, v_qk_gain, v_sink, v_c_norm_g, v_c_norm_b, v_c_ws, v_c_bs, v_out_gain, v_w_out, v_ln_ffn_g, v_w_up, v_conv_w, v_conv_b, v_w_down, v_ln_ple_g, v_w_ple_gate, v_w_ple_proj):
    env = dict(locals())
    wt = {n: env[n] for n in WEIGHT_NAMES}
    mom_m = {n: env["m_" + n] for n in WEIGHT_NAMES}
    mom_v = {n: env["v_" + n] for n in WEIGHT_NAMES}
    bl = x.shape[0]
    t = bl * SEQ
    me = 4 * lax.axis_index("x") + 2 * lax.axis_index("y") + lax.axis_index("c")

    big = COL_SHARDED + ROW_SHARDED
    full = {}
    small_shapes = [wt[n].shape for n in SMALL_SHARDED]
    small = _allgather_vmem(_pack_rows([wt[n] for n in SMALL_SHARDED]), reduce=False, name="gather_small")
    small = small.reshape(N_DEV, -1)
    off = 0
    for n, shp in zip(SMALL_SHARDED, small_shapes):
        cnt = int(np.prod(shp))
        g = small[:, off:off + cnt].reshape((N_DEV,) + tuple(shp))
        full[n] = jnp.transpose(g, (1, 2, 0, 3)).reshape(shp[0], shp[1], N_DEV * shp[2])
        off += _piece_rows(shp) * LANES

    def head_gain(li, a, b, reps):
        g = jnp.tile(qk_gain[li, a, b], reps)
        return jnp.pad(g, (0, 256 - g.shape[0]))

    wts = []
    for li in range(DEPTH):
        rows = [head_gain(li, 0, 0, 4), head_gain(li, 0, 1, 4), head_gain(li, 1, 0, 4), head_gain(li, 1, 1, 2),
                head_gain(li, 2, 0, 4), head_gain(li, 2, 1, 2), jnp.zeros((256,), F32), jnp.zeros((256,), F32)]
        wts.append(dict(
            g_mix=ln_mix_g[li].reshape(1, -1), qk_gains=jnp.stack(rows),
            sink_t=jnp.broadcast_to(sink[li][:, None, None], (4, 8, 128)),
            c_g=c_norm_g[li].reshape(1, -1), c_b=c_norm_b[li].reshape(1, -1), c_ws=c_ws[li].astype(BF16),
            c_wst=jnp.transpose(c_ws[li], (0, 2, 1)).astype(BF16), c_bst=jnp.repeat(c_bs[li].T, 64, axis=1),
            out_gain=full["out_gain"][li].reshape(1, -1), g_ffn=ln_ffn_g[li].reshape(1, -1),
            conv_w=full["conv_w"][li], conv_b=conv_b[li].reshape(1, -1), g_ple=ln_ple_g[li].reshape(1, -1)))

    local_key = {"w_ple_gate": "w_gate", "w_ple_proj": "w_proj"}

    gather_names = {"in": ("w_in",), "rest": tuple(n for n in big if n != "w_in")}
    gathered = {}
    for cid, (li, names) in enumerate(((0, gather_names["in"]), (0, gather_names["rest"]), (1, big))):
        lands = _sc_allgather([wt[n][li].astype(BF16) for n in names], collective_id=cid,
                              name=f"gather_{li}_{len(names)}")
        gathered.setdefault(li, {}).update(zip(names, lands))

    def matmul_weights(li, part, after):
        out = {}
        for n in gather_names[part]:
            g, _ = lax.optimization_barrier((gathered[li][n], after))
            out[local_key.get(n, n)] = _full_from_gathered(n, g)
        return out

    mid_names = ("w_ple_gate", "w_ple_proj", "w_down", "w_up", "w_out")
    end_names = ("w_in",)
    landed = {}

    def start_exchange(li, names, g, tag, cid):
        slots = [_slots_from_full(n, g[local_key.get(n, n)]) for n in names]
        lands = _sc_exchange(slots, scatter=True, collective_id=cid, name=f"grads_{li}_{tag}")
        landed.update({(n, li): land for n, land in zip(names, lands)})

    def grads_ready(li, stage, g):
        if li == 0:
            start_exchange(li, mid_names if stage == "mid" else end_names, g, stage, 5 if stage == "mid" else 6)
        elif stage == "end":
            start_exchange(li, mid_names + end_names, g, stage, 4)

    loss_part, dx, grads, d_rel_bias = _local_step(
        x.reshape(t, D_MODEL), p.reshape(DEPTH * t, PLE_DIM), loss_target.reshape(t, D_MODEL), rel_bias, wts,
        matmul_weights, grads_ready)
    loss = lax.psum(loss_part, ("x", "y", "c"))

    def local_grad(n):
        if n == "rel_bias":
            return d_rel_bias
        key = LOCAL_GRAD_KEY.get(n, n)
        return jnp.stack([grads[li][key].reshape(wt[n].shape[1:]) if n in REPLICATED else grads[li][key]
                          for li in range(DEPTH)])

    out_g, out_d, out_m, out_v = {}, {}, {}, {}
    for n in big:
        shp = wt[n].shape
        two_d = lambda a: a.reshape(-1, shp[-1])
        res = _adamw_reduce([landed[n, li] for li in range(DEPTH)], two_d(wt[n]), two_d(mom_m[n]), two_d(mom_v[n]),
                            tr=32 if n == "w_down" else 128, name="adamw_" + n)
        out_g[n], out_d[n], out_m[n], out_v[n] = [r.reshape(shp) for r in res]

    small_names = REPLICATED + SMALL_SHARDED
    small_full_shapes = [wt[n].shape if n in REPLICATED else full[n].shape for n in small_names]
    reduced = _allgather_vmem(_pack_rows([local_grad(n) for n in small_names]), reduce=True, name="allreduce_small")
    reduced = dict(zip(small_names, _unpack_rows(reduced, small_full_shapes)))
    rep_shapes = [wt[n].shape for n in REPLICATED]
    upd = _adamw_plain(_pack_rows([reduced[n] for n in REPLICATED]), _pack_rows([wt[n] for n in REPLICATED]),
                       _pack_rows([mom_m[n] for n in REPLICATED]), _pack_rows([mom_v[n] for n in REPLICATED]),
                       name="adamw_replicated")
    for dst, packed in zip((out_d, out_m, out_v), upd):
        dst.update(zip(REPLICATED, _unpack_rows(packed, rep_shapes)))
    for n in REPLICATED:
        out_g[n] = reduced[n]
    for n in SMALL_SHARDED:
        shp = wt[n].shape
        g = reduced[n].reshape(shp[0], shp[1], N_DEV, shp[2])
        g = lax.dynamic_index_in_dim(g, me, axis=2, keepdims=False)
        two_d = lambda a: a.reshape(-1, shp[-1])
        res = _adamw_plain(two_d(g), two_d(wt[n]), two_d(mom_m[n]), two_d(mom_v[n]), name="adamw_" + n)
        out_g[n] = g
        out_d[n], out_m[n], out_v[n] = [r.reshape(shp) for r in res]

    return (loss, dx.reshape(bl, SEQ, D_MODEL), *[out_g[n] for n in WEIGHT_NAMES], *[out_d[n] for n in WEIGHT_NAMES],
            *[out_m[n] for n in WEIGHT_NAMES], *[out_v[n] for n in WEIGHT_NAMES])
```

```python
import math

import jax
import jax.numpy as jnp
import numpy as np
from jax import lax
from jax.experimental import pallas as pl
from jax.experimental.pallas import tpu as pltpu
from jax.experimental.pallas import tpu_sc as plsc

F32 = jnp.float32
BF16 = jnp.bfloat16
HI = lax.Precision.HIGHEST

N_DEV = 8
D_MODEL = 1024
SEQ = 2048
DEPTH = 2
HEAD_DIM = 64
IN_WIDTH = 2304
D_FF = 2816
PLE_DIM = 256
C_CHUNK = 128
C_GROUPS = 4
DILATED_CFGS = ((128, 1), (512, 4), (2048, 16))
DILATIONS = tuple(d for _, d in DILATED_CFGS)
A_RADIUS = 64
SWA_RADIUS = 128
BAND_BLOCK = 256
GRID_W = 64
ROPE_THETA = 10000.0
REL_BUCKETS = 32
REL_MAX_DIST = 1024
EPS = 1e-6
NEG_INF = -1e30
ATTN_SCALE = HEAD_DIM ** -0.5
LANES = 128

ADAM_LR = 0.001
ADAM_B1 = 0.9
ADAM_B2 = 0.999
ADAM_EPS = 1e-08
ADAM_WD = 0.01
ADAM_STEP = 10

MESH = pl.DeviceIdType.MESH
NT = (((1,), (1,)), ((), ()))
TN = (((0,), (0,)), ((), ()))
ARB = "arbitrary"
PAR = "parallel"


def _cparams(*sem):
    return pltpu.CompilerParams(dimension_semantics=tuple(sem))


def _sds(shape, dtype):
    return jax.ShapeDtypeStruct(tuple(shape), dtype)


def _group_sum_matrix(n, same_group):
    r = lax.broadcasted_iota(jnp.int32, (n, n), 0)
    c = lax.broadcasted_iota(jnp.int32, (n, n), 1)
    if same_group:
        return ((r >> 6) == (c >> 6)).astype(F32)
    return ((r & 63) == (c & 63)).astype(F32)


def _seg_sum(x, e):
    eb = e.astype(BF16)
    hi = x.astype(BF16)
    lo = (x - hi.astype(F32)).astype(BF16)
    return jnp.dot(hi, eb, preferred_element_type=F32) + jnp.dot(lo, eb, preferred_element_type=F32)


def _gelu(x):
    c = math.sqrt(2.0 / math.pi)
    return 0.5 * x * (1.0 + jnp.tanh(c * (x + 0.044715 * (x * x * x))))


def _gelu_grad(x):
    c = math.sqrt(2.0 / math.pi)
    t = jnp.tanh(c * (x + 0.044715 * (x * x * x)))
    return 0.5 * (1.0 + t) + 0.5 * x * (1.0 - t * t) * c * (1.0 + 3.0 * 0.044715 * (x * x))


def _sigmoid(x):
    return 1.0 / (1.0 + jnp.exp(-x))


def _scatter_cols(scratch, first, val):
    for c in range(val.shape[1] // LANES):
        scratch[first + c] = val[:, c * LANES:(c + 1) * LANES]


def _gather_cols(scratch, first, ncol):
    return jnp.concatenate([scratch[first + c] for c in range(ncol)], axis=1)


def _read_residue(scratch, first, ncol, r, d):
    n = scratch.shape[1] // d
    return jnp.concatenate([scratch.at[first + c][pl.ds(r, n, stride=d), :] for c in range(ncol)], axis=1)


def _write_residue(scratch, first, r, d, val):
    n = scratch.shape[1] // d
    for c in range(val.shape[1] // LANES):
        scratch.at[first + c][pl.ds(r, n, stride=d), :] = val[:, c * LANES:(c + 1) * LANES]


def _norm_mm(xs, gain, w, res, *, tm, tn, name, out_dtype=F32):
    t = xs[0].shape[0]
    k = sum(x.shape[1] for x in xs)
    n = w.shape[1]
    ng = len(xs)
    has_res = res is not None

    def body(*refs):
        x_refs = refs[:ng]
        g_ref, w_ref = refs[ng], refs[ng + 1]
        res_ref = refs[ng + 2] if has_res else None
        hn_ref, o_ref, hn_s = refs[ng + 2 + has_res:]

        @pl.when(pl.program_id(1) == 0)
        def _():
            off = 0
            for xr in x_refs:
                x = xr[...]
                wd = x.shape[1]
                r = lax.rsqrt(jnp.mean(x * x, axis=-1, keepdims=True) + EPS)
                hn_s[:, off:off + wd] = (x * r * g_ref[:, off:off + wd]).astype(BF16)
                off += wd
            hn_ref[...] = hn_s[...]

        acc = jnp.dot(hn_s[...], w_ref[...], preferred_element_type=F32)
        if has_res:
            acc = acc + res_ref[...]
        o_ref[...] = acc.astype(out_dtype)

    in_specs = [pl.BlockSpec((tm, x.shape[1]), lambda i, j: (i, 0)) for x in xs]
    in_specs += [pl.BlockSpec((1, k), lambda i, j: (0, 0)), pl.BlockSpec((k, tn), lambda i, j: (0, j))]
    args = list(xs) + [gain, w]
    if has_res:
        in_specs.append(pl.BlockSpec((tm, tn), lambda i, j: (i, j)))
        args.append(res)
    return pl.pallas_call(
        body, name=name, grid=(t // tm, n // tn), in_specs=in_specs,
        out_specs=[pl.BlockSpec((tm, k), lambda i, j: (i, 0)), pl.BlockSpec((tm, tn), lambda i, j: (i, j))],
        out_shape=[_sds((t, k), BF16), _sds((t, n), out_dtype)],
        scratch_shapes=[pltpu.VMEM((tm, k), BF16)],
        compiler_params=_cparams(PAR, ARB),
    )(*args)


def _mm(a, b, mode, res, *, tm, tn, out_dtype, name, a_rows=None):
    if mode == "tn":
        kk, m = a.shape
        blk_a = 0
        if a_rows is not None:
            blk_a, kk = a_rows
        a_spec = pl.BlockSpec((kk, tm), lambda i, j: (blk_a, i))
    else:
        m, kk = a.shape
        a_spec = pl.BlockSpec((tm, kk), lambda i, j: (i, 0))
    if mode == "nt":
        n = b.shape[0]
        b_spec = pl.BlockSpec((tn, kk), lambda i, j: (j, 0))
    else:
        n = b.shape[1]
        b_spec = pl.BlockSpec((kk, tn), lambda i, j: (0, j))
    has_res = res is not None

    def body(*refs):
        a_ref, b_ref = refs[0], refs[1]
        o_ref = refs[-1]
        av = a_ref[...].astype(BF16)
        bv = b_ref[...].astype(BF16)
        if mode == "nn":
            acc = jnp.dot(av, bv, preferred_element_type=F32)
        elif mode == "nt":
            acc = lax.dot_general(av, bv, NT, preferred_element_type=F32)
        else:
            acc = lax.dot_general(av, bv, TN, preferred_element_type=F32)
        if has_res:
            acc = acc + refs[2][...]
        o_ref[...] = acc.astype(out_dtype)

    in_specs = [a_spec, b_spec]
    args = [a, b]
    if has_res:
        in_specs.append(pl.BlockSpec((tm, tn), lambda i, j: (i, j)))
        args.append(res)
    return pl.pallas_call(
        body, name=name, grid=(m // tm, n // tn), in_specs=in_specs,
        out_specs=pl.BlockSpec((tm, tn), lambda i, j: (i, j)),
        out_shape=_sds((m, n), out_dtype),
        compiler_params=_cparams(PAR, PAR),
    )(*args)


def _mm_bt_normbwd(dys, w, xs, gain, dres, *, tm, tn, name, emit_bf16=False):
    t, wd_each = dys[0].shape
    nd = len(dys)
    per = wd_each // tn
    nj = nd * per
    k = w.shape[0]
    ng = len(xs)
    has_res = dres is not None

    def body(*refs):
        dy_refs = refs[:nd]
        w_ref = refs[nd]
        x_refs = refs[nd + 1:nd + 1 + ng]
        g_ref = refs[nd + 1 + ng]
        dres_ref = refs[nd + 2 + ng] if has_res else None
        outs = refs[nd + 2 + ng + has_res:]
        dx_ref = outs[0]
        dxb_ref = outs[1] if emit_bf16 else None
        dg_ref, acc = outs[1 + emit_bf16:]
        i, j = pl.program_id(0), pl.program_id(1)

        @pl.when(j == 0)
        def _():
            acc[...] = jnp.zeros_like(acc)

        for d, dy_ref in enumerate(dy_refs):
            @pl.when((j >= d * per) & (j < (d + 1) * per))
            def _(dy_ref=dy_ref):
                acc[...] += lax.dot_general(dy_ref[...].astype(BF16), w_ref[...], NT, preferred_element_type=F32)

        @pl.when(j == nj - 1)
        def _():
            @pl.when(i == 0)
            def _():
                dg_ref[...] = jnp.zeros_like(dg_ref)

            off = 0
            for xr in x_refs:
                x = xr[...]
                wd = x.shape[1]
                g = g_ref[:, off:off + wd]
                dyn = acc[:, off:off + wd]
                r = lax.rsqrt(jnp.mean(x * x, axis=-1, keepdims=True) + EPS)
                gdy = dyn * g
                dx = r * gdy - x * (r * r * r * jnp.mean(gdy * x, axis=-1, keepdims=True))
                if has_res:
                    dx = dx + dres_ref[:, off:off + wd]
                dx_ref[:, off:off + wd] = dx
                if emit_bf16:
                    dxb_ref[:, off:off + wd] = dx.astype(BF16)
                dg_ref[:, off:off + wd] += jnp.sum(dyn * x * r, axis=0, keepdims=True)
                off += wd

    def dy_map(d):
        return lambda i, j: (i, jnp.clip(j - d * per, 0, per - 1))

    in_specs = [pl.BlockSpec((tm, tn), dy_map(d)) for d in range(nd)]
    in_specs.append(pl.BlockSpec((k, tn), lambda i, j: (0, j)))
    in_specs += [pl.BlockSpec((tm, x.shape[1]), lambda i, j: (i, 0)) for x in xs]
    in_specs.append(pl.BlockSpec((1, k), lambda i, j: (0, 0)))
    args = list(dys) + [w] + list(xs) + [gain]
    if has_res:
        in_specs.append(pl.BlockSpec((tm, k), lambda i, j: (i, 0)))
        args.append(dres)
    row = pl.BlockSpec((tm, k), lambda i, j: (i, 0))
    out_specs = [row] + ([row] if emit_bf16 else []) + [pl.BlockSpec((1, k), lambda i, j: (0, 0))]
    out_shape = [_sds((t, k), F32)] + ([_sds((t, k), BF16)] if emit_bf16 else []) + [_sds((1, k), F32)]
    return pl.pallas_call(
        body, name=name, grid=(t // tm, nj), in_specs=in_specs, out_specs=out_specs, out_shape=out_shape,
        scratch_shapes=[pltpu.VMEM((tm, k), F32)],
        compiler_params=_cparams(ARB, ARB),
    )(*args)


def _rope_partner(y):
    n = y.shape[1]
    lane = lax.broadcasted_iota(jnp.int32, y.shape, 1)
    return jnp.where((lane & 31) < 16, pltpu.roll(y, n - 16, 1), pltpu.roll(y, 16, 1))


def _residue_specs(tm, width, nt):
    specs = [pl.BlockSpec((tm, width), lambda b, i: (b * nt + i, 0))]
    for d in DILATIONS[1:]:
        specs.append(pl.BlockSpec((None, d, tm // d, width), lambda b, i: (b, 0, i, 0)))
    return specs


def _residue_shapes(bl, width, dtype):
    return [_sds((bl * SEQ, width), dtype)] + [_sds((bl, d, SEQ // d, width), dtype) for d in DILATIONS[1:]]


def _qkprep_fwd(proj, gains, cos, sins, *, tm, name):
    t = proj.shape[0]
    bl = t // SEQ
    nt = SEQ // tm

    def body(p_ref, g_ref, c_ref, s_ref, qa1_ref, qa4_ref, qa16_ref, qb_ref, qd_ref, scr):
        e = _group_sum_matrix(256, True)

        def hn(x, row):
            wd = x.shape[1]
            ms = _seg_sum(x * x, e[:wd, :wd]) * (1.0 / HEAD_DIM)
            return x * lax.rsqrt(ms + EPS) * g_ref[row:row + 1, :wd]

        qa = jnp.concatenate([hn(p_ref[:, 0:256], 0) * ATTN_SCALE, hn(p_ref[:, 256:512], 1), p_ref[:, 512:768]], axis=1)
        qa1_ref[...] = qa.astype(BF16)
        _scatter_cols(scr, 0, qa)
        for d, ref in ((4, qa4_ref), (16, qa16_ref)):
            for r in range(d):
                ref[r] = _read_residue(scr, 0, 6, r, d).astype(BF16)
        qb_ref[:, 0:256] = (hn(p_ref[:, 768:1024], 2) * ATTN_SCALE).astype(BF16)
        qb_ref[:, 256:384] = hn(p_ref[:, 1024:1152], 3).astype(BF16)
        qb_ref[:, 384:512] = p_ref[:, 1152:1280].astype(BF16)
        yq = hn(p_ref[:, 1792:2048], 4)
        yq = yq * c_ref[...] + _rope_partner(yq) * s_ref[...]
        qd_ref[:, 0:256] = (yq * ATTN_SCALE).astype(BF16)
        yk = hn(p_ref[:, 2048:2176], 5)
        yk = yk * c_ref[:, 0:128] + _rope_partner(yk) * s_ref[:, 0:128]
        qd_ref[:, 256:384] = yk.astype(BF16)
        qd_ref[:, 384:512] = p_ref[:, 2176:2304].astype(BF16)

    row = lambda width: pl.BlockSpec((tm, width), lambda b, i: (b * nt + i, 0))
    tab = pl.BlockSpec((tm, 256), lambda b, i: (i, 0))
    return pl.pallas_call(
        body, name=name, grid=(bl, nt),
        in_specs=[row(IN_WIDTH), pl.BlockSpec((8, 256), lambda b, i: (0, 0)), tab, tab],
        out_specs=_residue_specs(tm, 768, nt) + [row(512), row(512)],
        out_shape=_residue_shapes(bl, 768, BF16) + [_sds((t, 512), BF16), _sds((t, 512), BF16)],
        scratch_shapes=[pltpu.VMEM((6, tm, LANES), F32)],
        compiler_params=_cparams(PAR, PAR),
    )(proj, gains, cos, sins)


def _qkprep_bwd(proj, da, db, dd, dcu, dcv, gains, cos, sins, *, tm, name):
    t = proj.shape[0]
    bl = t // SEQ
    nt = SEQ // tm
    flat = [a for cfg in da for a in cfg] + list(db) + list(dd) + [dcu, dcv]

    def body(*refs):
        p_ref, g_ref, c_ref, s_ref = refs[:4]
        d_refs = refs[4:4 + len(flat)]
        dp_ref, dg_ref, scr = refs[4 + len(flat):]
        a_refs = d_refs[:9]
        dqb_ref, dkb_ref, dvb_ref, dqd_ref, dkd_ref, dvd_ref, dcu_ref, dcv_ref = d_refs[9:]
        e = _group_sum_matrix(256, True)
        first = (pl.program_id(0) == 0) & (pl.program_id(1) == 0)
        last = (pl.program_id(0) == bl - 1) & (pl.program_id(1) == nt - 1)

        @pl.when(first)
        def _():
            dg_ref[...] = jnp.zeros_like(dg_ref)

        def hn_bwd(x, dy, row):
            wd = x.shape[1]
            ee = e[:wd, :wd]
            g = g_ref[row:row + 1, :wd]
            r = lax.rsqrt(_seg_sum(x * x, ee) * (1.0 / HEAD_DIM) + EPS)
            gdy = dy * g
            dx = r * gdy - x * (r * r * r * (_seg_sum(gdy * x, ee) * (1.0 / HEAD_DIM)))
            dg_ref[row:row + 1, :wd] += jnp.sum(dy * x * r, axis=0, keepdims=True)
            return dx

        def rope_bwd(dy, wd):
            return dy * c_ref[:, :wd] + _rope_partner(dy * s_ref[:, :wd])

        dqkv = jnp.concatenate([a_refs[0][...], a_refs[1][...], a_refs[2][...]], axis=1)
        for ci, d in ((1, 4), (2, 16)):
            for r in range(d):
                part = jnp.concatenate([a_refs[3 * ci + m][r] for m in range(3)], axis=1)
                _write_residue(scr, 0, r, d, part)
            dqkv = dqkv + _gather_cols(scr, 0, 6)
        dp_ref[:, 0:256] = hn_bwd(p_ref[:, 0:256], dqkv[:, 0:256] * ATTN_SCALE, 0).astype(BF16)
        dp_ref[:, 256:512] = hn_bwd(p_ref[:, 256:512], dqkv[:, 256:512], 1).astype(BF16)
        dp_ref[:, 512:768] = dqkv[:, 512:768].astype(BF16)
        dp_ref[:, 768:1024] = hn_bwd(p_ref[:, 768:1024], dqb_ref[...] * ATTN_SCALE, 2).astype(BF16)
        dp_ref[:, 1024:1152] = hn_bwd(p_ref[:, 1024:1152], dkb_ref[...], 3).astype(BF16)
        dp_ref[:, 1152:1280] = dvb_ref[...].astype(BF16)
        dp_ref[:, 1280:1536] = dcu_ref[...].astype(BF16)
        dp_ref[:, 1536:1792] = dcv_ref[...].astype(BF16)
        dp_ref[:, 1792:2048] = hn_bwd(p_ref[:, 1792:2048], rope_bwd(dqd_ref[...] * ATTN_SCALE, 256), 4).astype(BF16)
        dp_ref[:, 2048:2176] = hn_bwd(p_ref[:, 2048:2176], rope_bwd(dkd_ref[...], 128), 5).astype(BF16)
        dp_ref[:, 2176:2304] = dvd_ref[...].astype(BF16)

        @pl.when(last)
        def _():
            dg_ref[...] = _seg_sum(dg_ref[...], _group_sum_matrix(256, False))

    row = lambda width: pl.BlockSpec((tm, width), lambda b, i: (b * nt + i, 0))
    tab = pl.BlockSpec((tm, 256), lambda b, i: (i, 0))
    in_specs = [row(IN_WIDTH), pl.BlockSpec((8, 256), lambda b, i: (0, 0)), tab, tab]
    res_specs = _residue_specs(tm, 256, nt)
    in_specs += [res_specs[ci] for ci in range(3) for _ in range(3)]
    in_specs += [row(a.shape[1]) for a in flat[9:]]
    return pl.pallas_call(
        body, name=name, grid=(bl, nt), in_specs=in_specs,
        out_specs=[row(IN_WIDTH), pl.BlockSpec((8, 256), lambda b, i: (0, 0))],
        out_shape=[_sds((t, IN_WIDTH), BF16), _sds((8, 256), F32)],
        scratch_shapes=[pltpu.VMEM((6, tm, LANES), F32)],
        compiler_params=_cparams(ARB, ARB),
    )(proj, gains, cos, sins, *flat)


BAND_ROWS_PER_STEP = 512


def _residues_per_step(dil, seq_len):
    return min(dil, max(1, BAND_ROWS_PER_STEP // seq_len))


def _band_spec(seq_len, spec, rb):
    width, idx = spec
    return pl.BlockSpec((None, rb, seq_len, width), lambda b, r: (b, r, 0, idx))


def _fill_padded(dst, src_ref, rad, seq_len):
    z = jnp.zeros((rad, dst.shape[1]), dst.dtype)
    dst[0:rad, :] = z
    dst[rad + seq_len:rad + seq_len + rad, :] = z
    dst[rad:rad + seq_len, :] = src_ref[...]


def _band_fwd(src, qs, ks, vs, bias, sink, *, rad, nh, nkv, name):
    bl, dil, sl, _ = src.shape
    blk = bias.shape[1]
    kw = blk + 2 * rad
    nb = sl // blk
    rep = nh // nkv
    has_sink = sink is not None
    rb = _residues_per_step(dil, sl)

    def body(*refs):
        q_all, k_all, v_all, b_ref = refs[:4]
        s_ref = refs[4] if has_sink else None
        o_all, l_all, kp, vp = refs[4 + has_sink:]
        for ri in range(rb):
            one_sequence(q_all.at[ri], k_all.at[ri], v_all.at[ri], b_ref, s_ref, o_all.at[ri], l_all.at[ri], kp, vp)

    def one_sequence(q_ref, k_ref, v_ref, b_ref, s_ref, o_ref, l_ref, kp, vp):
        _fill_padded(kp, k_ref, rad, sl)
        _fill_padded(vp, v_ref, rad, sl)

        def blk_body(i, carry):
            r0 = pl.multiple_of(i * blk, blk)
            qb = q_ref[pl.ds(r0, blk), :]
            kwin = kp[pl.ds(r0, kw), :]
            vwin = vp[pl.ds(r0, kw), :]
            col = r0 - rad + lax.broadcasted_iota(jnp.int32, (blk, kw), 1)
            neg = jnp.where((col >= 0) & (col < sl), 0.0, NEG_INF).astype(F32)
            for h in range(nh):
                g = h // rep
                hs = slice(h * HEAD_DIM, (h + 1) * HEAD_DIM)
                gs = slice(g * HEAD_DIM, (g + 1) * HEAD_DIM)
                s = lax.dot_general(qb[:, hs], kwin[:, gs], NT, preferred_element_type=F32)
                s = s + b_ref[h] + neg
                m = jnp.max(s, axis=1, keepdims=True)
                if has_sink:
                    sk = s_ref[h][0:1, 0:1]
                    m = jnp.maximum(m, sk)
                p = jnp.exp(s - m)
                den = jnp.sum(p, axis=1, keepdims=True)
                if has_sink:
                    den = den + jnp.exp(sk - m)
                o = jnp.dot(p.astype(BF16), vwin[:, gs], preferred_element_type=F32) / den
                o_ref[pl.ds(r0, blk), hs] = o
                l_ref[pl.ds(r0, blk), hs] = jnp.broadcast_to(m + jnp.log(den), (blk, HEAD_DIM))
            return carry

        lax.fori_loop(0, nb, blk_body, 0)

    in_specs = [_band_spec(sl, qs, rb), _band_spec(sl, ks, rb), _band_spec(sl, vs, rb),
                pl.BlockSpec((nh, blk, kw), lambda b, r: (0, 0, 0))]
    args = [src] * 3 + [bias]
    if has_sink:
        in_specs.append(pl.BlockSpec((nh, 8, 128), lambda b, r: (0, 0, 0)))
        args.append(sink)
    return pl.pallas_call(
        body, name=name, grid=(bl, dil // rb), in_specs=in_specs,
        out_specs=[_band_spec(sl, (256, 0), rb)] * 2,
        out_shape=[_sds((bl, dil, sl, 256), F32)] * 2,
        scratch_shapes=[pltpu.VMEM((sl + 2 * rad, ks[0]), BF16), pltpu.VMEM((sl + 2 * rad, vs[0]), BF16)],
        compiler_params=_cparams(PAR, PAR),
    )(*args)


def _band_bwd(src, qs, ks, vs, bias, sink, dy, dcol, lse, delta, *, rad, nh, nkv, name):
    bl, dil, sl, _ = src.shape
    blk = bias.shape[1]
    kw = blk + 2 * rad
    nb = sl // blk
    rep = nh // nkv
    has_sink = sink is not None
    rb = _residues_per_step(dil, sl)
    wk, wv = ks[0], vs[0]

    def body(*refs):
        q_all, k_all, v_all, b_ref = refs[:4]
        s_ref = refs[4] if has_sink else None
        do_all, l_all, dl_all = refs[4 + has_sink:7 + has_sink]
        outs = refs[7 + has_sink:]
        dsk_ref = None
        if has_sink:
            dq_all, dk_all, dv_all, db_ref, dsk_ref, kp, vp, dka, dva = outs
        else:
            dq_all, dk_all, dv_all, db_ref, kp, vp, dka, dva = outs

        @pl.when((pl.program_id(0) == 0) & (pl.program_id(1) == 0))
        def _():
            db_ref[...] = jnp.zeros_like(db_ref)
            if has_sink:
                dsk_ref[...] = jnp.zeros_like(dsk_ref)

        for ri in range(rb):
            one_sequence(q_all.at[ri], k_all.at[ri], v_all.at[ri], b_ref, s_ref, do_all.at[ri], l_all.at[ri],
                         dl_all.at[ri], dq_all.at[ri], dk_all.at[ri], dv_all.at[ri], db_ref, dsk_ref, kp, vp, dka, dva)

    def one_sequence(q_ref, k_ref, v_ref, b_ref, s_ref, do_ref, l_ref, dl_ref, dq_ref, dk_ref, dv_ref, db_ref, dsk_ref,
                     kp, vp, dka, dva):
        _fill_padded(kp, k_ref, rad, sl)
        _fill_padded(vp, v_ref, rad, sl)
        dka[...] = jnp.zeros_like(dka)
        dva[...] = jnp.zeros_like(dva)

        def blk_body(i, carry):
            r0 = pl.multiple_of(i * blk, blk)
            qb = q_ref[pl.ds(r0, blk), :]
            kwin = kp[pl.ds(r0, kw), :]
            vwin = vp[pl.ds(r0, kw), :]
            dob = do_ref[pl.ds(r0, blk), :].astype(BF16)
            lb = l_ref[pl.ds(r0, blk), :]
            dlb = dl_ref[pl.ds(r0, blk), :]
            col = r0 - rad + lax.broadcasted_iota(jnp.int32, (blk, kw), 1)
            neg = jnp.where((col >= 0) & (col < sl), 0.0, NEG_INF).astype(F32)
            for h in range(nh):
                g = h // rep
                hs = slice(h * HEAD_DIM, (h + 1) * HEAD_DIM)
                gs = slice(g * HEAD_DIM, (g + 1) * HEAD_DIM)
                qh, kh, vh, doh = qb[:, hs], kwin[:, gs], vwin[:, gs], dob[:, hs]
                lh = lb[:, h * HEAD_DIM:h * HEAD_DIM + 1]
                dlh = dlb[:, h * HEAD_DIM:h * HEAD_DIM + 1]
                s = lax.dot_general(qh, kh, NT, preferred_element_type=F32) + b_ref[h] + neg
                p = jnp.exp(s - lh)
                dp = lax.dot_general(doh, vh, NT, preferred_element_type=F32)
                ds = p * (dp - dlh)
                dsb = ds.astype(BF16)
                dq_ref[pl.ds(r0, blk), hs] = jnp.dot(dsb, kh, preferred_element_type=F32)
                dka[pl.ds(r0, kw), gs] += lax.dot_general(dsb, qh, TN, preferred_element_type=F32)
                dva[pl.ds(r0, kw), gs] += lax.dot_general(p.astype(BF16), doh, TN, preferred_element_type=F32)
                db_ref[h] += ds
                if has_sink:
                    ps = jnp.exp(s_ref[h][0:1, 0:1] - lh)
                    dsk_ref[h] += jnp.broadcast_to(-jnp.sum(ps * dlh, axis=0, keepdims=True), (8, 128))
            return carry

        lax.fori_loop(0, nb, blk_body, 0)
        dk_ref[...] = dka[rad:rad + sl, :]
        dv_ref[...] = dva[rad:rad + sl, :]

    const3 = lambda b, r: (0, 0, 0)
    in_specs = [_band_spec(sl, qs, rb), _band_spec(sl, ks, rb), _band_spec(sl, vs, rb),
                pl.BlockSpec((nh, blk, kw), const3)]
    args = [src] * 3 + [bias]
    if has_sink:
        in_specs.append(pl.BlockSpec((nh, 8, 128), const3))
        args.append(sink)
    row = _band_spec(sl, (256, 0), rb)
    in_specs += [_band_spec(sl, (256, dcol), rb), row, row]
    args += [dy, lse, delta]
    out_specs = [row, _band_spec(sl, (wk, 0), rb), _band_spec(sl, (wv, 0), rb), pl.BlockSpec((nh, blk, kw), const3)]
    out_shape = [_sds((bl, dil, sl, 256), F32), _sds((bl, dil, sl, wk), F32), _sds((bl, dil, sl, wv), F32),
                 _sds((nh, blk, kw), F32)]
    if has_sink:
        out_specs.append(pl.BlockSpec((nh, 8, 128), const3))
        out_shape.append(_sds((nh, 8, 128), F32))
    return pl.pallas_call(
        body, name=name, grid=(bl, dil // rb), in_specs=in_specs, out_specs=out_specs, out_shape=out_shape,
        scratch_shapes=[pltpu.VMEM((sl + 2 * rad, wk), BF16), pltpu.VMEM((sl + 2 * rad, wv), BF16),
                        pltpu.VMEM((sl + 2 * rad, wk), F32), pltpu.VMEM((sl + 2 * rad, wv), F32)],
        compiler_params=_cparams(ARB, ARB),
    )(*args)


def _combine_a(os_, ls_, *, tm, name):
    bl = os_[1].shape[0]
    t = bl * SEQ
    nt = SEQ // tm

    def body(o1, o4, o16, l1, l4, l16, y_ref, lt_ref, scr):
        for k, (d, ref) in enumerate(((4, o4), (16, o16), (4, l4), (16, l16))):
            for r in range(d):
                _write_residue(scr, 2 * k, r, d, ref[r])
        o2, o3, b, c = (_gather_cols(scr, 2 * k, 2) for k in range(4))
        a = l1[...]
        m = jnp.maximum(jnp.maximum(a, b), c)
        ea, eb, ec = jnp.exp(a - m), jnp.exp(b - m), jnp.exp(c - m)
        den = ea + eb + ec
        y_ref[...] = (ea / den) * o1[...] + (eb / den) * o2 + (ec / den) * o3
        lt_ref[...] = m + jnp.log(den)

    specs = _residue_specs(tm, 256, nt)
    return pl.pallas_call(
        body, name=name, grid=(bl, nt), in_specs=specs * 2, out_specs=[specs[0]] * 2,
        out_shape=[_sds((t, 256), F32)] * 2, scratch_shapes=[pltpu.VMEM((8, tm, LANES), F32)],
        compiler_params=_cparams(PAR, PAR),
    )(*os_, *ls_)


def _deltas(dycat, ya, yb, yd, lse_a, *, tm, name):
    t = ya.shape[0]
    bl = t // SEQ
    nt = SEQ // tm

    def body(dy_ref, ya_ref, yb_ref, yd_ref, la_ref, dy4, dy16, l4, l16, da1, da4, da16, db_ref, dd_ref, scr):
        e = _group_sum_matrix(256, True)
        dya = dy_ref[:, 0:256]
        dla = _seg_sum(dya * ya_ref[...], e)
        da1[...] = dla
        db_ref[...] = _seg_sum(dy_ref[:, 256:512] * yb_ref[...], e)
        dd_ref[...] = _seg_sum(dy_ref[:, 768:1024] * yd_ref[...], e)
        for k, (val, r4, r16) in enumerate(((dya, dy4, dy16), (la_ref[...], l4, l16), (dla, da4, da16))):
            _scatter_cols(scr, 2 * k, val)
            for d, ref in ((4, r4), (16, r16)):
                for r in range(d):
                    ref[r] = _read_residue(scr, 2 * k, 2, r, d)

    specs = _residue_specs(tm, 256, nt)
    nat = specs[0]
    shapes = _residue_shapes(bl, 256, F32)
    outs = pl.pallas_call(
        body, name=name, grid=(bl, nt),
        in_specs=[pl.BlockSpec((tm, 1024), lambda b, i: (b * nt + i, 0)), nat, nat, nat, nat],
        out_specs=specs[1:] + specs[1:] + specs + [nat, nat],
        out_shape=shapes[1:] + shapes[1:] + shapes + [shapes[0], shapes[0]],
        scratch_shapes=[pltpu.VMEM((6, tm, LANES), F32)],
        compiler_params=_cparams(PAR, PAR),
    )(dycat, ya, yb, yd, lse_a)
    return outs[0:2], outs[2:4], outs[4:7], outs[7], outs[8]


def _dense_fwd(qd, *, tq, name):
    t = qd.shape[0]
    bl = t // SEQ
    nq = SEQ // tq

    def body(q_ref, k_ref, v_ref, o_ref, l_ref):
        q = q_ref[...]
        for g in range(2):
            h0, h1 = 2 * g, 2 * g + 1
            q2 = jnp.concatenate([q[:, h0 * 64:(h0 + 1) * 64], q[:, h1 * 64:(h1 + 1) * 64]], axis=0)
            kg = k_ref[:, g * 64:(g + 1) * 64]
            vg = v_ref[:, g * 64:(g + 1) * 64]
            s = lax.dot_general(q2, kg, NT, preferred_element_type=F32)
            m = jnp.max(s, axis=1, keepdims=True)
            p = jnp.exp(s - m)
            den = jnp.sum(p, axis=1, keepdims=True)
            o2 = jnp.dot(p.astype(BF16), vg, preferred_element_type=F32) / den
            l2 = jnp.broadcast_to(m + jnp.log(den), (2 * tq, 64))
            o_ref[:, h0 * 64:(h0 + 1) * 64] = o2[:tq]
            o_ref[:, h1 * 64:(h1 + 1) * 64] = o2[tq:]
            l_ref[:, h0 * 64:(h0 + 1) * 64] = l2[:tq]
            l_ref[:, h1 * 64:(h1 + 1) * 64] = l2[tq:]

    q3 = qd.reshape(bl, SEQ, 512)
    o, lse = pl.pallas_call(
        body, name=name, grid=(bl, nq),
        in_specs=[pl.BlockSpec((None, tq, 256), lambda b, i: (b, i, 0)),
                  pl.BlockSpec((None, SEQ, 128), lambda b, i: (b, 0, 2)),
                  pl.BlockSpec((None, SEQ, 128), lambda b, i: (b, 0, 3))],
        out_specs=[pl.BlockSpec((None, tq, 256), lambda b, i: (b, i, 0))] * 2,
        out_shape=[_sds((bl, SEQ, 256), F32)] * 2,
        compiler_params=_cparams(PAR, PAR),
    )(q3, q3, q3)
    return o.reshape(t, 256), lse.reshape(t, 256)


def _dense_bwd(qd, dycat, lse, delta, *, tq, name):
    t = qd.shape[0]
    bl = t // SEQ
    nq = SEQ // tq

    def body(q_ref, k_ref, v_ref, do_ref, l_ref, dl_ref, dq_ref, dk_ref, dv_ref, dkt, dvt):
        @pl.when(pl.program_id(1) == 0)
        def _():
            dkt[...] = jnp.zeros_like(dkt)
            dvt[...] = jnp.zeros_like(dvt)

        q = q_ref[...]
        do = do_ref[...].astype(BF16)
        lv = l_ref[...]
        dlv = dl_ref[...]
        for g in range(2):
            h0, h1 = 2 * g, 2 * g + 1
            q2 = jnp.concatenate([q[:, h0 * 64:(h0 + 1) * 64], q[:, h1 * 64:(h1 + 1) * 64]], axis=0)
            do2 = jnp.concatenate([do[:, h0 * 64:(h0 + 1) * 64], do[:, h1 * 64:(h1 + 1) * 64]], axis=0)
            l2 = jnp.concatenate([lv[:, h0 * 64:h0 * 64 + 1], lv[:, h1 * 64:h1 * 64 + 1]], axis=0)
            dl2 = jnp.concatenate([dlv[:, h0 * 64:h0 * 64 + 1], dlv[:, h1 * 64:h1 * 64 + 1]], axis=0)
            kg = k_ref[:, g * 64:(g + 1) * 64]
            vg = v_ref[:, g * 64:(g + 1) * 64]
            s = lax.dot_general(q2, kg, NT, preferred_element_type=F32)
            p = jnp.exp(s - l2)
            dp = lax.dot_general(do2, vg, NT, preferred_element_type=F32)
            ds = (p * (dp - dl2)).astype(BF16)
            dq2 = jnp.dot(ds, kg, preferred_element_type=F32)
            dq_ref[:, h0 * 64:(h0 + 1) * 64] = dq2[:tq]
            dq_ref[:, h1 * 64:(h1 + 1) * 64] = dq2[tq:]
            dkt[g * 64:(g + 1) * 64, :] += lax.dot_general(q2, ds, TN, preferred_element_type=F32)
            dvt[g * 64:(g + 1) * 64, :] += lax.dot_general(do2, p.astype(BF16), TN, preferred_element_type=F32)

        @pl.when(pl.program_id(1) == nq - 1)
        def _():
            dk_ref[...] = dkt[...].T
            dv_ref[...] = dvt[...].T

    q3 = qd.reshape(bl, SEQ, 512)
    tile = pl.BlockSpec((None, tq, 256), lambda b, i: (b, i, 0))
    full = pl.BlockSpec((None, SEQ, 128), lambda b, i: (b, 0, 0))
    dq, dk, dv = pl.pallas_call(
        body, name=name, grid=(bl, nq),
        in_specs=[tile, pl.BlockSpec((None, SEQ, 128), lambda b, i: (b, 0, 2)),
                  pl.BlockSpec((None, SEQ, 128), lambda b, i: (b, 0, 3)),
                  pl.BlockSpec((None, tq, 256), lambda b, i: (b, i, 3)), tile, tile],
        out_specs=[tile, full, full],
        out_shape=[_sds((bl, SEQ, 256), F32), _sds((bl, SEQ, 128), F32), _sds((bl, SEQ, 128), F32)],
        scratch_shapes=[pltpu.VMEM((128, SEQ), F32), pltpu.VMEM((128, SEQ), F32)],
        compiler_params=_cparams(PAR, ARB),
    )(q3, q3, q3, dycat.reshape(bl, SEQ, 1024), lse.reshape(bl, SEQ, 256), delta.reshape(bl, SEQ, 256))
    return dq.reshape(t, 256), dk.reshape(t, 128), dv.reshape(t, 128)


def _c_norm(cv, gam, bet):
    vg = _gelu(cv)
    mu = jnp.mean(vg, axis=-1, keepdims=True)
    xc = vg - mu
    r = lax.rsqrt(jnp.mean(xc * xc, axis=-1, keepdims=True) + EPS)
    xhat = xc * r
    return xhat * gam + bet, xhat, r


def _c_fwd(proj, gam, bet, ws, bst, *, tm, name):
    t = proj.shape[0]
    nch = tm // C_CHUNK

    def body(u_ref, v_ref, g_ref, b_ref, ws_ref, bs_ref, y_ref):
        vn, _, _ = _c_norm(v_ref[...], g_ref[...], b_ref[...])
        vnb = vn.astype(BF16)
        for c in range(nch):
            rows = slice(c * C_CHUNK, (c + 1) * C_CHUNK)
            for g in range(C_GROUPS):
                gs = slice(g * 64, (g + 1) * 64)
                mixed = jnp.dot(ws_ref[g], vnb[rows, gs], preferred_element_type=F32) + bs_ref[:, gs]
                y_ref[rows, gs] = _gelu(u_ref[rows, gs]) * mixed

    vec = pl.BlockSpec((1, 256), lambda i: (0, 0))
    return pl.pallas_call(
        body, name=name, grid=(t // tm,),
        in_specs=[pl.BlockSpec((tm, 256), lambda i: (i, 5)), pl.BlockSpec((tm, 256), lambda i: (i, 6)), vec, vec,
                  pl.BlockSpec((C_GROUPS, C_CHUNK, C_CHUNK), lambda i: (0, 0, 0)),
                  pl.BlockSpec((C_CHUNK, 256), lambda i: (0, 0))],
        out_specs=pl.BlockSpec((tm, 256), lambda i: (i, 0)), out_shape=_sds((t, 256), F32),
        compiler_params=_cparams(PAR),
    )(proj, proj, gam, bet, ws, bst)


def _c_bwd(proj, dycat, gam, bet, ws, wst, bst, *, tm, name):
    t = proj.shape[0]
    nch = tm // C_CHUNK
    nstep = t // tm

    def body(u_ref, v_ref, dy_ref, g_ref, b_ref, ws_ref, wst_ref, bs_ref,
             du_ref, dv_ref, dws_ref, dbs_ref, dg_ref, db_ref, dvn_s):
        step = pl.program_id(0)

        @pl.when(step == 0)
        def _():
            dws_ref[...] = jnp.zeros_like(dws_ref)
            dbs_ref[...] = jnp.zeros_like(dbs_ref)
            dg_ref[...] = jnp.zeros_like(dg_ref)
            db_ref[...] = jnp.zeros_like(db_ref)

        cv = v_ref[...]
        gam_v = g_ref[...]
        vn, xhat, r = _c_norm(cv, gam_v, b_ref[...])
        vnb = vn.astype(BF16)
        for c in range(nch):
            rows = slice(c * C_CHUNK, (c + 1) * C_CHUNK)
            for g in range(C_GROUPS):
                gs = slice(g * 64, (g + 1) * 64)
                cu = u_ref[rows, gs]
                dy = dy_ref[rows, gs]
                mixed = jnp.dot(ws_ref[g], vnb[rows, gs], preferred_element_type=F32) + bs_ref[:, gs]
                du_ref[rows, gs] = dy * mixed * _gelu_grad(cu)
                dmix = dy * _gelu(cu)
                dbs_ref[:, gs] += dmix
                dmb = dmix.astype(BF16)
                dws_ref[g] += lax.dot_general(dmb, vnb[rows, gs], NT, preferred_element_type=F32)
                dvn_s[rows, gs] = jnp.dot(wst_ref[g], dmb, preferred_element_type=F32)
        dvn = dvn_s[...]
        dg_ref[...] += jnp.sum(dvn * xhat, axis=0, keepdims=True)
        db_ref[...] += jnp.sum(dvn, axis=0, keepdims=True)
        dxh = dvn * gam_v
        dvg = r * (dxh - jnp.mean(dxh, axis=-1, keepdims=True) - xhat * jnp.mean(dxh * xhat, axis=-1, keepdims=True))
        dv_ref[...] = dvg * _gelu_grad(cv)

        @pl.when(step == nstep - 1)
        def _():
            dbs_ref[...] = _seg_sum(dbs_ref[...], _group_sum_matrix(256, True))

    vec = pl.BlockSpec((1, 256), lambda i: (0, 0))
    mat = pl.BlockSpec((C_GROUPS, C_CHUNK, C_CHUNK), lambda i: (0, 0, 0))
    bsp = pl.BlockSpec((C_CHUNK, 256), lambda i: (0, 0))
    tile = pl.BlockSpec((tm, 256), lambda i: (i, 0))
    return pl.pallas_call(
        body, name=name, grid=(nstep,),
        in_specs=[pl.BlockSpec((tm, 256), lambda i: (i, 5)), pl.BlockSpec((tm, 256), lambda i: (i, 6)),
                  pl.BlockSpec((tm, 256), lambda i: (i, 2)), vec, vec, mat, mat, bsp],
        out_specs=[tile, tile, mat, bsp, vec, vec],
        out_shape=[_sds((t, 256), F32), _sds((t, 256), F32), _sds((C_GROUPS, C_CHUNK, C_CHUNK), F32),
                   _sds((C_CHUNK, 256), F32), _sds((1, 256), F32), _sds((1, 256), F32)],
        scratch_shapes=[pltpu.VMEM((tm, 256), F32)],
        compiler_params=_cparams(ARB),
    )(proj, proj, dycat, gam, bet, ws, wst, bst)


FF_TC = 128
FF_NB = D_FF // FF_TC
FF_CH = 64
FF_HALO = 16


def _taps(ref, r0, win, where):
    z = jnp.zeros((FF_HALO, win.shape[1]), F32)
    if where == "first":
        win[0:FF_HALO, :] = z
        win[FF_HALO:, :] = ref[0:FF_CH + FF_HALO, :].astype(F32)
    elif where == "last":
        win[0:FF_CH + FF_HALO, :] = ref[SEQ - FF_CH - FF_HALO:SEQ, :].astype(F32)
        win[FF_CH + FF_HALO:, :] = z
    else:
        win[...] = ref[pl.ds(pl.multiple_of(r0 - FF_HALO, FF_HALO), FF_CH + 2 * FF_HALO), :].astype(F32)
    return tuple(win[FF_HALO + o:FF_HALO + o + FF_CH, :] for o in (-1, 0, 1))


def _chunk_loop(step):
    step(0, lambda ref, win: _taps(ref, 0, win, "first"))

    def mid(i, carry):
        r0 = pl.multiple_of(i * FF_CH, FF_CH)
        step(r0, lambda ref, win: _taps(ref, r0, win, "mid"))
        return carry

    lax.fori_loop(1, SEQ // FF_CH - 1, mid, 0)
    step(SEQ - FF_CH, lambda ref, win: _taps(ref, SEQ - FF_CH, win, "last"))


def _conv3(taps, w_ref, b_ref):
    dn, md, up = taps
    return w_ref[0:1, :] * dn + w_ref[1:2, :] * md + w_ref[2:3, :] * up + b_ref[...]


def _ff_specs(order):
    def at(fn):
        return (lambda b, j: fn(b, j)) if order == "bj" else (lambda j, b: fn(b, j))
    hs = [pl.BlockSpec((None, SEQ, FF_TC), at(lambda b, j, o=o: (b, 0, j + o))) for o in (0, FF_NB)]
    ws = [pl.BlockSpec((3, FF_TC), at(lambda b, j, o=o: (0, j + o))) for o in (0, FF_NB)]
    bs = [pl.BlockSpec((1, FF_TC), at(lambda b, j, o=o: (0, j + o))) for o in (0, FF_NB)]
    return hs, ws, bs


def _conv_gate_fwd(h, cw, cb, *, name):
    t = h.shape[0]
    bl = t // SEQ

    def body(hg_ref, hu_ref, wg_ref, wu_ref, bg_ref, bu_ref, a_ref, win):
        def step(r0, taps):
            cg = _conv3(taps(hg_ref, win.at[0]), wg_ref, bg_ref)
            cu = _conv3(taps(hu_ref, win.at[1]), wu_ref, bu_ref)
            a_ref[pl.ds(r0, FF_CH), :] = (cg * _sigmoid(cg) * cu).astype(BF16)

        _chunk_loop(step)

    hs, ws, bs = _ff_specs("bj")
    h3 = h.reshape(bl, SEQ, 2 * D_FF)
    act = pl.pallas_call(
        body, name=name, grid=(bl, FF_NB), in_specs=hs + ws + bs,
        out_specs=pl.BlockSpec((None, SEQ, FF_TC), lambda b, j: (b, 0, j)),
        out_shape=_sds((bl, SEQ, D_FF), BF16),
        scratch_shapes=[pltpu.VMEM((2, FF_CH + 2 * FF_HALO, FF_TC), F32)],
        compiler_params=_cparams(PAR, PAR),
    )(h3, h3, cw, cw, cb, cb)
    return act.reshape(t, D_FF)


def _conv_gate_bwd(h, dact, cw, cb, *, name):
    t = h.shape[0]
    bl = t // SEQ

    def body(hg_ref, hu_ref, wg_ref, wu_ref, bg_ref, bu_ref, da_ref,
             dhg_ref, dhu_ref, dwg_ref, dwu_ref, dbg_ref, dbu_ref, dg_s, du_s, win, sums):
        @pl.when(pl.program_id(1) == 0)
        def _():
            for ref in (dwg_ref, dwu_ref, dbg_ref, dbu_ref):
                ref[...] = jnp.zeros_like(ref)

        sums[...] = jnp.zeros_like(sums)
        red = lambda x: jnp.sum(x.reshape(FF_CH // 8, 8, x.shape[1]), axis=0)

        def pass1(r0, taps):
            tg, tu = taps(hg_ref, win.at[0]), taps(hu_ref, win.at[1])
            cg = _conv3(tg, wg_ref, bg_ref)
            cu = _conv3(tu, wu_ref, bu_ref)
            da = da_ref[pl.ds(r0, FF_CH), :].astype(F32)
            sg = _sigmoid(cg)
            dcg = da * cu * (sg * (1.0 + cg * (1.0 - sg)))
            dcu = da * (cg * sg)
            dg_s[pl.ds(r0, FF_CH), :] = dcg
            du_s[pl.ds(r0, FF_CH), :] = dcu
            for half, (d, tp) in enumerate(((dcg, tg), (dcu, tu))):
                for k in range(3):
                    sums[4 * half + k] += red(d * tp[k])
                sums[4 * half + 3] += red(d)

        _chunk_loop(pass1)
        for half, (dw_ref, db_ref) in enumerate(((dwg_ref, dbg_ref), (dwu_ref, dbu_ref))):
            for k in range(3):
                dw_ref[k:k + 1, :] += jnp.sum(sums[4 * half + k], axis=0, keepdims=True)
            db_ref[...] += jnp.sum(sums[4 * half + 3], axis=0, keepdims=True)

        def pass2(r0, taps):
            for k, (s, w_ref, o_ref) in enumerate(((dg_s, wg_ref, dhg_ref), (du_s, wu_ref, dhu_ref))):
                dn, md, up = taps(s, win.at[k])
                o_ref[pl.ds(r0, FF_CH), :] = (w_ref[0:1, :] * up + w_ref[1:2, :] * md + w_ref[2:3, :] * dn).astype(BF16)

        _chunk_loop(pass2)

    hs, ws, bs = _ff_specs("jb")
    half = pl.BlockSpec((None, SEQ, FF_TC), lambda j, b: (b, 0, j))
    wsp = pl.BlockSpec((3, FF_TC), lambda j, b: (0, j))
    bsp = pl.BlockSpec((1, FF_TC), lambda j, b: (0, j))
    h3 = h.reshape(bl, SEQ, 2 * D_FF)
    dhg, dhu, dwg, dwu, dbg, dbu = pl.pallas_call(
        body, name=name, grid=(FF_NB, bl), in_specs=hs + ws + bs + [half],
        out_specs=[half, half, wsp, wsp, bsp, bsp],
        out_shape=[_sds((bl, SEQ, D_FF), BF16), _sds((bl, SEQ, D_FF), BF16), _sds((3, D_FF), F32), _sds((3, D_FF), F32),
                   _sds((1, D_FF), F32), _sds((1, D_FF), F32)],
        scratch_shapes=[pltpu.VMEM((SEQ, FF_TC), F32), pltpu.VMEM((SEQ, FF_TC), F32),
                        pltpu.VMEM((2, FF_CH + 2 * FF_HALO, FF_TC), F32), pltpu.VMEM((8, 8, FF_TC), F32)],
        compiler_params=_cparams(PAR, ARB),
    )(h3, h3, cw, cw, cb, cb, dact.reshape(bl, SEQ, D_FF))
    return (dhg.reshape(t, D_FF), dhu.reshape(t, D_FF), jnp.concatenate([dwg, dwu], axis=1),
            jnp.concatenate([dbg, dbu], axis=1))


def _ple_fwd(x2, gain, wg, pe, pe_blk, wp, *, tm, name):
    t, k = x2.shape

    def body(x_ref, g_ref, wg_ref, pe_ref, wp_ref, hn_ref, x3_ref, gt_ref, pp_ref):
        x = x_ref[...]
        r = lax.rsqrt(jnp.mean(x * x, axis=-1, keepdims=True) + EPS)
        hn = (x * r * g_ref[...]).astype(BF16)
        hn_ref[...] = hn
        gate = _sigmoid(jnp.dot(hn, wg_ref[...], preferred_element_type=F32))
        pp = jnp.dot(pe_ref[...].astype(BF16), wp_ref[...], preferred_element_type=F32)
        gt_ref[...] = gate.astype(BF16)
        pp_ref[...] = pp.astype(BF16)
        x3_ref[...] = x + pp * gate

    row = pl.BlockSpec((tm, k), lambda i: (i, 0))
    return pl.pallas_call(
        body, name=name, grid=(t // tm,),
        in_specs=[row, pl.BlockSpec((1, k), lambda i: (0, 0)), pl.BlockSpec((k, k), lambda i: (0, 0)),
                  pl.BlockSpec((tm, PLE_DIM), lambda i: (pe_blk + i, 0)), pl.BlockSpec((PLE_DIM, k), lambda i: (0, 0))],
        out_specs=[row, row, row, row],
        out_shape=[_sds((t, k), BF16), _sds((t, k), F32), _sds((t, k), BF16), _sds((t, k), BF16)],
        compiler_params=_cparams(PAR),
    )(x2, gain, wg, pe, wp)


def _ple_bwd_ew(dx3, gate, pp, *, tm, name):
    t, n = dx3.shape

    def body(d_ref, g_ref, p_ref, dz_ref, dpp_ref):
        d, g = d_ref[...], g_ref[...]
        dz_ref[...] = (d * p_ref[...] * g * (1.0 - g)).astype(BF16)
        dpp_ref[...] = (d * g).astype(BF16)

    spec = pl.BlockSpec((tm, n), lambda i: (i, 0))
    return pl.pallas_call(
        body, name=name, grid=(t // tm,), in_specs=[spec] * 3, out_specs=[spec] * 2,
        out_shape=[_sds((t, n), BF16)] * 2, compiler_params=_cparams(PAR),
    )(dx3, gate, pp)


def _loss_head(y, tgt, *, tm, name):
    t, d = y.shape

    def body(y_ref, t_ref, l_ref, dy_ref):
        @pl.when(pl.program_id(0) == 0)
        def _():
            l_ref[...] = jnp.zeros_like(l_ref)

        e = y_ref[...] - t_ref[...]
        dy_ref[...] = e * (1.0 / d)
        s = jnp.sum(jnp.sum(e * e, axis=1, keepdims=True), axis=0, keepdims=True)
        l_ref[...] += jnp.broadcast_to(s * (0.5 / d), (8, 128))

    spec = pl.BlockSpec((tm, d), lambda i: (i, 0))
    return pl.pallas_call(
        body, name=name, grid=(t // tm,), in_specs=[spec, spec],
        out_specs=[pl.BlockSpec((8, 128), lambda i: (0, 0)), spec],
        out_shape=[_sds((8, 128), F32), _sds((t, d), F32)], compiler_params=_cparams(ARB),
    )(y, tgt)


BIAS_PC = 8192


def _onehot(bucket_row):
    rows = lax.broadcasted_iota(jnp.int32, (REL_BUCKETS, bucket_row.shape[1]), 0)
    return (rows == bucket_row).astype(BF16)


def _dot3(x, onehot, dims):
    acc = None
    for _ in range(3):
        term = x.astype(BF16)
        part = lax.dot_general(term, onehot, dims, preferred_element_type=F32)
        acc = part if acc is None else acc + part
        x = x - term.astype(F32)
    return acc


def _bias_lookup(table_t, bucket, *, name):
    h = table_t.shape[0]
    p = bucket.shape[1]

    def body(t_ref, b_ref, o_ref):
        bk = b_ref[...]
        val = _dot3(t_ref[...], _onehot(bk), (((1,), (0,)), ((), ())))
        o_ref[...] = jnp.where(bk >= 0, val, NEG_INF)

    return pl.pallas_call(
        body, name=name, grid=(p // BIAS_PC,),
        in_specs=[pl.BlockSpec((h, REL_BUCKETS), lambda i: (0, 0)), pl.BlockSpec((1, BIAS_PC), lambda i: (0, i))],
        out_specs=pl.BlockSpec((h, BIAS_PC), lambda i: (0, i)), out_shape=_sds((h, p), F32),
        compiler_params=_cparams(PAR),
    )(table_t, bucket)


def _bucket_reduce(dbiases, bucket, *, name):
    h, p = dbiases[0].shape
    nl = len(dbiases)

    def body(*refs):
        b_ref, o_ref = refs[nl], refs[nl + 1]

        @pl.when(pl.program_id(0) == 0)
        def _():
            o_ref[...] = jnp.zeros_like(o_ref)

        d = refs[0][...]
        for d_ref in refs[1:nl]:
            d = d + d_ref[...]
        o_ref[...] += _dot3(d, _onehot(b_ref[...]), NT)

    return pl.pallas_call(
        body, name=name, grid=(p // BIAS_PC,),
        in_specs=[pl.BlockSpec((h, BIAS_PC), lambda i: (0, i))] * nl + [pl.BlockSpec((1, BIAS_PC), lambda i: (0, i))],
        out_specs=pl.BlockSpec((h, REL_BUCKETS), lambda i: (0, 0)), out_shape=_sds((h, REL_BUCKETS), F32),
        compiler_params=_cparams(ARB),
    )(*dbiases, bucket)


def _adamw_math(w, g, m, v):
    m = ADAM_B1 * m + (1.0 - ADAM_B1) * g
    v = ADAM_B2 * v + (1.0 - ADAM_B2) * (g * g)
    m_hat = m / (1.0 - ADAM_B1 ** ADAM_STEP)
    v_hat = v / (1.0 - ADAM_B2 ** ADAM_STEP)
    delta = -ADAM_LR * (m_hat / (jnp.sqrt(v_hat) + ADAM_EPS) + ADAM_WD * w)
    return delta, m, v


def _adamw_reduce(parts, w, m, v, *, tr, name):
    nl = len(parts)
    rows, c = w.shape
    r = rows // nl
    nt = r // tr

    def body(*refs):
        p_refs = refs[:nl]
        w_ref, m_ref, v_ref, g_ref, d_ref, nm_ref, nv_ref = refs[nl:]
        for li, p_ref in enumerate(p_refs):
            @pl.when(pl.program_id(0) == li)
            def _(p_ref=p_ref):
                g = p_ref[0].astype(F32)
                for k in range(1, N_DEV):
                    g = g + p_ref[k].astype(F32)
                d, nm, nv = _adamw_math(w_ref[...], g, m_ref[...], v_ref[...])
                g_ref[...] = g
                d_ref[...] = d
                nm_ref[...] = nm
                nv_ref[...] = nv

    def part_map(li):
        return lambda l, i: (0, jnp.where(l == li, i, jnp.where(l < li, 0, nt - 1)), 0)

    spec = pl.BlockSpec((tr, c), lambda l, i: (l * nt + i, 0))
    return pl.pallas_call(
        body, name=name, grid=(nl, nt),
        in_specs=[pl.BlockSpec((N_DEV, tr, c), part_map(li)) for li in range(nl)] + [spec, spec, spec],
        out_specs=[spec] * 4, out_shape=[_sds((rows, c), F32)] * 4, compiler_params=_cparams(ARB, ARB),
    )(*parts, w, m, v)


def _adamw_plain(g, w, m, v, *, name):
    def body(g_ref, w_ref, m_ref, v_ref, d_ref, nm_ref, nv_ref):
        d, nm, nv = _adamw_math(w_ref[...], g_ref[...], m_ref[...], v_ref[...])
        d_ref[...] = d
        nm_ref[...] = nm
        nv_ref[...] = nv

    return pl.pallas_call(body, name=name, out_shape=[_sds(w.shape, F32)] * 3)(g, w, m, v)


def _mesh_pos():
    return lax.axis_index("x"), lax.axis_index("y"), lax.axis_index("c")


def _allgather_body(x_refs, out_refs, send_sems, recv_sems, local_sems, slot):
    x, y, c = _mesh_pos()
    me, sibling = (x, y, c), (x, y, 1 - c)
    chips = [(1 - x, y), (x, 1 - y), (1 - x, 1 - y)]
    waits = []
    for a, (x_ref, out_ref) in enumerate(zip(x_refs, out_refs)):
        def copy(k, block, to, src=None, out_ref=out_ref, a=a):
            return pltpu.make_async_remote_copy(
                src_ref=slot(out_ref, block) if src is None else src, dst_ref=slot(out_ref, block),
                send_sem=send_sems.at[a, k], recv_sem=recv_sems.at[a, k], device_id=to, device_id_type=MESH)

        mine = pltpu.make_async_copy(x_ref, slot(out_ref, me), local_sems.at[a])
        mine.start()
        first = [copy(0, me, sibling, src=x_ref)]
        first += [copy(1 + j, me, (*chip, c), src=x_ref) for j, chip in enumerate(chips)]
        for cp in first:
            cp.start()
        waits.append((copy, mine, first))
    sends = []
    for copy, mine, first in waits:
        passed = [copy(4 + j, (*chip, c), sibling) for j, chip in enumerate(chips)]
        for j, chip in enumerate(chips):
            copy(1 + j, (*chip, c), me).wait_recv()
            passed[j].start()
        sends.append(passed)
    for (copy, mine, first), passed in zip(waits, sends):
        copy(0, sibling, me).wait_recv()
        for j, chip in enumerate(chips):
            copy(4 + j, (*chip, 1 - c), me).wait_recv()
        for cp in first + passed:
            cp.wait_send()
        mine.wait()


PEER_FLIPS = ((0, 0, 1), (1, 0, 0), (0, 1, 0), (1, 1, 0), (1, 0, 1), (0, 1, 1), (1, 1, 1))


def _peer_copies(x_refs, land_refs, send_sem, recv_sem, scatter):
    x, y, c = _mesh_pos()
    me = 4 * x + 2 * y + c
    copies = []
    for x_ref, land_ref in zip(x_refs, land_refs):
        for fx, fy, fc in PEER_FLIPS:
            px, py, pc = x ^ fx, y ^ fy, c ^ fc
            src = x_ref.at[4 * px + 2 * py + pc] if scatter else x_ref
            copies.append(pltpu.make_async_remote_copy(
                src_ref=src, dst_ref=land_ref.at[me], send_sem=send_sem, recv_sem=recv_sem,
                device_id=(px, py, pc), device_id_type=MESH))
    return copies


def _sc_exchange(xs, *, scatter, collective_id, name):
    na = len(xs)
    land_shapes = [x.shape if scatter else (N_DEV,) + x.shape for x in xs]

    def body(*refs):
        x_refs, land_refs = refs[:na], refs[na:2 * na]
        send_sem, recv_sem, local_sem = refs[2 * na:]
        x, y, c = _mesh_pos()
        me = 4 * x + 2 * y + c
        barrier = pltpu.get_barrier_semaphore()
        for fx, fy, fc in PEER_FLIPS:
            pl.semaphore_signal(barrier, inc=1, device_id=(x ^ fx, y ^ fy, c ^ fc), device_id_type=MESH)
        pl.semaphore_wait(barrier, len(PEER_FLIPS))
        for x_ref, land_ref in zip(x_refs, land_refs):
            own = pltpu.make_async_copy(x_ref.at[me] if scatter else x_ref, land_ref.at[me], local_sem)
            own.start()
            own.wait()
        copies = _peer_copies(x_refs, land_refs, send_sem, recv_sem, scatter)
        for cp in copies:
            cp.start()
        for cp in copies:
            cp.wait()

    return pl.kernel(
        body, name=name, out_type=[_sds(s, x.dtype) for s, x in zip(land_shapes, xs)],
        mesh=plsc.ScalarSubcoreMesh(axis_name="sequencer", num_cores=1),
        scratch_types=[pltpu.SemaphoreType.DMA, pltpu.SemaphoreType.DMA, pltpu.SemaphoreType.DMA],
        compiler_params=pltpu.CompilerParams(collective_id=collective_id),
    )(*xs)


def _sc_allgather(xs, *, collective_id, name):
    na = len(xs)

    def body(*refs):
        x_refs, out_refs = refs[:na], refs[na:2 * na]
        send_sems, recv_sems, local_sems = refs[2 * na:]
        x, y, c = _mesh_pos()
        barrier = pltpu.get_barrier_semaphore()
        for fx, fy, fc in PEER_FLIPS:
            pl.semaphore_signal(barrier, inc=1, device_id=(x ^ fx, y ^ fy, c ^ fc), device_id_type=MESH)
        pl.semaphore_wait(barrier, len(PEER_FLIPS))
        _allgather_body(x_refs, out_refs, send_sems, recv_sems, local_sems,
                        lambda ref, pos: ref.at[4 * pos[0] + 2 * pos[1] + pos[2]])

    return pl.kernel(
        body, name=name, out_type=[_sds((N_DEV,) + x.shape, x.dtype) for x in xs],
        mesh=plsc.ScalarSubcoreMesh(axis_name="sequencer", num_cores=1),
        scratch_types=[pltpu.SemaphoreType.DMA((na, 7)), pltpu.SemaphoreType.DMA((na, 7)),
                       pltpu.SemaphoreType.DMA((na,))],
        compiler_params=pltpu.CompilerParams(collective_id=collective_id),
    )(*xs)


def _allgather_vmem(x, *, name):
    r, c = x.shape

    def body(x_ref, out_ref, send_sems, recv_sems, local_sems):
        _allgather_body([x_ref], [out_ref], send_sems, recv_sems, local_sems,
                        lambda ref, pos: ref.at[pl.ds((4 * pos[0] + 2 * pos[1] + pos[2]) * r, r), :])

    vm = pl.BlockSpec(memory_space=pltpu.VMEM)
    return pl.pallas_call(
        body, name=name, in_specs=[vm], out_specs=vm, out_shape=_sds((N_DEV * r, c), x.dtype),
        scratch_shapes=[pltpu.SemaphoreType.DMA((1, 7)), pltpu.SemaphoreType.DMA((1, 7)),
                        pltpu.SemaphoreType.DMA((1,))],
    )(x)


def _sum_slots(gathered, *, name):
    _, r, c = gathered.shape

    def body(g_ref, o_ref):
        acc = g_ref[0]
        for k in range(1, N_DEV):
            acc = acc + g_ref[k]
        o_ref[...] = acc

    return pl.pallas_call(body, name=name, out_shape=_sds((r, c), gathered.dtype))(gathered)


def _t5_bucket(rel):
    nb = REL_BUCKETS // 2
    ret = jnp.where(rel > 0, nb, 0)
    n = jnp.abs(rel)
    max_exact = nb // 2
    nf = jnp.maximum(n, 1).astype(F32)
    large = max_exact + (jnp.log(nf / max_exact) / math.log(REL_MAX_DIST / max_exact)
                         * (nb - max_exact)).astype(jnp.int32)
    large = jnp.minimum(large, nb - 1)
    return ret + jnp.where(n < max_exact, n, large)


def _band_pattern(block, radius, dil):
    kw = block + 2 * radius
    rel = jnp.arange(kw)[None, :] - radius - jnp.arange(block)[:, None]
    return jnp.where(jnp.abs(rel) <= radius, _t5_bucket(rel * dil), -1).astype(jnp.int32).reshape(1, block * kw)


def _rope_tables():
    lane = np.arange(64)
    seg, j = lane // 32, lane % 32
    inv = ROPE_THETA ** (-jnp.arange(0, 32, 2, dtype=F32) / 32)
    tpos = jnp.arange(SEQ)
    pos = jnp.where(jnp.asarray(seg)[None, :] == 0, (tpos // GRID_W)[:, None], (tpos % GRID_W)[:, None])
    ang = pos.astype(F32) * inv[jnp.asarray(j % 16)][None, :]
    cos = jnp.cos(ang)
    sins = jnp.where(jnp.asarray(j)[None, :] < 16, -jnp.sin(ang), jnp.sin(ang))
    return jnp.tile(cos, (1, 4)), jnp.tile(sins, (1, 4))


A_Q, A_K, A_V = (256, 0), (256, 1), (256, 2)
B_Q, B_K, B_V = (256, 0), (128, 2), (128, 3)
A_HEADS = dict(rad=A_RADIUS, nh=4, nkv=4)
B_HEADS = dict(rad=SWA_RADIUS, nh=4, nkv=2)


def _local_step(x, pe, tgt, rel_bias, wts, matmul_weights, grads_ready):
    t = x.shape[0]
    bl = t // SEQ
    cos, sins = _rope_tables()
    blocks_a = [min(BAND_BLOCK, SEQ // d) for d in DILATIONS]
    pats_a = [_band_pattern(blk, A_RADIUS, d) for blk, d in zip(blocks_a, DILATIONS)]
    pat_b = _band_pattern(BAND_BLOCK, SWA_RADIUS, 1)
    table_t = rel_bias.T
    bias_a = [_bias_lookup(table_t[:4], pt, name=f"bias_a{ci}").reshape(4, blk, blk + 2 * A_RADIUS)
              for ci, (pt, blk) in enumerate(zip(pats_a, blocks_a))]
    bias_b = _bias_lookup(table_t[4:], pat_b, name="bias_b").reshape(4, BAND_BLOCK, BAND_BLOCK + 2 * SWA_RADIUS)
    nat4 = lambda a: a.reshape(bl, 1, SEQ, a.shape[-1])

    saved = []
    for li in range(DEPTH):
        w = dict(wts[li])
        w.update(matmul_weights(li, "in", x))
        hn0, proj = _norm_mm((x,), w["g_mix"], w["w_in"], None, tm=1024, tn=1152, name="mix_in_fwd")
        qa1, qa4, qa16, qb, qd = _qkprep_fwd(proj, w["qk_gains"], cos, sins, tm=512, name="qkprep_fwd")
        qa = (nat4(qa1), qa4, qa16)
        oa, la = [], []
        for ci in range(3):
            o, l = _band_fwd(qa[ci], A_Q, A_K, A_V, bias_a[ci], None, name=f"band_a{ci}_fwd", **A_HEADS)
            oa.append(o)
            la.append(l)
        oa[0], la[0] = oa[0].reshape(t, 256), la[0].reshape(t, 256)
        ya, lse_a = _combine_a(oa, la, tm=512, name="combine_a")
        yb, lse_b = _band_fwd(nat4(qb), B_Q, B_K, B_V, bias_b, w["sink_t"], name="band_b_fwd", **B_HEADS)
        yb = yb.reshape(t, 256)
        yc = _c_fwd(proj, w["c_g"], w["c_b"], w["c_ws"], w["c_bst"], tm=512, name="c_fwd")
        yd, lse_d = _dense_fwd(qd, tq=256, name="dense_fwd")
        w.update(matmul_weights(li, "rest", yd))
        mixed, x1 = _norm_mm((ya, yb, yc, yd), w["out_gain"], w["w_out"], x, tm=1024, tn=1024, name="mix_out_fwd")
        hn1, h = _norm_mm((x1,), w["g_ffn"], w["w_up"], None, tm=1024, tn=1408, name="ffn_up_fwd", out_dtype=BF16)
        act = _conv_gate_fwd(h, w["conv_w"], w["conv_b"], name="conv_gate_fwd")
        x2 = _mm(act, w["w_down"], "nn", x1, tm=1024, tn=1024, out_dtype=F32, name="ffn_down_fwd")
        hn2, x3, gate, pp = _ple_fwd(x2, w["g_ple"], w["w_gate"], pe, li * (t // 1024), w["w_proj"], tm=1024,
                                     name="ple_fwd")
        saved.append(dict(w=w, x0=x, hn0=hn0, proj=proj, qa=qa, qb=qb, qd=qd, ya=ya, lse_a=lse_a, yb=yb, lse_b=lse_b,
                          yc=yc, yd=yd, lse_d=lse_d, mixed=mixed, x1=x1, hn1=hn1, h=h, act=act, x2=x2, hn2=hn2,
                          gate=gate, pp=pp))
        x = x3

    loss_tile, dx = _loss_head(x, tgt, tm=512, name="loss_head")
    grads = [None] * DEPTH
    dbias_a, dbias_bs = [[], [], []], []
    for li in reversed(range(DEPTH)):
        s = saved[li]
        w = s["w"]
        g = {}
        dz, dpp = _ple_bwd_ew(dx, s["gate"], s["pp"], tm=512, name="ple_bwd_ew")
        g["w_gate"] = _mm(s["hn2"], dz, "tn", None, tm=1024, tn=512, out_dtype=BF16, name="dw_gate")
        g["w_proj"] = _mm(pe, dpp, "tn", None, tm=256, tn=1024, out_dtype=BF16, name="dw_proj", a_rows=(li, t))
        dx2, dx2b, g["g_ple"] = _mm_bt_normbwd((dz,), w["w_gate"], (s["x2"],), w["g_ple"], dx, tm=1024, tn=1024,
                                               name="ple_bwd", emit_bf16=True)
        g["w_down"] = _mm(s["act"], dx2b, "tn", None, tm=1408, tn=512, out_dtype=BF16, name="dw_down")
        dact = _mm(dx2b, w["w_down"], "nt", None, tm=1024, tn=1408, out_dtype=BF16, name="ffn_down_bwd")
        dhg, dhu, g["conv_w"], g["conv_b"] = _conv_gate_bwd(s["h"], dact, w["conv_w"], w["conv_b"], name="conv_gate_bwd")
        g["w_up"] = jnp.concatenate(
            [_mm(s["hn1"], dhalf, "tn", None, tm=1024, tn=1408, out_dtype=BF16, name=f"dw_up_{nm}")
             for nm, dhalf in (("gate", dhg), ("up", dhu))], axis=1)
        dx1, dx1b, g["g_ffn"] = _mm_bt_normbwd((dhg, dhu), w["w_up"], (s["x1"],), w["g_ffn"], dx2, tm=1024, tn=1408,
                                               name="ffn_up_bwd", emit_bf16=True)
        g["w_out"] = _mm(s["mixed"], dx1b, "tn", None, tm=1024, tn=512, out_dtype=BF16, name="dw_out")
        grads_ready(li, "mid", g)
        dycat, g["out_gain"] = _mm_bt_normbwd((dx1b,), w["w_out"], (s["ya"], s["yb"], s["yc"], s["yd"]), w["out_gain"],
                                              None, tm=1024, tn=1024, name="mix_out_bwd")
        dy_r, lse_r, dl_a, dl_b, dl_d = _deltas(dycat, s["ya"], s["yb"], s["yd"], s["lse_a"], tm=512, name="deltas")
        dy_a = (nat4(dycat),) + tuple(dy_r)
        lse_a = (nat4(s["lse_a"]),) + tuple(lse_r)
        dl_a = (nat4(dl_a[0]),) + tuple(dl_a[1:])
        da = []
        for ci in range(3):
            dq, dk, dv, dbias = _band_bwd(s["qa"][ci], A_Q, A_K, A_V, bias_a[ci], None, dy_a[ci], 0, lse_a[ci],
                                          dl_a[ci], name=f"band_a{ci}_bwd", **A_HEADS)
            if ci == 0:
                dq, dk, dv = (a.reshape(t, 256) for a in (dq, dk, dv))
            da.append((dq, dk, dv))
            dbias_a[ci].append(dbias.reshape(4, -1))
        dqb, dkb, dvb, dbias_b, dsink = _band_bwd(nat4(s["qb"]), B_Q, B_K, B_V, bias_b, w["sink_t"], nat4(dycat), 1,
                                                  nat4(s["lse_b"]), nat4(dl_b), name="band_b_bwd", **B_HEADS)
        dbias_bs.append(dbias_b.reshape(4, -1))
        g["sink"] = dsink[:, 0, 0]
        dd = _dense_bwd(s["qd"], dycat, s["lse_d"], dl_d, tq=256, name="dense_bwd")
        dcu, dcv, g["c_ws"], dbs, g["c_g"], g["c_b"] = _c_bwd(s["proj"], dycat, w["c_g"], w["c_b"], w["c_ws"],
                                                               w["c_wst"], w["c_bst"], tm=512, name="c_bwd")
        g["c_bs"] = dbs[:, ::64].T
        db = (dqb.reshape(t, 256), dkb.reshape(t, 128), dvb.reshape(t, 128))
        dproj, dgains = _qkprep_bwd(s["proj"], da, db, dd, dcu, dcv, w["qk_gains"], cos, sins, tm=512, name="qkprep_bwd")
        g["qk_gain"] = dgains[:6, :64].reshape(3, 2, HEAD_DIM)
        g["w_in"] = _mm(s["hn0"], dproj, "tn", None, tm=1024, tn=1152, out_dtype=BF16, name="dw_in")
        dx, g["g_mix"] = _mm_bt_normbwd((dproj,), w["w_in"], (s["x0"],), w["g_mix"], dx1, tm=1024, tn=1152,
                                        name="mix_in_bwd")
        grads[li] = g
        grads_ready(li, "end", g)
    d_table_a = sum(_bucket_reduce(dbias_a[ci], pats_a[ci], name=f"bucket_a{ci}") for ci in range(3))
    d_table_b = _bucket_reduce(dbias_bs, pat_b, name="bucket_b")
    d_rel_bias = jnp.concatenate([d_table_a, d_table_b], axis=0).T
    return loss_tile[0, 0], dx, grads, d_rel_bias


WEIGHT_NAMES = ("rel_bias", "ln_mix_g", "w_in", "qk_gain", "sink", "c_norm_g", "c_norm_b", "c_ws", "c_bs", "out_gain",
                "w_out", "ln_ffn_g", "w_up", "conv_w", "conv_b", "w_down", "ln_ple_g", "w_ple_gate", "w_ple_proj")
COL_SHARDED = ("w_in", "w_up", "w_ple_proj")
ROW_SHARDED = ("w_out", "w_down", "w_ple_gate")
SMALL_SHARDED = ("conv_w", "out_gain")
REPLICATED = tuple(n for n in WEIGHT_NAMES if n not in COL_SHARDED + ROW_SHARDED + SMALL_SHARDED)
LOCAL_GRAD_KEY = {"ln_mix_g": "g_mix", "ln_ffn_g": "g_ffn", "ln_ple_g": "g_ple", "c_norm_g": "c_g", "c_norm_b": "c_b",
                  "w_ple_gate": "w_gate", "w_ple_proj": "w_proj"}


def _full_from_gathered(name, gathered):
    _, r, c = gathered.shape
    if name in ROW_SHARDED:
        return gathered.reshape(N_DEV * r, c)
    return jnp.transpose(gathered, (1, 0, 2)).reshape(r, N_DEV * c)


def _slots_from_full(name, full):
    rows, cols = full.shape
    if name in ROW_SHARDED:
        return full.reshape(N_DEV, rows // N_DEV, cols)
    return jnp.transpose(full.reshape(rows, N_DEV, cols // N_DEV), (1, 0, 2))


def _piece_rows(shape):
    return -(-int(np.prod(shape)) // 1024) * 8


def _pack_rows(arrays):
    pieces = []
    for a in arrays:
        n, rows = int(np.prod(a.shape)), _piece_rows(a.shape)
        flat = a.astype(F32).reshape(-1)
        if n != rows * LANES:
            flat = jnp.pad(flat, (0, rows * LANES - n))
        pieces.append(flat.reshape(rows, LANES))
    return jnp.concatenate(pieces, axis=0)


def _unpack_rows(packed, shapes):
    out, off = [], 0
    for shp in shapes:
        n, rows = int(np.prod(shp)), _piece_rows(shp)
        piece = packed[off:off + rows]
        out.append((piece if n == rows * LANES else piece.reshape(-1)[:n]).reshape(shp))
        off += rows
    return out


def kernel(x, p, rel_bias, ln_mix_g, w_in, qk_gain, sink, c_norm_g, c_norm_b, c_ws, c_bs, out_gain, w_out, ln_ffn_g, w_up, conv_w, conv_b, w_down, ln_ple_g, w_ple_gate, w_ple_proj, loss_target, m_rel_bias, m_ln_mix_g, m_w_in, m_qk_gain, m_sink, m_c_norm_g, m_c_norm_b, m_c_ws, m_c_bs, m_out_gain, m_w_out, m_ln_ffn_g, m_w_up, m_conv_w, m_conv_b, m_w_down, m_ln_ple_g, m_w_ple_gate, m_w_ple_proj, v_rel_bias, v_ln_mix_g, v_w_in, v_qk_gain, v_sink, v_c_norm_g, v_c_norm_b, v_c_ws, v_c_bs, v_out_gain, v_w_out, v_ln_ffn_g, v_w_up, v_conv_w, v_conv_b, v_w_down, v_ln_ple_g, v_w_ple_gate, v_w_ple_proj):
    env = dict(locals())
    wt = {n: env[n] for n in WEIGHT_NAMES}
    mom_m = {n: env["m_" + n] for n in WEIGHT_NAMES}
    mom_v = {n: env["v_" + n] for n in WEIGHT_NAMES}
    bl = x.shape[0]
    t = bl * SEQ
    me = 4 * lax.axis_index("x") + 2 * lax.axis_index("y") + lax.axis_index("c")

    big = COL_SHARDED + ROW_SHARDED
    full = {}
    small_shapes = [wt[n].shape for n in SMALL_SHARDED]
    small = _allgather_vmem(_pack_rows([wt[n] for n in SMALL_SHARDED]), name="gather_small")
    small = small.reshape(N_DEV, -1)
    off = 0
    for n, shp in zip(SMALL_SHARDED, small_shapes):
        cnt = int(np.prod(shp))
        g = small[:, off:off + cnt].reshape((N_DEV,) + tuple(shp))
        full[n] = jnp.transpose(g, (1, 2, 0, 3)).reshape(shp[0], shp[1], N_DEV * shp[2])
        off += _piece_rows(shp) * LANES

    def head_gain(li, a, b, reps):
        g = jnp.tile(qk_gain[li, a, b], reps)
        return jnp.pad(g, (0, 256 - g.shape[0]))

    wts = []
    for li in range(DEPTH):
        rows = [head_gain(li, 0, 0, 4), head_gain(li, 0, 1, 4), head_gain(li, 1, 0, 4), head_gain(li, 1, 1, 2),
                head_gain(li, 2, 0, 4), head_gain(li, 2, 1, 2), jnp.zeros((256,), F32), jnp.zeros((256,), F32)]
        wts.append(dict(
            g_mix=ln_mix_g[li].reshape(1, -1), qk_gains=jnp.stack(rows),
            sink_t=jnp.broadcast_to(sink[li][:, None, None], (4, 8, 128)),
            c_g=c_norm_g[li].reshape(1, -1), c_b=c_norm_b[li].reshape(1, -1), c_ws=c_ws[li].astype(BF16),
            c_wst=jnp.transpose(c_ws[li], (0, 2, 1)).astype(BF16), c_bst=jnp.repeat(c_bs[li].T, 64, axis=1),
            out_gain=full["out_gain"][li].reshape(1, -1), g_ffn=ln_ffn_g[li].reshape(1, -1),
            conv_w=full["conv_w"][li], conv_b=conv_b[li].reshape(1, -1), g_ple=ln_ple_g[li].reshape(1, -1)))

    local_key = {"w_ple_gate": "w_gate", "w_ple_proj": "w_proj"}

    gather_names = {"in": ("w_in",), "rest": tuple(n for n in big if n != "w_in")}
    gathered = {}
    for cid, (li, names) in enumerate(((0, gather_names["in"]), (0, gather_names["rest"]), (1, big))):
        lands = _sc_allgather([wt[n][li].astype(BF16) for n in names], collective_id=cid,
                              name=f"gather_{li}_{len(names)}")
        gathered.setdefault(li, {}).update(zip(names, lands))

    def matmul_weights(li, part, after):
        out = {}
        for n in gather_names[part]:
            g, _ = lax.optimization_barrier((gathered[li][n], after))
            out[local_key.get(n, n)] = _full_from_gathered(n, g)
        return out

    mid_names = ("w_ple_gate", "w_ple_proj", "w_down", "w_up", "w_out")
    end_names = ("w_in",)
    landed = {}

    def start_exchange(li, names, g, tag, cid):
        slots = [_slots_from_full(n, g[local_key.get(n, n)]) for n in names]
        lands = _sc_exchange(slots, scatter=True, collective_id=cid, name=f"grads_{li}_{tag}")
        landed.update({(n, li): land for n, land in zip(names, lands)})

    def grads_ready(li, stage, g):
        if li == 0:
            start_exchange(li, mid_names if stage == "mid" else end_names, g, stage, 5 if stage == "mid" else 6)
        elif stage == "end":
            start_exchange(li, mid_names + end_names, g, stage, 4)

    loss_part, dx, grads, d_rel_bias = _local_step(
        x.reshape(t, D_MODEL), p.reshape(DEPTH * t, PLE_DIM), loss_target.reshape(t, D_MODEL), rel_bias, wts,
        matmul_weights, grads_ready)
    loss = lax.psum(loss_part, ("x", "y", "c"))

    def local_grad(n):
        if n == "rel_bias":
            return d_rel_bias
        key = LOCAL_GRAD_KEY.get(n, n)
        return jnp.stack([grads[li][key].reshape(wt[n].shape[1:]) if n in REPLICATED else grads[li][key]
                          for li in range(DEPTH)])

    small_names = REPLICATED + SMALL_SHARDED
    small_full_shapes = [wt[n].shape if n in REPLICATED else full[n].shape for n in small_names]
    (small_parts,) = _sc_allgather([_pack_rows([local_grad(n) for n in small_names])], collective_id=7,
                                   name="allgather_small_grads")

    out_g, out_d, out_m, out_v = {}, {}, {}, {}
    for n in big:
        shp = wt[n].shape
        two_d = lambda a: a.reshape(-1, shp[-1])
        res = _adamw_reduce([landed[n, li] for li in range(DEPTH)], two_d(wt[n]), two_d(mom_m[n]), two_d(mom_v[n]),
                            tr=32 if n == "w_down" else 128, name="adamw_" + n)
        out_g[n], out_d[n], out_m[n], out_v[n] = [r.reshape(shp) for r in res]

    reduced = _sum_slots(small_parts, name="sum_small_grads")
    reduced = dict(zip(small_names, _unpack_rows(reduced, small_full_shapes)))
    rep_shapes = [wt[n].shape for n in REPLICATED]
    upd = _adamw_plain(_pack_rows([reduced[n] for n in REPLICATED]), _pack_rows([wt[n] for n in REPLICATED]),
                       _pack_rows([mom_m[n] for n in REPLICATED]), _pack_rows([mom_v[n] for n in REPLICATED]),
                       name="adamw_replicated")
    for dst, packed in zip((out_d, out_m, out_v), upd):
        dst.update(zip(REPLICATED, _unpack_rows(packed, rep_shapes)))
    for n in REPLICATED:
        out_g[n] = reduced[n]
    for n in SMALL_SHARDED:
        shp = wt[n].shape
        g = reduced[n].reshape(shp[0], shp[1], N_DEV, shp[2])
        g = lax.dynamic_index_in_dim(g, me, axis=2, keepdims=False)
        two_d = lambda a: a.reshape(-1, shp[-1])
        res = _adamw_plain(two_d(g), two_d(wt[n]), two_d(mom_m[n]), two_d(mom_v[n]), name="adamw_" + n)
        out_g[n] = g
        out_d[n], out_m[n], out_v[n] = [r.reshape(shp) for r in res]

    return (loss, dx.reshape(bl, SEQ, D_MODEL), *[out_g[n] for n in WEIGHT_NAMES], *[out_d[n] for n in WEIGHT_NAMES],
            *[out_m[n] for n in WEIGHT_NAMES], *[out_v[n] for n in WEIGHT_NAMES])
```

```python
import math

import jax
import jax.numpy as jnp
import numpy as np
from jax import lax
from jax.experimental import pallas as pl
from jax.experimental.pallas import tpu as pltpu
from jax.experimental.pallas import tpu_sc as plsc

F32 = jnp.float32
BF16 = jnp.bfloat16
HI = lax.Precision.HIGHEST

N_DEV = 8
D_MODEL = 1024
SEQ = 2048
DEPTH = 2
HEAD_DIM = 64
IN_WIDTH = 2304
D_FF = 2816
PLE_DIM = 256
C_CHUNK = 128
C_GROUPS = 4
DILATED_CFGS = ((128, 1), (512, 4), (2048, 16))
DILATIONS = tuple(d for _, d in DILATED_CFGS)
A_RADIUS = 64
SWA_RADIUS = 128
BAND_BLOCK = 256
GRID_W = 64
ROPE_THETA = 10000.0
REL_BUCKETS = 32
REL_MAX_DIST = 1024
EPS = 1e-6
NEG_INF = -1e30
ATTN_SCALE = HEAD_DIM ** -0.5
LANES = 128

ADAM_LR = 0.001
ADAM_B1 = 0.9
ADAM_B2 = 0.999
ADAM_EPS = 1e-08
ADAM_WD = 0.01
ADAM_STEP = 10

MESH = pl.DeviceIdType.MESH
NT = (((1,), (1,)), ((), ()))
TN = (((0,), (0,)), ((), ()))
ARB = "arbitrary"
PAR = "parallel"


def _cparams(*sem):
    return pltpu.CompilerParams(dimension_semantics=tuple(sem))


def _sds(shape, dtype):
    return jax.ShapeDtypeStruct(tuple(shape), dtype)


def _group_sum_matrix(n, same_group):
    r = lax.broadcasted_iota(jnp.int32, (n, n), 0)
    c = lax.broadcasted_iota(jnp.int32, (n, n), 1)
    if same_group:
        return ((r >> 6) == (c >> 6)).astype(F32)
    return ((r & 63) == (c & 63)).astype(F32)


def _seg_sum(x, e):
    eb = e.astype(BF16)
    hi = x.astype(BF16)
    lo = (x - hi.astype(F32)).astype(BF16)
    return jnp.dot(hi, eb, preferred_element_type=F32) + jnp.dot(lo, eb, preferred_element_type=F32)


def _gelu(x):
    c = math.sqrt(2.0 / math.pi)
    return 0.5 * x * (1.0 + jnp.tanh(c * (x + 0.044715 * (x * x * x))))


def _gelu_grad(x):
    c = math.sqrt(2.0 / math.pi)
    t = jnp.tanh(c * (x + 0.044715 * (x * x * x)))
    return 0.5 * (1.0 + t) + 0.5 * x * (1.0 - t * t) * c * (1.0 + 3.0 * 0.044715 * (x * x))


def _sigmoid(x):
    return 1.0 / (1.0 + jnp.exp(-x))


def _scatter_cols(scratch, first, val):
    for c in range(val.shape[1] // LANES):
        scratch[first + c] = val[:, c * LANES:(c + 1) * LANES]


def _gather_cols(scratch, first, ncol):
    return jnp.concatenate([scratch[first + c] for c in range(ncol)], axis=1)


def _read_residue(scratch, first, ncol, r, d):
    n = scratch.shape[1] // d
    return jnp.concatenate([scratch.at[first + c][pl.ds(r, n, stride=d), :] for c in range(ncol)], axis=1)


def _write_residue(scratch, first, r, d, val):
    n = scratch.shape[1] // d
    for c in range(val.shape[1] // LANES):
        scratch.at[first + c][pl.ds(r, n, stride=d), :] = val[:, c * LANES:(c + 1) * LANES]


def _norm_mm(xs, gain, w, res, *, tm, tn, name, out_dtype=F32):
    t = xs[0].shape[0]
    k = sum(x.shape[1] for x in xs)
    n = w.shape[1]
    ng = len(xs)
    has_res = res is not None

    def body(*refs):
        x_refs = refs[:ng]
        g_ref, w_ref = refs[ng], refs[ng + 1]
        res_ref = refs[ng + 2] if has_res else None
        hn_ref, o_ref, hn_s = refs[ng + 2 + has_res:]

        @pl.when(pl.program_id(1) == 0)
        def _():
            off = 0
            for xr in x_refs:
                x = xr[...]
                wd = x.shape[1]
                r = lax.rsqrt(jnp.mean(x * x, axis=-1, keepdims=True) + EPS)
                hn_s[:, off:off + wd] = (x * r * g_ref[:, off:off + wd]).astype(BF16)
                off += wd
            hn_ref[...] = hn_s[...]

        acc = jnp.dot(hn_s[...], w_ref[...], preferred_element_type=F32)
        if has_res:
            acc = acc + res_ref[...]
        o_ref[...] = acc.astype(out_dtype)

    in_specs = [pl.BlockSpec((tm, x.shape[1]), lambda i, j: (i, 0)) for x in xs]
    in_specs += [pl.BlockSpec((1, k), lambda i, j: (0, 0)), pl.BlockSpec((k, tn), lambda i, j: (0, j))]
    args = list(xs) + [gain, w]
    if has_res:
        in_specs.append(pl.BlockSpec((tm, tn), lambda i, j: (i, j)))
        args.append(res)
    return pl.pallas_call(
        body, name=name, grid=(t // tm, n // tn), in_specs=in_specs,
        out_specs=[pl.BlockSpec((tm, k), lambda i, j: (i, 0)), pl.BlockSpec((tm, tn), lambda i, j: (i, j))],
        out_shape=[_sds((t, k), BF16), _sds((t, n), out_dtype)],
        scratch_shapes=[pltpu.VMEM((tm, k), BF16)],
        compiler_params=_cparams(PAR, ARB),
    )(*args)


def _mm(a, b, mode, res, *, tm, tn, out_dtype, name, a_rows=None):
    if mode == "tn":
        kk, m = a.shape
        blk_a = 0
        if a_rows is not None:
            blk_a, kk = a_rows
        a_spec = pl.BlockSpec((kk, tm), lambda i, j: (blk_a, i))
    else:
        m, kk = a.shape
        a_spec = pl.BlockSpec((tm, kk), lambda i, j: (i, 0))
    if mode == "nt":
        n = b.shape[0]
        b_spec = pl.BlockSpec((tn, kk), lambda i, j: (j, 0))
    else:
        n = b.shape[1]
        b_spec = pl.BlockSpec((kk, tn), lambda i, j: (0, j))
    has_res = res is not None

    def body(*refs):
        a_ref, b_ref = refs[0], refs[1]
        o_ref = refs[-1]
        av = a_ref[...].astype(BF16)
        bv = b_ref[...].astype(BF16)
        if mode == "nn":
            acc = jnp.dot(av, bv, preferred_element_type=F32)
        elif mode == "nt":
            acc = lax.dot_general(av, bv, NT, preferred_element_type=F32)
        else:
            acc = lax.dot_general(av, bv, TN, preferred_element_type=F32)
        if has_res:
            acc = acc + refs[2][...]
        o_ref[...] = acc.astype(out_dtype)

    in_specs = [a_spec, b_spec]
    args = [a, b]
    if has_res:
        in_specs.append(pl.BlockSpec((tm, tn), lambda i, j: (i, j)))
        args.append(res)
    return pl.pallas_call(
        body, name=name, grid=(m // tm, n // tn), in_specs=in_specs,
        out_specs=pl.BlockSpec((tm, tn), lambda i, j: (i, j)),
        out_shape=_sds((m, n), out_dtype),
        compiler_params=_cparams(PAR, PAR),
    )(*args)


def _mm_bt_normbwd(dys, w, xs, gain, dres, *, tm, tn, name, emit_bf16=False):
    t, wd_each = dys[0].shape
    nd = len(dys)
    per = wd_each // tn
    nj = nd * per
    k = w.shape[0]
    ng = len(xs)
    has_res = dres is not None

    def body(*refs):
        dy_refs = refs[:nd]
        w_ref = refs[nd]
        x_refs = refs[nd + 1:nd + 1 + ng]
        g_ref = refs[nd + 1 + ng]
        dres_ref = refs[nd + 2 + ng] if has_res else None
        outs = refs[nd + 2 + ng + has_res:]
        dx_ref = outs[0]
        dxb_ref = outs[1] if emit_bf16 else None
        dg_ref, acc = outs[1 + emit_bf16:]
        i, j = pl.program_id(0), pl.program_id(1)

        @pl.when(j == 0)
        def _():
            acc[...] = jnp.zeros_like(acc)

        for d, dy_ref in enumerate(dy_refs):
            @pl.when((j >= d * per) & (j < (d + 1) * per))
            def _(dy_ref=dy_ref):
                acc[...] += lax.dot_general(dy_ref[...].astype(BF16), w_ref[...], NT, preferred_element_type=F32)

        @pl.when(j == nj - 1)
        def _():
            @pl.when(i == 0)
            def _():
                dg_ref[...] = jnp.zeros_like(dg_ref)

            off = 0
            for xr in x_refs:
                x = xr[...]
                wd = x.shape[1]
                g = g_ref[:, off:off + wd]
                dyn = acc[:, off:off + wd]
                r = lax.rsqrt(jnp.mean(x * x, axis=-1, keepdims=True) + EPS)
                gdy = dyn * g
                dx = r * gdy - x * (r * r * r * jnp.mean(gdy * x, axis=-1, keepdims=True))
                if has_res:
                    dx = dx + dres_ref[:, off:off + wd]
                dx_ref[:, off:off + wd] = dx
                if emit_bf16:
                    dxb_ref[:, off:off + wd] = dx.astype(BF16)
                dg_ref[:, off:off + wd] += jnp.sum(dyn * x * r, axis=0, keepdims=True)
                off += wd

    def dy_map(d):
        return lambda i, j: (i, jnp.clip(j - d * per, 0, per - 1))

    in_specs = [pl.BlockSpec((tm, tn), dy_map(d)) for d in range(nd)]
    in_specs.append(pl.BlockSpec((k, tn), lambda i, j: (0, j)))
    in_specs += [pl.BlockSpec((tm, x.shape[1]), lambda i, j: (i, 0)) for x in xs]
    in_specs.append(pl.BlockSpec((1, k), lambda i, j: (0, 0)))
    args = list(dys) + [w] + list(xs) + [gain]
    if has_res:
        in_specs.append(pl.BlockSpec((tm, k), lambda i, j: (i, 0)))
        args.append(dres)
    row = pl.BlockSpec((tm, k), lambda i, j: (i, 0))
    out_specs = [row] + ([row] if emit_bf16 else []) + [pl.BlockSpec((1, k), lambda i, j: (0, 0))]
    out_shape = [_sds((t, k), F32)] + ([_sds((t, k), BF16)] if emit_bf16 else []) + [_sds((1, k), F32)]
    return pl.pallas_call(
        body, name=name, grid=(t // tm, nj), in_specs=in_specs, out_specs=out_specs, out_shape=out_shape,
        scratch_shapes=[pltpu.VMEM((tm, k), F32)],
        compiler_params=_cparams(ARB, ARB),
    )(*args)


def _rope_partner(y):
    n = y.shape[1]
    lane = lax.broadcasted_iota(jnp.int32, y.shape, 1)
    return jnp.where((lane & 31) < 16, pltpu.roll(y, n - 16, 1), pltpu.roll(y, 16, 1))


def _residue_specs(tm, width, nt):
    specs = [pl.BlockSpec((tm, width), lambda b, i: (b * nt + i, 0))]
    for d in DILATIONS[1:]:
        specs.append(pl.BlockSpec((None, d, tm // d, width), lambda b, i: (b, 0, i, 0)))
    return specs


def _residue_shapes(bl, width, dtype):
    return [_sds((bl * SEQ, width), dtype)] + [_sds((bl, d, SEQ // d, width), dtype) for d in DILATIONS[1:]]


def _qkprep_fwd(proj, gains, cos, sins, *, tm, name):
    t = proj.shape[0]
    bl = t // SEQ
    nt = SEQ // tm

    def body(p_ref, g_ref, c_ref, s_ref, qa1_ref, qa4_ref, qa16_ref, qb_ref, qd_ref, scr):
        e = _group_sum_matrix(256, True)

        def hn(x, row):
            wd = x.shape[1]
            ms = _seg_sum(x * x, e[:wd, :wd]) * (1.0 / HEAD_DIM)
            return x * lax.rsqrt(ms + EPS) * g_ref[row:row + 1, :wd]

        qa = jnp.concatenate([hn(p_ref[:, 0:256], 0) * ATTN_SCALE, hn(p_ref[:, 256:512], 1), p_ref[:, 512:768]], axis=1)
        qa1_ref[...] = qa.astype(BF16)
        _scatter_cols(scr, 0, qa)
        for d, ref in ((4, qa4_ref), (16, qa16_ref)):
            for r in range(d):
                ref[r] = _read_residue(scr, 0, 6, r, d).astype(BF16)
        qb_ref[:, 0:256] = (hn(p_ref[:, 768:1024], 2) * ATTN_SCALE).astype(BF16)
        qb_ref[:, 256:384] = hn(p_ref[:, 1024:1152], 3).astype(BF16)
        qb_ref[:, 384:512] = p_ref[:, 1152:1280].astype(BF16)
        yq = hn(p_ref[:, 1792:2048], 4)
        yq = yq * c_ref[...] + _rope_partner(yq) * s_ref[...]
        qd_ref[:, 0:256] = (yq * ATTN_SCALE).astype(BF16)
        yk = hn(p_ref[:, 2048:2176], 5)
        yk = yk * c_ref[:, 0:128] + _rope_partner(yk) * s_ref[:, 0:128]
        qd_ref[:, 256:384] = yk.astype(BF16)
        qd_ref[:, 384:512] = p_ref[:, 2176:2304].astype(BF16)

    row = lambda width: pl.BlockSpec((tm, width), lambda b, i: (b * nt + i, 0))
    tab = pl.BlockSpec((tm, 256), lambda b, i: (i, 0))
    return pl.pallas_call(
        body, name=name, grid=(bl, nt),
        in_specs=[row(IN_WIDTH), pl.BlockSpec((8, 256), lambda b, i: (0, 0)), tab, tab],
        out_specs=_residue_specs(tm, 768, nt) + [row(512), row(512)],
        out_shape=_residue_shapes(bl, 768, BF16) + [_sds((t, 512), BF16), _sds((t, 512), BF16)],
        scratch_shapes=[pltpu.VMEM((6, tm, LANES), F32)],
        compiler_params=_cparams(PAR, PAR),
    )(proj, gains, cos, sins)


def _qkprep_bwd(proj, da, db, dd, dcu, dcv, gains, cos, sins, *, tm, name):
    t = proj.shape[0]
    bl = t // SEQ
    nt = SEQ // tm
    flat = [a for cfg in da for a in cfg] + list(db) + list(dd) + [dcu, dcv]

    def body(*refs):
        p_ref, g_ref, c_ref, s_ref = refs[:4]
        d_refs = refs[4:4 + len(flat)]
        dp_ref, dg_ref, scr = refs[4 + len(flat):]
        a_refs = d_refs[:9]
        dqb_ref, dkb_ref, dvb_ref, dqd_ref, dkd_ref, dvd_ref, dcu_ref, dcv_ref = d_refs[9:]
        e = _group_sum_matrix(256, True)
        first = (pl.program_id(0) == 0) & (pl.program_id(1) == 0)
        last = (pl.program_id(0) == bl - 1) & (pl.program_id(1) == nt - 1)

        @pl.when(first)
        def _():
            dg_ref[...] = jnp.zeros_like(dg_ref)

        def hn_bwd(x, dy, row):
            wd = x.shape[1]
            ee = e[:wd, :wd]
            g = g_ref[row:row + 1, :wd]
            r = lax.rsqrt(_seg_sum(x * x, ee) * (1.0 / HEAD_DIM) + EPS)
            gdy = dy * g
            dx = r * gdy - x * (r * r * r * (_seg_sum(gdy * x, ee) * (1.0 / HEAD_DIM)))
            dg_ref[row:row + 1, :wd] += jnp.sum(dy * x * r, axis=0, keepdims=True)
            return dx

        def rope_bwd(dy, wd):
            return dy * c_ref[:, :wd] + _rope_partner(dy * s_ref[:, :wd])

        dqkv = jnp.concatenate([a_refs[0][...], a_refs[1][...], a_refs[2][...]], axis=1)
        for ci, d in ((1, 4), (2, 16)):
            for r in range(d):
                part = jnp.concatenate([a_refs[3 * ci + m][r] for m in range(3)], axis=1)
                _write_residue(scr, 0, r, d, part)
            dqkv = dqkv + _gather_cols(scr, 0, 6)
        dp_ref[:, 0:256] = hn_bwd(p_ref[:, 0:256], dqkv[:, 0:256] * ATTN_SCALE, 0).astype(BF16)
        dp_ref[:, 256:512] = hn_bwd(p_ref[:, 256:512], dqkv[:, 256:512], 1).astype(BF16)
        dp_ref[:, 512:768] = dqkv[:, 512:768].astype(BF16)
        dp_ref[:, 768:1024] = hn_bwd(p_ref[:, 768:1024], dqb_ref[...] * ATTN_SCALE, 2).astype(BF16)
        dp_ref[:, 1024:1152] = hn_bwd(p_ref[:, 1024:1152], dkb_ref[...], 3).astype(BF16)
        dp_ref[:, 1152:1280] = dvb_ref[...].astype(BF16)
        dp_ref[:, 1280:1536] = dcu_ref[...].astype(BF16)
        dp_ref[:, 1536:1792] = dcv_ref[...].astype(BF16)
        dp_ref[:, 1792:2048] = hn_bwd(p_ref[:, 1792:2048], rope_bwd(dqd_ref[...] * ATTN_SCALE, 256), 4).astype(BF16)
        dp_ref[:, 2048:2176] = hn_bwd(p_ref[:, 2048:2176], rope_bwd(dkd_ref[...], 128), 5).astype(BF16)
        dp_ref[:, 2176:2304] = dvd_ref[...].astype(BF16)

        @pl.when(last)
        def _():
            dg_ref[...] = _seg_sum(dg_ref[...], _group_sum_matrix(256, False))

    row = lambda width: pl.BlockSpec((tm, width), lambda b, i: (b * nt + i, 0))
    tab = pl.BlockSpec((tm, 256), lambda b, i: (i, 0))
    in_specs = [row(IN_WIDTH), pl.BlockSpec((8, 256), lambda b, i: (0, 0)), tab, tab]
    res_specs = _residue_specs(tm, 256, nt)
    in_specs += [res_specs[ci] for ci in range(3) for _ in range(3)]
    in_specs += [row(a.shape[1]) for a in flat[9:]]
    return pl.pallas_call(
        body, name=name, grid=(bl, nt), in_specs=in_specs,
        out_specs=[row(IN_WIDTH), pl.BlockSpec((8, 256), lambda b, i: (0, 0))],
        out_shape=[_sds((t, IN_WIDTH), BF16), _sds((8, 256), F32)],
        scratch_shapes=[pltpu.VMEM((6, tm, LANES), F32)],
        compiler_params=_cparams(ARB, ARB),
    )(proj, gains, cos, sins, *flat)


BAND_ROWS_PER_STEP = 512


def _residues_per_step(dil, seq_len):
    return min(dil, max(1, BAND_ROWS_PER_STEP // seq_len))


def _band_spec(seq_len, spec, rb):
    width, idx = spec
    return pl.BlockSpec((None, rb, seq_len, width), lambda b, r: (b, r, 0, idx))


def _fill_padded(dst, src_ref, rad, seq_len):
    z = jnp.zeros((rad, dst.shape[1]), dst.dtype)
    dst[0:rad, :] = z
    dst[rad + seq_len:rad + seq_len + rad, :] = z
    dst[rad:rad + seq_len, :] = src_ref[...]


def _band_fwd(src, qs, ks, vs, bias, sink, *, rad, nh, nkv, name):
    bl, dil, sl, _ = src.shape
    blk = bias.shape[1]
    kw = blk + 2 * rad
    nb = sl // blk
    rep = nh // nkv
    has_sink = sink is not None
    rb = _residues_per_step(dil, sl)

    def body(*refs):
        q_all, k_all, v_all, b_ref = refs[:4]
        s_ref = refs[4] if has_sink else None
        o_all, l_all, kp, vp = refs[4 + has_sink:]
        for ri in range(rb):
            one_sequence(q_all.at[ri], k_all.at[ri], v_all.at[ri], b_ref, s_ref, o_all.at[ri], l_all.at[ri], kp, vp)

    def one_sequence(q_ref, k_ref, v_ref, b_ref, s_ref, o_ref, l_ref, kp, vp):
        _fill_padded(kp, k_ref, rad, sl)
        _fill_padded(vp, v_ref, rad, sl)

        def blk_body(i, carry):
            r0 = pl.multiple_of(i * blk, blk)
            qb = q_ref[pl.ds(r0, blk), :]
            kwin = kp[pl.ds(r0, kw), :]
            vwin = vp[pl.ds(r0, kw), :]
            col = r0 - rad + lax.broadcasted_iota(jnp.int32, (blk, kw), 1)
            neg = jnp.where((col >= 0) & (col < sl), 0.0, NEG_INF).astype(F32)
            for h in range(nh):
                g = h // rep
                hs = slice(h * HEAD_DIM, (h + 1) * HEAD_DIM)
                gs = slice(g * HEAD_DIM, (g + 1) * HEAD_DIM)
                s = lax.dot_general(qb[:, hs], kwin[:, gs], NT, preferred_element_type=F32)
                s = s + b_ref[h] + neg
                m = jnp.max(s, axis=1, keepdims=True)
                if has_sink:
                    sk = s_ref[h][0:1, 0:1]
                    m = jnp.maximum(m, sk)
                p = jnp.exp(s - m)
                den = jnp.sum(p, axis=1, keepdims=True)
                if has_sink:
                    den = den + jnp.exp(sk - m)
                o = jnp.dot(p.astype(BF16), vwin[:, gs], preferred_element_type=F32) / den
                o_ref[pl.ds(r0, blk), hs] = o
                l_ref[pl.ds(r0, blk), hs] = jnp.broadcast_to(m + jnp.log(den), (blk, HEAD_DIM))
            return carry

        lax.fori_loop(0, nb, blk_body, 0)

    in_specs = [_band_spec(sl, qs, rb), _band_spec(sl, ks, rb), _band_spec(sl, vs, rb),
                pl.BlockSpec((nh, blk, kw), lambda b, r: (0, 0, 0))]
    args = [src] * 3 + [bias]
    if has_sink:
        in_specs.append(pl.BlockSpec((nh, 8, 128), lambda b, r: (0, 0, 0)))
        args.append(sink)
    return pl.pallas_call(
        body, name=name, grid=(bl, dil // rb), in_specs=in_specs,
        out_specs=[_band_spec(sl, (256, 0), rb)] * 2,
        out_shape=[_sds((bl, dil, sl, 256), F32)] * 2,
        scratch_shapes=[pltpu.VMEM((sl + 2 * rad, ks[0]), BF16), pltpu.VMEM((sl + 2 * rad, vs[0]), BF16)],
        compiler_params=_cparams(PAR, PAR),
    )(*args)


def _band_bwd(src, qs, ks, vs, bias, sink, dy, dcol, lse, delta, *, rad, nh, nkv, name):
    bl, dil, sl, _ = src.shape
    blk = bias.shape[1]
    kw = blk + 2 * rad
    nb = sl // blk
    rep = nh // nkv
    has_sink = sink is not None
    rb = _residues_per_step(dil, sl)
    wk, wv = ks[0], vs[0]

    def body(*refs):
        q_all, k_all, v_all, b_ref = refs[:4]
        s_ref = refs[4] if has_sink else None
        do_all, l_all, dl_all = refs[4 + has_sink:7 + has_sink]
        outs = refs[7 + has_sink:]
        dsk_ref = None
        if has_sink:
            dq_all, dk_all, dv_all, db_ref, dsk_ref, kp, vp, dka, dva = outs
        else:
            dq_all, dk_all, dv_all, db_ref, kp, vp, dka, dva = outs

        @pl.when((pl.program_id(0) == 0) & (pl.program_id(1) == 0))
        def _():
            db_ref[...] = jnp.zeros_like(db_ref)
            if has_sink:
                dsk_ref[...] = jnp.zeros_like(dsk_ref)

        for ri in range(rb):
            one_sequence(q_all.at[ri], k_all.at[ri], v_all.at[ri], b_ref, s_ref, do_all.at[ri], l_all.at[ri],
                         dl_all.at[ri], dq_all.at[ri], dk_all.at[ri], dv_all.at[ri], db_ref, dsk_ref, kp, vp, dka, dva)

    def one_sequence(q_ref, k_ref, v_ref, b_ref, s_ref, do_ref, l_ref, dl_ref, dq_ref, dk_ref, dv_ref, db_ref, dsk_ref,
                     kp, vp, dka, dva):
        _fill_padded(kp, k_ref, rad, sl)
        _fill_padded(vp, v_ref, rad, sl)
        dka[...] = jnp.zeros_like(dka)
        dva[...] = jnp.zeros_like(dva)

        def blk_body(i, carry):
            r0 = pl.multiple_of(i * blk, blk)
            qb = q_ref[pl.ds(r0, blk), :]
            kwin = kp[pl.ds(r0, kw), :]
            vwin = vp[pl.ds(r0, kw), :]
            dob = do_ref[pl.ds(r0, blk), :].astype(BF16)
            lb = l_ref[pl.ds(r0, blk), :]
            dlb = dl_ref[pl.ds(r0, blk), :]
            col = r0 - rad + lax.broadcasted_iota(jnp.int32, (blk, kw), 1)
            neg = jnp.where((col >= 0) & (col < sl), 0.0, NEG_INF).astype(F32)
            for h in range(nh):
                g = h // rep
                hs = slice(h * HEAD_DIM, (h + 1) * HEAD_DIM)
                gs = slice(g * HEAD_DIM, (g + 1) * HEAD_DIM)
                qh, kh, vh, doh = qb[:, hs], kwin[:, gs], vwin[:, gs], dob[:, hs]
                lh = lb[:, h * HEAD_DIM:h * HEAD_DIM + 1]
                dlh = dlb[:, h * HEAD_DIM:h * HEAD_DIM + 1]
                s = lax.dot_general(qh, kh, NT, preferred_element_type=F32) + b_ref[h] + neg
                p = jnp.exp(s - lh)
                dp = lax.dot_general(doh, vh, NT, preferred_element_type=F32)
                ds = p * (dp - dlh)
                dsb = ds.astype(BF16)
                dq_ref[pl.ds(r0, blk), hs] = jnp.dot(dsb, kh, preferred_element_type=F32)
                dka[pl.ds(r0, kw), gs] += lax.dot_general(dsb, qh, TN, preferred_element_type=F32)
                dva[pl.ds(r0, kw), gs] += lax.dot_general(p.astype(BF16), doh, TN, preferred_element_type=F32)
                db_ref[h] += ds
                if has_sink:
                    ps = jnp.exp(s_ref[h][0:1, 0:1] - lh)
                    dsk_ref[h] += jnp.broadcast_to(-jnp.sum(ps * dlh, axis=0, keepdims=True), (8, 128))
            return carry

        lax.fori_loop(0, nb, blk_body, 0)
        dk_ref[...] = dka[rad:rad + sl, :]
        dv_ref[...] = dva[rad:rad + sl, :]

    const3 = lambda b, r: (0, 0, 0)
    in_specs = [_band_spec(sl, qs, rb), _band_spec(sl, ks, rb), _band_spec(sl, vs, rb),
                pl.BlockSpec((nh, blk, kw), const3)]
    args = [src] * 3 + [bias]
    if has_sink:
        in_specs.append(pl.BlockSpec((nh, 8, 128), const3))
        args.append(sink)
    row = _band_spec(sl, (256, 0), rb)
    in_specs += [_band_spec(sl, (256, dcol), rb), row, row]
    args += [dy, lse, delta]
    out_specs = [row, _band_spec(sl, (wk, 0), rb), _band_spec(sl, (wv, 0), rb), pl.BlockSpec((nh, blk, kw), const3)]
    out_shape = [_sds((bl, dil, sl, 256), F32), _sds((bl, dil, sl, wk), F32), _sds((bl, dil, sl, wv), F32),
                 _sds((nh, blk, kw), F32)]
    if has_sink:
        out_specs.append(pl.BlockSpec((nh, 8, 128), const3))
        out_shape.append(_sds((nh, 8, 128), F32))
    return pl.pallas_call(
        body, name=name, grid=(bl, dil // rb), in_specs=in_specs, out_specs=out_specs, out_shape=out_shape,
        scratch_shapes=[pltpu.VMEM((sl + 2 * rad, wk), BF16), pltpu.VMEM((sl + 2 * rad, wv), BF16),
                        pltpu.VMEM((sl + 2 * rad, wk), F32), pltpu.VMEM((sl + 2 * rad, wv), F32)],
        compiler_params=_cparams(ARB, ARB),
    )(*args)


def _combine_a(os_, ls_, *, tm, name):
    bl = os_[1].shape[0]
    t = bl * SEQ
    nt = SEQ // tm

    def body(o1, o4, o16, l1, l4, l16, y_ref, lt_ref, scr):
        for k, (d, ref) in enumerate(((4, o4), (16, o16), (4, l4), (16, l16))):
            for r in range(d):
                _write_residue(scr, 2 * k, r, d, ref[r])
        o2, o3, b, c = (_gather_cols(scr, 2 * k, 2) for k in range(4))
        a = l1[...]
        m = jnp.maximum(jnp.maximum(a, b), c)
        ea, eb, ec = jnp.exp(a - m), jnp.exp(b - m), jnp.exp(c - m)
        den = ea + eb + ec
        y_ref[...] = (ea / den) * o1[...] + (eb / den) * o2 + (ec / den) * o3
        lt_ref[...] = m + jnp.log(den)

    specs = _residue_specs(tm, 256, nt)
    return pl.pallas_call(
        body, name=name, grid=(bl, nt), in_specs=specs * 2, out_specs=[specs[0]] * 2,
        out_shape=[_sds((t, 256), F32)] * 2, scratch_shapes=[pltpu.VMEM((8, tm, LANES), F32)],
        compiler_params=_cparams(PAR, PAR),
    )(*os_, *ls_)


def _deltas(dycat, ya, yb, yd, lse_a, *, tm, name):
    t = ya.shape[0]
    bl = t // SEQ
    nt = SEQ // tm

    def body(dy_ref, ya_ref, yb_ref, yd_ref, la_ref, dy4, dy16, l4, l16, da1, da4, da16, db_ref, dd_ref, scr):
        e = _group_sum_matrix(256, True)
        dya = dy_ref[:, 0:256]
        dla = _seg_sum(dya * ya_ref[...], e)
        da1[...] = dla
        db_ref[...] = _seg_sum(dy_ref[:, 256:512] * yb_ref[...], e)
        dd_ref[...] = _seg_sum(dy_ref[:, 768:1024] * yd_ref[...], e)
        for k, (val, r4, r16) in enumerate(((dya, dy4, dy16), (la_ref[...], l4, l16), (dla, da4, da16))):
            _scatter_cols(scr, 2 * k, val)
            for d, ref in ((4, r4), (16, r16)):
                for r in range(d):
                    ref[r] = _read_residue(scr, 2 * k, 2, r, d)

    specs = _residue_specs(tm, 256, nt)
    nat = specs[0]
    shapes = _residue_shapes(bl, 256, F32)
    outs = pl.pallas_call(
        body, name=name, grid=(bl, nt),
        in_specs=[pl.BlockSpec((tm, 1024), lambda b, i: (b * nt + i, 0)), nat, nat, nat, nat],
        out_specs=specs[1:] + specs[1:] + specs + [nat, nat],
        out_shape=shapes[1:] + shapes[1:] + shapes + [shapes[0], shapes[0]],
        scratch_shapes=[pltpu.VMEM((6, tm, LANES), F32)],
        compiler_params=_cparams(PAR, PAR),
    )(dycat, ya, yb, yd, lse_a)
    return outs[0:2], outs[2:4], outs[4:7], outs[7], outs[8]


def _dense_fwd(qd, *, tq, name):
    t = qd.shape[0]
    bl = t // SEQ
    nq = SEQ // tq

    def body(q_ref, k_ref, v_ref, o_ref, l_ref):
        q = q_ref[...]
        for g in range(2):
            h0, h1 = 2 * g, 2 * g + 1
            q2 = jnp.concatenate([q[:, h0 * 64:(h0 + 1) * 64], q[:, h1 * 64:(h1 + 1) * 64]], axis=0)
            kg = k_ref[:, g * 64:(g + 1) * 64]
            vg = v_ref[:, g * 64:(g + 1) * 64]
            s = lax.dot_general(q2, kg, NT, preferred_element_type=F32)
            m = jnp.max(s, axis=1, keepdims=True)
            p = jnp.exp(s - m)
            den = jnp.sum(p, axis=1, keepdims=True)
            o2 = jnp.dot(p.astype(BF16), vg, preferred_element_type=F32) / den
            l2 = jnp.broadcast_to(m + jnp.log(den), (2 * tq, 64))
            o_ref[:, h0 * 64:(h0 + 1) * 64] = o2[:tq]
            o_ref[:, h1 * 64:(h1 + 1) * 64] = o2[tq:]
            l_ref[:, h0 * 64:(h0 + 1) * 64] = l2[:tq]
            l_ref[:, h1 * 64:(h1 + 1) * 64] = l2[tq:]

    q3 = qd.reshape(bl, SEQ, 512)
    o, lse = pl.pallas_call(
        body, name=name, grid=(bl, nq),
        in_specs=[pl.BlockSpec((None, tq, 256), lambda b, i: (b, i, 0)),
                  pl.BlockSpec((None, SEQ, 128), lambda b, i: (b, 0, 2)),
                  pl.BlockSpec((None, SEQ, 128), lambda b, i: (b, 0, 3))],
        out_specs=[pl.BlockSpec((None, tq, 256), lambda b, i: (b, i, 0))] * 2,
        out_shape=[_sds((bl, SEQ, 256), F32)] * 2,
        compiler_params=_cparams(PAR, PAR),
    )(q3, q3, q3)
    return o.reshape(t, 256), lse.reshape(t, 256)


def _dense_bwd(qd, dycat, lse, delta, *, tq, name):
    t = qd.shape[0]
    bl = t // SEQ
    nq = SEQ // tq

    def body(q_ref, k_ref, v_ref, do_ref, l_ref, dl_ref, dq_ref, dk_ref, dv_ref, dkt, dvt):
        @pl.when(pl.program_id(1) == 0)
        def _():
            dkt[...] = jnp.zeros_like(dkt)
            dvt[...] = jnp.zeros_like(dvt)

        q = q_ref[...]
        do = do_ref[...].astype(BF16)
        lv = l_ref[...]
        dlv = dl_ref[...]
        for g in range(2):
            h0, h1 = 2 * g, 2 * g + 1
            q2 = jnp.concatenate([q[:, h0 * 64:(h0 + 1) * 64], q[:, h1 * 64:(h1 + 1) * 64]], axis=0)
            do2 = jnp.concatenate([do[:, h0 * 64:(h0 + 1) * 64], do[:, h1 * 64:(h1 + 1) * 64]], axis=0)
            l2 = jnp.concatenate([lv[:, h0 * 64:h0 * 64 + 1], lv[:, h1 * 64:h1 * 64 + 1]], axis=0)
            dl2 = jnp.concatenate([dlv[:, h0 * 64:h0 * 64 + 1], dlv[:, h1 * 64:h1 * 64 + 1]], axis=0)
            kg = k_ref[:, g * 64:(g + 1) * 64]
            vg = v_ref[:, g * 64:(g + 1) * 64]
            s = lax.dot_general(q2, kg, NT, preferred_element_type=F32)
            p = jnp.exp(s - l2)
            dp = lax.dot_general(do2, vg, NT, preferred_element_type=F32)
            ds = (p * (dp - dl2)).astype(BF16)
            dq2 = jnp.dot(ds, kg, preferred_element_type=F32)
            dq_ref[:, h0 * 64:(h0 + 1) * 64] = dq2[:tq]
            dq_ref[:, h1 * 64:(h1 + 1) * 64] = dq2[tq:]
            dkt[g * 64:(g + 1) * 64, :] += lax.dot_general(q2, ds, TN, preferred_element_type=F32)
            dvt[g * 64:(g + 1) * 64, :] += lax.dot_general(do2, p.astype(BF16), TN, preferred_element_type=F32)

        @pl.when(pl.program_id(1) == nq - 1)
        def _():
            dk_ref[...] = dkt[...].T
            dv_ref[...] = dvt[...].T

    q3 = qd.reshape(bl, SEQ, 512)
    tile = pl.BlockSpec((None, tq, 256), lambda b, i: (b, i, 0))
    full = pl.BlockSpec((None, SEQ, 128), lambda b, i: (b, 0, 0))
    dq, dk, dv = pl.pallas_call(
        body, name=name, grid=(bl, nq),
        in_specs=[tile, pl.BlockSpec((None, SEQ, 128), lambda b, i: (b, 0, 2)),
                  pl.BlockSpec((None, SEQ, 128), lambda b, i: (b, 0, 3)),
                  pl.BlockSpec((None, tq, 256), lambda b, i: (b, i, 3)), tile, tile],
        out_specs=[tile, full, full],
        out_shape=[_sds((bl, SEQ, 256), F32), _sds((bl, SEQ, 128), F32), _sds((bl, SEQ, 128), F32)],
        scratch_shapes=[pltpu.VMEM((128, SEQ), F32), pltpu.VMEM((128, SEQ), F32)],
        compiler_params=_cparams(PAR, ARB),
    )(q3, q3, q3, dycat.reshape(bl, SEQ, 1024), lse.reshape(bl, SEQ, 256), delta.reshape(bl, SEQ, 256))
    return dq.reshape(t, 256), dk.reshape(t, 128), dv.reshape(t, 128)


def _c_norm(cv, gam, bet):
    vg = _gelu(cv)
    mu = jnp.mean(vg, axis=-1, keepdims=True)
    xc = vg - mu
    r = lax.rsqrt(jnp.mean(xc * xc, axis=-1, keepdims=True) + EPS)
    xhat = xc * r
    return xhat * gam + bet, xhat, r


def _c_fwd(proj, gam, bet, ws, bst, *, tm, name):
    t = proj.shape[0]
    nch = tm // C_CHUNK

    def body(u_ref, v_ref, g_ref, b_ref, ws_ref, bs_ref, y_ref):
        vn, _, _ = _c_norm(v_ref[...], g_ref[...], b_ref[...])
        vnb = vn.astype(BF16)
        for c in range(nch):
            rows = slice(c * C_CHUNK, (c + 1) * C_CHUNK)
            for g in range(C_GROUPS):
                gs = slice(g * 64, (g + 1) * 64)
                mixed = jnp.dot(ws_ref[g], vnb[rows, gs], preferred_element_type=F32) + bs_ref[:, gs]
                y_ref[rows, gs] = _gelu(u_ref[rows, gs]) * mixed

    vec = pl.BlockSpec((1, 256), lambda i: (0, 0))
    return pl.pallas_call(
        body, name=name, grid=(t // tm,),
        in_specs=[pl.BlockSpec((tm, 256), lambda i: (i, 5)), pl.BlockSpec((tm, 256), lambda i: (i, 6)), vec, vec,
                  pl.BlockSpec((C_GROUPS, C_CHUNK, C_CHUNK), lambda i: (0, 0, 0)),
                  pl.BlockSpec((C_CHUNK, 256), lambda i: (0, 0))],
        out_specs=pl.BlockSpec((tm, 256), lambda i: (i, 0)), out_shape=_sds((t, 256), F32),
        compiler_params=_cparams(PAR),
    )(proj, proj, gam, bet, ws, bst)


def _c_bwd(proj, dycat, gam, bet, ws, wst, bst, *, tm, name):
    t = proj.shape[0]
    nch = tm // C_CHUNK
    nstep = t // tm

    def body(u_ref, v_ref, dy_ref, g_ref, b_ref, ws_ref, wst_ref, bs_ref,
             du_ref, dv_ref, dws_ref, dbs_ref, dg_ref, db_ref, dvn_s):
        step = pl.program_id(0)

        @pl.when(step == 0)
        def _():
            dws_ref[...] = jnp.zeros_like(dws_ref)
            dbs_ref[...] = jnp.zeros_like(dbs_ref)
            dg_ref[...] = jnp.zeros_like(dg_ref)
            db_ref[...] = jnp.zeros_like(db_ref)

        cv = v_ref[...]
        gam_v = g_ref[...]
        vn, xhat, r = _c_norm(cv, gam_v, b_ref[...])
        vnb = vn.astype(BF16)
        for c in range(nch):
            rows = slice(c * C_CHUNK, (c + 1) * C_CHUNK)
            for g in range(C_GROUPS):
                gs = slice(g * 64, (g + 1) * 64)
                cu = u_ref[rows, gs]
                dy = dy_ref[rows, gs]
                mixed = jnp.dot(ws_ref[g], vnb[rows, gs], preferred_element_type=F32) + bs_ref[:, gs]
                du_ref[rows, gs] = dy * mixed * _gelu_grad(cu)
                dmix = dy * _gelu(cu)
                dbs_ref[:, gs] += dmix
                dmb = dmix.astype(BF16)
                dws_ref[g] += lax.dot_general(dmb, vnb[rows, gs], NT, preferred_element_type=F32)
                dvn_s[rows, gs] = jnp.dot(wst_ref[g], dmb, preferred_element_type=F32)
        dvn = dvn_s[...]
        dg_ref[...] += jnp.sum(dvn * xhat, axis=0, keepdims=True)
        db_ref[...] += jnp.sum(dvn, axis=0, keepdims=True)
        dxh = dvn * gam_v
        dvg = r * (dxh - jnp.mean(dxh, axis=-1, keepdims=True) - xhat * jnp.mean(dxh * xhat, axis=-1, keepdims=True))
        dv_ref[...] = dvg * _gelu_grad(cv)

        @pl.when(step == nstep - 1)
        def _():
            dbs_ref[...] = _seg_sum(dbs_ref[...], _group_sum_matrix(256, True))

    vec = pl.BlockSpec((1, 256), lambda i: (0, 0))
    mat = pl.BlockSpec((C_GROUPS, C_CHUNK, C_CHUNK), lambda i: (0, 0, 0))
    bsp = pl.BlockSpec((C_CHUNK, 256), lambda i: (0, 0))
    tile = pl.BlockSpec((tm, 256), lambda i: (i, 0))
    return pl.pallas_call(
        body, name=name, grid=(nstep,),
        in_specs=[pl.BlockSpec((tm, 256), lambda i: (i, 5)), pl.BlockSpec((tm, 256), lambda i: (i, 6)),
                  pl.BlockSpec((tm, 256), lambda i: (i, 2)), vec, vec, mat, mat, bsp],
        out_specs=[tile, tile, mat, bsp, vec, vec],
        out_shape=[_sds((t, 256), F32), _sds((t, 256), F32), _sds((C_GROUPS, C_CHUNK, C_CHUNK), F32),
                   _sds((C_CHUNK, 256), F32), _sds((1, 256), F32), _sds((1, 256), F32)],
        scratch_shapes=[pltpu.VMEM((tm, 256), F32)],
        compiler_params=_cparams(ARB),
    )(proj, proj, dycat, gam, bet, ws, wst, bst)


FF_TC = 128
FF_NB = D_FF // FF_TC
FF_CH = 64
FF_HALO = 16


def _taps(ref, r0, win, where):
    z = jnp.zeros((FF_HALO, win.shape[1]), F32)
    if where == "first":
        win[0:FF_HALO, :] = z
        win[FF_HALO:, :] = ref[0:FF_CH + FF_HALO, :].astype(F32)
    elif where == "last":
        win[0:FF_CH + FF_HALO, :] = ref[SEQ - FF_CH - FF_HALO:SEQ, :].astype(F32)
        win[FF_CH + FF_HALO:, :] = z
    else:
        win[...] = ref[pl.ds(pl.multiple_of(r0 - FF_HALO, FF_HALO), FF_CH + 2 * FF_HALO), :].astype(F32)
    return tuple(win[FF_HALO + o:FF_HALO + o + FF_CH, :] for o in (-1, 0, 1))


def _chunk_loop(step):
    step(0, lambda ref, win: _taps(ref, 0, win, "first"))

    def mid(i, carry):
        r0 = pl.multiple_of(i * FF_CH, FF_CH)
        step(r0, lambda ref, win: _taps(ref, r0, win, "mid"))
        return carry

    lax.fori_loop(1, SEQ // FF_CH - 1, mid, 0)
    step(SEQ - FF_CH, lambda ref, win: _taps(ref, SEQ - FF_CH, win, "last"))


def _conv3(taps, w_ref, b_ref):
    dn, md, up = taps
    return w_ref[0:1, :] * dn + w_ref[1:2, :] * md + w_ref[2:3, :] * up + b_ref[...]


def _ff_specs(order):
    def at(fn):
        return (lambda b, j: fn(b, j)) if order == "bj" else (lambda j, b: fn(b, j))
    hs = [pl.BlockSpec((None, SEQ, FF_TC), at(lambda b, j, o=o: (b, 0, j + o))) for o in (0, FF_NB)]
    ws = [pl.BlockSpec((3, FF_TC), at(lambda b, j, o=o: (0, j + o))) for o in (0, FF_NB)]
    bs = [pl.BlockSpec((1, FF_TC), at(lambda b, j, o=o: (0, j + o))) for o in (0, FF_NB)]
    return hs, ws, bs


def _conv_gate_fwd(h, cw, cb, *, name):
    t = h.shape[0]
    bl = t // SEQ

    def body(hg_ref, hu_ref, wg_ref, wu_ref, bg_ref, bu_ref, a_ref, win):
        def step(r0, taps):
            cg = _conv3(taps(hg_ref, win.at[0]), wg_ref, bg_ref)
            cu = _conv3(taps(hu_ref, win.at[1]), wu_ref, bu_ref)
            a_ref[pl.ds(r0, FF_CH), :] = (cg * _sigmoid(cg) * cu).astype(BF16)

        _chunk_loop(step)

    hs, ws, bs = _ff_specs("bj")
    h3 = h.reshape(bl, SEQ, 2 * D_FF)
    act = pl.pallas_call(
        body, name=name, grid=(bl, FF_NB), in_specs=hs + ws + bs,
        out_specs=pl.BlockSpec((None, SEQ, FF_TC), lambda b, j: (b, 0, j)),
        out_shape=_sds((bl, SEQ, D_FF), BF16),
        scratch_shapes=[pltpu.VMEM((2, FF_CH + 2 * FF_HALO, FF_TC), F32)],
        compiler_params=_cparams(PAR, PAR),
    )(h3, h3, cw, cw, cb, cb)
    return act.reshape(t, D_FF)


def _conv_gate_bwd(h, dact, cw, cb, *, name):
    t = h.shape[0]
    bl = t // SEQ

    def body(hg_ref, hu_ref, wg_ref, wu_ref, bg_ref, bu_ref, da_ref,
             dhg_ref, dhu_ref, dwg_ref, dwu_ref, dbg_ref, dbu_ref, dg_s, du_s, win, sums):
        @pl.when(pl.program_id(1) == 0)
        def _():
            for ref in (dwg_ref, dwu_ref, dbg_ref, dbu_ref):
                ref[...] = jnp.zeros_like(ref)

        sums[...] = jnp.zeros_like(sums)
        red = lambda x: jnp.sum(x.reshape(FF_CH // 8, 8, x.shape[1]), axis=0)

        def pass1(r0, taps):
            tg, tu = taps(hg_ref, win.at[0]), taps(hu_ref, win.at[1])
            cg = _conv3(tg, wg_ref, bg_ref)
            cu = _conv3(tu, wu_ref, bu_ref)
            da = da_ref[pl.ds(r0, FF_CH), :].astype(F32)
            sg = _sigmoid(cg)
            dcg = da * cu * (sg * (1.0 + cg * (1.0 - sg)))
            dcu = da * (cg * sg)
            dg_s[pl.ds(r0, FF_CH), :] = dcg
            du_s[pl.ds(r0, FF_CH), :] = dcu
            for half, (d, tp) in enumerate(((dcg, tg), (dcu, tu))):
                for k in range(3):
                    sums[4 * half + k] += red(d * tp[k])
                sums[4 * half + 3] += red(d)

        _chunk_loop(pass1)
        for half, (dw_ref, db_ref) in enumerate(((dwg_ref, dbg_ref), (dwu_ref, dbu_ref))):
            for k in range(3):
                dw_ref[k:k + 1, :] += jnp.sum(sums[4 * half + k], axis=0, keepdims=True)
            db_ref[...] += jnp.sum(sums[4 * half + 3], axis=0, keepdims=True)

        def pass2(r0, taps):
            for k, (s, w_ref, o_ref) in enumerate(((dg_s, wg_ref, dhg_ref), (du_s, wu_ref, dhu_ref))):
                dn, md, up = taps(s, win.at[k])
                o_ref[pl.ds(r0, FF_CH), :] = (w_ref[0:1, :] * up + w_ref[1:2, :] * md + w_ref[2:3, :] * dn).astype(BF16)

        _chunk_loop(pass2)

    hs, ws, bs = _ff_specs("jb")
    half = pl.BlockSpec((None, SEQ, FF_TC), lambda j, b: (b, 0, j))
    wsp = pl.BlockSpec((3, FF_TC), lambda j, b: (0, j))
    bsp = pl.BlockSpec((1, FF_TC), lambda j, b: (0, j))
    h3 = h.reshape(bl, SEQ, 2 * D_FF)
    dhg, dhu, dwg, dwu, dbg, dbu = pl.pallas_call(
        body, name=name, grid=(FF_NB, bl), in_specs=hs + ws + bs + [half],
        out_specs=[half, half, wsp, wsp, bsp, bsp],
        out_shape=[_sds((bl, SEQ, D_FF), BF16), _sds((bl, SEQ, D_FF), BF16), _sds((3, D_FF), F32), _sds((3, D_FF), F32),
                   _sds((1, D_FF), F32), _sds((1, D_FF), F32)],
        scratch_shapes=[pltpu.VMEM((SEQ, FF_TC), F32), pltpu.VMEM((SEQ, FF_TC), F32),
                        pltpu.VMEM((2, FF_CH + 2 * FF_HALO, FF_TC), F32), pltpu.VMEM((8, 8, FF_TC), F32)],
        compiler_params=_cparams(PAR, ARB),
    )(h3, h3, cw, cw, cb, cb, dact.reshape(bl, SEQ, D_FF))
    return (dhg.reshape(t, D_FF), dhu.reshape(t, D_FF), jnp.concatenate([dwg, dwu], axis=1),
            jnp.concatenate([dbg, dbu], axis=1))


def _ple_fwd(x2, gain, wg, pe, pe_blk, wp, *, tm, name):
    t, k = x2.shape

    def body(x_ref, g_ref, wg_ref, pe_ref, wp_ref, hn_ref, x3_ref, gt_ref, pp_ref):
        x = x_ref[...]
        r = lax.rsqrt(jnp.mean(x * x, axis=-1, keepdims=True) + EPS)
        hn = (x * r * g_ref[...]).astype(BF16)
        hn_ref[...] = hn
        gate = _sigmoid(jnp.dot(hn, wg_ref[...], preferred_element_type=F32))
        pp = jnp.dot(pe_ref[...].astype(BF16), wp_ref[...], preferred_element_type=F32)
        gt_ref[...] = gate.astype(BF16)
        pp_ref[...] = pp.astype(BF16)
        x3_ref[...] = x + pp * gate

    row = pl.BlockSpec((tm, k), lambda i: (i, 0))
    return pl.pallas_call(
        body, name=name, grid=(t // tm,),
        in_specs=[row, pl.BlockSpec((1, k), lambda i: (0, 0)), pl.BlockSpec((k, k), lambda i: (0, 0)),
                  pl.BlockSpec((tm, PLE_DIM), lambda i: (pe_blk + i, 0)), pl.BlockSpec((PLE_DIM, k), lambda i: (0, 0))],
        out_specs=[row, row, row, row],
        out_shape=[_sds((t, k), BF16), _sds((t, k), F32), _sds((t, k), BF16), _sds((t, k), BF16)],
        compiler_params=_cparams(PAR),
    )(x2, gain, wg, pe, wp)


def _ple_bwd_ew(dx3, gate, pp, *, tm, name):
    t, n = dx3.shape

    def body(d_ref, g_ref, p_ref, dz_ref, dpp_ref):
        d, g = d_ref[...], g_ref[...]
        dz_ref[...] = (d * p_ref[...] * g * (1.0 - g)).astype(BF16)
        dpp_ref[...] = (d * g).astype(BF16)

    spec = pl.BlockSpec((tm, n), lambda i: (i, 0))
    return pl.pallas_call(
        body, name=name, grid=(t // tm,), in_specs=[spec] * 3, out_specs=[spec] * 2,
        out_shape=[_sds((t, n), BF16)] * 2, compiler_params=_cparams(PAR),
    )(dx3, gate, pp)


def _loss_head(y, tgt, *, tm, name):
    t, d = y.shape

    def body(y_ref, t_ref, l_ref, dy_ref):
        @pl.when(pl.program_id(0) == 0)
        def _():
            l_ref[...] = jnp.zeros_like(l_ref)

        e = y_ref[...] - t_ref[...]
        dy_ref[...] = e * (1.0 / d)
        s = jnp.sum(jnp.sum(e * e, axis=1, keepdims=True), axis=0, keepdims=True)
        l_ref[...] += jnp.broadcast_to(s * (0.5 / d), (8, 128))

    spec = pl.BlockSpec((tm, d), lambda i: (i, 0))
    return pl.pallas_call(
        body, name=name, grid=(t // tm,), in_specs=[spec, spec],
        out_specs=[pl.BlockSpec((8, 128), lambda i: (0, 0)), spec],
        out_shape=[_sds((8, 128), F32), _sds((t, d), F32)], compiler_params=_cparams(ARB),
    )(y, tgt)


BIAS_PC = 8192


def _onehot(bucket_row):
    rows = lax.broadcasted_iota(jnp.int32, (REL_BUCKETS, bucket_row.shape[1]), 0)
    return (rows == bucket_row).astype(BF16)


def _dot3(x, onehot, dims):
    acc = None
    for _ in range(3):
        term = x.astype(BF16)
        part = lax.dot_general(term, onehot, dims, preferred_element_type=F32)
        acc = part if acc is None else acc + part
        x = x - term.astype(F32)
    return acc


def _bias_lookup(table_t, bucket, *, name):
    h = table_t.shape[0]
    p = bucket.shape[1]

    def body(t_ref, b_ref, o_ref):
        bk = b_ref[...]
        val = _dot3(t_ref[...], _onehot(bk), (((1,), (0,)), ((), ())))
        o_ref[...] = jnp.where(bk >= 0, val, NEG_INF)

    return pl.pallas_call(
        body, name=name, grid=(p // BIAS_PC,),
        in_specs=[pl.BlockSpec((h, REL_BUCKETS), lambda i: (0, 0)), pl.BlockSpec((1, BIAS_PC), lambda i: (0, i))],
        out_specs=pl.BlockSpec((h, BIAS_PC), lambda i: (0, i)), out_shape=_sds((h, p), F32),
        compiler_params=_cparams(PAR),
    )(table_t, bucket)


def _bucket_reduce(dbiases, bucket, *, name):
    h, p = dbiases[0].shape
    nl = len(dbiases)

    def body(*refs):
        b_ref, o_ref = refs[nl], refs[nl + 1]

        @pl.when(pl.program_id(0) == 0)
        def _():
            o_ref[...] = jnp.zeros_like(o_ref)

        d = refs[0][...]
        for d_ref in refs[1:nl]:
            d = d + d_ref[...]
        o_ref[...] += _dot3(d, _onehot(b_ref[...]), NT)

    return pl.pallas_call(
        body, name=name, grid=(p // BIAS_PC,),
        in_specs=[pl.BlockSpec((h, BIAS_PC), lambda i: (0, i))] * nl + [pl.BlockSpec((1, BIAS_PC), lambda i: (0, i))],
        out_specs=pl.BlockSpec((h, REL_BUCKETS), lambda i: (0, 0)), out_shape=_sds((h, REL_BUCKETS), F32),
        compiler_params=_cparams(ARB),
    )(*dbiases, bucket)


def _adamw_math(w, g, m, v):
    m = ADAM_B1 * m + (1.0 - ADAM_B1) * g
    v = ADAM_B2 * v + (1.0 - ADAM_B2) * (g * g)
    m_hat = m / (1.0 - ADAM_B1 ** ADAM_STEP)
    v_hat = v / (1.0 - ADAM_B2 ** ADAM_STEP)
    delta = -ADAM_LR * (m_hat / (jnp.sqrt(v_hat) + ADAM_EPS) + ADAM_WD * w)
    return delta, m, v


def _adamw_reduce(parts, w, m, v, *, tr, name):
    nl = len(parts)
    rows, c = w.shape
    r = rows // nl
    nt = r // tr

    def body(*refs):
        p_refs = refs[:nl]
        w_ref, m_ref, v_ref, g_ref, d_ref, nm_ref, nv_ref = refs[nl:]
        for li, p_ref in enumerate(p_refs):
            @pl.when(pl.program_id(0) == li)
            def _(p_ref=p_ref):
                g = p_ref[0].astype(F32)
                for k in range(1, N_DEV):
                    g = g + p_ref[k].astype(F32)
                d, nm, nv = _adamw_math(w_ref[...], g, m_ref[...], v_ref[...])
                g_ref[...] = g
                d_ref[...] = d
                nm_ref[...] = nm
                nv_ref[...] = nv

    def part_map(li):
        return lambda l, i: (0, jnp.where(l == li, i, jnp.where(l < li, 0, nt - 1)), 0)

    spec = pl.BlockSpec((tr, c), lambda l, i: (l * nt + i, 0))
    return pl.pallas_call(
        body, name=name, grid=(nl, nt),
        in_specs=[pl.BlockSpec((N_DEV, tr, c), part_map(li)) for li in range(nl)] + [spec, spec, spec],
        out_specs=[spec] * 4, out_shape=[_sds((rows, c), F32)] * 4, compiler_params=_cparams(ARB, ARB),
    )(*parts, w, m, v)


def _adamw_plain(g, w, m, v, *, name):
    def body(g_ref, w_ref, m_ref, v_ref, d_ref, nm_ref, nv_ref):
        d, nm, nv = _adamw_math(w_ref[...], g_ref[...], m_ref[...], v_ref[...])
        d_ref[...] = d
        nm_ref[...] = nm
        nv_ref[...] = nv

    return pl.pallas_call(body, name=name, out_shape=[_sds(w.shape, F32)] * 3)(g, w, m, v)


def _mesh_pos():
    return lax.axis_index("x"), lax.axis_index("y"), lax.axis_index("c")


def _allgather_body(x_refs, out_refs, send_sems, recv_sems, local_sems, slot):
    x, y, c = _mesh_pos()
    me, sibling = (x, y, c), (x, y, 1 - c)
    chips = [(1 - x, y), (x, 1 - y), (1 - x, 1 - y)]
    waits = []
    for a, (x_ref, out_ref) in enumerate(zip(x_refs, out_refs)):
        def copy(k, block, to, src=None, out_ref=out_ref, a=a):
            return pltpu.make_async_remote_copy(
                src_ref=slot(out_ref, block) if src is None else src, dst_ref=slot(out_ref, block),
                send_sem=send_sems.at[a, k], recv_sem=recv_sems.at[a, k], device_id=to, device_id_type=MESH)

        mine = pltpu.make_async_copy(x_ref, slot(out_ref, me), local_sems.at[a])
        mine.start()
        first = [copy(0, me, sibling, src=x_ref)]
        first += [copy(1 + j, me, (*chip, c), src=x_ref) for j, chip in enumerate(chips)]
        for cp in first:
            cp.start()
        waits.append((copy, mine, first))
    sends = []
    for copy, mine, first in waits:
        passed = [copy(4 + j, (*chip, c), sibling) for j, chip in enumerate(chips)]
        for j, chip in enumerate(chips):
            copy(1 + j, (*chip, c), me).wait_recv()
            passed[j].start()
        sends.append(passed)
    for (copy, mine, first), passed in zip(waits, sends):
        copy(0, sibling, me).wait_recv()
        for j, chip in enumerate(chips):
            copy(4 + j, (*chip, 1 - c), me).wait_recv()
        for cp in first + passed:
            cp.wait_send()
        mine.wait()


PEER_FLIPS = ((0, 0, 1), (1, 0, 0), (0, 1, 0), (1, 1, 0), (1, 0, 1), (0, 1, 1), (1, 1, 1))


def _peer_copies(x_refs, land_refs, send_sem, recv_sem, scatter):
    x, y, c = _mesh_pos()
    me = 4 * x + 2 * y + c
    copies = []
    for x_ref, land_ref in zip(x_refs, land_refs):
        for fx, fy, fc in PEER_FLIPS:
            px, py, pc = x ^ fx, y ^ fy, c ^ fc
            src = x_ref.at[4 * px + 2 * py + pc] if scatter else x_ref
            copies.append(pltpu.make_async_remote_copy(
                src_ref=src, dst_ref=land_ref.at[me], send_sem=send_sem, recv_sem=recv_sem,
                device_id=(px, py, pc), device_id_type=MESH))
    return copies


def _sc_exchange(xs, *, scatter, collective_id, name):
    na = len(xs)
    land_shapes = [x.shape if scatter else (N_DEV,) + x.shape for x in xs]

    def body(*refs):
        x_refs, land_refs = refs[:na], refs[na:2 * na]
        send_sem, recv_sem, local_sem = refs[2 * na:]
        x, y, c = _mesh_pos()
        me = 4 * x + 2 * y + c
        barrier = pltpu.get_barrier_semaphore()
        for fx, fy, fc in PEER_FLIPS:
            pl.semaphore_signal(barrier, inc=1, device_id=(x ^ fx, y ^ fy, c ^ fc), device_id_type=MESH)
        pl.semaphore_wait(barrier, len(PEER_FLIPS))
        for x_ref, land_ref in zip(x_refs, land_refs):
            own = pltpu.make_async_copy(x_ref.at[me] if scatter else x_ref, land_ref.at[me], local_sem)
            own.start()
            own.wait()
        copies = _peer_copies(x_refs, land_refs, send_sem, recv_sem, scatter)
        for cp in copies:
            cp.start()
        for cp in copies:
            cp.wait()

    return pl.kernel(
        body, name=name, out_type=[_sds(s, x.dtype) for s, x in zip(land_shapes, xs)],
        mesh=plsc.ScalarSubcoreMesh(axis_name="sequencer", num_cores=1),
        scratch_types=[pltpu.SemaphoreType.DMA, pltpu.SemaphoreType.DMA, pltpu.SemaphoreType.DMA],
        compiler_params=pltpu.CompilerParams(collective_id=collective_id),
    )(*xs)


def _sc_allgather(xs, *, collective_id, name):
    na = len(xs)

    def body(*refs):
        x_refs, out_refs = refs[:na], refs[na:2 * na]
        send_sems, recv_sems, local_sems = refs[2 * na:]
        x, y, c = _mesh_pos()
        barrier = pltpu.get_barrier_semaphore()
        for fx, fy, fc in PEER_FLIPS:
            pl.semaphore_signal(barrier, inc=1, device_id=(x ^ fx, y ^ fy, c ^ fc), device_id_type=MESH)
        pl.semaphore_wait(barrier, len(PEER_FLIPS))
        _allgather_body(x_refs, out_refs, send_sems, recv_sems, local_sems,
                        lambda ref, pos: ref.at[4 * pos[0] + 2 * pos[1] + pos[2]])

    return pl.kernel(
        body, name=name, out_type=[_sds((N_DEV,) + x.shape, x.dtype) for x in xs],
        mesh=plsc.ScalarSubcoreMesh(axis_name="sequencer", num_cores=1),
        scratch_types=[pltpu.SemaphoreType.DMA((na, 7)), pltpu.SemaphoreType.DMA((na, 7)),
                       pltpu.SemaphoreType.DMA((na,))],
        compiler_params=pltpu.CompilerParams(collective_id=collective_id),
    )(*xs)


def _allgather_vmem(x, *, name):
    r, c = x.shape

    def body(x_ref, out_ref, send_sems, recv_sems, local_sems):
        _allgather_body([x_ref], [out_ref], send_sems, recv_sems, local_sems,
                        lambda ref, pos: ref.at[pl.ds((4 * pos[0] + 2 * pos[1] + pos[2]) * r, r), :])

    vm = pl.BlockSpec(memory_space=pltpu.VMEM)
    return pl.pallas_call(
        body, name=name, in_specs=[vm], out_specs=vm, out_shape=_sds((N_DEV * r, c), x.dtype),
        scratch_shapes=[pltpu.SemaphoreType.DMA((1, 7)), pltpu.SemaphoreType.DMA((1, 7)),
                        pltpu.SemaphoreType.DMA((1,))],
    )(x)


def _sum_slots(gathered, *, name):
    _, r, c = gathered.shape

    def body(g_ref, o_ref):
        acc = g_ref[0]
        for k in range(1, N_DEV):
            acc = acc + g_ref[k]
        o_ref[...] = acc

    return pl.pallas_call(body, name=name, out_shape=_sds((r, c), gathered.dtype))(gathered)


def _t5_bucket(rel):
    nb = REL_BUCKETS // 2
    ret = jnp.where(rel > 0, nb, 0)
    n = jnp.abs(rel)
    max_exact = nb // 2
    nf = jnp.maximum(n, 1).astype(F32)
    large = max_exact + (jnp.log(nf / max_exact) / math.log(REL_MAX_DIST / max_exact)
                         * (nb - max_exact)).astype(jnp.int32)
    large = jnp.minimum(large, nb - 1)
    return ret + jnp.where(n < max_exact, n, large)


def _band_pattern(block, radius, dil):
    kw = block + 2 * radius
    rel = jnp.arange(kw)[None, :] - radius - jnp.arange(block)[:, None]
    return jnp.where(jnp.abs(rel) <= radius, _t5_bucket(rel * dil), -1).astype(jnp.int32).reshape(1, block * kw)


def _rope_tables():
    lane = np.arange(64)
    seg, j = lane // 32, lane % 32
    inv = ROPE_THETA ** (-jnp.arange(0, 32, 2, dtype=F32) / 32)
    tpos = jnp.arange(SEQ)
    pos = jnp.where(jnp.asarray(seg)[None, :] == 0, (tpos // GRID_W)[:, None], (tpos % GRID_W)[:, None])
    ang = pos.astype(F32) * inv[jnp.asarray(j % 16)][None, :]
    cos = jnp.cos(ang)
    sins = jnp.where(jnp.asarray(j)[None, :] < 16, -jnp.sin(ang), jnp.sin(ang))
    return jnp.tile(cos, (1, 4)), jnp.tile(sins, (1, 4))


A_Q, A_K, A_V = (256, 0), (256, 1), (256, 2)
B_Q, B_K, B_V = (256, 0), (128, 2), (128, 3)
A_HEADS = dict(rad=A_RADIUS, nh=4, nkv=4)
B_HEADS = dict(rad=SWA_RADIUS, nh=4, nkv=2)


def _local_step(x, pe, tgt, rel_bias, wts, matmul_weights, grads_ready):
    t = x.shape[0]
    bl = t // SEQ
    cos, sins = _rope_tables()
    blocks_a = [min(BAND_BLOCK, SEQ // d) for d in DILATIONS]
    pats_a = [_band_pattern(blk, A_RADIUS, d) for blk, d in zip(blocks_a, DILATIONS)]
    pat_b = _band_pattern(BAND_BLOCK, SWA_RADIUS, 1)
    table_t = rel_bias.T
    bias_a = [_bias_lookup(table_t[:4], pt, name=f"bias_a{ci}").reshape(4, blk, blk + 2 * A_RADIUS)
              for ci, (pt, blk) in enumerate(zip(pats_a, blocks_a))]
    bias_b = _bias_lookup(table_t[4:], pat_b, name="bias_b").reshape(4, BAND_BLOCK, BAND_BLOCK + 2 * SWA_RADIUS)
    nat4 = lambda a: a.reshape(bl, 1, SEQ, a.shape[-1])

    saved = []
    for li in range(DEPTH):
        w = dict(wts[li])
        w.update(matmul_weights(li, "in", x))
        hn0, proj = _norm_mm((x,), w["g_mix"], w["w_in"], None, tm=1024, tn=1152, name="mix_in_fwd")
        qa1, qa4, qa16, qb, qd = _qkprep_fwd(proj, w["qk_gains"], cos, sins, tm=512, name="qkprep_fwd")
        qa = (nat4(qa1), qa4, qa16)
        oa, la = [], []
        for ci in range(3):
            o, l = _band_fwd(qa[ci], A_Q, A_K, A_V, bias_a[ci], None, name=f"band_a{ci}_fwd", **A_HEADS)
            oa.append(o)
            la.append(l)
        oa[0], la[0] = oa[0].reshape(t, 256), la[0].reshape(t, 256)
        ya, lse_a = _combine_a(oa, la, tm=512, name="combine_a")
        yb, lse_b = _band_fwd(nat4(qb), B_Q, B_K, B_V, bias_b, w["sink_t"], name="band_b_fwd", **B_HEADS)
        yb = yb.reshape(t, 256)
        yc = _c_fwd(proj, w["c_g"], w["c_b"], w["c_ws"], w["c_bst"], tm=512, name="c_fwd")
        yd, lse_d = _dense_fwd(qd, tq=256, name="dense_fwd")
        w.update(matmul_weights(li, "rest", yd))
        mixed, x1 = _norm_mm((ya, yb, yc, yd), w["out_gain"], w["w_out"], x, tm=1024, tn=1024, name="mix_out_fwd")
        hn1, h = _norm_mm((x1,), w["g_ffn"], w["w_up"], None, tm=1024, tn=1408, name="ffn_up_fwd", out_dtype=BF16)
        act = _conv_gate_fwd(h, w["conv_w"], w["conv_b"], name="conv_gate_fwd")
        x2 = _mm(act, w["w_down"], "nn", x1, tm=1024, tn=1024, out_dtype=F32, name="ffn_down_fwd")
        hn2, x3, gate, pp = _ple_fwd(x2, w["g_ple"], w["w_gate"], pe, li * (t // 1024), w["w_proj"], tm=1024,
                                     name="ple_fwd")
        saved.append(dict(w=w, x0=x, hn0=hn0, proj=proj, qa=qa, qb=qb, qd=qd, ya=ya, lse_a=lse_a, yb=yb, lse_b=lse_b,
                          yc=yc, yd=yd, lse_d=lse_d, mixed=mixed, x1=x1, hn1=hn1, h=h, act=act, x2=x2, hn2=hn2,
                          gate=gate, pp=pp))
        x = x3

    loss_tile, dx = _loss_head(x, tgt, tm=512, name="loss_head")
    grads = [None] * DEPTH
    dbias_a, dbias_bs = [[], [], []], []
    for li in reversed(range(DEPTH)):
        s = saved[li]
        w = s["w"]
        g = {}
        dz, dpp = _ple_bwd_ew(dx, s["gate"], s["pp"], tm=512, name="ple_bwd_ew")
        g["w_gate"] = _mm(s["hn2"], dz, "tn", None, tm=1024, tn=512, out_dtype=BF16, name="dw_gate")
        g["w_proj"] = _mm(pe, dpp, "tn", None, tm=256, tn=1024, out_dtype=BF16, name="dw_proj", a_rows=(li, t))
        dx2, dx2b, g["g_ple"] = _mm_bt_normbwd((dz,), w["w_gate"], (s["x2"],), w["g_ple"], dx, tm=1024, tn=1024,
                                               name="ple_bwd", emit_bf16=True)
        g["w_down"] = _mm(s["act"], dx2b, "tn", None, tm=1408, tn=512, out_dtype=BF16, name="dw_down")
        dact = _mm(dx2b, w["w_down"], "nt", None, tm=1024, tn=1408, out_dtype=BF16, name="ffn_down_bwd")
        dhg, dhu, g["conv_w"], g["conv_b"] = _conv_gate_bwd(s["h"], dact, w["conv_w"], w["conv_b"], name="conv_gate_bwd")
        g["w_up"] = jnp.concatenate(
            [_mm(s["hn1"], dhalf, "tn", None, tm=1024, tn=1408, out_dtype=BF16, name=f"dw_up_{nm}")
             for nm, dhalf in (("gate", dhg), ("up", dhu))], axis=1)
        dx1, dx1b, g["g_ffn"] = _mm_bt_normbwd((dhg, dhu), w["w_up"], (s["x1"],), w["g_ffn"], dx2, tm=1024, tn=1408,
                                               name="ffn_up_bwd", emit_bf16=True)
        g["w_out"] = _mm(s["mixed"], dx1b, "tn", None, tm=1024, tn=512, out_dtype=BF16, name="dw_out")
        grads_ready(li, "mid", g)
        dycat, g["out_gain"] = _mm_bt_normbwd((dx1b,), w["w_out"], (s["ya"], s["yb"], s["yc"], s["yd"]), w["out_gain"],
                                              None, tm=1024, tn=1024, name="mix_out_bwd")
        dy_r, lse_r, dl_a, dl_b, dl_d = _deltas(dycat, s["ya"], s["yb"], s["yd"], s["lse_a"], tm=512, name="deltas")
        dy_a = (nat4(dycat),) + tuple(dy_r)
        lse_a = (nat4(s["lse_a"]),) + tuple(lse_r)
        dl_a = (nat4(dl_a[0]),) + tuple(dl_a[1:])
        da = []
        for ci in range(3):
            dq, dk, dv, dbias = _band_bwd(s["qa"][ci], A_Q, A_K, A_V, bias_a[ci], None, dy_a[ci], 0, lse_a[ci],
                                          dl_a[ci], name=f"band_a{ci}_bwd", **A_HEADS)
            if ci == 0:
                dq, dk, dv = (a.reshape(t, 256) for a in (dq, dk, dv))
            da.append((dq, dk, dv))
            dbias_a[ci].append(dbias.reshape(4, -1))
        dqb, dkb, dvb, dbias_b, dsink = _band_bwd(nat4(s["qb"]), B_Q, B_K, B_V, bias_b, w["sink_t"], nat4(dycat), 1,
                                                  nat4(s["lse_b"]), nat4(dl_b), name="band_b_bwd", **B_HEADS)
        dbias_bs.append(dbias_b.reshape(4, -1))
        g["sink"] = dsink[:, 0, 0]
        dd = _dense_bwd(s["qd"], dycat, s["lse_d"], dl_d, tq=256, name="dense_bwd")
        dcu, dcv, g["c_ws"], dbs, g["c_g"], g["c_b"] = _c_bwd(s["proj"], dycat, w["c_g"], w["c_b"], w["c_ws"],
                                                               w["c_wst"], w["c_bst"], tm=512, name="c_bwd")
        g["c_bs"] = dbs[:, ::64].T
        db = (dqb.reshape(t, 256), dkb.reshape(t, 128), dvb.reshape(t, 128))
        dproj, dgains = _qkprep_bwd(s["proj"], da, db, dd, dcu, dcv, w["qk_gains"], cos, sins, tm=512, name="qkprep_bwd")
        g["qk_gain"] = dgains[:6, :64].reshape(3, 2, HEAD_DIM)
        g["w_in"] = _mm(s["hn0"], dproj, "tn", None, tm=1024, tn=1152, out_dtype=BF16, name="dw_in")
        dx, g["g_mix"] = _mm_bt_normbwd((dproj,), w["w_in"], (s["x0"],), w["g_mix"], dx1, tm=1024, tn=1152,
                                        name="mix_in_bwd")
        grads[li] = g
        grads_ready(li, "end", g)
    d_table_a = sum(_bucket_reduce(dbias_a[ci], pats_a[ci], name=f"bucket_a{ci}") for ci in range(3))
    d_table_b = _bucket_reduce(dbias_bs, pat_b, name="bucket_b")
    d_rel_bias = jnp.concatenate([d_table_a, d_table_b], axis=0).T
    return loss_tile[0, 0], dx, grads, d_rel_bias


WEIGHT_NAMES = ("rel_bias", "ln_mix_g", "w_in", "qk_gain", "sink", "c_norm_g", "c_norm_b", "c_ws", "c_bs", "out_gain",
                "w_out", "ln_ffn_g", "w_up", "conv_w", "conv_b", "w_down", "ln_ple_g", "w_ple_gate", "w_ple_proj")
COL_SHARDED = ("w_in", "w_up", "w_ple_proj")
ROW_SHARDED = ("w_out", "w_down", "w_ple_gate")
SMALL_SHARDED = ("conv_w", "out_gain")
REPLICATED = tuple(n for n in WEIGHT_NAMES if n not in COL_SHARDED + ROW_SHARDED + SMALL_SHARDED)
LOCAL_GRAD_KEY = {"ln_mix_g": "g_mix", "ln_ffn_g": "g_ffn", "ln_ple_g": "g_ple", "c_norm_g": "c_g", "c_norm_b": "c_b",
                  "w_ple_gate": "w_gate", "w_ple_proj": "w_proj"}


def _full_from_gathered(name, gathered):
    _, r, c = gathered.shape
    if name in ROW_SHARDED:
        return gathered.reshape(N_DEV * r, c)
    return jnp.transpose(gathered, (1, 0, 2)).reshape(r, N_DEV * c)


def _slots_from_full(name, full):
    rows, cols = full.shape
    if name in ROW_SHARDED:
        return full.reshape(N_DEV, rows // N_DEV, cols)
    return jnp.transpose(full.reshape(rows, N_DEV, cols // N_DEV), (1, 0, 2))


def _piece_rows(shape):
    return -(-int(np.prod(shape)) // 1024) * 8


def _pack_rows(arrays):
    pieces = []
    for a in arrays:
        n, rows = int(np.prod(a.shape)), _piece_rows(a.shape)
        flat = a.astype(F32).reshape(-1)
        if n != rows * LANES:
            flat = jnp.pad(flat, (0, rows * LANES - n))
        pieces.append(flat.reshape(rows, LANES))
    return jnp.concatenate(pieces, axis=0)


def _unpack_rows(packed, shapes):
    out, off = [], 0
    for shp in shapes:
        n, rows = int(np.prod(shp)), _piece_rows(shp)
        piece = packed[off:off + rows]
        out.append((piece if n == rows * LANES else piece.reshape(-1)[:n]).reshape(shp))
        off += rows
    return out


def kernel(x, p, rel_bias, ln_mix_g, w_in, qk_gain, sink, c_norm_g, c_norm_b, c_ws, c_bs, out_gain, w_out, ln_ffn_g, w_up, conv_w, conv_b, w_down, ln_ple_g, w_ple_gate, w_ple_proj, loss_target, m_rel_bias, m_ln_mix_g, m_w_in, m_qk_gain, m_sink, m_c_norm_g, m_c_norm_b, m_c_ws, m_c_bs, m_out_gain, m_w_out, m_ln_ffn_g, m_w_up, m_conv_w, m_conv_b, m_w_down, m_ln_ple_g, m_w_ple_gate, m_w_ple_proj, v_rel_bias, v_ln_mix_g, v_w_in, v_qk_gain, v_sink, v_c_norm_g, v_c_norm_b, v_c_ws, v_c_bs, v_out_gain, v_w_out, v_ln_ffn_g, v_w_up, v_conv_w, v_conv_b, v_w_down, v_ln_ple_g, v_w_ple_gate, v_w_ple_proj):
    env = dict(locals())
    wt = {n: env[n] for n in WEIGHT_NAMES}
    mom_m = {n: env["m_" + n] for n in WEIGHT_NAMES}
    mom_v = {n: env["v_" + n] for n in WEIGHT_NAMES}
    bl = x.shape[0]
    t = bl * SEQ
    me = 4 * lax.axis_index("x") + 2 * lax.axis_index("y") + lax.axis_index("c")

    big = COL_SHARDED + ROW_SHARDED
    full = {}
    small_shapes = [wt[n].shape for n in SMALL_SHARDED]
    small = _allgather_vmem(_pack_rows([wt[n] for n in SMALL_SHARDED]), name="gather_small")
    small = small.reshape(N_DEV, -1)
    off = 0
    for n, shp in zip(SMALL_SHARDED, small_shapes):
        cnt = int(np.prod(shp))
        g = small[:, off:off + cnt].reshape((N_DEV,) + tuple(shp))
        full[n] = jnp.transpose(g, (1, 2, 0, 3)).reshape(shp[0], shp[1], N_DEV * shp[2])
        off += _piece_rows(shp) * LANES

    def head_gain(li, a, b, reps):
        g = jnp.tile(qk_gain[li, a, b], reps)
        return jnp.pad(g, (0, 256 - g.shape[0]))

    wts = []
    for li in range(DEPTH):
        rows = [head_gain(li, 0, 0, 4), head_gain(li, 0, 1, 4), head_gain(li, 1, 0, 4), head_gain(li, 1, 1, 2),
                head_gain(li, 2, 0, 4), head_gain(li, 2, 1, 2), jnp.zeros((256,), F32), jnp.zeros((256,), F32)]
        wts.append(dict(
            g_mix=ln_mix_g[li].reshape(1, -1), qk_gains=jnp.stack(rows),
            sink_t=jnp.broadcast_to(sink[li][:, None, None], (4, 8, 128)),
            c_g=c_norm_g[li].reshape(1, -1), c_b=c_norm_b[li].reshape(1, -1), c_ws=c_ws[li].astype(BF16),
            c_wst=jnp.transpose(c_ws[li], (0, 2, 1)).astype(BF16), c_bst=jnp.repeat(c_bs[li].T, 64, axis=1),
            out_gain=full["out_gain"][li].reshape(1, -1), g_ffn=ln_ffn_g[li].reshape(1, -1),
            conv_w=full["conv_w"][li], conv_b=conv_b[li].reshape(1, -1), g_ple=ln_ple_g[li].reshape(1, -1)))

    local_key = {"w_ple_gate": "w_gate", "w_ple_proj": "w_proj"}

    gather_names = {"in": ("w_in",), "rest": tuple(n for n in big if n != "w_in")}
    gathered = {}
    for cid, (li, names) in enumerate(((0, gather_names["in"]), (0, gather_names["rest"]), (1, big))):
        lands = _sc_allgather([wt[n][li].astype(BF16) for n in names], collective_id=cid,
                              name=f"gather_{li}_{len(names)}")
        gathered.setdefault(li, {}).update(zip(names, lands))

    def matmul_weights(li, part, after):
        out = {}
        for n in gather_names[part]:
            g, _ = lax.optimization_barrier((gathered[li][n], after))
            out[local_key.get(n, n)] = _full_from_gathered(n, g)
        return out

    mid_names = ("w_ple_gate", "w_ple_proj", "w_down", "w_up", "w_out")
    end_names = ("w_in",)
    landed = {}

    def start_exchange(li, names, g, tag, cid):
        slots = [_slots_from_full(n, g[local_key.get(n, n)]) for n in names]
        lands = _sc_exchange(slots, scatter=True, collective_id=cid, name=f"grads_{li}_{tag}")
        landed.update({(n, li): land for n, land in zip(names, lands)})

    def grads_ready(li, stage, g):
        if li == 0:
            start_exchange(li, mid_names if stage == "mid" else end_names, g, stage, 5 if stage == "mid" else 6)
        elif stage == "end":
            start_exchange(li, mid_names + end_names, g, stage, 4)

    loss_part, dx, grads, d_rel_bias = _local_step(
        x.reshape(t, D_MODEL), p.reshape(DEPTH * t, PLE_DIM), loss_target.reshape(t, D_MODEL), rel_bias, wts,
        matmul_weights, grads_ready)
    loss = lax.psum(loss_part, ("x", "y", "c"))

    def local_grad(n):
        if n == "rel_bias":
            return d_rel_bias
        key = LOCAL_GRAD_KEY.get(n, n)
        return jnp.stack([grads[li][key].reshape(wt[n].shape[1:]) if n in REPLICATED else grads[li][key]
                          for li in range(DEPTH)])

    small_names = REPLICATED + SMALL_SHARDED
    small_full_shapes = [wt[n].shape if n in REPLICATED else full[n].shape for n in small_names]
    small_parts = _allgather_vmem(_pack_rows([local_grad(n) for n in small_names]), name="allgather_small_grads")
    small_parts = small_parts.reshape(N_DEV, -1, LANES)

    out_g, out_d, out_m, out_v = {}, {}, {}, {}
    for n in big:
        shp = wt[n].shape
        two_d = lambda a: a.reshape(-1, shp[-1])
        res = _adamw_reduce([landed[n, li] for li in range(DEPTH)], two_d(wt[n]), two_d(mom_m[n]), two_d(mom_v[n]),
                            tr=32 if n == "w_down" else 128, name="adamw_" + n)
        out_g[n], out_d[n], out_m[n], out_v[n] = [r.reshape(shp) for r in res]

    reduced = _sum_slots(small_parts, name="sum_small_grads")
    reduced = dict(zip(small_names, _unpack_rows(reduced, small_full_shapes)))
    rep_shapes = [wt[n].shape for n in REPLICATED]
    upd = _adamw_plain(_pack_rows([reduced[n] for n in REPLICATED]), _pack_rows([wt[n] for n in REPLICATED]),
                       _pack_rows([mom_m[n] for n in REPLICATED]), _pack_rows([mom_v[n] for n in REPLICATED]),
                       name="adamw_replicated")
    for dst, packed in zip((out_d, out_m, out_v), upd):
        dst.update(zip(REPLICATED, _unpack_rows(packed, rep_shapes)))
    for n in REPLICATED:
        out_g[n] = reduced[n]
    for n in SMALL_SHARDED:
        shp = wt[n].shape
        g = reduced[n].reshape(shp[0], shp[1], N_DEV, shp[2])
        g = lax.dynamic_index_in_dim(g, me, axis=2, keepdims=False)
        two_d = lambda a: a.reshape(-1, shp[-1])
        res = _adamw_plain(two_d(g), two_d(wt[n]), two_d(mom_m[n]), two_d(mom_v[n]), name="adamw_" + n)
        out_g[n] = g
        out_d[n], out_m[n], out_v[n] = [r.reshape(shp) for r in res]

    return (loss, dx.reshape(bl, SEQ, D_MODEL), *[out_g[n] for n in WEIGHT_NAMES], *[out_d[n] for n in WEIGHT_NAMES],
            *[out_m[n] for n in WEIGHT_NAMES], *[out_v[n] for n in WEIGHT_NAMES])
```

```python
import math

import jax
import jax.numpy as jnp
import numpy as np
from jax import lax
from jax.experimental import pallas as pl
from jax.experimental.pallas import tpu as pltpu
from jax.experimental.pallas import tpu_sc as plsc

F32 = jnp.float32
BF16 = jnp.bfloat16
HI = lax.Precision.HIGHEST

N_DEV = 8
D_MODEL = 1024
SEQ = 2048
DEPTH = 2
HEAD_DIM = 64
IN_WIDTH = 2304
D_FF = 2816
PLE_DIM = 256
C_CHUNK = 128
C_GROUPS = 4
DILATED_CFGS = ((128, 1), (512, 4), (2048, 16))
DILATIONS = tuple(d for _, d in DILATED_CFGS)
A_RADIUS = 64
SWA_RADIUS = 128
BAND_BLOCK = 256
GRID_W = 64
ROPE_THETA = 10000.0
REL_BUCKETS = 32
REL_MAX_DIST = 1024
EPS = 1e-6
NEG_INF = -1e30
ATTN_SCALE = HEAD_DIM ** -0.5
LANES = 128

ADAM_LR = 0.001
ADAM_B1 = 0.9
ADAM_B2 = 0.999
ADAM_EPS = 1e-08
ADAM_WD = 0.01
ADAM_STEP = 10

MESH = pl.DeviceIdType.MESH
NT = (((1,), (1,)), ((), ()))
TN = (((0,), (0,)), ((), ()))
ARB = "arbitrary"
PAR = "parallel"


def _cparams(*sem):
    return pltpu.CompilerParams(dimension_semantics=tuple(sem))


def _sds(shape, dtype):
    return jax.ShapeDtypeStruct(tuple(shape), dtype)


def _group_sum_matrix(n, same_group):
    r = lax.broadcasted_iota(jnp.int32, (n, n), 0)
    c = lax.broadcasted_iota(jnp.int32, (n, n), 1)
    if same_group:
        return ((r >> 6) == (c >> 6)).astype(F32)
    return ((r & 63) == (c & 63)).astype(F32)


def _seg_sum(x, e):
    eb = e.astype(BF16)
    hi = x.astype(BF16)
    lo = (x - hi.astype(F32)).astype(BF16)
    return jnp.dot(hi, eb, preferred_element_type=F32) + jnp.dot(lo, eb, preferred_element_type=F32)


def _gelu(x):
    c = math.sqrt(2.0 / math.pi)
    return 0.5 * x * (1.0 + jnp.tanh(c * (x + 0.044715 * (x * x * x))))


def _gelu_grad(x):
    c = math.sqrt(2.0 / math.pi)
    t = jnp.tanh(c * (x + 0.044715 * (x * x * x)))
    return 0.5 * (1.0 + t) + 0.5 * x * (1.0 - t * t) * c * (1.0 + 3.0 * 0.044715 * (x * x))


def _sigmoid(x):
    return 1.0 / (1.0 + jnp.exp(-x))


def _scatter_cols(scratch, first, val):
    for c in range(val.shape[1] // LANES):
        scratch[first + c] = val[:, c * LANES:(c + 1) * LANES]


def _gather_cols(scratch, first, ncol):
    return jnp.concatenate([scratch[first + c] for c in range(ncol)], axis=1)


def _read_residue(scratch, first, ncol, r, d):
    n = scratch.shape[1] // d
    return jnp.concatenate([scratch.at[first + c][pl.ds(r, n, stride=d), :] for c in range(ncol)], axis=1)


def _write_residue(scratch, first, r, d, val):
    n = scratch.shape[1] // d
    for c in range(val.shape[1] // LANES):
        scratch.at[first + c][pl.ds(r, n, stride=d), :] = val[:, c * LANES:(c + 1) * LANES]


def _norm_mm(xs, gain, w, res, *, tm, tn, name, out_dtype=F32):
    t = xs[0].shape[0]
    k = sum(x.shape[1] for x in xs)
    n = w.shape[1]
    ng = len(xs)
    has_res = res is not None

    def body(*refs):
        x_refs = refs[:ng]
        g_ref, w_ref = refs[ng], refs[ng + 1]
        res_ref = refs[ng + 2] if has_res else None
        hn_ref, o_ref, hn_s = refs[ng + 2 + has_res:]

        @pl.when(pl.program_id(1) == 0)
        def _():
            off = 0
            for xr in x_refs:
                x = xr[...]
                wd = x.shape[1]
                r = lax.rsqrt(jnp.mean(x * x, axis=-1, keepdims=True) + EPS)
                hn_s[:, off:off + wd] = (x * r * g_ref[:, off:off + wd]).astype(BF16)
                off += wd
            hn_ref[...] = hn_s[...]

        acc = jnp.dot(hn_s[...], w_ref[...], preferred_element_type=F32)
        if has_res:
            acc = acc + res_ref[...]
        o_ref[...] = acc.astype(out_dtype)

    in_specs = [pl.BlockSpec((tm, x.shape[1]), lambda i, j: (i, 0)) for x in xs]
    in_specs += [pl.BlockSpec((1, k), lambda i, j: (0, 0)), pl.BlockSpec((k, tn), lambda i, j: (0, j))]
    args = list(xs) + [gain, w]
    if has_res:
        in_specs.append(pl.BlockSpec((tm, tn), lambda i, j: (i, j)))
        args.append(res)
    return pl.pallas_call(
        body, name=name, grid=(t // tm, n // tn), in_specs=in_specs,
        out_specs=[pl.BlockSpec((tm, k), lambda i, j: (i, 0)), pl.BlockSpec((tm, tn), lambda i, j: (i, j))],
        out_shape=[_sds((t, k), BF16), _sds((t, n), out_dtype)],
        scratch_shapes=[pltpu.VMEM((tm, k), BF16)],
        compiler_params=_cparams(PAR, ARB),
    )(*args)


def _mm(a, b, mode, res, *, tm, tn, out_dtype, name, a_rows=None):
    if mode == "tn":
        kk, m = a.shape
        blk_a = 0
        if a_rows is not None:
            blk_a, kk = a_rows
        a_spec = pl.BlockSpec((kk, tm), lambda i, j: (blk_a, i))
    else:
        m, kk = a.shape
        a_spec = pl.BlockSpec((tm, kk), lambda i, j: (i, 0))
    if mode == "nt":
        n = b.shape[0]
        b_spec = pl.BlockSpec((tn, kk), lambda i, j: (j, 0))
    else:
        n = b.shape[1]
        b_spec = pl.BlockSpec((kk, tn), lambda i, j: (0, j))
    has_res = res is not None

    def body(*refs):
        a_ref, b_ref = refs[0], refs[1]
        o_ref = refs[-1]
        av = a_ref[...].astype(BF16)
        bv = b_ref[...].astype(BF16)
        if mode == "nn":
            acc = jnp.dot(av, bv, preferred_element_type=F32)
        elif mode == "nt":
            acc = lax.dot_general(av, bv, NT, preferred_element_type=F32)
        else:
            acc = lax.dot_general(av, bv, TN, preferred_element_type=F32)
        if has_res:
            acc = acc + refs[2][...]
        o_ref[...] = acc.astype(out_dtype)

    in_specs = [a_spec, b_spec]
    args = [a, b]
    if has_res:
        in_specs.append(pl.BlockSpec((tm, tn), lambda i, j: (i, j)))
        args.append(res)
    return pl.pallas_call(
        body, name=name, grid=(m // tm, n // tn), in_specs=in_specs,
        out_specs=pl.BlockSpec((tm, tn), lambda i, j: (i, j)),
        out_shape=_sds((m, n), out_dtype),
        compiler_params=_cparams(PAR, PAR),
    )(*args)


def _mm_bt_normbwd(dys, w, xs, gain, dres, *, tm, tn, name, emit_bf16=False):
    t, wd_each = dys[0].shape
    nd = len(dys)
    per = wd_each // tn
    nj = nd * per
    k = w.shape[0]
    ng = len(xs)
    has_res = dres is not None

    def body(*refs):
        dy_refs = refs[:nd]
        w_ref = refs[nd]
        x_refs = refs[nd + 1:nd + 1 + ng]
        g_ref = refs[nd + 1 + ng]
        dres_ref = refs[nd + 2 + ng] if has_res else None
        outs = refs[nd + 2 + ng + has_res:]
        dx_ref = outs[0]
        dxb_ref = outs[1] if emit_bf16 else None
        dg_ref, acc = outs[1 + emit_bf16:]
        i, j = pl.program_id(0), pl.program_id(1)

        @pl.when(j == 0)
        def _():
            acc[...] = jnp.zeros_like(acc)

        for d, dy_ref in enumerate(dy_refs):
            @pl.when((j >= d * per) & (j < (d + 1) * per))
            def _(dy_ref=dy_ref):
                acc[...] += lax.dot_general(dy_ref[...].astype(BF16), w_ref[...], NT, preferred_element_type=F32)

        @pl.when(j == nj - 1)
        def _():
            @pl.when(i == 0)
            def _():
                dg_ref[...] = jnp.zeros_like(dg_ref)

            off = 0
            for xr in x_refs:
                x = xr[...]
                wd = x.shape[1]
                g = g_ref[:, off:off + wd]
                dyn = acc[:, off:off + wd]
                r = lax.rsqrt(jnp.mean(x * x, axis=-1, keepdims=True) + EPS)
                gdy = dyn * g
                dx = r * gdy - x * (r * r * r * jnp.mean(gdy * x, axis=-1, keepdims=True))
                if has_res:
                    dx = dx + dres_ref[:, off:off + wd]
                dx_ref[:, off:off + wd] = dx
                if emit_bf16:
                    dxb_ref[:, off:off + wd] = dx.astype(BF16)
                dg_ref[:, off:off + wd] += jnp.sum(dyn * x * r, axis=0, keepdims=True)
                off += wd

    def dy_map(d):
        return lambda i, j: (i, jnp.clip(j - d * per, 0, per - 1))

    in_specs = [pl.BlockSpec((tm, tn), dy_map(d)) for d in range(nd)]
    in_specs.append(pl.BlockSpec((k, tn), lambda i, j: (0, j)))
    in_specs += [pl.BlockSpec((tm, x.shape[1]), lambda i, j: (i, 0)) for x in xs]
    in_specs.append(pl.BlockSpec((1, k), lambda i, j: (0, 0)))
    args = list(dys) + [w] + list(xs) + [gain]
    if has_res:
        in_specs.append(pl.BlockSpec((tm, k), lambda i, j: (i, 0)))
        args.append(dres)
    row = pl.BlockSpec((tm, k), lambda i, j: (i, 0))
    out_specs = [row] + ([row] if emit_bf16 else []) + [pl.BlockSpec((1, k), lambda i, j: (0, 0))]
    out_shape = [_sds((t, k), F32)] + ([_sds((t, k), BF16)] if emit_bf16 else []) + [_sds((1, k), F32)]
    return pl.pallas_call(
        body, name=name, grid=(t // tm, nj), in_specs=in_specs, out_specs=out_specs, out_shape=out_shape,
        scratch_shapes=[pltpu.VMEM((tm, k), F32)],
        compiler_params=_cparams(ARB, ARB),
    )(*args)


def _rope_partner(y):
    n = y.shape[1]
    lane = lax.broadcasted_iota(jnp.int32, y.shape, 1)
    return jnp.where((lane & 31) < 16, pltpu.roll(y, n - 16, 1), pltpu.roll(y, 16, 1))


def _residue_specs(tm, width, nt):
    specs = [pl.BlockSpec((tm, width), lambda b, i: (b * nt + i, 0))]
    for d in DILATIONS[1:]:
        specs.append(pl.BlockSpec((None, d, tm // d, width), lambda b, i: (b, 0, i, 0)))
    return specs


def _residue_shapes(bl, width, dtype):
    return [_sds((bl * SEQ, width), dtype)] + [_sds((bl, d, SEQ // d, width), dtype) for d in DILATIONS[1:]]


def _qkprep_fwd(proj, gains, cos, sins, *, tm, name):
    t = proj.shape[0]
    bl = t // SEQ
    nt = SEQ // tm

    def body(p_ref, g_ref, c_ref, s_ref, qa1_ref, qa4_ref, qa16_ref, qb_ref, qd_ref, scr):
        e = _group_sum_matrix(256, True)

        def hn(x, row):
            wd = x.shape[1]
            ms = _seg_sum(x * x, e[:wd, :wd]) * (1.0 / HEAD_DIM)
            return x * lax.rsqrt(ms + EPS) * g_ref[row:row + 1, :wd]

        qa = jnp.concatenate([hn(p_ref[:, 0:256], 0) * ATTN_SCALE, hn(p_ref[:, 256:512], 1), p_ref[:, 512:768]], axis=1)
        qa1_ref[...] = qa.astype(BF16)
        _scatter_cols(scr, 0, qa)
        for d, ref in ((4, qa4_ref), (16, qa16_ref)):
            for r in range(d):
                ref[r] = _read_residue(scr, 0, 6, r, d).astype(BF16)
        qb_ref[:, 0:256] = (hn(p_ref[:, 768:1024], 2) * ATTN_SCALE).astype(BF16)
        qb_ref[:, 256:384] = hn(p_ref[:, 1024:1152], 3).astype(BF16)
        qb_ref[:, 384:512] = p_ref[:, 1152:1280].astype(BF16)
        yq = hn(p_ref[:, 1792:2048], 4)
        yq = yq * c_ref[...] + _rope_partner(yq) * s_ref[...]
        qd_ref[:, 0:256] = (yq * ATTN_SCALE).astype(BF16)
        yk = hn(p_ref[:, 2048:2176], 5)
        yk = yk * c_ref[:, 0:128] + _rope_partner(yk) * s_ref[:, 0:128]
        qd_ref[:, 256:384] = yk.astype(BF16)
        qd_ref[:, 384:512] = p_ref[:, 2176:2304].astype(BF16)

    row = lambda width: pl.BlockSpec((tm, width), lambda b, i: (b * nt + i, 0))
    tab = pl.BlockSpec((tm, 256), lambda b, i: (i, 0))
    return pl.pallas_call(
        body, name=name, grid=(bl, nt),
        in_specs=[row(IN_WIDTH), pl.BlockSpec((8, 256), lambda b, i: (0, 0)), tab, tab],
        out_specs=_residue_specs(tm, 768, nt) + [row(512), row(512)],
        out_shape=_residue_shapes(bl, 768, BF16) + [_sds((t, 512), BF16), _sds((t, 512), BF16)],
        scratch_shapes=[pltpu.VMEM((6, tm, LANES), F32)],
        compiler_params=_cparams(PAR, PAR),
    )(proj, gains, cos, sins)


def _qkprep_bwd(proj, da, db, dd, dcu, dcv, gains, cos, sins, *, tm, name):
    t = proj.shape[0]
    bl = t // SEQ
    nt = SEQ // tm
    flat = [a for cfg in da for a in cfg] + list(db) + list(dd) + [dcu, dcv]

    def body(*refs):
        p_ref, g_ref, c_ref, s_ref = refs[:4]
        d_refs = refs[4:4 + len(flat)]
        dp_ref, dg_ref, scr = refs[4 + len(flat):]
        a_refs = d_refs[:9]
        dqb_ref, dkb_ref, dvb_ref, dqd_ref, dkd_ref, dvd_ref, dcu_ref, dcv_ref = d_refs[9:]
        e = _group_sum_matrix(256, True)
        first = (pl.program_id(0) == 0) & (pl.program_id(1) == 0)
        last = (pl.program_id(0) == bl - 1) & (pl.program_id(1) == nt - 1)

        @pl.when(first)
        def _():
            dg_ref[...] = jnp.zeros_like(dg_ref)

        def hn_bwd(x, dy, row):
            wd = x.shape[1]
            ee = e[:wd, :wd]
            g = g_ref[row:row + 1, :wd]
            r = lax.rsqrt(_seg_sum(x * x, ee) * (1.0 / HEAD_DIM) + EPS)
            gdy = dy * g
            dx = r * gdy - x * (r * r * r * (_seg_sum(gdy * x, ee) * (1.0 / HEAD_DIM)))
            dg_ref[row:row + 1, :wd] += jnp.sum(dy * x * r, axis=0, keepdims=True)
            return dx

        def rope_bwd(dy, wd):
            return dy * c_ref[:, :wd] + _rope_partner(dy * s_ref[:, :wd])

        dqkv = jnp.concatenate([a_refs[0][...], a_refs[1][...], a_refs[2][...]], axis=1)
        for ci, d in ((1, 4), (2, 16)):
            for r in range(d):
                part = jnp.concatenate([a_refs[3 * ci + m][r] for m in range(3)], axis=1)
                _write_residue(scr, 0, r, d, part)
            dqkv = dqkv + _gather_cols(scr, 0, 6)
        dp_ref[:, 0:256] = hn_bwd(p_ref[:, 0:256], dqkv[:, 0:256] * ATTN_SCALE, 0).astype(BF16)
        dp_ref[:, 256:512] = hn_bwd(p_ref[:, 256:512], dqkv[:, 256:512], 1).astype(BF16)
        dp_ref[:, 512:768] = dqkv[:, 512:768].astype(BF16)
        dp_ref[:, 768:1024] = hn_bwd(p_ref[:, 768:1024], dqb_ref[...] * ATTN_SCALE, 2).astype(BF16)
        dp_ref[:, 1024:1152] = hn_bwd(p_ref[:, 1024:1152], dkb_ref[...], 3).astype(BF16)
        dp_ref[:, 1152:1280] = dvb_ref[...].astype(BF16)
        dp_ref[:, 1280:1536] = dcu_ref[...].astype(BF16)
        dp_ref[:, 1536:1792] = dcv_ref[...].astype(BF16)
        dp_ref[:, 1792:2048] = hn_bwd(p_ref[:, 1792:2048], rope_bwd(dqd_ref[...] * ATTN_SCALE, 256), 4).astype(BF16)
        dp_ref[:, 2048:2176] = hn_bwd(p_ref[:, 2048:2176], rope_bwd(dkd_ref[...], 128), 5).astype(BF16)
        dp_ref[:, 2176:2304] = dvd_ref[...].astype(BF16)

        @pl.when(last)
        def _():
            dg_ref[...] = _seg_sum(dg_ref[...], _group_sum_matrix(256, False))

    row = lambda width: pl.BlockSpec((tm, width), lambda b, i: (b * nt + i, 0))
    tab = pl.BlockSpec((tm, 256), lambda b, i: (i, 0))
    in_specs = [row(IN_WIDTH), pl.BlockSpec((8, 256), lambda b, i: (0, 0)), tab, tab]
    res_specs = _residue_specs(tm, 256, nt)
    in_specs += [res_specs[ci] for ci in range(3) for _ in range(3)]
    in_specs += [row(a.shape[1]) for a in flat[9:]]
    return pl.pallas_call(
        body, name=name, grid=(bl, nt), in_specs=in_specs,
        out_specs=[row(IN_WIDTH), pl.BlockSpec((8, 256), lambda b, i: (0, 0))],
        out_shape=[_sds((t, IN_WIDTH), BF16), _sds((8, 256), F32)],
        scratch_shapes=[pltpu.VMEM((6, tm, LANES), F32)],
        compiler_params=_cparams(ARB, ARB),
    )(proj, gains, cos, sins, *flat)


BAND_ROWS_PER_STEP = 512


def _residues_per_step(dil, seq_len):
    return min(dil, max(1, BAND_ROWS_PER_STEP // seq_len))


def _band_spec(seq_len, spec, rb):
    width, idx = spec
    return pl.BlockSpec((None, rb, seq_len, width), lambda b, r: (b, r, 0, idx))


def _fill_padded(dst, src_ref, rad, seq_len):
    z = jnp.zeros((rad, dst.shape[1]), dst.dtype)
    dst[0:rad, :] = z
    dst[rad + seq_len:rad + seq_len + rad, :] = z
    dst[rad:rad + seq_len, :] = src_ref[...]


def _band_fwd(src, qs, ks, vs, bias, sink, *, rad, nh, nkv, name):
    bl, dil, sl, _ = src.shape
    blk = bias.shape[1]
    kw = blk + 2 * rad
    nb = sl // blk
    rep = nh // nkv
    has_sink = sink is not None
    rb = _residues_per_step(dil, sl)

    def body(*refs):
        q_all, k_all, v_all, b_ref = refs[:4]
        s_ref = refs[4] if has_sink else None
        o_all, l_all, kp, vp = refs[4 + has_sink:]
        for ri in range(rb):
            one_sequence(q_all.at[ri], k_all.at[ri], v_all.at[ri], b_ref, s_ref, o_all.at[ri], l_all.at[ri], kp, vp)

    def one_sequence(q_ref, k_ref, v_ref, b_ref, s_ref, o_ref, l_ref, kp, vp):
        _fill_padded(kp, k_ref, rad, sl)
        _fill_padded(vp, v_ref, rad, sl)

        def blk_body(i, carry):
            r0 = pl.multiple_of(i * blk, blk)
            qb = q_ref[pl.ds(r0, blk), :]
            kwin = kp[pl.ds(r0, kw), :]
            vwin = vp[pl.ds(r0, kw), :]
            col = r0 - rad + lax.broadcasted_iota(jnp.int32, (blk, kw), 1)
            neg = jnp.where((col >= 0) & (col < sl), 0.0, NEG_INF).astype(F32)
            for h in range(nh):
                g = h // rep
                hs = slice(h * HEAD_DIM, (h + 1) * HEAD_DIM)
                gs = slice(g * HEAD_DIM, (g + 1) * HEAD_DIM)
                s = lax.dot_general(qb[:, hs], kwin[:, gs], NT, preferred_element_type=F32)
                s = s + b_ref[h] + neg
                m = jnp.max(s, axis=1, keepdims=True)
                if has_sink:
                    sk = s_ref[h][0:1, 0:1]
                    m = jnp.maximum(m, sk)
                p = jnp.exp(s - m)
                den = jnp.sum(p, axis=1, keepdims=True)
                if has_sink:
                    den = den + jnp.exp(sk - m)
                o = jnp.dot(p.astype(BF16), vwin[:, gs], preferred_element_type=F32) / den
                o_ref[pl.ds(r0, blk), hs] = o
                l_ref[pl.ds(r0, blk), hs] = jnp.broadcast_to(m + jnp.log(den), (blk, HEAD_DIM))
            return carry

        lax.fori_loop(0, nb, blk_body, 0)

    in_specs = [_band_spec(sl, qs, rb), _band_spec(sl, ks, rb), _band_spec(sl, vs, rb),
                pl.BlockSpec((nh, blk, kw), lambda b, r: (0, 0, 0))]
    args = [src] * 3 + [bias]
    if has_sink:
        in_specs.append(pl.BlockSpec((nh, 8, 128), lambda b, r: (0, 0, 0)))
        args.append(sink)
    return pl.pallas_call(
        body, name=name, grid=(bl, dil // rb), in_specs=in_specs,
        out_specs=[_band_spec(sl, (256, 0), rb)] * 2,
        out_shape=[_sds((bl, dil, sl, 256), F32)] * 2,
        scratch_shapes=[pltpu.VMEM((sl + 2 * rad, ks[0]), BF16), pltpu.VMEM((sl + 2 * rad, vs[0]), BF16)],
        compiler_params=_cparams(PAR, PAR),
    )(*args)


def _band_bwd(src, qs, ks, vs, bias, sink, dy, dcol, lse, delta, *, rad, nh, nkv, name):
    bl, dil, sl, _ = src.shape
    blk = bias.shape[1]
    kw = blk + 2 * rad
    nb = sl // blk
    rep = nh // nkv
    has_sink = sink is not None
    rb = _residues_per_step(dil, sl)
    wk, wv = ks[0], vs[0]

    def body(*refs):
        q_all, k_all, v_all, b_ref = refs[:4]
        s_ref = refs[4] if has_sink else None
        do_all, l_all, dl_all = refs[4 + has_sink:7 + has_sink]
        outs = refs[7 + has_sink:]
        dsk_ref = None
        if has_sink:
            dq_all, dk_all, dv_all, db_ref, dsk_ref, kp, vp, dka, dva = outs
        else:
            dq_all, dk_all, dv_all, db_ref, kp, vp, dka, dva = outs

        @pl.when((pl.program_id(0) == 0) & (pl.program_id(1) == 0))
        def _():
            db_ref[...] = jnp.zeros_like(db_ref)
            if has_sink:
                dsk_ref[...] = jnp.zeros_like(dsk_ref)

        for ri in range(rb):
            one_sequence(q_all.at[ri], k_all.at[ri], v_all.at[ri], b_ref, s_ref, do_all.at[ri], l_all.at[ri],
                         dl_all.at[ri], dq_all.at[ri], dk_all.at[ri], dv_all.at[ri], db_ref, dsk_ref, kp, vp, dka, dva)

    def one_sequence(q_ref, k_ref, v_ref, b_ref, s_ref, do_ref, l_ref, dl_ref, dq_ref, dk_ref, dv_ref, db_ref, dsk_ref,
                     kp, vp, dka, dva):
        _fill_padded(kp, k_ref, rad, sl)
        _fill_padded(vp, v_ref, rad, sl)
        dka[...] = jnp.zeros_like(dka)
        dva[...] = jnp.zeros_like(dva)

        def blk_body(i, carry):
            r0 = pl.multiple_of(i * blk, blk)
            qb = q_ref[pl.ds(r0, blk), :]
            kwin = kp[pl.ds(r0, kw), :]
            vwin = vp[pl.ds(r0, kw), :]
            dob = do_ref[pl.ds(r0, blk), :].astype(BF16)
            lb = l_ref[pl.ds(r0, blk), :]
            dlb = dl_ref[pl.ds(r0, blk), :]
            col = r0 - rad + lax.broadcasted_iota(jnp.int32, (blk, kw), 1)
            neg = jnp.where((col >= 0) & (col < sl), 0.0, NEG_INF).astype(F32)
            for h in range(nh):
                g = h // rep
                hs = slice(h * HEAD_DIM, (h + 1) * HEAD_DIM)
                gs = slice(g * HEAD_DIM, (g + 1) * HEAD_DIM)
                qh, kh, vh, doh = qb[:, hs], kwin[:, gs], vwin[:, gs], dob[:, hs]
                lh = lb[:, h * HEAD_DIM:h * HEAD_DIM + 1]
                dlh = dlb[:, h * HEAD_DIM:h * HEAD_DIM + 1]
                s = lax.dot_general(qh, kh, NT, preferred_element_type=F32) + b_ref[h] + neg
                p = jnp.exp(s - lh)
                dp = lax.dot_general(doh, vh, NT, preferred_element_type=F32)
                ds = p * (dp - dlh)
                dsb = ds.astype(BF16)
                dq_ref[pl.ds(r0, blk), hs] = jnp.dot(dsb, kh, preferred_element_type=F32)
                dka[pl.ds(r0, kw), gs] += lax.dot_general(dsb, qh, TN, preferred_element_type=F32)
                dva[pl.ds(r0, kw), gs] += lax.dot_general(p.astype(BF16), doh, TN, preferred_element_type=F32)
                db_ref[h] += ds
                if has_sink:
                    ps = jnp.exp(s_ref[h][0:1, 0:1] - lh)
                    dsk_ref[h] += jnp.broadcast_to(-jnp.sum(ps * dlh, axis=0, keepdims=True), (8, 128))
            return carry

        lax.fori_loop(0, nb, blk_body, 0)
        dk_ref[...] = dka[rad:rad + sl, :]
        dv_ref[...] = dva[rad:rad + sl, :]

    const3 = lambda b, r: (0, 0, 0)
    in_specs = [_band_spec(sl, qs, rb), _band_spec(sl, ks, rb), _band_spec(sl, vs, rb),
                pl.BlockSpec((nh, blk, kw), const3)]
    args = [src] * 3 + [bias]
    if has_sink:
        in_specs.append(pl.BlockSpec((nh, 8, 128), const3))
        args.append(sink)
    row = _band_spec(sl, (256, 0), rb)
    in_specs += [_band_spec(sl, (256, dcol), rb), row, row]
    args += [dy, lse, delta]
    out_specs = [row, _band_spec(sl, (wk, 0), rb), _band_spec(sl, (wv, 0), rb), pl.BlockSpec((nh, blk, kw), const3)]
    out_shape = [_sds((bl, dil, sl, 256), F32), _sds((bl, dil, sl, wk), F32), _sds((bl, dil, sl, wv), F32),
                 _sds((nh, blk, kw), F32)]
    if has_sink:
        out_specs.append(pl.BlockSpec((nh, 8, 128), const3))
        out_shape.append(_sds((nh, 8, 128), F32))
    return pl.pallas_call(
        body, name=name, grid=(bl, dil // rb), in_specs=in_specs, out_specs=out_specs, out_shape=out_shape,
        scratch_shapes=[pltpu.VMEM((sl + 2 * rad, wk), BF16), pltpu.VMEM((sl + 2 * rad, wv), BF16),
                        pltpu.VMEM((sl + 2 * rad, wk), F32), pltpu.VMEM((sl + 2 * rad, wv), F32)],
        compiler_params=_cparams(ARB, ARB),
    )(*args)


def _combine_a(os_, ls_, *, tm, name):
    bl = os_[1].shape[0]
    t = bl * SEQ
    nt = SEQ // tm

    def body(o1, o4, o16, l1, l4, l16, y_ref, lt_ref, scr):
        for k, (d, ref) in enumerate(((4, o4), (16, o16), (4, l4), (16, l16))):
            for r in range(d):
                _write_residue(scr, 2 * k, r, d, ref[r])
        o2, o3, b, c = (_gather_cols(scr, 2 * k, 2) for k in range(4))
        a = l1[...]
        m = jnp.maximum(jnp.maximum(a, b), c)
        ea, eb, ec = jnp.exp(a - m), jnp.exp(b - m), jnp.exp(c - m)
        den = ea + eb + ec
        y_ref[...] = (ea / den) * o1[...] + (eb / den) * o2 + (ec / den) * o3
        lt_ref[...] = m + jnp.log(den)

    specs = _residue_specs(tm, 256, nt)
    return pl.pallas_call(
        body, name=name, grid=(bl, nt), in_specs=specs * 2, out_specs=[specs[0]] * 2,
        out_shape=[_sds((t, 256), F32)] * 2, scratch_shapes=[pltpu.VMEM((8, tm, LANES), F32)],
        compiler_params=_cparams(PAR, PAR),
    )(*os_, *ls_)


def _deltas(dycat, ya, yb, yd, lse_a, *, tm, name):
    t = ya.shape[0]
    bl = t // SEQ
    nt = SEQ // tm

    def body(dy_ref, ya_ref, yb_ref, yd_ref, la_ref, dy4, dy16, l4, l16, da1, da4, da16, db_ref, dd_ref, scr):
        e = _group_sum_matrix(256, True)
        dya = dy_ref[:, 0:256]
        dla = _seg_sum(dya * ya_ref[...], e)
        da1[...] = dla
        db_ref[...] = _seg_sum(dy_ref[:, 256:512] * yb_ref[...], e)
        dd_ref[...] = _seg_sum(dy_ref[:, 768:1024] * yd_ref[...], e)
        for k, (val, r4, r16) in enumerate(((dya, dy4, dy16), (la_ref[...], l4, l16), (dla, da4, da16))):
            _scatter_cols(scr, 2 * k, val)
            for d, ref in ((4, r4), (16, r16)):
                for r in range(d):
                    ref[r] = _read_residue(scr, 2 * k, 2, r, d)

    specs = _residue_specs(tm, 256, nt)
    nat = specs[0]
    shapes = _residue_shapes(bl, 256, F32)
    outs = pl.pallas_call(
        body, name=name, grid=(bl, nt),
        in_specs=[pl.BlockSpec((tm, 1024), lambda b, i: (b * nt + i, 0)), nat, nat, nat, nat],
        out_specs=specs[1:] + specs[1:] + specs + [nat, nat],
        out_shape=shapes[1:] + shapes[1:] + shapes + [shapes[0], shapes[0]],
        scratch_shapes=[pltpu.VMEM((6, tm, LANES), F32)],
        compiler_params=_cparams(PAR, PAR),
    )(dycat, ya, yb, yd, lse_a)
    return outs[0:2], outs[2:4], outs[4:7], outs[7], outs[8]


def _dense_fwd(qd, *, tq, name):
    t = qd.shape[0]
    bl = t // SEQ
    nq = SEQ // tq

    def body(q_ref, k_ref, v_ref, o_ref, l_ref):
        q = q_ref[...]
        for g in range(2):
            h0, h1 = 2 * g, 2 * g + 1
            q2 = jnp.concatenate([q[:, h0 * 64:(h0 + 1) * 64], q[:, h1 * 64:(h1 + 1) * 64]], axis=0)
            kg = k_ref[:, g * 64:(g + 1) * 64]
            vg = v_ref[:, g * 64:(g + 1) * 64]
            s = lax.dot_general(q2, kg, NT, preferred_element_type=F32)
            m = jnp.max(s, axis=1, keepdims=True)
            p = jnp.exp(s - m)
            den = jnp.sum(p, axis=1, keepdims=True)
            o2 = jnp.dot(p.astype(BF16), vg, preferred_element_type=F32) / den
            l2 = jnp.broadcast_to(m + jnp.log(den), (2 * tq, 64))
            o_ref[:, h0 * 64:(h0 + 1) * 64] = o2[:tq]
            o_ref[:, h1 * 64:(h1 + 1) * 64] = o2[tq:]
            l_ref[:, h0 * 64:(h0 + 1) * 64] = l2[:tq]
            l_ref[:, h1 * 64:(h1 + 1) * 64] = l2[tq:]

    q3 = qd.reshape(bl, SEQ, 512)
    o, lse = pl.pallas_call(
        body, name=name, grid=(bl, nq),
        in_specs=[pl.BlockSpec((None, tq, 256), lambda b, i: (b, i, 0)),
                  pl.BlockSpec((None, SEQ, 128), lambda b, i: (b, 0, 2)),
                  pl.BlockSpec((None, SEQ, 128), lambda b, i: (b, 0, 3))],
        out_specs=[pl.BlockSpec((None, tq, 256), lambda b, i: (b, i, 0))] * 2,
        out_shape=[_sds((bl, SEQ, 256), F32)] * 2,
        compiler_params=_cparams(PAR, PAR),
    )(q3, q3, q3)
    return o.reshape(t, 256), lse.reshape(t, 256)


def _dense_bwd(qd, dycat, lse, delta, *, tq, name):
    t = qd.shape[0]
    bl = t // SEQ
    nq = SEQ // tq

    def body(q_ref, k_ref, v_ref, do_ref, l_ref, dl_ref, dq_ref, dk_ref, dv_ref, dkt, dvt):
        @pl.when(pl.program_id(1) == 0)
        def _():
            dkt[...] = jnp.zeros_like(dkt)
            dvt[...] = jnp.zeros_like(dvt)

        q = q_ref[...]
        do = do_ref[...].astype(BF16)
        lv = l_ref[...]
        dlv = dl_ref[...]
        for g in range(2):
            h0, h1 = 2 * g, 2 * g + 1
            q2 = jnp.concatenate([q[:, h0 * 64:(h0 + 1) * 64], q[:, h1 * 64:(h1 + 1) * 64]], axis=0)
            do2 = jnp.concatenate([do[:, h0 * 64:(h0 + 1) * 64], do[:, h1 * 64:(h1 + 1) * 64]], axis=0)
            l2 = jnp.concatenate([lv[:, h0 * 64:h0 * 64 + 1], lv[:, h1 * 64:h1 * 64 + 1]], axis=0)
            dl2 = jnp.concatenate([dlv[:, h0 * 64:h0 * 64 + 1], dlv[:, h1 * 64:h1 * 64 + 1]], axis=0)
            kg = k_ref[:, g * 64:(g + 1) * 64]
            vg = v_ref[:, g * 64:(g + 1) * 64]
            s = lax.dot_general(q2, kg, NT, preferred_element_type=F32)
            p = jnp.exp(s - l2)
            dp = lax.dot_general(do2, vg, NT, preferred_element_type=F32)
            ds = (p * (dp - dl2)).astype(BF16)
            dq2 = jnp.dot(ds, kg, preferred_element_type=F32)
            dq_ref[:, h0 * 64:(h0 + 1) * 64] = dq2[:tq]
            dq_ref[:, h1 * 64:(h1 + 1) * 64] = dq2[tq:]
            dkt[g * 64:(g + 1) * 64, :] += lax.dot_general(q2, ds, TN, preferred_element_type=F32)
            dvt[g * 64:(g + 1) * 64, :] += lax.dot_general(do2, p.astype(BF16), TN, preferred_element_type=F32)

        @pl.when(pl.program_id(1) == nq - 1)
        def _():
            dk_ref[...] = dkt[...].T
            dv_ref[...] = dvt[...].T

    q3 = qd.reshape(bl, SEQ, 512)
    tile = pl.BlockSpec((None, tq, 256), lambda b, i: (b, i, 0))
    full = pl.BlockSpec((None, SEQ, 128), lambda b, i: (b, 0, 0))
    dq, dk, dv = pl.pallas_call(
        body, name=name, grid=(bl, nq),
        in_specs=[tile, pl.BlockSpec((None, SEQ, 128), lambda b, i: (b, 0, 2)),
                  pl.BlockSpec((None, SEQ, 128), lambda b, i: (b, 0, 3)),
                  pl.BlockSpec((None, tq, 256), lambda b, i: (b, i, 3)), tile, tile],
        out_specs=[tile, full, full],
        out_shape=[_sds((bl, SEQ, 256), F32), _sds((bl, SEQ, 128), F32), _sds((bl, SEQ, 128), F32)],
        scratch_shapes=[pltpu.VMEM((128, SEQ), F32), pltpu.VMEM((128, SEQ), F32)],
        compiler_params=_cparams(PAR, ARB),
    )(q3, q3, q3, dycat.reshape(bl, SEQ, 1024), lse.reshape(bl, SEQ, 256), delta.reshape(bl, SEQ, 256))
    return dq.reshape(t, 256), dk.reshape(t, 128), dv.reshape(t, 128)


def _c_norm(cv, gam, bet):
    vg = _gelu(cv)
    mu = jnp.mean(vg, axis=-1, keepdims=True)
    xc = vg - mu
    r = lax.rsqrt(jnp.mean(xc * xc, axis=-1, keepdims=True) + EPS)
    xhat = xc * r
    return xhat * gam + bet, xhat, r


def _c_fwd(proj, gam, bet, ws, bst, *, tm, name):
    t = proj.shape[0]
    nch = tm // C_CHUNK

    def body(u_ref, v_ref, g_ref, b_ref, ws_ref, bs_ref, y_ref):
        vn, _, _ = _c_norm(v_ref[...], g_ref[...], b_ref[...])
        vnb = vn.astype(BF16)
        for c in range(nch):
            rows = slice(c * C_CHUNK, (c + 1) * C_CHUNK)
            for g in range(C_GROUPS):
                gs = slice(g * 64, (g + 1) * 64)
                mixed = jnp.dot(ws_ref[g], vnb[rows, gs], preferred_element_type=F32) + bs_ref[:, gs]
                y_ref[rows, gs] = _gelu(u_ref[rows, gs]) * mixed

    vec = pl.BlockSpec((1, 256), lambda i: (0, 0))
    return pl.pallas_call(
        body, name=name, grid=(t // tm,),
        in_specs=[pl.BlockSpec((tm, 256), lambda i: (i, 5)), pl.BlockSpec((tm, 256), lambda i: (i, 6)), vec, vec,
                  pl.BlockSpec((C_GROUPS, C_CHUNK, C_CHUNK), lambda i: (0, 0, 0)),
                  pl.BlockSpec((C_CHUNK, 256), lambda i: (0, 0))],
        out_specs=pl.BlockSpec((tm, 256), lambda i: (i, 0)), out_shape=_sds((t, 256), F32),
        compiler_params=_cparams(PAR),
    )(proj, proj, gam, bet, ws, bst)


def _c_bwd(proj, dycat, gam, bet, ws, wst, bst, *, tm, name):
    t = proj.shape[0]
    nch = tm // C_CHUNK
    nstep = t // tm

    def body(u_ref, v_ref, dy_ref, g_ref, b_ref, ws_ref, wst_ref, bs_ref,
             du_ref, dv_ref, dws_ref, dbs_ref, dg_ref, db_ref, dvn_s):
        step = pl.program_id(0)

        @pl.when(step == 0)
        def _():
            dws_ref[...] = jnp.zeros_like(dws_ref)
            dbs_ref[...] = jnp.zeros_like(dbs_ref)
            dg_ref[...] = jnp.zeros_like(dg_ref)
            db_ref[...] = jnp.zeros_like(db_ref)

        cv = v_ref[...]
        gam_v = g_ref[...]
        vn, xhat, r = _c_norm(cv, gam_v, b_ref[...])
        vnb = vn.astype(BF16)
        for c in range(nch):
            rows = slice(c * C_CHUNK, (c + 1) * C_CHUNK)
            for g in range(C_GROUPS):
                gs = slice(g * 64, (g + 1) * 64)
                cu = u_ref[rows, gs]
                dy = dy_ref[rows, gs]
                mixed = jnp.dot(ws_ref[g], vnb[rows, gs], preferred_element_type=F32) + bs_ref[:, gs]
                du_ref[rows, gs] = dy * mixed * _gelu_grad(cu)
                dmix = dy * _gelu(cu)
                dbs_ref[:, gs] += dmix
                dmb = dmix.astype(BF16)
                dws_ref[g] += lax.dot_general(dmb, vnb[rows, gs], NT, preferred_element_type=F32)
                dvn_s[rows, gs] = jnp.dot(wst_ref[g], dmb, preferred_element_type=F32)
        dvn = dvn_s[...]
        dg_ref[...] += jnp.sum(dvn * xhat, axis=0, keepdims=True)
        db_ref[...] += jnp.sum(dvn, axis=0, keepdims=True)
        dxh = dvn * gam_v
        dvg = r * (dxh - jnp.mean(dxh, axis=-1, keepdims=True) - xhat * jnp.mean(dxh * xhat, axis=-1, keepdims=True))
        dv_ref[...] = dvg * _gelu_grad(cv)

        @pl.when(step == nstep - 1)
        def _():
            dbs_ref[...] = _seg_sum(dbs_ref[...], _group_sum_matrix(256, True))

    vec = pl.BlockSpec((1, 256), lambda i: (0, 0))
    mat = pl.BlockSpec((C_GROUPS, C_CHUNK, C_CHUNK), lambda i: (0, 0, 0))
    bsp = pl.BlockSpec((C_CHUNK, 256), lambda i: (0, 0))
    tile = pl.BlockSpec((tm, 256), lambda i: (i, 0))
    return pl.pallas_call(
        body, name=name, grid=(nstep,),
        in_specs=[pl.BlockSpec((tm, 256), lambda i: (i, 5)), pl.BlockSpec((tm, 256), lambda i: (i, 6)),
                  pl.BlockSpec((tm, 256), lambda i: (i, 2)), vec, vec, mat, mat, bsp],
        out_specs=[tile, tile, mat, bsp, vec, vec],
        out_shape=[_sds((t, 256), F32), _sds((t, 256), F32), _sds((C_GROUPS, C_CHUNK, C_CHUNK), F32),
                   _sds((C_CHUNK, 256), F32), _sds((1, 256), F32), _sds((1, 256), F32)],
        scratch_shapes=[pltpu.VMEM((tm, 256), F32)],
        compiler_params=_cparams(ARB),
    )(proj, proj, dycat, gam, bet, ws, wst, bst)


FF_TC = 128
FF_NB = D_FF // FF_TC
FF_CH = 64
FF_HALO = 16


def _taps(ref, r0, win, where):
    z = jnp.zeros((FF_HALO, win.shape[1]), F32)
    if where == "first":
        win[0:FF_HALO, :] = z
        win[FF_HALO:, :] = ref[0:FF_CH + FF_HALO, :].astype(F32)
    elif where == "last":
        win[0:FF_CH + FF_HALO, :] = ref[SEQ - FF_CH - FF_HALO:SEQ, :].astype(F32)
        win[FF_CH + FF_HALO:, :] = z
    else:
        win[...] = ref[pl.ds(pl.multiple_of(r0 - FF_HALO, FF_HALO), FF_CH + 2 * FF_HALO), :].astype(F32)
    return tuple(win[FF_HALO + o:FF_HALO + o + FF_CH, :] for o in (-1, 0, 1))


def _chunk_loop(step):
    step(0, lambda ref, win: _taps(ref, 0, win, "first"))

    def mid(i, carry):
        r0 = pl.multiple_of(i * FF_CH, FF_CH)
        step(r0, lambda ref, win: _taps(ref, r0, win, "mid"))
        return carry

    lax.fori_loop(1, SEQ // FF_CH - 1, mid, 0)
    step(SEQ - FF_CH, lambda ref, win: _taps(ref, SEQ - FF_CH, win, "last"))


def _conv3(taps, w_ref, b_ref):
    dn, md, up = taps
    return w_ref[0:1, :] * dn + w_ref[1:2, :] * md + w_ref[2:3, :] * up + b_ref[...]


def _ff_specs(order):
    def at(fn):
        return (lambda b, j: fn(b, j)) if order == "bj" else (lambda j, b: fn(b, j))
    hs = [pl.BlockSpec((None, SEQ, FF_TC), at(lambda b, j, o=o: (b, 0, j + o))) for o in (0, FF_NB)]
    ws = [pl.BlockSpec((3, FF_TC), at(lambda b, j, o=o: (0, j + o))) for o in (0, FF_NB)]
    bs = [pl.BlockSpec((1, FF_TC), at(lambda b, j, o=o: (0, j + o))) for o in (0, FF_NB)]
    return hs, ws, bs


def _conv_gate_fwd(h, cw, cb, *, name):
    t = h.shape[0]
    bl = t // SEQ

    def body(hg_ref, hu_ref, wg_ref, wu_ref, bg_ref, bu_ref, a_ref, win):
        def step(r0, taps):
            cg = _conv3(taps(hg_ref, win.at[0]), wg_ref, bg_ref)
            cu = _conv3(taps(hu_ref, win.at[1]), wu_ref, bu_ref)
            a_ref[pl.ds(r0, FF_CH), :] = (cg * _sigmoid(cg) * cu).astype(BF16)

        _chunk_loop(step)

    hs, ws, bs = _ff_specs("bj")
    h3 = h.reshape(bl, SEQ, 2 * D_FF)
    act = pl.pallas_call(
        body, name=name, grid=(bl, FF_NB), in_specs=hs + ws + bs,
        out_specs=pl.BlockSpec((None, SEQ, FF_TC), lambda b, j: (b, 0, j)),
        out_shape=_sds((bl, SEQ, D_FF), BF16),
        scratch_shapes=[pltpu.VMEM((2, FF_CH + 2 * FF_HALO, FF_TC), F32)],
        compiler_params=_cparams(PAR, PAR),
    )(h3, h3, cw, cw, cb, cb)
    return act.reshape(t, D_FF)


def _conv_gate_bwd(h, dact, cw, cb, *, name):
    t = h.shape[0]
    bl = t // SEQ

    def body(hg_ref, hu_ref, wg_ref, wu_ref, bg_ref, bu_ref, da_ref,
             dhg_ref, dhu_ref, dwg_ref, dwu_ref, dbg_ref, dbu_ref, dg_s, du_s, win, sums):
        @pl.when(pl.program_id(1) == 0)
        def _():
            for ref in (dwg_ref, dwu_ref, dbg_ref, dbu_ref):
                ref[...] = jnp.zeros_like(ref)

        sums[...] = jnp.zeros_like(sums)
        red = lambda x: jnp.sum(x.reshape(FF_CH // 8, 8, x.shape[1]), axis=0)

        def pass1(r0, taps):
            tg, tu = taps(hg_ref, win.at[0]), taps(hu_ref, win.at[1])
            cg = _conv3(tg, wg_ref, bg_ref)
            cu = _conv3(tu, wu_ref, bu_ref)
            da = da_ref[pl.ds(r0, FF_CH), :].astype(F32)
            sg = _sigmoid(cg)
            dcg = da * cu * (sg * (1.0 + cg * (1.0 - sg)))
            dcu = da * (cg * sg)
            dg_s[pl.ds(r0, FF_CH), :] = dcg
            du_s[pl.ds(r0, FF_CH), :] = dcu
            for half, (d, tp) in enumerate(((dcg, tg), (dcu, tu))):
                for k in range(3):
                    sums[4 * half + k] += red(d * tp[k])
                sums[4 * half + 3] += red(d)

        _chunk_loop(pass1)
        for half, (dw_ref, db_ref) in enumerate(((dwg_ref, dbg_ref), (dwu_ref, dbu_ref))):
            for k in range(3):
                dw_ref[k:k + 1, :] += jnp.sum(sums[4 * half + k], axis=0, keepdims=True)
            db_ref[...] += jnp.sum(sums[4 * half + 3], axis=0, keepdims=True)

        def pass2(r0, taps):
            for k, (s, w_ref, o_ref) in enumerate(((dg_s, wg_ref, dhg_ref), (du_s, wu_ref, dhu_ref))):
                dn, md, up = taps(s, win.at[k])
                o_ref[pl.ds(r0, FF_CH), :] = (w_ref[0:1, :] * up + w_ref[1:2, :] * md + w_ref[2:3, :] * dn).astype(BF16)

        _chunk_loop(pass2)

    hs, ws, bs = _ff_specs("jb")
    half = pl.BlockSpec((None, SEQ, FF_TC), lambda j, b: (b, 0, j))
    wsp = pl.BlockSpec((3, FF_TC), lambda j, b: (0, j))
    bsp = pl.BlockSpec((1, FF_TC), lambda j, b: (0, j))
    h3 = h.reshape(bl, SEQ, 2 * D_FF)
    dhg, dhu, dwg, dwu, dbg, dbu = pl.pallas_call(
        body, name=name, grid=(FF_NB, bl), in_specs=hs + ws + bs + [half],
        out_specs=[half, half, wsp, wsp, bsp, bsp],
        out_shape=[_sds((bl, SEQ, D_FF), BF16), _sds((bl, SEQ, D_FF), BF16), _sds((3, D_FF), F32), _sds((3, D_FF), F32),
                   _sds((1, D_FF), F32), _sds((1, D_FF), F32)],
        scratch_shapes=[pltpu.VMEM((SEQ, FF_TC), F32), pltpu.VMEM((SEQ, FF_TC), F32),
                        pltpu.VMEM((2, FF_CH + 2 * FF_HALO, FF_TC), F32), pltpu.VMEM((8, 8, FF_TC), F32)],
        compiler_params=_cparams(PAR, ARB),
    )(h3, h3, cw, cw, cb, cb, dact.reshape(bl, SEQ, D_FF))
    return (dhg.reshape(t, D_FF), dhu.reshape(t, D_FF), jnp.concatenate([dwg, dwu], axis=1),
            jnp.concatenate([dbg, dbu], axis=1))


def _ple_fwd(x2, gain, wg, pe, pe_blk, wp, *, tm, name):
    t, k = x2.shape

    def body(x_ref, g_ref, wg_ref, pe_ref, wp_ref, hn_ref, x3_ref, gt_ref, pp_ref):
        x = x_ref[...]
        r = lax.rsqrt(jnp.mean(x * x, axis=-1, keepdims=True) + EPS)
        hn = (x * r * g_ref[...]).astype(BF16)
        hn_ref[...] = hn
        gate = _sigmoid(jnp.dot(hn, wg_ref[...], preferred_element_type=F32))
        pp = jnp.dot(pe_ref[...].astype(BF16), wp_ref[...], preferred_element_type=F32)
        gt_ref[...] = gate.astype(BF16)
        pp_ref[...] = pp.astype(BF16)
        x3_ref[...] = x + pp * gate

    row = pl.BlockSpec((tm, k), lambda i: (i, 0))
    return pl.pallas_call(
        body, name=name, grid=(t // tm,),
        in_specs=[row, pl.BlockSpec((1, k), lambda i: (0, 0)), pl.BlockSpec((k, k), lambda i: (0, 0)),
                  pl.BlockSpec((tm, PLE_DIM), lambda i: (pe_blk + i, 0)), pl.BlockSpec((PLE_DIM, k), lambda i: (0, 0))],
        out_specs=[row, row, row, row],
        out_shape=[_sds((t, k), BF16), _sds((t, k), F32), _sds((t, k), BF16), _sds((t, k), BF16)],
        compiler_params=_cparams(PAR),
    )(x2, gain, wg, pe, wp)


def _ple_bwd_ew(dx3, gate, pp, *, tm, name):
    t, n = dx3.shape

    def body(d_ref, g_ref, p_ref, dz_ref, dpp_ref):
        d, g = d_ref[...], g_ref[...]
        dz_ref[...] = (d * p_ref[...] * g * (1.0 - g)).astype(BF16)
        dpp_ref[...] = (d * g).astype(BF16)

    spec = pl.BlockSpec((tm, n), lambda i: (i, 0))
    return pl.pallas_call(
        body, name=name, grid=(t // tm,), in_specs=[spec] * 3, out_specs=[spec] * 2,
        out_shape=[_sds((t, n), BF16)] * 2, compiler_params=_cparams(PAR),
    )(dx3, gate, pp)


def _loss_head(y, tgt, *, tm, name):
    t, d = y.shape

    def body(y_ref, t_ref, l_ref, dy_ref):
        @pl.when(pl.program_id(0) == 0)
        def _():
            l_ref[...] = jnp.zeros_like(l_ref)

        e = y_ref[...] - t_ref[...]
        dy_ref[...] = e * (1.0 / d)
        s = jnp.sum(jnp.sum(e * e, axis=1, keepdims=True), axis=0, keepdims=True)
        l_ref[...] += jnp.broadcast_to(s * (0.5 / d), (8, 128))

    spec = pl.BlockSpec((tm, d), lambda i: (i, 0))
    return pl.pallas_call(
        body, name=name, grid=(t // tm,), in_specs=[spec, spec],
        out_specs=[pl.BlockSpec((8, 128), lambda i: (0, 0)), spec],
        out_shape=[_sds((8, 128), F32), _sds((t, d), F32)], compiler_params=_cparams(ARB),
    )(y, tgt)


BIAS_PC = 8192


def _onehot(bucket_row):
    rows = lax.broadcasted_iota(jnp.int32, (REL_BUCKETS, bucket_row.shape[1]), 0)
    return (rows == bucket_row).astype(BF16)


def _dot3(x, onehot, dims):
    acc = None
    for _ in range(3):
        term = x.astype(BF16)
        part = lax.dot_general(term, onehot, dims, preferred_element_type=F32)
        acc = part if acc is None else acc + part
        x = x - term.astype(F32)
    return acc


def _bias_lookup(table_t, bucket, *, name):
    h = table_t.shape[0]
    p = bucket.shape[1]

    def body(t_ref, b_ref, o_ref):
        bk = b_ref[...]
        val = _dot3(t_ref[...], _onehot(bk), (((1,), (0,)), ((), ())))
        o_ref[...] = jnp.where(bk >= 0, val, NEG_INF)

    return pl.pallas_call(
        body, name=name, grid=(p // BIAS_PC,),
        in_specs=[pl.BlockSpec((h, REL_BUCKETS), lambda i: (0, 0)), pl.BlockSpec((1, BIAS_PC), lambda i: (0, i))],
        out_specs=pl.BlockSpec((h, BIAS_PC), lambda i: (0, i)), out_shape=_sds((h, p), F32),
        compiler_params=_cparams(PAR),
    )(table_t, bucket)


def _bucket_reduce(dbiases, bucket, *, name):
    h, p = dbiases[0].shape
    nl = len(dbiases)

    def body(*refs):
        b_ref, o_ref = refs[nl], refs[nl + 1]

        @pl.when(pl.program_id(0) == 0)
        def _():
            o_ref[...] = jnp.zeros_like(o_ref)

        d = refs[0][...]
        for d_ref in refs[1:nl]:
            d = d + d_ref[...]
        o_ref[...] += _dot3(d, _onehot(b_ref[...]), NT)

    return pl.pallas_call(
        body, name=name, grid=(p // BIAS_PC,),
        in_specs=[pl.BlockSpec((h, BIAS_PC), lambda i: (0, i))] * nl + [pl.BlockSpec((1, BIAS_PC), lambda i: (0, i))],
        out_specs=pl.BlockSpec((h, REL_BUCKETS), lambda i: (0, 0)), out_shape=_sds((h, REL_BUCKETS), F32),
        compiler_params=_cparams(ARB),
    )(*dbiases, bucket)


def _adamw_math(w, g, m, v):
    m = ADAM_B1 * m + (1.0 - ADAM_B1) * g
    v = ADAM_B2 * v + (1.0 - ADAM_B2) * (g * g)
    m_hat = m / (1.0 - ADAM_B1 ** ADAM_STEP)
    v_hat = v / (1.0 - ADAM_B2 ** ADAM_STEP)
    delta = -ADAM_LR * (m_hat / (jnp.sqrt(v_hat) + ADAM_EPS) + ADAM_WD * w)
    return delta, m, v


def _adamw_reduce(parts, w, m, v, *, tr, name):
    nl, r, c = w.shape
    nt = r // tr

    def body(*refs):
        p_refs = refs[:nl]
        w_ref, m_ref, v_ref, g_ref, d_ref, nm_ref, nv_ref = refs[nl:]
        for li, p_ref in enumerate(p_refs):
            @pl.when(pl.program_id(0) == li)
            def _(p_ref=p_ref):
                g = p_ref[0].astype(F32)
                for k in range(1, N_DEV):
                    g = g + p_ref[k].astype(F32)
                d, nm, nv = _adamw_math(w_ref[...], g, m_ref[...], v_ref[...])
                g_ref[...] = g
                d_ref[...] = d
                nm_ref[...] = nm
                nv_ref[...] = nv

    def part_map(li):
        return lambda l, i: (0, jnp.where(l == li, i, jnp.where(l < li, 0, nt - 1)), 0)

    spec = pl.BlockSpec((None, tr, c), lambda l, i: (l, i, 0))
    return pl.pallas_call(
        body, name=name, grid=(nl, nt),
        in_specs=[pl.BlockSpec((N_DEV, tr, c), part_map(li)) for li in range(nl)] + [spec, spec, spec],
        out_specs=[spec] * 4, out_shape=[_sds((nl, r, c), F32)] * 4, compiler_params=_cparams(ARB, ARB),
    )(*parts, w, m, v)


def _adamw_plain(g, w, m, v, *, name):
    def body(g_ref, w_ref, m_ref, v_ref, d_ref, nm_ref, nv_ref):
        d, nm, nv = _adamw_math(w_ref[...], g_ref[...], m_ref[...], v_ref[...])
        d_ref[...] = d
        nm_ref[...] = nm
        nv_ref[...] = nv

    return pl.pallas_call(body, name=name, out_shape=[_sds(w.shape, F32)] * 3)(g, w, m, v)


def _mesh_pos():
    return lax.axis_index("x"), lax.axis_index("y"), lax.axis_index("c")


def _allgather_body(x_refs, out_refs, send_sems, recv_sems, local_sems, slot):
    x, y, c = _mesh_pos()
    me, sibling = (x, y, c), (x, y, 1 - c)
    chips = [(1 - x, y), (x, 1 - y), (1 - x, 1 - y)]
    waits = []
    for a, (x_ref, out_ref) in enumerate(zip(x_refs, out_refs)):
        def copy(k, block, to, src=None, out_ref=out_ref, a=a):
            return pltpu.make_async_remote_copy(
                src_ref=slot(out_ref, block) if src is None else src, dst_ref=slot(out_ref, block),
                send_sem=send_sems.at[a, k], recv_sem=recv_sems.at[a, k], device_id=to, device_id_type=MESH)

        mine = pltpu.make_async_copy(x_ref, slot(out_ref, me), local_sems.at[a])
        mine.start()
        first = [copy(0, me, sibling, src=x_ref)]
        first += [copy(1 + j, me, (*chip, c), src=x_ref) for j, chip in enumerate(chips)]
        for cp in first:
            cp.start()
        waits.append((copy, mine, first))
    sends = []
    for copy, mine, first in waits:
        passed = [copy(4 + j, (*chip, c), sibling) for j, chip in enumerate(chips)]
        for j, chip in enumerate(chips):
            copy(1 + j, (*chip, c), me).wait_recv()
            passed[j].start()
        sends.append(passed)
    for (copy, mine, first), passed in zip(waits, sends):
        copy(0, sibling, me).wait_recv()
        for j, chip in enumerate(chips):
            copy(4 + j, (*chip, 1 - c), me).wait_recv()
        for cp in first + passed:
            cp.wait_send()
        mine.wait()


PEER_FLIPS = ((0, 0, 1), (1, 0, 0), (0, 1, 0), (1, 1, 0), (1, 0, 1), (0, 1, 1), (1, 1, 1))


def _peer_copies(x_refs, land_refs, send_sem, recv_sem, scatter):
    x, y, c = _mesh_pos()
    me = 4 * x + 2 * y + c
    copies = []
    for x_ref, land_ref in zip(x_refs, land_refs):
        for fx, fy, fc in PEER_FLIPS:
            px, py, pc = x ^ fx, y ^ fy, c ^ fc
            src = x_ref.at[4 * px + 2 * py + pc] if scatter else x_ref
            copies.append(pltpu.make_async_remote_copy(
                src_ref=src, dst_ref=land_ref.at[me], send_sem=send_sem, recv_sem=recv_sem,
                device_id=(px, py, pc), device_id_type=MESH))
    return copies


def _sc_exchange(xs, *, scatter, collective_id, name):
    na = len(xs)
    land_shapes = [x.shape if scatter else (N_DEV,) + x.shape for x in xs]

    def body(*refs):
        x_refs, land_refs = refs[:na], refs[na:2 * na]
        send_sem, recv_sem, local_sem = refs[2 * na:]
        x, y, c = _mesh_pos()
        me = 4 * x + 2 * y + c
        barrier = pltpu.get_barrier_semaphore()
        for fx, fy, fc in PEER_FLIPS:
            pl.semaphore_signal(barrier, inc=1, device_id=(x ^ fx, y ^ fy, c ^ fc), device_id_type=MESH)
        pl.semaphore_wait(barrier, len(PEER_FLIPS))
        for x_ref, land_ref in zip(x_refs, land_refs):
            own = pltpu.make_async_copy(x_ref.at[me] if scatter else x_ref, land_ref.at[me], local_sem)
            own.start()
            own.wait()
        copies = _peer_copies(x_refs, land_refs, send_sem, recv_sem, scatter)
        for cp in copies:
            cp.start()
        for cp in copies:
            cp.wait()

    return pl.kernel(
        body, name=name, out_type=[_sds(s, x.dtype) for s, x in zip(land_shapes, xs)],
        mesh=plsc.ScalarSubcoreMesh(axis_name="sequencer", num_cores=1),
        scratch_types=[pltpu.SemaphoreType.DMA, pltpu.SemaphoreType.DMA, pltpu.SemaphoreType.DMA],
        compiler_params=pltpu.CompilerParams(collective_id=collective_id),
    )(*xs)


def _sc_allgather(xs, *, collective_id, name):
    na = len(xs)

    def body(*refs):
        x_refs, out_refs = refs[:na], refs[na:2 * na]
        send_sems, recv_sems, local_sems = refs[2 * na:]
        x, y, c = _mesh_pos()
        barrier = pltpu.get_barrier_semaphore()
        for fx, fy, fc in PEER_FLIPS:
            pl.semaphore_signal(barrier, inc=1, device_id=(x ^ fx, y ^ fy, c ^ fc), device_id_type=MESH)
        pl.semaphore_wait(barrier, len(PEER_FLIPS))
        _allgather_body(x_refs, out_refs, send_sems, recv_sems, local_sems,
                        lambda ref, pos: ref.at[4 * pos[0] + 2 * pos[1] + pos[2]])

    return pl.kernel(
        body, name=name, out_type=[_sds((N_DEV,) + x.shape, x.dtype) for x in xs],
        mesh=plsc.ScalarSubcoreMesh(axis_name="sequencer", num_cores=1),
        scratch_types=[pltpu.SemaphoreType.DMA((na, 7)), pltpu.SemaphoreType.DMA((na, 7)),
                       pltpu.SemaphoreType.DMA((na,))],
        compiler_params=pltpu.CompilerParams(collective_id=collective_id),
    )(*xs)


def _allgather_vmem(x, *, name):
    r, c = x.shape

    def body(x_ref, out_ref, send_sems, recv_sems, local_sems):
        _allgather_body([x_ref], [out_ref], send_sems, recv_sems, local_sems,
                        lambda ref, pos: ref.at[pl.ds((4 * pos[0] + 2 * pos[1] + pos[2]) * r, r), :])

    vm = pl.BlockSpec(memory_space=pltpu.VMEM)
    return pl.pallas_call(
        body, name=name, in_specs=[vm], out_specs=vm, out_shape=_sds((N_DEV * r, c), x.dtype),
        scratch_shapes=[pltpu.SemaphoreType.DMA((1, 7)), pltpu.SemaphoreType.DMA((1, 7)),
                        pltpu.SemaphoreType.DMA((1,))],
    )(x)


def _sum_slots(gathered, *, name):
    _, r, c = gathered.shape

    def body(g_ref, o_ref):
        acc = g_ref[0]
        for k in range(1, N_DEV):
            acc = acc + g_ref[k]
        o_ref[...] = acc

    return pl.pallas_call(body, name=name, out_shape=_sds((r, c), gathered.dtype))(gathered)


def _t5_bucket(rel):
    nb = REL_BUCKETS // 2
    ret = jnp.where(rel > 0, nb, 0)
    n = jnp.abs(rel)
    max_exact = nb // 2
    nf = jnp.maximum(n, 1).astype(F32)
    large = max_exact + (jnp.log(nf / max_exact) / math.log(REL_MAX_DIST / max_exact)
                         * (nb - max_exact)).astype(jnp.int32)
    large = jnp.minimum(large, nb - 1)
    return ret + jnp.where(n < max_exact, n, large)


def _band_pattern(block, radius, dil):
    kw = block + 2 * radius
    rel = jnp.arange(kw)[None, :] - radius - jnp.arange(block)[:, None]
    return jnp.where(jnp.abs(rel) <= radius, _t5_bucket(rel * dil), -1).astype(jnp.int32).reshape(1, block * kw)


def _rope_tables():
    lane = np.arange(64)
    seg, j = lane // 32, lane % 32
    inv = ROPE_THETA ** (-jnp.arange(0, 32, 2, dtype=F32) / 32)
    tpos = jnp.arange(SEQ)
    pos = jnp.where(jnp.asarray(seg)[None, :] == 0, (tpos // GRID_W)[:, None], (tpos % GRID_W)[:, None])
    ang = pos.astype(F32) * inv[jnp.asarray(j % 16)][None, :]
    cos = jnp.cos(ang)
    sins = jnp.where(jnp.asarray(j)[None, :] < 16, -jnp.sin(ang), jnp.sin(ang))
    return jnp.tile(cos, (1, 4)), jnp.tile(sins, (1, 4))


A_Q, A_K, A_V = (256, 0), (256, 1), (256, 2)
B_Q, B_K, B_V = (256, 0), (128, 2), (128, 3)
A_HEADS = dict(rad=A_RADIUS, nh=4, nkv=4)
B_HEADS = dict(rad=SWA_RADIUS, nh=4, nkv=2)


def _local_step(x, pe, tgt, rel_bias, wts, matmul_weights, grads_ready):
    t = x.shape[0]
    bl = t // SEQ
    cos, sins = _rope_tables()
    blocks_a = [min(BAND_BLOCK, SEQ // d) for d in DILATIONS]
    pats_a = [_band_pattern(blk, A_RADIUS, d) for blk, d in zip(blocks_a, DILATIONS)]
    pat_b = _band_pattern(BAND_BLOCK, SWA_RADIUS, 1)
    table_t = rel_bias.T
    bias_a = [_bias_lookup(table_t[:4], pt, name=f"bias_a{ci}").reshape(4, blk, blk + 2 * A_RADIUS)
              for ci, (pt, blk) in enumerate(zip(pats_a, blocks_a))]
    bias_b = _bias_lookup(table_t[4:], pat_b, name="bias_b").reshape(4, BAND_BLOCK, BAND_BLOCK + 2 * SWA_RADIUS)
    nat4 = lambda a: a.reshape(bl, 1, SEQ, a.shape[-1])

    saved = []
    for li in range(DEPTH):
        w = dict(wts[li])
        w.update(matmul_weights(li, "in", x))
        hn0, proj = _norm_mm((x,), w["g_mix"], w["w_in"], None, tm=1024, tn=1152, name="mix_in_fwd")
        qa1, qa4, qa16, qb, qd = _qkprep_fwd(proj, w["qk_gains"], cos, sins, tm=512, name="qkprep_fwd")
        qa = (nat4(qa1), qa4, qa16)
        oa, la = [], []
        for ci in range(3):
            o, l = _band_fwd(qa[ci], A_Q, A_K, A_V, bias_a[ci], None, name=f"band_a{ci}_fwd", **A_HEADS)
            oa.append(o)
            la.append(l)
        oa[0], la[0] = oa[0].reshape(t, 256), la[0].reshape(t, 256)
        ya, lse_a = _combine_a(oa, la, tm=512, name="combine_a")
        yb, lse_b = _band_fwd(nat4(qb), B_Q, B_K, B_V, bias_b, w["sink_t"], name="band_b_fwd", **B_HEADS)
        yb = yb.reshape(t, 256)
        yc = _c_fwd(proj, w["c_g"], w["c_b"], w["c_ws"], w["c_bst"], tm=512, name="c_fwd")
        yd, lse_d = _dense_fwd(qd, tq=256, name="dense_fwd")
        w.update(matmul_weights(li, "rest", yd))
        mixed, x1 = _norm_mm((ya, yb, yc, yd), w["out_gain"], w["w_out"], x, tm=1024, tn=1024, name="mix_out_fwd")
        hn1, h = _norm_mm((x1,), w["g_ffn"], w["w_up"], None, tm=1024, tn=1408, name="ffn_up_fwd", out_dtype=BF16)
        act = _conv_gate_fwd(h, w["conv_w"], w["conv_b"], name="conv_gate_fwd")
        x2 = _mm(act, w["w_down"], "nn", x1, tm=1024, tn=1024, out_dtype=F32, name="ffn_down_fwd")
        hn2, x3, gate, pp = _ple_fwd(x2, w["g_ple"], w["w_gate"], pe, li * (t // 1024), w["w_proj"], tm=1024,
                                     name="ple_fwd")
        saved.append(dict(w=w, x0=x, hn0=hn0, proj=proj, qa=qa, qb=qb, qd=qd, ya=ya, lse_a=lse_a, yb=yb, lse_b=lse_b,
                          yc=yc, yd=yd, lse_d=lse_d, mixed=mixed, x1=x1, hn1=hn1, h=h, act=act, x2=x2, hn2=hn2,
                          gate=gate, pp=pp))
        x = x3

    loss_tile, dx = _loss_head(x, tgt, tm=512, name="loss_head")
    grads = [None] * DEPTH
    dbias_a, dbias_bs = [[], [], []], []
    for li in reversed(range(DEPTH)):
        s = saved[li]
        w = s["w"]
        g = {}
        dz, dpp = _ple_bwd_ew(dx, s["gate"], s["pp"], tm=512, name="ple_bwd_ew")
        g["w_gate"] = _mm(s["hn2"], dz, "tn", None, tm=1024, tn=512, out_dtype=BF16, name="dw_gate")
        g["w_proj"] = _mm(pe, dpp, "tn", None, tm=256, tn=1024, out_dtype=BF16, name="dw_proj", a_rows=(li, t))
        dx2, dx2b, g["g_ple"] = _mm_bt_normbwd((dz,), w["w_gate"], (s["x2"],), w["g_ple"], dx, tm=1024, tn=1024,
                                               name="ple_bwd", emit_bf16=True)
        g["w_down"] = _mm(s["act"], dx2b, "tn", None, tm=1408, tn=512, out_dtype=BF16, name="dw_down")
        dact = _mm(dx2b, w["w_down"], "nt", None, tm=1024, tn=1408, out_dtype=BF16, name="ffn_down_bwd")
        dhg, dhu, g["conv_w"], g["conv_b"] = _conv_gate_bwd(s["h"], dact, w["conv_w"], w["conv_b"], name="conv_gate_bwd")
        g["w_up"] = jnp.concatenate(
            [_mm(s["hn1"], dhalf, "tn", None, tm=1024, tn=1408, out_dtype=BF16, name=f"dw_up_{nm}")
             for nm, dhalf in (("gate", dhg), ("up", dhu))], axis=1)
        dx1, dx1b, g["g_ffn"] = _mm_bt_normbwd((dhg, dhu), w["w_up"], (s["x1"],), w["g_ffn"], dx2, tm=1024, tn=1408,
                                               name="ffn_up_bwd", emit_bf16=True)
        g["w_out"] = _mm(s["mixed"], dx1b, "tn", None, tm=1024, tn=512, out_dtype=BF16, name="dw_out")
        grads_ready(li, "mid", g)
        dycat, g["out_gain"] = _mm_bt_normbwd((dx1b,), w["w_out"], (s["ya"], s["yb"], s["yc"], s["yd"]), w["out_gain"],
                                              None, tm=1024, tn=1024, name="mix_out_bwd")
        dy_r, lse_r, dl_a, dl_b, dl_d = _deltas(dycat, s["ya"], s["yb"], s["yd"], s["lse_a"], tm=512, name="deltas")
        dy_a = (nat4(dycat),) + tuple(dy_r)
        lse_a = (nat4(s["lse_a"]),) + tuple(lse_r)
        dl_a = (nat4(dl_a[0]),) + tuple(dl_a[1:])
        da = []
        for ci in range(3):
            dq, dk, dv, dbias = _band_bwd(s["qa"][ci], A_Q, A_K, A_V, bias_a[ci], None, dy_a[ci], 0, lse_a[ci],
                                          dl_a[ci], name=f"band_a{ci}_bwd", **A_HEADS)
            if ci == 0:
                dq, dk, dv = (a.reshape(t, 256) for a in (dq, dk, dv))
            da.append((dq, dk, dv))
            dbias_a[ci].append(dbias.reshape(4, -1))
        dqb, dkb, dvb, dbias_b, dsink = _band_bwd(nat4(s["qb"]), B_Q, B_K, B_V, bias_b, w["sink_t"], nat4(dycat), 1,
                                                  nat4(s["lse_b"]), nat4(dl_b), name="band_b_bwd", **B_HEADS)
        dbias_bs.append(dbias_b.reshape(4, -1))
        g["sink"] = dsink[:, 0, 0]
        dd = _dense_bwd(s["qd"], dycat, s["lse_d"], dl_d, tq=256, name="dense_bwd")
        dcu, dcv, g["c_ws"], dbs, g["c_g"], g["c_b"] = _c_bwd(s["proj"], dycat, w["c_g"], w["c_b"], w["c_ws"],
                                                               w["c_wst"], w["c_bst"], tm=512, name="c_bwd")
        g["c_bs"] = dbs[:, ::64].T
        db = (dqb.reshape(t, 256), dkb.reshape(t, 128), dvb.reshape(t, 128))
        dproj, dgains = _qkprep_bwd(s["proj"], da, db, dd, dcu, dcv, w["qk_gains"], cos, sins, tm=512, name="qkprep_bwd")
        g["qk_gain"] = dgains[:6, :64].reshape(3, 2, HEAD_DIM)
        g["w_in"] = _mm(s["hn0"], dproj, "tn", None, tm=1024, tn=1152, out_dtype=BF16, name="dw_in")
        dx, g["g_mix"] = _mm_bt_normbwd((dproj,), w["w_in"], (s["x0"],), w["g_mix"], dx1, tm=1024, tn=1152,
                                        name="mix_in_bwd")
        grads[li] = g
        grads_ready(li, "end", g)
    d_table_a = sum(_bucket_reduce(dbias_a[ci], pats_a[ci], name=f"bucket_a{ci}") for ci in range(3))
    d_table_b = _bucket_reduce(dbias_bs, pat_b, name="bucket_b")
    d_rel_bias = jnp.concatenate([d_table_a, d_table_b], axis=0).T
    return loss_tile[0, 0], dx, grads, d_rel_bias


WEIGHT_NAMES = ("rel_bias", "ln_mix_g", "w_in", "qk_gain", "sink", "c_norm_g", "c_norm_b", "c_ws", "c_bs", "out_gain",
                "w_out", "ln_ffn_g", "w_up", "conv_w", "conv_b", "w_down", "ln_ple_g", "w_ple_gate", "w_ple_proj")
COL_SHARDED = ("w_in", "w_up", "w_ple_proj")
ROW_SHARDED = ("w_out", "w_down", "w_ple_gate")
SMALL_SHARDED = ("conv_w", "out_gain")
REPLICATED = tuple(n for n in WEIGHT_NAMES if n not in COL_SHARDED + ROW_SHARDED + SMALL_SHARDED)
LOCAL_GRAD_KEY = {"ln_mix_g": "g_mix", "ln_ffn_g": "g_ffn", "ln_ple_g": "g_ple", "c_norm_g": "c_g", "c_norm_b": "c_b",
                  "w_ple_gate": "w_gate", "w_ple_proj": "w_proj"}


def _full_from_gathered(name, gathered):
    _, r, c = gathered.shape
    if name in ROW_SHARDED:
        return gathered.reshape(N_DEV * r, c)
    return jnp.transpose(gathered, (1, 0, 2)).reshape(r, N_DEV * c)


def _slots_from_full(name, full):
    rows, cols = full.shape
    if name in ROW_SHARDED:
        return full.reshape(N_DEV, rows // N_DEV, cols)
    return jnp.transpose(full.reshape(rows, N_DEV, cols // N_DEV), (1, 0, 2))


def _piece_rows(shape):
    return -(-int(np.prod(shape)) // 1024) * 8


def _pack_rows(arrays):
    pieces = []
    for a in arrays:
        n, rows = int(np.prod(a.shape)), _piece_rows(a.shape)
        flat = a.astype(F32).reshape(-1)
        if n != rows * LANES:
            flat = jnp.pad(flat, (0, rows * LANES - n))
        pieces.append(flat.reshape(rows, LANES))
    return jnp.concatenate(pieces, axis=0)


def _unpack_rows(packed, shapes):
    out, off = [], 0
    for shp in shapes:
        n, rows = int(np.prod(shp)), _piece_rows(shp)
        piece = packed[off:off + rows]
        out.append((piece if n == rows * LANES else piece.reshape(-1)[:n]).reshape(shp))
        off += rows
    return out


def kernel(x, p, rel_bias, ln_mix_g, w_in, qk_gain, sink, c_norm_g, c_norm_b, c_ws, c_bs, out_gain, w_out, ln_ffn_g, w_up, conv_w, conv_b, w_down, ln_ple_g, w_ple_gate, w_ple_proj, loss_target, m_rel_bias, m_ln_mix_g, m_w_in, m_qk_gain, m_sink, m_c_norm_g, m_c_norm_b, m_c_ws, m_c_bs, m_out_gain, m_w_out, m_ln_ffn_g, m_w_up, m_conv_w, m_conv_b, m_w_down, m_ln_ple_g, m_w_ple_gate, m_w_ple_proj, v_rel_bias, v_ln_mix_g, v_w_in, v_qk_gain, v_sink, v_c_norm_g, v_c_norm_b, v_c_ws, v_c_bs, v_out_gain, v_w_out, v_ln_ffn_g, v_w_up, v_conv_w, v_conv_b, v_w_down, v_ln_ple_g, v_w_ple_gate, v_w_ple_proj):
    env = dict(locals())
    wt = {n: env[n] for n in WEIGHT_NAMES}
    mom_m = {n: env["m_" + n] for n in WEIGHT_NAMES}
    mom_v = {n: env["v_" + n] for n in WEIGHT_NAMES}
    bl = x.shape[0]
    t = bl * SEQ
    me = 4 * lax.axis_index("x") + 2 * lax.axis_index("y") + lax.axis_index("c")

    big = COL_SHARDED + ROW_SHARDED
    full = {}
    small_shapes = [wt[n].shape for n in SMALL_SHARDED]
    small = _allgather_vmem(_pack_rows([wt[n] for n in SMALL_SHARDED]), name="gather_small")
    small = small.reshape(N_DEV, -1)
    off = 0
    for n, shp in zip(SMALL_SHARDED, small_shapes):
        cnt = int(np.prod(shp))
        g = small[:, off:off + cnt].reshape((N_DEV,) + tuple(shp))
        full[n] = jnp.transpose(g, (1, 2, 0, 3)).reshape(shp[0], shp[1], N_DEV * shp[2])
        off += _piece_rows(shp) * LANES

    def head_gain(li, a, b, reps):
        g = jnp.tile(qk_gain[li, a, b], reps)
        return jnp.pad(g, (0, 256 - g.shape[0]))

    wts = []
    for li in range(DEPTH):
        rows = [head_gain(li, 0, 0, 4), head_gain(li, 0, 1, 4), head_gain(li, 1, 0, 4), head_gain(li, 1, 1, 2),
                head_gain(li, 2, 0, 4), head_gain(li, 2, 1, 2), jnp.zeros((256,), F32), jnp.zeros((256,), F32)]
        wts.append(dict(
            g_mix=ln_mix_g[li].reshape(1, -1), qk_gains=jnp.stack(rows),
            sink_t=jnp.broadcast_to(sink[li][:, None, None], (4, 8, 128)),
            c_g=c_norm_g[li].reshape(1, -1), c_b=c_norm_b[li].reshape(1, -1), c_ws=c_ws[li].astype(BF16),
            c_wst=jnp.transpose(c_ws[li], (0, 2, 1)).astype(BF16), c_bst=jnp.repeat(c_bs[li].T, 64, axis=1),
            out_gain=full["out_gain"][li].reshape(1, -1), g_ffn=ln_ffn_g[li].reshape(1, -1),
            conv_w=full["conv_w"][li], conv_b=conv_b[li].reshape(1, -1), g_ple=ln_ple_g[li].reshape(1, -1)))

    local_key = {"w_ple_gate": "w_gate", "w_ple_proj": "w_proj"}

    gather_names = {"in": ("w_in",), "rest": tuple(n for n in big if n != "w_in")}
    gathered = {}
    for cid, (li, names) in enumerate(((0, gather_names["in"]), (0, gather_names["rest"]), (1, big))):
        lands = _sc_allgather([wt[n][li].astype(BF16) for n in names], collective_id=cid,
                              name=f"gather_{li}_{len(names)}")
        gathered.setdefault(li, {}).update(zip(names, lands))

    def matmul_weights(li, part, after):
        out = {}
        for n in gather_names[part]:
            g, _ = lax.optimization_barrier((gathered[li][n], after))
            out[local_key.get(n, n)] = _full_from_gathered(n, g)
        return out

    mid_names = ("w_ple_gate", "w_ple_proj", "w_down", "w_up", "w_out")
    end_names = ("w_in",)
    landed = {}

    def start_exchange(li, names, g, tag, cid):
        slots = [_slots_from_full(n, g[local_key.get(n, n)]) for n in names]
        lands = _sc_exchange(slots, scatter=True, collective_id=cid, name=f"grads_{li}_{tag}")
        landed.update({(n, li): land for n, land in zip(names, lands)})

    def grads_ready(li, stage, g):
        if li == 0:
            start_exchange(li, mid_names if stage == "mid" else end_names, g, stage, 5 if stage == "mid" else 6)
        elif stage == "end":
            start_exchange(li, mid_names + end_names, g, stage, 4)

    loss_part, dx, grads, d_rel_bias = _local_step(
        x.reshape(t, D_MODEL), p.reshape(DEPTH * t, PLE_DIM), loss_target.reshape(t, D_MODEL), rel_bias, wts,
        matmul_weights, grads_ready)
    loss = lax.psum(loss_part, ("x", "y", "c"))

    def local_grad(n):
        if n == "rel_bias":
            return d_rel_bias
        key = LOCAL_GRAD_KEY.get(n, n)
        return jnp.stack([grads[li][key].reshape(wt[n].shape[1:]) if n in REPLICATED else grads[li][key]
                          for li in range(DEPTH)])

    small_names = REPLICATED + SMALL_SHARDED
    small_full_shapes = [wt[n].shape if n in REPLICATED else full[n].shape for n in small_names]
    small_parts = _allgather_vmem(_pack_rows([local_grad(n) for n in small_names]), name="allgather_small_grads")
    small_parts = small_parts.reshape(N_DEV, -1, LANES)

    out_g, out_d, out_m, out_v = {}, {}, {}, {}
    for n in big:
        out_g[n], out_d[n], out_m[n], out_v[n] = _adamw_reduce(
            [landed[n, li] for li in range(DEPTH)], wt[n], mom_m[n], mom_v[n], tr=32 if n == "w_down" else 128,
            name="adamw_" + n)

    reduced = _sum_slots(small_parts, name="sum_small_grads")
    reduced = dict(zip(small_names, _unpack_rows(reduced, small_full_shapes)))
    rep_shapes = [wt[n].shape for n in REPLICATED]
    upd = _adamw_plain(_pack_rows([reduced[n] for n in REPLICATED]), _pack_rows([wt[n] for n in REPLICATED]),
                       _pack_rows([mom_m[n] for n in REPLICATED]), _pack_rows([mom_v[n] for n in REPLICATED]),
                       name="adamw_replicated")
    for dst, packed in zip((out_d, out_m, out_v), upd):
        dst.update(zip(REPLICATED, _unpack_rows(packed, rep_shapes)))
    for n in REPLICATED:
        out_g[n] = reduced[n]
    for n in SMALL_SHARDED:
        shp = wt[n].shape
        g = reduced[n].reshape(shp[0], shp[1], N_DEV, shp[2])
        g = lax.dynamic_index_in_dim(g, me, axis=2, keepdims=False)
        two_d = lambda a: a.reshape(-1, shp[-1])
        res = _adamw_plain(two_d(g), two_d(wt[n]), two_d(mom_m[n]), two_d(mom_v[n]), name="adamw_" + n)
        out_g[n] = g
        out_d[n], out_m[n], out_v[n] = [r.reshape(shp) for r in res]

    return (loss, dx.reshape(bl, SEQ, D_MODEL), *[out_g[n] for n in WEIGHT_NAMES], *[out_d[n] for n in WEIGHT_NAMES],
            *[out_m[n] for n in WEIGHT_NAMES], *[out_v[n] for n in WEIGHT_NAMES])
```

```python
import math

import jax
import jax.numpy as jnp
import numpy as np
from jax import lax
from jax.experimental import pallas as pl
from jax.experimental.pallas import tpu as pltpu
from jax.experimental.pallas import tpu_sc as plsc

F32 = jnp.float32
BF16 = jnp.bfloat16
HI = lax.Precision.HIGHEST

N_DEV = 8
D_MODEL = 1024
SEQ = 2048
DEPTH = 2
HEAD_DIM = 64
IN_WIDTH = 2304
D_FF = 2816
PLE_DIM = 256
C_CHUNK = 128
C_GROUPS = 4
DILATED_CFGS = ((128, 1), (512, 4), (2048, 16))
DILATIONS = tuple(d for _, d in DILATED_CFGS)
A_RADIUS = 64
SWA_RADIUS = 128
BAND_BLOCK = 256
GRID_W = 64
ROPE_THETA = 10000.0
REL_BUCKETS = 32
REL_MAX_DIST = 1024
EPS = 1e-6
NEG_INF = -1e30
ATTN_SCALE = HEAD_DIM ** -0.5
LANES = 128

ADAM_LR = 0.001
ADAM_B1 = 0.9
ADAM_B2 = 0.999
ADAM_EPS = 1e-08
ADAM_WD = 0.01
ADAM_STEP = 10

MESH = pl.DeviceIdType.MESH
NT = (((1,), (1,)), ((), ()))
TN = (((0,), (0,)), ((), ()))
ARB = "arbitrary"
PAR = "parallel"


def _cparams(*sem):
    return pltpu.CompilerParams(dimension_semantics=tuple(sem))


def _sds(shape, dtype):
    return jax.ShapeDtypeStruct(tuple(shape), dtype)


def _group_sum_matrix(n, same_group):
    r = lax.broadcasted_iota(jnp.int32, (n, n), 0)
    c = lax.broadcasted_iota(jnp.int32, (n, n), 1)
    if same_group:
        return ((r >> 6) == (c >> 6)).astype(F32)
    return ((r & 63) == (c & 63)).astype(F32)


def _seg_sum(x, e):
    eb = e.astype(BF16)
    hi = x.astype(BF16)
    lo = (x - hi.astype(F32)).astype(BF16)
    return jnp.dot(hi, eb, preferred_element_type=F32) + jnp.dot(lo, eb, preferred_element_type=F32)


def _gelu(x):
    c = math.sqrt(2.0 / math.pi)
    return 0.5 * x * (1.0 + jnp.tanh(c * (x + 0.044715 * (x * x * x))))


def _gelu_grad(x):
    c = math.sqrt(2.0 / math.pi)
    t = jnp.tanh(c * (x + 0.044715 * (x * x * x)))
    return 0.5 * (1.0 + t) + 0.5 * x * (1.0 - t * t) * c * (1.0 + 3.0 * 0.044715 * (x * x))


def _sigmoid(x):
    return 1.0 / (1.0 + jnp.exp(-x))


def _scatter_cols(scratch, first, val):
    for c in range(val.shape[1] // LANES):
        scratch[first + c] = val[:, c * LANES:(c + 1) * LANES]


def _gather_cols(scratch, first, ncol):
    return jnp.concatenate([scratch[first + c] for c in range(ncol)], axis=1)


def _read_residue(scratch, first, ncol, r, d):
    n = scratch.shape[1] // d
    return jnp.concatenate([scratch.at[first + c][pl.ds(r, n, stride=d), :] for c in range(ncol)], axis=1)


def _write_residue(scratch, first, r, d, val):
    n = scratch.shape[1] // d
    for c in range(val.shape[1] // LANES):
        scratch.at[first + c][pl.ds(r, n, stride=d), :] = val[:, c * LANES:(c + 1) * LANES]


def _norm_mm(xs, gain, w, res, *, tm, tn, name, out_dtype=F32):
    t = xs[0].shape[0]
    k = sum(x.shape[1] for x in xs)
    n = w.shape[1]
    ng = len(xs)
    has_res = res is not None

    def body(*refs):
        x_refs = refs[:ng]
        g_ref, w_ref = refs[ng], refs[ng + 1]
        res_ref = refs[ng + 2] if has_res else None
        hn_ref, o_ref, hn_s = refs[ng + 2 + has_res:]

        @pl.when(pl.program_id(1) == 0)
        def _():
            off = 0
            for xr in x_refs:
                x = xr[...]
                wd = x.shape[1]
                r = lax.rsqrt(jnp.mean(x * x, axis=-1, keepdims=True) + EPS)
                hn_s[:, off:off + wd] = (x * r * g_ref[:, off:off + wd]).astype(BF16)
                off += wd
            hn_ref[...] = hn_s[...]

        acc = jnp.dot(hn_s[...], w_ref[...], preferred_element_type=F32)
        if has_res:
            acc = acc + res_ref[...]
        o_ref[...] = acc.astype(out_dtype)

    in_specs = [pl.BlockSpec((tm, x.shape[1]), lambda i, j: (i, 0)) for x in xs]
    in_specs += [pl.BlockSpec((1, k), lambda i, j: (0, 0)), pl.BlockSpec((k, tn), lambda i, j: (0, j))]
    args = list(xs) + [gain, w]
    if has_res:
        in_specs.append(pl.BlockSpec((tm, tn), lambda i, j: (i, j)))
        args.append(res)
    return pl.pallas_call(
        body, name=name, grid=(t // tm, n // tn), in_specs=in_specs,
        out_specs=[pl.BlockSpec((tm, k), lambda i, j: (i, 0)), pl.BlockSpec((tm, tn), lambda i, j: (i, j))],
        out_shape=[_sds((t, k), BF16), _sds((t, n), out_dtype)],
        scratch_shapes=[pltpu.VMEM((tm, k), BF16)],
        compiler_params=_cparams(PAR, ARB),
    )(*args)


def _mm(a, b, mode, res, *, tm, tn, out_dtype, name, a_rows=None):
    if mode == "tn":
        kk, m = a.shape
        blk_a = 0
        if a_rows is not None:
            blk_a, kk = a_rows
        a_spec = pl.BlockSpec((kk, tm), lambda i, j: (blk_a, i))
    else:
        m, kk = a.shape
        a_spec = pl.BlockSpec((tm, kk), lambda i, j: (i, 0))
    if mode == "nt":
        n = b.shape[0]
        b_spec = pl.BlockSpec((tn, kk), lambda i, j: (j, 0))
    else:
        n = b.shape[1]
        b_spec = pl.BlockSpec((kk, tn), lambda i, j: (0, j))
    has_res = res is not None

    def body(*refs):
        a_ref, b_ref = refs[0], refs[1]
        o_ref = refs[-1]
        av = a_ref[...].astype(BF16)
        bv = b_ref[...].astype(BF16)
        if mode == "nn":
            acc = jnp.dot(av, bv, preferred_element_type=F32)
        elif mode == "nt":
            acc = lax.dot_general(av, bv, NT, preferred_element_type=F32)
        else:
            acc = lax.dot_general(av, bv, TN, preferred_element_type=F32)
        if has_res:
            acc = acc + refs[2][...]
        o_ref[...] = acc.astype(out_dtype)

    in_specs = [a_spec, b_spec]
    args = [a, b]
    if has_res:
        in_specs.append(pl.BlockSpec((tm, tn), lambda i, j: (i, j)))
        args.append(res)
    return pl.pallas_call(
        body, name=name, grid=(m // tm, n // tn), in_specs=in_specs,
        out_specs=pl.BlockSpec((tm, tn), lambda i, j: (i, j)),
        out_shape=_sds((m, n), out_dtype),
        compiler_params=_cparams(PAR, PAR),
    )(*args)


def _mm_bt_normbwd(dys, w, xs, gain, dres, *, tm, tn, name, emit_bf16=False):
    t, wd_each = dys[0].shape
    nd = len(dys)
    per = wd_each // tn
    nj = nd * per
    k = w.shape[0]
    ng = len(xs)
    has_res = dres is not None

    def body(*refs):
        dy_refs = refs[:nd]
        w_ref = refs[nd]
        x_refs = refs[nd + 1:nd + 1 + ng]
        g_ref = refs[nd + 1 + ng]
        dres_ref = refs[nd + 2 + ng] if has_res else None
        outs = refs[nd + 2 + ng + has_res:]
        dx_ref = outs[0]
        dxb_ref = outs[1] if emit_bf16 else None
        dg_ref, acc = outs[1 + emit_bf16:]
        i, j = pl.program_id(0), pl.program_id(1)

        @pl.when(j == 0)
        def _():
            acc[...] = jnp.zeros_like(acc)

        for d, dy_ref in enumerate(dy_refs):
            @pl.when((j >= d * per) & (j < (d + 1) * per))
            def _(dy_ref=dy_ref):
                acc[...] += lax.dot_general(dy_ref[...].astype(BF16), w_ref[...], NT, preferred_element_type=F32)

        @pl.when(j == nj - 1)
        def _():
            @pl.when(i == 0)
            def _():
                dg_ref[...] = jnp.zeros_like(dg_ref)

            off = 0
            for xr in x_refs:
                x = xr[...]
                wd = x.shape[1]
                g = g_ref[:, off:off + wd]
                dyn = acc[:, off:off + wd]
                r = lax.rsqrt(jnp.mean(x * x, axis=-1, keepdims=True) + EPS)
                gdy = dyn * g
                dx = r * gdy - x * (r * r * r * jnp.mean(gdy * x, axis=-1, keepdims=True))
                if has_res:
                    dx = dx + dres_ref[:, off:off + wd]
                dx_ref[:, off:off + wd] = dx
                if emit_bf16:
                    dxb_ref[:, off:off + wd] = dx.astype(BF16)
                dg_ref[:, off:off + wd] += jnp.sum(dyn * x * r, axis=0, keepdims=True)
                off += wd

    def dy_map(d):
        return lambda i, j: (i, jnp.clip(j - d * per, 0, per - 1))

    in_specs = [pl.BlockSpec((tm, tn), dy_map(d)) for d in range(nd)]
    in_specs.append(pl.BlockSpec((k, tn), lambda i, j: (0, j)))
    in_specs += [pl.BlockSpec((tm, x.shape[1]), lambda i, j: (i, 0)) for x in xs]
    in_specs.append(pl.BlockSpec((1, k), lambda i, j: (0, 0)))
    args = list(dys) + [w] + list(xs) + [gain]
    if has_res:
        in_specs.append(pl.BlockSpec((tm, k), lambda i, j: (i, 0)))
        args.append(dres)
    row = pl.BlockSpec((tm, k), lambda i, j: (i, 0))
    out_specs = [row] + ([row] if emit_bf16 else []) + [pl.BlockSpec((1, k), lambda i, j: (0, 0))]
    out_shape = [_sds((t, k), F32)] + ([_sds((t, k), BF16)] if emit_bf16 else []) + [_sds((1, k), F32)]
    return pl.pallas_call(
        body, name=name, grid=(t // tm, nj), in_specs=in_specs, out_specs=out_specs, out_shape=out_shape,
        scratch_shapes=[pltpu.VMEM((tm, k), F32)],
        compiler_params=_cparams(ARB, ARB),
    )(*args)


def _rope_partner(y):
    n = y.shape[1]
    lane = lax.broadcasted_iota(jnp.int32, y.shape, 1)
    return jnp.where((lane & 31) < 16, pltpu.roll(y, n - 16, 1), pltpu.roll(y, 16, 1))


def _residue_specs(tm, width, nt):
    specs = [pl.BlockSpec((tm, width), lambda b, i: (b * nt + i, 0))]
    for d in DILATIONS[1:]:
        specs.append(pl.BlockSpec((None, d, tm // d, width), lambda b, i: (b, 0, i, 0)))
    return specs


def _residue_shapes(bl, width, dtype):
    return [_sds((bl * SEQ, width), dtype)] + [_sds((bl, d, SEQ // d, width), dtype) for d in DILATIONS[1:]]


def _qkprep_fwd(proj, gains, cos, sins, *, tm, name):
    t = proj.shape[0]
    bl = t // SEQ
    nt = SEQ // tm

    def body(p_ref, g_ref, c_ref, s_ref, qa1_ref, qa4_ref, qa16_ref, qb_ref, qd_ref, scr):
        e = _group_sum_matrix(256, True)

        def hn(x, row):
            x = x.astype(F32)
            wd = x.shape[1]
            ms = _seg_sum(x * x, e[:wd, :wd]) * (1.0 / HEAD_DIM)
            return x * lax.rsqrt(ms + EPS) * g_ref[row:row + 1, :wd]

        qa = jnp.concatenate([hn(p_ref[:, 0:256], 0) * ATTN_SCALE, hn(p_ref[:, 256:512], 1),
                              p_ref[:, 512:768].astype(F32)], axis=1)
        qa1_ref[...] = qa.astype(BF16)
        _scatter_cols(scr, 0, qa)
        for d, ref in ((4, qa4_ref), (16, qa16_ref)):
            for r in range(d):
                ref[r] = _read_residue(scr, 0, 6, r, d).astype(BF16)
        qb_ref[:, 0:256] = (hn(p_ref[:, 768:1024], 2) * ATTN_SCALE).astype(BF16)
        qb_ref[:, 256:384] = hn(p_ref[:, 1024:1152], 3).astype(BF16)
        qb_ref[:, 384:512] = p_ref[:, 1152:1280].astype(BF16)
        yq = hn(p_ref[:, 1792:2048], 4)
        yq = yq * c_ref[...] + _rope_partner(yq) * s_ref[...]
        qd_ref[:, 0:256] = (yq * ATTN_SCALE).astype(BF16)
        yk = hn(p_ref[:, 2048:2176], 5)
        yk = yk * c_ref[:, 0:128] + _rope_partner(yk) * s_ref[:, 0:128]
        qd_ref[:, 256:384] = yk.astype(BF16)
        qd_ref[:, 384:512] = p_ref[:, 2176:2304].astype(BF16)

    row = lambda width: pl.BlockSpec((tm, width), lambda b, i: (b * nt + i, 0))
    tab = pl.BlockSpec((tm, 256), lambda b, i: (i, 0))
    return pl.pallas_call(
        body, name=name, grid=(bl, nt),
        in_specs=[row(IN_WIDTH), pl.BlockSpec((8, 256), lambda b, i: (0, 0)), tab, tab],
        out_specs=_residue_specs(tm, 768, nt) + [row(512), row(512)],
        out_shape=_residue_shapes(bl, 768, BF16) + [_sds((t, 512), BF16), _sds((t, 512), BF16)],
        scratch_shapes=[pltpu.VMEM((6, tm, LANES), F32)],
        compiler_params=_cparams(PAR, PAR),
    )(proj, gains, cos, sins)


def _qkprep_bwd(proj, da, db, dd, dcu, dcv, gains, cos, sins, *, tm, name):
    t = proj.shape[0]
    bl = t // SEQ
    nt = SEQ // tm
    flat = [a for cfg in da for a in cfg] + list(db) + list(dd) + [dcu, dcv]

    def body(*refs):
        p_ref, g_ref, c_ref, s_ref = refs[:4]
        d_refs = refs[4:4 + len(flat)]
        dp_ref, dg_ref, scr = refs[4 + len(flat):]
        a_refs = d_refs[:9]
        dqb_ref, dkb_ref, dvb_ref, dqd_ref, dkd_ref, dvd_ref, dcu_ref, dcv_ref = d_refs[9:]
        e = _group_sum_matrix(256, True)
        first = (pl.program_id(0) == 0) & (pl.program_id(1) == 0)
        last = (pl.program_id(0) == bl - 1) & (pl.program_id(1) == nt - 1)

        @pl.when(first)
        def _():
            dg_ref[...] = jnp.zeros_like(dg_ref)

        def hn_bwd(x, dy, row):
            x, dy = x.astype(F32), dy.astype(F32)
            wd = x.shape[1]
            ee = e[:wd, :wd]
            g = g_ref[row:row + 1, :wd]
            r = lax.rsqrt(_seg_sum(x * x, ee) * (1.0 / HEAD_DIM) + EPS)
            gdy = dy * g
            dx = r * gdy - x * (r * r * r * (_seg_sum(gdy * x, ee) * (1.0 / HEAD_DIM)))
            dg_ref[row:row + 1, :wd] += jnp.sum(dy * x * r, axis=0, keepdims=True)
            return dx

        def rope_bwd(dy, wd):
            dy = dy.astype(F32)
            return dy * c_ref[:, :wd] + _rope_partner(dy * s_ref[:, :wd])

        dqkv = jnp.concatenate([a_refs[m][...].astype(F32) for m in range(3)], axis=1)
        for ci, d in ((1, 4), (2, 16)):
            for r in range(d):
                part = jnp.concatenate([a_refs[3 * ci + m][r].astype(F32) for m in range(3)], axis=1)
                _write_residue(scr, 0, r, d, part)
            dqkv = dqkv + _gather_cols(scr, 0, 6)
        dp_ref[:, 0:256] = hn_bwd(p_ref[:, 0:256], dqkv[:, 0:256] * ATTN_SCALE, 0).astype(BF16)
        dp_ref[:, 256:512] = hn_bwd(p_ref[:, 256:512], dqkv[:, 256:512], 1).astype(BF16)
        dp_ref[:, 512:768] = dqkv[:, 512:768].astype(BF16)
        dp_ref[:, 768:1024] = hn_bwd(p_ref[:, 768:1024], dqb_ref[...] * ATTN_SCALE, 2).astype(BF16)
        dp_ref[:, 1024:1152] = hn_bwd(p_ref[:, 1024:1152], dkb_ref[...], 3).astype(BF16)
        dp_ref[:, 1152:1280] = dvb_ref[...].astype(BF16)
        dp_ref[:, 1280:1536] = dcu_ref[...].astype(BF16)
        dp_ref[:, 1536:1792] = dcv_ref[...].astype(BF16)
        dp_ref[:, 1792:2048] = hn_bwd(p_ref[:, 1792:2048], rope_bwd(dqd_ref[...] * ATTN_SCALE, 256), 4).astype(BF16)
        dp_ref[:, 2048:2176] = hn_bwd(p_ref[:, 2048:2176], rope_bwd(dkd_ref[...], 128), 5).astype(BF16)
        dp_ref[:, 2176:2304] = dvd_ref[...].astype(BF16)

        @pl.when(last)
        def _():
            dg_ref[...] = _seg_sum(dg_ref[...], _group_sum_matrix(256, False))

    row = lambda width: pl.BlockSpec((tm, width), lambda b, i: (b * nt + i, 0))
    tab = pl.BlockSpec((tm, 256), lambda b, i: (i, 0))
    in_specs = [row(IN_WIDTH), pl.BlockSpec((8, 256), lambda b, i: (0, 0)), tab, tab]
    res_specs = _residue_specs(tm, 256, nt)
    in_specs += [res_specs[ci] for ci in range(3) for _ in range(3)]
    in_specs += [row(a.shape[1]) for a in flat[9:]]
    return pl.pallas_call(
        body, name=name, grid=(bl, nt), in_specs=in_specs,
        out_specs=[row(IN_WIDTH), pl.BlockSpec((8, 256), lambda b, i: (0, 0))],
        out_shape=[_sds((t, IN_WIDTH), BF16), _sds((8, 256), F32)],
        scratch_shapes=[pltpu.VMEM((6, tm, LANES), F32)],
        compiler_params=_cparams(ARB, ARB),
    )(proj, gains, cos, sins, *flat)


BAND_ROWS_PER_STEP = 512


def _residues_per_step(dil, seq_len):
    return min(dil, max(1, BAND_ROWS_PER_STEP // seq_len))


def _band_spec(seq_len, spec, rb):
    width, idx = spec
    return pl.BlockSpec((None, rb, seq_len, width), lambda b, r: (b, r, 0, idx))


def _fill_padded(dst, src_ref, rad, seq_len):
    z = jnp.zeros((rad, dst.shape[1]), dst.dtype)
    dst[0:rad, :] = z
    dst[rad + seq_len:rad + seq_len + rad, :] = z
    dst[rad:rad + seq_len, :] = src_ref[...]


def _band_fwd(src, qs, ks, vs, bias, sink, *, rad, nh, nkv, name):
    bl, dil, sl, _ = src.shape
    blk = bias.shape[1]
    kw = blk + 2 * rad
    nb = sl // blk
    rep = nh // nkv
    has_sink = sink is not None
    rb = _residues_per_step(dil, sl)

    def body(*refs):
        q_all, k_all, v_all, b_ref = refs[:4]
        s_ref = refs[4] if has_sink else None
        o_all, l_all, kp, vp = refs[4 + has_sink:]
        for ri in range(rb):
            one_sequence(q_all.at[ri], k_all.at[ri], v_all.at[ri], b_ref, s_ref, o_all.at[ri], l_all.at[ri], kp, vp)

    def one_sequence(q_ref, k_ref, v_ref, b_ref, s_ref, o_ref, l_ref, kp, vp):
        _fill_padded(kp, k_ref, rad, sl)
        _fill_padded(vp, v_ref, rad, sl)

        def blk_body(i, carry):
            r0 = pl.multiple_of(i * blk, blk)
            qb = q_ref[pl.ds(r0, blk), :]
            kwin = kp[pl.ds(r0, kw), :]
            vwin = vp[pl.ds(r0, kw), :]
            col = r0 - rad + lax.broadcasted_iota(jnp.int32, (blk, kw), 1)
            neg = jnp.where((col >= 0) & (col < sl), 0.0, NEG_INF).astype(F32)
            for h in range(nh):
                g = h // rep
                hs = slice(h * HEAD_DIM, (h + 1) * HEAD_DIM)
                gs = slice(g * HEAD_DIM, (g + 1) * HEAD_DIM)
                s = lax.dot_general(qb[:, hs], kwin[:, gs], NT, preferred_element_type=F32)
                s = s + b_ref[h] + neg
                m = jnp.max(s, axis=1, keepdims=True)
                if has_sink:
                    sk = s_ref[h][0:1, 0:1]
                    m = jnp.maximum(m, sk)
                p = jnp.exp(s - m)
                den = jnp.sum(p, axis=1, keepdims=True)
                if has_sink:
                    den = den + jnp.exp(sk - m)
                o = jnp.dot(p.astype(BF16), vwin[:, gs], preferred_element_type=F32) / den
                o_ref[pl.ds(r0, blk), hs] = o
                l_ref[pl.ds(r0, blk), hs] = jnp.broadcast_to(m + jnp.log(den), (blk, HEAD_DIM))
            return carry

        lax.fori_loop(0, nb, blk_body, 0)

    in_specs = [_band_spec(sl, qs, rb), _band_spec(sl, ks, rb), _band_spec(sl, vs, rb),
                pl.BlockSpec((nh, blk, kw), lambda b, r: (0, 0, 0))]
    args = [src] * 3 + [bias]
    if has_sink:
        in_specs.append(pl.BlockSpec((nh, 8, 128), lambda b, r: (0, 0, 0)))
        args.append(sink)
    return pl.pallas_call(
        body, name=name, grid=(bl, dil // rb), in_specs=in_specs,
        out_specs=[_band_spec(sl, (256, 0), rb)] * 2,
        out_shape=[_sds((bl, dil, sl, 256), F32)] * 2,
        scratch_shapes=[pltpu.VMEM((sl + 2 * rad, ks[0]), BF16), pltpu.VMEM((sl + 2 * rad, vs[0]), BF16)],
        compiler_params=_cparams(PAR, PAR),
    )(*args)


def _band_bwd(src, qs, ks, vs, bias, sink, dy, dcol, lse, delta, *, rad, nh, nkv, name):
    bl, dil, sl, _ = src.shape
    blk = bias.shape[1]
    kw = blk + 2 * rad
    nb = sl // blk
    rep = nh // nkv
    has_sink = sink is not None
    rb = _residues_per_step(dil, sl)
    wk, wv = ks[0], vs[0]

    def body(*refs):
        q_all, k_all, v_all, b_ref = refs[:4]
        s_ref = refs[4] if has_sink else None
        do_all, l_all, dl_all = refs[4 + has_sink:7 + has_sink]
        outs = refs[7 + has_sink:]
        dsk_ref = None
        if has_sink:
            dq_all, dk_all, dv_all, db_ref, dsk_ref, kp, vp, dka, dva = outs
        else:
            dq_all, dk_all, dv_all, db_ref, kp, vp, dka, dva = outs

        @pl.when((pl.program_id(0) == 0) & (pl.program_id(1) == 0))
        def _():
            db_ref[...] = jnp.zeros_like(db_ref)
            if has_sink:
                dsk_ref[...] = jnp.zeros_like(dsk_ref)

        for ri in range(rb):
            one_sequence(q_all.at[ri], k_all.at[ri], v_all.at[ri], b_ref, s_ref, do_all.at[ri], l_all.at[ri],
                         dl_all.at[ri], dq_all.at[ri], dk_all.at[ri], dv_all.at[ri], db_ref, dsk_ref, kp, vp, dka, dva)

    def one_sequence(q_ref, k_ref, v_ref, b_ref, s_ref, do_ref, l_ref, dl_ref, dq_ref, dk_ref, dv_ref, db_ref, dsk_ref,
                     kp, vp, dka, dva):
        _fill_padded(kp, k_ref, rad, sl)
        _fill_padded(vp, v_ref, rad, sl)
        dka[...] = jnp.zeros_like(dka)
        dva[...] = jnp.zeros_like(dva)

        def blk_body(i, carry):
            r0 = pl.multiple_of(i * blk, blk)
            qb = q_ref[pl.ds(r0, blk), :]
            kwin = kp[pl.ds(r0, kw), :]
            vwin = vp[pl.ds(r0, kw), :]
            dob = do_ref[pl.ds(r0, blk), :].astype(BF16)
            lb = l_ref[pl.ds(r0, blk), :]
            dlb = dl_ref[pl.ds(r0, blk), :]
            col = r0 - rad + lax.broadcasted_iota(jnp.int32, (blk, kw), 1)
            neg = jnp.where((col >= 0) & (col < sl), 0.0, NEG_INF).astype(F32)
            for h in range(nh):
                g = h // rep
                hs = slice(h * HEAD_DIM, (h + 1) * HEAD_DIM)
                gs = slice(g * HEAD_DIM, (g + 1) * HEAD_DIM)
                qh, kh, vh, doh = qb[:, hs], kwin[:, gs], vwin[:, gs], dob[:, hs]
                lh = lb[:, h * HEAD_DIM:h * HEAD_DIM + 1]
                dlh = dlb[:, h * HEAD_DIM:h * HEAD_DIM + 1]
                s = lax.dot_general(qh, kh, NT, preferred_element_type=F32) + b_ref[h] + neg
                p = jnp.exp(s - lh)
                dp = lax.dot_general(doh, vh, NT, preferred_element_type=F32)
                ds = p * (dp - dlh)
                dsb = ds.astype(BF16)
                dq_ref[pl.ds(r0, blk), hs] = jnp.dot(dsb, kh, preferred_element_type=F32).astype(BF16)
                dka[pl.ds(r0, kw), gs] += lax.dot_general(dsb, qh, TN, preferred_element_type=F32)
                dva[pl.ds(r0, kw), gs] += lax.dot_general(p.astype(BF16), doh, TN, preferred_element_type=F32)
                db_ref[h] += ds
                if has_sink:
                    ps = jnp.exp(s_ref[h][0:1, 0:1] - lh)
                    dsk_ref[h] += jnp.broadcast_to(-jnp.sum(ps * dlh, axis=0, keepdims=True), (8, 128))
            return carry

        lax.fori_loop(0, nb, blk_body, 0)
        dk_ref[...] = dka[rad:rad + sl, :].astype(BF16)
        dv_ref[...] = dva[rad:rad + sl, :].astype(BF16)

    const3 = lambda b, r: (0, 0, 0)
    in_specs = [_band_spec(sl, qs, rb), _band_spec(sl, ks, rb), _band_spec(sl, vs, rb),
                pl.BlockSpec((nh, blk, kw), const3)]
    args = [src] * 3 + [bias]
    if has_sink:
        in_specs.append(pl.BlockSpec((nh, 8, 128), const3))
        args.append(sink)
    row = _band_spec(sl, (256, 0), rb)
    in_specs += [_band_spec(sl, (256, dcol), rb), row, row]
    args += [dy, lse, delta]
    out_specs = [row, _band_spec(sl, (wk, 0), rb), _band_spec(sl, (wv, 0), rb), pl.BlockSpec((nh, blk, kw), const3)]
    out_shape = [_sds((bl, dil, sl, 256), BF16), _sds((bl, dil, sl, wk), BF16), _sds((bl, dil, sl, wv), BF16),
                 _sds((nh, blk, kw), F32)]
    if has_sink:
        out_specs.append(pl.BlockSpec((nh, 8, 128), const3))
        out_shape.append(_sds((nh, 8, 128), F32))
    return pl.pallas_call(
        body, name=name, grid=(bl, dil // rb), in_specs=in_specs, out_specs=out_specs, out_shape=out_shape,
        scratch_shapes=[pltpu.VMEM((sl + 2 * rad, wk), BF16), pltpu.VMEM((sl + 2 * rad, wv), BF16),
                        pltpu.VMEM((sl + 2 * rad, wk), F32), pltpu.VMEM((sl + 2 * rad, wv), F32)],
        compiler_params=_cparams(ARB, ARB),
    )(*args)


def _combine_a(os_, ls_, *, tm, name):
    bl = os_[1].shape[0]
    t = bl * SEQ
    nt = SEQ // tm

    def body(o1, o4, o16, l1, l4, l16, y_ref, lt_ref, scr):
        for k, (d, ref) in enumerate(((4, o4), (16, o16), (4, l4), (16, l16))):
            for r in range(d):
                _write_residue(scr, 2 * k, r, d, ref[r])
        o2, o3, b, c = (_gather_cols(scr, 2 * k, 2) for k in range(4))
        a = l1[...]
        m = jnp.maximum(jnp.maximum(a, b), c)
        ea, eb, ec = jnp.exp(a - m), jnp.exp(b - m), jnp.exp(c - m)
        den = ea + eb + ec
        y_ref[...] = (ea / den) * o1[...] + (eb / den) * o2 + (ec / den) * o3
        lt_ref[...] = m + jnp.log(den)

    specs = _residue_specs(tm, 256, nt)
    return pl.pallas_call(
        body, name=name, grid=(bl, nt), in_specs=specs * 2, out_specs=[specs[0]] * 2,
        out_shape=[_sds((t, 256), F32)] * 2, scratch_shapes=[pltpu.VMEM((8, tm, LANES), F32)],
        compiler_params=_cparams(PAR, PAR),
    )(*os_, *ls_)


def _deltas(dycat, ya, yb, yd, lse_a, *, tm, name):
    t = ya.shape[0]
    bl = t // SEQ
    nt = SEQ // tm

    def body(dy_ref, ya_ref, yb_ref, yd_ref, la_ref, dy4, dy16, l4, l16, da1, da4, da16, db_ref, dd_ref, scr):
        e = _group_sum_matrix(256, True)
        dya = dy_ref[:, 0:256]
        dla = _seg_sum(dya * ya_ref[...], e)
        da1[...] = dla
        db_ref[...] = _seg_sum(dy_ref[:, 256:512] * yb_ref[...], e)
        dd_ref[...] = _seg_sum(dy_ref[:, 768:1024] * yd_ref[...], e)
        for k, (val, r4, r16) in enumerate(((dya, dy4, dy16), (la_ref[...], l4, l16), (dla, da4, da16))):
            _scatter_cols(scr, 2 * k, val)
            for d, ref in ((4, r4), (16, r16)):
                for r in range(d):
                    ref[r] = _read_residue(scr, 2 * k, 2, r, d)

    specs = _residue_specs(tm, 256, nt)
    nat = specs[0]
    shapes = _residue_shapes(bl, 256, F32)
    outs = pl.pallas_call(
        body, name=name, grid=(bl, nt),
        in_specs=[pl.BlockSpec((tm, 1024), lambda b, i: (b * nt + i, 0)), nat, nat, nat, nat],
        out_specs=specs[1:] + specs[1:] + specs + [nat, nat],
        out_shape=shapes[1:] + shapes[1:] + shapes + [shapes[0], shapes[0]],
        scratch_shapes=[pltpu.VMEM((6, tm, LANES), F32)],
        compiler_params=_cparams(PAR, PAR),
    )(dycat, ya, yb, yd, lse_a)
    return outs[0:2], outs[2:4], outs[4:7], outs[7], outs[8]


def _dense_fwd(qd, *, tq, name):
    t = qd.shape[0]
    bl = t // SEQ
    nq = SEQ // tq

    def body(q_ref, k_ref, v_ref, o_ref, l_ref):
        q = q_ref[...]
        for g in range(2):
            h0, h1 = 2 * g, 2 * g + 1
            q2 = jnp.concatenate([q[:, h0 * 64:(h0 + 1) * 64], q[:, h1 * 64:(h1 + 1) * 64]], axis=0)
            kg = k_ref[:, g * 64:(g + 1) * 64]
            vg = v_ref[:, g * 64:(g + 1) * 64]
            s = lax.dot_general(q2, kg, NT, preferred_element_type=F32)
            m = jnp.max(s, axis=1, keepdims=True)
            p = jnp.exp(s - m)
            den = jnp.sum(p, axis=1, keepdims=True)
            o2 = jnp.dot(p.astype(BF16), vg, preferred_element_type=F32) / den
            l2 = jnp.broadcast_to(m + jnp.log(den), (2 * tq, 64))
            o_ref[:, h0 * 64:(h0 + 1) * 64] = o2[:tq]
            o_ref[:, h1 * 64:(h1 + 1) * 64] = o2[tq:]
            l_ref[:, h0 * 64:(h0 + 1) * 64] = l2[:tq]
            l_ref[:, h1 * 64:(h1 + 1) * 64] = l2[tq:]

    q3 = qd.reshape(bl, SEQ, 512)
    o, lse = pl.pallas_call(
        body, name=name, grid=(bl, nq),
        in_specs=[pl.BlockSpec((None, tq, 256), lambda b, i: (b, i, 0)),
                  pl.BlockSpec((None, SEQ, 128), lambda b, i: (b, 0, 2)),
                  pl.BlockSpec((None, SEQ, 128), lambda b, i: (b, 0, 3))],
        out_specs=[pl.BlockSpec((None, tq, 256), lambda b, i: (b, i, 0))] * 2,
        out_shape=[_sds((bl, SEQ, 256), F32)] * 2,
        compiler_params=_cparams(PAR, PAR),
    )(q3, q3, q3)
    return o.reshape(t, 256), lse.reshape(t, 256)


def _dense_bwd(qd, dycat, lse, delta, *, tq, name):
    t = qd.shape[0]
    bl = t // SEQ
    nq = SEQ // tq

    def body(q_ref, k_ref, v_ref, do_ref, l_ref, dl_ref, dq_ref, dk_ref, dv_ref, dkt, dvt):
        @pl.when(pl.program_id(1) == 0)
        def _():
            dkt[...] = jnp.zeros_like(dkt)
            dvt[...] = jnp.zeros_like(dvt)

        q = q_ref[...]
        do = do_ref[...].astype(BF16)
        lv = l_ref[...]
        dlv = dl_ref[...]
        for g in range(2):
            h0, h1 = 2 * g, 2 * g + 1
            q2 = jnp.concatenate([q[:, h0 * 64:(h0 + 1) * 64], q[:, h1 * 64:(h1 + 1) * 64]], axis=0)
            do2 = jnp.concatenate([do[:, h0 * 64:(h0 + 1) * 64], do[:, h1 * 64:(h1 + 1) * 64]], axis=0)
            l2 = jnp.concatenate([lv[:, h0 * 64:h0 * 64 + 1], lv[:, h1 * 64:h1 * 64 + 1]], axis=0)
            dl2 = jnp.concatenate([dlv[:, h0 * 64:h0 * 64 + 1], dlv[:, h1 * 64:h1 * 64 + 1]], axis=0)
            kg = k_ref[:, g * 64:(g + 1) * 64]
            vg = v_ref[:, g * 64:(g + 1) * 64]
            s = lax.dot_general(q2, kg, NT, preferred_element_type=F32)
            p = jnp.exp(s - l2)
            dp = lax.dot_general(do2, vg, NT, preferred_element_type=F32)
            ds = (p * (dp - dl2)).astype(BF16)
            dq2 = jnp.dot(ds, kg, preferred_element_type=F32)
            dq_ref[:, h0 * 64:(h0 + 1) * 64] = dq2[:tq].astype(BF16)
            dq_ref[:, h1 * 64:(h1 + 1) * 64] = dq2[tq:].astype(BF16)
            dkt[g * 64:(g + 1) * 64, :] += lax.dot_general(q2, ds, TN, preferred_element_type=F32)
            dvt[g * 64:(g + 1) * 64, :] += lax.dot_general(do2, p.astype(BF16), TN, preferred_element_type=F32)

        @pl.when(pl.program_id(1) == nq - 1)
        def _():
            dk_ref[...] = dkt[...].T.astype(BF16)
            dv_ref[...] = dvt[...].T.astype(BF16)

    q3 = qd.reshape(bl, SEQ, 512)
    tile = pl.BlockSpec((None, tq, 256), lambda b, i: (b, i, 0))
    full = pl.BlockSpec((None, SEQ, 128), lambda b, i: (b, 0, 0))
    dq, dk, dv = pl.pallas_call(
        body, name=name, grid=(bl, nq),
        in_specs=[tile, pl.BlockSpec((None, SEQ, 128), lambda b, i: (b, 0, 2)),
                  pl.BlockSpec((None, SEQ, 128), lambda b, i: (b, 0, 3)),
                  pl.BlockSpec((None, tq, 256), lambda b, i: (b, i, 3)), tile, tile],
        out_specs=[tile, full, full],
        out_shape=[_sds((bl, SEQ, 256), BF16), _sds((bl, SEQ, 128), BF16), _sds((bl, SEQ, 128), BF16)],
        scratch_shapes=[pltpu.VMEM((128, SEQ), F32), pltpu.VMEM((128, SEQ), F32)],
        compiler_params=_cparams(PAR, ARB),
    )(q3, q3, q3, dycat.reshape(bl, SEQ, 1024), lse.reshape(bl, SEQ, 256), delta.reshape(bl, SEQ, 256))
    return dq.reshape(t, 256), dk.reshape(t, 128), dv.reshape(t, 128)


def _c_norm(cv, gam, bet):
    vg = _gelu(cv)
    mu = jnp.mean(vg, axis=-1, keepdims=True)
    xc = vg - mu
    r = lax.rsqrt(jnp.mean(xc * xc, axis=-1, keepdims=True) + EPS)
    xhat = xc * r
    return xhat * gam + bet, xhat, r


def _c_fwd(proj, gam, bet, ws, bst, *, tm, name):
    t = proj.shape[0]
    nch = tm // C_CHUNK

    def body(u_ref, v_ref, g_ref, b_ref, ws_ref, bs_ref, y_ref):
        vn, _, _ = _c_norm(v_ref[...].astype(F32), g_ref[...], b_ref[...])
        vnb = vn.astype(BF16)
        for c in range(nch):
            rows = slice(c * C_CHUNK, (c + 1) * C_CHUNK)
            for g in range(C_GROUPS):
                gs = slice(g * 64, (g + 1) * 64)
                mixed = jnp.dot(ws_ref[g], vnb[rows, gs], preferred_element_type=F32) + bs_ref[:, gs]
                y_ref[rows, gs] = _gelu(u_ref[rows, gs].astype(F32)) * mixed

    vec = pl.BlockSpec((1, 256), lambda i: (0, 0))
    return pl.pallas_call(
        body, name=name, grid=(t // tm,),
        in_specs=[pl.BlockSpec((tm, 256), lambda i: (i, 5)), pl.BlockSpec((tm, 256), lambda i: (i, 6)), vec, vec,
                  pl.BlockSpec((C_GROUPS, C_CHUNK, C_CHUNK), lambda i: (0, 0, 0)),
                  pl.BlockSpec((C_CHUNK, 256), lambda i: (0, 0))],
        out_specs=pl.BlockSpec((tm, 256), lambda i: (i, 0)), out_shape=_sds((t, 256), F32),
        compiler_params=_cparams(PAR),
    )(proj, proj, gam, bet, ws, bst)


def _c_bwd(proj, dycat, gam, bet, ws, wst, bst, *, tm, name):
    t = proj.shape[0]
    nch = tm // C_CHUNK
    nstep = t // tm

    def body(u_ref, v_ref, dy_ref, g_ref, b_ref, ws_ref, wst_ref, bs_ref,
             du_ref, dv_ref, dws_ref, dbs_ref, dg_ref, db_ref, dvn_s):
        step = pl.program_id(0)

        @pl.when(step == 0)
        def _():
            dws_ref[...] = jnp.zeros_like(dws_ref)
            dbs_ref[...] = jnp.zeros_like(dbs_ref)
            dg_ref[...] = jnp.zeros_like(dg_ref)
            db_ref[...] = jnp.zeros_like(db_ref)

        cv = v_ref[...].astype(F32)
        gam_v = g_ref[...]
        vn, xhat, r = _c_norm(cv, gam_v, b_ref[...])
        vnb = vn.astype(BF16)
        for c in range(nch):
            rows = slice(c * C_CHUNK, (c + 1) * C_CHUNK)
            for g in range(C_GROUPS):
                gs = slice(g * 64, (g + 1) * 64)
                cu = u_ref[rows, gs].astype(F32)
                dy = dy_ref[rows, gs]
                mixed = jnp.dot(ws_ref[g], vnb[rows, gs], preferred_element_type=F32) + bs_ref[:, gs]
                du_ref[rows, gs] = (dy * mixed * _gelu_grad(cu)).astype(BF16)
                dmix = dy * _gelu(cu)
                dbs_ref[:, gs] += dmix
                dmb = dmix.astype(BF16)
                dws_ref[g] += lax.dot_general(dmb, vnb[rows, gs], NT, preferred_element_type=F32)
                dvn_s[rows, gs] = jnp.dot(wst_ref[g], dmb, preferred_element_type=F32)
        dvn = dvn_s[...]
        dg_ref[...] += jnp.sum(dvn * xhat, axis=0, keepdims=True)
        db_ref[...] += jnp.sum(dvn, axis=0, keepdims=True)
        dxh = dvn * gam_v
        dvg = r * (dxh - jnp.mean(dxh, axis=-1, keepdims=True) - xhat * jnp.mean(dxh * xhat, axis=-1, keepdims=True))
        dv_ref[...] = (dvg * _gelu_grad(cv)).astype(BF16)

        @pl.when(step == nstep - 1)
        def _():
            dbs_ref[...] = _seg_sum(dbs_ref[...], _group_sum_matrix(256, True))

    vec = pl.BlockSpec((1, 256), lambda i: (0, 0))
    mat = pl.BlockSpec((C_GROUPS, C_CHUNK, C_CHUNK), lambda i: (0, 0, 0))
    bsp = pl.BlockSpec((C_CHUNK, 256), lambda i: (0, 0))
    tile = pl.BlockSpec((tm, 256), lambda i: (i, 0))
    return pl.pallas_call(
        body, name=name, grid=(nstep,),
        in_specs=[pl.BlockSpec((tm, 256), lambda i: (i, 5)), pl.BlockSpec((tm, 256), lambda i: (i, 6)),
                  pl.BlockSpec((tm, 256), lambda i: (i, 2)), vec, vec, mat, mat, bsp],
        out_specs=[tile, tile, mat, bsp, vec, vec],
        out_shape=[_sds((t, 256), BF16), _sds((t, 256), BF16), _sds((C_GROUPS, C_CHUNK, C_CHUNK), F32),
                   _sds((C_CHUNK, 256), F32), _sds((1, 256), F32), _sds((1, 256), F32)],
        scratch_shapes=[pltpu.VMEM((tm, 256), F32)],
        compiler_params=_cparams(ARB),
    )(proj, proj, dycat, gam, bet, ws, wst, bst)


FF_TC = 128
FF_NB = D_FF // FF_TC
FF_CH = 64
FF_HALO = 16


def _taps(ref, r0, win, where):
    z = jnp.zeros((FF_HALO, win.shape[1]), F32)
    if where == "first":
        win[0:FF_HALO, :] = z
        win[FF_HALO:, :] = ref[0:FF_CH + FF_HALO, :].astype(F32)
    elif where == "last":
        win[0:FF_CH + FF_HALO, :] = ref[SEQ - FF_CH - FF_HALO:SEQ, :].astype(F32)
        win[FF_CH + FF_HALO:, :] = z
    else:
        win[...] = ref[pl.ds(pl.multiple_of(r0 - FF_HALO, FF_HALO), FF_CH + 2 * FF_HALO), :].astype(F32)
    return tuple(win[FF_HALO + o:FF_HALO + o + FF_CH, :] for o in (-1, 0, 1))


def _chunk_loop(step):
    step(0, lambda ref, win: _taps(ref, 0, win, "first"))

    def mid(i, carry):
        r0 = pl.multiple_of(i * FF_CH, FF_CH)
        step(r0, lambda ref, win: _taps(ref, r0, win, "mid"))
        return carry

    lax.fori_loop(1, SEQ // FF_CH - 1, mid, 0)
    step(SEQ - FF_CH, lambda ref, win: _taps(ref, SEQ - FF_CH, win, "last"))


def _conv3(taps, w_ref, b_ref):
    dn, md, up = taps
    return w_ref[0:1, :] * dn + w_ref[1:2, :] * md + w_ref[2:3, :] * up + b_ref[...]


def _ff_specs(order):
    def at(fn):
        return (lambda b, j: fn(b, j)) if order == "bj" else (lambda j, b: fn(b, j))
    hs = [pl.BlockSpec((None, SEQ, FF_TC), at(lambda b, j, o=o: (b, 0, j + o))) for o in (0, FF_NB)]
    ws = [pl.BlockSpec((3, FF_TC), at(lambda b, j, o=o: (0, j + o))) for o in (0, FF_NB)]
    bs = [pl.BlockSpec((1, FF_TC), at(lambda b, j, o=o: (0, j + o))) for o in (0, FF_NB)]
    return hs, ws, bs


def _conv_gate_fwd(h, cw, cb, *, name):
    t = h.shape[0]
    bl = t // SEQ

    def body(hg_ref, hu_ref, wg_ref, wu_ref, bg_ref, bu_ref, a_ref, win):
        def step(r0, taps):
            cg = _conv3(taps(hg_ref, win.at[0]), wg_ref, bg_ref)
            cu = _conv3(taps(hu_ref, win.at[1]), wu_ref, bu_ref)
            a_ref[pl.ds(r0, FF_CH), :] = (cg * _sigmoid(cg) * cu).astype(BF16)

        _chunk_loop(step)

    hs, ws, bs = _ff_specs("bj")
    h3 = h.reshape(bl, SEQ, 2 * D_FF)
    act = pl.pallas_call(
        body, name=name, grid=(bl, FF_NB), in_specs=hs + ws + bs,
        out_specs=pl.BlockSpec((None, SEQ, FF_TC), lambda b, j: (b, 0, j)),
        out_shape=_sds((bl, SEQ, D_FF), BF16),
        scratch_shapes=[pltpu.VMEM((2, FF_CH + 2 * FF_HALO, FF_TC), F32)],
        compiler_params=_cparams(PAR, PAR),
    )(h3, h3, cw, cw, cb, cb)
    return act.reshape(t, D_FF)


def _conv_gate_bwd(h, dact, cw, cb, *, name):
    t = h.shape[0]
    bl = t // SEQ

    def body(hg_ref, hu_ref, wg_ref, wu_ref, bg_ref, bu_ref, da_ref,
             dhg_ref, dhu_ref, dwg_ref, dwu_ref, dbg_ref, dbu_ref, dg_s, du_s, win, sums):
        @pl.when(pl.program_id(1) == 0)
        def _():
            for ref in (dwg_ref, dwu_ref, dbg_ref, dbu_ref):
                ref[...] = jnp.zeros_like(ref)

        sums[...] = jnp.zeros_like(sums)
        red = lambda x: jnp.sum(x.reshape(FF_CH // 8, 8, x.shape[1]), axis=0)

        def pass1(r0, taps):
            tg, tu = taps(hg_ref, win.at[0]), taps(hu_ref, win.at[1])
            cg = _conv3(tg, wg_ref, bg_ref)
            cu = _conv3(tu, wu_ref, bu_ref)
            da = da_ref[pl.ds(r0, FF_CH), :].astype(F32)
            sg = _sigmoid(cg)
            dcg = da * cu * (sg * (1.0 + cg * (1.0 - sg)))
            dcu = da * (cg * sg)
            dg_s[pl.ds(r0, FF_CH), :] = dcg
            du_s[pl.ds(r0, FF_CH), :] = dcu
            for half, (d, tp) in enumerate(((dcg, tg), (dcu, tu))):
                for k in range(3):
                    sums[4 * half + k] += red(d * tp[k])
                sums[4 * half + 3] += red(d)

        _chunk_loop(pass1)
        for half, (dw_ref, db_ref) in enumerate(((dwg_ref, dbg_ref), (dwu_ref, dbu_ref))):
            for k in range(3):
                dw_ref[k:k + 1, :] += jnp.sum(sums[4 * half + k], axis=0, keepdims=True)
            db_ref[...] += jnp.sum(sums[4 * half + 3], axis=0, keepdims=True)

        def pass2(r0, taps):
            for k, (s, w_ref, o_ref) in enumerate(((dg_s, wg_ref, dhg_ref), (du_s, wu_ref, dhu_ref))):
                dn, md, up = taps(s, win.at[k])
                o_ref[pl.ds(r0, FF_CH), :] = (w_ref[0:1, :] * up + w_ref[1:2, :] * md + w_ref[2:3, :] * dn).astype(BF16)

        _chunk_loop(pass2)

    hs, ws, bs = _ff_specs("jb")
    half = pl.BlockSpec((None, SEQ, FF_TC), lambda j, b: (b, 0, j))
    wsp = pl.BlockSpec((3, FF_TC), lambda j, b: (0, j))
    bsp = pl.BlockSpec((1, FF_TC), lambda j, b: (0, j))
    h3 = h.reshape(bl, SEQ, 2 * D_FF)
    dhg, dhu, dwg, dwu, dbg, dbu = pl.pallas_call(
        body, name=name, grid=(FF_NB, bl), in_specs=hs + ws + bs + [half],
        out_specs=[half, half, wsp, wsp, bsp, bsp],
        out_shape=[_sds((bl, SEQ, D_FF), BF16), _sds((bl, SEQ, D_FF), BF16), _sds((3, D_FF), F32), _sds((3, D_FF), F32),
                   _sds((1, D_FF), F32), _sds((1, D_FF), F32)],
        scratch_shapes=[pltpu.VMEM((SEQ, FF_TC), F32), pltpu.VMEM((SEQ, FF_TC), F32),
                        pltpu.VMEM((2, FF_CH + 2 * FF_HALO, FF_TC), F32), pltpu.VMEM((8, 8, FF_TC), F32)],
        compiler_params=_cparams(PAR, ARB),
    )(h3, h3, cw, cw, cb, cb, dact.reshape(bl, SEQ, D_FF))
    return (dhg.reshape(t, D_FF), dhu.reshape(t, D_FF), jnp.concatenate([dwg, dwu], axis=1),
            jnp.concatenate([dbg, dbu], axis=1))


def _ple_fwd(x2, gain, wg, pe, pe_blk, wp, *, tm, name):
    t, k = x2.shape

    def body(x_ref, g_ref, wg_ref, pe_ref, wp_ref, hn_ref, x3_ref, gt_ref, pp_ref):
        x = x_ref[...]
        r = lax.rsqrt(jnp.mean(x * x, axis=-1, keepdims=True) + EPS)
        hn = (x * r * g_ref[...]).astype(BF16)
        hn_ref[...] = hn
        gate = _sigmoid(jnp.dot(hn, wg_ref[...], preferred_element_type=F32))
        pp = jnp.dot(pe_ref[...].astype(BF16), wp_ref[...], preferred_element_type=F32)
        gt_ref[...] = gate.astype(BF16)
        pp_ref[...] = pp.astype(BF16)
        x3_ref[...] = x + pp * gate

    row = pl.BlockSpec((tm, k), lambda i: (i, 0))
    return pl.pallas_call(
        body, name=name, grid=(t // tm,),
        in_specs=[row, pl.BlockSpec((1, k), lambda i: (0, 0)), pl.BlockSpec((k, k), lambda i: (0, 0)),
                  pl.BlockSpec((tm, PLE_DIM), lambda i: (pe_blk + i, 0)), pl.BlockSpec((PLE_DIM, k), lambda i: (0, 0))],
        out_specs=[row, row, row, row],
        out_shape=[_sds((t, k), BF16), _sds((t, k), F32), _sds((t, k), BF16), _sds((t, k), BF16)],
        compiler_params=_cparams(PAR),
    )(x2, gain, wg, pe, wp)


def _ple_bwd_ew(dx3, gate, pp, *, tm, name):
    t, n = dx3.shape

    def body(d_ref, g_ref, p_ref, dz_ref, dpp_ref):
        d, g = d_ref[...], g_ref[...]
        dz_ref[...] = (d * p_ref[...] * g * (1.0 - g)).astype(BF16)
        dpp_ref[...] = (d * g).astype(BF16)

    spec = pl.BlockSpec((tm, n), lambda i: (i, 0))
    return pl.pallas_call(
        body, name=name, grid=(t // tm,), in_specs=[spec] * 3, out_specs=[spec] * 2,
        out_shape=[_sds((t, n), BF16)] * 2, compiler_params=_cparams(PAR),
    )(dx3, gate, pp)


def _loss_head(y, tgt, *, tm, name):
    t, d = y.shape

    def body(y_ref, t_ref, l_ref, dy_ref):
        @pl.when(pl.program_id(0) == 0)
        def _():
            l_ref[...] = jnp.zeros_like(l_ref)

        e = y_ref[...] - t_ref[...]
        dy_ref[...] = e * (1.0 / d)
        s = jnp.sum(jnp.sum(e * e, axis=1, keepdims=True), axis=0, keepdims=True)
        l_ref[...] += jnp.broadcast_to(s * (0.5 / d), (8, 128))

    spec = pl.BlockSpec((tm, d), lambda i: (i, 0))
    return pl.pallas_call(
        body, name=name, grid=(t // tm,), in_specs=[spec, spec],
        out_specs=[pl.BlockSpec((8, 128), lambda i: (0, 0)), spec],
        out_shape=[_sds((8, 128), F32), _sds((t, d), F32)], compiler_params=_cparams(ARB),
    )(y, tgt)


BIAS_PC = 8192


def _onehot(bucket_row):
    rows = lax.broadcasted_iota(jnp.int32, (REL_BUCKETS, bucket_row.shape[1]), 0)
    return (rows == bucket_row).astype(BF16)


def _dot3(x, onehot, dims):
    acc = None
    for _ in range(3):
        term = x.astype(BF16)
        part = lax.dot_general(term, onehot, dims, preferred_element_type=F32)
        acc = part if acc is None else acc + part
        x = x - term.astype(F32)
    return acc


def _bias_lookup(table_t, bucket, *, name):
    h = table_t.shape[0]
    p = bucket.shape[1]

    def body(t_ref, b_ref, o_ref):
        bk = b_ref[...]
        val = _dot3(t_ref[...], _onehot(bk), (((1,), (0,)), ((), ())))
        o_ref[...] = jnp.where(bk >= 0, val, NEG_INF)

    return pl.pallas_call(
        body, name=name, grid=(p // BIAS_PC,),
        in_specs=[pl.BlockSpec((h, REL_BUCKETS), lambda i: (0, 0)), pl.BlockSpec((1, BIAS_PC), lambda i: (0, i))],
        out_specs=pl.BlockSpec((h, BIAS_PC), lambda i: (0, i)), out_shape=_sds((h, p), F32),
        compiler_params=_cparams(PAR),
    )(table_t, bucket)


def _bucket_reduce(dbiases, bucket, *, name):
    h, p = dbiases[0].shape
    nl = len(dbiases)

    def body(*refs):
        b_ref, o_ref = refs[nl], refs[nl + 1]

        @pl.when(pl.program_id(0) == 0)
        def _():
            o_ref[...] = jnp.zeros_like(o_ref)

        d = refs[0][...]
        for d_ref in refs[1:nl]:
            d = d + d_ref[...]
        o_ref[...] += _dot3(d, _onehot(b_ref[...]), NT)

    return pl.pallas_call(
        body, name=name, grid=(p // BIAS_PC,),
        in_specs=[pl.BlockSpec((h, BIAS_PC), lambda i: (0, i))] * nl + [pl.BlockSpec((1, BIAS_PC), lambda i: (0, i))],
        out_specs=pl.BlockSpec((h, REL_BUCKETS), lambda i: (0, 0)), out_shape=_sds((h, REL_BUCKETS), F32),
        compiler_params=_cparams(ARB),
    )(*dbiases, bucket)


def _adamw_math(w, g, m, v):
    m = ADAM_B1 * m + (1.0 - ADAM_B1) * g
    v = ADAM_B2 * v + (1.0 - ADAM_B2) * (g * g)
    m_hat = m / (1.0 - ADAM_B1 ** ADAM_STEP)
    v_hat = v / (1.0 - ADAM_B2 ** ADAM_STEP)
    delta = -ADAM_LR * (m_hat / (jnp.sqrt(v_hat) + ADAM_EPS) + ADAM_WD * w)
    return delta, m, v


def _adamw_reduce(parts, w, m, v, *, tr, name):
    nl = len(parts)
    rows, c = w.shape
    r = rows // nl
    nt = r // tr

    def body(*refs):
        p_refs = refs[:nl]
        w_ref, m_ref, v_ref, g_ref, d_ref, nm_ref, nv_ref = refs[nl:]
        for li, p_ref in enumerate(p_refs):
            @pl.when(pl.program_id(0) == li)
            def _(p_ref=p_ref):
                g = p_ref[0].astype(F32)
                for k in range(1, N_DEV):
                    g = g + p_ref[k].astype(F32)
                d, nm, nv = _adamw_math(w_ref[...], g, m_ref[...], v_ref[...])
                g_ref[...] = g
                d_ref[...] = d
                nm_ref[...] = nm
                nv_ref[...] = nv

    def part_map(li):
        return lambda l, i: (0, jnp.where(l == li, i, jnp.where(l < li, 0, nt - 1)), 0)

    spec = pl.BlockSpec((tr, c), lambda l, i: (l * nt + i, 0))
    return pl.pallas_call(
        body, name=name, grid=(nl, nt),
        in_specs=[pl.BlockSpec((N_DEV, tr, c), part_map(li)) for li in range(nl)] + [spec, spec, spec],
        out_specs=[spec] * 4, out_shape=[_sds((rows, c), F32)] * 4, compiler_params=_cparams(ARB, ARB),
    )(*parts, w, m, v)


def _adamw_plain(g, w, m, v, *, name):
    def body(g_ref, w_ref, m_ref, v_ref, d_ref, nm_ref, nv_ref):
        d, nm, nv = _adamw_math(w_ref[...], g_ref[...], m_ref[...], v_ref[...])
        d_ref[...] = d
        nm_ref[...] = nm
        nv_ref[...] = nv

    return pl.pallas_call(body, name=name, out_shape=[_sds(w.shape, F32)] * 3)(g, w, m, v)


def _mesh_pos():
    return lax.axis_index("x"), lax.axis_index("y"), lax.axis_index("c")


def _allgather_body(x_refs, out_refs, send_sems, recv_sems, local_sems, slot):
    x, y, c = _mesh_pos()
    me, sibling = (x, y, c), (x, y, 1 - c)
    chips = [(1 - x, y), (x, 1 - y), (1 - x, 1 - y)]
    waits = []
    for a, (x_ref, out_ref) in enumerate(zip(x_refs, out_refs)):
        def copy(k, block, to, src=None, out_ref=out_ref, a=a):
            return pltpu.make_async_remote_copy(
                src_ref=slot(out_ref, block) if src is None else src, dst_ref=slot(out_ref, block),
                send_sem=send_sems.at[a, k], recv_sem=recv_sems.at[a, k], device_id=to, device_id_type=MESH)

        mine = pltpu.make_async_copy(x_ref, slot(out_ref, me), local_sems.at[a])
        mine.start()
        first = [copy(0, me, sibling, src=x_ref)]
        first += [copy(1 + j, me, (*chip, c), src=x_ref) for j, chip in enumerate(chips)]
        for cp in first:
            cp.start()
        waits.append((copy, mine, first))
    sends = []
    for copy, mine, first in waits:
        passed = [copy(4 + j, (*chip, c), sibling) for j, chip in enumerate(chips)]
        for j, chip in enumerate(chips):
            copy(1 + j, (*chip, c), me).wait_recv()
            passed[j].start()
        sends.append(passed)
    for (copy, mine, first), passed in zip(waits, sends):
        copy(0, sibling, me).wait_recv()
        for j, chip in enumerate(chips):
            copy(4 + j, (*chip, 1 - c), me).wait_recv()
        for cp in first + passed:
            cp.wait_send()
        mine.wait()


PEER_FLIPS = ((0, 0, 1), (1, 0, 0), (0, 1, 0), (1, 1, 0), (1, 0, 1), (0, 1, 1), (1, 1, 1))


def _peer_copies(x_refs, land_refs, send_sem, recv_sem, scatter):
    x, y, c = _mesh_pos()
    me = 4 * x + 2 * y + c
    copies = []
    for x_ref, land_ref in zip(x_refs, land_refs):
        for fx, fy, fc in PEER_FLIPS:
            px, py, pc = x ^ fx, y ^ fy, c ^ fc
            src = x_ref.at[4 * px + 2 * py + pc] if scatter else x_ref
            copies.append(pltpu.make_async_remote_copy(
                src_ref=src, dst_ref=land_ref.at[me], send_sem=send_sem, recv_sem=recv_sem,
                device_id=(px, py, pc), device_id_type=MESH))
    return copies


def _sc_exchange(xs, *, scatter, collective_id, name):
    na = len(xs)
    land_shapes = [x.shape if scatter else (N_DEV,) + x.shape for x in xs]

    def body(*refs):
        x_refs, land_refs = refs[:na], refs[na:2 * na]
        send_sem, recv_sem, local_sem = refs[2 * na:]
        x, y, c = _mesh_pos()
        me = 4 * x + 2 * y + c
        barrier = pltpu.get_barrier_semaphore()
        for fx, fy, fc in PEER_FLIPS:
            pl.semaphore_signal(barrier, inc=1, device_id=(x ^ fx, y ^ fy, c ^ fc), device_id_type=MESH)
        pl.semaphore_wait(barrier, len(PEER_FLIPS))
        for x_ref, land_ref in zip(x_refs, land_refs):
            own = pltpu.make_async_copy(x_ref.at[me] if scatter else x_ref, land_ref.at[me], local_sem)
            own.start()
            own.wait()
        copies = _peer_copies(x_refs, land_refs, send_sem, recv_sem, scatter)
        for cp in copies:
            cp.start()
        for cp in copies:
            cp.wait()

    return pl.kernel(
        body, name=name, out_type=[_sds(s, x.dtype) for s, x in zip(land_shapes, xs)],
        mesh=plsc.ScalarSubcoreMesh(axis_name="sequencer", num_cores=1),
        scratch_types=[pltpu.SemaphoreType.DMA, pltpu.SemaphoreType.DMA, pltpu.SemaphoreType.DMA],
        compiler_params=pltpu.CompilerParams(collective_id=collective_id),
    )(*xs)


def _sc_allgather(xs, *, collective_id, name):
    na = len(xs)

    def body(*refs):
        x_refs, out_refs = refs[:na], refs[na:2 * na]
        send_sems, recv_sems, local_sems = refs[2 * na:]
        x, y, c = _mesh_pos()
        barrier = pltpu.get_barrier_semaphore()
        for fx, fy, fc in PEER_FLIPS:
            pl.semaphore_signal(barrier, inc=1, device_id=(x ^ fx, y ^ fy, c ^ fc), device_id_type=MESH)
        pl.semaphore_wait(barrier, len(PEER_FLIPS))
        _allgather_body(x_refs, out_refs, send_sems, recv_sems, local_sems,
                        lambda ref, pos: ref.at[4 * pos[0] + 2 * pos[1] + pos[2]])

    return pl.kernel(
        body, name=name, out_type=[_sds((N_DEV,) + x.shape, x.dtype) for x in xs],
        mesh=plsc.ScalarSubcoreMesh(axis_name="sequencer", num_cores=1),
        scratch_types=[pltpu.SemaphoreType.DMA((na, 7)), pltpu.SemaphoreType.DMA((na, 7)),
                       pltpu.SemaphoreType.DMA((na,))],
        compiler_params=pltpu.CompilerParams(collective_id=collective_id),
    )(*xs)


def _allgather_vmem(x, *, name):
    r, c = x.shape

    def body(x_ref, out_ref, send_sems, recv_sems, local_sems):
        _allgather_body([x_ref], [out_ref], send_sems, recv_sems, local_sems,
                        lambda ref, pos: ref.at[pl.ds((4 * pos[0] + 2 * pos[1] + pos[2]) * r, r), :])

    vm = pl.BlockSpec(memory_space=pltpu.VMEM)
    return pl.pallas_call(
        body, name=name, in_specs=[vm], out_specs=vm, out_shape=_sds((N_DEV * r, c), x.dtype),
        scratch_shapes=[pltpu.SemaphoreType.DMA((1, 7)), pltpu.SemaphoreType.DMA((1, 7)),
                        pltpu.SemaphoreType.DMA((1,))],
    )(x)


def _sum_slots(gathered, *, name):
    _, r, c = gathered.shape

    def body(g_ref, o_ref):
        acc = g_ref[0]
        for k in range(1, N_DEV):
            acc = acc + g_ref[k]
        o_ref[...] = acc

    return pl.pallas_call(body, name=name, out_shape=_sds((r, c), gathered.dtype))(gathered)


def _t5_bucket(rel):
    nb = REL_BUCKETS // 2
    ret = jnp.where(rel > 0, nb, 0)
    n = jnp.abs(rel)
    max_exact = nb // 2
    nf = jnp.maximum(n, 1).astype(F32)
    large = max_exact + (jnp.log(nf / max_exact) / math.log(REL_MAX_DIST / max_exact)
                         * (nb - max_exact)).astype(jnp.int32)
    large = jnp.minimum(large, nb - 1)
    return ret + jnp.where(n < max_exact, n, large)


def _band_pattern(block, radius, dil):
    kw = block + 2 * radius
    rel = jnp.arange(kw)[None, :] - radius - jnp.arange(block)[:, None]
    return jnp.where(jnp.abs(rel) <= radius, _t5_bucket(rel * dil), -1).astype(jnp.int32).reshape(1, block * kw)


def _rope_tables():
    lane = np.arange(64)
    seg, j = lane // 32, lane % 32
    inv = ROPE_THETA ** (-jnp.arange(0, 32, 2, dtype=F32) / 32)
    tpos = jnp.arange(SEQ)
    pos = jnp.where(jnp.asarray(seg)[None, :] == 0, (tpos // GRID_W)[:, None], (tpos % GRID_W)[:, None])
    ang = pos.astype(F32) * inv[jnp.asarray(j % 16)][None, :]
    cos = jnp.cos(ang)
    sins = jnp.where(jnp.asarray(j)[None, :] < 16, -jnp.sin(ang), jnp.sin(ang))
    return jnp.tile(cos, (1, 4)), jnp.tile(sins, (1, 4))


A_Q, A_K, A_V = (256, 0), (256, 1), (256, 2)
B_Q, B_K, B_V = (256, 0), (128, 2), (128, 3)
A_HEADS = dict(rad=A_RADIUS, nh=4, nkv=4)
B_HEADS = dict(rad=SWA_RADIUS, nh=4, nkv=2)


def _local_step(x, pe, tgt, rel_bias, wts, matmul_weights, grads_ready):
    t = x.shape[0]
    bl = t // SEQ
    cos, sins = _rope_tables()
    blocks_a = [min(BAND_BLOCK, SEQ // d) for d in DILATIONS]
    pats_a = [_band_pattern(blk, A_RADIUS, d) for blk, d in zip(blocks_a, DILATIONS)]
    pat_b = _band_pattern(BAND_BLOCK, SWA_RADIUS, 1)
    table_t = rel_bias.T
    bias_a = [_bias_lookup(table_t[:4], pt, name=f"bias_a{ci}").reshape(4, blk, blk + 2 * A_RADIUS)
              for ci, (pt, blk) in enumerate(zip(pats_a, blocks_a))]
    bias_b = _bias_lookup(table_t[4:], pat_b, name="bias_b").reshape(4, BAND_BLOCK, BAND_BLOCK + 2 * SWA_RADIUS)
    nat4 = lambda a: a.reshape(bl, 1, SEQ, a.shape[-1])

    saved = []
    for li in range(DEPTH):
        w = dict(wts[li])
        w.update(matmul_weights(li, "in", x))
        hn0, proj = _norm_mm((x,), w["g_mix"], w["w_in"], None, tm=1024, tn=1152, name="mix_in_fwd", out_dtype=BF16)
        qa1, qa4, qa16, qb, qd = _qkprep_fwd(proj, w["qk_gains"], cos, sins, tm=512, name="qkprep_fwd")
        qa = (nat4(qa1), qa4, qa16)
        oa, la = [], []
        for ci in range(3):
            o, l = _band_fwd(qa[ci], A_Q, A_K, A_V, bias_a[ci], None, name=f"band_a{ci}_fwd", **A_HEADS)
            oa.append(o)
            la.append(l)
        oa[0], la[0] = oa[0].reshape(t, 256), la[0].reshape(t, 256)
        ya, lse_a = _combine_a(oa, la, tm=512, name="combine_a")
        yb, lse_b = _band_fwd(nat4(qb), B_Q, B_K, B_V, bias_b, w["sink_t"], name="band_b_fwd", **B_HEADS)
        yb = yb.reshape(t, 256)
        yc = _c_fwd(proj, w["c_g"], w["c_b"], w["c_ws"], w["c_bst"], tm=512, name="c_fwd")
        yd, lse_d = _dense_fwd(qd, tq=256, name="dense_fwd")
        w.update(matmul_weights(li, "rest", yd))
        mixed, x1 = _norm_mm((ya, yb, yc, yd), w["out_gain"], w["w_out"], x, tm=1024, tn=1024, name="mix_out_fwd")
        hn1, h = _norm_mm((x1,), w["g_ffn"], w["w_up"], None, tm=1024, tn=1408, name="ffn_up_fwd", out_dtype=BF16)
        act = _conv_gate_fwd(h, w["conv_w"], w["conv_b"], name="conv_gate_fwd")
        x2 = _mm(act, w["w_down"], "nn", x1, tm=1024, tn=1024, out_dtype=F32, name="ffn_down_fwd")
        hn2, x3, gate, pp = _ple_fwd(x2, w["g_ple"], w["w_gate"], pe, li * (t // 1024), w["w_proj"], tm=1024,
                                     name="ple_fwd")
        saved.append(dict(w=w, x0=x, hn0=hn0, proj=proj, qa=qa, qb=qb, qd=qd, ya=ya, lse_a=lse_a, yb=yb, lse_b=lse_b,
                          yc=yc, yd=yd, lse_d=lse_d, mixed=mixed, x1=x1, hn1=hn1, h=h, act=act, x2=x2, hn2=hn2,
                          gate=gate, pp=pp))
        x = x3

    loss_tile, dx = _loss_head(x, tgt, tm=512, name="loss_head")
    grads = [None] * DEPTH
    dbias_a, dbias_bs = [[], [], []], []
    for li in reversed(range(DEPTH)):
        s = saved[li]
        w = s["w"]
        g = {}
        dz, dpp = _ple_bwd_ew(dx, s["gate"], s["pp"], tm=512, name="ple_bwd_ew")
        g["w_gate"] = _mm(s["hn2"], dz, "tn", None, tm=1024, tn=512, out_dtype=BF16, name="dw_gate")
        g["w_proj"] = _mm(pe, dpp, "tn", None, tm=256, tn=1024, out_dtype=BF16, name="dw_proj", a_rows=(li, t))
        dx2, dx2b, g["g_ple"] = _mm_bt_normbwd((dz,), w["w_gate"], (s["x2"],), w["g_ple"], dx, tm=1024, tn=1024,
                                               name="ple_bwd", emit_bf16=True)
        g["w_down"] = _mm(s["act"], dx2b, "tn", None, tm=1408, tn=512, out_dtype=BF16, name="dw_down")
        dact = _mm(dx2b, w["w_down"], "nt", None, tm=1024, tn=1408, out_dtype=BF16, name="ffn_down_bwd")
        dhg, dhu, g["conv_w"], g["conv_b"] = _conv_gate_bwd(s["h"], dact, w["conv_w"], w["conv_b"], name="conv_gate_bwd")
        g["w_up"] = jnp.concatenate(
            [_mm(s["hn1"], dhalf, "tn", None, tm=1024, tn=1408, out_dtype=BF16, name=f"dw_up_{nm}")
             for nm, dhalf in (("gate", dhg), ("up", dhu))], axis=1)
        dx1, dx1b, g["g_ffn"] = _mm_bt_normbwd((dhg, dhu), w["w_up"], (s["x1"],), w["g_ffn"], dx2, tm=1024, tn=1408,
                                               name="ffn_up_bwd", emit_bf16=True)
        g["w_out"] = _mm(s["mixed"], dx1b, "tn", None, tm=1024, tn=512, out_dtype=BF16, name="dw_out")
        grads_ready(li, "mid", g)
        dycat, g["out_gain"] = _mm_bt_normbwd((dx1b,), w["w_out"], (s["ya"], s["yb"], s["yc"], s["yd"]), w["out_gain"],
                                              None, tm=1024, tn=1024, name="mix_out_bwd")
        dy_r, lse_r, dl_a, dl_b, dl_d = _deltas(dycat, s["ya"], s["yb"], s["yd"], s["lse_a"], tm=512, name="deltas")
        dy_a = (nat4(dycat),) + tuple(dy_r)
        lse_a = (nat4(s["lse_a"]),) + tuple(lse_r)
        dl_a = (nat4(dl_a[0]),) + tuple(dl_a[1:])
        da = []
        for ci in range(3):
            dq, dk, dv, dbias = _band_bwd(s["qa"][ci], A_Q, A_K, A_V, bias_a[ci], None, dy_a[ci], 0, lse_a[ci],
                                          dl_a[ci], name=f"band_a{ci}_bwd", **A_HEADS)
            if ci == 0:
                dq, dk, dv = (a.reshape(t, 256) for a in (dq, dk, dv))
            da.append((dq, dk, dv))
            dbias_a[ci].append(dbias.reshape(4, -1))
        dqb, dkb, dvb, dbias_b, dsink = _band_bwd(nat4(s["qb"]), B_Q, B_K, B_V, bias_b, w["sink_t"], nat4(dycat), 1,
                                                  nat4(s["lse_b"]), nat4(dl_b), name="band_b_bwd", **B_HEADS)
        dbias_bs.append(dbias_b.reshape(4, -1))
        g["sink"] = dsink[:, 0, 0]
        dd = _dense_bwd(s["qd"], dycat, s["lse_d"], dl_d, tq=256, name="dense_bwd")
        dcu, dcv, g["c_ws"], dbs, g["c_g"], g["c_b"] = _c_bwd(s["proj"], dycat, w["c_g"], w["c_b"], w["c_ws"],
                                                               w["c_wst"], w["c_bst"], tm=512, name="c_bwd")
        g["c_bs"] = dbs[:, ::64].T
        db = (dqb.reshape(t, 256), dkb.reshape(t, 128), dvb.reshape(t, 128))
        dproj, dgains = _qkprep_bwd(s["proj"], da, db, dd, dcu, dcv, w["qk_gains"], cos, sins, tm=512, name="qkprep_bwd")
        g["qk_gain"] = dgains[:6, :64].reshape(3, 2, HEAD_DIM)
        g["w_in"] = _mm(s["hn0"], dproj, "tn", None, tm=1024, tn=1152, out_dtype=BF16, name="dw_in")
        dx, g["g_mix"] = _mm_bt_normbwd((dproj,), w["w_in"], (s["x0"],), w["g_mix"], dx1, tm=1024, tn=1152,
                                        name="mix_in_bwd")
        grads[li] = g
        grads_ready(li, "end", g)
    d_table_a = sum(_bucket_reduce(dbias_a[ci], pats_a[ci], name=f"bucket_a{ci}") for ci in range(3))
    d_table_b = _bucket_reduce(dbias_bs, pat_b, name="bucket_b")
    d_rel_bias = jnp.concatenate([d_table_a, d_table_b], axis=0).T
    return loss_tile[0, 0], dx, grads, d_rel_bias


WEIGHT_NAMES = ("rel_bias", "ln_mix_g", "w_in", "qk_gain", "sink", "c_norm_g", "c_norm_b", "c_ws", "c_bs", "out_gain",
                "w_out", "ln_ffn_g", "w_up", "conv_w", "conv_b", "w_down", "ln_ple_g", "w_ple_gate", "w_ple_proj")
COL_SHARDED = ("w_in", "w_up", "w_ple_proj")
ROW_SHARDED = ("w_out", "w_down", "w_ple_gate")
SMALL_SHARDED = ("conv_w", "out_gain")
REPLICATED = tuple(n for n in WEIGHT_NAMES if n not in COL_SHARDED + ROW_SHARDED + SMALL_SHARDED)
LOCAL_GRAD_KEY = {"ln_mix_g": "g_mix", "ln_ffn_g": "g_ffn", "ln_ple_g": "g_ple", "c_norm_g": "c_g", "c_norm_b": "c_b",
                  "w_ple_gate": "w_gate", "w_ple_proj": "w_proj"}


def _full_from_gathered(name, gathered):
    _, r, c = gathered.shape
    if name in ROW_SHARDED:
        return gathered.reshape(N_DEV * r, c)
    return jnp.transpose(gathered, (1, 0, 2)).reshape(r, N_DEV * c)


def _slots_from_full(name, full):
    rows, cols = full.shape
    if name in ROW_SHARDED:
        return full.reshape(N_DEV, rows // N_DEV, cols)
    return jnp.transpose(full.reshape(rows, N_DEV, cols // N_DEV), (1, 0, 2))


def _piece_rows(shape):
    return -(-int(np.prod(shape)) // 1024) * 8


def _pack_rows(arrays):
    pieces = []
    for a in arrays:
        n, rows = int(np.prod(a.shape)), _piece_rows(a.shape)
        flat = a.astype(F32).reshape(-1)
        if n != rows * LANES:
            flat = jnp.pad(flat, (0, rows * LANES - n))
        pieces.append(flat.reshape(rows, LANES))
    return jnp.concatenate(pieces, axis=0)


def _unpack_rows(packed, shapes):
    out, off = [], 0
    for shp in shapes:
        n, rows = int(np.prod(shp)), _piece_rows(shp)
        piece = packed[off:off + rows]
        out.append((piece if n == rows * LANES else piece.reshape(-1)[:n]).reshape(shp))
        off += rows
    return out


def kernel(x, p, rel_bias, ln_mix_g, w_in, qk_gain, sink, c_norm_g, c_norm_b, c_ws, c_bs, out_gain, w_out, ln_ffn_g, w_up, conv_w, conv_b, w_down, ln_ple_g, w_ple_gate, w_ple_proj, loss_target, m_rel_bias, m_ln_mix_g, m_w_in, m_qk_gain, m_sink, m_c_norm_g, m_c_norm_b, m_c_ws, m_c_bs, m_out_gain, m_w_out, m_ln_ffn_g, m_w_up, m_conv_w, m_conv_b, m_w_down, m_ln_ple_g, m_w_ple_gate, m_w_ple_proj, v_rel_bias, v_ln_mix_g, v_w_in, v_qk_gain, v_sink, v_c_norm_g, v_c_norm_b, v_c_ws, v_c_bs, v_out_gain, v_w_out, v_ln_ffn_g, v_w_up, v_conv_w, v_conv_b, v_w_down, v_ln_ple_g, v_w_ple_gate, v_w_ple_proj):
    env = dict(locals())
    wt = {n: env[n] for n in WEIGHT_NAMES}
    mom_m = {n: env["m_" + n] for n in WEIGHT_NAMES}
    mom_v = {n: env["v_" + n] for n in WEIGHT_NAMES}
    bl = x.shape[0]
    t = bl * SEQ
    me = 4 * lax.axis_index("x") + 2 * lax.axis_index("y") + lax.axis_index("c")

    big = COL_SHARDED + ROW_SHARDED
    full = {}
    small_shapes = [wt[n].shape for n in SMALL_SHARDED]
    small = _allgather_vmem(_pack_rows([wt[n] for n in SMALL_SHARDED]), name="gather_small")
    small = small.reshape(N_DEV, -1)
    off = 0
    for n, shp in zip(SMALL_SHARDED, small_shapes):
        cnt = int(np.prod(shp))
        g = small[:, off:off + cnt].reshape((N_DEV,) + tuple(shp))
        full[n] = jnp.transpose(g, (1, 2, 0, 3)).reshape(shp[0], shp[1], N_DEV * shp[2])
        off += _piece_rows(shp) * LANES

    def head_gain(li, a, b, reps):
        g = jnp.tile(qk_gain[li, a, b], reps)
        return jnp.pad(g, (0, 256 - g.shape[0]))

    wts = []
    for li in range(DEPTH):
        rows = [head_gain(li, 0, 0, 4), head_gain(li, 0, 1, 4), head_gain(li, 1, 0, 4), head_gain(li, 1, 1, 2),
                head_gain(li, 2, 0, 4), head_gain(li, 2, 1, 2), jnp.zeros((256,), F32), jnp.zeros((256,), F32)]
        wts.append(dict(
            g_mix=ln_mix_g[li].reshape(1, -1), qk_gains=jnp.stack(rows),
            sink_t=jnp.broadcast_to(sink[li][:, None, None], (4, 8, 128)),
            c_g=c_norm_g[li].reshape(1, -1), c_b=c_norm_b[li].reshape(1, -1), c_ws=c_ws[li].astype(BF16),
            c_wst=jnp.transpose(c_ws[li], (0, 2, 1)).astype(BF16), c_bst=jnp.repeat(c_bs[li].T, 64, axis=1),
            out_gain=full["out_gain"][li].reshape(1, -1), g_ffn=ln_ffn_g[li].reshape(1, -1),
            conv_w=full["conv_w"][li], conv_b=conv_b[li].reshape(1, -1), g_ple=ln_ple_g[li].reshape(1, -1)))

    local_key = {"w_ple_gate": "w_gate", "w_ple_proj": "w_proj"}

    gather_names = {"in": ("w_in",), "rest": tuple(n for n in big if n != "w_in")}
    gathered = {}
    for cid, (li, names) in enumerate(((0, gather_names["in"]), (0, gather_names["rest"]), (1, big))):
        lands = _sc_allgather([wt[n][li].astype(BF16) for n in names], collective_id=cid,
                              name=f"gather_{li}_{len(names)}")
        gathered.setdefault(li, {}).update(zip(names, lands))

    def matmul_weights(li, part, after):
        out = {}
        for n in gather_names[part]:
            g, _ = lax.optimization_barrier((gathered[li][n], after))
            out[local_key.get(n, n)] = _full_from_gathered(n, g)
        return out

    mid_names = ("w_ple_gate", "w_ple_proj", "w_down", "w_up", "w_out")
    end_names = ("w_in",)
    landed = {}

    def start_exchange(li, names, g, tag, cid):
        slots = [_slots_from_full(n, g[local_key.get(n, n)]) for n in names]
        lands = _sc_exchange(slots, scatter=True, collective_id=cid, name=f"grads_{li}_{tag}")
        landed.update({(n, li): land for n, land in zip(names, lands)})

    def grads_ready(li, stage, g):
        if li == 0:
            start_exchange(li, mid_names if stage == "mid" else end_names, g, stage, 5 if stage == "mid" else 6)
        elif stage == "end":
            start_exchange(li, mid_names + end_names, g, stage, 4)

    loss_part, dx, grads, d_rel_bias = _local_step(
        x.reshape(t, D_MODEL), p.reshape(DEPTH * t, PLE_DIM), loss_target.reshape(t, D_MODEL), rel_bias, wts,
        matmul_weights, grads_ready)
    loss = lax.psum(loss_part, ("x", "y", "c"))

    def local_grad(n):
        if n == "rel_bias":
            return d_rel_bias
        key = LOCAL_GRAD_KEY.get(n, n)
        return jnp.stack([grads[li][key].reshape(wt[n].shape[1:]) if n in REPLICATED else grads[li][key]
                          for li in range(DEPTH)])

    small_names = REPLICATED + SMALL_SHARDED
    small_full_shapes = [wt[n].shape if n in REPLICATED else full[n].shape for n in small_names]
    small_parts = _allgather_vmem(_pack_rows([local_grad(n) for n in small_names]), name="allgather_small_grads")
    small_parts = small_parts.reshape(N_DEV, -1, LANES)

    out_g, out_d, out_m, out_v = {}, {}, {}, {}
    for n in big:
        shp = wt[n].shape
        two_d = lambda a: a.reshape(-1, shp[-1])
        res = _adamw_reduce([landed[n, li] for li in range(DEPTH)], two_d(wt[n]), two_d(mom_m[n]), two_d(mom_v[n]),
                            tr=32 if n == "w_down" else 128, name="adamw_" + n)
        out_g[n], out_d[n], out_m[n], out_v[n] = [r.reshape(shp) for r in res]

    reduced = _sum_slots(small_parts, name="sum_small_grads")
    reduced = dict(zip(small_names, _unpack_rows(reduced, small_full_shapes)))
    rep_shapes = [wt[n].shape for n in REPLICATED]
    upd = _adamw_plain(_pack_rows([reduced[n] for n in REPLICATED]), _pack_rows([wt[n] for n in REPLICATED]),
                       _pack_rows([mom_m[n] for n in REPLICATED]), _pack_rows([mom_v[n] for n in REPLICATED]),
                       name="adamw_replicated")
    for dst, packed in zip((out_d, out_m, out_v), upd):
        dst.update(zip(REPLICATED, _unpack_rows(packed, rep_shapes)))
    for n in REPLICATED:
        out_g[n] = reduced[n]
    for n in SMALL_SHARDED:
        shp = wt[n].shape
        g = reduced[n].reshape(shp[0], shp[1], N_DEV, shp[2])
        g = lax.dynamic_index_in_dim(g, me, axis=2, keepdims=False)
        two_d = lambda a: a.reshape(-1, shp[-1])
        res = _adamw_plain(two_d(g), two_d(wt[n]), two_d(mom_m[n]), two_d(mom_v[n]), name="adamw_" + n)
        out_g[n] = g
        out_d[n], out_m[n], out_v[n] = [r.reshape(shp) for r in res]

    return (loss, dx.reshape(bl, SEQ, D_MODEL), *[out_g[n] for n in WEIGHT_NAMES], *[out_d[n] for n in WEIGHT_NAMES],
            *[out_m[n] for n in WEIGHT_NAMES], *[out_v[n] for n in WEIGHT_NAMES])
```

```python
import math

import jax
import jax.numpy as jnp
import numpy as np
from jax import lax
from jax.experimental import pallas as pl
from jax.experimental.pallas import tpu as pltpu
from jax.experimental.pallas import tpu_sc as plsc

F32 = jnp.float32
BF16 = jnp.bfloat16
HI = lax.Precision.HIGHEST

N_DEV = 8
D_MODEL = 1024
SEQ = 2048
DEPTH = 2
HEAD_DIM = 64
IN_WIDTH = 2304
D_FF = 2816
PLE_DIM = 256
C_CHUNK = 128
C_GROUPS = 4
DILATED_CFGS = ((128, 1), (512, 4), (2048, 16))
DILATIONS = tuple(d for _, d in DILATED_CFGS)
A_RADIUS = 64
SWA_RADIUS = 128
BAND_BLOCK = 256
GRID_W = 64
ROPE_THETA = 10000.0
REL_BUCKETS = 32
REL_MAX_DIST = 1024
EPS = 1e-6
NEG_INF = -1e30
ATTN_SCALE = HEAD_DIM ** -0.5
LANES = 128

ADAM_LR = 0.001
ADAM_B1 = 0.9
ADAM_B2 = 0.999
ADAM_EPS = 1e-08
ADAM_WD = 0.01
ADAM_STEP = 10

MESH = pl.DeviceIdType.MESH
NT = (((1,), (1,)), ((), ()))
TN = (((0,), (0,)), ((), ()))
ARB = "arbitrary"
PAR = "parallel"


def _cparams(*sem):
    return pltpu.CompilerParams(dimension_semantics=tuple(sem))


def _sds(shape, dtype):
    return jax.ShapeDtypeStruct(tuple(shape), dtype)


def _group_sum_matrix(n, same_group):
    r = lax.broadcasted_iota(jnp.int32, (n, n), 0)
    c = lax.broadcasted_iota(jnp.int32, (n, n), 1)
    if same_group:
        return ((r >> 6) == (c >> 6)).astype(F32)
    return ((r & 63) == (c & 63)).astype(F32)


def _seg_sum(x, e):
    eb = e.astype(BF16)
    hi = x.astype(BF16)
    lo = (x - hi.astype(F32)).astype(BF16)
    return jnp.dot(hi, eb, preferred_element_type=F32) + jnp.dot(lo, eb, preferred_element_type=F32)


def _gelu(x):
    c = math.sqrt(2.0 / math.pi)
    return 0.5 * x * (1.0 + jnp.tanh(c * (x + 0.044715 * (x * x * x))))


def _gelu_grad(x):
    c = math.sqrt(2.0 / math.pi)
    t = jnp.tanh(c * (x + 0.044715 * (x * x * x)))
    return 0.5 * (1.0 + t) + 0.5 * x * (1.0 - t * t) * c * (1.0 + 3.0 * 0.044715 * (x * x))


def _sigmoid(x):
    return 1.0 / (1.0 + jnp.exp(-x))


def _scatter_cols(scratch, first, val):
    for c in range(val.shape[1] // LANES):
        scratch[first + c] = val[:, c * LANES:(c + 1) * LANES]


def _gather_cols(scratch, first, ncol):
    return jnp.concatenate([scratch[first + c] for c in range(ncol)], axis=1)


def _read_residue(scratch, first, ncol, r, d):
    n = scratch.shape[1] // d
    return jnp.concatenate([scratch.at[first + c][pl.ds(r, n, stride=d), :] for c in range(ncol)], axis=1)


def _write_residue(scratch, first, r, d, val):
    n = scratch.shape[1] // d
    for c in range(val.shape[1] // LANES):
        scratch.at[first + c][pl.ds(r, n, stride=d), :] = val[:, c * LANES:(c + 1) * LANES]


def _norm_mm(xs, gain, w, res, *, tm, tn, name, out_dtype=F32):
    t = xs[0].shape[0]
    k = sum(x.shape[1] for x in xs)
    n = w.shape[1]
    ng = len(xs)
    has_res = res is not None

    def body(*refs):
        x_refs = refs[:ng]
        g_ref, w_ref = refs[ng], refs[ng + 1]
        res_ref = refs[ng + 2] if has_res else None
        hn_ref, o_ref, hn_s = refs[ng + 2 + has_res:]

        @pl.when(pl.program_id(1) == 0)
        def _():
            off = 0
            for xr in x_refs:
                x = xr[...]
                wd = x.shape[1]
                r = lax.rsqrt(jnp.mean(x * x, axis=-1, keepdims=True) + EPS)
                hn_s[:, off:off + wd] = (x * r * g_ref[:, off:off + wd]).astype(BF16)
                off += wd
            hn_ref[...] = hn_s[...]

        acc = jnp.dot(hn_s[...], w_ref[...], preferred_element_type=F32)
        if has_res:
            acc = acc + res_ref[...]
        o_ref[...] = acc.astype(out_dtype)

    in_specs = [pl.BlockSpec((tm, x.shape[1]), lambda i, j: (i, 0)) for x in xs]
    in_specs += [pl.BlockSpec((1, k), lambda i, j: (0, 0)), pl.BlockSpec((k, tn), lambda i, j: (0, j))]
    args = list(xs) + [gain, w]
    if has_res:
        in_specs.append(pl.BlockSpec((tm, tn), lambda i, j: (i, j)))
        args.append(res)
    return pl.pallas_call(
        body, name=name, grid=(t // tm, n // tn), in_specs=in_specs,
        out_specs=[pl.BlockSpec((tm, k), lambda i, j: (i, 0)), pl.BlockSpec((tm, tn), lambda i, j: (i, j))],
        out_shape=[_sds((t, k), BF16), _sds((t, n), out_dtype)],
        scratch_shapes=[pltpu.VMEM((tm, k), BF16)],
        compiler_params=_cparams(PAR, ARB),
    )(*args)


def _mm(a, b, mode, res, *, tm, tn, out_dtype, name, a_rows=None):
    if mode == "tn":
        kk, m = a.shape
        blk_a = 0
        if a_rows is not None:
            blk_a, kk = a_rows
        a_spec = pl.BlockSpec((kk, tm), lambda i, j: (blk_a, i))
    else:
        m, kk = a.shape
        a_spec = pl.BlockSpec((tm, kk), lambda i, j: (i, 0))
    if mode == "nt":
        n = b.shape[0]
        b_spec = pl.BlockSpec((tn, kk), lambda i, j: (j, 0))
    else:
        n = b.shape[1]
        b_spec = pl.BlockSpec((kk, tn), lambda i, j: (0, j))
    has_res = res is not None

    def body(*refs):
        a_ref, b_ref = refs[0], refs[1]
        o_ref = refs[-1]
        av = a_ref[...].astype(BF16)
        bv = b_ref[...].astype(BF16)
        if mode == "nn":
            acc = jnp.dot(av, bv, preferred_element_type=F32)
        elif mode == "nt":
            acc = lax.dot_general(av, bv, NT, preferred_element_type=F32)
        else:
            acc = lax.dot_general(av, bv, TN, preferred_element_type=F32)
        if has_res:
            acc = acc + refs[2][...]
        o_ref[...] = acc.astype(out_dtype)

    in_specs = [a_spec, b_spec]
    args = [a, b]
    if has_res:
        in_specs.append(pl.BlockSpec((tm, tn), lambda i, j: (i, j)))
        args.append(res)
    return pl.pallas_call(
        body, name=name, grid=(m // tm, n // tn), in_specs=in_specs,
        out_specs=pl.BlockSpec((tm, tn), lambda i, j: (i, j)),
        out_shape=_sds((m, n), out_dtype),
        compiler_params=_cparams(PAR, PAR),
    )(*args)


def _mm_bt_normbwd(dys, w, xs, gain, dres, *, tm, tn, name, emit_bf16=False):
    t, wd_each = dys[0].shape
    nd = len(dys)
    per = wd_each // tn
    nj = nd * per
    k = w.shape[0]
    ng = len(xs)
    has_res = dres is not None

    def body(*refs):
        dy_refs = refs[:nd]
        w_ref = refs[nd]
        x_refs = refs[nd + 1:nd + 1 + ng]
        g_ref = refs[nd + 1 + ng]
        dres_ref = refs[nd + 2 + ng] if has_res else None
        outs = refs[nd + 2 + ng + has_res:]
        dx_ref = outs[0]
        dxb_ref = outs[1] if emit_bf16 else None
        dg_ref, acc = outs[1 + emit_bf16:]
        i, j = pl.program_id(0), pl.program_id(1)

        @pl.when(j == 0)
        def _():
            acc[...] = jnp.zeros_like(acc)

        for d, dy_ref in enumerate(dy_refs):
            @pl.when((j >= d * per) & (j < (d + 1) * per))
            def _(dy_ref=dy_ref):
                acc[...] += lax.dot_general(dy_ref[...].astype(BF16), w_ref[...], NT, preferred_element_type=F32)

        @pl.when(j == nj - 1)
        def _():
            @pl.when(i == 0)
            def _():
                dg_ref[...] = jnp.zeros_like(dg_ref)

            off = 0
            for xr in x_refs:
                x = xr[...]
                wd = x.shape[1]
                g = g_ref[:, off:off + wd]
                dyn = acc[:, off:off + wd]
                r = lax.rsqrt(jnp.mean(x * x, axis=-1, keepdims=True) + EPS)
                gdy = dyn * g
                dx = r * gdy - x * (r * r * r * jnp.mean(gdy * x, axis=-1, keepdims=True))
                if has_res:
                    dx = dx + dres_ref[:, off:off + wd]
                dx_ref[:, off:off + wd] = dx
                if emit_bf16:
                    dxb_ref[:, off:off + wd] = dx.astype(BF16)
                dg_ref[:, off:off + wd] += jnp.sum(dyn * x * r, axis=0, keepdims=True)
                off += wd

    def dy_map(d):
        return lambda i, j: (i, jnp.clip(j - d * per, 0, per - 1))

    in_specs = [pl.BlockSpec((tm, tn), dy_map(d)) for d in range(nd)]
    in_specs.append(pl.BlockSpec((k, tn), lambda i, j: (0, j)))
    in_specs += [pl.BlockSpec((tm, x.shape[1]), lambda i, j: (i, 0)) for x in xs]
    in_specs.append(pl.BlockSpec((1, k), lambda i, j: (0, 0)))
    args = list(dys) + [w] + list(xs) + [gain]
    if has_res:
        in_specs.append(pl.BlockSpec((tm, k), lambda i, j: (i, 0)))
        args.append(dres)
    row = pl.BlockSpec((tm, k), lambda i, j: (i, 0))
    out_specs = [row] + ([row] if emit_bf16 else []) + [pl.BlockSpec((1, k), lambda i, j: (0, 0))]
    out_shape = [_sds((t, k), F32)] + ([_sds((t, k), BF16)] if emit_bf16 else []) + [_sds((1, k), F32)]
    return pl.pallas_call(
        body, name=name, grid=(t // tm, nj), in_specs=in_specs, out_specs=out_specs, out_shape=out_shape,
        scratch_shapes=[pltpu.VMEM((tm, k), F32)],
        compiler_params=_cparams(ARB, ARB),
    )(*args)


def _rope_partner(y):
    n = y.shape[1]
    lane = lax.broadcasted_iota(jnp.int32, y.shape, 1)
    return jnp.where((lane & 31) < 16, pltpu.roll(y, n - 16, 1), pltpu.roll(y, 16, 1))


def _residue_specs(tm, width, nt):
    specs = [pl.BlockSpec((tm, width), lambda b, i: (b * nt + i, 0))]
    for d in DILATIONS[1:]:
        specs.append(pl.BlockSpec((None, d, tm // d, width), lambda b, i: (b, 0, i, 0)))
    return specs


def _residue_shapes(bl, width, dtype):
    return [_sds((bl * SEQ, width), dtype)] + [_sds((bl, d, SEQ // d, width), dtype) for d in DILATIONS[1:]]


def _qkprep_fwd(proj, gains, cos, sins, *, tm, name):
    t = proj.shape[0]
    bl = t // SEQ
    nt = SEQ // tm

    def body(p_ref, g_ref, c_ref, s_ref, qa1_ref, qa4_ref, qa16_ref, qb_ref, qd_ref, scr):
        e = _group_sum_matrix(256, True)

        def hn(x, row):
            x = x.astype(F32)
            wd = x.shape[1]
            ms = _seg_sum(x * x, e[:wd, :wd]) * (1.0 / HEAD_DIM)
            return x * lax.rsqrt(ms + EPS) * g_ref[row:row + 1, :wd]

        qa = jnp.concatenate([hn(p_ref[:, 0:256], 0) * ATTN_SCALE, hn(p_ref[:, 256:512], 1),
                              p_ref[:, 512:768].astype(F32)], axis=1)
        qa1_ref[...] = qa.astype(BF16)
        _scatter_cols(scr, 0, qa)
        for d, ref in ((4, qa4_ref), (16, qa16_ref)):
            for r in range(d):
                ref[r] = _read_residue(scr, 0, 6, r, d).astype(BF16)
        qb_ref[:, 0:256] = (hn(p_ref[:, 768:1024], 2) * ATTN_SCALE).astype(BF16)
        qb_ref[:, 256:384] = hn(p_ref[:, 1024:1152], 3).astype(BF16)
        qb_ref[:, 384:512] = p_ref[:, 1152:1280].astype(BF16)
        yq = hn(p_ref[:, 1792:2048], 4)
        yq = yq * c_ref[...] + _rope_partner(yq) * s_ref[...]
        qd_ref[:, 0:256] = (yq * ATTN_SCALE).astype(BF16)
        yk = hn(p_ref[:, 2048:2176], 5)
        yk = yk * c_ref[:, 0:128] + _rope_partner(yk) * s_ref[:, 0:128]
        qd_ref[:, 256:384] = yk.astype(BF16)
        qd_ref[:, 384:512] = p_ref[:, 2176:2304].astype(BF16)

    row = lambda width: pl.BlockSpec((tm, width), lambda b, i: (b * nt + i, 0))
    tab = pl.BlockSpec((tm, 256), lambda b, i: (i, 0))
    return pl.pallas_call(
        body, name=name, grid=(bl, nt),
        in_specs=[row(IN_WIDTH), pl.BlockSpec((8, 256), lambda b, i: (0, 0)), tab, tab],
        out_specs=_residue_specs(tm, 768, nt) + [row(512), row(512)],
        out_shape=_residue_shapes(bl, 768, BF16) + [_sds((t, 512), BF16), _sds((t, 512), BF16)],
        scratch_shapes=[pltpu.VMEM((6, tm, LANES), F32)],
        compiler_params=_cparams(PAR, PAR),
    )(proj, gains, cos, sins)


def _qkprep_bwd(proj, da, db, dd, dcu, dcv, gains, cos, sins, *, tm, name):
    t = proj.shape[0]
    bl = t // SEQ
    nt = SEQ // tm
    flat = [a for cfg in da for a in cfg] + list(db) + list(dd) + [dcu, dcv]

    def body(*refs):
        p_ref, g_ref, c_ref, s_ref = refs[:4]
        d_refs = refs[4:4 + len(flat)]
        dp_ref, dg_ref, scr = refs[4 + len(flat):]
        a_refs = d_refs[:9]
        dqb_ref, dkb_ref, dvb_ref, dqd_ref, dkd_ref, dvd_ref, dcu_ref, dcv_ref = d_refs[9:]
        e = _group_sum_matrix(256, True)
        first = (pl.program_id(0) == 0) & (pl.program_id(1) == 0)
        last = (pl.program_id(0) == bl - 1) & (pl.program_id(1) == nt - 1)

        @pl.when(first)
        def _():
            dg_ref[...] = jnp.zeros_like(dg_ref)

        def hn_bwd(x, dy, row):
            x, dy = x.astype(F32), dy.astype(F32)
            wd = x.shape[1]
            ee = e[:wd, :wd]
            g = g_ref[row:row + 1, :wd]
            r = lax.rsqrt(_seg_sum(x * x, ee) * (1.0 / HEAD_DIM) + EPS)
            gdy = dy * g
            dx = r * gdy - x * (r * r * r * (_seg_sum(gdy * x, ee) * (1.0 / HEAD_DIM)))
            dg_ref[row:row + 1, :wd] += jnp.sum(dy * x * r, axis=0, keepdims=True)
            return dx

        def rope_bwd(dy, wd):
            dy = dy.astype(F32)
            return dy * c_ref[:, :wd] + _rope_partner(dy * s_ref[:, :wd])

        dqkv = jnp.concatenate([a_refs[m][...].astype(F32) for m in range(3)], axis=1)
        for ci, d in ((1, 4), (2, 16)):
            for r in range(d):
                part = jnp.concatenate([a_refs[3 * ci + m][r].astype(F32) for m in range(3)], axis=1)
                _write_residue(scr, 0, r, d, part)
            dqkv = dqkv + _gather_cols(scr, 0, 6)
        dp_ref[:, 0:256] = hn_bwd(p_ref[:, 0:256], dqkv[:, 0:256] * ATTN_SCALE, 0).astype(BF16)
        dp_ref[:, 256:512] = hn_bwd(p_ref[:, 256:512], dqkv[:, 256:512], 1).astype(BF16)
        dp_ref[:, 512:768] = dqkv[:, 512:768].astype(BF16)
        dp_ref[:, 768:1024] = hn_bwd(p_ref[:, 768:1024], dqb_ref[...] * ATTN_SCALE, 2).astype(BF16)
        dp_ref[:, 1024:1152] = hn_bwd(p_ref[:, 1024:1152], dkb_ref[...], 3).astype(BF16)
        dp_ref[:, 1152:1280] = dvb_ref[...].astype(BF16)
        dp_ref[:, 1280:1536] = dcu_ref[...].astype(BF16)
        dp_ref[:, 1536:1792] = dcv_ref[...].astype(BF16)
        dp_ref[:, 1792:2048] = hn_bwd(p_ref[:, 1792:2048], rope_bwd(dqd_ref[...] * ATTN_SCALE, 256), 4).astype(BF16)
        dp_ref[:, 2048:2176] = hn_bwd(p_ref[:, 2048:2176], rope_bwd(dkd_ref[...], 128), 5).astype(BF16)
        dp_ref[:, 2176:2304] = dvd_ref[...].astype(BF16)

        @pl.when(last)
        def _():
            dg_ref[...] = _seg_sum(dg_ref[...], _group_sum_matrix(256, False))

    row = lambda width: pl.BlockSpec((tm, width), lambda b, i: (b * nt + i, 0))
    tab = pl.BlockSpec((tm, 256), lambda b, i: (i, 0))
    in_specs = [row(IN_WIDTH), pl.BlockSpec((8, 256), lambda b, i: (0, 0)), tab, tab]
    res_specs = _residue_specs(tm, 256, nt)
    in_specs += [res_specs[ci] for ci in range(3) for _ in range(3)]
    in_specs += [row(a.shape[1]) for a in flat[9:]]
    return pl.pallas_call(
        body, name=name, grid=(bl, nt), in_specs=in_specs,
        out_specs=[row(IN_WIDTH), pl.BlockSpec((8, 256), lambda b, i: (0, 0))],
        out_shape=[_sds((t, IN_WIDTH), BF16), _sds((8, 256), F32)],
        scratch_shapes=[pltpu.VMEM((6, tm, LANES), F32)],
        compiler_params=_cparams(ARB, ARB),
    )(proj, gains, cos, sins, *flat)


BAND_ROWS_PER_STEP = 512


def _residues_per_step(dil, seq_len):
    return min(dil, max(1, BAND_ROWS_PER_STEP // seq_len))


def _band_spec(seq_len, spec, rb):
    width, idx = spec
    return pl.BlockSpec((None, rb, seq_len, width), lambda b, r: (b, r, 0, idx))


def _fill_padded(dst, src_ref, rad, seq_len):
    z = jnp.zeros((rad, dst.shape[1]), dst.dtype)
    dst[0:rad, :] = z
    dst[rad + seq_len:rad + seq_len + rad, :] = z
    dst[rad:rad + seq_len, :] = src_ref[...]


def _band_fwd(src, qs, ks, vs, bias, sink, *, rad, nh, nkv, name):
    bl, dil, sl, _ = src.shape
    blk = bias.shape[1]
    kw = blk + 2 * rad
    nb = sl // blk
    rep = nh // nkv
    has_sink = sink is not None
    rb = _residues_per_step(dil, sl)

    def body(*refs):
        q_all, k_all, v_all, b_ref = refs[:4]
        s_ref = refs[4] if has_sink else None
        o_all, l_all, kp, vp = refs[4 + has_sink:]
        for ri in range(rb):
            one_sequence(q_all.at[ri], k_all.at[ri], v_all.at[ri], b_ref, s_ref, o_all.at[ri], l_all.at[ri], kp, vp)

    def one_sequence(q_ref, k_ref, v_ref, b_ref, s_ref, o_ref, l_ref, kp, vp):
        _fill_padded(kp, k_ref, rad, sl)
        _fill_padded(vp, v_ref, rad, sl)

        def blk_body(i, carry):
            r0 = pl.multiple_of(i * blk, blk)
            qb = q_ref[pl.ds(r0, blk), :]
            kwin = kp[pl.ds(r0, kw), :]
            vwin = vp[pl.ds(r0, kw), :]
            col = r0 - rad + lax.broadcasted_iota(jnp.int32, (blk, kw), 1)
            neg = jnp.where((col >= 0) & (col < sl), 0.0, NEG_INF).astype(F32)
            for h in range(nh):
                g = h // rep
                hs = slice(h * HEAD_DIM, (h + 1) * HEAD_DIM)
                gs = slice(g * HEAD_DIM, (g + 1) * HEAD_DIM)
                s = lax.dot_general(qb[:, hs], kwin[:, gs], NT, preferred_element_type=F32)
                s = s + b_ref[h] + neg
                m = jnp.max(s, axis=1, keepdims=True)
                if has_sink:
                    sk = s_ref[h][0:1, 0:1]
                    m = jnp.maximum(m, sk)
                p = jnp.exp(s - m)
                den = jnp.sum(p, axis=1, keepdims=True)
                if has_sink:
                    den = den + jnp.exp(sk - m)
                o = jnp.dot(p.astype(BF16), vwin[:, gs], preferred_element_type=F32) / den
                o_ref[pl.ds(r0, blk), hs] = o
                l_ref[pl.ds(r0, blk), hs] = jnp.broadcast_to(m + jnp.log(den), (blk, HEAD_DIM))
            return carry

        lax.fori_loop(0, nb, blk_body, 0)

    in_specs = [_band_spec(sl, qs, rb), _band_spec(sl, ks, rb), _band_spec(sl, vs, rb),
                pl.BlockSpec((nh, blk, kw), lambda b, r: (0, 0, 0))]
    args = [src] * 3 + [bias]
    if has_sink:
        in_specs.append(pl.BlockSpec((nh, 8, 128), lambda b, r: (0, 0, 0)))
        args.append(sink)
    return pl.pallas_call(
        body, name=name, grid=(bl, dil // rb), in_specs=in_specs,
        out_specs=[_band_spec(sl, (256, 0), rb)] * 2,
        out_shape=[_sds((bl, dil, sl, 256), F32)] * 2,
        scratch_shapes=[pltpu.VMEM((sl + 2 * rad, ks[0]), BF16), pltpu.VMEM((sl + 2 * rad, vs[0]), BF16)],
        compiler_params=_cparams(PAR, PAR),
    )(*args)


def _band_a_fwd(qa, biases, *, name):
    bl = qa[0].shape[0]
    rad = A_RADIUS
    nh = 4
    tile = 256

    def body(q1, q4, q16, b1, b4, b16, y_ref, lt_ref, kp, vp, tmp_o, tmp_l, acc):
        for ci, (d, src, b_ref) in enumerate(((1, q1, b1), (4, q4, b4), (16, q16, b16))):
            sl = SEQ // d
            blk = b_ref.shape[1]
            kw = blk + 2 * rad
            z = jnp.zeros((rad, 256), BF16)
            for r in range(d):
                seq = src.at[r]
                for pad, c0 in ((kp, 256), (vp, 512)):
                    pad[0:rad, :] = z
                    pad[rad + sl:rad + sl + rad, :] = z
                    pad[rad:rad + sl, :] = seq[:, c0:c0 + 256]

                def blk_body(i, carry, seq=seq, sl=sl, blk=blk, kw=kw, b_ref=b_ref, ci=ci, d=d):
                    r0 = pl.multiple_of(i * blk, blk)
                    qb = seq[pl.ds(r0, blk), 0:256]
                    kwin = kp[pl.ds(r0, kw), :]
                    vwin = vp[pl.ds(r0, kw), :]
                    col = r0 - rad + lax.broadcasted_iota(jnp.int32, (blk, kw), 1)
                    neg = jnp.where((col >= 0) & (col < sl), 0.0, NEG_INF).astype(F32)
                    for h in range(nh):
                        hs = slice(h * HEAD_DIM, (h + 1) * HEAD_DIM)
                        s = lax.dot_general(qb[:, hs], kwin[:, hs], NT, preferred_element_type=F32) + b_ref[h] + neg
                        m = jnp.max(s, axis=1, keepdims=True)
                        p = jnp.exp(s - m)
                        den = jnp.sum(p, axis=1, keepdims=True)
                        o = jnp.dot(p.astype(BF16), vwin[:, hs], preferred_element_type=F32) / den
                        lse = jnp.broadcast_to(m + jnp.log(den), (blk, HEAD_DIM))
                        if d == 1:
                            half = slice((h % 2) * HEAD_DIM, (h % 2 + 1) * HEAD_DIM)
                            acc.at[4 * ci + h // 2][pl.ds(r0, blk), half] = o
                            acc.at[4 * ci + 2 + h // 2][pl.ds(r0, blk), half] = lse
                        else:
                            tmp_o[pl.ds(r0, blk), hs] = o
                            tmp_l[pl.ds(r0, blk), hs] = lse
                    return carry

                lax.fori_loop(0, sl // blk, blk_body, 0)
                if d > 1:
                    _write_residue(acc, 4 * ci, r, d, tmp_o[0:sl, :])
                    _write_residue(acc, 4 * ci + 2, r, d, tmp_l[0:sl, :])

        def mix(i, carry):
            rows = pl.ds(pl.multiple_of(i * tile, tile), tile)
            part = lambda k: jnp.concatenate([acc.at[k][rows, :], acc.at[k + 1][rows, :]], axis=1)
            o = [part(4 * ci) for ci in range(3)]
            l = [part(4 * ci + 2) for ci in range(3)]
            m = jnp.maximum(jnp.maximum(l[0], l[1]), l[2])
            e = [jnp.exp(li - m) for li in l]
            den = e[0] + e[1] + e[2]
            y_ref[rows, :] = (e[0] / den) * o[0] + (e[1] / den) * o[1] + (e[2] / den) * o[2]
            lt_ref[rows, :] = m + jnp.log(den)
            return carry

        lax.fori_loop(0, SEQ // tile, mix, 0)

    in_specs = [pl.BlockSpec((None,) + q.shape[1:], lambda b: (b, 0, 0, 0)) for q in qa]
    in_specs += [pl.BlockSpec(bias.shape, lambda b: (0, 0, 0)) for bias in biases]
    out = pl.BlockSpec((SEQ, 256), lambda b: (b, 0))
    return pl.pallas_call(
        body, name=name, grid=(bl,), in_specs=in_specs, out_specs=[out, out],
        out_shape=[_sds((bl * SEQ, 256), F32)] * 2,
        scratch_shapes=[pltpu.VMEM((SEQ + 2 * rad, 256), BF16), pltpu.VMEM((SEQ + 2 * rad, 256), BF16),
                        pltpu.VMEM((SEQ // 4, 256), F32), pltpu.VMEM((SEQ // 4, 256), F32),
                        pltpu.VMEM((12, SEQ, LANES), F32)],
        compiler_params=_cparams(PAR),
    )(*qa, *biases)


def _band_bwd(src, qs, ks, vs, bias, sink, dy, dcol, lse, delta, *, rad, nh, nkv, name):
    bl, dil, sl, _ = src.shape
    blk = bias.shape[1]
    kw = blk + 2 * rad
    nb = sl // blk
    rep = nh // nkv
    has_sink = sink is not None
    rb = _residues_per_step(dil, sl)
    wk, wv = ks[0], vs[0]

    def body(*refs):
        q_all, k_all, v_all, b_ref = refs[:4]
        s_ref = refs[4] if has_sink else None
        do_all, l_all, dl_all = refs[4 + has_sink:7 + has_sink]
        outs = refs[7 + has_sink:]
        dsk_ref = None
        if has_sink:
            dq_all, dk_all, dv_all, db_ref, dsk_ref, kp, vp, dka, dva = outs
        else:
            dq_all, dk_all, dv_all, db_ref, kp, vp, dka, dva = outs

        @pl.when((pl.program_id(0) == 0) & (pl.program_id(1) == 0))
        def _():
            db_ref[...] = jnp.zeros_like(db_ref)
            if has_sink:
                dsk_ref[...] = jnp.zeros_like(dsk_ref)

        for ri in range(rb):
            one_sequence(q_all.at[ri], k_all.at[ri], v_all.at[ri], b_ref, s_ref, do_all.at[ri], l_all.at[ri],
                         dl_all.at[ri], dq_all.at[ri], dk_all.at[ri], dv_all.at[ri], db_ref, dsk_ref, kp, vp, dka, dva)

    def one_sequence(q_ref, k_ref, v_ref, b_ref, s_ref, do_ref, l_ref, dl_ref, dq_ref, dk_ref, dv_ref, db_ref, dsk_ref,
                     kp, vp, dka, dva):
        _fill_padded(kp, k_ref, rad, sl)
        _fill_padded(vp, v_ref, rad, sl)
        dka[...] = jnp.zeros_like(dka)
        dva[...] = jnp.zeros_like(dva)

        def blk_body(i, carry):
            r0 = pl.multiple_of(i * blk, blk)
            qb = q_ref[pl.ds(r0, blk), :]
            kwin = kp[pl.ds(r0, kw), :]
            vwin = vp[pl.ds(r0, kw), :]
            dob = do_ref[pl.ds(r0, blk), :].astype(BF16)
            lb = l_ref[pl.ds(r0, blk), :]
            dlb = dl_ref[pl.ds(r0, blk), :]
            col = r0 - rad + lax.broadcasted_iota(jnp.int32, (blk, kw), 1)
            neg = jnp.where((col >= 0) & (col < sl), 0.0, NEG_INF).astype(F32)
            for h in range(nh):
                g = h // rep
                hs = slice(h * HEAD_DIM, (h + 1) * HEAD_DIM)
                gs = slice(g * HEAD_DIM, (g + 1) * HEAD_DIM)
                qh, kh, vh, doh = qb[:, hs], kwin[:, gs], vwin[:, gs], dob[:, hs]
                lh = lb[:, h * HEAD_DIM:h * HEAD_DIM + 1]
                dlh = dlb[:, h * HEAD_DIM:h * HEAD_DIM + 1]
                s = lax.dot_general(qh, kh, NT, preferred_element_type=F32) + b_ref[h] + neg
                p = jnp.exp(s - lh)
                dp = lax.dot_general(doh, vh, NT, preferred_element_type=F32)
                ds = p * (dp - dlh)
                dsb = ds.astype(BF16)
                dq_ref[pl.ds(r0, blk), hs] = jnp.dot(dsb, kh, preferred_element_type=F32).astype(BF16)
                dka[pl.ds(r0, kw), gs] += lax.dot_general(dsb, qh, TN, preferred_element_type=F32)
                dva[pl.ds(r0, kw), gs] += lax.dot_general(p.astype(BF16), doh, TN, preferred_element_type=F32)
                db_ref[h] += ds
                if has_sink:
                    ps = jnp.exp(s_ref[h][0:1, 0:1] - lh)
                    dsk_ref[h] += jnp.broadcast_to(-jnp.sum(ps * dlh, axis=0, keepdims=True), (8, 128))
            return carry

        lax.fori_loop(0, nb, blk_body, 0)
        dk_ref[...] = dka[rad:rad + sl, :].astype(BF16)
        dv_ref[...] = dva[rad:rad + sl, :].astype(BF16)

    const3 = lambda b, r: (0, 0, 0)
    in_specs = [_band_spec(sl, qs, rb), _band_spec(sl, ks, rb), _band_spec(sl, vs, rb),
                pl.BlockSpec((nh, blk, kw), const3)]
    args = [src] * 3 + [bias]
    if has_sink:
        in_specs.append(pl.BlockSpec((nh, 8, 128), const3))
        args.append(sink)
    row = _band_spec(sl, (256, 0), rb)
    in_specs += [_band_spec(sl, (256, dcol), rb), row, row]
    args += [dy, lse, delta]
    out_specs = [row, _band_spec(sl, (wk, 0), rb), _band_spec(sl, (wv, 0), rb), pl.BlockSpec((nh, blk, kw), const3)]
    out_shape = [_sds((bl, dil, sl, 256), BF16), _sds((bl, dil, sl, wk), BF16), _sds((bl, dil, sl, wv), BF16),
                 _sds((nh, blk, kw), F32)]
    if has_sink:
        out_specs.append(pl.BlockSpec((nh, 8, 128), const3))
        out_shape.append(_sds((nh, 8, 128), F32))
    return pl.pallas_call(
        body, name=name, grid=(bl, dil // rb), in_specs=in_specs, out_specs=out_specs, out_shape=out_shape,
        scratch_shapes=[pltpu.VMEM((sl + 2 * rad, wk), BF16), pltpu.VMEM((sl + 2 * rad, wv), BF16),
                        pltpu.VMEM((sl + 2 * rad, wk), F32), pltpu.VMEM((sl + 2 * rad, wv), F32)],
        compiler_params=_cparams(ARB, ARB),
    )(*args)


def _combine_a(os_, ls_, *, tm, name):
    bl = os_[1].shape[0]
    t = bl * SEQ
    nt = SEQ // tm

    def body(o1, o4, o16, l1, l4, l16, y_ref, lt_ref, scr):
        for k, (d, ref) in enumerate(((4, o4), (16, o16), (4, l4), (16, l16))):
            for r in range(d):
                _write_residue(scr, 2 * k, r, d, ref[r])
        o2, o3, b, c = (_gather_cols(scr, 2 * k, 2) for k in range(4))
        a = l1[...]
        m = jnp.maximum(jnp.maximum(a, b), c)
        ea, eb, ec = jnp.exp(a - m), jnp.exp(b - m), jnp.exp(c - m)
        den = ea + eb + ec
        y_ref[...] = (ea / den) * o1[...] + (eb / den) * o2 + (ec / den) * o3
        lt_ref[...] = m + jnp.log(den)

    specs = _residue_specs(tm, 256, nt)
    return pl.pallas_call(
        body, name=name, grid=(bl, nt), in_specs=specs * 2, out_specs=[specs[0]] * 2,
        out_shape=[_sds((t, 256), F32)] * 2, scratch_shapes=[pltpu.VMEM((8, tm, LANES), F32)],
        compiler_params=_cparams(PAR, PAR),
    )(*os_, *ls_)


def _deltas(dycat, ya, yb, yd, lse_a, *, tm, name):
    t = ya.shape[0]
    bl = t // SEQ
    nt = SEQ // tm

    def body(dy_ref, ya_ref, yb_ref, yd_ref, la_ref, dy4, dy16, l4, l16, da1, da4, da16, db_ref, dd_ref, scr):
        e = _group_sum_matrix(256, True)
        dya = dy_ref[:, 0:256]
        dla = _seg_sum(dya * ya_ref[...], e)
        da1[...] = dla
        db_ref[...] = _seg_sum(dy_ref[:, 256:512] * yb_ref[...], e)
        dd_ref[...] = _seg_sum(dy_ref[:, 768:1024] * yd_ref[...], e)
        for k, (val, r4, r16) in enumerate(((dya, dy4, dy16), (la_ref[...], l4, l16), (dla, da4, da16))):
            _scatter_cols(scr, 2 * k, val)
            for d, ref in ((4, r4), (16, r16)):
                for r in range(d):
                    ref[r] = _read_residue(scr, 2 * k, 2, r, d)

    specs = _residue_specs(tm, 256, nt)
    nat = specs[0]
    shapes = _residue_shapes(bl, 256, F32)
    outs = pl.pallas_call(
        body, name=name, grid=(bl, nt),
        in_specs=[pl.BlockSpec((tm, 1024), lambda b, i: (b * nt + i, 0)), nat, nat, nat, nat],
        out_specs=specs[1:] + specs[1:] + specs + [nat, nat],
        out_shape=shapes[1:] + shapes[1:] + shapes + [shapes[0], shapes[0]],
        scratch_shapes=[pltpu.VMEM((6, tm, LANES), F32)],
        compiler_params=_cparams(PAR, PAR),
    )(dycat, ya, yb, yd, lse_a)
    return outs[0:2], outs[2:4], outs[4:7], outs[7], outs[8]


def _dense_fwd(qd, *, tq, name):
    t = qd.shape[0]
    bl = t // SEQ
    nq = SEQ // tq

    def body(q_ref, k_ref, v_ref, o_ref, l_ref):
        q = q_ref[...]
        for g in range(2):
            h0, h1 = 2 * g, 2 * g + 1
            q2 = jnp.concatenate([q[:, h0 * 64:(h0 + 1) * 64], q[:, h1 * 64:(h1 + 1) * 64]], axis=0)
            kg = k_ref[:, g * 64:(g + 1) * 64]
            vg = v_ref[:, g * 64:(g + 1) * 64]
            s = lax.dot_general(q2, kg, NT, preferred_element_type=F32)
            m = jnp.max(s, axis=1, keepdims=True)
            p = jnp.exp(s - m)
            den = jnp.sum(p, axis=1, keepdims=True)
            o2 = jnp.dot(p.astype(BF16), vg, preferred_element_type=F32) / den
            l2 = jnp.broadcast_to(m + jnp.log(den), (2 * tq, 64))
            o_ref[:, h0 * 64:(h0 + 1) * 64] = o2[:tq]
            o_ref[:, h1 * 64:(h1 + 1) * 64] = o2[tq:]
            l_ref[:, h0 * 64:(h0 + 1) * 64] = l2[:tq]
            l_ref[:, h1 * 64:(h1 + 1) * 64] = l2[tq:]

    q3 = qd.reshape(bl, SEQ, 512)
    o, lse = pl.pallas_call(
        body, name=name, grid=(bl, nq),
        in_specs=[pl.BlockSpec((None, tq, 256), lambda b, i: (b, i, 0)),
                  pl.BlockSpec((None, SEQ, 128), lambda b, i: (b, 0, 2)),
                  pl.BlockSpec((None, SEQ, 128), lambda b, i: (b, 0, 3))],
        out_specs=[pl.BlockSpec((None, tq, 256), lambda b, i: (b, i, 0))] * 2,
        out_shape=[_sds((bl, SEQ, 256), F32)] * 2,
        compiler_params=_cparams(PAR, PAR),
    )(q3, q3, q3)
    return o.reshape(t, 256), lse.reshape(t, 256)


def _dense_bwd(qd, dycat, lse, delta, *, tq, name):
    t = qd.shape[0]
    bl = t // SEQ
    nq = SEQ // tq

    def body(q_ref, k_ref, v_ref, do_ref, l_ref, dl_ref, dq_ref, dk_ref, dv_ref, dkt, dvt):
        @pl.when(pl.program_id(1) == 0)
        def _():
            dkt[...] = jnp.zeros_like(dkt)
            dvt[...] = jnp.zeros_like(dvt)

        q = q_ref[...]
        do = do_ref[...].astype(BF16)
        lv = l_ref[...]
        dlv = dl_ref[...]
        for g in range(2):
            h0, h1 = 2 * g, 2 * g + 1
            q2 = jnp.concatenate([q[:, h0 * 64:(h0 + 1) * 64], q[:, h1 * 64:(h1 + 1) * 64]], axis=0)
            do2 = jnp.concatenate([do[:, h0 * 64:(h0 + 1) * 64], do[:, h1 * 64:(h1 + 1) * 64]], axis=0)
            l2 = jnp.concatenate([lv[:, h0 * 64:h0 * 64 + 1], lv[:, h1 * 64:h1 * 64 + 1]], axis=0)
            dl2 = jnp.concatenate([dlv[:, h0 * 64:h0 * 64 + 1], dlv[:, h1 * 64:h1 * 64 + 1]], axis=0)
            kg = k_ref[:, g * 64:(g + 1) * 64]
            vg = v_ref[:, g * 64:(g + 1) * 64]
            s = lax.dot_general(q2, kg, NT, preferred_element_type=F32)
            p = jnp.exp(s - l2)
            dp = lax.dot_general(do2, vg, NT, preferred_element_type=F32)
            ds = (p * (dp - dl2)).astype(BF16)
            dq2 = jnp.dot(ds, kg, preferred_element_type=F32)
            dq_ref[:, h0 * 64:(h0 + 1) * 64] = dq2[:tq].astype(BF16)
            dq_ref[:, h1 * 64:(h1 + 1) * 64] = dq2[tq:].astype(BF16)
            dkt[g * 64:(g + 1) * 64, :] += lax.dot_general(q2, ds, TN, preferred_element_type=F32)
            dvt[g * 64:(g + 1) * 64, :] += lax.dot_general(do2, p.astype(BF16), TN, preferred_element_type=F32)

        @pl.when(pl.program_id(1) == nq - 1)
        def _():
            dk_ref[...] = dkt[...].T.astype(BF16)
            dv_ref[...] = dvt[...].T.astype(BF16)

    q3 = qd.reshape(bl, SEQ, 512)
    tile = pl.BlockSpec((None, tq, 256), lambda b, i: (b, i, 0))
    full = pl.BlockSpec((None, SEQ, 128), lambda b, i: (b, 0, 0))
    dq, dk, dv = pl.pallas_call(
        body, name=name, grid=(bl, nq),
        in_specs=[tile, pl.BlockSpec((None, SEQ, 128), lambda b, i: (b, 0, 2)),
                  pl.BlockSpec((None, SEQ, 128), lambda b, i: (b, 0, 3)),
                  pl.BlockSpec((None, tq, 256), lambda b, i: (b, i, 3)), tile, tile],
        out_specs=[tile, full, full],
        out_shape=[_sds((bl, SEQ, 256), BF16), _sds((bl, SEQ, 128), BF16), _sds((bl, SEQ, 128), BF16)],
        scratch_shapes=[pltpu.VMEM((128, SEQ), F32), pltpu.VMEM((128, SEQ), F32)],
        compiler_params=_cparams(PAR, ARB),
    )(q3, q3, q3, dycat.reshape(bl, SEQ, 1024), lse.reshape(bl, SEQ, 256), delta.reshape(bl, SEQ, 256))
    return dq.reshape(t, 256), dk.reshape(t, 128), dv.reshape(t, 128)


def _c_norm(cv, gam, bet):
    vg = _gelu(cv)
    mu = jnp.mean(vg, axis=-1, keepdims=True)
    xc = vg - mu
    r = lax.rsqrt(jnp.mean(xc * xc, axis=-1, keepdims=True) + EPS)
    xhat = xc * r
    return xhat * gam + bet, xhat, r


def _c_fwd(proj, gam, bet, ws, bst, *, tm, name):
    t = proj.shape[0]
    nch = tm // C_CHUNK

    def body(u_ref, v_ref, g_ref, b_ref, ws_ref, bs_ref, y_ref):
        vn, _, _ = _c_norm(v_ref[...].astype(F32), g_ref[...], b_ref[...])
        vnb = vn.astype(BF16)
        for c in range(nch):
            rows = slice(c * C_CHUNK, (c + 1) * C_CHUNK)
            for g in range(C_GROUPS):
                gs = slice(g * 64, (g + 1) * 64)
                mixed = jnp.dot(ws_ref[g], vnb[rows, gs], preferred_element_type=F32) + bs_ref[:, gs]
                y_ref[rows, gs] = _gelu(u_ref[rows, gs].astype(F32)) * mixed

    vec = pl.BlockSpec((1, 256), lambda i: (0, 0))
    return pl.pallas_call(
        body, name=name, grid=(t // tm,),
        in_specs=[pl.BlockSpec((tm, 256), lambda i: (i, 5)), pl.BlockSpec((tm, 256), lambda i: (i, 6)), vec, vec,
                  pl.BlockSpec((C_GROUPS, C_CHUNK, C_CHUNK), lambda i: (0, 0, 0)),
                  pl.BlockSpec((C_CHUNK, 256), lambda i: (0, 0))],
        out_specs=pl.BlockSpec((tm, 256), lambda i: (i, 0)), out_shape=_sds((t, 256), F32),
        compiler_params=_cparams(PAR),
    )(proj, proj, gam, bet, ws, bst)


def _c_bwd(proj, dycat, gam, bet, ws, wst, bst, *, tm, name):
    t = proj.shape[0]
    nch = tm // C_CHUNK
    nstep = t // tm

    def body(u_ref, v_ref, dy_ref, g_ref, b_ref, ws_ref, wst_ref, bs_ref,
             du_ref, dv_ref, dws_ref, dbs_ref, dg_ref, db_ref, dvn_s):
        step = pl.program_id(0)

        @pl.when(step == 0)
        def _():
            dws_ref[...] = jnp.zeros_like(dws_ref)
            dbs_ref[...] = jnp.zeros_like(dbs_ref)
            dg_ref[...] = jnp.zeros_like(dg_ref)
            db_ref[...] = jnp.zeros_like(db_ref)

        cv = v_ref[...].astype(F32)
        gam_v = g_ref[...]
        vn, xhat, r = _c_norm(cv, gam_v, b_ref[...])
        vnb = vn.astype(BF16)
        for c in range(nch):
            rows = slice(c * C_CHUNK, (c + 1) * C_CHUNK)
            for g in range(C_GROUPS):
                gs = slice(g * 64, (g + 1) * 64)
                cu = u_ref[rows, gs].astype(F32)
                dy = dy_ref[rows, gs]
                mixed = jnp.dot(ws_ref[g], vnb[rows, gs], preferred_element_type=F32) + bs_ref[:, gs]
                du_ref[rows, gs] = (dy * mixed * _gelu_grad(cu)).astype(BF16)
                dmix = dy * _gelu(cu)
                dbs_ref[:, gs] += dmix
                dmb = dmix.astype(BF16)
                dws_ref[g] += lax.dot_general(dmb, vnb[rows, gs], NT, preferred_element_type=F32)
                dvn_s[rows, gs] = jnp.dot(wst_ref[g], dmb, preferred_element_type=F32)
        dvn = dvn_s[...]
        dg_ref[...] += jnp.sum(dvn * xhat, axis=0, keepdims=True)
        db_ref[...] += jnp.sum(dvn, axis=0, keepdims=True)
        dxh = dvn * gam_v
        dvg = r * (dxh - jnp.mean(dxh, axis=-1, keepdims=True) - xhat * jnp.mean(dxh * xhat, axis=-1, keepdims=True))
        dv_ref[...] = (dvg * _gelu_grad(cv)).astype(BF16)

        @pl.when(step == nstep - 1)
        def _():
            dbs_ref[...] = _seg_sum(dbs_ref[...], _group_sum_matrix(256, True))

    vec = pl.BlockSpec((1, 256), lambda i: (0, 0))
    mat = pl.BlockSpec((C_GROUPS, C_CHUNK, C_CHUNK), lambda i: (0, 0, 0))
    bsp = pl.BlockSpec((C_CHUNK, 256), lambda i: (0, 0))
    tile = pl.BlockSpec((tm, 256), lambda i: (i, 0))
    return pl.pallas_call(
        body, name=name, grid=(nstep,),
        in_specs=[pl.BlockSpec((tm, 256), lambda i: (i, 5)), pl.BlockSpec((tm, 256), lambda i: (i, 6)),
                  pl.BlockSpec((tm, 256), lambda i: (i, 2)), vec, vec, mat, mat, bsp],
        out_specs=[tile, tile, mat, bsp, vec, vec],
        out_shape=[_sds((t, 256), BF16), _sds((t, 256), BF16), _sds((C_GROUPS, C_CHUNK, C_CHUNK), F32),
                   _sds((C_CHUNK, 256), F32), _sds((1, 256), F32), _sds((1, 256), F32)],
        scratch_shapes=[pltpu.VMEM((tm, 256), F32)],
        compiler_params=_cparams(ARB),
    )(proj, proj, dycat, gam, bet, ws, wst, bst)


FF_TC = 128
FF_NB = D_FF // FF_TC
FF_CH = 64
FF_HALO = 16


def _taps(ref, r0, win, where):
    z = jnp.zeros((FF_HALO, win.shape[1]), F32)
    if where == "first":
        win[0:FF_HALO, :] = z
        win[FF_HALO:, :] = ref[0:FF_CH + FF_HALO, :].astype(F32)
    elif where == "last":
        win[0:FF_CH + FF_HALO, :] = ref[SEQ - FF_CH - FF_HALO:SEQ, :].astype(F32)
        win[FF_CH + FF_HALO:, :] = z
    else:
        win[...] = ref[pl.ds(pl.multiple_of(r0 - FF_HALO, FF_HALO), FF_CH + 2 * FF_HALO), :].astype(F32)
    return tuple(win[FF_HALO + o:FF_HALO + o + FF_CH, :] for o in (-1, 0, 1))


def _chunk_loop(step):
    step(0, lambda ref, win: _taps(ref, 0, win, "first"))

    def mid(i, carry):
        r0 = pl.multiple_of(i * FF_CH, FF_CH)
        step(r0, lambda ref, win: _taps(ref, r0, win, "mid"))
        return carry

    lax.fori_loop(1, SEQ // FF_CH - 1, mid, 0)
    step(SEQ - FF_CH, lambda ref, win: _taps(ref, SEQ - FF_CH, win, "last"))


def _conv3(taps, w_ref, b_ref):
    dn, md, up = taps
    return w_ref[0:1, :] * dn + w_ref[1:2, :] * md + w_ref[2:3, :] * up + b_ref[...]


def _ff_specs(order):
    def at(fn):
        return (lambda b, j: fn(b, j)) if order == "bj" else (lambda j, b: fn(b, j))
    hs = [pl.BlockSpec((None, SEQ, FF_TC), at(lambda b, j, o=o: (b, 0, j + o))) for o in (0, FF_NB)]
    ws = [pl.BlockSpec((3, FF_TC), at(lambda b, j, o=o: (0, j + o))) for o in (0, FF_NB)]
    bs = [pl.BlockSpec((1, FF_TC), at(lambda b, j, o=o: (0, j + o))) for o in (0, FF_NB)]
    return hs, ws, bs


def _conv_gate_fwd(h, cw, cb, *, name):
    t = h.shape[0]
    bl = t // SEQ

    def body(hg_ref, hu_ref, wg_ref, wu_ref, bg_ref, bu_ref, a_ref, win):
        def step(r0, taps):
            cg = _conv3(taps(hg_ref, win.at[0]), wg_ref, bg_ref)
            cu = _conv3(taps(hu_ref, win.at[1]), wu_ref, bu_ref)
            a_ref[pl.ds(r0, FF_CH), :] = (cg * _sigmoid(cg) * cu).astype(BF16)

        _chunk_loop(step)

    hs, ws, bs = _ff_specs("bj")
    h3 = h.reshape(bl, SEQ, 2 * D_FF)
    act = pl.pallas_call(
        body, name=name, grid=(bl, FF_NB), in_specs=hs + ws + bs,
        out_specs=pl.BlockSpec((None, SEQ, FF_TC), lambda b, j: (b, 0, j)),
        out_shape=_sds((bl, SEQ, D_FF), BF16),
        scratch_shapes=[pltpu.VMEM((2, FF_CH + 2 * FF_HALO, FF_TC), F32)],
        compiler_params=_cparams(PAR, PAR),
    )(h3, h3, cw, cw, cb, cb)
    return act.reshape(t, D_FF)


def _conv_gate_bwd(h, dact, cw, cb, *, name):
    t = h.shape[0]
    bl = t // SEQ

    def body(hg_ref, hu_ref, wg_ref, wu_ref, bg_ref, bu_ref, da_ref,
             dhg_ref, dhu_ref, dwg_ref, dwu_ref, dbg_ref, dbu_ref, dg_s, du_s, win, sums):
        @pl.when(pl.program_id(1) == 0)
        def _():
            for ref in (dwg_ref, dwu_ref, dbg_ref, dbu_ref):
                ref[...] = jnp.zeros_like(ref)

        sums[...] = jnp.zeros_like(sums)
        red = lambda x: jnp.sum(x.reshape(FF_CH // 8, 8, x.shape[1]), axis=0)

        def pass1(r0, taps):
            tg, tu = taps(hg_ref, win.at[0]), taps(hu_ref, win.at[1])
            cg = _conv3(tg, wg_ref, bg_ref)
            cu = _conv3(tu, wu_ref, bu_ref)
            da = da_ref[pl.ds(r0, FF_CH), :].astype(F32)
            sg = _sigmoid(cg)
            dcg = da * cu * (sg * (1.0 + cg * (1.0 - sg)))
            dcu = da * (cg * sg)
            dg_s[pl.ds(r0, FF_CH), :] = dcg
            du_s[pl.ds(r0, FF_CH), :] = dcu
            for half, (d, tp) in enumerate(((dcg, tg), (dcu, tu))):
                for k in range(3):
                    sums[4 * half + k] += red(d * tp[k])
                sums[4 * half + 3] += red(d)

        _chunk_loop(pass1)
        for half, (dw_ref, db_ref) in enumerate(((dwg_ref, dbg_ref), (dwu_ref, dbu_ref))):
            for k in range(3):
                dw_ref[k:k + 1, :] += jnp.sum(sums[4 * half + k], axis=0, keepdims=True)
            db_ref[...] += jnp.sum(sums[4 * half + 3], axis=0, keepdims=True)

        def pass2(r0, taps):
            for k, (s, w_ref, o_ref) in enumerate(((dg_s, wg_ref, dhg_ref), (du_s, wu_ref, dhu_ref))):
                dn, md, up = taps(s, win.at[k])
                o_ref[pl.ds(r0, FF_CH), :] = (w_ref[0:1, :] * up + w_ref[1:2, :] * md + w_ref[2:3, :] * dn).astype(BF16)

        _chunk_loop(pass2)

    hs, ws, bs = _ff_specs("jb")
    half = pl.BlockSpec((None, SEQ, FF_TC), lambda j, b: (b, 0, j))
    wsp = pl.BlockSpec((3, FF_TC), lambda j, b: (0, j))
    bsp = pl.BlockSpec((1, FF_TC), lambda j, b: (0, j))
    h3 = h.reshape(bl, SEQ, 2 * D_FF)
    dhg, dhu, dwg, dwu, dbg, dbu = pl.pallas_call(
        body, name=name, grid=(FF_NB, bl), in_specs=hs + ws + bs + [half],
        out_specs=[half, half, wsp, wsp, bsp, bsp],
        out_shape=[_sds((bl, SEQ, D_FF), BF16), _sds((bl, SEQ, D_FF), BF16), _sds((3, D_FF), F32), _sds((3, D_FF), F32),
                   _sds((1, D_FF), F32), _sds((1, D_FF), F32)],
        scratch_shapes=[pltpu.VMEM((SEQ, FF_TC), F32), pltpu.VMEM((SEQ, FF_TC), F32),
                        pltpu.VMEM((2, FF_CH + 2 * FF_HALO, FF_TC), F32), pltpu.VMEM((8, 8, FF_TC), F32)],
        compiler_params=_cparams(PAR, ARB),
    )(h3, h3, cw, cw, cb, cb, dact.reshape(bl, SEQ, D_FF))
    return (dhg.reshape(t, D_FF), dhu.reshape(t, D_FF), jnp.concatenate([dwg, dwu], axis=1),
            jnp.concatenate([dbg, dbu], axis=1))


def _ple_fwd(x2, gain, wg, pe, pe_blk, wp, *, tm, name):
    t, k = x2.shape

    def body(x_ref, g_ref, wg_ref, pe_ref, wp_ref, hn_ref, x3_ref, gt_ref, pp_ref):
        x = x_ref[...]
        r = lax.rsqrt(jnp.mean(x * x, axis=-1, keepdims=True) + EPS)
        hn = (x * r * g_ref[...]).astype(BF16)
        hn_ref[...] = hn
        gate = _sigmoid(jnp.dot(hn, wg_ref[...], preferred_element_type=F32))
        pp = jnp.dot(pe_ref[...].astype(BF16), wp_ref[...], preferred_element_type=F32)
        gt_ref[...] = gate.astype(BF16)
        pp_ref[...] = pp.astype(BF16)
        x3_ref[...] = x + pp * gate

    row = pl.BlockSpec((tm, k), lambda i: (i, 0))
    return pl.pallas_call(
        body, name=name, grid=(t // tm,),
        in_specs=[row, pl.BlockSpec((1, k), lambda i: (0, 0)), pl.BlockSpec((k, k), lambda i: (0, 0)),
                  pl.BlockSpec((tm, PLE_DIM), lambda i: (pe_blk + i, 0)), pl.BlockSpec((PLE_DIM, k), lambda i: (0, 0))],
        out_specs=[row, row, row, row],
        out_shape=[_sds((t, k), BF16), _sds((t, k), F32), _sds((t, k), BF16), _sds((t, k), BF16)],
        compiler_params=_cparams(PAR),
    )(x2, gain, wg, pe, wp)


def _ple_bwd_ew(dx3, gate, pp, *, tm, name):
    t, n = dx3.shape

    def body(d_ref, g_ref, p_ref, dz_ref, dpp_ref):
        d, g = d_ref[...], g_ref[...]
        dz_ref[...] = (d * p_ref[...] * g * (1.0 - g)).astype(BF16)
        dpp_ref[...] = (d * g).astype(BF16)

    spec = pl.BlockSpec((tm, n), lambda i: (i, 0))
    return pl.pallas_call(
        body, name=name, grid=(t // tm,), in_specs=[spec] * 3, out_specs=[spec] * 2,
        out_shape=[_sds((t, n), BF16)] * 2, compiler_params=_cparams(PAR),
    )(dx3, gate, pp)


def _loss_head(y, tgt, *, tm, name):
    t, d = y.shape

    def body(y_ref, t_ref, l_ref, dy_ref):
        @pl.when(pl.program_id(0) == 0)
        def _():
            l_ref[...] = jnp.zeros_like(l_ref)

        e = y_ref[...] - t_ref[...]
        dy_ref[...] = e * (1.0 / d)
        s = jnp.sum(jnp.sum(e * e, axis=1, keepdims=True), axis=0, keepdims=True)
        l_ref[...] += jnp.broadcast_to(s * (0.5 / d), (8, 128))

    spec = pl.BlockSpec((tm, d), lambda i: (i, 0))
    return pl.pallas_call(
        body, name=name, grid=(t // tm,), in_specs=[spec, spec],
        out_specs=[pl.BlockSpec((8, 128), lambda i: (0, 0)), spec],
        out_shape=[_sds((8, 128), F32), _sds((t, d), F32)], compiler_params=_cparams(ARB),
    )(y, tgt)


BIAS_PC = 8192


def _onehot(bucket_row):
    rows = lax.broadcasted_iota(jnp.int32, (REL_BUCKETS, bucket_row.shape[1]), 0)
    return (rows == bucket_row).astype(BF16)


def _dot3(x, onehot, dims):
    acc = None
    for _ in range(3):
        term = x.astype(BF16)
        part = lax.dot_general(term, onehot, dims, preferred_element_type=F32)
        acc = part if acc is None else acc + part
        x = x - term.astype(F32)
    return acc


def _bias_lookup(table_t, bucket, *, name):
    h = table_t.shape[0]
    p = bucket.shape[1]

    def body(t_ref, b_ref, o_ref):
        bk = b_ref[...]
        val = _dot3(t_ref[...], _onehot(bk), (((1,), (0,)), ((), ())))
        o_ref[...] = jnp.where(bk >= 0, val, NEG_INF)

    return pl.pallas_call(
        body, name=name, grid=(p // BIAS_PC,),
        in_specs=[pl.BlockSpec((h, REL_BUCKETS), lambda i: (0, 0)), pl.BlockSpec((1, BIAS_PC), lambda i: (0, i))],
        out_specs=pl.BlockSpec((h, BIAS_PC), lambda i: (0, i)), out_shape=_sds((h, p), F32),
        compiler_params=_cparams(PAR),
    )(table_t, bucket)


def _bucket_reduce(dbiases, bucket, *, name):
    h, p = dbiases[0].shape
    nl = len(dbiases)

    def body(*refs):
        b_ref, o_ref = refs[nl], refs[nl + 1]

        @pl.when(pl.program_id(0) == 0)
        def _():
            o_ref[...] = jnp.zeros_like(o_ref)

        d = refs[0][...]
        for d_ref in refs[1:nl]:
            d = d + d_ref[...]
        o_ref[...] += _dot3(d, _onehot(b_ref[...]), NT)

    return pl.pallas_call(
        body, name=name, grid=(p // BIAS_PC,),
        in_specs=[pl.BlockSpec((h, BIAS_PC), lambda i: (0, i))] * nl + [pl.BlockSpec((1, BIAS_PC), lambda i: (0, i))],
        out_specs=pl.BlockSpec((h, REL_BUCKETS), lambda i: (0, 0)), out_shape=_sds((h, REL_BUCKETS), F32),
        compiler_params=_cparams(ARB),
    )(*dbiases, bucket)


def _adamw_math(w, g, m, v):
    m = ADAM_B1 * m + (1.0 - ADAM_B1) * g
    v = ADAM_B2 * v + (1.0 - ADAM_B2) * (g * g)
    m_hat = m / (1.0 - ADAM_B1 ** ADAM_STEP)
    v_hat = v / (1.0 - ADAM_B2 ** ADAM_STEP)
    delta = -ADAM_LR * (m_hat / (jnp.sqrt(v_hat) + ADAM_EPS) + ADAM_WD * w)
    return delta, m, v


def _adamw_reduce(parts, w, m, v, *, tr, name):
    nl = len(parts)
    rows, c = w.shape
    r = rows // nl
    nt = r // tr

    def body(*refs):
        p_refs = refs[:nl]
        w_ref, m_ref, v_ref, g_ref, d_ref, nm_ref, nv_ref = refs[nl:]
        for li, p_ref in enumerate(p_refs):
            @pl.when(pl.program_id(0) == li)
            def _(p_ref=p_ref):
                g = p_ref[0].astype(F32)
                for k in range(1, N_DEV):
                    g = g + p_ref[k].astype(F32)
                d, nm, nv = _adamw_math(w_ref[...], g, m_ref[...], v_ref[...])
                g_ref[...] = g
                d_ref[...] = d
                nm_ref[...] = nm
                nv_ref[...] = nv

    def part_map(li):
        return lambda l, i: (0, jnp.where(l == li, i, jnp.where(l < li, 0, nt - 1)), 0)

    spec = pl.BlockSpec((tr, c), lambda l, i: (l * nt + i, 0))
    return pl.pallas_call(
        body, name=name, grid=(nl, nt),
        in_specs=[pl.BlockSpec((N_DEV, tr, c), part_map(li)) for li in range(nl)] + [spec, spec, spec],
        out_specs=[spec] * 4, out_shape=[_sds((rows, c), F32)] * 4, compiler_params=_cparams(ARB, ARB),
    )(*parts, w, m, v)


def _adamw_plain(g, w, m, v, *, name):
    def body(g_ref, w_ref, m_ref, v_ref, d_ref, nm_ref, nv_ref):
        d, nm, nv = _adamw_math(w_ref[...], g_ref[...], m_ref[...], v_ref[...])
        d_ref[...] = d
        nm_ref[...] = nm
        nv_ref[...] = nv

    return pl.pallas_call(body, name=name, out_shape=[_sds(w.shape, F32)] * 3)(g, w, m, v)


def _mesh_pos():
    return lax.axis_index("x"), lax.axis_index("y"), lax.axis_index("c")


def _allgather_body(x_refs, out_refs, send_sems, recv_sems, local_sems, slot):
    x, y, c = _mesh_pos()
    me, sibling = (x, y, c), (x, y, 1 - c)
    chips = [(1 - x, y), (x, 1 - y), (1 - x, 1 - y)]
    waits = []
    for a, (x_ref, out_ref) in enumerate(zip(x_refs, out_refs)):
        def copy(k, block, to, src=None, out_ref=out_ref, a=a):
            return pltpu.make_async_remote_copy(
                src_ref=slot(out_ref, block) if src is None else src, dst_ref=slot(out_ref, block),
                send_sem=send_sems.at[a, k], recv_sem=recv_sems.at[a, k], device_id=to, device_id_type=MESH)

        mine = pltpu.make_async_copy(x_ref, slot(out_ref, me), local_sems.at[a])
        mine.start()
        first = [copy(0, me, sibling, src=x_ref)]
        first += [copy(1 + j, me, (*chip, c), src=x_ref) for j, chip in enumerate(chips)]
        for cp in first:
            cp.start()
        waits.append((copy, mine, first))
    sends = []
    for copy, mine, first in waits:
        passed = [copy(4 + j, (*chip, c), sibling) for j, chip in enumerate(chips)]
        for j, chip in enumerate(chips):
            copy(1 + j, (*chip, c), me).wait_recv()
            passed[j].start()
        sends.append(passed)
    for (copy, mine, first), passed in zip(waits, sends):
        copy(0, sibling, me).wait_recv()
        for j, chip in enumerate(chips):
            copy(4 + j, (*chip, 1 - c), me).wait_recv()
        for cp in first + passed:
            cp.wait_send()
        mine.wait()


PEER_FLIPS = ((0, 0, 1), (1, 0, 0), (0, 1, 0), (1, 1, 0), (1, 0, 1), (0, 1, 1), (1, 1, 1))


def _peer_copies(x_refs, land_refs, send_sem, recv_sem, scatter):
    x, y, c = _mesh_pos()
    me = 4 * x + 2 * y + c
    copies = []
    for x_ref, land_ref in zip(x_refs, land_refs):
        for fx, fy, fc in PEER_FLIPS:
            px, py, pc = x ^ fx, y ^ fy, c ^ fc
            src = x_ref.at[4 * px + 2 * py + pc] if scatter else x_ref
            copies.append(pltpu.make_async_remote_copy(
                src_ref=src, dst_ref=land_ref.at[me], send_sem=send_sem, recv_sem=recv_sem,
                device_id=(px, py, pc), device_id_type=MESH))
    return copies


def _sc_exchange(xs, *, scatter, collective_id, name):
    na = len(xs)
    land_shapes = [x.shape if scatter else (N_DEV,) + x.shape for x in xs]

    def body(*refs):
        x_refs, land_refs = refs[:na], refs[na:2 * na]
        send_sem, recv_sem, local_sem = refs[2 * na:]
        x, y, c = _mesh_pos()
        me = 4 * x + 2 * y + c
        barrier = pltpu.get_barrier_semaphore()
        for fx, fy, fc in PEER_FLIPS:
            pl.semaphore_signal(barrier, inc=1, device_id=(x ^ fx, y ^ fy, c ^ fc), device_id_type=MESH)
        pl.semaphore_wait(barrier, len(PEER_FLIPS))
        for x_ref, land_ref in zip(x_refs, land_refs):
            own = pltpu.make_async_copy(x_ref.at[me] if scatter else x_ref, land_ref.at[me], local_sem)
            own.start()
            own.wait()
        copies = _peer_copies(x_refs, land_refs, send_sem, recv_sem, scatter)
        for cp in copies:
            cp.start()
        for cp in copies:
            cp.wait()

    return pl.kernel(
        body, name=name, out_type=[_sds(s, x.dtype) for s, x in zip(land_shapes, xs)],
        mesh=plsc.ScalarSubcoreMesh(axis_name="sequencer", num_cores=1),
        scratch_types=[pltpu.SemaphoreType.DMA, pltpu.SemaphoreType.DMA, pltpu.SemaphoreType.DMA],
        compiler_params=pltpu.CompilerParams(collective_id=collective_id),
    )(*xs)


def _sc_allgather(xs, *, collective_id, name):
    na = len(xs)

    def body(*refs):
        x_refs, out_refs = refs[:na], refs[na:2 * na]
        send_sems, recv_sems, local_sems = refs[2 * na:]
        x, y, c = _mesh_pos()
        barrier = pltpu.get_barrier_semaphore()
        for fx, fy, fc in PEER_FLIPS:
            pl.semaphore_signal(barrier, inc=1, device_id=(x ^ fx, y ^ fy, c ^ fc), device_id_type=MESH)
        pl.semaphore_wait(barrier, len(PEER_FLIPS))
        _allgather_body(x_refs, out_refs, send_sems, recv_sems, local_sems,
                        lambda ref, pos: ref.at[4 * pos[0] + 2 * pos[1] + pos[2]])

    return pl.kernel(
        body, name=name, out_type=[_sds((N_DEV,) + x.shape, x.dtype) for x in xs],
        mesh=plsc.ScalarSubcoreMesh(axis_name="sequencer", num_cores=1),
        scratch_types=[pltpu.SemaphoreType.DMA((na, 7)), pltpu.SemaphoreType.DMA((na, 7)),
                       pltpu.SemaphoreType.DMA((na,))],
        compiler_params=pltpu.CompilerParams(collective_id=collective_id),
    )(*xs)


def _allgather_vmem(x, *, name):
    r, c = x.shape

    def body(x_ref, out_ref, send_sems, recv_sems, local_sems):
        _allgather_body([x_ref], [out_ref], send_sems, recv_sems, local_sems,
                        lambda ref, pos: ref.at[pl.ds((4 * pos[0] + 2 * pos[1] + pos[2]) * r, r), :])

    vm = pl.BlockSpec(memory_space=pltpu.VMEM)
    return pl.pallas_call(
        body, name=name, in_specs=[vm], out_specs=vm, out_shape=_sds((N_DEV * r, c), x.dtype),
        scratch_shapes=[pltpu.SemaphoreType.DMA((1, 7)), pltpu.SemaphoreType.DMA((1, 7)),
                        pltpu.SemaphoreType.DMA((1,))],
    )(x)


def _sum_slots(gathered, *, name):
    _, r, c = gathered.shape

    def body(g_ref, o_ref):
        acc = g_ref[0]
        for k in range(1, N_DEV):
            acc = acc + g_ref[k]
        o_ref[...] = acc

    return pl.pallas_call(body, name=name, out_shape=_sds((r, c), gathered.dtype))(gathered)


def _t5_bucket(rel):
    nb = REL_BUCKETS // 2
    ret = jnp.where(rel > 0, nb, 0)
    n = jnp.abs(rel)
    max_exact = nb // 2
    nf = jnp.maximum(n, 1).astype(F32)
    large = max_exact + (jnp.log(nf / max_exact) / math.log(REL_MAX_DIST / max_exact)
                         * (nb - max_exact)).astype(jnp.int32)
    large = jnp.minimum(large, nb - 1)
    return ret + jnp.where(n < max_exact, n, large)


def _band_pattern(block, radius, dil):
    kw = block + 2 * radius
    rel = jnp.arange(kw)[None, :] - radius - jnp.arange(block)[:, None]
    return jnp.where(jnp.abs(rel) <= radius, _t5_bucket(rel * dil), -1).astype(jnp.int32).reshape(1, block * kw)


def _rope_tables():
    lane = np.arange(64)
    seg, j = lane // 32, lane % 32
    inv = ROPE_THETA ** (-jnp.arange(0, 32, 2, dtype=F32) / 32)
    tpos = jnp.arange(SEQ)
    pos = jnp.where(jnp.asarray(seg)[None, :] == 0, (tpos // GRID_W)[:, None], (tpos % GRID_W)[:, None])
    ang = pos.astype(F32) * inv[jnp.asarray(j % 16)][None, :]
    cos = jnp.cos(ang)
    sins = jnp.where(jnp.asarray(j)[None, :] < 16, -jnp.sin(ang), jnp.sin(ang))
    return jnp.tile(cos, (1, 4)), jnp.tile(sins, (1, 4))


A_Q, A_K, A_V = (256, 0), (256, 1), (256, 2)
B_Q, B_K, B_V = (256, 0), (128, 2), (128, 3)
A_HEADS = dict(rad=A_RADIUS, nh=4, nkv=4)
B_HEADS = dict(rad=SWA_RADIUS, nh=4, nkv=2)


def _local_step(x, pe, tgt, rel_bias, wts, matmul_weights, grads_ready):
    t = x.shape[0]
    bl = t // SEQ
    cos, sins = _rope_tables()
    blocks_a = [min(BAND_BLOCK, SEQ // d) for d in DILATIONS]
    pats_a = [_band_pattern(blk, A_RADIUS, d) for blk, d in zip(blocks_a, DILATIONS)]
    pat_b = _band_pattern(BAND_BLOCK, SWA_RADIUS, 1)
    table_t = rel_bias.T
    bias_a = [_bias_lookup(table_t[:4], pt, name=f"bias_a{ci}").reshape(4, blk, blk + 2 * A_RADIUS)
              for ci, (pt, blk) in enumerate(zip(pats_a, blocks_a))]
    bias_b = _bias_lookup(table_t[4:], pat_b, name="bias_b").reshape(4, BAND_BLOCK, BAND_BLOCK + 2 * SWA_RADIUS)
    nat4 = lambda a: a.reshape(bl, 1, SEQ, a.shape[-1])

    saved = []
    for li in range(DEPTH):
        w = dict(wts[li])
        w.update(matmul_weights(li, "in", x))
        hn0, proj = _norm_mm((x,), w["g_mix"], w["w_in"], None, tm=1024, tn=1152, name="mix_in_fwd", out_dtype=BF16)
        qa1, qa4, qa16, qb, qd = _qkprep_fwd(proj, w["qk_gains"], cos, sins, tm=512, name="qkprep_fwd")
        qa = (nat4(qa1), qa4, qa16)
        ya, lse_a = _band_a_fwd(qa, bias_a, name="band_a_fwd")
        yb, lse_b = _band_fwd(nat4(qb), B_Q, B_K, B_V, bias_b, w["sink_t"], name="band_b_fwd", **B_HEADS)
        yb = yb.reshape(t, 256)
        yc = _c_fwd(proj, w["c_g"], w["c_b"], w["c_ws"], w["c_bst"], tm=512, name="c_fwd")
        yd, lse_d = _dense_fwd(qd, tq=256, name="dense_fwd")
        w.update(matmul_weights(li, "rest", yd))
        mixed, x1 = _norm_mm((ya, yb, yc, yd), w["out_gain"], w["w_out"], x, tm=1024, tn=1024, name="mix_out_fwd")
        hn1, h = _norm_mm((x1,), w["g_ffn"], w["w_up"], None, tm=1024, tn=1408, name="ffn_up_fwd", out_dtype=BF16)
        act = _conv_gate_fwd(h, w["conv_w"], w["conv_b"], name="conv_gate_fwd")
        x2 = _mm(act, w["w_down"], "nn", x1, tm=1024, tn=1024, out_dtype=F32, name="ffn_down_fwd")
        hn2, x3, gate, pp = _ple_fwd(x2, w["g_ple"], w["w_gate"], pe, li * (t // 1024), w["w_proj"], tm=1024,
                                     name="ple_fwd")
        saved.append(dict(w=w, x0=x, hn0=hn0, proj=proj, qa=qa, qb=qb, qd=qd, ya=ya, lse_a=lse_a, yb=yb, lse_b=lse_b,
                          yc=yc, yd=yd, lse_d=lse_d, mixed=mixed, x1=x1, hn1=hn1, h=h, act=act, x2=x2, hn2=hn2,
                          gate=gate, pp=pp))
        x = x3

    loss_tile, dx = _loss_head(x, tgt, tm=512, name="loss_head")
    grads = [None] * DEPTH
    dbias_a, dbias_bs = [[], [], []], []
    for li in reversed(range(DEPTH)):
        s = saved[li]
        w = s["w"]
        g = {}
        dz, dpp = _ple_bwd_ew(dx, s["gate"], s["pp"], tm=512, name="ple_bwd_ew")
        g["w_gate"] = _mm(s["hn2"], dz, "tn", None, tm=1024, tn=512, out_dtype=BF16, name="dw_gate")
        g["w_proj"] = _mm(pe, dpp, "tn", None, tm=256, tn=1024, out_dtype=BF16, name="dw_proj", a_rows=(li, t))
        dx2, dx2b, g["g_ple"] = _mm_bt_normbwd((dz,), w["w_gate"], (s["x2"],), w["g_ple"], dx, tm=1024, tn=1024,
                                               name="ple_bwd", emit_bf16=True)
        g["w_down"] = _mm(s["act"], dx2b, "tn", None, tm=1408, tn=512, out_dtype=BF16, name="dw_down")
        dact = _mm(dx2b, w["w_down"], "nt", None, tm=1024, tn=1408, out_dtype=BF16, name="ffn_down_bwd")
        dhg, dhu, g["conv_w"], g["conv_b"] = _conv_gate_bwd(s["h"], dact, w["conv_w"], w["conv_b"], name="conv_gate_bwd")
        g["w_up"] = jnp.concatenate(
            [_mm(s["hn1"], dhalf, "tn", None, tm=1024, tn=1408, out_dtype=BF16, name=f"dw_up_{nm}")
             for nm, dhalf in (("gate", dhg), ("up", dhu))], axis=1)
        dx1, dx1b, g["g_ffn"] = _mm_bt_normbwd((dhg, dhu), w["w_up"], (s["x1"],), w["g_ffn"], dx2, tm=1024, tn=1408,
                                               name="ffn_up_bwd", emit_bf16=True)
        g["w_out"] = _mm(s["mixed"], dx1b, "tn", None, tm=1024, tn=512, out_dtype=BF16, name="dw_out")
        grads_ready(li, "mid", g)
        dycat, g["out_gain"] = _mm_bt_normbwd((dx1b,), w["w_out"], (s["ya"], s["yb"], s["yc"], s["yd"]), w["out_gain"],
                                              None, tm=1024, tn=1024, name="mix_out_bwd")
        dy_r, lse_r, dl_a, dl_b, dl_d = _deltas(dycat, s["ya"], s["yb"], s["yd"], s["lse_a"], tm=512, name="deltas")
        dy_a = (nat4(dycat),) + tuple(dy_r)
        lse_a = (nat4(s["lse_a"]),) + tuple(lse_r)
        dl_a = (nat4(dl_a[0]),) + tuple(dl_a[1:])
        da = []
        for ci in range(3):
            dq, dk, dv, dbias = _band_bwd(s["qa"][ci], A_Q, A_K, A_V, bias_a[ci], None, dy_a[ci], 0, lse_a[ci],
                                          dl_a[ci], name=f"band_a{ci}_bwd", **A_HEADS)
            if ci == 0:
                dq, dk, dv = (a.reshape(t, 256) for a in (dq, dk, dv))
            da.append((dq, dk, dv))
            dbias_a[ci].append(dbias.reshape(4, -1))
        dqb, dkb, dvb, dbias_b, dsink = _band_bwd(nat4(s["qb"]), B_Q, B_K, B_V, bias_b, w["sink_t"], nat4(dycat), 1,
                                                  nat4(s["lse_b"]), nat4(dl_b), name="band_b_bwd", **B_HEADS)
        dbias_bs.append(dbias_b.reshape(4, -1))
        g["sink"] = dsink[:, 0, 0]
        dd = _dense_bwd(s["qd"], dycat, s["lse_d"], dl_d, tq=256, name="dense_bwd")
        dcu, dcv, g["c_ws"], dbs, g["c_g"], g["c_b"] = _c_bwd(s["proj"], dycat, w["c_g"], w["c_b"], w["c_ws"],
                                                               w["c_wst"], w["c_bst"], tm=512, name="c_bwd")
        g["c_bs"] = dbs[:, ::64].T
        db = (dqb.reshape(t, 256), dkb.reshape(t, 128), dvb.reshape(t, 128))
        dproj, dgains = _qkprep_bwd(s["proj"], da, db, dd, dcu, dcv, w["qk_gains"], cos, sins, tm=512, name="qkprep_bwd")
        g["qk_gain"] = dgains[:6, :64].reshape(3, 2, HEAD_DIM)
        g["w_in"] = _mm(s["hn0"], dproj, "tn", None, tm=1024, tn=1152, out_dtype=BF16, name="dw_in")
        dx, g["g_mix"] = _mm_bt_normbwd((dproj,), w["w_in"], (s["x0"],), w["g_mix"], dx1, tm=1024, tn=1152,
                                        name="mix_in_bwd")
        grads[li] = g
        grads_ready(li, "end", g)
    d_table_a = sum(_bucket_reduce(dbias_a[ci], pats_a[ci], name=f"bucket_a{ci}") for ci in range(3))
    d_table_b = _bucket_reduce(dbias_bs, pat_b, name="bucket_b")
    d_rel_bias = jnp.concatenate([d_table_a, d_table_b], axis=0).T
    return loss_tile[0, 0], dx, grads, d_rel_bias


WEIGHT_NAMES = ("rel_bias", "ln_mix_g", "w_in", "qk_gain", "sink", "c_norm_g", "c_norm_b", "c_ws", "c_bs", "out_gain",
                "w_out", "ln_ffn_g", "w_up", "conv_w", "conv_b", "w_down", "ln_ple_g", "w_ple_gate", "w_ple_proj")
COL_SHARDED = ("w_in", "w_up", "w_ple_proj")
ROW_SHARDED = ("w_out", "w_down", "w_ple_gate")
SMALL_SHARDED = ("conv_w", "out_gain")
REPLICATED = tuple(n for n in WEIGHT_NAMES if n not in COL_SHARDED + ROW_SHARDED + SMALL_SHARDED)
LOCAL_GRAD_KEY = {"ln_mix_g": "g_mix", "ln_ffn_g": "g_ffn", "ln_ple_g": "g_ple", "c_norm_g": "c_g", "c_norm_b": "c_b",
                  "w_ple_gate": "w_gate", "w_ple_proj": "w_proj"}


def _full_from_gathered(name, gathered):
    _, r, c = gathered.shape
    if name in ROW_SHARDED:
        return gathered.reshape(N_DEV * r, c)
    return jnp.transpose(gathered, (1, 0, 2)).reshape(r, N_DEV * c)


def _slots_from_full(name, full):
    rows, cols = full.shape
    if name in ROW_SHARDED:
        return full.reshape(N_DEV, rows // N_DEV, cols)
    return jnp.transpose(full.reshape(rows, N_DEV, cols // N_DEV), (1, 0, 2))


def _piece_rows(shape):
    return -(-int(np.prod(shape)) // 1024) * 8


def _pack_rows(arrays):
    pieces = []
    for a in arrays:
        n, rows = int(np.prod(a.shape)), _piece_rows(a.shape)
        flat = a.astype(F32).reshape(-1)
        if n != rows * LANES:
            flat = jnp.pad(flat, (0, rows * LANES - n))
        pieces.append(flat.reshape(rows, LANES))
    return jnp.concatenate(pieces, axis=0)


def _unpack_rows(packed, shapes):
    out, off = [], 0
    for shp in shapes:
        n, rows = int(np.prod(shp)), _piece_rows(shp)
        piece = packed[off:off + rows]
        out.append((piece if n == rows * LANES else piece.reshape(-1)[:n]).reshape(shp))
        off += rows
    return out


def kernel(x, p, rel_bias, ln_mix_g, w_in, qk_gain, sink, c_norm_g, c_norm_b, c_ws, c_bs, out_gain, w_out, ln_ffn_g, w_up, conv_w, conv_b, w_down, ln_ple_g, w_ple_gate, w_ple_proj, loss_target, m_rel_bias, m_ln_mix_g, m_w_in, m_qk_gain, m_sink, m_c_norm_g, m_c_norm_b, m_c_ws, m_c_bs, m_out_gain, m_w_out, m_ln_ffn_g, m_w_up, m_conv_w, m_conv_b, m_w_down, m_ln_ple_g, m_w_ple_gate, m_w_ple_proj, v_rel_bias, v_ln_mix_g, v_w_in, v_qk_gain, v_sink, v_c_norm_g, v_c_norm_b, v_c_ws, v_c_bs, v_out_gain, v_w_out, v_ln_ffn_g, v_w_up, v_conv_w, v_conv_b, v_w_down, v_ln_ple_g, v_w_ple_gate, v_w_ple_proj):
    env = dict(locals())
    wt = {n: env[n] for n in WEIGHT_NAMES}
    mom_m = {n: env["m_" + n] for n in WEIGHT_NAMES}
    mom_v = {n: env["v_" + n] for n in WEIGHT_NAMES}
    bl = x.shape[0]
    t = bl * SEQ
    me = 4 * lax.axis_index("x") + 2 * lax.axis_index("y") + lax.axis_index("c")

    big = COL_SHARDED + ROW_SHARDED
    full = {}
    small_shapes = [wt[n].shape for n in SMALL_SHARDED]
    small = _allgather_vmem(_pack_rows([wt[n] for n in SMALL_SHARDED]), name="gather_small")
    small = small.reshape(N_DEV, -1)
    off = 0
    for n, shp in zip(SMALL_SHARDED, small_shapes):
        cnt = int(np.prod(shp))
        g = small[:, off:off + cnt].reshape((N_DEV,) + tuple(shp))
        full[n] = jnp.transpose(g, (1, 2, 0, 3)).reshape(shp[0], shp[1], N_DEV * shp[2])
        off += _piece_rows(shp) * LANES

    def head_gain(li, a, b, reps):
        g = jnp.tile(qk_gain[li, a, b], reps)
        return jnp.pad(g, (0, 256 - g.shape[0]))

    wts = []
    for li in range(DEPTH):
        rows = [head_gain(li, 0, 0, 4), head_gain(li, 0, 1, 4), head_gain(li, 1, 0, 4), head_gain(li, 1, 1, 2),
                head_gain(li, 2, 0, 4), head_gain(li, 2, 1, 2), jnp.zeros((256,), F32), jnp.zeros((256,), F32)]
        wts.append(dict(
            g_mix=ln_mix_g[li].reshape(1, -1), qk_gains=jnp.stack(rows),
            sink_t=jnp.broadcast_to(sink[li][:, None, None], (4, 8, 128)),
            c_g=c_norm_g[li].reshape(1, -1), c_b=c_norm_b[li].reshape(1, -1), c_ws=c_ws[li].astype(BF16),
            c_wst=jnp.transpose(c_ws[li], (0, 2, 1)).astype(BF16), c_bst=jnp.repeat(c_bs[li].T, 64, axis=1),
            out_gain=full["out_gain"][li].reshape(1, -1), g_ffn=ln_ffn_g[li].reshape(1, -1),
            conv_w=full["conv_w"][li], conv_b=conv_b[li].reshape(1, -1), g_ple=ln_ple_g[li].reshape(1, -1)))

    local_key = {"w_ple_gate": "w_gate", "w_ple_proj": "w_proj"}

    gather_names = {"in": ("w_in",), "rest": tuple(n for n in big if n != "w_in")}
    gathered = {}
    for cid, (li, names) in enumerate(((0, gather_names["in"]), (0, gather_names["rest"]), (1, big))):
        lands = _sc_allgather([wt[n][li].astype(BF16) for n in names], collective_id=cid,
                              name=f"gather_{li}_{len(names)}")
        gathered.setdefault(li, {}).update(zip(names, lands))

    def matmul_weights(li, part, after):
        out = {}
        for n in gather_names[part]:
            g, _ = lax.optimization_barrier((gathered[li][n], after))
            out[local_key.get(n, n)] = _full_from_gathered(n, g)
        return out

    mid_names = ("w_ple_gate", "w_ple_proj", "w_down", "w_up", "w_out")
    end_names = ("w_in",)
    landed = {}

    def start_exchange(li, names, g, tag, cid):
        slots = [_slots_from_full(n, g[local_key.get(n, n)]) for n in names]
        lands = _sc_exchange(slots, scatter=True, collective_id=cid, name=f"grads_{li}_{tag}")
        landed.update({(n, li): land for n, land in zip(names, lands)})

    def grads_ready(li, stage, g):
        if li == 0:
            start_exchange(li, mid_names if stage == "mid" else end_names, g, stage, 5 if stage == "mid" else 6)
        elif stage == "end":
            start_exchange(li, mid_names + end_names, g, stage, 4)

    loss_part, dx, grads, d_rel_bias = _local_step(
        x.reshape(t, D_MODEL), p.reshape(DEPTH * t, PLE_DIM), loss_target.reshape(t, D_MODEL), rel_bias, wts,
        matmul_weights, grads_ready)
    loss = lax.psum(loss_part, ("x", "y", "c"))

    def local_grad(n):
        if n == "rel_bias":
            return d_rel_bias
        key = LOCAL_GRAD_KEY.get(n, n)
        return jnp.stack([grads[li][key].reshape(wt[n].shape[1:]) if n in REPLICATED else grads[li][key]
                          for li in range(DEPTH)])

    small_names = REPLICATED + SMALL_SHARDED
    small_full_shapes = [wt[n].shape if n in REPLICATED else full[n].shape for n in small_names]
    small_parts = _allgather_vmem(_pack_rows([local_grad(n) for n in small_names]), name="allgather_small_grads")
    small_parts = small_parts.reshape(N_DEV, -1, LANES)

    out_g, out_d, out_m, out_v = {}, {}, {}, {}
    for n in big:
        shp = wt[n].shape
        two_d = lambda a: a.reshape(-1, shp[-1])
        res = _adamw_reduce([landed[n, li] for li in range(DEPTH)], two_d(wt[n]), two_d(mom_m[n]), two_d(mom_v[n]),
                            tr=32 if n == "w_down" else 128, name="adamw_" + n)
        out_g[n], out_d[n], out_m[n], out_v[n] = [r.reshape(shp) for r in res]

    reduced = _sum_slots(small_parts, name="sum_small_grads")
    reduced = dict(zip(small_names, _unpack_rows(reduced, small_full_shapes)))
    rep_shapes = [wt[n].shape for n in REPLICATED]
    upd = _adamw_plain(_pack_rows([reduced[n] for n in REPLICATED]), _pack_rows([wt[n] for n in REPLICATED]),
                       _pack_rows([mom_m[n] for n in REPLICATED]), _pack_rows([mom_v[n] for n in REPLICATED]),
                       name="adamw_replicated")
    for dst, packed in zip((out_d, out_m, out_v), upd):
        dst.update(zip(REPLICATED, _unpack_rows(packed, rep_shapes)))
    for n in REPLICATED:
        out_g[n] = reduced[n]
    for n in SMALL_SHARDED:
        shp = wt[n].shape
        g = reduced[n].reshape(shp[0], shp[1], N_DEV, shp[2])
        g = lax.dynamic_index_in_dim(g, me, axis=2, keepdims=False)
        two_d = lambda a: a.reshape(-1, shp[-1])
        res = _adamw_plain(two_d(g), two_d(wt[n]), two_d(mom_m[n]), two_d(mom_v[n]), name="adamw_" + n)
        out_g[n] = g
        out_d[n], out_m[n], out_v[n] = [r.reshape(shp) for r in res]

    return (loss, dx.reshape(bl, SEQ, D_MODEL), *[out_g[n] for n in WEIGHT_NAMES], *[out_d[n] for n in WEIGHT_NAMES],
            *[out_m[n] for n in WEIGHT_NAMES], *[out_v[n] for n in WEIGHT_NAMES])
```

```python
import math

import jax
import jax.numpy as jnp
import numpy as np
from jax import lax
from jax.experimental import pallas as pl
from jax.experimental.pallas import tpu as pltpu
from jax.experimental.pallas import tpu_sc as plsc

F32 = jnp.float32
BF16 = jnp.bfloat16

N_DEV = 8
D_MODEL = 1024
SEQ = 2048
DEPTH = 2
HEAD_DIM = 64
IN_WIDTH = 2304
D_FF = 2816
PLE_DIM = 256
C_CHUNK = 128
C_GROUPS = 4
DILATED_CFGS = ((128, 1), (512, 4), (2048, 16))
DILATIONS = tuple(d for _, d in DILATED_CFGS)
A_RADIUS = 64
SWA_RADIUS = 128
BAND_BLOCK = 256
GRID_W = 64
ROPE_THETA = 10000.0
REL_BUCKETS = 32
REL_MAX_DIST = 1024
EPS = 1e-6
NEG_INF = -1e30
ATTN_SCALE = HEAD_DIM ** -0.5
LANES = 128

ADAM_LR = 0.001
ADAM_B1 = 0.9
ADAM_B2 = 0.999
ADAM_EPS = 1e-08
ADAM_WD = 0.01
ADAM_STEP = 10

MESH = pl.DeviceIdType.MESH
NT = (((1,), (1,)), ((), ()))
TN = (((0,), (0,)), ((), ()))
ARB = "arbitrary"
PAR = "parallel"


def _cparams(*sem):
    return pltpu.CompilerParams(dimension_semantics=tuple(sem))


def _sds(shape, dtype):
    return jax.ShapeDtypeStruct(tuple(shape), dtype)


def _group_sum_matrix(n, same_group):
    r = lax.broadcasted_iota(jnp.int32, (n, n), 0)
    c = lax.broadcasted_iota(jnp.int32, (n, n), 1)
    if same_group:
        return ((r >> 6) == (c >> 6)).astype(F32)
    return ((r & 63) == (c & 63)).astype(F32)


def _seg_sum(x, e):
    eb = e.astype(BF16)
    hi = x.astype(BF16)
    lo = (x - hi.astype(F32)).astype(BF16)
    return jnp.dot(hi, eb, preferred_element_type=F32) + jnp.dot(lo, eb, preferred_element_type=F32)


def _gelu(x):
    c = math.sqrt(2.0 / math.pi)
    return 0.5 * x * (1.0 + jnp.tanh(c * (x + 0.044715 * (x * x * x))))


def _gelu_grad(x):
    c = math.sqrt(2.0 / math.pi)
    t = jnp.tanh(c * (x + 0.044715 * (x * x * x)))
    return 0.5 * (1.0 + t) + 0.5 * x * (1.0 - t * t) * c * (1.0 + 3.0 * 0.044715 * (x * x))


def _sigmoid(x):
    return 1.0 / (1.0 + jnp.exp(-x))


def _scatter_cols(scratch, first, val):
    for c in range(val.shape[1] // LANES):
        scratch[first + c] = val[:, c * LANES:(c + 1) * LANES]


def _gather_cols(scratch, first, ncol):
    return jnp.concatenate([scratch[first + c] for c in range(ncol)], axis=1)


def _read_residue(scratch, first, ncol, r, d):
    n = scratch.shape[1] // d
    return jnp.concatenate([scratch.at[first + c][pl.ds(r, n, stride=d), :] for c in range(ncol)], axis=1)


def _write_residue(scratch, first, r, d, val):
    n = scratch.shape[1] // d
    for c in range(val.shape[1] // LANES):
        scratch.at[first + c][pl.ds(r, n, stride=d), :] = val[:, c * LANES:(c + 1) * LANES]


def _norm_mm(xs, gain, w, res, *, tm, tn, name, out_dtype=F32):
    t = xs[0].shape[0]
    k = sum(x.shape[1] for x in xs)
    n = w.shape[1]
    ng = len(xs)
    has_res = res is not None

    def body(*refs):
        x_refs = refs[:ng]
        g_ref, w_ref = refs[ng], refs[ng + 1]
        res_ref = refs[ng + 2] if has_res else None
        hn_ref, o_ref, hn_s = refs[ng + 2 + has_res:]

        @pl.when(pl.program_id(1) == 0)
        def _():
            off = 0
            for xr in x_refs:
                x = xr[...]
                wd = x.shape[1]
                r = lax.rsqrt(jnp.mean(x * x, axis=-1, keepdims=True) + EPS)
                hn_s[:, off:off + wd] = (x * r * g_ref[:, off:off + wd]).astype(BF16)
                off += wd
            hn_ref[...] = hn_s[...]

        acc = jnp.dot(hn_s[...], w_ref[...], preferred_element_type=F32)
        if has_res:
            acc = acc + res_ref[...]
        o_ref[...] = acc.astype(out_dtype)

    in_specs = [pl.BlockSpec((tm, x.shape[1]), lambda i, j: (i, 0)) for x in xs]
    in_specs += [pl.BlockSpec((1, k), lambda i, j: (0, 0)), pl.BlockSpec((k, tn), lambda i, j: (0, j))]
    args = list(xs) + [gain, w]
    if has_res:
        in_specs.append(pl.BlockSpec((tm, tn), lambda i, j: (i, j)))
        args.append(res)
    return pl.pallas_call(
        body, name=name, grid=(t // tm, n // tn), in_specs=in_specs,
        out_specs=[pl.BlockSpec((tm, k), lambda i, j: (i, 0)), pl.BlockSpec((tm, tn), lambda i, j: (i, j))],
        out_shape=[_sds((t, k), BF16), _sds((t, n), out_dtype)],
        scratch_shapes=[pltpu.VMEM((tm, k), BF16)],
        compiler_params=_cparams(PAR, ARB),
    )(*args)


def _mm(a, b, mode, res, *, tm, tn, out_dtype, name, a_rows=None):
    if mode == "tn":
        kk, m = a.shape
        blk_a = 0
        if a_rows is not None:
            blk_a, kk = a_rows
        a_spec = pl.BlockSpec((kk, tm), lambda i, j: (blk_a, i))
    else:
        m, kk = a.shape
        a_spec = pl.BlockSpec((tm, kk), lambda i, j: (i, 0))
    if mode == "nt":
        n = b.shape[0]
        b_spec = pl.BlockSpec((tn, kk), lambda i, j: (j, 0))
    else:
        n = b.shape[1]
        b_spec = pl.BlockSpec((kk, tn), lambda i, j: (0, j))
    has_res = res is not None

    def body(*refs):
        a_ref, b_ref = refs[0], refs[1]
        o_ref = refs[-1]
        av = a_ref[...].astype(BF16)
        bv = b_ref[...].astype(BF16)
        if mode == "nn":
            acc = jnp.dot(av, bv, preferred_element_type=F32)
        elif mode == "nt":
            acc = lax.dot_general(av, bv, NT, preferred_element_type=F32)
        else:
            acc = lax.dot_general(av, bv, TN, preferred_element_type=F32)
        if has_res:
            acc = acc + refs[2][...]
        o_ref[...] = acc.astype(out_dtype)

    in_specs = [a_spec, b_spec]
    args = [a, b]
    if has_res:
        in_specs.append(pl.BlockSpec((tm, tn), lambda i, j: (i, j)))
        args.append(res)
    return pl.pallas_call(
        body, name=name, grid=(m // tm, n // tn), in_specs=in_specs,
        out_specs=pl.BlockSpec((tm, tn), lambda i, j: (i, j)),
        out_shape=_sds((m, n), out_dtype),
        compiler_params=_cparams(PAR, PAR),
    )(*args)


def _mm_bt_normbwd(dys, w, xs, gain, dres, *, tm, tn, name, emit_bf16=False):
    t, wd_each = dys[0].shape
    nd = len(dys)
    per = wd_each // tn
    nj = nd * per
    k = w.shape[0]
    ng = len(xs)
    has_res = dres is not None

    def body(*refs):
        dy_refs = refs[:nd]
        w_ref = refs[nd]
        x_refs = refs[nd + 1:nd + 1 + ng]
        g_ref = refs[nd + 1 + ng]
        dres_ref = refs[nd + 2 + ng] if has_res else None
        outs = refs[nd + 2 + ng + has_res:]
        dx_ref = outs[0]
        dxb_ref = outs[1] if emit_bf16 else None
        dg_ref, acc = outs[1 + emit_bf16:]
        i, j = pl.program_id(0), pl.program_id(1)

        @pl.when(j == 0)
        def _():
            acc[...] = jnp.zeros_like(acc)

        for d, dy_ref in enumerate(dy_refs):
            @pl.when((j >= d * per) & (j < (d + 1) * per))
            def _(dy_ref=dy_ref):
                acc[...] += lax.dot_general(dy_ref[...].astype(BF16), w_ref[...], NT, preferred_element_type=F32)

        @pl.when(j == nj - 1)
        def _():
            @pl.when(i == 0)
            def _():
                dg_ref[...] = jnp.zeros_like(dg_ref)

            off = 0
            for xr in x_refs:
                x = xr[...]
                wd = x.shape[1]
                g = g_ref[:, off:off + wd]
                dyn = acc[:, off:off + wd]
                r = lax.rsqrt(jnp.mean(x * x, axis=-1, keepdims=True) + EPS)
                gdy = dyn * g
                dx = r * gdy - x * (r * r * r * jnp.mean(gdy * x, axis=-1, keepdims=True))
                if has_res:
                    dx = dx + dres_ref[:, off:off + wd]
                dx_ref[:, off:off + wd] = dx
                if emit_bf16:
                    dxb_ref[:, off:off + wd] = dx.astype(BF16)
                dg_ref[:, off:off + wd] += jnp.sum(dyn * x * r, axis=0, keepdims=True)
                off += wd

    def dy_map(d):
        return lambda i, j: (i, jnp.clip(j - d * per, 0, per - 1))

    in_specs = [pl.BlockSpec((tm, tn), dy_map(d)) for d in range(nd)]
    in_specs.append(pl.BlockSpec((k, tn), lambda i, j: (0, j)))
    in_specs += [pl.BlockSpec((tm, x.shape[1]), lambda i, j: (i, 0)) for x in xs]
    in_specs.append(pl.BlockSpec((1, k), lambda i, j: (0, 0)))
    args = list(dys) + [w] + list(xs) + [gain]
    if has_res:
        in_specs.append(pl.BlockSpec((tm, k), lambda i, j: (i, 0)))
        args.append(dres)
    row = pl.BlockSpec((tm, k), lambda i, j: (i, 0))
    out_specs = [row] + ([row] if emit_bf16 else []) + [pl.BlockSpec((1, k), lambda i, j: (0, 0))]
    out_shape = [_sds((t, k), F32)] + ([_sds((t, k), BF16)] if emit_bf16 else []) + [_sds((1, k), F32)]
    return pl.pallas_call(
        body, name=name, grid=(t // tm, nj), in_specs=in_specs, out_specs=out_specs, out_shape=out_shape,
        scratch_shapes=[pltpu.VMEM((tm, k), F32)],
        compiler_params=_cparams(ARB, ARB),
    )(*args)


def _rope_partner(y):
    n = y.shape[1]
    lane = lax.broadcasted_iota(jnp.int32, y.shape, 1)
    return jnp.where((lane & 31) < 16, pltpu.roll(y, n - 16, 1), pltpu.roll(y, 16, 1))


def _residue_specs(tm, width, nt):
    specs = [pl.BlockSpec((tm, width), lambda b, i: (b * nt + i, 0))]
    for d in DILATIONS[1:]:
        specs.append(pl.BlockSpec((None, d, tm // d, width), lambda b, i: (b, 0, i, 0)))
    return specs


def _residue_shapes(bl, width, dtype):
    return [_sds((bl * SEQ, width), dtype)] + [_sds((bl, d, SEQ // d, width), dtype) for d in DILATIONS[1:]]


def _qkprep_fwd(proj, gains, cos, sins, *, tm, name):
    t = proj.shape[0]
    bl = t // SEQ
    nt = SEQ // tm

    def body(p_ref, g_ref, c_ref, s_ref, qa1_ref, qa4_ref, qa16_ref, qb_ref, qd_ref, scr):
        e = _group_sum_matrix(256, True)

        def hn(x, row):
            x = x.astype(F32)
            wd = x.shape[1]
            ms = _seg_sum(x * x, e[:wd, :wd]) * (1.0 / HEAD_DIM)
            return x * lax.rsqrt(ms + EPS) * g_ref[row:row + 1, :wd]

        qa = jnp.concatenate([hn(p_ref[:, 0:256], 0) * ATTN_SCALE, hn(p_ref[:, 256:512], 1),
                              p_ref[:, 512:768].astype(F32)], axis=1)
        qa1_ref[...] = qa.astype(BF16)
        _scatter_cols(scr, 0, qa)
        for d, ref in ((4, qa4_ref), (16, qa16_ref)):
            for r in range(d):
                ref[r] = _read_residue(scr, 0, 6, r, d).astype(BF16)
        qb_ref[:, 0:256] = (hn(p_ref[:, 768:1024], 2) * ATTN_SCALE).astype(BF16)
        qb_ref[:, 256:384] = hn(p_ref[:, 1024:1152], 3).astype(BF16)
        qb_ref[:, 384:512] = p_ref[:, 1152:1280].astype(BF16)
        yq = hn(p_ref[:, 1792:2048], 4)
        yq = yq * c_ref[...] + _rope_partner(yq) * s_ref[...]
        qd_ref[:, 0:256] = (yq * ATTN_SCALE).astype(BF16)
        yk = hn(p_ref[:, 2048:2176], 5)
        yk = yk * c_ref[:, 0:128] + _rope_partner(yk) * s_ref[:, 0:128]
        qd_ref[:, 256:384] = yk.astype(BF16)
        qd_ref[:, 384:512] = p_ref[:, 2176:2304].astype(BF16)

    row = lambda width: pl.BlockSpec((tm, width), lambda b, i: (b * nt + i, 0))
    tab = pl.BlockSpec((tm, 256), lambda b, i: (i, 0))
    return pl.pallas_call(
        body, name=name, grid=(bl, nt),
        in_specs=[row(IN_WIDTH), pl.BlockSpec((8, 256), lambda b, i: (0, 0)), tab, tab],
        out_specs=_residue_specs(tm, 768, nt) + [row(512), row(512)],
        out_shape=_residue_shapes(bl, 768, BF16) + [_sds((t, 512), BF16), _sds((t, 512), BF16)],
        scratch_shapes=[pltpu.VMEM((6, tm, LANES), F32)],
        compiler_params=_cparams(PAR, PAR),
    )(proj, gains, cos, sins)


def _qkprep_bwd(proj, da, db, dd, dcu, dcv, gains, cos, sins, *, tm, name):
    t = proj.shape[0]
    bl = t // SEQ
    nt = SEQ // tm
    flat = [a for cfg in da for a in cfg] + list(db) + list(dd) + [dcu, dcv]

    def body(*refs):
        p_ref, g_ref, c_ref, s_ref = refs[:4]
        d_refs = refs[4:4 + len(flat)]
        dp_ref, dg_ref, scr = refs[4 + len(flat):]
        a_refs = d_refs[:9]
        dqb_ref, dkb_ref, dvb_ref, dqd_ref, dkd_ref, dvd_ref, dcu_ref, dcv_ref = d_refs[9:]
        e = _group_sum_matrix(256, True)
        first = (pl.program_id(0) == 0) & (pl.program_id(1) == 0)
        last = (pl.program_id(0) == bl - 1) & (pl.program_id(1) == nt - 1)

        @pl.when(first)
        def _():
            dg_ref[...] = jnp.zeros_like(dg_ref)

        def hn_bwd(x, dy, row):
            x, dy = x.astype(F32), dy.astype(F32)
            wd = x.shape[1]
            ee = e[:wd, :wd]
            g = g_ref[row:row + 1, :wd]
            r = lax.rsqrt(_seg_sum(x * x, ee) * (1.0 / HEAD_DIM) + EPS)
            gdy = dy * g
            dx = r * gdy - x * (r * r * r * (_seg_sum(gdy * x, ee) * (1.0 / HEAD_DIM)))
            dg_ref[row:row + 1, :wd] += jnp.sum(dy * x * r, axis=0, keepdims=True)
            return dx

        def rope_bwd(dy, wd):
            dy = dy.astype(F32)
            return dy * c_ref[:, :wd] + _rope_partner(dy * s_ref[:, :wd])

        dqkv = jnp.concatenate([a_refs[m][...].astype(F32) for m in range(3)], axis=1)
        for ci, d in ((1, 4), (2, 16)):
            for r in range(d):
                part = jnp.concatenate([a_refs[3 * ci + m][r].astype(F32) for m in range(3)], axis=1)
                _write_residue(scr, 0, r, d, part)
            dqkv = dqkv + _gather_cols(scr, 0, 6)
        dp_ref[:, 0:256] = hn_bwd(p_ref[:, 0:256], dqkv[:, 0:256] * ATTN_SCALE, 0).astype(BF16)
        dp_ref[:, 256:512] = hn_bwd(p_ref[:, 256:512], dqkv[:, 256:512], 1).astype(BF16)
        dp_ref[:, 512:768] = dqkv[:, 512:768].astype(BF16)
        dp_ref[:, 768:1024] = hn_bwd(p_ref[:, 768:1024], dqb_ref[...] * ATTN_SCALE, 2).astype(BF16)
        dp_ref[:, 1024:1152] = hn_bwd(p_ref[:, 1024:1152], dkb_ref[...], 3).astype(BF16)
        dp_ref[:, 1152:1280] = dvb_ref[...].astype(BF16)
        dp_ref[:, 1280:1536] = dcu_ref[...].astype(BF16)
        dp_ref[:, 1536:1792] = dcv_ref[...].astype(BF16)
        dp_ref[:, 1792:2048] = hn_bwd(p_ref[:, 1792:2048], rope_bwd(dqd_ref[...] * ATTN_SCALE, 256), 4).astype(BF16)
        dp_ref[:, 2048:2176] = hn_bwd(p_ref[:, 2048:2176], rope_bwd(dkd_ref[...], 128), 5).astype(BF16)
        dp_ref[:, 2176:2304] = dvd_ref[...].astype(BF16)

        @pl.when(last)
        def _():
            dg_ref[...] = _seg_sum(dg_ref[...], _group_sum_matrix(256, False))

    row = lambda width: pl.BlockSpec((tm, width), lambda b, i: (b * nt + i, 0))
    tab = pl.BlockSpec((tm, 256), lambda b, i: (i, 0))
    in_specs = [row(IN_WIDTH), pl.BlockSpec((8, 256), lambda b, i: (0, 0)), tab, tab]
    res_specs = _residue_specs(tm, 256, nt)
    in_specs += [res_specs[ci] for ci in range(3) for _ in range(3)]
    in_specs += [row(a.shape[1]) for a in flat[9:]]
    return pl.pallas_call(
        body, name=name, grid=(bl, nt), in_specs=in_specs,
        out_specs=[row(IN_WIDTH), pl.BlockSpec((8, 256), lambda b, i: (0, 0))],
        out_shape=[_sds((t, IN_WIDTH), BF16), _sds((8, 256), F32)],
        scratch_shapes=[pltpu.VMEM((6, tm, LANES), F32)],
        compiler_params=_cparams(ARB, ARB),
    )(proj, gains, cos, sins, *flat)


BAND_ROWS_PER_STEP = 512


def _residues_per_step(dil, seq_len):
    return min(dil, max(1, BAND_ROWS_PER_STEP // seq_len))


def _band_spec(seq_len, spec, rb):
    width, idx = spec
    return pl.BlockSpec((None, rb, seq_len, width), lambda b, r: (b, r, 0, idx))


def _fill_padded(dst, src_ref, rad, seq_len):
    z = jnp.zeros((rad, dst.shape[1]), dst.dtype)
    dst[0:rad, :] = z
    dst[rad + seq_len:rad + seq_len + rad, :] = z
    dst[rad:rad + seq_len, :] = src_ref[...]


def _band_fwd(src, qs, ks, vs, bias, sink, *, rad, nh, nkv, name):
    bl, dil, sl, _ = src.shape
    blk = bias.shape[1]
    kw = blk + 2 * rad
    nb = sl // blk
    rep = nh // nkv
    has_sink = sink is not None
    rb = _residues_per_step(dil, sl)

    def body(*refs):
        q_all, k_all, v_all, b_ref = refs[:4]
        s_ref = refs[4] if has_sink else None
        o_all, l_all, kp, vp = refs[4 + has_sink:]
        for ri in range(rb):
            one_sequence(q_all.at[ri], k_all.at[ri], v_all.at[ri], b_ref, s_ref, o_all.at[ri], l_all.at[ri], kp, vp)

    def one_sequence(q_ref, k_ref, v_ref, b_ref, s_ref, o_ref, l_ref, kp, vp):
        _fill_padded(kp, k_ref, rad, sl)
        _fill_padded(vp, v_ref, rad, sl)

        def blk_body(i, carry):
            r0 = pl.multiple_of(i * blk, blk)
            qb = q_ref[pl.ds(r0, blk), :]
            kwin = kp[pl.ds(r0, kw), :]
            vwin = vp[pl.ds(r0, kw), :]
            col = r0 - rad + lax.broadcasted_iota(jnp.int32, (blk, kw), 1)
            neg = jnp.where((col >= 0) & (col < sl), 0.0, NEG_INF).astype(F32)
            for h in range(nh):
                g = h // rep
                hs = slice(h * HEAD_DIM, (h + 1) * HEAD_DIM)
                gs = slice(g * HEAD_DIM, (g + 1) * HEAD_DIM)
                s = lax.dot_general(qb[:, hs], kwin[:, gs], NT, preferred_element_type=F32)
                s = s + b_ref[h] + neg
                m = jnp.max(s, axis=1, keepdims=True)
                if has_sink:
                    sk = s_ref[h][0:1, 0:1]
                    m = jnp.maximum(m, sk)
                p = jnp.exp(s - m)
                den = jnp.sum(p, axis=1, keepdims=True)
                if has_sink:
                    den = den + jnp.exp(sk - m)
                o = jnp.dot(p.astype(BF16), vwin[:, gs], preferred_element_type=F32) / den
                o_ref[pl.ds(r0, blk), hs] = o
                l_ref[pl.ds(r0, blk), hs] = jnp.broadcast_to(m + jnp.log(den), (blk, HEAD_DIM))
            return carry

        lax.fori_loop(0, nb, blk_body, 0)

    in_specs = [_band_spec(sl, qs, rb), _band_spec(sl, ks, rb), _band_spec(sl, vs, rb),
                pl.BlockSpec((nh, blk, kw), lambda b, r: (0, 0, 0))]
    args = [src] * 3 + [bias]
    if has_sink:
        in_specs.append(pl.BlockSpec((nh, 8, 128), lambda b, r: (0, 0, 0)))
        args.append(sink)
    return pl.pallas_call(
        body, name=name, grid=(bl, dil // rb), in_specs=in_specs,
        out_specs=[_band_spec(sl, (256, 0), rb)] * 2,
        out_shape=[_sds((bl, dil, sl, 256), F32)] * 2,
        scratch_shapes=[pltpu.VMEM((sl + 2 * rad, ks[0]), BF16), pltpu.VMEM((sl + 2 * rad, vs[0]), BF16)],
        compiler_params=_cparams(PAR, PAR),
    )(*args)


def _band_bwd(src, qs, ks, vs, bias, sink, dy, dcol, lse, delta, *, rad, nh, nkv, name):
    bl, dil, sl, _ = src.shape
    blk = bias.shape[1]
    kw = blk + 2 * rad
    nb = sl // blk
    rep = nh // nkv
    has_sink = sink is not None
    rb = _residues_per_step(dil, sl)
    wk, wv = ks[0], vs[0]

    def body(*refs):
        q_all, k_all, v_all, b_ref = refs[:4]
        s_ref = refs[4] if has_sink else None
        do_all, l_all, dl_all = refs[4 + has_sink:7 + has_sink]
        outs = refs[7 + has_sink:]
        dsk_ref = None
        if has_sink:
            dq_all, dk_all, dv_all, db_ref, dsk_ref, kp, vp, dka, dva = outs
        else:
            dq_all, dk_all, dv_all, db_ref, kp, vp, dka, dva = outs

        @pl.when((pl.program_id(0) == 0) & (pl.program_id(1) == 0))
        def _():
            db_ref[...] = jnp.zeros_like(db_ref)
            if has_sink:
                dsk_ref[...] = jnp.zeros_like(dsk_ref)

        for ri in range(rb):
            one_sequence(q_all.at[ri], k_all.at[ri], v_all.at[ri], b_ref, s_ref, do_all.at[ri], l_all.at[ri],
                         dl_all.at[ri], dq_all.at[ri], dk_all.at[ri], dv_all.at[ri], db_ref, dsk_ref, kp, vp, dka, dva)

    def one_sequence(q_ref, k_ref, v_ref, b_ref, s_ref, do_ref, l_ref, dl_ref, dq_ref, dk_ref, dv_ref, db_ref, dsk_ref,
                     kp, vp, dka, dva):
        _fill_padded(kp, k_ref, rad, sl)
        _fill_padded(vp, v_ref, rad, sl)
        dka[...] = jnp.zeros_like(dka)
        dva[...] = jnp.zeros_like(dva)

        def blk_body(i, carry):
            r0 = pl.multiple_of(i * blk, blk)
            qb = q_ref[pl.ds(r0, blk), :]
            kwin = kp[pl.ds(r0, kw), :]
            vwin = vp[pl.ds(r0, kw), :]
            dob = do_ref[pl.ds(r0, blk), :].astype(BF16)
            lb = l_ref[pl.ds(r0, blk), :]
            dlb = dl_ref[pl.ds(r0, blk), :]
            col = r0 - rad + lax.broadcasted_iota(jnp.int32, (blk, kw), 1)
            neg = jnp.where((col >= 0) & (col < sl), 0.0, NEG_INF).astype(F32)
            for h in range(nh):
                g = h // rep
                hs = slice(h * HEAD_DIM, (h + 1) * HEAD_DIM)
                gs = slice(g * HEAD_DIM, (g + 1) * HEAD_DIM)
                qh, kh, vh, doh = qb[:, hs], kwin[:, gs], vwin[:, gs], dob[:, hs]
                lh = lb[:, h * HEAD_DIM:h * HEAD_DIM + 1]
                dlh = dlb[:, h * HEAD_DIM:h * HEAD_DIM + 1]
                s = lax.dot_general(qh, kh, NT, preferred_element_type=F32) + b_ref[h] + neg
                p = jnp.exp(s - lh)
                dp = lax.dot_general(doh, vh, NT, preferred_element_type=F32)
                ds = p * (dp - dlh)
                dsb = ds.astype(BF16)
                dq_ref[pl.ds(r0, blk), hs] = jnp.dot(dsb, kh, preferred_element_type=F32).astype(BF16)
                dka[pl.ds(r0, kw), gs] += lax.dot_general(dsb, qh, TN, preferred_element_type=F32)
                dva[pl.ds(r0, kw), gs] += lax.dot_general(p.astype(BF16), doh, TN, preferred_element_type=F32)
                db_ref[h] += ds
                if has_sink:
                    ps = jnp.exp(s_ref[h][0:1, 0:1] - lh)
                    dsk_ref[h] += jnp.broadcast_to(-jnp.sum(ps * dlh, axis=0, keepdims=True), (8, 128))
            return carry

        lax.fori_loop(0, nb, blk_body, 0)
        dk_ref[...] = dka[rad:rad + sl, :].astype(BF16)
        dv_ref[...] = dva[rad:rad + sl, :].astype(BF16)

    const3 = lambda b, r: (0, 0, 0)
    in_specs = [_band_spec(sl, qs, rb), _band_spec(sl, ks, rb), _band_spec(sl, vs, rb),
                pl.BlockSpec((nh, blk, kw), const3)]
    args = [src] * 3 + [bias]
    if has_sink:
        in_specs.append(pl.BlockSpec((nh, 8, 128), const3))
        args.append(sink)
    row = _band_spec(sl, (256, 0), rb)
    in_specs += [_band_spec(sl, (256, dcol), rb), row, row]
    args += [dy, lse, delta]
    out_specs = [row, _band_spec(sl, (wk, 0), rb), _band_spec(sl, (wv, 0), rb), pl.BlockSpec((nh, blk, kw), const3)]
    out_shape = [_sds((bl, dil, sl, 256), BF16), _sds((bl, dil, sl, wk), BF16), _sds((bl, dil, sl, wv), BF16),
                 _sds((nh, blk, kw), F32)]
    if has_sink:
        out_specs.append(pl.BlockSpec((nh, 8, 128), const3))
        out_shape.append(_sds((nh, 8, 128), F32))
    return pl.pallas_call(
        body, name=name, grid=(bl, dil // rb), in_specs=in_specs, out_specs=out_specs, out_shape=out_shape,
        scratch_shapes=[pltpu.VMEM((sl + 2 * rad, wk), BF16), pltpu.VMEM((sl + 2 * rad, wv), BF16),
                        pltpu.VMEM((sl + 2 * rad, wk), F32), pltpu.VMEM((sl + 2 * rad, wv), F32)],
        compiler_params=_cparams(ARB, ARB),
    )(*args)


def _combine_a(os_, ls_, *, tm, name):
    bl = os_[1].shape[0]
    t = bl * SEQ
    nt = SEQ // tm

    def body(o1, o4, o16, l1, l4, l16, y_ref, lt_ref, scr):
        for k, (d, ref) in enumerate(((4, o4), (16, o16), (4, l4), (16, l16))):
            for r in range(d):
                _write_residue(scr, 2 * k, r, d, ref[r])
        o2, o3, b, c = (_gather_cols(scr, 2 * k, 2) for k in range(4))
        a = l1[...]
        m = jnp.maximum(jnp.maximum(a, b), c)
        ea, eb, ec = jnp.exp(a - m), jnp.exp(b - m), jnp.exp(c - m)
        den = ea + eb + ec
        y_ref[...] = (ea / den) * o1[...] + (eb / den) * o2 + (ec / den) * o3
        lt_ref[...] = m + jnp.log(den)

    specs = _residue_specs(tm, 256, nt)
    return pl.pallas_call(
        body, name=name, grid=(bl, nt), in_specs=specs * 2, out_specs=[specs[0]] * 2,
        out_shape=[_sds((t, 256), F32)] * 2, scratch_shapes=[pltpu.VMEM((8, tm, LANES), F32)],
        compiler_params=_cparams(PAR, PAR),
    )(*os_, *ls_)


def _deltas(dycat, ya, yb, yd, lse_a, *, tm, name):
    t = ya.shape[0]
    bl = t // SEQ
    nt = SEQ // tm

    def body(dy_ref, ya_ref, yb_ref, yd_ref, la_ref, dy4, dy16, l4, l16, da1, da4, da16, db_ref, dd_ref, scr):
        e = _group_sum_matrix(256, True)
        dya = dy_ref[:, 0:256]
        dla = _seg_sum(dya * ya_ref[...], e)
        da1[...] = dla
        db_ref[...] = _seg_sum(dy_ref[:, 256:512] * yb_ref[...], e)
        dd_ref[...] = _seg_sum(dy_ref[:, 768:1024] * yd_ref[...], e)
        for k, (val, r4, r16) in enumerate(((dya, dy4, dy16), (la_ref[...], l4, l16), (dla, da4, da16))):
            _scatter_cols(scr, 2 * k, val)
            for d, ref in ((4, r4), (16, r16)):
                for r in range(d):
                    ref[r] = _read_residue(scr, 2 * k, 2, r, d)

    specs = _residue_specs(tm, 256, nt)
    nat = specs[0]
    shapes = _residue_shapes(bl, 256, F32)
    outs = pl.pallas_call(
        body, name=name, grid=(bl, nt),
        in_specs=[pl.BlockSpec((tm, 1024), lambda b, i: (b * nt + i, 0)), nat, nat, nat, nat],
        out_specs=specs[1:] + specs[1:] + specs + [nat, nat],
        out_shape=shapes[1:] + shapes[1:] + shapes + [shapes[0], shapes[0]],
        scratch_shapes=[pltpu.VMEM((6, tm, LANES), F32)],
        compiler_params=_cparams(PAR, PAR),
    )(dycat, ya, yb, yd, lse_a)
    return outs[0:2], outs[2:4], outs[4:7], outs[7], outs[8]


def _dense_fwd(qd, *, tq, name):
    t = qd.shape[0]
    bl = t // SEQ
    nq = SEQ // tq

    def body(q_ref, k_ref, v_ref, o_ref, l_ref):
        q = q_ref[...]
        for g in range(2):
            h0, h1 = 2 * g, 2 * g + 1
            q2 = jnp.concatenate([q[:, h0 * 64:(h0 + 1) * 64], q[:, h1 * 64:(h1 + 1) * 64]], axis=0)
            kg = k_ref[:, g * 64:(g + 1) * 64]
            vg = v_ref[:, g * 64:(g + 1) * 64]
            s = lax.dot_general(q2, kg, NT, preferred_element_type=F32)
            m = jnp.max(s, axis=1, keepdims=True)
            p = jnp.exp(s - m)
            den = jnp.sum(p, axis=1, keepdims=True)
            o2 = jnp.dot(p.astype(BF16), vg, preferred_element_type=F32) / den
            l2 = jnp.broadcast_to(m + jnp.log(den), (2 * tq, 64))
            o_ref[:, h0 * 64:(h0 + 1) * 64] = o2[:tq]
            o_ref[:, h1 * 64:(h1 + 1) * 64] = o2[tq:]
            l_ref[:, h0 * 64:(h0 + 1) * 64] = l2[:tq]
            l_ref[:, h1 * 64:(h1 + 1) * 64] = l2[tq:]

    q3 = qd.reshape(bl, SEQ, 512)
    o, lse = pl.pallas_call(
        body, name=name, grid=(bl, nq),
        in_specs=[pl.BlockSpec((None, tq, 256), lambda b, i: (b, i, 0)),
                  pl.BlockSpec((None, SEQ, 128), lambda b, i: (b, 0, 2)),
                  pl.BlockSpec((None, SEQ, 128), lambda b, i: (b, 0, 3))],
        out_specs=[pl.BlockSpec((None, tq, 256), lambda b, i: (b, i, 0))] * 2,
        out_shape=[_sds((bl, SEQ, 256), F32)] * 2,
        compiler_params=_cparams(PAR, PAR),
    )(q3, q3, q3)
    return o.reshape(t, 256), lse.reshape(t, 256)


def _dense_bwd(qd, dycat, lse, delta, *, tq, name):
    t = qd.shape[0]
    bl = t // SEQ
    nq = SEQ // tq

    def body(q_ref, k_ref, v_ref, do_ref, l_ref, dl_ref, dq_ref, dk_ref, dv_ref, dkt, dvt):
        @pl.when(pl.program_id(1) == 0)
        def _():
            dkt[...] = jnp.zeros_like(dkt)
            dvt[...] = jnp.zeros_like(dvt)

        q = q_ref[...]
        do = do_ref[...].astype(BF16)
        lv = l_ref[...]
        dlv = dl_ref[...]
        for g in range(2):
            h0, h1 = 2 * g, 2 * g + 1
            q2 = jnp.concatenate([q[:, h0 * 64:(h0 + 1) * 64], q[:, h1 * 64:(h1 + 1) * 64]], axis=0)
            do2 = jnp.concatenate([do[:, h0 * 64:(h0 + 1) * 64], do[:, h1 * 64:(h1 + 1) * 64]], axis=0)
            l2 = jnp.concatenate([lv[:, h0 * 64:h0 * 64 + 1], lv[:, h1 * 64:h1 * 64 + 1]], axis=0)
            dl2 = jnp.concatenate([dlv[:, h0 * 64:h0 * 64 + 1], dlv[:, h1 * 64:h1 * 64 + 1]], axis=0)
            kg = k_ref[:, g * 64:(g + 1) * 64]
            vg = v_ref[:, g * 64:(g + 1) * 64]
            s = lax.dot_general(q2, kg, NT, preferred_element_type=F32)
            p = jnp.exp(s - l2)
            dp = lax.dot_general(do2, vg, NT, preferred_element_type=F32)
            ds = (p * (dp - dl2)).astype(BF16)
            dq2 = jnp.dot(ds, kg, preferred_element_type=F32)
            dq_ref[:, h0 * 64:(h0 + 1) * 64] = dq2[:tq].astype(BF16)
            dq_ref[:, h1 * 64:(h1 + 1) * 64] = dq2[tq:].astype(BF16)
            dkt[g * 64:(g + 1) * 64, :] += lax.dot_general(q2, ds, TN, preferred_element_type=F32)
            dvt[g * 64:(g + 1) * 64, :] += lax.dot_general(do2, p.astype(BF16), TN, preferred_element_type=F32)

        @pl.when(pl.program_id(1) == nq - 1)
        def _():
            dk_ref[...] = dkt[...].T.astype(BF16)
            dv_ref[...] = dvt[...].T.astype(BF16)

    q3 = qd.reshape(bl, SEQ, 512)
    tile = pl.BlockSpec((None, tq, 256), lambda b, i: (b, i, 0))
    full = pl.BlockSpec((None, SEQ, 128), lambda b, i: (b, 0, 0))
    dq, dk, dv = pl.pallas_call(
        body, name=name, grid=(bl, nq),
        in_specs=[tile, pl.BlockSpec((None, SEQ, 128), lambda b, i: (b, 0, 2)),
                  pl.BlockSpec((None, SEQ, 128), lambda b, i: (b, 0, 3)),
                  pl.BlockSpec((None, tq, 256), lambda b, i: (b, i, 3)), tile, tile],
        out_specs=[tile, full, full],
        out_shape=[_sds((bl, SEQ, 256), BF16), _sds((bl, SEQ, 128), BF16), _sds((bl, SEQ, 128), BF16)],
        scratch_shapes=[pltpu.VMEM((128, SEQ), F32), pltpu.VMEM((128, SEQ), F32)],
        compiler_params=_cparams(PAR, ARB),
    )(q3, q3, q3, dycat.reshape(bl, SEQ, 1024), lse.reshape(bl, SEQ, 256), delta.reshape(bl, SEQ, 256))
    return dq.reshape(t, 256), dk.reshape(t, 128), dv.reshape(t, 128)


def _c_norm(cv, gam, bet):
    vg = _gelu(cv)
    mu = jnp.mean(vg, axis=-1, keepdims=True)
    xc = vg - mu
    r = lax.rsqrt(jnp.mean(xc * xc, axis=-1, keepdims=True) + EPS)
    xhat = xc * r
    return xhat * gam + bet, xhat, r


def _c_fwd(proj, gam, bet, ws, bst, *, tm, name):
    t = proj.shape[0]
    nch = tm // C_CHUNK

    def body(u_ref, v_ref, g_ref, b_ref, ws_ref, bs_ref, y_ref):
        vn, _, _ = _c_norm(v_ref[...].astype(F32), g_ref[...], b_ref[...])
        vnb = vn.astype(BF16)
        for c in range(nch):
            rows = slice(c * C_CHUNK, (c + 1) * C_CHUNK)
            for g in range(C_GROUPS):
                gs = slice(g * 64, (g + 1) * 64)
                mixed = jnp.dot(ws_ref[g], vnb[rows, gs], preferred_element_type=F32) + bs_ref[:, gs]
                y_ref[rows, gs] = _gelu(u_ref[rows, gs].astype(F32)) * mixed

    vec = pl.BlockSpec((1, 256), lambda i: (0, 0))
    return pl.pallas_call(
        body, name=name, grid=(t // tm,),
        in_specs=[pl.BlockSpec((tm, 256), lambda i: (i, 5)), pl.BlockSpec((tm, 256), lambda i: (i, 6)), vec, vec,
                  pl.BlockSpec((C_GROUPS, C_CHUNK, C_CHUNK), lambda i: (0, 0, 0)),
                  pl.BlockSpec((C_CHUNK, 256), lambda i: (0, 0))],
        out_specs=pl.BlockSpec((tm, 256), lambda i: (i, 0)), out_shape=_sds((t, 256), F32),
        compiler_params=_cparams(PAR),
    )(proj, proj, gam, bet, ws, bst)


def _c_bwd(proj, dycat, gam, bet, ws, wst, bst, *, tm, name):
    t = proj.shape[0]
    nch = tm // C_CHUNK
    nstep = t // tm

    def body(u_ref, v_ref, dy_ref, g_ref, b_ref, ws_ref, wst_ref, bs_ref,
             du_ref, dv_ref, dws_ref, dbs_ref, dg_ref, db_ref, dvn_s):
        step = pl.program_id(0)

        @pl.when(step == 0)
        def _():
            dws_ref[...] = jnp.zeros_like(dws_ref)
            dbs_ref[...] = jnp.zeros_like(dbs_ref)
            dg_ref[...] = jnp.zeros_like(dg_ref)
            db_ref[...] = jnp.zeros_like(db_ref)

        cv = v_ref[...].astype(F32)
        gam_v = g_ref[...]
        vn, xhat, r = _c_norm(cv, gam_v, b_ref[...])
        vnb = vn.astype(BF16)
        for c in range(nch):
            rows = slice(c * C_CHUNK, (c + 1) * C_CHUNK)
            for g in range(C_GROUPS):
                gs = slice(g * 64, (g + 1) * 64)
                cu = u_ref[rows, gs].astype(F32)
                dy = dy_ref[rows, gs]
                mixed = jnp.dot(ws_ref[g], vnb[rows, gs], preferred_element_type=F32) + bs_ref[:, gs]
                du_ref[rows, gs] = (dy * mixed * _gelu_grad(cu)).astype(BF16)
                dmix = dy * _gelu(cu)
                dbs_ref[:, gs] += dmix
                dmb = dmix.astype(BF16)
                dws_ref[g] += lax.dot_general(dmb, vnb[rows, gs], NT, preferred_element_type=F32)
                dvn_s[rows, gs] = jnp.dot(wst_ref[g], dmb, preferred_element_type=F32)
        dvn = dvn_s[...]
        dg_ref[...] += jnp.sum(dvn * xhat, axis=0, keepdims=True)
        db_ref[...] += jnp.sum(dvn, axis=0, keepdims=True)
        dxh = dvn * gam_v
        dvg = r * (dxh - jnp.mean(dxh, axis=-1, keepdims=True) - xhat * jnp.mean(dxh * xhat, axis=-1, keepdims=True))
        dv_ref[...] = (dvg * _gelu_grad(cv)).astype(BF16)

        @pl.when(step == nstep - 1)
        def _():
            dbs_ref[...] = _seg_sum(dbs_ref[...], _group_sum_matrix(256, True))

    vec = pl.BlockSpec((1, 256), lambda i: (0, 0))
    mat = pl.BlockSpec((C_GROUPS, C_CHUNK, C_CHUNK), lambda i: (0, 0, 0))
    bsp = pl.BlockSpec((C_CHUNK, 256), lambda i: (0, 0))
    tile = pl.BlockSpec((tm, 256), lambda i: (i, 0))
    return pl.pallas_call(
        body, name=name, grid=(nstep,),
        in_specs=[pl.BlockSpec((tm, 256), lambda i: (i, 5)), pl.BlockSpec((tm, 256), lambda i: (i, 6)),
                  pl.BlockSpec((tm, 256), lambda i: (i, 2)), vec, vec, mat, mat, bsp],
        out_specs=[tile, tile, mat, bsp, vec, vec],
        out_shape=[_sds((t, 256), BF16), _sds((t, 256), BF16), _sds((C_GROUPS, C_CHUNK, C_CHUNK), F32),
                   _sds((C_CHUNK, 256), F32), _sds((1, 256), F32), _sds((1, 256), F32)],
        scratch_shapes=[pltpu.VMEM((tm, 256), F32)],
        compiler_params=_cparams(ARB),
    )(proj, proj, dycat, gam, bet, ws, wst, bst)


FF_TC = 128
FF_NB = D_FF // FF_TC
FF_CH = 64
FF_HALO = 16


def _taps(ref, r0, win, where):
    z = jnp.zeros((FF_HALO, win.shape[1]), F32)
    if where == "first":
        win[0:FF_HALO, :] = z
        win[FF_HALO:, :] = ref[0:FF_CH + FF_HALO, :].astype(F32)
    elif where == "last":
        win[0:FF_CH + FF_HALO, :] = ref[SEQ - FF_CH - FF_HALO:SEQ, :].astype(F32)
        win[FF_CH + FF_HALO:, :] = z
    else:
        win[...] = ref[pl.ds(pl.multiple_of(r0 - FF_HALO, FF_HALO), FF_CH + 2 * FF_HALO), :].astype(F32)
    return tuple(win[FF_HALO + o:FF_HALO + o + FF_CH, :] for o in (-1, 0, 1))


def _chunk_loop(step):
    step(0, lambda ref, win: _taps(ref, 0, win, "first"))

    def mid(i, carry):
        r0 = pl.multiple_of(i * FF_CH, FF_CH)
        step(r0, lambda ref, win: _taps(ref, r0, win, "mid"))
        return carry

    lax.fori_loop(1, SEQ // FF_CH - 1, mid, 0)
    step(SEQ - FF_CH, lambda ref, win: _taps(ref, SEQ - FF_CH, win, "last"))


def _conv3(taps, w_ref, b_ref):
    dn, md, up = taps
    return w_ref[0:1, :] * dn + w_ref[1:2, :] * md + w_ref[2:3, :] * up + b_ref[...]


def _ff_specs(order):
    def at(fn):
        return (lambda b, j: fn(b, j)) if order == "bj" else (lambda j, b: fn(b, j))
    hs = [pl.BlockSpec((None, SEQ, FF_TC), at(lambda b, j, o=o: (b, 0, j + o))) for o in (0, FF_NB)]
    ws = [pl.BlockSpec((3, FF_TC), at(lambda b, j, o=o: (0, j + o))) for o in (0, FF_NB)]
    bs = [pl.BlockSpec((1, FF_TC), at(lambda b, j, o=o: (0, j + o))) for o in (0, FF_NB)]
    return hs, ws, bs


def _conv_gate_fwd(h, cw, cb, *, name):
    t = h.shape[0]
    bl = t // SEQ

    def body(hg_ref, hu_ref, wg_ref, wu_ref, bg_ref, bu_ref, a_ref, win):
        def step(r0, taps):
            cg = _conv3(taps(hg_ref, win.at[0]), wg_ref, bg_ref)
            cu = _conv3(taps(hu_ref, win.at[1]), wu_ref, bu_ref)
            a_ref[pl.ds(r0, FF_CH), :] = (cg * _sigmoid(cg) * cu).astype(BF16)

        _chunk_loop(step)

    hs, ws, bs = _ff_specs("bj")
    h3 = h.reshape(bl, SEQ, 2 * D_FF)
    act = pl.pallas_call(
        body, name=name, grid=(bl, FF_NB), in_specs=hs + ws + bs,
        out_specs=pl.BlockSpec((None, SEQ, FF_TC), lambda b, j: (b, 0, j)),
        out_shape=_sds((bl, SEQ, D_FF), BF16),
        scratch_shapes=[pltpu.VMEM((2, FF_CH + 2 * FF_HALO, FF_TC), F32)],
        compiler_params=_cparams(PAR, PAR),
    )(h3, h3, cw, cw, cb, cb)
    return act.reshape(t, D_FF)


def _conv_gate_bwd(h, dact, cw, cb, *, name):
    t = h.shape[0]
    bl = t // SEQ

    def body(hg_ref, hu_ref, wg_ref, wu_ref, bg_ref, bu_ref, da_ref,
             dhg_ref, dhu_ref, dwg_ref, dwu_ref, dbg_ref, dbu_ref, dg_s, du_s, win, sums):
        @pl.when(pl.program_id(1) == 0)
        def _():
            for ref in (dwg_ref, dwu_ref, dbg_ref, dbu_ref):
                ref[...] = jnp.zeros_like(ref)

        sums[...] = jnp.zeros_like(sums)
        red = lambda x: jnp.sum(x.reshape(FF_CH // 8, 8, x.shape[1]), axis=0)

        def pass1(r0, taps):
            tg, tu = taps(hg_ref, win.at[0]), taps(hu_ref, win.at[1])
            cg = _conv3(tg, wg_ref, bg_ref)
            cu = _conv3(tu, wu_ref, bu_ref)
            da = da_ref[pl.ds(r0, FF_CH), :].astype(F32)
            sg = _sigmoid(cg)
            dcg = da * cu * (sg * (1.0 + cg * (1.0 - sg)))
            dcu = da * (cg * sg)
            dg_s[pl.ds(r0, FF_CH), :] = dcg
            du_s[pl.ds(r0, FF_CH), :] = dcu
            for half, (d, tp) in enumerate(((dcg, tg), (dcu, tu))):
                for k in range(3):
                    sums[4 * half + k] += red(d * tp[k])
                sums[4 * half + 3] += red(d)

        _chunk_loop(pass1)
        for half, (dw_ref, db_ref) in enumerate(((dwg_ref, dbg_ref), (dwu_ref, dbu_ref))):
            for k in range(3):
                dw_ref[k:k + 1, :] += jnp.sum(sums[4 * half + k], axis=0, keepdims=True)
            db_ref[...] += jnp.sum(sums[4 * half + 3], axis=0, keepdims=True)

        def pass2(r0, taps):
            for k, (s, w_ref, o_ref) in enumerate(((dg_s, wg_ref, dhg_ref), (du_s, wu_ref, dhu_ref))):
                dn, md, up = taps(s, win.at[k])
                o_ref[pl.ds(r0, FF_CH), :] = (w_ref[0:1, :] * up + w_ref[1:2, :] * md + w_ref[2:3, :] * dn).astype(BF16)

        _chunk_loop(pass2)

    hs, ws, bs = _ff_specs("jb")
    half = pl.BlockSpec((None, SEQ, FF_TC), lambda j, b: (b, 0, j))
    wsp = pl.BlockSpec((3, FF_TC), lambda j, b: (0, j))
    bsp = pl.BlockSpec((1, FF_TC), lambda j, b: (0, j))
    h3 = h.reshape(bl, SEQ, 2 * D_FF)
    dhg, dhu, dwg, dwu, dbg, dbu = pl.pallas_call(
        body, name=name, grid=(FF_NB, bl), in_specs=hs + ws + bs + [half],
        out_specs=[half, half, wsp, wsp, bsp, bsp],
        out_shape=[_sds((bl, SEQ, D_FF), BF16), _sds((bl, SEQ, D_FF), BF16), _sds((3, D_FF), F32), _sds((3, D_FF), F32),
                   _sds((1, D_FF), F32), _sds((1, D_FF), F32)],
        scratch_shapes=[pltpu.VMEM((SEQ, FF_TC), F32), pltpu.VMEM((SEQ, FF_TC), F32),
                        pltpu.VMEM((2, FF_CH + 2 * FF_HALO, FF_TC), F32), pltpu.VMEM((8, 8, FF_TC), F32)],
        compiler_params=_cparams(PAR, ARB),
    )(h3, h3, cw, cw, cb, cb, dact.reshape(bl, SEQ, D_FF))
    return (dhg.reshape(t, D_FF), dhu.reshape(t, D_FF), jnp.concatenate([dwg, dwu], axis=1),
            jnp.concatenate([dbg, dbu], axis=1))


def _ple_fwd(x2, gain, wg, pe, pe_blk, wp, *, tm, name):
    t, k = x2.shape

    def body(x_ref, g_ref, wg_ref, pe_ref, wp_ref, hn_ref, x3_ref, gt_ref, pp_ref):
        x = x_ref[...]
        r = lax.rsqrt(jnp.mean(x * x, axis=-1, keepdims=True) + EPS)
        hn = (x * r * g_ref[...]).astype(BF16)
        hn_ref[...] = hn
        gate = _sigmoid(jnp.dot(hn, wg_ref[...], preferred_element_type=F32))
        pp = jnp.dot(pe_ref[...].astype(BF16), wp_ref[...], preferred_element_type=F32)
        gt_ref[...] = gate.astype(BF16)
        pp_ref[...] = pp.astype(BF16)
        x3_ref[...] = x + pp * gate

    row = pl.BlockSpec((tm, k), lambda i: (i, 0))
    return pl.pallas_call(
        body, name=name, grid=(t // tm,),
        in_specs=[row, pl.BlockSpec((1, k), lambda i: (0, 0)), pl.BlockSpec((k, k), lambda i: (0, 0)),
                  pl.BlockSpec((tm, PLE_DIM), lambda i: (pe_blk + i, 0)), pl.BlockSpec((PLE_DIM, k), lambda i: (0, 0))],
        out_specs=[row, row, row, row],
        out_shape=[_sds((t, k), BF16), _sds((t, k), F32), _sds((t, k), BF16), _sds((t, k), BF16)],
        compiler_params=_cparams(PAR),
    )(x2, gain, wg, pe, wp)


def _ple_bwd_ew(dx3, gate, pp, *, tm, name):
    t, n = dx3.shape

    def body(d_ref, g_ref, p_ref, dz_ref, dpp_ref):
        d, g = d_ref[...], g_ref[...]
        dz_ref[...] = (d * p_ref[...] * g * (1.0 - g)).astype(BF16)
        dpp_ref[...] = (d * g).astype(BF16)

    spec = pl.BlockSpec((tm, n), lambda i: (i, 0))
    return pl.pallas_call(
        body, name=name, grid=(t // tm,), in_specs=[spec] * 3, out_specs=[spec] * 2,
        out_shape=[_sds((t, n), BF16)] * 2, compiler_params=_cparams(PAR),
    )(dx3, gate, pp)


def _loss_head(y, tgt, *, tm, name):
    t, d = y.shape

    def body(y_ref, t_ref, l_ref, dy_ref):
        @pl.when(pl.program_id(0) == 0)
        def _():
            l_ref[...] = jnp.zeros_like(l_ref)

        e = y_ref[...] - t_ref[...]
        dy_ref[...] = e * (1.0 / d)
        s = jnp.sum(jnp.sum(e * e, axis=1, keepdims=True), axis=0, keepdims=True)
        l_ref[...] += jnp.broadcast_to(s * (0.5 / d), (8, 128))

    spec = pl.BlockSpec((tm, d), lambda i: (i, 0))
    return pl.pallas_call(
        body, name=name, grid=(t // tm,), in_specs=[spec, spec],
        out_specs=[pl.BlockSpec((8, 128), lambda i: (0, 0)), spec],
        out_shape=[_sds((8, 128), F32), _sds((t, d), F32)], compiler_params=_cparams(ARB),
    )(y, tgt)


BIAS_PC = 8192


def _onehot(bucket_row):
    rows = lax.broadcasted_iota(jnp.int32, (REL_BUCKETS, bucket_row.shape[1]), 0)
    return (rows == bucket_row).astype(BF16)


def _dot3(x, onehot, dims):
    acc = None
    for _ in range(3):
        term = x.astype(BF16)
        part = lax.dot_general(term, onehot, dims, preferred_element_type=F32)
        acc = part if acc is None else acc + part
        x = x - term.astype(F32)
    return acc


def _bias_lookup(table_t, bucket, *, name):
    h = table_t.shape[0]
    p = bucket.shape[1]

    def body(t_ref, b_ref, o_ref):
        bk = b_ref[...]
        val = _dot3(t_ref[...], _onehot(bk), (((1,), (0,)), ((), ())))
        o_ref[...] = jnp.where(bk >= 0, val, NEG_INF)

    return pl.pallas_call(
        body, name=name, grid=(p // BIAS_PC,),
        in_specs=[pl.BlockSpec((h, REL_BUCKETS), lambda i: (0, 0)), pl.BlockSpec((1, BIAS_PC), lambda i: (0, i))],
        out_specs=pl.BlockSpec((h, BIAS_PC), lambda i: (0, i)), out_shape=_sds((h, p), F32),
        compiler_params=_cparams(PAR),
    )(table_t, bucket)


def _bucket_reduce(dbiases, bucket, *, name):
    h, p = dbiases[0].shape
    nl = len(dbiases)

    def body(*refs):
        b_ref, o_ref = refs[nl], refs[nl + 1]

        @pl.when(pl.program_id(0) == 0)
        def _():
            o_ref[...] = jnp.zeros_like(o_ref)

        d = refs[0][...]
        for d_ref in refs[1:nl]:
            d = d + d_ref[...]
        o_ref[...] += _dot3(d, _onehot(b_ref[...]), NT)

    return pl.pallas_call(
        body, name=name, grid=(p // BIAS_PC,),
        in_specs=[pl.BlockSpec((h, BIAS_PC), lambda i: (0, i))] * nl + [pl.BlockSpec((1, BIAS_PC), lambda i: (0, i))],
        out_specs=pl.BlockSpec((h, REL_BUCKETS), lambda i: (0, 0)), out_shape=_sds((h, REL_BUCKETS), F32),
        compiler_params=_cparams(ARB),
    )(*dbiases, bucket)


def _adamw_math(w, g, m, v):
    m = ADAM_B1 * m + (1.0 - ADAM_B1) * g
    v = ADAM_B2 * v + (1.0 - ADAM_B2) * (g * g)
    m_hat = m / (1.0 - ADAM_B1 ** ADAM_STEP)
    v_hat = v / (1.0 - ADAM_B2 ** ADAM_STEP)
    delta = -ADAM_LR * (m_hat / (jnp.sqrt(v_hat) + ADAM_EPS) + ADAM_WD * w)
    return delta, m, v


def _adamw_reduce(parts, w, m, v, *, tr, name):
    nl = len(parts)
    rows, c = w.shape
    r = rows // nl
    nt = r // tr

    def body(*refs):
        p_refs = refs[:nl]
        w_ref, m_ref, v_ref, g_ref, d_ref, nm_ref, nv_ref = refs[nl:]
        for li, p_ref in enumerate(p_refs):
            @pl.when(pl.program_id(0) == li)
            def _(p_ref=p_ref):
                g = p_ref[0].astype(F32)
                for k in range(1, N_DEV):
                    g = g + p_ref[k].astype(F32)
                d, nm, nv = _adamw_math(w_ref[...], g, m_ref[...], v_ref[...])
                g_ref[...] = g
                d_ref[...] = d
                nm_ref[...] = nm
                nv_ref[...] = nv

    def part_map(li):
        return lambda l, i: (0, jnp.where(l == li, i, jnp.where(l < li, 0, nt - 1)), 0)

    spec = pl.BlockSpec((tr, c), lambda l, i: (l * nt + i, 0))
    return pl.pallas_call(
        body, name=name, grid=(nl, nt),
        in_specs=[pl.BlockSpec((N_DEV, tr, c), part_map(li)) for li in range(nl)] + [spec, spec, spec],
        out_specs=[spec] * 4, out_shape=[_sds((rows, c), F32)] * 4, compiler_params=_cparams(ARB, ARB),
    )(*parts, w, m, v)


def _adamw_plain(g, w, m, v, *, name):
    def body(g_ref, w_ref, m_ref, v_ref, d_ref, nm_ref, nv_ref):
        d, nm, nv = _adamw_math(w_ref[...], g_ref[...], m_ref[...], v_ref[...])
        d_ref[...] = d
        nm_ref[...] = nm
        nv_ref[...] = nv

    return pl.pallas_call(body, name=name, out_shape=[_sds(w.shape, F32)] * 3)(g, w, m, v)


def _mesh_pos():
    return lax.axis_index("x"), lax.axis_index("y"), lax.axis_index("c")


def _allgather_body(x_refs, out_refs, send_sems, recv_sems, local_sems, slot):
    x, y, c = _mesh_pos()
    me, sibling = (x, y, c), (x, y, 1 - c)
    chips = [(1 - x, y), (x, 1 - y), (1 - x, 1 - y)]
    waits = []
    for a, (x_ref, out_ref) in enumerate(zip(x_refs, out_refs)):
        def copy(k, block, to, src=None, out_ref=out_ref, a=a):
            return pltpu.make_async_remote_copy(
                src_ref=slot(out_ref, block) if src is None else src, dst_ref=slot(out_ref, block),
                send_sem=send_sems.at[a, k], recv_sem=recv_sems.at[a, k], device_id=to, device_id_type=MESH)

        mine = pltpu.make_async_copy(x_ref, slot(out_ref, me), local_sems.at[a])
        mine.start()
        first = [copy(0, me, sibling, src=x_ref)]
        first += [copy(1 + j, me, (*chip, c), src=x_ref) for j, chip in enumerate(chips)]
        for cp in first:
            cp.start()
        waits.append((copy, mine, first))
    sends = []
    for copy, mine, first in waits:
        passed = [copy(4 + j, (*chip, c), sibling) for j, chip in enumerate(chips)]
        for j, chip in enumerate(chips):
            copy(1 + j, (*chip, c), me).wait_recv()
            passed[j].start()
        sends.append(passed)
    for (copy, mine, first), passed in zip(waits, sends):
        copy(0, sibling, me).wait_recv()
        for j, chip in enumerate(chips):
            copy(4 + j, (*chip, 1 - c), me).wait_recv()
        for cp in first + passed:
            cp.wait_send()
        mine.wait()


PEER_FLIPS = ((0, 0, 1), (1, 0, 0), (0, 1, 0), (1, 1, 0), (1, 0, 1), (0, 1, 1), (1, 1, 1))


def _peer_copies(x_refs, land_refs, send_sem, recv_sem, scatter):
    x, y, c = _mesh_pos()
    me = 4 * x + 2 * y + c
    copies = []
    for x_ref, land_ref in zip(x_refs, land_refs):
        for fx, fy, fc in PEER_FLIPS:
            px, py, pc = x ^ fx, y ^ fy, c ^ fc
            src = x_ref.at[4 * px + 2 * py + pc] if scatter else x_ref
            copies.append(pltpu.make_async_remote_copy(
                src_ref=src, dst_ref=land_ref.at[me], send_sem=send_sem, recv_sem=recv_sem,
                device_id=(px, py, pc), device_id_type=MESH))
    return copies


def _sc_exchange(xs, *, scatter, collective_id, name):
    na = len(xs)
    land_shapes = [x.shape if scatter else (N_DEV,) + x.shape for x in xs]

    def body(*refs):
        x_refs, land_refs = refs[:na], refs[na:2 * na]
        send_sem, recv_sem, local_sem = refs[2 * na:]
        x, y, c = _mesh_pos()
        me = 4 * x + 2 * y + c
        barrier = pltpu.get_barrier_semaphore()
        for fx, fy, fc in PEER_FLIPS:
            pl.semaphore_signal(barrier, inc=1, device_id=(x ^ fx, y ^ fy, c ^ fc), device_id_type=MESH)
        pl.semaphore_wait(barrier, len(PEER_FLIPS))
        for x_ref, land_ref in zip(x_refs, land_refs):
            own = pltpu.make_async_copy(x_ref.at[me] if scatter else x_ref, land_ref.at[me], local_sem)
            own.start()
            own.wait()
        copies = _peer_copies(x_refs, land_refs, send_sem, recv_sem, scatter)
        for cp in copies:
            cp.start()
        for cp in copies:
            cp.wait()

    return pl.kernel(
        body, name=name, out_type=[_sds(s, x.dtype) for s, x in zip(land_shapes, xs)],
        mesh=plsc.ScalarSubcoreMesh(axis_name="sequencer", num_cores=1),
        scratch_types=[pltpu.SemaphoreType.DMA, pltpu.SemaphoreType.DMA, pltpu.SemaphoreType.DMA],
        compiler_params=pltpu.CompilerParams(collective_id=collective_id),
    )(*xs)


def _sc_allgather(xs, *, collective_id, name):
    na = len(xs)

    def body(*refs):
        x_refs, out_refs = refs[:na], refs[na:2 * na]
        send_sems, recv_sems, local_sems = refs[2 * na:]
        x, y, c = _mesh_pos()
        barrier = pltpu.get_barrier_semaphore()
        for fx, fy, fc in PEER_FLIPS:
            pl.semaphore_signal(barrier, inc=1, device_id=(x ^ fx, y ^ fy, c ^ fc), device_id_type=MESH)
        pl.semaphore_wait(barrier, len(PEER_FLIPS))
        _allgather_body(x_refs, out_refs, send_sems, recv_sems, local_sems,
                        lambda ref, pos: ref.at[4 * pos[0] + 2 * pos[1] + pos[2]])

    return pl.kernel(
        body, name=name, out_type=[_sds((N_DEV,) + x.shape, x.dtype) for x in xs],
        mesh=plsc.ScalarSubcoreMesh(axis_name="sequencer", num_cores=1),
        scratch_types=[pltpu.SemaphoreType.DMA((na, 7)), pltpu.SemaphoreType.DMA((na, 7)),
                       pltpu.SemaphoreType.DMA((na,))],
        compiler_params=pltpu.CompilerParams(collective_id=collective_id),
    )(*xs)


def _allgather_vmem(x, *, name):
    r, c = x.shape

    def body(x_ref, out_ref, send_sems, recv_sems, local_sems):
        _allgather_body([x_ref], [out_ref], send_sems, recv_sems, local_sems,
                        lambda ref, pos: ref.at[pl.ds((4 * pos[0] + 2 * pos[1] + pos[2]) * r, r), :])

    vm = pl.BlockSpec(memory_space=pltpu.VMEM)
    return pl.pallas_call(
        body, name=name, in_specs=[vm], out_specs=vm, out_shape=_sds((N_DEV * r, c), x.dtype),
        scratch_shapes=[pltpu.SemaphoreType.DMA((1, 7)), pltpu.SemaphoreType.DMA((1, 7)),
                        pltpu.SemaphoreType.DMA((1,))],
    )(x)


def _sum_slots(gathered, *, name):
    _, r, c = gathered.shape

    def body(g_ref, o_ref):
        acc = g_ref[0]
        for k in range(1, N_DEV):
            acc = acc + g_ref[k]
        o_ref[...] = acc

    return pl.pallas_call(body, name=name, out_shape=_sds((r, c), gathered.dtype))(gathered)


def _t5_bucket(rel):
    nb = REL_BUCKETS // 2
    ret = jnp.where(rel > 0, nb, 0)
    n = jnp.abs(rel)
    max_exact = nb // 2
    nf = jnp.maximum(n, 1).astype(F32)
    large = max_exact + (jnp.log(nf / max_exact) / math.log(REL_MAX_DIST / max_exact)
                         * (nb - max_exact)).astype(jnp.int32)
    large = jnp.minimum(large, nb - 1)
    return ret + jnp.where(n < max_exact, n, large)


def _band_pattern(block, radius, dil):
    kw = block + 2 * radius
    rel = jnp.arange(kw)[None, :] - radius - jnp.arange(block)[:, None]
    return jnp.where(jnp.abs(rel) <= radius, _t5_bucket(rel * dil), -1).astype(jnp.int32).reshape(1, block * kw)


def _rope_tables():
    lane = np.arange(64)
    seg, j = lane // 32, lane % 32
    inv = ROPE_THETA ** (-jnp.arange(0, 32, 2, dtype=F32) / 32)
    tpos = jnp.arange(SEQ)
    pos = jnp.where(jnp.asarray(seg)[None, :] == 0, (tpos // GRID_W)[:, None], (tpos % GRID_W)[:, None])
    ang = pos.astype(F32) * inv[jnp.asarray(j % 16)][None, :]
    cos = jnp.cos(ang)
    sins = jnp.where(jnp.asarray(j)[None, :] < 16, -jnp.sin(ang), jnp.sin(ang))
    return jnp.tile(cos, (1, 4)), jnp.tile(sins, (1, 4))


A_Q, A_K, A_V = (256, 0), (256, 1), (256, 2)
B_Q, B_K, B_V = (256, 0), (128, 2), (128, 3)
A_HEADS = dict(rad=A_RADIUS, nh=4, nkv=4)
B_HEADS = dict(rad=SWA_RADIUS, nh=4, nkv=2)


def _local_step(x, pe, tgt, rel_bias, wts, matmul_weights, grads_ready):
    t = x.shape[0]
    bl = t // SEQ
    cos, sins = _rope_tables()
    blocks_a = [min(BAND_BLOCK, SEQ // d) for d in DILATIONS]
    pats_a = [_band_pattern(blk, A_RADIUS, d) for blk, d in zip(blocks_a, DILATIONS)]
    pat_b = _band_pattern(BAND_BLOCK, SWA_RADIUS, 1)
    table_t = rel_bias.T
    bias_a = [_bias_lookup(table_t[:4], pt, name=f"bias_a{ci}").reshape(4, blk, blk + 2 * A_RADIUS)
              for ci, (pt, blk) in enumerate(zip(pats_a, blocks_a))]
    bias_b = _bias_lookup(table_t[4:], pat_b, name="bias_b").reshape(4, BAND_BLOCK, BAND_BLOCK + 2 * SWA_RADIUS)
    nat4 = lambda a: a.reshape(bl, 1, SEQ, a.shape[-1])

    saved = []
    for li in range(DEPTH):
        w = dict(wts[li])
        w.update(matmul_weights(li, "in", x))
        hn0, proj = _norm_mm((x,), w["g_mix"], w["w_in"], None, tm=1024, tn=1152, name="mix_in_fwd", out_dtype=BF16)
        qa1, qa4, qa16, qb, qd = _qkprep_fwd(proj, w["qk_gains"], cos, sins, tm=512, name="qkprep_fwd")
        qa = (nat4(qa1), qa4, qa16)
        oa, la = [], []
        for ci in range(3):
            o, l = _band_fwd(qa[ci], A_Q, A_K, A_V, bias_a[ci], None, name=f"band_a{ci}_fwd", **A_HEADS)
            oa.append(o)
            la.append(l)
        oa[0], la[0] = oa[0].reshape(t, 256), la[0].reshape(t, 256)
        ya, lse_a = _combine_a(oa, la, tm=512, name="combine_a")
        yb, lse_b = _band_fwd(nat4(qb), B_Q, B_K, B_V, bias_b, w["sink_t"], name="band_b_fwd", **B_HEADS)
        yb = yb.reshape(t, 256)
        yc = _c_fwd(proj, w["c_g"], w["c_b"], w["c_ws"], w["c_bst"], tm=512, name="c_fwd")
        yd, lse_d = _dense_fwd(qd, tq=256, name="dense_fwd")
        w.update(matmul_weights(li, "rest", yd))
        mixed, x1 = _norm_mm((ya, yb, yc, yd), w["out_gain"], w["w_out"], x, tm=1024, tn=1024, name="mix_out_fwd")
        hn1, h = _norm_mm((x1,), w["g_ffn"], w["w_up"], None, tm=1024, tn=1408, name="ffn_up_fwd", out_dtype=BF16)
        act = _conv_gate_fwd(h, w["conv_w"], w["conv_b"], name="conv_gate_fwd")
        x2 = _mm(act, w["w_down"], "nn", x1, tm=1024, tn=1024, out_dtype=F32, name="ffn_down_fwd")
        hn2, x3, gate, pp = _ple_fwd(x2, w["g_ple"], w["w_gate"], pe, li * (t // 1024), w["w_proj"], tm=1024,
                                     name="ple_fwd")
        saved.append(dict(w=w, x0=x, hn0=hn0, proj=proj, qa=qa, qb=qb, qd=qd, ya=ya, lse_a=lse_a, yb=yb, lse_b=lse_b,
                          yc=yc, yd=yd, lse_d=lse_d, mixed=mixed, x1=x1, hn1=hn1, h=h, act=act, x2=x2, hn2=hn2,
                          gate=gate, pp=pp))
        x = x3

    loss_tile, dx = _loss_head(x, tgt, tm=512, name="loss_head")
    grads = [None] * DEPTH
    dbias_a, dbias_bs = [[], [], []], []
    for li in reversed(range(DEPTH)):
        s = saved[li]
        w = s["w"]
        g = {}
        dz, dpp = _ple_bwd_ew(dx, s["gate"], s["pp"], tm=512, name="ple_bwd_ew")
        g["w_gate"] = _mm(s["hn2"], dz, "tn", None, tm=1024, tn=512, out_dtype=BF16, name="dw_gate")
        g["w_proj"] = _mm(pe, dpp, "tn", None, tm=256, tn=1024, out_dtype=BF16, name="dw_proj", a_rows=(li, t))
        dx2, dx2b, g["g_ple"] = _mm_bt_normbwd((dz,), w["w_gate"], (s["x2"],), w["g_ple"], dx, tm=1024, tn=1024,
                                               name="ple_bwd", emit_bf16=True)
        g["w_down"] = _mm(s["act"], dx2b, "tn", None, tm=1408, tn=512, out_dtype=BF16, name="dw_down")
        dact = _mm(dx2b, w["w_down"], "nt", None, tm=1024, tn=1408, out_dtype=BF16, name="ffn_down_bwd")
        dhg, dhu, g["conv_w"], g["conv_b"] = _conv_gate_bwd(s["h"], dact, w["conv_w"], w["conv_b"], name="conv_gate_bwd")
        g["w_up"] = jnp.concatenate(
            [_mm(s["hn1"], dhalf, "tn", None, tm=1024, tn=1408, out_dtype=BF16, name=f"dw_up_{nm}")
             for nm, dhalf in (("gate", dhg), ("up", dhu))], axis=1)
        dx1, dx1b, g["g_ffn"] = _mm_bt_normbwd((dhg, dhu), w["w_up"], (s["x1"],), w["g_ffn"], dx2, tm=1024, tn=1408,
                                               name="ffn_up_bwd", emit_bf16=True)
        g["w_out"] = _mm(s["mixed"], dx1b, "tn", None, tm=1024, tn=512, out_dtype=BF16, name="dw_out")
        grads_ready(li, "mid", g)
        dycat, g["out_gain"] = _mm_bt_normbwd((dx1b,), w["w_out"], (s["ya"], s["yb"], s["yc"], s["yd"]), w["out_gain"],
                                              None, tm=1024, tn=1024, name="mix_out_bwd")
        dy_r, lse_r, dl_a, dl_b, dl_d = _deltas(dycat, s["ya"], s["yb"], s["yd"], s["lse_a"], tm=512, name="deltas")
        dy_a = (nat4(dycat),) + tuple(dy_r)
        lse_a = (nat4(s["lse_a"]),) + tuple(lse_r)
        dl_a = (nat4(dl_a[0]),) + tuple(dl_a[1:])
        da = []
        for ci in range(3):
            dq, dk, dv, dbias = _band_bwd(s["qa"][ci], A_Q, A_K, A_V, bias_a[ci], None, dy_a[ci], 0, lse_a[ci],
                                          dl_a[ci], name=f"band_a{ci}_bwd", **A_HEADS)
            if ci == 0:
                dq, dk, dv = (a.reshape(t, 256) for a in (dq, dk, dv))
            da.append((dq, dk, dv))
            dbias_a[ci].append(dbias.reshape(4, -1))
        dqb, dkb, dvb, dbias_b, dsink = _band_bwd(nat4(s["qb"]), B_Q, B_K, B_V, bias_b, w["sink_t"], nat4(dycat), 1,
                                                  nat4(s["lse_b"]), nat4(dl_b), name="band_b_bwd", **B_HEADS)
        dbias_bs.append(dbias_b.reshape(4, -1))
        g["sink"] = dsink[:, 0, 0]
        dd = _dense_bwd(s["qd"], dycat, s["lse_d"], dl_d, tq=128, name="dense_bwd")
        dcu, dcv, g["c_ws"], dbs, g["c_g"], g["c_b"] = _c_bwd(s["proj"], dycat, w["c_g"], w["c_b"], w["c_ws"],
                                                               w["c_wst"], w["c_bst"], tm=512, name="c_bwd")
        g["c_bs"] = dbs[:, ::64].T
        db = (dqb.reshape(t, 256), dkb.reshape(t, 128), dvb.reshape(t, 128))
        dproj, dgains = _qkprep_bwd(s["proj"], da, db, dd, dcu, dcv, w["qk_gains"], cos, sins, tm=512, name="qkprep_bwd")
        g["qk_gain"] = dgains[:6, :64].reshape(3, 2, HEAD_DIM)
        g["w_in"] = _mm(s["hn0"], dproj, "tn", None, tm=1024, tn=1152, out_dtype=BF16, name="dw_in")
        dx, g["g_mix"] = _mm_bt_normbwd((dproj,), w["w_in"], (s["x0"],), w["g_mix"], dx1, tm=1024, tn=1152,
                                        name="mix_in_bwd")
        grads[li] = g
        grads_ready(li, "end", g)
    d_table_a = sum(_bucket_reduce(dbias_a[ci], pats_a[ci], name=f"bucket_a{ci}") for ci in range(3))
    d_table_b = _bucket_reduce(dbias_bs, pat_b, name="bucket_b")
    d_rel_bias = jnp.concatenate([d_table_a, d_table_b], axis=0).T
    return loss_tile[0, 0], dx, grads, d_rel_bias


WEIGHT_NAMES = ("rel_bias", "ln_mix_g", "w_in", "qk_gain", "sink", "c_norm_g", "c_norm_b", "c_ws", "c_bs", "out_gain",
                "w_out", "ln_ffn_g", "w_up", "conv_w", "conv_b", "w_down", "ln_ple_g", "w_ple_gate", "w_ple_proj")
COL_SHARDED = ("w_in", "w_up", "w_ple_proj")
ROW_SHARDED = ("w_out", "w_down", "w_ple_gate")
SMALL_SHARDED = ("conv_w", "out_gain")
REPLICATED = tuple(n for n in WEIGHT_NAMES if n not in COL_SHARDED + ROW_SHARDED + SMALL_SHARDED)
LOCAL_GRAD_KEY = {"ln_mix_g": "g_mix", "ln_ffn_g": "g_ffn", "ln_ple_g": "g_ple", "c_norm_g": "c_g", "c_norm_b": "c_b",
                  "w_ple_gate": "w_gate", "w_ple_proj": "w_proj"}


def _full_from_gathered(name, gathered):
    _, r, c = gathered.shape
    if name in ROW_SHARDED:
        return gathered.reshape(N_DEV * r, c)
    return jnp.transpose(gathered, (1, 0, 2)).reshape(r, N_DEV * c)


def _slots_from_full(name, full):
    rows, cols = full.shape
    if name in ROW_SHARDED:
        return full.reshape(N_DEV, rows // N_DEV, cols)
    return jnp.transpose(full.reshape(rows, N_DEV, cols // N_DEV), (1, 0, 2))


def _piece_rows(shape):
    return -(-int(np.prod(shape)) // 1024) * 8


def _pack_rows(arrays):
    pieces = []
    for a in arrays:
        n, rows = int(np.prod(a.shape)), _piece_rows(a.shape)
        flat = a.astype(F32).reshape(-1)
        if n != rows * LANES:
            flat = jnp.pad(flat, (0, rows * LANES - n))
        pieces.append(flat.reshape(rows, LANES))
    return jnp.concatenate(pieces, axis=0)


def _unpack_rows(packed, shapes):
    out, off = [], 0
    for shp in shapes:
        n, rows = int(np.prod(shp)), _piece_rows(shp)
        piece = packed[off:off + rows]
        out.append((piece if n == rows * LANES else piece.reshape(-1)[:n]).reshape(shp))
        off += rows
    return out


def kernel(x, p, rel_bias, ln_mix_g, w_in, qk_gain, sink, c_norm_g, c_norm_b, c_ws, c_bs, out_gain, w_out, ln_ffn_g, w_up, conv_w, conv_b, w_down, ln_ple_g, w_ple_gate, w_ple_proj, loss_target, m_rel_bias, m_ln_mix_g, m_w_in, m_qk_gain, m_sink, m_c_norm_g, m_c_norm_b, m_c_ws, m_c_bs, m_out_gain, m_w_out, m_ln_ffn_g, m_w_up, m_conv_w, m_conv_b, m_w_down, m_ln_ple_g, m_w_ple_gate, m_w_ple_proj, v_rel_bias, v_ln_mix_g, v_w_in, v_qk_gain, v_sink, v_c_norm_g, v_c_norm_b, v_c_ws, v_c_bs, v_out_gain, v_w_out, v_ln_ffn_g, v_w_up, v_conv_w, v_conv_b, v_w_down, v_ln_ple_g, v_w_ple_gate, v_w_ple_proj):
    env = dict(locals())
    wt = {n: env[n] for n in WEIGHT_NAMES}
    mom_m = {n: env["m_" + n] for n in WEIGHT_NAMES}
    mom_v = {n: env["v_" + n] for n in WEIGHT_NAMES}
    bl = x.shape[0]
    t = bl * SEQ
    me = 4 * lax.axis_index("x") + 2 * lax.axis_index("y") + lax.axis_index("c")

    big = COL_SHARDED + ROW_SHARDED
    full = {}
    small_shapes = [wt[n].shape for n in SMALL_SHARDED]
    small = _allgather_vmem(_pack_rows([wt[n] for n in SMALL_SHARDED]), name="gather_small")
    small = small.reshape(N_DEV, -1)
    off = 0
    for n, shp in zip(SMALL_SHARDED, small_shapes):
        cnt = int(np.prod(shp))
        g = small[:, off:off + cnt].reshape((N_DEV,) + tuple(shp))
        full[n] = jnp.transpose(g, (1, 2, 0, 3)).reshape(shp[0], shp[1], N_DEV * shp[2])
        off += _piece_rows(shp) * LANES

    def head_gain(li, a, b, reps):
        g = jnp.tile(qk_gain[li, a, b], reps)
        return jnp.pad(g, (0, 256 - g.shape[0]))

    wts = []
    for li in range(DEPTH):
        rows = [head_gain(li, 0, 0, 4), head_gain(li, 0, 1, 4), head_gain(li, 1, 0, 4), head_gain(li, 1, 1, 2),
                head_gain(li, 2, 0, 4), head_gain(li, 2, 1, 2), jnp.zeros((256,), F32), jnp.zeros((256,), F32)]
        wts.append(dict(
            g_mix=ln_mix_g[li].reshape(1, -1), qk_gains=jnp.stack(rows),
            sink_t=jnp.broadcast_to(sink[li][:, None, None], (4, 8, 128)),
            c_g=c_norm_g[li].reshape(1, -1), c_b=c_norm_b[li].reshape(1, -1), c_ws=c_ws[li].astype(BF16),
            c_wst=jnp.transpose(c_ws[li], (0, 2, 1)).astype(BF16), c_bst=jnp.repeat(c_bs[li].T, 64, axis=1),
            out_gain=full["out_gain"][li].reshape(1, -1), g_ffn=ln_ffn_g[li].reshape(1, -1),
            conv_w=full["conv_w"][li], conv_b=conv_b[li].reshape(1, -1), g_ple=ln_ple_g[li].reshape(1, -1)))

    local_key = {"w_ple_gate": "w_gate", "w_ple_proj": "w_proj"}

    gather_names = {"in": ("w_in",), "rest": tuple(n for n in big if n != "w_in")}
    gathered = {}
    for cid, (li, names) in enumerate(((0, gather_names["in"]), (0, gather_names["rest"]), (1, big))):
        lands = _sc_allgather([wt[n][li].astype(BF16) for n in names], collective_id=cid,
                              name=f"gather_{li}_{len(names)}")
        gathered.setdefault(li, {}).update(zip(names, lands))

    def matmul_weights(li, part, after):
        out = {}
        for n in gather_names[part]:
            g, _ = lax.optimization_barrier((gathered[li][n], after))
            out[local_key.get(n, n)] = _full_from_gathered(n, g)
        return out

    mid_names = ("w_ple_gate", "w_ple_proj", "w_down", "w_up", "w_out")
    end_names = ("w_in",)
    landed = {}

    def start_exchange(li, names, g, tag, cid):
        slots = [_slots_from_full(n, g[local_key.get(n, n)]) for n in names]
        lands = _sc_exchange(slots, scatter=True, collective_id=cid, name=f"grads_{li}_{tag}")
        landed.update({(n, li): land for n, land in zip(names, lands)})

    def grads_ready(li, stage, g):
        if li == 0:
            start_exchange(li, mid_names if stage == "mid" else end_names, g, stage, 5 if stage == "mid" else 6)
        elif stage == "end":
            start_exchange(li, mid_names + end_names, g, stage, 4)

    loss_part, dx, grads, d_rel_bias = _local_step(
        x.reshape(t, D_MODEL), p.reshape(DEPTH * t, PLE_DIM), loss_target.reshape(t, D_MODEL), rel_bias, wts,
        matmul_weights, grads_ready)
    loss = lax.psum(loss_part, ("x", "y", "c"))

    def local_grad(n):
        if n == "rel_bias":
            return d_rel_bias
        key = LOCAL_GRAD_KEY.get(n, n)
        return jnp.stack([grads[li][key].reshape(wt[n].shape[1:]) if n in REPLICATED else grads[li][key]
                          for li in range(DEPTH)])

    small_names = REPLICATED + SMALL_SHARDED
    small_full_shapes = [wt[n].shape if n in REPLICATED else full[n].shape for n in small_names]
    small_parts = _allgather_vmem(_pack_rows([local_grad(n) for n in small_names]), name="allgather_small_grads")
    small_parts = small_parts.reshape(N_DEV, -1, LANES)

    out_g, out_d, out_m, out_v = {}, {}, {}, {}
    for n in big:
        shp = wt[n].shape
        two_d = lambda a: a.reshape(-1, shp[-1])
        res = _adamw_reduce([landed[n, li] for li in range(DEPTH)], two_d(wt[n]), two_d(mom_m[n]), two_d(mom_v[n]),
                            tr=32 if n == "w_down" else 128, name="adamw_" + n)
        out_g[n], out_d[n], out_m[n], out_v[n] = [r.reshape(shp) for r in res]

    reduced = _sum_slots(small_parts, name="sum_small_grads")
    reduced = dict(zip(small_names, _unpack_rows(reduced, small_full_shapes)))
    rep_shapes = [wt[n].shape for n in REPLICATED]
    upd = _adamw_plain(_pack_rows([reduced[n] for n in REPLICATED]), _pack_rows([wt[n] for n in REPLICATED]),
                       _pack_rows([mom_m[n] for n in REPLICATED]), _pack_rows([mom_v[n] for n in REPLICATED]),
                       name="adamw_replicated")
    for dst, packed in zip((out_d, out_m, out_v), upd):
        dst.update(zip(REPLICATED, _unpack_rows(packed, rep_shapes)))
    for n in REPLICATED:
        out_g[n] = reduced[n]
    for n in SMALL_SHARDED:
        shp = wt[n].shape
        g = reduced[n].reshape(shp[0], shp[1], N_DEV, shp[2])
        g = lax.dynamic_index_in_dim(g, me, axis=2, keepdims=False)
        two_d = lambda a: a.reshape(-1, shp[-1])
        res = _adamw_plain(two_d(g), two_d(wt[n]), two_d(mom_m[n]), two_d(mom_v[n]), name="adamw_" + n)
        out_g[n] = g
        out_d[n], out_m[n], out_v[n] = [r.reshape(shp) for r in res]

    return (loss, dx.reshape(bl, SEQ, D_MODEL), *[out_g[n] for n in WEIGHT_NAMES], *[out_d[n] for n in WEIGHT_NAMES],
            *[out_m[n] for n in WEIGHT_NAMES], *[out_v[n] for n in WEIGHT_NAMES])
```

```python
import math

import jax
import jax.numpy as jnp
import numpy as np
from jax import lax
from jax.experimental import pallas as pl
from jax.experimental.pallas import tpu as pltpu
from jax.experimental.pallas import tpu_sc as plsc

F32 = jnp.float32
BF16 = jnp.bfloat16

N_DEV = 8
D_MODEL = 1024
SEQ = 2048
DEPTH = 2
HEAD_DIM = 64
IN_WIDTH = 2304
D_FF = 2816
PLE_DIM = 256
C_CHUNK = 128
C_GROUPS = 4
DILATED_CFGS = ((128, 1), (512, 4), (2048, 16))
DILATIONS = tuple(d for _, d in DILATED_CFGS)
A_RADIUS = 64
SWA_RADIUS = 128
BAND_BLOCK = 256
GRID_W = 64
ROPE_THETA = 10000.0
REL_BUCKETS = 32
REL_MAX_DIST = 1024
EPS = 1e-6
NEG_INF = -1e30
ATTN_SCALE = HEAD_DIM ** -0.5
LANES = 128

ADAM_LR = 0.001
ADAM_B1 = 0.9
ADAM_B2 = 0.999
ADAM_EPS = 1e-08
ADAM_WD = 0.01
ADAM_STEP = 10

MESH = pl.DeviceIdType.MESH
NT = (((1,), (1,)), ((), ()))
TN = (((0,), (0,)), ((), ()))
ARB = "arbitrary"
PAR = "parallel"


def _cparams(*sem):
    return pltpu.CompilerParams(dimension_semantics=tuple(sem))


def _sds(shape, dtype):
    return jax.ShapeDtypeStruct(tuple(shape), dtype)


def _group_sum_matrix(n, same_group):
    r = lax.broadcasted_iota(jnp.int32, (n, n), 0)
    c = lax.broadcasted_iota(jnp.int32, (n, n), 1)
    if same_group:
        return ((r >> 6) == (c >> 6)).astype(F32)
    return ((r & 63) == (c & 63)).astype(F32)


def _seg_sum(x, e):
    eb = e.astype(BF16)
    hi = x.astype(BF16)
    lo = (x - hi.astype(F32)).astype(BF16)
    return jnp.dot(hi, eb, preferred_element_type=F32) + jnp.dot(lo, eb, preferred_element_type=F32)


def _gelu(x):
    c = math.sqrt(2.0 / math.pi)
    return 0.5 * x * (1.0 + jnp.tanh(c * (x + 0.044715 * (x * x * x))))


def _gelu_grad(x):
    c = math.sqrt(2.0 / math.pi)
    t = jnp.tanh(c * (x + 0.044715 * (x * x * x)))
    return 0.5 * (1.0 + t) + 0.5 * x * (1.0 - t * t) * c * (1.0 + 3.0 * 0.044715 * (x * x))


def _sigmoid(x):
    return 1.0 / (1.0 + jnp.exp(-x))


def _scatter_cols(scratch, first, val):
    for c in range(val.shape[1] // LANES):
        scratch[first + c] = val[:, c * LANES:(c + 1) * LANES]


def _gather_cols(scratch, first, ncol):
    return jnp.concatenate([scratch[first + c] for c in range(ncol)], axis=1)


def _read_residue(scratch, first, ncol, r, d):
    n = scratch.shape[1] // d
    return jnp.concatenate([scratch.at[first + c][pl.ds(r, n, stride=d), :] for c in range(ncol)], axis=1)


def _write_residue(scratch, first, r, d, val):
    n = scratch.shape[1] // d
    for c in range(val.shape[1] // LANES):
        scratch.at[first + c][pl.ds(r, n, stride=d), :] = val[:, c * LANES:(c + 1) * LANES]


def _norm_mm(xs, gain, w, res, *, tm, tn, name, out_dtype=F32):
    t = xs[0].shape[0]
    k = sum(x.shape[1] for x in xs)
    n = w.shape[1]
    ng = len(xs)
    has_res = res is not None

    def body(*refs):
        x_refs = refs[:ng]
        g_ref, w_ref = refs[ng], refs[ng + 1]
        res_ref = refs[ng + 2] if has_res else None
        hn_ref, o_ref, hn_s = refs[ng + 2 + has_res:]

        @pl.when(pl.program_id(1) == 0)
        def _():
            off = 0
            for xr in x_refs:
                x = xr[...]
                wd = x.shape[1]
                r = lax.rsqrt(jnp.mean(x * x, axis=-1, keepdims=True) + EPS)
                hn_s[:, off:off + wd] = (x * r * g_ref[:, off:off + wd]).astype(BF16)
                off += wd
            hn_ref[...] = hn_s[...]

        acc = jnp.dot(hn_s[...], w_ref[...], preferred_element_type=F32)
        if has_res:
            acc = acc + res_ref[...]
        o_ref[...] = acc.astype(out_dtype)

    in_specs = [pl.BlockSpec((tm, x.shape[1]), lambda i, j: (i, 0)) for x in xs]
    in_specs += [pl.BlockSpec((1, k), lambda i, j: (0, 0)), pl.BlockSpec((k, tn), lambda i, j: (0, j))]
    args = list(xs) + [gain, w]
    if has_res:
        in_specs.append(pl.BlockSpec((tm, tn), lambda i, j: (i, j)))
        args.append(res)
    return pl.pallas_call(
        body, name=name, grid=(t // tm, n // tn), in_specs=in_specs,
        out_specs=[pl.BlockSpec((tm, k), lambda i, j: (i, 0)), pl.BlockSpec((tm, tn), lambda i, j: (i, j))],
        out_shape=[_sds((t, k), BF16), _sds((t, n), out_dtype)],
        scratch_shapes=[pltpu.VMEM((tm, k), BF16)],
        compiler_params=_cparams(PAR, ARB),
    )(*args)


def _mm(a, b, mode, res, *, tm, tn, out_dtype, name, a_rows=None):
    if mode == "tn":
        kk, m = a.shape
        blk_a = 0
        if a_rows is not None:
            blk_a, kk = a_rows
        a_spec = pl.BlockSpec((kk, tm), lambda i, j: (blk_a, i))
    else:
        m, kk = a.shape
        a_spec = pl.BlockSpec((tm, kk), lambda i, j: (i, 0))
    if mode == "nt":
        n = b.shape[0]
        b_spec = pl.BlockSpec((tn, kk), lambda i, j: (j, 0))
    else:
        n = b.shape[1]
        b_spec = pl.BlockSpec((kk, tn), lambda i, j: (0, j))
    has_res = res is not None

    def body(*refs):
        a_ref, b_ref = refs[0], refs[1]
        o_ref = refs[-1]
        av = a_ref[...].astype(BF16)
        bv = b_ref[...].astype(BF16)
        if mode == "nn":
            acc = jnp.dot(av, bv, preferred_element_type=F32)
        elif mode == "nt":
            acc = lax.dot_general(av, bv, NT, preferred_element_type=F32)
        else:
            acc = lax.dot_general(av, bv, TN, preferred_element_type=F32)
        if has_res:
            acc = acc + refs[2][...]
        o_ref[...] = acc.astype(out_dtype)

    in_specs = [a_spec, b_spec]
    args = [a, b]
    if has_res:
        in_specs.append(pl.BlockSpec((tm, tn), lambda i, j: (i, j)))
        args.append(res)
    return pl.pallas_call(
        body, name=name, grid=(m // tm, n // tn), in_specs=in_specs,
        out_specs=pl.BlockSpec((tm, tn), lambda i, j: (i, j)),
        out_shape=_sds((m, n), out_dtype),
        compiler_params=_cparams(PAR, PAR),
    )(*args)


def _mm_bt_normbwd(dys, w, xs, gain, dres, *, tm, tn, name, emit_bf16=False):
    t, wd_each = dys[0].shape
    nd = len(dys)
    per = wd_each // tn
    nj = nd * per
    k = w.shape[0]
    ng = len(xs)
    has_res = dres is not None

    def body(*refs):
        dy_refs = refs[:nd]
        w_ref = refs[nd]
        x_refs = refs[nd + 1:nd + 1 + ng]
        g_ref = refs[nd + 1 + ng]
        dres_ref = refs[nd + 2 + ng] if has_res else None
        outs = refs[nd + 2 + ng + has_res:]
        dx_ref = outs[0]
        dxb_ref = outs[1] if emit_bf16 else None
        dg_ref, acc = outs[1 + emit_bf16:]
        i, j = pl.program_id(0), pl.program_id(1)

        @pl.when(j == 0)
        def _():
            acc[...] = jnp.zeros_like(acc)

        for d, dy_ref in enumerate(dy_refs):
            @pl.when((j >= d * per) & (j < (d + 1) * per))
            def _(dy_ref=dy_ref):
                acc[...] += lax.dot_general(dy_ref[...].astype(BF16), w_ref[...], NT, preferred_element_type=F32)

        @pl.when(j == nj - 1)
        def _():
            @pl.when(i == 0)
            def _():
                dg_ref[...] = jnp.zeros_like(dg_ref)

            off = 0
            for xr in x_refs:
                x = xr[...]
                wd = x.shape[1]
                g = g_ref[:, off:off + wd]
                dyn = acc[:, off:off + wd]
                r = lax.rsqrt(jnp.mean(x * x, axis=-1, keepdims=True) + EPS)
                gdy = dyn * g
                dx = r * gdy - x * (r * r * r * jnp.mean(gdy * x, axis=-1, keepdims=True))
                if has_res:
                    dx = dx + dres_ref[:, off:off + wd]
                dx_ref[:, off:off + wd] = dx
                if emit_bf16:
                    dxb_ref[:, off:off + wd] = dx.astype(BF16)
                dg_ref[:, off:off + wd] += jnp.sum(dyn * x * r, axis=0, keepdims=True)
                off += wd

    def dy_map(d):
        return lambda i, j: (i, jnp.clip(j - d * per, 0, per - 1))

    in_specs = [pl.BlockSpec((tm, tn), dy_map(d)) for d in range(nd)]
    in_specs.append(pl.BlockSpec((k, tn), lambda i, j: (0, j)))
    in_specs += [pl.BlockSpec((tm, x.shape[1]), lambda i, j: (i, 0)) for x in xs]
    in_specs.append(pl.BlockSpec((1, k), lambda i, j: (0, 0)))
    args = list(dys) + [w] + list(xs) + [gain]
    if has_res:
        in_specs.append(pl.BlockSpec((tm, k), lambda i, j: (i, 0)))
        args.append(dres)
    row = pl.BlockSpec((tm, k), lambda i, j: (i, 0))
    out_specs = [row] + ([row] if emit_bf16 else []) + [pl.BlockSpec((1, k), lambda i, j: (0, 0))]
    out_shape = [_sds((t, k), F32)] + ([_sds((t, k), BF16)] if emit_bf16 else []) + [_sds((1, k), F32)]
    return pl.pallas_call(
        body, name=name, grid=(t // tm, nj), in_specs=in_specs, out_specs=out_specs, out_shape=out_shape,
        scratch_shapes=[pltpu.VMEM((tm, k), F32)],
        compiler_params=_cparams(ARB, ARB),
    )(*args)


def _rope_partner(y):
    n = y.shape[1]
    lane = lax.broadcasted_iota(jnp.int32, y.shape, 1)
    return jnp.where((lane & 31) < 16, pltpu.roll(y, n - 16, 1), pltpu.roll(y, 16, 1))


def _residue_specs(tm, width, nt):
    specs = [pl.BlockSpec((tm, width), lambda b, i: (b * nt + i, 0))]
    for d in DILATIONS[1:]:
        specs.append(pl.BlockSpec((None, d, tm // d, width), lambda b, i: (b, 0, i, 0)))
    return specs


def _residue_shapes(bl, width, dtype):
    return [_sds((bl * SEQ, width), dtype)] + [_sds((bl, d, SEQ // d, width), dtype) for d in DILATIONS[1:]]


def _qkprep_fwd(proj, gains, cos, sins, *, tm, name):
    t = proj.shape[0]
    bl = t // SEQ
    nt = SEQ // tm

    def body(p_ref, g_ref, c_ref, s_ref, qa1_ref, qa4_ref, qa16_ref, qb_ref, qd_ref, scr):
        e = _group_sum_matrix(256, True)

        def hn(x, row):
            x = x.astype(F32)
            wd = x.shape[1]
            ms = _seg_sum(x * x, e[:wd, :wd]) * (1.0 / HEAD_DIM)
            return x * lax.rsqrt(ms + EPS) * g_ref[row:row + 1, :wd]

        qa = jnp.concatenate([hn(p_ref[:, 0:256], 0) * ATTN_SCALE, hn(p_ref[:, 256:512], 1),
                              p_ref[:, 512:768].astype(F32)], axis=1)
        qa1_ref[...] = qa.astype(BF16)
        _scatter_cols(scr, 0, qa)
        for d, ref in ((4, qa4_ref), (16, qa16_ref)):
            for r in range(d):
                ref[r] = _read_residue(scr, 0, 6, r, d).astype(BF16)
        qb_ref[:, 0:256] = (hn(p_ref[:, 768:1024], 2) * ATTN_SCALE).astype(BF16)
        qb_ref[:, 256:384] = hn(p_ref[:, 1024:1152], 3).astype(BF16)
        qb_ref[:, 384:512] = p_ref[:, 1152:1280].astype(BF16)
        yq = hn(p_ref[:, 1792:2048], 4)
        yq = yq * c_ref[...] + _rope_partner(yq) * s_ref[...]
        qd_ref[:, 0:256] = (yq * ATTN_SCALE).astype(BF16)
        yk = hn(p_ref[:, 2048:2176], 5)
        yk = yk * c_ref[:, 0:128] + _rope_partner(yk) * s_ref[:, 0:128]
        qd_ref[:, 256:384] = yk.astype(BF16)
        qd_ref[:, 384:512] = p_ref[:, 2176:2304].astype(BF16)

    row = lambda width: pl.BlockSpec((tm, width), lambda b, i: (b * nt + i, 0))
    tab = pl.BlockSpec((tm, 256), lambda b, i: (i, 0))
    return pl.pallas_call(
        body, name=name, grid=(bl, nt),
        in_specs=[row(IN_WIDTH), pl.BlockSpec((8, 256), lambda b, i: (0, 0)), tab, tab],
        out_specs=_residue_specs(tm, 768, nt) + [row(512), row(512)],
        out_shape=_residue_shapes(bl, 768, BF16) + [_sds((t, 512), BF16), _sds((t, 512), BF16)],
        scratch_shapes=[pltpu.VMEM((6, tm, LANES), F32)],
        compiler_params=_cparams(PAR, PAR),
    )(proj, gains, cos, sins)


def _qkprep_bwd(proj, da, db, dd, dcu, dcv, gains, cos, sins, *, tm, name):
    t = proj.shape[0]
    bl = t // SEQ
    nt = SEQ // tm
    flat = [a for cfg in da for a in cfg] + list(db) + list(dd) + [dcu, dcv]

    def body(*refs):
        p_ref, g_ref, c_ref, s_ref = refs[:4]
        d_refs = refs[4:4 + len(flat)]
        dp_ref, dg_ref, scr = refs[4 + len(flat):]
        a_refs = d_refs[:9]
        dqb_ref, dkb_ref, dvb_ref, dqd_ref, dkd_ref, dvd_ref, dcu_ref, dcv_ref = d_refs[9:]
        e = _group_sum_matrix(256, True)
        first = (pl.program_id(0) == 0) & (pl.program_id(1) == 0)
        last = (pl.program_id(0) == bl - 1) & (pl.program_id(1) == nt - 1)

        @pl.when(first)
        def _():
            dg_ref[...] = jnp.zeros_like(dg_ref)

        def hn_bwd(x, dy, row):
            x, dy = x.astype(F32), dy.astype(F32)
            wd = x.shape[1]
            ee = e[:wd, :wd]
            g = g_ref[row:row + 1, :wd]
            r = lax.rsqrt(_seg_sum(x * x, ee) * (1.0 / HEAD_DIM) + EPS)
            gdy = dy * g
            dx = r * gdy - x * (r * r * r * (_seg_sum(gdy * x, ee) * (1.0 / HEAD_DIM)))
            dg_ref[row:row + 1, :wd] += jnp.sum(dy * x * r, axis=0, keepdims=True)
            return dx

        def rope_bwd(dy, wd):
            dy = dy.astype(F32)
            return dy * c_ref[:, :wd] + _rope_partner(dy * s_ref[:, :wd])

        dqkv = jnp.concatenate([a_refs[m][...].astype(F32) for m in range(3)], axis=1)
        for ci, d in ((1, 4), (2, 16)):
            for r in range(d):
                part = jnp.concatenate([a_refs[3 * ci + m][r].astype(F32) for m in range(3)], axis=1)
                _write_residue(scr, 0, r, d, part)
            dqkv = dqkv + _gather_cols(scr, 0, 6)
        dp_ref[:, 0:256] = hn_bwd(p_ref[:, 0:256], dqkv[:, 0:256] * ATTN_SCALE, 0).astype(BF16)
        dp_ref[:, 256:512] = hn_bwd(p_ref[:, 256:512], dqkv[:, 256:512], 1).astype(BF16)
        dp_ref[:, 512:768] = dqkv[:, 512:768].astype(BF16)
        dp_ref[:, 768:1024] = hn_bwd(p_ref[:, 768:1024], dqb_ref[...] * ATTN_SCALE, 2).astype(BF16)
        dp_ref[:, 1024:1152] = hn_bwd(p_ref[:, 1024:1152], dkb_ref[...], 3).astype(BF16)
        dp_ref[:, 1152:1280] = dvb_ref[...].astype(BF16)
        dp_ref[:, 1280:1536] = dcu_ref[...].astype(BF16)
        dp_ref[:, 1536:1792] = dcv_ref[...].astype(BF16)
        dp_ref[:, 1792:2048] = hn_bwd(p_ref[:, 1792:2048], rope_bwd(dqd_ref[...] * ATTN_SCALE, 256), 4).astype(BF16)
        dp_ref[:, 2048:2176] = hn_bwd(p_ref[:, 2048:2176], rope_bwd(dkd_ref[...], 128), 5).astype(BF16)
        dp_ref[:, 2176:2304] = dvd_ref[...].astype(BF16)

        @pl.when(last)
        def _():
            dg_ref[...] = _seg_sum(dg_ref[...], _group_sum_matrix(256, False))

    row = lambda width: pl.BlockSpec((tm, width), lambda b, i: (b * nt + i, 0))
    tab = pl.BlockSpec((tm, 256), lambda b, i: (i, 0))
    in_specs = [row(IN_WIDTH), pl.BlockSpec((8, 256), lambda b, i: (0, 0)), tab, tab]
    res_specs = _residue_specs(tm, 256, nt)
    in_specs += [res_specs[ci] for ci in range(3) for _ in range(3)]
    in_specs += [row(a.shape[1]) for a in flat[9:]]
    return pl.pallas_call(
        body, name=name, grid=(bl, nt), in_specs=in_specs,
        out_specs=[row(IN_WIDTH), pl.BlockSpec((8, 256), lambda b, i: (0, 0))],
        out_shape=[_sds((t, IN_WIDTH), BF16), _sds((8, 256), F32)],
        scratch_shapes=[pltpu.VMEM((6, tm, LANES), F32)],
        compiler_params=_cparams(ARB, ARB),
    )(proj, gains, cos, sins, *flat)


BAND_ROWS_PER_STEP = 512


def _residues_per_step(dil, seq_len):
    return min(dil, max(1, BAND_ROWS_PER_STEP // seq_len))


def _band_spec(seq_len, spec, rb):
    width, idx = spec
    return pl.BlockSpec((None, rb, seq_len, width), lambda b, r: (b, r, 0, idx))


def _fill_padded(dst, src_ref, rad, seq_len):
    z = jnp.zeros((rad, dst.shape[1]), dst.dtype)
    dst[0:rad, :] = z
    dst[rad + seq_len:rad + seq_len + rad, :] = z
    dst[rad:rad + seq_len, :] = src_ref[...]


def _band_fwd(src, qs, ks, vs, bias, sink, *, rad, nh, nkv, name):
    bl, dil, sl, _ = src.shape
    blk = bias.shape[1]
    kw = blk + 2 * rad
    nb = sl // blk
    rep = nh // nkv
    has_sink = sink is not None
    rb = _residues_per_step(dil, sl)

    def body(*refs):
        q_all, k_all, v_all, b_ref = refs[:4]
        s_ref = refs[4] if has_sink else None
        o_all, l_all, kp, vp = refs[4 + has_sink:]
        for ri in range(rb):
            one_sequence(q_all.at[ri], k_all.at[ri], v_all.at[ri], b_ref, s_ref, o_all.at[ri], l_all.at[ri], kp, vp)

    def one_sequence(q_ref, k_ref, v_ref, b_ref, s_ref, o_ref, l_ref, kp, vp):
        _fill_padded(kp, k_ref, rad, sl)
        _fill_padded(vp, v_ref, rad, sl)

        def blk_body(i, carry):
            r0 = pl.multiple_of(i * blk, blk)
            qb = q_ref[pl.ds(r0, blk), :]
            kwin = kp[pl.ds(r0, kw), :]
            vwin = vp[pl.ds(r0, kw), :]
            col = r0 - rad + lax.broadcasted_iota(jnp.int32, (blk, kw), 1)
            neg = jnp.where((col >= 0) & (col < sl), 0.0, NEG_INF).astype(F32)
            for h in range(nh):
                g = h // rep
                hs = slice(h * HEAD_DIM, (h + 1) * HEAD_DIM)
                gs = slice(g * HEAD_DIM, (g + 1) * HEAD_DIM)
                s = lax.dot_general(qb[:, hs], kwin[:, gs], NT, preferred_element_type=F32)
                s = s + b_ref[h] + neg
                m = jnp.max(s, axis=1, keepdims=True)
                if has_sink:
                    sk = s_ref[h][0:1, 0:1]
                    m = jnp.maximum(m, sk)
                p = jnp.exp(s - m)
                den = jnp.sum(p, axis=1, keepdims=True)
                if has_sink:
                    den = den + jnp.exp(sk - m)
                o = jnp.dot(p.astype(BF16), vwin[:, gs], preferred_element_type=F32) / den
                o_ref[pl.ds(r0, blk), hs] = o
                l_ref[pl.ds(r0, blk), hs] = jnp.broadcast_to(m + jnp.log(den), (blk, HEAD_DIM))
            return carry

        lax.fori_loop(0, nb, blk_body, 0)

    in_specs = [_band_spec(sl, qs, rb), _band_spec(sl, ks, rb), _band_spec(sl, vs, rb),
                pl.BlockSpec((nh, blk, kw), lambda b, r: (0, 0, 0))]
    args = [src] * 3 + [bias]
    if has_sink:
        in_specs.append(pl.BlockSpec((nh, 8, 128), lambda b, r: (0, 0, 0)))
        args.append(sink)
    return pl.pallas_call(
        body, name=name, grid=(bl, dil // rb), in_specs=in_specs,
        out_specs=[_band_spec(sl, (256, 0), rb)] * 2,
        out_shape=[_sds((bl, dil, sl, 256), F32)] * 2,
        scratch_shapes=[pltpu.VMEM((sl + 2 * rad, ks[0]), BF16), pltpu.VMEM((sl + 2 * rad, vs[0]), BF16)],
        compiler_params=_cparams(PAR, PAR),
    )(*args)


def _band_bwd(src, qs, ks, vs, bias, sink, dy, dcol, lse, delta, *, rad, nh, nkv, name):
    bl, dil, sl, _ = src.shape
    blk = bias.shape[1]
    kw = blk + 2 * rad
    nb = sl // blk
    rep = nh // nkv
    has_sink = sink is not None
    rb = _residues_per_step(dil, sl)
    wk, wv = ks[0], vs[0]

    def body(*refs):
        q_all, k_all, v_all, b_ref = refs[:4]
        s_ref = refs[4] if has_sink else None
        do_all, l_all, dl_all = refs[4 + has_sink:7 + has_sink]
        outs = refs[7 + has_sink:]
        dsk_ref = None
        if has_sink:
            dq_all, dk_all, dv_all, db_ref, dsk_ref, kp, vp, dka, dva = outs
        else:
            dq_all, dk_all, dv_all, db_ref, kp, vp, dka, dva = outs

        @pl.when((pl.program_id(0) == 0) & (pl.program_id(1) == 0))
        def _():
            db_ref[...] = jnp.zeros_like(db_ref)
            if has_sink:
                dsk_ref[...] = jnp.zeros_like(dsk_ref)

        for ri in range(rb):
            one_sequence(q_all.at[ri], k_all.at[ri], v_all.at[ri], b_ref, s_ref, do_all.at[ri], l_all.at[ri],
                         dl_all.at[ri], dq_all.at[ri], dk_all.at[ri], dv_all.at[ri], db_ref, dsk_ref, kp, vp, dka, dva)

    def one_sequence(q_ref, k_ref, v_ref, b_ref, s_ref, do_ref, l_ref, dl_ref, dq_ref, dk_ref, dv_ref, db_ref, dsk_ref,
                     kp, vp, dka, dva):
        _fill_padded(kp, k_ref, rad, sl)
        _fill_padded(vp, v_ref, rad, sl)
        dka[...] = jnp.zeros_like(dka)
        dva[...] = jnp.zeros_like(dva)

        def blk_body(i, carry):
            r0 = pl.multiple_of(i * blk, blk)
            qb = q_ref[pl.ds(r0, blk), :]
            kwin = kp[pl.ds(r0, kw), :]
            vwin = vp[pl.ds(r0, kw), :]
            dob = do_ref[pl.ds(r0, blk), :].astype(BF16)
            lb = l_ref[pl.ds(r0, blk), :]
            dlb = dl_ref[pl.ds(r0, blk), :]
            col = r0 - rad + lax.broadcasted_iota(jnp.int32, (blk, kw), 1)
            neg = jnp.where((col >= 0) & (col < sl), 0.0, NEG_INF).astype(F32)
            for h in range(nh):
                g = h // rep
                hs = slice(h * HEAD_DIM, (h + 1) * HEAD_DIM)
                gs = slice(g * HEAD_DIM, (g + 1) * HEAD_DIM)
                qh, kh, vh, doh = qb[:, hs], kwin[:, gs], vwin[:, gs], dob[:, hs]
                lh = lb[:, h * HEAD_DIM:h * HEAD_DIM + 1]
                dlh = dlb[:, h * HEAD_DIM:h * HEAD_DIM + 1]
                s = lax.dot_general(qh, kh, NT, preferred_element_type=F32) + b_ref[h] + neg
                p = jnp.exp(s - lh)
                dp = lax.dot_general(doh, vh, NT, preferred_element_type=F32)
                ds = p * (dp - dlh)
                dsb = ds.astype(BF16)
                dq_ref[pl.ds(r0, blk), hs] = jnp.dot(dsb, kh, preferred_element_type=F32).astype(BF16)
                dka[pl.ds(r0, kw), gs] += lax.dot_general(dsb, qh, TN, preferred_element_type=F32)
                dva[pl.ds(r0, kw), gs] += lax.dot_general(p.astype(BF16), doh, TN, preferred_element_type=F32)
                db_ref[h] += ds
                if has_sink:
                    ps = jnp.exp(s_ref[h][0:1, 0:1] - lh)
                    dsk_ref[h] += jnp.broadcast_to(-jnp.sum(ps * dlh, axis=0, keepdims=True), (8, 128))
            return carry

        lax.fori_loop(0, nb, blk_body, 0)
        dk_ref[...] = dka[rad:rad + sl, :].astype(BF16)
        dv_ref[...] = dva[rad:rad + sl, :].astype(BF16)

    const3 = lambda b, r: (0, 0, 0)
    in_specs = [_band_spec(sl, qs, rb), _band_spec(sl, ks, rb), _band_spec(sl, vs, rb),
                pl.BlockSpec((nh, blk, kw), const3)]
    args = [src] * 3 + [bias]
    if has_sink:
        in_specs.append(pl.BlockSpec((nh, 8, 128), const3))
        args.append(sink)
    row = _band_spec(sl, (256, 0), rb)
    in_specs += [_band_spec(sl, (256, dcol), rb), row, row]
    args += [dy, lse, delta]
    out_specs = [row, _band_spec(sl, (wk, 0), rb), _band_spec(sl, (wv, 0), rb), pl.BlockSpec((nh, blk, kw), const3)]
    out_shape = [_sds((bl, dil, sl, 256), BF16), _sds((bl, dil, sl, wk), BF16), _sds((bl, dil, sl, wv), BF16),
                 _sds((nh, blk, kw), F32)]
    if has_sink:
        out_specs.append(pl.BlockSpec((nh, 8, 128), const3))
        out_shape.append(_sds((nh, 8, 128), F32))
    return pl.pallas_call(
        body, name=name, grid=(bl, dil // rb), in_specs=in_specs, out_specs=out_specs, out_shape=out_shape,
        scratch_shapes=[pltpu.VMEM((sl + 2 * rad, wk), BF16), pltpu.VMEM((sl + 2 * rad, wv), BF16),
                        pltpu.VMEM((sl + 2 * rad, wk), F32), pltpu.VMEM((sl + 2 * rad, wv), F32)],
        compiler_params=_cparams(ARB, ARB),
    )(*args)


def _combine_a(os_, ls_, *, tm, name):
    bl = os_[1].shape[0]
    t = bl * SEQ
    nt = SEQ // tm

    def body(o1, o4, o16, l1, l4, l16, y_ref, lt_ref, scr):
        for k, (d, ref) in enumerate(((4, o4), (16, o16), (4, l4), (16, l16))):
            for r in range(d):
                _write_residue(scr, 2 * k, r, d, ref[r])
        o2, o3, b, c = (_gather_cols(scr, 2 * k, 2) for k in range(4))
        a = l1[...]
        m = jnp.maximum(jnp.maximum(a, b), c)
        ea, eb, ec = jnp.exp(a - m), jnp.exp(b - m), jnp.exp(c - m)
        den = ea + eb + ec
        y_ref[...] = (ea / den) * o1[...] + (eb / den) * o2 + (ec / den) * o3
        lt_ref[...] = m + jnp.log(den)

    specs = _residue_specs(tm, 256, nt)
    return pl.pallas_call(
        body, name=name, grid=(bl, nt), in_specs=specs * 2, out_specs=[specs[0]] * 2,
        out_shape=[_sds((t, 256), F32)] * 2, scratch_shapes=[pltpu.VMEM((8, tm, LANES), F32)],
        compiler_params=_cparams(PAR, PAR),
    )(*os_, *ls_)


def _deltas(dycat, ya, yb, yd, lse_a, *, tm, name):
    t = ya.shape[0]
    bl = t // SEQ
    nt = SEQ // tm

    def body(dy_ref, ya_ref, yb_ref, yd_ref, la_ref, dy4, dy16, l4, l16, da1, da4, da16, db_ref, dd_ref, scr):
        e = _group_sum_matrix(256, True)
        dya = dy_ref[:, 0:256]
        dla = _seg_sum(dya * ya_ref[...], e)
        da1[...] = dla
        db_ref[...] = _seg_sum(dy_ref[:, 256:512] * yb_ref[...], e)
        dd_ref[...] = _seg_sum(dy_ref[:, 768:1024] * yd_ref[...], e)
        for k, (val, r4, r16) in enumerate(((dya, dy4, dy16), (la_ref[...], l4, l16), (dla, da4, da16))):
            _scatter_cols(scr, 2 * k, val)
            for d, ref in ((4, r4), (16, r16)):
                for r in range(d):
                    ref[r] = _read_residue(scr, 2 * k, 2, r, d)

    specs = _residue_specs(tm, 256, nt)
    nat = specs[0]
    shapes = _residue_shapes(bl, 256, F32)
    outs = pl.pallas_call(
        body, name=name, grid=(bl, nt),
        in_specs=[pl.BlockSpec((tm, 1024), lambda b, i: (b * nt + i, 0)), nat, nat, nat, nat],
        out_specs=specs[1:] + specs[1:] + specs + [nat, nat],
        out_shape=shapes[1:] + shapes[1:] + shapes + [shapes[0], shapes[0]],
        scratch_shapes=[pltpu.VMEM((6, tm, LANES), F32)],
        compiler_params=_cparams(PAR, PAR),
    )(dycat, ya, yb, yd, lse_a)
    return outs[0:2], outs[2:4], outs[4:7], outs[7], outs[8]


def _dense_fwd(qd, *, tq, name):
    t = qd.shape[0]
    bl = t // SEQ
    nq = SEQ // tq

    def body(q_ref, k_ref, v_ref, o_ref, l_ref):
        q = q_ref[...]
        for g in range(2):
            h0, h1 = 2 * g, 2 * g + 1
            q2 = jnp.concatenate([q[:, h0 * 64:(h0 + 1) * 64], q[:, h1 * 64:(h1 + 1) * 64]], axis=0)
            kg = k_ref[:, g * 64:(g + 1) * 64]
            vg = v_ref[:, g * 64:(g + 1) * 64]
            s = lax.dot_general(q2, kg, NT, preferred_element_type=F32)
            m = jnp.max(s, axis=1, keepdims=True)
            p = jnp.exp(s - m)
            den = jnp.sum(p, axis=1, keepdims=True)
            o2 = jnp.dot(p.astype(BF16), vg, preferred_element_type=F32) / den
            l2 = jnp.broadcast_to(m + jnp.log(den), (2 * tq, 64))
            o_ref[:, h0 * 64:(h0 + 1) * 64] = o2[:tq]
            o_ref[:, h1 * 64:(h1 + 1) * 64] = o2[tq:]
            l_ref[:, h0 * 64:(h0 + 1) * 64] = l2[:tq]
            l_ref[:, h1 * 64:(h1 + 1) * 64] = l2[tq:]

    q3 = qd.reshape(bl, SEQ, 512)
    o, lse = pl.pallas_call(
        body, name=name, grid=(bl, nq),
        in_specs=[pl.BlockSpec((None, tq, 256), lambda b, i: (b, i, 0)),
                  pl.BlockSpec((None, SEQ, 128), lambda b, i: (b, 0, 2)),
                  pl.BlockSpec((None, SEQ, 128), lambda b, i: (b, 0, 3))],
        out_specs=[pl.BlockSpec((None, tq, 256), lambda b, i: (b, i, 0))] * 2,
        out_shape=[_sds((bl, SEQ, 256), F32)] * 2,
        compiler_params=_cparams(PAR, PAR),
    )(q3, q3, q3)
    return o.reshape(t, 256), lse.reshape(t, 256)


def _dense_bwd(qd, dycat, lse, delta, *, tq, name):
    t = qd.shape[0]
    bl = t // SEQ
    nq = SEQ // tq

    def body(q_ref, k_ref, v_ref, do_ref, l_ref, dl_ref, dq_ref, dk_ref, dv_ref, dkt, dvt):
        @pl.when(pl.program_id(1) == 0)
        def _():
            dkt[...] = jnp.zeros_like(dkt)
            dvt[...] = jnp.zeros_like(dvt)

        q = q_ref[...]
        do = do_ref[...].astype(BF16)
        lv = l_ref[...]
        dlv = dl_ref[...]
        for g in range(2):
            h0, h1 = 2 * g, 2 * g + 1
            q2 = jnp.concatenate([q[:, h0 * 64:(h0 + 1) * 64], q[:, h1 * 64:(h1 + 1) * 64]], axis=0)
            do2 = jnp.concatenate([do[:, h0 * 64:(h0 + 1) * 64], do[:, h1 * 64:(h1 + 1) * 64]], axis=0)
            l2 = jnp.concatenate([lv[:, h0 * 64:h0 * 64 + 1], lv[:, h1 * 64:h1 * 64 + 1]], axis=0)
            dl2 = jnp.concatenate([dlv[:, h0 * 64:h0 * 64 + 1], dlv[:, h1 * 64:h1 * 64 + 1]], axis=0)
            kg = k_ref[:, g * 64:(g + 1) * 64]
            vg = v_ref[:, g * 64:(g + 1) * 64]
            s = lax.dot_general(q2, kg, NT, preferred_element_type=F32)
            p = jnp.exp(s - l2)
            dp = lax.dot_general(do2, vg, NT, preferred_element_type=F32)
            ds = (p * (dp - dl2)).astype(BF16)
            dq2 = jnp.dot(ds, kg, preferred_element_type=F32)
            dq_ref[:, h0 * 64:(h0 + 1) * 64] = dq2[:tq].astype(BF16)
            dq_ref[:, h1 * 64:(h1 + 1) * 64] = dq2[tq:].astype(BF16)
            dkt[g * 64:(g + 1) * 64, :] += lax.dot_general(q2, ds, TN, preferred_element_type=F32)
            dvt[g * 64:(g + 1) * 64, :] += lax.dot_general(do2, p.astype(BF16), TN, preferred_element_type=F32)

        @pl.when(pl.program_id(1) == nq - 1)
        def _():
            dk_ref[...] = dkt[...].T.astype(BF16)
            dv_ref[...] = dvt[...].T.astype(BF16)

    q3 = qd.reshape(bl, SEQ, 512)
    tile = pl.BlockSpec((None, tq, 256), lambda b, i: (b, i, 0))
    full = pl.BlockSpec((None, SEQ, 128), lambda b, i: (b, 0, 0))
    dq, dk, dv = pl.pallas_call(
        body, name=name, grid=(bl, nq),
        in_specs=[tile, pl.BlockSpec((None, SEQ, 128), lambda b, i: (b, 0, 2)),
                  pl.BlockSpec((None, SEQ, 128), lambda b, i: (b, 0, 3)),
                  pl.BlockSpec((None, tq, 256), lambda b, i: (b, i, 3)), tile, tile],
        out_specs=[tile, full, full],
        out_shape=[_sds((bl, SEQ, 256), BF16), _sds((bl, SEQ, 128), BF16), _sds((bl, SEQ, 128), BF16)],
        scratch_shapes=[pltpu.VMEM((128, SEQ), F32), pltpu.VMEM((128, SEQ), F32)],
        compiler_params=_cparams(PAR, ARB),
    )(q3, q3, q3, dycat.reshape(bl, SEQ, 1024), lse.reshape(bl, SEQ, 256), delta.reshape(bl, SEQ, 256))
    return dq.reshape(t, 256), dk.reshape(t, 128), dv.reshape(t, 128)


def _c_norm(cv, gam, bet):
    vg = _gelu(cv)
    mu = jnp.mean(vg, axis=-1, keepdims=True)
    xc = vg - mu
    r = lax.rsqrt(jnp.mean(xc * xc, axis=-1, keepdims=True) + EPS)
    xhat = xc * r
    return xhat * gam + bet, xhat, r


def _c_fwd(proj, gam, bet, ws, bst, *, tm, name):
    t = proj.shape[0]
    nch = tm // C_CHUNK

    def body(u_ref, v_ref, g_ref, b_ref, ws_ref, bs_ref, y_ref):
        vn, _, _ = _c_norm(v_ref[...].astype(F32), g_ref[...], b_ref[...])
        vnb = vn.astype(BF16)
        for c in range(nch):
            rows = slice(c * C_CHUNK, (c + 1) * C_CHUNK)
            for g in range(C_GROUPS):
                gs = slice(g * 64, (g + 1) * 64)
                mixed = jnp.dot(ws_ref[g], vnb[rows, gs], preferred_element_type=F32) + bs_ref[:, gs]
                y_ref[rows, gs] = _gelu(u_ref[rows, gs].astype(F32)) * mixed

    vec = pl.BlockSpec((1, 256), lambda i: (0, 0))
    return pl.pallas_call(
        body, name=name, grid=(t // tm,),
        in_specs=[pl.BlockSpec((tm, 256), lambda i: (i, 5)), pl.BlockSpec((tm, 256), lambda i: (i, 6)), vec, vec,
                  pl.BlockSpec((C_GROUPS, C_CHUNK, C_CHUNK), lambda i: (0, 0, 0)),
                  pl.BlockSpec((C_CHUNK, 256), lambda i: (0, 0))],
        out_specs=pl.BlockSpec((tm, 256), lambda i: (i, 0)), out_shape=_sds((t, 256), F32),
        compiler_params=_cparams(PAR),
    )(proj, proj, gam, bet, ws, bst)


def _c_bwd(proj, dycat, gam, bet, ws, wst, bst, *, tm, name):
    t = proj.shape[0]
    nch = tm // C_CHUNK
    nstep = t // tm

    def body(u_ref, v_ref, dy_ref, g_ref, b_ref, ws_ref, wst_ref, bs_ref,
             du_ref, dv_ref, dws_ref, dbs_ref, dg_ref, db_ref, dvn_s):
        step = pl.program_id(0)

        @pl.when(step == 0)
        def _():
            dws_ref[...] = jnp.zeros_like(dws_ref)
            dbs_ref[...] = jnp.zeros_like(dbs_ref)
            dg_ref[...] = jnp.zeros_like(dg_ref)
            db_ref[...] = jnp.zeros_like(db_ref)

        cv = v_ref[...].astype(F32)
        gam_v = g_ref[...]
        vn, xhat, r = _c_norm(cv, gam_v, b_ref[...])
        vnb = vn.astype(BF16)
        for c in range(nch):
            rows = slice(c * C_CHUNK, (c + 1) * C_CHUNK)
            for g in range(C_GROUPS):
                gs = slice(g * 64, (g + 1) * 64)
                cu = u_ref[rows, gs].astype(F32)
                dy = dy_ref[rows, gs]
                mixed = jnp.dot(ws_ref[g], vnb[rows, gs], preferred_element_type=F32) + bs_ref[:, gs]
                du_ref[rows, gs] = (dy * mixed * _gelu_grad(cu)).astype(BF16)
                dmix = dy * _gelu(cu)
                dbs_ref[:, gs] += dmix
                dmb = dmix.astype(BF16)
                dws_ref[g] += lax.dot_general(dmb, vnb[rows, gs], NT, preferred_element_type=F32)
                dvn_s[rows, gs] = jnp.dot(wst_ref[g], dmb, preferred_element_type=F32)
        dvn = dvn_s[...]
        dg_ref[...] += jnp.sum(dvn * xhat, axis=0, keepdims=True)
        db_ref[...] += jnp.sum(dvn, axis=0, keepdims=True)
        dxh = dvn * gam_v
        dvg = r * (dxh - jnp.mean(dxh, axis=-1, keepdims=True) - xhat * jnp.mean(dxh * xhat, axis=-1, keepdims=True))
        dv_ref[...] = (dvg * _gelu_grad(cv)).astype(BF16)

        @pl.when(step == nstep - 1)
        def _():
            dbs_ref[...] = _seg_sum(dbs_ref[...], _group_sum_matrix(256, True))

    vec = pl.BlockSpec((1, 256), lambda i: (0, 0))
    mat = pl.BlockSpec((C_GROUPS, C_CHUNK, C_CHUNK), lambda i: (0, 0, 0))
    bsp = pl.BlockSpec((C_CHUNK, 256), lambda i: (0, 0))
    tile = pl.BlockSpec((tm, 256), lambda i: (i, 0))
    return pl.pallas_call(
        body, name=name, grid=(nstep,),
        in_specs=[pl.BlockSpec((tm, 256), lambda i: (i, 5)), pl.BlockSpec((tm, 256), lambda i: (i, 6)),
                  pl.BlockSpec((tm, 256), lambda i: (i, 2)), vec, vec, mat, mat, bsp],
        out_specs=[tile, tile, mat, bsp, vec, vec],
        out_shape=[_sds((t, 256), BF16), _sds((t, 256), BF16), _sds((C_GROUPS, C_CHUNK, C_CHUNK), F32),
                   _sds((C_CHUNK, 256), F32), _sds((1, 256), F32), _sds((1, 256), F32)],
        scratch_shapes=[pltpu.VMEM((tm, 256), F32)],
        compiler_params=_cparams(ARB),
    )(proj, proj, dycat, gam, bet, ws, wst, bst)


FF_TC = 128
FF_NB = D_FF // FF_TC
FF_CH = 64
FF_HALO = 16


def _taps(ref, r0, win, where):
    z = jnp.zeros((FF_HALO, win.shape[1]), F32)
    if where == "first":
        win[0:FF_HALO, :] = z
        win[FF_HALO:, :] = ref[0:FF_CH + FF_HALO, :].astype(F32)
    elif where == "last":
        win[0:FF_CH + FF_HALO, :] = ref[SEQ - FF_CH - FF_HALO:SEQ, :].astype(F32)
        win[FF_CH + FF_HALO:, :] = z
    else:
        win[...] = ref[pl.ds(pl.multiple_of(r0 - FF_HALO, FF_HALO), FF_CH + 2 * FF_HALO), :].astype(F32)
    return tuple(win[FF_HALO + o:FF_HALO + o + FF_CH, :] for o in (-1, 0, 1))


def _chunk_loop(step):
    step(0, lambda ref, win: _taps(ref, 0, win, "first"))

    def mid(i, carry):
        r0 = pl.multiple_of(i * FF_CH, FF_CH)
        step(r0, lambda ref, win: _taps(ref, r0, win, "mid"))
        return carry

    lax.fori_loop(1, SEQ // FF_CH - 1, mid, 0)
    step(SEQ - FF_CH, lambda ref, win: _taps(ref, SEQ - FF_CH, win, "last"))


def _conv3(taps, w_ref, b_ref):
    dn, md, up = taps
    return w_ref[0:1, :] * dn + w_ref[1:2, :] * md + w_ref[2:3, :] * up + b_ref[...]


def _ff_specs(order):
    def at(fn):
        return (lambda b, j: fn(b, j)) if order == "bj" else (lambda j, b: fn(b, j))
    hs = [pl.BlockSpec((None, SEQ, FF_TC), at(lambda b, j, o=o: (b, 0, j + o))) for o in (0, FF_NB)]
    ws = [pl.BlockSpec((3, FF_TC), at(lambda b, j, o=o: (0, j + o))) for o in (0, FF_NB)]
    bs = [pl.BlockSpec((1, FF_TC), at(lambda b, j, o=o: (0, j + o))) for o in (0, FF_NB)]
    return hs, ws, bs


def _conv_gate_fwd(h, cw, cb, *, name):
    t = h.shape[0]
    bl = t // SEQ

    def body(hg_ref, hu_ref, wg_ref, wu_ref, bg_ref, bu_ref, a_ref, cg_ref, cu_ref, win):
        def step(r0, taps):
            cg = _conv3(taps(hg_ref, win.at[0]), wg_ref, bg_ref)
            cu = _conv3(taps(hu_ref, win.at[1]), wu_ref, bu_ref)
            a_ref[pl.ds(r0, FF_CH), :] = (cg * _sigmoid(cg) * cu).astype(BF16)
            cg_ref[pl.ds(r0, FF_CH), :] = cg.astype(BF16)
            cu_ref[pl.ds(r0, FF_CH), :] = cu.astype(BF16)

        _chunk_loop(step)

    hs, ws, bs = _ff_specs("bj")
    h3 = h.reshape(bl, SEQ, 2 * D_FF)
    half = pl.BlockSpec((None, SEQ, FF_TC), lambda b, j: (b, 0, j))
    outs = pl.pallas_call(
        body, name=name, grid=(bl, FF_NB), in_specs=hs + ws + bs, out_specs=[half] * 3,
        out_shape=[_sds((bl, SEQ, D_FF), BF16)] * 3,
        scratch_shapes=[pltpu.VMEM((2, FF_CH + 2 * FF_HALO, FF_TC), F32)],
        compiler_params=_cparams(PAR, PAR),
    )(h3, h3, cw, cw, cb, cb)
    return [o.reshape(t, D_FF) for o in outs]


def _conv_gate_bwd(h, cg_all, cu_all, dact, cw, cb, *, name):
    t = h.shape[0]
    bl = t // SEQ

    def body(hg_ref, hu_ref, wg_ref, wu_ref, bg_ref, bu_ref, da_ref, cg_ref, cu_ref,
             dhg_ref, dhu_ref, dwg_ref, dwu_ref, dbg_ref, dbu_ref, dg_s, du_s, win, sums):
        @pl.when(pl.program_id(1) == 0)
        def _():
            for ref in (dwg_ref, dwu_ref, dbg_ref, dbu_ref):
                ref[...] = jnp.zeros_like(ref)

        sums[...] = jnp.zeros_like(sums)
        red = lambda x: jnp.sum(x.reshape(FF_CH // 8, 8, x.shape[1]), axis=0)

        def pass1(r0, taps):
            tg, tu = taps(hg_ref, win.at[0]), taps(hu_ref, win.at[1])
            cg = cg_ref[pl.ds(r0, FF_CH), :].astype(F32)
            cu = cu_ref[pl.ds(r0, FF_CH), :].astype(F32)
            da = da_ref[pl.ds(r0, FF_CH), :].astype(F32)
            sg = _sigmoid(cg)
            dcg = da * cu * (sg * (1.0 + cg * (1.0 - sg)))
            dcu = da * (cg * sg)
            dg_s[pl.ds(r0, FF_CH), :] = dcg
            du_s[pl.ds(r0, FF_CH), :] = dcu
            for half, (d, tp) in enumerate(((dcg, tg), (dcu, tu))):
                for k in range(3):
                    sums[4 * half + k] += red(d * tp[k])
                sums[4 * half + 3] += red(d)

        _chunk_loop(pass1)
        for half, (dw_ref, db_ref) in enumerate(((dwg_ref, dbg_ref), (dwu_ref, dbu_ref))):
            for k in range(3):
                dw_ref[k:k + 1, :] += jnp.sum(sums[4 * half + k], axis=0, keepdims=True)
            db_ref[...] += jnp.sum(sums[4 * half + 3], axis=0, keepdims=True)

        def pass2(r0, taps):
            for k, (s, w_ref, o_ref) in enumerate(((dg_s, wg_ref, dhg_ref), (du_s, wu_ref, dhu_ref))):
                dn, md, up = taps(s, win.at[k])
                o_ref[pl.ds(r0, FF_CH), :] = (w_ref[0:1, :] * up + w_ref[1:2, :] * md + w_ref[2:3, :] * dn).astype(BF16)

        _chunk_loop(pass2)

    hs, ws, bs = _ff_specs("jb")
    half = pl.BlockSpec((None, SEQ, FF_TC), lambda j, b: (b, 0, j))
    wsp = pl.BlockSpec((3, FF_TC), lambda j, b: (0, j))
    bsp = pl.BlockSpec((1, FF_TC), lambda j, b: (0, j))
    h3 = h.reshape(bl, SEQ, 2 * D_FF)
    dhg, dhu, dwg, dwu, dbg, dbu = pl.pallas_call(
        body, name=name, grid=(FF_NB, bl), in_specs=hs + ws + bs + [half] * 3,
        out_specs=[half, half, wsp, wsp, bsp, bsp],
        out_shape=[_sds((bl, SEQ, D_FF), BF16), _sds((bl, SEQ, D_FF), BF16), _sds((3, D_FF), F32), _sds((3, D_FF), F32),
                   _sds((1, D_FF), F32), _sds((1, D_FF), F32)],
        scratch_shapes=[pltpu.VMEM((SEQ, FF_TC), F32), pltpu.VMEM((SEQ, FF_TC), F32),
                        pltpu.VMEM((2, FF_CH + 2 * FF_HALO, FF_TC), F32), pltpu.VMEM((8, 8, FF_TC), F32)],
        compiler_params=_cparams(PAR, ARB),
    )(h3, h3, cw, cw, cb, cb, *[a.reshape(bl, SEQ, D_FF) for a in (dact, cg_all, cu_all)])
    return (dhg.reshape(t, D_FF), dhu.reshape(t, D_FF), jnp.concatenate([dwg, dwu], axis=1),
            jnp.concatenate([dbg, dbu], axis=1))


def _ple_fwd(x2, gain, wg, pe, pe_blk, wp, *, tm, name):
    t, k = x2.shape

    def body(x_ref, g_ref, wg_ref, pe_ref, wp_ref, hn_ref, x3_ref, gt_ref, pp_ref):
        x = x_ref[...]
        r = lax.rsqrt(jnp.mean(x * x, axis=-1, keepdims=True) + EPS)
        hn = (x * r * g_ref[...]).astype(BF16)
        hn_ref[...] = hn
        gate = _sigmoid(jnp.dot(hn, wg_ref[...], preferred_element_type=F32))
        pp = jnp.dot(pe_ref[...].astype(BF16), wp_ref[...], preferred_element_type=F32)
        gt_ref[...] = gate.astype(BF16)
        pp_ref[...] = pp.astype(BF16)
        x3_ref[...] = x + pp * gate

    row = pl.BlockSpec((tm, k), lambda i: (i, 0))
    return pl.pallas_call(
        body, name=name, grid=(t // tm,),
        in_specs=[row, pl.BlockSpec((1, k), lambda i: (0, 0)), pl.BlockSpec((k, k), lambda i: (0, 0)),
                  pl.BlockSpec((tm, PLE_DIM), lambda i: (pe_blk + i, 0)), pl.BlockSpec((PLE_DIM, k), lambda i: (0, 0))],
        out_specs=[row, row, row, row],
        out_shape=[_sds((t, k), BF16), _sds((t, k), F32), _sds((t, k), BF16), _sds((t, k), BF16)],
        compiler_params=_cparams(PAR),
    )(x2, gain, wg, pe, wp)


def _ple_bwd_ew(dx3, gate, pp, *, tm, name):
    t, n = dx3.shape

    def body(d_ref, g_ref, p_ref, dz_ref, dpp_ref):
        d, g = d_ref[...], g_ref[...]
        dz_ref[...] = (d * p_ref[...] * g * (1.0 - g)).astype(BF16)
        dpp_ref[...] = (d * g).astype(BF16)

    spec = pl.BlockSpec((tm, n), lambda i: (i, 0))
    return pl.pallas_call(
        body, name=name, grid=(t // tm,), in_specs=[spec] * 3, out_specs=[spec] * 2,
        out_shape=[_sds((t, n), BF16)] * 2, compiler_params=_cparams(PAR),
    )(dx3, gate, pp)


def _loss_head(y, tgt, *, tm, name):
    t, d = y.shape

    def body(y_ref, t_ref, l_ref, dy_ref):
        @pl.when(pl.program_id(0) == 0)
        def _():
            l_ref[...] = jnp.zeros_like(l_ref)

        e = y_ref[...] - t_ref[...]
        dy_ref[...] = e * (1.0 / d)
        s = jnp.sum(jnp.sum(e * e, axis=1, keepdims=True), axis=0, keepdims=True)
        l_ref[...] += jnp.broadcast_to(s * (0.5 / d), (8, 128))

    spec = pl.BlockSpec((tm, d), lambda i: (i, 0))
    return pl.pallas_call(
        body, name=name, grid=(t // tm,), in_specs=[spec, spec],
        out_specs=[pl.BlockSpec((8, 128), lambda i: (0, 0)), spec],
        out_shape=[_sds((8, 128), F32), _sds((t, d), F32)], compiler_params=_cparams(ARB),
    )(y, tgt)


BIAS_PC = 8192


def _onehot(bucket_row):
    rows = lax.broadcasted_iota(jnp.int32, (REL_BUCKETS, bucket_row.shape[1]), 0)
    return (rows == bucket_row).astype(BF16)


def _dot3(x, onehot, dims):
    acc = None
    for _ in range(3):
        term = x.astype(BF16)
        part = lax.dot_general(term, onehot, dims, preferred_element_type=F32)
        acc = part if acc is None else acc + part
        x = x - term.astype(F32)
    return acc


def _bias_lookup(table_t, bucket, *, name):
    h = table_t.shape[0]
    p = bucket.shape[1]

    def body(t_ref, b_ref, o_ref):
        bk = b_ref[...]
        val = _dot3(t_ref[...], _onehot(bk), (((1,), (0,)), ((), ())))
        o_ref[...] = jnp.where(bk >= 0, val, NEG_INF)

    return pl.pallas_call(
        body, name=name, grid=(p // BIAS_PC,),
        in_specs=[pl.BlockSpec((h, REL_BUCKETS), lambda i: (0, 0)), pl.BlockSpec((1, BIAS_PC), lambda i: (0, i))],
        out_specs=pl.BlockSpec((h, BIAS_PC), lambda i: (0, i)), out_shape=_sds((h, p), F32),
        compiler_params=_cparams(PAR),
    )(table_t, bucket)


def _bucket_reduce(dbiases, bucket, *, name):
    h, p = dbiases[0].shape
    nl = len(dbiases)

    def body(*refs):
        b_ref, o_ref = refs[nl], refs[nl + 1]

        @pl.when(pl.program_id(0) == 0)
        def _():
            o_ref[...] = jnp.zeros_like(o_ref)

        d = refs[0][...]
        for d_ref in refs[1:nl]:
            d = d + d_ref[...]
        o_ref[...] += _dot3(d, _onehot(b_ref[...]), NT)

    return pl.pallas_call(
        body, name=name, grid=(p // BIAS_PC,),
        in_specs=[pl.BlockSpec((h, BIAS_PC), lambda i: (0, i))] * nl + [pl.BlockSpec((1, BIAS_PC), lambda i: (0, i))],
        out_specs=pl.BlockSpec((h, REL_BUCKETS), lambda i: (0, 0)), out_shape=_sds((h, REL_BUCKETS), F32),
        compiler_params=_cparams(ARB),
    )(*dbiases, bucket)


def _adamw_math(w, g, m, v):
    m = ADAM_B1 * m + (1.0 - ADAM_B1) * g
    v = ADAM_B2 * v + (1.0 - ADAM_B2) * (g * g)
    m_hat = m / (1.0 - ADAM_B1 ** ADAM_STEP)
    v_hat = v / (1.0 - ADAM_B2 ** ADAM_STEP)
    delta = -ADAM_LR * (m_hat / (jnp.sqrt(v_hat) + ADAM_EPS) + ADAM_WD * w)
    return delta, m, v


def _adamw_reduce(parts, w, m, v, *, tr, name):
    nl = len(parts)
    rows, c = w.shape
    r = rows // nl
    nt = r // tr

    def body(*refs):
        p_refs = refs[:nl]
        w_ref, m_ref, v_ref, g_ref, d_ref, nm_ref, nv_ref = refs[nl:]
        for li, p_ref in enumerate(p_refs):
            @pl.when(pl.program_id(0) == li)
            def _(p_ref=p_ref):
                g = p_ref[0].astype(F32)
                for k in range(1, N_DEV):
                    g = g + p_ref[k].astype(F32)
                d, nm, nv = _adamw_math(w_ref[...], g, m_ref[...], v_ref[...])
                g_ref[...] = g
                d_ref[...] = d
                nm_ref[...] = nm
                nv_ref[...] = nv

    def part_map(li):
        return lambda l, i: (0, jnp.where(l == li, i, jnp.where(l < li, 0, nt - 1)), 0)

    spec = pl.BlockSpec((tr, c), lambda l, i: (l * nt + i, 0))
    return pl.pallas_call(
        body, name=name, grid=(nl, nt),
        in_specs=[pl.BlockSpec((N_DEV, tr, c), part_map(li)) for li in range(nl)] + [spec, spec, spec],
        out_specs=[spec] * 4, out_shape=[_sds((rows, c), F32)] * 4, compiler_params=_cparams(ARB, ARB),
    )(*parts, w, m, v)


def _adamw_plain(g, w, m, v, *, name):
    def body(g_ref, w_ref, m_ref, v_ref, d_ref, nm_ref, nv_ref):
        d, nm, nv = _adamw_math(w_ref[...], g_ref[...], m_ref[...], v_ref[...])
        d_ref[...] = d
        nm_ref[...] = nm
        nv_ref[...] = nv

    return pl.pallas_call(body, name=name, out_shape=[_sds(w.shape, F32)] * 3)(g, w, m, v)


def _mesh_pos():
    return lax.axis_index("x"), lax.axis_index("y"), lax.axis_index("c")


def _allgather_body(x_refs, out_refs, send_sems, recv_sems, local_sems, slot):
    x, y, c = _mesh_pos()
    me, sibling = (x, y, c), (x, y, 1 - c)
    chips = [(1 - x, y), (x, 1 - y), (1 - x, 1 - y)]
    waits = []
    for a, (x_ref, out_ref) in enumerate(zip(x_refs, out_refs)):
        def copy(k, block, to, src=None, out_ref=out_ref, a=a):
            return pltpu.make_async_remote_copy(
                src_ref=slot(out_ref, block) if src is None else src, dst_ref=slot(out_ref, block),
                send_sem=send_sems.at[a, k], recv_sem=recv_sems.at[a, k], device_id=to, device_id_type=MESH)

        mine = pltpu.make_async_copy(x_ref, slot(out_ref, me), local_sems.at[a])
        mine.start()
        first = [copy(0, me, sibling, src=x_ref)]
        first += [copy(1 + j, me, (*chip, c), src=x_ref) for j, chip in enumerate(chips)]
        for cp in first:
            cp.start()
        waits.append((copy, mine, first))
    sends = []
    for copy, mine, first in waits:
        passed = [copy(4 + j, (*chip, c), sibling) for j, chip in enumerate(chips)]
        for j, chip in enumerate(chips):
            copy(1 + j, (*chip, c), me).wait_recv()
            passed[j].start()
        sends.append(passed)
    for (copy, mine, first), passed in zip(waits, sends):
        copy(0, sibling, me).wait_recv()
        for j, chip in enumerate(chips):
            copy(4 + j, (*chip, 1 - c), me).wait_recv()
        for cp in first + passed:
            cp.wait_send()
        mine.wait()


PEER_FLIPS = ((0, 0, 1), (1, 0, 0), (0, 1, 0), (1, 1, 0), (1, 0, 1), (0, 1, 1), (1, 1, 1))


def _peer_copies(x_refs, land_refs, send_sem, recv_sem, scatter):
    x, y, c = _mesh_pos()
    me = 4 * x + 2 * y + c
    copies = []
    for x_ref, land_ref in zip(x_refs, land_refs):
        for fx, fy, fc in PEER_FLIPS:
            px, py, pc = x ^ fx, y ^ fy, c ^ fc
            src = x_ref.at[4 * px + 2 * py + pc] if scatter else x_ref
            copies.append(pltpu.make_async_remote_copy(
                src_ref=src, dst_ref=land_ref.at[me], send_sem=send_sem, recv_sem=recv_sem,
                device_id=(px, py, pc), device_id_type=MESH))
    return copies


def _sc_exchange(xs, *, scatter, collective_id, name):
    na = len(xs)
    land_shapes = [x.shape if scatter else (N_DEV,) + x.shape for x in xs]

    def body(*refs):
        x_refs, land_refs = refs[:na], refs[na:2 * na]
        send_sem, recv_sem, local_sem = refs[2 * na:]
        x, y, c = _mesh_pos()
        me = 4 * x + 2 * y + c
        barrier = pltpu.get_barrier_semaphore()
        for fx, fy, fc in PEER_FLIPS:
            pl.semaphore_signal(barrier, inc=1, device_id=(x ^ fx, y ^ fy, c ^ fc), device_id_type=MESH)
        pl.semaphore_wait(barrier, len(PEER_FLIPS))
        for x_ref, land_ref in zip(x_refs, land_refs):
            own = pltpu.make_async_copy(x_ref.at[me] if scatter else x_ref, land_ref.at[me], local_sem)
            own.start()
            own.wait()
        copies = _peer_copies(x_refs, land_refs, send_sem, recv_sem, scatter)
        for cp in copies:
            cp.start()
        for cp in copies:
            cp.wait()

    return pl.kernel(
        body, name=name, out_type=[_sds(s, x.dtype) for s, x in zip(land_shapes, xs)],
        mesh=plsc.ScalarSubcoreMesh(axis_name="sequencer", num_cores=1),
        scratch_types=[pltpu.SemaphoreType.DMA, pltpu.SemaphoreType.DMA, pltpu.SemaphoreType.DMA],
        compiler_params=pltpu.CompilerParams(collective_id=collective_id),
    )(*xs)


def _sc_allgather(xs, *, collective_id, name):
    na = len(xs)

    def body(*refs):
        x_refs, out_refs = refs[:na], refs[na:2 * na]
        send_sems, recv_sems, local_sems = refs[2 * na:]
        x, y, c = _mesh_pos()
        barrier = pltpu.get_barrier_semaphore()
        for fx, fy, fc in PEER_FLIPS:
            pl.semaphore_signal(barrier, inc=1, device_id=(x ^ fx, y ^ fy, c ^ fc), device_id_type=MESH)
        pl.semaphore_wait(barrier, len(PEER_FLIPS))
        _allgather_body(x_refs, out_refs, send_sems, recv_sems, local_sems,
                        lambda ref, pos: ref.at[4 * pos[0] + 2 * pos[1] + pos[2]])

    return pl.kernel(
        body, name=name, out_type=[_sds((N_DEV,) + x.shape, x.dtype) for x in xs],
        mesh=plsc.ScalarSubcoreMesh(axis_name="sequencer", num_cores=1),
        scratch_types=[pltpu.SemaphoreType.DMA((na, 7)), pltpu.SemaphoreType.DMA((na, 7)),
                       pltpu.SemaphoreType.DMA((na,))],
        compiler_params=pltpu.CompilerParams(collective_id=collective_id),
    )(*xs)


def _allgather_vmem(x, *, name):
    r, c = x.shape

    def body(x_ref, out_ref, send_sems, recv_sems, local_sems):
        _allgather_body([x_ref], [out_ref], send_sems, recv_sems, local_sems,
                        lambda ref, pos: ref.at[pl.ds((4 * pos[0] + 2 * pos[1] + pos[2]) * r, r), :])

    vm = pl.BlockSpec(memory_space=pltpu.VMEM)
    return pl.pallas_call(
        body, name=name, in_specs=[vm], out_specs=vm, out_shape=_sds((N_DEV * r, c), x.dtype),
        scratch_shapes=[pltpu.SemaphoreType.DMA((1, 7)), pltpu.SemaphoreType.DMA((1, 7)),
                        pltpu.SemaphoreType.DMA((1,))],
    )(x)


def _sum_slots(gathered, *, name):
    _, r, c = gathered.shape

    def body(g_ref, o_ref):
        acc = g_ref[0]
        for k in range(1, N_DEV):
            acc = acc + g_ref[k]
        o_ref[...] = acc

    return pl.pallas_call(body, name=name, out_shape=_sds((r, c), gathered.dtype))(gathered)


def _t5_bucket(rel):
    nb = REL_BUCKETS // 2
    ret = jnp.where(rel > 0, nb, 0)
    n = jnp.abs(rel)
    max_exact = nb // 2
    nf = jnp.maximum(n, 1).astype(F32)
    large = max_exact + (jnp.log(nf / max_exact) / math.log(REL_MAX_DIST / max_exact)
                         * (nb - max_exact)).astype(jnp.int32)
    large = jnp.minimum(large, nb - 1)
    return ret + jnp.where(n < max_exact, n, large)


def _band_pattern(block, radius, dil):
    kw = block + 2 * radius
    rel = jnp.arange(kw)[None, :] - radius - jnp.arange(block)[:, None]
    return jnp.where(jnp.abs(rel) <= radius, _t5_bucket(rel * dil), -1).astype(jnp.int32).reshape(1, block * kw)


def _rope_tables():
    lane = np.arange(64)
    seg, j = lane // 32, lane % 32
    inv = ROPE_THETA ** (-jnp.arange(0, 32, 2, dtype=F32) / 32)
    tpos = jnp.arange(SEQ)
    pos = jnp.where(jnp.asarray(seg)[None, :] == 0, (tpos // GRID_W)[:, None], (tpos % GRID_W)[:, None])
    ang = pos.astype(F32) * inv[jnp.asarray(j % 16)][None, :]
    cos = jnp.cos(ang)
    sins = jnp.where(jnp.asarray(j)[None, :] < 16, -jnp.sin(ang), jnp.sin(ang))
    return jnp.tile(cos, (1, 4)), jnp.tile(sins, (1, 4))


A_Q, A_K, A_V = (256, 0), (256, 1), (256, 2)
B_Q, B_K, B_V = (256, 0), (128, 2), (128, 3)
A_HEADS = dict(rad=A_RADIUS, nh=4, nkv=4)
B_HEADS = dict(rad=SWA_RADIUS, nh=4, nkv=2)


def _local_step(x, pe, tgt, rel_bias, wts, matmul_weights, grads_ready):
    t = x.shape[0]
    bl = t // SEQ
    cos, sins = _rope_tables()
    blocks_a = [min(BAND_BLOCK, SEQ // d) for d in DILATIONS]
    pats_a = [_band_pattern(blk, A_RADIUS, d) for blk, d in zip(blocks_a, DILATIONS)]
    pat_b = _band_pattern(BAND_BLOCK, SWA_RADIUS, 1)
    table_t = rel_bias.T
    bias_a = [_bias_lookup(table_t[:4], pt, name=f"bias_a{ci}").reshape(4, blk, blk + 2 * A_RADIUS)
              for ci, (pt, blk) in enumerate(zip(pats_a, blocks_a))]
    bias_b = _bias_lookup(table_t[4:], pat_b, name="bias_b").reshape(4, BAND_BLOCK, BAND_BLOCK + 2 * SWA_RADIUS)
    nat4 = lambda a: a.reshape(bl, 1, SEQ, a.shape[-1])

    saved = []
    for li in range(DEPTH):
        w = dict(wts[li])
        w.update(matmul_weights(li, "in", x))
        hn0, proj = _norm_mm((x,), w["g_mix"], w["w_in"], None, tm=1024, tn=1152, name="mix_in_fwd", out_dtype=BF16)
        qa1, qa4, qa16, qb, qd = _qkprep_fwd(proj, w["qk_gains"], cos, sins, tm=512, name="qkprep_fwd")
        qa = (nat4(qa1), qa4, qa16)
        oa, la = [], []
        for ci in range(3):
            o, l = _band_fwd(qa[ci], A_Q, A_K, A_V, bias_a[ci], None, name=f"band_a{ci}_fwd", **A_HEADS)
            oa.append(o)
            la.append(l)
        oa[0], la[0] = oa[0].reshape(t, 256), la[0].reshape(t, 256)
        ya, lse_a = _combine_a(oa, la, tm=512, name="combine_a")
        yb, lse_b = _band_fwd(nat4(qb), B_Q, B_K, B_V, bias_b, w["sink_t"], name="band_b_fwd", **B_HEADS)
        yb = yb.reshape(t, 256)
        yc = _c_fwd(proj, w["c_g"], w["c_b"], w["c_ws"], w["c_bst"], tm=512, name="c_fwd")
        yd, lse_d = _dense_fwd(qd, tq=256, name="dense_fwd")
        w.update(matmul_weights(li, "rest", yd))
        mixed, x1 = _norm_mm((ya, yb, yc, yd), w["out_gain"], w["w_out"], x, tm=1024, tn=1024, name="mix_out_fwd")
        hn1, h = _norm_mm((x1,), w["g_ffn"], w["w_up"], None, tm=1024, tn=1408, name="ffn_up_fwd", out_dtype=BF16)
        act, cg, cu = _conv_gate_fwd(h, w["conv_w"], w["conv_b"], name="conv_gate_fwd")
        x2 = _mm(act, w["w_down"], "nn", x1, tm=1024, tn=1024, out_dtype=F32, name="ffn_down_fwd")
        hn2, x3, gate, pp = _ple_fwd(x2, w["g_ple"], w["w_gate"], pe, li * (t // 1024), w["w_proj"], tm=1024,
                                     name="ple_fwd")
        saved.append(dict(w=w, x0=x, hn0=hn0, proj=proj, qa=qa, qb=qb, qd=qd, ya=ya, lse_a=lse_a, yb=yb, lse_b=lse_b,
                          yc=yc, yd=yd, lse_d=lse_d, mixed=mixed, x1=x1, hn1=hn1, h=h, cg=cg, cu=cu, act=act, x2=x2, hn2=hn2,
                          gate=gate, pp=pp))
        x = x3

    loss_tile, dx = _loss_head(x, tgt, tm=512, name="loss_head")
    grads = [None] * DEPTH
    dbias_a, dbias_bs = [[], [], []], []
    for li in reversed(range(DEPTH)):
        s = saved[li]
        w = s["w"]
        g = {}
        dz, dpp = _ple_bwd_ew(dx, s["gate"], s["pp"], tm=512, name="ple_bwd_ew")
        g["w_gate"] = _mm(s["hn2"], dz, "tn", None, tm=1024, tn=512, out_dtype=BF16, name="dw_gate")
        g["w_proj"] = _mm(pe, dpp, "tn", None, tm=256, tn=1024, out_dtype=BF16, name="dw_proj", a_rows=(li, t))
        dx2, dx2b, g["g_ple"] = _mm_bt_normbwd((dz,), w["w_gate"], (s["x2"],), w["g_ple"], dx, tm=1024, tn=1024,
                                               name="ple_bwd", emit_bf16=True)
        g["w_down"] = _mm(s["act"], dx2b, "tn", None, tm=1408, tn=512, out_dtype=BF16, name="dw_down")
        dact = _mm(dx2b, w["w_down"], "nt", None, tm=1024, tn=1408, out_dtype=BF16, name="ffn_down_bwd")
        dhg, dhu, g["conv_w"], g["conv_b"] = _conv_gate_bwd(s["h"], s["cg"], s["cu"], dact, w["conv_w"], w["conv_b"],
                                                            name="conv_gate_bwd")
        g["w_up"] = jnp.concatenate(
            [_mm(s["hn1"], dhalf, "tn", None, tm=1024, tn=1408, out_dtype=BF16, name=f"dw_up_{nm}")
             for nm, dhalf in (("gate", dhg), ("up", dhu))], axis=1)
        dx1, dx1b, g["g_ffn"] = _mm_bt_normbwd((dhg, dhu), w["w_up"], (s["x1"],), w["g_ffn"], dx2, tm=1024, tn=1408,
                                               name="ffn_up_bwd", emit_bf16=True)
        g["w_out"] = _mm(s["mixed"], dx1b, "tn", None, tm=1024, tn=512, out_dtype=BF16, name="dw_out")
        grads_ready(li, "mid", g)
        dycat, g["out_gain"] = _mm_bt_normbwd((dx1b,), w["w_out"], (s["ya"], s["yb"], s["yc"], s["yd"]), w["out_gain"],
                                              None, tm=1024, tn=1024, name="mix_out_bwd")
        dy_r, lse_r, dl_a, dl_b, dl_d = _deltas(dycat, s["ya"], s["yb"], s["yd"], s["lse_a"], tm=512, name="deltas")
        dy_a = (nat4(dycat),) + tuple(dy_r)
        lse_a = (nat4(s["lse_a"]),) + tuple(lse_r)
        dl_a = (nat4(dl_a[0]),) + tuple(dl_a[1:])
        da = []
        for ci in range(3):
            dq, dk, dv, dbias = _band_bwd(s["qa"][ci], A_Q, A_K, A_V, bias_a[ci], None, dy_a[ci], 0, lse_a[ci],
                                          dl_a[ci], name=f"band_a{ci}_bwd", **A_HEADS)
            if ci == 0:
                dq, dk, dv = (a.reshape(t, 256) for a in (dq, dk, dv))
            da.append((dq, dk, dv))
            dbias_a[ci].append(dbias.reshape(4, -1))
        dqb, dkb, dvb, dbias_b, dsink = _band_bwd(nat4(s["qb"]), B_Q, B_K, B_V, bias_b, w["sink_t"], nat4(dycat), 1,
                                                  nat4(s["lse_b"]), nat4(dl_b), name="band_b_bwd", **B_HEADS)
        dbias_bs.append(dbias_b.reshape(4, -1))
        g["sink"] = dsink[:, 0, 0]
        dd = _dense_bwd(s["qd"], dycat, s["lse_d"], dl_d, tq=128, name="dense_bwd")
        dcu, dcv, g["c_ws"], dbs, g["c_g"], g["c_b"] = _c_bwd(s["proj"], dycat, w["c_g"], w["c_b"], w["c_ws"],
                                                               w["c_wst"], w["c_bst"], tm=512, name="c_bwd")
        g["c_bs"] = dbs[:, ::64].T
        db = (dqb.reshape(t, 256), dkb.reshape(t, 128), dvb.reshape(t, 128))
        dproj, dgains = _qkprep_bwd(s["proj"], da, db, dd, dcu, dcv, w["qk_gains"], cos, sins, tm=512, name="qkprep_bwd")
        g["qk_gain"] = dgains[:6, :64].reshape(3, 2, HEAD_DIM)
        g["w_in"] = _mm(s["hn0"], dproj, "tn", None, tm=1024, tn=1152, out_dtype=BF16, name="dw_in")
        dx, g["g_mix"] = _mm_bt_normbwd((dproj,), w["w_in"], (s["x0"],), w["g_mix"], dx1, tm=1024, tn=1152,
                                        name="mix_in_bwd")
        grads[li] = g
        grads_ready(li, "end", g)
    d_table_a = sum(_bucket_reduce(dbias_a[ci], pats_a[ci], name=f"bucket_a{ci}") for ci in range(3))
    d_table_b = _bucket_reduce(dbias_bs, pat_b, name="bucket_b")
    d_rel_bias = jnp.concatenate([d_table_a, d_table_b], axis=0).T
    return loss_tile[0, 0], dx, grads, d_rel_bias


WEIGHT_NAMES = ("rel_bias", "ln_mix_g", "w_in", "qk_gain", "sink", "c_norm_g", "c_norm_b", "c_ws", "c_bs", "out_gain",
                "w_out", "ln_ffn_g", "w_up", "conv_w", "conv_b", "w_down", "ln_ple_g", "w_ple_gate", "w_ple_proj")
COL_SHARDED = ("w_in", "w_up", "w_ple_proj")
ROW_SHARDED = ("w_out", "w_down", "w_ple_gate")
SMALL_SHARDED = ("conv_w", "out_gain")
REPLICATED = tuple(n for n in WEIGHT_NAMES if n not in COL_SHARDED + ROW_SHARDED + SMALL_SHARDED)
LOCAL_GRAD_KEY = {"ln_mix_g": "g_mix", "ln_ffn_g": "g_ffn", "ln_ple_g": "g_ple", "c_norm_g": "c_g", "c_norm_b": "c_b",
                  "w_ple_gate": "w_gate", "w_ple_proj": "w_proj"}


def _full_from_gathered(name, gathered):
    _, r, c = gathered.shape
    if name in ROW_SHARDED:
        return gathered.reshape(N_DEV * r, c)
    return jnp.transpose(gathered, (1, 0, 2)).reshape(r, N_DEV * c)


def _slots_from_full(name, full):
    rows, cols = full.shape
    if name in ROW_SHARDED:
        return full.reshape(N_DEV, rows // N_DEV, cols)
    return jnp.transpose(full.reshape(rows, N_DEV, cols // N_DEV), (1, 0, 2))


def _piece_rows(shape):
    return -(-int(np.prod(shape)) // 1024) * 8


def _pack_rows(arrays):
    pieces = []
    for a in arrays:
        n, rows = int(np.prod(a.shape)), _piece_rows(a.shape)
        flat = a.astype(F32).reshape(-1)
        if n != rows * LANES:
            flat = jnp.pad(flat, (0, rows * LANES - n))
        pieces.append(flat.reshape(rows, LANES))
    return jnp.concatenate(pieces, axis=0)


def _unpack_rows(packed, shapes):
    out, off = [], 0
    for shp in shapes:
        n, rows = int(np.prod(shp)), _piece_rows(shp)
        piece = packed[off:off + rows]
        out.append((piece if n == rows * LANES else piece.reshape(-1)[:n]).reshape(shp))
        off += rows
    return out


def kernel(x, p, rel_bias, ln_mix_g, w_in, qk_gain, sink, c_norm_g, c_norm_b, c_ws, c_bs, out_gain, w_out, ln_ffn_g, w_up, conv_w, conv_b, w_down, ln_ple_g, w_ple_gate, w_ple_proj, loss_target, m_rel_bias, m_ln_mix_g, m_w_in, m_qk_gain, m_sink, m_c_norm_g, m_c_norm_b, m_c_ws, m_c_bs, m_out_gain, m_w_out, m_ln_ffn_g, m_w_up, m_conv_w, m_conv_b, m_w_down, m_ln_ple_g, m_w_ple_gate, m_w_ple_proj, v_rel_bias, v_ln_mix_g, v_w_in, v_qk_gain, v_sink, v_c_norm_g, v_c_norm_b, v_c_ws, v_c_bs, v_out_gain, v_w_out, v_ln_ffn_g, v_w_up, v_conv_w, v_conv_b, v_w_down, v_ln_ple_g, v_w_ple_gate, v_w_ple_proj):
    env = dict(locals())
    wt = {n: env[n] for n in WEIGHT_NAMES}
    mom_m = {n: env["m_" + n] for n in WEIGHT_NAMES}
    mom_v = {n: env["v_" + n] for n in WEIGHT_NAMES}
    bl = x.shape[0]
    t = bl * SEQ
    me = 4 * lax.axis_index("x") + 2 * lax.axis_index("y") + lax.axis_index("c")

    big = COL_SHARDED + ROW_SHARDED
    full = {}
    small_shapes = [wt[n].shape for n in SMALL_SHARDED]
    small = _allgather_vmem(_pack_rows([wt[n] for n in SMALL_SHARDED]), name="gather_small")
    small = small.reshape(N_DEV, -1)
    off = 0
    for n, shp in zip(SMALL_SHARDED, small_shapes):
        cnt = int(np.prod(shp))
        g = small[:, off:off + cnt].reshape((N_DEV,) + tuple(shp))
        full[n] = jnp.transpose(g, (1, 2, 0, 3)).reshape(shp[0], shp[1], N_DEV * shp[2])
        off += _piece_rows(shp) * LANES

    def head_gain(li, a, b, reps):
        g = jnp.tile(qk_gain[li, a, b], reps)
        return jnp.pad(g, (0, 256 - g.shape[0]))

    wts = []
    for li in range(DEPTH):
        rows = [head_gain(li, 0, 0, 4), head_gain(li, 0, 1, 4), head_gain(li, 1, 0, 4), head_gain(li, 1, 1, 2),
                head_gain(li, 2, 0, 4), head_gain(li, 2, 1, 2), jnp.zeros((256,), F32), jnp.zeros((256,), F32)]
        wts.append(dict(
            g_mix=ln_mix_g[li].reshape(1, -1), qk_gains=jnp.stack(rows),
            sink_t=jnp.broadcast_to(sink[li][:, None, None], (4, 8, 128)),
            c_g=c_norm_g[li].reshape(1, -1), c_b=c_norm_b[li].reshape(1, -1), c_ws=c_ws[li].astype(BF16),
            c_wst=jnp.transpose(c_ws[li], (0, 2, 1)).astype(BF16), c_bst=jnp.repeat(c_bs[li].T, 64, axis=1),
            out_gain=full["out_gain"][li].reshape(1, -1), g_ffn=ln_ffn_g[li].reshape(1, -1),
            conv_w=full["conv_w"][li], conv_b=conv_b[li].reshape(1, -1), g_ple=ln_ple_g[li].reshape(1, -1)))

    local_key = {"w_ple_gate": "w_gate", "w_ple_proj": "w_proj"}

    gather_names = {"in": ("w_in",), "rest": tuple(n for n in big if n != "w_in")}
    gathered = {}
    for cid, (li, names) in enumerate(((0, gather_names["in"]), (0, gather_names["rest"]), (1, big))):
        lands = _sc_allgather([wt[n][li].astype(BF16) for n in names], collective_id=cid,
                              name=f"gather_{li}_{len(names)}")
        gathered.setdefault(li, {}).update(zip(names, lands))

    def matmul_weights(li, part, after):
        out = {}
        for n in gather_names[part]:
            g, _ = lax.optimization_barrier((gathered[li][n], after))
            out[local_key.get(n, n)] = _full_from_gathered(n, g)
        return out

    mid_names = ("w_ple_gate", "w_ple_proj", "w_down", "w_up", "w_out")
    end_names = ("w_in",)
    landed = {}

    def start_exchange(li, names, g, tag, cid):
        slots = [_slots_from_full(n, g[local_key.get(n, n)]) for n in names]
        lands = _sc_exchange(slots, scatter=True, collective_id=cid, name=f"grads_{li}_{tag}")
        landed.update({(n, li): land for n, land in zip(names, lands)})

    def grads_ready(li, stage, g):
        if li == 0:
            start_exchange(li, mid_names if stage == "mid" else end_names, g, stage, 5 if stage == "mid" else 6)
        elif stage == "end":
            start_exchange(li, mid_names + end_names, g, stage, 4)

    loss_part, dx, grads, d_rel_bias = _local_step(
        x.reshape(t, D_MODEL), p.reshape(DEPTH * t, PLE_DIM), loss_target.reshape(t, D_MODEL), rel_bias, wts,
        matmul_weights, grads_ready)
    loss = lax.psum(loss_part, ("x", "y", "c"))

    def local_grad(n):
        if n == "rel_bias":
            return d_rel_bias
        key = LOCAL_GRAD_KEY.get(n, n)
        return jnp.stack([grads[li][key].reshape(wt[n].shape[1:]) if n in REPLICATED else grads[li][key]
                          for li in range(DEPTH)])

    small_names = REPLICATED + SMALL_SHARDED
    small_full_shapes = [wt[n].shape if n in REPLICATED else full[n].shape for n in small_names]
    small_parts = _allgather_vmem(_pack_rows([local_grad(n) for n in small_names]), name="allgather_small_grads")
    small_parts = small_parts.reshape(N_DEV, -1, LANES)

    out_g, out_d, out_m, out_v = {}, {}, {}, {}
    for n in big:
        shp = wt[n].shape
        two_d = lambda a: a.reshape(-1, shp[-1])
        res = _adamw_reduce([landed[n, li] for li in range(DEPTH)], two_d(wt[n]), two_d(mom_m[n]), two_d(mom_v[n]),
                            tr=32 if n == "w_down" else 128, name="adamw_" + n)
        out_g[n], out_d[n], out_m[n], out_v[n] = [r.reshape(shp) for r in res]

    reduced = _sum_slots(small_parts, name="sum_small_grads")
    reduced = dict(zip(small_names, _unpack_rows(reduced, small_full_shapes)))
    rep_shapes = [wt[n].shape for n in REPLICATED]
    upd = _adamw_plain(_pack_rows([reduced[n] for n in REPLICATED]), _pack_rows([wt[n] for n in REPLICATED]),
                       _pack_rows([mom_m[n] for n in REPLICATED]), _pack_rows([mom_v[n] for n in REPLICATED]),
                       name="adamw_replicated")
    for dst, packed in zip((out_d, out_m, out_v), upd):
        dst.update(zip(REPLICATED, _unpack_rows(packed, rep_shapes)))
    for n in REPLICATED:
        out_g[n] = reduced[n]
    for n in SMALL_SHARDED:
        shp = wt[n].shape
        g = reduced[n].reshape(shp[0], shp[1], N_DEV, shp[2])
        g = lax.dynamic_index_in_dim(g, me, axis=2, keepdims=False)
        two_d = lambda a: a.reshape(-1, shp[-1])
        res = _adamw_plain(two_d(g), two_d(wt[n]), two_d(mom_m[n]), two_d(mom_v[n]), name="adamw_" + n)
        out_g[n] = g
        out_d[n], out_m[n], out_v[n] = [r.reshape(shp) for r in res]

    return (loss, dx.reshape(bl, SEQ, D_MODEL), *[out_g[n] for n in WEIGHT_NAMES], *[out_d[n] for n in WEIGHT_NAMES],
            *[out_m[n] for n in WEIGHT_NAMES], *[out_v[n] for n in WEIGHT_NAMES])
```

```python
import math

import jax
import jax.numpy as jnp
import numpy as np
from jax import lax
from jax.experimental import pallas as pl
from jax.experimental.pallas import tpu as pltpu
from jax.experimental.pallas import tpu_sc as plsc

F32 = jnp.float32
BF16 = jnp.bfloat16

N_DEV = 8
D_MODEL = 1024
SEQ = 2048
DEPTH = 2
HEAD_DIM = 64
IN_WIDTH = 2304
D_FF = 2816
PLE_DIM = 256
C_CHUNK = 128
C_GROUPS = 4
DILATED_CFGS = ((128, 1), (512, 4), (2048, 16))
DILATIONS = tuple(d for _, d in DILATED_CFGS)
A_RADIUS = 64
SWA_RADIUS = 128
BAND_BLOCK = 256
GRID_W = 64
ROPE_THETA = 10000.0
REL_BUCKETS = 32
REL_MAX_DIST = 1024
EPS = 1e-6
NEG_INF = -1e30
ATTN_SCALE = HEAD_DIM ** -0.5
LANES = 128

ADAM_LR = 0.001
ADAM_B1 = 0.9
ADAM_B2 = 0.999
ADAM_EPS = 1e-08
ADAM_WD = 0.01
ADAM_STEP = 10

MESH = pl.DeviceIdType.MESH
NT = (((1,), (1,)), ((), ()))
TN = (((0,), (0,)), ((), ()))
ARB = "arbitrary"
PAR = "parallel"


def _cparams(*sem):
    return pltpu.CompilerParams(dimension_semantics=tuple(sem))


def _sds(shape, dtype):
    return jax.ShapeDtypeStruct(tuple(shape), dtype)


def _group_sum_matrix(n, same_group):
    r = lax.broadcasted_iota(jnp.int32, (n, n), 0)
    c = lax.broadcasted_iota(jnp.int32, (n, n), 1)
    if same_group:
        return ((r >> 6) == (c >> 6)).astype(F32)
    return ((r & 63) == (c & 63)).astype(F32)


def _seg_sum(x, e):
    eb = e.astype(BF16)
    hi = x.astype(BF16)
    lo = (x - hi.astype(F32)).astype(BF16)
    return jnp.dot(hi, eb, preferred_element_type=F32) + jnp.dot(lo, eb, preferred_element_type=F32)


def _gelu(x):
    c = math.sqrt(2.0 / math.pi)
    return 0.5 * x * (1.0 + jnp.tanh(c * (x + 0.044715 * (x * x * x))))


def _gelu_grad(x):
    c = math.sqrt(2.0 / math.pi)
    t = jnp.tanh(c * (x + 0.044715 * (x * x * x)))
    return 0.5 * (1.0 + t) + 0.5 * x * (1.0 - t * t) * c * (1.0 + 3.0 * 0.044715 * (x * x))


def _sigmoid(x):
    return 1.0 / (1.0 + jnp.exp(-x))


def _scatter_cols(scratch, first, val):
    for c in range(val.shape[1] // LANES):
        scratch[first + c] = val[:, c * LANES:(c + 1) * LANES]


def _gather_cols(scratch, first, ncol):
    return jnp.concatenate([scratch[first + c] for c in range(ncol)], axis=1)


def _read_residue(scratch, first, ncol, r, d):
    n = scratch.shape[1] // d
    return jnp.concatenate([scratch.at[first + c][pl.ds(r, n, stride=d), :] for c in range(ncol)], axis=1)


def _write_residue(scratch, first, r, d, val):
    n = scratch.shape[1] // d
    for c in range(val.shape[1] // LANES):
        scratch.at[first + c][pl.ds(r, n, stride=d), :] = val[:, c * LANES:(c + 1) * LANES]


def _norm_mm(xs, gain, w, res, *, tm, tn, name, out_dtype=F32):
    t = xs[0].shape[0]
    k = sum(x.shape[1] for x in xs)
    n = w.shape[1]
    ng = len(xs)
    has_res = res is not None

    def body(*refs):
        x_refs = refs[:ng]
        g_ref, w_ref = refs[ng], refs[ng + 1]
        res_ref = refs[ng + 2] if has_res else None
        hn_ref, o_ref, hn_s = refs[ng + 2 + has_res:]

        @pl.when(pl.program_id(1) == 0)
        def _():
            off = 0
            for xr in x_refs:
                x = xr[...]
                wd = x.shape[1]
                r = lax.rsqrt(jnp.mean(x * x, axis=-1, keepdims=True) + EPS)
                hn_s[:, off:off + wd] = (x * r * g_ref[:, off:off + wd]).astype(BF16)
                off += wd
            hn_ref[...] = hn_s[...]

        acc = jnp.dot(hn_s[...], w_ref[...], preferred_element_type=F32)
        if has_res:
            acc = acc + res_ref[...]
        o_ref[...] = acc.astype(out_dtype)

    in_specs = [pl.BlockSpec((tm, x.shape[1]), lambda i, j: (i, 0)) for x in xs]
    in_specs += [pl.BlockSpec((1, k), lambda i, j: (0, 0)), pl.BlockSpec((k, tn), lambda i, j: (0, j))]
    args = list(xs) + [gain, w]
    if has_res:
        in_specs.append(pl.BlockSpec((tm, tn), lambda i, j: (i, j)))
        args.append(res)
    return pl.pallas_call(
        body, name=name, grid=(t // tm, n // tn), in_specs=in_specs,
        out_specs=[pl.BlockSpec((tm, k), lambda i, j: (i, 0)), pl.BlockSpec((tm, tn), lambda i, j: (i, j))],
        out_shape=[_sds((t, k), BF16), _sds((t, n), out_dtype)],
        scratch_shapes=[pltpu.VMEM((tm, k), BF16)],
        compiler_params=_cparams(PAR, ARB),
    )(*args)


def _mm(a, b, mode, res, *, tm, tn, out_dtype, name, a_rows=None):
    if mode == "tn":
        kk, m = a.shape
        blk_a = 0
        if a_rows is not None:
            blk_a, kk = a_rows
        a_spec = pl.BlockSpec((kk, tm), lambda i, j: (blk_a, i))
    else:
        m, kk = a.shape
        a_spec = pl.BlockSpec((tm, kk), lambda i, j: (i, 0))
    if mode == "nt":
        n = b.shape[0]
        b_spec = pl.BlockSpec((tn, kk), lambda i, j: (j, 0))
    else:
        n = b.shape[1]
        b_spec = pl.BlockSpec((kk, tn), lambda i, j: (0, j))
    has_res = res is not None

    def body(*refs):
        a_ref, b_ref = refs[0], refs[1]
        o_ref = refs[-1]
        av = a_ref[...].astype(BF16)
        bv = b_ref[...].astype(BF16)
        if mode == "nn":
            acc = jnp.dot(av, bv, preferred_element_type=F32)
        elif mode == "nt":
            acc = lax.dot_general(av, bv, NT, preferred_element_type=F32)
        else:
            acc = lax.dot_general(av, bv, TN, preferred_element_type=F32)
        if has_res:
            acc = acc + refs[2][...]
        o_ref[...] = acc.astype(out_dtype)

    in_specs = [a_spec, b_spec]
    args = [a, b]
    if has_res:
        in_specs.append(pl.BlockSpec((tm, tn), lambda i, j: (i, j)))
        args.append(res)
    return pl.pallas_call(
        body, name=name, grid=(m // tm, n // tn), in_specs=in_specs,
        out_specs=pl.BlockSpec((tm, tn), lambda i, j: (i, j)),
        out_shape=_sds((m, n), out_dtype),
        compiler_params=_cparams(PAR, PAR),
    )(*args)


def _mm_bt_normbwd(dys, w, xs, gain, dres, *, tm, tn, name, emit_bf16=False):
    t, wd_each = dys[0].shape
    nd = len(dys)
    per = wd_each // tn
    nj = nd * per
    k = w.shape[0]
    ng = len(xs)
    has_res = dres is not None

    def body(*refs):
        dy_refs = refs[:nd]
        w_ref = refs[nd]
        x_refs = refs[nd + 1:nd + 1 + ng]
        g_ref = refs[nd + 1 + ng]
        dres_ref = refs[nd + 2 + ng] if has_res else None
        outs = refs[nd + 2 + ng + has_res:]
        dx_ref = outs[0]
        dxb_ref = outs[1] if emit_bf16 else None
        dg_ref, acc = outs[1 + emit_bf16:]
        i, j = pl.program_id(0), pl.program_id(1)

        @pl.when(j == 0)
        def _():
            acc[...] = jnp.zeros_like(acc)

        for d, dy_ref in enumerate(dy_refs):
            @pl.when((j >= d * per) & (j < (d + 1) * per))
            def _(dy_ref=dy_ref):
                acc[...] += lax.dot_general(dy_ref[...].astype(BF16), w_ref[...], NT, preferred_element_type=F32)

        @pl.when(j == nj - 1)
        def _():
            @pl.when(i == 0)
            def _():
                dg_ref[...] = jnp.zeros_like(dg_ref)

            off = 0
            for xr in x_refs:
                x = xr[...]
                wd = x.shape[1]
                g = g_ref[:, off:off + wd]
                dyn = acc[:, off:off + wd]
                r = lax.rsqrt(jnp.mean(x * x, axis=-1, keepdims=True) + EPS)
                gdy = dyn * g
                dx = r * gdy - x * (r * r * r * jnp.mean(gdy * x, axis=-1, keepdims=True))
                if has_res:
                    dx = dx + dres_ref[:, off:off + wd]
                dx_ref[:, off:off + wd] = dx
                if emit_bf16:
                    dxb_ref[:, off:off + wd] = dx.astype(BF16)
                dg_ref[:, off:off + wd] += jnp.sum(dyn * x * r, axis=0, keepdims=True)
                off += wd

    def dy_map(d):
        return lambda i, j: (i, jnp.clip(j - d * per, 0, per - 1))

    in_specs = [pl.BlockSpec((tm, tn), dy_map(d)) for d in range(nd)]
    in_specs.append(pl.BlockSpec((k, tn), lambda i, j: (0, j)))
    in_specs += [pl.BlockSpec((tm, x.shape[1]), lambda i, j: (i, 0)) for x in xs]
    in_specs.append(pl.BlockSpec((1, k), lambda i, j: (0, 0)))
    args = list(dys) + [w] + list(xs) + [gain]
    if has_res:
        in_specs.append(pl.BlockSpec((tm, k), lambda i, j: (i, 0)))
        args.append(dres)
    row = pl.BlockSpec((tm, k), lambda i, j: (i, 0))
    out_specs = [row] + ([row] if emit_bf16 else []) + [pl.BlockSpec((1, k), lambda i, j: (0, 0))]
    out_shape = [_sds((t, k), F32)] + ([_sds((t, k), BF16)] if emit_bf16 else []) + [_sds((1, k), F32)]
    return pl.pallas_call(
        body, name=name, grid=(t // tm, nj), in_specs=in_specs, out_specs=out_specs, out_shape=out_shape,
        scratch_shapes=[pltpu.VMEM((tm, k), F32)],
        compiler_params=_cparams(ARB, ARB),
    )(*args)


def _rope_partner(y):
    n = y.shape[1]
    lane = lax.broadcasted_iota(jnp.int32, y.shape, 1)
    return jnp.where((lane & 31) < 16, pltpu.roll(y, n - 16, 1), pltpu.roll(y, 16, 1))


def _residue_specs(tm, width, nt):
    specs = [pl.BlockSpec((tm, width), lambda b, i: (b * nt + i, 0))]
    for d in DILATIONS[1:]:
        specs.append(pl.BlockSpec((None, d, tm // d, width), lambda b, i: (b, 0, i, 0)))
    return specs


def _residue_shapes(bl, width, dtype):
    return [_sds((bl * SEQ, width), dtype)] + [_sds((bl, d, SEQ // d, width), dtype) for d in DILATIONS[1:]]


def _qkprep_fwd(proj, gains, cos, sins, *, tm, name):
    t = proj.shape[0]
    bl = t // SEQ
    nt = SEQ // tm

    def body(p_ref, g_ref, c_ref, s_ref, qa1_ref, qa4_ref, qa16_ref, qb_ref, qd_ref, scr):
        e = _group_sum_matrix(256, True)

        def hn(x, row):
            x = x.astype(F32)
            wd = x.shape[1]
            ms = _seg_sum(x * x, e[:wd, :wd]) * (1.0 / HEAD_DIM)
            return x * lax.rsqrt(ms + EPS) * g_ref[row:row + 1, :wd]

        qa = jnp.concatenate([hn(p_ref[:, 0:256], 0) * ATTN_SCALE, hn(p_ref[:, 256:512], 1),
                              p_ref[:, 512:768].astype(F32)], axis=1)
        qa1_ref[...] = qa.astype(BF16)
        _scatter_cols(scr, 0, qa)
        for d, ref in ((4, qa4_ref), (16, qa16_ref)):
            for r in range(d):
                ref[r] = _read_residue(scr, 0, 6, r, d).astype(BF16)
        qb_ref[:, 0:256] = (hn(p_ref[:, 768:1024], 2) * ATTN_SCALE).astype(BF16)
        qb_ref[:, 256:384] = hn(p_ref[:, 1024:1152], 3).astype(BF16)
        qb_ref[:, 384:512] = p_ref[:, 1152:1280].astype(BF16)
        yq = hn(p_ref[:, 1792:2048], 4)
        yq = yq * c_ref[...] + _rope_partner(yq) * s_ref[...]
        qd_ref[:, 0:256] = (yq * ATTN_SCALE).astype(BF16)
        yk = hn(p_ref[:, 2048:2176], 5)
        yk = yk * c_ref[:, 0:128] + _rope_partner(yk) * s_ref[:, 0:128]
        qd_ref[:, 256:384] = yk.astype(BF16)
        qd_ref[:, 384:512] = p_ref[:, 2176:2304].astype(BF16)

    row = lambda width: pl.BlockSpec((tm, width), lambda b, i: (b * nt + i, 0))
    tab = pl.BlockSpec((tm, 256), lambda b, i: (i, 0))
    return pl.pallas_call(
        body, name=name, grid=(bl, nt),
        in_specs=[row(IN_WIDTH), pl.BlockSpec((8, 256), lambda b, i: (0, 0)), tab, tab],
        out_specs=_residue_specs(tm, 768, nt) + [row(512), row(512)],
        out_shape=_residue_shapes(bl, 768, BF16) + [_sds((t, 512), BF16), _sds((t, 512), BF16)],
        scratch_shapes=[pltpu.VMEM((6, tm, LANES), F32)],
        compiler_params=_cparams(PAR, PAR),
    )(proj, gains, cos, sins)


def _qkprep_bwd(proj, da, db, dd, dcu, dcv, gains, cos, sins, *, tm, name):
    t = proj.shape[0]
    bl = t // SEQ
    nt = SEQ // tm
    flat = [a for cfg in da for a in cfg] + list(db) + list(dd) + [dcu, dcv]

    def body(*refs):
        p_ref, g_ref, c_ref, s_ref = refs[:4]
        d_refs = refs[4:4 + len(flat)]
        dp_ref, dg_ref, scr = refs[4 + len(flat):]
        a_refs = d_refs[:9]
        dqb_ref, dkb_ref, dvb_ref, dqd_ref, dkd_ref, dvd_ref, dcu_ref, dcv_ref = d_refs[9:]
        e = _group_sum_matrix(256, True)
        first = (pl.program_id(0) == 0) & (pl.program_id(1) == 0)
        last = (pl.program_id(0) == bl - 1) & (pl.program_id(1) == nt - 1)

        @pl.when(first)
        def _():
            dg_ref[...] = jnp.zeros_like(dg_ref)

        def hn_bwd(x, dy, row):
            x, dy = x.astype(F32), dy.astype(F32)
            wd = x.shape[1]
            ee = e[:wd, :wd]
            g = g_ref[row:row + 1, :wd]
            r = lax.rsqrt(_seg_sum(x * x, ee) * (1.0 / HEAD_DIM) + EPS)
            gdy = dy * g
            dx = r * gdy - x * (r * r * r * (_seg_sum(gdy * x, ee) * (1.0 / HEAD_DIM)))
            dg_ref[row:row + 1, :wd] += jnp.sum(dy * x * r, axis=0, keepdims=True)
            return dx

        def rope_bwd(dy, wd):
            dy = dy.astype(F32)
            return dy * c_ref[:, :wd] + _rope_partner(dy * s_ref[:, :wd])

        dqkv = jnp.concatenate([a_refs[m][...].astype(F32) for m in range(3)], axis=1)
        for ci, d in ((1, 4), (2, 16)):
            for r in range(d):
                part = jnp.concatenate([a_refs[3 * ci + m][r].astype(F32) for m in range(3)], axis=1)
                _write_residue(scr, 0, r, d, part)
            dqkv = dqkv + _gather_cols(scr, 0, 6)
        dp_ref[:, 0:256] = hn_bwd(p_ref[:, 0:256], dqkv[:, 0:256] * ATTN_SCALE, 0).astype(BF16)
        dp_ref[:, 256:512] = hn_bwd(p_ref[:, 256:512], dqkv[:, 256:512], 1).astype(BF16)
        dp_ref[:, 512:768] = dqkv[:, 512:768].astype(BF16)
        dp_ref[:, 768:1024] = hn_bwd(p_ref[:, 768:1024], dqb_ref[...] * ATTN_SCALE, 2).astype(BF16)
        dp_ref[:, 1024:1152] = hn_bwd(p_ref[:, 1024:1152], dkb_ref[...], 3).astype(BF16)
        dp_ref[:, 1152:1280] = dvb_ref[...].astype(BF16)
        dp_ref[:, 1280:1536] = dcu_ref[...].astype(BF16)
        dp_ref[:, 1536:1792] = dcv_ref[...].astype(BF16)
        dp_ref[:, 1792:2048] = hn_bwd(p_ref[:, 1792:2048], rope_bwd(dqd_ref[...] * ATTN_SCALE, 256), 4).astype(BF16)
        dp_ref[:, 2048:2176] = hn_bwd(p_ref[:, 2048:2176], rope_bwd(dkd_ref[...], 128), 5).astype(BF16)
        dp_ref[:, 2176:2304] = dvd_ref[...].astype(BF16)

        @pl.when(last)
        def _():
            dg_ref[...] = _seg_sum(dg_ref[...], _group_sum_matrix(256, False))

    row = lambda width: pl.BlockSpec((tm, width), lambda b, i: (b * nt + i, 0))
    tab = pl.BlockSpec((tm, 256), lambda b, i: (i, 0))
    in_specs = [row(IN_WIDTH), pl.BlockSpec((8, 256), lambda b, i: (0, 0)), tab, tab]
    res_specs = _residue_specs(tm, 256, nt)
    in_specs += [res_specs[ci] for ci in range(3) for _ in range(3)]
    in_specs += [row(a.shape[1]) for a in flat[9:]]
    return pl.pallas_call(
        body, name=name, grid=(bl, nt), in_specs=in_specs,
        out_specs=[row(IN_WIDTH), pl.BlockSpec((8, 256), lambda b, i: (0, 0))],
        out_shape=[_sds((t, IN_WIDTH), BF16), _sds((8, 256), F32)],
        scratch_shapes=[pltpu.VMEM((6, tm, LANES), F32)],
        compiler_params=_cparams(ARB, ARB),
    )(proj, gains, cos, sins, *flat)


BAND_ROWS_PER_STEP = 512


def _residues_per_step(dil, seq_len):
    return min(dil, max(1, BAND_ROWS_PER_STEP // seq_len))


def _band_spec(seq_len, spec, rb):
    width, idx = spec
    return pl.BlockSpec((None, rb, seq_len, width), lambda b, r: (b, r, 0, idx))


def _fill_padded(dst, src_ref, rad, seq_len):
    z = jnp.zeros((rad, dst.shape[1]), dst.dtype)
    dst[0:rad, :] = z
    dst[rad + seq_len:rad + seq_len + rad, :] = z
    dst[rad:rad + seq_len, :] = src_ref[...]


def _band_fwd(src, qs, ks, vs, bias, sink, *, rad, nh, nkv, name):
    bl, dil, sl, _ = src.shape
    blk = bias.shape[1]
    kw = blk + 2 * rad
    nb = sl // blk
    rep = nh // nkv
    has_sink = sink is not None
    rb = _residues_per_step(dil, sl)

    def body(*refs):
        q_all, k_all, v_all, b_ref = refs[:4]
        s_ref = refs[4] if has_sink else None
        o_all, l_all, kp, vp = refs[4 + has_sink:]
        for ri in range(rb):
            one_sequence(q_all.at[ri], k_all.at[ri], v_all.at[ri], b_ref, s_ref, o_all.at[ri], l_all.at[ri], kp, vp)

    def one_sequence(q_ref, k_ref, v_ref, b_ref, s_ref, o_ref, l_ref, kp, vp):
        _fill_padded(kp, k_ref, rad, sl)
        _fill_padded(vp, v_ref, rad, sl)

        def blk_body(i, carry):
            r0 = pl.multiple_of(i * blk, blk)
            qb = q_ref[pl.ds(r0, blk), :]
            kwin = kp[pl.ds(r0, kw), :]
            vwin = vp[pl.ds(r0, kw), :]
            col = r0 - rad + lax.broadcasted_iota(jnp.int32, (blk, kw), 1)
            neg = jnp.where((col >= 0) & (col < sl), 0.0, NEG_INF).astype(F32)
            for h in range(nh):
                g = h // rep
                hs = slice(h * HEAD_DIM, (h + 1) * HEAD_DIM)
                gs = slice(g * HEAD_DIM, (g + 1) * HEAD_DIM)
                s = lax.dot_general(qb[:, hs], kwin[:, gs], NT, preferred_element_type=F32)
                s = s + b_ref[h] + neg
                m = jnp.max(s, axis=1, keepdims=True)
                if has_sink:
                    sk = s_ref[h][0:1, 0:1]
                    m = jnp.maximum(m, sk)
                p = jnp.exp(s - m)
                den = jnp.sum(p, axis=1, keepdims=True)
                if has_sink:
                    den = den + jnp.exp(sk - m)
                o = jnp.dot(p.astype(BF16), vwin[:, gs], preferred_element_type=F32) / den
                o_ref[pl.ds(r0, blk), hs] = o
                l_ref[pl.ds(r0, blk), hs] = jnp.broadcast_to(m + jnp.log(den), (blk, HEAD_DIM))
            return carry

        lax.fori_loop(0, nb, blk_body, 0)

    in_specs = [_band_spec(sl, qs, rb), _band_spec(sl, ks, rb), _band_spec(sl, vs, rb),
                pl.BlockSpec((nh, blk, kw), lambda b, r: (0, 0, 0))]
    args = [src] * 3 + [bias]
    if has_sink:
        in_specs.append(pl.BlockSpec((nh, 8, 128), lambda b, r: (0, 0, 0)))
        args.append(sink)
    return pl.pallas_call(
        body, name=name, grid=(bl, dil // rb), in_specs=in_specs,
        out_specs=[_band_spec(sl, (256, 0), rb)] * 2,
        out_shape=[_sds((bl, dil, sl, 256), F32)] * 2,
        scratch_shapes=[pltpu.VMEM((sl + 2 * rad, ks[0]), BF16), pltpu.VMEM((sl + 2 * rad, vs[0]), BF16)],
        compiler_params=_cparams(PAR, PAR),
    )(*args)


def _band_bwd(src, qs, ks, vs, bias, sink, dy, dcol, lse, delta, *, rad, nh, nkv, name):
    bl, dil, sl, _ = src.shape
    blk = bias.shape[1]
    kw = blk + 2 * rad
    nb = sl // blk
    rep = nh // nkv
    has_sink = sink is not None
    rb = _residues_per_step(dil, sl)
    wk, wv = ks[0], vs[0]

    def body(*refs):
        q_all, k_all, v_all, b_ref = refs[:4]
        s_ref = refs[4] if has_sink else None
        do_all, l_all, dl_all = refs[4 + has_sink:7 + has_sink]
        outs = refs[7 + has_sink:]
        dsk_ref = None
        if has_sink:
            dq_all, dk_all, dv_all, db_ref, dsk_ref, kp, vp, dka, dva = outs
        else:
            dq_all, dk_all, dv_all, db_ref, kp, vp, dka, dva = outs

        @pl.when((pl.program_id(0) == 0) & (pl.program_id(1) == 0))
        def _():
            db_ref[...] = jnp.zeros_like(db_ref)
            if has_sink:
                dsk_ref[...] = jnp.zeros_like(dsk_ref)

        for ri in range(rb):
            one_sequence(q_all.at[ri], k_all.at[ri], v_all.at[ri], b_ref, s_ref, do_all.at[ri], l_all.at[ri],
                         dl_all.at[ri], dq_all.at[ri], dk_all.at[ri], dv_all.at[ri], db_ref, dsk_ref, kp, vp, dka, dva)

    def one_sequence(q_ref, k_ref, v_ref, b_ref, s_ref, do_ref, l_ref, dl_ref, dq_ref, dk_ref, dv_ref, db_ref, dsk_ref,
                     kp, vp, dka, dva):
        _fill_padded(kp, k_ref, rad, sl)
        _fill_padded(vp, v_ref, rad, sl)
        dka[...] = jnp.zeros_like(dka)
        dva[...] = jnp.zeros_like(dva)

        def blk_body(i, carry):
            r0 = pl.multiple_of(i * blk, blk)
            qb = q_ref[pl.ds(r0, blk), :]
            kwin = kp[pl.ds(r0, kw), :]
            vwin = vp[pl.ds(r0, kw), :]
            dob = do_ref[pl.ds(r0, blk), :].astype(BF16)
            lb = l_ref[pl.ds(r0, blk), :]
            dlb = dl_ref[pl.ds(r0, blk), :]
            col = r0 - rad + lax.broadcasted_iota(jnp.int32, (blk, kw), 1)
            neg = jnp.where((col >= 0) & (col < sl), 0.0, NEG_INF).astype(F32)
            for h in range(nh):
                g = h // rep
                hs = slice(h * HEAD_DIM, (h + 1) * HEAD_DIM)
                gs = slice(g * HEAD_DIM, (g + 1) * HEAD_DIM)
                qh, kh, vh, doh = qb[:, hs], kwin[:, gs], vwin[:, gs], dob[:, hs]
                lh = lb[:, h * HEAD_DIM:h * HEAD_DIM + 1]
                dlh = dlb[:, h * HEAD_DIM:h * HEAD_DIM + 1]
                s = lax.dot_general(qh, kh, NT, preferred_element_type=F32) + b_ref[h] + neg
                p = jnp.exp(s - lh)
                dp = lax.dot_general(doh, vh, NT, preferred_element_type=F32)
                ds = p * (dp - dlh)
                dsb = ds.astype(BF16)
                dq_ref[pl.ds(r0, blk), hs] = jnp.dot(dsb, kh, preferred_element_type=F32).astype(BF16)
                dka[pl.ds(r0, kw), gs] += lax.dot_general(dsb, qh, TN, preferred_element_type=F32)
                dva[pl.ds(r0, kw), gs] += lax.dot_general(p.astype(BF16), doh, TN, preferred_element_type=F32)
                db_ref[h] += ds
                if has_sink:
                    ps = jnp.exp(s_ref[h][0:1, 0:1] - lh)
                    dsk_ref[h] += jnp.broadcast_to(-jnp.sum(ps * dlh, axis=0, keepdims=True), (8, 128))
            return carry

        lax.fori_loop(0, nb, blk_body, 0)
        dk_ref[...] = dka[rad:rad + sl, :].astype(BF16)
        dv_ref[...] = dva[rad:rad + sl, :].astype(BF16)

    const3 = lambda b, r: (0, 0, 0)
    in_specs = [_band_spec(sl, qs, rb), _band_spec(sl, ks, rb), _band_spec(sl, vs, rb),
                pl.BlockSpec((nh, blk, kw), const3)]
    args = [src] * 3 + [bias]
    if has_sink:
        in_specs.append(pl.BlockSpec((nh, 8, 128), const3))
        args.append(sink)
    row = _band_spec(sl, (256, 0), rb)
    in_specs += [_band_spec(sl, (256, dcol), rb), row, row]
    args += [dy, lse, delta]
    out_specs = [row, _band_spec(sl, (wk, 0), rb), _band_spec(sl, (wv, 0), rb), pl.BlockSpec((nh, blk, kw), const3)]
    out_shape = [_sds((bl, dil, sl, 256), BF16), _sds((bl, dil, sl, wk), BF16), _sds((bl, dil, sl, wv), BF16),
                 _sds((nh, blk, kw), F32)]
    if has_sink:
        out_specs.append(pl.BlockSpec((nh, 8, 128), const3))
        out_shape.append(_sds((nh, 8, 128), F32))
    return pl.pallas_call(
        body, name=name, grid=(bl, dil // rb), in_specs=in_specs, out_specs=out_specs, out_shape=out_shape,
        scratch_shapes=[pltpu.VMEM((sl + 2 * rad, wk), BF16), pltpu.VMEM((sl + 2 * rad, wv), BF16),
                        pltpu.VMEM((sl + 2 * rad, wk), F32), pltpu.VMEM((sl + 2 * rad, wv), F32)],
        compiler_params=_cparams(ARB, ARB),
    )(*args)


def _combine_a(os_, ls_, *, tm, name):
    bl = os_[1].shape[0]
    t = bl * SEQ
    nt = SEQ // tm

    def body(o1, o4, o16, l1, l4, l16, y_ref, lt_ref, scr):
        for k, (d, ref) in enumerate(((4, o4), (16, o16), (4, l4), (16, l16))):
            for r in range(d):
                _write_residue(scr, 2 * k, r, d, ref[r])
        o2, o3, b, c = (_gather_cols(scr, 2 * k, 2) for k in range(4))
        a = l1[...]
        m = jnp.maximum(jnp.maximum(a, b), c)
        ea, eb, ec = jnp.exp(a - m), jnp.exp(b - m), jnp.exp(c - m)
        den = ea + eb + ec
        y_ref[...] = (ea / den) * o1[...] + (eb / den) * o2 + (ec / den) * o3
        lt_ref[...] = m + jnp.log(den)

    specs = _residue_specs(tm, 256, nt)
    return pl.pallas_call(
        body, name=name, grid=(bl, nt), in_specs=specs * 2, out_specs=[specs[0]] * 2,
        out_shape=[_sds((t, 256), F32)] * 2, scratch_shapes=[pltpu.VMEM((8, tm, LANES), F32)],
        compiler_params=_cparams(PAR, PAR),
    )(*os_, *ls_)


def _deltas(dycat, ya, yb, yd, lse_a, *, tm, name):
    t = ya.shape[0]
    bl = t // SEQ
    nt = SEQ // tm

    def body(dy_ref, ya_ref, yb_ref, yd_ref, la_ref, dy4, dy16, l4, l16, da1, da4, da16, db_ref, dd_ref, scr):
        e = _group_sum_matrix(256, True)
        dya = dy_ref[:, 0:256]
        dla = _seg_sum(dya * ya_ref[...], e)
        da1[...] = dla
        db_ref[...] = _seg_sum(dy_ref[:, 256:512] * yb_ref[...], e)
        dd_ref[...] = _seg_sum(dy_ref[:, 768:1024] * yd_ref[...], e)
        for k, (val, r4, r16) in enumerate(((dya, dy4, dy16), (la_ref[...], l4, l16), (dla, da4, da16))):
            _scatter_cols(scr, 2 * k, val)
            for d, ref in ((4, r4), (16, r16)):
                for r in range(d):
                    ref[r] = _read_residue(scr, 2 * k, 2, r, d)

    specs = _residue_specs(tm, 256, nt)
    nat = specs[0]
    shapes = _residue_shapes(bl, 256, F32)
    outs = pl.pallas_call(
        body, name=name, grid=(bl, nt),
        in_specs=[pl.BlockSpec((tm, 1024), lambda b, i: (b * nt + i, 0)), nat, nat, nat, nat],
        out_specs=specs[1:] + specs[1:] + specs + [nat, nat],
        out_shape=shapes[1:] + shapes[1:] + shapes + [shapes[0], shapes[0]],
        scratch_shapes=[pltpu.VMEM((6, tm, LANES), F32)],
        compiler_params=_cparams(PAR, PAR),
    )(dycat, ya, yb, yd, lse_a)
    return outs[0:2], outs[2:4], outs[4:7], outs[7], outs[8]


def _dense_fwd(qd, *, tq, name):
    t = qd.shape[0]
    bl = t // SEQ
    nq = SEQ // tq

    def body(q_ref, k_ref, v_ref, o_ref, l_ref):
        q = q_ref[...]
        for g in range(2):
            h0, h1 = 2 * g, 2 * g + 1
            q2 = jnp.concatenate([q[:, h0 * 64:(h0 + 1) * 64], q[:, h1 * 64:(h1 + 1) * 64]], axis=0)
            kg = k_ref[:, g * 64:(g + 1) * 64]
            vg = v_ref[:, g * 64:(g + 1) * 64]
            s = lax.dot_general(q2, kg, NT, preferred_element_type=F32)
            m = jnp.max(s, axis=1, keepdims=True)
            p = jnp.exp(s - m)
            den = jnp.sum(p, axis=1, keepdims=True)
            o2 = jnp.dot(p.astype(BF16), vg, preferred_element_type=F32) / den
            l2 = jnp.broadcast_to(m + jnp.log(den), (2 * tq, 64))
            o_ref[:, h0 * 64:(h0 + 1) * 64] = o2[:tq]
            o_ref[:, h1 * 64:(h1 + 1) * 64] = o2[tq:]
            l_ref[:, h0 * 64:(h0 + 1) * 64] = l2[:tq]
            l_ref[:, h1 * 64:(h1 + 1) * 64] = l2[tq:]

    q3 = qd.reshape(bl, SEQ, 512)
    o, lse = pl.pallas_call(
        body, name=name, grid=(bl, nq),
        in_specs=[pl.BlockSpec((None, tq, 256), lambda b, i: (b, i, 0)),
                  pl.BlockSpec((None, SEQ, 128), lambda b, i: (b, 0, 2)),
                  pl.BlockSpec((None, SEQ, 128), lambda b, i: (b, 0, 3))],
        out_specs=[pl.BlockSpec((None, tq, 256), lambda b, i: (b, i, 0))] * 2,
        out_shape=[_sds((bl, SEQ, 256), F32)] * 2,
        compiler_params=_cparams(PAR, PAR),
    )(q3, q3, q3)
    return o.reshape(t, 256), lse.reshape(t, 256)


def _dense_bwd(qd, dycat, lse, delta, *, tq, name):
    t = qd.shape[0]
    bl = t // SEQ
    nq = SEQ // tq

    def body(q_ref, k_ref, v_ref, do_ref, l_ref, dl_ref, dq_ref, dk_ref, dv_ref, dkt, dvt):
        @pl.when(pl.program_id(1) == 0)
        def _():
            dkt[...] = jnp.zeros_like(dkt)
            dvt[...] = jnp.zeros_like(dvt)

        q = q_ref[...]
        do = do_ref[...].astype(BF16)
        lv = l_ref[...]
        dlv = dl_ref[...]
        for g in range(2):
            h0, h1 = 2 * g, 2 * g + 1
            q2 = jnp.concatenate([q[:, h0 * 64:(h0 + 1) * 64], q[:, h1 * 64:(h1 + 1) * 64]], axis=0)
            do2 = jnp.concatenate([do[:, h0 * 64:(h0 + 1) * 64], do[:, h1 * 64:(h1 + 1) * 64]], axis=0)
            l2 = jnp.concatenate([lv[:, h0 * 64:h0 * 64 + 1], lv[:, h1 * 64:h1 * 64 + 1]], axis=0)
            dl2 = jnp.concatenate([dlv[:, h0 * 64:h0 * 64 + 1], dlv[:, h1 * 64:h1 * 64 + 1]], axis=0)
            kg = k_ref[:, g * 64:(g + 1) * 64]
            vg = v_ref[:, g * 64:(g + 1) * 64]
            s = lax.dot_general(q2, kg, NT, preferred_element_type=F32)
            p = jnp.exp(s - l2)
            dp = lax.dot_general(do2, vg, NT, preferred_element_type=F32)
            ds = (p * (dp - dl2)).astype(BF16)
            dq2 = jnp.dot(ds, kg, preferred_element_type=F32)
            dq_ref[:, h0 * 64:(h0 + 1) * 64] = dq2[:tq].astype(BF16)
            dq_ref[:, h1 * 64:(h1 + 1) * 64] = dq2[tq:].astype(BF16)
            dkt[g * 64:(g + 1) * 64, :] += lax.dot_general(q2, ds, TN, preferred_element_type=F32)
            dvt[g * 64:(g + 1) * 64, :] += lax.dot_general(do2, p.astype(BF16), TN, preferred_element_type=F32)

        @pl.when(pl.program_id(1) == nq - 1)
        def _():
            dk_ref[...] = dkt[...].T.astype(BF16)
            dv_ref[...] = dvt[...].T.astype(BF16)

    q3 = qd.reshape(bl, SEQ, 512)
    tile = pl.BlockSpec((None, tq, 256), lambda b, i: (b, i, 0))
    full = pl.BlockSpec((None, SEQ, 128), lambda b, i: (b, 0, 0))
    dq, dk, dv = pl.pallas_call(
        body, name=name, grid=(bl, nq),
        in_specs=[tile, pl.BlockSpec((None, SEQ, 128), lambda b, i: (b, 0, 2)),
                  pl.BlockSpec((None, SEQ, 128), lambda b, i: (b, 0, 3)),
                  pl.BlockSpec((None, tq, 256), lambda b, i: (b, i, 3)), tile, tile],
        out_specs=[tile, full, full],
        out_shape=[_sds((bl, SEQ, 256), BF16), _sds((bl, SEQ, 128), BF16), _sds((bl, SEQ, 128), BF16)],
        scratch_shapes=[pltpu.VMEM((128, SEQ), F32), pltpu.VMEM((128, SEQ), F32)],
        compiler_params=_cparams(PAR, ARB),
    )(q3, q3, q3, dycat.reshape(bl, SEQ, 1024), lse.reshape(bl, SEQ, 256), delta.reshape(bl, SEQ, 256))
    return dq.reshape(t, 256), dk.reshape(t, 128), dv.reshape(t, 128)


def _c_norm(cv, gam, bet):
    vg = _gelu(cv)
    mu = jnp.mean(vg, axis=-1, keepdims=True)
    xc = vg - mu
    r = lax.rsqrt(jnp.mean(xc * xc, axis=-1, keepdims=True) + EPS)
    xhat = xc * r
    return xhat * gam + bet, xhat, r


def _c_fwd(proj, gam, bet, ws, bst, *, tm, name):
    t = proj.shape[0]
    nch = tm // C_CHUNK

    def body(u_ref, v_ref, g_ref, b_ref, ws_ref, bs_ref, y_ref):
        vn, _, _ = _c_norm(v_ref[...].astype(F32), g_ref[...], b_ref[...])
        vnb = vn.astype(BF16)
        for c in range(nch):
            rows = slice(c * C_CHUNK, (c + 1) * C_CHUNK)
            for g in range(C_GROUPS):
                gs = slice(g * 64, (g + 1) * 64)
                mixed = jnp.dot(ws_ref[g], vnb[rows, gs], preferred_element_type=F32) + bs_ref[:, gs]
                y_ref[rows, gs] = _gelu(u_ref[rows, gs].astype(F32)) * mixed

    vec = pl.BlockSpec((1, 256), lambda i: (0, 0))
    return pl.pallas_call(
        body, name=name, grid=(t // tm,),
        in_specs=[pl.BlockSpec((tm, 256), lambda i: (i, 5)), pl.BlockSpec((tm, 256), lambda i: (i, 6)), vec, vec,
                  pl.BlockSpec((C_GROUPS, C_CHUNK, C_CHUNK), lambda i: (0, 0, 0)),
                  pl.BlockSpec((C_CHUNK, 256), lambda i: (0, 0))],
        out_specs=pl.BlockSpec((tm, 256), lambda i: (i, 0)), out_shape=_sds((t, 256), F32),
        compiler_params=_cparams(PAR),
    )(proj, proj, gam, bet, ws, bst)


def _c_bwd(proj, dycat, gam, bet, ws, wst, bst, *, tm, name):
    t = proj.shape[0]
    nch = tm // C_CHUNK
    nstep = t // tm

    def body(u_ref, v_ref, dy_ref, g_ref, b_ref, ws_ref, wst_ref, bs_ref,
             du_ref, dv_ref, dws_ref, dbs_ref, dg_ref, db_ref, dvn_s):
        step = pl.program_id(0)

        @pl.when(step == 0)
        def _():
            dws_ref[...] = jnp.zeros_like(dws_ref)
            dbs_ref[...] = jnp.zeros_like(dbs_ref)
            dg_ref[...] = jnp.zeros_like(dg_ref)
            db_ref[...] = jnp.zeros_like(db_ref)

        cv = v_ref[...].astype(F32)
        gam_v = g_ref[...]
        vn, xhat, r = _c_norm(cv, gam_v, b_ref[...])
        vnb = vn.astype(BF16)
        for c in range(nch):
            rows = slice(c * C_CHUNK, (c + 1) * C_CHUNK)
            for g in range(C_GROUPS):
                gs = slice(g * 64, (g + 1) * 64)
                cu = u_ref[rows, gs].astype(F32)
                dy = dy_ref[rows, gs]
                mixed = jnp.dot(ws_ref[g], vnb[rows, gs], preferred_element_type=F32) + bs_ref[:, gs]
                du_ref[rows, gs] = (dy * mixed * _gelu_grad(cu)).astype(BF16)
                dmix = dy * _gelu(cu)
                dbs_ref[:, gs] += dmix
                dmb = dmix.astype(BF16)
                dws_ref[g] += lax.dot_general(dmb, vnb[rows, gs], NT, preferred_element_type=F32)
                dvn_s[rows, gs] = jnp.dot(wst_ref[g], dmb, preferred_element_type=F32)
        dvn = dvn_s[...]
        dg_ref[...] += jnp.sum(dvn * xhat, axis=0, keepdims=True)
        db_ref[...] += jnp.sum(dvn, axis=0, keepdims=True)
        dxh = dvn * gam_v
        dvg = r * (dxh - jnp.mean(dxh, axis=-1, keepdims=True) - xhat * jnp.mean(dxh * xhat, axis=-1, keepdims=True))
        dv_ref[...] = (dvg * _gelu_grad(cv)).astype(BF16)

        @pl.when(step == nstep - 1)
        def _():
            dbs_ref[...] = _seg_sum(dbs_ref[...], _group_sum_matrix(256, True))

    vec = pl.BlockSpec((1, 256), lambda i: (0, 0))
    mat = pl.BlockSpec((C_GROUPS, C_CHUNK, C_CHUNK), lambda i: (0, 0, 0))
    bsp = pl.BlockSpec((C_CHUNK, 256), lambda i: (0, 0))
    tile = pl.BlockSpec((tm, 256), lambda i: (i, 0))
    return pl.pallas_call(
        body, name=name, grid=(nstep,),
        in_specs=[pl.BlockSpec((tm, 256), lambda i: (i, 5)), pl.BlockSpec((tm, 256), lambda i: (i, 6)),
                  pl.BlockSpec((tm, 256), lambda i: (i, 2)), vec, vec, mat, mat, bsp],
        out_specs=[tile, tile, mat, bsp, vec, vec],
        out_shape=[_sds((t, 256), BF16), _sds((t, 256), BF16), _sds((C_GROUPS, C_CHUNK, C_CHUNK), F32),
                   _sds((C_CHUNK, 256), F32), _sds((1, 256), F32), _sds((1, 256), F32)],
        scratch_shapes=[pltpu.VMEM((tm, 256), F32)],
        compiler_params=_cparams(ARB),
    )(proj, proj, dycat, gam, bet, ws, wst, bst)


FF_TC = 128
FF_NB = D_FF // FF_TC
FF_CH = 64
FF_HALO = 16


def _taps(ref, r0, win, where):
    z = jnp.zeros((FF_HALO, win.shape[1]), F32)
    if where == "first":
        win[0:FF_HALO, :] = z
        win[FF_HALO:, :] = ref[0:FF_CH + FF_HALO, :].astype(F32)
    elif where == "last":
        win[0:FF_CH + FF_HALO, :] = ref[SEQ - FF_CH - FF_HALO:SEQ, :].astype(F32)
        win[FF_CH + FF_HALO:, :] = z
    else:
        win[...] = ref[pl.ds(pl.multiple_of(r0 - FF_HALO, FF_HALO), FF_CH + 2 * FF_HALO), :].astype(F32)
    return tuple(win[FF_HALO + o:FF_HALO + o + FF_CH, :] for o in (-1, 0, 1))


def _chunk_loop(step):
    step(0, lambda ref, win: _taps(ref, 0, win, "first"))

    def mid(i, carry):
        r0 = pl.multiple_of(i * FF_CH, FF_CH)
        step(r0, lambda ref, win: _taps(ref, r0, win, "mid"))
        return carry

    lax.fori_loop(1, SEQ // FF_CH - 1, mid, 0)
    step(SEQ - FF_CH, lambda ref, win: _taps(ref, SEQ - FF_CH, win, "last"))


def _conv3(taps, w_ref, b_ref):
    dn, md, up = taps
    return w_ref[0:1, :] * dn + w_ref[1:2, :] * md + w_ref[2:3, :] * up + b_ref[...]


def _ff_specs(order):
    def at(fn):
        return (lambda b, j: fn(b, j)) if order == "bj" else (lambda j, b: fn(b, j))
    hs = [pl.BlockSpec((None, SEQ, FF_TC), at(lambda b, j, o=o: (b, 0, j + o))) for o in (0, FF_NB)]
    ws = [pl.BlockSpec((3, FF_TC), at(lambda b, j, o=o: (0, j + o))) for o in (0, FF_NB)]
    bs = [pl.BlockSpec((1, FF_TC), at(lambda b, j, o=o: (0, j + o))) for o in (0, FF_NB)]
    return hs, ws, bs


def _conv_gate_fwd(h, cw, cb, *, name):
    t = h.shape[0]
    bl = t // SEQ

    def body(hg_ref, hu_ref, wg_ref, wu_ref, bg_ref, bu_ref, a_ref, cg_ref, cu_ref, win):
        def step(r0, taps):
            cg = _conv3(taps(hg_ref, win.at[0]), wg_ref, bg_ref)
            cu = _conv3(taps(hu_ref, win.at[1]), wu_ref, bu_ref)
            a_ref[pl.ds(r0, FF_CH), :] = (cg * _sigmoid(cg) * cu).astype(BF16)
            cg_ref[pl.ds(r0, FF_CH), :] = cg.astype(BF16)
            cu_ref[pl.ds(r0, FF_CH), :] = cu.astype(BF16)

        _chunk_loop(step)

    hs, ws, bs = _ff_specs("bj")
    h3 = h.reshape(bl, SEQ, 2 * D_FF)
    half = pl.BlockSpec((None, SEQ, FF_TC), lambda b, j: (b, 0, j))
    outs = pl.pallas_call(
        body, name=name, grid=(bl, FF_NB), in_specs=hs + ws + bs, out_specs=[half] * 3,
        out_shape=[_sds((bl, SEQ, D_FF), BF16)] * 3,
        scratch_shapes=[pltpu.VMEM((2, FF_CH + 2 * FF_HALO, FF_TC), F32)],
        compiler_params=_cparams(PAR, PAR),
    )(h3, h3, cw, cw, cb, cb)
    return [o.reshape(t, D_FF) for o in outs]


def _conv_gate_bwd(h, cg_all, cu_all, dact, cw, cb, *, name):
    t = h.shape[0]
    bl = t // SEQ

    def body(hg_ref, hu_ref, wg_ref, wu_ref, bg_ref, bu_ref, da_ref, cg_ref, cu_ref,
             dhg_ref, dhu_ref, dwg_ref, dwu_ref, dbg_ref, dbu_ref, dg_s, du_s, win, sums):
        @pl.when(pl.program_id(1) == 0)
        def _():
            for ref in (dwg_ref, dwu_ref, dbg_ref, dbu_ref):
                ref[...] = jnp.zeros_like(ref)

        sums[...] = jnp.zeros_like(sums)
        red = lambda x: jnp.sum(x.reshape(FF_CH // 8, 8, x.shape[1]), axis=0)

        def pass1(r0, taps):
            tg, tu = taps(hg_ref, win.at[0]), taps(hu_ref, win.at[1])
            cg = cg_ref[pl.ds(r0, FF_CH), :].astype(F32)
            cu = cu_ref[pl.ds(r0, FF_CH), :].astype(F32)
            da = da_ref[pl.ds(r0, FF_CH), :].astype(F32)
            sg = _sigmoid(cg)
            dcg = da * cu * (sg * (1.0 + cg * (1.0 - sg)))
            dcu = da * (cg * sg)
            dg_s[pl.ds(r0, FF_CH), :] = dcg
            du_s[pl.ds(r0, FF_CH), :] = dcu
            for half, (d, tp) in enumerate(((dcg, tg), (dcu, tu))):
                for k in range(3):
                    sums[4 * half + k] += red(d * tp[k])
                sums[4 * half + 3] += red(d)

        _chunk_loop(pass1)
        for half, (dw_ref, db_ref) in enumerate(((dwg_ref, dbg_ref), (dwu_ref, dbu_ref))):
            for k in range(3):
                dw_ref[k:k + 1, :] += jnp.sum(sums[4 * half + k], axis=0, keepdims=True)
            db_ref[...] += jnp.sum(sums[4 * half + 3], axis=0, keepdims=True)

        def pass2(r0, taps):
            for k, (s, w_ref, o_ref) in enumerate(((dg_s, wg_ref, dhg_ref), (du_s, wu_ref, dhu_ref))):
                dn, md, up = taps(s, win.at[k])
                o_ref[pl.ds(r0, FF_CH), :] = (w_ref[0:1, :] * up + w_ref[1:2, :] * md + w_ref[2:3, :] * dn).astype(BF16)

        _chunk_loop(pass2)

    hs, ws, bs = _ff_specs("jb")
    half = pl.BlockSpec((None, SEQ, FF_TC), lambda j, b: (b, 0, j))
    wsp = pl.BlockSpec((3, FF_TC), lambda j, b: (0, j))
    bsp = pl.BlockSpec((1, FF_TC), lambda j, b: (0, j))
    h3 = h.reshape(bl, SEQ, 2 * D_FF)
    dhg, dhu, dwg, dwu, dbg, dbu = pl.pallas_call(
        body, name=name, grid=(FF_NB, bl), in_specs=hs + ws + bs + [half] * 3,
        out_specs=[half, half, wsp, wsp, bsp, bsp],
        out_shape=[_sds((bl, SEQ, D_FF), BF16), _sds((bl, SEQ, D_FF), BF16), _sds((3, D_FF), F32), _sds((3, D_FF), F32),
                   _sds((1, D_FF), F32), _sds((1, D_FF), F32)],
        scratch_shapes=[pltpu.VMEM((SEQ, FF_TC), F32), pltpu.VMEM((SEQ, FF_TC), F32),
                        pltpu.VMEM((2, FF_CH + 2 * FF_HALO, FF_TC), F32), pltpu.VMEM((8, 8, FF_TC), F32)],
        compiler_params=_cparams(PAR, ARB),
    )(h3, h3, cw, cw, cb, cb, *[a.reshape(bl, SEQ, D_FF) for a in (dact, cg_all, cu_all)])
    return (dhg.reshape(t, D_FF), dhu.reshape(t, D_FF), jnp.concatenate([dwg, dwu], axis=1),
            jnp.concatenate([dbg, dbu], axis=1))


def _ple_fwd(x2, gain, wg, pe, pe_blk, wp, *, tm, name):
    t, k = x2.shape

    def body(x_ref, g_ref, wg_ref, pe_ref, wp_ref, hn_ref, x3_ref, gt_ref, pp_ref):
        x = x_ref[...]
        r = lax.rsqrt(jnp.mean(x * x, axis=-1, keepdims=True) + EPS)
        hn = (x * r * g_ref[...]).astype(BF16)
        hn_ref[...] = hn
        gate = _sigmoid(jnp.dot(hn, wg_ref[...], preferred_element_type=F32))
        pp = jnp.dot(pe_ref[...].astype(BF16), wp_ref[...], preferred_element_type=F32)
        gt_ref[...] = gate.astype(BF16)
        pp_ref[...] = pp.astype(BF16)
        x3_ref[...] = x + pp * gate

    row = pl.BlockSpec((tm, k), lambda i: (i, 0))
    return pl.pallas_call(
        body, name=name, grid=(t // tm,),
        in_specs=[row, pl.BlockSpec((1, k), lambda i: (0, 0)), pl.BlockSpec((k, k), lambda i: (0, 0)),
                  pl.BlockSpec((tm, PLE_DIM), lambda i: (pe_blk + i, 0)), pl.BlockSpec((PLE_DIM, k), lambda i: (0, 0))],
        out_specs=[row, row, row, row],
        out_shape=[_sds((t, k), BF16), _sds((t, k), F32), _sds((t, k), BF16), _sds((t, k), BF16)],
        compiler_params=_cparams(PAR),
    )(x2, gain, wg, pe, wp)


def _ple_bwd_ew(dx3, gate, pp, *, tm, name):
    t, n = dx3.shape

    def body(d_ref, g_ref, p_ref, dz_ref, dpp_ref):
        d, g = d_ref[...], g_ref[...]
        dz_ref[...] = (d * p_ref[...] * g * (1.0 - g)).astype(BF16)
        dpp_ref[...] = (d * g).astype(BF16)

    spec = pl.BlockSpec((tm, n), lambda i: (i, 0))
    return pl.pallas_call(
        body, name=name, grid=(t // tm,), in_specs=[spec] * 3, out_specs=[spec] * 2,
        out_shape=[_sds((t, n), BF16)] * 2, compiler_params=_cparams(PAR),
    )(dx3, gate, pp)


def _loss_head(y, tgt, *, tm, name):
    t, d = y.shape

    def body(y_ref, t_ref, l_ref, dy_ref):
        @pl.when(pl.program_id(0) == 0)
        def _():
            l_ref[...] = jnp.zeros_like(l_ref)

        e = y_ref[...] - t_ref[...]
        dy_ref[...] = e * (1.0 / d)
        s = jnp.sum(jnp.sum(e * e, axis=1, keepdims=True), axis=0, keepdims=True)
        l_ref[...] += jnp.broadcast_to(s * (0.5 / d), (8, 128))

    spec = pl.BlockSpec((tm, d), lambda i: (i, 0))
    return pl.pallas_call(
        body, name=name, grid=(t // tm,), in_specs=[spec, spec],
        out_specs=[pl.BlockSpec((8, 128), lambda i: (0, 0)), spec],
        out_shape=[_sds((8, 128), F32), _sds((t, d), F32)], compiler_params=_cparams(ARB),
    )(y, tgt)


BIAS_PC = 8192


def _onehot(bucket_row):
    rows = lax.broadcasted_iota(jnp.int32, (REL_BUCKETS, bucket_row.shape[1]), 0)
    return (rows == bucket_row).astype(BF16)


def _dot3(x, onehot, dims):
    acc = None
    for _ in range(3):
        term = x.astype(BF16)
        part = lax.dot_general(term, onehot, dims, preferred_element_type=F32)
        acc = part if acc is None else acc + part
        x = x - term.astype(F32)
    return acc


def _bias_lookup(table_t, bucket, *, name):
    h = table_t.shape[0]
    p = bucket.shape[1]

    def body(t_ref, b_ref, o_ref):
        bk = b_ref[...]
        val = _dot3(t_ref[...], _onehot(bk), (((1,), (0,)), ((), ())))
        o_ref[...] = jnp.where(bk >= 0, val, NEG_INF)

    return pl.pallas_call(
        body, name=name, grid=(p // BIAS_PC,),
        in_specs=[pl.BlockSpec((h, REL_BUCKETS), lambda i: (0, 0)), pl.BlockSpec((1, BIAS_PC), lambda i: (0, i))],
        out_specs=pl.BlockSpec((h, BIAS_PC), lambda i: (0, i)), out_shape=_sds((h, p), F32),
        compiler_params=_cparams(PAR),
    )(table_t, bucket)


def _bucket_reduce(dbiases, bucket, *, name):
    h, p = dbiases[0].shape
    nl = len(dbiases)

    def body(*refs):
        b_ref, o_ref = refs[nl], refs[nl + 1]

        @pl.when(pl.program_id(0) == 0)
        def _():
            o_ref[...] = jnp.zeros_like(o_ref)

        d = refs[0][...]
        for d_ref in refs[1:nl]:
            d = d + d_ref[...]
        o_ref[...] += _dot3(d, _onehot(b_ref[...]), NT)

    return pl.pallas_call(
        body, name=name, grid=(p // BIAS_PC,),
        in_specs=[pl.BlockSpec((h, BIAS_PC), lambda i: (0, i))] * nl + [pl.BlockSpec((1, BIAS_PC), lambda i: (0, i))],
        out_specs=pl.BlockSpec((h, REL_BUCKETS), lambda i: (0, 0)), out_shape=_sds((h, REL_BUCKETS), F32),
        compiler_params=_cparams(ARB),
    )(*dbiases, bucket)


def _adamw_math(w, g, m, v):
    m = ADAM_B1 * m + (1.0 - ADAM_B1) * g
    v = ADAM_B2 * v + (1.0 - ADAM_B2) * (g * g)
    m_hat = m / (1.0 - ADAM_B1 ** ADAM_STEP)
    v_hat = v / (1.0 - ADAM_B2 ** ADAM_STEP)
    delta = -ADAM_LR * (m_hat / (jnp.sqrt(v_hat) + ADAM_EPS) + ADAM_WD * w)
    return delta, m, v


def _adamw_reduce(parts, w, m, v, *, tr, name):
    nl = len(parts)
    rows, c = w.shape
    r = rows // nl
    nt = r // tr

    def body(*refs):
        p_refs = refs[:nl]
        w_ref, m_ref, v_ref, g_ref, d_ref, nm_ref, nv_ref = refs[nl:]
        for li, p_ref in enumerate(p_refs):
            @pl.when(pl.program_id(0) == li)
            def _(p_ref=p_ref):
                g = p_ref[0].astype(F32)
                for k in range(1, N_DEV):
                    g = g + p_ref[k].astype(F32)
                d, nm, nv = _adamw_math(w_ref[...], g, m_ref[...], v_ref[...])
                g_ref[...] = g
                d_ref[...] = d
                nm_ref[...] = nm
                nv_ref[...] = nv

    def part_map(li):
        return lambda l, i: (0, jnp.where(l == li, i, jnp.where(l < li, 0, nt - 1)), 0)

    spec = pl.BlockSpec((tr, c), lambda l, i: (l * nt + i, 0))
    return pl.pallas_call(
        body, name=name, grid=(nl, nt),
        in_specs=[pl.BlockSpec((N_DEV, tr, c), part_map(li)) for li in range(nl)] + [spec, spec, spec],
        out_specs=[spec] * 4, out_shape=[_sds((rows, c), F32)] * 4, compiler_params=_cparams(ARB, ARB),
    )(*parts, w, m, v)


def _adamw_plain(g, w, m, v, *, name):
    def body(g_ref, w_ref, m_ref, v_ref, d_ref, nm_ref, nv_ref):
        d, nm, nv = _adamw_math(w_ref[...], g_ref[...], m_ref[...], v_ref[...])
        d_ref[...] = d
        nm_ref[...] = nm
        nv_ref[...] = nv

    return pl.pallas_call(body, name=name, out_shape=[_sds(w.shape, F32)] * 3)(g, w, m, v)


def _mesh_pos():
    return lax.axis_index("x"), lax.axis_index("y"), lax.axis_index("c")


def _allgather_body(x_refs, out_refs, send_sems, recv_sems, local_sems, slot):
    x, y, c = _mesh_pos()
    me, sibling = (x, y, c), (x, y, 1 - c)
    chips = [(1 - x, y), (x, 1 - y), (1 - x, 1 - y)]
    waits = []
    for a, (x_ref, out_ref) in enumerate(zip(x_refs, out_refs)):
        def copy(k, block, to, src=None, out_ref=out_ref, a=a):
            return pltpu.make_async_remote_copy(
                src_ref=slot(out_ref, block) if src is None else src, dst_ref=slot(out_ref, block),
                send_sem=send_sems.at[a, k], recv_sem=recv_sems.at[a, k], device_id=to, device_id_type=MESH)

        mine = pltpu.make_async_copy(x_ref, slot(out_ref, me), local_sems.at[a])
        mine.start()
        first = [copy(0, me, sibling, src=x_ref)]
        first += [copy(1 + j, me, (*chip, c), src=x_ref) for j, chip in enumerate(chips)]
        for cp in first:
            cp.start()
        waits.append((copy, mine, first))
    sends = []
    for copy, mine, first in waits:
        passed = [copy(4 + j, (*chip, c), sibling) for j, chip in enumerate(chips)]
        for j, chip in enumerate(chips):
            copy(1 + j, (*chip, c), me).wait_recv()
            passed[j].start()
        sends.append(passed)
    for (copy, mine, first), passed in zip(waits, sends):
        copy(0, sibling, me).wait_recv()
        for j, chip in enumerate(chips):
            copy(4 + j, (*chip, 1 - c), me).wait_recv()
        for cp in first + passed:
            cp.wait_send()
        mine.wait()


PEER_FLIPS = ((0, 0, 1), (1, 0, 0), (0, 1, 0), (1, 1, 0), (1, 0, 1), (0, 1, 1), (1, 1, 1))


def _peer_copies(x_refs, land_refs, send_sem, recv_sem, scatter):
    x, y, c = _mesh_pos()
    me = 4 * x + 2 * y + c
    copies = []
    for x_ref, land_ref in zip(x_refs, land_refs):
        for fx, fy, fc in PEER_FLIPS:
            px, py, pc = x ^ fx, y ^ fy, c ^ fc
            src = x_ref.at[4 * px + 2 * py + pc] if scatter else x_ref
            copies.append(pltpu.make_async_remote_copy(
                src_ref=src, dst_ref=land_ref.at[me], send_sem=send_sem, recv_sem=recv_sem,
                device_id=(px, py, pc), device_id_type=MESH))
    return copies


def _sc_exchange(xs, *, scatter, collective_id, name):
    na = len(xs)
    land_shapes = [x.shape if scatter else (N_DEV,) + x.shape for x in xs]

    def body(*refs):
        x_refs, land_refs = refs[:na], refs[na:2 * na]
        send_sem, recv_sem, local_sem = refs[2 * na:]
        x, y, c = _mesh_pos()
        me = 4 * x + 2 * y + c
        barrier = pltpu.get_barrier_semaphore()
        for fx, fy, fc in PEER_FLIPS:
            pl.semaphore_signal(barrier, inc=1, device_id=(x ^ fx, y ^ fy, c ^ fc), device_id_type=MESH)
        pl.semaphore_wait(barrier, len(PEER_FLIPS))
        for x_ref, land_ref in zip(x_refs, land_refs):
            own = pltpu.make_async_copy(x_ref.at[me] if scatter else x_ref, land_ref.at[me], local_sem)
            own.start()
            own.wait()
        copies = _peer_copies(x_refs, land_refs, send_sem, recv_sem, scatter)
        for cp in copies:
            cp.start()
        for cp in copies:
            cp.wait()

    return pl.kernel(
        body, name=name, out_type=[_sds(s, x.dtype) for s, x in zip(land_shapes, xs)],
        mesh=plsc.ScalarSubcoreMesh(axis_name="sequencer", num_cores=1),
        scratch_types=[pltpu.SemaphoreType.DMA, pltpu.SemaphoreType.DMA, pltpu.SemaphoreType.DMA],
        compiler_params=pltpu.CompilerParams(collective_id=collective_id),
    )(*xs)


def _sc_allgather(xs, *, collective_id, name):
    na = len(xs)

    def body(*refs):
        x_refs, out_refs = refs[:na], refs[na:2 * na]
        send_sems, recv_sems, local_sems = refs[2 * na:]
        x, y, c = _mesh_pos()
        barrier = pltpu.get_barrier_semaphore()
        for fx, fy, fc in PEER_FLIPS:
            pl.semaphore_signal(barrier, inc=1, device_id=(x ^ fx, y ^ fy, c ^ fc), device_id_type=MESH)
        pl.semaphore_wait(barrier, len(PEER_FLIPS))
        _allgather_body(x_refs, out_refs, send_sems, recv_sems, local_sems,
                        lambda ref, pos: ref.at[4 * pos[0] + 2 * pos[1] + pos[2]])

    return pl.kernel(
        body, name=name, out_type=[_sds((N_DEV,) + x.shape, x.dtype) for x in xs],
        mesh=plsc.ScalarSubcoreMesh(axis_name="sequencer", num_cores=1),
        scratch_types=[pltpu.SemaphoreType.DMA((na, 7)), pltpu.SemaphoreType.DMA((na, 7)),
                       pltpu.SemaphoreType.DMA((na,))],
        compiler_params=pltpu.CompilerParams(collective_id=collective_id),
    )(*xs)


def _allgather_vmem(x, *, name):
    r, c = x.shape

    def body(x_ref, out_ref, send_sems, recv_sems, local_sems):
        _allgather_body([x_ref], [out_ref], send_sems, recv_sems, local_sems,
                        lambda ref, pos: ref.at[pl.ds((4 * pos[0] + 2 * pos[1] + pos[2]) * r, r), :])

    vm = pl.BlockSpec(memory_space=pltpu.VMEM)
    return pl.pallas_call(
        body, name=name, in_specs=[vm], out_specs=vm, out_shape=_sds((N_DEV * r, c), x.dtype),
        scratch_shapes=[pltpu.SemaphoreType.DMA((1, 7)), pltpu.SemaphoreType.DMA((1, 7)),
                        pltpu.SemaphoreType.DMA((1,))],
    )(x)


def _sum_slots(gathered, *, name):
    _, r, c = gathered.shape

    def body(g_ref, o_ref):
        acc = g_ref[0]
        for k in range(1, N_DEV):
            acc = acc + g_ref[k]
        o_ref[...] = acc

    return pl.pallas_call(body, name=name, out_shape=_sds((r, c), gathered.dtype))(gathered)


def _t5_bucket(rel):
    nb = REL_BUCKETS // 2
    ret = jnp.where(rel > 0, nb, 0)
    n = jnp.abs(rel)
    max_exact = nb // 2
    nf = jnp.maximum(n, 1).astype(F32)
    large = max_exact + (jnp.log(nf / max_exact) / math.log(REL_MAX_DIST / max_exact)
                         * (nb - max_exact)).astype(jnp.int32)
    large = jnp.minimum(large, nb - 1)
    return ret + jnp.where(n < max_exact, n, large)


def _band_pattern(block, radius, dil):
    kw = block + 2 * radius
    rel = jnp.arange(kw)[None, :] - radius - jnp.arange(block)[:, None]
    return jnp.where(jnp.abs(rel) <= radius, _t5_bucket(rel * dil), -1).astype(jnp.int32).reshape(1, block * kw)


def _rope_tables():
    lane = np.arange(64)
    seg, j = lane // 32, lane % 32
    inv = ROPE_THETA ** (-jnp.arange(0, 32, 2, dtype=F32) / 32)
    tpos = jnp.arange(SEQ)
    pos = jnp.where(jnp.asarray(seg)[None, :] == 0, (tpos // GRID_W)[:, None], (tpos % GRID_W)[:, None])
    ang = pos.astype(F32) * inv[jnp.asarray(j % 16)][None, :]
    cos = jnp.cos(ang)
    sins = jnp.where(jnp.asarray(j)[None, :] < 16, -jnp.sin(ang), jnp.sin(ang))
    return jnp.tile(cos, (1, 4)), jnp.tile(sins, (1, 4))


A_Q, A_K, A_V = (256, 0), (256, 1), (256, 2)
B_Q, B_K, B_V = (256, 0), (128, 2), (128, 3)
A_HEADS = dict(rad=A_RADIUS, nh=4, nkv=4)
B_HEADS = dict(rad=SWA_RADIUS, nh=4, nkv=2)


def _local_step(x, pe, tgt, rel_bias, wts, matmul_weights, grads_ready):
    t = x.shape[0]
    bl = t // SEQ
    cos, sins = _rope_tables()
    blocks_a = [min(BAND_BLOCK, SEQ // d) for d in DILATIONS]
    pats_a = [_band_pattern(blk, A_RADIUS, d) for blk, d in zip(blocks_a, DILATIONS)]
    pat_b = _band_pattern(BAND_BLOCK, SWA_RADIUS, 1)
    table_t = rel_bias.T
    bias_a = [_bias_lookup(table_t[:4], pt, name=f"bias_a{ci}").reshape(4, blk, blk + 2 * A_RADIUS)
              for ci, (pt, blk) in enumerate(zip(pats_a, blocks_a))]
    bias_b = _bias_lookup(table_t[4:], pat_b, name="bias_b").reshape(4, BAND_BLOCK, BAND_BLOCK + 2 * SWA_RADIUS)
    nat4 = lambda a: a.reshape(bl, 1, SEQ, a.shape[-1])

    saved = []
    for li in range(DEPTH):
        w = dict(wts[li])
        w.update(matmul_weights(li, "in", x))
        hn0, proj = _norm_mm((x,), w["g_mix"], w["w_in"], None, tm=1024, tn=1152, name="mix_in_fwd", out_dtype=BF16)
        qa1, qa4, qa16, qb, qd = _qkprep_fwd(proj, w["qk_gains"], cos, sins, tm=512, name="qkprep_fwd")
        qa = (nat4(qa1), qa4, qa16)
        oa, la = [], []
        for ci in range(3):
            o, l = _band_fwd(qa[ci], A_Q, A_K, A_V, bias_a[ci], None, name=f"band_a{ci}_fwd", **A_HEADS)
            oa.append(o)
            la.append(l)
        oa[0], la[0] = oa[0].reshape(t, 256), la[0].reshape(t, 256)
        ya, lse_a = _combine_a(oa, la, tm=512, name="combine_a")
        yb, lse_b = _band_fwd(nat4(qb), B_Q, B_K, B_V, bias_b, w["sink_t"], name="band_b_fwd", **B_HEADS)
        yb = yb.reshape(t, 256)
        yc = _c_fwd(proj, w["c_g"], w["c_b"], w["c_ws"], w["c_bst"], tm=512, name="c_fwd")
        yd, lse_d = _dense_fwd(qd, tq=256, name="dense_fwd")
        w.update(matmul_weights(li, "rest", yd))
        mixed, x1 = _norm_mm((ya, yb, yc, yd), w["out_gain"], w["w_out"], x, tm=1024, tn=1024, name="mix_out_fwd")
        hn1, h = _norm_mm((x1,), w["g_ffn"], w["w_up"], None, tm=1024, tn=1408, name="ffn_up_fwd", out_dtype=BF16)
        act, cg, cu = _conv_gate_fwd(h, w["conv_w"], w["conv_b"], name="conv_gate_fwd")
        x2 = _mm(act, w["w_down"], "nn", x1, tm=1024, tn=1024, out_dtype=F32, name="ffn_down_fwd")
        hn2, x3, gate, pp = _ple_fwd(x2, w["g_ple"], w["w_gate"], pe, li * (t // 1024), w["w_proj"], tm=1024,
                                     name="ple_fwd")
        saved.append(dict(w=w, x0=x, hn0=hn0, proj=proj, qa=qa, qb=qb, qd=qd, ya=ya, lse_a=lse_a, yb=yb, lse_b=lse_b,
                          yc=yc, yd=yd, lse_d=lse_d, mixed=mixed, x1=x1, hn1=hn1, h=h, cg=cg, cu=cu, act=act, x2=x2, hn2=hn2,
                          gate=gate, pp=pp))
        x = x3

    loss_tile, dx = _loss_head(x, tgt, tm=512, name="loss_head")
    grads = [None] * DEPTH
    dbias_a, dbias_bs = [[], [], []], []
    for li in reversed(range(DEPTH)):
        s = saved[li]
        w = s["w"]
        g = {}
        dz, dpp = _ple_bwd_ew(dx, s["gate"], s["pp"], tm=512, name="ple_bwd_ew")
        g["w_gate"] = _mm(s["hn2"], dz, "tn", None, tm=1024, tn=512, out_dtype=BF16, name="dw_gate")
        g["w_proj"] = _mm(pe, dpp, "tn", None, tm=256, tn=1024, out_dtype=BF16, name="dw_proj", a_rows=(li, t))
        dx2, dx2b, g["g_ple"] = _mm_bt_normbwd((dz,), w["w_gate"], (s["x2"],), w["g_ple"], dx, tm=1024, tn=1024,
                                               name="ple_bwd", emit_bf16=True)
        g["w_down"] = _mm(s["act"], dx2b, "tn", None, tm=1408, tn=512, out_dtype=BF16, name="dw_down")
        dact = _mm(dx2b, w["w_down"], "nt", None, tm=1024, tn=1408, out_dtype=BF16, name="ffn_down_bwd")
        dhg, dhu, g["conv_w"], g["conv_b"] = _conv_gate_bwd(s["h"], s["cg"], s["cu"], dact, w["conv_w"], w["conv_b"],
                                                            name="conv_gate_bwd")
        g["w_up"] = jnp.concatenate(
            [_mm(s["hn1"], dhalf, "tn", None, tm=1024, tn=1408, out_dtype=BF16, name=f"dw_up_{nm}")
             for nm, dhalf in (("gate", dhg), ("up", dhu))], axis=1)
        dx1, dx1b, g["g_ffn"] = _mm_bt_normbwd((dhg, dhu), w["w_up"], (s["x1"],), w["g_ffn"], dx2, tm=1024, tn=1408,
                                               name="ffn_up_bwd", emit_bf16=True)
        g["w_out"] = _mm(s["mixed"], dx1b, "tn", None, tm=1024, tn=512, out_dtype=BF16, name="dw_out")
        grads_ready(li, "mid", g)
        dycat, g["out_gain"] = _mm_bt_normbwd((dx1b,), w["w_out"], (s["ya"], s["yb"], s["yc"], s["yd"]), w["out_gain"],
                                              None, tm=1024, tn=1024, name="mix_out_bwd")
        dy_r, lse_r, dl_a, dl_b, dl_d = _deltas(dycat, s["ya"], s["yb"], s["yd"], s["lse_a"], tm=512, name="deltas")
        dy_a = (nat4(dycat),) + tuple(dy_r)
        lse_a = (nat4(s["lse_a"]),) + tuple(lse_r)
        dl_a = (nat4(dl_a[0]),) + tuple(dl_a[1:])
        da = []
        for ci in range(3):
            dq, dk, dv, dbias = _band_bwd(s["qa"][ci], A_Q, A_K, A_V, bias_a[ci], None, dy_a[ci], 0, lse_a[ci],
                                          dl_a[ci], name=f"band_a{ci}_bwd", **A_HEADS)
            if ci == 0:
                dq, dk, dv = (a.reshape(t, 256) for a in (dq, dk, dv))
            da.append((dq, dk, dv))
            dbias_a[ci].append(dbias.reshape(4, -1))
        dqb, dkb, dvb, dbias_b, dsink = _band_bwd(nat4(s["qb"]), B_Q, B_K, B_V, bias_b, w["sink_t"], nat4(dycat), 1,
                                                  nat4(s["lse_b"]), nat4(dl_b), name="band_b_bwd", **B_HEADS)
        dbias_bs.append(dbias_b.reshape(4, -1))
        g["sink"] = dsink[:, 0, 0]
        dd = _dense_bwd(s["qd"], dycat, s["lse_d"], dl_d, tq=128, name="dense_bwd")
        dcu, dcv, g["c_ws"], dbs, g["c_g"], g["c_b"] = _c_bwd(s["proj"], dycat, w["c_g"], w["c_b"], w["c_ws"],
                                                               w["c_wst"], w["c_bst"], tm=512, name="c_bwd")
        g["c_bs"] = dbs[:, ::64].T
        db = (dqb.reshape(t, 256), dkb.reshape(t, 128), dvb.reshape(t, 128))
        dproj, dgains = _qkprep_bwd(s["proj"], da, db, dd, dcu, dcv, w["qk_gains"], cos, sins, tm=512, name="qkprep_bwd")
        g["qk_gain"] = dgains[:6, :64].reshape(3, 2, HEAD_DIM)
        g["w_in"] = _mm(s["hn0"], dproj, "tn", None, tm=1024, tn=1152, out_dtype=BF16, name="dw_in")
        dx, g["g_mix"] = _mm_bt_normbwd((dproj,), w["w_in"], (s["x0"],), w["g_mix"], dx1, tm=1024, tn=1152,
                                        name="mix_in_bwd")
        grads[li] = g
        grads_ready(li, "end", g)
    d_table_a = sum(_bucket_reduce(dbias_a[ci], pats_a[ci], name=f"bucket_a{ci}") for ci in range(3))
    d_table_b = _bucket_reduce(dbias_bs, pat_b, name="bucket_b")
    d_rel_bias = jnp.concatenate([d_table_a, d_table_b], axis=0).T
    return loss_tile[0, 0], dx, grads, d_rel_bias


WEIGHT_NAMES = ("rel_bias", "ln_mix_g", "w_in", "qk_gain", "sink", "c_norm_g", "c_norm_b", "c_ws", "c_bs", "out_gain",
                "w_out", "ln_ffn_g", "w_up", "conv_w", "conv_b", "w_down", "ln_ple_g", "w_ple_gate", "w_ple_proj")
COL_SHARDED = ("w_in", "w_up", "w_ple_proj")
ROW_SHARDED = ("w_out", "w_down", "w_ple_gate")
SMALL_SHARDED = ("conv_w", "out_gain")
REPLICATED = tuple(n for n in WEIGHT_NAMES if n not in COL_SHARDED + ROW_SHARDED + SMALL_SHARDED)
LOCAL_GRAD_KEY = {"ln_mix_g": "g_mix", "ln_ffn_g": "g_ffn", "ln_ple_g": "g_ple", "c_norm_g": "c_g", "c_norm_b": "c_b",
                  "w_ple_gate": "w_gate", "w_ple_proj": "w_proj"}


def _full_from_gathered(name, gathered):
    _, r, c = gathered.shape
    if name in ROW_SHARDED:
        return gathered.reshape(N_DEV * r, c)
    return jnp.transpose(gathered, (1, 0, 2)).reshape(r, N_DEV * c)


def _slots_from_full(name, full):
    rows, cols = full.shape
    if name in ROW_SHARDED:
        return full.reshape(N_DEV, rows // N_DEV, cols)
    return jnp.transpose(full.reshape(rows, N_DEV, cols // N_DEV), (1, 0, 2))


def _piece_rows(shape):
    return -(-int(np.prod(shape)) // 1024) * 8


def _pack_rows(arrays):
    pieces = []
    for a in arrays:
        n, rows = int(np.prod(a.shape)), _piece_rows(a.shape)
        flat = a.astype(F32).reshape(-1)
        if n != rows * LANES:
            flat = jnp.pad(flat, (0, rows * LANES - n))
        pieces.append(flat.reshape(rows, LANES))
    return jnp.concatenate(pieces, axis=0)


def _unpack_rows(packed, shapes):
    out, off = [], 0
    for shp in shapes:
        n, rows = int(np.prod(shp)), _piece_rows(shp)
        piece = packed[off:off + rows]
        out.append((piece if n == rows * LANES else piece.reshape(-1)[:n]).reshape(shp))
        off += rows
    return out


def kernel(x, p, rel_bias, ln_mix_g, w_in, qk_gain, sink, c_norm_g, c_norm_b, c_ws, c_bs, out_gain, w_out, ln_ffn_g, w_up, conv_w, conv_b, w_down, ln_ple_g, w_ple_gate, w_ple_proj, loss_target, m_rel_bias, m_ln_mix_g, m_w_in, m_qk_gain, m_sink, m_c_norm_g, m_c_norm_b, m_c_ws, m_c_bs, m_out_gain, m_w_out, m_ln_ffn_g, m_w_up, m_conv_w, m_conv_b, m_w_down, m_ln_ple_g, m_w_ple_gate, m_w_ple_proj, v_rel_bias, v_ln_mix_g, v_w_in, v_qk_gain, v_sink, v_c_norm_g, v_c_norm_b, v_c_ws, v_c_bs, v_out_gain, v_w_out, v_ln_ffn_g, v_w_up, v_conv_w, v_conv_b, v_w_down, v_ln_ple_g, v_w_ple_gate, v_w_ple_proj):
    env = dict(locals())
    wt = {n: env[n] for n in WEIGHT_NAMES}
    mom_m = {n: env["m_" + n] for n in WEIGHT_NAMES}
    mom_v = {n: env["v_" + n] for n in WEIGHT_NAMES}
    bl = x.shape[0]
    t = bl * SEQ
    me = 4 * lax.axis_index("x") + 2 * lax.axis_index("y") + lax.axis_index("c")

    big = COL_SHARDED + ROW_SHARDED
    full = {}
    small_shapes = [wt[n].shape for n in SMALL_SHARDED]
    small = _allgather_vmem(_pack_rows([wt[n] for n in SMALL_SHARDED]), name="gather_small")
    small = small.reshape(N_DEV, -1)
    off = 0
    for n, shp in zip(SMALL_SHARDED, small_shapes):
        cnt = int(np.prod(shp))
        g = small[:, off:off + cnt].reshape((N_DEV,) + tuple(shp))
        full[n] = jnp.transpose(g, (1, 2, 0, 3)).reshape(shp[0], shp[1], N_DEV * shp[2])
        off += _piece_rows(shp) * LANES

    def head_gain(li, a, b, reps):
        g = jnp.tile(qk_gain[li, a, b], reps)
        return jnp.pad(g, (0, 256 - g.shape[0]))

    wts = []
    for li in range(DEPTH):
        rows = [head_gain(li, 0, 0, 4), head_gain(li, 0, 1, 4), head_gain(li, 1, 0, 4), head_gain(li, 1, 1, 2),
                head_gain(li, 2, 0, 4), head_gain(li, 2, 1, 2), jnp.zeros((256,), F32), jnp.zeros((256,), F32)]
        wts.append(dict(
            g_mix=ln_mix_g[li].reshape(1, -1), qk_gains=jnp.stack(rows),
            sink_t=jnp.broadcast_to(sink[li][:, None, None], (4, 8, 128)),
            c_g=c_norm_g[li].reshape(1, -1), c_b=c_norm_b[li].reshape(1, -1), c_ws=c_ws[li].astype(BF16),
            c_wst=jnp.transpose(c_ws[li], (0, 2, 1)).astype(BF16), c_bst=jnp.repeat(c_bs[li].T, 64, axis=1),
            out_gain=full["out_gain"][li].reshape(1, -1), g_ffn=ln_ffn_g[li].reshape(1, -1),
            conv_w=full["conv_w"][li], conv_b=conv_b[li].reshape(1, -1), g_ple=ln_ple_g[li].reshape(1, -1)))

    local_key = {"w_ple_gate": "w_gate", "w_ple_proj": "w_proj"}

    gather_names = {"in": ("w_in",), "rest": tuple(n for n in big if n != "w_in")}
    gathered = {}
    for cid, (li, names) in enumerate(((0, gather_names["in"]), (0, gather_names["rest"]), (1, big))):
        lands = _sc_allgather([wt[n][li].astype(BF16) for n in names], collective_id=cid,
                              name=f"gather_{li}_{len(names)}")
        gathered.setdefault(li, {}).update(zip(names, lands))

    def matmul_weights(li, part, after):
        out = {}
        for n in gather_names[part]:
            g, _ = lax.optimization_barrier((gathered[li][n], after))
            out[local_key.get(n, n)] = _full_from_gathered(n, g)
        return out

    mid_names = ("w_ple_gate", "w_ple_proj", "w_down", "w_up", "w_out")
    end_names = ("w_in",)
    landed = {}

    def start_exchange(li, names, g, tag, cid):
        slots = [_slots_from_full(n, g[local_key.get(n, n)]) for n in names]
        lands = _sc_exchange(slots, scatter=True, collective_id=cid, name=f"grads_{li}_{tag}")
        landed.update({(n, li): land for n, land in zip(names, lands)})

    def grads_ready(li, stage, g):
        if li == 0:
            start_exchange(li, mid_names if stage == "mid" else end_names, g, stage, 5 if stage == "mid" else 6)
        elif stage == "end":
            start_exchange(li, mid_names + end_names, g, stage, 4)

    loss_part, dx, grads, d_rel_bias = _local_step(
        x.reshape(t, D_MODEL), p.reshape(DEPTH * t, PLE_DIM), loss_target.reshape(t, D_MODEL), rel_bias, wts,
        matmul_weights, grads_ready)

    def local_grad(n):
        if n == "rel_bias":
            return d_rel_bias
        key = LOCAL_GRAD_KEY.get(n, n)
        return jnp.stack([grads[li][key].reshape(wt[n].shape[1:]) if n in REPLICATED else grads[li][key]
                          for li in range(DEPTH)])

    small_names = REPLICATED + SMALL_SHARDED
    small_full_shapes = [wt[n].shape if n in REPLICATED else full[n].shape for n in small_names]
    small_parts = _allgather_vmem(_pack_rows([local_grad(n) for n in small_names] + [loss_part.reshape(1)]),
                                  name="allgather_small_grads")
    small_parts = small_parts.reshape(N_DEV, -1, LANES)

    out_g, out_d, out_m, out_v = {}, {}, {}, {}
    for n in big:
        shp = wt[n].shape
        two_d = lambda a: a.reshape(-1, shp[-1])
        res = _adamw_reduce([landed[n, li] for li in range(DEPTH)], two_d(wt[n]), two_d(mom_m[n]), two_d(mom_v[n]),
                            tr=32 if n == "w_down" else 128, name="adamw_" + n)
        out_g[n], out_d[n], out_m[n], out_v[n] = [r.reshape(shp) for r in res]

    *reduced, loss = _unpack_rows(_sum_slots(small_parts, name="sum_small_grads"), small_full_shapes + [(1,)])
    loss = loss[0]
    reduced = dict(zip(small_names, reduced))
    rep_shapes = [wt[n].shape for n in REPLICATED]
    upd = _adamw_plain(_pack_rows([reduced[n] for n in REPLICATED]), _pack_rows([wt[n] for n in REPLICATED]),
                       _pack_rows([mom_m[n] for n in REPLICATED]), _pack_rows([mom_v[n] for n in REPLICATED]),
                       name="adamw_replicated")
    for dst, packed in zip((out_d, out_m, out_v), upd):
        dst.update(zip(REPLICATED, _unpack_rows(packed, rep_shapes)))
    for n in REPLICATED:
        out_g[n] = reduced[n]
    for n in SMALL_SHARDED:
        shp = wt[n].shape
        g = reduced[n].reshape(shp[0], shp[1], N_DEV, shp[2])
        g = lax.dynamic_index_in_dim(g, me, axis=2, keepdims=False)
        two_d = lambda a: a.reshape(-1, shp[-1])
        res = _adamw_plain(two_d(g), two_d(wt[n]), two_d(mom_m[n]), two_d(mom_v[n]), name="adamw_" + n)
        out_g[n] = g
        out_d[n], out_m[n], out_v[n] = [r.reshape(shp) for r in res]

    return (loss, dx.reshape(bl, SEQ, D_MODEL), *[out_g[n] for n in WEIGHT_NAMES], *[out_d[n] for n in WEIGHT_NAMES],
            *[out_m[n] for n in WEIGHT_NAMES], *[out_v[n] for n in WEIGHT_NAMES])
```

```python
import math

import jax
import jax.numpy as jnp
import numpy as np
from jax import lax
from jax.experimental import pallas as pl
from jax.experimental.pallas import tpu as pltpu
from jax.experimental.pallas import tpu_sc as plsc

F32 = jnp.float32
BF16 = jnp.bfloat16

N_DEV = 8
D_MODEL = 1024
SEQ = 2048
DEPTH = 2
HEAD_DIM = 64
IN_WIDTH = 2304
D_FF = 2816
PLE_DIM = 256
C_CHUNK = 128
C_GROUPS = 4
DILATED_CFGS = ((128, 1), (512, 4), (2048, 16))
DILATIONS = tuple(d for _, d in DILATED_CFGS)
A_RADIUS = 64
SWA_RADIUS = 128
BAND_BLOCK = 256
GRID_W = 64
ROPE_THETA = 10000.0
REL_BUCKETS = 32
REL_MAX_DIST = 1024
EPS = 1e-6
NEG_INF = -1e30
ATTN_SCALE = HEAD_DIM ** -0.5
LANES = 128

ADAM_LR = 0.001
ADAM_B1 = 0.9
ADAM_B2 = 0.999
ADAM_EPS = 1e-08
ADAM_WD = 0.01
ADAM_STEP = 10

MESH = pl.DeviceIdType.MESH
NT = (((1,), (1,)), ((), ()))
TN = (((0,), (0,)), ((), ()))
ARB = "arbitrary"
PAR = "parallel"


def _cparams(*sem):
    return pltpu.CompilerParams(dimension_semantics=tuple(sem))


def _sds(shape, dtype):
    return jax.ShapeDtypeStruct(tuple(shape), dtype)


def _group_sum_matrix(n, same_group):
    r = lax.broadcasted_iota(jnp.int32, (n, n), 0)
    c = lax.broadcasted_iota(jnp.int32, (n, n), 1)
    if same_group:
        return ((r >> 6) == (c >> 6)).astype(F32)
    return ((r & 63) == (c & 63)).astype(F32)


def _seg_sum(x, e):
    eb = e.astype(BF16)
    hi = x.astype(BF16)
    lo = (x - hi.astype(F32)).astype(BF16)
    return jnp.dot(hi, eb, preferred_element_type=F32) + jnp.dot(lo, eb, preferred_element_type=F32)


def _gelu(x):
    c = math.sqrt(2.0 / math.pi)
    return 0.5 * x * (1.0 + jnp.tanh(c * (x + 0.044715 * (x * x * x))))


def _gelu_grad(x):
    c = math.sqrt(2.0 / math.pi)
    t = jnp.tanh(c * (x + 0.044715 * (x * x * x)))
    return 0.5 * (1.0 + t) + 0.5 * x * (1.0 - t * t) * c * (1.0 + 3.0 * 0.044715 * (x * x))


def _sigmoid(x):
    return 1.0 / (1.0 + jnp.exp(-x))


def _scatter_cols(scratch, first, val):
    for c in range(val.shape[1] // LANES):
        scratch[first + c] = val[:, c * LANES:(c + 1) * LANES]


def _gather_cols(scratch, first, ncol):
    return jnp.concatenate([scratch[first + c] for c in range(ncol)], axis=1)


def _read_residue(scratch, first, ncol, r, d):
    n = scratch.shape[1] // d
    return jnp.concatenate([scratch.at[first + c][pl.ds(r, n, stride=d), :] for c in range(ncol)], axis=1)


def _write_residue(scratch, first, r, d, val):
    n = scratch.shape[1] // d
    for c in range(val.shape[1] // LANES):
        scratch.at[first + c][pl.ds(r, n, stride=d), :] = val[:, c * LANES:(c + 1) * LANES]


def _norm_mm(xs, gain, w, res, *, tm, tn, name, out_dtype=F32):
    t = xs[0].shape[0]
    k = sum(x.shape[1] for x in xs)
    n = w.shape[1]
    ng = len(xs)
    has_res = res is not None

    def body(*refs):
        x_refs = refs[:ng]
        g_ref, w_ref = refs[ng], refs[ng + 1]
        res_ref = refs[ng + 2] if has_res else None
        hn_ref, o_ref, hn_s = refs[ng + 2 + has_res:]

        @pl.when(pl.program_id(1) == 0)
        def _():
            off = 0
            for xr in x_refs:
                x = xr[...]
                wd = x.shape[1]
                r = lax.rsqrt(jnp.mean(x * x, axis=-1, keepdims=True) + EPS)
                hn_s[:, off:off + wd] = (x * r * g_ref[:, off:off + wd]).astype(BF16)
                off += wd
            hn_ref[...] = hn_s[...]

        acc = jnp.dot(hn_s[...], w_ref[...], preferred_element_type=F32)
        if has_res:
            acc = acc + res_ref[...]
        o_ref[...] = acc.astype(out_dtype)

    in_specs = [pl.BlockSpec((tm, x.shape[1]), lambda i, j: (i, 0)) for x in xs]
    in_specs += [pl.BlockSpec((1, k), lambda i, j: (0, 0)), pl.BlockSpec((k, tn), lambda i, j: (0, j))]
    args = list(xs) + [gain, w]
    if has_res:
        in_specs.append(pl.BlockSpec((tm, tn), lambda i, j: (i, j)))
        args.append(res)
    return pl.pallas_call(
        body, name=name, grid=(t // tm, n // tn), in_specs=in_specs,
        out_specs=[pl.BlockSpec((tm, k), lambda i, j: (i, 0)), pl.BlockSpec((tm, tn), lambda i, j: (i, j))],
        out_shape=[_sds((t, k), BF16), _sds((t, n), out_dtype)],
        scratch_shapes=[pltpu.VMEM((tm, k), BF16)],
        compiler_params=_cparams(PAR, ARB),
    )(*args)


def _mm(a, b, mode, res, *, tm, tn, out_dtype, name, a_rows=None):
    if mode == "tn":
        kk, m = a.shape
        blk_a = 0
        if a_rows is not None:
            blk_a, kk = a_rows
        a_spec = pl.BlockSpec((kk, tm), lambda i, j: (blk_a, i))
    else:
        m, kk = a.shape
        a_spec = pl.BlockSpec((tm, kk), lambda i, j: (i, 0))
    if mode == "nt":
        n = b.shape[0]
        b_spec = pl.BlockSpec((tn, kk), lambda i, j: (j, 0))
    else:
        n = b.shape[1]
        b_spec = pl.BlockSpec((kk, tn), lambda i, j: (0, j))
    has_res = res is not None

    def body(*refs):
        a_ref, b_ref = refs[0], refs[1]
        o_ref = refs[-1]
        av = a_ref[...].astype(BF16)
        bv = b_ref[...].astype(BF16)
        if mode == "nn":
            acc = jnp.dot(av, bv, preferred_element_type=F32)
        elif mode == "nt":
            acc = lax.dot_general(av, bv, NT, preferred_element_type=F32)
        else:
            acc = lax.dot_general(av, bv, TN, preferred_element_type=F32)
        if has_res:
            acc = acc + refs[2][...]
        o_ref[...] = acc.astype(out_dtype)

    in_specs = [a_spec, b_spec]
    args = [a, b]
    if has_res:
        in_specs.append(pl.BlockSpec((tm, tn), lambda i, j: (i, j)))
        args.append(res)
    return pl.pallas_call(
        body, name=name, grid=(m // tm, n // tn), in_specs=in_specs,
        out_specs=pl.BlockSpec((tm, tn), lambda i, j: (i, j)),
        out_shape=_sds((m, n), out_dtype),
        compiler_params=_cparams(PAR, PAR),
    )(*args)


def _mm_bt_normbwd(dys, w, xs, gain, dres, *, tm, tn, name, emit_bf16=False):
    t, wd_each = dys[0].shape
    nd = len(dys)
    per = wd_each // tn
    nj = nd * per
    k = w.shape[0]
    ng = len(xs)
    has_res = dres is not None

    def body(*refs):
        dy_refs = refs[:nd]
        w_ref = refs[nd]
        x_refs = refs[nd + 1:nd + 1 + ng]
        g_ref = refs[nd + 1 + ng]
        dres_ref = refs[nd + 2 + ng] if has_res else None
        outs = refs[nd + 2 + ng + has_res:]
        dx_ref = outs[0]
        dxb_ref = outs[1] if emit_bf16 else None
        dg_ref, acc = outs[1 + emit_bf16:]
        i, j = pl.program_id(0), pl.program_id(1)

        @pl.when(j == 0)
        def _():
            acc[...] = jnp.zeros_like(acc)

        for d, dy_ref in enumerate(dy_refs):
            @pl.when((j >= d * per) & (j < (d + 1) * per))
            def _(dy_ref=dy_ref):
                acc[...] += lax.dot_general(dy_ref[...].astype(BF16), w_ref[...], NT, preferred_element_type=F32)

        @pl.when(j == nj - 1)
        def _():
            @pl.when(i == 0)
            def _():
                dg_ref[...] = jnp.zeros_like(dg_ref)

            off = 0
            for xr in x_refs:
                x = xr[...]
                wd = x.shape[1]
                g = g_ref[:, off:off + wd]
                dyn = acc[:, off:off + wd]
                r = lax.rsqrt(jnp.mean(x * x, axis=-1, keepdims=True) + EPS)
                gdy = dyn * g
                dx = r * gdy - x * (r * r * r * jnp.mean(gdy * x, axis=-1, keepdims=True))
                if has_res:
                    dx = dx + dres_ref[:, off:off + wd]
                dx_ref[:, off:off + wd] = dx
                if emit_bf16:
                    dxb_ref[:, off:off + wd] = dx.astype(BF16)
                dg_ref[:, off:off + wd] += jnp.sum(dyn * x * r, axis=0, keepdims=True)
                off += wd

    def dy_map(d):
        return lambda i, j: (i, jnp.clip(j - d * per, 0, per - 1))

    in_specs = [pl.BlockSpec((tm, tn), dy_map(d)) for d in range(nd)]
    in_specs.append(pl.BlockSpec((k, tn), lambda i, j: (0, j)))
    in_specs += [pl.BlockSpec((tm, x.shape[1]), lambda i, j: (i, 0)) for x in xs]
    in_specs.append(pl.BlockSpec((1, k), lambda i, j: (0, 0)))
    args = list(dys) + [w] + list(xs) + [gain]
    if has_res:
        in_specs.append(pl.BlockSpec((tm, k), lambda i, j: (i, 0)))
        args.append(dres)
    row = pl.BlockSpec((tm, k), lambda i, j: (i, 0))
    out_specs = [row] + ([row] if emit_bf16 else []) + [pl.BlockSpec((1, k), lambda i, j: (0, 0))]
    out_shape = [_sds((t, k), F32)] + ([_sds((t, k), BF16)] if emit_bf16 else []) + [_sds((1, k), F32)]
    return pl.pallas_call(
        body, name=name, grid=(t // tm, nj), in_specs=in_specs, out_specs=out_specs, out_shape=out_shape,
        scratch_shapes=[pltpu.VMEM((tm, k), F32)],
        compiler_params=_cparams(ARB, ARB),
    )(*args)


def _rope_partner(y):
    n = y.shape[1]
    lane = lax.broadcasted_iota(jnp.int32, y.shape, 1)
    return jnp.where((lane & 31) < 16, pltpu.roll(y, n - 16, 1), pltpu.roll(y, 16, 1))


def _residue_specs(tm, width, nt):
    specs = [pl.BlockSpec((tm, width), lambda b, i: (b * nt + i, 0))]
    for d in DILATIONS[1:]:
        specs.append(pl.BlockSpec((None, d, tm // d, width), lambda b, i: (b, 0, i, 0)))
    return specs


def _residue_shapes(bl, width, dtype):
    return [_sds((bl * SEQ, width), dtype)] + [_sds((bl, d, SEQ // d, width), dtype) for d in DILATIONS[1:]]


def _qkprep_fwd(proj, gains, cos, sins, *, tm, name):
    t = proj.shape[0]
    bl = t // SEQ
    nt = SEQ // tm

    def body(p_ref, g_ref, c_ref, s_ref, qa1_ref, qa4_ref, qa16_ref, qb_ref, qd_ref, scr):
        e = _group_sum_matrix(256, True)

        def hn(x, row):
            x = x.astype(F32)
            wd = x.shape[1]
            ms = _seg_sum(x * x, e[:wd, :wd]) * (1.0 / HEAD_DIM)
            return x * lax.rsqrt(ms + EPS) * g_ref[row:row + 1, :wd]

        qa = jnp.concatenate([hn(p_ref[:, 0:256], 0) * ATTN_SCALE, hn(p_ref[:, 256:512], 1),
                              p_ref[:, 512:768].astype(F32)], axis=1)
        qa1_ref[...] = qa.astype(BF16)
        _scatter_cols(scr, 0, qa)
        for d, ref in ((4, qa4_ref), (16, qa16_ref)):
            for r in range(d):
                ref[r] = _read_residue(scr, 0, 6, r, d).astype(BF16)
        qb_ref[:, 0:256] = (hn(p_ref[:, 768:1024], 2) * ATTN_SCALE).astype(BF16)
        qb_ref[:, 256:384] = hn(p_ref[:, 1024:1152], 3).astype(BF16)
        qb_ref[:, 384:512] = p_ref[:, 1152:1280].astype(BF16)
        yq = hn(p_ref[:, 1792:2048], 4)
        yq = yq * c_ref[...] + _rope_partner(yq) * s_ref[...]
        qd_ref[:, 0:256] = (yq * ATTN_SCALE).astype(BF16)
        yk = hn(p_ref[:, 2048:2176], 5)
        yk = yk * c_ref[:, 0:128] + _rope_partner(yk) * s_ref[:, 0:128]
        qd_ref[:, 256:384] = yk.astype(BF16)
        qd_ref[:, 384:512] = p_ref[:, 2176:2304].astype(BF16)

    row = lambda width: pl.BlockSpec((tm, width), lambda b, i: (b * nt + i, 0))
    tab = pl.BlockSpec((tm, 256), lambda b, i: (i, 0))
    return pl.pallas_call(
        body, name=name, grid=(bl, nt),
        in_specs=[row(IN_WIDTH), pl.BlockSpec((8, 256), lambda b, i: (0, 0)), tab, tab],
        out_specs=_residue_specs(tm, 768, nt) + [row(512), row(512)],
        out_shape=_residue_shapes(bl, 768, BF16) + [_sds((t, 512), BF16), _sds((t, 512), BF16)],
        scratch_shapes=[pltpu.VMEM((6, tm, LANES), F32)],
        compiler_params=_cparams(PAR, PAR),
    )(proj, gains, cos, sins)


def _qkprep_bwd(proj, da, db, dd, dcu, dcv, gains, cos, sins, *, tm, name):
    t = proj.shape[0]
    bl = t // SEQ
    nt = SEQ // tm
    flat = [a for cfg in da for a in cfg] + list(db) + list(dd) + [dcu, dcv]

    def body(*refs):
        p_ref, g_ref, c_ref, s_ref = refs[:4]
        d_refs = refs[4:4 + len(flat)]
        dp_ref, dg_ref, scr = refs[4 + len(flat):]
        a_refs = d_refs[:9]
        dqb_ref, dkb_ref, dvb_ref, dqd_ref, dkd_ref, dvd_ref, dcu_ref, dcv_ref = d_refs[9:]
        e = _group_sum_matrix(256, True)
        first = (pl.program_id(0) == 0) & (pl.program_id(1) == 0)
        last = (pl.program_id(0) == bl - 1) & (pl.program_id(1) == nt - 1)

        @pl.when(first)
        def _():
            dg_ref[...] = jnp.zeros_like(dg_ref)

        def hn_bwd(x, dy, row):
            x, dy = x.astype(F32), dy.astype(F32)
            wd = x.shape[1]
            ee = e[:wd, :wd]
            g = g_ref[row:row + 1, :wd]
            r = lax.rsqrt(_seg_sum(x * x, ee) * (1.0 / HEAD_DIM) + EPS)
            gdy = dy * g
            dx = r * gdy - x * (r * r * r * (_seg_sum(gdy * x, ee) * (1.0 / HEAD_DIM)))
            dg_ref[row:row + 1, :wd] += jnp.sum(dy * x * r, axis=0, keepdims=True)
            return dx

        def rope_bwd(dy, wd):
            dy = dy.astype(F32)
            return dy * c_ref[:, :wd] + _rope_partner(dy * s_ref[:, :wd])

        dqkv = jnp.concatenate([a_refs[m][...].astype(F32) for m in range(3)], axis=1)
        for ci, d in ((1, 4), (2, 16)):
            for r in range(d):
                part = jnp.concatenate([a_refs[3 * ci + m][r].astype(F32) for m in range(3)], axis=1)
                _write_residue(scr, 0, r, d, part)
            dqkv = dqkv + _gather_cols(scr, 0, 6)
        dp_ref[:, 0:256] = hn_bwd(p_ref[:, 0:256], dqkv[:, 0:256] * ATTN_SCALE, 0).astype(BF16)
        dp_ref[:, 256:512] = hn_bwd(p_ref[:, 256:512], dqkv[:, 256:512], 1).astype(BF16)
        dp_ref[:, 512:768] = dqkv[:, 512:768].astype(BF16)
        dp_ref[:, 768:1024] = hn_bwd(p_ref[:, 768:1024], dqb_ref[...] * ATTN_SCALE, 2).astype(BF16)
        dp_ref[:, 1024:1152] = hn_bwd(p_ref[:, 1024:1152], dkb_ref[...], 3).astype(BF16)
        dp_ref[:, 1152:1280] = dvb_ref[...].astype(BF16)
        dp_ref[:, 1280:1536] = dcu_ref[...].astype(BF16)
        dp_ref[:, 1536:1792] = dcv_ref[...].astype(BF16)
        dp_ref[:, 1792:2048] = hn_bwd(p_ref[:, 1792:2048], rope_bwd(dqd_ref[...] * ATTN_SCALE, 256), 4).astype(BF16)
        dp_ref[:, 2048:2176] = hn_bwd(p_ref[:, 2048:2176], rope_bwd(dkd_ref[...], 128), 5).astype(BF16)
        dp_ref[:, 2176:2304] = dvd_ref[...].astype(BF16)

        @pl.when(last)
        def _():
            dg_ref[...] = _seg_sum(dg_ref[...], _group_sum_matrix(256, False))

    row = lambda width: pl.BlockSpec((tm, width), lambda b, i: (b * nt + i, 0))
    tab = pl.BlockSpec((tm, 256), lambda b, i: (i, 0))
    in_specs = [row(IN_WIDTH), pl.BlockSpec((8, 256), lambda b, i: (0, 0)), tab, tab]
    res_specs = _residue_specs(tm, 256, nt)
    in_specs += [res_specs[ci] for ci in range(3) for _ in range(3)]
    in_specs += [row(a.shape[1]) for a in flat[9:]]
    return pl.pallas_call(
        body, name=name, grid=(bl, nt), in_specs=in_specs,
        out_specs=[row(IN_WIDTH), pl.BlockSpec((8, 256), lambda b, i: (0, 0))],
        out_shape=[_sds((t, IN_WIDTH), BF16), _sds((8, 256), F32)],
        scratch_shapes=[pltpu.VMEM((6, tm, LANES), F32)],
        compiler_params=_cparams(ARB, ARB),
    )(proj, gains, cos, sins, *flat)


BAND_ROWS_PER_STEP = 512


def _residues_per_step(dil, seq_len):
    return min(dil, max(1, BAND_ROWS_PER_STEP // seq_len))


def _band_spec(seq_len, spec, rb):
    width, idx = spec
    return pl.BlockSpec((None, rb, seq_len, width), lambda b, r: (b, r, 0, idx))


def _fill_padded(dst, src_ref, rad, seq_len):
    z = jnp.zeros((rad, dst.shape[1]), dst.dtype)
    dst[0:rad, :] = z
    dst[rad + seq_len:rad + seq_len + rad, :] = z
    dst[rad:rad + seq_len, :] = src_ref[...]


def _band_fwd(src, qs, ks, vs, bias, sink, *, rad, nh, nkv, name):
    bl, dil, sl, _ = src.shape
    blk = bias.shape[1]
    kw = blk + 2 * rad
    nb = sl // blk
    rep = nh // nkv
    has_sink = sink is not None
    rb = _residues_per_step(dil, sl)

    def body(*refs):
        q_all, k_all, v_all, b_ref = refs[:4]
        s_ref = refs[4] if has_sink else None
        o_all, l_all, kp, vp = refs[4 + has_sink:]
        for ri in range(rb):
            one_sequence(q_all.at[ri], k_all.at[ri], v_all.at[ri], b_ref, s_ref, o_all.at[ri], l_all.at[ri], kp, vp)

    def one_sequence(q_ref, k_ref, v_ref, b_ref, s_ref, o_ref, l_ref, kp, vp):
        _fill_padded(kp, k_ref, rad, sl)
        _fill_padded(vp, v_ref, rad, sl)

        def blk_body(i, carry):
            r0 = pl.multiple_of(i * blk, blk)
            qb = q_ref[pl.ds(r0, blk), :]
            kwin = kp[pl.ds(r0, kw), :]
            vwin = vp[pl.ds(r0, kw), :]
            col = r0 - rad + lax.broadcasted_iota(jnp.int32, (blk, kw), 1)
            neg = jnp.where((col >= 0) & (col < sl), 0.0, NEG_INF).astype(F32)
            for h in range(nh):
                g = h // rep
                hs = slice(h * HEAD_DIM, (h + 1) * HEAD_DIM)
                gs = slice(g * HEAD_DIM, (g + 1) * HEAD_DIM)
                s = lax.dot_general(qb[:, hs], kwin[:, gs], NT, preferred_element_type=F32)
                s = s + b_ref[h] + neg
                m = jnp.max(s, axis=1, keepdims=True)
                if has_sink:
                    sk = s_ref[h][0:1, 0:1]
                    m = jnp.maximum(m, sk)
                p = jnp.exp(s - m)
                den = jnp.sum(p, axis=1, keepdims=True)
                if has_sink:
                    den = den + jnp.exp(sk - m)
                o = jnp.dot(p.astype(BF16), vwin[:, gs], preferred_element_type=F32) / den
                o_ref[pl.ds(r0, blk), hs] = o
                l_ref[pl.ds(r0, blk), hs] = jnp.broadcast_to(m + jnp.log(den), (blk, HEAD_DIM))
            return carry

        lax.fori_loop(0, nb, blk_body, 0)

    in_specs = [_band_spec(sl, qs, rb), _band_spec(sl, ks, rb), _band_spec(sl, vs, rb),
                pl.BlockSpec((nh, blk, kw), lambda b, r: (0, 0, 0))]
    args = [src] * 3 + [bias]
    if has_sink:
        in_specs.append(pl.BlockSpec((nh, 8, 128), lambda b, r: (0, 0, 0)))
        args.append(sink)
    return pl.pallas_call(
        body, name=name, grid=(bl, dil // rb), in_specs=in_specs,
        out_specs=[_band_spec(sl, (256, 0), rb)] * 2,
        out_shape=[_sds((bl, dil, sl, 256), F32)] * 2,
        scratch_shapes=[pltpu.VMEM((sl + 2 * rad, ks[0]), BF16), pltpu.VMEM((sl + 2 * rad, vs[0]), BF16)],
        compiler_params=_cparams(PAR, PAR),
    )(*args)


def _band_bwd(src, qs, ks, vs, bias, sink, dy, dcol, lse, delta, *, rad, nh, nkv, name):
    bl, dil, sl, _ = src.shape
    blk = bias.shape[1]
    kw = blk + 2 * rad
    nb = sl // blk
    rep = nh // nkv
    has_sink = sink is not None
    rb = _residues_per_step(dil, sl)
    wk, wv = ks[0], vs[0]

    def body(*refs):
        q_all, k_all, v_all, b_ref = refs[:4]
        s_ref = refs[4] if has_sink else None
        do_all, l_all, dl_all = refs[4 + has_sink:7 + has_sink]
        outs = refs[7 + has_sink:]
        dsk_ref = None
        if has_sink:
            dq_all, dk_all, dv_all, db_ref, dsk_ref, kp, vp, dka, dva = outs
        else:
            dq_all, dk_all, dv_all, db_ref, kp, vp, dka, dva = outs

        @pl.when((pl.program_id(0) == 0) & (pl.program_id(1) == 0))
        def _():
            db_ref[...] = jnp.zeros_like(db_ref)
            if has_sink:
                dsk_ref[...] = jnp.zeros_like(dsk_ref)

        for ri in range(rb):
            one_sequence(q_all.at[ri], k_all.at[ri], v_all.at[ri], b_ref, s_ref, do_all.at[ri], l_all.at[ri],
                         dl_all.at[ri], dq_all.at[ri], dk_all.at[ri], dv_all.at[ri], db_ref, dsk_ref, kp, vp, dka, dva)

    def one_sequence(q_ref, k_ref, v_ref, b_ref, s_ref, do_ref, l_ref, dl_ref, dq_ref, dk_ref, dv_ref, db_ref, dsk_ref,
                     kp, vp, dka, dva):
        _fill_padded(kp, k_ref, rad, sl)
        _fill_padded(vp, v_ref, rad, sl)
        dka[...] = jnp.zeros_like(dka)
        dva[...] = jnp.zeros_like(dva)

        def blk_body(i, carry):
            r0 = pl.multiple_of(i * blk, blk)
            qb = q_ref[pl.ds(r0, blk), :]
            kwin = kp[pl.ds(r0, kw), :]
            vwin = vp[pl.ds(r0, kw), :]
            dob = do_ref[pl.ds(r0, blk), :].astype(BF16)
            lb = l_ref[pl.ds(r0, blk), :]
            dlb = dl_ref[pl.ds(r0, blk), :]
            col = r0 - rad + lax.broadcasted_iota(jnp.int32, (blk, kw), 1)
            neg = jnp.where((col >= 0) & (col < sl), 0.0, NEG_INF).astype(F32)
            for h in range(nh):
                g = h // rep
                hs = slice(h * HEAD_DIM, (h + 1) * HEAD_DIM)
                gs = slice(g * HEAD_DIM, (g + 1) * HEAD_DIM)
                qh, kh, vh, doh = qb[:, hs], kwin[:, gs], vwin[:, gs], dob[:, hs]
                lh = lb[:, h * HEAD_DIM:h * HEAD_DIM + 1]
                dlh = dlb[:, h * HEAD_DIM:h * HEAD_DIM + 1]
                s = lax.dot_general(qh, kh, NT, preferred_element_type=F32) + b_ref[h] + neg
                p = jnp.exp(s - lh)
                dp = lax.dot_general(doh, vh, NT, preferred_element_type=F32)
                ds = p * (dp - dlh)
                dsb = ds.astype(BF16)
                dq_ref[pl.ds(r0, blk), hs] = jnp.dot(dsb, kh, preferred_element_type=F32).astype(BF16)
                dka[pl.ds(r0, kw), gs] += lax.dot_general(dsb, qh, TN, preferred_element_type=F32)
                dva[pl.ds(r0, kw), gs] += lax.dot_general(p.astype(BF16), doh, TN, preferred_element_type=F32)
                db_ref[h] += ds
                if has_sink:
                    ps = jnp.exp(s_ref[h][0:1, 0:1] - lh)
                    dsk_ref[h] += jnp.broadcast_to(-jnp.sum(ps * dlh, axis=0, keepdims=True), (8, 128))
            return carry

        lax.fori_loop(0, nb, blk_body, 0)
        dk_ref[...] = dka[rad:rad + sl, :].astype(BF16)
        dv_ref[...] = dva[rad:rad + sl, :].astype(BF16)

    const3 = lambda b, r: (0, 0, 0)
    in_specs = [_band_spec(sl, qs, rb), _band_spec(sl, ks, rb), _band_spec(sl, vs, rb),
                pl.BlockSpec((nh, blk, kw), const3)]
    args = [src] * 3 + [bias]
    if has_sink:
        in_specs.append(pl.BlockSpec((nh, 8, 128), const3))
        args.append(sink)
    row = _band_spec(sl, (256, 0), rb)
    in_specs += [_band_spec(sl, (256, dcol), rb), row, row]
    args += [dy, lse, delta]
    out_specs = [row, _band_spec(sl, (wk, 0), rb), _band_spec(sl, (wv, 0), rb), pl.BlockSpec((nh, blk, kw), const3)]
    out_shape = [_sds((bl, dil, sl, 256), BF16), _sds((bl, dil, sl, wk), BF16), _sds((bl, dil, sl, wv), BF16),
                 _sds((nh, blk, kw), F32)]
    if has_sink:
        out_specs.append(pl.BlockSpec((nh, 8, 128), const3))
        out_shape.append(_sds((nh, 8, 128), F32))
    return pl.pallas_call(
        body, name=name, grid=(bl, dil // rb), in_specs=in_specs, out_specs=out_specs, out_shape=out_shape,
        scratch_shapes=[pltpu.VMEM((sl + 2 * rad, wk), BF16), pltpu.VMEM((sl + 2 * rad, wv), BF16),
                        pltpu.VMEM((sl + 2 * rad, wk), F32), pltpu.VMEM((sl + 2 * rad, wv), F32)],
        compiler_params=_cparams(ARB, ARB),
    )(*args)


def _combine_a(os_, ls_, *, tm, name):
    bl = os_[1].shape[0]
    t = bl * SEQ
    nt = SEQ // tm

    def body(o1, o4, o16, l1, l4, l16, y_ref, lt_ref, scr):
        for k, (d, ref) in enumerate(((4, o4), (16, o16), (4, l4), (16, l16))):
            for r in range(d):
                _write_residue(scr, 2 * k, r, d, ref[r])
        o2, o3, b, c = (_gather_cols(scr, 2 * k, 2) for k in range(4))
        a = l1[...]
        m = jnp.maximum(jnp.maximum(a, b), c)
        ea, eb, ec = jnp.exp(a - m), jnp.exp(b - m), jnp.exp(c - m)
        den = ea + eb + ec
        y_ref[...] = (ea / den) * o1[...] + (eb / den) * o2 + (ec / den) * o3
        lt_ref[...] = m + jnp.log(den)

    specs = _residue_specs(tm, 256, nt)
    return pl.pallas_call(
        body, name=name, grid=(bl, nt), in_specs=specs * 2, out_specs=[specs[0]] * 2,
        out_shape=[_sds((t, 256), F32)] * 2, scratch_shapes=[pltpu.VMEM((8, tm, LANES), F32)],
        compiler_params=_cparams(PAR, PAR),
    )(*os_, *ls_)


def _deltas(dycat, ya, yb, yd, lse_a, *, tm, name):
    t = ya.shape[0]
    bl = t // SEQ
    nt = SEQ // tm

    def body(dy_ref, ya_ref, yb_ref, yd_ref, la_ref, dy4, dy16, l4, l16, da1, da4, da16, db_ref, dd_ref, scr):
        e = _group_sum_matrix(256, True)
        dya = dy_ref[:, 0:256]
        dla = _seg_sum(dya * ya_ref[...], e)
        da1[...] = dla
        db_ref[...] = _seg_sum(dy_ref[:, 256:512] * yb_ref[...], e)
        dd_ref[...] = _seg_sum(dy_ref[:, 768:1024] * yd_ref[...], e)
        for k, (val, r4, r16) in enumerate(((dya, dy4, dy16), (la_ref[...], l4, l16), (dla, da4, da16))):
            _scatter_cols(scr, 2 * k, val)
            for d, ref in ((4, r4), (16, r16)):
                for r in range(d):
                    ref[r] = _read_residue(scr, 2 * k, 2, r, d)

    specs = _residue_specs(tm, 256, nt)
    nat = specs[0]
    shapes = _residue_shapes(bl, 256, F32)
    outs = pl.pallas_call(
        body, name=name, grid=(bl, nt),
        in_specs=[pl.BlockSpec((tm, 1024), lambda b, i: (b * nt + i, 0)), nat, nat, nat, nat],
        out_specs=specs[1:] + specs[1:] + specs + [nat, nat],
        out_shape=shapes[1:] + shapes[1:] + shapes + [shapes[0], shapes[0]],
        scratch_shapes=[pltpu.VMEM((6, tm, LANES), F32)],
        compiler_params=_cparams(PAR, PAR),
    )(dycat, ya, yb, yd, lse_a)
    return outs[0:2], outs[2:4], outs[4:7], outs[7], outs[8]


def _dense_fwd(qd, *, tq, name):
    t = qd.shape[0]
    bl = t // SEQ
    nq = SEQ // tq

    def body(q_ref, k_ref, v_ref, o_ref, l_ref):
        q = q_ref[...]
        for g in range(2):
            h0, h1 = 2 * g, 2 * g + 1
            q2 = jnp.concatenate([q[:, h0 * 64:(h0 + 1) * 64], q[:, h1 * 64:(h1 + 1) * 64]], axis=0)
            kg = k_ref[:, g * 64:(g + 1) * 64]
            vg = v_ref[:, g * 64:(g + 1) * 64]
            s = lax.dot_general(q2, kg, NT, preferred_element_type=F32)
            m = jnp.max(s, axis=1, keepdims=True)
            p = jnp.exp(s - m)
            den = jnp.sum(p, axis=1, keepdims=True)
            o2 = jnp.dot(p.astype(BF16), vg, preferred_element_type=F32) / den
            l2 = jnp.broadcast_to(m + jnp.log(den), (2 * tq, 64))
            o_ref[:, h0 * 64:(h0 + 1) * 64] = o2[:tq]
            o_ref[:, h1 * 64:(h1 + 1) * 64] = o2[tq:]
            l_ref[:, h0 * 64:(h0 + 1) * 64] = l2[:tq]
            l_ref[:, h1 * 64:(h1 + 1) * 64] = l2[tq:]

    q3 = qd.reshape(bl, SEQ, 512)
    o, lse = pl.pallas_call(
        body, name=name, grid=(bl, nq),
        in_specs=[pl.BlockSpec((None, tq, 256), lambda b, i: (b, i, 0)),
                  pl.BlockSpec((None, SEQ, 128), lambda b, i: (b, 0, 2)),
                  pl.BlockSpec((None, SEQ, 128), lambda b, i: (b, 0, 3))],
        out_specs=[pl.BlockSpec((None, tq, 256), lambda b, i: (b, i, 0))] * 2,
        out_shape=[_sds((bl, SEQ, 256), F32)] * 2,
        compiler_params=_cparams(PAR, PAR),
    )(q3, q3, q3)
    return o.reshape(t, 256), lse.reshape(t, 256)


def _dense_bwd(qd, dycat, lse, delta, *, tq, name):
    t = qd.shape[0]
    bl = t // SEQ
    nq = SEQ // tq

    def body(q_ref, k_ref, v_ref, do_ref, l_ref, dl_ref, dq_ref, dk_ref, dv_ref, dkt, dvt):
        @pl.when(pl.program_id(1) == 0)
        def _():
            dkt[...] = jnp.zeros_like(dkt)
            dvt[...] = jnp.zeros_like(dvt)

        q = q_ref[...]
        do = do_ref[...].astype(BF16)
        lv = l_ref[...]
        dlv = dl_ref[...]
        for g in range(2):
            h0, h1 = 2 * g, 2 * g + 1
            q2 = jnp.concatenate([q[:, h0 * 64:(h0 + 1) * 64], q[:, h1 * 64:(h1 + 1) * 64]], axis=0)
            do2 = jnp.concatenate([do[:, h0 * 64:(h0 + 1) * 64], do[:, h1 * 64:(h1 + 1) * 64]], axis=0)
            l2 = jnp.concatenate([lv[:, h0 * 64:h0 * 64 + 1], lv[:, h1 * 64:h1 * 64 + 1]], axis=0)
            dl2 = jnp.concatenate([dlv[:, h0 * 64:h0 * 64 + 1], dlv[:, h1 * 64:h1 * 64 + 1]], axis=0)
            kg = k_ref[:, g * 64:(g + 1) * 64]
            vg = v_ref[:, g * 64:(g + 1) * 64]
            s = lax.dot_general(q2, kg, NT, preferred_element_type=F32)
            p = jnp.exp(s - l2)
            dp = lax.dot_general(do2, vg, NT, preferred_element_type=F32)
            ds = (p * (dp - dl2)).astype(BF16)
            dq2 = jnp.dot(ds, kg, preferred_element_type=F32)
            dq_ref[:, h0 * 64:(h0 + 1) * 64] = dq2[:tq].astype(BF16)
            dq_ref[:, h1 * 64:(h1 + 1) * 64] = dq2[tq:].astype(BF16)
            dkt[g * 64:(g + 1) * 64, :] += lax.dot_general(q2, ds, TN, preferred_element_type=F32)
            dvt[g * 64:(g + 1) * 64, :] += lax.dot_general(do2, p.astype(BF16), TN, preferred_element_type=F32)

        @pl.when(pl.program_id(1) == nq - 1)
        def _():
            dk_ref[...] = dkt[...].T.astype(BF16)
            dv_ref[...] = dvt[...].T.astype(BF16)

    q3 = qd.reshape(bl, SEQ, 512)
    tile = pl.BlockSpec((None, tq, 256), lambda b, i: (b, i, 0))
    full = pl.BlockSpec((None, SEQ, 128), lambda b, i: (b, 0, 0))
    dq, dk, dv = pl.pallas_call(
        body, name=name, grid=(bl, nq),
        in_specs=[tile, pl.BlockSpec((None, SEQ, 128), lambda b, i: (b, 0, 2)),
                  pl.BlockSpec((None, SEQ, 128), lambda b, i: (b, 0, 3)),
                  pl.BlockSpec((None, tq, 256), lambda b, i: (b, i, 3)), tile, tile],
        out_specs=[tile, full, full],
        out_shape=[_sds((bl, SEQ, 256), BF16), _sds((bl, SEQ, 128), BF16), _sds((bl, SEQ, 128), BF16)],
        scratch_shapes=[pltpu.VMEM((128, SEQ), F32), pltpu.VMEM((128, SEQ), F32)],
        compiler_params=_cparams(PAR, ARB),
    )(q3, q3, q3, dycat.reshape(bl, SEQ, 1024), lse.reshape(bl, SEQ, 256), delta.reshape(bl, SEQ, 256))
    return dq.reshape(t, 256), dk.reshape(t, 128), dv.reshape(t, 128)


def _c_norm(cv, gam, bet):
    vg = _gelu(cv)
    mu = jnp.mean(vg, axis=-1, keepdims=True)
    xc = vg - mu
    r = lax.rsqrt(jnp.mean(xc * xc, axis=-1, keepdims=True) + EPS)
    xhat = xc * r
    return xhat * gam + bet, xhat, r


def _c_fwd(proj, gam, bet, ws, bst, *, tm, name):
    t = proj.shape[0]
    nch = tm // C_CHUNK

    def body(u_ref, v_ref, g_ref, b_ref, ws_ref, bs_ref, y_ref):
        vn, _, _ = _c_norm(v_ref[...].astype(F32), g_ref[...], b_ref[...])
        vnb = vn.astype(BF16)
        for c in range(nch):
            rows = slice(c * C_CHUNK, (c + 1) * C_CHUNK)
            for g in range(C_GROUPS):
                gs = slice(g * 64, (g + 1) * 64)
                mixed = jnp.dot(ws_ref[g], vnb[rows, gs], preferred_element_type=F32) + bs_ref[:, gs]
                y_ref[rows, gs] = _gelu(u_ref[rows, gs].astype(F32)) * mixed

    vec = pl.BlockSpec((1, 256), lambda i: (0, 0))
    return pl.pallas_call(
        body, name=name, grid=(t // tm,),
        in_specs=[pl.BlockSpec((tm, 256), lambda i: (i, 5)), pl.BlockSpec((tm, 256), lambda i: (i, 6)), vec, vec,
                  pl.BlockSpec((C_GROUPS, C_CHUNK, C_CHUNK), lambda i: (0, 0, 0)),
                  pl.BlockSpec((C_CHUNK, 256), lambda i: (0, 0))],
        out_specs=pl.BlockSpec((tm, 256), lambda i: (i, 0)), out_shape=_sds((t, 256), F32),
        compiler_params=_cparams(PAR),
    )(proj, proj, gam, bet, ws, bst)


def _c_bwd(proj, dycat, gam, bet, ws, wst, bst, *, tm, name):
    t = proj.shape[0]
    nch = tm // C_CHUNK
    nstep = t // tm

    def body(u_ref, v_ref, dy_ref, g_ref, b_ref, ws_ref, wst_ref, bs_ref,
             du_ref, dv_ref, dws_ref, dbs_ref, dg_ref, db_ref, dvn_s):
        step = pl.program_id(0)

        @pl.when(step == 0)
        def _():
            dws_ref[...] = jnp.zeros_like(dws_ref)
            dbs_ref[...] = jnp.zeros_like(dbs_ref)
            dg_ref[...] = jnp.zeros_like(dg_ref)
            db_ref[...] = jnp.zeros_like(db_ref)

        cv = v_ref[...].astype(F32)
        gam_v = g_ref[...]
        vn, xhat, r = _c_norm(cv, gam_v, b_ref[...])
        vnb = vn.astype(BF16)
        for c in range(nch):
            rows = slice(c * C_CHUNK, (c + 1) * C_CHUNK)
            for g in range(C_GROUPS):
                gs = slice(g * 64, (g + 1) * 64)
                cu = u_ref[rows, gs].astype(F32)
                dy = dy_ref[rows, gs]
                mixed = jnp.dot(ws_ref[g], vnb[rows, gs], preferred_element_type=F32) + bs_ref[:, gs]
                du_ref[rows, gs] = (dy * mixed * _gelu_grad(cu)).astype(BF16)
                dmix = dy * _gelu(cu)
                dbs_ref[:, gs] += dmix
                dmb = dmix.astype(BF16)
                dws_ref[g] += lax.dot_general(dmb, vnb[rows, gs], NT, preferred_element_type=F32)
                dvn_s[rows, gs] = jnp.dot(wst_ref[g], dmb, preferred_element_type=F32)
        dvn = dvn_s[...]
        dg_ref[...] += jnp.sum(dvn * xhat, axis=0, keepdims=True)
        db_ref[...] += jnp.sum(dvn, axis=0, keepdims=True)
        dxh = dvn * gam_v
        dvg = r * (dxh - jnp.mean(dxh, axis=-1, keepdims=True) - xhat * jnp.mean(dxh * xhat, axis=-1, keepdims=True))
        dv_ref[...] = (dvg * _gelu_grad(cv)).astype(BF16)

        @pl.when(step == nstep - 1)
        def _():
            dbs_ref[...] = _seg_sum(dbs_ref[...], _group_sum_matrix(256, True))

    vec = pl.BlockSpec((1, 256), lambda i: (0, 0))
    mat = pl.BlockSpec((C_GROUPS, C_CHUNK, C_CHUNK), lambda i: (0, 0, 0))
    bsp = pl.BlockSpec((C_CHUNK, 256), lambda i: (0, 0))
    tile = pl.BlockSpec((tm, 256), lambda i: (i, 0))
    return pl.pallas_call(
        body, name=name, grid=(nstep,),
        in_specs=[pl.BlockSpec((tm, 256), lambda i: (i, 5)), pl.BlockSpec((tm, 256), lambda i: (i, 6)),
                  pl.BlockSpec((tm, 256), lambda i: (i, 2)), vec, vec, mat, mat, bsp],
        out_specs=[tile, tile, mat, bsp, vec, vec],
        out_shape=[_sds((t, 256), BF16), _sds((t, 256), BF16), _sds((C_GROUPS, C_CHUNK, C_CHUNK), F32),
                   _sds((C_CHUNK, 256), F32), _sds((1, 256), F32), _sds((1, 256), F32)],
        scratch_shapes=[pltpu.VMEM((tm, 256), F32)],
        compiler_params=_cparams(ARB),
    )(proj, proj, dycat, gam, bet, ws, wst, bst)


FF_TC = 128
FF_NB = D_FF // FF_TC
FF_CH = 64
FF_HALO = 16


def _taps(ref, r0, win, where):
    z = jnp.zeros((FF_HALO, win.shape[1]), F32)
    if where == "first":
        win[0:FF_HALO, :] = z
        win[FF_HALO:, :] = ref[0:FF_CH + FF_HALO, :].astype(F32)
    elif where == "last":
        win[0:FF_CH + FF_HALO, :] = ref[SEQ - FF_CH - FF_HALO:SEQ, :].astype(F32)
        win[FF_CH + FF_HALO:, :] = z
    else:
        win[...] = ref[pl.ds(pl.multiple_of(r0 - FF_HALO, FF_HALO), FF_CH + 2 * FF_HALO), :].astype(F32)
    return tuple(win[FF_HALO + o:FF_HALO + o + FF_CH, :] for o in (-1, 0, 1))


def _chunk_loop(step):
    step(0, lambda ref, win: _taps(ref, 0, win, "first"))

    def mid(i, carry):
        r0 = pl.multiple_of(i * FF_CH, FF_CH)
        step(r0, lambda ref, win: _taps(ref, r0, win, "mid"))
        return carry

    lax.fori_loop(1, SEQ // FF_CH - 1, mid, 0)
    step(SEQ - FF_CH, lambda ref, win: _taps(ref, SEQ - FF_CH, win, "last"))


def _conv3(taps, w_ref, b_ref):
    dn, md, up = taps
    return w_ref[0:1, :] * dn + w_ref[1:2, :] * md + w_ref[2:3, :] * up + b_ref[...]


def _ff_specs(order):
    def at(fn):
        return (lambda b, j: fn(b, j)) if order == "bj" else (lambda j, b: fn(b, j))
    hs = [pl.BlockSpec((None, SEQ, FF_TC), at(lambda b, j, o=o: (b, 0, j + o))) for o in (0, FF_NB)]
    ws = [pl.BlockSpec((3, FF_TC), at(lambda b, j, o=o: (0, j + o))) for o in (0, FF_NB)]
    bs = [pl.BlockSpec((1, FF_TC), at(lambda b, j, o=o: (0, j + o))) for o in (0, FF_NB)]
    return hs, ws, bs


def _conv_gate_fwd(h, cw, cb, *, name):
    t = h.shape[0]
    bl = t // SEQ

    def body(hg_ref, hu_ref, wg_ref, wu_ref, bg_ref, bu_ref, a_ref, cg_ref, cu_ref, win):
        def step(r0, taps):
            cg = _conv3(taps(hg_ref, win.at[0]), wg_ref, bg_ref)
            cu = _conv3(taps(hu_ref, win.at[1]), wu_ref, bu_ref)
            a_ref[pl.ds(r0, FF_CH), :] = (cg * _sigmoid(cg) * cu).astype(BF16)
            cg_ref[pl.ds(r0, FF_CH), :] = cg.astype(BF16)
            cu_ref[pl.ds(r0, FF_CH), :] = cu.astype(BF16)

        _chunk_loop(step)

    hs, ws, bs = _ff_specs("bj")
    h3 = h.reshape(bl, SEQ, 2 * D_FF)
    half = pl.BlockSpec((None, SEQ, FF_TC), lambda b, j: (b, 0, j))
    outs = pl.pallas_call(
        body, name=name, grid=(bl, FF_NB), in_specs=hs + ws + bs, out_specs=[half] * 3,
        out_shape=[_sds((bl, SEQ, D_FF), BF16)] * 3,
        scratch_shapes=[pltpu.VMEM((2, FF_CH + 2 * FF_HALO, FF_TC), F32)],
        compiler_params=_cparams(PAR, PAR),
    )(h3, h3, cw, cw, cb, cb)
    return [o.reshape(t, D_FF) for o in outs]


def _conv_gate_bwd(h, cg_all, cu_all, dact, cw, cb, *, name):
    t = h.shape[0]
    bl = t // SEQ

    def body(hg_ref, hu_ref, wg_ref, wu_ref, bg_ref, bu_ref, da_ref, cg_ref, cu_ref,
             dhg_ref, dhu_ref, dwg_ref, dwu_ref, dbg_ref, dbu_ref, dg_s, du_s, win, sums):
        @pl.when(pl.program_id(1) == 0)
        def _():
            for ref in (dwg_ref, dwu_ref, dbg_ref, dbu_ref):
                ref[...] = jnp.zeros_like(ref)

        sums[...] = jnp.zeros_like(sums)
        red = lambda x: jnp.sum(x.reshape(FF_CH // 8, 8, x.shape[1]), axis=0)

        def pass1(r0, taps):
            tg, tu = taps(hg_ref, win.at[0]), taps(hu_ref, win.at[1])
            cg = cg_ref[pl.ds(r0, FF_CH), :].astype(F32)
            cu = cu_ref[pl.ds(r0, FF_CH), :].astype(F32)
            da = da_ref[pl.ds(r0, FF_CH), :].astype(F32)
            sg = _sigmoid(cg)
            dcg = da * cu * (sg * (1.0 + cg * (1.0 - sg)))
            dcu = da * (cg * sg)
            dg_s[pl.ds(r0, FF_CH), :] = dcg
            du_s[pl.ds(r0, FF_CH), :] = dcu
            for half, (d, tp) in enumerate(((dcg, tg), (dcu, tu))):
                for k in range(3):
                    sums[4 * half + k] += red(d * tp[k])
                sums[4 * half + 3] += red(d)

        _chunk_loop(pass1)
        for half, (dw_ref, db_ref) in enumerate(((dwg_ref, dbg_ref), (dwu_ref, dbu_ref))):
            for k in range(3):
                dw_ref[k:k + 1, :] += jnp.sum(sums[4 * half + k], axis=0, keepdims=True)
            db_ref[...] += jnp.sum(sums[4 * half + 3], axis=0, keepdims=True)

        def pass2(r0, taps):
            for k, (s, w_ref, o_ref) in enumerate(((dg_s, wg_ref, dhg_ref), (du_s, wu_ref, dhu_ref))):
                dn, md, up = taps(s, win.at[k])
                o_ref[pl.ds(r0, FF_CH), :] = (w_ref[0:1, :] * up + w_ref[1:2, :] * md + w_ref[2:3, :] * dn).astype(BF16)

        _chunk_loop(pass2)

    hs, ws, bs = _ff_specs("jb")
    half = pl.BlockSpec((None, SEQ, FF_TC), lambda j, b: (b, 0, j))
    wsp = pl.BlockSpec((3, FF_TC), lambda j, b: (0, j))
    bsp = pl.BlockSpec((1, FF_TC), lambda j, b: (0, j))
    h3 = h.reshape(bl, SEQ, 2 * D_FF)
    dhg, dhu, dwg, dwu, dbg, dbu = pl.pallas_call(
        body, name=name, grid=(FF_NB, bl), in_specs=hs + ws + bs + [half] * 3,
        out_specs=[half, half, wsp, wsp, bsp, bsp],
        out_shape=[_sds((bl, SEQ, D_FF), BF16), _sds((bl, SEQ, D_FF), BF16), _sds((3, D_FF), F32), _sds((3, D_FF), F32),
                   _sds((1, D_FF), F32), _sds((1, D_FF), F32)],
        scratch_shapes=[pltpu.VMEM((SEQ, FF_TC), F32), pltpu.VMEM((SEQ, FF_TC), F32),
                        pltpu.VMEM((2, FF_CH + 2 * FF_HALO, FF_TC), F32), pltpu.VMEM((8, 8, FF_TC), F32)],
        compiler_params=_cparams(PAR, ARB),
    )(h3, h3, cw, cw, cb, cb, *[a.reshape(bl, SEQ, D_FF) for a in (dact, cg_all, cu_all)])
    return (dhg.reshape(t, D_FF), dhu.reshape(t, D_FF), jnp.concatenate([dwg, dwu], axis=1),
            jnp.concatenate([dbg, dbu], axis=1))


def _ple_fwd(x2, gain, wg, pe, pe_blk, wp, *, tm, name):
    t, k = x2.shape

    def body(x_ref, g_ref, wg_ref, pe_ref, wp_ref, hn_ref, x3_ref, gt_ref, pp_ref):
        x = x_ref[...]
        r = lax.rsqrt(jnp.mean(x * x, axis=-1, keepdims=True) + EPS)
        hn = (x * r * g_ref[...]).astype(BF16)
        hn_ref[...] = hn
        gate = _sigmoid(jnp.dot(hn, wg_ref[...], preferred_element_type=F32))
        pp = jnp.dot(pe_ref[...].astype(BF16), wp_ref[...], preferred_element_type=F32)
        gt_ref[...] = gate.astype(BF16)
        pp_ref[...] = pp.astype(BF16)
        x3_ref[...] = x + pp * gate

    row = pl.BlockSpec((tm, k), lambda i: (i, 0))
    return pl.pallas_call(
        body, name=name, grid=(t // tm,),
        in_specs=[row, pl.BlockSpec((1, k), lambda i: (0, 0)), pl.BlockSpec((k, k), lambda i: (0, 0)),
                  pl.BlockSpec((tm, PLE_DIM), lambda i: (pe_blk + i, 0)), pl.BlockSpec((PLE_DIM, k), lambda i: (0, 0))],
        out_specs=[row, row, row, row],
        out_shape=[_sds((t, k), BF16), _sds((t, k), F32), _sds((t, k), BF16), _sds((t, k), BF16)],
        compiler_params=_cparams(PAR),
    )(x2, gain, wg, pe, wp)


def _ple_bwd_ew(dx3, gate, pp, *, tm, name):
    t, n = dx3.shape

    def body(d_ref, g_ref, p_ref, dz_ref, dpp_ref):
        d, g = d_ref[...], g_ref[...]
        dz_ref[...] = (d * p_ref[...] * g * (1.0 - g)).astype(BF16)
        dpp_ref[...] = (d * g).astype(BF16)

    spec = pl.BlockSpec((tm, n), lambda i: (i, 0))
    return pl.pallas_call(
        body, name=name, grid=(t // tm,), in_specs=[spec] * 3, out_specs=[spec] * 2,
        out_shape=[_sds((t, n), BF16)] * 2, compiler_params=_cparams(PAR),
    )(dx3, gate, pp)


def _loss_head(y, tgt, *, tm, name):
    t, d = y.shape

    def body(y_ref, t_ref, l_ref, dy_ref):
        @pl.when(pl.program_id(0) == 0)
        def _():
            l_ref[...] = jnp.zeros_like(l_ref)

        e = y_ref[...] - t_ref[...]
        dy_ref[...] = e * (1.0 / d)
        s = jnp.sum(jnp.sum(e * e, axis=1, keepdims=True), axis=0, keepdims=True)
        l_ref[...] += jnp.broadcast_to(s * (0.5 / d), (8, 128))

    spec = pl.BlockSpec((tm, d), lambda i: (i, 0))
    return pl.pallas_call(
        body, name=name, grid=(t // tm,), in_specs=[spec, spec],
        out_specs=[pl.BlockSpec((8, 128), lambda i: (0, 0)), spec],
        out_shape=[_sds((8, 128), F32), _sds((t, d), F32)], compiler_params=_cparams(ARB),
    )(y, tgt)


BIAS_PC = 8192


def _onehot(bucket_row):
    rows = lax.broadcasted_iota(jnp.int32, (REL_BUCKETS, bucket_row.shape[1]), 0)
    return (rows == bucket_row).astype(BF16)


def _dot3(x, onehot, dims):
    acc = None
    for _ in range(3):
        term = x.astype(BF16)
        part = lax.dot_general(term, onehot, dims, preferred_element_type=F32)
        acc = part if acc is None else acc + part
        x = x - term.astype(F32)
    return acc


def _bias_lookup(table_t, bucket, *, name):
    h = table_t.shape[0]
    p = bucket.shape[1]

    def body(t_ref, b_ref, o_ref):
        bk = b_ref[...]
        val = _dot3(t_ref[...], _onehot(bk), (((1,), (0,)), ((), ())))
        o_ref[...] = jnp.where(bk >= 0, val, NEG_INF)

    return pl.pallas_call(
        body, name=name, grid=(p // BIAS_PC,),
        in_specs=[pl.BlockSpec((h, REL_BUCKETS), lambda i: (0, 0)), pl.BlockSpec((1, BIAS_PC), lambda i: (0, i))],
        out_specs=pl.BlockSpec((h, BIAS_PC), lambda i: (0, i)), out_shape=_sds((h, p), F32),
        compiler_params=_cparams(PAR),
    )(table_t, bucket)


def _bucket_reduce(dbiases, bucket, *, name):
    h, p = dbiases[0].shape
    nl = len(dbiases)

    def body(*refs):
        b_ref, o_ref = refs[nl], refs[nl + 1]

        @pl.when(pl.program_id(0) == 0)
        def _():
            o_ref[...] = jnp.zeros_like(o_ref)

        d = refs[0][...]
        for d_ref in refs[1:nl]:
            d = d + d_ref[...]
        o_ref[...] += _dot3(d, _onehot(b_ref[...]), NT)

    return pl.pallas_call(
        body, name=name, grid=(p // BIAS_PC,),
        in_specs=[pl.BlockSpec((h, BIAS_PC), lambda i: (0, i))] * nl + [pl.BlockSpec((1, BIAS_PC), lambda i: (0, i))],
        out_specs=pl.BlockSpec((h, REL_BUCKETS), lambda i: (0, 0)), out_shape=_sds((h, REL_BUCKETS), F32),
        compiler_params=_cparams(ARB),
    )(*dbiases, bucket)


def _adamw_math(w, g, m, v):
    m = ADAM_B1 * m + (1.0 - ADAM_B1) * g
    v = ADAM_B2 * v + (1.0 - ADAM_B2) * (g * g)
    m_hat = m / (1.0 - ADAM_B1 ** ADAM_STEP)
    v_hat = v / (1.0 - ADAM_B2 ** ADAM_STEP)
    delta = -ADAM_LR * (m_hat / (jnp.sqrt(v_hat) + ADAM_EPS) + ADAM_WD * w)
    return delta, m, v


def _adamw_reduce(parts, w, m, v, *, tr, name):
    nl = len(parts)
    rows, c = w.shape
    r = rows // nl
    nt = r // tr

    def body(*refs):
        p_refs = refs[:nl]
        w_ref, m_ref, v_ref, g_ref, d_ref, nm_ref, nv_ref = refs[nl:]
        for li, p_ref in enumerate(p_refs):
            @pl.when(pl.program_id(0) == li)
            def _(p_ref=p_ref):
                g = p_ref[0].astype(F32)
                for k in range(1, N_DEV):
                    g = g + p_ref[k].astype(F32)
                d, nm, nv = _adamw_math(w_ref[...], g, m_ref[...], v_ref[...])
                g_ref[...] = g
                d_ref[...] = d
                nm_ref[...] = nm
                nv_ref[...] = nv

    def part_map(li):
        return lambda l, i: (0, jnp.where(l == li, i, jnp.where(l < li, 0, nt - 1)), 0)

    spec = pl.BlockSpec((tr, c), lambda l, i: (l * nt + i, 0))
    return pl.pallas_call(
        body, name=name, grid=(nl, nt),
        in_specs=[pl.BlockSpec((N_DEV, tr, c), part_map(li)) for li in range(nl)] + [spec, spec, spec],
        out_specs=[spec] * 4, out_shape=[_sds((rows, c), F32)] * 4, compiler_params=_cparams(ARB, ARB),
    )(*parts, w, m, v)


def _adamw_plain(g, w, m, v, *, name):
    def body(g_ref, w_ref, m_ref, v_ref, d_ref, nm_ref, nv_ref):
        d, nm, nv = _adamw_math(w_ref[...], g_ref[...], m_ref[...], v_ref[...])
        d_ref[...] = d
        nm_ref[...] = nm
        nv_ref[...] = nv

    return pl.pallas_call(body, name=name, out_shape=[_sds(w.shape, F32)] * 3)(g, w, m, v)


def _mesh_pos():
    return lax.axis_index("x"), lax.axis_index("y"), lax.axis_index("c")


def _allgather_body(x_refs, out_refs, send_sems, recv_sems, local_sems, slot):
    x, y, c = _mesh_pos()
    me, sibling = (x, y, c), (x, y, 1 - c)
    chips = [(1 - x, y), (x, 1 - y), (1 - x, 1 - y)]
    waits = []
    for a, (x_ref, out_ref) in enumerate(zip(x_refs, out_refs)):
        def copy(k, block, to, src=None, out_ref=out_ref, a=a):
            return pltpu.make_async_remote_copy(
                src_ref=slot(out_ref, block) if src is None else src, dst_ref=slot(out_ref, block),
                send_sem=send_sems.at[a, k], recv_sem=recv_sems.at[a, k], device_id=to, device_id_type=MESH)

        mine = pltpu.make_async_copy(x_ref, slot(out_ref, me), local_sems.at[a])
        mine.start()
        first = [copy(0, me, sibling, src=x_ref)]
        first += [copy(1 + j, me, (*chip, c), src=x_ref) for j, chip in enumerate(chips)]
        for cp in first:
            cp.start()
        waits.append((copy, mine, first))
    sends = []
    for copy, mine, first in waits:
        passed = [copy(4 + j, (*chip, c), sibling) for j, chip in enumerate(chips)]
        for j, chip in enumerate(chips):
            copy(1 + j, (*chip, c), me).wait_recv()
            passed[j].start()
        sends.append(passed)
    for (copy, mine, first), passed in zip(waits, sends):
        copy(0, sibling, me).wait_recv()
        for j, chip in enumerate(chips):
            copy(4 + j, (*chip, 1 - c), me).wait_recv()
        for cp in first + passed:
            cp.wait_send()
        mine.wait()


PEER_FLIPS = ((0, 0, 1), (1, 0, 0), (0, 1, 0), (1, 1, 0), (1, 0, 1), (0, 1, 1), (1, 1, 1))


def _peer_copies(x_refs, land_refs, send_sem, recv_sem, scatter):
    x, y, c = _mesh_pos()
    me = 4 * x + 2 * y + c
    copies = []
    for x_ref, land_ref, scat in zip(x_refs, land_refs, scatter):
        for fx, fy, fc in PEER_FLIPS:
            px, py, pc = x ^ fx, y ^ fy, c ^ fc
            src = x_ref.at[4 * px + 2 * py + pc] if scat else x_ref
            copies.append(pltpu.make_async_remote_copy(
                src_ref=src, dst_ref=land_ref.at[me], send_sem=send_sem, recv_sem=recv_sem,
                device_id=(px, py, pc), device_id_type=MESH))
    return copies


def _sc_exchange(xs, *, scatter, collective_id, name):
    na = len(xs)
    land_shapes = [x.shape if scat else (N_DEV,) + x.shape for x, scat in zip(xs, scatter)]

    def body(*refs):
        x_refs, land_refs = refs[:na], refs[na:2 * na]
        send_sem, recv_sem, local_sem = refs[2 * na:]
        x, y, c = _mesh_pos()
        me = 4 * x + 2 * y + c
        barrier = pltpu.get_barrier_semaphore()
        for fx, fy, fc in PEER_FLIPS:
            pl.semaphore_signal(barrier, inc=1, device_id=(x ^ fx, y ^ fy, c ^ fc), device_id_type=MESH)
        pl.semaphore_wait(barrier, len(PEER_FLIPS))
        for x_ref, land_ref, scat in zip(x_refs, land_refs, scatter):
            own = pltpu.make_async_copy(x_ref.at[me] if scat else x_ref, land_ref.at[me], local_sem)
            own.start()
            own.wait()
        copies = _peer_copies(x_refs, land_refs, send_sem, recv_sem, scatter)
        for cp in copies:
            cp.start()
        for cp in copies:
            cp.wait()

    return pl.kernel(
        body, name=name, out_type=[_sds(s, x.dtype) for s, x in zip(land_shapes, xs)],
        mesh=plsc.ScalarSubcoreMesh(axis_name="sequencer", num_cores=1),
        scratch_types=[pltpu.SemaphoreType.DMA, pltpu.SemaphoreType.DMA, pltpu.SemaphoreType.DMA],
        compiler_params=pltpu.CompilerParams(collective_id=collective_id),
    )(*xs)


def _sc_allgather(xs, *, collective_id, name):
    na = len(xs)

    def body(*refs):
        x_refs, out_refs = refs[:na], refs[na:2 * na]
        send_sems, recv_sems, local_sems = refs[2 * na:]
        x, y, c = _mesh_pos()
        barrier = pltpu.get_barrier_semaphore()
        for fx, fy, fc in PEER_FLIPS:
            pl.semaphore_signal(barrier, inc=1, device_id=(x ^ fx, y ^ fy, c ^ fc), device_id_type=MESH)
        pl.semaphore_wait(barrier, len(PEER_FLIPS))
        _allgather_body(x_refs, out_refs, send_sems, recv_sems, local_sems,
                        lambda ref, pos: ref.at[4 * pos[0] + 2 * pos[1] + pos[2]])

    return pl.kernel(
        body, name=name, out_type=[_sds((N_DEV,) + x.shape, x.dtype) for x in xs],
        mesh=plsc.ScalarSubcoreMesh(axis_name="sequencer", num_cores=1),
        scratch_types=[pltpu.SemaphoreType.DMA((na, 7)), pltpu.SemaphoreType.DMA((na, 7)),
                       pltpu.SemaphoreType.DMA((na,))],
        compiler_params=pltpu.CompilerParams(collective_id=collective_id),
    )(*xs)


def _allgather_vmem(x, *, name):
    r, c = x.shape

    def body(x_ref, out_ref, send_sems, recv_sems, local_sems):
        _allgather_body([x_ref], [out_ref], send_sems, recv_sems, local_sems,
                        lambda ref, pos: ref.at[pl.ds((4 * pos[0] + 2 * pos[1] + pos[2]) * r, r), :])

    vm = pl.BlockSpec(memory_space=pltpu.VMEM)
    return pl.pallas_call(
        body, name=name, in_specs=[vm], out_specs=vm, out_shape=_sds((N_DEV * r, c), x.dtype),
        scratch_shapes=[pltpu.SemaphoreType.DMA((1, 7)), pltpu.SemaphoreType.DMA((1, 7)),
                        pltpu.SemaphoreType.DMA((1,))],
    )(x)


def _sum_slots(gathered, *, name):
    _, r, c = gathered.shape

    def body(g_ref, o_ref):
        acc = g_ref[0]
        for k in range(1, N_DEV):
            acc = acc + g_ref[k]
        o_ref[...] = acc

    return pl.pallas_call(body, name=name, out_shape=_sds((r, c), gathered.dtype))(gathered)


def _t5_bucket(rel):
    nb = REL_BUCKETS // 2
    ret = jnp.where(rel > 0, nb, 0)
    n = jnp.abs(rel)
    max_exact = nb // 2
    nf = jnp.maximum(n, 1).astype(F32)
    large = max_exact + (jnp.log(nf / max_exact) / math.log(REL_MAX_DIST / max_exact)
                         * (nb - max_exact)).astype(jnp.int32)
    large = jnp.minimum(large, nb - 1)
    return ret + jnp.where(n < max_exact, n, large)


def _band_pattern(block, radius, dil):
    kw = block + 2 * radius
    rel = jnp.arange(kw)[None, :] - radius - jnp.arange(block)[:, None]
    return jnp.where(jnp.abs(rel) <= radius, _t5_bucket(rel * dil), -1).astype(jnp.int32).reshape(1, block * kw)


def _rope_tables():
    lane = np.arange(64)
    seg, j = lane // 32, lane % 32
    inv = ROPE_THETA ** (-jnp.arange(0, 32, 2, dtype=F32) / 32)
    tpos = jnp.arange(SEQ)
    pos = jnp.where(jnp.asarray(seg)[None, :] == 0, (tpos // GRID_W)[:, None], (tpos % GRID_W)[:, None])
    ang = pos.astype(F32) * inv[jnp.asarray(j % 16)][None, :]
    cos = jnp.cos(ang)
    sins = jnp.where(jnp.asarray(j)[None, :] < 16, -jnp.sin(ang), jnp.sin(ang))
    return jnp.tile(cos, (1, 4)), jnp.tile(sins, (1, 4))


A_Q, A_K, A_V = (256, 0), (256, 1), (256, 2)
B_Q, B_K, B_V = (256, 0), (128, 2), (128, 3)
A_HEADS = dict(rad=A_RADIUS, nh=4, nkv=4)
B_HEADS = dict(rad=SWA_RADIUS, nh=4, nkv=2)


def _local_step(x, pe, tgt, rel_bias, wts, matmul_weights, grads_ready):
    t = x.shape[0]
    bl = t // SEQ
    cos, sins = _rope_tables()
    blocks_a = [min(BAND_BLOCK, SEQ // d) for d in DILATIONS]
    pats_a = [_band_pattern(blk, A_RADIUS, d) for blk, d in zip(blocks_a, DILATIONS)]
    pat_b = _band_pattern(BAND_BLOCK, SWA_RADIUS, 1)
    table_t = rel_bias.T
    bias_a = [_bias_lookup(table_t[:4], pt, name=f"bias_a{ci}").reshape(4, blk, blk + 2 * A_RADIUS)
              for ci, (pt, blk) in enumerate(zip(pats_a, blocks_a))]
    bias_b = _bias_lookup(table_t[4:], pat_b, name="bias_b").reshape(4, BAND_BLOCK, BAND_BLOCK + 2 * SWA_RADIUS)
    nat4 = lambda a: a.reshape(bl, 1, SEQ, a.shape[-1])

    saved = []
    for li in range(DEPTH):
        w = dict(wts[li])
        w.update(matmul_weights(li, "in", x))
        hn0, proj = _norm_mm((x,), w["g_mix"], w["w_in"], None, tm=1024, tn=1152, name="mix_in_fwd", out_dtype=BF16)
        qa1, qa4, qa16, qb, qd = _qkprep_fwd(proj, w["qk_gains"], cos, sins, tm=512, name="qkprep_fwd")
        qa = (nat4(qa1), qa4, qa16)
        oa, la = [], []
        for ci in range(3):
            o, l = _band_fwd(qa[ci], A_Q, A_K, A_V, bias_a[ci], None, name=f"band_a{ci}_fwd", **A_HEADS)
            oa.append(o)
            la.append(l)
        oa[0], la[0] = oa[0].reshape(t, 256), la[0].reshape(t, 256)
        ya, lse_a = _combine_a(oa, la, tm=512, name="combine_a")
        yb, lse_b = _band_fwd(nat4(qb), B_Q, B_K, B_V, bias_b, w["sink_t"], name="band_b_fwd", **B_HEADS)
        yb = yb.reshape(t, 256)
        yc = _c_fwd(proj, w["c_g"], w["c_b"], w["c_ws"], w["c_bst"], tm=512, name="c_fwd")
        yd, lse_d = _dense_fwd(qd, tq=256, name="dense_fwd")
        w.update(matmul_weights(li, "rest", yd))
        mixed, x1 = _norm_mm((ya, yb, yc, yd), w["out_gain"], w["w_out"], x, tm=1024, tn=1024, name="mix_out_fwd")
        hn1, h = _norm_mm((x1,), w["g_ffn"], w["w_up"], None, tm=1024, tn=1408, name="ffn_up_fwd", out_dtype=BF16)
        act, cg, cu = _conv_gate_fwd(h, w["conv_w"], w["conv_b"], name="conv_gate_fwd")
        x2 = _mm(act, w["w_down"], "nn", x1, tm=1024, tn=1024, out_dtype=F32, name="ffn_down_fwd")
        hn2, x3, gate, pp = _ple_fwd(x2, w["g_ple"], w["w_gate"], pe, li * (t // 1024), w["w_proj"], tm=1024,
                                     name="ple_fwd")
        saved.append(dict(w=w, x0=x, hn0=hn0, proj=proj, qa=qa, qb=qb, qd=qd, ya=ya, lse_a=lse_a, yb=yb, lse_b=lse_b,
                          yc=yc, yd=yd, lse_d=lse_d, mixed=mixed, x1=x1, hn1=hn1, h=h, cg=cg, cu=cu, act=act, x2=x2, hn2=hn2,
                          gate=gate, pp=pp))
        x = x3

    loss_tile, dx = _loss_head(x, tgt, tm=512, name="loss_head")
    grads = [None] * DEPTH
    dbias_a, dbias_bs = [[], [], []], []
    for li in reversed(range(DEPTH)):
        s = saved[li]
        w = s["w"]
        g = {}
        dz, dpp = _ple_bwd_ew(dx, s["gate"], s["pp"], tm=512, name="ple_bwd_ew")
        g["w_gate"] = _mm(s["hn2"], dz, "tn", None, tm=1024, tn=512, out_dtype=BF16, name="dw_gate")
        g["w_proj"] = _mm(pe, dpp, "tn", None, tm=256, tn=1024, out_dtype=BF16, name="dw_proj", a_rows=(li, t))
        dx2, dx2b, g["g_ple"] = _mm_bt_normbwd((dz,), w["w_gate"], (s["x2"],), w["g_ple"], dx, tm=1024, tn=1024,
                                               name="ple_bwd", emit_bf16=True)
        g["w_down"] = _mm(s["act"], dx2b, "tn", None, tm=1408, tn=512, out_dtype=BF16, name="dw_down")
        dact = _mm(dx2b, w["w_down"], "nt", None, tm=1024, tn=1408, out_dtype=BF16, name="ffn_down_bwd")
        dhg, dhu, g["conv_w"], g["conv_b"] = _conv_gate_bwd(s["h"], s["cg"], s["cu"], dact, w["conv_w"], w["conv_b"],
                                                            name="conv_gate_bwd")
        g["w_up"] = jnp.concatenate(
            [_mm(s["hn1"], dhalf, "tn", None, tm=1024, tn=1408, out_dtype=BF16, name=f"dw_up_{nm}")
             for nm, dhalf in (("gate", dhg), ("up", dhu))], axis=1)
        dx1, dx1b, g["g_ffn"] = _mm_bt_normbwd((dhg, dhu), w["w_up"], (s["x1"],), w["g_ffn"], dx2, tm=1024, tn=1408,
                                               name="ffn_up_bwd", emit_bf16=True)
        g["w_out"] = _mm(s["mixed"], dx1b, "tn", None, tm=1024, tn=512, out_dtype=BF16, name="dw_out")
        grads_ready(li, "mid", g)
        dycat, g["out_gain"] = _mm_bt_normbwd((dx1b,), w["w_out"], (s["ya"], s["yb"], s["yc"], s["yd"]), w["out_gain"],
                                              None, tm=1024, tn=1024, name="mix_out_bwd")
        dy_r, lse_r, dl_a, dl_b, dl_d = _deltas(dycat, s["ya"], s["yb"], s["yd"], s["lse_a"], tm=512, name="deltas")
        dy_a = (nat4(dycat),) + tuple(dy_r)
        lse_a = (nat4(s["lse_a"]),) + tuple(lse_r)
        dl_a = (nat4(dl_a[0]),) + tuple(dl_a[1:])
        da = []
        for ci in range(3):
            dq, dk, dv, dbias = _band_bwd(s["qa"][ci], A_Q, A_K, A_V, bias_a[ci], None, dy_a[ci], 0, lse_a[ci],
                                          dl_a[ci], name=f"band_a{ci}_bwd", **A_HEADS)
            if ci == 0:
                dq, dk, dv = (a.reshape(t, 256) for a in (dq, dk, dv))
            da.append((dq, dk, dv))
            dbias_a[ci].append(dbias.reshape(4, -1))
        dqb, dkb, dvb, dbias_b, dsink = _band_bwd(nat4(s["qb"]), B_Q, B_K, B_V, bias_b, w["sink_t"], nat4(dycat), 1,
                                                  nat4(s["lse_b"]), nat4(dl_b), name="band_b_bwd", **B_HEADS)
        dbias_bs.append(dbias_b.reshape(4, -1))
        g["sink"] = dsink[:, 0, 0]
        dd = _dense_bwd(s["qd"], dycat, s["lse_d"], dl_d, tq=128, name="dense_bwd")
        dcu, dcv, g["c_ws"], dbs, g["c_g"], g["c_b"] = _c_bwd(s["proj"], dycat, w["c_g"], w["c_b"], w["c_ws"],
                                                               w["c_wst"], w["c_bst"], tm=512, name="c_bwd")
        g["c_bs"] = dbs[:, ::64].T
        db = (dqb.reshape(t, 256), dkb.reshape(t, 128), dvb.reshape(t, 128))
        dproj, dgains = _qkprep_bwd(s["proj"], da, db, dd, dcu, dcv, w["qk_gains"], cos, sins, tm=512, name="qkprep_bwd")
        g["qk_gain"] = dgains[:6, :64].reshape(3, 2, HEAD_DIM)
        g["w_in"] = _mm(s["hn0"], dproj, "tn", None, tm=1024, tn=1152, out_dtype=BF16, name="dw_in")
        dx, g["g_mix"] = _mm_bt_normbwd((dproj,), w["w_in"], (s["x0"],), w["g_mix"], dx1, tm=1024, tn=1152,
                                        name="mix_in_bwd")
        grads[li] = g
        grads_ready(li, "end", g)
    d_table_a = sum(_bucket_reduce(dbias_a[ci], pats_a[ci], name=f"bucket_a{ci}") for ci in range(3))
    d_table_b = _bucket_reduce(dbias_bs, pat_b, name="bucket_b")
    d_rel_bias = jnp.concatenate([d_table_a, d_table_b], axis=0).T
    return loss_tile[0, 0], dx, grads, d_rel_bias


WEIGHT_NAMES = ("rel_bias", "ln_mix_g", "w_in", "qk_gain", "sink", "c_norm_g", "c_norm_b", "c_ws", "c_bs", "out_gain",
                "w_out", "ln_ffn_g", "w_up", "conv_w", "conv_b", "w_down", "ln_ple_g", "w_ple_gate", "w_ple_proj")
COL_SHARDED = ("w_in", "w_up", "w_ple_proj")
ROW_SHARDED = ("w_out", "w_down", "w_ple_gate")
SMALL_SHARDED = ("conv_w", "out_gain")
REPLICATED = tuple(n for n in WEIGHT_NAMES if n not in COL_SHARDED + ROW_SHARDED + SMALL_SHARDED)
LOCAL_GRAD_KEY = {"ln_mix_g": "g_mix", "ln_ffn_g": "g_ffn", "ln_ple_g": "g_ple", "c_norm_g": "c_g", "c_norm_b": "c_b",
                  "w_ple_gate": "w_gate", "w_ple_proj": "w_proj"}


def _full_from_gathered(name, gathered):
    _, r, c = gathered.shape
    if name in ROW_SHARDED:
        return gathered.reshape(N_DEV * r, c)
    return jnp.transpose(gathered, (1, 0, 2)).reshape(r, N_DEV * c)


def _slots_from_full(name, full):
    rows, cols = full.shape
    if name in ROW_SHARDED:
        return full.reshape(N_DEV, rows // N_DEV, cols)
    return jnp.transpose(full.reshape(rows, N_DEV, cols // N_DEV), (1, 0, 2))


def _piece_rows(shape):
    return -(-int(np.prod(shape)) // 1024) * 8


def _pack_rows(arrays):
    pieces = []
    for a in arrays:
        n, rows = int(np.prod(a.shape)), _piece_rows(a.shape)
        flat = a.astype(F32).reshape(-1)
        if n != rows * LANES:
            flat = jnp.pad(flat, (0, rows * LANES - n))
        pieces.append(flat.reshape(rows, LANES))
    return jnp.concatenate(pieces, axis=0)


def _unpack_rows(packed, shapes):
    out, off = [], 0
    for shp in shapes:
        n, rows = int(np.prod(shp)), _piece_rows(shp)
        piece = packed[off:off + rows]
        out.append((piece if n == rows * LANES else piece.reshape(-1)[:n]).reshape(shp))
        off += rows
    return out


def kernel(x, p, rel_bias, ln_mix_g, w_in, qk_gain, sink, c_norm_g, c_norm_b, c_ws, c_bs, out_gain, w_out, ln_ffn_g, w_up, conv_w, conv_b, w_down, ln_ple_g, w_ple_gate, w_ple_proj, loss_target, m_rel_bias, m_ln_mix_g, m_w_in, m_qk_gain, m_sink, m_c_norm_g, m_c_norm_b, m_c_ws, m_c_bs, m_out_gain, m_w_out, m_ln_ffn_g, m_w_up, m_conv_w, m_conv_b, m_w_down, m_ln_ple_g, m_w_ple_gate, m_w_ple_proj, v_rel_bias, v_ln_mix_g, v_w_in, v_qk_gain, v_sink, v_c_norm_g, v_c_norm_b, v_c_ws, v_c_bs, v_out_gain, v_w_out, v_ln_ffn_g, v_w_up, v_conv_w, v_conv_b, v_w_down, v_ln_ple_g, v_w_ple_gate, v_w_ple_proj):
    env = dict(locals())
    wt = {n: env[n] for n in WEIGHT_NAMES}
    mom_m = {n: env["m_" + n] for n in WEIGHT_NAMES}
    mom_v = {n: env["v_" + n] for n in WEIGHT_NAMES}
    bl = x.shape[0]
    t = bl * SEQ
    me = 4 * lax.axis_index("x") + 2 * lax.axis_index("y") + lax.axis_index("c")

    big = COL_SHARDED + ROW_SHARDED
    full = {}
    small_shapes = [wt[n].shape for n in SMALL_SHARDED]
    small = _allgather_vmem(_pack_rows([wt[n] for n in SMALL_SHARDED]), name="gather_small")
    small = small.reshape(N_DEV, -1)
    off = 0
    for n, shp in zip(SMALL_SHARDED, small_shapes):
        cnt = int(np.prod(shp))
        g = small[:, off:off + cnt].reshape((N_DEV,) + tuple(shp))
        full[n] = jnp.transpose(g, (1, 2, 0, 3)).reshape(shp[0], shp[1], N_DEV * shp[2])
        off += _piece_rows(shp) * LANES

    def head_gain(li, a, b, reps):
        g = jnp.tile(qk_gain[li, a, b], reps)
        return jnp.pad(g, (0, 256 - g.shape[0]))

    wts = []
    for li in range(DEPTH):
        rows = [head_gain(li, 0, 0, 4), head_gain(li, 0, 1, 4), head_gain(li, 1, 0, 4), head_gain(li, 1, 1, 2),
                head_gain(li, 2, 0, 4), head_gain(li, 2, 1, 2), jnp.zeros((256,), F32), jnp.zeros((256,), F32)]
        wts.append(dict(
            g_mix=ln_mix_g[li].reshape(1, -1), qk_gains=jnp.stack(rows),
            sink_t=jnp.broadcast_to(sink[li][:, None, None], (4, 8, 128)),
            c_g=c_norm_g[li].reshape(1, -1), c_b=c_norm_b[li].reshape(1, -1), c_ws=c_ws[li].astype(BF16),
            c_wst=jnp.transpose(c_ws[li], (0, 2, 1)).astype(BF16), c_bst=jnp.repeat(c_bs[li].T, 64, axis=1),
            out_gain=full["out_gain"][li].reshape(1, -1), g_ffn=ln_ffn_g[li].reshape(1, -1),
            conv_w=full["conv_w"][li], conv_b=conv_b[li].reshape(1, -1), g_ple=ln_ple_g[li].reshape(1, -1)))

    local_key = {"w_ple_gate": "w_gate", "w_ple_proj": "w_proj"}

    gather_names = {"in": ("w_in",), "rest": tuple(n for n in big if n != "w_in")}
    gathered = {}
    for cid, (li, names) in enumerate(((0, gather_names["in"]), (0, gather_names["rest"]), (1, big))):
        lands = _sc_allgather([wt[n][li].astype(BF16) for n in names], collective_id=cid,
                              name=f"gather_{li}_{len(names)}")
        gathered.setdefault(li, {}).update(zip(names, lands))

    def matmul_weights(li, part, after):
        out = {}
        for n in gather_names[part]:
            g, _ = lax.optimization_barrier((gathered[li][n], after))
            out[local_key.get(n, n)] = _full_from_gathered(n, g)
        return out

    mid_names = ("w_ple_gate", "w_ple_proj", "w_down", "w_up", "w_out")
    end_names = ("w_in",)
    landed = {}

    def start_exchange(li, names, g, tag, cid):
        slots = [_slots_from_full(n, g[local_key.get(n, n)]) for n in names]
        lands = _sc_exchange(slots, scatter=[True] * len(slots), collective_id=cid, name=f"grads_{li}_{tag}")
        landed.update({(n, li): land for n, land in zip(names, lands)})

    def grads_ready(li, stage, g):
        if li == 0 and stage == "mid":
            start_exchange(li, mid_names, g, stage, 5)
        elif li > 0 and stage == "end":
            start_exchange(li, mid_names + end_names, g, stage, 4)

    loss_part, dx, grads, d_rel_bias = _local_step(
        x.reshape(t, D_MODEL), p.reshape(DEPTH * t, PLE_DIM), loss_target.reshape(t, D_MODEL), rel_bias, wts,
        matmul_weights, grads_ready)

    def local_grad(n):
        if n == "rel_bias":
            return d_rel_bias
        key = LOCAL_GRAD_KEY.get(n, n)
        return jnp.stack([grads[li][key].reshape(wt[n].shape[1:]) if n in REPLICATED else grads[li][key]
                          for li in range(DEPTH)])

    small_names = REPLICATED + SMALL_SHARDED
    small_full_shapes = [wt[n].shape if n in REPLICATED else full[n].shape for n in small_names]
    small_pack = _pack_rows([local_grad(n) for n in small_names] + [loss_part.reshape(1)])
    last_slots = [_slots_from_full(n, grads[0][local_key.get(n, n)]) for n in end_names]
    *last_lands, small_parts = _sc_exchange(last_slots + [small_pack], scatter=[True] * len(last_slots) + [False],
                                            collective_id=6, name="grads_0_end")
    landed.update({(n, 0): land for n, land in zip(end_names, last_lands)})

    out_g, out_d, out_m, out_v = {}, {}, {}, {}
    for n in sorted(big, key=lambda n: n in end_names):
        shp = wt[n].shape
        two_d = lambda a: a.reshape(-1, shp[-1])
        res = _adamw_reduce([landed[n, li] for li in range(DEPTH)], two_d(wt[n]), two_d(mom_m[n]), two_d(mom_v[n]),
                            tr=32 if n == "w_down" else 128, name="adamw_" + n)
        out_g[n], out_d[n], out_m[n], out_v[n] = [r.reshape(shp) for r in res]

    *reduced, loss = _unpack_rows(_sum_slots(small_parts, name="sum_small_grads"), small_full_shapes + [(1,)])
    loss = loss[0]
    reduced = dict(zip(small_names, reduced))
    rep_shapes = [wt[n].shape for n in REPLICATED]
    upd = _adamw_plain(_pack_rows([reduced[n] for n in REPLICATED]), _pack_rows([wt[n] for n in REPLICATED]),
                       _pack_rows([mom_m[n] for n in REPLICATED]), _pack_rows([mom_v[n] for n in REPLICATED]),
                       name="adamw_replicated")
    for dst, packed in zip((out_d, out_m, out_v), upd):
        dst.update(zip(REPLICATED, _unpack_rows(packed, rep_shapes)))
    for n in REPLICATED:
        out_g[n] = reduced[n]
    for n in SMALL_SHARDED:
        shp = wt[n].shape
        g = reduced[n].reshape(shp[0], shp[1], N_DEV, shp[2])
        g = lax.dynamic_index_in_dim(g, me, axis=2, keepdims=False)
        two_d = lambda a: a.reshape(-1, shp[-1])
        res = _adamw_plain(two_d(g), two_d(wt[n]), two_d(mom_m[n]), two_d(mom_v[n]), name="adamw_" + n)
        out_g[n] = g
        out_d[n], out_m[n], out_v[n] = [r.reshape(shp) for r in res]

    return (loss, dx.reshape(bl, SEQ, D_MODEL), *[out_g[n] for n in WEIGHT_NAMES], *[out_d[n] for n in WEIGHT_NAMES],
            *[out_m[n] for n in WEIGHT_NAMES], *[out_v[n] for n in WEIGHT_NAMES])
```

```python
import math

import jax
import jax.numpy as jnp
import numpy as np
from jax import lax
from jax.experimental import pallas as pl
from jax.experimental.pallas import tpu as pltpu
from jax.experimental.pallas import tpu_sc as plsc

F32 = jnp.float32
BF16 = jnp.bfloat16

N_DEV = 8
D_MODEL = 1024
SEQ = 2048
DEPTH = 2
HEAD_DIM = 64
IN_WIDTH = 2304
D_FF = 2816
PLE_DIM = 256
C_CHUNK = 128
C_GROUPS = 4
DILATED_CFGS = ((128, 1), (512, 4), (2048, 16))
DILATIONS = tuple(d for _, d in DILATED_CFGS)
A_RADIUS = 64
SWA_RADIUS = 128
BAND_BLOCK = 256
GRID_W = 64
ROPE_THETA = 10000.0
REL_BUCKETS = 32
REL_MAX_DIST = 1024
EPS = 1e-6
NEG_INF = -1e30
ATTN_SCALE = HEAD_DIM ** -0.5
LANES = 128

ADAM_LR = 0.001
ADAM_B1 = 0.9
ADAM_B2 = 0.999
ADAM_EPS = 1e-08
ADAM_WD = 0.01
ADAM_STEP = 10

MESH = pl.DeviceIdType.MESH
NT = (((1,), (1,)), ((), ()))
TN = (((0,), (0,)), ((), ()))
ARB = "arbitrary"
PAR = "parallel"


def _cparams(*sem):
    return pltpu.CompilerParams(dimension_semantics=tuple(sem))


def _sds(shape, dtype):
    return jax.ShapeDtypeStruct(tuple(shape), dtype)


def _group_sum_matrix(n, same_group):
    r = lax.broadcasted_iota(jnp.int32, (n, n), 0)
    c = lax.broadcasted_iota(jnp.int32, (n, n), 1)
    if same_group:
        return ((r >> 6) == (c >> 6)).astype(F32)
    return ((r & 63) == (c & 63)).astype(F32)


def _seg_sum(x, e):
    eb = e.astype(BF16)
    hi = x.astype(BF16)
    lo = (x - hi.astype(F32)).astype(BF16)
    return jnp.dot(hi, eb, preferred_element_type=F32) + jnp.dot(lo, eb, preferred_element_type=F32)


def _gelu(x):
    c = math.sqrt(2.0 / math.pi)
    return 0.5 * x * (1.0 + jnp.tanh(c * (x + 0.044715 * (x * x * x))))


def _gelu_grad(x):
    c = math.sqrt(2.0 / math.pi)
    t = jnp.tanh(c * (x + 0.044715 * (x * x * x)))
    return 0.5 * (1.0 + t) + 0.5 * x * (1.0 - t * t) * c * (1.0 + 3.0 * 0.044715 * (x * x))


def _sigmoid(x):
    return 1.0 / (1.0 + jnp.exp(-x))


def _scatter_cols(scratch, first, val):
    for c in range(val.shape[1] // LANES):
        scratch[first + c] = val[:, c * LANES:(c + 1) * LANES]


def _gather_cols(scratch, first, ncol):
    return jnp.concatenate([scratch[first + c] for c in range(ncol)], axis=1)


def _read_residue(scratch, first, ncol, r, d):
    n = scratch.shape[1] // d
    return jnp.concatenate([scratch.at[first + c][pl.ds(r, n, stride=d), :] for c in range(ncol)], axis=1)


def _write_residue(scratch, first, r, d, val):
    n = scratch.shape[1] // d
    for c in range(val.shape[1] // LANES):
        scratch.at[first + c][pl.ds(r, n, stride=d), :] = val[:, c * LANES:(c + 1) * LANES]


def _norm_mm(xs, gain, w, res, *, tm, tn, name, out_dtype=F32):
    t = xs[0].shape[0]
    k = sum(x.shape[1] for x in xs)
    n = w.shape[1]
    ng = len(xs)
    has_res = res is not None

    def body(*refs):
        x_refs = refs[:ng]
        g_ref, w_ref = refs[ng], refs[ng + 1]
        res_ref = refs[ng + 2] if has_res else None
        hn_ref, o_ref, hn_s = refs[ng + 2 + has_res:]

        @pl.when(pl.program_id(1) == 0)
        def _():
            off = 0
            for xr in x_refs:
                x = xr[...].astype(F32)
                wd = x.shape[1]
                r = lax.rsqrt(jnp.mean(x * x, axis=-1, keepdims=True) + EPS)
                hn_s[:, off:off + wd] = (x * r * g_ref[:, off:off + wd]).astype(BF16)
                off += wd
            hn_ref[...] = hn_s[...]

        acc = jnp.dot(hn_s[...], w_ref[...], preferred_element_type=F32)
        if has_res:
            acc = acc + res_ref[...]
        o_ref[...] = acc.astype(out_dtype)

    in_specs = [pl.BlockSpec((tm, x.shape[1]), lambda i, j: (i, 0)) for x in xs]
    in_specs += [pl.BlockSpec((1, k), lambda i, j: (0, 0)), pl.BlockSpec((k, tn), lambda i, j: (0, j))]
    args = list(xs) + [gain, w]
    if has_res:
        in_specs.append(pl.BlockSpec((tm, tn), lambda i, j: (i, j)))
        args.append(res)
    return pl.pallas_call(
        body, name=name, grid=(t // tm, n // tn), in_specs=in_specs,
        out_specs=[pl.BlockSpec((tm, k), lambda i, j: (i, 0)), pl.BlockSpec((tm, tn), lambda i, j: (i, j))],
        out_shape=[_sds((t, k), BF16), _sds((t, n), out_dtype)],
        scratch_shapes=[pltpu.VMEM((tm, k), BF16)],
        compiler_params=_cparams(PAR, ARB),
    )(*args)


def _mm(a, b, mode, res, *, tm, tn, out_dtype, name, a_rows=None):
    if mode == "tn":
        kk, m = a.shape
        blk_a = 0
        if a_rows is not None:
            blk_a, kk = a_rows
        a_spec = pl.BlockSpec((kk, tm), lambda i, j: (blk_a, i))
    else:
        m, kk = a.shape
        a_spec = pl.BlockSpec((tm, kk), lambda i, j: (i, 0))
    if mode == "nt":
        n = b.shape[0]
        b_spec = pl.BlockSpec((tn, kk), lambda i, j: (j, 0))
    else:
        n = b.shape[1]
        b_spec = pl.BlockSpec((kk, tn), lambda i, j: (0, j))
    has_res = res is not None

    def body(*refs):
        a_ref, b_ref = refs[0], refs[1]
        o_ref = refs[-1]
        av = a_ref[...].astype(BF16)
        bv = b_ref[...].astype(BF16)
        if mode == "nn":
            acc = jnp.dot(av, bv, preferred_element_type=F32)
        elif mode == "nt":
            acc = lax.dot_general(av, bv, NT, preferred_element_type=F32)
        else:
            acc = lax.dot_general(av, bv, TN, preferred_element_type=F32)
        if has_res:
            acc = acc + refs[2][...]
        o_ref[...] = acc.astype(out_dtype)

    in_specs = [a_spec, b_spec]
    args = [a, b]
    if has_res:
        in_specs.append(pl.BlockSpec((tm, tn), lambda i, j: (i, j)))
        args.append(res)
    return pl.pallas_call(
        body, name=name, grid=(m // tm, n // tn), in_specs=in_specs,
        out_specs=pl.BlockSpec((tm, tn), lambda i, j: (i, j)),
        out_shape=_sds((m, n), out_dtype),
        compiler_params=_cparams(PAR, PAR),
    )(*args)


def _mm_bt_normbwd(dys, w, xs, gain, dres, *, tm, tn, name, emit_bf16=False):
    t, wd_each = dys[0].shape
    nd = len(dys)
    per = wd_each // tn
    nj = nd * per
    k = w.shape[0]
    ng = len(xs)
    has_res = dres is not None

    def body(*refs):
        dy_refs = refs[:nd]
        w_ref = refs[nd]
        x_refs = refs[nd + 1:nd + 1 + ng]
        g_ref = refs[nd + 1 + ng]
        dres_ref = refs[nd + 2 + ng] if has_res else None
        outs = refs[nd + 2 + ng + has_res:]
        dx_ref = outs[0]
        dxb_ref = outs[1] if emit_bf16 else None
        dg_ref, acc = outs[1 + emit_bf16:]
        i, j = pl.program_id(0), pl.program_id(1)

        @pl.when(j == 0)
        def _():
            acc[...] = jnp.zeros_like(acc)

        for d, dy_ref in enumerate(dy_refs):
            @pl.when((j >= d * per) & (j < (d + 1) * per))
            def _(dy_ref=dy_ref):
                acc[...] += lax.dot_general(dy_ref[...].astype(BF16), w_ref[...], NT, preferred_element_type=F32)

        @pl.when(j == nj - 1)
        def _():
            @pl.when(i == 0)
            def _():
                dg_ref[...] = jnp.zeros_like(dg_ref)

            off = 0
            for xr in x_refs:
                x = xr[...].astype(F32)
                wd = x.shape[1]
                g = g_ref[:, off:off + wd]
                dyn = acc[:, off:off + wd]
                r = lax.rsqrt(jnp.mean(x * x, axis=-1, keepdims=True) + EPS)
                gdy = dyn * g
                dx = r * gdy - x * (r * r * r * jnp.mean(gdy * x, axis=-1, keepdims=True))
                if has_res:
                    dx = dx + dres_ref[:, off:off + wd]
                dx_ref[:, off:off + wd] = dx
                if emit_bf16:
                    dxb_ref[:, off:off + wd] = dx.astype(BF16)
                dg_ref[:, off:off + wd] += jnp.sum(dyn * x * r, axis=0, keepdims=True)
                off += wd

    def dy_map(d):
        return lambda i, j: (i, jnp.clip(j - d * per, 0, per - 1))

    in_specs = [pl.BlockSpec((tm, tn), dy_map(d)) for d in range(nd)]
    in_specs.append(pl.BlockSpec((k, tn), lambda i, j: (0, j)))
    in_specs += [pl.BlockSpec((tm, x.shape[1]), lambda i, j: (i, 0)) for x in xs]
    in_specs.append(pl.BlockSpec((1, k), lambda i, j: (0, 0)))
    args = list(dys) + [w] + list(xs) + [gain]
    if has_res:
        in_specs.append(pl.BlockSpec((tm, k), lambda i, j: (i, 0)))
        args.append(dres)
    row = pl.BlockSpec((tm, k), lambda i, j: (i, 0))
    out_specs = [row] + ([row] if emit_bf16 else []) + [pl.BlockSpec((1, k), lambda i, j: (0, 0))]
    out_shape = [_sds((t, k), F32)] + ([_sds((t, k), BF16)] if emit_bf16 else []) + [_sds((1, k), F32)]
    return pl.pallas_call(
        body, name=name, grid=(t // tm, nj), in_specs=in_specs, out_specs=out_specs, out_shape=out_shape,
        scratch_shapes=[pltpu.VMEM((tm, k), F32)],
        compiler_params=_cparams(ARB, ARB),
    )(*args)


def _rope_partner(y):
    n = y.shape[1]
    lane = lax.broadcasted_iota(jnp.int32, y.shape, 1)
    return jnp.where((lane & 31) < 16, pltpu.roll(y, n - 16, 1), pltpu.roll(y, 16, 1))


def _residue_specs(tm, width, nt):
    specs = [pl.BlockSpec((tm, width), lambda b, i: (b * nt + i, 0))]
    for d in DILATIONS[1:]:
        specs.append(pl.BlockSpec((None, d, tm // d, width), lambda b, i: (b, 0, i, 0)))
    return specs


def _residue_shapes(bl, width, dtype):
    return [_sds((bl * SEQ, width), dtype)] + [_sds((bl, d, SEQ // d, width), dtype) for d in DILATIONS[1:]]


def _qkprep_fwd(proj, gains, cos, sins, *, tm, name):
    t = proj.shape[0]
    bl = t // SEQ
    nt = SEQ // tm

    def body(p_ref, g_ref, c_ref, s_ref, qa1_ref, qa4_ref, qa16_ref, qb_ref, qd_ref, scr):
        e = _group_sum_matrix(256, True)

        def hn(x, row):
            x = x.astype(F32)
            wd = x.shape[1]
            ms = _seg_sum(x * x, e[:wd, :wd]) * (1.0 / HEAD_DIM)
            return x * lax.rsqrt(ms + EPS) * g_ref[row:row + 1, :wd]

        qa = jnp.concatenate([hn(p_ref[:, 0:256], 0) * ATTN_SCALE, hn(p_ref[:, 256:512], 1),
                              p_ref[:, 512:768].astype(F32)], axis=1)
        qa1_ref[...] = qa.astype(BF16)
        _scatter_cols(scr, 0, qa)
        for d, ref in ((4, qa4_ref), (16, qa16_ref)):
            for r in range(d):
                ref[r] = _read_residue(scr, 0, 6, r, d).astype(BF16)
        qb_ref[:, 0:256] = (hn(p_ref[:, 768:1024], 2) * ATTN_SCALE).astype(BF16)
        qb_ref[:, 256:384] = hn(p_ref[:, 1024:1152], 3).astype(BF16)
        qb_ref[:, 384:512] = p_ref[:, 1152:1280].astype(BF16)
        yq = hn(p_ref[:, 1792:2048], 4)
        yq = yq * c_ref[...] + _rope_partner(yq) * s_ref[...]
        qd_ref[:, 0:256] = (yq * ATTN_SCALE).astype(BF16)
        yk = hn(p_ref[:, 2048:2176], 5)
        yk = yk * c_ref[:, 0:128] + _rope_partner(yk) * s_ref[:, 0:128]
        qd_ref[:, 256:384] = yk.astype(BF16)
        qd_ref[:, 384:512] = p_ref[:, 2176:2304].astype(BF16)

    row = lambda width: pl.BlockSpec((tm, width), lambda b, i: (b * nt + i, 0))
    tab = pl.BlockSpec((tm, 256), lambda b, i: (i, 0))
    return pl.pallas_call(
        body, name=name, grid=(bl, nt),
        in_specs=[row(IN_WIDTH), pl.BlockSpec((8, 256), lambda b, i: (0, 0)), tab, tab],
        out_specs=_residue_specs(tm, 768, nt) + [row(512), row(512)],
        out_shape=_residue_shapes(bl, 768, BF16) + [_sds((t, 512), BF16), _sds((t, 512), BF16)],
        scratch_shapes=[pltpu.VMEM((6, tm, LANES), F32)],
        compiler_params=_cparams(PAR, PAR),
    )(proj, gains, cos, sins)


def _qkprep_bwd(proj, da, db, dd, dcu, dcv, gains, cos, sins, *, tm, name):
    t = proj.shape[0]
    bl = t // SEQ
    nt = SEQ // tm
    flat = [a for cfg in da for a in cfg] + list(db) + list(dd) + [dcu, dcv]

    def body(*refs):
        p_ref, g_ref, c_ref, s_ref = refs[:4]
        d_refs = refs[4:4 + len(flat)]
        dp_ref, dg_ref, scr = refs[4 + len(flat):]
        a_refs = d_refs[:9]
        dqb_ref, dkb_ref, dvb_ref, dqd_ref, dkd_ref, dvd_ref, dcu_ref, dcv_ref = d_refs[9:]
        e = _group_sum_matrix(256, True)
        first = (pl.program_id(0) == 0) & (pl.program_id(1) == 0)
        last = (pl.program_id(0) == bl - 1) & (pl.program_id(1) == nt - 1)

        @pl.when(first)
        def _():
            dg_ref[...] = jnp.zeros_like(dg_ref)

        def hn_bwd(x, dy, row):
            x, dy = x.astype(F32), dy.astype(F32)
            wd = x.shape[1]
            ee = e[:wd, :wd]
            g = g_ref[row:row + 1, :wd]
            r = lax.rsqrt(_seg_sum(x * x, ee) * (1.0 / HEAD_DIM) + EPS)
            gdy = dy * g
            dx = r * gdy - x * (r * r * r * (_seg_sum(gdy * x, ee) * (1.0 / HEAD_DIM)))
            dg_ref[row:row + 1, :wd] += jnp.sum(dy * x * r, axis=0, keepdims=True)
            return dx

        def rope_bwd(dy, wd):
            dy = dy.astype(F32)
            return dy * c_ref[:, :wd] + _rope_partner(dy * s_ref[:, :wd])

        dqkv = jnp.concatenate([a_refs[m][...].astype(F32) for m in range(3)], axis=1)
        for ci, d in ((1, 4), (2, 16)):
            for r in range(d):
                part = jnp.concatenate([a_refs[3 * ci + m][r].astype(F32) for m in range(3)], axis=1)
                _write_residue(scr, 0, r, d, part)
            dqkv = dqkv + _gather_cols(scr, 0, 6)
        dp_ref[:, 0:256] = hn_bwd(p_ref[:, 0:256], dqkv[:, 0:256] * ATTN_SCALE, 0).astype(BF16)
        dp_ref[:, 256:512] = hn_bwd(p_ref[:, 256:512], dqkv[:, 256:512], 1).astype(BF16)
        dp_ref[:, 512:768] = dqkv[:, 512:768].astype(BF16)
        dp_ref[:, 768:1024] = hn_bwd(p_ref[:, 768:1024], dqb_ref[...] * ATTN_SCALE, 2).astype(BF16)
        dp_ref[:, 1024:1152] = hn_bwd(p_ref[:, 1024:1152], dkb_ref[...], 3).astype(BF16)
        dp_ref[:, 1152:1280] = dvb_ref[...].astype(BF16)
        dp_ref[:, 1280:1536] = dcu_ref[...].astype(BF16)
        dp_ref[:, 1536:1792] = dcv_ref[...].astype(BF16)
        dp_ref[:, 1792:2048] = hn_bwd(p_ref[:, 1792:2048], rope_bwd(dqd_ref[...] * ATTN_SCALE, 256), 4).astype(BF16)
        dp_ref[:, 2048:2176] = hn_bwd(p_ref[:, 2048:2176], rope_bwd(dkd_ref[...], 128), 5).astype(BF16)
        dp_ref[:, 2176:2304] = dvd_ref[...].astype(BF16)

        @pl.when(last)
        def _():
            dg_ref[...] = _seg_sum(dg_ref[...], _group_sum_matrix(256, False))

    row = lambda width: pl.BlockSpec((tm, width), lambda b, i: (b * nt + i, 0))
    tab = pl.BlockSpec((tm, 256), lambda b, i: (i, 0))
    in_specs = [row(IN_WIDTH), pl.BlockSpec((8, 256), lambda b, i: (0, 0)), tab, tab]
    res_specs = _residue_specs(tm, 256, nt)
    in_specs += [res_specs[ci] for ci in range(3) for _ in range(3)]
    in_specs += [row(a.shape[1]) for a in flat[9:]]
    return pl.pallas_call(
        body, name=name, grid=(bl, nt), in_specs=in_specs,
        out_specs=[row(IN_WIDTH), pl.BlockSpec((8, 256), lambda b, i: (0, 0))],
        out_shape=[_sds((t, IN_WIDTH), BF16), _sds((8, 256), F32)],
        scratch_shapes=[pltpu.VMEM((6, tm, LANES), F32)],
        compiler_params=_cparams(ARB, ARB),
    )(proj, gains, cos, sins, *flat)


BAND_ROWS_PER_STEP = 512


def _residues_per_step(dil, seq_len):
    return min(dil, max(1, BAND_ROWS_PER_STEP // seq_len))


def _band_spec(seq_len, spec, rb):
    width, idx = spec
    return pl.BlockSpec((None, rb, seq_len, width), lambda b, r: (b, r, 0, idx))


def _fill_padded(dst, src_ref, rad, seq_len):
    z = jnp.zeros((rad, dst.shape[1]), dst.dtype)
    dst[0:rad, :] = z
    dst[rad + seq_len:rad + seq_len + rad, :] = z
    dst[rad:rad + seq_len, :] = src_ref[...]


def _band_fwd(src, qs, ks, vs, bias, sink, *, rad, nh, nkv, name):
    bl, dil, sl, _ = src.shape
    blk = bias.shape[1]
    kw = blk + 2 * rad
    nb = sl // blk
    rep = nh // nkv
    has_sink = sink is not None
    rb = _residues_per_step(dil, sl)

    def body(*refs):
        q_all, k_all, v_all, b_ref = refs[:4]
        s_ref = refs[4] if has_sink else None
        o_all, l_all, kp, vp = refs[4 + has_sink:]
        for ri in range(rb):
            one_sequence(q_all.at[ri], k_all.at[ri], v_all.at[ri], b_ref, s_ref, o_all.at[ri], l_all.at[ri], kp, vp)

    def one_sequence(q_ref, k_ref, v_ref, b_ref, s_ref, o_ref, l_ref, kp, vp):
        _fill_padded(kp, k_ref, rad, sl)
        _fill_padded(vp, v_ref, rad, sl)

        def blk_body(i, carry):
            r0 = pl.multiple_of(i * blk, blk)
            qb = q_ref[pl.ds(r0, blk), :]
            kwin = kp[pl.ds(r0, kw), :]
            vwin = vp[pl.ds(r0, kw), :]
            col = r0 - rad + lax.broadcasted_iota(jnp.int32, (blk, kw), 1)
            neg = jnp.where((col >= 0) & (col < sl), 0.0, NEG_INF).astype(F32)
            for h in range(nh):
                g = h // rep
                hs = slice(h * HEAD_DIM, (h + 1) * HEAD_DIM)
                gs = slice(g * HEAD_DIM, (g + 1) * HEAD_DIM)
                s = lax.dot_general(qb[:, hs], kwin[:, gs], NT, preferred_element_type=F32)
                s = s + b_ref[h] + neg
                m = jnp.max(s, axis=1, keepdims=True)
                if has_sink:
                    sk = s_ref[h][0:1, 0:1]
                    m = jnp.maximum(m, sk)
                p = jnp.exp(s - m)
                den = jnp.sum(p, axis=1, keepdims=True)
                if has_sink:
                    den = den + jnp.exp(sk - m)
                o = jnp.dot(p.astype(BF16), vwin[:, gs], preferred_element_type=F32) / den
                o_ref[pl.ds(r0, blk), hs] = o.astype(BF16)
                l_ref[pl.ds(r0, blk), hs] = jnp.broadcast_to(m + jnp.log(den), (blk, HEAD_DIM))
            return carry

        lax.fori_loop(0, nb, blk_body, 0)

    in_specs = [_band_spec(sl, qs, rb), _band_spec(sl, ks, rb), _band_spec(sl, vs, rb),
                pl.BlockSpec((nh, blk, kw), lambda b, r: (0, 0, 0))]
    args = [src] * 3 + [bias]
    if has_sink:
        in_specs.append(pl.BlockSpec((nh, 8, 128), lambda b, r: (0, 0, 0)))
        args.append(sink)
    return pl.pallas_call(
        body, name=name, grid=(bl, dil // rb), in_specs=in_specs,
        out_specs=[_band_spec(sl, (256, 0), rb)] * 2,
        out_shape=[_sds((bl, dil, sl, 256), BF16), _sds((bl, dil, sl, 256), F32)],
        scratch_shapes=[pltpu.VMEM((sl + 2 * rad, ks[0]), BF16), pltpu.VMEM((sl + 2 * rad, vs[0]), BF16)],
        compiler_params=_cparams(PAR, PAR),
    )(*args)


def _band_bwd(src, qs, ks, vs, bias, sink, dy, dcol, lse, delta, *, rad, nh, nkv, name):
    bl, dil, sl, _ = src.shape
    blk = bias.shape[1]
    kw = blk + 2 * rad
    nb = sl // blk
    rep = nh // nkv
    has_sink = sink is not None
    rb = _residues_per_step(dil, sl)
    wk, wv = ks[0], vs[0]

    def body(*refs):
        q_all, k_all, v_all, b_ref = refs[:4]
        s_ref = refs[4] if has_sink else None
        do_all, l_all, dl_all = refs[4 + has_sink:7 + has_sink]
        outs = refs[7 + has_sink:]
        dsk_ref = None
        if has_sink:
            dq_all, dk_all, dv_all, db_ref, dsk_ref, kp, vp, dka, dva = outs
        else:
            dq_all, dk_all, dv_all, db_ref, kp, vp, dka, dva = outs

        @pl.when((pl.program_id(0) == 0) & (pl.program_id(1) == 0))
        def _():
            db_ref[...] = jnp.zeros_like(db_ref)
            if has_sink:
                dsk_ref[...] = jnp.zeros_like(dsk_ref)

        for ri in range(rb):
            one_sequence(q_all.at[ri], k_all.at[ri], v_all.at[ri], b_ref, s_ref, do_all.at[ri], l_all.at[ri],
                         dl_all.at[ri], dq_all.at[ri], dk_all.at[ri], dv_all.at[ri], db_ref, dsk_ref, kp, vp, dka, dva)

    def one_sequence(q_ref, k_ref, v_ref, b_ref, s_ref, do_ref, l_ref, dl_ref, dq_ref, dk_ref, dv_ref, db_ref, dsk_ref,
                     kp, vp, dka, dva):
        _fill_padded(kp, k_ref, rad, sl)
        _fill_padded(vp, v_ref, rad, sl)
        dka[...] = jnp.zeros_like(dka)
        dva[...] = jnp.zeros_like(dva)

        def blk_body(i, carry):
            r0 = pl.multiple_of(i * blk, blk)
            qb = q_ref[pl.ds(r0, blk), :]
            kwin = kp[pl.ds(r0, kw), :]
            vwin = vp[pl.ds(r0, kw), :]
            dob = do_ref[pl.ds(r0, blk), :].astype(BF16)
            lb = l_ref[pl.ds(r0, blk), :]
            dlb = dl_ref[pl.ds(r0, blk), :]
            col = r0 - rad + lax.broadcasted_iota(jnp.int32, (blk, kw), 1)
            neg = jnp.where((col >= 0) & (col < sl), 0.0, NEG_INF).astype(F32)
            for h in range(nh):
                g = h // rep
                hs = slice(h * HEAD_DIM, (h + 1) * HEAD_DIM)
                gs = slice(g * HEAD_DIM, (g + 1) * HEAD_DIM)
                qh, kh, vh, doh = qb[:, hs], kwin[:, gs], vwin[:, gs], dob[:, hs]
                lh = lb[:, h * HEAD_DIM:h * HEAD_DIM + 1]
                dlh = dlb[:, h * HEAD_DIM:h * HEAD_DIM + 1]
                s = lax.dot_general(qh, kh, NT, preferred_element_type=F32) + b_ref[h] + neg
                p = jnp.exp(s - lh)
                dp = lax.dot_general(doh, vh, NT, preferred_element_type=F32)
                ds = p * (dp - dlh)
                dsb = ds.astype(BF16)
                dq_ref[pl.ds(r0, blk), hs] = jnp.dot(dsb, kh, preferred_element_type=F32).astype(BF16)
                dka[pl.ds(r0, kw), gs] += lax.dot_general(dsb, qh, TN, preferred_element_type=F32)
                dva[pl.ds(r0, kw), gs] += lax.dot_general(p.astype(BF16), doh, TN, preferred_element_type=F32)
                db_ref[h] += ds
                if has_sink:
                    ps = jnp.exp(s_ref[h][0:1, 0:1] - lh)
                    dsk_ref[h] += jnp.broadcast_to(-jnp.sum(ps * dlh, axis=0, keepdims=True), (8, 128))
            return carry

        lax.fori_loop(0, nb, blk_body, 0)
        dk_ref[...] = dka[rad:rad + sl, :].astype(BF16)
        dv_ref[...] = dva[rad:rad + sl, :].astype(BF16)

    const3 = lambda b, r: (0, 0, 0)
    in_specs = [_band_spec(sl, qs, rb), _band_spec(sl, ks, rb), _band_spec(sl, vs, rb),
                pl.BlockSpec((nh, blk, kw), const3)]
    args = [src] * 3 + [bias]
    if has_sink:
        in_specs.append(pl.BlockSpec((nh, 8, 128), const3))
        args.append(sink)
    row = _band_spec(sl, (256, 0), rb)
    in_specs += [_band_spec(sl, (256, dcol), rb), row, row]
    args += [dy, lse, delta]
    out_specs = [row, _band_spec(sl, (wk, 0), rb), _band_spec(sl, (wv, 0), rb), pl.BlockSpec((nh, blk, kw), const3)]
    out_shape = [_sds((bl, dil, sl, 256), BF16), _sds((bl, dil, sl, wk), BF16), _sds((bl, dil, sl, wv), BF16),
                 _sds((nh, blk, kw), F32)]
    if has_sink:
        out_specs.append(pl.BlockSpec((nh, 8, 128), const3))
        out_shape.append(_sds((nh, 8, 128), F32))
    return pl.pallas_call(
        body, name=name, grid=(bl, dil // rb), in_specs=in_specs, out_specs=out_specs, out_shape=out_shape,
        scratch_shapes=[pltpu.VMEM((sl + 2 * rad, wk), BF16), pltpu.VMEM((sl + 2 * rad, wv), BF16),
                        pltpu.VMEM((sl + 2 * rad, wk), F32), pltpu.VMEM((sl + 2 * rad, wv), F32)],
        compiler_params=_cparams(ARB, ARB),
    )(*args)


def _combine_a(os_, ls_, *, tm, name):
    bl = os_[1].shape[0]
    t = bl * SEQ
    nt = SEQ // tm

    def body(o1, o4, o16, l1, l4, l16, y_ref, lt_ref, scr):
        for k, (d, ref) in enumerate(((4, o4), (16, o16), (4, l4), (16, l16))):
            for r in range(d):
                _write_residue(scr, 2 * k, r, d, ref[r].astype(F32))
        o2, o3, b, c = (_gather_cols(scr, 2 * k, 2) for k in range(4))
        a = l1[...]
        m = jnp.maximum(jnp.maximum(a, b), c)
        ea, eb, ec = jnp.exp(a - m), jnp.exp(b - m), jnp.exp(c - m)
        den = ea + eb + ec
        y_ref[...] = ((ea / den) * o1[...].astype(F32) + (eb / den) * o2 + (ec / den) * o3).astype(BF16)
        lt_ref[...] = m + jnp.log(den)

    specs = _residue_specs(tm, 256, nt)
    return pl.pallas_call(
        body, name=name, grid=(bl, nt), in_specs=specs * 2, out_specs=[specs[0]] * 2,
        out_shape=[_sds((t, 256), BF16), _sds((t, 256), F32)], scratch_shapes=[pltpu.VMEM((8, tm, LANES), F32)],
        compiler_params=_cparams(PAR, PAR),
    )(*os_, *ls_)


def _deltas(dycat, ya, yb, yd, lse_a, *, tm, name):
    t = ya.shape[0]
    bl = t // SEQ
    nt = SEQ // tm

    def body(dy_ref, ya_ref, yb_ref, yd_ref, la_ref, dy4, dy16, l4, l16, da1, da4, da16, db_ref, dd_ref, scr):
        e = _group_sum_matrix(256, True)
        dya = dy_ref[:, 0:256]
        dla = _seg_sum(dya * ya_ref[...].astype(F32), e)
        da1[...] = dla
        db_ref[...] = _seg_sum(dy_ref[:, 256:512] * yb_ref[...].astype(F32), e)
        dd_ref[...] = _seg_sum(dy_ref[:, 768:1024] * yd_ref[...].astype(F32), e)
        for k, (val, r4, r16) in enumerate(((dya, dy4, dy16), (la_ref[...], l4, l16), (dla, da4, da16))):
            _scatter_cols(scr, 2 * k, val)
            for d, ref in ((4, r4), (16, r16)):
                for r in range(d):
                    ref[r] = _read_residue(scr, 2 * k, 2, r, d)

    specs = _residue_specs(tm, 256, nt)
    nat = specs[0]
    shapes = _residue_shapes(bl, 256, F32)
    outs = pl.pallas_call(
        body, name=name, grid=(bl, nt),
        in_specs=[pl.BlockSpec((tm, 1024), lambda b, i: (b * nt + i, 0)), nat, nat, nat, nat],
        out_specs=specs[1:] + specs[1:] + specs + [nat, nat],
        out_shape=shapes[1:] + shapes[1:] + shapes + [shapes[0], shapes[0]],
        scratch_shapes=[pltpu.VMEM((6, tm, LANES), F32)],
        compiler_params=_cparams(PAR, PAR),
    )(dycat, ya, yb, yd, lse_a)
    return outs[0:2], outs[2:4], outs[4:7], outs[7], outs[8]


def _dense_fwd(qd, *, tq, name):
    t = qd.shape[0]
    bl = t // SEQ
    nq = SEQ // tq

    def body(q_ref, k_ref, v_ref, o_ref, l_ref):
        q = q_ref[...]
        for g in range(2):
            h0, h1 = 2 * g, 2 * g + 1
            q2 = jnp.concatenate([q[:, h0 * 64:(h0 + 1) * 64], q[:, h1 * 64:(h1 + 1) * 64]], axis=0)
            kg = k_ref[:, g * 64:(g + 1) * 64]
            vg = v_ref[:, g * 64:(g + 1) * 64]
            s = lax.dot_general(q2, kg, NT, preferred_element_type=F32)
            m = jnp.max(s, axis=1, keepdims=True)
            p = jnp.exp(s - m)
            den = jnp.sum(p, axis=1, keepdims=True)
            o2 = jnp.dot(p.astype(BF16), vg, preferred_element_type=F32) / den
            l2 = jnp.broadcast_to(m + jnp.log(den), (2 * tq, 64))
            o_ref[:, h0 * 64:(h0 + 1) * 64] = o2[:tq].astype(BF16)
            o_ref[:, h1 * 64:(h1 + 1) * 64] = o2[tq:].astype(BF16)
            l_ref[:, h0 * 64:(h0 + 1) * 64] = l2[:tq]
            l_ref[:, h1 * 64:(h1 + 1) * 64] = l2[tq:]

    q3 = qd.reshape(bl, SEQ, 512)
    o, lse = pl.pallas_call(
        body, name=name, grid=(bl, nq),
        in_specs=[pl.BlockSpec((None, tq, 256), lambda b, i: (b, i, 0)),
                  pl.BlockSpec((None, SEQ, 128), lambda b, i: (b, 0, 2)),
                  pl.BlockSpec((None, SEQ, 128), lambda b, i: (b, 0, 3))],
        out_specs=[pl.BlockSpec((None, tq, 256), lambda b, i: (b, i, 0))] * 2,
        out_shape=[_sds((bl, SEQ, 256), BF16), _sds((bl, SEQ, 256), F32)],
        compiler_params=_cparams(PAR, PAR),
    )(q3, q3, q3)
    return o.reshape(t, 256), lse.reshape(t, 256)


def _dense_bwd(qd, dycat, lse, delta, *, tq, name):
    t = qd.shape[0]
    bl = t // SEQ
    nq = SEQ // tq

    def body(q_ref, k_ref, v_ref, do_ref, l_ref, dl_ref, dq_ref, dk_ref, dv_ref, dkt, dvt):
        @pl.when(pl.program_id(1) == 0)
        def _():
            dkt[...] = jnp.zeros_like(dkt)
            dvt[...] = jnp.zeros_like(dvt)

        q = q_ref[...]
        do = do_ref[...].astype(BF16)
        lv = l_ref[...]
        dlv = dl_ref[...]
        for g in range(2):
            h0, h1 = 2 * g, 2 * g + 1
            q2 = jnp.concatenate([q[:, h0 * 64:(h0 + 1) * 64], q[:, h1 * 64:(h1 + 1) * 64]], axis=0)
            do2 = jnp.concatenate([do[:, h0 * 64:(h0 + 1) * 64], do[:, h1 * 64:(h1 + 1) * 64]], axis=0)
            l2 = jnp.concatenate([lv[:, h0 * 64:h0 * 64 + 1], lv[:, h1 * 64:h1 * 64 + 1]], axis=0)
            dl2 = jnp.concatenate([dlv[:, h0 * 64:h0 * 64 + 1], dlv[:, h1 * 64:h1 * 64 + 1]], axis=0)
            kg = k_ref[:, g * 64:(g + 1) * 64]
            vg = v_ref[:, g * 64:(g + 1) * 64]
            s = lax.dot_general(q2, kg, NT, preferred_element_type=F32)
            p = jnp.exp(s - l2)
            dp = lax.dot_general(do2, vg, NT, preferred_element_type=F32)
            ds = (p * (dp - dl2)).astype(BF16)
            dq2 = jnp.dot(ds, kg, preferred_element_type=F32)
            dq_ref[:, h0 * 64:(h0 + 1) * 64] = dq2[:tq].astype(BF16)
            dq_ref[:, h1 * 64:(h1 + 1) * 64] = dq2[tq:].astype(BF16)
            dkt[g * 64:(g + 1) * 64, :] += lax.dot_general(q2, ds, TN, preferred_element_type=F32)
            dvt[g * 64:(g + 1) * 64, :] += lax.dot_general(do2, p.astype(BF16), TN, preferred_element_type=F32)

        @pl.when(pl.program_id(1) == nq - 1)
        def _():
            dk_ref[...] = dkt[...].T.astype(BF16)
            dv_ref[...] = dvt[...].T.astype(BF16)

    q3 = qd.reshape(bl, SEQ, 512)
    tile = pl.BlockSpec((None, tq, 256), lambda b, i: (b, i, 0))
    full = pl.BlockSpec((None, SEQ, 128), lambda b, i: (b, 0, 0))
    dq, dk, dv = pl.pallas_call(
        body, name=name, grid=(bl, nq),
        in_specs=[tile, pl.BlockSpec((None, SEQ, 128), lambda b, i: (b, 0, 2)),
                  pl.BlockSpec((None, SEQ, 128), lambda b, i: (b, 0, 3)),
                  pl.BlockSpec((None, tq, 256), lambda b, i: (b, i, 3)), tile, tile],
        out_specs=[tile, full, full],
        out_shape=[_sds((bl, SEQ, 256), BF16), _sds((bl, SEQ, 128), BF16), _sds((bl, SEQ, 128), BF16)],
        scratch_shapes=[pltpu.VMEM((128, SEQ), F32), pltpu.VMEM((128, SEQ), F32)],
        compiler_params=_cparams(PAR, ARB),
    )(q3, q3, q3, dycat.reshape(bl, SEQ, 1024), lse.reshape(bl, SEQ, 256), delta.reshape(bl, SEQ, 256))
    return dq.reshape(t, 256), dk.reshape(t, 128), dv.reshape(t, 128)


def _c_norm(cv, gam, bet):
    vg = _gelu(cv)
    mu = jnp.mean(vg, axis=-1, keepdims=True)
    xc = vg - mu
    r = lax.rsqrt(jnp.mean(xc * xc, axis=-1, keepdims=True) + EPS)
    xhat = xc * r
    return xhat * gam + bet, xhat, r


def _c_fwd(proj, gam, bet, ws, bst, *, tm, name):
    t = proj.shape[0]
    nch = tm // C_CHUNK

    def body(u_ref, v_ref, g_ref, b_ref, ws_ref, bs_ref, y_ref):
        vn, _, _ = _c_norm(v_ref[...].astype(F32), g_ref[...], b_ref[...])
        vnb = vn.astype(BF16)
        for c in range(nch):
            rows = slice(c * C_CHUNK, (c + 1) * C_CHUNK)
            for g in range(C_GROUPS):
                gs = slice(g * 64, (g + 1) * 64)
                mixed = jnp.dot(ws_ref[g], vnb[rows, gs], preferred_element_type=F32) + bs_ref[:, gs]
                y_ref[rows, gs] = (_gelu(u_ref[rows, gs].astype(F32)) * mixed).astype(BF16)

    vec = pl.BlockSpec((1, 256), lambda i: (0, 0))
    return pl.pallas_call(
        body, name=name, grid=(t // tm,),
        in_specs=[pl.BlockSpec((tm, 256), lambda i: (i, 5)), pl.BlockSpec((tm, 256), lambda i: (i, 6)), vec, vec,
                  pl.BlockSpec((C_GROUPS, C_CHUNK, C_CHUNK), lambda i: (0, 0, 0)),
                  pl.BlockSpec((C_CHUNK, 256), lambda i: (0, 0))],
        out_specs=pl.BlockSpec((tm, 256), lambda i: (i, 0)), out_shape=_sds((t, 256), BF16),
        compiler_params=_cparams(PAR),
    )(proj, proj, gam, bet, ws, bst)


def _c_bwd(proj, dycat, gam, bet, ws, wst, bst, *, tm, name):
    t = proj.shape[0]
    nch = tm // C_CHUNK
    nstep = t // tm

    def body(u_ref, v_ref, dy_ref, g_ref, b_ref, ws_ref, wst_ref, bs_ref,
             du_ref, dv_ref, dws_ref, dbs_ref, dg_ref, db_ref, dvn_s):
        step = pl.program_id(0)

        @pl.when(step == 0)
        def _():
            dws_ref[...] = jnp.zeros_like(dws_ref)
            dbs_ref[...] = jnp.zeros_like(dbs_ref)
            dg_ref[...] = jnp.zeros_like(dg_ref)
            db_ref[...] = jnp.zeros_like(db_ref)

        cv = v_ref[...].astype(F32)
        gam_v = g_ref[...]
        vn, xhat, r = _c_norm(cv, gam_v, b_ref[...])
        vnb = vn.astype(BF16)
        for c in range(nch):
            rows = slice(c * C_CHUNK, (c + 1) * C_CHUNK)
            for g in range(C_GROUPS):
                gs = slice(g * 64, (g + 1) * 64)
                cu = u_ref[rows, gs].astype(F32)
                dy = dy_ref[rows, gs]
                mixed = jnp.dot(ws_ref[g], vnb[rows, gs], preferred_element_type=F32) + bs_ref[:, gs]
                du_ref[rows, gs] = (dy * mixed * _gelu_grad(cu)).astype(BF16)
                dmix = dy * _gelu(cu)
                dbs_ref[:, gs] += dmix
                dmb = dmix.astype(BF16)
                dws_ref[g] += lax.dot_general(dmb, vnb[rows, gs], NT, preferred_element_type=F32)
                dvn_s[rows, gs] = jnp.dot(wst_ref[g], dmb, preferred_element_type=F32)
        dvn = dvn_s[...]
        dg_ref[...] += jnp.sum(dvn * xhat, axis=0, keepdims=True)
        db_ref[...] += jnp.sum(dvn, axis=0, keepdims=True)
        dxh = dvn * gam_v
        dvg = r * (dxh - jnp.mean(dxh, axis=-1, keepdims=True) - xhat * jnp.mean(dxh * xhat, axis=-1, keepdims=True))
        dv_ref[...] = (dvg * _gelu_grad(cv)).astype(BF16)

        @pl.when(step == nstep - 1)
        def _():
            dbs_ref[...] = _seg_sum(dbs_ref[...], _group_sum_matrix(256, True))

    vec = pl.BlockSpec((1, 256), lambda i: (0, 0))
    mat = pl.BlockSpec((C_GROUPS, C_CHUNK, C_CHUNK), lambda i: (0, 0, 0))
    bsp = pl.BlockSpec((C_CHUNK, 256), lambda i: (0, 0))
    tile = pl.BlockSpec((tm, 256), lambda i: (i, 0))
    return pl.pallas_call(
        body, name=name, grid=(nstep,),
        in_specs=[pl.BlockSpec((tm, 256), lambda i: (i, 5)), pl.BlockSpec((tm, 256), lambda i: (i, 6)),
                  pl.BlockSpec((tm, 256), lambda i: (i, 2)), vec, vec, mat, mat, bsp],
        out_specs=[tile, tile, mat, bsp, vec, vec],
        out_shape=[_sds((t, 256), BF16), _sds((t, 256), BF16), _sds((C_GROUPS, C_CHUNK, C_CHUNK), F32),
                   _sds((C_CHUNK, 256), F32), _sds((1, 256), F32), _sds((1, 256), F32)],
        scratch_shapes=[pltpu.VMEM((tm, 256), F32)],
        compiler_params=_cparams(ARB),
    )(proj, proj, dycat, gam, bet, ws, wst, bst)


FF_TC = 128
FF_NB = D_FF // FF_TC
FF_CH = 64
FF_HALO = 16


def _taps(ref, r0, win, where):
    z = jnp.zeros((FF_HALO, win.shape[1]), F32)
    if where == "first":
        win[0:FF_HALO, :] = z
        win[FF_HALO:, :] = ref[0:FF_CH + FF_HALO, :].astype(F32)
    elif where == "last":
        win[0:FF_CH + FF_HALO, :] = ref[SEQ - FF_CH - FF_HALO:SEQ, :].astype(F32)
        win[FF_CH + FF_HALO:, :] = z
    else:
        win[...] = ref[pl.ds(pl.multiple_of(r0 - FF_HALO, FF_HALO), FF_CH + 2 * FF_HALO), :].astype(F32)
    return tuple(win[FF_HALO + o:FF_HALO + o + FF_CH, :] for o in (-1, 0, 1))


def _chunk_loop(step):
    step(0, lambda ref, win: _taps(ref, 0, win, "first"))

    def mid(i, carry):
        r0 = pl.multiple_of(i * FF_CH, FF_CH)
        step(r0, lambda ref, win: _taps(ref, r0, win, "mid"))
        return carry

    lax.fori_loop(1, SEQ // FF_CH - 1, mid, 0)
    step(SEQ - FF_CH, lambda ref, win: _taps(ref, SEQ - FF_CH, win, "last"))


def _conv3(taps, w_ref, b_ref):
    dn, md, up = taps
    return w_ref[0:1, :] * dn + w_ref[1:2, :] * md + w_ref[2:3, :] * up + b_ref[...]


def _ff_specs(order):
    def at(fn):
        return (lambda b, j: fn(b, j)) if order == "bj" else (lambda j, b: fn(b, j))
    hs = [pl.BlockSpec((None, SEQ, FF_TC), at(lambda b, j, o=o: (b, 0, j + o))) for o in (0, FF_NB)]
    ws = [pl.BlockSpec((3, FF_TC), at(lambda b, j, o=o: (0, j + o))) for o in (0, FF_NB)]
    bs = [pl.BlockSpec((1, FF_TC), at(lambda b, j, o=o: (0, j + o))) for o in (0, FF_NB)]
    return hs, ws, bs


def _conv_gate_fwd(h, cw, cb, *, name):
    t = h.shape[0]
    bl = t // SEQ

    def body(hg_ref, hu_ref, wg_ref, wu_ref, bg_ref, bu_ref, a_ref, cg_ref, cu_ref, win):
        def step(r0, taps):
            cg = _conv3(taps(hg_ref, win.at[0]), wg_ref, bg_ref)
            cu = _conv3(taps(hu_ref, win.at[1]), wu_ref, bu_ref)
            a_ref[pl.ds(r0, FF_CH), :] = (cg * _sigmoid(cg) * cu).astype(BF16)
            cg_ref[pl.ds(r0, FF_CH), :] = cg.astype(BF16)
            cu_ref[pl.ds(r0, FF_CH), :] = cu.astype(BF16)

        _chunk_loop(step)

    hs, ws, bs = _ff_specs("bj")
    h3 = h.reshape(bl, SEQ, 2 * D_FF)
    half = pl.BlockSpec((None, SEQ, FF_TC), lambda b, j: (b, 0, j))
    outs = pl.pallas_call(
        body, name=name, grid=(bl, FF_NB), in_specs=hs + ws + bs, out_specs=[half] * 3,
        out_shape=[_sds((bl, SEQ, D_FF), BF16)] * 3,
        scratch_shapes=[pltpu.VMEM((2, FF_CH + 2 * FF_HALO, FF_TC), F32)],
        compiler_params=_cparams(PAR, PAR),
    )(h3, h3, cw, cw, cb, cb)
    return [o.reshape(t, D_FF) for o in outs]


def _conv_gate_bwd(h, cg_all, cu_all, dact, cw, cb, *, name):
    t = h.shape[0]
    bl = t // SEQ

    def body(hg_ref, hu_ref, wg_ref, wu_ref, bg_ref, bu_ref, da_ref, cg_ref, cu_ref,
             dhg_ref, dhu_ref, dwg_ref, dwu_ref, dbg_ref, dbu_ref, dg_s, du_s, win, sums):
        @pl.when(pl.program_id(1) == 0)
        def _():
            for ref in (dwg_ref, dwu_ref, dbg_ref, dbu_ref):
                ref[...] = jnp.zeros_like(ref)

        sums[...] = jnp.zeros_like(sums)
        red = lambda x: jnp.sum(x.reshape(FF_CH // 8, 8, x.shape[1]), axis=0)

        def pass1(r0, taps):
            tg, tu = taps(hg_ref, win.at[0]), taps(hu_ref, win.at[1])
            cg = cg_ref[pl.ds(r0, FF_CH), :].astype(F32)
            cu = cu_ref[pl.ds(r0, FF_CH), :].astype(F32)
            da = da_ref[pl.ds(r0, FF_CH), :].astype(F32)
            sg = _sigmoid(cg)
            dcg = da * cu * (sg * (1.0 + cg * (1.0 - sg)))
            dcu = da * (cg * sg)
            dg_s[pl.ds(r0, FF_CH), :] = dcg
            du_s[pl.ds(r0, FF_CH), :] = dcu
            for half, (d, tp) in enumerate(((dcg, tg), (dcu, tu))):
                for k in range(3):
                    sums[4 * half + k] += red(d * tp[k])
                sums[4 * half + 3] += red(d)

        _chunk_loop(pass1)
        for half, (dw_ref, db_ref) in enumerate(((dwg_ref, dbg_ref), (dwu_ref, dbu_ref))):
            for k in range(3):
                dw_ref[k:k + 1, :] += jnp.sum(sums[4 * half + k], axis=0, keepdims=True)
            db_ref[...] += jnp.sum(sums[4 * half + 3], axis=0, keepdims=True)

        def pass2(r0, taps):
            for k, (s, w_ref, o_ref) in enumerate(((dg_s, wg_ref, dhg_ref), (du_s, wu_ref, dhu_ref))):
                dn, md, up = taps(s, win.at[k])
                o_ref[pl.ds(r0, FF_CH), :] = (w_ref[0:1, :] * up + w_ref[1:2, :] * md + w_ref[2:3, :] * dn).astype(BF16)

        _chunk_loop(pass2)

    hs, ws, bs = _ff_specs("jb")
    half = pl.BlockSpec((None, SEQ, FF_TC), lambda j, b: (b, 0, j))
    wsp = pl.BlockSpec((3, FF_TC), lambda j, b: (0, j))
    bsp = pl.BlockSpec((1, FF_TC), lambda j, b: (0, j))
    h3 = h.reshape(bl, SEQ, 2 * D_FF)
    dhg, dhu, dwg, dwu, dbg, dbu = pl.pallas_call(
        body, name=name, grid=(FF_NB, bl), in_specs=hs + ws + bs + [half] * 3,
        out_specs=[half, half, wsp, wsp, bsp, bsp],
        out_shape=[_sds((bl, SEQ, D_FF), BF16), _sds((bl, SEQ, D_FF), BF16), _sds((3, D_FF), F32), _sds((3, D_FF), F32),
                   _sds((1, D_FF), F32), _sds((1, D_FF), F32)],
        scratch_shapes=[pltpu.VMEM((SEQ, FF_TC), F32), pltpu.VMEM((SEQ, FF_TC), F32),
                        pltpu.VMEM((2, FF_CH + 2 * FF_HALO, FF_TC), F32), pltpu.VMEM((8, 8, FF_TC), F32)],
        compiler_params=_cparams(PAR, ARB),
    )(h3, h3, cw, cw, cb, cb, *[a.reshape(bl, SEQ, D_FF) for a in (dact, cg_all, cu_all)])
    return (dhg.reshape(t, D_FF), dhu.reshape(t, D_FF), jnp.concatenate([dwg, dwu], axis=1),
            jnp.concatenate([dbg, dbu], axis=1))


def _ple_fwd(x2, gain, wg, pe, pe_blk, wp, *, tm, name):
    t, k = x2.shape

    def body(x_ref, g_ref, wg_ref, pe_ref, wp_ref, hn_ref, x3_ref, gt_ref, pp_ref):
        x = x_ref[...]
        r = lax.rsqrt(jnp.mean(x * x, axis=-1, keepdims=True) + EPS)
        hn = (x * r * g_ref[...]).astype(BF16)
        hn_ref[...] = hn
        gate = _sigmoid(jnp.dot(hn, wg_ref[...], preferred_element_type=F32))
        pp = jnp.dot(pe_ref[...].astype(BF16), wp_ref[...], preferred_element_type=F32)
        gt_ref[...] = gate.astype(BF16)
        pp_ref[...] = pp.astype(BF16)
        x3_ref[...] = x + pp * gate

    row = pl.BlockSpec((tm, k), lambda i: (i, 0))
    return pl.pallas_call(
        body, name=name, grid=(t // tm,),
        in_specs=[row, pl.BlockSpec((1, k), lambda i: (0, 0)), pl.BlockSpec((k, k), lambda i: (0, 0)),
                  pl.BlockSpec((tm, PLE_DIM), lambda i: (pe_blk + i, 0)), pl.BlockSpec((PLE_DIM, k), lambda i: (0, 0))],
        out_specs=[row, row, row, row],
        out_shape=[_sds((t, k), BF16), _sds((t, k), F32), _sds((t, k), BF16), _sds((t, k), BF16)],
        compiler_params=_cparams(PAR),
    )(x2, gain, wg, pe, wp)


def _ple_bwd_ew(dx3, gate, pp, *, tm, name):
    t, n = dx3.shape

    def body(d_ref, g_ref, p_ref, dz_ref, dpp_ref):
        d, g = d_ref[...], g_ref[...]
        dz_ref[...] = (d * p_ref[...] * g * (1.0 - g)).astype(BF16)
        dpp_ref[...] = (d * g).astype(BF16)

    spec = pl.BlockSpec((tm, n), lambda i: (i, 0))
    return pl.pallas_call(
        body, name=name, grid=(t // tm,), in_specs=[spec] * 3, out_specs=[spec] * 2,
        out_shape=[_sds((t, n), BF16)] * 2, compiler_params=_cparams(PAR),
    )(dx3, gate, pp)


def _loss_head(y, tgt, *, tm, name):
    t, d = y.shape

    def body(y_ref, t_ref, l_ref, dy_ref):
        @pl.when(pl.program_id(0) == 0)
        def _():
            l_ref[...] = jnp.zeros_like(l_ref)

        e = y_ref[...] - t_ref[...]
        dy_ref[...] = e * (1.0 / d)
        s = jnp.sum(jnp.sum(e * e, axis=1, keepdims=True), axis=0, keepdims=True)
        l_ref[...] += jnp.broadcast_to(s * (0.5 / d), (8, 128))

    spec = pl.BlockSpec((tm, d), lambda i: (i, 0))
    return pl.pallas_call(
        body, name=name, grid=(t // tm,), in_specs=[spec, spec],
        out_specs=[pl.BlockSpec((8, 128), lambda i: (0, 0)), spec],
        out_shape=[_sds((8, 128), F32), _sds((t, d), F32)], compiler_params=_cparams(ARB),
    )(y, tgt)


BIAS_PC = 8192


def _onehot(bucket_row):
    rows = lax.broadcasted_iota(jnp.int32, (REL_BUCKETS, bucket_row.shape[1]), 0)
    return (rows == bucket_row).astype(BF16)


def _dot3(x, onehot, dims):
    acc = None
    for _ in range(3):
        term = x.astype(BF16)
        part = lax.dot_general(term, onehot, dims, preferred_element_type=F32)
        acc = part if acc is None else acc + part
        x = x - term.astype(F32)
    return acc


def _bias_lookup(table_t, bucket, *, name):
    h = table_t.shape[0]
    p = bucket.shape[1]

    def body(t_ref, b_ref, o_ref):
        bk = b_ref[...]
        val = _dot3(t_ref[...], _onehot(bk), (((1,), (0,)), ((), ())))
        o_ref[...] = jnp.where(bk >= 0, val, NEG_INF)

    return pl.pallas_call(
        body, name=name, grid=(p // BIAS_PC,),
        in_specs=[pl.BlockSpec((h, REL_BUCKETS), lambda i: (0, 0)), pl.BlockSpec((1, BIAS_PC), lambda i: (0, i))],
        out_specs=pl.BlockSpec((h, BIAS_PC), lambda i: (0, i)), out_shape=_sds((h, p), F32),
        compiler_params=_cparams(PAR),
    )(table_t, bucket)


def _bucket_reduce(dbiases, bucket, *, name):
    h, p = dbiases[0].shape
    nl = len(dbiases)

    def body(*refs):
        b_ref, o_ref = refs[nl], refs[nl + 1]

        @pl.when(pl.program_id(0) == 0)
        def _():
            o_ref[...] = jnp.zeros_like(o_ref)

        d = refs[0][...]
        for d_ref in refs[1:nl]:
            d = d + d_ref[...]
        o_ref[...] += _dot3(d, _onehot(b_ref[...]), NT)

    return pl.pallas_call(
        body, name=name, grid=(p // BIAS_PC,),
        in_specs=[pl.BlockSpec((h, BIAS_PC), lambda i: (0, i))] * nl + [pl.BlockSpec((1, BIAS_PC), lambda i: (0, i))],
        out_specs=pl.BlockSpec((h, REL_BUCKETS), lambda i: (0, 0)), out_shape=_sds((h, REL_BUCKETS), F32),
        compiler_params=_cparams(ARB),
    )(*dbiases, bucket)


def _adamw_math(w, g, m, v):
    m = ADAM_B1 * m + (1.0 - ADAM_B1) * g
    v = ADAM_B2 * v + (1.0 - ADAM_B2) * (g * g)
    m_hat = m / (1.0 - ADAM_B1 ** ADAM_STEP)
    v_hat = v / (1.0 - ADAM_B2 ** ADAM_STEP)
    delta = -ADAM_LR * (m_hat / (jnp.sqrt(v_hat) + ADAM_EPS) + ADAM_WD * w)
    return delta, m, v


def _adamw_reduce(parts, w, m, v, *, tr, name):
    nl = len(parts)
    rows, c = w.shape
    r = rows // nl
    nt = r // tr

    def body(*refs):
        p_refs = refs[:nl]
        w_ref, m_ref, v_ref, g_ref, d_ref, nm_ref, nv_ref = refs[nl:]
        for li, p_ref in enumerate(p_refs):
            @pl.when(pl.program_id(0) == li)
            def _(p_ref=p_ref):
                g = p_ref[0].astype(F32)
                for k in range(1, N_DEV):
                    g = g + p_ref[k].astype(F32)
                d, nm, nv = _adamw_math(w_ref[...], g, m_ref[...], v_ref[...])
                g_ref[...] = g
                d_ref[...] = d
                nm_ref[...] = nm
                nv_ref[...] = nv

    def part_map(li):
        return lambda l, i: (0, jnp.where(l == li, i, jnp.where(l < li, 0, nt - 1)), 0)

    spec = pl.BlockSpec((tr, c), lambda l, i: (l * nt + i, 0))
    return pl.pallas_call(
        body, name=name, grid=(nl, nt),
        in_specs=[pl.BlockSpec((N_DEV, tr, c), part_map(li)) for li in range(nl)] + [spec, spec, spec],
        out_specs=[spec] * 4, out_shape=[_sds((rows, c), F32)] * 4, compiler_params=_cparams(ARB, ARB),
    )(*parts, w, m, v)


def _adamw_plain(g, w, m, v, *, name):
    def body(g_ref, w_ref, m_ref, v_ref, d_ref, nm_ref, nv_ref):
        d, nm, nv = _adamw_math(w_ref[...], g_ref[...], m_ref[...], v_ref[...])
        d_ref[...] = d
        nm_ref[...] = nm
        nv_ref[...] = nv

    return pl.pallas_call(body, name=name, out_shape=[_sds(w.shape, F32)] * 3)(g, w, m, v)


def _mesh_pos():
    return lax.axis_index("x"), lax.axis_index("y"), lax.axis_index("c")


def _allgather_body(x_refs, out_refs, send_sems, recv_sems, local_sems, slot):
    x, y, c = _mesh_pos()
    me, sibling = (x, y, c), (x, y, 1 - c)
    chips = [(1 - x, y), (x, 1 - y), (1 - x, 1 - y)]
    waits = []
    for a, (x_ref, out_ref) in enumerate(zip(x_refs, out_refs)):
        def copy(k, block, to, src=None, out_ref=out_ref, a=a):
            return pltpu.make_async_remote_copy(
                src_ref=slot(out_ref, block) if src is None else src, dst_ref=slot(out_ref, block),
                send_sem=send_sems.at[a, k], recv_sem=recv_sems.at[a, k], device_id=to, device_id_type=MESH)

        mine = pltpu.make_async_copy(x_ref, slot(out_ref, me), local_sems.at[a])
        mine.start()
        first = [copy(0, me, sibling, src=x_ref)]
        first += [copy(1 + j, me, (*chip, c), src=x_ref) for j, chip in enumerate(chips)]
        for cp in first:
            cp.start()
        waits.append((copy, mine, first))
    sends = []
    for copy, mine, first in waits:
        passed = [copy(4 + j, (*chip, c), sibling) for j, chip in enumerate(chips)]
        for j, chip in enumerate(chips):
            copy(1 + j, (*chip, c), me).wait_recv()
            passed[j].start()
        sends.append(passed)
    for (copy, mine, first), passed in zip(waits, sends):
        copy(0, sibling, me).wait_recv()
        for j, chip in enumerate(chips):
            copy(4 + j, (*chip, 1 - c), me).wait_recv()
        for cp in first + passed:
            cp.wait_send()
        mine.wait()


PEER_FLIPS = ((0, 0, 1), (1, 0, 0), (0, 1, 0), (1, 1, 0), (1, 0, 1), (0, 1, 1), (1, 1, 1))


def _peer_copies(x_refs, land_refs, send_sem, recv_sem, scatter):
    x, y, c = _mesh_pos()
    me = 4 * x + 2 * y + c
    copies = []
    for x_ref, land_ref in zip(x_refs, land_refs):
        for fx, fy, fc in PEER_FLIPS:
            px, py, pc = x ^ fx, y ^ fy, c ^ fc
            src = x_ref.at[4 * px + 2 * py + pc] if scatter else x_ref
            copies.append(pltpu.make_async_remote_copy(
                src_ref=src, dst_ref=land_ref.at[me], send_sem=send_sem, recv_sem=recv_sem,
                device_id=(px, py, pc), device_id_type=MESH))
    return copies


def _sc_exchange(xs, *, scatter, collective_id, name):
    na = len(xs)
    land_shapes = [x.shape if scatter else (N_DEV,) + x.shape for x in xs]

    def body(*refs):
        x_refs, land_refs = refs[:na], refs[na:2 * na]
        send_sem, recv_sem, local_sem = refs[2 * na:]
        x, y, c = _mesh_pos()
        me = 4 * x + 2 * y + c
        barrier = pltpu.get_barrier_semaphore()
        for fx, fy, fc in PEER_FLIPS:
            pl.semaphore_signal(barrier, inc=1, device_id=(x ^ fx, y ^ fy, c ^ fc), device_id_type=MESH)
        pl.semaphore_wait(barrier, len(PEER_FLIPS))
        for x_ref, land_ref in zip(x_refs, land_refs):
            own = pltpu.make_async_copy(x_ref.at[me] if scatter else x_ref, land_ref.at[me], local_sem)
            own.start()
            own.wait()
        copies = _peer_copies(x_refs, land_refs, send_sem, recv_sem, scatter)
        for cp in copies:
            cp.start()
        for cp in copies:
            cp.wait()

    return pl.kernel(
        body, name=name, out_type=[_sds(s, x.dtype) for s, x in zip(land_shapes, xs)],
        mesh=plsc.ScalarSubcoreMesh(axis_name="sequencer", num_cores=1),
        scratch_types=[pltpu.SemaphoreType.DMA, pltpu.SemaphoreType.DMA, pltpu.SemaphoreType.DMA],
        compiler_params=pltpu.CompilerParams(collective_id=collective_id),
    )(*xs)


def _sc_allgather(xs, *, collective_id, name):
    na = len(xs)

    def body(*refs):
        x_refs, out_refs = refs[:na], refs[na:2 * na]
        send_sems, recv_sems, local_sems = refs[2 * na:]
        x, y, c = _mesh_pos()
        barrier = pltpu.get_barrier_semaphore()
        for fx, fy, fc in PEER_FLIPS:
            pl.semaphore_signal(barrier, inc=1, device_id=(x ^ fx, y ^ fy, c ^ fc), device_id_type=MESH)
        pl.semaphore_wait(barrier, len(PEER_FLIPS))
        _allgather_body(x_refs, out_refs, send_sems, recv_sems, local_sems,
                        lambda ref, pos: ref.at[4 * pos[0] + 2 * pos[1] + pos[2]])

    return pl.kernel(
        body, name=name, out_type=[_sds((N_DEV,) + x.shape, x.dtype) for x in xs],
        mesh=plsc.ScalarSubcoreMesh(axis_name="sequencer", num_cores=1),
        scratch_types=[pltpu.SemaphoreType.DMA((na, 7)), pltpu.SemaphoreType.DMA((na, 7)),
                       pltpu.SemaphoreType.DMA((na,))],
        compiler_params=pltpu.CompilerParams(collective_id=collective_id),
    )(*xs)


def _allgather_vmem(x, *, name):
    r, c = x.shape

    def body(x_ref, out_ref, send_sems, recv_sems, local_sems):
        _allgather_body([x_ref], [out_ref], send_sems, recv_sems, local_sems,
                        lambda ref, pos: ref.at[pl.ds((4 * pos[0] + 2 * pos[1] + pos[2]) * r, r), :])

    vm = pl.BlockSpec(memory_space=pltpu.VMEM)
    return pl.pallas_call(
        body, name=name, in_specs=[vm], out_specs=vm, out_shape=_sds((N_DEV * r, c), x.dtype),
        scratch_shapes=[pltpu.SemaphoreType.DMA((1, 7)), pltpu.SemaphoreType.DMA((1, 7)),
                        pltpu.SemaphoreType.DMA((1,))],
    )(x)


def _sum_slots(gathered, *, name):
    _, r, c = gathered.shape

    def body(g_ref, o_ref):
        acc = g_ref[0]
        for k in range(1, N_DEV):
            acc = acc + g_ref[k]
        o_ref[...] = acc

    return pl.pallas_call(body, name=name, out_shape=_sds((r, c), gathered.dtype))(gathered)


def _t5_bucket(rel):
    nb = REL_BUCKETS // 2
    ret = jnp.where(rel > 0, nb, 0)
    n = jnp.abs(rel)
    max_exact = nb // 2
    nf = jnp.maximum(n, 1).astype(F32)
    large = max_exact + (jnp.log(nf / max_exact) / math.log(REL_MAX_DIST / max_exact)
                         * (nb - max_exact)).astype(jnp.int32)
    large = jnp.minimum(large, nb - 1)
    return ret + jnp.where(n < max_exact, n, large)


def _band_pattern(block, radius, dil):
    kw = block + 2 * radius
    rel = jnp.arange(kw)[None, :] - radius - jnp.arange(block)[:, None]
    return jnp.where(jnp.abs(rel) <= radius, _t5_bucket(rel * dil), -1).astype(jnp.int32).reshape(1, block * kw)


def _rope_tables():
    lane = np.arange(64)
    seg, j = lane // 32, lane % 32
    inv = ROPE_THETA ** (-jnp.arange(0, 32, 2, dtype=F32) / 32)
    tpos = jnp.arange(SEQ)
    pos = jnp.where(jnp.asarray(seg)[None, :] == 0, (tpos // GRID_W)[:, None], (tpos % GRID_W)[:, None])
    ang = pos.astype(F32) * inv[jnp.asarray(j % 16)][None, :]
    cos = jnp.cos(ang)
    sins = jnp.where(jnp.asarray(j)[None, :] < 16, -jnp.sin(ang), jnp.sin(ang))
    return jnp.tile(cos, (1, 4)), jnp.tile(sins, (1, 4))


A_Q, A_K, A_V = (256, 0), (256, 1), (256, 2)
B_Q, B_K, B_V = (256, 0), (128, 2), (128, 3)
A_HEADS = dict(rad=A_RADIUS, nh=4, nkv=4)
B_HEADS = dict(rad=SWA_RADIUS, nh=4, nkv=2)


def _local_step(x, pe, tgt, rel_bias, wts, matmul_weights, grads_ready):
    t = x.shape[0]
    bl = t // SEQ
    cos, sins = _rope_tables()
    blocks_a = [min(BAND_BLOCK, SEQ // d) for d in DILATIONS]
    pats_a = [_band_pattern(blk, A_RADIUS, d) for blk, d in zip(blocks_a, DILATIONS)]
    pat_b = _band_pattern(BAND_BLOCK, SWA_RADIUS, 1)
    table_t = rel_bias.T
    bias_a = [_bias_lookup(table_t[:4], pt, name=f"bias_a{ci}").reshape(4, blk, blk + 2 * A_RADIUS)
              for ci, (pt, blk) in enumerate(zip(pats_a, blocks_a))]
    bias_b = _bias_lookup(table_t[4:], pat_b, name="bias_b").reshape(4, BAND_BLOCK, BAND_BLOCK + 2 * SWA_RADIUS)
    nat4 = lambda a: a.reshape(bl, 1, SEQ, a.shape[-1])

    saved = []
    for li in range(DEPTH):
        w = dict(wts[li])
        w.update(matmul_weights(li, "in", x))
        hn0, proj = _norm_mm((x,), w["g_mix"], w["w_in"], None, tm=1024, tn=1152, name="mix_in_fwd", out_dtype=BF16)
        qa1, qa4, qa16, qb, qd = _qkprep_fwd(proj, w["qk_gains"], cos, sins, tm=512, name="qkprep_fwd")
        qa = (nat4(qa1), qa4, qa16)
        oa, la = [], []
        for ci in range(3):
            o, l = _band_fwd(qa[ci], A_Q, A_K, A_V, bias_a[ci], None, name=f"band_a{ci}_fwd", **A_HEADS)
            oa.append(o)
            la.append(l)
        oa[0], la[0] = oa[0].reshape(t, 256), la[0].reshape(t, 256)
        ya, lse_a = _combine_a(oa, la, tm=512, name="combine_a")
        yb, lse_b = _band_fwd(nat4(qb), B_Q, B_K, B_V, bias_b, w["sink_t"], name="band_b_fwd", **B_HEADS)
        yb = yb.reshape(t, 256)
        yc = _c_fwd(proj, w["c_g"], w["c_b"], w["c_ws"], w["c_bst"], tm=512, name="c_fwd")
        yd, lse_d = _dense_fwd(qd, tq=256, name="dense_fwd")
        w.update(matmul_weights(li, "rest", yd))
        mixed, x1 = _norm_mm((ya, yb, yc, yd), w["out_gain"], w["w_out"], x, tm=1024, tn=1024, name="mix_out_fwd")
        hn1, h = _norm_mm((x1,), w["g_ffn"], w["w_up"], None, tm=1024, tn=1408, name="ffn_up_fwd", out_dtype=BF16)
        act, cg, cu = _conv_gate_fwd(h, w["conv_w"], w["conv_b"], name="conv_gate_fwd")
        x2 = _mm(act, w["w_down"], "nn", x1, tm=1024, tn=1024, out_dtype=F32, name="ffn_down_fwd")
        hn2, x3, gate, pp = _ple_fwd(x2, w["g_ple"], w["w_gate"], pe, li * (t // 1024), w["w_proj"], tm=1024,
                                     name="ple_fwd")
        saved.append(dict(w=w, x0=x, hn0=hn0, proj=proj, qa=qa, qb=qb, qd=qd, ya=ya, lse_a=lse_a, yb=yb, lse_b=lse_b,
                          yc=yc, yd=yd, lse_d=lse_d, mixed=mixed, x1=x1, hn1=hn1, h=h, cg=cg, cu=cu, act=act, x2=x2, hn2=hn2,
                          gate=gate, pp=pp))
        x = x3

    loss_tile, dx = _loss_head(x, tgt, tm=512, name="loss_head")
    grads = [None] * DEPTH
    dbias_a, dbias_bs = [[], [], []], []
    for li in reversed(range(DEPTH)):
        s = saved[li]
        w = s["w"]
        g = {}
        dz, dpp = _ple_bwd_ew(dx, s["gate"], s["pp"], tm=512, name="ple_bwd_ew")
        g["w_gate"] = _mm(s["hn2"], dz, "tn", None, tm=1024, tn=512, out_dtype=BF16, name="dw_gate")
        g["w_proj"] = _mm(pe, dpp, "tn", None, tm=256, tn=1024, out_dtype=BF16, name="dw_proj", a_rows=(li, t))
        dx2, dx2b, g["g_ple"] = _mm_bt_normbwd((dz,), w["w_gate"], (s["x2"],), w["g_ple"], dx, tm=1024, tn=1024,
                                               name="ple_bwd", emit_bf16=True)
        g["w_down"] = _mm(s["act"], dx2b, "tn", None, tm=1408, tn=512, out_dtype=BF16, name="dw_down")
        dact = _mm(dx2b, w["w_down"], "nt", None, tm=1024, tn=1408, out_dtype=BF16, name="ffn_down_bwd")
        dhg, dhu, g["conv_w"], g["conv_b"] = _conv_gate_bwd(s["h"], s["cg"], s["cu"], dact, w["conv_w"], w["conv_b"],
                                                            name="conv_gate_bwd")
        g["w_up"] = jnp.concatenate(
            [_mm(s["hn1"], dhalf, "tn", None, tm=1024, tn=1408, out_dtype=BF16, name=f"dw_up_{nm}")
             for nm, dhalf in (("gate", dhg), ("up", dhu))], axis=1)
        dx1, dx1b, g["g_ffn"] = _mm_bt_normbwd((dhg, dhu), w["w_up"], (s["x1"],), w["g_ffn"], dx2, tm=1024, tn=1408,
                                               name="ffn_up_bwd", emit_bf16=True)
        g["w_out"] = _mm(s["mixed"], dx1b, "tn", None, tm=1024, tn=512, out_dtype=BF16, name="dw_out")
        grads_ready(li, "mid", g)
        dycat, g["out_gain"] = _mm_bt_normbwd((dx1b,), w["w_out"], (s["ya"], s["yb"], s["yc"], s["yd"]), w["out_gain"],
                                              None, tm=1024, tn=1024, name="mix_out_bwd")
        dy_r, lse_r, dl_a, dl_b, dl_d = _deltas(dycat, s["ya"], s["yb"], s["yd"], s["lse_a"], tm=512, name="deltas")
        dy_a = (nat4(dycat),) + tuple(dy_r)
        lse_a = (nat4(s["lse_a"]),) + tuple(lse_r)
        dl_a = (nat4(dl_a[0]),) + tuple(dl_a[1:])
        da = []
        for ci in range(3):
            dq, dk, dv, dbias = _band_bwd(s["qa"][ci], A_Q, A_K, A_V, bias_a[ci], None, dy_a[ci], 0, lse_a[ci],
                                          dl_a[ci], name=f"band_a{ci}_bwd", **A_HEADS)
            if ci == 0:
                dq, dk, dv = (a.reshape(t, 256) for a in (dq, dk, dv))
            da.append((dq, dk, dv))
            dbias_a[ci].append(dbias.reshape(4, -1))
        dqb, dkb, dvb, dbias_b, dsink = _band_bwd(nat4(s["qb"]), B_Q, B_K, B_V, bias_b, w["sink_t"], nat4(dycat), 1,
                                                  nat4(s["lse_b"]), nat4(dl_b), name="band_b_bwd", **B_HEADS)
        dbias_bs.append(dbias_b.reshape(4, -1))
        g["sink"] = dsink[:, 0, 0]
        dd = _dense_bwd(s["qd"], dycat, s["lse_d"], dl_d, tq=128, name="dense_bwd")
        dcu, dcv, g["c_ws"], dbs, g["c_g"], g["c_b"] = _c_bwd(s["proj"], dycat, w["c_g"], w["c_b"], w["c_ws"],
                                                               w["c_wst"], w["c_bst"], tm=512, name="c_bwd")
        g["c_bs"] = dbs[:, ::64].T
        db = (dqb.reshape(t, 256), dkb.reshape(t, 128), dvb.reshape(t, 128))
        dproj, dgains = _qkprep_bwd(s["proj"], da, db, dd, dcu, dcv, w["qk_gains"], cos, sins, tm=512, name="qkprep_bwd")
        g["qk_gain"] = dgains[:6, :64].reshape(3, 2, HEAD_DIM)
        g["w_in"] = _mm(s["hn0"], dproj, "tn", None, tm=1024, tn=1152, out_dtype=BF16, name="dw_in")
        dx, g["g_mix"] = _mm_bt_normbwd((dproj,), w["w_in"], (s["x0"],), w["g_mix"], dx1, tm=1024, tn=1152,
                                        name="mix_in_bwd")
        grads[li] = g
        grads_ready(li, "end", g)
    d_table_a = sum(_bucket_reduce(dbias_a[ci], pats_a[ci], name=f"bucket_a{ci}") for ci in range(3))
    d_table_b = _bucket_reduce(dbias_bs, pat_b, name="bucket_b")
    d_rel_bias = jnp.concatenate([d_table_a, d_table_b], axis=0).T
    return loss_tile[0, 0], dx, grads, d_rel_bias


WEIGHT_NAMES = ("rel_bias", "ln_mix_g", "w_in", "qk_gain", "sink", "c_norm_g", "c_norm_b", "c_ws", "c_bs", "out_gain",
                "w_out", "ln_ffn_g", "w_up", "conv_w", "conv_b", "w_down", "ln_ple_g", "w_ple_gate", "w_ple_proj")
COL_SHARDED = ("w_in", "w_up", "w_ple_proj")
ROW_SHARDED = ("w_out", "w_down", "w_ple_gate")
SMALL_SHARDED = ("conv_w", "out_gain")
REPLICATED = tuple(n for n in WEIGHT_NAMES if n not in COL_SHARDED + ROW_SHARDED + SMALL_SHARDED)
LOCAL_GRAD_KEY = {"ln_mix_g": "g_mix", "ln_ffn_g": "g_ffn", "ln_ple_g": "g_ple", "c_norm_g": "c_g", "c_norm_b": "c_b",
                  "w_ple_gate": "w_gate", "w_ple_proj": "w_proj"}


def _full_from_gathered(name, gathered):
    _, r, c = gathered.shape
    if name in ROW_SHARDED:
        return gathered.reshape(N_DEV * r, c)
    return jnp.transpose(gathered, (1, 0, 2)).reshape(r, N_DEV * c)


def _slots_from_full(name, full):
    rows, cols = full.shape
    if name in ROW_SHARDED:
        return full.reshape(N_DEV, rows // N_DEV, cols)
    return jnp.transpose(full.reshape(rows, N_DEV, cols // N_DEV), (1, 0, 2))


def _piece_rows(shape):
    return -(-int(np.prod(shape)) // 1024) * 8


def _pack_rows(arrays):
    pieces = []
    for a in arrays:
        n, rows = int(np.prod(a.shape)), _piece_rows(a.shape)
        flat = a.astype(F32).reshape(-1)
        if n != rows * LANES:
            flat = jnp.pad(flat, (0, rows * LANES - n))
        pieces.append(flat.reshape(rows, LANES))
    return jnp.concatenate(pieces, axis=0)


def _unpack_rows(packed, shapes):
    out, off = [], 0
    for shp in shapes:
        n, rows = int(np.prod(shp)), _piece_rows(shp)
        piece = packed[off:off + rows]
        out.append((piece if n == rows * LANES else piece.reshape(-1)[:n]).reshape(shp))
        off += rows
    return out


def kernel(x, p, rel_bias, ln_mix_g, w_in, qk_gain, sink, c_norm_g, c_norm_b, c_ws, c_bs, out_gain, w_out, ln_ffn_g, w_up, conv_w, conv_b, w_down, ln_ple_g, w_ple_gate, w_ple_proj, loss_target, m_rel_bias, m_ln_mix_g, m_w_in, m_qk_gain, m_sink, m_c_norm_g, m_c_norm_b, m_c_ws, m_c_bs, m_out_gain, m_w_out, m_ln_ffn_g, m_w_up, m_conv_w, m_conv_b, m_w_down, m_ln_ple_g, m_w_ple_gate, m_w_ple_proj, v_rel_bias, v_ln_mix_g, v_w_in, v_qk_gain, v_sink, v_c_norm_g, v_c_norm_b, v_c_ws, v_c_bs, v_out_gain, v_w_out, v_ln_ffn_g, v_w_up, v_conv_w, v_conv_b, v_w_down, v_ln_ple_g, v_w_ple_gate, v_w_ple_proj):
    env = dict(locals())
    wt = {n: env[n] for n in WEIGHT_NAMES}
    mom_m = {n: env["m_" + n] for n in WEIGHT_NAMES}
    mom_v = {n: env["v_" + n] for n in WEIGHT_NAMES}
    bl = x.shape[0]
    t = bl * SEQ
    me = 4 * lax.axis_index("x") + 2 * lax.axis_index("y") + lax.axis_index("c")

    big = COL_SHARDED + ROW_SHARDED
    full = {}
    small_shapes = [wt[n].shape for n in SMALL_SHARDED]
    small = _allgather_vmem(_pack_rows([wt[n] for n in SMALL_SHARDED]), name="gather_small")
    small = small.reshape(N_DEV, -1)
    off = 0
    for n, shp in zip(SMALL_SHARDED, small_shapes):
        cnt = int(np.prod(shp))
        g = small[:, off:off + cnt].reshape((N_DEV,) + tuple(shp))
        full[n] = jnp.transpose(g, (1, 2, 0, 3)).reshape(shp[0], shp[1], N_DEV * shp[2])
        off += _piece_rows(shp) * LANES

    def head_gain(li, a, b, reps):
        g = jnp.tile(qk_gain[li, a, b], reps)
        return jnp.pad(g, (0, 256 - g.shape[0]))

    wts = []
    for li in range(DEPTH):
        rows = [head_gain(li, 0, 0, 4), head_gain(li, 0, 1, 4), head_gain(li, 1, 0, 4), head_gain(li, 1, 1, 2),
                head_gain(li, 2, 0, 4), head_gain(li, 2, 1, 2), jnp.zeros((256,), F32), jnp.zeros((256,), F32)]
        wts.append(dict(
            g_mix=ln_mix_g[li].reshape(1, -1), qk_gains=jnp.stack(rows),
            sink_t=jnp.broadcast_to(sink[li][:, None, None], (4, 8, 128)),
            c_g=c_norm_g[li].reshape(1, -1), c_b=c_norm_b[li].reshape(1, -1), c_ws=c_ws[li].astype(BF16),
            c_wst=jnp.transpose(c_ws[li], (0, 2, 1)).astype(BF16), c_bst=jnp.repeat(c_bs[li].T, 64, axis=1),
            out_gain=full["out_gain"][li].reshape(1, -1), g_ffn=ln_ffn_g[li].reshape(1, -1),
            conv_w=full["conv_w"][li], conv_b=conv_b[li].reshape(1, -1), g_ple=ln_ple_g[li].reshape(1, -1)))

    local_key = {"w_ple_gate": "w_gate", "w_ple_proj": "w_proj"}

    gather_names = {"in": ("w_in",), "rest": tuple(n for n in big if n != "w_in")}
    gathered = {}
    for cid, (li, names) in enumerate(((0, gather_names["in"]), (0, gather_names["rest"]), (1, big))):
        lands = _sc_allgather([wt[n][li].astype(BF16) for n in names], collective_id=cid,
                              name=f"gather_{li}_{len(names)}")
        gathered.setdefault(li, {}).update(zip(names, lands))

    def matmul_weights(li, part, after):
        out = {}
        for n in gather_names[part]:
            g, _ = lax.optimization_barrier((gathered[li][n], after))
            out[local_key.get(n, n)] = _full_from_gathered(n, g)
        return out

    mid_names = ("w_ple_gate", "w_ple_proj", "w_down", "w_up", "w_out")
    end_names = ("w_in",)
    landed = {}

    def start_exchange(li, names, g, tag, cid):
        slots = [_slots_from_full(n, g[local_key.get(n, n)]) for n in names]
        lands = _sc_exchange(slots, scatter=True, collective_id=cid, name=f"grads_{li}_{tag}")
        landed.update({(n, li): land for n, land in zip(names, lands)})

    def grads_ready(li, stage, g):
        if li == 0:
            start_exchange(li, mid_names if stage == "mid" else end_names, g, stage, 5 if stage == "mid" else 6)
        elif stage == "end":
            start_exchange(li, mid_names + end_names, g, stage, 4)

    loss_part, dx, grads, d_rel_bias = _local_step(
        x.reshape(t, D_MODEL), p.reshape(DEPTH * t, PLE_DIM), loss_target.reshape(t, D_MODEL), rel_bias, wts,
        matmul_weights, grads_ready)

    def local_grad(n):
        if n == "rel_bias":
            return d_rel_bias
        key = LOCAL_GRAD_KEY.get(n, n)
        return jnp.stack([grads[li][key].reshape(wt[n].shape[1:]) if n in REPLICATED else grads[li][key]
                          for li in range(DEPTH)])

    small_names = REPLICATED + SMALL_SHARDED
    small_full_shapes = [wt[n].shape if n in REPLICATED else full[n].shape for n in small_names]
    small_parts = _allgather_vmem(_pack_rows([local_grad(n) for n in small_names] + [loss_part.reshape(1)]),
                                  name="allgather_small_grads")
    small_parts = small_parts.reshape(N_DEV, -1, LANES)

    out_g, out_d, out_m, out_v = {}, {}, {}, {}
    for n in big:
        shp = wt[n].shape
        two_d = lambda a: a.reshape(-1, shp[-1])
        res = _adamw_reduce([landed[n, li] for li in range(DEPTH)], two_d(wt[n]), two_d(mom_m[n]), two_d(mom_v[n]),
                            tr=32 if n == "w_down" else 128, name="adamw_" + n)
        out_g[n], out_d[n], out_m[n], out_v[n] = [r.reshape(shp) for r in res]

    *reduced, loss = _unpack_rows(_sum_slots(small_parts, name="sum_small_grads"), small_full_shapes + [(1,)])
    loss = loss[0]
    reduced = dict(zip(small_names, reduced))
    rep_shapes = [wt[n].shape for n in REPLICATED]
    upd = _adamw_plain(_pack_rows([reduced[n] for n in REPLICATED]), _pack_rows([wt[n] for n in REPLICATED]),
                       _pack_rows([mom_m[n] for n in REPLICATED]), _pack_rows([mom_v[n] for n in REPLICATED]),
                       name="adamw_replicated")
    for dst, packed in zip((out_d, out_m, out_v), upd):
        dst.update(zip(REPLICATED, _unpack_rows(packed, rep_shapes)))
    for n in REPLICATED:
        out_g[n] = reduced[n]
    for n in SMALL_SHARDED:
        shp = wt[n].shape
        g = reduced[n].reshape(shp[0], shp[1], N_DEV, shp[2])
        g = lax.dynamic_index_in_dim(g, me, axis=2, keepdims=False)
        two_d = lambda a: a.reshape(-1, shp[-1])
        res = _adamw_plain(two_d(g), two_d(wt[n]), two_d(mom_m[n]), two_d(mom_v[n]), name="adamw_" + n)
        out_g[n] = g
        out_d[n], out_m[n], out_v[n] = [r.reshape(shp) for r in res]

    return (loss, dx.reshape(bl, SEQ, D_MODEL), *[out_g[n] for n in WEIGHT_NAMES], *[out_d[n] for n in WEIGHT_NAMES],
            *[out_m[n] for n in WEIGHT_NAMES], *[out_v[n] for n in WEIGHT_NAMES])
```

```python
import math

import jax
import jax.numpy as jnp
import numpy as np
from jax import lax
from jax.experimental import pallas as pl
from jax.experimental.pallas import tpu as pltpu
from jax.experimental.pallas import tpu_sc as plsc

F32 = jnp.float32
BF16 = jnp.bfloat16

N_DEV = 8
D_MODEL = 1024
SEQ = 2048
DEPTH = 2
HEAD_DIM = 64
IN_WIDTH = 2304
D_FF = 2816
PLE_DIM = 256
C_CHUNK = 128
C_GROUPS = 4
DILATED_CFGS = ((128, 1), (512, 4), (2048, 16))
DILATIONS = tuple(d for _, d in DILATED_CFGS)
A_RADIUS = 64
SWA_RADIUS = 128
BAND_BLOCK = 256
GRID_W = 64
ROPE_THETA = 10000.0
REL_BUCKETS = 32
REL_MAX_DIST = 1024
EPS = 1e-6
NEG_INF = -1e30
ATTN_SCALE = HEAD_DIM ** -0.5
LANES = 128

ADAM_LR = 0.001
ADAM_B1 = 0.9
ADAM_B2 = 0.999
ADAM_EPS = 1e-08
ADAM_WD = 0.01
ADAM_STEP = 10

MESH = pl.DeviceIdType.MESH
NT = (((1,), (1,)), ((), ()))
TN = (((0,), (0,)), ((), ()))
ARB = "arbitrary"
PAR = "parallel"


def _cparams(*sem):
    return pltpu.CompilerParams(dimension_semantics=tuple(sem))


def _sds(shape, dtype):
    return jax.ShapeDtypeStruct(tuple(shape), dtype)


def _group_sum_matrix(n, same_group):
    r = lax.broadcasted_iota(jnp.int32, (n, n), 0)
    c = lax.broadcasted_iota(jnp.int32, (n, n), 1)
    if same_group:
        return ((r >> 6) == (c >> 6)).astype(F32)
    return ((r & 63) == (c & 63)).astype(F32)


def _seg_sum(x, e):
    eb = e.astype(BF16)
    hi = x.astype(BF16)
    lo = (x - hi.astype(F32)).astype(BF16)
    return jnp.dot(hi, eb, preferred_element_type=F32) + jnp.dot(lo, eb, preferred_element_type=F32)


def _gelu(x):
    c = math.sqrt(2.0 / math.pi)
    return 0.5 * x * (1.0 + jnp.tanh(c * (x + 0.044715 * (x * x * x))))


def _gelu_grad(x):
    c = math.sqrt(2.0 / math.pi)
    t = jnp.tanh(c * (x + 0.044715 * (x * x * x)))
    return 0.5 * (1.0 + t) + 0.5 * x * (1.0 - t * t) * c * (1.0 + 3.0 * 0.044715 * (x * x))


def _sigmoid(x):
    return 1.0 / (1.0 + jnp.exp(-x))


def _scatter_cols(scratch, first, val):
    for c in range(val.shape[1] // LANES):
        scratch[first + c] = val[:, c * LANES:(c + 1) * LANES]


def _gather_cols(scratch, first, ncol):
    return jnp.concatenate([scratch[first + c] for c in range(ncol)], axis=1)


def _read_residue(scratch, first, ncol, r, d):
    n = scratch.shape[1] // d
    return jnp.concatenate([scratch.at[first + c][pl.ds(r, n, stride=d), :] for c in range(ncol)], axis=1)


def _write_residue(scratch, first, r, d, val):
    n = scratch.shape[1] // d
    for c in range(val.shape[1] // LANES):
        scratch.at[first + c][pl.ds(r, n, stride=d), :] = val[:, c * LANES:(c + 1) * LANES]


def _norm_mm(xs, gain, w, res, *, tm, tn, name, out_dtype=F32):
    t = xs[0].shape[0]
    k = sum(x.shape[1] for x in xs)
    n = w.shape[1]
    ng = len(xs)
    has_res = res is not None

    def body(*refs):
        x_refs = refs[:ng]
        g_ref, w_ref = refs[ng], refs[ng + 1]
        res_ref = refs[ng + 2] if has_res else None
        hn_ref, o_ref, hn_s = refs[ng + 2 + has_res:]

        @pl.when(pl.program_id(1) == 0)
        def _():
            off = 0
            for xr in x_refs:
                x = xr[...].astype(F32)
                wd = x.shape[1]
                r = lax.rsqrt(jnp.mean(x * x, axis=-1, keepdims=True) + EPS)
                hn_s[:, off:off + wd] = (x * r * g_ref[:, off:off + wd]).astype(BF16)
                off += wd
            hn_ref[...] = hn_s[...]

        acc = jnp.dot(hn_s[...], w_ref[...], preferred_element_type=F32)
        if has_res:
            acc = acc + res_ref[...]
        o_ref[...] = acc.astype(out_dtype)

    in_specs = [pl.BlockSpec((tm, x.shape[1]), lambda i, j: (i, 0)) for x in xs]
    in_specs += [pl.BlockSpec((1, k), lambda i, j: (0, 0)), pl.BlockSpec((k, tn), lambda i, j: (0, j))]
    args = list(xs) + [gain, w]
    if has_res:
        in_specs.append(pl.BlockSpec((tm, tn), lambda i, j: (i, j)))
        args.append(res)
    return pl.pallas_call(
        body, name=name, grid=(t // tm, n // tn), in_specs=in_specs,
        out_specs=[pl.BlockSpec((tm, k), lambda i, j: (i, 0)), pl.BlockSpec((tm, tn), lambda i, j: (i, j))],
        out_shape=[_sds((t, k), BF16), _sds((t, n), out_dtype)],
        scratch_shapes=[pltpu.VMEM((tm, k), BF16)],
        compiler_params=_cparams(PAR, ARB),
    )(*args)


def _mm(a, b, mode, res, *, tm, tn, out_dtype, name, a_rows=None):
    if mode == "tn":
        kk, m = a.shape
        blk_a = 0
        if a_rows is not None:
            blk_a, kk = a_rows
        a_spec = pl.BlockSpec((kk, tm), lambda i, j: (blk_a, i))
    else:
        m, kk = a.shape
        a_spec = pl.BlockSpec((tm, kk), lambda i, j: (i, 0))
    if mode == "nt":
        n = b.shape[0]
        b_spec = pl.BlockSpec((tn, kk), lambda i, j: (j, 0))
    else:
        n = b.shape[1]
        b_spec = pl.BlockSpec((kk, tn), lambda i, j: (0, j))
    has_res = res is not None

    def body(*refs):
        a_ref, b_ref = refs[0], refs[1]
        o_ref = refs[-1]
        av = a_ref[...].astype(BF16)
        bv = b_ref[...].astype(BF16)
        if mode == "nn":
            acc = jnp.dot(av, bv, preferred_element_type=F32)
        elif mode == "nt":
            acc = lax.dot_general(av, bv, NT, preferred_element_type=F32)
        else:
            acc = lax.dot_general(av, bv, TN, preferred_element_type=F32)
        if has_res:
            acc = acc + refs[2][...]
        o_ref[...] = acc.astype(out_dtype)

    in_specs = [a_spec, b_spec]
    args = [a, b]
    if has_res:
        in_specs.append(pl.BlockSpec((tm, tn), lambda i, j: (i, j)))
        args.append(res)
    return pl.pallas_call(
        body, name=name, grid=(m // tm, n // tn), in_specs=in_specs,
        out_specs=pl.BlockSpec((tm, tn), lambda i, j: (i, j)),
        out_shape=_sds((m, n), out_dtype),
        compiler_params=_cparams(PAR, PAR),
    )(*args)


def _mm_bt_normbwd(dys, w, xs, gain, dres, *, tm, tn, name, emit_bf16=False):
    t, wd_each = dys[0].shape
    nd = len(dys)
    per = wd_each // tn
    nj = nd * per
    k = w.shape[0]
    ng = len(xs)
    has_res = dres is not None

    def body(*refs):
        dy_refs = refs[:nd]
        w_ref = refs[nd]
        x_refs = refs[nd + 1:nd + 1 + ng]
        g_ref = refs[nd + 1 + ng]
        dres_ref = refs[nd + 2 + ng] if has_res else None
        outs = refs[nd + 2 + ng + has_res:]
        dx_ref = outs[0]
        dxb_ref = outs[1] if emit_bf16 else None
        dg_ref, acc = outs[1 + emit_bf16:]
        i, j = pl.program_id(0), pl.program_id(1)

        @pl.when(j == 0)
        def _():
            acc[...] = jnp.zeros_like(acc)

        for d, dy_ref in enumerate(dy_refs):
            @pl.when((j >= d * per) & (j < (d + 1) * per))
            def _(dy_ref=dy_ref):
                acc[...] += lax.dot_general(dy_ref[...].astype(BF16), w_ref[...], NT, preferred_element_type=F32)

        @pl.when(j == nj - 1)
        def _():
            @pl.when(i == 0)
            def _():
                dg_ref[...] = jnp.zeros_like(dg_ref)

            off = 0
            for xr in x_refs:
                x = xr[...].astype(F32)
                wd = x.shape[1]
                g = g_ref[:, off:off + wd]
                dyn = acc[:, off:off + wd]
                r = lax.rsqrt(jnp.mean(x * x, axis=-1, keepdims=True) + EPS)
                gdy = dyn * g
                dx = r * gdy - x * (r * r * r * jnp.mean(gdy * x, axis=-1, keepdims=True))
                if has_res:
                    dx = dx + dres_ref[:, off:off + wd]
                dx_ref[:, off:off + wd] = dx
                if emit_bf16:
                    dxb_ref[:, off:off + wd] = dx.astype(BF16)
                dg_ref[:, off:off + wd] += jnp.sum(dyn * x * r, axis=0, keepdims=True)
                off += wd

    def dy_map(d):
        return lambda i, j: (i, jnp.clip(j - d * per, 0, per - 1))

    in_specs = [pl.BlockSpec((tm, tn), dy_map(d)) for d in range(nd)]
    in_specs.append(pl.BlockSpec((k, tn), lambda i, j: (0, j)))
    in_specs += [pl.BlockSpec((tm, x.shape[1]), lambda i, j: (i, 0)) for x in xs]
    in_specs.append(pl.BlockSpec((1, k), lambda i, j: (0, 0)))
    args = list(dys) + [w] + list(xs) + [gain]
    if has_res:
        in_specs.append(pl.BlockSpec((tm, k), lambda i, j: (i, 0)))
        args.append(dres)
    row = pl.BlockSpec((tm, k), lambda i, j: (i, 0))
    out_specs = [row] + ([row] if emit_bf16 else []) + [pl.BlockSpec((1, k), lambda i, j: (0, 0))]
    out_shape = [_sds((t, k), F32)] + ([_sds((t, k), BF16)] if emit_bf16 else []) + [_sds((1, k), F32)]
    return pl.pallas_call(
        body, name=name, grid=(t // tm, nj), in_specs=in_specs, out_specs=out_specs, out_shape=out_shape,
        scratch_shapes=[pltpu.VMEM((tm, k), F32)],
        compiler_params=_cparams(ARB, ARB),
    )(*args)


def _rope_partner(y):
    n = y.shape[1]
    lane = lax.broadcasted_iota(jnp.int32, y.shape, 1)
    return jnp.where((lane & 31) < 16, pltpu.roll(y, n - 16, 1), pltpu.roll(y, 16, 1))


def _residue_specs(tm, width, nt):
    specs = [pl.BlockSpec((tm, width), lambda b, i: (b * nt + i, 0))]
    for d in DILATIONS[1:]:
        specs.append(pl.BlockSpec((None, d, tm // d, width), lambda b, i: (b, 0, i, 0)))
    return specs


def _residue_shapes(bl, width, dtype):
    return [_sds((bl * SEQ, width), dtype)] + [_sds((bl, d, SEQ // d, width), dtype) for d in DILATIONS[1:]]


def _qkprep_fwd(proj, gains, cos, sins, *, tm, name):
    t = proj.shape[0]
    bl = t // SEQ
    nt = SEQ // tm

    def body(p_ref, g_ref, c_ref, s_ref, qa1_ref, qa4_ref, qa16_ref, qb_ref, qd_ref, scr):
        e = _group_sum_matrix(256, True)

        def hn(x, row):
            x = x.astype(F32)
            wd = x.shape[1]
            ms = _seg_sum(x * x, e[:wd, :wd]) * (1.0 / HEAD_DIM)
            return x * lax.rsqrt(ms + EPS) * g_ref[row:row + 1, :wd]

        qa = jnp.concatenate([hn(p_ref[:, 0:256], 0) * ATTN_SCALE, hn(p_ref[:, 256:512], 1),
                              p_ref[:, 512:768].astype(F32)], axis=1)
        qa1_ref[...] = qa.astype(BF16)
        _scatter_cols(scr, 0, qa)
        for d, ref in ((4, qa4_ref), (16, qa16_ref)):
            for r in range(d):
                ref[r] = _read_residue(scr, 0, 6, r, d).astype(BF16)
        qb_ref[:, 0:256] = (hn(p_ref[:, 768:1024], 2) * ATTN_SCALE).astype(BF16)
        qb_ref[:, 256:384] = hn(p_ref[:, 1024:1152], 3).astype(BF16)
        qb_ref[:, 384:512] = p_ref[:, 1152:1280].astype(BF16)
        yq = hn(p_ref[:, 1792:2048], 4)
        yq = yq * c_ref[...] + _rope_partner(yq) * s_ref[...]
        qd_ref[:, 0:256] = (yq * ATTN_SCALE).astype(BF16)
        yk = hn(p_ref[:, 2048:2176], 5)
        yk = yk * c_ref[:, 0:128] + _rope_partner(yk) * s_ref[:, 0:128]
        qd_ref[:, 256:384] = yk.astype(BF16)
        qd_ref[:, 384:512] = p_ref[:, 2176:2304].astype(BF16)

    row = lambda width: pl.BlockSpec((tm, width), lambda b, i: (b * nt + i, 0))
    tab = pl.BlockSpec((tm, 256), lambda b, i: (i, 0))
    return pl.pallas_call(
        body, name=name, grid=(bl, nt),
        in_specs=[row(IN_WIDTH), pl.BlockSpec((8, 256), lambda b, i: (0, 0)), tab, tab],
        out_specs=_residue_specs(tm, 768, nt) + [row(512), row(512)],
        out_shape=_residue_shapes(bl, 768, BF16) + [_sds((t, 512), BF16), _sds((t, 512), BF16)],
        scratch_shapes=[pltpu.VMEM((6, tm, LANES), F32)],
        compiler_params=_cparams(PAR, PAR),
    )(proj, gains, cos, sins)


def _qkprep_bwd(proj, da, db, dd, dcu, dcv, gains, cos, sins, *, tm, name):
    t = proj.shape[0]
    bl = t // SEQ
    nt = SEQ // tm
    flat = [a for cfg in da for a in cfg] + list(db) + list(dd) + [dcu, dcv]

    def body(*refs):
        p_ref, g_ref, c_ref, s_ref = refs[:4]
        d_refs = refs[4:4 + len(flat)]
        dp_ref, dg_ref, scr = refs[4 + len(flat):]
        a_refs = d_refs[:9]
        dqb_ref, dkb_ref, dvb_ref, dqd_ref, dkd_ref, dvd_ref, dcu_ref, dcv_ref = d_refs[9:]
        e = _group_sum_matrix(256, True)
        first = (pl.program_id(0) == 0) & (pl.program_id(1) == 0)
        last = (pl.program_id(0) == bl - 1) & (pl.program_id(1) == nt - 1)

        @pl.when(first)
        def _():
            dg_ref[...] = jnp.zeros_like(dg_ref)

        def hn_bwd(x, dy, row):
            x, dy = x.astype(F32), dy.astype(F32)
            wd = x.shape[1]
            ee = e[:wd, :wd]
            g = g_ref[row:row + 1, :wd]
            r = lax.rsqrt(_seg_sum(x * x, ee) * (1.0 / HEAD_DIM) + EPS)
            gdy = dy * g
            dx = r * gdy - x * (r * r * r * (_seg_sum(gdy * x, ee) * (1.0 / HEAD_DIM)))
            dg_ref[row:row + 1, :wd] += jnp.sum(dy * x * r, axis=0, keepdims=True)
            return dx

        def rope_bwd(dy, wd):
            dy = dy.astype(F32)
            return dy * c_ref[:, :wd] + _rope_partner(dy * s_ref[:, :wd])

        dqkv = jnp.concatenate([a_refs[m][...].astype(F32) for m in range(3)], axis=1)
        for ci, d in ((1, 4), (2, 16)):
            for r in range(d):
                part = jnp.concatenate([a_refs[3 * ci + m][r].astype(F32) for m in range(3)], axis=1)
                _write_residue(scr, 0, r, d, part)
            dqkv = dqkv + _gather_cols(scr, 0, 6)
        dp_ref[:, 0:256] = hn_bwd(p_ref[:, 0:256], dqkv[:, 0:256] * ATTN_SCALE, 0).astype(BF16)
        dp_ref[:, 256:512] = hn_bwd(p_ref[:, 256:512], dqkv[:, 256:512], 1).astype(BF16)
        dp_ref[:, 512:768] = dqkv[:, 512:768].astype(BF16)
        dp_ref[:, 768:1024] = hn_bwd(p_ref[:, 768:1024], dqb_ref[...] * ATTN_SCALE, 2).astype(BF16)
        dp_ref[:, 1024:1152] = hn_bwd(p_ref[:, 1024:1152], dkb_ref[...], 3).astype(BF16)
        dp_ref[:, 1152:1280] = dvb_ref[...].astype(BF16)
        dp_ref[:, 1280:1536] = dcu_ref[...].astype(BF16)
        dp_ref[:, 1536:1792] = dcv_ref[...].astype(BF16)
        dp_ref[:, 1792:2048] = hn_bwd(p_ref[:, 1792:2048], rope_bwd(dqd_ref[...] * ATTN_SCALE, 256), 4).astype(BF16)
        dp_ref[:, 2048:2176] = hn_bwd(p_ref[:, 2048:2176], rope_bwd(dkd_ref[...], 128), 5).astype(BF16)
        dp_ref[:, 2176:2304] = dvd_ref[...].astype(BF16)

        @pl.when(last)
        def _():
            dg_ref[...] = _seg_sum(dg_ref[...], _group_sum_matrix(256, False))

    row = lambda width: pl.BlockSpec((tm, width), lambda b, i: (b * nt + i, 0))
    tab = pl.BlockSpec((tm, 256), lambda b, i: (i, 0))
    in_specs = [row(IN_WIDTH), pl.BlockSpec((8, 256), lambda b, i: (0, 0)), tab, tab]
    res_specs = _residue_specs(tm, 256, nt)
    in_specs += [res_specs[ci] for ci in range(3) for _ in range(3)]
    in_specs += [row(a.shape[1]) for a in flat[9:]]
    return pl.pallas_call(
        body, name=name, grid=(bl, nt), in_specs=in_specs,
        out_specs=[row(IN_WIDTH), pl.BlockSpec((8, 256), lambda b, i: (0, 0))],
        out_shape=[_sds((t, IN_WIDTH), BF16), _sds((8, 256), F32)],
        scratch_shapes=[pltpu.VMEM((6, tm, LANES), F32)],
        compiler_params=_cparams(ARB, ARB),
    )(proj, gains, cos, sins, *flat)


BAND_ROWS_PER_STEP = 512


def _residues_per_step(dil, seq_len):
    return min(dil, max(1, BAND_ROWS_PER_STEP // seq_len))


def _band_spec(seq_len, spec, rb):
    width, idx = spec
    return pl.BlockSpec((None, rb, seq_len, width), lambda b, r: (b, r, 0, idx))


def _fill_padded(dst, src_ref, rad, seq_len):
    z = jnp.zeros((rad, dst.shape[1]), dst.dtype)
    dst[0:rad, :] = z
    dst[rad + seq_len:rad + seq_len + rad, :] = z
    dst[rad:rad + seq_len, :] = src_ref[...]


def _band_fwd(src, qs, ks, vs, bias, sink, *, rad, nh, nkv, name):
    bl, dil, sl, _ = src.shape
    blk = bias.shape[1]
    kw = blk + 2 * rad
    nb = sl // blk
    rep = nh // nkv
    has_sink = sink is not None
    rb = _residues_per_step(dil, sl)

    def body(*refs):
        q_all, k_all, v_all, b_ref = refs[:4]
        s_ref = refs[4] if has_sink else None
        o_all, l_all, kp, vp = refs[4 + has_sink:]
        for ri in range(rb):
            one_sequence(q_all.at[ri], k_all.at[ri], v_all.at[ri], b_ref, s_ref, o_all.at[ri], l_all.at[ri], kp, vp)

    def one_sequence(q_ref, k_ref, v_ref, b_ref, s_ref, o_ref, l_ref, kp, vp):
        _fill_padded(kp, k_ref, rad, sl)
        _fill_padded(vp, v_ref, rad, sl)

        def blk_body(i, carry):
            r0 = pl.multiple_of(i * blk, blk)
            qb = q_ref[pl.ds(r0, blk), :]
            kwin = kp[pl.ds(r0, kw), :]
            vwin = vp[pl.ds(r0, kw), :]
            col = r0 - rad + lax.broadcasted_iota(jnp.int32, (blk, kw), 1)
            neg = jnp.where((col >= 0) & (col < sl), 0.0, NEG_INF).astype(F32)
            for h in range(nh):
                g = h // rep
                hs = slice(h * HEAD_DIM, (h + 1) * HEAD_DIM)
                gs = slice(g * HEAD_DIM, (g + 1) * HEAD_DIM)
                s = lax.dot_general(qb[:, hs], kwin[:, gs], NT, preferred_element_type=F32)
                s = s + b_ref[h] + neg
                m = jnp.max(s, axis=1, keepdims=True)
                if has_sink:
                    sk = s_ref[h][0:1, 0:1]
                    m = jnp.maximum(m, sk)
                p = jnp.exp(s - m)
                den = jnp.sum(p, axis=1, keepdims=True)
                if has_sink:
                    den = den + jnp.exp(sk - m)
                o = jnp.dot(p.astype(BF16), vwin[:, gs], preferred_element_type=F32) / den
                o_ref[pl.ds(r0, blk), hs] = o.astype(BF16)
                l_ref[pl.ds(r0, blk), hs] = jnp.broadcast_to(m + jnp.log(den), (blk, HEAD_DIM))
            return carry

        lax.fori_loop(0, nb, blk_body, 0)

    in_specs = [_band_spec(sl, qs, rb), _band_spec(sl, ks, rb), _band_spec(sl, vs, rb),
                pl.BlockSpec((nh, blk, kw), lambda b, r: (0, 0, 0))]
    args = [src] * 3 + [bias]
    if has_sink:
        in_specs.append(pl.BlockSpec((nh, 8, 128), lambda b, r: (0, 0, 0)))
        args.append(sink)
    return pl.pallas_call(
        body, name=name, grid=(bl, dil // rb), in_specs=in_specs,
        out_specs=[_band_spec(sl, (256, 0), rb)] * 2,
        out_shape=[_sds((bl, dil, sl, 256), BF16), _sds((bl, dil, sl, 256), F32)],
        scratch_shapes=[pltpu.VMEM((sl + 2 * rad, ks[0]), BF16), pltpu.VMEM((sl + 2 * rad, vs[0]), BF16)],
        compiler_params=_cparams(PAR, PAR),
    )(*args)


def _band_bwd(src, qs, ks, vs, bias, sink, dy, dcol, lse, delta, *, rad, nh, nkv, name):
    bl, dil, sl, _ = src.shape
    blk = bias.shape[1]
    kw = blk + 2 * rad
    nb = sl // blk
    rep = nh // nkv
    has_sink = sink is not None
    rb = _residues_per_step(dil, sl)
    wk, wv = ks[0], vs[0]

    def body(*refs):
        q_all, k_all, v_all, b_ref = refs[:4]
        s_ref = refs[4] if has_sink else None
        do_all, l_all, dl_all = refs[4 + has_sink:7 + has_sink]
        outs = refs[7 + has_sink:]
        dsk_ref = None
        if has_sink:
            dq_all, dk_all, dv_all, db_ref, dsk_ref, kp, vp, dka, dva = outs
        else:
            dq_all, dk_all, dv_all, db_ref, kp, vp, dka, dva = outs

        @pl.when((pl.program_id(0) == 0) & (pl.program_id(1) == 0))
        def _():
            db_ref[...] = jnp.zeros_like(db_ref)
            if has_sink:
                dsk_ref[...] = jnp.zeros_like(dsk_ref)

        for ri in range(rb):
            one_sequence(q_all.at[ri], k_all.at[ri], v_all.at[ri], b_ref, s_ref, do_all.at[ri], l_all.at[ri],
                         dl_all.at[ri], dq_all.at[ri], dk_all.at[ri], dv_all.at[ri], db_ref, dsk_ref, kp, vp, dka, dva)

    def one_sequence(q_ref, k_ref, v_ref, b_ref, s_ref, do_ref, l_ref, dl_ref, dq_ref, dk_ref, dv_ref, db_ref, dsk_ref,
                     kp, vp, dka, dva):
        _fill_padded(kp, k_ref, rad, sl)
        _fill_padded(vp, v_ref, rad, sl)
        dka[...] = jnp.zeros_like(dka)
        dva[...] = jnp.zeros_like(dva)

        def blk_body(i, carry):
            r0 = pl.multiple_of(i * blk, blk)
            qb = q_ref[pl.ds(r0, blk), :]
            kwin = kp[pl.ds(r0, kw), :]
            vwin = vp[pl.ds(r0, kw), :]
            dob = do_ref[pl.ds(r0, blk), :].astype(BF16)
            lb = l_ref[pl.ds(r0, blk), :]
            dlb = dl_ref[pl.ds(r0, blk), :]
            col = r0 - rad + lax.broadcasted_iota(jnp.int32, (blk, kw), 1)
            neg = jnp.where((col >= 0) & (col < sl), 0.0, NEG_INF).astype(F32)
            for h in range(nh):
                g = h // rep
                hs = slice(h * HEAD_DIM, (h + 1) * HEAD_DIM)
                gs = slice(g * HEAD_DIM, (g + 1) * HEAD_DIM)
                qh, kh, vh, doh = qb[:, hs], kwin[:, gs], vwin[:, gs], dob[:, hs]
                lh = lb[:, h * HEAD_DIM:h * HEAD_DIM + 1]
                dlh = dlb[:, h * HEAD_DIM:h * HEAD_DIM + 1]
                s = lax.dot_general(qh, kh, NT, preferred_element_type=F32) + b_ref[h] + neg
                p = jnp.exp(s - lh)
                dp = lax.dot_general(doh, vh, NT, preferred_element_type=F32)
                ds = p * (dp - dlh)
                dsb = ds.astype(BF16)
                dq_ref[pl.ds(r0, blk), hs] = jnp.dot(dsb, kh, preferred_element_type=F32).astype(BF16)
                dka[pl.ds(r0, kw), gs] += lax.dot_general(dsb, qh, TN, preferred_element_type=F32)
                dva[pl.ds(r0, kw), gs] += lax.dot_general(p.astype(BF16), doh, TN, preferred_element_type=F32)
                db_ref[h] += ds
                if has_sink:
                    ps = jnp.exp(s_ref[h][0:1, 0:1] - lh)
                    dsk_ref[h] += jnp.broadcast_to(-jnp.sum(ps * dlh, axis=0, keepdims=True), (8, 128))
            return carry

        lax.fori_loop(0, nb, blk_body, 0)
        dk_ref[...] = dka[rad:rad + sl, :].astype(BF16)
        dv_ref[...] = dva[rad:rad + sl, :].astype(BF16)

    const3 = lambda b, r: (0, 0, 0)
    in_specs = [_band_spec(sl, qs, rb), _band_spec(sl, ks, rb), _band_spec(sl, vs, rb),
                pl.BlockSpec((nh, blk, kw), const3)]
    args = [src] * 3 + [bias]
    if has_sink:
        in_specs.append(pl.BlockSpec((nh, 8, 128), const3))
        args.append(sink)
    row = _band_spec(sl, (256, 0), rb)
    in_specs += [_band_spec(sl, (256, dcol), rb), row, row]
    args += [dy, lse, delta]
    out_specs = [row, _band_spec(sl, (wk, 0), rb), _band_spec(sl, (wv, 0), rb), pl.BlockSpec((nh, blk, kw), const3)]
    out_shape = [_sds((bl, dil, sl, 256), BF16), _sds((bl, dil, sl, wk), BF16), _sds((bl, dil, sl, wv), BF16),
                 _sds((nh, blk, kw), F32)]
    if has_sink:
        out_specs.append(pl.BlockSpec((nh, 8, 128), const3))
        out_shape.append(_sds((nh, 8, 128), F32))
    return pl.pallas_call(
        body, name=name, grid=(bl, dil // rb), in_specs=in_specs, out_specs=out_specs, out_shape=out_shape,
        scratch_shapes=[pltpu.VMEM((sl + 2 * rad, wk), BF16), pltpu.VMEM((sl + 2 * rad, wv), BF16),
                        pltpu.VMEM((sl + 2 * rad, wk), F32), pltpu.VMEM((sl + 2 * rad, wv), F32)],
        compiler_params=_cparams(ARB, ARB),
    )(*args)


def _combine_a(os_, ls_, *, tm, name):
    bl = os_[1].shape[0]
    t = bl * SEQ
    nt = SEQ // tm

    def body(o1, o4, o16, l1, l4, l16, y_ref, lt_ref, scr):
        for k, (d, ref) in enumerate(((4, o4), (16, o16), (4, l4), (16, l16))):
            for r in range(d):
                _write_residue(scr, 2 * k, r, d, ref[r].astype(F32))
        o2, o3, b, c = (_gather_cols(scr, 2 * k, 2) for k in range(4))
        a = l1[...]
        m = jnp.maximum(jnp.maximum(a, b), c)
        ea, eb, ec = jnp.exp(a - m), jnp.exp(b - m), jnp.exp(c - m)
        den = ea + eb + ec
        y_ref[...] = ((ea / den) * o1[...].astype(F32) + (eb / den) * o2 + (ec / den) * o3).astype(BF16)
        lt_ref[...] = m + jnp.log(den)

    specs = _residue_specs(tm, 256, nt)
    return pl.pallas_call(
        body, name=name, grid=(bl, nt), in_specs=specs * 2, out_specs=[specs[0]] * 2,
        out_shape=[_sds((t, 256), BF16), _sds((t, 256), F32)], scratch_shapes=[pltpu.VMEM((8, tm, LANES), F32)],
        compiler_params=_cparams(PAR, PAR),
    )(*os_, *ls_)


def _deltas(dycat, ya, yb, yd, lse_a, *, tm, name):
    t = ya.shape[0]
    bl = t // SEQ
    nt = SEQ // tm

    def body(dy_ref, ya_ref, yb_ref, yd_ref, la_ref, dy4, dy16, l4, l16, da1, da4, da16, db_ref, dd_ref, scr):
        e = _group_sum_matrix(256, True)
        dya = dy_ref[:, 0:256]
        dla = _seg_sum(dya * ya_ref[...].astype(F32), e)
        da1[...] = dla
        db_ref[...] = _seg_sum(dy_ref[:, 256:512] * yb_ref[...].astype(F32), e)
        dd_ref[...] = _seg_sum(dy_ref[:, 768:1024] * yd_ref[...].astype(F32), e)
        for k, (val, r4, r16) in enumerate(((dya, dy4, dy16), (la_ref[...], l4, l16), (dla, da4, da16))):
            _scatter_cols(scr, 2 * k, val)
            for d, ref in ((4, r4), (16, r16)):
                for r in range(d):
                    ref[r] = _read_residue(scr, 2 * k, 2, r, d)

    specs = _residue_specs(tm, 256, nt)
    nat = specs[0]
    shapes = _residue_shapes(bl, 256, F32)
    outs = pl.pallas_call(
        body, name=name, grid=(bl, nt),
        in_specs=[pl.BlockSpec((tm, 1024), lambda b, i: (b * nt + i, 0)), nat, nat, nat, nat],
        out_specs=specs[1:] + specs[1:] + specs + [nat, nat],
        out_shape=shapes[1:] + shapes[1:] + shapes + [shapes[0], shapes[0]],
        scratch_shapes=[pltpu.VMEM((6, tm, LANES), F32)],
        compiler_params=_cparams(PAR, PAR),
    )(dycat, ya, yb, yd, lse_a)
    return outs[0:2], outs[2:4], outs[4:7], outs[7], outs[8]


def _dense_fwd(qd, *, tq, name):
    t = qd.shape[0]
    bl = t // SEQ
    nq = SEQ // tq

    def body(q_ref, k_ref, v_ref, o_ref, l_ref):
        q = q_ref[...]
        for g in range(2):
            h0, h1 = 2 * g, 2 * g + 1
            q2 = jnp.concatenate([q[:, h0 * 64:(h0 + 1) * 64], q[:, h1 * 64:(h1 + 1) * 64]], axis=0)
            kg = k_ref[:, g * 64:(g + 1) * 64]
            vg = v_ref[:, g * 64:(g + 1) * 64]
            s = lax.dot_general(q2, kg, NT, preferred_element_type=F32)
            m = jnp.max(s, axis=1, keepdims=True)
            p = jnp.exp(s - m)
            den = jnp.sum(p, axis=1, keepdims=True)
            o2 = jnp.dot(p.astype(BF16), vg, preferred_element_type=F32) / den
            l2 = jnp.broadcast_to(m + jnp.log(den), (2 * tq, 64))
            o_ref[:, h0 * 64:(h0 + 1) * 64] = o2[:tq].astype(BF16)
            o_ref[:, h1 * 64:(h1 + 1) * 64] = o2[tq:].astype(BF16)
            l_ref[:, h0 * 64:(h0 + 1) * 64] = l2[:tq]
            l_ref[:, h1 * 64:(h1 + 1) * 64] = l2[tq:]

    q3 = qd.reshape(bl, SEQ, 512)
    o, lse = pl.pallas_call(
        body, name=name, grid=(bl, nq),
        in_specs=[pl.BlockSpec((None, tq, 256), lambda b, i: (b, i, 0)),
                  pl.BlockSpec((None, SEQ, 128), lambda b, i: (b, 0, 2)),
                  pl.BlockSpec((None, SEQ, 128), lambda b, i: (b, 0, 3))],
        out_specs=[pl.BlockSpec((None, tq, 256), lambda b, i: (b, i, 0))] * 2,
        out_shape=[_sds((bl, SEQ, 256), BF16), _sds((bl, SEQ, 256), F32)],
        compiler_params=_cparams(PAR, PAR),
    )(q3, q3, q3)
    return o.reshape(t, 256), lse.reshape(t, 256)


def _dense_bwd(qd, dycat, lse, delta, *, tq, name):
    t = qd.shape[0]
    bl = t // SEQ
    nq = SEQ // tq

    def body(q_ref, k_ref, v_ref, do_ref, l_ref, dl_ref, dq_ref, dk_ref, dv_ref, dkt, dvt):
        @pl.when(pl.program_id(1) == 0)
        def _():
            dkt[...] = jnp.zeros_like(dkt)
            dvt[...] = jnp.zeros_like(dvt)

        q = q_ref[...]
        do = do_ref[...].astype(BF16)
        lv = l_ref[...]
        dlv = dl_ref[...]
        for g in range(2):
            h0, h1 = 2 * g, 2 * g + 1
            q2 = jnp.concatenate([q[:, h0 * 64:(h0 + 1) * 64], q[:, h1 * 64:(h1 + 1) * 64]], axis=0)
            do2 = jnp.concatenate([do[:, h0 * 64:(h0 + 1) * 64], do[:, h1 * 64:(h1 + 1) * 64]], axis=0)
            l2 = jnp.concatenate([lv[:, h0 * 64:h0 * 64 + 1], lv[:, h1 * 64:h1 * 64 + 1]], axis=0)
            dl2 = jnp.concatenate([dlv[:, h0 * 64:h0 * 64 + 1], dlv[:, h1 * 64:h1 * 64 + 1]], axis=0)
            kg = k_ref[:, g * 64:(g + 1) * 64]
            vg = v_ref[:, g * 64:(g + 1) * 64]
            s = lax.dot_general(q2, kg, NT, preferred_element_type=F32)
            p = jnp.exp(s - l2)
            dp = lax.dot_general(do2, vg, NT, preferred_element_type=F32)
            ds = (p * (dp - dl2)).astype(BF16)
            dq2 = jnp.dot(ds, kg, preferred_element_type=F32)
            dq_ref[:, h0 * 64:(h0 + 1) * 64] = dq2[:tq].astype(BF16)
            dq_ref[:, h1 * 64:(h1 + 1) * 64] = dq2[tq:].astype(BF16)
            dkt[g * 64:(g + 1) * 64, :] += lax.dot_general(q2, ds, TN, preferred_element_type=F32)
            dvt[g * 64:(g + 1) * 64, :] += lax.dot_general(do2, p.astype(BF16), TN, preferred_element_type=F32)

        @pl.when(pl.program_id(1) == nq - 1)
        def _():
            dk_ref[...] = dkt[...].T.astype(BF16)
            dv_ref[...] = dvt[...].T.astype(BF16)

    q3 = qd.reshape(bl, SEQ, 512)
    tile = pl.BlockSpec((None, tq, 256), lambda b, i: (b, i, 0))
    full = pl.BlockSpec((None, SEQ, 128), lambda b, i: (b, 0, 0))
    dq, dk, dv = pl.pallas_call(
        body, name=name, grid=(bl, nq),
        in_specs=[tile, pl.BlockSpec((None, SEQ, 128), lambda b, i: (b, 0, 2)),
                  pl.BlockSpec((None, SEQ, 128), lambda b, i: (b, 0, 3)),
                  pl.BlockSpec((None, tq, 256), lambda b, i: (b, i, 3)), tile, tile],
        out_specs=[tile, full, full],
        out_shape=[_sds((bl, SEQ, 256), BF16), _sds((bl, SEQ, 128), BF16), _sds((bl, SEQ, 128), BF16)],
        scratch_shapes=[pltpu.VMEM((128, SEQ), F32), pltpu.VMEM((128, SEQ), F32)],
        compiler_params=_cparams(PAR, ARB),
    )(q3, q3, q3, dycat.reshape(bl, SEQ, 1024), lse.reshape(bl, SEQ, 256), delta.reshape(bl, SEQ, 256))
    return dq.reshape(t, 256), dk.reshape(t, 128), dv.reshape(t, 128)


def _c_norm(cv, gam, bet):
    vg = _gelu(cv)
    mu = jnp.mean(vg, axis=-1, keepdims=True)
    xc = vg - mu
    r = lax.rsqrt(jnp.mean(xc * xc, axis=-1, keepdims=True) + EPS)
    xhat = xc * r
    return xhat * gam + bet, xhat, r


def _c_fwd(proj, gam, bet, ws, bst, *, tm, name):
    t = proj.shape[0]
    nch = tm // C_CHUNK

    def body(u_ref, v_ref, g_ref, b_ref, ws_ref, bs_ref, y_ref):
        vn, _, _ = _c_norm(v_ref[...].astype(F32), g_ref[...], b_ref[...])
        vnb = vn.astype(BF16)
        for c in range(nch):
            rows = slice(c * C_CHUNK, (c + 1) * C_CHUNK)
            for g in range(C_GROUPS):
                gs = slice(g * 64, (g + 1) * 64)
                mixed = jnp.dot(ws_ref[g], vnb[rows, gs], preferred_element_type=F32) + bs_ref[:, gs]
                y_ref[rows, gs] = (_gelu(u_ref[rows, gs].astype(F32)) * mixed).astype(BF16)

    vec = pl.BlockSpec((1, 256), lambda i: (0, 0))
    return pl.pallas_call(
        body, name=name, grid=(t // tm,),
        in_specs=[pl.BlockSpec((tm, 256), lambda i: (i, 5)), pl.BlockSpec((tm, 256), lambda i: (i, 6)), vec, vec,
                  pl.BlockSpec((C_GROUPS, C_CHUNK, C_CHUNK), lambda i: (0, 0, 0)),
                  pl.BlockSpec((C_CHUNK, 256), lambda i: (0, 0))],
        out_specs=pl.BlockSpec((tm, 256), lambda i: (i, 0)), out_shape=_sds((t, 256), BF16),
        compiler_params=_cparams(PAR),
    )(proj, proj, gam, bet, ws, bst)


def _c_bwd(proj, dycat, gam, bet, ws, wst, bst, *, tm, name):
    t = proj.shape[0]
    nch = tm // C_CHUNK
    nstep = t // tm

    def body(u_ref, v_ref, dy_ref, g_ref, b_ref, ws_ref, wst_ref, bs_ref,
             du_ref, dv_ref, dws_ref, dbs_ref, dg_ref, db_ref, dvn_s):
        step = pl.program_id(0)

        @pl.when(step == 0)
        def _():
            dws_ref[...] = jnp.zeros_like(dws_ref)
            dbs_ref[...] = jnp.zeros_like(dbs_ref)
            dg_ref[...] = jnp.zeros_like(dg_ref)
            db_ref[...] = jnp.zeros_like(db_ref)

        cv = v_ref[...].astype(F32)
        gam_v = g_ref[...]
        vn, xhat, r = _c_norm(cv, gam_v, b_ref[...])
        vnb = vn.astype(BF16)
        for c in range(nch):
            rows = slice(c * C_CHUNK, (c + 1) * C_CHUNK)
            for g in range(C_GROUPS):
                gs = slice(g * 64, (g + 1) * 64)
                cu = u_ref[rows, gs].astype(F32)
                dy = dy_ref[rows, gs]
                mixed = jnp.dot(ws_ref[g], vnb[rows, gs], preferred_element_type=F32) + bs_ref[:, gs]
                du_ref[rows, gs] = (dy * mixed * _gelu_grad(cu)).astype(BF16)
                dmix = dy * _gelu(cu)
                dbs_ref[:, gs] += dmix
                dmb = dmix.astype(BF16)
                dws_ref[g] += lax.dot_general(dmb, vnb[rows, gs], NT, preferred_element_type=F32)
                dvn_s[rows, gs] = jnp.dot(wst_ref[g], dmb, preferred_element_type=F32)
        dvn = dvn_s[...]
        dg_ref[...] += jnp.sum(dvn * xhat, axis=0, keepdims=True)
        db_ref[...] += jnp.sum(dvn, axis=0, keepdims=True)
        dxh = dvn * gam_v
        dvg = r * (dxh - jnp.mean(dxh, axis=-1, keepdims=True) - xhat * jnp.mean(dxh * xhat, axis=-1, keepdims=True))
        dv_ref[...] = (dvg * _gelu_grad(cv)).astype(BF16)

        @pl.when(step == nstep - 1)
        def _():
            dbs_ref[...] = _seg_sum(dbs_ref[...], _group_sum_matrix(256, True))

    vec = pl.BlockSpec((1, 256), lambda i: (0, 0))
    mat = pl.BlockSpec((C_GROUPS, C_CHUNK, C_CHUNK), lambda i: (0, 0, 0))
    bsp = pl.BlockSpec((C_CHUNK, 256), lambda i: (0, 0))
    tile = pl.BlockSpec((tm, 256), lambda i: (i, 0))
    return pl.pallas_call(
        body, name=name, grid=(nstep,),
        in_specs=[pl.BlockSpec((tm, 256), lambda i: (i, 5)), pl.BlockSpec((tm, 256), lambda i: (i, 6)),
                  pl.BlockSpec((tm, 256), lambda i: (i, 2)), vec, vec, mat, mat, bsp],
        out_specs=[tile, tile, mat, bsp, vec, vec],
        out_shape=[_sds((t, 256), BF16), _sds((t, 256), BF16), _sds((C_GROUPS, C_CHUNK, C_CHUNK), F32),
                   _sds((C_CHUNK, 256), F32), _sds((1, 256), F32), _sds((1, 256), F32)],
        scratch_shapes=[pltpu.VMEM((tm, 256), F32)],
        compiler_params=_cparams(ARB),
    )(proj, proj, dycat, gam, bet, ws, wst, bst)


FF_TC = 128
FF_NB = D_FF // FF_TC
FF_CH = 64
FF_HALO = 16


def _taps(ref, r0, win, where):
    z = jnp.zeros((FF_HALO, win.shape[1]), F32)
    if where == "first":
        win[0:FF_HALO, :] = z
        win[FF_HALO:, :] = ref[0:FF_CH + FF_HALO, :].astype(F32)
    elif where == "last":
        win[0:FF_CH + FF_HALO, :] = ref[SEQ - FF_CH - FF_HALO:SEQ, :].astype(F32)
        win[FF_CH + FF_HALO:, :] = z
    else:
        win[...] = ref[pl.ds(pl.multiple_of(r0 - FF_HALO, FF_HALO), FF_CH + 2 * FF_HALO), :].astype(F32)
    return tuple(win[FF_HALO + o:FF_HALO + o + FF_CH, :] for o in (-1, 0, 1))


def _chunk_loop(step):
    step(0, lambda ref, win: _taps(ref, 0, win, "first"))

    def mid(i, carry):
        r0 = pl.multiple_of(i * FF_CH, FF_CH)
        step(r0, lambda ref, win: _taps(ref, r0, win, "mid"))
        return carry

    lax.fori_loop(1, SEQ // FF_CH - 1, mid, 0)
    step(SEQ - FF_CH, lambda ref, win: _taps(ref, SEQ - FF_CH, win, "last"))


def _conv3(taps, w_ref, b_ref):
    dn, md, up = taps
    return w_ref[0:1, :] * dn + w_ref[1:2, :] * md + w_ref[2:3, :] * up + b_ref[...]


def _ff_specs(order):
    def at(fn):
        return (lambda b, j: fn(b, j)) if order == "bj" else (lambda j, b: fn(b, j))
    hs = [pl.BlockSpec((None, SEQ, FF_TC), at(lambda b, j, o=o: (b, 0, j + o))) for o in (0, FF_NB)]
    ws = [pl.BlockSpec((3, FF_TC), at(lambda b, j, o=o: (0, j + o))) for o in (0, FF_NB)]
    bs = [pl.BlockSpec((1, FF_TC), at(lambda b, j, o=o: (0, j + o))) for o in (0, FF_NB)]
    return hs, ws, bs


def _conv_gate_fwd(h, cw, cb, *, name):
    t = h.shape[0]
    bl = t // SEQ

    def body(hg_ref, hu_ref, wg_ref, wu_ref, bg_ref, bu_ref, a_ref, cg_ref, cu_ref, win):
        def step(r0, taps):
            cg = _conv3(taps(hg_ref, win.at[0]), wg_ref, bg_ref)
            cu = _conv3(taps(hu_ref, win.at[1]), wu_ref, bu_ref)
            a_ref[pl.ds(r0, FF_CH), :] = (cg * _sigmoid(cg) * cu).astype(BF16)
            cg_ref[pl.ds(r0, FF_CH), :] = cg.astype(BF16)
            cu_ref[pl.ds(r0, FF_CH), :] = cu.astype(BF16)

        _chunk_loop(step)

    hs, ws, bs = _ff_specs("bj")
    h3 = h.reshape(bl, SEQ, 2 * D_FF)
    half = pl.BlockSpec((None, SEQ, FF_TC), lambda b, j: (b, 0, j))
    outs = pl.pallas_call(
        body, name=name, grid=(bl, FF_NB), in_specs=hs + ws + bs, out_specs=[half] * 3,
        out_shape=[_sds((bl, SEQ, D_FF), BF16)] * 3,
        scratch_shapes=[pltpu.VMEM((2, FF_CH + 2 * FF_HALO, FF_TC), F32)],
        compiler_params=_cparams(PAR, PAR),
    )(h3, h3, cw, cw, cb, cb)
    return [o.reshape(t, D_FF) for o in outs]


def _conv_gate_bwd(h, cg_all, cu_all, dact, cw, cb, *, name):
    t = h.shape[0]
    bl = t // SEQ

    def body(hg_ref, hu_ref, wg_ref, wu_ref, bg_ref, bu_ref, da_ref, cg_ref, cu_ref,
             dhg_ref, dhu_ref, dwg_ref, dwu_ref, dbg_ref, dbu_ref, dg_s, du_s, win, sums):
        @pl.when(pl.program_id(1) == 0)
        def _():
            for ref in (dwg_ref, dwu_ref, dbg_ref, dbu_ref):
                ref[...] = jnp.zeros_like(ref)

        sums[...] = jnp.zeros_like(sums)
        red = lambda x: jnp.sum(x.reshape(FF_CH // 8, 8, x.shape[1]), axis=0)

        def pass1(r0, taps):
            tg, tu = taps(hg_ref, win.at[0]), taps(hu_ref, win.at[1])
            cg = cg_ref[pl.ds(r0, FF_CH), :].astype(F32)
            cu = cu_ref[pl.ds(r0, FF_CH), :].astype(F32)
            da = da_ref[pl.ds(r0, FF_CH), :].astype(F32)
            sg = _sigmoid(cg)
            dcg = da * cu * (sg * (1.0 + cg * (1.0 - sg)))
            dcu = da * (cg * sg)
            dg_s[pl.ds(r0, FF_CH), :] = dcg
            du_s[pl.ds(r0, FF_CH), :] = dcu
            for half, (d, tp) in enumerate(((dcg, tg), (dcu, tu))):
                for k in range(3):
                    sums[4 * half + k] += red(d * tp[k])
                sums[4 * half + 3] += red(d)

        _chunk_loop(pass1)
        for half, (dw_ref, db_ref) in enumerate(((dwg_ref, dbg_ref), (dwu_ref, dbu_ref))):
            for k in range(3):
                dw_ref[k:k + 1, :] += jnp.sum(sums[4 * half + k], axis=0, keepdims=True)
            db_ref[...] += jnp.sum(sums[4 * half + 3], axis=0, keepdims=True)

        def pass2(r0, taps):
            for k, (s, w_ref, o_ref) in enumerate(((dg_s, wg_ref, dhg_ref), (du_s, wu_ref, dhu_ref))):
                dn, md, up = taps(s, win.at[k])
                o_ref[pl.ds(r0, FF_CH), :] = (w_ref[0:1, :] * up + w_ref[1:2, :] * md + w_ref[2:3, :] * dn).astype(BF16)

        _chunk_loop(pass2)

    hs, ws, bs = _ff_specs("jb")
    half = pl.BlockSpec((None, SEQ, FF_TC), lambda j, b: (b, 0, j))
    wsp = pl.BlockSpec((3, FF_TC), lambda j, b: (0, j))
    bsp = pl.BlockSpec((1, FF_TC), lambda j, b: (0, j))
    h3 = h.reshape(bl, SEQ, 2 * D_FF)
    dhg, dhu, dwg, dwu, dbg, dbu = pl.pallas_call(
        body, name=name, grid=(FF_NB, bl), in_specs=hs + ws + bs + [half] * 3,
        out_specs=[half, half, wsp, wsp, bsp, bsp],
        out_shape=[_sds((bl, SEQ, D_FF), BF16), _sds((bl, SEQ, D_FF), BF16), _sds((3, D_FF), F32), _sds((3, D_FF), F32),
                   _sds((1, D_FF), F32), _sds((1, D_FF), F32)],
        scratch_shapes=[pltpu.VMEM((SEQ, FF_TC), F32), pltpu.VMEM((SEQ, FF_TC), F32),
                        pltpu.VMEM((2, FF_CH + 2 * FF_HALO, FF_TC), F32), pltpu.VMEM((8, 8, FF_TC), F32)],
        compiler_params=_cparams(PAR, ARB),
    )(h3, h3, cw, cw, cb, cb, *[a.reshape(bl, SEQ, D_FF) for a in (dact, cg_all, cu_all)])
    return (dhg.reshape(t, D_FF), dhu.reshape(t, D_FF), jnp.concatenate([dwg, dwu], axis=1),
            jnp.concatenate([dbg, dbu], axis=1))


def _ple_fwd(x2, gain, wg, pe, pe_blk, wp, *, tm, name):
    t, k = x2.shape

    def body(x_ref, g_ref, wg_ref, pe_ref, wp_ref, hn_ref, x3_ref, gt_ref, pp_ref):
        x = x_ref[...]
        r = lax.rsqrt(jnp.mean(x * x, axis=-1, keepdims=True) + EPS)
        hn = (x * r * g_ref[...]).astype(BF16)
        hn_ref[...] = hn
        gate = _sigmoid(jnp.dot(hn, wg_ref[...], preferred_element_type=F32))
        pp = jnp.dot(pe_ref[...].astype(BF16), wp_ref[...], preferred_element_type=F32)
        gt_ref[...] = gate.astype(BF16)
        pp_ref[...] = pp.astype(BF16)
        x3_ref[...] = x + pp * gate

    row = pl.BlockSpec((tm, k), lambda i: (i, 0))
    return pl.pallas_call(
        body, name=name, grid=(t // tm,),
        in_specs=[row, pl.BlockSpec((1, k), lambda i: (0, 0)), pl.BlockSpec((k, k), lambda i: (0, 0)),
                  pl.BlockSpec((tm, PLE_DIM), lambda i: (pe_blk + i, 0)), pl.BlockSpec((PLE_DIM, k), lambda i: (0, 0))],
        out_specs=[row, row, row, row],
        out_shape=[_sds((t, k), BF16), _sds((t, k), F32), _sds((t, k), BF16), _sds((t, k), BF16)],
        compiler_params=_cparams(PAR),
    )(x2, gain, wg, pe, wp)


def _ple_bwd_ew(dx3, gate, pp, *, tm, name):
    t, n = dx3.shape

    def body(d_ref, g_ref, p_ref, dz_ref, dpp_ref):
        d, g = d_ref[...], g_ref[...]
        dz_ref[...] = (d * p_ref[...] * g * (1.0 - g)).astype(BF16)
        dpp_ref[...] = (d * g).astype(BF16)

    spec = pl.BlockSpec((tm, n), lambda i: (i, 0))
    return pl.pallas_call(
        body, name=name, grid=(t // tm,), in_specs=[spec] * 3, out_specs=[spec] * 2,
        out_shape=[_sds((t, n), BF16)] * 2, compiler_params=_cparams(PAR),
    )(dx3, gate, pp)


def _loss_head(y, tgt, *, tm, name):
    t, d = y.shape

    def body(y_ref, t_ref, l_ref, dy_ref):
        @pl.when(pl.program_id(0) == 0)
        def _():
            l_ref[...] = jnp.zeros_like(l_ref)

        e = y_ref[...] - t_ref[...]
        dy_ref[...] = e * (1.0 / d)
        s = jnp.sum(jnp.sum(e * e, axis=1, keepdims=True), axis=0, keepdims=True)
        l_ref[...] += jnp.broadcast_to(s * (0.5 / d), (8, 128))

    spec = pl.BlockSpec((tm, d), lambda i: (i, 0))
    return pl.pallas_call(
        body, name=name, grid=(t // tm,), in_specs=[spec, spec],
        out_specs=[pl.BlockSpec((8, 128), lambda i: (0, 0)), spec],
        out_shape=[_sds((8, 128), F32), _sds((t, d), F32)], compiler_params=_cparams(ARB),
    )(y, tgt)


BIAS_PC = 8192


def _onehot(bucket_row):
    rows = lax.broadcasted_iota(jnp.int32, (REL_BUCKETS, bucket_row.shape[1]), 0)
    return (rows == bucket_row).astype(BF16)


def _dot3(x, onehot, dims):
    acc = None
    for _ in range(3):
        term = x.astype(BF16)
        part = lax.dot_general(term, onehot, dims, preferred_element_type=F32)
        acc = part if acc is None else acc + part
        x = x - term.astype(F32)
    return acc


def _bias_lookup(table_t, bucket, *, name):
    h = table_t.shape[0]
    p = bucket.shape[1]

    def body(t_ref, b_ref, o_ref):
        bk = b_ref[...]
        val = _dot3(t_ref[...], _onehot(bk), (((1,), (0,)), ((), ())))
        o_ref[...] = jnp.where(bk >= 0, val, NEG_INF)

    return pl.pallas_call(
        body, name=name, grid=(p // BIAS_PC,),
        in_specs=[pl.BlockSpec((h, REL_BUCKETS), lambda i: (0, 0)), pl.BlockSpec((1, BIAS_PC), lambda i: (0, i))],
        out_specs=pl.BlockSpec((h, BIAS_PC), lambda i: (0, i)), out_shape=_sds((h, p), F32),
        compiler_params=_cparams(PAR),
    )(table_t, bucket)


def _bucket_reduce(dbiases, bucket, *, name):
    h, p = dbiases[0].shape
    nl = len(dbiases)

    def body(*refs):
        b_ref, o_ref = refs[nl], refs[nl + 1]

        @pl.when(pl.program_id(0) == 0)
        def _():
            o_ref[...] = jnp.zeros_like(o_ref)

        d = refs[0][...]
        for d_ref in refs[1:nl]:
            d = d + d_ref[...]
        o_ref[...] += _dot3(d, _onehot(b_ref[...]), NT)

    return pl.pallas_call(
        body, name=name, grid=(p // BIAS_PC,),
        in_specs=[pl.BlockSpec((h, BIAS_PC), lambda i: (0, i))] * nl + [pl.BlockSpec((1, BIAS_PC), lambda i: (0, i))],
        out_specs=pl.BlockSpec((h, REL_BUCKETS), lambda i: (0, 0)), out_shape=_sds((h, REL_BUCKETS), F32),
        compiler_params=_cparams(ARB),
    )(*dbiases, bucket)


def _adamw_math(w, g, m, v):
    m = ADAM_B1 * m + (1.0 - ADAM_B1) * g
    v = ADAM_B2 * v + (1.0 - ADAM_B2) * (g * g)
    m_hat = m / (1.0 - ADAM_B1 ** ADAM_STEP)
    v_hat = v / (1.0 - ADAM_B2 ** ADAM_STEP)
    delta = -ADAM_LR * (m_hat / (jnp.sqrt(v_hat) + ADAM_EPS) + ADAM_WD * w)
    return delta, m, v


def _adamw_reduce(parts, w, m, v, *, tr, name):
    nl = len(parts)
    rows, c = w.shape
    r = rows // nl
    nt = r // tr

    def body(*refs):
        p_refs = refs[:nl]
        w_ref, m_ref, v_ref, g_ref, d_ref, nm_ref, nv_ref = refs[nl:]
        for li, p_ref in enumerate(p_refs):
            @pl.when(pl.program_id(0) == li)
            def _(p_ref=p_ref):
                g = p_ref[0].astype(F32)
                for k in range(1, N_DEV):
                    g = g + p_ref[k].astype(F32)
                d, nm, nv = _adamw_math(w_ref[...], g, m_ref[...], v_ref[...])
                g_ref[...] = g
                d_ref[...] = d
                nm_ref[...] = nm
                nv_ref[...] = nv

    def part_map(li):
        return lambda l, i: (0, jnp.where(l == li, i, jnp.where(l < li, 0, nt - 1)), 0)

    spec = pl.BlockSpec((tr, c), lambda l, i: (l * nt + i, 0))
    return pl.pallas_call(
        body, name=name, grid=(nl, nt),
        in_specs=[pl.BlockSpec((N_DEV, tr, c), part_map(li)) for li in range(nl)] + [spec, spec, spec],
        out_specs=[spec] * 4, out_shape=[_sds((rows, c), F32)] * 4, compiler_params=_cparams(ARB, ARB),
    )(*parts, w, m, v)


def _adamw_plain(g, w, m, v, *, name):
    def body(g_ref, w_ref, m_ref, v_ref, d_ref, nm_ref, nv_ref):
        d, nm, nv = _adamw_math(w_ref[...], g_ref[...], m_ref[...], v_ref[...])
        d_ref[...] = d
        nm_ref[...] = nm
        nv_ref[...] = nv

    return pl.pallas_call(body, name=name, out_shape=[_sds(w.shape, F32)] * 3)(g, w, m, v)


def _mesh_pos():
    return lax.axis_index("x"), lax.axis_index("y"), lax.axis_index("c")


def _allgather_body(x_refs, out_refs, send_sems, recv_sems, local_sems, slot):
    x, y, c = _mesh_pos()
    me, sibling = (x, y, c), (x, y, 1 - c)
    chips = [(1 - x, y), (x, 1 - y), (1 - x, 1 - y)]
    waits = []
    for a, (x_ref, out_ref) in enumerate(zip(x_refs, out_refs)):
        def copy(k, block, to, src=None, out_ref=out_ref, a=a):
            return pltpu.make_async_remote_copy(
                src_ref=slot(out_ref, block) if src is None else src, dst_ref=slot(out_ref, block),
                send_sem=send_sems.at[a, k], recv_sem=recv_sems.at[a, k], device_id=to, device_id_type=MESH)

        mine = pltpu.make_async_copy(x_ref, slot(out_ref, me), local_sems.at[a])
        mine.start()
        first = [copy(0, me, sibling, src=x_ref)]
        first += [copy(1 + j, me, (*chip, c), src=x_ref) for j, chip in enumerate(chips)]
        for cp in first:
            cp.start()
        waits.append((copy, mine, first))
    sends = []
    for copy, mine, first in waits:
        passed = [copy(4 + j, (*chip, c), sibling) for j, chip in enumerate(chips)]
        for j, chip in enumerate(chips):
            copy(1 + j, (*chip, c), me).wait_recv()
            passed[j].start()
        sends.append(passed)
    for (copy, mine, first), passed in zip(waits, sends):
        copy(0, sibling, me).wait_recv()
        for j, chip in enumerate(chips):
            copy(4 + j, (*chip, 1 - c), me).wait_recv()
        for cp in first + passed:
            cp.wait_send()
        mine.wait()


PEER_FLIPS = ((0, 0, 1), (1, 0, 0), (0, 1, 0), (1, 1, 0), (1, 0, 1), (0, 1, 1), (1, 1, 1))


def _peer_copies(x_refs, land_refs, send_sem, recv_sem, scatter):
    x, y, c = _mesh_pos()
    me = 4 * x + 2 * y + c
    copies = []
    for x_ref, land_ref in zip(x_refs, land_refs):
        for fx, fy, fc in PEER_FLIPS:
            px, py, pc = x ^ fx, y ^ fy, c ^ fc
            src = x_ref.at[4 * px + 2 * py + pc] if scatter else x_ref
            copies.append(pltpu.make_async_remote_copy(
                src_ref=src, dst_ref=land_ref.at[me], send_sem=send_sem, recv_sem=recv_sem,
                device_id=(px, py, pc), device_id_type=MESH))
    return copies


def _sc_exchange(xs, *, scatter, collective_id, name):
    na = len(xs)
    land_shapes = [x.shape if scatter else (N_DEV,) + x.shape for x in xs]

    def body(*refs):
        x_refs, land_refs = refs[:na], refs[na:2 * na]
        send_sem, recv_sem, local_sem = refs[2 * na:]
        x, y, c = _mesh_pos()
        me = 4 * x + 2 * y + c
        barrier = pltpu.get_barrier_semaphore()
        for fx, fy, fc in PEER_FLIPS:
            pl.semaphore_signal(barrier, inc=1, device_id=(x ^ fx, y ^ fy, c ^ fc), device_id_type=MESH)
        pl.semaphore_wait(barrier, len(PEER_FLIPS))
        for x_ref, land_ref in zip(x_refs, land_refs):
            own = pltpu.make_async_copy(x_ref.at[me] if scatter else x_ref, land_ref.at[me], local_sem)
            own.start()
            own.wait()
        copies = _peer_copies(x_refs, land_refs, send_sem, recv_sem, scatter)
        for cp in copies:
            cp.start()
        for cp in copies:
            cp.wait()

    return pl.kernel(
        body, name=name, out_type=[_sds(s, x.dtype) for s, x in zip(land_shapes, xs)],
        mesh=plsc.ScalarSubcoreMesh(axis_name="sequencer", num_cores=1),
        scratch_types=[pltpu.SemaphoreType.DMA, pltpu.SemaphoreType.DMA, pltpu.SemaphoreType.DMA],
        compiler_params=pltpu.CompilerParams(collective_id=collective_id),
    )(*xs)


def _sc_allgather(xs, *, collective_id, name):
    na = len(xs)

    def body(*refs):
        x_refs, out_refs = refs[:na], refs[na:2 * na]
        send_sems, recv_sems, local_sems = refs[2 * na:]
        x, y, c = _mesh_pos()
        barrier = pltpu.get_barrier_semaphore()
        for fx, fy, fc in PEER_FLIPS:
            pl.semaphore_signal(barrier, inc=1, device_id=(x ^ fx, y ^ fy, c ^ fc), device_id_type=MESH)
        pl.semaphore_wait(barrier, len(PEER_FLIPS))
        _allgather_body(x_refs, out_refs, send_sems, recv_sems, local_sems,
                        lambda ref, pos: ref.at[4 * pos[0] + 2 * pos[1] + pos[2]])

    return pl.kernel(
        body, name=name, out_type=[_sds((N_DEV,) + x.shape, x.dtype) for x in xs],
        mesh=plsc.ScalarSubcoreMesh(axis_name="sequencer", num_cores=1),
        scratch_types=[pltpu.SemaphoreType.DMA((na, 7)), pltpu.SemaphoreType.DMA((na, 7)),
                       pltpu.SemaphoreType.DMA((na,))],
        compiler_params=pltpu.CompilerParams(collective_id=collective_id),
    )(*xs)


def _allgather_vmem(x, *, name):
    r, c = x.shape

    def body(x_ref, out_ref, send_sems, recv_sems, local_sems):
        _allgather_body([x_ref], [out_ref], send_sems, recv_sems, local_sems,
                        lambda ref, pos: ref.at[pl.ds((4 * pos[0] + 2 * pos[1] + pos[2]) * r, r), :])

    vm = pl.BlockSpec(memory_space=pltpu.VMEM)
    return pl.pallas_call(
        body, name=name, in_specs=[vm], out_specs=vm, out_shape=_sds((N_DEV * r, c), x.dtype),
        scratch_shapes=[pltpu.SemaphoreType.DMA((1, 7)), pltpu.SemaphoreType.DMA((1, 7)),
                        pltpu.SemaphoreType.DMA((1,))],
    )(x)


def _sum_slots(gathered, *, name):
    _, r, c = gathered.shape

    def body(g_ref, o_ref):
        acc = g_ref[0]
        for k in range(1, N_DEV):
            acc = acc + g_ref[k]
        o_ref[...] = acc

    return pl.pallas_call(body, name=name, out_shape=_sds((r, c), gathered.dtype))(gathered)


def _t5_bucket(rel):
    nb = REL_BUCKETS // 2
    ret = jnp.where(rel > 0, nb, 0)
    n = jnp.abs(rel)
    max_exact = nb // 2
    nf = jnp.maximum(n, 1).astype(F32)
    large = max_exact + (jnp.log(nf / max_exact) / math.log(REL_MAX_DIST / max_exact)
                         * (nb - max_exact)).astype(jnp.int32)
    large = jnp.minimum(large, nb - 1)
    return ret + jnp.where(n < max_exact, n, large)


def _band_pattern(block, radius, dil):
    kw = block + 2 * radius
    rel = jnp.arange(kw)[None, :] - radius - jnp.arange(block)[:, None]
    return jnp.where(jnp.abs(rel) <= radius, _t5_bucket(rel * dil), -1).astype(jnp.int32).reshape(1, block * kw)


def _rope_tables():
    lane = np.arange(64)
    seg, j = lane // 32, lane % 32
    inv = ROPE_THETA ** (-jnp.arange(0, 32, 2, dtype=F32) / 32)
    tpos = jnp.arange(SEQ)
    pos = jnp.where(jnp.asarray(seg)[None, :] == 0, (tpos // GRID_W)[:, None], (tpos % GRID_W)[:, None])
    ang = pos.astype(F32) * inv[jnp.asarray(j % 16)][None, :]
    cos = jnp.cos(ang)
    sins = jnp.where(jnp.asarray(j)[None, :] < 16, -jnp.sin(ang), jnp.sin(ang))
    return jnp.tile(cos, (1, 4)), jnp.tile(sins, (1, 4))


A_Q, A_K, A_V = (256, 0), (256, 1), (256, 2)
B_Q, B_K, B_V = (256, 0), (128, 2), (128, 3)
A_HEADS = dict(rad=A_RADIUS, nh=4, nkv=4)
B_HEADS = dict(rad=SWA_RADIUS, nh=4, nkv=2)


def _local_step(x, pe, tgt, rel_bias, wts, matmul_weights, grads_ready):
    t = x.shape[0]
    bl = t // SEQ
    cos, sins = _rope_tables()
    blocks_a = [min(BAND_BLOCK, SEQ // d) for d in DILATIONS]
    pats_a = [_band_pattern(blk, A_RADIUS, d) for blk, d in zip(blocks_a, DILATIONS)]
    pat_b = _band_pattern(BAND_BLOCK, SWA_RADIUS, 1)
    table_t = rel_bias.T
    bias_a = [_bias_lookup(table_t[:4], pt, name=f"bias_a{ci}").reshape(4, blk, blk + 2 * A_RADIUS)
              for ci, (pt, blk) in enumerate(zip(pats_a, blocks_a))]
    bias_b = _bias_lookup(table_t[4:], pat_b, name="bias_b").reshape(4, BAND_BLOCK, BAND_BLOCK + 2 * SWA_RADIUS)
    nat4 = lambda a: a.reshape(bl, 1, SEQ, a.shape[-1])

    saved = []
    for li in range(DEPTH):
        w = dict(wts[li])
        w.update(matmul_weights(li, "in", x))
        hn0, proj = _norm_mm((x,), w["g_mix"], w["w_in"], None, tm=1024, tn=1152, name="mix_in_fwd", out_dtype=BF16)
        qa1, qa4, qa16, qb, qd = _qkprep_fwd(proj, w["qk_gains"], cos, sins, tm=512, name="qkprep_fwd")
        qa = (nat4(qa1), qa4, qa16)
        oa, la = [], []
        for ci in range(3):
            o, l = _band_fwd(qa[ci], A_Q, A_K, A_V, bias_a[ci], None, name=f"band_a{ci}_fwd", **A_HEADS)
            oa.append(o)
            la.append(l)
        oa[0], la[0] = oa[0].reshape(t, 256), la[0].reshape(t, 256)
        ya, lse_a = _combine_a(oa, la, tm=512, name="combine_a")
        yb, lse_b = _band_fwd(nat4(qb), B_Q, B_K, B_V, bias_b, w["sink_t"], name="band_b_fwd", **B_HEADS)
        yb = yb.reshape(t, 256)
        yc = _c_fwd(proj, w["c_g"], w["c_b"], w["c_ws"], w["c_bst"], tm=512, name="c_fwd")
        yd, lse_d = _dense_fwd(qd, tq=512, name="dense_fwd")
        w.update(matmul_weights(li, "rest", yd))
        mixed, x1 = _norm_mm((ya, yb, yc, yd), w["out_gain"], w["w_out"], x, tm=1024, tn=1024, name="mix_out_fwd")
        hn1, h = _norm_mm((x1,), w["g_ffn"], w["w_up"], None, tm=1024, tn=1408, name="ffn_up_fwd", out_dtype=BF16)
        act, cg, cu = _conv_gate_fwd(h, w["conv_w"], w["conv_b"], name="conv_gate_fwd")
        x2 = _mm(act, w["w_down"], "nn", x1, tm=1024, tn=1024, out_dtype=F32, name="ffn_down_fwd")
        hn2, x3, gate, pp = _ple_fwd(x2, w["g_ple"], w["w_gate"], pe, li * (t // 1024), w["w_proj"], tm=1024,
                                     name="ple_fwd")
        saved.append(dict(w=w, x0=x, hn0=hn0, proj=proj, qa=qa, qb=qb, qd=qd, ya=ya, lse_a=lse_a, yb=yb, lse_b=lse_b,
                          yc=yc, yd=yd, lse_d=lse_d, mixed=mixed, x1=x1, hn1=hn1, h=h, cg=cg, cu=cu, act=act, x2=x2, hn2=hn2,
                          gate=gate, pp=pp))
        x = x3

    loss_tile, dx = _loss_head(x, tgt, tm=512, name="loss_head")
    grads = [None] * DEPTH
    dbias_a, dbias_bs = [[], [], []], []
    for li in reversed(range(DEPTH)):
        s = saved[li]
        w = s["w"]
        g = {}
        dz, dpp = _ple_bwd_ew(dx, s["gate"], s["pp"], tm=512, name="ple_bwd_ew")
        g["w_gate"] = _mm(s["hn2"], dz, "tn", None, tm=1024, tn=512, out_dtype=BF16, name="dw_gate")
        g["w_proj"] = _mm(pe, dpp, "tn", None, tm=256, tn=1024, out_dtype=BF16, name="dw_proj", a_rows=(li, t))
        dx2, dx2b, g["g_ple"] = _mm_bt_normbwd((dz,), w["w_gate"], (s["x2"],), w["g_ple"], dx, tm=1024, tn=1024,
                                               name="ple_bwd", emit_bf16=True)
        g["w_down"] = _mm(s["act"], dx2b, "tn", None, tm=1408, tn=512, out_dtype=BF16, name="dw_down")
        dact = _mm(dx2b, w["w_down"], "nt", None, tm=1024, tn=1408, out_dtype=BF16, name="ffn_down_bwd")
        dhg, dhu, g["conv_w"], g["conv_b"] = _conv_gate_bwd(s["h"], s["cg"], s["cu"], dact, w["conv_w"], w["conv_b"],
                                                            name="conv_gate_bwd")
        g["w_up"] = jnp.concatenate(
            [_mm(s["hn1"], dhalf, "tn", None, tm=1024, tn=1408, out_dtype=BF16, name=f"dw_up_{nm}")
             for nm, dhalf in (("gate", dhg), ("up", dhu))], axis=1)
        dx1, dx1b, g["g_ffn"] = _mm_bt_normbwd((dhg, dhu), w["w_up"], (s["x1"],), w["g_ffn"], dx2, tm=1024, tn=1408,
                                               name="ffn_up_bwd", emit_bf16=True)
        g["w_out"] = _mm(s["mixed"], dx1b, "tn", None, tm=1024, tn=512, out_dtype=BF16, name="dw_out")
        grads_ready(li, "mid", g)
        dycat, g["out_gain"] = _mm_bt_normbwd((dx1b,), w["w_out"], (s["ya"], s["yb"], s["yc"], s["yd"]), w["out_gain"],
                                              None, tm=1024, tn=1024, name="mix_out_bwd")
        dy_r, lse_r, dl_a, dl_b, dl_d = _deltas(dycat, s["ya"], s["yb"], s["yd"], s["lse_a"], tm=512, name="deltas")
        dy_a = (nat4(dycat),) + tuple(dy_r)
        lse_a = (nat4(s["lse_a"]),) + tuple(lse_r)
        dl_a = (nat4(dl_a[0]),) + tuple(dl_a[1:])
        da = []
        for ci in range(3):
            dq, dk, dv, dbias = _band_bwd(s["qa"][ci], A_Q, A_K, A_V, bias_a[ci], None, dy_a[ci], 0, lse_a[ci],
                                          dl_a[ci], name=f"band_a{ci}_bwd", **A_HEADS)
            if ci == 0:
                dq, dk, dv = (a.reshape(t, 256) for a in (dq, dk, dv))
            da.append((dq, dk, dv))
            dbias_a[ci].append(dbias.reshape(4, -1))
        dqb, dkb, dvb, dbias_b, dsink = _band_bwd(nat4(s["qb"]), B_Q, B_K, B_V, bias_b, w["sink_t"], nat4(dycat), 1,
                                                  nat4(s["lse_b"]), nat4(dl_b), name="band_b_bwd", **B_HEADS)
        dbias_bs.append(dbias_b.reshape(4, -1))
        g["sink"] = dsink[:, 0, 0]
        dd = _dense_bwd(s["qd"], dycat, s["lse_d"], dl_d, tq=128, name="dense_bwd")
        dcu, dcv, g["c_ws"], dbs, g["c_g"], g["c_b"] = _c_bwd(s["proj"], dycat, w["c_g"], w["c_b"], w["c_ws"],
                                                               w["c_wst"], w["c_bst"], tm=512, name="c_bwd")
        g["c_bs"] = dbs[:, ::64].T
        db = (dqb.reshape(t, 256), dkb.reshape(t, 128), dvb.reshape(t, 128))
        dproj, dgains = _qkprep_bwd(s["proj"], da, db, dd, dcu, dcv, w["qk_gains"], cos, sins, tm=512, name="qkprep_bwd")
        g["qk_gain"] = dgains[:6, :64].reshape(3, 2, HEAD_DIM)
        g["w_in"] = _mm(s["hn0"], dproj, "tn", None, tm=1024, tn=1152, out_dtype=BF16, name="dw_in")
        dx, g["g_mix"] = _mm_bt_normbwd((dproj,), w["w_in"], (s["x0"],), w["g_mix"], dx1, tm=1024, tn=1152,
                                        name="mix_in_bwd")
        grads[li] = g
        grads_ready(li, "end", g)
    d_table_a = sum(_bucket_reduce(dbias_a[ci], pats_a[ci], name=f"bucket_a{ci}") for ci in range(3))
    d_table_b = _bucket_reduce(dbias_bs, pat_b, name="bucket_b")
    d_rel_bias = jnp.concatenate([d_table_a, d_table_b], axis=0).T
    return loss_tile[0, 0], dx, grads, d_rel_bias


WEIGHT_NAMES = ("rel_bias", "ln_mix_g", "w_in", "qk_gain", "sink", "c_norm_g", "c_norm_b", "c_ws", "c_bs", "out_gain",
                "w_out", "ln_ffn_g", "w_up", "conv_w", "conv_b", "w_down", "ln_ple_g", "w_ple_gate", "w_ple_proj")
COL_SHARDED = ("w_in", "w_up", "w_ple_proj")
ROW_SHARDED = ("w_out", "w_down", "w_ple_gate")
SMALL_SHARDED = ("conv_w", "out_gain")
REPLICATED = tuple(n for n in WEIGHT_NAMES if n not in COL_SHARDED + ROW_SHARDED + SMALL_SHARDED)
LOCAL_GRAD_KEY = {"ln_mix_g": "g_mix", "ln_ffn_g": "g_ffn", "ln_ple_g": "g_ple", "c_norm_g": "c_g", "c_norm_b": "c_b",
                  "w_ple_gate": "w_gate", "w_ple_proj": "w_proj"}


def _full_from_gathered(name, gathered):
    _, r, c = gathered.shape
    if name in ROW_SHARDED:
        return gathered.reshape(N_DEV * r, c)
    return jnp.transpose(gathered, (1, 0, 2)).reshape(r, N_DEV * c)


def _slots_from_full(name, full):
    rows, cols = full.shape
    if name in ROW_SHARDED:
        return full.reshape(N_DEV, rows // N_DEV, cols)
    return jnp.transpose(full.reshape(rows, N_DEV, cols // N_DEV), (1, 0, 2))


def _piece_rows(shape):
    return -(-int(np.prod(shape)) // 1024) * 8


def _pack_rows(arrays):
    pieces = []
    for a in arrays:
        n, rows = int(np.prod(a.shape)), _piece_rows(a.shape)
        flat = a.astype(F32).reshape(-1)
        if n != rows * LANES:
            flat = jnp.pad(flat, (0, rows * LANES - n))
        pieces.append(flat.reshape(rows, LANES))
    return jnp.concatenate(pieces, axis=0)


def _unpack_rows(packed, shapes):
    out, off = [], 0
    for shp in shapes:
        n, rows = int(np.prod(shp)), _piece_rows(shp)
        piece = packed[off:off + rows]
        out.append((piece if n == rows * LANES else piece.reshape(-1)[:n]).reshape(shp))
        off += rows
    return out


def kernel(x, p, rel_bias, ln_mix_g, w_in, qk_gain, sink, c_norm_g, c_norm_b, c_ws, c_bs, out_gain, w_out, ln_ffn_g, w_up, conv_w, conv_b, w_down, ln_ple_g, w_ple_gate, w_ple_proj, loss_target, m_rel_bias, m_ln_mix_g, m_w_in, m_qk_gain, m_sink, m_c_norm_g, m_c_norm_b, m_c_ws, m_c_bs, m_out_gain, m_w_out, m_ln_ffn_g, m_w_up, m_conv_w, m_conv_b, m_w_down, m_ln_ple_g, m_w_ple_gate, m_w_ple_proj, v_rel_bias, v_ln_mix_g, v_w_in, v_qk_gain, v_sink, v_c_norm_g, v_c_norm_b, v_c_ws, v_c_bs, v_out_gain, v_w_out, v_ln_ffn_g, v_w_up, v_conv_w, v_conv_b, v_w_down, v_ln_ple_g, v_w_ple_gate, v_w_ple_proj):
    env = dict(locals())
    wt = {n: env[n] for n in WEIGHT_NAMES}
    mom_m = {n: env["m_" + n] for n in WEIGHT_NAMES}
    mom_v = {n: env["v_" + n] for n in WEIGHT_NAMES}
    bl = x.shape[0]
    t = bl * SEQ
    me = 4 * lax.axis_index("x") + 2 * lax.axis_index("y") + lax.axis_index("c")

    big = COL_SHARDED + ROW_SHARDED
    full = {}
    small_shapes = [wt[n].shape for n in SMALL_SHARDED]
    small = _allgather_vmem(_pack_rows([wt[n] for n in SMALL_SHARDED]), name="gather_small")
    small = small.reshape(N_DEV, -1)
    off = 0
    for n, shp in zip(SMALL_SHARDED, small_shapes):
        cnt = int(np.prod(shp))
        g = small[:, off:off + cnt].reshape((N_DEV,) + tuple(shp))
        full[n] = jnp.transpose(g, (1, 2, 0, 3)).reshape(shp[0], shp[1], N_DEV * shp[2])
        off += _piece_rows(shp) * LANES

    def head_gain(li, a, b, reps):
        g = jnp.tile(qk_gain[li, a, b], reps)
        return jnp.pad(g, (0, 256 - g.shape[0]))

    wts = []
    for li in range(DEPTH):
        rows = [head_gain(li, 0, 0, 4), head_gain(li, 0, 1, 4), head_gain(li, 1, 0, 4), head_gain(li, 1, 1, 2),
                head_gain(li, 2, 0, 4), head_gain(li, 2, 1, 2), jnp.zeros((256,), F32), jnp.zeros((256,), F32)]
        wts.append(dict(
            g_mix=ln_mix_g[li].reshape(1, -1), qk_gains=jnp.stack(rows),
            sink_t=jnp.broadcast_to(sink[li][:, None, None], (4, 8, 128)),
            c_g=c_norm_g[li].reshape(1, -1), c_b=c_norm_b[li].reshape(1, -1), c_ws=c_ws[li].astype(BF16),
            c_wst=jnp.transpose(c_ws[li], (0, 2, 1)).astype(BF16), c_bst=jnp.repeat(c_bs[li].T, 64, axis=1),
            out_gain=full["out_gain"][li].reshape(1, -1), g_ffn=ln_ffn_g[li].reshape(1, -1),
            conv_w=full["conv_w"][li], conv_b=conv_b[li].reshape(1, -1), g_ple=ln_ple_g[li].reshape(1, -1)))

    local_key = {"w_ple_gate": "w_gate", "w_ple_proj": "w_proj"}

    gather_names = {"in": ("w_in",), "rest": tuple(n for n in big if n != "w_in")}
    gathered = {}
    for cid, (li, names) in enumerate(((0, gather_names["in"]), (0, gather_names["rest"]), (1, big))):
        lands = _sc_allgather([wt[n][li].astype(BF16) for n in names], collective_id=cid,
                              name=f"gather_{li}_{len(names)}")
        gathered.setdefault(li, {}).update(zip(names, lands))

    def matmul_weights(li, part, after):
        out = {}
        for n in gather_names[part]:
            g, _ = lax.optimization_barrier((gathered[li][n], after))
            out[local_key.get(n, n)] = _full_from_gathered(n, g)
        return out

    mid_names = ("w_ple_gate", "w_ple_proj", "w_down", "w_up", "w_out")
    end_names = ("w_in",)
    landed = {}

    def start_exchange(li, names, g, tag, cid):
        slots = [_slots_from_full(n, g[local_key.get(n, n)]) for n in names]
        lands = _sc_exchange(slots, scatter=True, collective_id=cid, name=f"grads_{li}_{tag}")
        landed.update({(n, li): land for n, land in zip(names, lands)})

    def grads_ready(li, stage, g):
        if li == 0:
            start_exchange(li, mid_names if stage == "mid" else end_names, g, stage, 5 if stage == "mid" else 6)
        elif stage == "end":
            start_exchange(li, mid_names + end_names, g, stage, 4)

    loss_part, dx, grads, d_rel_bias = _local_step(
        x.reshape(t, D_MODEL), p.reshape(DEPTH * t, PLE_DIM), loss_target.reshape(t, D_MODEL), rel_bias, wts,
        matmul_weights, grads_ready)

    def local_grad(n):
        if n == "rel_bias":
            return d_rel_bias
        key = LOCAL_GRAD_KEY.get(n, n)
        return jnp.stack([grads[li][key].reshape(wt[n].shape[1:]) if n in REPLICATED else grads[li][key]
                          for li in range(DEPTH)])

    small_names = REPLICATED + SMALL_SHARDED
    small_full_shapes = [wt[n].shape if n in REPLICATED else full[n].shape for n in small_names]
    small_parts = _allgather_vmem(_pack_rows([local_grad(n) for n in small_names] + [loss_part.reshape(1)]),
                                  name="allgather_small_grads")
    small_parts = small_parts.reshape(N_DEV, -1, LANES)

    out_g, out_d, out_m, out_v = {}, {}, {}, {}
    for n in big:
        shp = wt[n].shape
        two_d = lambda a: a.reshape(-1, shp[-1])
        res = _adamw_reduce([landed[n, li] for li in range(DEPTH)], two_d(wt[n]), two_d(mom_m[n]), two_d(mom_v[n]),
                            tr=32 if n == "w_down" else 128, name="adamw_" + n)
        out_g[n], out_d[n], out_m[n], out_v[n] = [r.reshape(shp) for r in res]

    *reduced, loss = _unpack_rows(_sum_slots(small_parts, name="sum_small_grads"), small_full_shapes + [(1,)])
    loss = loss[0]
    reduced = dict(zip(small_names, reduced))
    rep_shapes = [wt[n].shape for n in REPLICATED]
    upd = _adamw_plain(_pack_rows([reduced[n] for n in REPLICATED]), _pack_rows([wt[n] for n in REPLICATED]),
                       _pack_rows([mom_m[n] for n in REPLICATED]), _pack_rows([mom_v[n] for n in REPLICATED]),
                       name="adamw_replicated")
    for dst, packed in zip((out_d, out_m, out_v), upd):
        dst.update(zip(REPLICATED, _unpack_rows(packed, rep_shapes)))
    for n in REPLICATED:
        out_g[n] = reduced[n]
    for n in SMALL_SHARDED:
        shp = wt[n].shape
        g = reduced[n].reshape(shp[0], shp[1], N_DEV, shp[2])
        g = lax.dynamic_index_in_dim(g, me, axis=2, keepdims=False)
        two_d = lambda a: a.reshape(-1, shp[-1])
        res = _adamw_plain(two_d(g), two_d(wt[n]), two_d(mom_m[n]), two_d(mom_v[n]), name="adamw_" + n)
        out_g[n] = g
        out_d[n], out_m[n], out_v[n] = [r.reshape(shp) for r in res]

    return (loss, dx.reshape(bl, SEQ, D_MODEL), *[out_g[n] for n in WEIGHT_NAMES], *[out_d[n] for n in WEIGHT_NAMES],
            *[out_m[n] for n in WEIGHT_NAMES], *[out_v[n] for n in WEIGHT_NAMES])
```

```python
import math

import jax
import jax.numpy as jnp
import numpy as np
from jax import lax
from jax.experimental import pallas as pl
from jax.experimental.pallas import tpu as pltpu
from jax.experimental.pallas import tpu_sc as plsc

F32 = jnp.float32
BF16 = jnp.bfloat16

N_DEV = 8
D_MODEL = 1024
SEQ = 2048
DEPTH = 2
HEAD_DIM = 64
IN_WIDTH = 2304
D_FF = 2816
PLE_DIM = 256
C_CHUNK = 128
C_GROUPS = 4
DILATED_CFGS = ((128, 1), (512, 4), (2048, 16))
DILATIONS = tuple(d for _, d in DILATED_CFGS)
A_RADIUS = 64
SWA_RADIUS = 128
BAND_BLOCK = 256
GRID_W = 64
ROPE_THETA = 10000.0
REL_BUCKETS = 32
REL_MAX_DIST = 1024
EPS = 1e-6
NEG_INF = -1e30
ATTN_SCALE = HEAD_DIM ** -0.5
LANES = 128

ADAM_LR = 0.001
ADAM_B1 = 0.9
ADAM_B2 = 0.999
ADAM_EPS = 1e-08
ADAM_WD = 0.01
ADAM_STEP = 10

MESH = pl.DeviceIdType.MESH
NT = (((1,), (1,)), ((), ()))
TN = (((0,), (0,)), ((), ()))
ARB = "arbitrary"
PAR = "parallel"


def _cparams(*sem):
    return pltpu.CompilerParams(dimension_semantics=tuple(sem))


def _sds(shape, dtype):
    return jax.ShapeDtypeStruct(tuple(shape), dtype)


def _group_sum_matrix(n, same_group):
    r = lax.broadcasted_iota(jnp.int32, (n, n), 0)
    c = lax.broadcasted_iota(jnp.int32, (n, n), 1)
    if same_group:
        return ((r >> 6) == (c >> 6)).astype(F32)
    return ((r & 63) == (c & 63)).astype(F32)


def _seg_sum(x, e):
    eb = e.astype(BF16)
    hi = x.astype(BF16)
    lo = (x - hi.astype(F32)).astype(BF16)
    return jnp.dot(hi, eb, preferred_element_type=F32) + jnp.dot(lo, eb, preferred_element_type=F32)


def _gelu(x):
    c = math.sqrt(2.0 / math.pi)
    return 0.5 * x * (1.0 + jnp.tanh(c * (x + 0.044715 * (x * x * x))))


def _gelu_grad(x):
    c = math.sqrt(2.0 / math.pi)
    t = jnp.tanh(c * (x + 0.044715 * (x * x * x)))
    return 0.5 * (1.0 + t) + 0.5 * x * (1.0 - t * t) * c * (1.0 + 3.0 * 0.044715 * (x * x))


def _sigmoid(x):
    return 1.0 / (1.0 + jnp.exp(-x))


def _scatter_cols(scratch, first, val):
    for c in range(val.shape[1] // LANES):
        scratch[first + c] = val[:, c * LANES:(c + 1) * LANES]


def _gather_cols(scratch, first, ncol):
    return jnp.concatenate([scratch[first + c] for c in range(ncol)], axis=1)


def _read_residue(scratch, first, ncol, r, d):
    n = scratch.shape[1] // d
    return jnp.concatenate([scratch.at[first + c][pl.ds(r, n, stride=d), :] for c in range(ncol)], axis=1)


def _write_residue(scratch, first, r, d, val):
    n = scratch.shape[1] // d
    for c in range(val.shape[1] // LANES):
        scratch.at[first + c][pl.ds(r, n, stride=d), :] = val[:, c * LANES:(c + 1) * LANES]


def _norm_mm(xs, gain, w, res, *, tm, tn, name, out_dtype=F32):
    t = xs[0].shape[0]
    k = sum(x.shape[1] for x in xs)
    n = w.shape[1]
    ng = len(xs)
    has_res = res is not None

    def body(*refs):
        x_refs = refs[:ng]
        g_ref, w_ref = refs[ng], refs[ng + 1]
        res_ref = refs[ng + 2] if has_res else None
        hn_ref, o_ref, hn_s = refs[ng + 2 + has_res:]

        @pl.when(pl.program_id(1) == 0)
        def _():
            off = 0
            for xr in x_refs:
                x = xr[...].astype(F32)
                wd = x.shape[1]
                r = lax.rsqrt(jnp.mean(x * x, axis=-1, keepdims=True) + EPS)
                hn_s[:, off:off + wd] = (x * r * g_ref[:, off:off + wd]).astype(BF16)
                off += wd
            hn_ref[...] = hn_s[...]

        acc = jnp.dot(hn_s[...], w_ref[...], preferred_element_type=F32)
        if has_res:
            acc = acc + res_ref[...]
        o_ref[...] = acc.astype(out_dtype)

    in_specs = [pl.BlockSpec((tm, x.shape[1]), lambda i, j: (i, 0)) for x in xs]
    in_specs += [pl.BlockSpec((1, k), lambda i, j: (0, 0)), pl.BlockSpec((k, tn), lambda i, j: (0, j))]
    args = list(xs) + [gain, w]
    if has_res:
        in_specs.append(pl.BlockSpec((tm, tn), lambda i, j: (i, j)))
        args.append(res)
    return pl.pallas_call(
        body, name=name, grid=(t // tm, n // tn), in_specs=in_specs,
        out_specs=[pl.BlockSpec((tm, k), lambda i, j: (i, 0)), pl.BlockSpec((tm, tn), lambda i, j: (i, j))],
        out_shape=[_sds((t, k), BF16), _sds((t, n), out_dtype)],
        scratch_shapes=[pltpu.VMEM((tm, k), BF16)],
        compiler_params=_cparams(PAR, ARB),
    )(*args)


def _mm(a, b, mode, res, *, tm, tn, out_dtype, name, a_rows=None):
    if mode == "tn":
        kk, m = a.shape
        blk_a = 0
        if a_rows is not None:
            blk_a, kk = a_rows
        a_spec = pl.BlockSpec((kk, tm), lambda i, j: (blk_a, i))
    else:
        m, kk = a.shape
        a_spec = pl.BlockSpec((tm, kk), lambda i, j: (i, 0))
    if mode == "nt":
        n = b.shape[0]
        b_spec = pl.BlockSpec((tn, kk), lambda i, j: (j, 0))
    else:
        n = b.shape[1]
        b_spec = pl.BlockSpec((kk, tn), lambda i, j: (0, j))
    has_res = res is not None

    def body(*refs):
        a_ref, b_ref = refs[0], refs[1]
        o_ref = refs[-1]
        av = a_ref[...].astype(BF16)
        bv = b_ref[...].astype(BF16)
        if mode == "nn":
            acc = jnp.dot(av, bv, preferred_element_type=F32)
        elif mode == "nt":
            acc = lax.dot_general(av, bv, NT, preferred_element_type=F32)
        else:
            acc = lax.dot_general(av, bv, TN, preferred_element_type=F32)
        if has_res:
            acc = acc + refs[2][...]
        o_ref[...] = acc.astype(out_dtype)

    in_specs = [a_spec, b_spec]
    args = [a, b]
    if has_res:
        in_specs.append(pl.BlockSpec((tm, tn), lambda i, j: (i, j)))
        args.append(res)
    return pl.pallas_call(
        body, name=name, grid=(m // tm, n // tn), in_specs=in_specs,
        out_specs=pl.BlockSpec((tm, tn), lambda i, j: (i, j)),
        out_shape=_sds((m, n), out_dtype),
        compiler_params=_cparams(PAR, PAR),
    )(*args)


def _mm_bt_normbwd(dys, w, xs, gain, dres, *, tm, tn, name, emit_bf16=False):
    t, wd_each = dys[0].shape
    nd = len(dys)
    per = wd_each // tn
    nj = nd * per
    k = w.shape[0]
    ng = len(xs)
    has_res = dres is not None

    def body(*refs):
        dy_refs = refs[:nd]
        w_ref = refs[nd]
        x_refs = refs[nd + 1:nd + 1 + ng]
        g_ref = refs[nd + 1 + ng]
        dres_ref = refs[nd + 2 + ng] if has_res else None
        outs = refs[nd + 2 + ng + has_res:]
        dx_ref = outs[0]
        dxb_ref = outs[1] if emit_bf16 else None
        dg_ref, acc = outs[1 + emit_bf16:]
        i, j = pl.program_id(0), pl.program_id(1)

        @pl.when(j == 0)
        def _():
            acc[...] = jnp.zeros_like(acc)

        for d, dy_ref in enumerate(dy_refs):
            @pl.when((j >= d * per) & (j < (d + 1) * per))
            def _(dy_ref=dy_ref):
                acc[...] += lax.dot_general(dy_ref[...].astype(BF16), w_ref[...], NT, preferred_element_type=F32)

        @pl.when(j == nj - 1)
        def _():
            @pl.when(i == 0)
            def _():
                dg_ref[...] = jnp.zeros_like(dg_ref)

            off = 0
            for xr in x_refs:
                x = xr[...].astype(F32)
                wd = x.shape[1]
                g = g_ref[:, off:off + wd]
                dyn = acc[:, off:off + wd]
                r = lax.rsqrt(jnp.mean(x * x, axis=-1, keepdims=True) + EPS)
                gdy = dyn * g
                dx = r * gdy - x * (r * r * r * jnp.mean(gdy * x, axis=-1, keepdims=True))
                if has_res:
                    dx = dx + dres_ref[:, off:off + wd]
                dx_ref[:, off:off + wd] = dx
                if emit_bf16:
                    dxb_ref[:, off:off + wd] = dx.astype(BF16)
                dg_ref[:, off:off + wd] += jnp.sum(dyn * x * r, axis=0, keepdims=True)
                off += wd

    def dy_map(d):
        return lambda i, j: (i, jnp.clip(j - d * per, 0, per - 1))

    in_specs = [pl.BlockSpec((tm, tn), dy_map(d)) for d in range(nd)]
    in_specs.append(pl.BlockSpec((k, tn), lambda i, j: (0, j)))
    in_specs += [pl.BlockSpec((tm, x.shape[1]), lambda i, j: (i, 0)) for x in xs]
    in_specs.append(pl.BlockSpec((1, k), lambda i, j: (0, 0)))
    args = list(dys) + [w] + list(xs) + [gain]
    if has_res:
        in_specs.append(pl.BlockSpec((tm, k), lambda i, j: (i, 0)))
        args.append(dres)
    row = pl.BlockSpec((tm, k), lambda i, j: (i, 0))
    out_specs = [row] + ([row] if emit_bf16 else []) + [pl.BlockSpec((1, k), lambda i, j: (0, 0))]
    out_shape = [_sds((t, k), F32)] + ([_sds((t, k), BF16)] if emit_bf16 else []) + [_sds((1, k), F32)]
    return pl.pallas_call(
        body, name=name, grid=(t // tm, nj), in_specs=in_specs, out_specs=out_specs, out_shape=out_shape,
        scratch_shapes=[pltpu.VMEM((tm, k), F32)],
        compiler_params=_cparams(ARB, ARB),
    )(*args)


def _rope_partner(y):
    n = y.shape[1]
    lane = lax.broadcasted_iota(jnp.int32, y.shape, 1)
    return jnp.where((lane & 31) < 16, pltpu.roll(y, n - 16, 1), pltpu.roll(y, 16, 1))


def _residue_specs(tm, width, nt):
    specs = [pl.BlockSpec((tm, width), lambda b, i: (b * nt + i, 0))]
    for d in DILATIONS[1:]:
        specs.append(pl.BlockSpec((None, d, tm // d, width), lambda b, i: (b, 0, i, 0)))
    return specs


def _residue_shapes(bl, width, dtype):
    return [_sds((bl * SEQ, width), dtype)] + [_sds((bl, d, SEQ // d, width), dtype) for d in DILATIONS[1:]]


def _qkprep_fwd(proj, gains, cos, sins, *, tm, name):
    t = proj.shape[0]
    bl = t // SEQ
    nt = SEQ // tm

    def body(p_ref, g_ref, c_ref, s_ref, qa1_ref, qa4_ref, qa16_ref, qb_ref, qd_ref, scr):
        e = _group_sum_matrix(256, True)

        def hn(x, row):
            x = x.astype(F32)
            wd = x.shape[1]
            ms = _seg_sum(x * x, e[:wd, :wd]) * (1.0 / HEAD_DIM)
            return x * lax.rsqrt(ms + EPS) * g_ref[row:row + 1, :wd]

        qa = jnp.concatenate([hn(p_ref[:, 0:256], 0) * ATTN_SCALE, hn(p_ref[:, 256:512], 1),
                              p_ref[:, 512:768].astype(F32)], axis=1)
        qa1_ref[...] = qa.astype(BF16)
        _scatter_cols(scr, 0, qa)
        for d, ref in ((4, qa4_ref), (16, qa16_ref)):
            for r in range(d):
                ref[r] = _read_residue(scr, 0, 6, r, d).astype(BF16)
        qb_ref[:, 0:256] = (hn(p_ref[:, 768:1024], 2) * ATTN_SCALE).astype(BF16)
        qb_ref[:, 256:384] = hn(p_ref[:, 1024:1152], 3).astype(BF16)
        qb_ref[:, 384:512] = p_ref[:, 1152:1280].astype(BF16)
        yq = hn(p_ref[:, 1792:2048], 4)
        yq = yq * c_ref[...] + _rope_partner(yq) * s_ref[...]
        qd_ref[:, 0:256] = (yq * ATTN_SCALE).astype(BF16)
        yk = hn(p_ref[:, 2048:2176], 5)
        yk = yk * c_ref[:, 0:128] + _rope_partner(yk) * s_ref[:, 0:128]
        qd_ref[:, 256:384] = yk.astype(BF16)
        qd_ref[:, 384:512] = p_ref[:, 2176:2304].astype(BF16)

    row = lambda width: pl.BlockSpec((tm, width), lambda b, i: (b * nt + i, 0))
    tab = pl.BlockSpec((tm, 256), lambda b, i: (i, 0))
    return pl.pallas_call(
        body, name=name, grid=(bl, nt),
        in_specs=[row(IN_WIDTH), pl.BlockSpec((8, 256), lambda b, i: (0, 0)), tab, tab],
        out_specs=_residue_specs(tm, 768, nt) + [row(512), row(512)],
        out_shape=_residue_shapes(bl, 768, BF16) + [_sds((t, 512), BF16), _sds((t, 512), BF16)],
        scratch_shapes=[pltpu.VMEM((6, tm, LANES), F32)],
        compiler_params=_cparams(PAR, PAR),
    )(proj, gains, cos, sins)


def _qkprep_bwd(proj, da, db, dd, dcu, dcv, gains, cos, sins, *, tm, name):
    t = proj.shape[0]
    bl = t // SEQ
    nt = SEQ // tm
    flat = [a for cfg in da for a in cfg] + list(db) + list(dd) + [dcu, dcv]

    def body(*refs):
        p_ref, g_ref, c_ref, s_ref = refs[:4]
        d_refs = refs[4:4 + len(flat)]
        dp_ref, dg_ref, scr = refs[4 + len(flat):]
        a_refs = d_refs[:9]
        dqb_ref, dkb_ref, dvb_ref, dqd_ref, dkd_ref, dvd_ref, dcu_ref, dcv_ref = d_refs[9:]
        e = _group_sum_matrix(256, True)
        first = (pl.program_id(0) == 0) & (pl.program_id(1) == 0)
        last = (pl.program_id(0) == bl - 1) & (pl.program_id(1) == nt - 1)

        @pl.when(first)
        def _():
            dg_ref[...] = jnp.zeros_like(dg_ref)

        def hn_bwd(x, dy, row):
            x, dy = x.astype(F32), dy.astype(F32)
            wd = x.shape[1]
            ee = e[:wd, :wd]
            g = g_ref[row:row + 1, :wd]
            r = lax.rsqrt(_seg_sum(x * x, ee) * (1.0 / HEAD_DIM) + EPS)
            gdy = dy * g
            dx = r * gdy - x * (r * r * r * (_seg_sum(gdy * x, ee) * (1.0 / HEAD_DIM)))
            dg_ref[row:row + 1, :wd] += jnp.sum(dy * x * r, axis=0, keepdims=True)
            return dx

        def rope_bwd(dy, wd):
            dy = dy.astype(F32)
            return dy * c_ref[:, :wd] + _rope_partner(dy * s_ref[:, :wd])

        dqkv = jnp.concatenate([a_refs[m][...].astype(F32) for m in range(3)], axis=1)
        for ci, d in ((1, 4), (2, 16)):
            for r in range(d):
                part = jnp.concatenate([a_refs[3 * ci + m][r].astype(F32) for m in range(3)], axis=1)
                _write_residue(scr, 0, r, d, part)
            dqkv = dqkv + _gather_cols(scr, 0, 6)
        dp_ref[:, 0:256] = hn_bwd(p_ref[:, 0:256], dqkv[:, 0:256] * ATTN_SCALE, 0).astype(BF16)
        dp_ref[:, 256:512] = hn_bwd(p_ref[:, 256:512], dqkv[:, 256:512], 1).astype(BF16)
        dp_ref[:, 512:768] = dqkv[:, 512:768].astype(BF16)
        dp_ref[:, 768:1024] = hn_bwd(p_ref[:, 768:1024], dqb_ref[...] * ATTN_SCALE, 2).astype(BF16)
        dp_ref[:, 1024:1152] = hn_bwd(p_ref[:, 1024:1152], dkb_ref[...], 3).astype(BF16)
        dp_ref[:, 1152:1280] = dvb_ref[...].astype(BF16)
        dp_ref[:, 1280:1536] = dcu_ref[...].astype(BF16)
        dp_ref[:, 1536:1792] = dcv_ref[...].astype(BF16)
        dp_ref[:, 1792:2048] = hn_bwd(p_ref[:, 1792:2048], rope_bwd(dqd_ref[...] * ATTN_SCALE, 256), 4).astype(BF16)
        dp_ref[:, 2048:2176] = hn_bwd(p_ref[:, 2048:2176], rope_bwd(dkd_ref[...], 128), 5).astype(BF16)
        dp_ref[:, 2176:2304] = dvd_ref[...].astype(BF16)

        @pl.when(last)
        def _():
            dg_ref[...] = _seg_sum(dg_ref[...], _group_sum_matrix(256, False))

    row = lambda width: pl.BlockSpec((tm, width), lambda b, i: (b * nt + i, 0))
    tab = pl.BlockSpec((tm, 256), lambda b, i: (i, 0))
    in_specs = [row(IN_WIDTH), pl.BlockSpec((8, 256), lambda b, i: (0, 0)), tab, tab]
    res_specs = _residue_specs(tm, 256, nt)
    in_specs += [res_specs[ci] for ci in range(3) for _ in range(3)]
    in_specs += [row(a.shape[1]) for a in flat[9:]]
    return pl.pallas_call(
        body, name=name, grid=(bl, nt), in_specs=in_specs,
        out_specs=[row(IN_WIDTH), pl.BlockSpec((8, 256), lambda b, i: (0, 0))],
        out_shape=[_sds((t, IN_WIDTH), BF16), _sds((8, 256), F32)],
        scratch_shapes=[pltpu.VMEM((6, tm, LANES), F32)],
        compiler_params=_cparams(ARB, ARB),
    )(proj, gains, cos, sins, *flat)


BAND_ROWS_PER_STEP = 512


def _residues_per_step(dil, seq_len):
    return min(dil, max(1, BAND_ROWS_PER_STEP // seq_len))


def _band_spec(seq_len, spec, rb):
    width, idx = spec
    return pl.BlockSpec((None, rb, seq_len, width), lambda b, r: (b, r, 0, idx))


def _fill_padded(dst, src_ref, rad, seq_len):
    z = jnp.zeros((rad, dst.shape[1]), dst.dtype)
    dst[0:rad, :] = z
    dst[rad + seq_len:rad + seq_len + rad, :] = z
    dst[rad:rad + seq_len, :] = src_ref[...]


def _band_fwd(src, qs, ks, vs, bias, sink, *, rad, nh, nkv, name):
    bl, dil, sl, _ = src.shape
    blk = bias.shape[1]
    kw = blk + 2 * rad
    nb = sl // blk
    rep = nh // nkv
    has_sink = sink is not None
    rb = _residues_per_step(dil, sl)

    def body(*refs):
        q_all, k_all, v_all, b_ref = refs[:4]
        s_ref = refs[4] if has_sink else None
        o_all, l_all, kp, vp = refs[4 + has_sink:]
        for ri in range(rb):
            one_sequence(q_all.at[ri], k_all.at[ri], v_all.at[ri], b_ref, s_ref, o_all.at[ri], l_all.at[ri], kp, vp)

    def one_sequence(q_ref, k_ref, v_ref, b_ref, s_ref, o_ref, l_ref, kp, vp):
        _fill_padded(kp, k_ref, rad, sl)
        _fill_padded(vp, v_ref, rad, sl)

        def blk_body(i, carry):
            r0 = pl.multiple_of(i * blk, blk)
            qb = q_ref[pl.ds(r0, blk), :]
            kwin = kp[pl.ds(r0, kw), :]
            vwin = vp[pl.ds(r0, kw), :]
            col = r0 - rad + lax.broadcasted_iota(jnp.int32, (blk, kw), 1)
            neg = jnp.where((col >= 0) & (col < sl), 0.0, NEG_INF).astype(F32)
            for h in range(nh):
                g = h // rep
                hs = slice(h * HEAD_DIM, (h + 1) * HEAD_DIM)
                gs = slice(g * HEAD_DIM, (g + 1) * HEAD_DIM)
                s = lax.dot_general(qb[:, hs], kwin[:, gs], NT, preferred_element_type=F32)
                s = s + b_ref[h] + neg
                m = jnp.max(s, axis=1, keepdims=True)
                if has_sink:
                    sk = s_ref[h][0:1, 0:1]
                    m = jnp.maximum(m, sk)
                p = jnp.exp(s - m)
                den = jnp.sum(p, axis=1, keepdims=True)
                if has_sink:
                    den = den + jnp.exp(sk - m)
                o = jnp.dot(p.astype(BF16), vwin[:, gs], preferred_element_type=F32) / den
                o_ref[pl.ds(r0, blk), hs] = o.astype(BF16)
                l_ref[pl.ds(r0, blk), hs] = jnp.broadcast_to(m + jnp.log(den), (blk, HEAD_DIM))
            return carry

        lax.fori_loop(0, nb, blk_body, 0)

    in_specs = [_band_spec(sl, qs, rb), _band_spec(sl, ks, rb), _band_spec(sl, vs, rb),
                pl.BlockSpec((nh, blk, kw), lambda b, r: (0, 0, 0))]
    args = [src] * 3 + [bias]
    if has_sink:
        in_specs.append(pl.BlockSpec((nh, 8, 128), lambda b, r: (0, 0, 0)))
        args.append(sink)
    return pl.pallas_call(
        body, name=name, grid=(bl, dil // rb), in_specs=in_specs,
        out_specs=[_band_spec(sl, (256, 0), rb)] * 2,
        out_shape=[_sds((bl, dil, sl, 256), BF16), _sds((bl, dil, sl, 256), F32)],
        scratch_shapes=[pltpu.VMEM((sl + 2 * rad, ks[0]), BF16), pltpu.VMEM((sl + 2 * rad, vs[0]), BF16)],
        compiler_params=_cparams(PAR, PAR),
    )(*args)


def _band_bwd(src, qs, ks, vs, bias, sink, dy, dcol, lse, delta, *, rad, nh, nkv, name):
    bl, dil, sl, _ = src.shape
    blk = bias.shape[1]
    kw = blk + 2 * rad
    nb = sl // blk
    rep = nh // nkv
    has_sink = sink is not None
    rb = _residues_per_step(dil, sl)
    wk, wv = ks[0], vs[0]

    def body(*refs):
        q_all, k_all, v_all, b_ref = refs[:4]
        s_ref = refs[4] if has_sink else None
        do_all, l_all, dl_all = refs[4 + has_sink:7 + has_sink]
        outs = refs[7 + has_sink:]
        dsk_ref = None
        if has_sink:
            dq_all, dk_all, dv_all, db_ref, dsk_ref, kp, vp, dka, dva = outs
        else:
            dq_all, dk_all, dv_all, db_ref, kp, vp, dka, dva = outs

        @pl.when((pl.program_id(0) == 0) & (pl.program_id(1) == 0))
        def _():
            db_ref[...] = jnp.zeros_like(db_ref)
            if has_sink:
                dsk_ref[...] = jnp.zeros_like(dsk_ref)

        for ri in range(rb):
            one_sequence(q_all.at[ri], k_all.at[ri], v_all.at[ri], b_ref, s_ref, do_all.at[ri], l_all.at[ri],
                         dl_all.at[ri], dq_all.at[ri], dk_all.at[ri], dv_all.at[ri], db_ref, dsk_ref, kp, vp, dka, dva)

    def one_sequence(q_ref, k_ref, v_ref, b_ref, s_ref, do_ref, l_ref, dl_ref, dq_ref, dk_ref, dv_ref, db_ref, dsk_ref,
                     kp, vp, dka, dva):
        _fill_padded(kp, k_ref, rad, sl)
        _fill_padded(vp, v_ref, rad, sl)
        dka[...] = jnp.zeros_like(dka)
        dva[...] = jnp.zeros_like(dva)

        def blk_body(i, carry):
            r0 = pl.multiple_of(i * blk, blk)
            qb = q_ref[pl.ds(r0, blk), :]
            kwin = kp[pl.ds(r0, kw), :]
            vwin = vp[pl.ds(r0, kw), :]
            dob = do_ref[pl.ds(r0, blk), :].astype(BF16)
            lb = l_ref[pl.ds(r0, blk), :]
            dlb = dl_ref[pl.ds(r0, blk), :]
            col = r0 - rad + lax.broadcasted_iota(jnp.int32, (blk, kw), 1)
            neg = jnp.where((col >= 0) & (col < sl), 0.0, NEG_INF).astype(F32)
            for h in range(nh):
                g = h // rep
                hs = slice(h * HEAD_DIM, (h + 1) * HEAD_DIM)
                gs = slice(g * HEAD_DIM, (g + 1) * HEAD_DIM)
                qh, kh, vh, doh = qb[:, hs], kwin[:, gs], vwin[:, gs], dob[:, hs]
                lh = lb[:, h * HEAD_DIM:h * HEAD_DIM + 1]
                dlh = dlb[:, h * HEAD_DIM:h * HEAD_DIM + 1]
                s = lax.dot_general(qh, kh, NT, preferred_element_type=F32) + b_ref[h] + neg
                p = jnp.exp(s - lh)
                dp = lax.dot_general(doh, vh, NT, preferred_element_type=F32)
                ds = p * (dp - dlh)
                dsb = ds.astype(BF16)
                dq_ref[pl.ds(r0, blk), hs] = jnp.dot(dsb, kh, preferred_element_type=F32).astype(BF16)
                dka[pl.ds(r0, kw), gs] += lax.dot_general(dsb, qh, TN, preferred_element_type=F32)
                dva[pl.ds(r0, kw), gs] += lax.dot_general(p.astype(BF16), doh, TN, preferred_element_type=F32)
                db_ref[h] += ds
                if has_sink:
                    ps = jnp.exp(s_ref[h][0:1, 0:1] - lh)
                    dsk_ref[h] += jnp.broadcast_to(-jnp.sum(ps * dlh, axis=0, keepdims=True), (8, 128))
            return carry

        lax.fori_loop(0, nb, blk_body, 0)
        dk_ref[...] = dka[rad:rad + sl, :].astype(BF16)
        dv_ref[...] = dva[rad:rad + sl, :].astype(BF16)

    const3 = lambda b, r: (0, 0, 0)
    in_specs = [_band_spec(sl, qs, rb), _band_spec(sl, ks, rb), _band_spec(sl, vs, rb),
                pl.BlockSpec((nh, blk, kw), const3)]
    args = [src] * 3 + [bias]
    if has_sink:
        in_specs.append(pl.BlockSpec((nh, 8, 128), const3))
        args.append(sink)
    row = _band_spec(sl, (256, 0), rb)
    in_specs += [_band_spec(sl, (256, dcol), rb), row, row]
    args += [dy, lse, delta]
    out_specs = [row, _band_spec(sl, (wk, 0), rb), _band_spec(sl, (wv, 0), rb), pl.BlockSpec((nh, blk, kw), const3)]
    out_shape = [_sds((bl, dil, sl, 256), BF16), _sds((bl, dil, sl, wk), BF16), _sds((bl, dil, sl, wv), BF16),
                 _sds((nh, blk, kw), F32)]
    if has_sink:
        out_specs.append(pl.BlockSpec((nh, 8, 128), const3))
        out_shape.append(_sds((nh, 8, 128), F32))
    return pl.pallas_call(
        body, name=name, grid=(bl, dil // rb), in_specs=in_specs, out_specs=out_specs, out_shape=out_shape,
        scratch_shapes=[pltpu.VMEM((sl + 2 * rad, wk), BF16), pltpu.VMEM((sl + 2 * rad, wv), BF16),
                        pltpu.VMEM((sl + 2 * rad, wk), F32), pltpu.VMEM((sl + 2 * rad, wv), F32)],
        compiler_params=_cparams(ARB, ARB),
    )(*args)


def _combine_a(os_, ls_, *, tm, name):
    bl = os_[1].shape[0]
    t = bl * SEQ
    nt = SEQ // tm

    def body(o1, o4, o16, l1, l4, l16, y_ref, lt_ref, scr):
        for k, (d, ref) in enumerate(((4, o4), (16, o16), (4, l4), (16, l16))):
            for r in range(d):
                _write_residue(scr, 2 * k, r, d, ref[r].astype(F32))
        o2, o3, b, c = (_gather_cols(scr, 2 * k, 2) for k in range(4))
        a = l1[...]
        m = jnp.maximum(jnp.maximum(a, b), c)
        ea, eb, ec = jnp.exp(a - m), jnp.exp(b - m), jnp.exp(c - m)
        den = ea + eb + ec
        y_ref[...] = ((ea / den) * o1[...].astype(F32) + (eb / den) * o2 + (ec / den) * o3).astype(BF16)
        lt_ref[...] = m + jnp.log(den)

    specs = _residue_specs(tm, 256, nt)
    return pl.pallas_call(
        body, name=name, grid=(bl, nt), in_specs=specs * 2, out_specs=[specs[0]] * 2,
        out_shape=[_sds((t, 256), BF16), _sds((t, 256), F32)], scratch_shapes=[pltpu.VMEM((8, tm, LANES), F32)],
        compiler_params=_cparams(PAR, PAR),
    )(*os_, *ls_)


def _deltas(dycat, ya, yb, yd, lse_a, *, tm, name):
    t = ya.shape[0]
    bl = t // SEQ
    nt = SEQ // tm

    def body(dy_ref, ya_ref, yb_ref, yd_ref, la_ref, dy4, dy16, l4, l16, da1, da4, da16, db_ref, dd_ref, scr):
        e = _group_sum_matrix(256, True)
        dya = dy_ref[:, 0:256]
        dla = _seg_sum(dya * ya_ref[...].astype(F32), e)
        da1[...] = dla
        db_ref[...] = _seg_sum(dy_ref[:, 256:512] * yb_ref[...].astype(F32), e)
        dd_ref[...] = _seg_sum(dy_ref[:, 768:1024] * yd_ref[...].astype(F32), e)
        for k, (val, r4, r16) in enumerate(((dya, dy4, dy16), (la_ref[...], l4, l16), (dla, da4, da16))):
            _scatter_cols(scr, 2 * k, val)
            for d, ref in ((4, r4), (16, r16)):
                for r in range(d):
                    ref[r] = _read_residue(scr, 2 * k, 2, r, d)

    specs = _residue_specs(tm, 256, nt)
    nat = specs[0]
    shapes = _residue_shapes(bl, 256, F32)
    outs = pl.pallas_call(
        body, name=name, grid=(bl, nt),
        in_specs=[pl.BlockSpec((tm, 1024), lambda b, i: (b * nt + i, 0)), nat, nat, nat, nat],
        out_specs=specs[1:] + specs[1:] + specs + [nat, nat],
        out_shape=shapes[1:] + shapes[1:] + shapes + [shapes[0], shapes[0]],
        scratch_shapes=[pltpu.VMEM((6, tm, LANES), F32)],
        compiler_params=_cparams(PAR, PAR),
    )(dycat, ya, yb, yd, lse_a)
    return outs[0:2], outs[2:4], outs[4:7], outs[7], outs[8]


def _dense_fwd(qd, *, tq, name):
    t = qd.shape[0]
    bl = t // SEQ
    nq = SEQ // tq

    def body(q_ref, k_ref, v_ref, o_ref, l_ref):
        q = q_ref[...]
        for g in range(2):
            h0, h1 = 2 * g, 2 * g + 1
            q2 = jnp.concatenate([q[:, h0 * 64:(h0 + 1) * 64], q[:, h1 * 64:(h1 + 1) * 64]], axis=0)
            kg = k_ref[:, g * 64:(g + 1) * 64]
            vg = v_ref[:, g * 64:(g + 1) * 64]
            s = lax.dot_general(q2, kg, NT, preferred_element_type=F32)
            m = jnp.max(s, axis=1, keepdims=True)
            p = jnp.exp(s - m)
            den = jnp.sum(p, axis=1, keepdims=True)
            o2 = jnp.dot(p.astype(BF16), vg, preferred_element_type=F32) / den
            l2 = jnp.broadcast_to(m + jnp.log(den), (2 * tq, 64))
            o_ref[:, h0 * 64:(h0 + 1) * 64] = o2[:tq].astype(BF16)
            o_ref[:, h1 * 64:(h1 + 1) * 64] = o2[tq:].astype(BF16)
            l_ref[:, h0 * 64:(h0 + 1) * 64] = l2[:tq]
            l_ref[:, h1 * 64:(h1 + 1) * 64] = l2[tq:]

    q3 = qd.reshape(bl, SEQ, 512)
    o, lse = pl.pallas_call(
        body, name=name, grid=(bl, nq),
        in_specs=[pl.BlockSpec((None, tq, 256), lambda b, i: (b, i, 0)),
                  pl.BlockSpec((None, SEQ, 128), lambda b, i: (b, 0, 2)),
                  pl.BlockSpec((None, SEQ, 128), lambda b, i: (b, 0, 3))],
        out_specs=[pl.BlockSpec((None, tq, 256), lambda b, i: (b, i, 0))] * 2,
        out_shape=[_sds((bl, SEQ, 256), BF16), _sds((bl, SEQ, 256), F32)],
        compiler_params=_cparams(PAR, PAR),
    )(q3, q3, q3)
    return o.reshape(t, 256), lse.reshape(t, 256)


def _dense_bwd(qd, dycat, lse, delta, *, tq, name):
    t = qd.shape[0]
    bl = t // SEQ
    nq = SEQ // tq

    def body(q_ref, k_ref, v_ref, do_ref, l_ref, dl_ref, dq_ref, dk_ref, dv_ref, dkt, dvt):
        @pl.when(pl.program_id(1) == 0)
        def _():
            dkt[...] = jnp.zeros_like(dkt)
            dvt[...] = jnp.zeros_like(dvt)

        q = q_ref[...]
        do = do_ref[...].astype(BF16)
        lv = l_ref[...]
        dlv = dl_ref[...]
        for g in range(2):
            h0, h1 = 2 * g, 2 * g + 1
            q2 = jnp.concatenate([q[:, h0 * 64:(h0 + 1) * 64], q[:, h1 * 64:(h1 + 1) * 64]], axis=0)
            do2 = jnp.concatenate([do[:, h0 * 64:(h0 + 1) * 64], do[:, h1 * 64:(h1 + 1) * 64]], axis=0)
            l2 = jnp.concatenate([lv[:, h0 * 64:h0 * 64 + 1], lv[:, h1 * 64:h1 * 64 + 1]], axis=0)
            dl2 = jnp.concatenate([dlv[:, h0 * 64:h0 * 64 + 1], dlv[:, h1 * 64:h1 * 64 + 1]], axis=0)
            kg = k_ref[:, g * 64:(g + 1) * 64]
            vg = v_ref[:, g * 64:(g + 1) * 64]
            s = lax.dot_general(q2, kg, NT, preferred_element_type=F32)
            p = jnp.exp(s - l2)
            dp = lax.dot_general(do2, vg, NT, preferred_element_type=F32)
            ds = (p * (dp - dl2)).astype(BF16)
            dq2 = jnp.dot(ds, kg, preferred_element_type=F32)
            dq_ref[:, h0 * 64:(h0 + 1) * 64] = dq2[:tq].astype(BF16)
            dq_ref[:, h1 * 64:(h1 + 1) * 64] = dq2[tq:].astype(BF16)
            dkt[g * 64:(g + 1) * 64, :] += lax.dot_general(q2, ds, TN, preferred_element_type=F32)
            dvt[g * 64:(g + 1) * 64, :] += lax.dot_general(do2, p.astype(BF16), TN, preferred_element_type=F32)

        @pl.when(pl.program_id(1) == nq - 1)
        def _():
            dk_ref[...] = dkt[...].T.astype(BF16)
            dv_ref[...] = dvt[...].T.astype(BF16)

    q3 = qd.reshape(bl, SEQ, 512)
    tile = pl.BlockSpec((None, tq, 256), lambda b, i: (b, i, 0))
    full = pl.BlockSpec((None, SEQ, 128), lambda b, i: (b, 0, 0))
    dq, dk, dv = pl.pallas_call(
        body, name=name, grid=(bl, nq),
        in_specs=[tile, pl.BlockSpec((None, SEQ, 128), lambda b, i: (b, 0, 2)),
                  pl.BlockSpec((None, SEQ, 128), lambda b, i: (b, 0, 3)),
                  pl.BlockSpec((None, tq, 256), lambda b, i: (b, i, 3)), tile, tile],
        out_specs=[tile, full, full],
        out_shape=[_sds((bl, SEQ, 256), BF16), _sds((bl, SEQ, 128), BF16), _sds((bl, SEQ, 128), BF16)],
        scratch_shapes=[pltpu.VMEM((128, SEQ), F32), pltpu.VMEM((128, SEQ), F32)],
        compiler_params=_cparams(PAR, ARB),
    )(q3, q3, q3, dycat.reshape(bl, SEQ, 1024), lse.reshape(bl, SEQ, 256), delta.reshape(bl, SEQ, 256))
    return dq.reshape(t, 256), dk.reshape(t, 128), dv.reshape(t, 128)


def _c_norm(cv, gam, bet):
    vg = _gelu(cv)
    mu = jnp.mean(vg, axis=-1, keepdims=True)
    xc = vg - mu
    r = lax.rsqrt(jnp.mean(xc * xc, axis=-1, keepdims=True) + EPS)
    xhat = xc * r
    return xhat * gam + bet, xhat, r


def _c_fwd(proj, gam, bet, ws, bst, *, tm, name):
    t = proj.shape[0]
    nch = tm // C_CHUNK

    def body(u_ref, v_ref, g_ref, b_ref, ws_ref, bs_ref, y_ref):
        vn, _, _ = _c_norm(v_ref[...].astype(F32), g_ref[...], b_ref[...])
        vnb = vn.astype(BF16)
        for c in range(nch):
            rows = slice(c * C_CHUNK, (c + 1) * C_CHUNK)
            for g in range(C_GROUPS):
                gs = slice(g * 64, (g + 1) * 64)
                mixed = jnp.dot(ws_ref[g], vnb[rows, gs], preferred_element_type=F32) + bs_ref[:, gs]
                y_ref[rows, gs] = (_gelu(u_ref[rows, gs].astype(F32)) * mixed).astype(BF16)

    vec = pl.BlockSpec((1, 256), lambda i: (0, 0))
    return pl.pallas_call(
        body, name=name, grid=(t // tm,),
        in_specs=[pl.BlockSpec((tm, 256), lambda i: (i, 5)), pl.BlockSpec((tm, 256), lambda i: (i, 6)), vec, vec,
                  pl.BlockSpec((C_GROUPS, C_CHUNK, C_CHUNK), lambda i: (0, 0, 0)),
                  pl.BlockSpec((C_CHUNK, 256), lambda i: (0, 0))],
        out_specs=pl.BlockSpec((tm, 256), lambda i: (i, 0)), out_shape=_sds((t, 256), BF16),
        compiler_params=_cparams(PAR),
    )(proj, proj, gam, bet, ws, bst)


def _c_bwd(proj, dycat, gam, bet, ws, wst, bst, *, tm, name):
    t = proj.shape[0]
    nch = tm // C_CHUNK
    nstep = t // tm

    def body(u_ref, v_ref, dy_ref, g_ref, b_ref, ws_ref, wst_ref, bs_ref,
             du_ref, dv_ref, dws_ref, dbs_ref, dg_ref, db_ref, dvn_s):
        step = pl.program_id(0)

        @pl.when(step == 0)
        def _():
            dws_ref[...] = jnp.zeros_like(dws_ref)
            dbs_ref[...] = jnp.zeros_like(dbs_ref)
            dg_ref[...] = jnp.zeros_like(dg_ref)
            db_ref[...] = jnp.zeros_like(db_ref)

        cv = v_ref[...].astype(F32)
        gam_v = g_ref[...]
        vn, xhat, r = _c_norm(cv, gam_v, b_ref[...])
        vnb = vn.astype(BF16)
        for c in range(nch):
            rows = slice(c * C_CHUNK, (c + 1) * C_CHUNK)
            for g in range(C_GROUPS):
                gs = slice(g * 64, (g + 1) * 64)
                cu = u_ref[rows, gs].astype(F32)
                dy = dy_ref[rows, gs]
                mixed = jnp.dot(ws_ref[g], vnb[rows, gs], preferred_element_type=F32) + bs_ref[:, gs]
                du_ref[rows, gs] = (dy * mixed * _gelu_grad(cu)).astype(BF16)
                dmix = dy * _gelu(cu)
                dbs_ref[:, gs] += dmix
                dmb = dmix.astype(BF16)
                dws_ref[g] += lax.dot_general(dmb, vnb[rows, gs], NT, preferred_element_type=F32)
                dvn_s[rows, gs] = jnp.dot(wst_ref[g], dmb, preferred_element_type=F32)
        dvn = dvn_s[...]
        dg_ref[...] += jnp.sum(dvn * xhat, axis=0, keepdims=True)
        db_ref[...] += jnp.sum(dvn, axis=0, keepdims=True)
        dxh = dvn * gam_v
        dvg = r * (dxh - jnp.mean(dxh, axis=-1, keepdims=True) - xhat * jnp.mean(dxh * xhat, axis=-1, keepdims=True))
        dv_ref[...] = (dvg * _gelu_grad(cv)).astype(BF16)

        @pl.when(step == nstep - 1)
        def _():
            dbs_ref[...] = _seg_sum(dbs_ref[...], _group_sum_matrix(256, True))

    vec = pl.BlockSpec((1, 256), lambda i: (0, 0))
    mat = pl.BlockSpec((C_GROUPS, C_CHUNK, C_CHUNK), lambda i: (0, 0, 0))
    bsp = pl.BlockSpec((C_CHUNK, 256), lambda i: (0, 0))
    tile = pl.BlockSpec((tm, 256), lambda i: (i, 0))
    return pl.pallas_call(
        body, name=name, grid=(nstep,),
        in_specs=[pl.BlockSpec((tm, 256), lambda i: (i, 5)), pl.BlockSpec((tm, 256), lambda i: (i, 6)),
                  pl.BlockSpec((tm, 256), lambda i: (i, 2)), vec, vec, mat, mat, bsp],
        out_specs=[tile, tile, mat, bsp, vec, vec],
        out_shape=[_sds((t, 256), BF16), _sds((t, 256), BF16), _sds((C_GROUPS, C_CHUNK, C_CHUNK), F32),
                   _sds((C_CHUNK, 256), F32), _sds((1, 256), F32), _sds((1, 256), F32)],
        scratch_shapes=[pltpu.VMEM((tm, 256), F32)],
        compiler_params=_cparams(ARB),
    )(proj, proj, dycat, gam, bet, ws, wst, bst)


FF_TC = 128
FF_NB = D_FF // FF_TC
FF_CH = 64
FF_HALO = 16


def _taps(ref, r0, win, where):
    z = jnp.zeros((FF_HALO, win.shape[1]), F32)
    if where == "first":
        win[0:FF_HALO, :] = z
        win[FF_HALO:, :] = ref[0:FF_CH + FF_HALO, :].astype(F32)
    elif where == "last":
        win[0:FF_CH + FF_HALO, :] = ref[SEQ - FF_CH - FF_HALO:SEQ, :].astype(F32)
        win[FF_CH + FF_HALO:, :] = z
    else:
        win[...] = ref[pl.ds(pl.multiple_of(r0 - FF_HALO, FF_HALO), FF_CH + 2 * FF_HALO), :].astype(F32)
    return tuple(win[FF_HALO + o:FF_HALO + o + FF_CH, :] for o in (-1, 0, 1))


def _chunk_loop(step):
    step(0, lambda ref, win: _taps(ref, 0, win, "first"))

    def mid(i, carry):
        r0 = pl.multiple_of(i * FF_CH, FF_CH)
        step(r0, lambda ref, win: _taps(ref, r0, win, "mid"))
        return carry

    lax.fori_loop(1, SEQ // FF_CH - 1, mid, 0)
    step(SEQ - FF_CH, lambda ref, win: _taps(ref, SEQ - FF_CH, win, "last"))


def _conv3(taps, w_ref, b_ref):
    dn, md, up = taps
    return w_ref[0:1, :] * dn + w_ref[1:2, :] * md + w_ref[2:3, :] * up + b_ref[...]


def _ff_specs(order):
    def at(fn):
        return (lambda b, j: fn(b, j)) if order == "bj" else (lambda j, b: fn(b, j))
    hs = [pl.BlockSpec((None, SEQ, FF_TC), at(lambda b, j, o=o: (b, 0, j + o))) for o in (0, FF_NB)]
    ws = [pl.BlockSpec((3, FF_TC), at(lambda b, j, o=o: (0, j + o))) for o in (0, FF_NB)]
    bs = [pl.BlockSpec((1, FF_TC), at(lambda b, j, o=o: (0, j + o))) for o in (0, FF_NB)]
    return hs, ws, bs


def _conv_gate_fwd(h, cw, cb, *, name):
    t = h.shape[0]
    bl = t // SEQ

    def body(hg_ref, hu_ref, wg_ref, wu_ref, bg_ref, bu_ref, a_ref, cg_ref, cu_ref, win):
        def step(r0, taps):
            cg = _conv3(taps(hg_ref, win.at[0]), wg_ref, bg_ref)
            cu = _conv3(taps(hu_ref, win.at[1]), wu_ref, bu_ref)
            a_ref[pl.ds(r0, FF_CH), :] = (cg * _sigmoid(cg) * cu).astype(BF16)
            cg_ref[pl.ds(r0, FF_CH), :] = cg.astype(BF16)
            cu_ref[pl.ds(r0, FF_CH), :] = cu.astype(BF16)

        _chunk_loop(step)

    hs, ws, bs = _ff_specs("bj")
    h3 = h.reshape(bl, SEQ, 2 * D_FF)
    half = pl.BlockSpec((None, SEQ, FF_TC), lambda b, j: (b, 0, j))
    outs = pl.pallas_call(
        body, name=name, grid=(bl, FF_NB), in_specs=hs + ws + bs, out_specs=[half] * 3,
        out_shape=[_sds((bl, SEQ, D_FF), BF16)] * 3,
        scratch_shapes=[pltpu.VMEM((2, FF_CH + 2 * FF_HALO, FF_TC), F32)],
        compiler_params=_cparams(PAR, PAR),
    )(h3, h3, cw, cw, cb, cb)
    return [o.reshape(t, D_FF) for o in outs]


def _conv_gate_bwd(h, cg_all, cu_all, dact, cw, cb, *, name):
    t = h.shape[0]
    bl = t // SEQ

    def body(hg_ref, hu_ref, wg_ref, wu_ref, bg_ref, bu_ref, da_ref, cg_ref, cu_ref,
             dhg_ref, dhu_ref, dwg_ref, dwu_ref, dbg_ref, dbu_ref, dg_s, du_s, win, sums):
        @pl.when(pl.program_id(1) == 0)
        def _():
            for ref in (dwg_ref, dwu_ref, dbg_ref, dbu_ref):
                ref[...] = jnp.zeros_like(ref)

        sums[...] = jnp.zeros_like(sums)
        red = lambda x: jnp.sum(x.reshape(FF_CH // 8, 8, x.shape[1]), axis=0)

        def pass1(r0, taps):
            tg, tu = taps(hg_ref, win.at[0]), taps(hu_ref, win.at[1])
            cg = cg_ref[pl.ds(r0, FF_CH), :].astype(F32)
            cu = cu_ref[pl.ds(r0, FF_CH), :].astype(F32)
            da = da_ref[pl.ds(r0, FF_CH), :].astype(F32)
            sg = _sigmoid(cg)
            dcg = da * cu * (sg * (1.0 + cg * (1.0 - sg)))
            dcu = da * (cg * sg)
            dg_s[pl.ds(r0, FF_CH), :] = dcg
            du_s[pl.ds(r0, FF_CH), :] = dcu
            for half, (d, tp) in enumerate(((dcg, tg), (dcu, tu))):
                for k in range(3):
                    sums[4 * half + k] += red(d * tp[k])
                sums[4 * half + 3] += red(d)

        _chunk_loop(pass1)
        for half, (dw_ref, db_ref) in enumerate(((dwg_ref, dbg_ref), (dwu_ref, dbu_ref))):
            for k in range(3):
                dw_ref[k:k + 1, :] += jnp.sum(sums[4 * half + k], axis=0, keepdims=True)
            db_ref[...] += jnp.sum(sums[4 * half + 3], axis=0, keepdims=True)

        def pass2(r0, taps):
            for k, (s, w_ref, o_ref) in enumerate(((dg_s, wg_ref, dhg_ref), (du_s, wu_ref, dhu_ref))):
                dn, md, up = taps(s, win.at[k])
                o_ref[pl.ds(r0, FF_CH), :] = (w_ref[0:1, :] * up + w_ref[1:2, :] * md + w_ref[2:3, :] * dn).astype(BF16)

        _chunk_loop(pass2)

    hs, ws, bs = _ff_specs("jb")
    half = pl.BlockSpec((None, SEQ, FF_TC), lambda j, b: (b, 0, j))
    wsp = pl.BlockSpec((3, FF_TC), lambda j, b: (0, j))
    bsp = pl.BlockSpec((1, FF_TC), lambda j, b: (0, j))
    h3 = h.reshape(bl, SEQ, 2 * D_FF)
    dhg, dhu, dwg, dwu, dbg, dbu = pl.pallas_call(
        body, name=name, grid=(FF_NB, bl), in_specs=hs + ws + bs + [half] * 3,
        out_specs=[half, half, wsp, wsp, bsp, bsp],
        out_shape=[_sds((bl, SEQ, D_FF), BF16), _sds((bl, SEQ, D_FF), BF16), _sds((3, D_FF), F32), _sds((3, D_FF), F32),
                   _sds((1, D_FF), F32), _sds((1, D_FF), F32)],
        scratch_shapes=[pltpu.VMEM((SEQ, FF_TC), F32), pltpu.VMEM((SEQ, FF_TC), F32),
                        pltpu.VMEM((2, FF_CH + 2 * FF_HALO, FF_TC), F32), pltpu.VMEM((8, 8, FF_TC), F32)],
        compiler_params=_cparams(PAR, ARB),
    )(h3, h3, cw, cw, cb, cb, *[a.reshape(bl, SEQ, D_FF) for a in (dact, cg_all, cu_all)])
    return (dhg.reshape(t, D_FF), dhu.reshape(t, D_FF), jnp.concatenate([dwg, dwu], axis=1),
            jnp.concatenate([dbg, dbu], axis=1))


def _ple_fwd(x2, gain, wg, pe, pe_blk, wp, *, tm, name):
    t, k = x2.shape

    def body(x_ref, g_ref, wg_ref, pe_ref, wp_ref, hn_ref, x3_ref, gt_ref, pp_ref):
        x = x_ref[...]
        r = lax.rsqrt(jnp.mean(x * x, axis=-1, keepdims=True) + EPS)
        hn = (x * r * g_ref[...]).astype(BF16)
        hn_ref[...] = hn
        gate = _sigmoid(jnp.dot(hn, wg_ref[...], preferred_element_type=F32))
        pp = jnp.dot(pe_ref[...].astype(BF16), wp_ref[...], preferred_element_type=F32)
        gt_ref[...] = gate.astype(BF16)
        pp_ref[...] = pp.astype(BF16)
        x3_ref[...] = x + pp * gate

    row = pl.BlockSpec((tm, k), lambda i: (i, 0))
    return pl.pallas_call(
        body, name=name, grid=(t // tm,),
        in_specs=[row, pl.BlockSpec((1, k), lambda i: (0, 0)), pl.BlockSpec((k, k), lambda i: (0, 0)),
                  pl.BlockSpec((tm, PLE_DIM), lambda i: (pe_blk + i, 0)), pl.BlockSpec((PLE_DIM, k), lambda i: (0, 0))],
        out_specs=[row, row, row, row],
        out_shape=[_sds((t, k), BF16), _sds((t, k), F32), _sds((t, k), BF16), _sds((t, k), BF16)],
        compiler_params=_cparams(PAR),
    )(x2, gain, wg, pe, wp)


def _ple_bwd_ew(dx3, gate, pp, *, tm, name):
    t, n = dx3.shape

    def body(d_ref, g_ref, p_ref, dz_ref, dpp_ref):
        d, g = d_ref[...], g_ref[...]
        dz_ref[...] = (d * p_ref[...] * g * (1.0 - g)).astype(BF16)
        dpp_ref[...] = (d * g).astype(BF16)

    spec = pl.BlockSpec((tm, n), lambda i: (i, 0))
    return pl.pallas_call(
        body, name=name, grid=(t // tm,), in_specs=[spec] * 3, out_specs=[spec] * 2,
        out_shape=[_sds((t, n), BF16)] * 2, compiler_params=_cparams(PAR),
    )(dx3, gate, pp)


def _loss_head(y, tgt, *, tm, name):
    t, d = y.shape

    def body(y_ref, t_ref, l_ref, dy_ref):
        @pl.when(pl.program_id(0) == 0)
        def _():
            l_ref[...] = jnp.zeros_like(l_ref)

        e = y_ref[...] - t_ref[...]
        dy_ref[...] = e * (1.0 / d)
        s = jnp.sum(jnp.sum(e * e, axis=1, keepdims=True), axis=0, keepdims=True)
        l_ref[...] += jnp.broadcast_to(s * (0.5 / d), (8, 128))

    spec = pl.BlockSpec((tm, d), lambda i: (i, 0))
    return pl.pallas_call(
        body, name=name, grid=(t // tm,), in_specs=[spec, spec],
        out_specs=[pl.BlockSpec((8, 128), lambda i: (0, 0)), spec],
        out_shape=[_sds((8, 128), F32), _sds((t, d), F32)], compiler_params=_cparams(ARB),
    )(y, tgt)


BIAS_PC = 8192


def _onehot(bucket_row):
    rows = lax.broadcasted_iota(jnp.int32, (REL_BUCKETS, bucket_row.shape[1]), 0)
    return (rows == bucket_row).astype(BF16)


def _dot3(x, onehot, dims):
    acc = None
    for _ in range(3):
        term = x.astype(BF16)
        part = lax.dot_general(term, onehot, dims, preferred_element_type=F32)
        acc = part if acc is None else acc + part
        x = x - term.astype(F32)
    return acc


def _bias_lookup(table_t, bucket, *, name):
    h = table_t.shape[0]
    p = bucket.shape[1]

    def body(t_ref, b_ref, o_ref):
        bk = b_ref[...]
        val = _dot3(t_ref[...], _onehot(bk), (((1,), (0,)), ((), ())))
        o_ref[...] = jnp.where(bk >= 0, val, NEG_INF)

    return pl.pallas_call(
        body, name=name, grid=(p // BIAS_PC,),
        in_specs=[pl.BlockSpec((h, REL_BUCKETS), lambda i: (0, 0)), pl.BlockSpec((1, BIAS_PC), lambda i: (0, i))],
        out_specs=pl.BlockSpec((h, BIAS_PC), lambda i: (0, i)), out_shape=_sds((h, p), F32),
        compiler_params=_cparams(PAR),
    )(table_t, bucket)


def _bucket_reduce(dbiases, bucket, *, name):
    h, p = dbiases[0].shape
    nl = len(dbiases)

    def body(*refs):
        b_ref, o_ref = refs[nl], refs[nl + 1]

        @pl.when(pl.program_id(0) == 0)
        def _():
            o_ref[...] = jnp.zeros_like(o_ref)

        d = refs[0][...]
        for d_ref in refs[1:nl]:
            d = d + d_ref[...]
        o_ref[...] += _dot3(d, _onehot(b_ref[...]), NT)

    return pl.pallas_call(
        body, name=name, grid=(p // BIAS_PC,),
        in_specs=[pl.BlockSpec((h, BIAS_PC), lambda i: (0, i))] * nl + [pl.BlockSpec((1, BIAS_PC), lambda i: (0, i))],
        out_specs=pl.BlockSpec((h, REL_BUCKETS), lambda i: (0, 0)), out_shape=_sds((h, REL_BUCKETS), F32),
        compiler_params=_cparams(ARB),
    )(*dbiases, bucket)


def _adamw_math(w, g, m, v):
    m = ADAM_B1 * m + (1.0 - ADAM_B1) * g
    v = ADAM_B2 * v + (1.0 - ADAM_B2) * (g * g)
    m_hat = m / (1.0 - ADAM_B1 ** ADAM_STEP)
    v_hat = v / (1.0 - ADAM_B2 ** ADAM_STEP)
    delta = -ADAM_LR * (m_hat / (jnp.sqrt(v_hat) + ADAM_EPS) + ADAM_WD * w)
    return delta, m, v


def _adamw_reduce(parts, w, m, v, *, tr, name):
    nl = len(parts)
    rows, c = w.shape
    r = rows // nl
    nt = r // tr

    def body(*refs):
        p_refs = refs[:nl]
        w_ref, m_ref, v_ref, g_ref, d_ref, nm_ref, nv_ref = refs[nl:]
        for li, p_ref in enumerate(p_refs):
            @pl.when(pl.program_id(0) == li)
            def _(p_ref=p_ref):
                g = p_ref[0].astype(F32)
                for k in range(1, N_DEV):
                    g = g + p_ref[k].astype(F32)
                d, nm, nv = _adamw_math(w_ref[...], g, m_ref[...], v_ref[...])
                g_ref[...] = g
                d_ref[...] = d
                nm_ref[...] = nm
                nv_ref[...] = nv

    def part_map(li):
        return lambda l, i: (0, jnp.where(l == li, i, jnp.where(l < li, 0, nt - 1)), 0)

    spec = pl.BlockSpec((tr, c), lambda l, i: (l * nt + i, 0))
    return pl.pallas_call(
        body, name=name, grid=(nl, nt),
        in_specs=[pl.BlockSpec((N_DEV, tr, c), part_map(li)) for li in range(nl)] + [spec, spec, spec],
        out_specs=[spec] * 4, out_shape=[_sds((rows, c), F32)] * 4, compiler_params=_cparams(ARB, ARB),
    )(*parts, w, m, v)


def _adamw_plain(g, w, m, v, *, name):
    def body(g_ref, w_ref, m_ref, v_ref, d_ref, nm_ref, nv_ref):
        d, nm, nv = _adamw_math(w_ref[...], g_ref[...], m_ref[...], v_ref[...])
        d_ref[...] = d
        nm_ref[...] = nm
        nv_ref[...] = nv

    return pl.pallas_call(body, name=name, out_shape=[_sds(w.shape, F32)] * 3)(g, w, m, v)


def _mesh_pos():
    return lax.axis_index("x"), lax.axis_index("y"), lax.axis_index("c")


def _allgather_body(x_refs, out_refs, send_sems, recv_sems, local_sems, slot):
    x, y, c = _mesh_pos()
    me, sibling = (x, y, c), (x, y, 1 - c)
    chips = [(1 - x, y), (x, 1 - y), (1 - x, 1 - y)]
    waits = []
    for a, (x_ref, out_ref) in enumerate(zip(x_refs, out_refs)):
        def copy(k, block, to, src=None, out_ref=out_ref, a=a):
            return pltpu.make_async_remote_copy(
                src_ref=slot(out_ref, block) if src is None else src, dst_ref=slot(out_ref, block),
                send_sem=send_sems.at[a, k], recv_sem=recv_sems.at[a, k], device_id=to, device_id_type=MESH)

        mine = pltpu.make_async_copy(x_ref, slot(out_ref, me), local_sems.at[a])
        mine.start()
        first = [copy(0, me, sibling, src=x_ref)]
        first += [copy(1 + j, me, (*chip, c), src=x_ref) for j, chip in enumerate(chips)]
        for cp in first:
            cp.start()
        waits.append((copy, mine, first))
    sends = []
    for copy, mine, first in waits:
        passed = [copy(4 + j, (*chip, c), sibling) for j, chip in enumerate(chips)]
        for j, chip in enumerate(chips):
            copy(1 + j, (*chip, c), me).wait_recv()
            passed[j].start()
        sends.append(passed)
    for (copy, mine, first), passed in zip(waits, sends):
        copy(0, sibling, me).wait_recv()
        for j, chip in enumerate(chips):
            copy(4 + j, (*chip, 1 - c), me).wait_recv()
        for cp in first + passed:
            cp.wait_send()
        mine.wait()


PEER_FLIPS = ((0, 0, 1), (1, 0, 0), (0, 1, 0), (1, 1, 0), (1, 0, 1), (0, 1, 1), (1, 1, 1))


def _peer_copies(x_refs, land_refs, send_sem, recv_sem, scatter):
    x, y, c = _mesh_pos()
    me = 4 * x + 2 * y + c
    copies = []
    for x_ref, land_ref in zip(x_refs, land_refs):
        for fx, fy, fc in PEER_FLIPS:
            px, py, pc = x ^ fx, y ^ fy, c ^ fc
            src = x_ref.at[4 * px + 2 * py + pc] if scatter else x_ref
            copies.append(pltpu.make_async_remote_copy(
                src_ref=src, dst_ref=land_ref.at[me], send_sem=send_sem, recv_sem=recv_sem,
                device_id=(px, py, pc), device_id_type=MESH))
    return copies


def _sc_exchange(xs, *, scatter, collective_id, name):
    na = len(xs)
    land_shapes = [x.shape if scatter else (N_DEV,) + x.shape for x in xs]

    def body(*refs):
        x_refs, land_refs = refs[:na], refs[na:2 * na]
        send_sem, recv_sem, local_sem = refs[2 * na:]
        x, y, c = _mesh_pos()
        me = 4 * x + 2 * y + c
        barrier = pltpu.get_barrier_semaphore()
        for fx, fy, fc in PEER_FLIPS:
            pl.semaphore_signal(barrier, inc=1, device_id=(x ^ fx, y ^ fy, c ^ fc), device_id_type=MESH)
        pl.semaphore_wait(barrier, len(PEER_FLIPS))
        for x_ref, land_ref in zip(x_refs, land_refs):
            own = pltpu.make_async_copy(x_ref.at[me] if scatter else x_ref, land_ref.at[me], local_sem)
            own.start()
            own.wait()
        copies = _peer_copies(x_refs, land_refs, send_sem, recv_sem, scatter)
        for cp in copies:
            cp.start()
        for cp in copies:
            cp.wait()

    return pl.kernel(
        body, name=name, out_type=[_sds(s, x.dtype) for s, x in zip(land_shapes, xs)],
        mesh=plsc.ScalarSubcoreMesh(axis_name="sequencer", num_cores=1),
        scratch_types=[pltpu.SemaphoreType.DMA, pltpu.SemaphoreType.DMA, pltpu.SemaphoreType.DMA],
        compiler_params=pltpu.CompilerParams(collective_id=collective_id),
    )(*xs)


def _sc_allgather(xs, *, collective_id, name):
    na = len(xs)

    def body(*refs):
        x_refs, out_refs = refs[:na], refs[na:2 * na]
        send_sems, recv_sems, local_sems = refs[2 * na:]
        x, y, c = _mesh_pos()
        barrier = pltpu.get_barrier_semaphore()
        for fx, fy, fc in PEER_FLIPS:
            pl.semaphore_signal(barrier, inc=1, device_id=(x ^ fx, y ^ fy, c ^ fc), device_id_type=MESH)
        pl.semaphore_wait(barrier, len(PEER_FLIPS))
        _allgather_body(x_refs, out_refs, send_sems, recv_sems, local_sems,
                        lambda ref, pos: ref.at[4 * pos[0] + 2 * pos[1] + pos[2]])

    return pl.kernel(
        body, name=name, out_type=[_sds((N_DEV,) + x.shape, x.dtype) for x in xs],
        mesh=plsc.ScalarSubcoreMesh(axis_name="sequencer", num_cores=1),
        scratch_types=[pltpu.SemaphoreType.DMA((na, 7)), pltpu.SemaphoreType.DMA((na, 7)),
                       pltpu.SemaphoreType.DMA((na,))],
        compiler_params=pltpu.CompilerParams(collective_id=collective_id),
    )(*xs)


def _allgather_vmem(x, *, name):
    r, c = x.shape

    def body(x_ref, out_ref, send_sems, recv_sems, local_sems):
        _allgather_body([x_ref], [out_ref], send_sems, recv_sems, local_sems,
                        lambda ref, pos: ref.at[pl.ds((4 * pos[0] + 2 * pos[1] + pos[2]) * r, r), :])

    vm = pl.BlockSpec(memory_space=pltpu.VMEM)
    return pl.pallas_call(
        body, name=name, in_specs=[vm], out_specs=vm, out_shape=_sds((N_DEV * r, c), x.dtype),
        scratch_shapes=[pltpu.SemaphoreType.DMA((1, 7)), pltpu.SemaphoreType.DMA((1, 7)),
                        pltpu.SemaphoreType.DMA((1,))],
    )(x)


def _sum_slots(gathered, *, name):
    _, r, c = gathered.shape

    def body(g_ref, o_ref):
        acc = g_ref[0]
        for k in range(1, N_DEV):
            acc = acc + g_ref[k]
        o_ref[...] = acc

    return pl.pallas_call(body, name=name, out_shape=_sds((r, c), gathered.dtype))(gathered)


def _t5_bucket(rel):
    nb = REL_BUCKETS // 2
    ret = jnp.where(rel > 0, nb, 0)
    n = jnp.abs(rel)
    max_exact = nb // 2
    nf = jnp.maximum(n, 1).astype(F32)
    large = max_exact + (jnp.log(nf / max_exact) / math.log(REL_MAX_DIST / max_exact)
                         * (nb - max_exact)).astype(jnp.int32)
    large = jnp.minimum(large, nb - 1)
    return ret + jnp.where(n < max_exact, n, large)


def _band_pattern(block, radius, dil):
    kw = block + 2 * radius
    rel = jnp.arange(kw)[None, :] - radius - jnp.arange(block)[:, None]
    return jnp.where(jnp.abs(rel) <= radius, _t5_bucket(rel * dil), -1).astype(jnp.int32).reshape(1, block * kw)


def _rope_tables():
    lane = np.arange(64)
    seg, j = lane // 32, lane % 32
    inv = ROPE_THETA ** (-jnp.arange(0, 32, 2, dtype=F32) / 32)
    tpos = jnp.arange(SEQ)
    pos = jnp.where(jnp.asarray(seg)[None, :] == 0, (tpos // GRID_W)[:, None], (tpos % GRID_W)[:, None])
    ang = pos.astype(F32) * inv[jnp.asarray(j % 16)][None, :]
    cos = jnp.cos(ang)
    sins = jnp.where(jnp.asarray(j)[None, :] < 16, -jnp.sin(ang), jnp.sin(ang))
    return jnp.tile(cos, (1, 4)), jnp.tile(sins, (1, 4))


A_Q, A_K, A_V = (256, 0), (256, 1), (256, 2)
B_Q, B_K, B_V = (256, 0), (128, 2), (128, 3)
A_HEADS = dict(rad=A_RADIUS, nh=4, nkv=4)
B_HEADS = dict(rad=SWA_RADIUS, nh=4, nkv=2)


def _local_step(x, pe, tgt, rel_bias, wts, matmul_weights, grads_ready):
    t = x.shape[0]
    bl = t // SEQ
    cos, sins = _rope_tables()
    blocks_a = [min(BAND_BLOCK, SEQ // d) for d in DILATIONS]
    pats_a = [_band_pattern(blk, A_RADIUS, d) for blk, d in zip(blocks_a, DILATIONS)]
    pat_b = _band_pattern(BAND_BLOCK, SWA_RADIUS, 1)
    table_t = rel_bias.T
    bias_a = [_bias_lookup(table_t[:4], pt, name=f"bias_a{ci}").reshape(4, blk, blk + 2 * A_RADIUS)
              for ci, (pt, blk) in enumerate(zip(pats_a, blocks_a))]
    bias_b = _bias_lookup(table_t[4:], pat_b, name="bias_b").reshape(4, BAND_BLOCK, BAND_BLOCK + 2 * SWA_RADIUS)
    nat4 = lambda a: a.reshape(bl, 1, SEQ, a.shape[-1])

    saved = []
    for li in range(DEPTH):
        w = dict(wts[li])
        w.update(matmul_weights(li, "in", x))
        hn0, proj = _norm_mm((x,), w["g_mix"], w["w_in"], None, tm=1024, tn=1152, name="mix_in_fwd", out_dtype=BF16)
        qa1, qa4, qa16, qb, qd = _qkprep_fwd(proj, w["qk_gains"], cos, sins, tm=512, name="qkprep_fwd")
        qa = (nat4(qa1), qa4, qa16)
        oa, la = [], []
        for ci in range(3):
            o, l = _band_fwd(qa[ci], A_Q, A_K, A_V, bias_a[ci], None, name=f"band_a{ci}_fwd", **A_HEADS)
            oa.append(o)
            la.append(l)
        oa[0], la[0] = oa[0].reshape(t, 256), la[0].reshape(t, 256)
        ya, lse_a = _combine_a(oa, la, tm=512, name="combine_a")
        yb, lse_b = _band_fwd(nat4(qb), B_Q, B_K, B_V, bias_b, w["sink_t"], name="band_b_fwd", **B_HEADS)
        yb = yb.reshape(t, 256)
        yc = _c_fwd(proj, w["c_g"], w["c_b"], w["c_ws"], w["c_bst"], tm=512, name="c_fwd")
        yd, lse_d = _dense_fwd(qd, tq=256, name="dense_fwd")
        w.update(matmul_weights(li, "rest", yd))
        mixed, x1 = _norm_mm((ya, yb, yc, yd), w["out_gain"], w["w_out"], x, tm=1024, tn=1024, name="mix_out_fwd")
        hn1, h = _norm_mm((x1,), w["g_ffn"], w["w_up"], None, tm=1024, tn=1408, name="ffn_up_fwd", out_dtype=BF16)
        act, cg, cu = _conv_gate_fwd(h, w["conv_w"], w["conv_b"], name="conv_gate_fwd")
        x2 = _mm(act, w["w_down"], "nn", x1, tm=1024, tn=1024, out_dtype=F32, name="ffn_down_fwd")
        hn2, x3, gate, pp = _ple_fwd(x2, w["g_ple"], w["w_gate"], pe, li * (t // 1024), w["w_proj"], tm=1024,
                                     name="ple_fwd")
        saved.append(dict(w=w, x0=x, hn0=hn0, proj=proj, qa=qa, qb=qb, qd=qd, ya=ya, lse_a=lse_a, yb=yb, lse_b=lse_b,
                          yc=yc, yd=yd, lse_d=lse_d, mixed=mixed, x1=x1, hn1=hn1, h=h, cg=cg, cu=cu, act=act, x2=x2, hn2=hn2,
                          gate=gate, pp=pp))
        x = x3

    loss_tile, dx = _loss_head(x, tgt, tm=512, name="loss_head")
    grads = [None] * DEPTH
    dbias_a, dbias_bs = [[], [], []], []
    for li in reversed(range(DEPTH)):
        s = saved[li]
        w = s["w"]
        g = {}
        dz, dpp = _ple_bwd_ew(dx, s["gate"], s["pp"], tm=512, name="ple_bwd_ew")
        g["w_gate"] = _mm(s["hn2"], dz, "tn", None, tm=1024, tn=512, out_dtype=BF16, name="dw_gate")
        g["w_proj"] = _mm(pe, dpp, "tn", None, tm=256, tn=1024, out_dtype=BF16, name="dw_proj", a_rows=(li, t))
        dx2, dx2b, g["g_ple"] = _mm_bt_normbwd((dz,), w["w_gate"], (s["x2"],), w["g_ple"], dx, tm=1024, tn=1024,
                                               name="ple_bwd", emit_bf16=True)
        g["w_down"] = _mm(s["act"], dx2b, "tn", None, tm=1408, tn=512, out_dtype=BF16, name="dw_down")
        dact = _mm(dx2b, w["w_down"], "nt", None, tm=1024, tn=1408, out_dtype=BF16, name="ffn_down_bwd")
        dhg, dhu, g["conv_w"], g["conv_b"] = _conv_gate_bwd(s["h"], s["cg"], s["cu"], dact, w["conv_w"], w["conv_b"],
                                                            name="conv_gate_bwd")
        g["w_up"] = tuple(_mm(s["hn1"], dhalf, "tn", None, tm=1024, tn=1408, out_dtype=BF16, name=f"dw_up_{nm}")
                          for nm, dhalf in (("gate", dhg), ("up", dhu)))
        dx1, dx1b, g["g_ffn"] = _mm_bt_normbwd((dhg, dhu), w["w_up"], (s["x1"],), w["g_ffn"], dx2, tm=1024, tn=1408,
                                               name="ffn_up_bwd", emit_bf16=True)
        g["w_out"] = _mm(s["mixed"], dx1b, "tn", None, tm=1024, tn=512, out_dtype=BF16, name="dw_out")
        grads_ready(li, "mid", g)
        dycat, g["out_gain"] = _mm_bt_normbwd((dx1b,), w["w_out"], (s["ya"], s["yb"], s["yc"], s["yd"]), w["out_gain"],
                                              None, tm=1024, tn=1024, name="mix_out_bwd")
        dy_r, lse_r, dl_a, dl_b, dl_d = _deltas(dycat, s["ya"], s["yb"], s["yd"], s["lse_a"], tm=512, name="deltas")
        dy_a = (nat4(dycat),) + tuple(dy_r)
        lse_a = (nat4(s["lse_a"]),) + tuple(lse_r)
        dl_a = (nat4(dl_a[0]),) + tuple(dl_a[1:])
        da = []
        for ci in range(3):
            dq, dk, dv, dbias = _band_bwd(s["qa"][ci], A_Q, A_K, A_V, bias_a[ci], None, dy_a[ci], 0, lse_a[ci],
                                          dl_a[ci], name=f"band_a{ci}_bwd", **A_HEADS)
            if ci == 0:
                dq, dk, dv = (a.reshape(t, 256) for a in (dq, dk, dv))
            da.append((dq, dk, dv))
            dbias_a[ci].append(dbias.reshape(4, -1))
        dqb, dkb, dvb, dbias_b, dsink = _band_bwd(nat4(s["qb"]), B_Q, B_K, B_V, bias_b, w["sink_t"], nat4(dycat), 1,
                                                  nat4(s["lse_b"]), nat4(dl_b), name="band_b_bwd", **B_HEADS)
        dbias_bs.append(dbias_b.reshape(4, -1))
        g["sink"] = dsink[:, 0, 0]
        dd = _dense_bwd(s["qd"], dycat, s["lse_d"], dl_d, tq=128, name="dense_bwd")
        dcu, dcv, g["c_ws"], dbs, g["c_g"], g["c_b"] = _c_bwd(s["proj"], dycat, w["c_g"], w["c_b"], w["c_ws"],
                                                               w["c_wst"], w["c_bst"], tm=512, name="c_bwd")
        g["c_bs"] = dbs[:, ::64].T
        db = (dqb.reshape(t, 256), dkb.reshape(t, 128), dvb.reshape(t, 128))
        dproj, dgains = _qkprep_bwd(s["proj"], da, db, dd, dcu, dcv, w["qk_gains"], cos, sins, tm=512, name="qkprep_bwd")
        g["qk_gain"] = dgains[:6, :64].reshape(3, 2, HEAD_DIM)
        g["w_in"] = _mm(s["hn0"], dproj, "tn", None, tm=1024, tn=1152, out_dtype=BF16, name="dw_in")
        dx, g["g_mix"] = _mm_bt_normbwd((dproj,), w["w_in"], (s["x0"],), w["g_mix"], dx1, tm=1024, tn=1152,
                                        name="mix_in_bwd")
        grads[li] = g
        grads_ready(li, "end", g)
    d_table_a = sum(_bucket_reduce(dbias_a[ci], pats_a[ci], name=f"bucket_a{ci}") for ci in range(3))
    d_table_b = _bucket_reduce(dbias_bs, pat_b, name="bucket_b")
    d_rel_bias = jnp.concatenate([d_table_a, d_table_b], axis=0).T
    return loss_tile[0, 0], dx, grads, d_rel_bias


WEIGHT_NAMES = ("rel_bias", "ln_mix_g", "w_in", "qk_gain", "sink", "c_norm_g", "c_norm_b", "c_ws", "c_bs", "out_gain",
                "w_out", "ln_ffn_g", "w_up", "conv_w", "conv_b", "w_down", "ln_ple_g", "w_ple_gate", "w_ple_proj")
COL_SHARDED = ("w_in", "w_up", "w_ple_proj")
ROW_SHARDED = ("w_out", "w_down", "w_ple_gate")
SMALL_SHARDED = ("conv_w", "out_gain")
REPLICATED = tuple(n for n in WEIGHT_NAMES if n not in COL_SHARDED + ROW_SHARDED + SMALL_SHARDED)
LOCAL_GRAD_KEY = {"ln_mix_g": "g_mix", "ln_ffn_g": "g_ffn", "ln_ple_g": "g_ple", "c_norm_g": "c_g", "c_norm_b": "c_b",
                  "w_ple_gate": "w_gate", "w_ple_proj": "w_proj"}


def _full_from_gathered(name, gathered):
    _, r, c = gathered.shape
    if name in ROW_SHARDED:
        return gathered.reshape(N_DEV * r, c)
    return jnp.transpose(gathered, (1, 0, 2)).reshape(r, N_DEV * c)


def _slots_from_full(name, full):
    if isinstance(full, tuple):
        per = N_DEV // len(full)
        return jnp.concatenate([jnp.transpose(part.reshape(part.shape[0], per, part.shape[1] // per), (1, 0, 2))
                                for part in full], axis=0)
    rows, cols = full.shape
    if name in ROW_SHARDED:
        return full.reshape(N_DEV, rows // N_DEV, cols)
    return jnp.transpose(full.reshape(rows, N_DEV, cols // N_DEV), (1, 0, 2))


def _piece_rows(shape):
    return -(-int(np.prod(shape)) // 1024) * 8


def _pack_rows(arrays):
    pieces = []
    for a in arrays:
        n, rows = int(np.prod(a.shape)), _piece_rows(a.shape)
        flat = a.astype(F32).reshape(-1)
        if n != rows * LANES:
            flat = jnp.pad(flat, (0, rows * LANES - n))
        pieces.append(flat.reshape(rows, LANES))
    return jnp.concatenate(pieces, axis=0)


def _unpack_rows(packed, shapes):
    out, off = [], 0
    for shp in shapes:
        n, rows = int(np.prod(shp)), _piece_rows(shp)
        piece = packed[off:off + rows]
        out.append((piece if n == rows * LANES else piece.reshape(-1)[:n]).reshape(shp))
        off += rows
    return out


def kernel(x, p, rel_bias, ln_mix_g, w_in, qk_gain, sink, c_norm_g, c_norm_b, c_ws, c_bs, out_gain, w_out, ln_ffn_g, w_up, conv_w, conv_b, w_down, ln_ple_g, w_ple_gate, w_ple_proj, loss_target, m_rel_bias, m_ln_mix_g, m_w_in, m_qk_gain, m_sink, m_c_norm_g, m_c_norm_b, m_c_ws, m_c_bs, m_out_gain, m_w_out, m_ln_ffn_g, m_w_up, m_conv_w, m_conv_b, m_w_down, m_ln_ple_g, m_w_ple_gate, m_w_ple_proj, v_rel_bias, v_ln_mix_g, v_w_in, v_qk_gain, v_sink, v_c_norm_g, v_c_norm_b, v_c_ws, v_c_bs, v_out_gain, v_w_out, v_ln_ffn_g, v_w_up, v_conv_w, v_conv_b, v_w_down, v_ln_ple_g, v_w_ple_gate, v_w_ple_proj):
    env = dict(locals())
    wt = {n: env[n] for n in WEIGHT_NAMES}
    mom_m = {n: env["m_" + n] for n in WEIGHT_NAMES}
    mom_v = {n: env["v_" + n] for n in WEIGHT_NAMES}
    bl = x.shape[0]
    t = bl * SEQ
    me = 4 * lax.axis_index("x") + 2 * lax.axis_index("y") + lax.axis_index("c")

    big = COL_SHARDED + ROW_SHARDED
    full = {}
    small_shapes = [wt[n].shape for n in SMALL_SHARDED]
    small = _allgather_vmem(_pack_rows([wt[n] for n in SMALL_SHARDED]), name="gather_small")
    small = small.reshape(N_DEV, -1)
    off = 0
    for n, shp in zip(SMALL_SHARDED, small_shapes):
        cnt = int(np.prod(shp))
        g = small[:, off:off + cnt].reshape((N_DEV,) + tuple(shp))
        full[n] = jnp.transpose(g, (1, 2, 0, 3)).reshape(shp[0], shp[1], N_DEV * shp[2])
        off += _piece_rows(shp) * LANES

    def head_gain(li, a, b, reps):
        g = jnp.tile(qk_gain[li, a, b], reps)
        return jnp.pad(g, (0, 256 - g.shape[0]))

    wts = []
    for li in range(DEPTH):
        rows = [head_gain(li, 0, 0, 4), head_gain(li, 0, 1, 4), head_gain(li, 1, 0, 4), head_gain(li, 1, 1, 2),
                head_gain(li, 2, 0, 4), head_gain(li, 2, 1, 2), jnp.zeros((256,), F32), jnp.zeros((256,), F32)]
        wts.append(dict(
            g_mix=ln_mix_g[li].reshape(1, -1), qk_gains=jnp.stack(rows),
            sink_t=jnp.broadcast_to(sink[li][:, None, None], (4, 8, 128)),
            c_g=c_norm_g[li].reshape(1, -1), c_b=c_norm_b[li].reshape(1, -1), c_ws=c_ws[li].astype(BF16),
            c_wst=jnp.transpose(c_ws[li], (0, 2, 1)).astype(BF16), c_bst=jnp.repeat(c_bs[li].T, 64, axis=1),
            out_gain=full["out_gain"][li].reshape(1, -1), g_ffn=ln_ffn_g[li].reshape(1, -1),
            conv_w=full["conv_w"][li], conv_b=conv_b[li].reshape(1, -1), g_ple=ln_ple_g[li].reshape(1, -1)))

    local_key = {"w_ple_gate": "w_gate", "w_ple_proj": "w_proj"}

    gather_names = {"in": ("w_in",), "rest": tuple(n for n in big if n != "w_in")}
    gathered = {}
    for cid, (li, names) in enumerate(((0, gather_names["in"]), (0, gather_names["rest"]), (1, big))):
        lands = _sc_allgather([wt[n][li].astype(BF16) for n in names], collective_id=cid,
                              name=f"gather_{li}_{len(names)}")
        gathered.setdefault(li, {}).update(zip(names, lands))

    def matmul_weights(li, part, after):
        out = {}
        for n in gather_names[part]:
            g, _ = lax.optimization_barrier((gathered[li][n], after))
            out[local_key.get(n, n)] = _full_from_gathered(n, g)
        return out

    mid_names = ("w_ple_gate", "w_ple_proj", "w_down", "w_up", "w_out")
    end_names = ("w_in",)
    landed = {}

    def start_exchange(li, names, g, tag, cid):
        slots = [_slots_from_full(n, g[local_key.get(n, n)]) for n in names]
        lands = _sc_exchange(slots, scatter=True, collective_id=cid, name=f"grads_{li}_{tag}")
        landed.update({(n, li): land for n, land in zip(names, lands)})

    def grads_ready(li, stage, g):
        if li == 0:
            start_exchange(li, mid_names if stage == "mid" else end_names, g, stage, 5 if stage == "mid" else 6)
        elif stage == "end":
            start_exchange(li, mid_names + end_names, g, stage, 4)

    loss_part, dx, grads, d_rel_bias = _local_step(
        x.reshape(t, D_MODEL), p.reshape(DEPTH * t, PLE_DIM), loss_target.reshape(t, D_MODEL), rel_bias, wts,
        matmul_weights, grads_ready)

    def local_grad(n):
        if n == "rel_bias":
            return d_rel_bias
        key = LOCAL_GRAD_KEY.get(n, n)
        return jnp.stack([grads[li][key].reshape(wt[n].shape[1:]) if n in REPLICATED else grads[li][key]
                          for li in range(DEPTH)])

    small_names = REPLICATED + SMALL_SHARDED
    small_full_shapes = [wt[n].shape if n in REPLICATED else full[n].shape for n in small_names]
    small_parts = _allgather_vmem(_pack_rows([local_grad(n) for n in small_names] + [loss_part.reshape(1)]),
                                  name="allgather_small_grads")
    small_parts = small_parts.reshape(N_DEV, -1, LANES)

    out_g, out_d, out_m, out_v = {}, {}, {}, {}
    for n in big:
        shp = wt[n].shape
        two_d = lambda a: a.reshape(-1, shp[-1])
        res = _adamw_reduce([landed[n, li] for li in range(DEPTH)], two_d(wt[n]), two_d(mom_m[n]), two_d(mom_v[n]),
                            tr=32 if n == "w_down" else 128, name="adamw_" + n)
        out_g[n], out_d[n], out_m[n], out_v[n] = [r.reshape(shp) for r in res]

    *reduced, loss = _unpack_rows(_sum_slots(small_parts, name="sum_small_grads"), small_full_shapes + [(1,)])
    loss = loss[0]
    reduced = dict(zip(small_names, reduced))
    rep_shapes = [wt[n].shape for n in REPLICATED]
    upd = _adamw_plain(_pack_rows([reduced[n] for n in REPLICATED]), _pack_rows([wt[n] for n in REPLICATED]),
                       _pack_rows([mom_m[n] for n in REPLICATED]), _pack_rows([mom_v[n] for n in REPLICATED]),
                       name="adamw_replicated")
    for dst, packed in zip((out_d, out_m, out_v), upd):
        dst.update(zip(REPLICATED, _unpack_rows(packed, rep_shapes)))
    for n in REPLICATED:
        out_g[n] = reduced[n]
    for n in SMALL_SHARDED:
        shp = wt[n].shape
        g = reduced[n].reshape(shp[0], shp[1], N_DEV, shp[2])
        g = lax.dynamic_index_in_dim(g, me, axis=2, keepdims=False)
        two_d = lambda a: a.reshape(-1, shp[-1])
        res = _adamw_plain(two_d(g), two_d(wt[n]), two_d(mom_m[n]), two_d(mom_v[n]), name="adamw_" + n)
        out_g[n] = g
        out_d[n], out_m[n], out_v[n] = [r.reshape(shp) for r in res]

    return (loss, dx.reshape(bl, SEQ, D_MODEL), *[out_g[n] for n in WEIGHT_NAMES], *[out_d[n] for n in WEIGHT_NAMES],
            *[out_m[n] for n in WEIGHT_NAMES], *[out_v[n] for n in WEIGHT_NAMES])
```

```python
import math

import jax
import jax.numpy as jnp
import numpy as np
from jax import lax
from jax.experimental import pallas as pl
from jax.experimental.pallas import tpu as pltpu
from jax.experimental.pallas import tpu_sc as plsc

F32 = jnp.float32
BF16 = jnp.bfloat16

N_DEV = 8
D_MODEL = 1024
SEQ = 2048
DEPTH = 2
HEAD_DIM = 64
IN_WIDTH = 2304
D_FF = 2816
PLE_DIM = 256
C_CHUNK = 128
C_GROUPS = 4
DILATED_CFGS = ((128, 1), (512, 4), (2048, 16))
DILATIONS = tuple(d for _, d in DILATED_CFGS)
A_RADIUS = 64
SWA_RADIUS = 128
BAND_BLOCK = 256
GRID_W = 64
ROPE_THETA = 10000.0
REL_BUCKETS = 32
REL_MAX_DIST = 1024
EPS = 1e-6
NEG_INF = -1e30
ATTN_SCALE = HEAD_DIM ** -0.5
LANES = 128

ADAM_LR = 0.001
ADAM_B1 = 0.9
ADAM_B2 = 0.999
ADAM_EPS = 1e-08
ADAM_WD = 0.01
ADAM_STEP = 10

MESH = pl.DeviceIdType.MESH
NT = (((1,), (1,)), ((), ()))
TN = (((0,), (0,)), ((), ()))
ARB = "arbitrary"
PAR = "parallel"


def _cparams(*sem):
    return pltpu.CompilerParams(dimension_semantics=tuple(sem))


def _sds(shape, dtype):
    return jax.ShapeDtypeStruct(tuple(shape), dtype)


def _group_sum_matrix(n, same_group):
    r = lax.broadcasted_iota(jnp.int32, (n, n), 0)
    c = lax.broadcasted_iota(jnp.int32, (n, n), 1)
    if same_group:
        return ((r >> 6) == (c >> 6)).astype(F32)
    return ((r & 63) == (c & 63)).astype(F32)


def _seg_sum(x, e):
    eb = e.astype(BF16)
    hi = x.astype(BF16)
    lo = (x - hi.astype(F32)).astype(BF16)
    return jnp.dot(hi, eb, preferred_element_type=F32) + jnp.dot(lo, eb, preferred_element_type=F32)


def _gelu(x):
    c = math.sqrt(2.0 / math.pi)
    return 0.5 * x * (1.0 + jnp.tanh(c * (x + 0.044715 * (x * x * x))))


def _gelu_grad(x):
    c = math.sqrt(2.0 / math.pi)
    t = jnp.tanh(c * (x + 0.044715 * (x * x * x)))
    return 0.5 * (1.0 + t) + 0.5 * x * (1.0 - t * t) * c * (1.0 + 3.0 * 0.044715 * (x * x))


def _sigmoid(x):
    return 1.0 / (1.0 + jnp.exp(-x))


def _scatter_cols(scratch, first, val):
    for c in range(val.shape[1] // LANES):
        scratch[first + c] = val[:, c * LANES:(c + 1) * LANES]


def _gather_cols(scratch, first, ncol):
    return jnp.concatenate([scratch[first + c] for c in range(ncol)], axis=1)


def _read_residue(scratch, first, ncol, r, d):
    n = scratch.shape[1] // d
    return jnp.concatenate([scratch.at[first + c][pl.ds(r, n, stride=d), :] for c in range(ncol)], axis=1)


def _write_residue(scratch, first, r, d, val):
    n = scratch.shape[1] // d
    for c in range(val.shape[1] // LANES):
        scratch.at[first + c][pl.ds(r, n, stride=d), :] = val[:, c * LANES:(c + 1) * LANES]


def _norm_mm(xs, gain, w, res, *, tm, tn, name, out_dtype=F32):
    t = xs[0].shape[0]
    k = sum(x.shape[1] for x in xs)
    n = w.shape[1]
    ng = len(xs)
    has_res = res is not None

    def body(*refs):
        x_refs = refs[:ng]
        g_ref, w_ref = refs[ng], refs[ng + 1]
        res_ref = refs[ng + 2] if has_res else None
        hn_ref, o_ref, hn_s = refs[ng + 2 + has_res:]

        @pl.when(pl.program_id(1) == 0)
        def _():
            off = 0
            for xr in x_refs:
                x = xr[...].astype(F32)
                wd = x.shape[1]
                r = lax.rsqrt(jnp.mean(x * x, axis=-1, keepdims=True) + EPS)
                hn_s[:, off:off + wd] = (x * r * g_ref[:, off:off + wd]).astype(BF16)
                off += wd
            hn_ref[...] = hn_s[...]

        acc = jnp.dot(hn_s[...], w_ref[...], preferred_element_type=F32)
        if has_res:
            acc = acc + res_ref[...]
        o_ref[...] = acc.astype(out_dtype)

    in_specs = [pl.BlockSpec((tm, x.shape[1]), lambda i, j: (i, 0)) for x in xs]
    in_specs += [pl.BlockSpec((1, k), lambda i, j: (0, 0)), pl.BlockSpec((k, tn), lambda i, j: (0, j))]
    args = list(xs) + [gain, w]
    if has_res:
        in_specs.append(pl.BlockSpec((tm, tn), lambda i, j: (i, j)))
        args.append(res)
    return pl.pallas_call(
        body, name=name, grid=(t // tm, n // tn), in_specs=in_specs,
        out_specs=[pl.BlockSpec((tm, k), lambda i, j: (i, 0)), pl.BlockSpec((tm, tn), lambda i, j: (i, j))],
        out_shape=[_sds((t, k), BF16), _sds((t, n), out_dtype)],
        scratch_shapes=[pltpu.VMEM((tm, k), BF16)],
        compiler_params=_cparams(PAR, ARB),
    )(*args)


def _mm(a, b, mode, res, *, tm, tn, out_dtype, name, a_rows=None):
    if mode == "tn":
        kk, m = a.shape
        blk_a = 0
        if a_rows is not None:
            blk_a, kk = a_rows
        a_spec = pl.BlockSpec((kk, tm), lambda i, j: (blk_a, i))
    else:
        m, kk = a.shape
        a_spec = pl.BlockSpec((tm, kk), lambda i, j: (i, 0))
    if mode == "nt":
        n = b.shape[0]
        b_spec = pl.BlockSpec((tn, kk), lambda i, j: (j, 0))
    else:
        n = b.shape[1]
        b_spec = pl.BlockSpec((kk, tn), lambda i, j: (0, j))
    has_res = res is not None

    def body(*refs):
        a_ref, b_ref = refs[0], refs[1]
        o_ref = refs[-1]
        av = a_ref[...].astype(BF16)
        bv = b_ref[...].astype(BF16)
        if mode == "nn":
            acc = jnp.dot(av, bv, preferred_element_type=F32)
        elif mode == "nt":
            acc = lax.dot_general(av, bv, NT, preferred_element_type=F32)
        else:
            acc = lax.dot_general(av, bv, TN, preferred_element_type=F32)
        if has_res:
            acc = acc + refs[2][...]
        o_ref[...] = acc.astype(out_dtype)

    in_specs = [a_spec, b_spec]
    args = [a, b]
    if has_res:
        in_specs.append(pl.BlockSpec((tm, tn), lambda i, j: (i, j)))
        args.append(res)
    return pl.pallas_call(
        body, name=name, grid=(m // tm, n // tn), in_specs=in_specs,
        out_specs=pl.BlockSpec((tm, tn), lambda i, j: (i, j)),
        out_shape=_sds((m, n), out_dtype),
        compiler_params=_cparams(PAR, PAR),
    )(*args)


def _mm_bt_normbwd(dys, w, xs, gain, dres, *, tm, tn, name, emit_bf16=False):
    t, wd_each = dys[0].shape
    nd = len(dys)
    per = wd_each // tn
    nj = nd * per
    k = w.shape[0]
    ng = len(xs)
    has_res = dres is not None

    def body(*refs):
        dy_refs = refs[:nd]
        w_ref = refs[nd]
        x_refs = refs[nd + 1:nd + 1 + ng]
        g_ref = refs[nd + 1 + ng]
        dres_ref = refs[nd + 2 + ng] if has_res else None
        outs = refs[nd + 2 + ng + has_res:]
        dx_ref = outs[0]
        dxb_ref = outs[1] if emit_bf16 else None
        dg_ref, acc = outs[1 + emit_bf16:]
        i, j = pl.program_id(0), pl.program_id(1)

        @pl.when(j == 0)
        def _():
            acc[...] = jnp.zeros_like(acc)

        for d, dy_ref in enumerate(dy_refs):
            @pl.when((j >= d * per) & (j < (d + 1) * per))
            def _(dy_ref=dy_ref):
                acc[...] += lax.dot_general(dy_ref[...].astype(BF16), w_ref[...], NT, preferred_element_type=F32)

        @pl.when(j == nj - 1)
        def _():
            @pl.when(i == 0)
            def _():
                dg_ref[...] = jnp.zeros_like(dg_ref)

            off = 0
            for xr in x_refs:
                x = xr[...].astype(F32)
                wd = x.shape[1]
                g = g_ref[:, off:off + wd]
                dyn = acc[:, off:off + wd]
                r = lax.rsqrt(jnp.mean(x * x, axis=-1, keepdims=True) + EPS)
                gdy = dyn * g
                dx = r * gdy - x * (r * r * r * jnp.mean(gdy * x, axis=-1, keepdims=True))
                if has_res:
                    dx = dx + dres_ref[:, off:off + wd]
                dx_ref[:, off:off + wd] = dx
                if emit_bf16:
                    dxb_ref[:, off:off + wd] = dx.astype(BF16)
                dg_ref[:, off:off + wd] += jnp.sum(dyn * x * r, axis=0, keepdims=True)
                off += wd

    def dy_map(d):
        return lambda i, j: (i, jnp.clip(j - d * per, 0, per - 1))

    in_specs = [pl.BlockSpec((tm, tn), dy_map(d)) for d in range(nd)]
    in_specs.append(pl.BlockSpec((k, tn), lambda i, j: (0, j)))
    in_specs += [pl.BlockSpec((tm, x.shape[1]), lambda i, j: (i, 0)) for x in xs]
    in_specs.append(pl.BlockSpec((1, k), lambda i, j: (0, 0)))
    args = list(dys) + [w] + list(xs) + [gain]
    if has_res:
        in_specs.append(pl.BlockSpec((tm, k), lambda i, j: (i, 0)))
        args.append(dres)
    row = pl.BlockSpec((tm, k), lambda i, j: (i, 0))
    out_specs = [row] + ([row] if emit_bf16 else []) + [pl.BlockSpec((1, k), lambda i, j: (0, 0))]
    out_shape = [_sds((t, k), F32)] + ([_sds((t, k), BF16)] if emit_bf16 else []) + [_sds((1, k), F32)]
    return pl.pallas_call(
        body, name=name, grid=(t // tm, nj), in_specs=in_specs, out_specs=out_specs, out_shape=out_shape,
        scratch_shapes=[pltpu.VMEM((tm, k), F32)],
        compiler_params=_cparams(ARB, ARB),
    )(*args)


def _rope_partner(y):
    n = y.shape[1]
    lane = lax.broadcasted_iota(jnp.int32, y.shape, 1)
    return jnp.where((lane & 31) < 16, pltpu.roll(y, n - 16, 1), pltpu.roll(y, 16, 1))


def _residue_specs(tm, width, nt):
    specs = [pl.BlockSpec((tm, width), lambda b, i: (b * nt + i, 0))]
    for d in DILATIONS[1:]:
        specs.append(pl.BlockSpec((None, d, tm // d, width), lambda b, i: (b, 0, i, 0)))
    return specs


def _residue_shapes(bl, width, dtype):
    return [_sds((bl * SEQ, width), dtype)] + [_sds((bl, d, SEQ // d, width), dtype) for d in DILATIONS[1:]]


def _qkprep_fwd(proj, gains, cos, sins, *, tm, name):
    t = proj.shape[0]
    bl = t // SEQ
    nt = SEQ // tm

    def body(p_ref, g_ref, c_ref, s_ref, qa1_ref, qa4_ref, qa16_ref, qb_ref, qd_ref, scr):
        e = _group_sum_matrix(256, True)

        def hn(x, row):
            x = x.astype(F32)
            wd = x.shape[1]
            ms = _seg_sum(x * x, e[:wd, :wd]) * (1.0 / HEAD_DIM)
            return x * lax.rsqrt(ms + EPS) * g_ref[row:row + 1, :wd]

        qa = jnp.concatenate([hn(p_ref[:, 0:256], 0) * ATTN_SCALE, hn(p_ref[:, 256:512], 1),
                              p_ref[:, 512:768].astype(F32)], axis=1)
        qa1_ref[...] = qa.astype(BF16)
        _scatter_cols(scr, 0, qa)
        for d, ref in ((4, qa4_ref), (16, qa16_ref)):
            for r in range(d):
                ref[r] = _read_residue(scr, 0, 6, r, d).astype(BF16)
        qb_ref[:, 0:256] = (hn(p_ref[:, 768:1024], 2) * ATTN_SCALE).astype(BF16)
        qb_ref[:, 256:384] = hn(p_ref[:, 1024:1152], 3).astype(BF16)
        qb_ref[:, 384:512] = p_ref[:, 1152:1280].astype(BF16)
        yq = hn(p_ref[:, 1792:2048], 4)
        yq = yq * c_ref[...] + _rope_partner(yq) * s_ref[...]
        qd_ref[:, 0:256] = (yq * ATTN_SCALE).astype(BF16)
        yk = hn(p_ref[:, 2048:2176], 5)
        yk = yk * c_ref[:, 0:128] + _rope_partner(yk) * s_ref[:, 0:128]
        qd_ref[:, 256:384] = yk.astype(BF16)
        qd_ref[:, 384:512] = p_ref[:, 2176:2304].astype(BF16)

    row = lambda width: pl.BlockSpec((tm, width), lambda b, i: (b * nt + i, 0))
    tab = pl.BlockSpec((tm, 256), lambda b, i: (i, 0))
    return pl.pallas_call(
        body, name=name, grid=(bl, nt),
        in_specs=[row(IN_WIDTH), pl.BlockSpec((8, 256), lambda b, i: (0, 0)), tab, tab],
        out_specs=_residue_specs(tm, 768, nt) + [row(512), row(512)],
        out_shape=_residue_shapes(bl, 768, BF16) + [_sds((t, 512), BF16), _sds((t, 512), BF16)],
        scratch_shapes=[pltpu.VMEM((6, tm, LANES), F32)],
        compiler_params=_cparams(PAR, PAR),
    )(proj, gains, cos, sins)


def _qkprep_bwd(proj, da, db, dd, dcu, dcv, gains, cos, sins, *, tm, name):
    t = proj.shape[0]
    bl = t // SEQ
    nt = SEQ // tm
    flat = [a for cfg in da for a in cfg] + list(db) + list(dd) + [dcu, dcv]

    def body(*refs):
        p_ref, g_ref, c_ref, s_ref = refs[:4]
        d_refs = refs[4:4 + len(flat)]
        dp_ref, dg_ref, scr = refs[4 + len(flat):]
        a_refs = d_refs[:9]
        dqb_ref, dkb_ref, dvb_ref, dqd_ref, dkd_ref, dvd_ref, dcu_ref, dcv_ref = d_refs[9:]
        e = _group_sum_matrix(256, True)
        first = (pl.program_id(0) == 0) & (pl.program_id(1) == 0)
        last = (pl.program_id(0) == bl - 1) & (pl.program_id(1) == nt - 1)

        @pl.when(first)
        def _():
            dg_ref[...] = jnp.zeros_like(dg_ref)

        def hn_bwd(x, dy, row):
            x, dy = x.astype(F32), dy.astype(F32)
            wd = x.shape[1]
            ee = e[:wd, :wd]
            g = g_ref[row:row + 1, :wd]
            r = lax.rsqrt(_seg_sum(x * x, ee) * (1.0 / HEAD_DIM) + EPS)
            gdy = dy * g
            dx = r * gdy - x * (r * r * r * (_seg_sum(gdy * x, ee) * (1.0 / HEAD_DIM)))
            dg_ref[row:row + 1, :wd] += jnp.sum(dy * x * r, axis=0, keepdims=True)
            return dx

        def rope_bwd(dy, wd):
            dy = dy.astype(F32)
            return dy * c_ref[:, :wd] + _rope_partner(dy * s_ref[:, :wd])

        dqkv = jnp.concatenate([a_refs[m][...].astype(F32) for m in range(3)], axis=1)
        for ci, d in ((1, 4), (2, 16)):
            for r in range(d):
                part = jnp.concatenate([a_refs[3 * ci + m][r].astype(F32) for m in range(3)], axis=1)
                _write_residue(scr, 0, r, d, part)
            dqkv = dqkv + _gather_cols(scr, 0, 6)
        dp_ref[:, 0:256] = hn_bwd(p_ref[:, 0:256], dqkv[:, 0:256] * ATTN_SCALE, 0).astype(BF16)
        dp_ref[:, 256:512] = hn_bwd(p_ref[:, 256:512], dqkv[:, 256:512], 1).astype(BF16)
        dp_ref[:, 512:768] = dqkv[:, 512:768].astype(BF16)
        dp_ref[:, 768:1024] = hn_bwd(p_ref[:, 768:1024], dqb_ref[...] * ATTN_SCALE, 2).astype(BF16)
        dp_ref[:, 1024:1152] = hn_bwd(p_ref[:, 1024:1152], dkb_ref[...], 3).astype(BF16)
        dp_ref[:, 1152:1280] = dvb_ref[...].astype(BF16)
        dp_ref[:, 1280:1536] = dcu_ref[...].astype(BF16)
        dp_ref[:, 1536:1792] = dcv_ref[...].astype(BF16)
        dp_ref[:, 1792:2048] = hn_bwd(p_ref[:, 1792:2048], rope_bwd(dqd_ref[...] * ATTN_SCALE, 256), 4).astype(BF16)
        dp_ref[:, 2048:2176] = hn_bwd(p_ref[:, 2048:2176], rope_bwd(dkd_ref[...], 128), 5).astype(BF16)
        dp_ref[:, 2176:2304] = dvd_ref[...].astype(BF16)

        @pl.when(last)
        def _():
            dg_ref[...] = _seg_sum(dg_ref[...], _group_sum_matrix(256, False))

    row = lambda width: pl.BlockSpec((tm, width), lambda b, i: (b * nt + i, 0))
    tab = pl.BlockSpec((tm, 256), lambda b, i: (i, 0))
    in_specs = [row(IN_WIDTH), pl.BlockSpec((8, 256), lambda b, i: (0, 0)), tab, tab]
    res_specs = _residue_specs(tm, 256, nt)
    in_specs += [res_specs[ci] for ci in range(3) for _ in range(3)]
    in_specs += [row(a.shape[1]) for a in flat[9:]]
    return pl.pallas_call(
        body, name=name, grid=(bl, nt), in_specs=in_specs,
        out_specs=[row(IN_WIDTH), pl.BlockSpec((8, 256), lambda b, i: (0, 0))],
        out_shape=[_sds((t, IN_WIDTH), BF16), _sds((8, 256), F32)],
        scratch_shapes=[pltpu.VMEM((6, tm, LANES), F32)],
        compiler_params=_cparams(ARB, ARB),
    )(proj, gains, cos, sins, *flat)


BAND_ROWS_PER_STEP = 512


def _residues_per_step(dil, seq_len):
    return min(dil, max(1, BAND_ROWS_PER_STEP // seq_len))


def _band_spec(seq_len, spec, rb):
    width, idx = spec
    return pl.BlockSpec((None, rb, seq_len, width), lambda b, r: (b, r, 0, idx))


def _fill_padded(dst, src_ref, rad, seq_len):
    z = jnp.zeros((rad, dst.shape[1]), dst.dtype)
    dst[0:rad, :] = z
    dst[rad + seq_len:rad + seq_len + rad, :] = z
    dst[rad:rad + seq_len, :] = src_ref[...]


def _band_fwd(src, qs, ks, vs, bias, sink, *, rad, nh, nkv, name):
    bl, dil, sl, _ = src.shape
    blk = bias.shape[1]
    kw = blk + 2 * rad
    nb = sl // blk
    rep = nh // nkv
    has_sink = sink is not None
    rb = _residues_per_step(dil, sl)

    def body(*refs):
        q_all, k_all, v_all, b_ref = refs[:4]
        s_ref = refs[4] if has_sink else None
        o_all, l_all, kp, vp = refs[4 + has_sink:]
        for ri in range(rb):
            one_sequence(q_all.at[ri], k_all.at[ri], v_all.at[ri], b_ref, s_ref, o_all.at[ri], l_all.at[ri], kp, vp)

    def one_sequence(q_ref, k_ref, v_ref, b_ref, s_ref, o_ref, l_ref, kp, vp):
        _fill_padded(kp, k_ref, rad, sl)
        _fill_padded(vp, v_ref, rad, sl)

        def blk_body(i, carry):
            r0 = pl.multiple_of(i * blk, blk)
            qb = q_ref[pl.ds(r0, blk), :]
            kwin = kp[pl.ds(r0, kw), :]
            vwin = vp[pl.ds(r0, kw), :]
            col = r0 - rad + lax.broadcasted_iota(jnp.int32, (blk, kw), 1)
            neg = jnp.where((col >= 0) & (col < sl), 0.0, NEG_INF).astype(F32)
            for h in range(nh):
                g = h // rep
                hs = slice(h * HEAD_DIM, (h + 1) * HEAD_DIM)
                gs = slice(g * HEAD_DIM, (g + 1) * HEAD_DIM)
                s = lax.dot_general(qb[:, hs], kwin[:, gs], NT, preferred_element_type=F32)
                s = s + b_ref[h] + neg
                m = jnp.max(s, axis=1, keepdims=True)
                if has_sink:
                    sk = s_ref[h][0:1, 0:1]
                    m = jnp.maximum(m, sk)
                p = jnp.exp(s - m)
                den = jnp.sum(p, axis=1, keepdims=True)
                if has_sink:
                    den = den + jnp.exp(sk - m)
                o = jnp.dot(p.astype(BF16), vwin[:, gs], preferred_element_type=F32) / den
                o_ref[pl.ds(r0, blk), hs] = o.astype(BF16)
                l_ref[pl.ds(r0, blk), hs] = jnp.broadcast_to(m + jnp.log(den), (blk, HEAD_DIM))
            return carry

        lax.fori_loop(0, nb, blk_body, 0)

    in_specs = [_band_spec(sl, qs, rb), _band_spec(sl, ks, rb), _band_spec(sl, vs, rb),
                pl.BlockSpec((nh, blk, kw), lambda b, r: (0, 0, 0))]
    args = [src] * 3 + [bias]
    if has_sink:
        in_specs.append(pl.BlockSpec((nh, 8, 128), lambda b, r: (0, 0, 0)))
        args.append(sink)
    return pl.pallas_call(
        body, name=name, grid=(bl, dil // rb), in_specs=in_specs,
        out_specs=[_band_spec(sl, (256, 0), rb)] * 2,
        out_shape=[_sds((bl, dil, sl, 256), BF16), _sds((bl, dil, sl, 256), F32)],
        scratch_shapes=[pltpu.VMEM((sl + 2 * rad, ks[0]), BF16), pltpu.VMEM((sl + 2 * rad, vs[0]), BF16)],
        compiler_params=_cparams(PAR, PAR),
    )(*args)


def _band_bwd(src, qs, ks, vs, bias, sink, dy, dcol, lse, delta, *, rad, nh, nkv, name):
    bl, dil, sl, _ = src.shape
    blk = bias.shape[1]
    kw = blk + 2 * rad
    nb = sl // blk
    rep = nh // nkv
    has_sink = sink is not None
    rb = _residues_per_step(dil, sl)
    wk, wv = ks[0], vs[0]

    def body(*refs):
        q_all, k_all, v_all, b_ref = refs[:4]
        s_ref = refs[4] if has_sink else None
        do_all, l_all, dl_all = refs[4 + has_sink:7 + has_sink]
        outs = refs[7 + has_sink:]
        dsk_ref = None
        if has_sink:
            dq_all, dk_all, dv_all, db_ref, dsk_ref, kp, vp, dka, dva = outs
        else:
            dq_all, dk_all, dv_all, db_ref, kp, vp, dka, dva = outs

        @pl.when((pl.program_id(0) == 0) & (pl.program_id(1) == 0))
        def _():
            db_ref[...] = jnp.zeros_like(db_ref)
            if has_sink:
                dsk_ref[...] = jnp.zeros_like(dsk_ref)

        for ri in range(rb):
            one_sequence(q_all.at[ri], k_all.at[ri], v_all.at[ri], b_ref, s_ref, do_all.at[ri], l_all.at[ri],
                         dl_all.at[ri], dq_all.at[ri], dk_all.at[ri], dv_all.at[ri], db_ref, dsk_ref, kp, vp, dka, dva)

    def one_sequence(q_ref, k_ref, v_ref, b_ref, s_ref, do_ref, l_ref, dl_ref, dq_ref, dk_ref, dv_ref, db_ref, dsk_ref,
                     kp, vp, dka, dva):
        _fill_padded(kp, k_ref, rad, sl)
        _fill_padded(vp, v_ref, rad, sl)
        dka[...] = jnp.zeros_like(dka)
        dva[...] = jnp.zeros_like(dva)

        def blk_body(i, carry):
            r0 = pl.multiple_of(i * blk, blk)
            qb = q_ref[pl.ds(r0, blk), :]
            kwin = kp[pl.ds(r0, kw), :]
            vwin = vp[pl.ds(r0, kw), :]
            dob = do_ref[pl.ds(r0, blk), :].astype(BF16)
            lb = l_ref[pl.ds(r0, blk), :]
            dlb = dl_ref[pl.ds(r0, blk), :]
            col = r0 - rad + lax.broadcasted_iota(jnp.int32, (blk, kw), 1)
            neg = jnp.where((col >= 0) & (col < sl), 0.0, NEG_INF).astype(F32)
            for h in range(nh):
                g = h // rep
                hs = slice(h * HEAD_DIM, (h + 1) * HEAD_DIM)
                gs = slice(g * HEAD_DIM, (g + 1) * HEAD_DIM)
                qh, kh, vh, doh = qb[:, hs], kwin[:, gs], vwin[:, gs], dob[:, hs]
                lh = lb[:, h * HEAD_DIM:h * HEAD_DIM + 1]
                dlh = dlb[:, h * HEAD_DIM:h * HEAD_DIM + 1]
                s = lax.dot_general(qh, kh, NT, preferred_element_type=F32) + b_ref[h] + neg
                p = jnp.exp(s - lh)
                dp = lax.dot_general(doh, vh, NT, preferred_element_type=F32)
                ds = p * (dp - dlh)
                dsb = ds.astype(BF16)
                dq_ref[pl.ds(r0, blk), hs] = jnp.dot(dsb, kh, preferred_element_type=F32).astype(BF16)
                dka[pl.ds(r0, kw), gs] += lax.dot_general(dsb, qh, TN, preferred_element_type=F32)
                dva[pl.ds(r0, kw), gs] += lax.dot_general(p.astype(BF16), doh, TN, preferred_element_type=F32)
                db_ref[h] += ds
                if has_sink:
                    ps = jnp.exp(s_ref[h][0:1, 0:1] - lh)
                    dsk_ref[h] += jnp.broadcast_to(-jnp.sum(ps * dlh, axis=0, keepdims=True), (8, 128))
            return carry

        lax.fori_loop(0, nb, blk_body, 0)
        dk_ref[...] = dka[rad:rad + sl, :].astype(BF16)
        dv_ref[...] = dva[rad:rad + sl, :].astype(BF16)

    const3 = lambda b, r: (0, 0, 0)
    in_specs = [_band_spec(sl, qs, rb), _band_spec(sl, ks, rb), _band_spec(sl, vs, rb),
                pl.BlockSpec((nh, blk, kw), const3)]
    args = [src] * 3 + [bias]
    if has_sink:
        in_specs.append(pl.BlockSpec((nh, 8, 128), const3))
        args.append(sink)
    row = _band_spec(sl, (256, 0), rb)
    in_specs += [_band_spec(sl, (256, dcol), rb), row, row]
    args += [dy, lse, delta]
    out_specs = [row, _band_spec(sl, (wk, 0), rb), _band_spec(sl, (wv, 0), rb), pl.BlockSpec((nh, blk, kw), const3)]
    out_shape = [_sds((bl, dil, sl, 256), BF16), _sds((bl, dil, sl, wk), BF16), _sds((bl, dil, sl, wv), BF16),
                 _sds((nh, blk, kw), F32)]
    if has_sink:
        out_specs.append(pl.BlockSpec((nh, 8, 128), const3))
        out_shape.append(_sds((nh, 8, 128), F32))
    return pl.pallas_call(
        body, name=name, grid=(bl, dil // rb), in_specs=in_specs, out_specs=out_specs, out_shape=out_shape,
        scratch_shapes=[pltpu.VMEM((sl + 2 * rad, wk), BF16), pltpu.VMEM((sl + 2 * rad, wv), BF16),
                        pltpu.VMEM((sl + 2 * rad, wk), F32), pltpu.VMEM((sl + 2 * rad, wv), F32)],
        compiler_params=_cparams(ARB, ARB),
    )(*args)


def _combine_a(os_, ls_, *, tm, name):
    bl = os_[1].shape[0]
    t = bl * SEQ
    nt = SEQ // tm

    def body(o1, o4, o16, l1, l4, l16, y_ref, lt_ref, scr):
        for k, (d, ref) in enumerate(((4, o4), (16, o16), (4, l4), (16, l16))):
            for r in range(d):
                _write_residue(scr, 2 * k, r, d, ref[r].astype(F32))
        o2, o3, b, c = (_gather_cols(scr, 2 * k, 2) for k in range(4))
        a = l1[...]
        m = jnp.maximum(jnp.maximum(a, b), c)
        ea, eb, ec = jnp.exp(a - m), jnp.exp(b - m), jnp.exp(c - m)
        den = ea + eb + ec
        y_ref[...] = ((ea / den) * o1[...].astype(F32) + (eb / den) * o2 + (ec / den) * o3).astype(BF16)
        lt_ref[...] = m + jnp.log(den)

    specs = _residue_specs(tm, 256, nt)
    return pl.pallas_call(
        body, name=name, grid=(bl, nt), in_specs=specs * 2, out_specs=[specs[0]] * 2,
        out_shape=[_sds((t, 256), BF16), _sds((t, 256), F32)], scratch_shapes=[pltpu.VMEM((8, tm, LANES), F32)],
        compiler_params=_cparams(PAR, PAR),
    )(*os_, *ls_)


def _deltas(dycat, ya, yb, yd, lse_a, *, tm, name):
    t = ya.shape[0]
    bl = t // SEQ
    nt = SEQ // tm

    def body(dy_ref, ya_ref, yb_ref, yd_ref, la_ref, dy4, dy16, l4, l16, da1, da4, da16, db_ref, dd_ref, scr):
        e = _group_sum_matrix(256, True)
        dya = dy_ref[:, 0:256]
        dla = _seg_sum(dya * ya_ref[...].astype(F32), e)
        da1[...] = dla
        db_ref[...] = _seg_sum(dy_ref[:, 256:512] * yb_ref[...].astype(F32), e)
        dd_ref[...] = _seg_sum(dy_ref[:, 768:1024] * yd_ref[...].astype(F32), e)
        for k, (val, r4, r16) in enumerate(((dya, dy4, dy16), (la_ref[...], l4, l16), (dla, da4, da16))):
            _scatter_cols(scr, 2 * k, val)
            for d, ref in ((4, r4), (16, r16)):
                for r in range(d):
                    ref[r] = _read_residue(scr, 2 * k, 2, r, d)

    specs = _residue_specs(tm, 256, nt)
    nat = specs[0]
    shapes = _residue_shapes(bl, 256, F32)
    outs = pl.pallas_call(
        body, name=name, grid=(bl, nt),
        in_specs=[pl.BlockSpec((tm, 1024), lambda b, i: (b * nt + i, 0)), nat, nat, nat, nat],
        out_specs=specs[1:] + specs[1:] + specs + [nat, nat],
        out_shape=shapes[1:] + shapes[1:] + shapes + [shapes[0], shapes[0]],
        scratch_shapes=[pltpu.VMEM((6, tm, LANES), F32)],
        compiler_params=_cparams(PAR, PAR),
    )(dycat, ya, yb, yd, lse_a)
    return outs[0:2], outs[2:4], outs[4:7], outs[7], outs[8]


def _dense_fwd(qd, *, tq, name):
    t = qd.shape[0]
    bl = t // SEQ
    nq = SEQ // tq

    def body(q_ref, k_ref, v_ref, o_ref, l_ref):
        q = q_ref[...]
        for g in range(2):
            h0, h1 = 2 * g, 2 * g + 1
            q2 = jnp.concatenate([q[:, h0 * 64:(h0 + 1) * 64], q[:, h1 * 64:(h1 + 1) * 64]], axis=0)
            kg = k_ref[:, g * 64:(g + 1) * 64]
            vg = v_ref[:, g * 64:(g + 1) * 64]
            s = lax.dot_general(q2, kg, NT, preferred_element_type=F32)
            m = jnp.max(s, axis=1, keepdims=True)
            p = jnp.exp(s - m)
            den = jnp.sum(p, axis=1, keepdims=True)
            o2 = jnp.dot(p.astype(BF16), vg, preferred_element_type=F32) / den
            l2 = jnp.broadcast_to(m + jnp.log(den), (2 * tq, 64))
            o_ref[:, h0 * 64:(h0 + 1) * 64] = o2[:tq].astype(BF16)
            o_ref[:, h1 * 64:(h1 + 1) * 64] = o2[tq:].astype(BF16)
            l_ref[:, h0 * 64:(h0 + 1) * 64] = l2[:tq]
            l_ref[:, h1 * 64:(h1 + 1) * 64] = l2[tq:]

    q3 = qd.reshape(bl, SEQ, 512)
    o, lse = pl.pallas_call(
        body, name=name, grid=(bl, nq),
        in_specs=[pl.BlockSpec((None, tq, 256), lambda b, i: (b, i, 0)),
                  pl.BlockSpec((None, SEQ, 128), lambda b, i: (b, 0, 2)),
                  pl.BlockSpec((None, SEQ, 128), lambda b, i: (b, 0, 3))],
        out_specs=[pl.BlockSpec((None, tq, 256), lambda b, i: (b, i, 0))] * 2,
        out_shape=[_sds((bl, SEQ, 256), BF16), _sds((bl, SEQ, 256), F32)],
        compiler_params=_cparams(PAR, PAR),
    )(q3, q3, q3)
    return o.reshape(t, 256), lse.reshape(t, 256)


def _dense_bwd(qd, dycat, lse, delta, *, tq, name):
    t = qd.shape[0]
    bl = t // SEQ
    nq = SEQ // tq

    def body(q_ref, k_ref, v_ref, do_ref, l_ref, dl_ref, dq_ref, dk_ref, dv_ref, dkt, dvt):
        @pl.when(pl.program_id(1) == 0)
        def _():
            dkt[...] = jnp.zeros_like(dkt)
            dvt[...] = jnp.zeros_like(dvt)

        q = q_ref[...]
        do = do_ref[...].astype(BF16)
        lv = l_ref[...]
        dlv = dl_ref[...]
        for g in range(2):
            h0, h1 = 2 * g, 2 * g + 1
            q2 = jnp.concatenate([q[:, h0 * 64:(h0 + 1) * 64], q[:, h1 * 64:(h1 + 1) * 64]], axis=0)
            do2 = jnp.concatenate([do[:, h0 * 64:(h0 + 1) * 64], do[:, h1 * 64:(h1 + 1) * 64]], axis=0)
            l2 = jnp.concatenate([lv[:, h0 * 64:h0 * 64 + 1], lv[:, h1 * 64:h1 * 64 + 1]], axis=0)
            dl2 = jnp.concatenate([dlv[:, h0 * 64:h0 * 64 + 1], dlv[:, h1 * 64:h1 * 64 + 1]], axis=0)
            kg = k_ref[:, g * 64:(g + 1) * 64]
            vg = v_ref[:, g * 64:(g + 1) * 64]
            s = lax.dot_general(q2, kg, NT, preferred_element_type=F32)
            p = jnp.exp(s - l2)
            dp = lax.dot_general(do2, vg, NT, preferred_element_type=F32)
            ds = (p * (dp - dl2)).astype(BF16)
            dq2 = jnp.dot(ds, kg, preferred_element_type=F32)
            dq_ref[:, h0 * 64:(h0 + 1) * 64] = dq2[:tq].astype(BF16)
            dq_ref[:, h1 * 64:(h1 + 1) * 64] = dq2[tq:].astype(BF16)
            dkt[g * 64:(g + 1) * 64, :] += lax.dot_general(q2, ds, TN, preferred_element_type=F32)
            dvt[g * 64:(g + 1) * 64, :] += lax.dot_general(do2, p.astype(BF16), TN, preferred_element_type=F32)

        @pl.when(pl.program_id(1) == nq - 1)
        def _():
            dk_ref[...] = dkt[...].T.astype(BF16)
            dv_ref[...] = dvt[...].T.astype(BF16)

    q3 = qd.reshape(bl, SEQ, 512)
    tile = pl.BlockSpec((None, tq, 256), lambda b, i: (b, i, 0))
    full = pl.BlockSpec((None, SEQ, 128), lambda b, i: (b, 0, 0))
    dq, dk, dv = pl.pallas_call(
        body, name=name, grid=(bl, nq),
        in_specs=[tile, pl.BlockSpec((None, SEQ, 128), lambda b, i: (b, 0, 2)),
                  pl.BlockSpec((None, SEQ, 128), lambda b, i: (b, 0, 3)),
                  pl.BlockSpec((None, tq, 256), lambda b, i: (b, i, 3)), tile, tile],
        out_specs=[tile, full, full],
        out_shape=[_sds((bl, SEQ, 256), BF16), _sds((bl, SEQ, 128), BF16), _sds((bl, SEQ, 128), BF16)],
        scratch_shapes=[pltpu.VMEM((128, SEQ), F32), pltpu.VMEM((128, SEQ), F32)],
        compiler_params=_cparams(PAR, ARB),
    )(q3, q3, q3, dycat.reshape(bl, SEQ, 1024), lse.reshape(bl, SEQ, 256), delta.reshape(bl, SEQ, 256))
    return dq.reshape(t, 256), dk.reshape(t, 128), dv.reshape(t, 128)


def _c_norm(cv, gam, bet):
    vg = _gelu(cv)
    mu = jnp.mean(vg, axis=-1, keepdims=True)
    xc = vg - mu
    r = lax.rsqrt(jnp.mean(xc * xc, axis=-1, keepdims=True) + EPS)
    xhat = xc * r
    return xhat * gam + bet, xhat, r


def _c_fwd(proj, gam, bet, ws, bst, *, tm, name):
    t = proj.shape[0]
    nch = tm // C_CHUNK

    def body(u_ref, v_ref, g_ref, b_ref, ws_ref, bs_ref, y_ref):
        vn, _, _ = _c_norm(v_ref[...].astype(F32), g_ref[...], b_ref[...])
        vnb = vn.astype(BF16)
        for c in range(nch):
            rows = slice(c * C_CHUNK, (c + 1) * C_CHUNK)
            for g in range(C_GROUPS):
                gs = slice(g * 64, (g + 1) * 64)
                mixed = jnp.dot(ws_ref[g], vnb[rows, gs], preferred_element_type=F32) + bs_ref[:, gs]
                y_ref[rows, gs] = (_gelu(u_ref[rows, gs].astype(F32)) * mixed).astype(BF16)

    vec = pl.BlockSpec((1, 256), lambda i: (0, 0))
    return pl.pallas_call(
        body, name=name, grid=(t // tm,),
        in_specs=[pl.BlockSpec((tm, 256), lambda i: (i, 5)), pl.BlockSpec((tm, 256), lambda i: (i, 6)), vec, vec,
                  pl.BlockSpec((C_GROUPS, C_CHUNK, C_CHUNK), lambda i: (0, 0, 0)),
                  pl.BlockSpec((C_CHUNK, 256), lambda i: (0, 0))],
        out_specs=pl.BlockSpec((tm, 256), lambda i: (i, 0)), out_shape=_sds((t, 256), BF16),
        compiler_params=_cparams(PAR),
    )(proj, proj, gam, bet, ws, bst)


def _c_bwd(proj, dycat, gam, bet, ws, wst, bst, *, tm, name):
    t = proj.shape[0]
    nch = tm // C_CHUNK
    nstep = t // tm

    def body(u_ref, v_ref, dy_ref, g_ref, b_ref, ws_ref, wst_ref, bs_ref,
             du_ref, dv_ref, dws_ref, dbs_ref, dg_ref, db_ref, dvn_s):
        step = pl.program_id(0)

        @pl.when(step == 0)
        def _():
            dws_ref[...] = jnp.zeros_like(dws_ref)
            dbs_ref[...] = jnp.zeros_like(dbs_ref)
            dg_ref[...] = jnp.zeros_like(dg_ref)
            db_ref[...] = jnp.zeros_like(db_ref)

        cv = v_ref[...].astype(F32)
        gam_v = g_ref[...]
        vn, xhat, r = _c_norm(cv, gam_v, b_ref[...])
        vnb = vn.astype(BF16)
        for c in range(nch):
            rows = slice(c * C_CHUNK, (c + 1) * C_CHUNK)
            for g in range(C_GROUPS):
                gs = slice(g * 64, (g + 1) * 64)
                cu = u_ref[rows, gs].astype(F32)
                dy = dy_ref[rows, gs]
                mixed = jnp.dot(ws_ref[g], vnb[rows, gs], preferred_element_type=F32) + bs_ref[:, gs]
                du_ref[rows, gs] = (dy * mixed * _gelu_grad(cu)).astype(BF16)
                dmix = dy * _gelu(cu)
                dbs_ref[:, gs] += dmix
                dmb = dmix.astype(BF16)
                dws_ref[g] += lax.dot_general(dmb, vnb[rows, gs], NT, preferred_element_type=F32)
                dvn_s[rows, gs] = jnp.dot(wst_ref[g], dmb, preferred_element_type=F32)
        dvn = dvn_s[...]
        dg_ref[...] += jnp.sum(dvn * xhat, axis=0, keepdims=True)
        db_ref[...] += jnp.sum(dvn, axis=0, keepdims=True)
        dxh = dvn * gam_v
        dvg = r * (dxh - jnp.mean(dxh, axis=-1, keepdims=True) - xhat * jnp.mean(dxh * xhat, axis=-1, keepdims=True))
        dv_ref[...] = (dvg * _gelu_grad(cv)).astype(BF16)

        @pl.when(step == nstep - 1)
        def _():
            dbs_ref[...] = _seg_sum(dbs_ref[...], _group_sum_matrix(256, True))

    vec = pl.BlockSpec((1, 256), lambda i: (0, 0))
    mat = pl.BlockSpec((C_GROUPS, C_CHUNK, C_CHUNK), lambda i: (0, 0, 0))
    bsp = pl.BlockSpec((C_CHUNK, 256), lambda i: (0, 0))
    tile = pl.BlockSpec((tm, 256), lambda i: (i, 0))
    return pl.pallas_call(
        body, name=name, grid=(nstep,),
        in_specs=[pl.BlockSpec((tm, 256), lambda i: (i, 5)), pl.BlockSpec((tm, 256), lambda i: (i, 6)),
                  pl.BlockSpec((tm, 256), lambda i: (i, 2)), vec, vec, mat, mat, bsp],
        out_specs=[tile, tile, mat, bsp, vec, vec],
        out_shape=[_sds((t, 256), BF16), _sds((t, 256), BF16), _sds((C_GROUPS, C_CHUNK, C_CHUNK), F32),
                   _sds((C_CHUNK, 256), F32), _sds((1, 256), F32), _sds((1, 256), F32)],
        scratch_shapes=[pltpu.VMEM((tm, 256), F32)],
        compiler_params=_cparams(ARB),
    )(proj, proj, dycat, gam, bet, ws, wst, bst)


FF_TC = 128
FF_NB = D_FF // FF_TC
FF_CH = 64
FF_HALO = 16


def _taps(ref, r0, win, where):
    z = jnp.zeros((FF_HALO, win.shape[1]), F32)
    if where == "first":
        win[0:FF_HALO, :] = z
        win[FF_HALO:, :] = ref[0:FF_CH + FF_HALO, :].astype(F32)
    elif where == "last":
        win[0:FF_CH + FF_HALO, :] = ref[SEQ - FF_CH - FF_HALO:SEQ, :].astype(F32)
        win[FF_CH + FF_HALO:, :] = z
    else:
        win[...] = ref[pl.ds(pl.multiple_of(r0 - FF_HALO, FF_HALO), FF_CH + 2 * FF_HALO), :].astype(F32)
    return tuple(win[FF_HALO + o:FF_HALO + o + FF_CH, :] for o in (-1, 0, 1))


def _chunk_loop(step):
    step(0, lambda ref, win: _taps(ref, 0, win, "first"))

    def mid(i, carry):
        r0 = pl.multiple_of(i * FF_CH, FF_CH)
        step(r0, lambda ref, win: _taps(ref, r0, win, "mid"))
        return carry

    lax.fori_loop(1, SEQ // FF_CH - 1, mid, 0)
    step(SEQ - FF_CH, lambda ref, win: _taps(ref, SEQ - FF_CH, win, "last"))


def _conv3(taps, w_ref, b_ref):
    dn, md, up = taps
    return w_ref[0:1, :] * dn + w_ref[1:2, :] * md + w_ref[2:3, :] * up + b_ref[...]


def _ff_specs(order):
    def at(fn):
        return (lambda b, j: fn(b, j)) if order == "bj" else (lambda j, b: fn(b, j))
    hs = [pl.BlockSpec((None, SEQ, FF_TC), at(lambda b, j, o=o: (b, 0, j + o))) for o in (0, FF_NB)]
    ws = [pl.BlockSpec((3, FF_TC), at(lambda b, j, o=o: (0, j + o))) for o in (0, FF_NB)]
    bs = [pl.BlockSpec((1, FF_TC), at(lambda b, j, o=o: (0, j + o))) for o in (0, FF_NB)]
    return hs, ws, bs


def _conv_gate_fwd(h, cw, cb, *, name):
    t = h.shape[0]
    bl = t // SEQ

    def body(hg_ref, hu_ref, wg_ref, wu_ref, bg_ref, bu_ref, a_ref, cg_ref, cu_ref, win):
        def step(r0, taps):
            cg = _conv3(taps(hg_ref, win.at[0]), wg_ref, bg_ref)
            cu = _conv3(taps(hu_ref, win.at[1]), wu_ref, bu_ref)
            a_ref[pl.ds(r0, FF_CH), :] = (cg * _sigmoid(cg) * cu).astype(BF16)
            cg_ref[pl.ds(r0, FF_CH), :] = cg.astype(BF16)
            cu_ref[pl.ds(r0, FF_CH), :] = cu.astype(BF16)

        _chunk_loop(step)

    hs, ws, bs = _ff_specs("bj")
    h3 = h.reshape(bl, SEQ, 2 * D_FF)
    half = pl.BlockSpec((None, SEQ, FF_TC), lambda b, j: (b, 0, j))
    outs = pl.pallas_call(
        body, name=name, grid=(bl, FF_NB), in_specs=hs + ws + bs, out_specs=[half] * 3,
        out_shape=[_sds((bl, SEQ, D_FF), BF16)] * 3,
        scratch_shapes=[pltpu.VMEM((2, FF_CH + 2 * FF_HALO, FF_TC), F32)],
        compiler_params=_cparams(PAR, PAR),
    )(h3, h3, cw, cw, cb, cb)
    return [o.reshape(t, D_FF) for o in outs]


def _conv_gate_bwd(h, cg_all, cu_all, dact, cw, cb, *, name):
    t = h.shape[0]
    bl = t // SEQ

    def body(hg_ref, hu_ref, wg_ref, wu_ref, bg_ref, bu_ref, da_ref, cg_ref, cu_ref,
             dhg_ref, dhu_ref, dwg_ref, dwu_ref, dbg_ref, dbu_ref, dg_s, du_s, win, sums):
        @pl.when(pl.program_id(1) == 0)
        def _():
            for ref in (dwg_ref, dwu_ref, dbg_ref, dbu_ref):
                ref[...] = jnp.zeros_like(ref)

        sums[...] = jnp.zeros_like(sums)
        red = lambda x: jnp.sum(x.reshape(FF_CH // 8, 8, x.shape[1]), axis=0)

        def pass1(r0, taps):
            tg, tu = taps(hg_ref, win.at[0]), taps(hu_ref, win.at[1])
            cg = cg_ref[pl.ds(r0, FF_CH), :].astype(F32)
            cu = cu_ref[pl.ds(r0, FF_CH), :].astype(F32)
            da = da_ref[pl.ds(r0, FF_CH), :].astype(F32)
            sg = _sigmoid(cg)
            dcg = da * cu * (sg * (1.0 + cg * (1.0 - sg)))
            dcu = da * (cg * sg)
            dg_s[pl.ds(r0, FF_CH), :] = dcg
            du_s[pl.ds(r0, FF_CH), :] = dcu
            for half, (d, tp) in enumerate(((dcg, tg), (dcu, tu))):
                for k in range(3):
                    sums[4 * half + k] += red(d * tp[k])
                sums[4 * half + 3] += red(d)

        _chunk_loop(pass1)
        for half, (dw_ref, db_ref) in enumerate(((dwg_ref, dbg_ref), (dwu_ref, dbu_ref))):
            for k in range(3):
                dw_ref[k:k + 1, :] += jnp.sum(sums[4 * half + k], axis=0, keepdims=True)
            db_ref[...] += jnp.sum(sums[4 * half + 3], axis=0, keepdims=True)

        def pass2(r0, taps):
            for k, (s, w_ref, o_ref) in enumerate(((dg_s, wg_ref, dhg_ref), (du_s, wu_ref, dhu_ref))):
                dn, md, up = taps(s, win.at[k])
                o_ref[pl.ds(r0, FF_CH), :] = (w_ref[0:1, :] * up + w_ref[1:2, :] * md + w_ref[2:3, :] * dn).astype(BF16)

        _chunk_loop(pass2)

    hs, ws, bs = _ff_specs("jb")
    half = pl.BlockSpec((None, SEQ, FF_TC), lambda j, b: (b, 0, j))
    wsp = pl.BlockSpec((3, FF_TC), lambda j, b: (0, j))
    bsp = pl.BlockSpec((1, FF_TC), lambda j, b: (0, j))
    h3 = h.reshape(bl, SEQ, 2 * D_FF)
    dhg, dhu, dwg, dwu, dbg, dbu = pl.pallas_call(
        body, name=name, grid=(FF_NB, bl), in_specs=hs + ws + bs + [half] * 3,
        out_specs=[half, half, wsp, wsp, bsp, bsp],
        out_shape=[_sds((bl, SEQ, D_FF), BF16), _sds((bl, SEQ, D_FF), BF16), _sds((3, D_FF), F32), _sds((3, D_FF), F32),
                   _sds((1, D_FF), F32), _sds((1, D_FF), F32)],
        scratch_shapes=[pltpu.VMEM((SEQ, FF_TC), F32), pltpu.VMEM((SEQ, FF_TC), F32),
                        pltpu.VMEM((2, FF_CH + 2 * FF_HALO, FF_TC), F32), pltpu.VMEM((8, 8, FF_TC), F32)],
        compiler_params=_cparams(PAR, ARB),
    )(h3, h3, cw, cw, cb, cb, *[a.reshape(bl, SEQ, D_FF) for a in (dact, cg_all, cu_all)])
    return (dhg.reshape(t, D_FF), dhu.reshape(t, D_FF), jnp.concatenate([dwg, dwu], axis=1),
            jnp.concatenate([dbg, dbu], axis=1))


def _ple_fwd(x2, gain, wg, pe, pe_blk, wp, *, tm, name):
    t, k = x2.shape

    def body(x_ref, g_ref, wg_ref, pe_ref, wp_ref, hn_ref, x3_ref, gt_ref, pp_ref):
        x = x_ref[...]
        r = lax.rsqrt(jnp.mean(x * x, axis=-1, keepdims=True) + EPS)
        hn = (x * r * g_ref[...]).astype(BF16)
        hn_ref[...] = hn
        gate = _sigmoid(jnp.dot(hn, wg_ref[...], preferred_element_type=F32))
        pp = jnp.dot(pe_ref[...].astype(BF16), wp_ref[...], preferred_element_type=F32)
        gt_ref[...] = gate.astype(BF16)
        pp_ref[...] = pp.astype(BF16)
        x3_ref[...] = x + pp * gate

    row = pl.BlockSpec((tm, k), lambda i: (i, 0))
    return pl.pallas_call(
        body, name=name, grid=(t // tm,),
        in_specs=[row, pl.BlockSpec((1, k), lambda i: (0, 0)), pl.BlockSpec((k, k), lambda i: (0, 0)),
                  pl.BlockSpec((tm, PLE_DIM), lambda i: (pe_blk + i, 0)), pl.BlockSpec((PLE_DIM, k), lambda i: (0, 0))],
        out_specs=[row, row, row, row],
        out_shape=[_sds((t, k), BF16), _sds((t, k), F32), _sds((t, k), BF16), _sds((t, k), BF16)],
        compiler_params=_cparams(PAR),
    )(x2, gain, wg, pe, wp)


def _ple_bwd_ew(dx3, gate, pp, *, tm, name):
    t, n = dx3.shape

    def body(d_ref, g_ref, p_ref, dz_ref, dpp_ref):
        d, g = d_ref[...], g_ref[...]
        dz_ref[...] = (d * p_ref[...] * g * (1.0 - g)).astype(BF16)
        dpp_ref[...] = (d * g).astype(BF16)

    spec = pl.BlockSpec((tm, n), lambda i: (i, 0))
    return pl.pallas_call(
        body, name=name, grid=(t // tm,), in_specs=[spec] * 3, out_specs=[spec] * 2,
        out_shape=[_sds((t, n), BF16)] * 2, compiler_params=_cparams(PAR),
    )(dx3, gate, pp)


def _loss_head(y, tgt, *, tm, name):
    t, d = y.shape

    def body(y_ref, t_ref, l_ref, dy_ref):
        @pl.when(pl.program_id(0) == 0)
        def _():
            l_ref[...] = jnp.zeros_like(l_ref)

        e = y_ref[...] - t_ref[...]
        dy_ref[...] = e * (1.0 / d)
        s = jnp.sum(jnp.sum(e * e, axis=1, keepdims=True), axis=0, keepdims=True)
        l_ref[...] += jnp.broadcast_to(s * (0.5 / d), (8, 128))

    spec = pl.BlockSpec((tm, d), lambda i: (i, 0))
    return pl.pallas_call(
        body, name=name, grid=(t // tm,), in_specs=[spec, spec],
        out_specs=[pl.BlockSpec((8, 128), lambda i: (0, 0)), spec],
        out_shape=[_sds((8, 128), F32), _sds((t, d), F32)], compiler_params=_cparams(ARB),
    )(y, tgt)


BIAS_PC = 8192


def _onehot(bucket_row):
    rows = lax.broadcasted_iota(jnp.int32, (REL_BUCKETS, bucket_row.shape[1]), 0)
    return (rows == bucket_row).astype(BF16)


def _dot3(x, onehot, dims):
    acc = None
    for _ in range(3):
        term = x.astype(BF16)
        part = lax.dot_general(term, onehot, dims, preferred_element_type=F32)
        acc = part if acc is None else acc + part
        x = x - term.astype(F32)
    return acc


def _bias_lookup(table_t, bucket, *, name):
    h = table_t.shape[0]
    p = bucket.shape[1]

    def body(t_ref, b_ref, o_ref):
        bk = b_ref[...]
        val = _dot3(t_ref[...], _onehot(bk), (((1,), (0,)), ((), ())))
        o_ref[...] = jnp.where(bk >= 0, val, NEG_INF)

    return pl.pallas_call(
        body, name=name, grid=(p // BIAS_PC,),
        in_specs=[pl.BlockSpec((h, REL_BUCKETS), lambda i: (0, 0)), pl.BlockSpec((1, BIAS_PC), lambda i: (0, i))],
        out_specs=pl.BlockSpec((h, BIAS_PC), lambda i: (0, i)), out_shape=_sds((h, p), F32),
        compiler_params=_cparams(PAR),
    )(table_t, bucket)


def _bucket_reduce(dbiases, bucket, *, name):
    h, p = dbiases[0].shape
    nl = len(dbiases)

    def body(*refs):
        b_ref, o_ref = refs[nl], refs[nl + 1]

        @pl.when(pl.program_id(0) == 0)
        def _():
            o_ref[...] = jnp.zeros_like(o_ref)

        d = refs[0][...]
        for d_ref in refs[1:nl]:
            d = d + d_ref[...]
        o_ref[...] += _dot3(d, _onehot(b_ref[...]), NT)

    return pl.pallas_call(
        body, name=name, grid=(p // BIAS_PC,),
        in_specs=[pl.BlockSpec((h, BIAS_PC), lambda i: (0, i))] * nl + [pl.BlockSpec((1, BIAS_PC), lambda i: (0, i))],
        out_specs=pl.BlockSpec((h, REL_BUCKETS), lambda i: (0, 0)), out_shape=_sds((h, REL_BUCKETS), F32),
        compiler_params=_cparams(ARB),
    )(*dbiases, bucket)


def _adamw_math(w, g, m, v):
    m = ADAM_B1 * m + (1.0 - ADAM_B1) * g
    v = ADAM_B2 * v + (1.0 - ADAM_B2) * (g * g)
    m_hat = m / (1.0 - ADAM_B1 ** ADAM_STEP)
    v_hat = v / (1.0 - ADAM_B2 ** ADAM_STEP)
    delta = -ADAM_LR * (m_hat / (jnp.sqrt(v_hat) + ADAM_EPS) + ADAM_WD * w)
    return delta, m, v


def _adamw_reduce(parts, w, m, v, *, tr, name):
    nl = len(parts)
    rows, c = w.shape
    r = rows // nl
    nt = r // tr

    def body(*refs):
        p_refs = refs[:nl]
        w_ref, m_ref, v_ref, g_ref, d_ref, nm_ref, nv_ref = refs[nl:]
        for li, p_ref in enumerate(p_refs):
            @pl.when(pl.program_id(0) == li)
            def _(p_ref=p_ref):
                g = p_ref[0].astype(F32)
                for k in range(1, N_DEV):
                    g = g + p_ref[k].astype(F32)
                d, nm, nv = _adamw_math(w_ref[...], g, m_ref[...], v_ref[...])
                g_ref[...] = g
                d_ref[...] = d
                nm_ref[...] = nm
                nv_ref[...] = nv

    def part_map(li):
        return lambda l, i: (0, jnp.where(l == li, i, jnp.where(l < li, 0, nt - 1)), 0)

    spec = pl.BlockSpec((tr, c), lambda l, i: (l * nt + i, 0))
    return pl.pallas_call(
        body, name=name, grid=(nl, nt),
        in_specs=[pl.BlockSpec((N_DEV, tr, c), part_map(li)) for li in range(nl)] + [spec, spec, spec],
        out_specs=[spec] * 4, out_shape=[_sds((rows, c), F32)] * 4, compiler_params=_cparams(ARB, ARB),
    )(*parts, w, m, v)


def _adamw_plain(g, w, m, v, *, name):
    def body(g_ref, w_ref, m_ref, v_ref, d_ref, nm_ref, nv_ref):
        d, nm, nv = _adamw_math(w_ref[...], g_ref[...], m_ref[...], v_ref[...])
        d_ref[...] = d
        nm_ref[...] = nm
        nv_ref[...] = nv

    return pl.pallas_call(body, name=name, out_shape=[_sds(w.shape, F32)] * 3)(g, w, m, v)


def _mesh_pos():
    return lax.axis_index("x"), lax.axis_index("y"), lax.axis_index("c")


def _allgather_body(x_refs, out_refs, send_sems, recv_sems, local_sems, slot):
    x, y, c = _mesh_pos()
    me, sibling = (x, y, c), (x, y, 1 - c)
    chips = [(1 - x, y), (x, 1 - y), (1 - x, 1 - y)]
    waits = []
    for a, (x_ref, out_ref) in enumerate(zip(x_refs, out_refs)):
        def copy(k, block, to, src=None, out_ref=out_ref, a=a):
            return pltpu.make_async_remote_copy(
                src_ref=slot(out_ref, block) if src is None else src, dst_ref=slot(out_ref, block),
                send_sem=send_sems.at[a, k], recv_sem=recv_sems.at[a, k], device_id=to, device_id_type=MESH)

        mine = pltpu.make_async_copy(x_ref, slot(out_ref, me), local_sems.at[a])
        mine.start()
        first = [copy(0, me, sibling, src=x_ref)]
        first += [copy(1 + j, me, (*chip, c), src=x_ref) for j, chip in enumerate(chips)]
        for cp in first:
            cp.start()
        waits.append((copy, mine, first))
    sends = []
    for copy, mine, first in waits:
        passed = [copy(4 + j, (*chip, c), sibling) for j, chip in enumerate(chips)]
        for j, chip in enumerate(chips):
            copy(1 + j, (*chip, c), me).wait_recv()
            passed[j].start()
        sends.append(passed)
    for (copy, mine, first), passed in zip(waits, sends):
        copy(0, sibling, me).wait_recv()
        for j, chip in enumerate(chips):
            copy(4 + j, (*chip, 1 - c), me).wait_recv()
        for cp in first + passed:
            cp.wait_send()
        mine.wait()


PEER_FLIPS = ((0, 0, 1), (1, 0, 0), (0, 1, 0), (1, 1, 0), (1, 0, 1), (0, 1, 1), (1, 1, 1))


def _peer_copies(x_refs, land_refs, send_sem, recv_sem, scatter):
    x, y, c = _mesh_pos()
    me = 4 * x + 2 * y + c
    copies = []
    for x_ref, land_ref in zip(x_refs, land_refs):
        for fx, fy, fc in PEER_FLIPS:
            px, py, pc = x ^ fx, y ^ fy, c ^ fc
            src = x_ref.at[4 * px + 2 * py + pc] if scatter else x_ref
            copies.append(pltpu.make_async_remote_copy(
                src_ref=src, dst_ref=land_ref.at[me], send_sem=send_sem, recv_sem=recv_sem,
                device_id=(px, py, pc), device_id_type=MESH))
    return copies


def _sc_exchange(xs, *, scatter, collective_id, name):
    na = len(xs)
    land_shapes = [x.shape if scatter else (N_DEV,) + x.shape for x in xs]

    def body(*refs):
        x_refs, land_refs = refs[:na], refs[na:2 * na]
        send_sem, recv_sem, local_sem = refs[2 * na:]
        x, y, c = _mesh_pos()
        me = 4 * x + 2 * y + c
        barrier = pltpu.get_barrier_semaphore()
        for fx, fy, fc in PEER_FLIPS:
            pl.semaphore_signal(barrier, inc=1, device_id=(x ^ fx, y ^ fy, c ^ fc), device_id_type=MESH)
        pl.semaphore_wait(barrier, len(PEER_FLIPS))
        for x_ref, land_ref in zip(x_refs, land_refs):
            own = pltpu.make_async_copy(x_ref.at[me] if scatter else x_ref, land_ref.at[me], local_sem)
            own.start()
            own.wait()
        copies = _peer_copies(x_refs, land_refs, send_sem, recv_sem, scatter)
        for cp in copies:
            cp.start()
        for cp in copies:
            cp.wait()

    return pl.kernel(
        body, name=name, out_type=[_sds(s, x.dtype) for s, x in zip(land_shapes, xs)],
        mesh=plsc.ScalarSubcoreMesh(axis_name="sequencer", num_cores=1),
        scratch_types=[pltpu.SemaphoreType.DMA, pltpu.SemaphoreType.DMA, pltpu.SemaphoreType.DMA],
        compiler_params=pltpu.CompilerParams(collective_id=collective_id),
    )(*xs)


def _sc_allgather(xs, *, collective_id, name):
    na = len(xs)

    def body(*refs):
        x_refs, out_refs = refs[:na], refs[na:2 * na]
        send_sems, recv_sems, local_sems = refs[2 * na:]
        x, y, c = _mesh_pos()
        barrier = pltpu.get_barrier_semaphore()
        for fx, fy, fc in PEER_FLIPS:
            pl.semaphore_signal(barrier, inc=1, device_id=(x ^ fx, y ^ fy, c ^ fc), device_id_type=MESH)
        pl.semaphore_wait(barrier, len(PEER_FLIPS))
        _allgather_body(x_refs, out_refs, send_sems, recv_sems, local_sems,
                        lambda ref, pos: ref.at[4 * pos[0] + 2 * pos[1] + pos[2]])

    return pl.kernel(
        body, name=name, out_type=[_sds((N_DEV,) + x.shape, x.dtype) for x in xs],
        mesh=plsc.ScalarSubcoreMesh(axis_name="sequencer", num_cores=1),
        scratch_types=[pltpu.SemaphoreType.DMA((na, 7)), pltpu.SemaphoreType.DMA((na, 7)),
                       pltpu.SemaphoreType.DMA((na,))],
        compiler_params=pltpu.CompilerParams(collective_id=collective_id),
    )(*xs)


def _allgather_vmem(x, *, name):
    r, c = x.shape

    def body(x_ref, out_ref, send_sems, recv_sems, local_sems):
        _allgather_body([x_ref], [out_ref], send_sems, recv_sems, local_sems,
                        lambda ref, pos: ref.at[pl.ds((4 * pos[0] + 2 * pos[1] + pos[2]) * r, r), :])

    vm = pl.BlockSpec(memory_space=pltpu.VMEM)
    return pl.pallas_call(
        body, name=name, in_specs=[vm], out_specs=vm, out_shape=_sds((N_DEV * r, c), x.dtype),
        scratch_shapes=[pltpu.SemaphoreType.DMA((1, 7)), pltpu.SemaphoreType.DMA((1, 7)),
                        pltpu.SemaphoreType.DMA((1,))],
    )(x)


def _sum_slots(gathered, *, name):
    _, r, c = gathered.shape

    def body(g_ref, o_ref):
        acc = g_ref[0]
        for k in range(1, N_DEV):
            acc = acc + g_ref[k]
        o_ref[...] = acc

    return pl.pallas_call(body, name=name, out_shape=_sds((r, c), gathered.dtype))(gathered)


def _t5_bucket(rel):
    nb = REL_BUCKETS // 2
    ret = jnp.where(rel > 0, nb, 0)
    n = jnp.abs(rel)
    max_exact = nb // 2
    nf = jnp.maximum(n, 1).astype(F32)
    large = max_exact + (jnp.log(nf / max_exact) / math.log(REL_MAX_DIST / max_exact)
                         * (nb - max_exact)).astype(jnp.int32)
    large = jnp.minimum(large, nb - 1)
    return ret + jnp.where(n < max_exact, n, large)


def _band_pattern(block, radius, dil):
    kw = block + 2 * radius
    rel = jnp.arange(kw)[None, :] - radius - jnp.arange(block)[:, None]
    return jnp.where(jnp.abs(rel) <= radius, _t5_bucket(rel * dil), -1).astype(jnp.int32).reshape(1, block * kw)


def _rope_tables():
    lane = np.arange(64)
    seg, j = lane // 32, lane % 32
    inv = ROPE_THETA ** (-jnp.arange(0, 32, 2, dtype=F32) / 32)
    tpos = jnp.arange(SEQ)
    pos = jnp.where(jnp.asarray(seg)[None, :] == 0, (tpos // GRID_W)[:, None], (tpos % GRID_W)[:, None])
    ang = pos.astype(F32) * inv[jnp.asarray(j % 16)][None, :]
    cos = jnp.cos(ang)
    sins = jnp.where(jnp.asarray(j)[None, :] < 16, -jnp.sin(ang), jnp.sin(ang))
    return jnp.tile(cos, (1, 4)), jnp.tile(sins, (1, 4))


A_Q, A_K, A_V = (256, 0), (256, 1), (256, 2)
B_Q, B_K, B_V = (256, 0), (128, 2), (128, 3)
A_HEADS = dict(rad=A_RADIUS, nh=4, nkv=4)
B_HEADS = dict(rad=SWA_RADIUS, nh=4, nkv=2)


def _local_step(x, pe, tgt, rel_bias, wts, matmul_weights, grads_ready):
    t = x.shape[0]
    bl = t // SEQ
    cos, sins = _rope_tables()
    blocks_a = [min(BAND_BLOCK, SEQ // d) for d in DILATIONS]
    pats_a = [_band_pattern(blk, A_RADIUS, d) for blk, d in zip(blocks_a, DILATIONS)]
    pat_b = _band_pattern(BAND_BLOCK, SWA_RADIUS, 1)
    table_t = rel_bias.T
    bias_a = [_bias_lookup(table_t[:4], pt, name=f"bias_a{ci}").reshape(4, blk, blk + 2 * A_RADIUS)
              for ci, (pt, blk) in enumerate(zip(pats_a, blocks_a))]
    bias_b = _bias_lookup(table_t[4:], pat_b, name="bias_b").reshape(4, BAND_BLOCK, BAND_BLOCK + 2 * SWA_RADIUS)
    nat4 = lambda a: a.reshape(bl, 1, SEQ, a.shape[-1])

    saved = []
    for li in range(DEPTH):
        w = dict(wts[li])
        w.update(matmul_weights(li, "in", x))
        hn0, proj = _norm_mm((x,), w["g_mix"], w["w_in"], None, tm=1024, tn=768, name="mix_in_fwd", out_dtype=BF16)
        qa1, qa4, qa16, qb, qd = _qkprep_fwd(proj, w["qk_gains"], cos, sins, tm=512, name="qkprep_fwd")
        qa = (nat4(qa1), qa4, qa16)
        oa, la = [], []
        for ci in range(3):
            o, l = _band_fwd(qa[ci], A_Q, A_K, A_V, bias_a[ci], None, name=f"band_a{ci}_fwd", **A_HEADS)
            oa.append(o)
            la.append(l)
        oa[0], la[0] = oa[0].reshape(t, 256), la[0].reshape(t, 256)
        ya, lse_a = _combine_a(oa, la, tm=512, name="combine_a")
        yb, lse_b = _band_fwd(nat4(qb), B_Q, B_K, B_V, bias_b, w["sink_t"], name="band_b_fwd", **B_HEADS)
        yb = yb.reshape(t, 256)
        yc = _c_fwd(proj, w["c_g"], w["c_b"], w["c_ws"], w["c_bst"], tm=512, name="c_fwd")
        yd, lse_d = _dense_fwd(qd, tq=256, name="dense_fwd")
        w.update(matmul_weights(li, "rest", yd))
        mixed, x1 = _norm_mm((ya, yb, yc, yd), w["out_gain"], w["w_out"], x, tm=1024, tn=1024, name="mix_out_fwd")
        hn1, h = _norm_mm((x1,), w["g_ffn"], w["w_up"], None, tm=1024, tn=2816, name="ffn_up_fwd", out_dtype=BF16)
        act, cg, cu = _conv_gate_fwd(h, w["conv_w"], w["conv_b"], name="conv_gate_fwd")
        x2 = _mm(act, w["w_down"], "nn", x1, tm=1024, tn=1024, out_dtype=F32, name="ffn_down_fwd")
        hn2, x3, gate, pp = _ple_fwd(x2, w["g_ple"], w["w_gate"], pe, li * (t // 1024), w["w_proj"], tm=1024,
                                     name="ple_fwd")
        saved.append(dict(w=w, x0=x, hn0=hn0, proj=proj, qa=qa, qb=qb, qd=qd, ya=ya, lse_a=lse_a, yb=yb, lse_b=lse_b,
                          yc=yc, yd=yd, lse_d=lse_d, mixed=mixed, x1=x1, hn1=hn1, h=h, cg=cg, cu=cu, act=act, x2=x2, hn2=hn2,
                          gate=gate, pp=pp))
        x = x3

    loss_tile, dx = _loss_head(x, tgt, tm=512, name="loss_head")
    grads = [None] * DEPTH
    dbias_a, dbias_bs = [[], [], []], []
    for li in reversed(range(DEPTH)):
        s = saved[li]
        w = s["w"]
        g = {}
        dz, dpp = _ple_bwd_ew(dx, s["gate"], s["pp"], tm=512, name="ple_bwd_ew")
        g["w_gate"] = _mm(s["hn2"], dz, "tn", None, tm=1024, tn=512, out_dtype=BF16, name="dw_gate")
        g["w_proj"] = _mm(pe, dpp, "tn", None, tm=256, tn=1024, out_dtype=BF16, name="dw_proj", a_rows=(li, t))
        dx2, dx2b, g["g_ple"] = _mm_bt_normbwd((dz,), w["w_gate"], (s["x2"],), w["g_ple"], dx, tm=1024, tn=1024,
                                               name="ple_bwd", emit_bf16=True)
        g["w_down"] = _mm(s["act"], dx2b, "tn", None, tm=1408, tn=512, out_dtype=BF16, name="dw_down")
        dact = _mm(dx2b, w["w_down"], "nt", None, tm=1024, tn=2816, out_dtype=BF16, name="ffn_down_bwd")
        dhg, dhu, g["conv_w"], g["conv_b"] = _conv_gate_bwd(s["h"], s["cg"], s["cu"], dact, w["conv_w"], w["conv_b"],
                                                            name="conv_gate_bwd")
        g["w_up"] = jnp.concatenate(
            [_mm(s["hn1"], dhalf, "tn", None, tm=1024, tn=1408, out_dtype=BF16, name=f"dw_up_{nm}")
             for nm, dhalf in (("gate", dhg), ("up", dhu))], axis=1)
        dx1, dx1b, g["g_ffn"] = _mm_bt_normbwd((dhg, dhu), w["w_up"], (s["x1"],), w["g_ffn"], dx2, tm=1024, tn=1408,
                                               name="ffn_up_bwd", emit_bf16=True)
        g["w_out"] = _mm(s["mixed"], dx1b, "tn", None, tm=1024, tn=512, out_dtype=BF16, name="dw_out")
        grads_ready(li, "mid", g)
        dycat, g["out_gain"] = _mm_bt_normbwd((dx1b,), w["w_out"], (s["ya"], s["yb"], s["yc"], s["yd"]), w["out_gain"],
                                              None, tm=1024, tn=1024, name="mix_out_bwd")
        dy_r, lse_r, dl_a, dl_b, dl_d = _deltas(dycat, s["ya"], s["yb"], s["yd"], s["lse_a"], tm=512, name="deltas")
        dy_a = (nat4(dycat),) + tuple(dy_r)
        lse_a = (nat4(s["lse_a"]),) + tuple(lse_r)
        dl_a = (nat4(dl_a[0]),) + tuple(dl_a[1:])
        da = []
        for ci in range(3):
            dq, dk, dv, dbias = _band_bwd(s["qa"][ci], A_Q, A_K, A_V, bias_a[ci], None, dy_a[ci], 0, lse_a[ci],
                                          dl_a[ci], name=f"band_a{ci}_bwd", **A_HEADS)
            if ci == 0:
                dq, dk, dv = (a.reshape(t, 256) for a in (dq, dk, dv))
            da.append((dq, dk, dv))
            dbias_a[ci].append(dbias.reshape(4, -1))
        dqb, dkb, dvb, dbias_b, dsink = _band_bwd(nat4(s["qb"]), B_Q, B_K, B_V, bias_b, w["sink_t"], nat4(dycat), 1,
                                                  nat4(s["lse_b"]), nat4(dl_b), name="band_b_bwd", **B_HEADS)
        dbias_bs.append(dbias_b.reshape(4, -1))
        g["sink"] = dsink[:, 0, 0]
        dd = _dense_bwd(s["qd"], dycat, s["lse_d"], dl_d, tq=128, name="dense_bwd")
        dcu, dcv, g["c_ws"], dbs, g["c_g"], g["c_b"] = _c_bwd(s["proj"], dycat, w["c_g"], w["c_b"], w["c_ws"],
                                                               w["c_wst"], w["c_bst"], tm=512, name="c_bwd")
        g["c_bs"] = dbs[:, ::64].T
        db = (dqb.reshape(t, 256), dkb.reshape(t, 128), dvb.reshape(t, 128))
        dproj, dgains = _qkprep_bwd(s["proj"], da, db, dd, dcu, dcv, w["qk_gains"], cos, sins, tm=512, name="qkprep_bwd")
        g["qk_gain"] = dgains[:6, :64].reshape(3, 2, HEAD_DIM)
        g["w_in"] = _mm(s["hn0"], dproj, "tn", None, tm=1024, tn=768, out_dtype=BF16, name="dw_in")
        dx, g["g_mix"] = _mm_bt_normbwd((dproj,), w["w_in"], (s["x0"],), w["g_mix"], dx1, tm=1024, tn=768,
                                        name="mix_in_bwd")
        grads[li] = g
        grads_ready(li, "end", g)
    d_table_a = sum(_bucket_reduce(dbias_a[ci], pats_a[ci], name=f"bucket_a{ci}") for ci in range(3))
    d_table_b = _bucket_reduce(dbias_bs, pat_b, name="bucket_b")
    d_rel_bias = jnp.concatenate([d_table_a, d_table_b], axis=0).T
    return loss_tile[0, 0], dx, grads, d_rel_bias


WEIGHT_NAMES = ("rel_bias", "ln_mix_g", "w_in", "qk_gain", "sink", "c_norm_g", "c_norm_b", "c_ws", "c_bs", "out_gain",
                "w_out", "ln_ffn_g", "w_up", "conv_w", "conv_b", "w_down", "ln_ple_g", "w_ple_gate", "w_ple_proj")
COL_SHARDED = ("w_in", "w_up", "w_ple_proj")
ROW_SHARDED = ("w_out", "w_down", "w_ple_gate")
SMALL_SHARDED = ("conv_w", "out_gain")
REPLICATED = tuple(n for n in WEIGHT_NAMES if n not in COL_SHARDED + ROW_SHARDED + SMALL_SHARDED)
LOCAL_GRAD_KEY = {"ln_mix_g": "g_mix", "ln_ffn_g": "g_ffn", "ln_ple_g": "g_ple", "c_norm_g": "c_g", "c_norm_b": "c_b",
                  "w_ple_gate": "w_gate", "w_ple_proj": "w_proj"}


def _full_from_gathered(name, gathered):
    _, r, c = gathered.shape
    if name in ROW_SHARDED:
        return gathered.reshape(N_DEV * r, c)
    return jnp.transpose(gathered, (1, 0, 2)).reshape(r, N_DEV * c)


def _slots_from_full(name, full):
    rows, cols = full.shape
    if name in ROW_SHARDED:
        return full.reshape(N_DEV, rows // N_DEV, cols)
    return jnp.transpose(full.reshape(rows, N_DEV, cols // N_DEV), (1, 0, 2))


def _piece_rows(shape):
    return -(-int(np.prod(shape)) // 1024) * 8


def _pack_rows(arrays):
    pieces = []
    for a in arrays:
        n, rows = int(np.prod(a.shape)), _piece_rows(a.shape)
        flat = a.astype(F32).reshape(-1)
        if n != rows * LANES:
            flat = jnp.pad(flat, (0, rows * LANES - n))
        pieces.append(flat.reshape(rows, LANES))
    return jnp.concatenate(pieces, axis=0)


def _unpack_rows(packed, shapes):
    out, off = [], 0
    for shp in shapes:
        n, rows = int(np.prod(shp)), _piece_rows(shp)
        piece = packed[off:off + rows]
        out.append((piece if n == rows * LANES else piece.reshape(-1)[:n]).reshape(shp))
        off += rows
    return out


def kernel(x, p, rel_bias, ln_mix_g, w_in, qk_gain, sink, c_norm_g, c_norm_b, c_ws, c_bs, out_gain, w_out, ln_ffn_g, w_up, conv_w, conv_b, w_down, ln_ple_g, w_ple_gate, w_ple_proj, loss_target, m_rel_bias, m_ln_mix_g, m_w_in, m_qk_gain, m_sink, m_c_norm_g, m_c_norm_b, m_c_ws, m_c_bs, m_out_gain, m_w_out, m_ln_ffn_g, m_w_up, m_conv_w, m_conv_b, m_w_down, m_ln_ple_g, m_w_ple_gate, m_w_ple_proj, v_rel_bias, v_ln_mix_g, v_w_in, v_qk_gain, v_sink, v_c_norm_g, v_c_norm_b, v_c_ws, v_c_bs, v_out_gain, v_w_out, v_ln_ffn_g, v_w_up, v_conv_w, v_conv_b, v_w_down, v_ln_ple_g, v_w_ple_gate, v_w_ple_proj):
    env = dict(locals())
    wt = {n: env[n] for n in WEIGHT_NAMES}
    mom_m = {n: env["m_" + n] for n in WEIGHT_NAMES}
    mom_v = {n: env["v_" + n] for n in WEIGHT_NAMES}
    bl = x.shape[0]
    t = bl * SEQ
    me = 4 * lax.axis_index("x") + 2 * lax.axis_index("y") + lax.axis_index("c")

    big = COL_SHARDED + ROW_SHARDED
    full = {}
    small_shapes = [wt[n].shape for n in SMALL_SHARDED]
    small = _allgather_vmem(_pack_rows([wt[n] for n in SMALL_SHARDED]), name="gather_small")
    small = small.reshape(N_DEV, -1)
    off = 0
    for n, shp in zip(SMALL_SHARDED, small_shapes):
        cnt = int(np.prod(shp))
        g = small[:, off:off + cnt].reshape((N_DEV,) + tuple(shp))
        full[n] = jnp.transpose(g, (1, 2, 0, 3)).reshape(shp[0], shp[1], N_DEV * shp[2])
        off += _piece_rows(shp) * LANES

    def head_gain(li, a, b, reps):
        g = jnp.tile(qk_gain[li, a, b], reps)
        return jnp.pad(g, (0, 256 - g.shape[0]))

    wts = []
    for li in range(DEPTH):
        rows = [head_gain(li, 0, 0, 4), head_gain(li, 0, 1, 4), head_gain(li, 1, 0, 4), head_gain(li, 1, 1, 2),
                head_gain(li, 2, 0, 4), head_gain(li, 2, 1, 2), jnp.zeros((256,), F32), jnp.zeros((256,), F32)]
        wts.append(dict(
            g_mix=ln_mix_g[li].reshape(1, -1), qk_gains=jnp.stack(rows),
            sink_t=jnp.broadcast_to(sink[li][:, None, None], (4, 8, 128)),
            c_g=c_norm_g[li].reshape(1, -1), c_b=c_norm_b[li].reshape(1, -1), c_ws=c_ws[li].astype(BF16),
            c_wst=jnp.transpose(c_ws[li], (0, 2, 1)).astype(BF16), c_bst=jnp.repeat(c_bs[li].T, 64, axis=1),
            out_gain=full["out_gain"][li].reshape(1, -1), g_ffn=ln_ffn_g[li].reshape(1, -1),
            conv_w=full["conv_w"][li], conv_b=conv_b[li].reshape(1, -1), g_ple=ln_ple_g[li].reshape(1, -1)))

    local_key = {"w_ple_gate": "w_gate", "w_ple_proj": "w_proj"}

    gather_names = {"in": ("w_in",), "rest": tuple(n for n in big if n != "w_in")}
    gathered = {}
    for cid, (li, names) in enumerate(((0, gather_names["in"]), (0, gather_names["rest"]), (1, big))):
        lands = _sc_allgather([wt[n][li].astype(BF16) for n in names], collective_id=cid,
                              name=f"gather_{li}_{len(names)}")
        gathered.setdefault(li, {}).update(zip(names, lands))

    def matmul_weights(li, part, after):
        out = {}
        for n in gather_names[part]:
            g, _ = lax.optimization_barrier((gathered[li][n], after))
            out[local_key.get(n, n)] = _full_from_gathered(n, g)
        return out

    mid_names = ("w_ple_gate", "w_ple_proj", "w_down", "w_up", "w_out")
    end_names = ("w_in",)
    landed = {}

    def start_exchange(li, names, g, tag, cid):
        slots = [_slots_from_full(n, g[local_key.get(n, n)]) for n in names]
        lands = _sc_exchange(slots, scatter=True, collective_id=cid, name=f"grads_{li}_{tag}")
        landed.update({(n, li): land for n, land in zip(names, lands)})

    def grads_ready(li, stage, g):
        if li == 0:
            start_exchange(li, mid_names if stage == "mid" else end_names, g, stage, 5 if stage == "mid" else 6)
        elif stage == "end":
            start_exchange(li, mid_names + end_names, g, stage, 4)

    loss_part, dx, grads, d_rel_bias = _local_step(
        x.reshape(t, D_MODEL), p.reshape(DEPTH * t, PLE_DIM), loss_target.reshape(t, D_MODEL), rel_bias, wts,
        matmul_weights, grads_ready)

    def local_grad(n):
        if n == "rel_bias":
            return d_rel_bias
        key = LOCAL_GRAD_KEY.get(n, n)
        return jnp.stack([grads[li][key].reshape(wt[n].shape[1:]) if n in REPLICATED else grads[li][key]
                          for li in range(DEPTH)])

    small_names = REPLICATED + SMALL_SHARDED
    small_full_shapes = [wt[n].shape if n in REPLICATED else full[n].shape for n in small_names]
    small_parts = _allgather_vmem(_pack_rows([local_grad(n) for n in small_names] + [loss_part.reshape(1)]),
                                  name="allgather_small_grads")
    small_parts = small_parts.reshape(N_DEV, -1, LANES)

    out_g, out_d, out_m, out_v = {}, {}, {}, {}
    for n in big:
        shp = wt[n].shape
        two_d = lambda a: a.reshape(-1, shp[-1])
        res = _adamw_reduce([landed[n, li] for li in range(DEPTH)], two_d(wt[n]), two_d(mom_m[n]), two_d(mom_v[n]),
                            tr=32 if n == "w_down" else 128, name="adamw_" + n)
        out_g[n], out_d[n], out_m[n], out_v[n] = [r.reshape(shp) for r in res]

    *reduced, loss = _unpack_rows(_sum_slots(small_parts, name="sum_small_grads"), small_full_shapes + [(1,)])
    loss = loss[0]
    reduced = dict(zip(small_names, reduced))
    rep_shapes = [wt[n].shape for n in REPLICATED]
    upd = _adamw_plain(_pack_rows([reduced[n] for n in REPLICATED]), _pack_rows([wt[n] for n in REPLICATED]),
                       _pack_rows([mom_m[n] for n in REPLICATED]), _pack_rows([mom_v[n] for n in REPLICATED]),
                       name="adamw_replicated")
    for dst, packed in zip((out_d, out_m, out_v), upd):
        dst.update(zip(REPLICATED, _unpack_rows(packed, rep_shapes)))
    for n in REPLICATED:
        out_g[n] = reduced[n]
    for n in SMALL_SHARDED:
        shp = wt[n].shape
        g = reduced[n].reshape(shp[0], shp[1], N_DEV, shp[2])
        g = lax.dynamic_index_in_dim(g, me, axis=2, keepdims=False)
        two_d = lambda a: a.reshape(-1, shp[-1])
        res = _adamw_plain(two_d(g), two_d(wt[n]), two_d(mom_m[n]), two_d(mom_v[n]), name="adamw_" + n)
        out_g[n] = g
        out_d[n], out_m[n], out_v[n] = [r.reshape(shp) for r in res]

    return (loss, dx.reshape(bl, SEQ, D_MODEL), *[out_g[n] for n in WEIGHT_NAMES], *[out_d[n] for n in WEIGHT_NAMES],
            *[out_m[n] for n in WEIGHT_NAMES], *[out_v[n] for n in WEIGHT_NAMES])
```

```python
import math

import jax
import jax.numpy as jnp
import numpy as np
from jax import lax
from jax.experimental import pallas as pl
from jax.experimental.pallas import tpu as pltpu
from jax.experimental.pallas import tpu_sc as plsc

F32 = jnp.float32
BF16 = jnp.bfloat16

N_DEV = 8
D_MODEL = 1024
SEQ = 2048
DEPTH = 2
HEAD_DIM = 64
IN_WIDTH = 2304
D_FF = 2816
PLE_DIM = 256
C_CHUNK = 128
C_GROUPS = 4
DILATED_CFGS = ((128, 1), (512, 4), (2048, 16))
DILATIONS = tuple(d for _, d in DILATED_CFGS)
A_RADIUS = 64
SWA_RADIUS = 128
BAND_BLOCK = 256
GRID_W = 64
ROPE_THETA = 10000.0
REL_BUCKETS = 32
REL_MAX_DIST = 1024
EPS = 1e-6
NEG_INF = -1e30
ATTN_SCALE = HEAD_DIM ** -0.5
LANES = 128

ADAM_LR = 0.001
ADAM_B1 = 0.9
ADAM_B2 = 0.999
ADAM_EPS = 1e-08
ADAM_WD = 0.01
ADAM_STEP = 10

MESH = pl.DeviceIdType.MESH
NT = (((1,), (1,)), ((), ()))
TN = (((0,), (0,)), ((), ()))
ARB = "arbitrary"
PAR = "parallel"


def _cparams(*sem):
    return pltpu.CompilerParams(dimension_semantics=tuple(sem))


def _sds(shape, dtype):
    return jax.ShapeDtypeStruct(tuple(shape), dtype)


def _group_sum_matrix(n, same_group):
    r = lax.broadcasted_iota(jnp.int32, (n, n), 0)
    c = lax.broadcasted_iota(jnp.int32, (n, n), 1)
    if same_group:
        return ((r >> 6) == (c >> 6)).astype(F32)
    return ((r & 63) == (c & 63)).astype(F32)


def _seg_sum(x, e):
    eb = e.astype(BF16)
    hi = x.astype(BF16)
    lo = (x - hi.astype(F32)).astype(BF16)
    return jnp.dot(hi, eb, preferred_element_type=F32) + jnp.dot(lo, eb, preferred_element_type=F32)


def _gelu(x):
    c = math.sqrt(2.0 / math.pi)
    return 0.5 * x * (1.0 + jnp.tanh(c * (x + 0.044715 * (x * x * x))))


def _gelu_grad(x):
    c = math.sqrt(2.0 / math.pi)
    t = jnp.tanh(c * (x + 0.044715 * (x * x * x)))
    return 0.5 * (1.0 + t) + 0.5 * x * (1.0 - t * t) * c * (1.0 + 3.0 * 0.044715 * (x * x))


def _sigmoid(x):
    return 1.0 / (1.0 + jnp.exp(-x))


def _scatter_cols(scratch, first, val):
    for c in range(val.shape[1] // LANES):
        scratch[first + c] = val[:, c * LANES:(c + 1) * LANES]


def _gather_cols(scratch, first, ncol):
    return jnp.concatenate([scratch[first + c] for c in range(ncol)], axis=1)


def _read_residue(scratch, first, ncol, r, d):
    n = scratch.shape[1] // d
    return jnp.concatenate([scratch.at[first + c][pl.ds(r, n, stride=d), :] for c in range(ncol)], axis=1)


def _write_residue(scratch, first, r, d, val):
    n = scratch.shape[1] // d
    for c in range(val.shape[1] // LANES):
        scratch.at[first + c][pl.ds(r, n, stride=d), :] = val[:, c * LANES:(c + 1) * LANES]


def _norm_mm(xs, gain, w, res, *, tm, tn, name, out_dtype=F32):
    t = xs[0].shape[0]
    k = sum(x.shape[1] for x in xs)
    n = w.shape[1]
    ng = len(xs)
    has_res = res is not None

    def body(*refs):
        x_refs = refs[:ng]
        g_ref, w_ref = refs[ng], refs[ng + 1]
        res_ref = refs[ng + 2] if has_res else None
        hn_ref, o_ref, hn_s = refs[ng + 2 + has_res:]

        @pl.when(pl.program_id(1) == 0)
        def _():
            off = 0
            for xr in x_refs:
                x = xr[...].astype(F32)
                wd = x.shape[1]
                r = lax.rsqrt(jnp.mean(x * x, axis=-1, keepdims=True) + EPS)
                hn_s[:, off:off + wd] = (x * r * g_ref[:, off:off + wd]).astype(BF16)
                off += wd
            hn_ref[...] = hn_s[...]

        acc = jnp.dot(hn_s[...], w_ref[...], preferred_element_type=F32)
        if has_res:
            acc = acc + res_ref[...]
        o_ref[...] = acc.astype(out_dtype)

    in_specs = [pl.BlockSpec((tm, x.shape[1]), lambda i, j: (i, 0)) for x in xs]
    in_specs += [pl.BlockSpec((1, k), lambda i, j: (0, 0)), pl.BlockSpec((k, tn), lambda i, j: (0, j))]
    args = list(xs) + [gain, w]
    if has_res:
        in_specs.append(pl.BlockSpec((tm, tn), lambda i, j: (i, j)))
        args.append(res)
    return pl.pallas_call(
        body, name=name, grid=(t // tm, n // tn), in_specs=in_specs,
        out_specs=[pl.BlockSpec((tm, k), lambda i, j: (i, 0)), pl.BlockSpec((tm, tn), lambda i, j: (i, j))],
        out_shape=[_sds((t, k), BF16), _sds((t, n), out_dtype)],
        scratch_shapes=[pltpu.VMEM((tm, k), BF16)],
        compiler_params=_cparams(PAR, ARB),
    )(*args)


def _mm(a, b, mode, res, *, tm, tn, out_dtype, name, a_rows=None):
    if mode == "tn":
        kk, m = a.shape
        blk_a = 0
        if a_rows is not None:
            blk_a, kk = a_rows
        a_spec = pl.BlockSpec((kk, tm), lambda i, j: (blk_a, i))
    else:
        m, kk = a.shape
        a_spec = pl.BlockSpec((tm, kk), lambda i, j: (i, 0))
    if mode == "nt":
        n = b.shape[0]
        b_spec = pl.BlockSpec((tn, kk), lambda i, j: (j, 0))
    else:
        n = b.shape[1]
        b_spec = pl.BlockSpec((kk, tn), lambda i, j: (0, j))
    has_res = res is not None

    def body(*refs):
        a_ref, b_ref = refs[0], refs[1]
        o_ref = refs[-1]
        av = a_ref[...].astype(BF16)
        bv = b_ref[...].astype(BF16)
        if mode == "nn":
            acc = jnp.dot(av, bv, preferred_element_type=F32)
        elif mode == "nt":
            acc = lax.dot_general(av, bv, NT, preferred_element_type=F32)
        else:
            acc = lax.dot_general(av, bv, TN, preferred_element_type=F32)
        if has_res:
            acc = acc + refs[2][...]
        o_ref[...] = acc.astype(out_dtype)

    in_specs = [a_spec, b_spec]
    args = [a, b]
    if has_res:
        in_specs.append(pl.BlockSpec((tm, tn), lambda i, j: (i, j)))
        args.append(res)
    return pl.pallas_call(
        body, name=name, grid=(m // tm, n // tn), in_specs=in_specs,
        out_specs=pl.BlockSpec((tm, tn), lambda i, j: (i, j)),
        out_shape=_sds((m, n), out_dtype),
        compiler_params=_cparams(PAR, PAR),
    )(*args)


def _mm_bt_normbwd(dys, w, xs, gain, dres, *, tm, tn, name, emit_bf16=False):
    t, wd_each = dys[0].shape
    nd = len(dys)
    per = wd_each // tn
    nj = nd * per
    k = w.shape[0]
    ng = len(xs)
    has_res = dres is not None

    def body(*refs):
        dy_refs = refs[:nd]
        w_ref = refs[nd]
        x_refs = refs[nd + 1:nd + 1 + ng]
        g_ref = refs[nd + 1 + ng]
        dres_ref = refs[nd + 2 + ng] if has_res else None
        outs = refs[nd + 2 + ng + has_res:]
        dx_ref = outs[0]
        dxb_ref = outs[1] if emit_bf16 else None
        dg_ref, acc = outs[1 + emit_bf16:]
        i, j = pl.program_id(0), pl.program_id(1)

        @pl.when(j == 0)
        def _():
            acc[...] = jnp.zeros_like(acc)

        for d, dy_ref in enumerate(dy_refs):
            @pl.when((j >= d * per) & (j < (d + 1) * per))
            def _(dy_ref=dy_ref):
                acc[...] += lax.dot_general(dy_ref[...].astype(BF16), w_ref[...], NT, preferred_element_type=F32)

        @pl.when(j == nj - 1)
        def _():
            @pl.when(i == 0)
            def _():
                dg_ref[...] = jnp.zeros_like(dg_ref)

            off = 0
            for xr in x_refs:
                x = xr[...].astype(F32)
                wd = x.shape[1]
                g = g_ref[:, off:off + wd]
                dyn = acc[:, off:off + wd]
                r = lax.rsqrt(jnp.mean(x * x, axis=-1, keepdims=True) + EPS)
                gdy = dyn * g
                dx = r * gdy - x * (r * r * r * jnp.mean(gdy * x, axis=-1, keepdims=True))
                if has_res:
                    dx = dx + dres_ref[:, off:off + wd]
                dx_ref[:, off:off + wd] = dx
                if emit_bf16:
                    dxb_ref[:, off:off + wd] = dx.astype(BF16)
                dg_ref[:, off:off + wd] += jnp.sum(dyn * x * r, axis=0, keepdims=True)
                off += wd

    def dy_map(d):
        return lambda i, j: (i, jnp.clip(j - d * per, 0, per - 1))

    in_specs = [pl.BlockSpec((tm, tn), dy_map(d)) for d in range(nd)]
    in_specs.append(pl.BlockSpec((k, tn), lambda i, j: (0, j)))
    in_specs += [pl.BlockSpec((tm, x.shape[1]), lambda i, j: (i, 0)) for x in xs]
    in_specs.append(pl.BlockSpec((1, k), lambda i, j: (0, 0)))
    args = list(dys) + [w] + list(xs) + [gain]
    if has_res:
        in_specs.append(pl.BlockSpec((tm, k), lambda i, j: (i, 0)))
        args.append(dres)
    row = pl.BlockSpec((tm, k), lambda i, j: (i, 0))
    out_specs = [row] + ([row] if emit_bf16 else []) + [pl.BlockSpec((1, k), lambda i, j: (0, 0))]
    out_shape = [_sds((t, k), F32)] + ([_sds((t, k), BF16)] if emit_bf16 else []) + [_sds((1, k), F32)]
    return pl.pallas_call(
        body, name=name, grid=(t // tm, nj), in_specs=in_specs, out_specs=out_specs, out_shape=out_shape,
        scratch_shapes=[pltpu.VMEM((tm, k), F32)],
        compiler_params=_cparams(ARB, ARB),
    )(*args)


def _rope_partner(y):
    n = y.shape[1]
    lane = lax.broadcasted_iota(jnp.int32, y.shape, 1)
    return jnp.where((lane & 31) < 16, pltpu.roll(y, n - 16, 1), pltpu.roll(y, 16, 1))


def _residue_specs(tm, width, nt):
    specs = [pl.BlockSpec((tm, width), lambda b, i: (b * nt + i, 0))]
    for d in DILATIONS[1:]:
        specs.append(pl.BlockSpec((None, d, tm // d, width), lambda b, i: (b, 0, i, 0)))
    return specs


def _residue_shapes(bl, width, dtype):
    return [_sds((bl * SEQ, width), dtype)] + [_sds((bl, d, SEQ // d, width), dtype) for d in DILATIONS[1:]]


def _qkprep_fwd(proj, gains, cos, sins, *, tm, name):
    t = proj.shape[0]
    bl = t // SEQ
    nt = SEQ // tm

    def body(p_ref, g_ref, c_ref, s_ref, qa1_ref, qa4_ref, qa16_ref, qb_ref, qd_ref, scr):
        e = _group_sum_matrix(256, True)

        def hn(x, row):
            x = x.astype(F32)
            wd = x.shape[1]
            ms = _seg_sum(x * x, e[:wd, :wd]) * (1.0 / HEAD_DIM)
            return x * lax.rsqrt(ms + EPS) * g_ref[row:row + 1, :wd]

        qa = jnp.concatenate([hn(p_ref[:, 0:256], 0) * ATTN_SCALE, hn(p_ref[:, 256:512], 1),
                              p_ref[:, 512:768].astype(F32)], axis=1)
        qa1_ref[...] = qa.astype(BF16)
        _scatter_cols(scr, 0, qa)
        for d, ref in ((4, qa4_ref), (16, qa16_ref)):
            for r in range(d):
                ref[r] = _read_residue(scr, 0, 6, r, d).astype(BF16)
        qb_ref[:, 0:256] = (hn(p_ref[:, 768:1024], 2) * ATTN_SCALE).astype(BF16)
        qb_ref[:, 256:384] = hn(p_ref[:, 1024:1152], 3).astype(BF16)
        qb_ref[:, 384:512] = p_ref[:, 1152:1280].astype(BF16)
        yq = hn(p_ref[:, 1792:2048], 4)
        yq = yq * c_ref[...] + _rope_partner(yq) * s_ref[...]
        qd_ref[:, 0:256] = (yq * ATTN_SCALE).astype(BF16)
        yk = hn(p_ref[:, 2048:2176], 5)
        yk = yk * c_ref[:, 0:128] + _rope_partner(yk) * s_ref[:, 0:128]
        qd_ref[:, 256:384] = yk.astype(BF16)
        qd_ref[:, 384:512] = p_ref[:, 2176:2304].astype(BF16)

    row = lambda width: pl.BlockSpec((tm, width), lambda b, i: (b * nt + i, 0))
    tab = pl.BlockSpec((tm, 256), lambda b, i: (i, 0))
    return pl.pallas_call(
        body, name=name, grid=(bl, nt),
        in_specs=[row(IN_WIDTH), pl.BlockSpec((8, 256), lambda b, i: (0, 0)), tab, tab],
        out_specs=_residue_specs(tm, 768, nt) + [row(512), row(512)],
        out_shape=_residue_shapes(bl, 768, BF16) + [_sds((t, 512), BF16), _sds((t, 512), BF16)],
        scratch_shapes=[pltpu.VMEM((6, tm, LANES), F32)],
        compiler_params=_cparams(PAR, PAR),
    )(proj, gains, cos, sins)


def _qkprep_bwd(proj, da, db, dd, dcu, dcv, gains, cos, sins, *, tm, name):
    t = proj.shape[0]
    bl = t // SEQ
    nt = SEQ // tm
    flat = [a for cfg in da for a in cfg] + list(db) + list(dd) + [dcu, dcv]

    def body(*refs):
        p_ref, g_ref, c_ref, s_ref = refs[:4]
        d_refs = refs[4:4 + len(flat)]
        dp_ref, dg_ref, scr = refs[4 + len(flat):]
        a_refs = d_refs[:9]
        dqb_ref, dkb_ref, dvb_ref, dqd_ref, dkd_ref, dvd_ref, dcu_ref, dcv_ref = d_refs[9:]
        e = _group_sum_matrix(256, True)
        first = (pl.program_id(0) == 0) & (pl.program_id(1) == 0)
        last = (pl.program_id(0) == bl - 1) & (pl.program_id(1) == nt - 1)

        @pl.when(first)
        def _():
            dg_ref[...] = jnp.zeros_like(dg_ref)

        def hn_bwd(x, dy, row):
            x, dy = x.astype(F32), dy.astype(F32)
            wd = x.shape[1]
            ee = e[:wd, :wd]
            g = g_ref[row:row + 1, :wd]
            r = lax.rsqrt(_seg_sum(x * x, ee) * (1.0 / HEAD_DIM) + EPS)
            gdy = dy * g
            dx = r * gdy - x * (r * r * r * (_seg_sum(gdy * x, ee) * (1.0 / HEAD_DIM)))
            dg_ref[row:row + 1, :wd] += jnp.sum(dy * x * r, axis=0, keepdims=True)
            return dx

        def rope_bwd(dy, wd):
            dy = dy.astype(F32)
            return dy * c_ref[:, :wd] + _rope_partner(dy * s_ref[:, :wd])

        dqkv = jnp.concatenate([a_refs[m][...].astype(F32) for m in range(3)], axis=1)
        for ci, d in ((1, 4), (2, 16)):
            for r in range(d):
                part = jnp.concatenate([a_refs[3 * ci + m][r].astype(F32) for m in range(3)], axis=1)
                _write_residue(scr, 0, r, d, part)
            dqkv = dqkv + _gather_cols(scr, 0, 6)
        dp_ref[:, 0:256] = hn_bwd(p_ref[:, 0:256], dqkv[:, 0:256] * ATTN_SCALE, 0).astype(BF16)
        dp_ref[:, 256:512] = hn_bwd(p_ref[:, 256:512], dqkv[:, 256:512], 1).astype(BF16)
        dp_ref[:, 512:768] = dqkv[:, 512:768].astype(BF16)
        dp_ref[:, 768:1024] = hn_bwd(p_ref[:, 768:1024], dqb_ref[...] * ATTN_SCALE, 2).astype(BF16)
        dp_ref[:, 1024:1152] = hn_bwd(p_ref[:, 1024:1152], dkb_ref[...], 3).astype(BF16)
        dp_ref[:, 1152:1280] = dvb_ref[...].astype(BF16)
        dp_ref[:, 1280:1536] = dcu_ref[...].astype(BF16)
        dp_ref[:, 1536:1792] = dcv_ref[...].astype(BF16)
        dp_ref[:, 1792:2048] = hn_bwd(p_ref[:, 1792:2048], rope_bwd(dqd_ref[...] * ATTN_SCALE, 256), 4).astype(BF16)
        dp_ref[:, 2048:2176] = hn_bwd(p_ref[:, 2048:2176], rope_bwd(dkd_ref[...], 128), 5).astype(BF16)
        dp_ref[:, 2176:2304] = dvd_ref[...].astype(BF16)

        @pl.when(last)
        def _():
            dg_ref[...] = _seg_sum(dg_ref[...], _group_sum_matrix(256, False))

    row = lambda width: pl.BlockSpec((tm, width), lambda b, i: (b * nt + i, 0))
    tab = pl.BlockSpec((tm, 256), lambda b, i: (i, 0))
    in_specs = [row(IN_WIDTH), pl.BlockSpec((8, 256), lambda b, i: (0, 0)), tab, tab]
    res_specs = _residue_specs(tm, 256, nt)
    in_specs += [res_specs[ci] for ci in range(3) for _ in range(3)]
    in_specs += [row(a.shape[1]) for a in flat[9:]]
    return pl.pallas_call(
        body, name=name, grid=(bl, nt), in_specs=in_specs,
        out_specs=[row(IN_WIDTH), pl.BlockSpec((8, 256), lambda b, i: (0, 0))],
        out_shape=[_sds((t, IN_WIDTH), BF16), _sds((8, 256), F32)],
        scratch_shapes=[pltpu.VMEM((6, tm, LANES), F32)],
        compiler_params=_cparams(ARB, ARB),
    )(proj, gains, cos, sins, *flat)


BAND_ROWS_PER_STEP = 512


def _residues_per_step(dil, seq_len):
    return min(dil, max(1, BAND_ROWS_PER_STEP // seq_len))


def _band_spec(seq_len, spec, rb):
    width, idx = spec
    return pl.BlockSpec((None, rb, seq_len, width), lambda b, r: (b, r, 0, idx))


def _fill_padded(dst, src_ref, rad, seq_len):
    z = jnp.zeros((rad, dst.shape[1]), dst.dtype)
    dst[0:rad, :] = z
    dst[rad + seq_len:rad + seq_len + rad, :] = z
    dst[rad:rad + seq_len, :] = src_ref[...]


def _band_fwd(src, qs, ks, vs, bias, sink, *, rad, nh, nkv, name):
    bl, dil, sl, _ = src.shape
    blk = bias.shape[1]
    kw = blk + 2 * rad
    nb = sl // blk
    rep = nh // nkv
    has_sink = sink is not None
    rb = _residues_per_step(dil, sl)

    def body(*refs):
        q_all, k_all, v_all, b_ref = refs[:4]
        s_ref = refs[4] if has_sink else None
        o_all, l_all, kp, vp = refs[4 + has_sink:]
        for ri in range(rb):
            one_sequence(q_all.at[ri], k_all.at[ri], v_all.at[ri], b_ref, s_ref, o_all.at[ri], l_all.at[ri], kp, vp)

    def one_sequence(q_ref, k_ref, v_ref, b_ref, s_ref, o_ref, l_ref, kp, vp):
        _fill_padded(kp, k_ref, rad, sl)
        _fill_padded(vp, v_ref, rad, sl)

        def blk_body(i, carry):
            r0 = pl.multiple_of(i * blk, blk)
            qb = q_ref[pl.ds(r0, blk), :]
            kwin = kp[pl.ds(r0, kw), :]
            vwin = vp[pl.ds(r0, kw), :]
            col = r0 - rad + lax.broadcasted_iota(jnp.int32, (blk, kw), 1)
            neg = jnp.where((col >= 0) & (col < sl), 0.0, NEG_INF).astype(F32)
            for h in range(nh):
                g = h // rep
                hs = slice(h * HEAD_DIM, (h + 1) * HEAD_DIM)
                gs = slice(g * HEAD_DIM, (g + 1) * HEAD_DIM)
                s = lax.dot_general(qb[:, hs], kwin[:, gs], NT, preferred_element_type=F32)
                s = s + b_ref[h] + neg
                m = jnp.max(s, axis=1, keepdims=True)
                if has_sink:
                    sk = s_ref[h][0:1, 0:1]
                    m = jnp.maximum(m, sk)
                p = jnp.exp(s - m)
                den = jnp.sum(p, axis=1, keepdims=True)
                if has_sink:
                    den = den + jnp.exp(sk - m)
                o = jnp.dot(p.astype(BF16), vwin[:, gs], preferred_element_type=F32) / den
                o_ref[pl.ds(r0, blk), hs] = o.astype(BF16)
                l_ref[pl.ds(r0, blk), hs] = jnp.broadcast_to(m + jnp.log(den), (blk, HEAD_DIM))
            return carry

        lax.fori_loop(0, nb, blk_body, 0)

    in_specs = [_band_spec(sl, qs, rb), _band_spec(sl, ks, rb), _band_spec(sl, vs, rb),
                pl.BlockSpec((nh, blk, kw), lambda b, r: (0, 0, 0))]
    args = [src] * 3 + [bias]
    if has_sink:
        in_specs.append(pl.BlockSpec((nh, 8, 128), lambda b, r: (0, 0, 0)))
        args.append(sink)
    return pl.pallas_call(
        body, name=name, grid=(bl, dil // rb), in_specs=in_specs,
        out_specs=[_band_spec(sl, (256, 0), rb)] * 2,
        out_shape=[_sds((bl, dil, sl, 256), BF16), _sds((bl, dil, sl, 256), F32)],
        scratch_shapes=[pltpu.VMEM((sl + 2 * rad, ks[0]), BF16), pltpu.VMEM((sl + 2 * rad, vs[0]), BF16)],
        compiler_params=_cparams(PAR, PAR),
    )(*args)


def _band_bwd(src, qs, ks, vs, bias, sink, dy, dcol, lse, delta, *, rad, nh, nkv, name):
    bl, dil, sl, _ = src.shape
    blk = bias.shape[1]
    kw = blk + 2 * rad
    nb = sl // blk
    rep = nh // nkv
    has_sink = sink is not None
    rb = _residues_per_step(dil, sl)
    wk, wv = ks[0], vs[0]

    def body(*refs):
        q_all, k_all, v_all, b_ref = refs[:4]
        s_ref = refs[4] if has_sink else None
        do_all, l_all, dl_all = refs[4 + has_sink:7 + has_sink]
        outs = refs[7 + has_sink:]
        dsk_ref = None
        if has_sink:
            dq_all, dk_all, dv_all, db_ref, dsk_ref, kp, vp, dka, dva = outs
        else:
            dq_all, dk_all, dv_all, db_ref, kp, vp, dka, dva = outs

        @pl.when((pl.program_id(0) == 0) & (pl.program_id(1) == 0))
        def _():
            db_ref[...] = jnp.zeros_like(db_ref)
            if has_sink:
                dsk_ref[...] = jnp.zeros_like(dsk_ref)

        for ri in range(rb):
            one_sequence(q_all.at[ri], k_all.at[ri], v_all.at[ri], b_ref, s_ref, do_all.at[ri], l_all.at[ri],
                         dl_all.at[ri], dq_all.at[ri], dk_all.at[ri], dv_all.at[ri], db_ref, dsk_ref, kp, vp, dka, dva)

    def one_sequence(q_ref, k_ref, v_ref, b_ref, s_ref, do_ref, l_ref, dl_ref, dq_ref, dk_ref, dv_ref, db_ref, dsk_ref,
                     kp, vp, dka, dva):
        _fill_padded(kp, k_ref, rad, sl)
        _fill_padded(vp, v_ref, rad, sl)
        dka[...] = jnp.zeros_like(dka)
        dva[...] = jnp.zeros_like(dva)

        def blk_body(i, carry):
            r0 = pl.multiple_of(i * blk, blk)
            qb = q_ref[pl.ds(r0, blk), :]
            kwin = kp[pl.ds(r0, kw), :]
            vwin = vp[pl.ds(r0, kw), :]
            dob = do_ref[pl.ds(r0, blk), :].astype(BF16)
            lb = l_ref[pl.ds(r0, blk), :]
            dlb = dl_ref[pl.ds(r0, blk), :]
            col = r0 - rad + lax.broadcasted_iota(jnp.int32, (blk, kw), 1)
            neg = jnp.where((col >= 0) & (col < sl), 0.0, NEG_INF).astype(F32)
            for h in range(nh):
                g = h // rep
                hs = slice(h * HEAD_DIM, (h + 1) * HEAD_DIM)
                gs = slice(g * HEAD_DIM, (g + 1) * HEAD_DIM)
                qh, kh, vh, doh = qb[:, hs], kwin[:, gs], vwin[:, gs], dob[:, hs]
                lh = lb[:, h * HEAD_DIM:h * HEAD_DIM + 1]
                dlh = dlb[:, h * HEAD_DIM:h * HEAD_DIM + 1]
                s = lax.dot_general(qh, kh, NT, preferred_element_type=F32) + b_ref[h] + neg
                p = jnp.exp(s - lh)
                dp = lax.dot_general(doh, vh, NT, preferred_element_type=F32)
                ds = p * (dp - dlh)
                dsb = ds.astype(BF16)
                dq_ref[pl.ds(r0, blk), hs] = jnp.dot(dsb, kh, preferred_element_type=F32).astype(BF16)
                dka[pl.ds(r0, kw), gs] += lax.dot_general(dsb, qh, TN, preferred_element_type=F32)
                dva[pl.ds(r0, kw), gs] += lax.dot_general(p.astype(BF16), doh, TN, preferred_element_type=F32)
                db_ref[h] += ds
                if has_sink:
                    ps = jnp.exp(s_ref[h][0:1, 0:1] - lh)
                    dsk_ref[h] += jnp.broadcast_to(-jnp.sum(ps * dlh, axis=0, keepdims=True), (8, 128))
            return carry

        lax.fori_loop(0, nb, blk_body, 0)
        dk_ref[...] = dka[rad:rad + sl, :].astype(BF16)
        dv_ref[...] = dva[rad:rad + sl, :].astype(BF16)

    const3 = lambda b, r: (0, 0, 0)
    in_specs = [_band_spec(sl, qs, rb), _band_spec(sl, ks, rb), _band_spec(sl, vs, rb),
                pl.BlockSpec((nh, blk, kw), const3)]
    args = [src] * 3 + [bias]
    if has_sink:
        in_specs.append(pl.BlockSpec((nh, 8, 128), const3))
        args.append(sink)
    row = _band_spec(sl, (256, 0), rb)
    in_specs += [_band_spec(sl, (256, dcol), rb), row, row]
    args += [dy, lse, delta]
    out_specs = [row, _band_spec(sl, (wk, 0), rb), _band_spec(sl, (wv, 0), rb), pl.BlockSpec((nh, blk, kw), const3)]
    out_shape = [_sds((bl, dil, sl, 256), BF16), _sds((bl, dil, sl, wk), BF16), _sds((bl, dil, sl, wv), BF16),
                 _sds((nh, blk, kw), F32)]
    if has_sink:
        out_specs.append(pl.BlockSpec((nh, 8, 128), const3))
        out_shape.append(_sds((nh, 8, 128), F32))
    return pl.pallas_call(
        body, name=name, grid=(bl, dil // rb), in_specs=in_specs, out_specs=out_specs, out_shape=out_shape,
        scratch_shapes=[pltpu.VMEM((sl + 2 * rad, wk), BF16), pltpu.VMEM((sl + 2 * rad, wv), BF16),
                        pltpu.VMEM((sl + 2 * rad, wk), F32), pltpu.VMEM((sl + 2 * rad, wv), F32)],
        compiler_params=_cparams(ARB, ARB),
    )(*args)


def _combine_a(os_, ls_, *, tm, name):
    bl = os_[1].shape[0]
    t = bl * SEQ
    nt = SEQ // tm

    def body(o1, o4, o16, l1, l4, l16, y_ref, lt_ref, scr):
        for k, (d, ref) in enumerate(((4, o4), (16, o16), (4, l4), (16, l16))):
            for r in range(d):
                _write_residue(scr, 2 * k, r, d, ref[r].astype(F32))
        o2, o3, b, c = (_gather_cols(scr, 2 * k, 2) for k in range(4))
        a = l1[...]
        m = jnp.maximum(jnp.maximum(a, b), c)
        ea, eb, ec = jnp.exp(a - m), jnp.exp(b - m), jnp.exp(c - m)
        den = ea + eb + ec
        y_ref[...] = ((ea / den) * o1[...].astype(F32) + (eb / den) * o2 + (ec / den) * o3).astype(BF16)
        lt_ref[...] = m + jnp.log(den)

    specs = _residue_specs(tm, 256, nt)
    return pl.pallas_call(
        body, name=name, grid=(bl, nt), in_specs=specs * 2, out_specs=[specs[0]] * 2,
        out_shape=[_sds((t, 256), BF16), _sds((t, 256), F32)], scratch_shapes=[pltpu.VMEM((8, tm, LANES), F32)],
        compiler_params=_cparams(PAR, PAR),
    )(*os_, *ls_)


def _deltas(dycat, ya, yb, yd, lse_a, *, tm, name):
    t = ya.shape[0]
    bl = t // SEQ
    nt = SEQ // tm

    def body(dy_ref, ya_ref, yb_ref, yd_ref, la_ref, dy4, dy16, l4, l16, da1, da4, da16, db_ref, dd_ref, scr):
        e = _group_sum_matrix(256, True)
        dya = dy_ref[:, 0:256]
        dla = _seg_sum(dya * ya_ref[...].astype(F32), e)
        da1[...] = dla
        db_ref[...] = _seg_sum(dy_ref[:, 256:512] * yb_ref[...].astype(F32), e)
        dd_ref[...] = _seg_sum(dy_ref[:, 768:1024] * yd_ref[...].astype(F32), e)
        for k, (val, r4, r16) in enumerate(((dya, dy4, dy16), (la_ref[...], l4, l16), (dla, da4, da16))):
            _scatter_cols(scr, 2 * k, val)
            for d, ref in ((4, r4), (16, r16)):
                for r in range(d):
                    ref[r] = _read_residue(scr, 2 * k, 2, r, d)

    specs = _residue_specs(tm, 256, nt)
    nat = specs[0]
    shapes = _residue_shapes(bl, 256, F32)
    outs = pl.pallas_call(
        body, name=name, grid=(bl, nt),
        in_specs=[pl.BlockSpec((tm, 1024), lambda b, i: (b * nt + i, 0)), nat, nat, nat, nat],
        out_specs=specs[1:] + specs[1:] + specs + [nat, nat],
        out_shape=shapes[1:] + shapes[1:] + shapes + [shapes[0], shapes[0]],
        scratch_shapes=[pltpu.VMEM((6, tm, LANES), F32)],
        compiler_params=_cparams(PAR, PAR),
    )(dycat, ya, yb, yd, lse_a)
    return outs[0:2], outs[2:4], outs[4:7], outs[7], outs[8]


def _dense_fwd(qd, *, tq, name):
    t = qd.shape[0]
    bl = t // SEQ
    nq = SEQ // tq

    def body(q_ref, k_ref, v_ref, o_ref, l_ref):
        q = q_ref[...]
        for g in range(2):
            h0, h1 = 2 * g, 2 * g + 1
            q2 = jnp.concatenate([q[:, h0 * 64:(h0 + 1) * 64], q[:, h1 * 64:(h1 + 1) * 64]], axis=0)
            kg = k_ref[:, g * 64:(g + 1) * 64]
            vg = v_ref[:, g * 64:(g + 1) * 64]
            s = lax.dot_general(q2, kg, NT, preferred_element_type=F32)
            m = jnp.max(s, axis=1, keepdims=True)
            p = jnp.exp(s - m)
            den = jnp.sum(p, axis=1, keepdims=True)
            o2 = jnp.dot(p.astype(BF16), vg, preferred_element_type=F32) / den
            l2 = jnp.broadcast_to(m + jnp.log(den), (2 * tq, 64))
            o_ref[:, h0 * 64:(h0 + 1) * 64] = o2[:tq].astype(BF16)
            o_ref[:, h1 * 64:(h1 + 1) * 64] = o2[tq:].astype(BF16)
            l_ref[:, h0 * 64:(h0 + 1) * 64] = l2[:tq]
            l_ref[:, h1 * 64:(h1 + 1) * 64] = l2[tq:]

    q3 = qd.reshape(bl, SEQ, 512)
    o, lse = pl.pallas_call(
        body, name=name, grid=(bl, nq),
        in_specs=[pl.BlockSpec((None, tq, 256), lambda b, i: (b, i, 0)),
                  pl.BlockSpec((None, SEQ, 128), lambda b, i: (b, 0, 2)),
                  pl.BlockSpec((None, SEQ, 128), lambda b, i: (b, 0, 3))],
        out_specs=[pl.BlockSpec((None, tq, 256), lambda b, i: (b, i, 0))] * 2,
        out_shape=[_sds((bl, SEQ, 256), BF16), _sds((bl, SEQ, 256), F32)],
        compiler_params=_cparams(PAR, PAR),
    )(q3, q3, q3)
    return o.reshape(t, 256), lse.reshape(t, 256)


def _dense_bwd(qd, dycat, lse, delta, *, tq, name):
    t = qd.shape[0]
    bl = t // SEQ
    nq = SEQ // tq

    def body(q_ref, k_ref, v_ref, do_ref, l_ref, dl_ref, dq_ref, dk_ref, dv_ref, dkt, dvt):
        @pl.when(pl.program_id(1) == 0)
        def _():
            dkt[...] = jnp.zeros_like(dkt)
            dvt[...] = jnp.zeros_like(dvt)

        q = q_ref[...]
        do = do_ref[...].astype(BF16)
        lv = l_ref[...]
        dlv = dl_ref[...]
        for g in range(2):
            h0, h1 = 2 * g, 2 * g + 1
            q2 = jnp.concatenate([q[:, h0 * 64:(h0 + 1) * 64], q[:, h1 * 64:(h1 + 1) * 64]], axis=0)
            do2 = jnp.concatenate([do[:, h0 * 64:(h0 + 1) * 64], do[:, h1 * 64:(h1 + 1) * 64]], axis=0)
            l2 = jnp.concatenate([lv[:, h0 * 64:h0 * 64 + 1], lv[:, h1 * 64:h1 * 64 + 1]], axis=0)
            dl2 = jnp.concatenate([dlv[:, h0 * 64:h0 * 64 + 1], dlv[:, h1 * 64:h1 * 64 + 1]], axis=0)
            kg = k_ref[:, g * 64:(g + 1) * 64]
            vg = v_ref[:, g * 64:(g + 1) * 64]
            s = lax.dot_general(q2, kg, NT, preferred_element_type=F32)
            p = jnp.exp(s - l2)
            dp = lax.dot_general(do2, vg, NT, preferred_element_type=F32)
            ds = (p * (dp - dl2)).astype(BF16)
            dq2 = jnp.dot(ds, kg, preferred_element_type=F32)
            dq_ref[:, h0 * 64:(h0 + 1) * 64] = dq2[:tq].astype(BF16)
            dq_ref[:, h1 * 64:(h1 + 1) * 64] = dq2[tq:].astype(BF16)
            dkt[g * 64:(g + 1) * 64, :] += lax.dot_general(q2, ds, TN, preferred_element_type=F32)
            dvt[g * 64:(g + 1) * 64, :] += lax.dot_general(do2, p.astype(BF16), TN, preferred_element_type=F32)

        @pl.when(pl.program_id(1) == nq - 1)
        def _():
            dk_ref[...] = dkt[...].T.astype(BF16)
            dv_ref[...] = dvt[...].T.astype(BF16)

    q3 = qd.reshape(bl, SEQ, 512)
    tile = pl.BlockSpec((None, tq, 256), lambda b, i: (b, i, 0))
    full = pl.BlockSpec((None, SEQ, 128), lambda b, i: (b, 0, 0))
    dq, dk, dv = pl.pallas_call(
        body, name=name, grid=(bl, nq),
        in_specs=[tile, pl.BlockSpec((None, SEQ, 128), lambda b, i: (b, 0, 2)),
                  pl.BlockSpec((None, SEQ, 128), lambda b, i: (b, 0, 3)),
                  pl.BlockSpec((None, tq, 256), lambda b, i: (b, i, 3)), tile, tile],
        out_specs=[tile, full, full],
        out_shape=[_sds((bl, SEQ, 256), BF16), _sds((bl, SEQ, 128), BF16), _sds((bl, SEQ, 128), BF16)],
        scratch_shapes=[pltpu.VMEM((128, SEQ), F32), pltpu.VMEM((128, SEQ), F32)],
        compiler_params=_cparams(PAR, ARB),
    )(q3, q3, q3, dycat.reshape(bl, SEQ, 1024), lse.reshape(bl, SEQ, 256), delta.reshape(bl, SEQ, 256))
    return dq.reshape(t, 256), dk.reshape(t, 128), dv.reshape(t, 128)


def _c_norm(cv, gam, bet):
    vg = _gelu(cv)
    mu = jnp.mean(vg, axis=-1, keepdims=True)
    xc = vg - mu
    r = lax.rsqrt(jnp.mean(xc * xc, axis=-1, keepdims=True) + EPS)
    xhat = xc * r
    return xhat * gam + bet, xhat, r


def _c_fwd(proj, gam, bet, ws, bst, *, tm, name):
    t = proj.shape[0]
    nch = tm // C_CHUNK

    def body(u_ref, v_ref, g_ref, b_ref, ws_ref, bs_ref, y_ref):
        vn, _, _ = _c_norm(v_ref[...].astype(F32), g_ref[...], b_ref[...])
        vnb = vn.astype(BF16)
        for c in range(nch):
            rows = slice(c * C_CHUNK, (c + 1) * C_CHUNK)
            for g in range(C_GROUPS):
                gs = slice(g * 64, (g + 1) * 64)
                mixed = jnp.dot(ws_ref[g], vnb[rows, gs], preferred_element_type=F32) + bs_ref[:, gs]
                y_ref[rows, gs] = (_gelu(u_ref[rows, gs].astype(F32)) * mixed).astype(BF16)

    vec = pl.BlockSpec((1, 256), lambda i: (0, 0))
    return pl.pallas_call(
        body, name=name, grid=(t // tm,),
        in_specs=[pl.BlockSpec((tm, 256), lambda i: (i, 5)), pl.BlockSpec((tm, 256), lambda i: (i, 6)), vec, vec,
                  pl.BlockSpec((C_GROUPS, C_CHUNK, C_CHUNK), lambda i: (0, 0, 0)),
                  pl.BlockSpec((C_CHUNK, 256), lambda i: (0, 0))],
        out_specs=pl.BlockSpec((tm, 256), lambda i: (i, 0)), out_shape=_sds((t, 256), BF16),
        compiler_params=_cparams(PAR),
    )(proj, proj, gam, bet, ws, bst)


def _c_bwd(proj, dycat, gam, bet, ws, wst, bst, *, tm, name):
    t = proj.shape[0]
    nch = tm // C_CHUNK
    nstep = t // tm

    def body(u_ref, v_ref, dy_ref, g_ref, b_ref, ws_ref, wst_ref, bs_ref,
             du_ref, dv_ref, dws_ref, dbs_ref, dg_ref, db_ref, dvn_s):
        step = pl.program_id(0)

        @pl.when(step == 0)
        def _():
            dws_ref[...] = jnp.zeros_like(dws_ref)
            dbs_ref[...] = jnp.zeros_like(dbs_ref)
            dg_ref[...] = jnp.zeros_like(dg_ref)
            db_ref[...] = jnp.zeros_like(db_ref)

        cv = v_ref[...].astype(F32)
        gam_v = g_ref[...]
        vn, xhat, r = _c_norm(cv, gam_v, b_ref[...])
        vnb = vn.astype(BF16)
        for c in range(nch):
            rows = slice(c * C_CHUNK, (c + 1) * C_CHUNK)
            for g in range(C_GROUPS):
                gs = slice(g * 64, (g + 1) * 64)
                cu = u_ref[rows, gs].astype(F32)
                dy = dy_ref[rows, gs]
                mixed = jnp.dot(ws_ref[g], vnb[rows, gs], preferred_element_type=F32) + bs_ref[:, gs]
                du_ref[rows, gs] = (dy * mixed * _gelu_grad(cu)).astype(BF16)
                dmix = dy * _gelu(cu)
                dbs_ref[:, gs] += dmix
                dmb = dmix.astype(BF16)
                dws_ref[g] += lax.dot_general(dmb, vnb[rows, gs], NT, preferred_element_type=F32)
                dvn_s[rows, gs] = jnp.dot(wst_ref[g], dmb, preferred_element_type=F32)
        dvn = dvn_s[...]
        dg_ref[...] += jnp.sum(dvn * xhat, axis=0, keepdims=True)
        db_ref[...] += jnp.sum(dvn, axis=0, keepdims=True)
        dxh = dvn * gam_v
        dvg = r * (dxh - jnp.mean(dxh, axis=-1, keepdims=True) - xhat * jnp.mean(dxh * xhat, axis=-1, keepdims=True))
        dv_ref[...] = (dvg * _gelu_grad(cv)).astype(BF16)

        @pl.when(step == nstep - 1)
        def _():
            dbs_ref[...] = _seg_sum(dbs_ref[...], _group_sum_matrix(256, True))

    vec = pl.BlockSpec((1, 256), lambda i: (0, 0))
    mat = pl.BlockSpec((C_GROUPS, C_CHUNK, C_CHUNK), lambda i: (0, 0, 0))
    bsp = pl.BlockSpec((C_CHUNK, 256), lambda i: (0, 0))
    tile = pl.BlockSpec((tm, 256), lambda i: (i, 0))
    return pl.pallas_call(
        body, name=name, grid=(nstep,),
        in_specs=[pl.BlockSpec((tm, 256), lambda i: (i, 5)), pl.BlockSpec((tm, 256), lambda i: (i, 6)),
                  pl.BlockSpec((tm, 256), lambda i: (i, 2)), vec, vec, mat, mat, bsp],
        out_specs=[tile, tile, mat, bsp, vec, vec],
        out_shape=[_sds((t, 256), BF16), _sds((t, 256), BF16), _sds((C_GROUPS, C_CHUNK, C_CHUNK), F32),
                   _sds((C_CHUNK, 256), F32), _sds((1, 256), F32), _sds((1, 256), F32)],
        scratch_shapes=[pltpu.VMEM((tm, 256), F32)],
        compiler_params=_cparams(ARB),
    )(proj, proj, dycat, gam, bet, ws, wst, bst)


FF_TC = 128
FF_NB = D_FF // FF_TC
FF_CH = 64
FF_HALO = 16


def _taps(ref, r0, win, where):
    z = jnp.zeros((FF_HALO, win.shape[1]), F32)
    if where == "first":
        win[0:FF_HALO, :] = z
        win[FF_HALO:, :] = ref[0:FF_CH + FF_HALO, :].astype(F32)
    elif where == "last":
        win[0:FF_CH + FF_HALO, :] = ref[SEQ - FF_CH - FF_HALO:SEQ, :].astype(F32)
        win[FF_CH + FF_HALO:, :] = z
    else:
        win[...] = ref[pl.ds(pl.multiple_of(r0 - FF_HALO, FF_HALO), FF_CH + 2 * FF_HALO), :].astype(F32)
    return tuple(win[FF_HALO + o:FF_HALO + o + FF_CH, :] for o in (-1, 0, 1))


def _chunk_loop(step):
    step(0, lambda ref, win: _taps(ref, 0, win, "first"))

    def mid(i, carry):
        r0 = pl.multiple_of(i * FF_CH, FF_CH)
        step(r0, lambda ref, win: _taps(ref, r0, win, "mid"))
        return carry

    lax.fori_loop(1, SEQ // FF_CH - 1, mid, 0)
    step(SEQ - FF_CH, lambda ref, win: _taps(ref, SEQ - FF_CH, win, "last"))


def _conv3(taps, w_ref, b_ref):
    dn, md, up = taps
    return w_ref[0:1, :] * dn + w_ref[1:2, :] * md + w_ref[2:3, :] * up + b_ref[...]


def _ff_specs(order):
    def at(fn):
        return (lambda b, j: fn(b, j)) if order == "bj" else (lambda j, b: fn(b, j))
    hs = [pl.BlockSpec((None, SEQ, FF_TC), at(lambda b, j, o=o: (b, 0, j + o))) for o in (0, FF_NB)]
    ws = [pl.BlockSpec((3, FF_TC), at(lambda b, j, o=o: (0, j + o))) for o in (0, FF_NB)]
    bs = [pl.BlockSpec((1, FF_TC), at(lambda b, j, o=o: (0, j + o))) for o in (0, FF_NB)]
    return hs, ws, bs


def _conv_gate_fwd(h, cw, cb, *, name):
    t = h.shape[0]
    bl = t // SEQ

    def body(hg_ref, hu_ref, wg_ref, wu_ref, bg_ref, bu_ref, a_ref, cg_ref, cu_ref, win):
        def step(r0, taps):
            cg = _conv3(taps(hg_ref, win.at[0]), wg_ref, bg_ref)
            cu = _conv3(taps(hu_ref, win.at[1]), wu_ref, bu_ref)
            a_ref[pl.ds(r0, FF_CH), :] = (cg * _sigmoid(cg) * cu).astype(BF16)
            cg_ref[pl.ds(r0, FF_CH), :] = cg.astype(BF16)
            cu_ref[pl.ds(r0, FF_CH), :] = cu.astype(BF16)

        _chunk_loop(step)

    hs, ws, bs = _ff_specs("bj")
    h3 = h.reshape(bl, SEQ, 2 * D_FF)
    half = pl.BlockSpec((None, SEQ, FF_TC), lambda b, j: (b, 0, j))
    outs = pl.pallas_call(
        body, name=name, grid=(bl, FF_NB), in_specs=hs + ws + bs, out_specs=[half] * 3,
        out_shape=[_sds((bl, SEQ, D_FF), BF16)] * 3,
        scratch_shapes=[pltpu.VMEM((2, FF_CH + 2 * FF_HALO, FF_TC), F32)],
        compiler_params=_cparams(PAR, PAR),
    )(h3, h3, cw, cw, cb, cb)
    return [o.reshape(t, D_FF) for o in outs]


def _conv_gate_bwd(h, cg_all, cu_all, dact, cw, cb, *, name):
    t = h.shape[0]
    bl = t // SEQ

    def body(hg_ref, hu_ref, wg_ref, wu_ref, bg_ref, bu_ref, da_ref, cg_ref, cu_ref,
             dhg_ref, dhu_ref, dwg_ref, dwu_ref, dbg_ref, dbu_ref, dg_s, du_s, win, sums):
        @pl.when(pl.program_id(1) == 0)
        def _():
            for ref in (dwg_ref, dwu_ref, dbg_ref, dbu_ref):
                ref[...] = jnp.zeros_like(ref)

        sums[...] = jnp.zeros_like(sums)
        red = lambda x: jnp.sum(x.reshape(FF_CH // 8, 8, x.shape[1]), axis=0)

        def pass1(r0, taps):
            tg, tu = taps(hg_ref, win.at[0]), taps(hu_ref, win.at[1])
            cg = cg_ref[pl.ds(r0, FF_CH), :].astype(F32)
            cu = cu_ref[pl.ds(r0, FF_CH), :].astype(F32)
            da = da_ref[pl.ds(r0, FF_CH), :].astype(F32)
            sg = _sigmoid(cg)
            dcg = da * cu * (sg * (1.0 + cg * (1.0 - sg)))
            dcu = da * (cg * sg)
            dg_s[pl.ds(r0, FF_CH), :] = dcg
            du_s[pl.ds(r0, FF_CH), :] = dcu
            for half, (d, tp) in enumerate(((dcg, tg), (dcu, tu))):
                for k in range(3):
                    sums[4 * half + k] += red(d * tp[k])
                sums[4 * half + 3] += red(d)

        _chunk_loop(pass1)
        for half, (dw_ref, db_ref) in enumerate(((dwg_ref, dbg_ref), (dwu_ref, dbu_ref))):
            for k in range(3):
                dw_ref[k:k + 1, :] += jnp.sum(sums[4 * half + k], axis=0, keepdims=True)
            db_ref[...] += jnp.sum(sums[4 * half + 3], axis=0, keepdims=True)

        def pass2(r0, taps):
            for k, (s, w_ref, o_ref) in enumerate(((dg_s, wg_ref, dhg_ref), (du_s, wu_ref, dhu_ref))):
                dn, md, up = taps(s, win.at[k])
                o_ref[pl.ds(r0, FF_CH), :] = (w_ref[0:1, :] * up + w_ref[1:2, :] * md + w_ref[2:3, :] * dn).astype(BF16)

        _chunk_loop(pass2)

    hs, ws, bs = _ff_specs("jb")
    half = pl.BlockSpec((None, SEQ, FF_TC), lambda j, b: (b, 0, j))
    wsp = pl.BlockSpec((3, FF_TC), lambda j, b: (0, j))
    bsp = pl.BlockSpec((1, FF_TC), lambda j, b: (0, j))
    h3 = h.reshape(bl, SEQ, 2 * D_FF)
    dhg, dhu, dwg, dwu, dbg, dbu = pl.pallas_call(
        body, name=name, grid=(FF_NB, bl), in_specs=hs + ws + bs + [half] * 3,
        out_specs=[half, half, wsp, wsp, bsp, bsp],
        out_shape=[_sds((bl, SEQ, D_FF), BF16), _sds((bl, SEQ, D_FF), BF16), _sds((3, D_FF), F32), _sds((3, D_FF), F32),
                   _sds((1, D_FF), F32), _sds((1, D_FF), F32)],
        scratch_shapes=[pltpu.VMEM((SEQ, FF_TC), F32), pltpu.VMEM((SEQ, FF_TC), F32),
                        pltpu.VMEM((2, FF_CH + 2 * FF_HALO, FF_TC), F32), pltpu.VMEM((8, 8, FF_TC), F32)],
        compiler_params=_cparams(PAR, ARB),
    )(h3, h3, cw, cw, cb, cb, *[a.reshape(bl, SEQ, D_FF) for a in (dact, cg_all, cu_all)])
    return (dhg.reshape(t, D_FF), dhu.reshape(t, D_FF), jnp.concatenate([dwg, dwu], axis=1),
            jnp.concatenate([dbg, dbu], axis=1))


def _ple_fwd(x2, gain, wg, pe, pe_blk, wp, *, tm, name):
    t, k = x2.shape

    def body(x_ref, g_ref, wg_ref, pe_ref, wp_ref, hn_ref, x3_ref, gt_ref, pp_ref):
        x = x_ref[...]
        r = lax.rsqrt(jnp.mean(x * x, axis=-1, keepdims=True) + EPS)
        hn = (x * r * g_ref[...]).astype(BF16)
        hn_ref[...] = hn
        gate = _sigmoid(jnp.dot(hn, wg_ref[...], preferred_element_type=F32))
        pp = jnp.dot(pe_ref[...].astype(BF16), wp_ref[...], preferred_element_type=F32)
        gt_ref[...] = gate.astype(BF16)
        pp_ref[...] = pp.astype(BF16)
        x3_ref[...] = x + pp * gate

    row = pl.BlockSpec((tm, k), lambda i: (i, 0))
    return pl.pallas_call(
        body, name=name, grid=(t // tm,),
        in_specs=[row, pl.BlockSpec((1, k), lambda i: (0, 0)), pl.BlockSpec((k, k), lambda i: (0, 0)),
                  pl.BlockSpec((tm, PLE_DIM), lambda i: (pe_blk + i, 0)), pl.BlockSpec((PLE_DIM, k), lambda i: (0, 0))],
        out_specs=[row, row, row, row],
        out_shape=[_sds((t, k), BF16), _sds((t, k), F32), _sds((t, k), BF16), _sds((t, k), BF16)],
        compiler_params=_cparams(PAR),
    )(x2, gain, wg, pe, wp)


def _ple_bwd_ew(dx3, gate, pp, *, tm, name):
    t, n = dx3.shape

    def body(d_ref, g_ref, p_ref, dz_ref, dpp_ref):
        d, g = d_ref[...], g_ref[...]
        dz_ref[...] = (d * p_ref[...] * g * (1.0 - g)).astype(BF16)
        dpp_ref[...] = (d * g).astype(BF16)

    spec = pl.BlockSpec((tm, n), lambda i: (i, 0))
    return pl.pallas_call(
        body, name=name, grid=(t // tm,), in_specs=[spec] * 3, out_specs=[spec] * 2,
        out_shape=[_sds((t, n), BF16)] * 2, compiler_params=_cparams(PAR),
    )(dx3, gate, pp)


def _loss_head(y, tgt, *, tm, name):
    t, d = y.shape

    def body(y_ref, t_ref, l_ref, dy_ref):
        @pl.when(pl.program_id(0) == 0)
        def _():
            l_ref[...] = jnp.zeros_like(l_ref)

        e = y_ref[...] - t_ref[...]
        dy_ref[...] = e * (1.0 / d)
        s = jnp.sum(jnp.sum(e * e, axis=1, keepdims=True), axis=0, keepdims=True)
        l_ref[...] += jnp.broadcast_to(s * (0.5 / d), (8, 128))

    spec = pl.BlockSpec((tm, d), lambda i: (i, 0))
    return pl.pallas_call(
        body, name=name, grid=(t // tm,), in_specs=[spec, spec],
        out_specs=[pl.BlockSpec((8, 128), lambda i: (0, 0)), spec],
        out_shape=[_sds((8, 128), F32), _sds((t, d), F32)], compiler_params=_cparams(ARB),
    )(y, tgt)


BIAS_PC = 8192


def _onehot(bucket_row):
    rows = lax.broadcasted_iota(jnp.int32, (REL_BUCKETS, bucket_row.shape[1]), 0)
    return (rows == bucket_row).astype(BF16)


def _dot3(x, onehot, dims):
    acc = None
    for _ in range(3):
        term = x.astype(BF16)
        part = lax.dot_general(term, onehot, dims, preferred_element_type=F32)
        acc = part if acc is None else acc + part
        x = x - term.astype(F32)
    return acc


def _bias_lookup(table_t, bucket, *, name):
    h = table_t.shape[0]
    p = bucket.shape[1]

    def body(t_ref, b_ref, o_ref):
        bk = b_ref[...]
        val = _dot3(t_ref[...], _onehot(bk), (((1,), (0,)), ((), ())))
        o_ref[...] = jnp.where(bk >= 0, val, NEG_INF)

    return pl.pallas_call(
        body, name=name, grid=(p // BIAS_PC,),
        in_specs=[pl.BlockSpec((h, REL_BUCKETS), lambda i: (0, 0)), pl.BlockSpec((1, BIAS_PC), lambda i: (0, i))],
        out_specs=pl.BlockSpec((h, BIAS_PC), lambda i: (0, i)), out_shape=_sds((h, p), F32),
        compiler_params=_cparams(PAR),
    )(table_t, bucket)


def _bucket_reduce(dbiases, bucket, *, name):
    h, p = dbiases[0].shape
    nl = len(dbiases)

    def body(*refs):
        b_ref, o_ref = refs[nl], refs[nl + 1]

        @pl.when(pl.program_id(0) == 0)
        def _():
            o_ref[...] = jnp.zeros_like(o_ref)

        d = refs[0][...]
        for d_ref in refs[1:nl]:
            d = d + d_ref[...]
        o_ref[...] += _dot3(d, _onehot(b_ref[...]), NT)

    return pl.pallas_call(
        body, name=name, grid=(p // BIAS_PC,),
        in_specs=[pl.BlockSpec((h, BIAS_PC), lambda i: (0, i))] * nl + [pl.BlockSpec((1, BIAS_PC), lambda i: (0, i))],
        out_specs=pl.BlockSpec((h, REL_BUCKETS), lambda i: (0, 0)), out_shape=_sds((h, REL_BUCKETS), F32),
        compiler_params=_cparams(ARB),
    )(*dbiases, bucket)


def _adamw_math(w, g, m, v):
    m = ADAM_B1 * m + (1.0 - ADAM_B1) * g
    v = ADAM_B2 * v + (1.0 - ADAM_B2) * (g * g)
    m_hat = m / (1.0 - ADAM_B1 ** ADAM_STEP)
    v_hat = v / (1.0 - ADAM_B2 ** ADAM_STEP)
    delta = -ADAM_LR * (m_hat / (jnp.sqrt(v_hat) + ADAM_EPS) + ADAM_WD * w)
    return delta, m, v


def _adamw_reduce(parts, w, m, v, *, tr, name):
    nl = len(parts)
    rows, c = w.shape
    r = rows // nl
    nt = r // tr

    def body(*refs):
        p_refs = refs[:nl]
        w_ref, m_ref, v_ref, g_ref, d_ref, nm_ref, nv_ref = refs[nl:]
        for li, p_ref in enumerate(p_refs):
            @pl.when(pl.program_id(0) == li)
            def _(p_ref=p_ref):
                g = p_ref[0].astype(F32)
                for k in range(1, N_DEV):
                    g = g + p_ref[k].astype(F32)
                d, nm, nv = _adamw_math(w_ref[...], g, m_ref[...], v_ref[...])
                g_ref[...] = g
                d_ref[...] = d
                nm_ref[...] = nm
                nv_ref[...] = nv

    def part_map(li):
        return lambda l, i: (0, jnp.where(l == li, i, jnp.where(l < li, 0, nt - 1)), 0)

    spec = pl.BlockSpec((tr, c), lambda l, i: (l * nt + i, 0))
    return pl.pallas_call(
        body, name=name, grid=(nl, nt),
        in_specs=[pl.BlockSpec((N_DEV, tr, c), part_map(li)) for li in range(nl)] + [spec, spec, spec],
        out_specs=[spec] * 4, out_shape=[_sds((rows, c), F32)] * 4, compiler_params=_cparams(ARB, ARB),
    )(*parts, w, m, v)


def _adamw_plain(g, w, m, v, *, name):
    def body(g_ref, w_ref, m_ref, v_ref, d_ref, nm_ref, nv_ref):
        d, nm, nv = _adamw_math(w_ref[...], g_ref[...], m_ref[...], v_ref[...])
        d_ref[...] = d
        nm_ref[...] = nm
        nv_ref[...] = nv

    return pl.pallas_call(body, name=name, out_shape=[_sds(w.shape, F32)] * 3)(g, w, m, v)


def _mesh_pos():
    return lax.axis_index("x"), lax.axis_index("y"), lax.axis_index("c")


def _allgather_body(x_refs, out_refs, send_sems, recv_sems, local_sems, slot):
    x, y, c = _mesh_pos()
    me, sibling = (x, y, c), (x, y, 1 - c)
    chips = [(1 - x, y), (x, 1 - y), (1 - x, 1 - y)]
    waits = []
    for a, (x_ref, out_ref) in enumerate(zip(x_refs, out_refs)):
        def copy(k, block, to, src=None, out_ref=out_ref, a=a):
            return pltpu.make_async_remote_copy(
                src_ref=slot(out_ref, block) if src is None else src, dst_ref=slot(out_ref, block),
                send_sem=send_sems.at[a, k], recv_sem=recv_sems.at[a, k], device_id=to, device_id_type=MESH)

        mine = pltpu.make_async_copy(x_ref, slot(out_ref, me), local_sems.at[a])
        mine.start()
        first = [copy(0, me, sibling, src=x_ref)]
        first += [copy(1 + j, me, (*chip, c), src=x_ref) for j, chip in enumerate(chips)]
        for cp in first:
            cp.start()
        waits.append((copy, mine, first))
    sends = []
    for copy, mine, first in waits:
        passed = [copy(4 + j, (*chip, c), sibling) for j, chip in enumerate(chips)]
        for j, chip in enumerate(chips):
            copy(1 + j, (*chip, c), me).wait_recv()
            passed[j].start()
        sends.append(passed)
    for (copy, mine, first), passed in zip(waits, sends):
        copy(0, sibling, me).wait_recv()
        for j, chip in enumerate(chips):
            copy(4 + j, (*chip, 1 - c), me).wait_recv()
        for cp in first + passed:
            cp.wait_send()
        mine.wait()


PEER_FLIPS = ((0, 0, 1), (1, 0, 0), (0, 1, 0), (1, 1, 0), (1, 0, 1), (0, 1, 1), (1, 1, 1))


def _peer_copies(x_refs, land_refs, send_sem, recv_sem, scatter):
    x, y, c = _mesh_pos()
    me = 4 * x + 2 * y + c
    copies = []
    for x_ref, land_ref in zip(x_refs, land_refs):
        for fx, fy, fc in PEER_FLIPS:
            px, py, pc = x ^ fx, y ^ fy, c ^ fc
            src = x_ref.at[4 * px + 2 * py + pc] if scatter else x_ref
            copies.append(pltpu.make_async_remote_copy(
                src_ref=src, dst_ref=land_ref.at[me], send_sem=send_sem, recv_sem=recv_sem,
                device_id=(px, py, pc), device_id_type=MESH))
    return copies


def _sc_exchange(xs, *, scatter, collective_id, name):
    na = len(xs)
    land_shapes = [x.shape if scatter else (N_DEV,) + x.shape for x in xs]

    def body(*refs):
        x_refs, land_refs = refs[:na], refs[na:2 * na]
        send_sem, recv_sem, local_sem = refs[2 * na:]
        x, y, c = _mesh_pos()
        me = 4 * x + 2 * y + c
        barrier = pltpu.get_barrier_semaphore()
        for fx, fy, fc in PEER_FLIPS:
            pl.semaphore_signal(barrier, inc=1, device_id=(x ^ fx, y ^ fy, c ^ fc), device_id_type=MESH)
        pl.semaphore_wait(barrier, len(PEER_FLIPS))
        for x_ref, land_ref in zip(x_refs, land_refs):
            own = pltpu.make_async_copy(x_ref.at[me] if scatter else x_ref, land_ref.at[me], local_sem)
            own.start()
            own.wait()
        copies = _peer_copies(x_refs, land_refs, send_sem, recv_sem, scatter)
        for cp in copies:
            cp.start()
        for cp in copies:
            cp.wait()

    return pl.kernel(
        body, name=name, out_type=[_sds(s, x.dtype) for s, x in zip(land_shapes, xs)],
        mesh=plsc.ScalarSubcoreMesh(axis_name="sequencer", num_cores=1),
        scratch_types=[pltpu.SemaphoreType.DMA, pltpu.SemaphoreType.DMA, pltpu.SemaphoreType.DMA],
        compiler_params=pltpu.CompilerParams(collective_id=collective_id),
    )(*xs)


def _sc_allgather(xs, *, collective_id, name):
    na = len(xs)

    def body(*refs):
        x_refs, out_refs = refs[:na], refs[na:2 * na]
        send_sems, recv_sems, local_sems = refs[2 * na:]
        x, y, c = _mesh_pos()
        barrier = pltpu.get_barrier_semaphore()
        for fx, fy, fc in PEER_FLIPS:
            pl.semaphore_signal(barrier, inc=1, device_id=(x ^ fx, y ^ fy, c ^ fc), device_id_type=MESH)
        pl.semaphore_wait(barrier, len(PEER_FLIPS))
        _allgather_body(x_refs, out_refs, send_sems, recv_sems, local_sems,
                        lambda ref, pos: ref.at[4 * pos[0] + 2 * pos[1] + pos[2]])

    return pl.kernel(
        body, name=name, out_type=[_sds((N_DEV,) + x.shape, x.dtype) for x in xs],
        mesh=plsc.ScalarSubcoreMesh(axis_name="sequencer", num_cores=1),
        scratch_types=[pltpu.SemaphoreType.DMA((na, 7)), pltpu.SemaphoreType.DMA((na, 7)),
                       pltpu.SemaphoreType.DMA((na,))],
        compiler_params=pltpu.CompilerParams(collective_id=collective_id),
    )(*xs)


def _allgather_vmem(x, *, name):
    r, c = x.shape

    def body(x_ref, out_ref, send_sems, recv_sems, local_sems):
        _allgather_body([x_ref], [out_ref], send_sems, recv_sems, local_sems,
                        lambda ref, pos: ref.at[pl.ds((4 * pos[0] + 2 * pos[1] + pos[2]) * r, r), :])

    vm = pl.BlockSpec(memory_space=pltpu.VMEM)
    return pl.pallas_call(
        body, name=name, in_specs=[vm], out_specs=vm, out_shape=_sds((N_DEV * r, c), x.dtype),
        scratch_shapes=[pltpu.SemaphoreType.DMA((1, 7)), pltpu.SemaphoreType.DMA((1, 7)),
                        pltpu.SemaphoreType.DMA((1,))],
    )(x)


def _sum_slots(gathered, *, name):
    _, r, c = gathered.shape

    def body(g_ref, o_ref):
        acc = g_ref[0]
        for k in range(1, N_DEV):
            acc = acc + g_ref[k]
        o_ref[...] = acc

    return pl.pallas_call(body, name=name, out_shape=_sds((r, c), gathered.dtype))(gathered)


def _t5_bucket(rel):
    nb = REL_BUCKETS // 2
    ret = jnp.where(rel > 0, nb, 0)
    n = jnp.abs(rel)
    max_exact = nb // 2
    nf = jnp.maximum(n, 1).astype(F32)
    large = max_exact + (jnp.log(nf / max_exact) / math.log(REL_MAX_DIST / max_exact)
                         * (nb - max_exact)).astype(jnp.int32)
    large = jnp.minimum(large, nb - 1)
    return ret + jnp.where(n < max_exact, n, large)


def _band_pattern(block, radius, dil):
    kw = block + 2 * radius
    rel = jnp.arange(kw)[None, :] - radius - jnp.arange(block)[:, None]
    return jnp.where(jnp.abs(rel) <= radius, _t5_bucket(rel * dil), -1).astype(jnp.int32).reshape(1, block * kw)


def _rope_tables():
    lane = np.arange(64)
    seg, j = lane // 32, lane % 32
    inv = ROPE_THETA ** (-jnp.arange(0, 32, 2, dtype=F32) / 32)
    tpos = jnp.arange(SEQ)
    pos = jnp.where(jnp.asarray(seg)[None, :] == 0, (tpos // GRID_W)[:, None], (tpos % GRID_W)[:, None])
    ang = pos.astype(F32) * inv[jnp.asarray(j % 16)][None, :]
    cos = jnp.cos(ang)
    sins = jnp.where(jnp.asarray(j)[None, :] < 16, -jnp.sin(ang), jnp.sin(ang))
    return jnp.tile(cos, (1, 4)), jnp.tile(sins, (1, 4))


A_Q, A_K, A_V = (256, 0), (256, 1), (256, 2)
B_Q, B_K, B_V = (256, 0), (128, 2), (128, 3)
A_HEADS = dict(rad=A_RADIUS, nh=4, nkv=4)
B_HEADS = dict(rad=SWA_RADIUS, nh=4, nkv=2)


def _local_step(x, pe, tgt, rel_bias, wts, matmul_weights, grads_ready):
    t = x.shape[0]
    bl = t // SEQ
    cos, sins = _rope_tables()
    blocks_a = [min(BAND_BLOCK, SEQ // d) for d in DILATIONS]
    pats_a = [_band_pattern(blk, A_RADIUS, d) for blk, d in zip(blocks_a, DILATIONS)]
    pat_b = _band_pattern(BAND_BLOCK, SWA_RADIUS, 1)
    table_t = rel_bias.T
    bias_a = [_bias_lookup(table_t[:4], pt, name=f"bias_a{ci}").reshape(4, blk, blk + 2 * A_RADIUS)
              for ci, (pt, blk) in enumerate(zip(pats_a, blocks_a))]
    bias_b = _bias_lookup(table_t[4:], pat_b, name="bias_b").reshape(4, BAND_BLOCK, BAND_BLOCK + 2 * SWA_RADIUS)
    nat4 = lambda a: a.reshape(bl, 1, SEQ, a.shape[-1])

    saved = []
    for li in range(DEPTH):
        w = dict(wts[li])
        w.update(matmul_weights(li, "in", x))
        hn0, proj = _norm_mm((x,), w["g_mix"], w["w_in"], None, tm=1024, tn=1152, name="mix_in_fwd", out_dtype=BF16)
        qa1, qa4, qa16, qb, qd = _qkprep_fwd(proj, w["qk_gains"], cos, sins, tm=512, name="qkprep_fwd")
        qa = (nat4(qa1), qa4, qa16)
        oa, la = [], []
        for ci in range(3):
            o, l = _band_fwd(qa[ci], A_Q, A_K, A_V, bias_a[ci], None, name=f"band_a{ci}_fwd", **A_HEADS)
            oa.append(o)
            la.append(l)
        oa[0], la[0] = oa[0].reshape(t, 256), la[0].reshape(t, 256)
        ya, lse_a = _combine_a(oa, la, tm=512, name="combine_a")
        yb, lse_b = _band_fwd(nat4(qb), B_Q, B_K, B_V, bias_b, w["sink_t"], name="band_b_fwd", **B_HEADS)
        yb = yb.reshape(t, 256)
        yc = _c_fwd(proj, w["c_g"], w["c_b"], w["c_ws"], w["c_bst"], tm=512, name="c_fwd")
        yd, lse_d = _dense_fwd(qd, tq=256, name="dense_fwd")
        w.update(matmul_weights(li, "rest", yd))
        mixed, x1 = _norm_mm((ya, yb, yc, yd), w["out_gain"], w["w_out"], x, tm=1024, tn=1024, name="mix_out_fwd")
        hn1, h = _norm_mm((x1,), w["g_ffn"], w["w_up"], None, tm=1024, tn=2816, name="ffn_up_fwd", out_dtype=BF16)
        act, cg, cu = _conv_gate_fwd(h, w["conv_w"], w["conv_b"], name="conv_gate_fwd")
        x2 = _mm(act, w["w_down"], "nn", x1, tm=1024, tn=1024, out_dtype=F32, name="ffn_down_fwd")
        hn2, x3, gate, pp = _ple_fwd(x2, w["g_ple"], w["w_gate"], pe, li * (t // 1024), w["w_proj"], tm=1024,
                                     name="ple_fwd")
        saved.append(dict(w=w, x0=x, hn0=hn0, proj=proj, qa=qa, qb=qb, qd=qd, ya=ya, lse_a=lse_a, yb=yb, lse_b=lse_b,
                          yc=yc, yd=yd, lse_d=lse_d, mixed=mixed, x1=x1, hn1=hn1, h=h, cg=cg, cu=cu, act=act, x2=x2, hn2=hn2,
                          gate=gate, pp=pp))
        x = x3

    loss_tile, dx = _loss_head(x, tgt, tm=512, name="loss_head")
    grads = [None] * DEPTH
    dbias_a, dbias_bs = [[], [], []], []
    for li in reversed(range(DEPTH)):
        s = saved[li]
        w = s["w"]
        g = {}
        dz, dpp = _ple_bwd_ew(dx, s["gate"], s["pp"], tm=512, name="ple_bwd_ew")
        g["w_gate"] = _mm(s["hn2"], dz, "tn", None, tm=1024, tn=512, out_dtype=BF16, name="dw_gate")
        g["w_proj"] = _mm(pe, dpp, "tn", None, tm=256, tn=1024, out_dtype=BF16, name="dw_proj", a_rows=(li, t))
        dx2, dx2b, g["g_ple"] = _mm_bt_normbwd((dz,), w["w_gate"], (s["x2"],), w["g_ple"], dx, tm=1024, tn=1024,
                                               name="ple_bwd", emit_bf16=True)
        g["w_down"] = _mm(s["act"], dx2b, "tn", None, tm=1408, tn=512, out_dtype=BF16, name="dw_down")
        dact = _mm(dx2b, w["w_down"], "nt", None, tm=1024, tn=2816, out_dtype=BF16, name="ffn_down_bwd")
        dhg, dhu, g["conv_w"], g["conv_b"] = _conv_gate_bwd(s["h"], s["cg"], s["cu"], dact, w["conv_w"], w["conv_b"],
                                                            name="conv_gate_bwd")
        g["w_up"] = jnp.concatenate(
            [_mm(s["hn1"], dhalf, "tn", None, tm=1024, tn=1408, out_dtype=BF16, name=f"dw_up_{nm}")
             for nm, dhalf in (("gate", dhg), ("up", dhu))], axis=1)
        dx1, dx1b, g["g_ffn"] = _mm_bt_normbwd((dhg, dhu), w["w_up"], (s["x1"],), w["g_ffn"], dx2, tm=1024, tn=1408,
                                               name="ffn_up_bwd", emit_bf16=True)
        g["w_out"] = _mm(s["mixed"], dx1b, "tn", None, tm=1024, tn=512, out_dtype=BF16, name="dw_out")
        grads_ready(li, "mid", g)
        dycat, g["out_gain"] = _mm_bt_normbwd((dx1b,), w["w_out"], (s["ya"], s["yb"], s["yc"], s["yd"]), w["out_gain"],
                                              None, tm=1024, tn=1024, name="mix_out_bwd")
        dy_r, lse_r, dl_a, dl_b, dl_d = _deltas(dycat, s["ya"], s["yb"], s["yd"], s["lse_a"], tm=512, name="deltas")
        dy_a = (nat4(dycat),) + tuple(dy_r)
        lse_a = (nat4(s["lse_a"]),) + tuple(lse_r)
        dl_a = (nat4(dl_a[0]),) + tuple(dl_a[1:])
        da = []
        for ci in range(3):
            dq, dk, dv, dbias = _band_bwd(s["qa"][ci], A_Q, A_K, A_V, bias_a[ci], None, dy_a[ci], 0, lse_a[ci],
                                          dl_a[ci], name=f"band_a{ci}_bwd", **A_HEADS)
            if ci == 0:
                dq, dk, dv = (a.reshape(t, 256) for a in (dq, dk, dv))
            da.append((dq, dk, dv))
            dbias_a[ci].append(dbias.reshape(4, -1))
        dqb, dkb, dvb, dbias_b, dsink = _band_bwd(nat4(s["qb"]), B_Q, B_K, B_V, bias_b, w["sink_t"], nat4(dycat), 1,
                                                  nat4(s["lse_b"]), nat4(dl_b), name="band_b_bwd", **B_HEADS)
        dbias_bs.append(dbias_b.reshape(4, -1))
        g["sink"] = dsink[:, 0, 0]
        dd = _dense_bwd(s["qd"], dycat, s["lse_d"], dl_d, tq=128, name="dense_bwd")
        dcu, dcv, g["c_ws"], dbs, g["c_g"], g["c_b"] = _c_bwd(s["proj"], dycat, w["c_g"], w["c_b"], w["c_ws"],
                                                               w["c_wst"], w["c_bst"], tm=512, name="c_bwd")
        g["c_bs"] = dbs[:, ::64].T
        db = (dqb.reshape(t, 256), dkb.reshape(t, 128), dvb.reshape(t, 128))
        dproj, dgains = _qkprep_bwd(s["proj"], da, db, dd, dcu, dcv, w["qk_gains"], cos, sins, tm=512, name="qkprep_bwd")
        g["qk_gain"] = dgains[:6, :64].reshape(3, 2, HEAD_DIM)
        g["w_in"] = _mm(s["hn0"], dproj, "tn", None, tm=1024, tn=768, out_dtype=BF16, name="dw_in")
        dx, g["g_mix"] = _mm_bt_normbwd((dproj,), w["w_in"], (s["x0"],), w["g_mix"], dx1, tm=1024, tn=1152,
                                        name="mix_in_bwd")
        grads[li] = g
        grads_ready(li, "end", g)
    d_table_a = sum(_bucket_reduce(dbias_a[ci], pats_a[ci], name=f"bucket_a{ci}") for ci in range(3))
    d_table_b = _bucket_reduce(dbias_bs, pat_b, name="bucket_b")
    d_rel_bias = jnp.concatenate([d_table_a, d_table_b], axis=0).T
    return loss_tile[0, 0], dx, grads, d_rel_bias


WEIGHT_NAMES = ("rel_bias", "ln_mix_g", "w_in", "qk_gain", "sink", "c_norm_g", "c_norm_b", "c_ws", "c_bs", "out_gain",
                "w_out", "ln_ffn_g", "w_up", "conv_w", "conv_b", "w_down", "ln_ple_g", "w_ple_gate", "w_ple_proj")
COL_SHARDED = ("w_in", "w_up", "w_ple_proj")
ROW_SHARDED = ("w_out", "w_down", "w_ple_gate")
SMALL_SHARDED = ("conv_w", "out_gain")
REPLICATED = tuple(n for n in WEIGHT_NAMES if n not in COL_SHARDED + ROW_SHARDED + SMALL_SHARDED)
LOCAL_GRAD_KEY = {"ln_mix_g": "g_mix", "ln_ffn_g": "g_ffn", "ln_ple_g": "g_ple", "c_norm_g": "c_g", "c_norm_b": "c_b",
                  "w_ple_gate": "w_gate", "w_ple_proj": "w_proj"}


def _full_from_gathered(name, gathered):
    _, r, c = gathered.shape
    if name in ROW_SHARDED:
        return gathered.reshape(N_DEV * r, c)
    return jnp.transpose(gathered, (1, 0, 2)).reshape(r, N_DEV * c)


def _slots_from_full(name, full):
    rows, cols = full.shape
    if name in ROW_SHARDED:
        return full.reshape(N_DEV, rows // N_DEV, cols)
    return jnp.transpose(full.reshape(rows, N_DEV, cols // N_DEV), (1, 0, 2))


def _piece_rows(shape):
    return -(-int(np.prod(shape)) // 1024) * 8


def _pack_rows(arrays):
    pieces = []
    for a in arrays:
        n, rows = int(np.prod(a.shape)), _piece_rows(a.shape)
        flat = a.astype(F32).reshape(-1)
        if n != rows * LANES:
            flat = jnp.pad(flat, (0, rows * LANES - n))
        pieces.append(flat.reshape(rows, LANES))
    return jnp.concatenate(pieces, axis=0)


def _unpack_rows(packed, shapes):
    out, off = [], 0
    for shp in shapes:
        n, rows = int(np.prod(shp)), _piece_rows(shp)
        piece = packed[off:off + rows]
        out.append((piece if n == rows * LANES else piece.reshape(-1)[:n]).reshape(shp))
        off += rows
    return out


def kernel(x, p, rel_bias, ln_mix_g, w_in, qk_gain, sink, c_norm_g, c_norm_b, c_ws, c_bs, out_gain, w_out, ln_ffn_g, w_up, conv_w, conv_b, w_down, ln_ple_g, w_ple_gate, w_ple_proj, loss_target, m_rel_bias, m_ln_mix_g, m_w_in, m_qk_gain, m_sink, m_c_norm_g, m_c_norm_b, m_c_ws, m_c_bs, m_out_gain, m_w_out, m_ln_ffn_g, m_w_up, m_conv_w, m_conv_b, m_w_down, m_ln_ple_g, m_w_ple_gate, m_w_ple_proj, v_rel_bias, v_ln_mix_g, v_w_in, v_qk_gain, v_sink, v_c_norm_g, v_c_norm_b, v_c_ws, v_c_bs, v_out_gain, v_w_out, v_ln_ffn_g, v_w_up, v_conv_w, v_conv_b, v_w_down, v_ln_ple_g, v_w_ple_gate, v_w_ple_proj):
    env = dict(locals())
    wt = {n: env[n] for n in WEIGHT_NAMES}
    mom_m = {n: env["m_" + n] for n in WEIGHT_NAMES}
    mom_v = {n: env["v_" + n] for n in WEIGHT_NAMES}
    bl = x.shape[0]
    t = bl * SEQ
    me = 4 * lax.axis_index("x") + 2 * lax.axis_index("y") + lax.axis_index("c")

    big = COL_SHARDED + ROW_SHARDED
    full = {}
    small_shapes = [wt[n].shape for n in SMALL_SHARDED]
    small = _allgather_vmem(_pack_rows([wt[n] for n in SMALL_SHARDED]), name="gather_small")
    small = small.reshape(N_DEV, -1)
    off = 0
    for n, shp in zip(SMALL_SHARDED, small_shapes):
        cnt = int(np.prod(shp))
        g = small[:, off:off + cnt].reshape((N_DEV,) + tuple(shp))
        full[n] = jnp.transpose(g, (1, 2, 0, 3)).reshape(shp[0], shp[1], N_DEV * shp[2])
        off += _piece_rows(shp) * LANES

    def head_gain(li, a, b, reps):
        g = jnp.tile(qk_gain[li, a, b], reps)
        return jnp.pad(g, (0, 256 - g.shape[0]))

    wts = []
    for li in range(DEPTH):
        rows = [head_gain(li, 0, 0, 4), head_gain(li, 0, 1, 4), head_gain(li, 1, 0, 4), head_gain(li, 1, 1, 2),
                head_gain(li, 2, 0, 4), head_gain(li, 2, 1, 2), jnp.zeros((256,), F32), jnp.zeros((256,), F32)]
        wts.append(dict(
            g_mix=ln_mix_g[li].reshape(1, -1), qk_gains=jnp.stack(rows),
            sink_t=jnp.broadcast_to(sink[li][:, None, None], (4, 8, 128)),
            c_g=c_norm_g[li].reshape(1, -1), c_b=c_norm_b[li].reshape(1, -1), c_ws=c_ws[li].astype(BF16),
            c_wst=jnp.transpose(c_ws[li], (0, 2, 1)).astype(BF16), c_bst=jnp.repeat(c_bs[li].T, 64, axis=1),
            out_gain=full["out_gain"][li].reshape(1, -1), g_ffn=ln_ffn_g[li].reshape(1, -1),
            conv_w=full["conv_w"][li], conv_b=conv_b[li].reshape(1, -1), g_ple=ln_ple_g[li].reshape(1, -1)))

    local_key = {"w_ple_gate": "w_gate", "w_ple_proj": "w_proj"}

    gather_names = {"in": ("w_in",), "rest": tuple(n for n in big if n != "w_in")}
    gathered = {}
    for cid, (li, names) in enumerate(((0, gather_names["in"]), (0, gather_names["rest"]), (1, big))):
        lands = _sc_allgather([wt[n][li].astype(BF16) for n in names], collective_id=cid,
                              name=f"gather_{li}_{len(names)}")
        gathered.setdefault(li, {}).update(zip(names, lands))

    def matmul_weights(li, part, after):
        out = {}
        for n in gather_names[part]:
            g, _ = lax.optimization_barrier((gathered[li][n], after))
            out[local_key.get(n, n)] = _full_from_gathered(n, g)
        return out

    mid_names = ("w_ple_gate", "w_ple_proj", "w_down", "w_up", "w_out")
    end_names = ("w_in",)
    landed = {}

    def start_exchange(li, names, g, tag, cid):
        slots = [_slots_from_full(n, g[local_key.get(n, n)]) for n in names]
        lands = _sc_exchange(slots, scatter=True, collective_id=cid, name=f"grads_{li}_{tag}")
        landed.update({(n, li): land for n, land in zip(names, lands)})

    def grads_ready(li, stage, g):
        if li == 0:
            start_exchange(li, mid_names if stage == "mid" else end_names, g, stage, 5 if stage == "mid" else 6)
        elif stage == "end":
            start_exchange(li, mid_names + end_names, g, stage, 4)

    loss_part, dx, grads, d_rel_bias = _local_step(
        x.reshape(t, D_MODEL), p.reshape(DEPTH * t, PLE_DIM), loss_target.reshape(t, D_MODEL), rel_bias, wts,
        matmul_weights, grads_ready)

    def local_grad(n):
        if n == "rel_bias":
            return d_rel_bias
        key = LOCAL_GRAD_KEY.get(n, n)
        return jnp.stack([grads[li][key].reshape(wt[n].shape[1:]) if n in REPLICATED else grads[li][key]
                          for li in range(DEPTH)])

    small_names = REPLICATED + SMALL_SHARDED
    small_full_shapes = [wt[n].shape if n in REPLICATED else full[n].shape for n in small_names]
    small_parts = _allgather_vmem(_pack_rows([local_grad(n) for n in small_names] + [loss_part.reshape(1)]),
                                  name="allgather_small_grads")
    small_parts = small_parts.reshape(N_DEV, -1, LANES)

    out_g, out_d, out_m, out_v = {}, {}, {}, {}
    for n in big:
        shp = wt[n].shape
        two_d = lambda a: a.reshape(-1, shp[-1])
        res = _adamw_reduce([landed[n, li] for li in range(DEPTH)], two_d(wt[n]), two_d(mom_m[n]), two_d(mom_v[n]),
                            tr=32 if n == "w_down" else 128, name="adamw_" + n)
        out_g[n], out_d[n], out_m[n], out_v[n] = [r.reshape(shp) for r in res]

    *reduced, loss = _unpack_rows(_sum_slots(small_parts, name="sum_small_grads"), small_full_shapes + [(1,)])
    loss = loss[0]
    reduced = dict(zip(small_names, reduced))
    rep_shapes = [wt[n].shape for n in REPLICATED]
    upd = _adamw_plain(_pack_rows([reduced[n] for n in REPLICATED]), _pack_rows([wt[n] for n in REPLICATED]),
                       _pack_rows([mom_m[n] for n in REPLICATED]), _pack_rows([mom_v[n] for n in REPLICATED]),
                       name="adamw_replicated")
    for dst, packed in zip((out_d, out_m, out_v), upd):
        dst.update(zip(REPLICATED, _unpack_rows(packed, rep_shapes)))
    for n in REPLICATED:
        out_g[n] = reduced[n]
    for n in SMALL_SHARDED:
        shp = wt[n].shape
        g = reduced[n].reshape(shp[0], shp[1], N_DEV, shp[2])
        g = lax.dynamic_index_in_dim(g, me, axis=2, keepdims=False)
        two_d = lambda a: a.reshape(-1, shp[-1])
        res = _adamw_plain(two_d(g), two_d(wt[n]), two_d(mom_m[n]), two_d(mom_v[n]), name="adamw_" + n)
        out_g[n] = g
        out_d[n], out_m[n], out_v[n] = [r.reshape(shp) for r in res]

    return (loss, dx.reshape(bl, SEQ, D_MODEL), *[out_g[n] for n in WEIGHT_NAMES], *[out_d[n] for n in WEIGHT_NAMES],
            *[out_m[n] for n in WEIGHT_NAMES], *[out_v[n] for n in WEIGHT_NAMES])
```

```python
import math

import jax
import jax.numpy as jnp
import numpy as np
from jax import lax
from jax.experimental import pallas as pl
from jax.experimental.pallas import tpu as pltpu
from jax.experimental.pallas import tpu_sc as plsc

F32 = jnp.float32
BF16 = jnp.bfloat16

N_DEV = 8
D_MODEL = 1024
SEQ = 2048
DEPTH = 2
HEAD_DIM = 64
IN_WIDTH = 2304
D_FF = 2816
PLE_DIM = 256
C_CHUNK = 128
C_GROUPS = 4
DILATED_CFGS = ((128, 1), (512, 4), (2048, 16))
DILATIONS = tuple(d for _, d in DILATED_CFGS)
A_RADIUS = 64
SWA_RADIUS = 128
BAND_BLOCK = 256
GRID_W = 64
ROPE_THETA = 10000.0
REL_BUCKETS = 32
REL_MAX_DIST = 1024
EPS = 1e-6
NEG_INF = -1e30
ATTN_SCALE = HEAD_DIM ** -0.5
LANES = 128

ADAM_LR = 0.001
ADAM_B1 = 0.9
ADAM_B2 = 0.999
ADAM_EPS = 1e-08
ADAM_WD = 0.01
ADAM_STEP = 10

MESH = pl.DeviceIdType.MESH
NT = (((1,), (1,)), ((), ()))
TN = (((0,), (0,)), ((), ()))
ARB = "arbitrary"
PAR = "parallel"


def _cparams(*sem):
    return pltpu.CompilerParams(dimension_semantics=tuple(sem))


def _sds(shape, dtype):
    return jax.ShapeDtypeStruct(tuple(shape), dtype)


def _group_sum_matrix(n, same_group):
    r = lax.broadcasted_iota(jnp.int32, (n, n), 0)
    c = lax.broadcasted_iota(jnp.int32, (n, n), 1)
    if same_group:
        return ((r >> 6) == (c >> 6)).astype(F32)
    return ((r & 63) == (c & 63)).astype(F32)


def _seg_sum(x, e):
    eb = e.astype(BF16)
    hi = x.astype(BF16)
    lo = (x - hi.astype(F32)).astype(BF16)
    return jnp.dot(hi, eb, preferred_element_type=F32) + jnp.dot(lo, eb, preferred_element_type=F32)


def _gelu(x):
    c = math.sqrt(2.0 / math.pi)
    return 0.5 * x * (1.0 + jnp.tanh(c * (x + 0.044715 * (x * x * x))))


def _gelu_grad(x):
    c = math.sqrt(2.0 / math.pi)
    t = jnp.tanh(c * (x + 0.044715 * (x * x * x)))
    return 0.5 * (1.0 + t) + 0.5 * x * (1.0 - t * t) * c * (1.0 + 3.0 * 0.044715 * (x * x))


def _sigmoid(x):
    return 1.0 / (1.0 + jnp.exp(-x))


def _scatter_cols(scratch, first, val):
    for c in range(val.shape[1] // LANES):
        scratch[first + c] = val[:, c * LANES:(c + 1) * LANES]


def _gather_cols(scratch, first, ncol):
    return jnp.concatenate([scratch[first + c] for c in range(ncol)], axis=1)


def _read_residue(scratch, first, ncol, r, d):
    n = scratch.shape[1] // d
    return jnp.concatenate([scratch.at[first + c][pl.ds(r, n, stride=d), :] for c in range(ncol)], axis=1)


def _write_residue(scratch, first, r, d, val):
    n = scratch.shape[1] // d
    for c in range(val.shape[1] // LANES):
        scratch.at[first + c][pl.ds(r, n, stride=d), :] = val[:, c * LANES:(c + 1) * LANES]


def _norm_mm(xs, gain, w, res, *, tm, tn, name, out_dtype=F32):
    t = xs[0].shape[0]
    k = sum(x.shape[1] for x in xs)
    n = w.shape[1]
    ng = len(xs)
    has_res = res is not None

    def body(*refs):
        x_refs = refs[:ng]
        g_ref, w_ref = refs[ng], refs[ng + 1]
        res_ref = refs[ng + 2] if has_res else None
        hn_ref, o_ref, hn_s = refs[ng + 2 + has_res:]

        @pl.when(pl.program_id(1) == 0)
        def _():
            off = 0
            for xr in x_refs:
                x = xr[...].astype(F32)
                wd = x.shape[1]
                r = lax.rsqrt(jnp.mean(x * x, axis=-1, keepdims=True) + EPS)
                hn_s[:, off:off + wd] = (x * r * g_ref[:, off:off + wd]).astype(BF16)
                off += wd
            hn_ref[...] = hn_s[...]

        acc = jnp.dot(hn_s[...], w_ref[...], preferred_element_type=F32)
        if has_res:
            acc = acc + res_ref[...]
        o_ref[...] = acc.astype(out_dtype)

    in_specs = [pl.BlockSpec((tm, x.shape[1]), lambda i, j: (i, 0)) for x in xs]
    in_specs += [pl.BlockSpec((1, k), lambda i, j: (0, 0)), pl.BlockSpec((k, tn), lambda i, j: (0, j))]
    args = list(xs) + [gain, w]
    if has_res:
        in_specs.append(pl.BlockSpec((tm, tn), lambda i, j: (i, j)))
        args.append(res)
    return pl.pallas_call(
        body, name=name, grid=(t // tm, n // tn), in_specs=in_specs,
        out_specs=[pl.BlockSpec((tm, k), lambda i, j: (i, 0)), pl.BlockSpec((tm, tn), lambda i, j: (i, j))],
        out_shape=[_sds((t, k), BF16), _sds((t, n), out_dtype)],
        scratch_shapes=[pltpu.VMEM((tm, k), BF16)],
        compiler_params=_cparams(PAR, ARB),
    )(*args)


def _mm(a, b, mode, res, *, tm, tn, out_dtype, name, a_rows=None):
    if mode == "tn":
        kk, m = a.shape
        blk_a = 0
        if a_rows is not None:
            blk_a, kk = a_rows
        a_spec = pl.BlockSpec((kk, tm), lambda i, j: (blk_a, i))
    else:
        m, kk = a.shape
        a_spec = pl.BlockSpec((tm, kk), lambda i, j: (i, 0))
    if mode == "nt":
        n = b.shape[0]
        b_spec = pl.BlockSpec((tn, kk), lambda i, j: (j, 0))
    else:
        n = b.shape[1]
        b_spec = pl.BlockSpec((kk, tn), lambda i, j: (0, j))
    has_res = res is not None

    def body(*refs):
        a_ref, b_ref = refs[0], refs[1]
        o_ref = refs[-1]
        av = a_ref[...].astype(BF16)
        bv = b_ref[...].astype(BF16)
        if mode == "nn":
            acc = jnp.dot(av, bv, preferred_element_type=F32)
        elif mode == "nt":
            acc = lax.dot_general(av, bv, NT, preferred_element_type=F32)
        else:
            acc = lax.dot_general(av, bv, TN, preferred_element_type=F32)
        if has_res:
            acc = acc + refs[2][...]
        o_ref[...] = acc.astype(out_dtype)

    in_specs = [a_spec, b_spec]
    args = [a, b]
    if has_res:
        in_specs.append(pl.BlockSpec((tm, tn), lambda i, j: (i, j)))
        args.append(res)
    return pl.pallas_call(
        body, name=name, grid=(m // tm, n // tn), in_specs=in_specs,
        out_specs=pl.BlockSpec((tm, tn), lambda i, j: (i, j)),
        out_shape=_sds((m, n), out_dtype),
        compiler_params=_cparams(PAR, PAR),
    )(*args)


def _mm_bt_normbwd(dys, w, xs, gain, dres, *, tm, tn, name, emit_bf16=False):
    t, wd_each = dys[0].shape
    nd = len(dys)
    per = wd_each // tn
    nj = nd * per
    k = w.shape[0]
    ng = len(xs)
    has_res = dres is not None

    def body(*refs):
        dy_refs = refs[:nd]
        w_ref = refs[nd]
        x_refs = refs[nd + 1:nd + 1 + ng]
        g_ref = refs[nd + 1 + ng]
        dres_ref = refs[nd + 2 + ng] if has_res else None
        outs = refs[nd + 2 + ng + has_res:]
        dx_ref = outs[0]
        dxb_ref = outs[1] if emit_bf16 else None
        dg_ref, acc = outs[1 + emit_bf16:]
        i, j = pl.program_id(0), pl.program_id(1)

        @pl.when(j == 0)
        def _():
            acc[...] = jnp.zeros_like(acc)

        for d, dy_ref in enumerate(dy_refs):
            @pl.when((j >= d * per) & (j < (d + 1) * per))
            def _(dy_ref=dy_ref):
                acc[...] += lax.dot_general(dy_ref[...].astype(BF16), w_ref[...], NT, preferred_element_type=F32)

        @pl.when(j == nj - 1)
        def _():
            @pl.when(i == 0)
            def _():
                dg_ref[...] = jnp.zeros_like(dg_ref)

            off = 0
            for xr in x_refs:
                x = xr[...].astype(F32)
                wd = x.shape[1]
                g = g_ref[:, off:off + wd]
                dyn = acc[:, off:off + wd]
                r = lax.rsqrt(jnp.mean(x * x, axis=-1, keepdims=True) + EPS)
                gdy = dyn * g
                dx = r * gdy - x * (r * r * r * jnp.mean(gdy * x, axis=-1, keepdims=True))
                if has_res:
                    dx = dx + dres_ref[:, off:off + wd]
                dx_ref[:, off:off + wd] = dx
                if emit_bf16:
                    dxb_ref[:, off:off + wd] = dx.astype(BF16)
                dg_ref[:, off:off + wd] += jnp.sum(dyn * x * r, axis=0, keepdims=True)
                off += wd

    def dy_map(d):
        return lambda i, j: (i, jnp.clip(j - d * per, 0, per - 1))

    in_specs = [pl.BlockSpec((tm, tn), dy_map(d)) for d in range(nd)]
    in_specs.append(pl.BlockSpec((k, tn), lambda i, j: (0, j)))
    in_specs += [pl.BlockSpec((tm, x.shape[1]), lambda i, j: (i, 0)) for x in xs]
    in_specs.append(pl.BlockSpec((1, k), lambda i, j: (0, 0)))
    args = list(dys) + [w] + list(xs) + [gain]
    if has_res:
        in_specs.append(pl.BlockSpec((tm, k), lambda i, j: (i, 0)))
        args.append(dres)
    row = pl.BlockSpec((tm, k), lambda i, j: (i, 0))
    out_specs = [row] + ([row] if emit_bf16 else []) + [pl.BlockSpec((1, k), lambda i, j: (0, 0))]
    out_shape = [_sds((t, k), F32)] + ([_sds((t, k), BF16)] if emit_bf16 else []) + [_sds((1, k), F32)]
    return pl.pallas_call(
        body, name=name, grid=(t // tm, nj), in_specs=in_specs, out_specs=out_specs, out_shape=out_shape,
        scratch_shapes=[pltpu.VMEM((tm, k), F32)],
        compiler_params=_cparams(ARB, ARB),
    )(*args)


def _rope_partner(y):
    n = y.shape[1]
    lane = lax.broadcasted_iota(jnp.int32, y.shape, 1)
    return jnp.where((lane & 31) < 16, pltpu.roll(y, n - 16, 1), pltpu.roll(y, 16, 1))


def _residue_specs(tm, width, nt):
    specs = [pl.BlockSpec((tm, width), lambda b, i: (b * nt + i, 0))]
    for d in DILATIONS[1:]:
        specs.append(pl.BlockSpec((None, d, tm // d, width), lambda b, i: (b, 0, i, 0)))
    return specs


def _residue_shapes(bl, width, dtype):
    return [_sds((bl * SEQ, width), dtype)] + [_sds((bl, d, SEQ // d, width), dtype) for d in DILATIONS[1:]]


def _qkprep_fwd(proj, gains, cos, sins, *, tm, name):
    t = proj.shape[0]
    bl = t // SEQ
    nt = SEQ // tm

    def body(p_ref, g_ref, c_ref, s_ref, qa1_ref, qa4_ref, qa16_ref, qb_ref, qd_ref, scr):
        e = _group_sum_matrix(256, True)

        def hn(x, row):
            x = x.astype(F32)
            wd = x.shape[1]
            ms = _seg_sum(x * x, e[:wd, :wd]) * (1.0 / HEAD_DIM)
            return x * lax.rsqrt(ms + EPS) * g_ref[row:row + 1, :wd]

        qa = jnp.concatenate([hn(p_ref[:, 0:256], 0) * ATTN_SCALE, hn(p_ref[:, 256:512], 1),
                              p_ref[:, 512:768].astype(F32)], axis=1)
        qa1_ref[...] = qa.astype(BF16)
        _scatter_cols(scr, 0, qa)
        for d, ref in ((4, qa4_ref), (16, qa16_ref)):
            for r in range(d):
                ref[r] = _read_residue(scr, 0, 6, r, d).astype(BF16)
        qb_ref[:, 0:256] = (hn(p_ref[:, 768:1024], 2) * ATTN_SCALE).astype(BF16)
        qb_ref[:, 256:384] = hn(p_ref[:, 1024:1152], 3).astype(BF16)
        qb_ref[:, 384:512] = p_ref[:, 1152:1280].astype(BF16)
        yq = hn(p_ref[:, 1792:2048], 4)
        yq = yq * c_ref[...] + _rope_partner(yq) * s_ref[...]
        qd_ref[:, 0:256] = (yq * ATTN_SCALE).astype(BF16)
        yk = hn(p_ref[:, 2048:2176], 5)
        yk = yk * c_ref[:, 0:128] + _rope_partner(yk) * s_ref[:, 0:128]
        qd_ref[:, 256:384] = yk.astype(BF16)
        qd_ref[:, 384:512] = p_ref[:, 2176:2304].astype(BF16)

    row = lambda width: pl.BlockSpec((tm, width), lambda b, i: (b * nt + i, 0))
    tab = pl.BlockSpec((tm, 256), lambda b, i: (i, 0))
    return pl.pallas_call(
        body, name=name, grid=(bl, nt),
        in_specs=[row(IN_WIDTH), pl.BlockSpec((8, 256), lambda b, i: (0, 0)), tab, tab],
        out_specs=_residue_specs(tm, 768, nt) + [row(512), row(512)],
        out_shape=_residue_shapes(bl, 768, BF16) + [_sds((t, 512), BF16), _sds((t, 512), BF16)],
        scratch_shapes=[pltpu.VMEM((6, tm, LANES), F32)],
        compiler_params=_cparams(PAR, PAR),
    )(proj, gains, cos, sins)


def _qkprep_bwd(proj, da, db, dd, dcu, dcv, gains, cos, sins, *, tm, name):
    t = proj.shape[0]
    bl = t // SEQ
    nt = SEQ // tm
    flat = [a for cfg in da for a in cfg] + list(db) + list(dd) + [dcu, dcv]

    def body(*refs):
        p_ref, g_ref, c_ref, s_ref = refs[:4]
        d_refs = refs[4:4 + len(flat)]
        dp_ref, dg_ref, scr = refs[4 + len(flat):]
        a_refs = d_refs[:9]
        dqb_ref, dkb_ref, dvb_ref, dqd_ref, dkd_ref, dvd_ref, dcu_ref, dcv_ref = d_refs[9:]
        e = _group_sum_matrix(256, True)
        first = (pl.program_id(0) == 0) & (pl.program_id(1) == 0)
        last = (pl.program_id(0) == bl - 1) & (pl.program_id(1) == nt - 1)

        @pl.when(first)
        def _():
            dg_ref[...] = jnp.zeros_like(dg_ref)

        def hn_bwd(x, dy, row):
            x, dy = x.astype(F32), dy.astype(F32)
            wd = x.shape[1]
            ee = e[:wd, :wd]
            g = g_ref[row:row + 1, :wd]
            r = lax.rsqrt(_seg_sum(x * x, ee) * (1.0 / HEAD_DIM) + EPS)
            gdy = dy * g
            dx = r * gdy - x * (r * r * r * (_seg_sum(gdy * x, ee) * (1.0 / HEAD_DIM)))
            dg_ref[row:row + 1, :wd] += jnp.sum(dy * x * r, axis=0, keepdims=True)
            return dx

        def rope_bwd(dy, wd):
            dy = dy.astype(F32)
            return dy * c_ref[:, :wd] + _rope_partner(dy * s_ref[:, :wd])

        dqkv = jnp.concatenate([a_refs[m][...].astype(F32) for m in range(3)], axis=1)
        for ci, d in ((1, 4), (2, 16)):
            for r in range(d):
                part = jnp.concatenate([a_refs[3 * ci + m][r].astype(F32) for m in range(3)], axis=1)
                _write_residue(scr, 0, r, d, part)
            dqkv = dqkv + _gather_cols(scr, 0, 6)
        dp_ref[:, 0:256] = hn_bwd(p_ref[:, 0:256], dqkv[:, 0:256] * ATTN_SCALE, 0).astype(BF16)
        dp_ref[:, 256:512] = hn_bwd(p_ref[:, 256:512], dqkv[:, 256:512], 1).astype(BF16)
        dp_ref[:, 512:768] = dqkv[:, 512:768].astype(BF16)
        dp_ref[:, 768:1024] = hn_bwd(p_ref[:, 768:1024], dqb_ref[...] * ATTN_SCALE, 2).astype(BF16)
        dp_ref[:, 1024:1152] = hn_bwd(p_ref[:, 1024:1152], dkb_ref[...], 3).astype(BF16)
        dp_ref[:, 1152:1280] = dvb_ref[...].astype(BF16)
        dp_ref[:, 1280:1536] = dcu_ref[...].astype(BF16)
        dp_ref[:, 1536:1792] = dcv_ref[...].astype(BF16)
        dp_ref[:, 1792:2048] = hn_bwd(p_ref[:, 1792:2048], rope_bwd(dqd_ref[...] * ATTN_SCALE, 256), 4).astype(BF16)
        dp_ref[:, 2048:2176] = hn_bwd(p_ref[:, 2048:2176], rope_bwd(dkd_ref[...], 128), 5).astype(BF16)
        dp_ref[:, 2176:2304] = dvd_ref[...].astype(BF16)

        @pl.when(last)
        def _():
            dg_ref[...] = _seg_sum(dg_ref[...], _group_sum_matrix(256, False))

    row = lambda width: pl.BlockSpec((tm, width), lambda b, i: (b * nt + i, 0))
    tab = pl.BlockSpec((tm, 256), lambda b, i: (i, 0))
    in_specs = [row(IN_WIDTH), pl.BlockSpec((8, 256), lambda b, i: (0, 0)), tab, tab]
    res_specs = _residue_specs(tm, 256, nt)
    in_specs += [res_specs[ci] for ci in range(3) for _ in range(3)]
    in_specs += [row(a.shape[1]) for a in flat[9:]]
    return pl.pallas_call(
        body, name=name, grid=(bl, nt), in_specs=in_specs,
        out_specs=[row(IN_WIDTH), pl.BlockSpec((8, 256), lambda b, i: (0, 0))],
        out_shape=[_sds((t, IN_WIDTH), BF16), _sds((8, 256), F32)],
        scratch_shapes=[pltpu.VMEM((6, tm, LANES), F32)],
        compiler_params=_cparams(ARB, ARB),
    )(proj, gains, cos, sins, *flat)


BAND_ROWS_PER_STEP = 512


def _residues_per_step(dil, seq_len):
    return min(dil, max(1, BAND_ROWS_PER_STEP // seq_len))


def _band_spec(seq_len, spec, rb):
    width, idx = spec
    return pl.BlockSpec((None, rb, seq_len, width), lambda b, r: (b, r, 0, idx))


def _fill_padded(dst, src_ref, rad, seq_len):
    z = jnp.zeros((rad, dst.shape[1]), dst.dtype)
    dst[0:rad, :] = z
    dst[rad + seq_len:rad + seq_len + rad, :] = z
    dst[rad:rad + seq_len, :] = src_ref[...]


def _band_fwd(src, qs, ks, vs, bias, sink, *, rad, nh, nkv, name):
    bl, dil, sl, _ = src.shape
    blk = bias.shape[1]
    kw = blk + 2 * rad
    nb = sl // blk
    rep = nh // nkv
    has_sink = sink is not None
    rb = _residues_per_step(dil, sl)

    def body(*refs):
        q_all, k_all, v_all, b_ref = refs[:4]
        s_ref = refs[4] if has_sink else None
        o_all, l_all, kp, vp = refs[4 + has_sink:]
        for ri in range(rb):
            one_sequence(q_all.at[ri], k_all.at[ri], v_all.at[ri], b_ref, s_ref, o_all.at[ri], l_all.at[ri], kp, vp)

    def one_sequence(q_ref, k_ref, v_ref, b_ref, s_ref, o_ref, l_ref, kp, vp):
        _fill_padded(kp, k_ref, rad, sl)
        _fill_padded(vp, v_ref, rad, sl)

        def blk_body(i, carry):
            r0 = pl.multiple_of(i * blk, blk)
            qb = q_ref[pl.ds(r0, blk), :]
            kwin = kp[pl.ds(r0, kw), :]
            vwin = vp[pl.ds(r0, kw), :]
            col = r0 - rad + lax.broadcasted_iota(jnp.int32, (blk, kw), 1)
            neg = jnp.where((col >= 0) & (col < sl), 0.0, NEG_INF).astype(F32)
            for h in range(nh):
                g = h // rep
                hs = slice(h * HEAD_DIM, (h + 1) * HEAD_DIM)
                gs = slice(g * HEAD_DIM, (g + 1) * HEAD_DIM)
                s = lax.dot_general(qb[:, hs], kwin[:, gs], NT, preferred_element_type=F32)
                s = s + b_ref[h] + neg
                m = jnp.max(s, axis=1, keepdims=True)
                if has_sink:
                    sk = s_ref[h][0:1, 0:1]
                    m = jnp.maximum(m, sk)
                p = jnp.exp(s - m)
                den = jnp.sum(p, axis=1, keepdims=True)
                if has_sink:
                    den = den + jnp.exp(sk - m)
                o = jnp.dot(p.astype(BF16), vwin[:, gs], preferred_element_type=F32) / den
                o_ref[pl.ds(r0, blk), hs] = o.astype(BF16)
                l_ref[pl.ds(r0, blk), hs] = jnp.broadcast_to(m + jnp.log(den), (blk, HEAD_DIM))
            return carry

        lax.fori_loop(0, nb, blk_body, 0)

    in_specs = [_band_spec(sl, qs, rb), _band_spec(sl, ks, rb), _band_spec(sl, vs, rb),
                pl.BlockSpec((nh, blk, kw), lambda b, r: (0, 0, 0))]
    args = [src] * 3 + [bias]
    if has_sink:
        in_specs.append(pl.BlockSpec((nh, 8, 128), lambda b, r: (0, 0, 0)))
        args.append(sink)
    return pl.pallas_call(
        body, name=name, grid=(bl, dil // rb), in_specs=in_specs,
        out_specs=[_band_spec(sl, (256, 0), rb)] * 2,
        out_shape=[_sds((bl, dil, sl, 256), BF16), _sds((bl, dil, sl, 256), F32)],
        scratch_shapes=[pltpu.VMEM((sl + 2 * rad, ks[0]), BF16), pltpu.VMEM((sl + 2 * rad, vs[0]), BF16)],
        compiler_params=_cparams(PAR, PAR),
    )(*args)


def _band_bwd(src, qs, ks, vs, bias, sink, dy, dcol, lse, delta, *, rad, nh, nkv, name):
    bl, dil, sl, _ = src.shape
    blk = bias.shape[1]
    kw = blk + 2 * rad
    nb = sl // blk
    rep = nh // nkv
    has_sink = sink is not None
    rb = _residues_per_step(dil, sl)
    wk, wv = ks[0], vs[0]

    def body(*refs):
        q_all, k_all, v_all, b_ref = refs[:4]
        s_ref = refs[4] if has_sink else None
        do_all, l_all, dl_all = refs[4 + has_sink:7 + has_sink]
        outs = refs[7 + has_sink:]
        dsk_ref = None
        if has_sink:
            dq_all, dk_all, dv_all, db_ref, dsk_ref, kp, vp, dka, dva = outs
        else:
            dq_all, dk_all, dv_all, db_ref, kp, vp, dka, dva = outs

        @pl.when((pl.program_id(0) == 0) & (pl.program_id(1) == 0))
        def _():
            db_ref[...] = jnp.zeros_like(db_ref)
            if has_sink:
                dsk_ref[...] = jnp.zeros_like(dsk_ref)

        for ri in range(rb):
            one_sequence(q_all.at[ri], k_all.at[ri], v_all.at[ri], b_ref, s_ref, do_all.at[ri], l_all.at[ri],
                         dl_all.at[ri], dq_all.at[ri], dk_all.at[ri], dv_all.at[ri], db_ref, dsk_ref, kp, vp, dka, dva)

    def one_sequence(q_ref, k_ref, v_ref, b_ref, s_ref, do_ref, l_ref, dl_ref, dq_ref, dk_ref, dv_ref, db_ref, dsk_ref,
                     kp, vp, dka, dva):
        _fill_padded(kp, k_ref, rad, sl)
        _fill_padded(vp, v_ref, rad, sl)
        dka[...] = jnp.zeros_like(dka)
        dva[...] = jnp.zeros_like(dva)

        def blk_body(i, carry):
            r0 = pl.multiple_of(i * blk, blk)
            qb = q_ref[pl.ds(r0, blk), :]
            kwin = kp[pl.ds(r0, kw), :]
            vwin = vp[pl.ds(r0, kw), :]
            dob = do_ref[pl.ds(r0, blk), :].astype(BF16)
            lb = l_ref[pl.ds(r0, blk), :]
            dlb = dl_ref[pl.ds(r0, blk), :]
            col = r0 - rad + lax.broadcasted_iota(jnp.int32, (blk, kw), 1)
            neg = jnp.where((col >= 0) & (col < sl), 0.0, NEG_INF).astype(F32)
            for h in range(nh):
                g = h // rep
                hs = slice(h * HEAD_DIM, (h + 1) * HEAD_DIM)
                gs = slice(g * HEAD_DIM, (g + 1) * HEAD_DIM)
                qh, kh, vh, doh = qb[:, hs], kwin[:, gs], vwin[:, gs], dob[:, hs]
                lh = lb[:, h * HEAD_DIM:h * HEAD_DIM + 1]
                dlh = dlb[:, h * HEAD_DIM:h * HEAD_DIM + 1]
                s = lax.dot_general(qh, kh, NT, preferred_element_type=F32) + b_ref[h] + neg
                p = jnp.exp(s - lh)
                dp = lax.dot_general(doh, vh, NT, preferred_element_type=F32)
                ds = p * (dp - dlh)
                dsb = ds.astype(BF16)
                dq_ref[pl.ds(r0, blk), hs] = jnp.dot(dsb, kh, preferred_element_type=F32).astype(BF16)
                dka[pl.ds(r0, kw), gs] += lax.dot_general(dsb, qh, TN, preferred_element_type=F32)
                dva[pl.ds(r0, kw), gs] += lax.dot_general(p.astype(BF16), doh, TN, preferred_element_type=F32)
                db_ref[h] += ds
                if has_sink:
                    ps = jnp.exp(s_ref[h][0:1, 0:1] - lh)
                    dsk_ref[h] += jnp.broadcast_to(-jnp.sum(ps * dlh, axis=0, keepdims=True), (8, 128))
            return carry

        lax.fori_loop(0, nb, blk_body, 0)
        dk_ref[...] = dka[rad:rad + sl, :].astype(BF16)
        dv_ref[...] = dva[rad:rad + sl, :].astype(BF16)

    const3 = lambda b, r: (0, 0, 0)
    in_specs = [_band_spec(sl, qs, rb), _band_spec(sl, ks, rb), _band_spec(sl, vs, rb),
                pl.BlockSpec((nh, blk, kw), const3)]
    args = [src] * 3 + [bias]
    if has_sink:
        in_specs.append(pl.BlockSpec((nh, 8, 128), const3))
        args.append(sink)
    row = _band_spec(sl, (256, 0), rb)
    in_specs += [_band_spec(sl, (256, dcol), rb), row, row]
    args += [dy, lse, delta]
    out_specs = [row, _band_spec(sl, (wk, 0), rb), _band_spec(sl, (wv, 0), rb), pl.BlockSpec((nh, blk, kw), const3)]
    out_shape = [_sds((bl, dil, sl, 256), BF16), _sds((bl, dil, sl, wk), BF16), _sds((bl, dil, sl, wv), BF16),
                 _sds((nh, blk, kw), F32)]
    if has_sink:
        out_specs.append(pl.BlockSpec((nh, 8, 128), const3))
        out_shape.append(_sds((nh, 8, 128), F32))
    return pl.pallas_call(
        body, name=name, grid=(bl, dil // rb), in_specs=in_specs, out_specs=out_specs, out_shape=out_shape,
        scratch_shapes=[pltpu.VMEM((sl + 2 * rad, wk), BF16), pltpu.VMEM((sl + 2 * rad, wv), BF16),
                        pltpu.VMEM((sl + 2 * rad, wk), F32), pltpu.VMEM((sl + 2 * rad, wv), F32)],
        compiler_params=_cparams(ARB, ARB),
    )(*args)


def _combine_a(os_, ls_, *, tm, name):
    bl = os_[1].shape[0]
    t = bl * SEQ
    nt = SEQ // tm

    def body(o1, o4, o16, l1, l4, l16, y_ref, lt_ref, scr):
        for k, (d, ref) in enumerate(((4, o4), (16, o16), (4, l4), (16, l16))):
            for r in range(d):
                _write_residue(scr, 2 * k, r, d, ref[r].astype(F32))
        o2, o3, b, c = (_gather_cols(scr, 2 * k, 2) for k in range(4))
        a = l1[...]
        m = jnp.maximum(jnp.maximum(a, b), c)
        ea, eb, ec = jnp.exp(a - m), jnp.exp(b - m), jnp.exp(c - m)
        den = ea + eb + ec
        y_ref[...] = ((ea / den) * o1[...].astype(F32) + (eb / den) * o2 + (ec / den) * o3).astype(BF16)
        lt_ref[...] = m + jnp.log(den)

    specs = _residue_specs(tm, 256, nt)
    return pl.pallas_call(
        body, name=name, grid=(bl, nt), in_specs=specs * 2, out_specs=[specs[0]] * 2,
        out_shape=[_sds((t, 256), BF16), _sds((t, 256), F32)], scratch_shapes=[pltpu.VMEM((8, tm, LANES), F32)],
        compiler_params=_cparams(PAR, PAR),
    )(*os_, *ls_)


def _deltas(dycat, ya, yb, yd, lse_a, *, tm, name):
    t = ya.shape[0]
    bl = t // SEQ
    nt = SEQ // tm

    def body(dy_ref, ya_ref, yb_ref, yd_ref, la_ref, dy4, dy16, l4, l16, da1, da4, da16, db_ref, dd_ref, scr):
        e = _group_sum_matrix(256, True)
        dya = dy_ref[:, 0:256]
        dla = _seg_sum(dya * ya_ref[...].astype(F32), e)
        da1[...] = dla
        db_ref[...] = _seg_sum(dy_ref[:, 256:512] * yb_ref[...].astype(F32), e)
        dd_ref[...] = _seg_sum(dy_ref[:, 768:1024] * yd_ref[...].astype(F32), e)
        for k, (val, r4, r16) in enumerate(((dya, dy4, dy16), (la_ref[...], l4, l16), (dla, da4, da16))):
            _scatter_cols(scr, 2 * k, val)
            for d, ref in ((4, r4), (16, r16)):
                for r in range(d):
                    ref[r] = _read_residue(scr, 2 * k, 2, r, d)

    specs = _residue_specs(tm, 256, nt)
    nat = specs[0]
    shapes = _residue_shapes(bl, 256, F32)
    outs = pl.pallas_call(
        body, name=name, grid=(bl, nt),
        in_specs=[pl.BlockSpec((tm, 1024), lambda b, i: (b * nt + i, 0)), nat, nat, nat, nat],
        out_specs=specs[1:] + specs[1:] + specs + [nat, nat],
        out_shape=shapes[1:] + shapes[1:] + shapes + [shapes[0], shapes[0]],
        scratch_shapes=[pltpu.VMEM((6, tm, LANES), F32)],
        compiler_params=_cparams(PAR, PAR),
    )(dycat, ya, yb, yd, lse_a)
    return outs[0:2], outs[2:4], outs[4:7], outs[7], outs[8]


def _dense_fwd(qd, *, tq, name):
    t = qd.shape[0]
    bl = t // SEQ
    nq = SEQ // tq

    def body(q_ref, k_ref, v_ref, o_ref, l_ref):
        q = q_ref[...]
        for g in range(2):
            h0, h1 = 2 * g, 2 * g + 1
            q2 = jnp.concatenate([q[:, h0 * 64:(h0 + 1) * 64], q[:, h1 * 64:(h1 + 1) * 64]], axis=0)
            kg = k_ref[:, g * 64:(g + 1) * 64]
            vg = v_ref[:, g * 64:(g + 1) * 64]
            s = lax.dot_general(q2, kg, NT, preferred_element_type=F32)
            m = jnp.max(s, axis=1, keepdims=True)
            p = jnp.exp(s - m)
            den = jnp.sum(p, axis=1, keepdims=True)
            o2 = jnp.dot(p.astype(BF16), vg, preferred_element_type=F32) / den
            l2 = jnp.broadcast_to(m + jnp.log(den), (2 * tq, 64))
            o_ref[:, h0 * 64:(h0 + 1) * 64] = o2[:tq].astype(BF16)
            o_ref[:, h1 * 64:(h1 + 1) * 64] = o2[tq:].astype(BF16)
            l_ref[:, h0 * 64:(h0 + 1) * 64] = l2[:tq]
            l_ref[:, h1 * 64:(h1 + 1) * 64] = l2[tq:]

    q3 = qd.reshape(bl, SEQ, 512)
    o, lse = pl.pallas_call(
        body, name=name, grid=(bl, nq),
        in_specs=[pl.BlockSpec((None, tq, 256), lambda b, i: (b, i, 0)),
                  pl.BlockSpec((None, SEQ, 128), lambda b, i: (b, 0, 2)),
                  pl.BlockSpec((None, SEQ, 128), lambda b, i: (b, 0, 3))],
        out_specs=[pl.BlockSpec((None, tq, 256), lambda b, i: (b, i, 0))] * 2,
        out_shape=[_sds((bl, SEQ, 256), BF16), _sds((bl, SEQ, 256), F32)],
        compiler_params=_cparams(PAR, PAR),
    )(q3, q3, q3)
    return o.reshape(t, 256), lse.reshape(t, 256)


def _dense_bwd(qd, dycat, lse, delta, *, tq, name):
    t = qd.shape[0]
    bl = t // SEQ
    nq = SEQ // tq

    def body(q_ref, k_ref, v_ref, do_ref, l_ref, dl_ref, dq_ref, dk_ref, dv_ref, dkt, dvt):
        @pl.when(pl.program_id(1) == 0)
        def _():
            dkt[...] = jnp.zeros_like(dkt)
            dvt[...] = jnp.zeros_like(dvt)

        q = q_ref[...]
        do = do_ref[...].astype(BF16)
        lv = l_ref[...]
        dlv = dl_ref[...]
        for g in range(2):
            h0, h1 = 2 * g, 2 * g + 1
            q2 = jnp.concatenate([q[:, h0 * 64:(h0 + 1) * 64], q[:, h1 * 64:(h1 + 1) * 64]], axis=0)
            do2 = jnp.concatenate([do[:, h0 * 64:(h0 + 1) * 64], do[:, h1 * 64:(h1 + 1) * 64]], axis=0)
            l2 = jnp.concatenate([lv[:, h0 * 64:h0 * 64 + 1], lv[:, h1 * 64:h1 * 64 + 1]], axis=0)
            dl2 = jnp.concatenate([dlv[:, h0 * 64:h0 * 64 + 1], dlv[:, h1 * 64:h1 * 64 + 1]], axis=0)
            kg = k_ref[:, g * 64:(g + 1) * 64]
            vg = v_ref[:, g * 64:(g + 1) * 64]
            s = lax.dot_general(q2, kg, NT, preferred_element_type=F32)
            p = jnp.exp(s - l2)
            dp = lax.dot_general(do2, vg, NT, preferred_element_type=F32)
            ds = (p * (dp - dl2)).astype(BF16)
            dq2 = jnp.dot(ds, kg, preferred_element_type=F32)
            dq_ref[:, h0 * 64:(h0 + 1) * 64] = dq2[:tq].astype(BF16)
            dq_ref[:, h1 * 64:(h1 + 1) * 64] = dq2[tq:].astype(BF16)
            dkt[g * 64:(g + 1) * 64, :] += lax.dot_general(q2, ds, TN, preferred_element_type=F32)
            dvt[g * 64:(g + 1) * 64, :] += lax.dot_general(do2, p.astype(BF16), TN, preferred_element_type=F32)

        @pl.when(pl.program_id(1) == nq - 1)
        def _():
            dk_ref[...] = dkt[...].T.astype(BF16)
            dv_ref[...] = dvt[...].T.astype(BF16)

    q3 = qd.reshape(bl, SEQ, 512)
    tile = pl.BlockSpec((None, tq, 256), lambda b, i: (b, i, 0))
    full = pl.BlockSpec((None, SEQ, 128), lambda b, i: (b, 0, 0))
    dq, dk, dv = pl.pallas_call(
        body, name=name, grid=(bl, nq),
        in_specs=[tile, pl.BlockSpec((None, SEQ, 128), lambda b, i: (b, 0, 2)),
                  pl.BlockSpec((None, SEQ, 128), lambda b, i: (b, 0, 3)),
                  pl.BlockSpec((None, tq, 256), lambda b, i: (b, i, 3)), tile, tile],
        out_specs=[tile, full, full],
        out_shape=[_sds((bl, SEQ, 256), BF16), _sds((bl, SEQ, 128), BF16), _sds((bl, SEQ, 128), BF16)],
        scratch_shapes=[pltpu.VMEM((128, SEQ), F32), pltpu.VMEM((128, SEQ), F32)],
        compiler_params=_cparams(PAR, ARB),
    )(q3, q3, q3, dycat.reshape(bl, SEQ, 1024), lse.reshape(bl, SEQ, 256), delta.reshape(bl, SEQ, 256))
    return dq.reshape(t, 256), dk.reshape(t, 128), dv.reshape(t, 128)


def _c_norm(cv, gam, bet):
    vg = _gelu(cv)
    mu = jnp.mean(vg, axis=-1, keepdims=True)
    xc = vg - mu
    r = lax.rsqrt(jnp.mean(xc * xc, axis=-1, keepdims=True) + EPS)
    xhat = xc * r
    return xhat * gam + bet, xhat, r


def _c_fwd(proj, gam, bet, ws, bst, *, tm, name):
    t = proj.shape[0]
    nch = tm // C_CHUNK

    def body(u_ref, v_ref, g_ref, b_ref, ws_ref, bs_ref, y_ref):
        vn, _, _ = _c_norm(v_ref[...].astype(F32), g_ref[...], b_ref[...])
        vnb = vn.astype(BF16)
        for c in range(nch):
            rows = slice(c * C_CHUNK, (c + 1) * C_CHUNK)
            for g in range(C_GROUPS):
                gs = slice(g * 64, (g + 1) * 64)
                mixed = jnp.dot(ws_ref[g], vnb[rows, gs], preferred_element_type=F32) + bs_ref[:, gs]
                y_ref[rows, gs] = (_gelu(u_ref[rows, gs].astype(F32)) * mixed).astype(BF16)

    vec = pl.BlockSpec((1, 256), lambda i: (0, 0))
    return pl.pallas_call(
        body, name=name, grid=(t // tm,),
        in_specs=[pl.BlockSpec((tm, 256), lambda i: (i, 5)), pl.BlockSpec((tm, 256), lambda i: (i, 6)), vec, vec,
                  pl.BlockSpec((C_GROUPS, C_CHUNK, C_CHUNK), lambda i: (0, 0, 0)),
                  pl.BlockSpec((C_CHUNK, 256), lambda i: (0, 0))],
        out_specs=pl.BlockSpec((tm, 256), lambda i: (i, 0)), out_shape=_sds((t, 256), BF16),
        compiler_params=_cparams(PAR),
    )(proj, proj, gam, bet, ws, bst)


def _c_bwd(proj, dycat, gam, bet, ws, wst, bst, *, tm, name):
    t = proj.shape[0]
    nch = tm // C_CHUNK
    nstep = t // tm

    def body(u_ref, v_ref, dy_ref, g_ref, b_ref, ws_ref, wst_ref, bs_ref,
             du_ref, dv_ref, dws_ref, dbs_ref, dg_ref, db_ref, dvn_s):
        step = pl.program_id(0)

        @pl.when(step == 0)
        def _():
            dws_ref[...] = jnp.zeros_like(dws_ref)
            dbs_ref[...] = jnp.zeros_like(dbs_ref)
            dg_ref[...] = jnp.zeros_like(dg_ref)
            db_ref[...] = jnp.zeros_like(db_ref)

        cv = v_ref[...].astype(F32)
        gam_v = g_ref[...]
        vn, xhat, r = _c_norm(cv, gam_v, b_ref[...])
        vnb = vn.astype(BF16)
        for c in range(nch):
            rows = slice(c * C_CHUNK, (c + 1) * C_CHUNK)
            for g in range(C_GROUPS):
                gs = slice(g * 64, (g + 1) * 64)
                cu = u_ref[rows, gs].astype(F32)
                dy = dy_ref[rows, gs]
                mixed = jnp.dot(ws_ref[g], vnb[rows, gs], preferred_element_type=F32) + bs_ref[:, gs]
                du_ref[rows, gs] = (dy * mixed * _gelu_grad(cu)).astype(BF16)
                dmix = dy * _gelu(cu)
                dbs_ref[:, gs] += dmix
                dmb = dmix.astype(BF16)
                dws_ref[g] += lax.dot_general(dmb, vnb[rows, gs], NT, preferred_element_type=F32)
                dvn_s[rows, gs] = jnp.dot(wst_ref[g], dmb, preferred_element_type=F32)
        dvn = dvn_s[...]
        dg_ref[...] += jnp.sum(dvn * xhat, axis=0, keepdims=True)
        db_ref[...] += jnp.sum(dvn, axis=0, keepdims=True)
        dxh = dvn * gam_v
        dvg = r * (dxh - jnp.mean(dxh, axis=-1, keepdims=True) - xhat * jnp.mean(dxh * xhat, axis=-1, keepdims=True))
        dv_ref[...] = (dvg * _gelu_grad(cv)).astype(BF16)

        @pl.when(step == nstep - 1)
        def _():
            dbs_ref[...] = _seg_sum(dbs_ref[...], _group_sum_matrix(256, True))

    vec = pl.BlockSpec((1, 256), lambda i: (0, 0))
    mat = pl.BlockSpec((C_GROUPS, C_CHUNK, C_CHUNK), lambda i: (0, 0, 0))
    bsp = pl.BlockSpec((C_CHUNK, 256), lambda i: (0, 0))
    tile = pl.BlockSpec((tm, 256), lambda i: (i, 0))
    return pl.pallas_call(
        body, name=name, grid=(nstep,),
        in_specs=[pl.BlockSpec((tm, 256), lambda i: (i, 5)), pl.BlockSpec((tm, 256), lambda i: (i, 6)),
                  pl.BlockSpec((tm, 256), lambda i: (i, 2)), vec, vec, mat, mat, bsp],
        out_specs=[tile, tile, mat, bsp, vec, vec],
        out_shape=[_sds((t, 256), BF16), _sds((t, 256), BF16), _sds((C_GROUPS, C_CHUNK, C_CHUNK), F32),
                   _sds((C_CHUNK, 256), F32), _sds((1, 256), F32), _sds((1, 256), F32)],
        scratch_shapes=[pltpu.VMEM((tm, 256), F32)],
        compiler_params=_cparams(ARB),
    )(proj, proj, dycat, gam, bet, ws, wst, bst)


FF_TC = 128
FF_NB = D_FF // FF_TC
FF_CH = 64
FF_HALO = 16


def _taps(ref, r0, win, where):
    z = jnp.zeros((FF_HALO, win.shape[1]), F32)
    if where == "first":
        win[0:FF_HALO, :] = z
        win[FF_HALO:, :] = ref[0:FF_CH + FF_HALO, :].astype(F32)
    elif where == "last":
        win[0:FF_CH + FF_HALO, :] = ref[SEQ - FF_CH - FF_HALO:SEQ, :].astype(F32)
        win[FF_CH + FF_HALO:, :] = z
    else:
        win[...] = ref[pl.ds(pl.multiple_of(r0 - FF_HALO, FF_HALO), FF_CH + 2 * FF_HALO), :].astype(F32)
    return tuple(win[FF_HALO + o:FF_HALO + o + FF_CH, :] for o in (-1, 0, 1))


def _chunk_loop(step):
    step(0, lambda ref, win: _taps(ref, 0, win, "first"))

    def mid(i, carry):
        r0 = pl.multiple_of(i * FF_CH, FF_CH)
        step(r0, lambda ref, win: _taps(ref, r0, win, "mid"))
        return carry

    lax.fori_loop(1, SEQ // FF_CH - 1, mid, 0)
    step(SEQ - FF_CH, lambda ref, win: _taps(ref, SEQ - FF_CH, win, "last"))


def _conv3(taps, w_ref, b_ref):
    dn, md, up = taps
    return w_ref[0:1, :] * dn + w_ref[1:2, :] * md + w_ref[2:3, :] * up + b_ref[...]


def _ff_specs(order):
    def at(fn):
        return (lambda b, j: fn(b, j)) if order == "bj" else (lambda j, b: fn(b, j))
    hs = [pl.BlockSpec((None, SEQ, FF_TC), at(lambda b, j, o=o: (b, 0, j + o))) for o in (0, FF_NB)]
    ws = [pl.BlockSpec((3, FF_TC), at(lambda b, j, o=o: (0, j + o))) for o in (0, FF_NB)]
    bs = [pl.BlockSpec((1, FF_TC), at(lambda b, j, o=o: (0, j + o))) for o in (0, FF_NB)]
    return hs, ws, bs


def _conv_gate_fwd(h, cw, cb, *, name):
    t = h.shape[0]
    bl = t // SEQ

    def body(hg_ref, hu_ref, wg_ref, wu_ref, bg_ref, bu_ref, a_ref, cg_ref, cu_ref, win):
        def step(r0, taps):
            cg = _conv3(taps(hg_ref, win.at[0]), wg_ref, bg_ref)
            cu = _conv3(taps(hu_ref, win.at[1]), wu_ref, bu_ref)
            a_ref[pl.ds(r0, FF_CH), :] = (cg * _sigmoid(cg) * cu).astype(BF16)
            cg_ref[pl.ds(r0, FF_CH), :] = cg.astype(BF16)
            cu_ref[pl.ds(r0, FF_CH), :] = cu.astype(BF16)

        _chunk_loop(step)

    hs, ws, bs = _ff_specs("bj")
    h3 = h.reshape(bl, SEQ, 2 * D_FF)
    half = pl.BlockSpec((None, SEQ, FF_TC), lambda b, j: (b, 0, j))
    outs = pl.pallas_call(
        body, name=name, grid=(bl, FF_NB), in_specs=hs + ws + bs, out_specs=[half] * 3,
        out_shape=[_sds((bl, SEQ, D_FF), BF16)] * 3,
        scratch_shapes=[pltpu.VMEM((2, FF_CH + 2 * FF_HALO, FF_TC), F32)],
        compiler_params=_cparams(PAR, PAR),
    )(h3, h3, cw, cw, cb, cb)
    return [o.reshape(t, D_FF) for o in outs]


def _conv_gate_bwd(h, cg_all, cu_all, dact, cw, cb, *, name):
    t = h.shape[0]
    bl = t // SEQ

    def body(hg_ref, hu_ref, wg_ref, wu_ref, bg_ref, bu_ref, da_ref, cg_ref, cu_ref,
             dhg_ref, dhu_ref, dwg_ref, dwu_ref, dbg_ref, dbu_ref, dg_s, du_s, win, sums):
        @pl.when(pl.program_id(1) == 0)
        def _():
            for ref in (dwg_ref, dwu_ref, dbg_ref, dbu_ref):
                ref[...] = jnp.zeros_like(ref)

        sums[...] = jnp.zeros_like(sums)
        red = lambda x: jnp.sum(x.reshape(FF_CH // 8, 8, x.shape[1]), axis=0)

        def pass1(r0, taps):
            tg, tu = taps(hg_ref, win.at[0]), taps(hu_ref, win.at[1])
            cg = cg_ref[pl.ds(r0, FF_CH), :].astype(F32)
            cu = cu_ref[pl.ds(r0, FF_CH), :].astype(F32)
            da = da_ref[pl.ds(r0, FF_CH), :].astype(F32)
            sg = _sigmoid(cg)
            dcg = da * cu * (sg * (1.0 + cg * (1.0 - sg)))
            dcu = da * (cg * sg)
            dg_s[pl.ds(r0, FF_CH), :] = dcg
            du_s[pl.ds(r0, FF_CH), :] = dcu
            for half, (d, tp) in enumerate(((dcg, tg), (dcu, tu))):
                for k in range(3):
                    sums[4 * half + k] += red(d * tp[k])
                sums[4 * half + 3] += red(d)

        _chunk_loop(pass1)
        for half, (dw_ref, db_ref) in enumerate(((dwg_ref, dbg_ref), (dwu_ref, dbu_ref))):
            for k in range(3):
                dw_ref[k:k + 1, :] += jnp.sum(sums[4 * half + k], axis=0, keepdims=True)
            db_ref[...] += jnp.sum(sums[4 * half + 3], axis=0, keepdims=True)

        def pass2(r0, taps):
            for k, (s, w_ref, o_ref) in enumerate(((dg_s, wg_ref, dhg_ref), (du_s, wu_ref, dhu_ref))):
                dn, md, up = taps(s, win.at[k])
                o_ref[pl.ds(r0, FF_CH), :] = (w_ref[0:1, :] * up + w_ref[1:2, :] * md + w_ref[2:3, :] * dn).astype(BF16)

        _chunk_loop(pass2)

    hs, ws, bs = _ff_specs("jb")
    half = pl.BlockSpec((None, SEQ, FF_TC), lambda j, b: (b, 0, j))
    wsp = pl.BlockSpec((3, FF_TC), lambda j, b: (0, j))
    bsp = pl.BlockSpec((1, FF_TC), lambda j, b: (0, j))
    h3 = h.reshape(bl, SEQ, 2 * D_FF)
    dhg, dhu, dwg, dwu, dbg, dbu = pl.pallas_call(
        body, name=name, grid=(FF_NB, bl), in_specs=hs + ws + bs + [half] * 3,
        out_specs=[half, half, wsp, wsp, bsp, bsp],
        out_shape=[_sds((bl, SEQ, D_FF), BF16), _sds((bl, SEQ, D_FF), BF16), _sds((3, D_FF), F32), _sds((3, D_FF), F32),
                   _sds((1, D_FF), F32), _sds((1, D_FF), F32)],
        scratch_shapes=[pltpu.VMEM((SEQ, FF_TC), F32), pltpu.VMEM((SEQ, FF_TC), F32),
                        pltpu.VMEM((2, FF_CH + 2 * FF_HALO, FF_TC), F32), pltpu.VMEM((8, 8, FF_TC), F32)],
        compiler_params=_cparams(PAR, ARB),
    )(h3, h3, cw, cw, cb, cb, *[a.reshape(bl, SEQ, D_FF) for a in (dact, cg_all, cu_all)])
    return (dhg.reshape(t, D_FF), dhu.reshape(t, D_FF), jnp.concatenate([dwg, dwu], axis=1),
            jnp.concatenate([dbg, dbu], axis=1))


def _ple_fwd(x2, gain, wg, pe, pe_blk, wp, *, tm, name):
    t, k = x2.shape

    def body(x_ref, g_ref, wg_ref, pe_ref, wp_ref, hn_ref, x3_ref, gt_ref, pp_ref):
        x = x_ref[...]
        r = lax.rsqrt(jnp.mean(x * x, axis=-1, keepdims=True) + EPS)
        hn = (x * r * g_ref[...]).astype(BF16)
        hn_ref[...] = hn
        gate = _sigmoid(jnp.dot(hn, wg_ref[...], preferred_element_type=F32))
        pp = jnp.dot(pe_ref[...].astype(BF16), wp_ref[...], preferred_element_type=F32)
        gt_ref[...] = gate.astype(BF16)
        pp_ref[...] = pp.astype(BF16)
        x3_ref[...] = x + pp * gate

    row = pl.BlockSpec((tm, k), lambda i: (i, 0))
    return pl.pallas_call(
        body, name=name, grid=(t // tm,),
        in_specs=[row, pl.BlockSpec((1, k), lambda i: (0, 0)), pl.BlockSpec((k, k), lambda i: (0, 0)),
                  pl.BlockSpec((tm, PLE_DIM), lambda i: (pe_blk + i, 0)), pl.BlockSpec((PLE_DIM, k), lambda i: (0, 0))],
        out_specs=[row, row, row, row],
        out_shape=[_sds((t, k), BF16), _sds((t, k), F32), _sds((t, k), BF16), _sds((t, k), BF16)],
        compiler_params=_cparams(PAR),
    )(x2, gain, wg, pe, wp)


def _ple_bwd_ew(dx3, gate, pp, *, tm, name):
    t, n = dx3.shape

    def body(d_ref, g_ref, p_ref, dz_ref, dpp_ref):
        d, g = d_ref[...], g_ref[...]
        dz_ref[...] = (d * p_ref[...] * g * (1.0 - g)).astype(BF16)
        dpp_ref[...] = (d * g).astype(BF16)

    spec = pl.BlockSpec((tm, n), lambda i: (i, 0))
    return pl.pallas_call(
        body, name=name, grid=(t // tm,), in_specs=[spec] * 3, out_specs=[spec] * 2,
        out_shape=[_sds((t, n), BF16)] * 2, compiler_params=_cparams(PAR),
    )(dx3, gate, pp)


def _loss_head(y, tgt, gate, pp, *, tm, name):
    t, d = y.shape

    def body(y_ref, t_ref, g_ref, p_ref, l_ref, dy_ref, dz_ref, dpp_ref):
        @pl.when(pl.program_id(0) == 0)
        def _():
            l_ref[...] = jnp.zeros_like(l_ref)

        e = y_ref[...] - t_ref[...]
        dy = e * (1.0 / d)
        dy_ref[...] = dy
        g = g_ref[...].astype(F32)
        dz_ref[...] = (dy * p_ref[...] * g * (1.0 - g)).astype(BF16)
        dpp_ref[...] = (dy * g).astype(BF16)
        s = jnp.sum(jnp.sum(e * e, axis=1, keepdims=True), axis=0, keepdims=True)
        l_ref[...] += jnp.broadcast_to(s * (0.5 / d), (8, 128))

    spec = pl.BlockSpec((tm, d), lambda i: (i, 0))
    return pl.pallas_call(
        body, name=name, grid=(t // tm,), in_specs=[spec] * 4,
        out_specs=[pl.BlockSpec((8, 128), lambda i: (0, 0)), spec, spec, spec],
        out_shape=[_sds((8, 128), F32), _sds((t, d), F32), _sds((t, d), BF16), _sds((t, d), BF16)],
        compiler_params=_cparams(ARB),
    )(y, tgt, gate, pp)


BIAS_PC = 8192


def _onehot(bucket_row):
    rows = lax.broadcasted_iota(jnp.int32, (REL_BUCKETS, bucket_row.shape[1]), 0)
    return (rows == bucket_row).astype(BF16)


def _dot3(x, onehot, dims):
    acc = None
    for _ in range(3):
        term = x.astype(BF16)
        part = lax.dot_general(term, onehot, dims, preferred_element_type=F32)
        acc = part if acc is None else acc + part
        x = x - term.astype(F32)
    return acc


def _bias_lookup(table_t, bucket, *, name):
    h = table_t.shape[0]
    p = bucket.shape[1]

    def body(t_ref, b_ref, o_ref):
        bk = b_ref[...]
        val = _dot3(t_ref[...], _onehot(bk), (((1,), (0,)), ((), ())))
        o_ref[...] = jnp.where(bk >= 0, val, NEG_INF)

    return pl.pallas_call(
        body, name=name, grid=(p // BIAS_PC,),
        in_specs=[pl.BlockSpec((h, REL_BUCKETS), lambda i: (0, 0)), pl.BlockSpec((1, BIAS_PC), lambda i: (0, i))],
        out_specs=pl.BlockSpec((h, BIAS_PC), lambda i: (0, i)), out_shape=_sds((h, p), F32),
        compiler_params=_cparams(PAR),
    )(table_t, bucket)


def _bucket_reduce(dbiases, bucket, *, name):
    h, p = dbiases[0].shape
    nl = len(dbiases)

    def body(*refs):
        b_ref, o_ref = refs[nl], refs[nl + 1]

        @pl.when(pl.program_id(0) == 0)
        def _():
            o_ref[...] = jnp.zeros_like(o_ref)

        d = refs[0][...]
        for d_ref in refs[1:nl]:
            d = d + d_ref[...]
        o_ref[...] += _dot3(d, _onehot(b_ref[...]), NT)

    return pl.pallas_call(
        body, name=name, grid=(p // BIAS_PC,),
        in_specs=[pl.BlockSpec((h, BIAS_PC), lambda i: (0, i))] * nl + [pl.BlockSpec((1, BIAS_PC), lambda i: (0, i))],
        out_specs=pl.BlockSpec((h, REL_BUCKETS), lambda i: (0, 0)), out_shape=_sds((h, REL_BUCKETS), F32),
        compiler_params=_cparams(ARB),
    )(*dbiases, bucket)


def _adamw_math(w, g, m, v):
    m = ADAM_B1 * m + (1.0 - ADAM_B1) * g
    v = ADAM_B2 * v + (1.0 - ADAM_B2) * (g * g)
    m_hat = m / (1.0 - ADAM_B1 ** ADAM_STEP)
    v_hat = v / (1.0 - ADAM_B2 ** ADAM_STEP)
    delta = -ADAM_LR * (m_hat / (jnp.sqrt(v_hat) + ADAM_EPS) + ADAM_WD * w)
    return delta, m, v


def _adamw_reduce(parts, w, m, v, *, tr, name):
    nl = len(parts)
    rows, c = w.shape
    r = rows // nl
    nt = r // tr

    def body(*refs):
        p_refs = refs[:nl]
        w_ref, m_ref, v_ref, g_ref, d_ref, nm_ref, nv_ref = refs[nl:]
        for li, p_ref in enumerate(p_refs):
            @pl.when(pl.program_id(0) == li)
            def _(p_ref=p_ref):
                g = p_ref[0].astype(F32)
                for k in range(1, N_DEV):
                    g = g + p_ref[k].astype(F32)
                d, nm, nv = _adamw_math(w_ref[...], g, m_ref[...], v_ref[...])
                g_ref[...] = g
                d_ref[...] = d
                nm_ref[...] = nm
                nv_ref[...] = nv

    def part_map(li):
        return lambda l, i: (0, jnp.where(l == li, i, jnp.where(l < li, 0, nt - 1)), 0)

    spec = pl.BlockSpec((tr, c), lambda l, i: (l * nt + i, 0))
    return pl.pallas_call(
        body, name=name, grid=(nl, nt),
        in_specs=[pl.BlockSpec((N_DEV, tr, c), part_map(li)) for li in range(nl)] + [spec, spec, spec],
        out_specs=[spec] * 4, out_shape=[_sds((rows, c), F32)] * 4, compiler_params=_cparams(ARB, ARB),
    )(*parts, w, m, v)


def _adamw_plain(g, w, m, v, *, name):
    def body(g_ref, w_ref, m_ref, v_ref, d_ref, nm_ref, nv_ref):
        d, nm, nv = _adamw_math(w_ref[...], g_ref[...], m_ref[...], v_ref[...])
        d_ref[...] = d
        nm_ref[...] = nm
        nv_ref[...] = nv

    return pl.pallas_call(body, name=name, out_shape=[_sds(w.shape, F32)] * 3)(g, w, m, v)


def _mesh_pos():
    return lax.axis_index("x"), lax.axis_index("y"), lax.axis_index("c")


def _allgather_body(x_refs, out_refs, send_sems, recv_sems, local_sems, slot):
    x, y, c = _mesh_pos()
    me, sibling = (x, y, c), (x, y, 1 - c)
    chips = [(1 - x, y), (x, 1 - y), (1 - x, 1 - y)]
    waits = []
    for a, (x_ref, out_ref) in enumerate(zip(x_refs, out_refs)):
        def copy(k, block, to, src=None, out_ref=out_ref, a=a):
            return pltpu.make_async_remote_copy(
                src_ref=slot(out_ref, block) if src is None else src, dst_ref=slot(out_ref, block),
                send_sem=send_sems.at[a, k], recv_sem=recv_sems.at[a, k], device_id=to, device_id_type=MESH)

        mine = pltpu.make_async_copy(x_ref, slot(out_ref, me), local_sems.at[a])
        mine.start()
        first = [copy(0, me, sibling, src=x_ref)]
        first += [copy(1 + j, me, (*chip, c), src=x_ref) for j, chip in enumerate(chips)]
        for cp in first:
            cp.start()
        waits.append((copy, mine, first))
    sends = []
    for copy, mine, first in waits:
        passed = [copy(4 + j, (*chip, c), sibling) for j, chip in enumerate(chips)]
        for j, chip in enumerate(chips):
            copy(1 + j, (*chip, c), me).wait_recv()
            passed[j].start()
        sends.append(passed)
    for (copy, mine, first), passed in zip(waits, sends):
        copy(0, sibling, me).wait_recv()
        for j, chip in enumerate(chips):
            copy(4 + j, (*chip, 1 - c), me).wait_recv()
        for cp in first + passed:
            cp.wait_send()
        mine.wait()


PEER_FLIPS = ((0, 0, 1), (1, 0, 0), (0, 1, 0), (1, 1, 0), (1, 0, 1), (0, 1, 1), (1, 1, 1))


def _peer_copies(x_refs, land_refs, send_sem, recv_sem, scatter):
    x, y, c = _mesh_pos()
    me = 4 * x + 2 * y + c
    copies = []
    for x_ref, land_ref in zip(x_refs, land_refs):
        for fx, fy, fc in PEER_FLIPS:
            px, py, pc = x ^ fx, y ^ fy, c ^ fc
            src = x_ref.at[4 * px + 2 * py + pc] if scatter else x_ref
            copies.append(pltpu.make_async_remote_copy(
                src_ref=src, dst_ref=land_ref.at[me], send_sem=send_sem, recv_sem=recv_sem,
                device_id=(px, py, pc), device_id_type=MESH))
    return copies


def _sc_exchange(xs, *, scatter, collective_id, name):
    na = len(xs)
    land_shapes = [x.shape if scatter else (N_DEV,) + x.shape for x in xs]

    def body(*refs):
        x_refs, land_refs = refs[:na], refs[na:2 * na]
        send_sem, recv_sem, local_sem = refs[2 * na:]
        x, y, c = _mesh_pos()
        me = 4 * x + 2 * y + c
        barrier = pltpu.get_barrier_semaphore()
        for fx, fy, fc in PEER_FLIPS:
            pl.semaphore_signal(barrier, inc=1, device_id=(x ^ fx, y ^ fy, c ^ fc), device_id_type=MESH)
        pl.semaphore_wait(barrier, len(PEER_FLIPS))
        for x_ref, land_ref in zip(x_refs, land_refs):
            own = pltpu.make_async_copy(x_ref.at[me] if scatter else x_ref, land_ref.at[me], local_sem)
            own.start()
            own.wait()
        copies = _peer_copies(x_refs, land_refs, send_sem, recv_sem, scatter)
        for cp in copies:
            cp.start()
        for cp in copies:
            cp.wait()

    return pl.kernel(
        body, name=name, out_type=[_sds(s, x.dtype) for s, x in zip(land_shapes, xs)],
        mesh=plsc.ScalarSubcoreMesh(axis_name="sequencer", num_cores=1),
        scratch_types=[pltpu.SemaphoreType.DMA, pltpu.SemaphoreType.DMA, pltpu.SemaphoreType.DMA],
        compiler_params=pltpu.CompilerParams(collective_id=collective_id),
    )(*xs)


def _sc_allgather(xs, *, collective_id, name):
    na = len(xs)

    def body(*refs):
        x_refs, out_refs = refs[:na], refs[na:2 * na]
        send_sems, recv_sems, local_sems = refs[2 * na:]
        x, y, c = _mesh_pos()
        barrier = pltpu.get_barrier_semaphore()
        for fx, fy, fc in PEER_FLIPS:
            pl.semaphore_signal(barrier, inc=1, device_id=(x ^ fx, y ^ fy, c ^ fc), device_id_type=MESH)
        pl.semaphore_wait(barrier, len(PEER_FLIPS))
        _allgather_body(x_refs, out_refs, send_sems, recv_sems, local_sems,
                        lambda ref, pos: ref.at[4 * pos[0] + 2 * pos[1] + pos[2]])

    return pl.kernel(
        body, name=name, out_type=[_sds((N_DEV,) + x.shape, x.dtype) for x in xs],
        mesh=plsc.ScalarSubcoreMesh(axis_name="sequencer", num_cores=1),
        scratch_types=[pltpu.SemaphoreType.DMA((na, 7)), pltpu.SemaphoreType.DMA((na, 7)),
                       pltpu.SemaphoreType.DMA((na,))],
        compiler_params=pltpu.CompilerParams(collective_id=collective_id),
    )(*xs)


def _allgather_vmem(x, *, name):
    r, c = x.shape

    def body(x_ref, out_ref, send_sems, recv_sems, local_sems):
        _allgather_body([x_ref], [out_ref], send_sems, recv_sems, local_sems,
                        lambda ref, pos: ref.at[pl.ds((4 * pos[0] + 2 * pos[1] + pos[2]) * r, r), :])

    vm = pl.BlockSpec(memory_space=pltpu.VMEM)
    return pl.pallas_call(
        body, name=name, in_specs=[vm], out_specs=vm, out_shape=_sds((N_DEV * r, c), x.dtype),
        scratch_shapes=[pltpu.SemaphoreType.DMA((1, 7)), pltpu.SemaphoreType.DMA((1, 7)),
                        pltpu.SemaphoreType.DMA((1,))],
    )(x)


def _sum_slots(gathered, *, name):
    _, r, c = gathered.shape

    def body(g_ref, o_ref):
        acc = g_ref[0]
        for k in range(1, N_DEV):
            acc = acc + g_ref[k]
        o_ref[...] = acc

    return pl.pallas_call(body, name=name, out_shape=_sds((r, c), gathered.dtype))(gathered)


def _t5_bucket(rel):
    nb = REL_BUCKETS // 2
    ret = jnp.where(rel > 0, nb, 0)
    n = jnp.abs(rel)
    max_exact = nb // 2
    nf = jnp.maximum(n, 1).astype(F32)
    large = max_exact + (jnp.log(nf / max_exact) / math.log(REL_MAX_DIST / max_exact)
                         * (nb - max_exact)).astype(jnp.int32)
    large = jnp.minimum(large, nb - 1)
    return ret + jnp.where(n < max_exact, n, large)


def _band_pattern(block, radius, dil):
    kw = block + 2 * radius
    rel = jnp.arange(kw)[None, :] - radius - jnp.arange(block)[:, None]
    return jnp.where(jnp.abs(rel) <= radius, _t5_bucket(rel * dil), -1).astype(jnp.int32).reshape(1, block * kw)


def _rope_tables():
    lane = np.arange(64)
    seg, j = lane // 32, lane % 32
    inv = ROPE_THETA ** (-jnp.arange(0, 32, 2, dtype=F32) / 32)
    tpos = jnp.arange(SEQ)
    pos = jnp.where(jnp.asarray(seg)[None, :] == 0, (tpos // GRID_W)[:, None], (tpos % GRID_W)[:, None])
    ang = pos.astype(F32) * inv[jnp.asarray(j % 16)][None, :]
    cos = jnp.cos(ang)
    sins = jnp.where(jnp.asarray(j)[None, :] < 16, -jnp.sin(ang), jnp.sin(ang))
    return jnp.tile(cos, (1, 4)), jnp.tile(sins, (1, 4))


A_Q, A_K, A_V = (256, 0), (256, 1), (256, 2)
B_Q, B_K, B_V = (256, 0), (128, 2), (128, 3)
A_HEADS = dict(rad=A_RADIUS, nh=4, nkv=4)
B_HEADS = dict(rad=SWA_RADIUS, nh=4, nkv=2)


def _local_step(x, pe, tgt, rel_bias, wts, matmul_weights, grads_ready):
    t = x.shape[0]
    bl = t // SEQ
    cos, sins = _rope_tables()
    blocks_a = [min(BAND_BLOCK, SEQ // d) for d in DILATIONS]
    pats_a = [_band_pattern(blk, A_RADIUS, d) for blk, d in zip(blocks_a, DILATIONS)]
    pat_b = _band_pattern(BAND_BLOCK, SWA_RADIUS, 1)
    table_t = rel_bias.T
    bias_a = [_bias_lookup(table_t[:4], pt, name=f"bias_a{ci}").reshape(4, blk, blk + 2 * A_RADIUS)
              for ci, (pt, blk) in enumerate(zip(pats_a, blocks_a))]
    bias_b = _bias_lookup(table_t[4:], pat_b, name="bias_b").reshape(4, BAND_BLOCK, BAND_BLOCK + 2 * SWA_RADIUS)
    nat4 = lambda a: a.reshape(bl, 1, SEQ, a.shape[-1])

    saved = []
    for li in range(DEPTH):
        w = dict(wts[li])
        w.update(matmul_weights(li, "in", x))
        hn0, proj = _norm_mm((x,), w["g_mix"], w["w_in"], None, tm=1024, tn=1152, name="mix_in_fwd", out_dtype=BF16)
        qa1, qa4, qa16, qb, qd = _qkprep_fwd(proj, w["qk_gains"], cos, sins, tm=512, name="qkprep_fwd")
        qa = (nat4(qa1), qa4, qa16)
        oa, la = [], []
        for ci in range(3):
            o, l = _band_fwd(qa[ci], A_Q, A_K, A_V, bias_a[ci], None, name=f"band_a{ci}_fwd", **A_HEADS)
            oa.append(o)
            la.append(l)
        oa[0], la[0] = oa[0].reshape(t, 256), la[0].reshape(t, 256)
        ya, lse_a = _combine_a(oa, la, tm=512, name="combine_a")
        yb, lse_b = _band_fwd(nat4(qb), B_Q, B_K, B_V, bias_b, w["sink_t"], name="band_b_fwd", **B_HEADS)
        yb = yb.reshape(t, 256)
        yc = _c_fwd(proj, w["c_g"], w["c_b"], w["c_ws"], w["c_bst"], tm=512, name="c_fwd")
        yd, lse_d = _dense_fwd(qd, tq=256, name="dense_fwd")
        w.update(matmul_weights(li, "rest", yd))
        mixed, x1 = _norm_mm((ya, yb, yc, yd), w["out_gain"], w["w_out"], x, tm=1024, tn=1024, name="mix_out_fwd")
        hn1, h = _norm_mm((x1,), w["g_ffn"], w["w_up"], None, tm=1024, tn=2816, name="ffn_up_fwd", out_dtype=BF16)
        act, cg, cu = _conv_gate_fwd(h, w["conv_w"], w["conv_b"], name="conv_gate_fwd")
        x2 = _mm(act, w["w_down"], "nn", x1, tm=1024, tn=1024, out_dtype=F32, name="ffn_down_fwd")
        hn2, x3, gate, pp = _ple_fwd(x2, w["g_ple"], w["w_gate"], pe, li * (t // 1024), w["w_proj"], tm=1024,
                                     name="ple_fwd")
        saved.append(dict(w=w, x0=x, hn0=hn0, proj=proj, qa=qa, qb=qb, qd=qd, ya=ya, lse_a=lse_a, yb=yb, lse_b=lse_b,
                          yc=yc, yd=yd, lse_d=lse_d, mixed=mixed, x1=x1, hn1=hn1, h=h, cg=cg, cu=cu, act=act, x2=x2, hn2=hn2,
                          gate=gate, pp=pp))
        x = x3

    loss_tile, dx, dz_last, dpp_last = _loss_head(x, tgt, saved[-1]["gate"], saved[-1]["pp"], tm=512, name="loss_head")
    grads = [None] * DEPTH
    dbias_a, dbias_bs = [[], [], []], []
    for li in reversed(range(DEPTH)):
        s = saved[li]
        w = s["w"]
        g = {}
        if li == DEPTH - 1:
            dz, dpp = dz_last, dpp_last
        else:
            dz, dpp = _ple_bwd_ew(dx, s["gate"], s["pp"], tm=512, name="ple_bwd_ew")
        g["w_gate"] = _mm(s["hn2"], dz, "tn", None, tm=1024, tn=512, out_dtype=BF16, name="dw_gate")
        g["w_proj"] = _mm(pe, dpp, "tn", None, tm=256, tn=1024, out_dtype=BF16, name="dw_proj", a_rows=(li, t))
        dx2, dx2b, g["g_ple"] = _mm_bt_normbwd((dz,), w["w_gate"], (s["x2"],), w["g_ple"], dx, tm=1024, tn=1024,
                                               name="ple_bwd", emit_bf16=True)
        g["w_down"] = _mm(s["act"], dx2b, "tn", None, tm=1408, tn=512, out_dtype=BF16, name="dw_down")
        dact = _mm(dx2b, w["w_down"], "nt", None, tm=1024, tn=2816, out_dtype=BF16, name="ffn_down_bwd")
        dhg, dhu, g["conv_w"], g["conv_b"] = _conv_gate_bwd(s["h"], s["cg"], s["cu"], dact, w["conv_w"], w["conv_b"],
                                                            name="conv_gate_bwd")
        g["w_up"] = jnp.concatenate(
            [_mm(s["hn1"], dhalf, "tn", None, tm=1024, tn=1408, out_dtype=BF16, name=f"dw_up_{nm}")
             for nm, dhalf in (("gate", dhg), ("up", dhu))], axis=1)
        dx1, dx1b, g["g_ffn"] = _mm_bt_normbwd((dhg, dhu), w["w_up"], (s["x1"],), w["g_ffn"], dx2, tm=1024, tn=1408,
                                               name="ffn_up_bwd", emit_bf16=True)
        g["w_out"] = _mm(s["mixed"], dx1b, "tn", None, tm=1024, tn=512, out_dtype=BF16, name="dw_out")
        grads_ready(li, "mid", g)
        dycat, g["out_gain"] = _mm_bt_normbwd((dx1b,), w["w_out"], (s["ya"], s["yb"], s["yc"], s["yd"]), w["out_gain"],
                                              None, tm=1024, tn=1024, name="mix_out_bwd")
        dy_r, lse_r, dl_a, dl_b, dl_d = _deltas(dycat, s["ya"], s["yb"], s["yd"], s["lse_a"], tm=512, name="deltas")
        dy_a = (nat4(dycat),) + tuple(dy_r)
        lse_a = (nat4(s["lse_a"]),) + tuple(lse_r)
        dl_a = (nat4(dl_a[0]),) + tuple(dl_a[1:])
        da = []
        for ci in range(3):
            dq, dk, dv, dbias = _band_bwd(s["qa"][ci], A_Q, A_K, A_V, bias_a[ci], None, dy_a[ci], 0, lse_a[ci],
                                          dl_a[ci], name=f"band_a{ci}_bwd", **A_HEADS)
            if ci == 0:
                dq, dk, dv = (a.reshape(t, 256) for a in (dq, dk, dv))
            da.append((dq, dk, dv))
            dbias_a[ci].append(dbias.reshape(4, -1))
        dqb, dkb, dvb, dbias_b, dsink = _band_bwd(nat4(s["qb"]), B_Q, B_K, B_V, bias_b, w["sink_t"], nat4(dycat), 1,
                                                  nat4(s["lse_b"]), nat4(dl_b), name="band_b_bwd", **B_HEADS)
        dbias_bs.append(dbias_b.reshape(4, -1))
        g["sink"] = dsink[:, 0, 0]
        dd = _dense_bwd(s["qd"], dycat, s["lse_d"], dl_d, tq=128, name="dense_bwd")
        dcu, dcv, g["c_ws"], dbs, g["c_g"], g["c_b"] = _c_bwd(s["proj"], dycat, w["c_g"], w["c_b"], w["c_ws"],
                                                               w["c_wst"], w["c_bst"], tm=512, name="c_bwd")
        g["c_bs"] = dbs[:, ::64].T
        db = (dqb.reshape(t, 256), dkb.reshape(t, 128), dvb.reshape(t, 128))
        dproj, dgains = _qkprep_bwd(s["proj"], da, db, dd, dcu, dcv, w["qk_gains"], cos, sins, tm=512, name="qkprep_bwd")
        g["qk_gain"] = dgains[:6, :64].reshape(3, 2, HEAD_DIM)
        g["w_in"] = _mm(s["hn0"], dproj, "tn", None, tm=1024, tn=768, out_dtype=BF16, name="dw_in")
        dx, g["g_mix"] = _mm_bt_normbwd((dproj,), w["w_in"], (s["x0"],), w["g_mix"], dx1, tm=1024, tn=1152,
                                        name="mix_in_bwd")
        grads[li] = g
        grads_ready(li, "end", g)
    d_table_a = sum(_bucket_reduce(dbias_a[ci], pats_a[ci], name=f"bucket_a{ci}") for ci in range(3))
    d_table_b = _bucket_reduce(dbias_bs, pat_b, name="bucket_b")
    d_rel_bias = jnp.concatenate([d_table_a, d_table_b], axis=0).T
    return loss_tile[0, 0], dx, grads, d_rel_bias


WEIGHT_NAMES = ("rel_bias", "ln_mix_g", "w_in", "qk_gain", "sink", "c_norm_g", "c_norm_b", "c_ws", "c_bs", "out_gain",
                "w_out", "ln_ffn_g", "w_up", "conv_w", "conv_b", "w_down", "ln_ple_g", "w_ple_gate", "w_ple_proj")
COL_SHARDED = ("w_in", "w_up", "w_ple_proj")
ROW_SHARDED = ("w_out", "w_down", "w_ple_gate")
SMALL_SHARDED = ("conv_w", "out_gain")
REPLICATED = tuple(n for n in WEIGHT_NAMES if n not in COL_SHARDED + ROW_SHARDED + SMALL_SHARDED)
LOCAL_GRAD_KEY = {"ln_mix_g": "g_mix", "ln_ffn_g": "g_ffn", "ln_ple_g": "g_ple", "c_norm_g": "c_g", "c_norm_b": "c_b",
                  "w_ple_gate": "w_gate", "w_ple_proj": "w_proj"}


def _full_from_gathered(name, gathered):
    _, r, c = gathered.shape
    if name in ROW_SHARDED:
        return gathered.reshape(N_DEV * r, c)
    return jnp.transpose(gathered, (1, 0, 2)).reshape(r, N_DEV * c)


def _slots_from_full(name, full):
    rows, cols = full.shape
    if name in ROW_SHARDED:
        return full.reshape(N_DEV, rows // N_DEV, cols)
    return jnp.transpose(full.reshape(rows, N_DEV, cols // N_DEV), (1, 0, 2))


def _piece_rows(shape):
    return -(-int(np.prod(shape)) // 1024) * 8


def _pack_rows(arrays):
    pieces = []
    for a in arrays:
        n, rows = int(np.prod(a.shape)), _piece_rows(a.shape)
        flat = a.astype(F32).reshape(-1)
        if n != rows * LANES:
            flat = jnp.pad(flat, (0, rows * LANES - n))
        pieces.append(flat.reshape(rows, LANES))
    return jnp.concatenate(pieces, axis=0)


def _unpack_rows(packed, shapes):
    out, off = [], 0
    for shp in shapes:
        n, rows = int(np.prod(shp)), _piece_rows(shp)
        piece = packed[off:off + rows]
        out.append((piece if n == rows * LANES else piece.reshape(-1)[:n]).reshape(shp))
        off += rows
    return out


def kernel(x, p, rel_bias, ln_mix_g, w_in, qk_gain, sink, c_norm_g, c_norm_b, c_ws, c_bs, out_gain, w_out, ln_ffn_g, w_up, conv_w, conv_b, w_down, ln_ple_g, w_ple_gate, w_ple_proj, loss_target, m_rel_bias, m_ln_mix_g, m_w_in, m_qk_gain, m_sink, m_c_norm_g, m_c_norm_b, m_c_ws, m_c_bs, m_out_gain, m_w_out, m_ln_ffn_g, m_w_up, m_conv_w, m_conv_b, m_w_down, m_ln_ple_g, m_w_ple_gate, m_w_ple_proj, v_rel_bias, v_ln_mix_g, v_w_in, v_qk_gain, v_sink, v_c_norm_g, v_c_norm_b, v_c_ws, v_c_bs, v_out_gain, v_w_out, v_ln_ffn_g, v_w_up, v_conv_w, v_conv_b, v_w_down, v_ln_ple_g, v_w_ple_gate, v_w_ple_proj):
    env = dict(locals())
    wt = {n: env[n] for n in WEIGHT_NAMES}
    mom_m = {n: env["m_" + n] for n in WEIGHT_NAMES}
    mom_v = {n: env["v_" + n] for n in WEIGHT_NAMES}
    bl = x.shape[0]
    t = bl * SEQ
    me = 4 * lax.axis_index("x") + 2 * lax.axis_index("y") + lax.axis_index("c")

    big = COL_SHARDED + ROW_SHARDED
    full = {}
    small_shapes = [wt[n].shape for n in SMALL_SHARDED]
    small = _allgather_vmem(_pack_rows([wt[n] for n in SMALL_SHARDED]), name="gather_small")
    small = small.reshape(N_DEV, -1)
    off = 0
    for n, shp in zip(SMALL_SHARDED, small_shapes):
        cnt = int(np.prod(shp))
        g = small[:, off:off + cnt].reshape((N_DEV,) + tuple(shp))
        full[n] = jnp.transpose(g, (1, 2, 0, 3)).reshape(shp[0], shp[1], N_DEV * shp[2])
        off += _piece_rows(shp) * LANES

    def head_gain(li, a, b, reps):
        g = jnp.tile(qk_gain[li, a, b], reps)
        return jnp.pad(g, (0, 256 - g.shape[0]))

    wts = []
    for li in range(DEPTH):
        rows = [head_gain(li, 0, 0, 4), head_gain(li, 0, 1, 4), head_gain(li, 1, 0, 4), head_gain(li, 1, 1, 2),
                head_gain(li, 2, 0, 4), head_gain(li, 2, 1, 2), jnp.zeros((256,), F32), jnp.zeros((256,), F32)]
        wts.append(dict(
            g_mix=ln_mix_g[li].reshape(1, -1), qk_gains=jnp.stack(rows),
            sink_t=jnp.broadcast_to(sink[li][:, None, None], (4, 8, 128)),
            c_g=c_norm_g[li].reshape(1, -1), c_b=c_norm_b[li].reshape(1, -1), c_ws=c_ws[li].astype(BF16),
            c_wst=jnp.transpose(c_ws[li], (0, 2, 1)).astype(BF16), c_bst=jnp.repeat(c_bs[li].T, 64, axis=1),
            out_gain=full["out_gain"][li].reshape(1, -1), g_ffn=ln_ffn_g[li].reshape(1, -1),
            conv_w=full["conv_w"][li], conv_b=conv_b[li].reshape(1, -1), g_ple=ln_ple_g[li].reshape(1, -1)))

    local_key = {"w_ple_gate": "w_gate", "w_ple_proj": "w_proj"}

    gather_names = {"in": ("w_in",), "rest": tuple(n for n in big if n != "w_in")}
    gathered = {}
    for cid, (li, names) in enumerate(((0, gather_names["in"]), (0, gather_names["rest"]), (1, big))):
        lands = _sc_allgather([wt[n][li].astype(BF16) for n in names], collective_id=cid,
                              name=f"gather_{li}_{len(names)}")
        gathered.setdefault(li, {}).update(zip(names, lands))

    def matmul_weights(li, part, after):
        out = {}
        for n in gather_names[part]:
            g, _ = lax.optimization_barrier((gathered[li][n], after))
            out[local_key.get(n, n)] = _full_from_gathered(n, g)
        return out

    mid_names = ("w_ple_gate", "w_ple_proj", "w_down", "w_up", "w_out")
    end_names = ("w_in",)
    landed = {}

    def start_exchange(li, names, g, tag, cid):
        slots = [_slots_from_full(n, g[local_key.get(n, n)]) for n in names]
        lands = _sc_exchange(slots, scatter=True, collective_id=cid, name=f"grads_{li}_{tag}")
        landed.update({(n, li): land for n, land in zip(names, lands)})

    def grads_ready(li, stage, g):
        if li == 0:
            start_exchange(li, mid_names if stage == "mid" else end_names, g, stage, 5 if stage == "mid" else 6)
        elif stage == "end":
            start_exchange(li, mid_names + end_names, g, stage, 4)

    loss_part, dx, grads, d_rel_bias = _local_step(
        x.reshape(t, D_MODEL), p.reshape(DEPTH * t, PLE_DIM), loss_target.reshape(t, D_MODEL), rel_bias, wts,
        matmul_weights, grads_ready)

    def local_grad(n):
        if n == "rel_bias":
            return d_rel_bias
        key = LOCAL_GRAD_KEY.get(n, n)
        return jnp.stack([grads[li][key].reshape(wt[n].shape[1:]) if n in REPLICATED else grads[li][key]
                          for li in range(DEPTH)])

    small_names = REPLICATED + SMALL_SHARDED
    small_full_shapes = [wt[n].shape if n in REPLICATED else full[n].shape for n in small_names]
    small_parts = _allgather_vmem(_pack_rows([local_grad(n) for n in small_names] + [loss_part.reshape(1)]),
                                  name="allgather_small_grads")
    small_parts = small_parts.reshape(N_DEV, -1, LANES)

    out_g, out_d, out_m, out_v = {}, {}, {}, {}
    for n in big:
        shp = wt[n].shape
        two_d = lambda a: a.reshape(-1, shp[-1])
        res = _adamw_reduce([landed[n, li] for li in range(DEPTH)], two_d(wt[n]), two_d(mom_m[n]), two_d(mom_v[n]),
                            tr=32 if n == "w_down" else 128, name="adamw_" + n)
        out_g[n], out_d[n], out_m[n], out_v[n] = [r.reshape(shp) for r in res]

    *reduced, loss = _unpack_rows(_sum_slots(small_parts, name="sum_small_grads"), small_full_shapes + [(1,)])
    loss = loss[0]
    reduced = dict(zip(small_names, reduced))
    rep_shapes = [wt[n].shape for n in REPLICATED]
    upd = _adamw_plain(_pack_rows([reduced[n] for n in REPLICATED]), _pack_rows([wt[n] for n in REPLICATED]),
                       _pack_rows([mom_m[n] for n in REPLICATED]), _pack_rows([mom_v[n] for n in REPLICATED]),
                       name="adamw_replicated")
    for dst, packed in zip((out_d, out_m, out_v), upd):
        dst.update(zip(REPLICATED, _unpack_rows(packed, rep_shapes)))
    for n in REPLICATED:
        out_g[n] = reduced[n]
    for n in SMALL_SHARDED:
        shp = wt[n].shape
        g = reduced[n].reshape(shp[0], shp[1], N_DEV, shp[2])
        g = lax.dynamic_index_in_dim(g, me, axis=2, keepdims=False)
        two_d = lambda a: a.reshape(-1, shp[-1])
        res = _adamw_plain(two_d(g), two_d(wt[n]), two_d(mom_m[n]), two_d(mom_v[n]), name="adamw_" + n)
        out_g[n] = g
        out_d[n], out_m[n], out_v[n] = [r.reshape(shp) for r in res]

    return (loss, dx.reshape(bl, SEQ, D_MODEL), *[out_g[n] for n in WEIGHT_NAMES], *[out_d[n] for n in WEIGHT_NAMES],
            *[out_m[n] for n in WEIGHT_NAMES], *[out_v[n] for n in WEIGHT_NAMES])
```
